```python
import math
import jax, jax.numpy as jnp
from jax import lax
import numpy as np

D_MODEL = 1024
BATCH = 8
SEQ = 4096
DEPTH = 2

GRID_W = 64
CTX_LEN = 256
EPS = 1e-6
F32 = jnp.float32
HEAD_DIM = 64
ROPE_BASE = 10000.0

WA_HEADS = 4
WA_KV_HEADS = 2
WA_WINDOW = 128
WA_BLOCK = 128
NA_HEADS = 4
NA_MAX_KH = 8
NA_KW = 16
NA_QBW = 16
NA_KBW = NA_QBW + NA_KW
SSM_HEADS = 8
SSM_HEAD_DIM = 64
SSM_INNER = SSM_HEADS * SSM_HEAD_DIM
SSM_GROUPS = 2
SSM_STATE = 128
SSM_CONV = 7
SSM_CHUNK = 128
D_FF = ((8 * D_MODEL + 3 * 256 - 1) // (3 * 256)) * 256

QA_COLS = WA_HEADS * HEAD_DIM
QB_COLS = NA_HEADS * HEAD_DIM
Z_COLS = SSM_INNER
Q_SIDE = QA_COLS + QB_COLS + Z_COLS
KA_COLS = WA_KV_HEADS * HEAD_DIM
KB_COLS = NA_HEADS * HEAD_DIM
XBC_COLS = SSM_INNER + 2 * SSM_GROUPS * SSM_STATE
DT_COLS = 2 * SSM_HEADS
IN_COLS = Q_SIDE + 2 * KA_COLS + 2 * KB_COLS + XBC_COLS + DT_COLS
MIX_WIDTH = QA_COLS + QB_COLS + SSM_INNER

kernel_name = 'hymba_style_window_natten_ssd_prefix_dit'


def rms_norm(x, g):
    xf = x.astype(F32)
    y = xf * lax.rsqrt(jnp.mean(xf * xf, axis=-1, keepdims=True) + EPS)
    return (y * g.astype(F32)).astype(x.dtype)


def modulate(h, shift, scale):
    return h * (1 + scale) + shift


def split_cols(p, sizes):
    out, off = [], 0
    for s in sizes:
        out.append(p[..., off:off + s])
        off += s
    return out


def rope_2d(x, rows, cols):
    d = x.shape[-1]
    half = d // 2
    quarter = half // 2
    inv_freq = ROPE_BASE ** (-jnp.arange(quarter, dtype=F32) / quarter)
    xf = x.astype(F32)

    def rot(xp, pos):
        ang = pos.astype(F32)[:, None] * inv_freq[None, :]
        cos = jnp.cos(ang)[None, :, None, :]
        sin = jnp.sin(ang)[None, :, None, :]
        x1, x2 = xp[..., :quarter], xp[..., quarter:]
        return jnp.concatenate([x1 * cos - x2 * sin, x2 * cos + x1 * sin], axis=-1)

    return jnp.concatenate([rot(xf[..., :half], rows), rot(xf[..., half:], cols)], axis=-1).astype(x.dtype)


def window_attention(q, k, v, k_ctx, v_ctx, sink):
    b, L, H, d = q.shape
    G = k.shape[2]
    rep = H // G
    blk = WA_BLOCK
    nb = L // blk
    Lc = k_ctx.shape[1]
    scale = d ** -0.5
    qb = q.reshape(b, nb, blk, G, rep, d)

    def band(t):
        tp = jnp.pad(t, ((0, 0), (blk, blk), (0, 0), (0, 0))).reshape(b, nb + 2, blk, G, d)
        return jnp.concatenate([tp[:, 0:nb], tp[:, 1:nb + 1], tp[:, 2:nb + 2]], axis=2)

    kb, vb = band(k), band(v)
    qpos = jnp.arange(nb)[:, None] * blk + jnp.arange(blk)[None, :]
    kpos = (jnp.arange(nb)[:, None] - 1) * blk + jnp.arange(3 * blk)[None, :]
    mask = ((jnp.abs(qpos[:, :, None] - kpos[:, None, :]) <= WA_WINDOW)
            & (kpos[:, None, :] >= 0) & (kpos[:, None, :] < L))
    s_loc = jnp.einsum('bnqgrd,bnkgd->bngrqk', qb, kb).astype(F32) * scale
    s_loc = jnp.where(mask[None, :, None, None], s_loc, -jnp.inf)
    s_ctx = jnp.einsum('bnqgrd,bcgd->bngrqc', qb, k_ctx).astype(F32) * scale
    s_sink = jnp.broadcast_to(sink.astype(F32).reshape(G, rep)[None, None, :, :, None, None],
                              s_loc.shape[:-1] + (1,))
    p = jax.nn.softmax(jnp.concatenate([s_loc, s_ctx, s_sink], axis=-1), axis=-1).astype(v.dtype)
    nk = 3 * blk
    o = (jnp.einsum('bngrqk,bnkgd->bnqgrd', p[..., :nk], vb)
         + jnp.einsum('bngrqc,bcgd->bnqgrd', p[..., nk:nk + Lc], v_ctx))
    return o.reshape(b, L, H * d)


def context_attention(q, k, v, sink):
    b, Lc, H, d = q.shape
    G = k.shape[2]
    rep = H // G
    qg = q.reshape(b, Lc, G, rep, d)
    s = jnp.einsum('bqgrd,bkgd->bgrqk', qg, k).astype(F32) * d ** -0.5
    if sink is not None:
        s_sink = jnp.broadcast_to(sink.astype(F32).reshape(G, rep)[None, :, :, None, None], s.shape[:-1] + (1,))
        s = jnp.concatenate([s, s_sink], axis=-1)
    p = jax.nn.softmax(s, axis=-1)[..., :Lc].astype(v.dtype)
    o = jnp.einsum('bgrqk,bkgd->bqgrd', p, v)
    return o.reshape(b, Lc, H * d)


def neighbourhood_attention(q, k, v, k_ctx, v_ctx, rpb, grid_rows):
    b, L, H, d = q.shape
    kh = min(NA_MAX_KH, grid_rows)
    ncb = GRID_W // NA_QBW
    scale = d ** -0.5
    r = jnp.arange(grid_rows)
    row_idx = jnp.clip(r - kh // 2, 0, grid_rows - kh)[:, None] + jnp.arange(kh)[None, :]
    cb = jnp.arange(ncb)
    col_idx = jnp.clip(cb * NA_QBW - NA_KW // 2, 0, GRID_W - NA_KBW)[:, None] + jnp.arange(NA_KBW)[None, :]
    qcol = cb[:, None] * NA_QBW + jnp.arange(NA_QBW)[None, :]
    cstart = jnp.clip(qcol - NA_KW // 2, 0, GRID_W - NA_KW)
    cmask = (col_idx[:, None, :] >= cstart[:, :, None]) & (col_idx[:, None, :] < cstart[:, :, None] + NA_KW)
    dy = row_idx - r[:, None] + (NA_MAX_KH - 1)
    dx = jnp.clip(col_idx[:, None, :] - qcol[:, :, None], -(NA_KW - 1), NA_KW - 1) + (NA_KW - 1)
    bias = rpb.astype(F32)[:, dy[:, None, None, :, None], dx[None, :, :, None, :]]
    bias = jnp.moveaxis(bias, 0, 2)
    qg = q.reshape(b, grid_rows, ncb, NA_QBW, H, d)
    kg = k.reshape(b, grid_rows, GRID_W, H, d)
    vg = v.reshape(b, grid_rows, GRID_W, H, d)
    ri = row_idx[:, None, :, None]
    ci = col_idx[None, :, None, :]
    kwin = kg[:, ri, ci]
    vwin = vg[:, ri, ci]
    s = jnp.einsum('brcqhd,brcyxhd->brchqyx', qg, kwin).astype(F32) * scale + bias[None]
    s = jnp.where(cmask[None, None, :, None, :, None, :], s, -jnp.inf)
    nloc = kh * NA_KBW
    s = s.reshape(b, grid_rows, ncb, H, NA_QBW, nloc)
    s_ctx = jnp.einsum('brcqhd,bkhd->brchqk', qg, k_ctx).astype(F32) * scale
    p = jax.nn.softmax(jnp.concatenate([s, s_ctx], axis=-1), axis=-1).astype(v.dtype)
    p_loc = p[..., :nloc].reshape(b, grid_rows, ncb, H, NA_QBW, kh, NA_KBW)
    o = (jnp.einsum('brchqyx,brcyxhd->brcqhd', p_loc, vwin)
         + jnp.einsum('brchqk,bkhd->brcqhd', p[..., nloc:], v_ctx))
    return o.reshape(b, L, H * d)


def depthwise_conv(x, w, bias):
    k = w.shape[0]
    y = lax.conv_general_dilated(x, w[:, None, :].astype(x.dtype), window_strides=(1,),
                                 padding=[(k // 2, k // 2)], dimension_numbers=('NWC', 'WIO', 'NWC'),
                                 feature_group_count=x.shape[-1])
    return y + bias


def ssm_prepare(xbc_raw, dt_raw, conv_w, conv_b, dt_bias):
    xbc = jax.nn.silu(depthwise_conv(xbc_raw, conv_w, conv_b))
    xs, bm, cm = split_cols(xbc, (SSM_INNER, SSM_GROUPS * SSM_STATE, SSM_GROUPS * SSM_STATE))
    b, L = xs.shape[:2]
    rep = SSM_HEADS // SSM_GROUPS
    xs = xs.reshape(b, L, SSM_HEADS, SSM_HEAD_DIM)
    bm = jnp.repeat(bm.reshape(b, L, SSM_GROUPS, SSM_STATE), rep, axis=2)
    cm = jnp.repeat(cm.reshape(b, L, SSM_GROUPS, SSM_STATE), rep, axis=2)
    dt = jax.nn.softplus(dt_raw.astype(F32) + dt_bias.astype(F32).reshape(2 * SSM_HEADS))
    return xs, bm, cm, dt.reshape(b, L, 2, SSM_HEADS)


def ssd_scan(x, dt, A, Bm, Cm, h0, with_output):
    b, L, H, P = x.shape
    N = Bm.shape[-1]
    Q = SSM_CHUNK
    nc = L // Q
    xc = x.astype(F32).reshape(b, nc, Q, H, P)
    dtc = dt.astype(F32).reshape(b, nc, Q, H)
    bc = Bm.astype(F32).reshape(b, nc, Q, H, N)
    cc = Cm.astype(F32).reshape(b, nc, Q, H, N)
    acum = jnp.cumsum(dtc * A, axis=2)
    decay_to_end = jnp.exp(acum[:, :, -1:, :] - acum)
    states = jnp.einsum('bcjhn,bcjh,bcjhp->bchpn', bc, decay_to_end * dtc, xc)
    chunk_decay = jnp.exp(acum[:, :, -1, :])

    def step(h, inp):
        st, dec = inp
        return h * dec[:, :, None, None] + st, h

    h_final, h_enter = lax.scan(step, h0, (jnp.moveaxis(states, 1, 0), jnp.moveaxis(chunk_decay, 1, 0)))
    if not with_output:
        return h_final
    h_enter = jnp.moveaxis(h_enter, 0, 1)
    seg = acum[:, :, :, None, :] - acum[:, :, None, :, :]
    lower = jnp.tril(jnp.ones((Q, Q), dtype=bool))
    decay_ij = jnp.exp(jnp.where(lower[None, None, :, :, None], seg, -jnp.inf))
    w = jnp.einsum('bcihn,bcjhn->bcijh', cc, bc) * decay_ij * dtc[:, :, None, :, :]
    y = (jnp.einsum('bcijh,bcjhp->bcihp', w, xc)
         + jnp.einsum('bcihn,bchpn->bcihp', cc, h_enter) * jnp.exp(acum)[..., None])
    return y.reshape(b, L, H, P), h_final


def ssd_bidir(xs, bm, cm, dt, A, h0_f, h0_b, with_output):
    rev = lambda t: jnp.flip(t, axis=1)
    fwd = ssd_scan(xs, dt[:, :, 0], A[0], bm, cm, h0_f, with_output)
    bwd = ssd_scan(rev(xs), rev(dt[:, :, 1]), A[1], rev(bm), rev(cm), h0_b, with_output)
    if not with_output:
        return fwd, bwd
    (y_f, h_f), (y_b, h_b) = fwd, bwd
    return y_f + rev(y_b), h_f, h_b


def ssm_output(y, xs, z, d_skip, g):
    b, L = y.shape[:2]
    y = y + d_skip.astype(F32)[:, None] * xs.astype(F32)
    y = y.reshape(b, L, SSM_INNER) * jax.nn.silu(z.astype(F32))
    return rms_norm(y, g).astype(z.dtype)


def swiglu(h, w_in, w_out):
    gate, up = jnp.split(h @ w_in, 2, axis=-1)
    return (jax.nn.silu(gate) * up) @ w_out


def hybrid_layer(xl, xc, sc, scc, rows, cols, grid_rows, w_mod, b_mod, g_mix, w_in, wa_sink, na_rpb,
                 conv_w, conv_b, dt_bias, a_log, d_skip, ssm_g, w_out, g_ffn, w_ffn_in, w_ffn_out, ctx_out):
    D = D_MODEL
    b, L, _ = xl.shape
    Lc = xc.shape[1]
    hd = HEAD_DIM
    mod_l = sc @ w_mod + b_mod
    sh1, sc1, gt1, sh2, sc2, gt2 = [m[:, None, :] for m in jnp.split(mod_l, 6, axis=-1)]
    n_ctx_mod = 6 if ctx_out else 2
    mods_c = jnp.split(scc @ w_mod[:, :n_ctx_mod * D] + b_mod[:n_ctx_mod * D], n_ctx_mod)

    hl = modulate(rms_norm(xl, g_mix), sh1, sc1)
    hc = modulate(rms_norm(xc, g_mix), mods_c[0], mods_c[1])
    kv_sizes = (KA_COLS, KA_COLS, KB_COLS, KB_COLS, XBC_COLS, DT_COLS)
    qa, qb, z, ka, va, kb, vb, xbc, dtr = split_cols(hl @ w_in, (QA_COLS, QB_COLS, Z_COLS) + kv_sizes)
    if ctx_out:
        qa_c, qb_c, z_c, ka_c, va_c, kb_c, vb_c, xbc_c, dtr_c = split_cols(hc @ w_in, (QA_COLS, QB_COLS, Z_COLS) + kv_sizes)
    else:
        ka_c, va_c, kb_c, vb_c, xbc_c, dtr_c = split_cols(hc @ w_in[:, Q_SIDE:], kv_sizes)

    ka_c = ka_c.reshape(b, Lc, WA_KV_HEADS, hd)
    va_c = va_c.reshape(b, Lc, WA_KV_HEADS, hd)
    o_a = window_attention(rope_2d(qa.reshape(b, L, WA_HEADS, hd), rows, cols),
                           rope_2d(ka.reshape(b, L, WA_KV_HEADS, hd), rows, cols),
                           va.reshape(b, L, WA_KV_HEADS, hd), ka_c, va_c, wa_sink)
    kb_c = kb_c.reshape(b, Lc, NA_HEADS, hd)
    vb_c = vb_c.reshape(b, Lc, NA_HEADS, hd)
    o_b = neighbourhood_attention(qb.reshape(b, L, NA_HEADS, hd), kb.reshape(b, L, NA_HEADS, hd),
                                  vb.reshape(b, L, NA_HEADS, hd), kb_c, vb_c, na_rpb, grid_rows)
    A = -jnp.exp(a_log.astype(F32))
    xs_c, bm_c, cm_c, dt_c = ssm_prepare(xbc_c, dtr_c, conv_w, conv_b, dt_bias)
    h0 = jnp.zeros((b, SSM_HEADS, SSM_HEAD_DIM, SSM_STATE), F32)
    if ctx_out:
        y_c, h_f, h_b = ssd_bidir(xs_c, bm_c, cm_c, dt_c, A, h0, h0, True)
    else:
        h_f, h_b = ssd_bidir(xs_c, bm_c, cm_c, dt_c, A, h0, h0, False)
    xs, bm, cm, dt = ssm_prepare(xbc, dtr, conv_w, conv_b, dt_bias)
    y_l, _, _ = ssd_bidir(xs, bm, cm, dt, A, h_f, h_b, True)
    o_c = ssm_output(y_l, xs, z, d_skip, ssm_g)

    mix = jnp.concatenate([o_a, o_b, o_c.astype(o_a.dtype)], axis=-1) @ w_out
    xl = xl + gt1 * mix
    xl = xl + gt2 * swiglu(modulate(rms_norm(xl, g_ffn), sh2, sc2), w_ffn_in, w_ffn_out)
    if not ctx_out:
        return xl, None

    o_ac = context_attention(qa_c.reshape(b, Lc, WA_HEADS, hd), ka_c, va_c, wa_sink)
    o_bc = context_attention(qb_c.reshape(b, Lc, NA_HEADS, hd), kb_c, vb_c, None)
    o_cc = ssm_output(y_c, xs_c, z_c, d_skip, ssm_g)
    mix_c = jnp.concatenate([o_ac, o_bc, o_cc.astype(o_ac.dtype)], axis=-1) @ w_out
    xc = xc + mods_c[2] * mix_c
    xc = xc + mods_c[5] * swiglu(modulate(rms_norm(xc, g_ffn), mods_c[3], mods_c[4]), w_ffn_in, w_ffn_out)
    return xl, xc


def _fwd_setup_inputs(seed: int = 0) -> dict:
    key = jax.random.key(seed)
    ks = jax.random.split(key, 24)
    nrm = jax.random.normal
    D = D_MODEL
    dt0 = jnp.exp(jax.random.uniform(ks[12], (DEPTH, 2, SSM_HEADS), minval=math.log(1e-3), maxval=math.log(0.1)))
    return {
        'x': nrm(ks[0], (BATCH, SEQ, D), F32),
        'c': nrm(ks[1], (BATCH, D), F32),
        'ctx': nrm(ks[2], (BATCH, CTX_LEN, D), F32),
        'c_ctx': nrm(ks[3], (D,), F32),
        'w_mod': nrm(ks[4], (DEPTH, D, 6 * D), F32) * (0.5 * D ** -0.5),
        'b_mod': nrm(ks[5], (DEPTH, 6 * D), F32) * 0.01,
        'g_mix': 1.0 + 0.05 * nrm(ks[6], (DEPTH, D), F32),
        'w_in': nrm(ks[7], (DEPTH, D, IN_COLS), F32) * D ** -0.5,
        'wa_sink': nrm(ks[8], (DEPTH, WA_HEADS), F32) * 0.5,
        'na_rpb': nrm(ks[9], (DEPTH, NA_HEADS, 2 * NA_MAX_KH - 1, 2 * NA_KW - 1), F32) * 0.1,
        'ssm_conv_w': nrm(ks[10], (DEPTH, SSM_CONV, XBC_COLS), F32) * SSM_CONV ** -0.5,
        'ssm_conv_b': nrm(ks[11], (DEPTH, XBC_COLS), F32) * 0.01,
        'ssm_dt_bias': dt0 + jnp.log(-jnp.expm1(-dt0)),
        'ssm_a_log': jnp.log(jax.random.uniform(ks[13], (DEPTH, 2, SSM_HEADS), minval=1.0, maxval=16.0)),
        'ssm_d': 1.0 + 0.1 * nrm(ks[14], (DEPTH, SSM_HEADS), F32),
        'ssm_norm_g': 1.0 + 0.05 * nrm(ks[15], (DEPTH, SSM_INNER), F32),
        'w_out': nrm(ks[16], (DEPTH, MIX_WIDTH, D), F32) * MIX_WIDTH ** -0.5,
        'g_ffn': 1.0 + 0.05 * nrm(ks[17], (DEPTH, D), F32),
        'w_ffn_in': nrm(ks[18], (DEPTH, D, 2 * D_FF), F32) * D ** -0.5,
        'w_ffn_out': nrm(ks[19], (DEPTH, D_FF, D), F32) * D_FF ** -0.5,
        'g_final': 1.0 + 0.05 * nrm(ks[20], (D,), F32),
    }


def _fwd_reference(x, c, ctx, c_ctx, w_mod, b_mod, g_mix, w_in, wa_sink, na_rpb, ssm_conv_w, ssm_conv_b,
              ssm_dt_bias, ssm_a_log, ssm_d, ssm_norm_g, w_out, g_ffn, w_ffn_in, w_ffn_out, g_final):
    L = x.shape[1]
    grid_rows = L // GRID_W
    t = jnp.arange(L)
    rows, cols = t // GRID_W, t % GRID_W
    sc = jax.nn.silu(c)
    scc = jax.nn.silu(c_ctx)
    xl, xc = x, ctx
    for i in range(DEPTH):
        xl, xc = hybrid_layer(xl, xc, sc, scc, rows, cols, grid_rows, w_mod[i], b_mod[i], g_mix[i], w_in[i],
                              wa_sink[i], na_rpb[i], ssm_conv_w[i], ssm_conv_b[i], ssm_dt_bias[i], ssm_a_log[i],
                              ssm_d[i], ssm_norm_g[i], w_out[i], g_ffn[i], w_ffn_in[i], w_ffn_out[i],
                              ctx_out=(i < DEPTH - 1))
    return rms_norm(xl, g_final)


import jax as _jax
import jax.numpy as _jnp

TWIN_FORMAT = 'train_step'
FWD_PARAMS = ['x', 'c', 'ctx', 'c_ctx', 'w_mod', 'b_mod', 'g_mix', 'w_in', 'wa_sink', 'na_rpb', 'ssm_conv_w', 'ssm_conv_b', 'ssm_dt_bias', 'ssm_a_log', 'ssm_d', 'ssm_norm_g', 'w_out', 'g_ffn', 'w_ffn_in', 'w_ffn_out', 'g_final']
TWIN_WEIGHTS = ['c_ctx', 'w_mod', 'b_mod', 'g_mix', 'w_in', 'wa_sink', 'na_rpb', 'ssm_conv_w', 'ssm_conv_b', 'ssm_dt_bias', 'ssm_a_log', 'ssm_d', 'ssm_norm_g', 'w_out', 'g_ffn', 'w_ffn_in', 'w_ffn_out', 'g_final']
TWIN_DIFF_INPUT = 'x'
TWIN_INPUTS = ['x', 'c', 'ctx', 'c_ctx', 'w_mod', 'b_mod', 'g_mix', 'w_in', 'wa_sink', 'na_rpb', 'ssm_conv_w', 'ssm_conv_b', 'ssm_dt_bias', 'ssm_a_log', 'ssm_d', 'ssm_norm_g', 'w_out', 'g_ffn', 'w_ffn_in', 'w_ffn_out', 'g_final', 'loss_target', 'm_c_ctx', 'm_w_mod', 'm_b_mod', 'm_g_mix', 'm_w_in', 'm_wa_sink', 'm_na_rpb', 'm_ssm_conv_w', 'm_ssm_conv_b', 'm_ssm_dt_bias', 'm_ssm_a_log', 'm_ssm_d', 'm_ssm_norm_g', 'm_w_out', 'm_g_ffn', 'm_w_ffn_in', 'm_w_ffn_out', 'm_g_final', 'v_c_ctx', 'v_w_mod', 'v_b_mod', 'v_g_mix', 'v_w_in', 'v_wa_sink', 'v_na_rpb', 'v_ssm_conv_w', 'v_ssm_conv_b', 'v_ssm_dt_bias', 'v_ssm_a_log', 'v_ssm_d', 'v_ssm_norm_g', 'v_w_out', 'v_g_ffn', 'v_w_ffn_in', 'v_w_ffn_out', 'v_g_final']
TWIN_OUTPUTS = ['loss', 'grad_x', 'grad_c_ctx', 'grad_w_mod', 'grad_b_mod', 'grad_g_mix', 'grad_w_in', 'grad_wa_sink', 'grad_na_rpb', 'grad_ssm_conv_w', 'grad_ssm_conv_b', 'grad_ssm_dt_bias', 'grad_ssm_a_log', 'grad_ssm_d', 'grad_ssm_norm_g', 'grad_w_out', 'grad_g_ffn', 'grad_w_ffn_in', 'grad_w_ffn_out', 'grad_g_final', 'delta_c_ctx', 'delta_w_mod', 'delta_b_mod', 'delta_g_mix', 'delta_w_in', 'delta_wa_sink', 'delta_na_rpb', 'delta_ssm_conv_w', 'delta_ssm_conv_b', 'delta_ssm_dt_bias', 'delta_ssm_a_log', 'delta_ssm_d', 'delta_ssm_norm_g', 'delta_w_out', 'delta_g_ffn', 'delta_w_ffn_in', 'delta_w_ffn_out', 'delta_g_final', 'new_m_c_ctx', 'new_m_w_mod', 'new_m_b_mod', 'new_m_g_mix', 'new_m_w_in', 'new_m_wa_sink', 'new_m_na_rpb', 'new_m_ssm_conv_w', 'new_m_ssm_conv_b', 'new_m_ssm_dt_bias', 'new_m_ssm_a_log', 'new_m_ssm_d', 'new_m_ssm_norm_g', 'new_m_w_out', 'new_m_g_ffn', 'new_m_w_ffn_in', 'new_m_w_ffn_out', 'new_m_g_final', 'new_v_c_ctx', 'new_v_w_mod', 'new_v_b_mod', 'new_v_g_mix', 'new_v_w_in', 'new_v_wa_sink', 'new_v_na_rpb', 'new_v_ssm_conv_w', 'new_v_ssm_conv_b', 'new_v_ssm_dt_bias', 'new_v_ssm_a_log', 'new_v_ssm_d', 'new_v_ssm_norm_g', 'new_v_w_out', 'new_v_g_ffn', 'new_v_w_ffn_in', 'new_v_w_ffn_out', 'new_v_g_final']
TWIN_LEAF_KINDS = {'loss': 'loss', 'grad_x': 'grad_x', 'grad_c_ctx': 'grad_w', 'grad_w_mod': 'grad_w', 'grad_b_mod': 'grad_w', 'grad_g_mix': 'grad_w', 'grad_w_in': 'grad_w', 'grad_wa_sink': 'grad_w', 'grad_na_rpb': 'grad_w', 'grad_ssm_conv_w': 'grad_w', 'grad_ssm_conv_b': 'grad_w', 'grad_ssm_dt_bias': 'grad_w', 'grad_ssm_a_log': 'grad_w', 'grad_ssm_d': 'grad_w', 'grad_ssm_norm_g': 'grad_w', 'grad_w_out': 'grad_w', 'grad_g_ffn': 'grad_w', 'grad_w_ffn_in': 'grad_w', 'grad_w_ffn_out': 'grad_w', 'grad_g_final': 'grad_w', 'delta_c_ctx': 'delta_w', 'delta_w_mod': 'delta_w', 'delta_b_mod': 'delta_w', 'delta_g_mix': 'delta_w', 'delta_w_in': 'delta_w', 'delta_wa_sink': 'delta_w', 'delta_na_rpb': 'delta_w', 'delta_ssm_conv_w': 'delta_w', 'delta_ssm_conv_b': 'delta_w', 'delta_ssm_dt_bias': 'delta_w', 'delta_ssm_a_log': 'delta_w', 'delta_ssm_d': 'delta_w', 'delta_ssm_norm_g': 'delta_w', 'delta_w_out': 'delta_w', 'delta_g_ffn': 'delta_w', 'delta_w_ffn_in': 'delta_w', 'delta_w_ffn_out': 'delta_w', 'delta_g_final': 'delta_w', 'new_m_c_ctx': 'new_m', 'new_m_w_mod': 'new_m', 'new_m_b_mod': 'new_m', 'new_m_g_mix': 'new_m', 'new_m_w_in': 'new_m', 'new_m_wa_sink': 'new_m', 'new_m_na_rpb': 'new_m', 'new_m_ssm_conv_w': 'new_m', 'new_m_ssm_conv_b': 'new_m', 'new_m_ssm_dt_bias': 'new_m', 'new_m_ssm_a_log': 'new_m', 'new_m_ssm_d': 'new_m', 'new_m_ssm_norm_g': 'new_m', 'new_m_w_out': 'new_m', 'new_m_g_ffn': 'new_m', 'new_m_w_ffn_in': 'new_m', 'new_m_w_ffn_out': 'new_m', 'new_m_g_final': 'new_m', 'new_v_c_ctx': 'new_v', 'new_v_w_mod': 'new_v', 'new_v_b_mod': 'new_v', 'new_v_g_mix': 'new_v', 'new_v_w_in': 'new_v', 'new_v_wa_sink': 'new_v', 'new_v_na_rpb': 'new_v', 'new_v_ssm_conv_w': 'new_v', 'new_v_ssm_conv_b': 'new_v', 'new_v_ssm_dt_bias': 'new_v', 'new_v_ssm_a_log': 'new_v', 'new_v_ssm_d': 'new_v', 'new_v_ssm_norm_g': 'new_v', 'new_v_w_out': 'new_v', 'new_v_g_ffn': 'new_v', 'new_v_w_ffn_in': 'new_v', 'new_v_w_ffn_out': 'new_v', 'new_v_g_final': 'new_v'}


def _forward(args):
    return _fwd_reference(*[args[k] for k in FWD_PARAMS])


def _output_shape():
    def fwd():
        inp = _fwd_setup_inputs(0)
        return _fwd_reference(*[inp[k] for k in FWD_PARAMS])
    out = _jax.eval_shape(fwd)
    return out.shape, out.dtype

N_MICROBATCH = 1
ADAM_LR = 0.001
ADAM_B1 = 0.9
ADAM_B2 = 0.999
ADAM_EPS = 1e-08
ADAM_WD = 0.01
ADAM_STEP = 10
PER_EXAMPLE_BATCH_AXIS = {'x': 0, 'c': 0, 'ctx': 0, 'loss_target': 0}
SHARED_INPUTS = []
_WEIGHT_DTYPES = {'c_ctx': _jnp.float32, 'w_mod': _jnp.float32, 'b_mod': _jnp.float32, 'g_mix': _jnp.float32, 'w_in': _jnp.float32, 'wa_sink': _jnp.float32, 'na_rpb': _jnp.float32, 'ssm_conv_w': _jnp.float32, 'ssm_conv_b': _jnp.float32, 'ssm_dt_bias': _jnp.float32, 'ssm_a_log': _jnp.float32, 'ssm_d': _jnp.float32, 'ssm_norm_g': _jnp.float32, 'w_out': _jnp.float32, 'g_ffn': _jnp.float32, 'w_ffn_in': _jnp.float32, 'w_ffn_out': _jnp.float32, 'g_final': _jnp.float32}
MOMENT_SCALE = {'c_ctx': 1.366725e-02, 'w_mod': 6.459761e-02, 'b_mod': 1.192834e-01, 'g_mix': 5.273272e-02, 'w_in': 3.470901e-02, 'wa_sink': 1.239526e-04, 'na_rpb': 1.374704e-03, 'ssm_conv_w': 3.938052e-02, 'ssm_conv_b': 5.823410e-02, 'ssm_dt_bias': 1.017854e-01, 'ssm_a_log': 1.621405e-01, 'ssm_d': 2.371641e-01, 'ssm_norm_g': 5.409143e-02, 'w_out': 4.095769e-02, 'g_ffn': 5.156053e-02, 'w_ffn_in': 2.286814e-02, 'w_ffn_out': 3.740475e-02, 'g_final': 3.213572e+01}


def _to_microbatches(a, axis):
    t = _jnp.moveaxis(a, axis, 0)
    t = t.reshape((N_MICROBATCH, t.shape[0] // N_MICROBATCH) + t.shape[1:])
    return _jnp.moveaxis(t, 1, axis + 1)


def setup_inputs(seed: int = 0) -> dict:
    inp = _fwd_setup_inputs(seed)
    key = _jax.random.fold_in(_jax.random.key(seed), 7919)
    shape, _ = _output_shape()
    out = dict(inp)
    out["loss_target"] = _jax.random.normal(_jax.random.fold_in(key, 0), shape, _jnp.float32)
    for i, name in enumerate(TWIN_WEIGHTS):
        w = inp[name].astype(_jnp.float32)
        if MOMENT_SCALE is None:
            s = _jnp.sqrt(_jnp.mean(_jnp.square(w)) + 1e-30)
        else:
            s = MOMENT_SCALE[name]
        km, kv = _jax.random.split(_jax.random.fold_in(key, i + 1))
        out[name] = w
        out["m_" + name] = s * _jax.random.normal(km, w.shape, _jnp.float32)
        out["v_" + name] = (s * s) * _jax.random.uniform(kv, w.shape, _jnp.float32, 0.5, 1.5)
    if N_MICROBATCH > 1:
        for name, axis in PER_EXAMPLE_BATCH_AXIS.items():
            out[name] = _to_microbatches(out[name], axis)
    return {'x': out['x'], 'c': out['c'], 'ctx': out['ctx'], 'c_ctx': out['c_ctx'], 'w_mod': out['w_mod'], 'b_mod': out['b_mod'], 'g_mix': out['g_mix'], 'w_in': out['w_in'], 'wa_sink': out['wa_sink'], 'na_rpb': out['na_rpb'], 'ssm_conv_w': out['ssm_conv_w'], 'ssm_conv_b': out['ssm_conv_b'], 'ssm_dt_bias': out['ssm_dt_bias'], 'ssm_a_log': out['ssm_a_log'], 'ssm_d': out['ssm_d'], 'ssm_norm_g': out['ssm_norm_g'], 'w_out': out['w_out'], 'g_ffn': out['g_ffn'], 'w_ffn_in': out['w_ffn_in'], 'w_ffn_out': out['w_ffn_out'], 'g_final': out['g_final'], 'loss_target': out['loss_target'], 'm_c_ctx': out['m_c_ctx'], 'm_w_mod': out['m_w_mod'], 'm_b_mod': out['m_b_mod'], 'm_g_mix': out['m_g_mix'], 'm_w_in': out['m_w_in'], 'm_wa_sink': out['m_wa_sink'], 'm_na_rpb': out['m_na_rpb'], 'm_ssm_conv_w': out['m_ssm_conv_w'], 'm_ssm_conv_b': out['m_ssm_conv_b'], 'm_ssm_dt_bias': out['m_ssm_dt_bias'], 'm_ssm_a_log': out['m_ssm_a_log'], 'm_ssm_d': out['m_ssm_d'], 'm_ssm_norm_g': out['m_ssm_norm_g'], 'm_w_out': out['m_w_out'], 'm_g_ffn': out['m_g_ffn'], 'm_w_ffn_in': out['m_w_ffn_in'], 'm_w_ffn_out': out['m_w_ffn_out'], 'm_g_final': out['m_g_final'], 'v_c_ctx': out['v_c_ctx'], 'v_w_mod': out['v_w_mod'], 'v_b_mod': out['v_b_mod'], 'v_g_mix': out['v_g_mix'], 'v_w_in': out['v_w_in'], 'v_wa_sink': out['v_wa_sink'], 'v_na_rpb': out['v_na_rpb'], 'v_ssm_conv_w': out['v_ssm_conv_w'], 'v_ssm_conv_b': out['v_ssm_conv_b'], 'v_ssm_dt_bias': out['v_ssm_dt_bias'], 'v_ssm_a_log': out['v_ssm_a_log'], 'v_ssm_d': out['v_ssm_d'], 'v_ssm_norm_g': out['v_ssm_norm_g'], 'v_w_out': out['v_w_out'], 'v_g_ffn': out['v_g_ffn'], 'v_w_ffn_in': out['v_w_ffn_in'], 'v_w_ffn_out': out['v_w_ffn_out'], 'v_g_final': out['v_g_final']}


def _loss(weights, diff, rest, loss_target):
    with _jax.named_scope("forward"):
        args = {**rest, TWIN_DIFF_INPUT: diff, **{k: w.astype(_WEIGHT_DTYPES[k]) for k, w in weights.items()}}
        y = _forward(args)
    with _jax.named_scope("loss_head"):
        err = _jnp.square(y.astype(_jnp.float32) - loss_target)
        return 0.5 * _jnp.sum(_jnp.mean(err, axis=-1)) if err.ndim else 0.5 * err


def _adamw(w, g, m, v):
    m = ADAM_B1 * m + (1.0 - ADAM_B1) * g
    v = ADAM_B2 * v + (1.0 - ADAM_B2) * _jnp.square(g)
    m_hat = m / (1.0 - ADAM_B1 ** ADAM_STEP)
    v_hat = v / (1.0 - ADAM_B2 ** ADAM_STEP)
    delta = -ADAM_LR * (m_hat / (_jnp.sqrt(v_hat) + ADAM_EPS) + ADAM_WD * w)
    return delta, m, v


def reference(x, c, ctx, c_ctx, w_mod, b_mod, g_mix, w_in, wa_sink, na_rpb, ssm_conv_w, ssm_conv_b, ssm_dt_bias, ssm_a_log, ssm_d, ssm_norm_g, w_out, g_ffn, w_ffn_in, w_ffn_out, g_final, loss_target, m_c_ctx, m_w_mod, m_b_mod, m_g_mix, m_w_in, m_wa_sink, m_na_rpb, m_ssm_conv_w, m_ssm_conv_b, m_ssm_dt_bias, m_ssm_a_log, m_ssm_d, m_ssm_norm_g, m_w_out, m_g_ffn, m_w_ffn_in, m_w_ffn_out, m_g_final, v_c_ctx, v_w_mod, v_b_mod, v_g_mix, v_w_in, v_wa_sink, v_na_rpb, v_ssm_conv_w, v_ssm_conv_b, v_ssm_dt_bias, v_ssm_a_log, v_ssm_d, v_ssm_norm_g, v_w_out, v_g_ffn, v_w_ffn_in, v_w_ffn_out, v_g_final):
    given = dict(x=x, c=c, ctx=ctx, c_ctx=c_ctx, w_mod=w_mod, b_mod=b_mod, g_mix=g_mix, w_in=w_in, wa_sink=wa_sink, na_rpb=na_rpb, ssm_conv_w=ssm_conv_w, ssm_conv_b=ssm_conv_b, ssm_dt_bias=ssm_dt_bias, ssm_a_log=ssm_a_log, ssm_d=ssm_d, ssm_norm_g=ssm_norm_g, w_out=w_out, g_ffn=g_ffn, w_ffn_in=w_ffn_in, w_ffn_out=w_ffn_out, g_final=g_final, loss_target=loss_target, m_c_ctx=m_c_ctx, m_w_mod=m_w_mod, m_b_mod=m_b_mod, m_g_mix=m_g_mix, m_w_in=m_w_in, m_wa_sink=m_wa_sink, m_na_rpb=m_na_rpb, m_ssm_conv_w=m_ssm_conv_w, m_ssm_conv_b=m_ssm_conv_b, m_ssm_dt_bias=m_ssm_dt_bias, m_ssm_a_log=m_ssm_a_log, m_ssm_d=m_ssm_d, m_ssm_norm_g=m_ssm_norm_g, m_w_out=m_w_out, m_g_ffn=m_g_ffn, m_w_ffn_in=m_w_ffn_in, m_w_ffn_out=m_w_ffn_out, m_g_final=m_g_final, v_c_ctx=v_c_ctx, v_w_mod=v_w_mod, v_b_mod=v_b_mod, v_g_mix=v_g_mix, v_w_in=v_w_in, v_wa_sink=v_wa_sink, v_na_rpb=v_na_rpb, v_ssm_conv_w=v_ssm_conv_w, v_ssm_conv_b=v_ssm_conv_b, v_ssm_dt_bias=v_ssm_dt_bias, v_ssm_a_log=v_ssm_a_log, v_ssm_d=v_ssm_d, v_ssm_norm_g=v_ssm_norm_g, v_w_out=v_w_out, v_g_ffn=v_g_ffn, v_w_ffn_in=v_w_ffn_in, v_w_ffn_out=v_w_ffn_out, v_g_final=v_g_final)
    weights = {n: given[n] for n in TWIN_WEIGHTS}
    shared = {n: given[n] for n in SHARED_INPUTS}
    per_example = {n: given[n] for n in ['x', 'c', 'ctx']}
    grad_fn = _jax.value_and_grad(_loss, argnums=(0, 1))

    def one_microbatch(ex, loss_target):
        ex = dict(ex)
        diff = ex.pop(TWIN_DIFF_INPUT)
        return grad_fn(weights, diff, {**shared, **ex}, loss_target)

    if N_MICROBATCH == 1:
        loss, (grad_w, grad_x) = one_microbatch(per_example, given["loss_target"])
    else:
        def body(carry, xs):
            loss_sum, grad_sum = carry
            l_k, (gw_k, gx_k) = one_microbatch(xs[0], xs[1])
            with _jax.named_scope("update"):
                return (loss_sum + l_k, _jax.tree.map(_jnp.add, grad_sum, gw_k)), gx_k

        init = (_jnp.zeros((), _jnp.float32), _jax.tree.map(_jnp.zeros_like, weights))
        (loss, grad_w), grad_x = _jax.lax.scan(body, init, (per_example, given["loss_target"]))
    with _jax.named_scope("update"):
        delta_w, new_m, new_v = {}, {}, {}
        for n in TWIN_WEIGHTS:
            delta_w[n], new_m[n], new_v[n] = _adamw(weights[n], grad_w[n], given["m_" + n], given["v_" + n])
    return (loss, grad_x, *[grad_w[n] for n in TWIN_WEIGHTS], *[delta_w[n] for n in TWIN_WEIGHTS],
            *[new_m[n] for n in TWIN_WEIGHTS], *[new_v[n] for n in TWIN_WEIGHTS])
```

```python
import functools

import numpy as np
import jax
import jax.numpy as jnp
from jax import lax
from jax.experimental import pallas as pl
from jax.experimental.pallas import tpu as pltpu

F32 = jnp.float32
BF16 = jnp.bfloat16
_MXU = jnp.bfloat16
_HI = lax.Precision.HIGHEST
MESH = pl.DeviceIdType.MESH

D = 1024
HD = 64
GRID_W = 64
EPS = 1e-6
ROPE_BASE = 10000.0
WA_HEADS, WA_KV = 4, 2
WA_BLK = 128
NA_HEADS, NA_KH, NA_KW = 4, 8, 16
S_HEADS, S_P, S_INNER, S_GROUPS, S_N, S_CONV, S_Q = 8, 64, 512, 2, 128, 7, 128
D_FF = 2816
IN_COLS = 2832
IN_PAD = 2944
C_QA, C_QB, C_Z, C_KA, C_VA, C_KB, C_VB, C_XBC, C_DT = 0, 256, 512, 1024, 1152, 1280, 1536, 1792, 2816
ADAM_LR, ADAM_B1, ADAM_B2, ADAM_EPS, ADAM_WD, ADAM_STEP = 0.001, 0.9, 0.999, 1e-08, 0.01, 10

TR = 256
NEG = -1e30
VMEM_CAP = 56 * 1024 * 1024


def _pc(body, **kw):
    return pl.pallas_call(body, **kw)


def _cp(sem=None, vmem=None):
    kw = {}
    if sem is not None:
        kw["dimension_semantics"] = sem
    if vmem is not None:
        kw["vmem_limit_bytes"] = int(min(max(vmem, 16 * 1024 * 1024), VMEM_CAP))
    return pltpu.CompilerParams(**kw)


def _sds(shape, dtype):
    return jax.ShapeDtypeStruct(tuple(shape), dtype)


_DIMS = {"nn": ((1,), (0,)), "nt": ((1,), (1,)), "tn": ((0,), (0,))}


def _dg(a, b, dims):
    return lax.dot_general(a.astype(_MXU), b.astype(_MXU), (dims, ((), ())), preferred_element_type=F32)


@functools.partial(jax.custom_vjp, nondiff_argnums=(2,))
def bdot(a, b, mode):
    return _dg(a, b, _DIMS[mode])


def _bdot_fwd(a, b, mode):
    return bdot(a, b, mode), (a, b)


def _bdot_bwd(mode, res, g):
    a, b = res
    if mode == "nn":
        return bdot(g, b, "nt"), bdot(a, g, "tn")
    if mode == "nt":
        return bdot(g, b, "nn"), bdot(g, a, "tn")
    return bdot(b, g, "nt"), bdot(a, g, "nn")


bdot.defvjp(_bdot_fwd, _bdot_bwd)


def hdot(a, b, mode="nn"):
    return lax.dot_general(a, b, (_DIMS[mode], ((), ())), precision=_HI, preferred_element_type=F32)


def _silu(x):
    return x / (1.0 + jnp.exp(-x))


def _softplus(x):
    return jnp.maximum(x, 0.0) + jnp.log(1.0 + jnp.exp(-jnp.abs(x)))


def _div_tile(n, cap, mult):
    if n <= cap:
        return n
    best = None
    for t in range(mult, cap + 1, mult):
        if n % t == 0:
            best = t
    assert best is not None, (n, cap, mult)
    return best


def matmul(a, b, mode, out_dtype, name, tm=640, tn=1536, tk=1408, hi=False):
    if mode == "tn":
        K, M = a.shape
    else:
        M, K = a.shape
    N = b.shape[0] if mode == "nt" else b.shape[1]
    tm = _div_tile(M, tm, 128 if mode == "tn" else 16)
    tn = _div_tile(N, tn, 128)
    tk = _div_tile(K, tk, 128 if mode != "tn" else 16)
    nk = K // tk
    dims = _DIMS[mode]

    def body(a_ref, b_ref, o_ref, *acc):
        if hi:
            part = lax.dot_general(a_ref[...], b_ref[...], (dims, ((), ())), precision=_HI, preferred_element_type=F32)
        else:
            part = _dg(a_ref[...], b_ref[...], dims)
        if nk == 1:
            o_ref[...] = part.astype(o_ref.dtype)
        else:
            k = pl.program_id(2)

            @pl.when(k == 0)
            def _():
                acc[0][...] = part

            @pl.when(k > 0)
            def _():
                acc[0][...] += part

            @pl.when(k == nk - 1)
            def _():
                o_ref[...] = acc[0][...].astype(o_ref.dtype)

    if mode == "tn":
        a_spec = pl.BlockSpec((tk, tm), lambda i, j, k: (k, i))
    else:
        a_spec = pl.BlockSpec((tm, tk), lambda i, j, k: (i, k))
    if mode == "nt":
        b_spec = pl.BlockSpec((tn, tk), lambda i, j, k: (j, k))
    else:
        b_spec = pl.BlockSpec((tk, tn), lambda i, j, k: (k, j))
    isz = lambda x: jnp.dtype(x.dtype).itemsize
    vmem = 2 * (tm * tk * isz(a) + tk * tn * isz(b) + tm * tn * jnp.dtype(out_dtype).itemsize) + 3 * tm * tn * 4
    return _pc(
        body, name=name, grid=(M // tm, N // tn, nk),
        in_specs=[a_spec, b_spec], out_specs=pl.BlockSpec((tm, tn), lambda i, j, k: (i, j)),
        out_shape=_sds((M, N), out_dtype),
        scratch_shapes=[pltpu.VMEM((tm, tn), F32)] if nk > 1 else [],
        compiler_params=_cp(("parallel", "parallel", "arbitrary"), vmem + (8 << 20)),
    )(a, b)


def _norm_mod(xo, shift, scale, g):
    r = lax.rsqrt(jnp.mean(xo * xo, axis=-1, keepdims=True) + EPS)
    return (xo * r) * g * (1.0 + scale) + shift


def res_norm_mod(x, y, gsv, g, nL, name):
    T = x.shape[0]
    has_y = y is not None

    def body(*refs):
        if has_y:
            x_ref, y_ref, gsv_ref, g_ref, xo_ref, h_ref = refs
            xo = x_ref[...] + gsv_ref[0, 0:1, :] * y_ref[...]
            xo_ref[...] = xo
        else:
            x_ref, gsv_ref, g_ref, h_ref = refs
            xo = x_ref[...]
        h_ref[...] = _norm_mod(xo, gsv_ref[0, 1:2, :], gsv_ref[0, 2:3, :], g_ref[...]).astype(h_ref.dtype)

    row = pl.BlockSpec((TR, D), lambda i: (i, 0))
    in_specs = [row] + ([row] if has_y else []) + [pl.BlockSpec((1, 8, D), lambda i: (i // nL, 0, 0)),
                                                     pl.BlockSpec((1, D), lambda i: (0, 0))]
    out_specs = ([row] if has_y else []) + [row]
    out_shape = ([_sds((T, D), F32)] if has_y else []) + [_sds((T, D), BF16)]
    args = (x, y, gsv, g) if has_y else (x, gsv, g)
    outs = _pc(body, name=name, grid=(T // TR,), in_specs=in_specs, out_specs=out_specs, out_shape=out_shape,
               compiler_params=_cp(("arbitrary",), 24 << 20))(*args)
    return (outs[0], outs[1]) if has_y else (None, outs[0])


def res_norm_mod_bwd(xo, y, gsv, g, dh, dres, nL, name):
    T = xo.shape[0]
    has_y = y is not None

    def body(*refs):
        if has_y:
            xo_ref, y_ref, gsv_ref, g_ref, dh_ref, dres_ref, dx_ref, dy_ref, dgsv_ref, dg_ref = refs
        else:
            xo_ref, gsv_ref, g_ref, dh_ref, dres_ref, dx_ref, dgsv_ref, dg_ref = refs
        i = pl.program_id(0)

        @pl.when((i == 0) | (i == nL))
        def _():
            dgsv_ref[...] = jnp.zeros_like(dgsv_ref)

        @pl.when(i == 0)
        def _():
            dg_ref[...] = jnp.zeros_like(dg_ref)

        _, vjp = jax.vjp(_norm_mod, xo_ref[...], gsv_ref[0, 1:2, :], gsv_ref[0, 2:3, :], g_ref[...])
        dxn, dshift, dscale, dg = vjp(dh_ref[...].astype(F32))
        dxo = dres_ref[...] + dxn
        dx_ref[...] = dxo
        if has_y:
            dy_ref[...] = (gsv_ref[0, 0:1, :] * dxo).astype(dy_ref.dtype)
            dgsv_ref[0, 0:1, :] += jnp.sum(y_ref[...] * dxo, axis=0, keepdims=True)
        dgsv_ref[0, 1:2, :] += dshift
        dgsv_ref[0, 2:3, :] += dscale
        dg_ref[0:1, :] += dg

    row = pl.BlockSpec((TR, D), lambda i: (i, 0))
    gspec = pl.BlockSpec((1, 8, D), lambda i: (i // nL, 0, 0))
    in_specs = [row] + ([row] if has_y else []) + [gspec, pl.BlockSpec((1, D), lambda i: (0, 0)), row, row]
    out_specs = [row] + ([row] if has_y else []) + [gspec, pl.BlockSpec((8, D), lambda i: (0, 0))]
    out_shape = [_sds((T, D), F32)] + ([_sds((T, D), BF16)] if has_y else []) + [_sds((2, 8, D), F32), _sds((8, D), F32)]
    args = (xo, y, gsv, g, dh, dres) if has_y else (xo, gsv, g, dh, dres)
    outs = _pc(body, name=name, grid=(T // TR,), in_specs=in_specs, out_specs=out_specs, out_shape=out_shape,
               compiler_params=_cp(("arbitrary",), 32 << 20))(*args)
    if has_y:
        return outs
    return outs[0], None, outs[1], outs[2]


def final_loss(x, y, gsv, g, target, nL, name):
    T = x.shape[0]

    def lossf(xo, gv, t):
        yn = (xo * lax.rsqrt(jnp.mean(xo * xo, axis=-1, keepdims=True) + EPS)) * gv
        e = yn - t
        return 0.5 * jnp.sum(jnp.sum(e * e, axis=-1, keepdims=True) * (1.0 / D), axis=0, keepdims=True)

    def body(x_ref, y_ref, gsv_ref, g_ref, t_ref, loss_ref, dx_ref, dy_ref, dgsv_ref, dg_ref):
        i = pl.program_id(0)

        @pl.when(i == 0)
        def _():
            loss_ref[...] = jnp.zeros_like(loss_ref)
            dg_ref[...] = jnp.zeros_like(dg_ref)

        @pl.when((i == 0) | (i == nL))
        def _():
            dgsv_ref[...] = jnp.zeros_like(dgsv_ref)

        @pl.when(i < nL)
        def _():
            gate = gsv_ref[0, 0:1, :]
            yv = y_ref[...]
            xo = x_ref[...] + gate * yv
            lv, vjp = jax.vjp(lossf, xo, g_ref[...], t_ref[...])
            dxo, dg, _ = vjp(jnp.ones((1, 1), F32))
            loss_ref[...] += jnp.broadcast_to(lv, loss_ref.shape)
            dx_ref[...] = dxo
            dy_ref[...] = (gate * dxo).astype(dy_ref.dtype)
            dgsv_ref[0, 0:1, :] += jnp.sum(yv * dxo, axis=0, keepdims=True)
            dg_ref[0:1, :] += dg

        @pl.when(i >= nL)
        def _():
            dx_ref[...] = jnp.zeros_like(dx_ref)
            dy_ref[...] = jnp.zeros_like(dy_ref)

    row = pl.BlockSpec((TR, D), lambda i: (i, 0))
    gspec = pl.BlockSpec((1, 8, D), lambda i: (i // nL, 0, 0))
    return _pc(
        body, name=name, grid=(T // TR,),
        in_specs=[row, row, gspec, pl.BlockSpec((1, D), lambda i: (0, 0)),
                  pl.BlockSpec((TR, D), lambda i: (jnp.minimum(i, nL - 1), 0))],
        out_specs=[pl.BlockSpec((8, 128), lambda i: (0, 0)), row, row, gspec, pl.BlockSpec((8, D), lambda i: (0, 0))],
        out_shape=[_sds((8, 128), F32), _sds((T, D), F32), _sds((T, D), BF16), _sds((2, 8, D), F32), _sds((8, D), F32)],
        compiler_params=_cp(("arbitrary",), 32 << 20),
    )(x, y, gsv, g, target)


def _swiglu(gate, up):
    return _silu(gate) * up


def swiglu_fwd(gu, name):
    T = gu.shape[0]
    tc = D_FF // 2

    def body(g_ref, u_ref, o_ref):
        o_ref[...] = _swiglu(g_ref[...].astype(F32), u_ref[...].astype(F32)).astype(o_ref.dtype)

    return _pc(body, name=name, grid=(T // TR, 2),
               in_specs=[pl.BlockSpec((TR, tc), lambda i, j: (i, j)), pl.BlockSpec((TR, tc), lambda i, j: (i, 2 + j))],
               out_specs=pl.BlockSpec((TR, tc), lambda i, j: (i, j)), out_shape=_sds((T, D_FF), BF16),
               compiler_params=_cp(("parallel", "parallel"), 24 << 20))(gu, gu)


def swiglu_bwd(gu, dact, name):
    T = gu.shape[0]
    tc = D_FF // 2

    def body(g_ref, u_ref, d_ref, o_ref):
        j = pl.program_id(1)
        _, vjp = jax.vjp(_swiglu, g_ref[...].astype(F32), u_ref[...].astype(F32))
        dg, du = vjp(d_ref[...].astype(F32))

        @pl.when(j < 2)
        def _():
            o_ref[...] = dg.astype(o_ref.dtype)

        @pl.when(j >= 2)
        def _():
            o_ref[...] = du.astype(o_ref.dtype)

    return _pc(body, name=name, grid=(T // TR, 4),
               in_specs=[pl.BlockSpec((TR, tc), lambda i, j: (i, j % 2)), pl.BlockSpec((TR, tc), lambda i, j: (i, 2 + j % 2)),
                         pl.BlockSpec((TR, tc), lambda i, j: (i, j % 2))],
               out_specs=pl.BlockSpec((TR, tc), lambda i, j: (i, j)), out_shape=_sds((T, 2 * D_FF), BF16),
               compiler_params=_cp(("parallel", "arbitrary"), 32 << 20))(gu, gu, dact)


def rope_tables(L, Lc):
    t = np.arange(L)
    rows, cols = t // GRID_W, t % GRID_W
    inv = ROPE_BASE ** (-np.arange(16, dtype=np.float32) / 16)
    lane = np.arange(64)
    pos = np.where((lane // 32)[None, :] == 0, rows[:, None], cols[:, None]).astype(np.float32)
    ang = jnp.asarray(pos) * jnp.asarray(inv[lane % 16])[None, :]
    cos = jnp.concatenate([jnp.cos(ang), jnp.ones((Lc, 64), F32)], axis=0)
    sin = jnp.concatenate([jnp.sin(ang), jnp.zeros((Lc, 64), F32)], axis=0)
    R = np.zeros((128, 128), np.float32)
    for i in range(128):
        if (i % 32) < 16:
            R[i + 16, i] = -1.0
        else:
            R[i - 16, i] = 1.0
    return jnp.tile(cos, (1, 2)), jnp.tile(sin, (1, 2)), jnp.asarray(R)


def rope_apply(q_src, q_col, k_src, k_col, cos, sin, R, transpose, name):
    T = cos.shape[0]

    def rot(x, c, s, Rm):
        if transpose:
            return x * c + hdot(x * s, Rm, "nt")
        return x * c + hdot(x, Rm) * s

    def body(q_ref, k_ref, c_ref, s_ref, R_ref, qo_ref, ko_ref):
        c, s, Rm = c_ref[...], s_ref[...], R_ref[...]
        for j in range(2):
            qo_ref[:, j * 128:(j + 1) * 128] = rot(q_ref[:, j * 128:(j + 1) * 128].astype(F32), c, s, Rm).astype(qo_ref.dtype)
        ko_ref[...] = rot(k_ref[...].astype(F32), c, s, Rm).astype(ko_ref.dtype)

    tab = pl.BlockSpec((TR, 128), lambda i: (i, 0))
    return _pc(body, name=name, grid=(T // TR,),
               in_specs=[pl.BlockSpec((TR, 256), lambda i: (i, q_col)), pl.BlockSpec((TR, 128), lambda i: (i, k_col)),
                         tab, tab, pl.BlockSpec((128, 128), lambda i: (0, 0))],
               out_specs=[pl.BlockSpec((TR, 256), lambda i: (i, 0)), tab],
               out_shape=[_sds((T, 256), BF16), _sds((T, 128), BF16)],
               compiler_params=_cp(("parallel",), 16 << 20))(q_src, k_src, cos, sin, R)


_SCALE = HD ** -0.5


def _attn_tile(qh, ks, vs, extra):
    ss = []
    for k, add in ks:
        s = bdot(qh, k, "nt") * _SCALE
        ss.append(s if add is None else s + add)
    m = ss[0].max(axis=-1, keepdims=True)
    for s in ss[1:]:
        m = jnp.maximum(m, s.max(axis=-1, keepdims=True))
    if extra is not None:
        m = jnp.maximum(m, extra)
    ps = [jnp.exp(s - m) for s in ss]
    den = ps[0].sum(axis=-1, keepdims=True)
    for p in ps[1:]:
        den = den + p.sum(axis=-1, keepdims=True)
    if extra is not None:
        den = den + jnp.exp(extra - m)
    num = bdot(ps[0], vs[0], "nn")
    for p, v in zip(ps[1:], vs[1:]):
        num = num + bdot(p, v, "nn")
    return num / den


def _wa_mask(n, L):
    qpos = n * WA_BLK + lax.broadcasted_iota(jnp.int32, (WA_BLK, 3 * WA_BLK), 0)
    kpos = (n - 1) * WA_BLK + lax.broadcasted_iota(jnp.int32, (WA_BLK, 3 * WA_BLK), 1)
    ok = (jnp.abs(qpos - kpos) <= WA_BLK) & (kpos >= 0) & (kpos < L)
    return jnp.where(ok, 0.0, NEG).astype(F32)


def _wa_specs(L, Lc):
    nb = L // WA_BLK
    cb = L // Lc
    lat = lambda n: jnp.minimum(n, nb - 1)
    prv = lambda n: jnp.clip(n - 1, 0, nb - 1)
    nxt = lambda n: jnp.minimum(n + 1, nb - 1)
    kspecs = [pl.BlockSpec((WA_BLK, 128), lambda n: (prv(n), 0)), pl.BlockSpec((WA_BLK, 128), lambda n: (lat(n), 0)),
              pl.BlockSpec((WA_BLK, 128), lambda n: (nxt(n), 0)), pl.BlockSpec((Lc, 128), lambda n: (cb, 0))]
    vcol = C_VA // 128
    vspecs = [pl.BlockSpec((WA_BLK, 128), lambda n: (prv(n), vcol)), pl.BlockSpec((WA_BLK, 128), lambda n: (lat(n), vcol)),
              pl.BlockSpec((WA_BLK, 128), lambda n: (nxt(n), vcol)), pl.BlockSpec((Lc, 128), lambda n: (cb, vcol))]
    return nb, kspecs, vspecs


def win_attn_fwd(qr, kr, P, sink, L, Lc, name):
    T = L + Lc
    nb, kspecs, vspecs = _wa_specs(L, Lc)

    def body(q_ref, kp, kc, kn, kx, vp, vc, vn, vx, s_ref, o_ref):
        n = pl.program_id(0)

        @pl.when(n < nb)
        def _():
            mask = _wa_mask(n, L)
            for g in range(WA_KV):
                sl = slice(g * HD, (g + 1) * HD)
                k3 = jnp.concatenate([kp[:, sl], kc[:, sl], kn[:, sl]], axis=0)
                v3 = jnp.concatenate([vp[:, sl], vc[:, sl], vn[:, sl]], axis=0)
                for r in range(2):
                    h = 2 * g + r
                    o = _attn_tile(q_ref[:, h * HD:(h + 1) * HD], [(k3, mask), (kx[:, sl], None)], [v3, vx[:, sl]],
                                   s_ref[h:h + 1, 0:1])
                    o_ref[:, h * HD:(h + 1) * HD] = o.astype(o_ref.dtype)

        @pl.when(n >= nb)
        def _():
            for h in range(WA_HEADS):
                sl = slice((h // 2) * HD, (h // 2 + 1) * HD)
                o = _attn_tile(q_ref[:, h * HD:(h + 1) * HD], [(kx[:, sl], None)], [vx[:, sl]], s_ref[h:h + 1, 0:1])
                o_ref[:, h * HD:(h + 1) * HD] = o.astype(o_ref.dtype)

    qspec = pl.BlockSpec((WA_BLK, 256), lambda n: (n, 0))
    return _pc(body, name=name, grid=(T // WA_BLK,),
               in_specs=[qspec] + kspecs + vspecs + [pl.BlockSpec((8, 128), lambda n: (0, 0))],
               out_specs=qspec, out_shape=_sds((T, 256), BF16),
               compiler_params=_cp(("arbitrary",), 32 << 20))(qr, kr, kr, kr, kr, P, P, P, P, sink)


def win_attn_bwd(qr, kr, P, sink, do_src, L, Lc, name):
    T = L + Lc
    nb, kspecs, vspecs = _wa_specs(L, Lc)
    cx = WA_BLK + L

    def body(q_ref, kp, kc, kn, kx, vp, vc, vn, vx, s_ref, do_ref, dq_ref, dk_ref, dv_ref, ds_ref):
        n = pl.program_id(0)

        @pl.when(n == 0)
        def _():
            dk_ref[...] = jnp.zeros_like(dk_ref)
            dv_ref[...] = jnp.zeros_like(dv_ref)
            ds_ref[...] = jnp.zeros_like(ds_ref)

        @pl.when(n < nb)
        def _():
            mask = _wa_mask(n, L)
            rows = pl.ds(pl.multiple_of(n * WA_BLK, WA_BLK), 3 * WA_BLK)
            for g in range(WA_KV):
                sl = slice(g * HD, (g + 1) * HD)
                k3 = jnp.concatenate([kp[:, sl], kc[:, sl], kn[:, sl]], axis=0)
                v3 = jnp.concatenate([vp[:, sl], vc[:, sl], vn[:, sl]], axis=0)
                kxg, vxg = kx[:, sl], vx[:, sl]
                acc = None
                for r in range(2):
                    h = 2 * g + r
                    hs = slice(h * HD, (h + 1) * HD)
                    f = lambda q, k3_, v3_, kx_, vx_, s_: _attn_tile(q, [(k3_, mask), (kx_, None)], [v3_, vx_], s_)
                    _, vjp = jax.vjp(f, q_ref[:, hs].astype(F32), k3.astype(F32), v3.astype(F32), kxg.astype(F32),
                                     vxg.astype(F32), s_ref[h:h + 1, 0:1])
                    dq, dk3, dv3, dkx, dvx, dsk = vjp(do_ref[:, hs].astype(F32))
                    dq_ref[:, hs] = dq
                    ds_ref[h:h + 1, :] += jnp.broadcast_to(dsk, (1, 128))
                    acc = (dk3, dv3, dkx, dvx) if acc is None else tuple(a + b for a, b in zip(acc, (dk3, dv3, dkx, dvx)))
                dk_ref[rows, sl] += acc[0]
                dv_ref[rows, sl] += acc[1]
                dk_ref[cx:cx + Lc, sl] += acc[2]
                dv_ref[cx:cx + Lc, sl] += acc[3]

        @pl.when(n >= nb)
        def _():
            for h in range(WA_HEADS):
                sl = slice((h // 2) * HD, (h // 2 + 1) * HD)
                hs = slice(h * HD, (h + 1) * HD)
                f = lambda q, kx_, vx_, s_: _attn_tile(q, [(kx_, None)], [vx_], s_)
                _, vjp = jax.vjp(f, q_ref[:, hs].astype(F32), kx[:, sl].astype(F32), vx[:, sl].astype(F32), s_ref[h:h + 1, 0:1])
                dq, dkx, dvx, dsk = vjp(do_ref[:, hs].astype(F32))
                dq_ref[:, hs] = dq
                ds_ref[h:h + 1, :] += jnp.broadcast_to(dsk, (1, 128))
                dk_ref[cx:cx + Lc, sl] += dkx
                dv_ref[cx:cx + Lc, sl] += dvx

    qspec = pl.BlockSpec((WA_BLK, 256), lambda n: (n, 0))
    acc_spec = pl.BlockSpec((T + 2 * WA_BLK, 128), lambda n: (0, 0))
    return _pc(body, name=name, grid=(T // WA_BLK,),
               in_specs=[qspec] + kspecs + vspecs + [pl.BlockSpec((8, 128), lambda n: (0, 0)), qspec],
               out_specs=[qspec, acc_spec, acc_spec, pl.BlockSpec((8, 128), lambda n: (0, 0))],
               out_shape=[_sds((T, 256), F32), _sds((T + 2 * WA_BLK, 128), F32), _sds((T + 2 * WA_BLK, 128), F32), _sds((8, 128), F32)],
               compiler_params=_cp(("arbitrary",), 40 << 20))(qr, kr, kr, kr, kr, P, P, P, P, sink, do_src)


def na_index_tables():
    qc = np.arange(GRID_W)[:, None]
    kc = np.arange(GRID_W)[None, :]
    cstart = np.clip(qc - NA_KW // 2, 0, GRID_W - NA_KW)
    ok = (kc >= cstart) & (kc < cstart + NA_KW)
    dx = np.clip(kc - qc, -(NA_KW - 1), NA_KW - 1) + (NA_KW - 1)
    off = np.arange(NA_KH)[:, None]
    kr = np.arange(NA_KH)[None, :]
    dy = kr - off + (NA_KH - 1)
    return ok, dx, dy


def _na_selectors():
    ok, dx, dy = na_index_tables()
    e1 = np.zeros((GRID_W * GRID_W, 128), np.float32)
    qi, ki = np.nonzero(ok)
    e1[qi * GRID_W + ki, dx[qi, ki]] = 1.0
    e2 = np.zeros((16, NA_KH * NA_KH), np.float32)
    oi, ri = np.meshgrid(np.arange(NA_KH), np.arange(NA_KH), indexing="ij")
    e2[dy[oi, ri].ravel(), (oi * NA_KH + ri).ravel()] = 1.0
    return ok, jnp.asarray(e1), jnp.asarray(np.kron(np.eye(NA_HEADS, dtype=np.float32), e2))


def na_bias_table(rpb, tag):
    ok, e1, e2 = _na_selectors()
    r2 = jnp.pad(rpb.astype(F32), ((0, 0), (0, 1), (0, 128 - (2 * NA_KW - 1)))).reshape(NA_HEADS * 16, 128)
    r1 = matmul(e2, r2, "tn", F32, f"na_bias_sel1_{tag}", hi=True)
    x = matmul(r1, e1, "nt", F32, f"na_bias_sel2_{tag}", hi=True)
    b = x.reshape(NA_HEADS, NA_KH, NA_KH, GRID_W, GRID_W).transpose(0, 1, 3, 2, 4)
    b = b + jnp.asarray(np.where(ok, 0.0, NEG).astype(np.float32))[None, None, :, None, :]
    return b.reshape(NA_HEADS, NA_KH, GRID_W, NA_KH * GRID_W)


def _na_rows(r, GR):
    r0 = jnp.clip(r - NA_KH // 2, 0, GR - NA_KH)
    return r0, jnp.clip(r - r0, 0, NA_KH - 1)


def na_fwd(P, kb, vb, bias, L, Lc, name):
    T = L + Lc
    GR = L // GRID_W
    W = NA_KH * GRID_W

    def body(q_ref, k_ref, v_ref, b_ref, o_ref):
        r = pl.program_id(0)

        @pl.when(r < GR)
        def _():
            r0, off = _na_rows(r, GR)
            rows = pl.ds(pl.multiple_of(r0 * GRID_W, GRID_W), W)
            for h in range(NA_HEADS):
                hs = slice(h * HD, (h + 1) * HD)
                o = _attn_tile(q_ref[:, hs], [(k_ref[rows, hs], b_ref[h, off]), (k_ref[L:T, hs], None)],
                               [v_ref[rows, hs], v_ref[L:T, hs]], None)
                o_ref[:, hs] = o.astype(o_ref.dtype)

        @pl.when(r >= GR)
        def _():
            for h in range(NA_HEADS):
                hs = slice(h * HD, (h + 1) * HD)
                o = _attn_tile(q_ref[:, hs], [(k_ref[L:T, hs], None)], [v_ref[L:T, hs]], None)
                o_ref[:, hs] = o.astype(o_ref.dtype)

    one = pl.Buffered(1)
    return _pc(body, name=name, grid=(T // GRID_W,),
               in_specs=[pl.BlockSpec((GRID_W, 256), lambda r: (r, C_QB // 256)),
                         pl.BlockSpec((T, 256), lambda r: (0, 0), pipeline_mode=one),
                         pl.BlockSpec((T, 256), lambda r: (0, 0), pipeline_mode=one),
                         pl.BlockSpec((NA_HEADS, NA_KH, GRID_W, W), lambda r: (0, 0, 0, 0), pipeline_mode=one)],
               out_specs=pl.BlockSpec((GRID_W, 256), lambda r: (r, 0)), out_shape=_sds((T, 256), BF16),
               compiler_params=_cp(("arbitrary",), 32 << 20))(P, kb, vb, bias)


def na_bwd(P, kb, vb, bias, do_src, L, Lc, name):
    T = L + Lc
    GR = L // GRID_W
    W = NA_KH * GRID_W

    def body(q_ref, k_ref, v_ref, b_ref, do_ref, dq_ref, dk_ref, dv_ref, db_ref):
        r = pl.program_id(0)

        @pl.when(r == 0)
        def _():
            dk_ref[...] = jnp.zeros_like(dk_ref)
            dv_ref[...] = jnp.zeros_like(dv_ref)
            db_ref[...] = jnp.zeros_like(db_ref)

        @pl.when(r < GR)
        def _():
            r0, off = _na_rows(r, GR)
            rows = pl.ds(pl.multiple_of(r0 * GRID_W, GRID_W), W)
            for h in range(NA_HEADS):
                hs = slice(h * HD, (h + 1) * HD)
                f = lambda q, kw, vw, kx, vx, b: _attn_tile(q, [(kw, b), (kx, None)], [vw, vx], None)
                _, vjp = jax.vjp(f, q_ref[:, hs].astype(F32), k_ref[rows, hs].astype(F32), v_ref[rows, hs].astype(F32),
                                 k_ref[L:T, hs].astype(F32), v_ref[L:T, hs].astype(F32), b_ref[h, off])
                dq, dkw, dvw, dkx, dvx, db = vjp(do_ref[:, hs].astype(F32))
                dq_ref[:, hs] = dq.astype(dq_ref.dtype)
                dk_ref[rows, hs] += dkw
                dv_ref[rows, hs] += dvw
                dk_ref[L:T, hs] += dkx
                dv_ref[L:T, hs] += dvx
                db_ref[h, off] += db

        @pl.when(r >= GR)
        def _():
            for h in range(NA_HEADS):
                hs = slice(h * HD, (h + 1) * HD)
                f = lambda q, kx, vx: _attn_tile(q, [(kx, None)], [vx], None)
                _, vjp = jax.vjp(f, q_ref[:, hs].astype(F32), k_ref[L:T, hs].astype(F32), v_ref[L:T, hs].astype(F32))
                dq, dkx, dvx = vjp(do_ref[:, hs].astype(F32))
                dq_ref[:, hs] = dq.astype(dq_ref.dtype)
                dk_ref[L:T, hs] += dkx
                dv_ref[L:T, hs] += dvx

    one = pl.Buffered(1)
    full = lambda shape: pl.BlockSpec(shape, lambda r: (0,) * len(shape), pipeline_mode=one)
    return _pc(body, name=name, grid=(T // GRID_W,),
               in_specs=[pl.BlockSpec((GRID_W, 256), lambda r: (r, C_QB // 256)), full((T, 256)), full((T, 256)),
                         full((NA_HEADS, NA_KH, GRID_W, W)), pl.BlockSpec((GRID_W, 256), lambda r: (r, 1))],
               out_specs=[pl.BlockSpec((GRID_W, 256), lambda r: (r, 0)), full((T, 256)), full((T, 256)),
                          full((NA_HEADS, NA_KH, GRID_W, W))],
               out_shape=[_sds((T, 256), BF16), _sds((T, 256), F32), _sds((T, 256), F32), _sds((NA_HEADS, NA_KH, GRID_W, W), F32)],
               compiler_params=_cp(("arbitrary",), 48 << 20))(P, kb, vb, bias, do_src)


def na_rpb_grad(dbias, tag):
    _, e1, e2 = _na_selectors()
    x = dbias.reshape(NA_HEADS, NA_KH, GRID_W, NA_KH, GRID_W).transpose(0, 1, 3, 2, 4).reshape(NA_HEADS * NA_KH * NA_KH, GRID_W * GRID_W)
    r1 = matmul(x, e1, "nn", F32, f"na_rpb_sel1_{tag}", hi=True, tk=1024)
    r2 = matmul(e2, r1, "nn", F32, f"na_rpb_sel2_{tag}", hi=True)
    return r2.reshape(NA_HEADS, 16, 128)[:, :2 * NA_KH - 1, :2 * NA_KW - 1]


_HALO = 8


def _halo_specs(T, col0):
    nh = TR // _HALO
    cur = pl.BlockSpec((TR, 256), lambda i, j: (i, col0 + j))
    prv = pl.BlockSpec((_HALO, 256), lambda i, j: (jnp.maximum(i * nh - 1, 0), col0 + j))
    nxt = pl.BlockSpec((_HALO, 256), lambda i, j: (jnp.minimum((i + 1) * nh, T // _HALO - 1), col0 + j))
    return prv, cur, nxt


def _fill_ext(ext, prv, cur, nxt, i, nL, nT):
    has_prev = jnp.where((i != 0) & (i != nL), 1.0, 0.0)
    has_next = jnp.where((i != nL - 1) & (i != nT - 1), 1.0, 0.0)
    ext[0:_HALO, :] = prv[...].astype(F32) * has_prev
    ext[_HALO:_HALO + TR, :] = cur[...].astype(F32)
    ext[_HALO + TR:, :] = nxt[...].astype(F32) * has_next


def conv_silu_fwd(P, w8, b, nL, name):
    T = P.shape[0]
    nT = T // TR

    def body(prv, cur, nxt, w_ref, b_ref, pre_ref, act_ref, ext):
        i = pl.program_id(0)
        _fill_ext(ext, prv, cur, nxt, i, nL, nT)
        y = jnp.broadcast_to(b_ref[...], (TR, 256))
        for k in range(S_CONV):
            y = y + w_ref[k:k + 1, :] * ext[pl.ds(_HALO - S_CONV // 2 + k, TR), :]
        pre_ref[...] = y
        act_ref[...] = _silu(y)

    prv, cur, nxt = _halo_specs(T, C_XBC // 256)
    out = pl.BlockSpec((TR, 256), lambda i, j: (i, j))
    return _pc(body, name=name, grid=(nT, 4),
               in_specs=[prv, cur, nxt, pl.BlockSpec((8, 256), lambda i, j: (0, j)), pl.BlockSpec((1, 256), lambda i, j: (0, j))],
               out_specs=[out, out], out_shape=[_sds((T, 1024), F32), _sds((T, 1024), F32)],
               scratch_shapes=[pltpu.VMEM((TR + 2 * _HALO, 256), F32)],
               compiler_params=_cp(("parallel", "parallel"), 16 << 20))(P, P, P, w8, b)


def dsilu(pre, dxs_list, db_list, dc_list, name):
    T = pre.shape[0]
    n1, n2, n3 = len(dxs_list), len(db_list), len(dc_list)

    def body(*refs):
        pre_ref = refs[0]
        ins = refs[1:1 + n1 + n2 + n3]
        out = refs[-1]

        def part(rs, lo, hi):
            g = rs[0][...].astype(F32)
            for r in rs[1:]:
                g = g + r[...].astype(F32)
            _, vjp = jax.vjp(_silu, pre_ref[:, lo:hi])
            out[:, lo:hi] = vjp(g)[0]

        part(ins[:n1], 0, 512)
        part(ins[n1:n1 + n2], 512, 768)
        part(ins[n1 + n2:], 768, 1024)

    spec = lambda w: pl.BlockSpec((TR, w), lambda i: (i, 0))
    return _pc(body, name=name, grid=(T // TR,),
               in_specs=[spec(1024)] + [spec(512)] * n1 + [spec(256)] * (n2 + n3),
               out_specs=spec(1024), out_shape=_sds((T, 1024), F32),
               compiler_params=_cp(("parallel",), 32 << 20))(pre, *dxs_list, *db_list, *dc_list)


def conv_bwd(dpre, P, w8, nL, name):
    T = P.shape[0]
    nT = T // TR

    def body(dp, dc, dn, xp, xc, xn, w_ref, dx_ref, dw_ref, db_ref, extd, extx):
        i = pl.program_id(1)
        _fill_ext(extd, dp, dc, dn, i, nL, nT)
        _fill_ext(extx, xp, xc, xn, i, nL, nT)

        @pl.when(i == 0)
        def _():
            dw_ref[...] = jnp.zeros_like(dw_ref)
            db_ref[...] = jnp.zeros_like(db_ref)

        d = dc[...]
        dx = jnp.zeros((TR, 256), F32)
        for k in range(S_CONV):
            dx = dx + w_ref[k:k + 1, :] * extd[pl.ds(_HALO + S_CONV // 2 - k, TR), :]
            dw_ref[k:k + 1, :] += jnp.sum(d * extx[pl.ds(_HALO - S_CONV // 2 + k, TR), :], axis=0, keepdims=True)
        dx_ref[...] = dx.astype(dx_ref.dtype)
        db_ref[0:1, :] += jnp.sum(d, axis=0, keepdims=True)

    def swap(spec):
        f = spec.index_map
        return pl.BlockSpec(spec.block_shape, lambda j, i: f(i, j))

    dprv, dcur, dnxt = [swap(s) for s in _halo_specs(T, 0)]
    xprv, xcur, xnxt = [swap(s) for s in _halo_specs(T, C_XBC // 256)]
    acc = pl.BlockSpec((8, 256), lambda j, i: (0, j))
    return _pc(body, name=name, grid=(4, nT),
               in_specs=[dprv, dcur, dnxt, xprv, xcur, xnxt, acc],
               out_specs=[pl.BlockSpec((TR, 256), lambda j, i: (i, j)), acc, acc],
               out_shape=[_sds((T, 1024), BF16), _sds((8, 1024), F32), _sds((8, 1024), F32)],
               scratch_shapes=[pltpu.VMEM((TR + 2 * _HALO, 256), F32), pltpu.VMEM((TR + 2 * _HALO, 256), F32)],
               compiler_params=_cp(("parallel", "arbitrary"), 16 << 20))(dpre, dpre, dpre, P, P, P, w8)


def _onehot_row(h, n):
    return (lax.broadcasted_iota(jnp.int32, (1, n), 1) == h).astype(F32)


def _onehot_col(h, n):
    return (lax.broadcasted_iota(jnp.int32, (n, 1), 0) == h).astype(F32)


def _ssd_chunk(xs, dtr, dtb, alog, bm, cm, hin, reverse):
    Qn = S_Q
    ii = lax.broadcasted_iota(jnp.int32, (Qn, Qn), 0)
    jj = lax.broadcasted_iota(jnp.int32, (Qn, Qn), 1)
    keep = (ii <= jj) if reverse else (ii >= jj)
    tri = keep.astype(F32)
    triT = ((jj <= ii) if reverse else (jj >= ii)).astype(F32)
    eye = (ii == jj).astype(F32)
    dt = _softplus(dtr + dtb)
    a = dt * (-jnp.exp(alog))
    cs = hdot(tri, a)
    csT = hdot(a, triT, "tn")
    dtT = hdot(dt, eye, "tn")
    last = _onehot_row(0 if reverse else Qn - 1, Qn)
    ys, houts = [], []
    for g in range(S_GROUPS):
        G = bdot(cm[g], bm[g], "nt")
        for r in range(S_HEADS // S_GROUPS):
            h = g * (S_HEADS // S_GROUPS) + r
            eh_r, eh_c = _onehot_row(h, S_HEADS), _onehot_col(h, S_HEADS)
            cs_c = jnp.sum(cs * eh_r, axis=1, keepdims=True)
            dt_c = jnp.sum(dt * eh_r, axis=1, keepdims=True)
            cs_r = jnp.sum(csT * eh_c, axis=0, keepdims=True)
            dt_r = jnp.sum(dtT * eh_c, axis=0, keepdims=True)
            tot = jnp.sum(cs_r * last, axis=1, keepdims=True)
            decay = jnp.exp(jnp.where(keep, cs_c - cs_r, NEG))
            w = G * decay * dt_r
            y = bdot(w, xs[h], "nn") + bdot(cm[g], hin[h], "nt") * jnp.exp(cs_c)
            xsc = xs[h] * (jnp.exp(tot - cs_c) * dt_c)
            hout = hin[h] * jnp.exp(tot) + bdot(xsc, bm[g], "tn")
            ys.append(y)
            houts.append(hout)
    return ys, houts


def _ssd_orders(L, Lc):
    nl, ncx = L // S_Q, Lc // S_Q
    fwd = lambda s: jnp.where(s < ncx, nl + s, s - ncx)
    bwd = lambda s: nl + ncx - 1 - s
    return nl + ncx, fwd, bwd


def _ssd_in_specs(fo, bo, step):
    def at(order, w, col):
        return pl.BlockSpec((S_Q, w), lambda u: (order(step(u)), col))
    specs = []
    for order in (fo, bo):
        specs += [at(order, 512, 0), at(order, 256, 2), at(order, 256, 3), at(order, 128, C_DT // 128)]
    return specs


def ssd_fwd(act, P, dtb, alog, L, Lc, name):
    T = L + Lc
    ns, fo, bo = _ssd_orders(L, Lc)

    def body(xf, bf, cf, df, xb, bb, cb, db, dtb_ref, al_ref, yf, yb, hsf, hsb, Hf, Hb):
        s = pl.program_id(0)

        @pl.when(s == 0)
        def _():
            Hf[...] = jnp.zeros_like(Hf)
            Hb[...] = jnp.zeros_like(Hb)

        for d, (x_r, b_r, c_r, dt_r, y_r, hs_r, H) in enumerate(((xf, bf, cf, df, yf, hsf, Hf), (xb, bb, cb, db, yb, hsb, Hb))):
            hin = [H[h] for h in range(S_HEADS)]
            hs_r[0] = H[...]
            ys, houts = _ssd_chunk(
                [x_r[:, h * S_P:(h + 1) * S_P] for h in range(S_HEADS)], dt_r[:, d * 8:(d + 1) * 8],
                dtb_ref[d:d + 1, 0:8], al_ref[d:d + 1, 0:8],
                [b_r[:, g * S_N:(g + 1) * S_N] for g in range(S_GROUPS)], [c_r[:, g * S_N:(g + 1) * S_N] for g in range(S_GROUPS)],
                hin, reverse=(d == 1))
            for h in range(S_HEADS):
                y_r[:, h * S_P:(h + 1) * S_P] = ys[h]
                H[h] = houts[h]

    ident = lambda u: u
    small = pl.BlockSpec((8, 128), lambda u: (0, 0))
    hspec = pl.BlockSpec((1, S_HEADS, S_P, S_N), lambda u: (u, 0, 0, 0))
    return _pc(body, name=name, grid=(ns,),
               in_specs=_ssd_in_specs(fo, bo, ident) + [small, small],
               out_specs=[pl.BlockSpec((S_Q, 512), lambda u: (fo(u), 0)), pl.BlockSpec((S_Q, 512), lambda u: (bo(u), 0)), hspec, hspec],
               out_shape=[_sds((T, 512), F32), _sds((T, 512), F32), _sds((ns, S_HEADS, S_P, S_N), F32), _sds((ns, S_HEADS, S_P, S_N), F32)],
               scratch_shapes=[pltpu.VMEM((S_HEADS, S_P, S_N), F32), pltpu.VMEM((S_HEADS, S_P, S_N), F32)],
               compiler_params=_cp(("arbitrary",), 32 << 20))(act, act, act, P, act, act, act, P, dtb, alog)


def ssd_bwd(act, P, dtb, alog, hsf, hsb, dy, L, Lc, name):
    T = L + Lc
    ns, fo, bo = _ssd_orders(L, Lc)
    step = lambda u: ns - 1 - u

    def body(xf, bf, cf, df, xb, bb, cb, db, dtb_ref, al_ref, hsf_r, hsb_r, dyf, dyb,
             dxf, dbf, dcf, ddf, dxb, dbb, dcb, ddb, ddtb, dal, dHf, dHb):
        u = pl.program_id(0)

        @pl.when(u == 0)
        def _():
            dHf[...] = jnp.zeros_like(dHf)
            dHb[...] = jnp.zeros_like(dHb)
            ddtb[...] = jnp.zeros_like(ddtb)
            dal[...] = jnp.zeros_like(dal)

        dirs = ((xf, bf, cf, df, hsf_r, dyf, dxf, dbf, dcf, ddf, dHf), (xb, bb, cb, db, hsb_r, dyb, dxb, dbb, dcb, ddb, dHb))
        for d, (x_r, b_r, c_r, dt_r, hs_r, dy_r, dx_o, db_o, dc_o, dd_o, dH) in enumerate(dirs):
            f = functools.partial(_ssd_chunk, reverse=(d == 1))
            _, vjp = jax.vjp(
                f, [x_r[:, h * S_P:(h + 1) * S_P] for h in range(S_HEADS)], dt_r[:, d * 8:(d + 1) * 8],
                dtb_ref[d:d + 1, 0:8], al_ref[d:d + 1, 0:8],
                [b_r[:, g * S_N:(g + 1) * S_N] for g in range(S_GROUPS)], [c_r[:, g * S_N:(g + 1) * S_N] for g in range(S_GROUPS)],
                [hs_r[0, h] for h in range(S_HEADS)])
            gx, gdt, gdtb, gal, gb, gc, gh = vjp(([dy_r[:, h * S_P:(h + 1) * S_P] for h in range(S_HEADS)],
                                                  [dH[h] for h in range(S_HEADS)]))
            for h in range(S_HEADS):
                dx_o[:, h * S_P:(h + 1) * S_P] = gx[h]
                dH[h] = gh[h]
            for g in range(S_GROUPS):
                db_o[:, g * S_N:(g + 1) * S_N] = gb[g]
                dc_o[:, g * S_N:(g + 1) * S_N] = gc[g]
            dd_o[...] = gdt
            ddtb[d:d + 1, 0:8] += gdtb
            dal[d:d + 1, 0:8] += gal

    small = pl.BlockSpec((8, 128), lambda u: (0, 0))
    hspec = pl.BlockSpec((1, S_HEADS, S_P, S_N), lambda u: (step(u), 0, 0, 0))
    at = lambda order, w: pl.BlockSpec((S_Q, w), lambda u: (order(step(u)), 0))
    outs = []
    for order in (fo, bo):
        outs += [at(order, 512), at(order, 256), at(order, 256), at(order, 8)]
    oshape = [_sds((T, 512), F32), _sds((T, 256), F32), _sds((T, 256), F32), _sds((T, 8), F32)]
    return _pc(body, name=name, grid=(ns,),
               in_specs=_ssd_in_specs(fo, bo, step) + [small, small, hspec, hspec, at(fo, 512), at(bo, 512)],
               out_specs=outs + [small, small], out_shape=oshape + oshape + [_sds((8, 128), F32), _sds((8, 128), F32)],
               scratch_shapes=[pltpu.VMEM((S_HEADS, S_P, S_N), F32), pltpu.VMEM((S_HEADS, S_P, S_N), F32)],
               compiler_params=_cp(("arbitrary",), 40 << 20))(act, act, act, P, act, act, act, P, dtb, alog, hsf, hsb, dy, dy)


def _ssm_out(yf, yb, xs, z, dskip, g):
    y = (yf + yb + dskip * xs) * _silu(z)
    return (y * lax.rsqrt(jnp.mean(y * y, axis=-1, keepdims=True) + EPS)) * g


def ssm_out_fwd(yf, yb, act, P, dskip, g, name):
    T = yf.shape[0]

    def body(yf_r, yb_r, xs_r, z_r, d_r, g_r, o_r):
        o_r[...] = _ssm_out(yf_r[...], yb_r[...], xs_r[...], z_r[...], d_r[...], g_r[...]).astype(o_r.dtype)

    row = pl.BlockSpec((TR, 512), lambda i: (i, 0))
    vec = pl.BlockSpec((1, 512), lambda i: (0, 0))
    return _pc(body, name=name, grid=(T // TR,),
               in_specs=[row, row, row, pl.BlockSpec((TR, 512), lambda i: (i, C_Z // 512)), vec, vec],
               out_specs=row, out_shape=_sds((T, 512), BF16),
               compiler_params=_cp(("parallel",), 16 << 20))(yf, yb, act, P, dskip, g)


def ssm_out_bwd(yf, yb, act, P, dskip, g, do_src, name):
    T = yf.shape[0]

    def body(yf_r, yb_r, xs_r, z_r, d_r, g_r, do_r, dy_r, dxs_r, dz_r, dv_r):
        @pl.when(pl.program_id(0) == 0)
        def _():
            dv_r[...] = jnp.zeros_like(dv_r)

        _, vjp = jax.vjp(_ssm_out, yf_r[...], yb_r[...], xs_r[...], z_r[...], d_r[...], g_r[...])
        dyf, _, dxs, dz, dd, dg = vjp(do_r[...].astype(F32))
        dy_r[...] = dyf
        dxs_r[...] = dxs
        dz_r[...] = dz.astype(dz_r.dtype)
        dv_r[0:1, :] += dd
        dv_r[1:2, :] += dg

    row = pl.BlockSpec((TR, 512), lambda i: (i, 0))
    vec = pl.BlockSpec((1, 512), lambda i: (0, 0))
    return _pc(body, name=name, grid=(T // TR,),
               in_specs=[row, row, row, pl.BlockSpec((TR, 512), lambda i: (i, C_Z // 512)), vec, vec,
                         pl.BlockSpec((TR, 512), lambda i: (i, 1))],
               out_specs=[row, row, row, pl.BlockSpec((8, 512), lambda i: (0, 0))],
               out_shape=[_sds((T, 512), F32), _sds((T, 512), F32), _sds((T, 512), BF16), _sds((8, 512), F32)],
               compiler_params=_cp(("arbitrary",), 24 << 20))(yf, yb, act, P, dskip, g, do_src)


def add_pair(a, b, out_dtype, name):
    R = a.shape[0]
    tr = _div_tile(R, 4096, 16)

    def body(a_ref, b_ref, o_ref):
        o_ref[...] = (a_ref[...].astype(F32) + b_ref[...].astype(F32)).astype(o_ref.dtype)

    spec = pl.BlockSpec((tr, 128), lambda i: (i, 0))
    return _pc(body, name=name, grid=(R // tr,), in_specs=[spec, spec], out_specs=spec, out_shape=_sds((R, 128), out_dtype),
               compiler_params=_cp(("parallel",), 16 << 20))(a, b)


def sum_slots(a, name):
    n, R, _ = a.shape
    tr = _div_tile(R, 2048, 16)

    def body(a_ref, o_ref):
        acc = a_ref[0].astype(F32)
        for k in range(1, n):
            acc = acc + a_ref[k].astype(F32)
        o_ref[...] = acc

    return _pc(body, name=name, grid=(R // tr,), in_specs=[pl.BlockSpec((n, tr, 128), lambda i: (0, i, 0))],
               out_specs=pl.BlockSpec((tr, 128), lambda i: (i, 0)), out_shape=_sds((R, 128), F32),
               compiler_params=_cp(("parallel",), 16 << 20))(a)


def adamw(w, g, m, v, name):
    R, C = w.shape
    tr = _div_tile(R, max(8, (1 << 19) // max(C, 1) // 8 * 8), 8) if R % 8 == 0 else R
    c1 = 1.0 / (1.0 - ADAM_B1 ** ADAM_STEP)
    c2 = 1.0 / (1.0 - ADAM_B2 ** ADAM_STEP)

    def body(w_ref, g_ref, m_ref, v_ref, d_ref, mo_ref, vo_ref):
        gg = g_ref[...]
        mn = ADAM_B1 * m_ref[...] + (1.0 - ADAM_B1) * gg
        vn = ADAM_B2 * v_ref[...] + (1.0 - ADAM_B2) * (gg * gg)
        d_ref[...] = -ADAM_LR * ((mn * c1) / (jnp.sqrt(vn * c2) + ADAM_EPS) + ADAM_WD * w_ref[...])
        mo_ref[...] = mn
        vo_ref[...] = vn

    spec = pl.BlockSpec((tr, C), lambda i: (i, 0))
    return _pc(body, name=name, grid=(R // tr,), in_specs=[spec] * 4, out_specs=[spec] * 3,
               out_shape=[_sds((R, C), F32)] * 3, compiler_params=_cp(("parallel",), 32 << 20))(w, g, m, v)


def _me():
    return lax.axis_index("x"), lax.axis_index("y"), lax.axis_index("c")


def _flip(v, bit):
    return 1 - v if bit else v


def allgather8(xv, name):
    R = xv.shape[0]

    def body(x_ref, out_ref, sum_ref, send_sems, recv_sems):
        mx, my, mc = _me()
        me = 4 * mx + 2 * my + mc
        out_ref[me] = x_ref[...]
        sends, recvs = [], []
        for k in range(1, 8):
            px, py, pc = _flip(mx, k & 4), _flip(my, k & 2), _flip(mc, k & 1)
            peer = 4 * px + 2 * py + pc
            sends.append(pltpu.make_async_remote_copy(src_ref=x_ref, dst_ref=out_ref.at[me], send_sem=send_sems.at[k - 1],
                                                      recv_sem=recv_sems.at[k - 1], device_id=(px, py, pc), device_id_type=MESH))
            recvs.append(pltpu.make_async_remote_copy(src_ref=x_ref, dst_ref=out_ref.at[peer], send_sem=send_sems.at[k - 1],
                                                      recv_sem=recv_sems.at[k - 1], device_id=(px, py, pc), device_id_type=MESH))
        for cp in sends:
            cp.start()
        for cp in recvs:
            cp.wait_recv()
        for cp in sends:
            cp.wait_send()
        acc = out_ref[0]
        for d in range(1, 8):
            acc = acc + out_ref[d]
        sum_ref[...] = acc

    vm = pl.BlockSpec(memory_space=pltpu.VMEM)
    return _pc(body, name=name, in_specs=[vm], out_specs=[vm, vm], out_shape=[_sds((8, R, 128), F32), _sds((R, 128), F32)],
               scratch_shapes=[pltpu.SemaphoreType.DMA((7,)), pltpu.SemaphoreType.DMA((7,))],
               compiler_params=_cp(None, 32 << 20))(xv)


def _other_chips(mx, my):
    return [(1 - mx, my), (mx, 1 - my), (1 - mx, 1 - my)]


def allgather_chips(xv, name):
    R = xv.shape[0]
    H = R // 2

    def body(x_ref, out_ref, send_sems, recv_sems, local_sem):
        mx, my, mc = _me()
        chip = 2 * mx + my
        sib = (mx, my, 1 - mc)
        mine_rows = pl.ds(pl.multiple_of(mc * H, 16), H)
        sib_rows = pl.ds(pl.multiple_of((1 - mc) * H, 16), H)
        local = pltpu.make_async_copy(x_ref, out_ref.at[chip], local_sem)
        local.start()
        chips = _other_chips(mx, my)
        idx = [2 * cx + cy for cx, cy in chips]

        def cp(k, src, dst, to):
            return pltpu.make_async_remote_copy(src_ref=src, dst_ref=dst, send_sem=send_sems.at[k], recv_sem=recv_sems.at[k],
                                                device_id=to, device_id_type=MESH)

        first = [cp(j, x_ref.at[mine_rows], out_ref.at[chip, mine_rows], (cx, cy, mc)) for j, (cx, cy) in enumerate(chips)]
        for c_ in first:
            c_.start()
        passed = [cp(3 + j, out_ref.at[idx[j], mine_rows], out_ref.at[idx[j], mine_rows], sib) for j in range(3)]
        for j in range(3):
            cp(j, x_ref.at[mine_rows], out_ref.at[idx[j], mine_rows], sib).wait_recv()
            passed[j].start()
        for j in range(3):
            cp(3 + j, x_ref.at[sib_rows], out_ref.at[idx[j], sib_rows], sib).wait_recv()
        for c_ in first + passed:
            c_.wait_send()
        local.wait()

    hbm = pl.BlockSpec(memory_space=pl.ANY)
    return _pc(body, name=name, in_specs=[hbm], out_specs=hbm, out_shape=_sds((4, R, 128), xv.dtype),
               scratch_shapes=[pltpu.SemaphoreType.DMA((6,)), pltpu.SemaphoreType.DMA((6,)), pltpu.SemaphoreType.DMA])(xv)


def sibling_swap(xv, name):
    def body(x_ref, out_ref, send_sem, recv_sem):
        mx, my, mc = _me()
        cp = pltpu.make_async_remote_copy(src_ref=x_ref, dst_ref=out_ref, send_sem=send_sem, recv_sem=recv_sem,
                                          device_id=(mx, my, 1 - mc), device_id_type=MESH)
        cp.start()
        cp.wait()

    hbm = pl.BlockSpec(memory_space=pl.ANY)
    return _pc(body, name=name, in_specs=[hbm], out_specs=hbm, out_shape=_sds(xv.shape, xv.dtype),
               scratch_shapes=[pltpu.SemaphoreType.DMA, pltpu.SemaphoreType.DMA])(xv)


def scatter_chips(sv, name):
    def body(s_ref, out_ref, send_sems, recv_sems, local_sem):
        mx, my, mc = _me()
        chip = 2 * mx + my
        local = pltpu.make_async_copy(s_ref.at[chip], out_ref.at[chip], local_sem)
        local.start()
        chips = _other_chips(mx, my)
        idx = [2 * cx + cy for cx, cy in chips]

        def cp(j, src, dst, to):
            return pltpu.make_async_remote_copy(src_ref=src, dst_ref=dst, send_sem=send_sems.at[j], recv_sem=recv_sems.at[j],
                                                device_id=to, device_id_type=MESH)

        sends = [cp(j, s_ref.at[idx[j]], out_ref.at[chip], (cx, cy, mc)) for j, (cx, cy) in enumerate(chips)]
        for c_ in sends:
            c_.start()
        for j, (cx, cy) in enumerate(chips):
            cp(j, s_ref.at[idx[j]], out_ref.at[idx[j]], (cx, cy, mc)).wait_recv()
        for c_ in sends:
            c_.wait_send()
        local.wait()

    hbm = pl.BlockSpec(memory_space=pl.ANY)
    return _pc(body, name=name, in_specs=[hbm], out_specs=hbm, out_shape=_sds(sv.shape, sv.dtype),
               scratch_shapes=[pltpu.SemaphoreType.DMA((3,)), pltpu.SemaphoreType.DMA((3,)), pltpu.SemaphoreType.DMA])(sv)


_BIG = (("w_in", (D, IN_COLS), 1), ("w_out", (D, D), 0), ("w_ffn_in", (D, 2 * D_FF), 1), ("w_ffn_out", (D_FF, D), 0))
N_CHIPS = 4
DEPTH = 2


def _shard_shape(shape, axis):
    s = list(shape)
    s[axis] //= N_CHIPS
    return tuple(s)


PAYLOAD_ROWS = 1024


def _pack_rows(parts):
    flat = jnp.concatenate([p.reshape(-1) for p in parts])
    flat = jnp.pad(flat, (0, -flat.shape[0] % (PAYLOAD_ROWS * 128)))
    return flat.reshape(-1, 128)


def pack_own_shards(sh):
    return _pack_rows([sh[n][l] for l in range(DEPTH) for n, _, _ in _BIG])


def unpack_shards(rows):
    flat = rows.reshape(-1)
    out = {n: [] for n, _, _ in _BIG}
    o = 0
    for l in range(DEPTH):
        for n, shape, ax in _BIG:
            ss = _shard_shape(shape, ax)
            sz = ss[0] * ss[1]
            out[n].append(flat[o:o + sz].reshape(ss))
            o += sz
    return {n: jnp.stack(v) for n, v in out.items()}


def unpack_gathered(g):
    per_chip = [unpack_shards(g[k]) for k in range(N_CHIPS)]
    return {n: [jnp.concatenate([per_chip[k][n][l] for k in range(N_CHIPS)], axis=ax) for l in range(DEPTH)] for n, _, ax in _BIG}


def pack_for_chips(full):
    slots = []
    for k in range(N_CHIPS):
        parts = []
        for l in range(DEPTH):
            for n, shape, ax in _BIG:
                ss = _shard_shape(shape, ax)
                parts.append(lax.slice_in_dim(full[n][l], k * ss[ax], (k + 1) * ss[ax], axis=ax))
        slots.append(_pack_rows(parts))
    return jnp.stack(slots)


def _pad_rows(v, mult=8):
    n = v.shape[0]
    rows = -(-n // 128)
    rows = -(-rows // mult) * mult
    return jnp.pad(v, (0, rows * 128 - n)).reshape(rows, 128)


class _Flat:
    def __init__(self):
        self.items = []

    def add(self, name, a):
        self.items.append((name, a.shape, a.reshape(-1).astype(F32)))

    def rows(self):
        return _pad_rows(jnp.concatenate([a for _, _, a in self.items]))

    def split(self, rows):
        flat = rows.reshape(-1)
        out, o = {}, 0
        for name, shape, a in self.items:
            out[name] = flat[o:o + a.shape[0]].reshape(shape)
            o += a.shape[0]
        return out

    def split_lead(self, rows3):
        n = rows3.shape[0]
        flat = rows3.reshape(n, -1)
        out, o = {}, 0
        for name, shape, a in self.items:
            out[name] = flat[:, o:o + a.shape[0]].reshape((n,) + tuple(shape))
            o += a.shape[0]
        return out


def _gsv(rows):
    z = jnp.zeros((2, D), F32)
    r = [z if a is None else a for a in rows] + [z] * 5
    return jnp.stack(r, axis=1)


def _pad8(a, rows=8, cols=128):
    return jnp.zeros((rows, cols), F32).at[:a.shape[0], :a.shape[1]].set(a.astype(F32))


def kernel(x, c, ctx, c_ctx, w_mod, b_mod, g_mix, w_in, wa_sink, na_rpb, ssm_conv_w, ssm_conv_b, ssm_dt_bias, ssm_a_log, ssm_d, ssm_norm_g, w_out, g_ffn, w_ffn_in, w_ffn_out, g_final, loss_target, m_c_ctx, m_w_mod, m_b_mod, m_g_mix, m_w_in, m_wa_sink, m_na_rpb, m_ssm_conv_w, m_ssm_conv_b, m_ssm_dt_bias, m_ssm_a_log, m_ssm_d, m_ssm_norm_g, m_w_out, m_g_ffn, m_w_ffn_in, m_w_ffn_out, m_g_final, v_c_ctx, v_w_mod, v_b_mod, v_g_mix, v_w_in, v_wa_sink, v_na_rpb, v_ssm_conv_w, v_ssm_conv_b, v_ssm_dt_bias, v_ssm_a_log, v_ssm_d, v_ssm_norm_g, v_w_out, v_g_ffn, v_w_ffn_in, v_w_ffn_out, v_g_final):
    L, Lc = x.shape[1], ctx.shape[1]
    T = L + Lc
    nL = L // TR
    mx, my, mc = lax.axis_index("x"), lax.axis_index("y"), lax.axis_index("c")
    dev = 4 * mx + 2 * my + mc
    chip = 2 * mx + my
    MODW = 6 * D // N_CHIPS
    CW = 1024 // N_CHIPS

    sc = _silu(c.astype(F32))
    scc = _silu(c_ctx.astype(F32))[None]
    f1 = _Flat()
    f1.add("sc", sc)
    f1.add("conv_w", ssm_conv_w)
    g1, _ = allgather8(f1.rows(), "gather_cond")
    g1 = f1.split_lead(g1)
    sc_all = g1["sc"][:, 0]
    conv_w = jnp.concatenate([g1["conv_w"][2 * k] for k in range(N_CHIPS)], axis=-1)
    A16 = jnp.concatenate([sc_all, scc, jnp.zeros((7, D), F32)], axis=0)

    mod_part = jnp.stack([matmul(A16, w_mod[l], "nn", F32, f"mod_fwd{l}") for l in range(DEPTH)])
    f2 = _Flat()
    f2.add("mod", mod_part)
    g2, _ = allgather8(f2.rows(), "gather_mod")
    g2 = f2.split_lead(g2)["mod"]
    mods = jnp.concatenate([g2[2 * k] for k in range(N_CHIPS)], axis=-1) + b_mod[:, None, :]
    mod_l = lax.dynamic_index_in_dim(mods, dev, axis=1, keepdims=False).reshape(DEPTH, 6, D)
    mod_c = mods[:, 8].reshape(DEPTH, 6, D)
    mod = jnp.stack([mod_l, mod_c], axis=1)
    mrow = lambda l, j: mod[l, :, j]

    own = {"w_in": w_in, "w_out": w_out, "w_ffn_in": w_ffn_in, "w_ffn_out": w_ffn_out}
    wfull = unpack_gathered(allgather_chips(pack_own_shards({n: a.astype(BF16) for n, a in own.items()}), "gather_weights"))
    W_in = [jnp.pad(w, ((0, 0), (0, IN_PAD - IN_COLS))) for w in wfull["w_in"]]
    W_out, W_fi, W_fo = wfull["w_out"], wfull["w_ffn_in"], wfull["w_ffn_out"]

    cos, sin, rotm = rope_tables(L, Lc)
    x0 = jnp.concatenate([x[0], ctx[0]], axis=0).astype(F32)

    sv = []
    xin = x0
    gsv_first = _gsv([None, mrow(0, 0), mrow(0, 1)])
    _, h1 = res_norm_mod(x0, None, gsv_first, g_mix[0][None], nL, "norm_first")
    for l in range(DEPTH):
        s = {"xin": xin, "h1": h1}
        P = matmul(h1, W_in[l], "nn", F32, f"in_proj{l}", tn=IN_PAD)
        qr, kr = rope_apply(P, C_QA // 256, P, C_KA // 128, cos, sin, rotm, False, f"rope{l}")
        sink8 = _pad8(jnp.broadcast_to(wa_sink[l][:, None], (WA_HEADS, 128)))
        oa = win_attn_fwd(qr, kr, P, sink8, L, Lc, f"wa_fwd{l}")
        kb, vb = P[:, C_KB:C_KB + 256].astype(BF16), P[:, C_VB:C_VB + 256].astype(BF16)
        bias = na_bias_table(na_rpb[l], l)
        ob = na_fwd(P, kb, vb, bias, L, Lc, f"na_fwd{l}")
        w8 = jnp.concatenate([conv_w[l], jnp.zeros((1, 1024), F32)], axis=0)
        pre, act = conv_silu_fwd(P, w8, ssm_conv_b[l][None], nL, f"conv_fwd{l}")
        dtb8, al8 = _pad8(ssm_dt_bias[l]), _pad8(ssm_a_log[l])
        yf, yb, hsf, hsb = ssd_fwd(act, P, dtb8, al8, L, Lc, f"ssd_fwd{l}")
        dskip = jnp.repeat(ssm_d[l], S_P)[None]
        oc = ssm_out_fwd(yf, yb, act, P, dskip, ssm_norm_g[l][None], f"ssm_out_fwd{l}")
        mixin = jnp.concatenate([oa, ob, oc], axis=1)
        mix = matmul(mixin, W_out[l], "nn", F32, f"out_proj{l}")
        gsv_mid = _gsv([mrow(l, 2), mrow(l, 3), mrow(l, 4)])
        x1, h2 = res_norm_mod(xin, mix, gsv_mid, g_ffn[l][None], nL, f"norm_mid{l}")
        gu = matmul(h2, W_fi[l], "nn", BF16, f"ffn_in{l}")
        af = swiglu_fwd(gu, f"swiglu_fwd{l}")
        fo = matmul(af, W_fo[l], "nn", F32, f"ffn_out{l}")
        s.update(P=P, qr=qr, kr=kr, sink8=sink8, kb=kb, vb=vb, bias=bias, w8=w8, pre=pre, act=act, dtb8=dtb8, al8=al8, yf=yf,
                 yb=yb, hsf=hsf, hsb=hsb, dskip=dskip, mixin=mixin, mix=mix, gsv_mid=gsv_mid, x1=x1, h2=h2, gu=gu, af=af, fo=fo)
        if l + 1 < DEPTH:
            s["gsv_end"] = _gsv([mrow(l, 5), mrow(l + 1, 0), mrow(l + 1, 1)])
            xin, h1 = res_norm_mod(x1, fo, s["gsv_end"], g_mix[l + 1][None], nL, f"norm_end{l}")
        else:
            s["gsv_end"] = _gsv([mrow(l, 5), None, None])
        sv.append(s)

    last = sv[-1]
    loss8, dres, dfo, dgsv_end, dg_final = final_loss(last["x1"], last["fo"], last["gsv_end"], g_final[None], loss_target[0].astype(F32), nL, "final_loss")
    loss = lax.psum(loss8[0, 0], ("x", "y", "c"))

    dmod = [[None] * 6 for _ in range(DEPTH)]
    gW = {n: [None] * DEPTH for n, _, _ in _BIG}
    small = [dict() for _ in range(DEPTH)]
    grad_x = None
    for l in reversed(range(DEPTH)):
        s = sv[l]
        dmod[l][5] = dgsv_end[:, 0]
        if l + 1 < DEPTH:
            dmod[l + 1][0], dmod[l + 1][1] = dgsv_end[:, 1], dgsv_end[:, 2]
        daf = matmul(dfo, W_fo[l], "nt", BF16, f"ffn_out_dx{l}")
        gW["w_ffn_out"][l] = matmul(s["af"], dfo, "tn", BF16, f"ffn_out_dw{l}", tm=1408)
        dgu = swiglu_bwd(s["gu"], daf, f"swiglu_bwd{l}")
        dh2 = matmul(dgu, W_fi[l], "nt", F32, f"ffn_in_dx{l}", tk=2816)
        gW["w_ffn_in"][l] = matmul(s["h2"], dgu, "tn", BF16, f"ffn_in_dw{l}", tm=1024)
        dres, dmix, dgsv_mid, dg_ffn = res_norm_mod_bwd(s["x1"], s["mix"], s["gsv_mid"], g_ffn[l][None], dh2, dres, nL, f"norm_mid_bwd{l}")
        dmod[l][2], dmod[l][3], dmod[l][4] = dgsv_mid[:, 0], dgsv_mid[:, 1], dgsv_mid[:, 2]
        dmixin = matmul(dmix, W_out[l], "nt", F32, f"out_proj_dx{l}")
        gW["w_out"][l] = matmul(s["mixin"], dmix, "tn", BF16, f"out_proj_dw{l}", tm=1024)
        P = s["P"]
        dqr, dkr, dva, dsink = win_attn_bwd(s["qr"], s["kr"], P, s["sink8"], dmixin, L, Lc, f"wa_bwd{l}")
        dqa, dka = rope_apply(dqr, 0, dkr[WA_BLK:WA_BLK + T], 0, cos, sin, rotm, True, f"rope_bwd{l}")
        dqb, dkb, dvb, dbias = na_bwd(P, s["kb"], s["vb"], s["bias"], dmixin, L, Lc, f"na_bwd{l}")
        dy, dxs1, dz, dvec = ssm_out_bwd(s["yf"], s["yb"], s["act"], P, s["dskip"], ssm_norm_g[l][None], dmixin, f"ssm_out_bwd{l}")
        dxf, dbf, dcf, ddf, dxb, dbb, dcb, ddb, ddtb, dal = ssd_bwd(s["act"], P, s["dtb8"], s["al8"], s["hsf"], s["hsb"], dy, L, Lc, f"ssd_bwd{l}")
        dpre = dsilu(s["pre"], [dxf, dxb, dxs1], [dbf, dbb], [dcf, dcb], f"dsilu{l}")
        dxbc, dw8, db8 = conv_bwd(dpre, P, s["w8"], nL, f"conv_bwd{l}")
        dP = jnp.concatenate([dqa, dqb, dz, dka, dva[WA_BLK:WA_BLK + T].astype(BF16), dkb.astype(BF16), dvb.astype(BF16), dxbc,
                              ddf.astype(BF16), ddb.astype(BF16), jnp.zeros((T, IN_PAD - IN_COLS), BF16)], axis=1)
        dh1 = matmul(dP, W_in[l], "nt", F32, f"in_proj_dx{l}", tk=IN_PAD)
        gW["w_in"][l] = matmul(s["h1"], dP, "tn", BF16, f"in_proj_dw{l}", tm=512, tn=IN_PAD)[:, :IN_COLS]
        small[l] = dict(g_ffn=dg_ffn[0], wa_sink=dsink[:WA_HEADS, 0], na_rpb=na_rpb_grad(dbias, l), conv_w=dw8[:S_CONV], conv_b=db8[0],
                        dt_bias=ddtb[:2, :8], a_log=dal[:2, :8], ssm_d=dvec[0].reshape(S_HEADS, S_P).sum(axis=1), norm_g=dvec[1])
        if l > 0:
            p = sv[l - 1]
            dres, dfo, dgsv_end, dg_mix = res_norm_mod_bwd(s["xin"], p["fo"], p["gsv_end"], g_mix[l][None], dh1, dres, nL, f"norm_end_bwd{l - 1}")
        else:
            grad_x, _, dgsv_first, dg_mix = res_norm_mod_bwd(s["xin"], None, gsv_first, g_mix[0][None], dh1, dres, nL, "norm_first_bwd")
            dmod[0][0], dmod[0][1] = dgsv_first[:, 1], dgsv_first[:, 2]
        small[l]["g_mix"] = dg_mix[0]
    for l in range(DEPTH):
        for j in range(6):
            if dmod[l][j] is None:
                dmod[l][j] = jnp.zeros((2, D), F32)
    dmod = jnp.stack([jnp.stack(r, axis=1) for r in dmod])

    f3 = _Flat()
    f3.add("dmod_l", dmod[:, 0].reshape(DEPTH, 6 * D))
    f3.add("dmod_c", dmod[:, 1].reshape(DEPTH, 6 * D))
    f3.add("g_final", dg_final[0])
    for n in ("g_mix", "g_ffn", "wa_sink", "na_rpb", "conv_w", "conv_b", "dt_bias", "a_log", "ssm_d", "norm_g"):
        f3.add(n, jnp.stack([small[l][n] for l in range(DEPTH)]))
    g3, s3 = allgather8(f3.rows(), "reduce_small")
    dmod_all = f3.split_lead(g3)["dmod_l"]
    s3 = f3.split(s3)
    dmodc_tot = s3["dmod_c"]
    col0 = chip * MODW
    G16, G16c = [], []
    for l in range(DEPTH):
        rows = jnp.concatenate([dmod_all[:, l], dmodc_tot[l][None], jnp.zeros((7, 6 * D), F32)], axis=0)
        G16.append(lax.dynamic_slice_in_dim(rows, col0, MODW, axis=1))
        rc = jnp.concatenate([dmodc_tot[l][None], jnp.zeros((15, 6 * D), F32)], axis=0)
        G16c.append(lax.dynamic_slice_in_dim(rc, col0, MODW, axis=1))
    grad_w_mod = jnp.stack([matmul(A16, G16[l], "tn", F32, f"mod_dw{l}") for l in range(DEPTH)])
    dscc_part = sum(matmul(G16c[l], w_mod[l], "nt", F32, f"mod_dx{l}")[0] for l in range(DEPTH))
    _, s4 = allgather8(_pad_rows(dscc_part * (mc == 1).astype(F32)), "reduce_cctx")
    dscc = s4.reshape(-1)[:D]
    cc = c_ctx.astype(F32)
    sg = 1.0 / (1.0 + jnp.exp(-cc))
    grad_c_ctx = dscc * (sg * (1.0 + cc * (1.0 - sg)))

    G = pack_for_chips(gW)
    R = G.shape[1]
    H = R // 2
    G2 = G.reshape(N_CHIPS, 2, H, 128)
    mine = lax.dynamic_index_in_dim(G2, mc, axis=1, keepdims=False)
    theirs = lax.dynamic_index_in_dim(G2, 1 - mc, axis=1, keepdims=False)
    got = sibling_swap(theirs, "reduce_d2d")
    chip_sum = add_pair(mine.reshape(N_CHIPS * H, 128), got.reshape(N_CHIPS * H, 128), BF16, "reduce_add_pair")
    parts = scatter_chips(chip_sum.reshape(N_CHIPS, H, 128), "reduce_ici")
    half = sum_slots(parts, "reduce_add_chips")
    other = sibling_swap(half, "reduce_share")
    lo = jnp.where(mc == 0, half, other)
    hi = jnp.where(mc == 0, other, half)
    gsh = unpack_shards(jnp.concatenate([lo, hi], axis=0))

    grads = {"c_ctx": grad_c_ctx, "w_mod": grad_w_mod, "b_mod": s3["dmod_l"] + s3["dmod_c"], "g_mix": s3["g_mix"], "w_in": gsh["w_in"],
             "wa_sink": s3["wa_sink"], "na_rpb": s3["na_rpb"],
             "ssm_conv_w": lax.dynamic_slice_in_dim(s3["conv_w"], chip * CW, CW, axis=2), "ssm_conv_b": s3["conv_b"],
             "ssm_dt_bias": s3["dt_bias"], "ssm_a_log": s3["a_log"], "ssm_d": s3["ssm_d"], "ssm_norm_g": s3["norm_g"],
             "w_out": gsh["w_out"], "g_ffn": s3["g_ffn"], "w_ffn_in": gsh["w_ffn_in"], "w_ffn_out": gsh["w_ffn_out"], "g_final": s3["g_final"]}
    wts = {"c_ctx": c_ctx, "w_mod": w_mod, "b_mod": b_mod, "g_mix": g_mix, "w_in": w_in, "wa_sink": wa_sink, "na_rpb": na_rpb,
           "ssm_conv_w": ssm_conv_w, "ssm_conv_b": ssm_conv_b, "ssm_dt_bias": ssm_dt_bias, "ssm_a_log": ssm_a_log, "ssm_d": ssm_d,
           "ssm_norm_g": ssm_norm_g, "w_out": w_out, "g_ffn": g_ffn, "w_ffn_in": w_ffn_in, "w_ffn_out": w_ffn_out, "g_final": g_final}
    ms = {"c_ctx": m_c_ctx, "w_mod": m_w_mod, "b_mod": m_b_mod, "g_mix": m_g_mix, "w_in": m_w_in, "wa_sink": m_wa_sink, "na_rpb": m_na_rpb,
          "ssm_conv_w": m_ssm_conv_w, "ssm_conv_b": m_ssm_conv_b, "ssm_dt_bias": m_ssm_dt_bias, "ssm_a_log": m_ssm_a_log, "ssm_d": m_ssm_d,
          "ssm_norm_g": m_ssm_norm_g, "w_out": m_w_out, "g_ffn": m_g_ffn, "w_ffn_in": m_w_ffn_in, "w_ffn_out": m_w_ffn_out, "g_final": m_g_final}
    vs = {"c_ctx": v_c_ctx, "w_mod": v_w_mod, "b_mod": v_b_mod, "g_mix": v_g_mix, "w_in": v_w_in, "wa_sink": v_wa_sink, "na_rpb": v_na_rpb,
          "ssm_conv_w": v_ssm_conv_w, "ssm_conv_b": v_ssm_conv_b, "ssm_dt_bias": v_ssm_dt_bias, "ssm_a_log": v_ssm_a_log, "ssm_d": v_ssm_d,
          "ssm_norm_g": v_ssm_norm_g, "w_out": v_w_out, "g_ffn": v_g_ffn, "w_ffn_in": v_w_ffn_in, "w_ffn_out": v_w_ffn_out, "g_final": v_g_final}
    names = list(wts)
    grads = {n: grads[n].reshape(wts[n].shape).astype(F32) for n in names}
    big = ("w_mod", "w_in", "w_out", "w_ffn_in", "w_ffn_out")
    delta, new_m, new_v = {}, {}, {}
    for n in big:
        shp = wts[n].shape
        two = lambda a: a.reshape(shp[0] * shp[1], shp[2])
        d_, m_, v_ = adamw(two(wts[n]), two(grads[n]), two(ms[n]), two(vs[n]), f"adamw_{n}")
        delta[n], new_m[n], new_v[n] = d_.reshape(shp), m_.reshape(shp), v_.reshape(shp)
    packs = []
    for src in (wts, grads, ms, vs):
        f = _Flat()
        for n in names:
            if n not in big:
                f.add(n, src[n])
        packs.append(f)
    d_, m_, v_ = adamw(*[f.rows() for f in packs], "adamw_small")
    for dst, rows in ((delta, d_), (new_m, m_), (new_v, v_)):
        dst.update(packs[0].split(rows))

    return (loss, grad_x[:L][None], *[grads[n] for n in names], *[delta[n] for n in names],
            *[new_m[n] for n in names], *[new_v[n] for n in names])
```

```python
import functools

import numpy as np
import jax
import jax.numpy as jnp
from jax import lax
from jax.experimental import pallas as pl
from jax.experimental.pallas import tpu as pltpu

F32 = jnp.float32
BF16 = jnp.bfloat16
_MXU = jnp.bfloat16
_HI = lax.Precision.HIGHEST
MESH = pl.DeviceIdType.MESH

D = 1024
HD = 64
GRID_W = 64
EPS = 1e-6
ROPE_BASE = 10000.0
WA_HEADS, WA_KV = 4, 2
WA_BLK = 128
NA_HEADS, NA_KH, NA_KW = 4, 8, 16
S_HEADS, S_P, S_INNER, S_GROUPS, S_N, S_CONV, S_Q = 8, 64, 512, 2, 128, 7, 128
D_FF = 2816
IN_COLS = 2832
IN_PAD = 2944
C_QA, C_QB, C_Z, C_KA, C_VA, C_KB, C_VB, C_XBC, C_DT = 0, 256, 512, 1024, 1152, 1280, 1536, 1792, 2816
ADAM_LR, ADAM_B1, ADAM_B2, ADAM_EPS, ADAM_WD, ADAM_STEP = 0.001, 0.9, 0.999, 1e-08, 0.01, 10

TR = 256
NEG = -1e30
VMEM_CAP = 56 * 1024 * 1024


def _pc(body, **kw):
    return pl.pallas_call(body, **kw)


def _cp(sem=None, vmem=None):
    kw = {}
    if sem is not None:
        kw["dimension_semantics"] = sem
    if vmem is not None:
        kw["vmem_limit_bytes"] = int(min(max(vmem, 16 * 1024 * 1024), VMEM_CAP))
    return pltpu.CompilerParams(**kw)


def _sds(shape, dtype):
    return jax.ShapeDtypeStruct(tuple(shape), dtype)


_DIMS = {"nn": ((1,), (0,)), "nt": ((1,), (1,)), "tn": ((0,), (0,))}


def _dg(a, b, dims):
    return lax.dot_general(a.astype(_MXU), b.astype(_MXU), (dims, ((), ())), preferred_element_type=F32)


@functools.partial(jax.custom_vjp, nondiff_argnums=(2,))
def bdot(a, b, mode):
    return _dg(a, b, _DIMS[mode])


def _bdot_fwd(a, b, mode):
    return bdot(a, b, mode), (a, b)


def _bdot_bwd(mode, res, g):
    a, b = res
    if mode == "nn":
        return bdot(g, b, "nt"), bdot(a, g, "tn")
    if mode == "nt":
        return bdot(g, b, "nn"), bdot(g, a, "tn")
    return bdot(b, g, "nt"), bdot(a, g, "nn")


bdot.defvjp(_bdot_fwd, _bdot_bwd)


def hdot(a, b, mode="nn"):
    return lax.dot_general(a, b, (_DIMS[mode], ((), ())), precision=_HI, preferred_element_type=F32)


def _silu(x):
    return x / (1.0 + jnp.exp(-x))


def _softplus(x):
    return jnp.maximum(x, 0.0) + jnp.log(1.0 + jnp.exp(-jnp.abs(x)))


def _div_tile(n, cap, mult):
    if n <= cap:
        return n
    best = None
    for t in range(mult, cap + 1, mult):
        if n % t == 0:
            best = t
    assert best is not None, (n, cap, mult)
    return best


def matmul(a, b, mode, out_dtype, name, tm=640, tn=1536, tk=1408, hi=False):
    if mode == "tn":
        K, M = a.shape
    else:
        M, K = a.shape
    N = b.shape[0] if mode == "nt" else b.shape[1]
    tm = _div_tile(M, tm, 128 if mode == "tn" else 16)
    tn = _div_tile(N, tn, 128)
    tk = _div_tile(K, tk, 128 if mode != "tn" else 16)
    nk = K // tk
    dims = _DIMS[mode]

    def body(a_ref, b_ref, o_ref, *acc):
        if hi:
            part = lax.dot_general(a_ref[...], b_ref[...], (dims, ((), ())), precision=_HI, preferred_element_type=F32)
        else:
            part = _dg(a_ref[...], b_ref[...], dims)
        if nk == 1:
            o_ref[...] = part.astype(o_ref.dtype)
        else:
            k = pl.program_id(2)

            @pl.when(k == 0)
            def _():
                acc[0][...] = part

            @pl.when(k > 0)
            def _():
                acc[0][...] += part

            @pl.when(k == nk - 1)
            def _():
                o_ref[...] = acc[0][...].astype(o_ref.dtype)

    if mode == "tn":
        a_spec = pl.BlockSpec((tk, tm), lambda i, j, k: (k, i))
    else:
        a_spec = pl.BlockSpec((tm, tk), lambda i, j, k: (i, k))
    if mode == "nt":
        b_spec = pl.BlockSpec((tn, tk), lambda i, j, k: (j, k))
    else:
        b_spec = pl.BlockSpec((tk, tn), lambda i, j, k: (k, j))
    isz = lambda x: jnp.dtype(x.dtype).itemsize
    vmem = 2 * (tm * tk * isz(a) + tk * tn * isz(b) + tm * tn * jnp.dtype(out_dtype).itemsize) + 3 * tm * tn * 4
    return _pc(
        body, name=name, grid=(M // tm, N // tn, nk),
        in_specs=[a_spec, b_spec], out_specs=pl.BlockSpec((tm, tn), lambda i, j, k: (i, j)),
        out_shape=_sds((M, N), out_dtype),
        scratch_shapes=[pltpu.VMEM((tm, tn), F32)] if nk > 1 else [],
        compiler_params=_cp(("parallel", "parallel", "arbitrary"), vmem + (8 << 20)),
    )(a, b)


def _norm_mod(xo, shift, scale, g):
    r = lax.rsqrt(jnp.mean(xo * xo, axis=-1, keepdims=True) + EPS)
    return (xo * r) * g * (1.0 + scale) + shift


def res_norm_mod(x, y, gsv, g, nL, name):
    T = x.shape[0]
    has_y = y is not None

    def body(*refs):
        if has_y:
            x_ref, y_ref, gsv_ref, g_ref, xo_ref, h_ref = refs
            xo = x_ref[...] + gsv_ref[0, 0:1, :] * y_ref[...]
            xo_ref[...] = xo
        else:
            x_ref, gsv_ref, g_ref, h_ref = refs
            xo = x_ref[...]
        h_ref[...] = _norm_mod(xo, gsv_ref[0, 1:2, :], gsv_ref[0, 2:3, :], g_ref[...]).astype(h_ref.dtype)

    row = pl.BlockSpec((TR, D), lambda i: (i, 0))
    in_specs = [row] + ([row] if has_y else []) + [pl.BlockSpec((1, 8, D), lambda i: (i // nL, 0, 0)),
                                                     pl.BlockSpec((1, D), lambda i: (0, 0))]
    out_specs = ([row] if has_y else []) + [row]
    out_shape = ([_sds((T, D), F32)] if has_y else []) + [_sds((T, D), BF16)]
    args = (x, y, gsv, g) if has_y else (x, gsv, g)
    outs = _pc(body, name=name, grid=(T // TR,), in_specs=in_specs, out_specs=out_specs, out_shape=out_shape,
               compiler_params=_cp(("arbitrary",), 24 << 20))(*args)
    return (outs[0], outs[1]) if has_y else (None, outs[0])


def res_norm_mod_bwd(xo, y, gsv, g, dh, dres, nL, name):
    T = xo.shape[0]
    has_y = y is not None

    def body(*refs):
        if has_y:
            xo_ref, y_ref, gsv_ref, g_ref, dh_ref, dres_ref, dx_ref, dy_ref, dgsv_ref, dg_ref = refs
        else:
            xo_ref, gsv_ref, g_ref, dh_ref, dres_ref, dx_ref, dgsv_ref, dg_ref = refs
        i = pl.program_id(0)

        @pl.when((i == 0) | (i == nL))
        def _():
            dgsv_ref[...] = jnp.zeros_like(dgsv_ref)

        @pl.when(i == 0)
        def _():
            dg_ref[...] = jnp.zeros_like(dg_ref)

        _, vjp = jax.vjp(_norm_mod, xo_ref[...], gsv_ref[0, 1:2, :], gsv_ref[0, 2:3, :], g_ref[...])
        dxn, dshift, dscale, dg = vjp(dh_ref[...].astype(F32))
        dxo = dres_ref[...] + dxn
        dx_ref[...] = dxo
        if has_y:
            dy_ref[...] = (gsv_ref[0, 0:1, :] * dxo).astype(dy_ref.dtype)
            dgsv_ref[0, 0:1, :] += jnp.sum(y_ref[...] * dxo, axis=0, keepdims=True)
        dgsv_ref[0, 1:2, :] += dshift
        dgsv_ref[0, 2:3, :] += dscale
        dg_ref[0:1, :] += dg

    row = pl.BlockSpec((TR, D), lambda i: (i, 0))
    gspec = pl.BlockSpec((1, 8, D), lambda i: (i // nL, 0, 0))
    in_specs = [row] + ([row] if has_y else []) + [gspec, pl.BlockSpec((1, D), lambda i: (0, 0)), row, row]
    out_specs = [row] + ([row] if has_y else []) + [gspec, pl.BlockSpec((8, D), lambda i: (0, 0))]
    out_shape = [_sds((T, D), F32)] + ([_sds((T, D), BF16)] if has_y else []) + [_sds((2, 8, D), F32), _sds((8, D), F32)]
    args = (xo, y, gsv, g, dh, dres) if has_y else (xo, gsv, g, dh, dres)
    outs = _pc(body, name=name, grid=(T // TR,), in_specs=in_specs, out_specs=out_specs, out_shape=out_shape,
               compiler_params=_cp(("arbitrary",), 32 << 20))(*args)
    if has_y:
        return outs
    return outs[0], None, outs[1], outs[2]


def final_loss(x, y, gsv, g, target, nL, name):
    T = x.shape[0]

    def lossf(xo, gv, t):
        yn = (xo * lax.rsqrt(jnp.mean(xo * xo, axis=-1, keepdims=True) + EPS)) * gv
        e = yn - t
        return 0.5 * jnp.sum(jnp.sum(e * e, axis=-1, keepdims=True) * (1.0 / D), axis=0, keepdims=True)

    def body(x_ref, y_ref, gsv_ref, g_ref, t_ref, loss_ref, dx_ref, dy_ref, dgsv_ref, dg_ref):
        i = pl.program_id(0)

        @pl.when(i == 0)
        def _():
            loss_ref[...] = jnp.zeros_like(loss_ref)
            dg_ref[...] = jnp.zeros_like(dg_ref)

        @pl.when((i == 0) | (i == nL))
        def _():
            dgsv_ref[...] = jnp.zeros_like(dgsv_ref)

        @pl.when(i < nL)
        def _():
            gate = gsv_ref[0, 0:1, :]
            yv = y_ref[...]
            xo = x_ref[...] + gate * yv
            lv, vjp = jax.vjp(lossf, xo, g_ref[...], t_ref[...])
            dxo, dg, _ = vjp(jnp.ones((1, 1), F32))
            loss_ref[...] += jnp.broadcast_to(lv, loss_ref.shape)
            dx_ref[...] = dxo
            dy_ref[...] = (gate * dxo).astype(dy_ref.dtype)
            dgsv_ref[0, 0:1, :] += jnp.sum(yv * dxo, axis=0, keepdims=True)
            dg_ref[0:1, :] += dg

        @pl.when(i >= nL)
        def _():
            dx_ref[...] = jnp.zeros_like(dx_ref)
            dy_ref[...] = jnp.zeros_like(dy_ref)

    row = pl.BlockSpec((TR, D), lambda i: (i, 0))
    gspec = pl.BlockSpec((1, 8, D), lambda i: (i // nL, 0, 0))
    return _pc(
        body, name=name, grid=(T // TR,),
        in_specs=[row, row, gspec, pl.BlockSpec((1, D), lambda i: (0, 0)),
                  pl.BlockSpec((TR, D), lambda i: (jnp.minimum(i, nL - 1), 0))],
        out_specs=[pl.BlockSpec((8, 128), lambda i: (0, 0)), row, row, gspec, pl.BlockSpec((8, D), lambda i: (0, 0))],
        out_shape=[_sds((8, 128), F32), _sds((T, D), F32), _sds((T, D), BF16), _sds((2, 8, D), F32), _sds((8, D), F32)],
        compiler_params=_cp(("arbitrary",), 32 << 20),
    )(x, y, gsv, g, target)


FI_BLK = 2 * D_FF // 4


def _fi_chip(j):
    return (j % 2) * 2 + j // 2


def matmul_fi(a, b, mode, out_dtype, name, tm=640, tk=1088):
    T = a.shape[0]
    if mode == "tn":
        tmd = 512
        tk = _div_tile(T, tk, 16)
        nk = T // tk

        def body(a_ref, b_ref, o_ref, acc):
            k = pl.program_id(2)
            part = _dg(a_ref[...], b_ref[...], _DIMS["tn"])

            @pl.when(k == 0)
            def _():
                acc[...] = part

            @pl.when(k > 0)
            def _():
                acc[...] += part

            @pl.when(k == nk - 1)
            def _():
                o_ref[0] = acc[...].astype(o_ref.dtype)

        return _pc(body, name=name, grid=(D // tmd, 4, nk),
                   in_specs=[pl.BlockSpec((tk, tmd), lambda i, j, k: (k, i)), pl.BlockSpec((tk, FI_BLK), lambda i, j, k: (k, j))],
                   out_specs=pl.BlockSpec((1, tmd, FI_BLK), lambda i, j, k: (_fi_chip(j), i, 0)),
                   out_shape=_sds((4, D, FI_BLK), out_dtype), scratch_shapes=[pltpu.VMEM((tmd, FI_BLK), F32)],
                   compiler_params=_cp(("parallel", "parallel", "arbitrary"), 40 << 20))(a, b)
    tm = _div_tile(T, tm, 16)
    if mode == "nn":
        def body(a_ref, b_ref, o_ref):
            o_ref[...] = _dg(a_ref[...], b_ref[0], _DIMS["nn"]).astype(o_ref.dtype)

        return _pc(body, name=name, grid=(T // tm, 4),
                   in_specs=[pl.BlockSpec((tm, D), lambda i, j: (i, 0)), pl.BlockSpec((1, D, FI_BLK), lambda i, j: (_fi_chip(j), 0, 0))],
                   out_specs=pl.BlockSpec((tm, FI_BLK), lambda i, j: (i, j)), out_shape=_sds((T, 4 * FI_BLK), out_dtype),
                   compiler_params=_cp(("parallel", "arbitrary"), 32 << 20))(a, b)

    def body(a_ref, b_ref, o_ref, acc):
        k = pl.program_id(1)
        part = _dg(a_ref[...], b_ref[0], _DIMS["nt"])

        @pl.when(k == 0)
        def _():
            acc[...] = part

        @pl.when(k > 0)
        def _():
            acc[...] += part

        @pl.when(k == 3)
        def _():
            o_ref[...] = acc[...].astype(o_ref.dtype)

    return _pc(body, name=name, grid=(T // tm, 4),
               in_specs=[pl.BlockSpec((tm, FI_BLK), lambda i, k: (i, k)), pl.BlockSpec((1, D, FI_BLK), lambda i, k: (_fi_chip(k), 0, 0))],
               out_specs=pl.BlockSpec((tm, D), lambda i, k: (i, 0)), out_shape=_sds((T, D), out_dtype),
               scratch_shapes=[pltpu.VMEM((tm, D), F32)], compiler_params=_cp(("parallel", "arbitrary"), 32 << 20))(a, b)


def _swiglu(gate, up):
    return _silu(gate) * up


def swiglu_fwd(gu, name):
    T = gu.shape[0]

    def body(x_ref, o_ref):
        o_ref[...] = _swiglu(x_ref[:, :FI_BLK].astype(F32), x_ref[:, FI_BLK:].astype(F32)).astype(o_ref.dtype)

    return _pc(body, name=name, grid=(T // TR, 2), in_specs=[pl.BlockSpec((TR, 2 * FI_BLK), lambda i, j: (i, j))],
               out_specs=pl.BlockSpec((TR, FI_BLK), lambda i, j: (i, j)), out_shape=_sds((T, D_FF), BF16),
               compiler_params=_cp(("parallel", "parallel"), 24 << 20))(gu)


def swiglu_bwd(gu, dact, name):
    T = gu.shape[0]

    def body(x_ref, d_ref, o_ref):
        _, vjp = jax.vjp(_swiglu, x_ref[:, :FI_BLK].astype(F32), x_ref[:, FI_BLK:].astype(F32))
        dg, du = vjp(d_ref[...].astype(F32))
        o_ref[:, :FI_BLK] = dg.astype(o_ref.dtype)
        o_ref[:, FI_BLK:] = du.astype(o_ref.dtype)

    return _pc(body, name=name, grid=(T // TR, 2),
               in_specs=[pl.BlockSpec((TR, 2 * FI_BLK), lambda i, j: (i, j)), pl.BlockSpec((TR, FI_BLK), lambda i, j: (i, j))],
               out_specs=pl.BlockSpec((TR, 2 * FI_BLK), lambda i, j: (i, j)), out_shape=_sds((T, 2 * D_FF), BF16),
               compiler_params=_cp(("parallel", "parallel"), 32 << 20))(gu, dact)


def rope_tables(L, Lc):
    t = np.arange(L)
    rows, cols = t // GRID_W, t % GRID_W
    inv = ROPE_BASE ** (-np.arange(16, dtype=np.float32) / 16)
    lane = np.arange(64)
    pos = np.where((lane // 32)[None, :] == 0, rows[:, None], cols[:, None]).astype(np.float32)
    ang = jnp.asarray(pos) * jnp.asarray(inv[lane % 16])[None, :]
    cos = jnp.concatenate([jnp.cos(ang), jnp.ones((Lc, 64), F32)], axis=0)
    sin = jnp.concatenate([jnp.sin(ang), jnp.zeros((Lc, 64), F32)], axis=0)
    R = np.zeros((128, 128), np.float32)
    for i in range(128):
        if (i % 32) < 16:
            R[i + 16, i] = -1.0
        else:
            R[i - 16, i] = 1.0
    return jnp.tile(cos, (1, 2)), jnp.tile(sin, (1, 2)), jnp.asarray(R)


def rope_apply(q_src, q_col, k_src, k_col, cos, sin, R, transpose, name, kv_src=None):
    T = cos.shape[0]
    with_kv = kv_src is not None

    def rot(x, c, s, Rm):
        if transpose:
            return x * c + hdot(x * s, Rm, "nt")
        return x * c + hdot(x, Rm) * s

    def body(q_ref, k_ref, c_ref, s_ref, R_ref, *rest):
        qo_ref, ko_ref = rest[-4:-2] if with_kv else rest
        c, s, Rm = c_ref[...], s_ref[...], R_ref[...]
        for j in range(2):
            qo_ref[:, j * 128:(j + 1) * 128] = rot(q_ref[:, j * 128:(j + 1) * 128].astype(F32), c, s, Rm).astype(qo_ref.dtype)
        ko_ref[...] = rot(k_ref[...].astype(F32), c, s, Rm).astype(ko_ref.dtype)
        if with_kv:
            rest[-2][...] = rest[0][...].astype(BF16)
            rest[-1][...] = rest[1][...].astype(BF16)

    tab = pl.BlockSpec((TR, 128), lambda i: (i, 0))
    wide = pl.BlockSpec((TR, 256), lambda i: (i, 0))
    kv_in = [pl.BlockSpec((TR, 256), lambda i: (i, C_KB // 256)), pl.BlockSpec((TR, 256), lambda i: (i, C_VB // 256))] if with_kv else []
    return _pc(body, name=name, grid=(T // TR,),
               in_specs=[pl.BlockSpec((TR, 256), lambda i: (i, q_col)), pl.BlockSpec((TR, 128), lambda i: (i, k_col)),
                         tab, tab, pl.BlockSpec((128, 128), lambda i: (0, 0))] + kv_in,
               out_specs=[wide, tab] + ([wide, wide] if with_kv else []),
               out_shape=[_sds((T, 256), BF16), _sds((T, 128), BF16)] + ([_sds((T, 256), BF16)] * 2 if with_kv else []),
               compiler_params=_cp(("parallel",), 16 << 20))(q_src, k_src, cos, sin, R, *([kv_src, kv_src] if with_kv else []))


_SCALE = HD ** -0.5


def _attn_tile(qh, ks, vs, extra):
    ss = []
    for k, add in ks:
        s = bdot(qh, k, "nt") * _SCALE
        ss.append(s if add is None else s + add)
    m = ss[0].max(axis=-1, keepdims=True)
    for s in ss[1:]:
        m = jnp.maximum(m, s.max(axis=-1, keepdims=True))
    if extra is not None:
        m = jnp.maximum(m, extra)
    ps = [jnp.exp(s - m) for s in ss]
    den = ps[0].sum(axis=-1, keepdims=True)
    for p in ps[1:]:
        den = den + p.sum(axis=-1, keepdims=True)
    if extra is not None:
        den = den + jnp.exp(extra - m)
    num = bdot(ps[0], vs[0], "nn")
    for p, v in zip(ps[1:], vs[1:]):
        num = num + bdot(p, v, "nn")
    return num / den


def _wa_mask(n, L):
    qpos = n * WA_BLK + lax.broadcasted_iota(jnp.int32, (WA_BLK, 3 * WA_BLK), 0)
    kpos = (n - 1) * WA_BLK + lax.broadcasted_iota(jnp.int32, (WA_BLK, 3 * WA_BLK), 1)
    ok = (jnp.abs(qpos - kpos) <= WA_BLK) & (kpos >= 0) & (kpos < L)
    return jnp.where(ok, 0.0, NEG).astype(F32)


def _wa_specs(L, Lc):
    nb = L // WA_BLK
    cb = L // Lc
    lat = lambda n: jnp.minimum(n, nb - 1)
    prv = lambda n: jnp.clip(n - 1, 0, nb - 1)
    nxt = lambda n: jnp.minimum(n + 1, nb - 1)
    kspecs = [pl.BlockSpec((WA_BLK, 128), lambda n: (prv(n), 0)), pl.BlockSpec((WA_BLK, 128), lambda n: (lat(n), 0)),
              pl.BlockSpec((WA_BLK, 128), lambda n: (nxt(n), 0)), pl.BlockSpec((Lc, 128), lambda n: (cb, 0))]
    vcol = C_VA // 128
    vspecs = [pl.BlockSpec((WA_BLK, 128), lambda n: (prv(n), vcol)), pl.BlockSpec((WA_BLK, 128), lambda n: (lat(n), vcol)),
              pl.BlockSpec((WA_BLK, 128), lambda n: (nxt(n), vcol)), pl.BlockSpec((Lc, 128), lambda n: (cb, vcol))]
    return nb, kspecs, vspecs


def win_attn_fwd(qr, kr, P, sink, L, Lc, name):
    T = L + Lc
    nb, kspecs, vspecs = _wa_specs(L, Lc)

    def body(q_ref, kp, kc, kn, kx, vp, vc, vn, vx, s_ref, o_ref):
        n = pl.program_id(0)

        @pl.when(n < nb)
        def _():
            mask = _wa_mask(n, L)
            for g in range(WA_KV):
                sl = slice(g * HD, (g + 1) * HD)
                k3 = jnp.concatenate([kp[:, sl], kc[:, sl], kn[:, sl]], axis=0)
                v3 = jnp.concatenate([vp[:, sl], vc[:, sl], vn[:, sl]], axis=0)
                for r in range(2):
                    h = 2 * g + r
                    o = _attn_tile(q_ref[:, h * HD:(h + 1) * HD], [(k3, mask), (kx[:, sl], None)], [v3, vx[:, sl]],
                                   s_ref[h:h + 1, 0:1])
                    o_ref[:, h * HD:(h + 1) * HD] = o.astype(o_ref.dtype)

        @pl.when(n >= nb)
        def _():
            for h in range(WA_HEADS):
                sl = slice((h // 2) * HD, (h // 2 + 1) * HD)
                o = _attn_tile(q_ref[:, h * HD:(h + 1) * HD], [(kx[:, sl], None)], [vx[:, sl]], s_ref[h:h + 1, 0:1])
                o_ref[:, h * HD:(h + 1) * HD] = o.astype(o_ref.dtype)

    qspec = pl.BlockSpec((WA_BLK, 256), lambda n: (n, 0))
    return _pc(body, name=name, grid=(T // WA_BLK,),
               in_specs=[qspec] + kspecs + vspecs + [pl.BlockSpec((8, 128), lambda n: (0, 0))],
               out_specs=qspec, out_shape=_sds((T, 256), BF16),
               compiler_params=_cp(("arbitrary",), 32 << 20))(qr, kr, kr, kr, kr, P, P, P, P, sink)


def win_attn_bwd(qr, kr, P, sink, do_src, L, Lc, name):
    T = L + Lc
    nb, kspecs, vspecs = _wa_specs(L, Lc)
    cx = WA_BLK + L

    def body(q_ref, kp, kc, kn, kx, vp, vc, vn, vx, s_ref, do_ref, dq_ref, dk_ref, dv_ref, ds_ref):
        n = pl.program_id(0)

        @pl.when(n == 0)
        def _():
            dk_ref[...] = jnp.zeros_like(dk_ref)
            dv_ref[...] = jnp.zeros_like(dv_ref)
            ds_ref[...] = jnp.zeros_like(ds_ref)

        @pl.when(n < nb)
        def _():
            mask = _wa_mask(n, L)
            rows = pl.ds(pl.multiple_of(n * WA_BLK, WA_BLK), 3 * WA_BLK)
            for g in range(WA_KV):
                sl = slice(g * HD, (g + 1) * HD)
                k3 = jnp.concatenate([kp[:, sl], kc[:, sl], kn[:, sl]], axis=0)
                v3 = jnp.concatenate([vp[:, sl], vc[:, sl], vn[:, sl]], axis=0)
                kxg, vxg = kx[:, sl], vx[:, sl]
                acc = None
                for r in range(2):
                    h = 2 * g + r
                    hs = slice(h * HD, (h + 1) * HD)
                    f = lambda q, k3_, v3_, kx_, vx_, s_: _attn_tile(q, [(k3_, mask), (kx_, None)], [v3_, vx_], s_)
                    _, vjp = jax.vjp(f, q_ref[:, hs].astype(F32), k3.astype(F32), v3.astype(F32), kxg.astype(F32),
                                     vxg.astype(F32), s_ref[h:h + 1, 0:1])
                    dq, dk3, dv3, dkx, dvx, dsk = vjp(do_ref[:, hs].astype(F32))
                    dq_ref[:, hs] = dq
                    ds_ref[h:h + 1, :] += jnp.broadcast_to(dsk, (1, 128))
                    acc = (dk3, dv3, dkx, dvx) if acc is None else tuple(a + b for a, b in zip(acc, (dk3, dv3, dkx, dvx)))
                dk_ref[rows, sl] += acc[0]
                dv_ref[rows, sl] += acc[1]
                dk_ref[cx:cx + Lc, sl] += acc[2]
                dv_ref[cx:cx + Lc, sl] += acc[3]

        @pl.when(n >= nb)
        def _():
            for h in range(WA_HEADS):
                sl = slice((h // 2) * HD, (h // 2 + 1) * HD)
                hs = slice(h * HD, (h + 1) * HD)
                f = lambda q, kx_, vx_, s_: _attn_tile(q, [(kx_, None)], [vx_], s_)
                _, vjp = jax.vjp(f, q_ref[:, hs].astype(F32), kx[:, sl].astype(F32), vx[:, sl].astype(F32), s_ref[h:h + 1, 0:1])
                dq, dkx, dvx, dsk = vjp(do_ref[:, hs].astype(F32))
                dq_ref[:, hs] = dq
                ds_ref[h:h + 1, :] += jnp.broadcast_to(dsk, (1, 128))
                dk_ref[cx:cx + Lc, sl] += dkx
                dv_ref[cx:cx + Lc, sl] += dvx

    qspec = pl.BlockSpec((WA_BLK, 256), lambda n: (n, 0))
    acc_spec = pl.BlockSpec((T + 2 * WA_BLK, 128), lambda n: (0, 0))
    return _pc(body, name=name, grid=(T // WA_BLK,),
               in_specs=[qspec] + kspecs + vspecs + [pl.BlockSpec((8, 128), lambda n: (0, 0)), qspec],
               out_specs=[qspec, acc_spec, acc_spec, pl.BlockSpec((8, 128), lambda n: (0, 0))],
               out_shape=[_sds((T, 256), F32), _sds((T + 2 * WA_BLK, 128), F32), _sds((T + 2 * WA_BLK, 128), F32), _sds((8, 128), F32)],
               compiler_params=_cp(("arbitrary",), 40 << 20))(qr, kr, kr, kr, kr, P, P, P, P, sink, do_src)


def na_index_tables():
    qc = np.arange(GRID_W)[:, None]
    kc = np.arange(GRID_W)[None, :]
    cstart = np.clip(qc - NA_KW // 2, 0, GRID_W - NA_KW)
    ok = (kc >= cstart) & (kc < cstart + NA_KW)
    dx = np.clip(kc - qc, -(NA_KW - 1), NA_KW - 1) + (NA_KW - 1)
    off = np.arange(NA_KH)[:, None]
    kr = np.arange(NA_KH)[None, :]
    dy = kr - off + (NA_KH - 1)
    return ok, dx, dy


def _na_selectors():
    ok, dx, dy = na_index_tables()
    e1 = np.zeros((GRID_W * GRID_W, 128), np.float32)
    qi, ki = np.nonzero(ok)
    e1[qi * GRID_W + ki, dx[qi, ki]] = 1.0
    e2 = np.zeros((16, NA_KH * NA_KH), np.float32)
    oi, ri = np.meshgrid(np.arange(NA_KH), np.arange(NA_KH), indexing="ij")
    e2[dy[oi, ri].ravel(), (oi * NA_KH + ri).ravel()] = 1.0
    return ok, jnp.asarray(e1), jnp.asarray(np.kron(np.eye(NA_HEADS, dtype=np.float32), e2))


def na_bias_table(rpb, tag):
    ok, e1, e2 = _na_selectors()
    r2 = jnp.pad(rpb.astype(F32), ((0, 0), (0, 1), (0, 128 - (2 * NA_KW - 1)))).reshape(NA_HEADS * 16, 128)
    r1 = matmul(e2, r2, "tn", F32, f"na_bias_sel1_{tag}", hi=True)
    x = matmul(r1, e1, "nt", F32, f"na_bias_sel2_{tag}", hi=True)
    b = x.reshape(NA_HEADS, NA_KH, NA_KH, GRID_W, GRID_W).transpose(0, 1, 3, 2, 4)
    b = b + jnp.asarray(np.where(ok, 0.0, NEG).astype(np.float32))[None, None, :, None, :]
    return b.reshape(NA_HEADS, NA_KH, GRID_W, NA_KH * GRID_W)


def _na_rows(r, GR):
    r0 = jnp.clip(r - NA_KH // 2, 0, GR - NA_KH)
    return r0, jnp.clip(r - r0, 0, NA_KH - 1)


NA_RPS = 2


def na_fwd(P, kb, vb, bias, L, Lc, name):
    T = L + Lc
    GR = L // GRID_W
    W = NA_KH * GRID_W
    QB = GRID_W * NA_RPS
    nlat = GR // NA_RPS

    def body(q_ref, k_ref, v_ref, b_ref, o_ref):
        s = pl.program_id(0)

        @pl.when(s < nlat)
        def _():
            for rr in range(NA_RPS):
                r0, off = _na_rows(s * NA_RPS + rr, GR)
                rows = pl.ds(pl.multiple_of(r0 * GRID_W, GRID_W), W)
                qs = slice(rr * GRID_W, (rr + 1) * GRID_W)
                for h in range(NA_HEADS):
                    hs = slice(h * HD, (h + 1) * HD)
                    o = _attn_tile(q_ref[qs, hs], [(k_ref[rows, hs], b_ref[h, off]), (k_ref[L:T, hs], None)],
                                   [v_ref[rows, hs], v_ref[L:T, hs]], None)
                    o_ref[qs, hs] = o.astype(o_ref.dtype)

        @pl.when(s >= nlat)
        def _():
            for h in range(NA_HEADS):
                hs = slice(h * HD, (h + 1) * HD)
                o = _attn_tile(q_ref[:, hs], [(k_ref[L:T, hs], None)], [v_ref[L:T, hs]], None)
                o_ref[:, hs] = o.astype(o_ref.dtype)

    one = pl.Buffered(1)
    return _pc(body, name=name, grid=(T // QB,),
               in_specs=[pl.BlockSpec((QB, 256), lambda r: (r, C_QB // 256)),
                         pl.BlockSpec((T, 256), lambda r: (0, 0), pipeline_mode=one),
                         pl.BlockSpec((T, 256), lambda r: (0, 0), pipeline_mode=one),
                         pl.BlockSpec((NA_HEADS, NA_KH, GRID_W, W), lambda r: (0, 0, 0, 0), pipeline_mode=one)],
               out_specs=pl.BlockSpec((QB, 256), lambda r: (r, 0)), out_shape=_sds((T, 256), BF16),
               compiler_params=_cp(("arbitrary",), 32 << 20))(P, kb, vb, bias)


def na_bwd(P, kb, vb, bias, do_src, L, Lc, name):
    T = L + Lc
    GR = L // GRID_W
    W = NA_KH * GRID_W
    QB = GRID_W * NA_RPS
    nlat = GR // NA_RPS

    def body(q_ref, k_ref, v_ref, b_ref, do_ref, dq_ref, dk_ref, dv_ref, db_ref):
        s = pl.program_id(0)

        @pl.when(s == 0)
        def _():
            dk_ref[...] = jnp.zeros_like(dk_ref)
            dv_ref[...] = jnp.zeros_like(dv_ref)
            db_ref[...] = jnp.zeros_like(db_ref)

        @pl.when(s < nlat)
        def _():
            for rr in range(NA_RPS):
                r0, off = _na_rows(s * NA_RPS + rr, GR)
                rows = pl.ds(pl.multiple_of(r0 * GRID_W, GRID_W), W)
                qs = slice(rr * GRID_W, (rr + 1) * GRID_W)
                for h in range(NA_HEADS):
                    hs = slice(h * HD, (h + 1) * HD)
                    f = lambda q, kw, vw, kx, vx, b: _attn_tile(q, [(kw, b), (kx, None)], [vw, vx], None)
                    _, vjp = jax.vjp(f, q_ref[qs, hs].astype(F32), k_ref[rows, hs].astype(F32), v_ref[rows, hs].astype(F32),
                                     k_ref[L:T, hs].astype(F32), v_ref[L:T, hs].astype(F32), b_ref[h, off])
                    dq, dkw, dvw, dkx, dvx, db = vjp(do_ref[qs, hs].astype(F32))
                    dq_ref[qs, hs] = dq.astype(dq_ref.dtype)
                    dk_ref[rows, hs] += dkw
                    dv_ref[rows, hs] += dvw
                    dk_ref[L:T, hs] += dkx
                    dv_ref[L:T, hs] += dvx
                    db_ref[h, off] += db

        @pl.when(s >= nlat)
        def _():
            for h in range(NA_HEADS):
                hs = slice(h * HD, (h + 1) * HD)
                f = lambda q, kx, vx: _attn_tile(q, [(kx, None)], [vx], None)
                _, vjp = jax.vjp(f, q_ref[:, hs].astype(F32), k_ref[L:T, hs].astype(F32), v_ref[L:T, hs].astype(F32))
                dq, dkx, dvx = vjp(do_ref[:, hs].astype(F32))
                dq_ref[:, hs] = dq.astype(dq_ref.dtype)
                dk_ref[L:T, hs] += dkx
                dv_ref[L:T, hs] += dvx

    one = pl.Buffered(1)
    full = lambda shape: pl.BlockSpec(shape, lambda r: (0,) * len(shape), pipeline_mode=one)
    return _pc(body, name=name, grid=(T // QB,),
               in_specs=[pl.BlockSpec((QB, 256), lambda r: (r, C_QB // 256)), full((T, 256)), full((T, 256)),
                         full((NA_HEADS, NA_KH, GRID_W, W)), pl.BlockSpec((QB, 256), lambda r: (r, 1))],
               out_specs=[pl.BlockSpec((QB, 256), lambda r: (r, 0)), full((T, 256)), full((T, 256)),
                          full((NA_HEADS, NA_KH, GRID_W, W))],
               out_shape=[_sds((T, 256), BF16), _sds((T, 256), F32), _sds((T, 256), F32), _sds((NA_HEADS, NA_KH, GRID_W, W), F32)],
               compiler_params=_cp(("arbitrary",), 48 << 20))(P, kb, vb, bias, do_src)


def na_rpb_grad(dbias, tag):
    _, e1, e2 = _na_selectors()
    x = dbias.reshape(NA_HEADS, NA_KH, GRID_W, NA_KH, GRID_W).transpose(0, 1, 3, 2, 4).reshape(NA_HEADS * NA_KH * NA_KH, GRID_W * GRID_W)
    r1 = matmul(x, e1, "nn", F32, f"na_rpb_sel1_{tag}", hi=True, tk=1024)
    r2 = matmul(e2, r1, "nn", F32, f"na_rpb_sel2_{tag}", hi=True)
    return r2.reshape(NA_HEADS, 16, 128)[:, :2 * NA_KH - 1, :2 * NA_KW - 1]


_HALO = 8


def _halo_specs(T, col0):
    nh = TR // _HALO
    cur = pl.BlockSpec((TR, 256), lambda i, j: (i, col0 + j))
    prv = pl.BlockSpec((_HALO, 256), lambda i, j: (jnp.maximum(i * nh - 1, 0), col0 + j))
    nxt = pl.BlockSpec((_HALO, 256), lambda i, j: (jnp.minimum((i + 1) * nh, T // _HALO - 1), col0 + j))
    return prv, cur, nxt


def _fill_ext(ext, prv, cur, nxt, i, nL, nT):
    has_prev = jnp.where((i != 0) & (i != nL), 1.0, 0.0)
    has_next = jnp.where((i != nL - 1) & (i != nT - 1), 1.0, 0.0)
    ext[0:_HALO, :] = prv[...].astype(F32) * has_prev
    ext[_HALO:_HALO + TR, :] = cur[...].astype(F32)
    ext[_HALO + TR:, :] = nxt[...].astype(F32) * has_next


def conv_silu_fwd(P, w8, b, nL, name):
    T = P.shape[0]
    nT = T // TR

    def body(prv, cur, nxt, w_ref, b_ref, pre_ref, act_ref, ext):
        i = pl.program_id(0)
        _fill_ext(ext, prv, cur, nxt, i, nL, nT)
        y = jnp.broadcast_to(b_ref[...], (TR, 256))
        for k in range(S_CONV):
            y = y + w_ref[k:k + 1, :] * ext[pl.ds(_HALO - S_CONV // 2 + k, TR), :]
        pre_ref[...] = y
        act_ref[...] = _silu(y)

    prv, cur, nxt = _halo_specs(T, C_XBC // 256)
    out = pl.BlockSpec((TR, 256), lambda i, j: (i, j))
    return _pc(body, name=name, grid=(nT, 4),
               in_specs=[prv, cur, nxt, pl.BlockSpec((8, 256), lambda i, j: (0, j)), pl.BlockSpec((1, 256), lambda i, j: (0, j))],
               out_specs=[out, out], out_shape=[_sds((T, 1024), F32), _sds((T, 1024), F32)],
               scratch_shapes=[pltpu.VMEM((TR + 2 * _HALO, 256), F32)],
               compiler_params=_cp(("parallel", "parallel"), 16 << 20))(P, P, P, w8, b)


def dsilu(pre, dxs_list, db_list, dc_list, name):
    T = pre.shape[0]
    n1, n2, n3 = len(dxs_list), len(db_list), len(dc_list)

    def body(*refs):
        pre_ref = refs[0]
        ins = refs[1:1 + n1 + n2 + n3]
        out = refs[-1]

        def part(rs, lo, hi):
            g = rs[0][...].astype(F32)
            for r in rs[1:]:
                g = g + r[...].astype(F32)
            _, vjp = jax.vjp(_silu, pre_ref[:, lo:hi])
            out[:, lo:hi] = vjp(g)[0]

        part(ins[:n1], 0, 512)
        part(ins[n1:n1 + n2], 512, 768)
        part(ins[n1 + n2:], 768, 1024)

    spec = lambda w: pl.BlockSpec((TR, w), lambda i: (i, 0))
    return _pc(body, name=name, grid=(T // TR,),
               in_specs=[spec(1024)] + [spec(512)] * n1 + [spec(256)] * (n2 + n3),
               out_specs=spec(1024), out_shape=_sds((T, 1024), F32),
               compiler_params=_cp(("parallel",), 32 << 20))(pre, *dxs_list, *db_list, *dc_list)


def conv_bwd(dpre, P, w8, nL, name):
    T = P.shape[0]
    nT = T // TR

    def body(dp, dc, dn, xp, xc, xn, w_ref, dx_ref, dw_ref, db_ref, extd, extx):
        i = pl.program_id(1)
        _fill_ext(extd, dp, dc, dn, i, nL, nT)
        _fill_ext(extx, xp, xc, xn, i, nL, nT)

        @pl.when(i == 0)
        def _():
            dw_ref[...] = jnp.zeros_like(dw_ref)
            db_ref[...] = jnp.zeros_like(db_ref)

        d = dc[...]
        dx = jnp.zeros((TR, 256), F32)
        for k in range(S_CONV):
            dx = dx + w_ref[k:k + 1, :] * extd[pl.ds(_HALO + S_CONV // 2 - k, TR), :]
            dw_ref[k:k + 1, :] += jnp.sum(d * extx[pl.ds(_HALO - S_CONV // 2 + k, TR), :], axis=0, keepdims=True)
        dx_ref[...] = dx.astype(dx_ref.dtype)
        db_ref[0:1, :] += jnp.sum(d, axis=0, keepdims=True)

    def swap(spec):
        f = spec.index_map
        return pl.BlockSpec(spec.block_shape, lambda j, i: f(i, j))

    dprv, dcur, dnxt = [swap(s) for s in _halo_specs(T, 0)]
    xprv, xcur, xnxt = [swap(s) for s in _halo_specs(T, C_XBC // 256)]
    acc = pl.BlockSpec((8, 256), lambda j, i: (0, j))
    return _pc(body, name=name, grid=(4, nT),
               in_specs=[dprv, dcur, dnxt, xprv, xcur, xnxt, acc],
               out_specs=[pl.BlockSpec((TR, 256), lambda j, i: (i, j)), acc, acc],
               out_shape=[_sds((T, 1024), BF16), _sds((8, 1024), F32), _sds((8, 1024), F32)],
               scratch_shapes=[pltpu.VMEM((TR + 2 * _HALO, 256), F32), pltpu.VMEM((TR + 2 * _HALO, 256), F32)],
               compiler_params=_cp(("parallel", "arbitrary"), 16 << 20))(dpre, dpre, dpre, P, P, P, w8)


def _onehot_row(h, n):
    return (lax.broadcasted_iota(jnp.int32, (1, n), 1) == h).astype(F32)


def _onehot_col(h, n):
    return (lax.broadcasted_iota(jnp.int32, (n, 1), 0) == h).astype(F32)


def _ssd_chunk(xs, dtr, dtb, alog, bm, cm, hin, reverse):
    Qn = S_Q
    ii = lax.broadcasted_iota(jnp.int32, (Qn, Qn), 0)
    jj = lax.broadcasted_iota(jnp.int32, (Qn, Qn), 1)
    keep = (ii <= jj) if reverse else (ii >= jj)
    tri = keep.astype(F32)
    triT = ((jj <= ii) if reverse else (jj >= ii)).astype(F32)
    eye = (ii == jj).astype(F32)
    dt = _softplus(dtr + dtb)
    a = dt * (-jnp.exp(alog))
    cs = hdot(tri, a)
    csT = hdot(a, triT, "tn")
    dtT = hdot(dt, eye, "tn")
    last = _onehot_row(0 if reverse else Qn - 1, Qn)
    ys, houts = [], []
    for g in range(S_GROUPS):
        G = bdot(cm[g], bm[g], "nt")
        for r in range(S_HEADS // S_GROUPS):
            h = g * (S_HEADS // S_GROUPS) + r
            eh_r, eh_c = _onehot_row(h, S_HEADS), _onehot_col(h, S_HEADS)
            cs_c = jnp.sum(cs * eh_r, axis=1, keepdims=True)
            dt_c = jnp.sum(dt * eh_r, axis=1, keepdims=True)
            cs_r = jnp.sum(csT * eh_c, axis=0, keepdims=True)
            dt_r = jnp.sum(dtT * eh_c, axis=0, keepdims=True)
            tot = jnp.sum(cs_r * last, axis=1, keepdims=True)
            decay = jnp.exp(jnp.where(keep, cs_c - cs_r, NEG))
            w = G * decay * dt_r
            y = bdot(w, xs[h], "nn") + bdot(cm[g], hin[h], "nt") * jnp.exp(cs_c)
            xsc = xs[h] * (jnp.exp(tot - cs_c) * dt_c)
            hout = hin[h] * jnp.exp(tot) + bdot(xsc, bm[g], "tn")
            ys.append(y)
            houts.append(hout)
    return ys, houts


def _ssd_orders(L, Lc):
    nl, ncx = L // S_Q, Lc // S_Q
    fwd = lambda s: jnp.where(s < ncx, nl + s, s - ncx)
    bwd = lambda s: nl + ncx - 1 - s
    return nl + ncx, fwd, bwd


def _ssd_in_specs(fo, bo, step):
    def at(order, w, col):
        return pl.BlockSpec((S_Q, w), lambda u: (order(step(u)), col))
    specs = []
    for order in (fo, bo):
        specs += [at(order, 512, 0), at(order, 256, 2), at(order, 256, 3), at(order, 128, C_DT // 128)]
    return specs


def ssd_fwd(act, P, dtb, alog, L, Lc, name):
    T = L + Lc
    ns, fo, bo = _ssd_orders(L, Lc)

    def body(xf, bf, cf, df, xb, bb, cb, db, dtb_ref, al_ref, yf, yb, hsf, hsb, Hf, Hb):
        s = pl.program_id(0)

        @pl.when(s == 0)
        def _():
            Hf[...] = jnp.zeros_like(Hf)
            Hb[...] = jnp.zeros_like(Hb)

        for d, (x_r, b_r, c_r, dt_r, y_r, hs_r, H) in enumerate(((xf, bf, cf, df, yf, hsf, Hf), (xb, bb, cb, db, yb, hsb, Hb))):
            hin = [H[h] for h in range(S_HEADS)]
            hs_r[0] = H[...]
            ys, houts = _ssd_chunk(
                [x_r[:, h * S_P:(h + 1) * S_P] for h in range(S_HEADS)], dt_r[:, d * 8:(d + 1) * 8],
                dtb_ref[d:d + 1, 0:8], al_ref[d:d + 1, 0:8],
                [b_r[:, g * S_N:(g + 1) * S_N] for g in range(S_GROUPS)], [c_r[:, g * S_N:(g + 1) * S_N] for g in range(S_GROUPS)],
                hin, reverse=(d == 1))
            for h in range(S_HEADS):
                y_r[:, h * S_P:(h + 1) * S_P] = ys[h]
                H[h] = houts[h]

    ident = lambda u: u
    small = pl.BlockSpec((8, 128), lambda u: (0, 0))
    hspec = pl.BlockSpec((1, S_HEADS, S_P, S_N), lambda u: (u, 0, 0, 0))
    return _pc(body, name=name, grid=(ns,),
               in_specs=_ssd_in_specs(fo, bo, ident) + [small, small],
               out_specs=[pl.BlockSpec((S_Q, 512), lambda u: (fo(u), 0)), pl.BlockSpec((S_Q, 512), lambda u: (bo(u), 0)), hspec, hspec],
               out_shape=[_sds((T, 512), F32), _sds((T, 512), F32), _sds((ns, S_HEADS, S_P, S_N), F32), _sds((ns, S_HEADS, S_P, S_N), F32)],
               scratch_shapes=[pltpu.VMEM((S_HEADS, S_P, S_N), F32), pltpu.VMEM((S_HEADS, S_P, S_N), F32)],
               compiler_params=_cp(("arbitrary",), 32 << 20))(act, act, act, P, act, act, act, P, dtb, alog)


def ssd_bwd(act, P, dtb, alog, hsf, hsb, dy, L, Lc, name):
    T = L + Lc
    ns, fo, bo = _ssd_orders(L, Lc)
    step = lambda u: ns - 1 - u

    def body(xf, bf, cf, df, xb, bb, cb, db, dtb_ref, al_ref, hsf_r, hsb_r, dyf, dyb,
             dxf, dbf, dcf, ddf, dxb, dbb, dcb, ddb, ddtb, dal, dHf, dHb):
        u = pl.program_id(0)

        @pl.when(u == 0)
        def _():
            dHf[...] = jnp.zeros_like(dHf)
            dHb[...] = jnp.zeros_like(dHb)
            ddtb[...] = jnp.zeros_like(ddtb)
            dal[...] = jnp.zeros_like(dal)

        dirs = ((xf, bf, cf, df, hsf_r, dyf, dxf, dbf, dcf, ddf, dHf), (xb, bb, cb, db, hsb_r, dyb, dxb, dbb, dcb, ddb, dHb))
        for d, (x_r, b_r, c_r, dt_r, hs_r, dy_r, dx_o, db_o, dc_o, dd_o, dH) in enumerate(dirs):
            f = functools.partial(_ssd_chunk, reverse=(d == 1))
            _, vjp = jax.vjp(
                f, [x_r[:, h * S_P:(h + 1) * S_P] for h in range(S_HEADS)], dt_r[:, d * 8:(d + 1) * 8],
                dtb_ref[d:d + 1, 0:8], al_ref[d:d + 1, 0:8],
                [b_r[:, g * S_N:(g + 1) * S_N] for g in range(S_GROUPS)], [c_r[:, g * S_N:(g + 1) * S_N] for g in range(S_GROUPS)],
                [hs_r[0, h] for h in range(S_HEADS)])
            gx, gdt, gdtb, gal, gb, gc, gh = vjp(([dy_r[:, h * S_P:(h + 1) * S_P] for h in range(S_HEADS)],
                                                  [dH[h] for h in range(S_HEADS)]))
            for h in range(S_HEADS):
                dx_o[:, h * S_P:(h + 1) * S_P] = gx[h]
                dH[h] = gh[h]
            for g in range(S_GROUPS):
                db_o[:, g * S_N:(g + 1) * S_N] = gb[g]
                dc_o[:, g * S_N:(g + 1) * S_N] = gc[g]
            dd_o[...] = gdt
            ddtb[d:d + 1, 0:8] += gdtb
            dal[d:d + 1, 0:8] += gal

    small = pl.BlockSpec((8, 128), lambda u: (0, 0))
    hspec = pl.BlockSpec((1, S_HEADS, S_P, S_N), lambda u: (step(u), 0, 0, 0))
    at = lambda order, w: pl.BlockSpec((S_Q, w), lambda u: (order(step(u)), 0))
    outs = []
    for order in (fo, bo):
        outs += [at(order, 512), at(order, 256), at(order, 256), at(order, 8)]
    oshape = [_sds((T, 512), F32), _sds((T, 256), F32), _sds((T, 256), F32), _sds((T, 8), F32)]
    return _pc(body, name=name, grid=(ns,),
               in_specs=_ssd_in_specs(fo, bo, step) + [small, small, hspec, hspec, at(fo, 512), at(bo, 512)],
               out_specs=outs + [small, small], out_shape=oshape + oshape + [_sds((8, 128), F32), _sds((8, 128), F32)],
               scratch_shapes=[pltpu.VMEM((S_HEADS, S_P, S_N), F32), pltpu.VMEM((S_HEADS, S_P, S_N), F32)],
               compiler_params=_cp(("arbitrary",), 40 << 20))(act, act, act, P, act, act, act, P, dtb, alog, hsf, hsb, dy, dy)


def _ssm_out(yf, yb, xs, z, dskip, g):
    y = (yf + yb + dskip * xs) * _silu(z)
    return (y * lax.rsqrt(jnp.mean(y * y, axis=-1, keepdims=True) + EPS)) * g


def ssm_out_fwd(yf, yb, act, P, dskip, g, name):
    T = yf.shape[0]

    def body(yf_r, yb_r, xs_r, z_r, d_r, g_r, o_r):
        o_r[...] = _ssm_out(yf_r[...], yb_r[...], xs_r[...], z_r[...], d_r[...], g_r[...]).astype(o_r.dtype)

    row = pl.BlockSpec((TR, 512), lambda i: (i, 0))
    vec = pl.BlockSpec((1, 512), lambda i: (0, 0))
    return _pc(body, name=name, grid=(T // TR,),
               in_specs=[row, row, row, pl.BlockSpec((TR, 512), lambda i: (i, C_Z // 512)), vec, vec],
               out_specs=row, out_shape=_sds((T, 512), BF16),
               compiler_params=_cp(("parallel",), 16 << 20))(yf, yb, act, P, dskip, g)


def ssm_out_bwd(yf, yb, act, P, dskip, g, do_src, name):
    T = yf.shape[0]

    def body(yf_r, yb_r, xs_r, z_r, d_r, g_r, do_r, dy_r, dxs_r, dz_r, dv_r):
        @pl.when(pl.program_id(0) == 0)
        def _():
            dv_r[...] = jnp.zeros_like(dv_r)

        _, vjp = jax.vjp(_ssm_out, yf_r[...], yb_r[...], xs_r[...], z_r[...], d_r[...], g_r[...])
        dyf, _, dxs, dz, dd, dg = vjp(do_r[...].astype(F32))
        dy_r[...] = dyf
        dxs_r[...] = dxs
        dz_r[...] = dz.astype(dz_r.dtype)
        dv_r[0:1, :] += dd
        dv_r[1:2, :] += dg

    row = pl.BlockSpec((TR, 512), lambda i: (i, 0))
    vec = pl.BlockSpec((1, 512), lambda i: (0, 0))
    return _pc(body, name=name, grid=(T // TR,),
               in_specs=[row, row, row, pl.BlockSpec((TR, 512), lambda i: (i, C_Z // 512)), vec, vec,
                         pl.BlockSpec((TR, 512), lambda i: (i, 1))],
               out_specs=[row, row, row, pl.BlockSpec((8, 512), lambda i: (0, 0))],
               out_shape=[_sds((T, 512), F32), _sds((T, 512), F32), _sds((T, 512), BF16), _sds((8, 512), F32)],
               compiler_params=_cp(("arbitrary",), 24 << 20))(yf, yb, act, P, dskip, g, do_src)


def add_halves(xv, got, cvec, name):
    n, r, cdim = xv.shape
    h = r // 2

    def body(c_ref, x_ref, g_ref, o_ref):
        o_ref[...] = (x_ref[...].astype(F32) + g_ref[...].astype(F32)).astype(o_ref.dtype)

    gs = pltpu.PrefetchScalarGridSpec(
        num_scalar_prefetch=1, grid=(n,),
        in_specs=[pl.BlockSpec((1, h, cdim), lambda k, c_ref: (k, c_ref[0], 0)), pl.BlockSpec((1, h, cdim), lambda k, c_ref: (k, 0, 0))],
        out_specs=pl.BlockSpec((1, h, cdim), lambda k, c_ref: (k, 0, 0)))
    return _pc(body, name=name, grid_spec=gs, out_shape=_sds((n, h, cdim), BF16),
               compiler_params=_cp(("arbitrary",), 24 << 20))(cvec, xv, got)


def sum_slots(a, name):
    n, r, cdim = a.shape
    tr = _div_tile(r, 512, 16)

    def body(a_ref, o_ref):
        acc = a_ref[0].astype(F32)
        for k in range(1, n):
            acc = acc + a_ref[k].astype(F32)
        o_ref[...] = acc

    return _pc(body, name=name, grid=(r // tr,), in_specs=[pl.BlockSpec((n, tr, cdim), lambda i: (0, i, 0))],
               out_specs=pl.BlockSpec((tr, cdim), lambda i: (i, 0)), out_shape=_sds((r, cdim), F32),
               compiler_params=_cp(("parallel",), 32 << 20))(a)


def adamw(w, g, m, v, name):
    R, C = w.shape
    tr = _div_tile(R, max(8, (1 << 19) // max(C, 1) // 8 * 8), 8) if R % 8 == 0 else R
    c1 = 1.0 / (1.0 - ADAM_B1 ** ADAM_STEP)
    c2 = 1.0 / (1.0 - ADAM_B2 ** ADAM_STEP)

    def body(w_ref, g_ref, m_ref, v_ref, d_ref, mo_ref, vo_ref):
        gg = g_ref[...]
        mn = ADAM_B1 * m_ref[...] + (1.0 - ADAM_B1) * gg
        vn = ADAM_B2 * v_ref[...] + (1.0 - ADAM_B2) * (gg * gg)
        d_ref[...] = -ADAM_LR * ((mn * c1) / (jnp.sqrt(vn * c2) + ADAM_EPS) + ADAM_WD * w_ref[...])
        mo_ref[...] = mn
        vo_ref[...] = vn

    spec = pl.BlockSpec((tr, C), lambda i: (i, 0))
    return _pc(body, name=name, grid=(R // tr,), in_specs=[spec] * 4, out_specs=[spec] * 3,
               out_shape=[_sds((R, C), F32)] * 3, compiler_params=_cp(("parallel",), 32 << 20))(w, g, m, v)


def _me():
    return lax.axis_index("x"), lax.axis_index("y"), lax.axis_index("c")


def _flip(v, bit):
    return 1 - v if bit else v


def allgather8(xv, name):
    R = xv.shape[0]

    def body(x_ref, out_ref, sum_ref, send_sems, recv_sems):
        mx, my, mc = _me()
        me = 4 * mx + 2 * my + mc
        out_ref[me] = x_ref[...]
        sends, recvs = [], []
        for k in range(1, 8):
            px, py, pc = _flip(mx, k & 4), _flip(my, k & 2), _flip(mc, k & 1)
            peer = 4 * px + 2 * py + pc
            sends.append(pltpu.make_async_remote_copy(src_ref=x_ref, dst_ref=out_ref.at[me], send_sem=send_sems.at[k - 1],
                                                      recv_sem=recv_sems.at[k - 1], device_id=(px, py, pc), device_id_type=MESH))
            recvs.append(pltpu.make_async_remote_copy(src_ref=x_ref, dst_ref=out_ref.at[peer], send_sem=send_sems.at[k - 1],
                                                      recv_sem=recv_sems.at[k - 1], device_id=(px, py, pc), device_id_type=MESH))
        for cp in sends:
            cp.start()
        for cp in recvs:
            cp.wait_recv()
        for cp in sends:
            cp.wait_send()
        acc = out_ref[0]
        for d in range(1, 8):
            acc = acc + out_ref[d]
        sum_ref[...] = acc

    vm = pl.BlockSpec(memory_space=pltpu.VMEM)
    return _pc(body, name=name, in_specs=[vm], out_specs=[vm, vm], out_shape=[_sds((8, R, 128), F32), _sds((R, 128), F32)],
               scratch_shapes=[pltpu.SemaphoreType.DMA((7,)), pltpu.SemaphoreType.DMA((7,))],
               compiler_params=_cp(None, 32 << 20))(xv)


def _other_chips(mx, my):
    return [(1 - mx, my), (mx, 1 - my), (1 - mx, 1 - my)]


def _halves(r, mc, mult):
    h = r // 2
    return pl.ds(pl.multiple_of(mc * h, mult), h), pl.ds(pl.multiple_of((1 - mc) * h, mult), h)


def _rcopy(src, dst, send_sems, recv_sems, k, to):
    return pltpu.make_async_remote_copy(src_ref=src, dst_ref=dst, send_sem=send_sems.at[k], recv_sem=recv_sems.at[k],
                                        device_id=to, device_id_type=MESH)


def gather_weights(shards, name):
    n = len(shards)

    def body(*refs):
        xs, outs = refs[:n], refs[n:2 * n]
        send_sems, recv_sems, local_sems = refs[2 * n:]
        mx, my, mc = _me()
        chip = 2 * mx + my
        sib = (mx, my, 1 - mc)
        chips = _other_chips(mx, my)
        idx = [2 * cx + cy for cx, cy in chips]
        cp = functools.partial(_rcopy, send_sems=send_sems, recv_sems=recv_sems)
        hv = [_halves(x.shape[0], mc, 16) for x in xs]
        local, first, passed = [], [], []
        for a in range(n):
            local.append(pltpu.make_async_copy(xs[a], outs[a].at[chip], local_sems.at[a]))
            local[-1].start()
            for j, (cx, cy) in enumerate(chips):
                first.append(cp(xs[a].at[hv[a][0]], outs[a].at[chip, hv[a][0]], k=6 * a + j, to=(cx, cy, mc)))
                first[-1].start()
        for a in range(n):
            for j in range(3):
                cp(xs[a].at[hv[a][0]], outs[a].at[idx[j], hv[a][0]], k=6 * a + j, to=sib).wait_recv()
                passed.append(cp(outs[a].at[idx[j], hv[a][0]], outs[a].at[idx[j], hv[a][0]], k=6 * a + 3 + j, to=sib))
                passed[-1].start()
        for a in range(n):
            for j in range(3):
                cp(xs[a].at[hv[a][1]], outs[a].at[idx[j], hv[a][1]], k=6 * a + 3 + j, to=sib).wait_recv()
        for c_ in first + passed:
            c_.wait_send()
        for c_ in local:
            c_.wait()

    hbm = pl.BlockSpec(memory_space=pl.ANY)
    return _pc(body, name=name, in_specs=[hbm] * n, out_specs=[hbm] * n, out_shape=[_sds((4,) + x.shape, x.dtype) for x in shards],
               scratch_shapes=[pltpu.SemaphoreType.DMA((6 * n,)), pltpu.SemaphoreType.DMA((6 * n,)), pltpu.SemaphoreType.DMA((n,))])(*shards)


def swap_halves(arrs, name):
    n = len(arrs)

    def body(*refs):
        xs, outs = refs[:n], refs[n:2 * n]
        send_sems, recv_sems = refs[2 * n:]
        mx, my, mc = _me()
        cps = []
        for a in range(n):
            theirs = _halves(xs[a].shape[1], mc, 16)[1]
            cps.append(_rcopy(xs[a].at[pl.ds(0, 4), theirs], outs[a], send_sems, recv_sems, a, (mx, my, 1 - mc)))
            cps[-1].start()
        for c_ in cps:
            c_.wait()

    hbm = pl.BlockSpec(memory_space=pl.ANY)
    return _pc(body, name=name, in_specs=[hbm] * n, out_specs=[hbm] * n,
               out_shape=[_sds((4, x.shape[1] // 2, x.shape[2]), x.dtype) for x in arrs],
               scratch_shapes=[pltpu.SemaphoreType.DMA((n,)), pltpu.SemaphoreType.DMA((n,))])(*arrs)


def scatter_chips(arrs, name):
    n = len(arrs)

    def body(*refs):
        xs, outs = refs[:n], refs[n:2 * n]
        send_sems, recv_sems, local_sems = refs[2 * n:]
        mx, my, mc = _me()
        chip = 2 * mx + my
        chips = _other_chips(mx, my)
        idx = [2 * cx + cy for cx, cy in chips]
        cp = functools.partial(_rcopy, send_sems=send_sems, recv_sems=recv_sems)
        local, sends = [], []
        for a in range(n):
            local.append(pltpu.make_async_copy(xs[a].at[chip], outs[a].at[chip], local_sems.at[a]))
            local[-1].start()
            for j, (cx, cy) in enumerate(chips):
                sends.append(cp(xs[a].at[idx[j]], outs[a].at[chip], k=3 * a + j, to=(cx, cy, mc)))
                sends[-1].start()
        for a in range(n):
            for j, (cx, cy) in enumerate(chips):
                cp(xs[a].at[idx[j]], outs[a].at[idx[j]], k=3 * a + j, to=(cx, cy, mc)).wait_recv()
        for c_ in sends:
            c_.wait_send()
        for c_ in local:
            c_.wait()

    hbm = pl.BlockSpec(memory_space=pl.ANY)
    return _pc(body, name=name, in_specs=[hbm] * n, out_specs=[hbm] * n, out_shape=[_sds(x.shape, x.dtype) for x in arrs],
               scratch_shapes=[pltpu.SemaphoreType.DMA((3 * n,)), pltpu.SemaphoreType.DMA((3 * n,)), pltpu.SemaphoreType.DMA((n,))])(*arrs)


def share_halves(parts, name):
    flat = [p for w in parts for p in w]
    nw, n = len(parts), len(flat)
    depth = n // nw

    def body(*refs):
        xs, outs = refs[:n], refs[n:n + nw]
        send_sems, recv_sems, local_sems = refs[n + nw:]
        mx, my, mc = _me()
        sib = (mx, my, 1 - mc)
        local, sends, recvs = [], [], []
        for a in range(n):
            w, l = a // depth, a % depth
            mine, theirs = _halves(outs[w].shape[1], mc, 8)
            local.append(pltpu.make_async_copy(xs[a], outs[w].at[l, mine], local_sems.at[a]))
            sends.append(_rcopy(xs[a], outs[w].at[l, mine], send_sems, recv_sems, a, sib))
            recvs.append(_rcopy(xs[a], outs[w].at[l, theirs], send_sems, recv_sems, a, sib))
            local[-1].start()
            sends[-1].start()
        for c_ in recvs:
            c_.wait_recv()
        for c_ in sends:
            c_.wait_send()
        for c_ in local:
            c_.wait()

    hbm = pl.BlockSpec(memory_space=pl.ANY)
    return _pc(body, name=name, in_specs=[hbm] * n, out_specs=[hbm] * nw,
               out_shape=[_sds((depth, 2 * w[0].shape[0], w[0].shape[1]), F32) for w in parts],
               scratch_shapes=[pltpu.SemaphoreType.DMA((n,)), pltpu.SemaphoreType.DMA((n,)), pltpu.SemaphoreType.DMA((n,))])(*flat)


_BIG = ("w_in", "w_out", "w_ffn_in", "w_ffn_out")
N_CHIPS = 4
DEPTH = 2


def _pad_rows(v, mult=8):
    n = v.shape[0]
    rows = -(-n // 128)
    rows = -(-rows // mult) * mult
    return jnp.pad(v, (0, rows * 128 - n)).reshape(rows, 128)


class _Flat:
    def __init__(self):
        self.items = []

    def add(self, name, a):
        self.items.append((name, a.shape, a.reshape(-1).astype(F32)))

    def rows(self):
        return _pad_rows(jnp.concatenate([a for _, _, a in self.items]))

    def split(self, rows):
        flat = rows.reshape(-1)
        out, o = {}, 0
        for name, shape, a in self.items:
            out[name] = flat[o:o + a.shape[0]].reshape(shape)
            o += a.shape[0]
        return out

    def split_lead(self, rows3):
        n = rows3.shape[0]
        flat = rows3.reshape(n, -1)
        out, o = {}, 0
        for name, shape, a in self.items:
            out[name] = flat[:, o:o + a.shape[0]].reshape((n,) + tuple(shape))
            o += a.shape[0]
        return out


def _gsv(rows):
    z = jnp.zeros((2, D), F32)
    r = [z if a is None else a for a in rows] + [z] * 5
    return jnp.stack(r, axis=1)


def _pad8(a, rows=8, cols=128):
    return jnp.zeros((rows, cols), F32).at[:a.shape[0], :a.shape[1]].set(a.astype(F32))


def kernel(x, c, ctx, c_ctx, w_mod, b_mod, g_mix, w_in, wa_sink, na_rpb, ssm_conv_w, ssm_conv_b, ssm_dt_bias, ssm_a_log, ssm_d, ssm_norm_g, w_out, g_ffn, w_ffn_in, w_ffn_out, g_final, loss_target, m_c_ctx, m_w_mod, m_b_mod, m_g_mix, m_w_in, m_wa_sink, m_na_rpb, m_ssm_conv_w, m_ssm_conv_b, m_ssm_dt_bias, m_ssm_a_log, m_ssm_d, m_ssm_norm_g, m_w_out, m_g_ffn, m_w_ffn_in, m_w_ffn_out, m_g_final, v_c_ctx, v_w_mod, v_b_mod, v_g_mix, v_w_in, v_wa_sink, v_na_rpb, v_ssm_conv_w, v_ssm_conv_b, v_ssm_dt_bias, v_ssm_a_log, v_ssm_d, v_ssm_norm_g, v_w_out, v_g_ffn, v_w_ffn_in, v_w_ffn_out, v_g_final):
    L, Lc = x.shape[1], ctx.shape[1]
    T = L + Lc
    nL = L // TR
    mx, my, mc = lax.axis_index("x"), lax.axis_index("y"), lax.axis_index("c")
    dev = 4 * mx + 2 * my + mc
    chip = 2 * mx + my
    MODW = 6 * D // N_CHIPS
    CW = 1024 // N_CHIPS

    sc = _silu(c.astype(F32))
    scc = _silu(c_ctx.astype(F32))[None]
    f1 = _Flat()
    f1.add("sc", sc)
    f1.add("conv_w", ssm_conv_w)
    g1, _ = allgather8(f1.rows(), "gather_cond")
    g1 = f1.split_lead(g1)
    sc_all = g1["sc"][:, 0]
    conv_w = jnp.concatenate([g1["conv_w"][2 * k] for k in range(N_CHIPS)], axis=-1)
    A16 = jnp.concatenate([sc_all, scc, jnp.zeros((7, D), F32)], axis=0)

    mod_part = jnp.stack([matmul(A16, w_mod[l], "nn", F32, f"mod_fwd{l}") for l in range(DEPTH)])
    f2 = _Flat()
    f2.add("mod", mod_part)
    g2, _ = allgather8(f2.rows(), "gather_mod")
    g2 = f2.split_lead(g2)["mod"]
    mods = jnp.concatenate([g2[2 * k] for k in range(N_CHIPS)], axis=-1) + b_mod[:, None, :]
    mod_l = lax.dynamic_index_in_dim(mods, dev, axis=1, keepdims=False).reshape(DEPTH, 6, D)
    mod_c = mods[:, 8].reshape(DEPTH, 6, D)
    mod = jnp.stack([mod_l, mod_c], axis=1)
    mrow = lambda l, j: mod[l, :, j]

    own = {"w_in": w_in, "w_out": w_out, "w_ffn_in": w_ffn_in, "w_ffn_out": w_ffn_out}
    gath = gather_weights([own[n][l].astype(BF16) for n in _BIG for l in range(DEPTH)], "gather_weights")
    gw = {n: [gath[DEPTH * i + l] for l in range(DEPTH)] for i, n in enumerate(_BIG)}
    W_in = [jnp.pad(jnp.concatenate([g[k] for k in range(N_CHIPS)], axis=1), ((0, 0), (0, IN_PAD - IN_COLS))) for g in gw["w_in"]]
    W_out = [g.reshape(D, D) for g in gw["w_out"]]
    W_fo = [g.reshape(D_FF, D) for g in gw["w_ffn_out"]]
    W_fi = gw["w_ffn_in"]

    cos, sin, rotm = rope_tables(L, Lc)
    x0 = jnp.concatenate([x[0], ctx[0]], axis=0).astype(F32)

    sv = []
    xin = x0
    gsv_first = _gsv([None, mrow(0, 0), mrow(0, 1)])
    _, h1 = res_norm_mod(x0, None, gsv_first, g_mix[0][None], nL, "norm_first")
    for l in range(DEPTH):
        s = {"xin": xin, "h1": h1}
        P = matmul(h1, W_in[l], "nn", F32, f"in_proj{l}", tn=IN_PAD)
        qr, kr, kb, vb = rope_apply(P, C_QA // 256, P, C_KA // 128, cos, sin, rotm, False, f"rope{l}", kv_src=P)
        sink8 = _pad8(jnp.broadcast_to(wa_sink[l][:, None], (WA_HEADS, 128)))
        oa = win_attn_fwd(qr, kr, P, sink8, L, Lc, f"wa_fwd{l}")
        bias = na_bias_table(na_rpb[l], l)
        ob = na_fwd(P, kb, vb, bias, L, Lc, f"na_fwd{l}")
        w8 = jnp.concatenate([conv_w[l], jnp.zeros((1, 1024), F32)], axis=0)
        pre, act = conv_silu_fwd(P, w8, ssm_conv_b[l][None], nL, f"conv_fwd{l}")
        dtb8, al8 = _pad8(ssm_dt_bias[l]), _pad8(ssm_a_log[l])
        yf, yb, hsf, hsb = ssd_fwd(act, P, dtb8, al8, L, Lc, f"ssd_fwd{l}")
        dskip = jnp.repeat(ssm_d[l], S_P)[None]
        oc = ssm_out_fwd(yf, yb, act, P, dskip, ssm_norm_g[l][None], f"ssm_out_fwd{l}")
        mixin = jnp.concatenate([oa, ob, oc], axis=1)
        mix = matmul(mixin, W_out[l], "nn", F32, f"out_proj{l}")
        gsv_mid = _gsv([mrow(l, 2), mrow(l, 3), mrow(l, 4)])
        x1, h2 = res_norm_mod(xin, mix, gsv_mid, g_ffn[l][None], nL, f"norm_mid{l}")
        gu = matmul_fi(h2, W_fi[l], "nn", BF16, f"ffn_in{l}")
        af = swiglu_fwd(gu, f"swiglu_fwd{l}")
        fo = matmul(af, W_fo[l], "nn", F32, f"ffn_out{l}")
        s.update(P=P, qr=qr, kr=kr, sink8=sink8, kb=kb, vb=vb, bias=bias, w8=w8, pre=pre, act=act, dtb8=dtb8, al8=al8, yf=yf,
                 yb=yb, hsf=hsf, hsb=hsb, dskip=dskip, mixin=mixin, mix=mix, gsv_mid=gsv_mid, x1=x1, h2=h2, gu=gu, af=af, fo=fo)
        if l + 1 < DEPTH:
            s["gsv_end"] = _gsv([mrow(l, 5), mrow(l + 1, 0), mrow(l + 1, 1)])
            xin, h1 = res_norm_mod(x1, fo, s["gsv_end"], g_mix[l + 1][None], nL, f"norm_end{l}")
        else:
            s["gsv_end"] = _gsv([mrow(l, 5), None, None])
        sv.append(s)

    last = sv[-1]
    loss8, dres, dfo, dgsv_end, dg_final = final_loss(last["x1"], last["fo"], last["gsv_end"], g_final[None], loss_target[0].astype(F32), nL, "final_loss")
    loss = lax.psum(loss8[0, 0], ("x", "y", "c"))

    dmod = [[None] * 6 for _ in range(DEPTH)]
    gW = {n: [None] * DEPTH for n in _BIG}
    small = [dict() for _ in range(DEPTH)]
    grad_x = None
    for l in reversed(range(DEPTH)):
        s = sv[l]
        dmod[l][5] = dgsv_end[:, 0]
        if l + 1 < DEPTH:
            dmod[l + 1][0], dmod[l + 1][1] = dgsv_end[:, 1], dgsv_end[:, 2]
        daf = matmul(dfo, W_fo[l], "nt", BF16, f"ffn_out_dx{l}")
        gW["w_ffn_out"][l] = matmul(s["af"], dfo, "tn", BF16, f"ffn_out_dw{l}", tm=1408).reshape(N_CHIPS, D_FF // N_CHIPS, D)
        dgu = swiglu_bwd(s["gu"], daf, f"swiglu_bwd{l}")
        dh2 = matmul_fi(dgu, W_fi[l], "nt", F32, f"ffn_in_dx{l}")
        gW["w_ffn_in"][l] = matmul_fi(s["h2"], dgu, "tn", BF16, f"ffn_in_dw{l}")
        dres, dmix, dgsv_mid, dg_ffn = res_norm_mod_bwd(s["x1"], s["mix"], s["gsv_mid"], g_ffn[l][None], dh2, dres, nL, f"norm_mid_bwd{l}")
        dmod[l][2], dmod[l][3], dmod[l][4] = dgsv_mid[:, 0], dgsv_mid[:, 1], dgsv_mid[:, 2]
        dmixin = matmul(dmix, W_out[l], "nt", F32, f"out_proj_dx{l}")
        gW["w_out"][l] = matmul(s["mixin"], dmix, "tn", BF16, f"out_proj_dw{l}", tm=1024).reshape(N_CHIPS, D // N_CHIPS, D)
        P = s["P"]
        dqr, dkr, dva, dsink = win_attn_bwd(s["qr"], s["kr"], P, s["sink8"], dmixin, L, Lc, f"wa_bwd{l}")
        dqa, dka = rope_apply(dqr, 0, dkr[WA_BLK:WA_BLK + T], 0, cos, sin, rotm, True, f"rope_bwd{l}")
        dqb, dkb, dvb, dbias = na_bwd(P, s["kb"], s["vb"], s["bias"], dmixin, L, Lc, f"na_bwd{l}")
        dy, dxs1, dz, dvec = ssm_out_bwd(s["yf"], s["yb"], s["act"], P, s["dskip"], ssm_norm_g[l][None], dmixin, f"ssm_out_bwd{l}")
        dxf, dbf, dcf, ddf, dxb, dbb, dcb, ddb, ddtb, dal = ssd_bwd(s["act"], P, s["dtb8"], s["al8"], s["hsf"], s["hsb"], dy, L, Lc, f"ssd_bwd{l}")
        dpre = dsilu(s["pre"], [dxf, dxb, dxs1], [dbf, dbb], [dcf, dcb], f"dsilu{l}")
        dxbc, dw8, db8 = conv_bwd(dpre, P, s["w8"], nL, f"conv_bwd{l}")
        dP = jnp.concatenate([dqa, dqb, dz, dka, dva[WA_BLK:WA_BLK + T].astype(BF16), dkb.astype(BF16), dvb.astype(BF16), dxbc,
                              ddf.astype(BF16), ddb.astype(BF16), jnp.zeros((T, IN_PAD - IN_COLS), BF16)], axis=1)
        dh1 = matmul(dP, W_in[l], "nt", F32, f"in_proj_dx{l}", tk=IN_PAD)
        dwin = matmul(s["h1"], dP, "tn", BF16, f"in_proj_dw{l}", tm=512, tn=IN_PAD)
        cw = IN_COLS // N_CHIPS
        gW["w_in"][l] = jnp.stack([dwin[:, k * cw:(k + 1) * cw] for k in range(N_CHIPS)])
        small[l] = dict(g_ffn=dg_ffn[0], wa_sink=dsink[:WA_HEADS, 0], na_rpb=na_rpb_grad(dbias, l), conv_w=dw8[:S_CONV], conv_b=db8[0],
                        dt_bias=ddtb[:2, :8], a_log=dal[:2, :8], ssm_d=dvec[0].reshape(S_HEADS, S_P).sum(axis=1), norm_g=dvec[1])
        if l > 0:
            p = sv[l - 1]
            dres, dfo, dgsv_end, dg_mix = res_norm_mod_bwd(s["xin"], p["fo"], p["gsv_end"], g_mix[l][None], dh1, dres, nL, f"norm_end_bwd{l - 1}")
        else:
            grad_x, _, dgsv_first, dg_mix = res_norm_mod_bwd(s["xin"], None, gsv_first, g_mix[0][None], dh1, dres, nL, "norm_first_bwd")
            dmod[0][0], dmod[0][1] = dgsv_first[:, 1], dgsv_first[:, 2]
        small[l]["g_mix"] = dg_mix[0]
    for l in range(DEPTH):
        for j in range(6):
            if dmod[l][j] is None:
                dmod[l][j] = jnp.zeros((2, D), F32)
    dmod = jnp.stack([jnp.stack(r, axis=1) for r in dmod])

    f3 = _Flat()
    f3.add("dmod_l", dmod[:, 0].reshape(DEPTH, 6 * D))
    f3.add("dmod_c", dmod[:, 1].reshape(DEPTH, 6 * D))
    f3.add("g_final", dg_final[0])
    for n in ("g_mix", "g_ffn", "wa_sink", "na_rpb", "conv_w", "conv_b", "dt_bias", "a_log", "ssm_d", "norm_g"):
        f3.add(n, jnp.stack([small[l][n] for l in range(DEPTH)]))
    g3, s3 = allgather8(f3.rows(), "reduce_small")
    dmod_all = f3.split_lead(g3)["dmod_l"]
    s3 = f3.split(s3)
    dmodc_tot = s3["dmod_c"]
    col0 = chip * MODW
    G16, G16c = [], []
    for l in range(DEPTH):
        rows = jnp.concatenate([dmod_all[:, l], dmodc_tot[l][None], jnp.zeros((7, 6 * D), F32)], axis=0)
        G16.append(lax.dynamic_slice_in_dim(rows, col0, MODW, axis=1))
        rc = jnp.concatenate([dmodc_tot[l][None], jnp.zeros((15, 6 * D), F32)], axis=0)
        G16c.append(lax.dynamic_slice_in_dim(rc, col0, MODW, axis=1))
    grad_w_mod = jnp.stack([matmul(A16, G16[l], "tn", F32, f"mod_dw{l}") for l in range(DEPTH)])
    dscc_part = sum(matmul(G16c[l], w_mod[l], "nt", F32, f"mod_dx{l}")[0] for l in range(DEPTH))
    _, s4 = allgather8(_pad_rows(dscc_part * (mc == 1).astype(F32)), "reduce_cctx")
    dscc = s4.reshape(-1)[:D]
    cc = c_ctx.astype(F32)
    sg = 1.0 / (1.0 + jnp.exp(-cc))
    grad_c_ctx = dscc * (sg * (1.0 + cc * (1.0 - sg)))

    garr = [gW[n][l] for n in _BIG for l in range(DEPTH)]
    got = swap_halves(garr, "reduce_d2d")
    cvec = mc.astype(jnp.int32).reshape(1)
    chip_sum = [add_halves(garr[a], got[a], cvec, f"reduce_add_pair{a}") for a in range(len(garr))]
    parts = scatter_chips(chip_sum, "reduce_ici")
    halves = [sum_slots(parts[a], f"reduce_add_chips{a}") for a in range(len(garr))]
    shared = share_halves([[halves[DEPTH * i + l] for l in range(DEPTH)] for i in range(len(_BIG))], "reduce_share")
    gsh = dict(zip(_BIG, shared))

    grads = {"c_ctx": grad_c_ctx, "w_mod": grad_w_mod, "b_mod": s3["dmod_l"] + s3["dmod_c"], "g_mix": s3["g_mix"], "w_in": gsh["w_in"],
             "wa_sink": s3["wa_sink"], "na_rpb": s3["na_rpb"],
             "ssm_conv_w": lax.dynamic_slice_in_dim(s3["conv_w"], chip * CW, CW, axis=2), "ssm_conv_b": s3["conv_b"],
             "ssm_dt_bias": s3["dt_bias"], "ssm_a_log": s3["a_log"], "ssm_d": s3["ssm_d"], "ssm_norm_g": s3["norm_g"],
             "w_out": gsh["w_out"], "g_ffn": s3["g_ffn"], "w_ffn_in": gsh["w_ffn_in"], "w_ffn_out": gsh["w_ffn_out"], "g_final": s3["g_final"]}
    wts = {"c_ctx": c_ctx, "w_mod": w_mod, "b_mod": b_mod, "g_mix": g_mix, "w_in": w_in, "wa_sink": wa_sink, "na_rpb": na_rpb,
           "ssm_conv_w": ssm_conv_w, "ssm_conv_b": ssm_conv_b, "ssm_dt_bias": ssm_dt_bias, "ssm_a_log": ssm_a_log, "ssm_d": ssm_d,
           "ssm_norm_g": ssm_norm_g, "w_out": w_out, "g_ffn": g_ffn, "w_ffn_in": w_ffn_in, "w_ffn_out": w_ffn_out, "g_final": g_final}
    ms = {"c_ctx": m_c_ctx, "w_mod": m_w_mod, "b_mod": m_b_mod, "g_mix": m_g_mix, "w_in": m_w_in, "wa_sink": m_wa_sink, "na_rpb": m_na_rpb,
          "ssm_conv_w": m_ssm_conv_w, "ssm_conv_b": m_ssm_conv_b, "ssm_dt_bias": m_ssm_dt_bias, "ssm_a_log": m_ssm_a_log, "ssm_d": m_ssm_d,
          "ssm_norm_g": m_ssm_norm_g, "w_out": m_w_out, "g_ffn": m_g_ffn, "w_ffn_in": m_w_ffn_in, "w_ffn_out": m_w_ffn_out, "g_final": m_g_final}
    vs = {"c_ctx": v_c_ctx, "w_mod": v_w_mod, "b_mod": v_b_mod, "g_mix": v_g_mix, "w_in": v_w_in, "wa_sink": v_wa_sink, "na_rpb": v_na_rpb,
          "ssm_conv_w": v_ssm_conv_w, "ssm_conv_b": v_ssm_conv_b, "ssm_dt_bias": v_ssm_dt_bias, "ssm_a_log": v_ssm_a_log, "ssm_d": v_ssm_d,
          "ssm_norm_g": v_ssm_norm_g, "w_out": v_w_out, "g_ffn": v_g_ffn, "w_ffn_in": v_w_ffn_in, "w_ffn_out": v_w_ffn_out, "g_final": v_g_final}
    names = list(wts)
    grads = {n: grads[n].reshape(wts[n].shape).astype(F32) for n in names}
    big = ("w_mod", "w_in", "w_out", "w_ffn_in", "w_ffn_out")
    delta, new_m, new_v = {}, {}, {}
    for n in big:
        shp = wts[n].shape
        two = lambda a: a.reshape(shp[0] * shp[1], shp[2])
        d_, m_, v_ = adamw(two(wts[n]), two(grads[n]), two(ms[n]), two(vs[n]), f"adamw_{n}")
        delta[n], new_m[n], new_v[n] = d_.reshape(shp), m_.reshape(shp), v_.reshape(shp)
    packs = []
    for src in (wts, grads, ms, vs):
        f = _Flat()
        for n in names:
            if n not in big:
                f.add(n, src[n])
        packs.append(f)
    d_, m_, v_ = adamw(*[f.rows() for f in packs], "adamw_small")
    for dst, rows in ((delta, d_), (new_m, m_), (new_v, v_)):
        dst.update(packs[0].split(rows))

    return (loss, grad_x[:L][None], *[grads[n] for n in names], *[delta[n] for n in names],
            *[new_m[n] for n in names], *[new_v[n] for n in names])
```

```python
import functools

import numpy as np
import jax
import jax.numpy as jnp
from jax import lax
from jax.experimental import pallas as pl
from jax.experimental.pallas import tpu as pltpu
from jax.experimental.pallas import tpu_sc as plsc

F32 = jnp.float32
BF16 = jnp.bfloat16
_MXU = jnp.bfloat16
_HI = lax.Precision.HIGHEST
MESH = pl.DeviceIdType.MESH

D = 1024
HD = 64
GRID_W = 64
EPS = 1e-6
ROPE_BASE = 10000.0
WA_HEADS, WA_KV = 4, 2
WA_BLK = 128
NA_HEADS, NA_KH, NA_KW = 4, 8, 16
S_HEADS, S_P, S_INNER, S_GROUPS, S_N, S_CONV, S_Q = 8, 64, 512, 2, 128, 7, 128
D_FF = 2816
IN_COLS = 2832
IN_PAD = 2944
C_QA, C_QB, C_Z, C_KA, C_VA, C_KB, C_VB, C_XBC, C_DT = 0, 256, 512, 1024, 1152, 1280, 1536, 1792, 2816
ADAM_LR, ADAM_B1, ADAM_B2, ADAM_EPS, ADAM_WD, ADAM_STEP = 0.001, 0.9, 0.999, 1e-08, 0.01, 10

TR = 256
NEG = -1e30
VMEM_CAP = 56 * 1024 * 1024


PIN_BYTES = 256 * 1024


def _is_big(a):
    return hasattr(a, "shape") and len(a.shape) >= 2 and int(np.prod(a.shape)) * jnp.dtype(a.dtype).itemsize >= PIN_BYTES


def _pc(body, *, out_shape, pin=True, **kw):
    if not pin:
        return pl.pallas_call(body, out_shape=out_shape, **kw)
    one = isinstance(out_shape, jax.ShapeDtypeStruct)
    outs = [pltpu.HBM(s.shape, s.dtype) if _is_big(s) else s for s in ([out_shape] if one else out_shape)]
    call = pl.pallas_call(body, out_shape=outs[0] if one else outs, **kw)
    return lambda *args: call(*[pltpu.with_memory_space_constraint(a, pltpu.HBM) if _is_big(a) else a for a in args])


def _cp(sem=None, vmem=None):
    kw = {}
    if sem is not None:
        kw["dimension_semantics"] = sem
    if vmem is not None:
        kw["vmem_limit_bytes"] = int(min(max(vmem, 16 * 1024 * 1024), VMEM_CAP))
    return pltpu.CompilerParams(**kw)


def _sds(shape, dtype):
    return jax.ShapeDtypeStruct(tuple(shape), dtype)


_DIMS = {"nn": ((1,), (0,)), "nt": ((1,), (1,)), "tn": ((0,), (0,))}


def _dg(a, b, dims):
    return lax.dot_general(a.astype(_MXU), b.astype(_MXU), (dims, ((), ())), preferred_element_type=F32)


@functools.partial(jax.custom_vjp, nondiff_argnums=(2,))
def bdot(a, b, mode):
    return _dg(a, b, _DIMS[mode])


def _bdot_fwd(a, b, mode):
    return bdot(a, b, mode), (a, b)


def _bdot_bwd(mode, res, g):
    a, b = res
    if mode == "nn":
        return bdot(g, b, "nt"), bdot(a, g, "tn")
    if mode == "nt":
        return bdot(g, b, "nn"), bdot(g, a, "tn")
    return bdot(b, g, "nt"), bdot(a, g, "nn")


bdot.defvjp(_bdot_fwd, _bdot_bwd)


def hdot(a, b, mode="nn"):
    return lax.dot_general(a, b, (_DIMS[mode], ((), ())), precision=_HI, preferred_element_type=F32)


def _silu(x):
    return x / (1.0 + jnp.exp(-x))


def _softplus(x):
    return jnp.maximum(x, 0.0) + jnp.log(1.0 + jnp.exp(-jnp.abs(x)))


def _div_tile(n, cap, mult):
    if n <= cap:
        return n
    best = None
    for t in range(mult, cap + 1, mult):
        if n % t == 0:
            best = t
    assert best is not None, (n, cap, mult)
    return best


def matmul(a, b, mode, out_dtype, name, tm=640, tn=1536, tk=1408, hi=False):
    if mode == "tn":
        K, M = a.shape
    else:
        M, K = a.shape
    N = b.shape[0] if mode == "nt" else b.shape[1]
    tm = _div_tile(M, tm, 128 if mode == "tn" else 16)
    tn = _div_tile(N, tn, 128)
    tk = _div_tile(K, tk, 128 if mode != "tn" else 16)
    nk = K // tk
    dims = _DIMS[mode]

    def body(a_ref, b_ref, o_ref, *acc):
        if hi:
            part = lax.dot_general(a_ref[...], b_ref[...], (dims, ((), ())), precision=_HI, preferred_element_type=F32)
        else:
            part = _dg(a_ref[...], b_ref[...], dims)
        if nk == 1:
            o_ref[...] = part.astype(o_ref.dtype)
        else:
            k = pl.program_id(2)

            @pl.when(k == 0)
            def _():
                acc[0][...] = part

            @pl.when(k > 0)
            def _():
                acc[0][...] += part

            @pl.when(k == nk - 1)
            def _():
                o_ref[...] = acc[0][...].astype(o_ref.dtype)

    if mode == "tn":
        a_spec = pl.BlockSpec((tk, tm), lambda i, j, k: (k, i))
    else:
        a_spec = pl.BlockSpec((tm, tk), lambda i, j, k: (i, k))
    if mode == "nt":
        b_spec = pl.BlockSpec((tn, tk), lambda i, j, k: (j, k))
    else:
        b_spec = pl.BlockSpec((tk, tn), lambda i, j, k: (k, j))
    isz = lambda x: jnp.dtype(x.dtype).itemsize
    vmem = 2 * (tm * tk * isz(a) + tk * tn * isz(b) + tm * tn * jnp.dtype(out_dtype).itemsize) + 3 * tm * tn * 4
    return _pc(
        body, name=name, grid=(M // tm, N // tn, nk),
        in_specs=[a_spec, b_spec], out_specs=pl.BlockSpec((tm, tn), lambda i, j, k: (i, j)),
        out_shape=_sds((M, N), out_dtype),
        scratch_shapes=[pltpu.VMEM((tm, tn), F32)] if nk > 1 else [],
        compiler_params=_cp(("parallel", "parallel", "arbitrary"), vmem + (8 << 20)),
    )(a, b)


def _norm_mod(xo, shift, scale, g):
    r = lax.rsqrt(jnp.mean(xo * xo, axis=-1, keepdims=True) + EPS)
    return (xo * r) * g * (1.0 + scale) + shift


def res_norm_mod(x, y, gsv, g, nL, name):
    T = x.shape[0]
    has_y = y is not None

    def body(*refs):
        if has_y:
            x_ref, y_ref, gsv_ref, g_ref, xo_ref, h_ref = refs
            xo = x_ref[...] + gsv_ref[0, 0:1, :] * y_ref[...]
            xo_ref[...] = xo
        else:
            x_ref, gsv_ref, g_ref, h_ref = refs
            xo = x_ref[...]
        h_ref[...] = _norm_mod(xo, gsv_ref[0, 1:2, :], gsv_ref[0, 2:3, :], g_ref[...]).astype(h_ref.dtype)

    row = pl.BlockSpec((TR, D), lambda i: (i, 0))
    in_specs = [row] + ([row] if has_y else []) + [pl.BlockSpec((1, 8, D), lambda i: (i // nL, 0, 0)),
                                                     pl.BlockSpec((1, D), lambda i: (0, 0))]
    out_specs = ([row] if has_y else []) + [row]
    out_shape = ([_sds((T, D), F32)] if has_y else []) + [_sds((T, D), BF16)]
    args = (x, y, gsv, g) if has_y else (x, gsv, g)
    outs = _pc(body, name=name, grid=(T // TR,), in_specs=in_specs, out_specs=out_specs, out_shape=out_shape,
               compiler_params=_cp(("arbitrary",), 24 << 20))(*args)
    return (outs[0], outs[1]) if has_y else (None, outs[0])


def res_norm_mod_bwd(xo, y, gsv, g, dh, dres, nL, name):
    T = xo.shape[0]
    has_y = y is not None

    def body(*refs):
        if has_y:
            xo_ref, y_ref, gsv_ref, g_ref, dh_ref, dres_ref, dx_ref, dy_ref, dgsv_ref, dg_ref = refs
        else:
            xo_ref, gsv_ref, g_ref, dh_ref, dres_ref, dx_ref, dgsv_ref, dg_ref = refs
        i = pl.program_id(0)

        @pl.when((i == 0) | (i == nL))
        def _():
            dgsv_ref[...] = jnp.zeros_like(dgsv_ref)

        @pl.when(i == 0)
        def _():
            dg_ref[...] = jnp.zeros_like(dg_ref)

        _, vjp = jax.vjp(_norm_mod, xo_ref[...], gsv_ref[0, 1:2, :], gsv_ref[0, 2:3, :], g_ref[...])
        dxn, dshift, dscale, dg = vjp(dh_ref[...].astype(F32))
        dxo = dres_ref[...] + dxn
        dx_ref[...] = dxo
        if has_y:
            dy_ref[...] = (gsv_ref[0, 0:1, :] * dxo).astype(dy_ref.dtype)
            dgsv_ref[0, 0:1, :] += jnp.sum(y_ref[...] * dxo, axis=0, keepdims=True)
        dgsv_ref[0, 1:2, :] += dshift
        dgsv_ref[0, 2:3, :] += dscale
        dg_ref[0:1, :] += dg

    row = pl.BlockSpec((TR, D), lambda i: (i, 0))
    gspec = pl.BlockSpec((1, 8, D), lambda i: (i // nL, 0, 0))
    in_specs = [row] + ([row] if has_y else []) + [gspec, pl.BlockSpec((1, D), lambda i: (0, 0)), row, row]
    out_specs = [row] + ([row] if has_y else []) + [gspec, pl.BlockSpec((8, D), lambda i: (0, 0))]
    out_shape = [_sds((T, D), F32)] + ([_sds((T, D), BF16)] if has_y else []) + [_sds((2, 8, D), F32), _sds((8, D), F32)]
    args = (xo, y, gsv, g, dh, dres) if has_y else (xo, gsv, g, dh, dres)
    outs = _pc(body, name=name, grid=(T // TR,), in_specs=in_specs, out_specs=out_specs, out_shape=out_shape,
               compiler_params=_cp(("arbitrary",), 32 << 20))(*args)
    if has_y:
        return outs
    return outs[0], None, outs[1], outs[2]


def final_loss(x, y, gsv, g, target, nL, name):
    T = x.shape[0]

    def lossf(xo, gv, t):
        yn = (xo * lax.rsqrt(jnp.mean(xo * xo, axis=-1, keepdims=True) + EPS)) * gv
        e = yn - t
        return 0.5 * jnp.sum(jnp.sum(e * e, axis=-1, keepdims=True) * (1.0 / D), axis=0, keepdims=True)

    def body(x_ref, y_ref, gsv_ref, g_ref, t_ref, loss_ref, dx_ref, dy_ref, dgsv_ref, dg_ref):
        i = pl.program_id(0)

        @pl.when(i == 0)
        def _():
            loss_ref[...] = jnp.zeros_like(loss_ref)
            dg_ref[...] = jnp.zeros_like(dg_ref)

        @pl.when((i == 0) | (i == nL))
        def _():
            dgsv_ref[...] = jnp.zeros_like(dgsv_ref)

        @pl.when(i < nL)
        def _():
            gate = gsv_ref[0, 0:1, :]
            yv = y_ref[...]
            xo = x_ref[...] + gate * yv
            lv, vjp = jax.vjp(lossf, xo, g_ref[...], t_ref[...])
            dxo, dg, _ = vjp(jnp.ones((1, 1), F32))
            loss_ref[...] += jnp.broadcast_to(lv, loss_ref.shape)
            dx_ref[...] = dxo
            dy_ref[...] = (gate * dxo).astype(dy_ref.dtype)
            dgsv_ref[0, 0:1, :] += jnp.sum(yv * dxo, axis=0, keepdims=True)
            dg_ref[0:1, :] += dg

        @pl.when(i >= nL)
        def _():
            dx_ref[...] = jnp.zeros_like(dx_ref)
            dy_ref[...] = jnp.zeros_like(dy_ref)

    row = pl.BlockSpec((TR, D), lambda i: (i, 0))
    gspec = pl.BlockSpec((1, 8, D), lambda i: (i // nL, 0, 0))
    return _pc(
        body, name=name, grid=(T // TR,),
        in_specs=[row, row, gspec, pl.BlockSpec((1, D), lambda i: (0, 0)),
                  pl.BlockSpec((TR, D), lambda i: (jnp.minimum(i, nL - 1), 0))],
        out_specs=[pl.BlockSpec((8, 128), lambda i: (0, 0)), row, row, gspec, pl.BlockSpec((8, D), lambda i: (0, 0))],
        out_shape=[_sds((8, 128), F32), _sds((T, D), F32), _sds((T, D), BF16), _sds((2, 8, D), F32), _sds((8, D), F32)],
        compiler_params=_cp(("arbitrary",), 32 << 20),
    )(x, y, gsv, g, target)


FI_BLK = 2 * D_FF // 4


def _fi_chip(j):
    return (j % 2) * 2 + j // 2


def matmul_fi(a, b, mode, out_dtype, name, tm=640, tk=1088):
    T = a.shape[0]
    if mode == "tn":
        tmd = 512
        tk = _div_tile(T, tk, 16)
        nk = T // tk

        def body(a_ref, b_ref, o_ref, acc):
            k = pl.program_id(2)
            part = _dg(a_ref[...], b_ref[...], _DIMS["tn"])

            @pl.when(k == 0)
            def _():
                acc[...] = part

            @pl.when(k > 0)
            def _():
                acc[...] += part

            @pl.when(k == nk - 1)
            def _():
                o_ref[0] = acc[...].astype(o_ref.dtype)

        return _pc(body, name=name, grid=(D // tmd, 4, nk),
                   in_specs=[pl.BlockSpec((tk, tmd), lambda i, j, k: (k, i)), pl.BlockSpec((tk, FI_BLK), lambda i, j, k: (k, j))],
                   out_specs=pl.BlockSpec((1, tmd, FI_BLK), lambda i, j, k: (_fi_chip(j), i, 0)),
                   out_shape=_sds((4, D, FI_BLK), out_dtype), scratch_shapes=[pltpu.VMEM((tmd, FI_BLK), F32)],
                   compiler_params=_cp(("parallel", "parallel", "arbitrary"), 40 << 20))(a, b)
    tm = _div_tile(T, tm, 16)
    if mode == "nn":
        def body(a_ref, b_ref, o_ref):
            o_ref[...] = _dg(a_ref[...], b_ref[0], _DIMS["nn"]).astype(o_ref.dtype)

        return _pc(body, name=name, grid=(T // tm, 4),
                   in_specs=[pl.BlockSpec((tm, D), lambda i, j: (i, 0)), pl.BlockSpec((1, D, FI_BLK), lambda i, j: (_fi_chip(j), 0, 0))],
                   out_specs=pl.BlockSpec((tm, FI_BLK), lambda i, j: (i, j)), out_shape=_sds((T, 4 * FI_BLK), out_dtype),
                   compiler_params=_cp(("parallel", "arbitrary"), 32 << 20))(a, b)

    def body(a_ref, b_ref, o_ref, acc):
        k = pl.program_id(1)
        part = _dg(a_ref[...], b_ref[0], _DIMS["nt"])

        @pl.when(k == 0)
        def _():
            acc[...] = part

        @pl.when(k > 0)
        def _():
            acc[...] += part

        @pl.when(k == 3)
        def _():
            o_ref[...] = acc[...].astype(o_ref.dtype)

    return _pc(body, name=name, grid=(T // tm, 4),
               in_specs=[pl.BlockSpec((tm, FI_BLK), lambda i, k: (i, k)), pl.BlockSpec((1, D, FI_BLK), lambda i, k: (_fi_chip(k), 0, 0))],
               out_specs=pl.BlockSpec((tm, D), lambda i, k: (i, 0)), out_shape=_sds((T, D), out_dtype),
               scratch_shapes=[pltpu.VMEM((tm, D), F32)], compiler_params=_cp(("parallel", "arbitrary"), 32 << 20))(a, b)


def _swiglu(gate, up):
    return _silu(gate) * up


def swiglu_fwd(gu, name):
    T = gu.shape[0]

    def body(x_ref, o_ref):
        o_ref[...] = _swiglu(x_ref[:, :FI_BLK].astype(F32), x_ref[:, FI_BLK:].astype(F32)).astype(o_ref.dtype)

    return _pc(body, name=name, grid=(T // TR, 2), in_specs=[pl.BlockSpec((TR, 2 * FI_BLK), lambda i, j: (i, j))],
               out_specs=pl.BlockSpec((TR, FI_BLK), lambda i, j: (i, j)), out_shape=_sds((T, D_FF), BF16),
               compiler_params=_cp(("parallel", "parallel"), 24 << 20))(gu)


def swiglu_bwd(gu, dact, name):
    T = gu.shape[0]

    def body(x_ref, d_ref, o_ref):
        _, vjp = jax.vjp(_swiglu, x_ref[:, :FI_BLK].astype(F32), x_ref[:, FI_BLK:].astype(F32))
        dg, du = vjp(d_ref[...].astype(F32))
        o_ref[:, :FI_BLK] = dg.astype(o_ref.dtype)
        o_ref[:, FI_BLK:] = du.astype(o_ref.dtype)

    return _pc(body, name=name, grid=(T // TR, 2),
               in_specs=[pl.BlockSpec((TR, 2 * FI_BLK), lambda i, j: (i, j)), pl.BlockSpec((TR, FI_BLK), lambda i, j: (i, j))],
               out_specs=pl.BlockSpec((TR, 2 * FI_BLK), lambda i, j: (i, j)), out_shape=_sds((T, 2 * D_FF), BF16),
               compiler_params=_cp(("parallel", "parallel"), 32 << 20))(gu, dact)


def rope_tables(L, Lc):
    t = np.arange(L)
    rows, cols = t // GRID_W, t % GRID_W
    inv = ROPE_BASE ** (-np.arange(16, dtype=np.float32) / 16)
    lane = np.arange(64)
    pos = np.where((lane // 32)[None, :] == 0, rows[:, None], cols[:, None]).astype(np.float32)
    ang = jnp.asarray(pos) * jnp.asarray(inv[lane % 16])[None, :]
    cos = jnp.concatenate([jnp.cos(ang), jnp.ones((Lc, 64), F32)], axis=0)
    sin = jnp.concatenate([jnp.sin(ang), jnp.zeros((Lc, 64), F32)], axis=0)
    R = np.zeros((128, 128), np.float32)
    for i in range(128):
        if (i % 32) < 16:
            R[i + 16, i] = -1.0
        else:
            R[i - 16, i] = 1.0
    return jnp.tile(cos, (1, 2)), jnp.tile(sin, (1, 2)), jnp.asarray(R)


def rope_apply(q_src, q_col, k_src, k_col, cos, sin, R, transpose, name, kv_src=None):
    T = cos.shape[0]
    with_kv = kv_src is not None

    def rot(x, c, s, Rm):
        if transpose:
            return x * c + hdot(x * s, Rm, "nt")
        return x * c + hdot(x, Rm) * s

    def body(q_ref, k_ref, c_ref, s_ref, R_ref, *rest):
        qo_ref, ko_ref = rest[-4:-2] if with_kv else rest
        c, s, Rm = c_ref[...], s_ref[...], R_ref[...]
        for j in range(2):
            qo_ref[:, j * 128:(j + 1) * 128] = rot(q_ref[:, j * 128:(j + 1) * 128].astype(F32), c, s, Rm).astype(qo_ref.dtype)
        ko_ref[...] = rot(k_ref[...].astype(F32), c, s, Rm).astype(ko_ref.dtype)
        if with_kv:
            rest[-2][...] = rest[0][...].astype(BF16)
            rest[-1][...] = rest[1][...].astype(BF16)

    tab = pl.BlockSpec((TR, 128), lambda i: (i, 0))
    wide = pl.BlockSpec((TR, 256), lambda i: (i, 0))
    kv_in = [pl.BlockSpec((TR, 256), lambda i: (i, C_KB // 256)), pl.BlockSpec((TR, 256), lambda i: (i, C_VB // 256))] if with_kv else []
    return _pc(body, name=name, grid=(T // TR,),
               in_specs=[pl.BlockSpec((TR, 256), lambda i: (i, q_col)), pl.BlockSpec((TR, 128), lambda i: (i, k_col)),
                         tab, tab, pl.BlockSpec((128, 128), lambda i: (0, 0))] + kv_in,
               out_specs=[wide, tab] + ([wide, wide] if with_kv else []),
               out_shape=[_sds((T, 256), BF16), _sds((T, 128), BF16)] + ([_sds((T, 256), BF16)] * 2 if with_kv else []),
               compiler_params=_cp(("parallel",), 16 << 20))(q_src, k_src, cos, sin, R, *([kv_src, kv_src] if with_kv else []))


_SCALE = HD ** -0.5


def _attn_tile(qh, ks, vs, extra):
    ss = []
    for k, add in ks:
        s = bdot(qh, k, "nt") * _SCALE
        ss.append(s if add is None else s + add)
    m = ss[0].max(axis=-1, keepdims=True)
    for s in ss[1:]:
        m = jnp.maximum(m, s.max(axis=-1, keepdims=True))
    if extra is not None:
        m = jnp.maximum(m, extra)
    ps = [jnp.exp(s - m) for s in ss]
    den = ps[0].sum(axis=-1, keepdims=True)
    for p in ps[1:]:
        den = den + p.sum(axis=-1, keepdims=True)
    if extra is not None:
        den = den + jnp.exp(extra - m)
    num = bdot(ps[0], vs[0], "nn")
    for p, v in zip(ps[1:], vs[1:]):
        num = num + bdot(p, v, "nn")
    return num / den


def _wa_mask(n, L):
    qpos = n * WA_BLK + lax.broadcasted_iota(jnp.int32, (WA_BLK, 3 * WA_BLK), 0)
    kpos = (n - 1) * WA_BLK + lax.broadcasted_iota(jnp.int32, (WA_BLK, 3 * WA_BLK), 1)
    ok = (jnp.abs(qpos - kpos) <= WA_BLK) & (kpos >= 0) & (kpos < L)
    return jnp.where(ok, 0.0, NEG).astype(F32)


def _wa_specs(L, Lc):
    nb = L // WA_BLK
    cb = L // Lc
    lat = lambda n: jnp.minimum(n, nb - 1)
    prv = lambda n: jnp.clip(n - 1, 0, nb - 1)
    nxt = lambda n: jnp.minimum(n + 1, nb - 1)
    kspecs = [pl.BlockSpec((WA_BLK, 128), lambda n: (prv(n), 0)), pl.BlockSpec((WA_BLK, 128), lambda n: (lat(n), 0)),
              pl.BlockSpec((WA_BLK, 128), lambda n: (nxt(n), 0)), pl.BlockSpec((Lc, 128), lambda n: (cb, 0))]
    vcol = C_VA // 128
    vspecs = [pl.BlockSpec((WA_BLK, 128), lambda n: (prv(n), vcol)), pl.BlockSpec((WA_BLK, 128), lambda n: (lat(n), vcol)),
              pl.BlockSpec((WA_BLK, 128), lambda n: (nxt(n), vcol)), pl.BlockSpec((Lc, 128), lambda n: (cb, vcol))]
    return nb, kspecs, vspecs


def win_attn_fwd(qr, kr, P, sink, L, Lc, name):
    T = L + Lc
    nb, kspecs, vspecs = _wa_specs(L, Lc)

    def body(q_ref, kp, kc, kn, kx, vp, vc, vn, vx, s_ref, o_ref):
        n = pl.program_id(0)

        @pl.when(n < nb)
        def _():
            mask = _wa_mask(n, L)
            for g in range(WA_KV):
                sl = slice(g * HD, (g + 1) * HD)
                k3 = jnp.concatenate([kp[:, sl], kc[:, sl], kn[:, sl]], axis=0)
                v3 = jnp.concatenate([vp[:, sl], vc[:, sl], vn[:, sl]], axis=0)
                for r in range(2):
                    h = 2 * g + r
                    o = _attn_tile(q_ref[:, h * HD:(h + 1) * HD], [(k3, mask), (kx[:, sl], None)], [v3, vx[:, sl]],
                                   s_ref[h:h + 1, 0:1])
                    o_ref[:, h * HD:(h + 1) * HD] = o.astype(o_ref.dtype)

        @pl.when(n >= nb)
        def _():
            for h in range(WA_HEADS):
                sl = slice((h // 2) * HD, (h // 2 + 1) * HD)
                o = _attn_tile(q_ref[:, h * HD:(h + 1) * HD], [(kx[:, sl], None)], [vx[:, sl]], s_ref[h:h + 1, 0:1])
                o_ref[:, h * HD:(h + 1) * HD] = o.astype(o_ref.dtype)

    qspec = pl.BlockSpec((WA_BLK, 256), lambda n: (n, 0))
    return _pc(body, name=name, grid=(T // WA_BLK,),
               in_specs=[qspec] + kspecs + vspecs + [pl.BlockSpec((8, 128), lambda n: (0, 0))],
               out_specs=qspec, out_shape=_sds((T, 256), BF16),
               compiler_params=_cp(("arbitrary",), 32 << 20))(qr, kr, kr, kr, kr, P, P, P, P, sink)


def win_attn_bwd(qr, kr, P, sink, do_src, L, Lc, name):
    T = L + Lc
    nb, kspecs, vspecs = _wa_specs(L, Lc)
    cx = WA_BLK + L

    def body(q_ref, kp, kc, kn, kx, vp, vc, vn, vx, s_ref, do_ref, dq_ref, dk_ref, dv_ref, ds_ref):
        n = pl.program_id(0)

        @pl.when(n == 0)
        def _():
            dk_ref[...] = jnp.zeros_like(dk_ref)
            dv_ref[...] = jnp.zeros_like(dv_ref)
            ds_ref[...] = jnp.zeros_like(ds_ref)

        @pl.when(n < nb)
        def _():
            mask = _wa_mask(n, L)
            rows = pl.ds(pl.multiple_of(n * WA_BLK, WA_BLK), 3 * WA_BLK)
            for g in range(WA_KV):
                sl = slice(g * HD, (g + 1) * HD)
                k3 = jnp.concatenate([kp[:, sl], kc[:, sl], kn[:, sl]], axis=0)
                v3 = jnp.concatenate([vp[:, sl], vc[:, sl], vn[:, sl]], axis=0)
                kxg, vxg = kx[:, sl], vx[:, sl]
                acc = None
                for r in range(2):
                    h = 2 * g + r
                    hs = slice(h * HD, (h + 1) * HD)
                    f = lambda q, k3_, v3_, kx_, vx_, s_: _attn_tile(q, [(k3_, mask), (kx_, None)], [v3_, vx_], s_)
                    _, vjp = jax.vjp(f, q_ref[:, hs].astype(F32), k3.astype(F32), v3.astype(F32), kxg.astype(F32),
                                     vxg.astype(F32), s_ref[h:h + 1, 0:1])
                    dq, dk3, dv3, dkx, dvx, dsk = vjp(do_ref[:, hs].astype(F32))
                    dq_ref[:, hs] = dq
                    ds_ref[h:h + 1, :] += jnp.broadcast_to(dsk, (1, 128))
                    acc = (dk3, dv3, dkx, dvx) if acc is None else tuple(a + b for a, b in zip(acc, (dk3, dv3, dkx, dvx)))
                dk_ref[rows, sl] += acc[0]
                dv_ref[rows, sl] += acc[1]
                dk_ref[cx:cx + Lc, sl] += acc[2]
                dv_ref[cx:cx + Lc, sl] += acc[3]

        @pl.when(n >= nb)
        def _():
            for h in range(WA_HEADS):
                sl = slice((h // 2) * HD, (h // 2 + 1) * HD)
                hs = slice(h * HD, (h + 1) * HD)
                f = lambda q, kx_, vx_, s_: _attn_tile(q, [(kx_, None)], [vx_], s_)
                _, vjp = jax.vjp(f, q_ref[:, hs].astype(F32), kx[:, sl].astype(F32), vx[:, sl].astype(F32), s_ref[h:h + 1, 0:1])
                dq, dkx, dvx, dsk = vjp(do_ref[:, hs].astype(F32))
                dq_ref[:, hs] = dq
                ds_ref[h:h + 1, :] += jnp.broadcast_to(dsk, (1, 128))
                dk_ref[cx:cx + Lc, sl] += dkx
                dv_ref[cx:cx + Lc, sl] += dvx

    qspec = pl.BlockSpec((WA_BLK, 256), lambda n: (n, 0))
    acc_spec = pl.BlockSpec((T + 2 * WA_BLK, 128), lambda n: (0, 0))
    return _pc(body, name=name, grid=(T // WA_BLK,),
               in_specs=[qspec] + kspecs + vspecs + [pl.BlockSpec((8, 128), lambda n: (0, 0)), qspec],
               out_specs=[qspec, acc_spec, acc_spec, pl.BlockSpec((8, 128), lambda n: (0, 0))],
               out_shape=[_sds((T, 256), F32), _sds((T + 2 * WA_BLK, 128), F32), _sds((T + 2 * WA_BLK, 128), F32), _sds((8, 128), F32)],
               compiler_params=_cp(("arbitrary",), 40 << 20))(qr, kr, kr, kr, kr, P, P, P, P, sink, do_src)


def na_index_tables():
    qc = np.arange(GRID_W)[:, None]
    kc = np.arange(GRID_W)[None, :]
    cstart = np.clip(qc - NA_KW // 2, 0, GRID_W - NA_KW)
    ok = (kc >= cstart) & (kc < cstart + NA_KW)
    dx = np.clip(kc - qc, -(NA_KW - 1), NA_KW - 1) + (NA_KW - 1)
    off = np.arange(NA_KH)[:, None]
    kr = np.arange(NA_KH)[None, :]
    dy = kr - off + (NA_KH - 1)
    return ok, dx, dy


def _na_selectors():
    ok, dx, dy = na_index_tables()
    e1 = np.zeros((GRID_W * GRID_W, 128), np.float32)
    qi, ki = np.nonzero(ok)
    e1[qi * GRID_W + ki, dx[qi, ki]] = 1.0
    e2 = np.zeros((16, NA_KH * NA_KH), np.float32)
    oi, ri = np.meshgrid(np.arange(NA_KH), np.arange(NA_KH), indexing="ij")
    e2[dy[oi, ri].ravel(), (oi * NA_KH + ri).ravel()] = 1.0
    return ok, jnp.asarray(e1), jnp.asarray(np.kron(np.eye(NA_HEADS, dtype=np.float32), e2))


def na_bias_table(rpb, tag):
    ok, e1, e2 = _na_selectors()
    r2 = jnp.pad(rpb.astype(F32), ((0, 0), (0, 1), (0, 128 - (2 * NA_KW - 1)))).reshape(NA_HEADS * 16, 128)
    r1 = matmul(e2, r2, "tn", F32, f"na_bias_sel1_{tag}", hi=True)
    x = matmul(r1, e1, "nt", F32, f"na_bias_sel2_{tag}", hi=True)
    b = x.reshape(NA_HEADS, NA_KH, NA_KH, GRID_W, GRID_W).transpose(0, 1, 3, 2, 4)
    b = b + jnp.asarray(np.where(ok, 0.0, NEG).astype(np.float32))[None, None, :, None, :]
    return b.reshape(NA_HEADS, NA_KH, GRID_W, NA_KH * GRID_W)


def _na_rows(r, GR):
    r0 = jnp.clip(r - NA_KH // 2, 0, GR - NA_KH)
    return r0, jnp.clip(r - r0, 0, NA_KH - 1)


NA_RPS = 2


def na_fwd(P, kb, vb, bias, L, Lc, name):
    T = L + Lc
    GR = L // GRID_W
    W = NA_KH * GRID_W
    QB = GRID_W * NA_RPS
    nlat = GR // NA_RPS

    def body(q_ref, k_ref, v_ref, b_ref, o_ref):
        s = pl.program_id(0)

        @pl.when(s < nlat)
        def _():
            for rr in range(NA_RPS):
                r0, off = _na_rows(s * NA_RPS + rr, GR)
                rows = pl.ds(pl.multiple_of(r0 * GRID_W, GRID_W), W)
                qs = slice(rr * GRID_W, (rr + 1) * GRID_W)
                for h in range(NA_HEADS):
                    hs = slice(h * HD, (h + 1) * HD)
                    o = _attn_tile(q_ref[qs, hs], [(k_ref[rows, hs], b_ref[h, off]), (k_ref[L:T, hs], None)],
                                   [v_ref[rows, hs], v_ref[L:T, hs]], None)
                    o_ref[qs, hs] = o.astype(o_ref.dtype)

        @pl.when(s >= nlat)
        def _():
            for h in range(NA_HEADS):
                hs = slice(h * HD, (h + 1) * HD)
                o = _attn_tile(q_ref[:, hs], [(k_ref[L:T, hs], None)], [v_ref[L:T, hs]], None)
                o_ref[:, hs] = o.astype(o_ref.dtype)

    one = pl.Buffered(1)
    return _pc(body, name=name, grid=(T // QB,),
               in_specs=[pl.BlockSpec((QB, 256), lambda r: (r, C_QB // 256)),
                         pl.BlockSpec((T, 256), lambda r: (0, 0), pipeline_mode=one),
                         pl.BlockSpec((T, 256), lambda r: (0, 0), pipeline_mode=one),
                         pl.BlockSpec((NA_HEADS, NA_KH, GRID_W, W), lambda r: (0, 0, 0, 0), pipeline_mode=one)],
               out_specs=pl.BlockSpec((QB, 256), lambda r: (r, 0)), out_shape=_sds((T, 256), BF16),
               compiler_params=_cp(("arbitrary",), 32 << 20))(P, kb, vb, bias)


def na_bwd(P, kb, vb, bias, do_src, L, Lc, name):
    T = L + Lc
    GR = L // GRID_W
    W = NA_KH * GRID_W
    QB = GRID_W * NA_RPS
    nlat = GR // NA_RPS

    def body(q_ref, k_ref, v_ref, b_ref, do_ref, dq_ref, dk_ref, dv_ref, db_ref):
        s = pl.program_id(0)

        @pl.when(s == 0)
        def _():
            dk_ref[...] = jnp.zeros_like(dk_ref)
            dv_ref[...] = jnp.zeros_like(dv_ref)
            db_ref[...] = jnp.zeros_like(db_ref)

        @pl.when(s < nlat)
        def _():
            for rr in range(NA_RPS):
                r0, off = _na_rows(s * NA_RPS + rr, GR)
                rows = pl.ds(pl.multiple_of(r0 * GRID_W, GRID_W), W)
                qs = slice(rr * GRID_W, (rr + 1) * GRID_W)
                for h in range(NA_HEADS):
                    hs = slice(h * HD, (h + 1) * HD)
                    f = lambda q, kw, vw, kx, vx, b: _attn_tile(q, [(kw, b), (kx, None)], [vw, vx], None)
                    _, vjp = jax.vjp(f, q_ref[qs, hs].astype(F32), k_ref[rows, hs].astype(F32), v_ref[rows, hs].astype(F32),
                                     k_ref[L:T, hs].astype(F32), v_ref[L:T, hs].astype(F32), b_ref[h, off])
                    dq, dkw, dvw, dkx, dvx, db = vjp(do_ref[qs, hs].astype(F32))
                    dq_ref[qs, hs] = dq.astype(dq_ref.dtype)
                    dk_ref[rows, hs] += dkw
                    dv_ref[rows, hs] += dvw
                    dk_ref[L:T, hs] += dkx
                    dv_ref[L:T, hs] += dvx
                    db_ref[h, off] += db

        @pl.when(s >= nlat)
        def _():
            for h in range(NA_HEADS):
                hs = slice(h * HD, (h + 1) * HD)
                f = lambda q, kx, vx: _attn_tile(q, [(kx, None)], [vx], None)
                _, vjp = jax.vjp(f, q_ref[:, hs].astype(F32), k_ref[L:T, hs].astype(F32), v_ref[L:T, hs].astype(F32))
                dq, dkx, dvx = vjp(do_ref[:, hs].astype(F32))
                dq_ref[:, hs] = dq.astype(dq_ref.dtype)
                dk_ref[L:T, hs] += dkx
                dv_ref[L:T, hs] += dvx

    one = pl.Buffered(1)
    full = lambda shape: pl.BlockSpec(shape, lambda r: (0,) * len(shape), pipeline_mode=one)
    return _pc(body, name=name, grid=(T // QB,),
               in_specs=[pl.BlockSpec((QB, 256), lambda r: (r, C_QB // 256)), full((T, 256)), full((T, 256)),
                         full((NA_HEADS, NA_KH, GRID_W, W)), pl.BlockSpec((QB, 256), lambda r: (r, 1))],
               out_specs=[pl.BlockSpec((QB, 256), lambda r: (r, 0)), full((T, 256)), full((T, 256)),
                          full((NA_HEADS, NA_KH, GRID_W, W))],
               out_shape=[_sds((T, 256), BF16), _sds((T, 256), F32), _sds((T, 256), F32), _sds((NA_HEADS, NA_KH, GRID_W, W), F32)],
               compiler_params=_cp(("arbitrary",), 48 << 20))(P, kb, vb, bias, do_src)


def na_rpb_grad(dbias, tag):
    _, e1, e2 = _na_selectors()
    x = dbias.reshape(NA_HEADS, NA_KH, GRID_W, NA_KH, GRID_W).transpose(0, 1, 3, 2, 4).reshape(NA_HEADS * NA_KH * NA_KH, GRID_W * GRID_W)
    r1 = matmul(x, e1, "nn", F32, f"na_rpb_sel1_{tag}", hi=True, tk=1024)
    r2 = matmul(e2, r1, "nn", F32, f"na_rpb_sel2_{tag}", hi=True)
    return r2.reshape(NA_HEADS, 16, 128)[:, :2 * NA_KH - 1, :2 * NA_KW - 1]


_HALO = 8


def _halo_specs(T, col0):
    nh = TR // _HALO
    cur = pl.BlockSpec((TR, 256), lambda i, j: (i, col0 + j))
    prv = pl.BlockSpec((_HALO, 256), lambda i, j: (jnp.maximum(i * nh - 1, 0), col0 + j))
    nxt = pl.BlockSpec((_HALO, 256), lambda i, j: (jnp.minimum((i + 1) * nh, T // _HALO - 1), col0 + j))
    return prv, cur, nxt


def _fill_ext(ext, prv, cur, nxt, i, nL, nT):
    has_prev = jnp.where((i != 0) & (i != nL), 1.0, 0.0)
    has_next = jnp.where((i != nL - 1) & (i != nT - 1), 1.0, 0.0)
    ext[0:_HALO, :] = prv[...].astype(F32) * has_prev
    ext[_HALO:_HALO + TR, :] = cur[...].astype(F32)
    ext[_HALO + TR:, :] = nxt[...].astype(F32) * has_next


def conv_silu_fwd(P, w8, b, nL, name):
    T = P.shape[0]
    nT = T // TR

    def body(prv, cur, nxt, w_ref, b_ref, pre_ref, act_ref, ext):
        i = pl.program_id(0)
        _fill_ext(ext, prv, cur, nxt, i, nL, nT)
        y = jnp.broadcast_to(b_ref[...], (TR, 256))
        for k in range(S_CONV):
            y = y + w_ref[k:k + 1, :] * ext[pl.ds(_HALO - S_CONV // 2 + k, TR), :]
        pre_ref[...] = y
        act_ref[...] = _silu(y)

    prv, cur, nxt = _halo_specs(T, C_XBC // 256)
    out = pl.BlockSpec((TR, 256), lambda i, j: (i, j))
    return _pc(body, name=name, grid=(nT, 4),
               in_specs=[prv, cur, nxt, pl.BlockSpec((8, 256), lambda i, j: (0, j)), pl.BlockSpec((1, 256), lambda i, j: (0, j))],
               out_specs=[out, out], out_shape=[_sds((T, 1024), F32), _sds((T, 1024), F32)],
               scratch_shapes=[pltpu.VMEM((TR + 2 * _HALO, 256), F32)],
               compiler_params=_cp(("parallel", "parallel"), 16 << 20))(P, P, P, w8, b)


def dsilu(pre, dxs_list, db_list, dc_list, name):
    T = pre.shape[0]
    n1, n2, n3 = len(dxs_list), len(db_list), len(dc_list)

    def body(*refs):
        pre_ref = refs[0]
        ins = refs[1:1 + n1 + n2 + n3]
        out = refs[-1]

        def part(rs, lo, hi):
            g = rs[0][...].astype(F32)
            for r in rs[1:]:
                g = g + r[...].astype(F32)
            _, vjp = jax.vjp(_silu, pre_ref[:, lo:hi])
            out[:, lo:hi] = vjp(g)[0]

        part(ins[:n1], 0, 512)
        part(ins[n1:n1 + n2], 512, 768)
        part(ins[n1 + n2:], 768, 1024)

    spec = lambda w: pl.BlockSpec((TR, w), lambda i: (i, 0))
    return _pc(body, name=name, grid=(T // TR,),
               in_specs=[spec(1024)] + [spec(512)] * n1 + [spec(256)] * (n2 + n3),
               out_specs=spec(1024), out_shape=_sds((T, 1024), F32),
               compiler_params=_cp(("parallel",), 32 << 20))(pre, *dxs_list, *db_list, *dc_list)


def conv_bwd(dpre, P, w8, nL, name):
    T = P.shape[0]
    nT = T // TR

    def body(dp, dc, dn, xp, xc, xn, w_ref, dx_ref, dw_ref, db_ref, extd, extx):
        i = pl.program_id(1)
        _fill_ext(extd, dp, dc, dn, i, nL, nT)
        _fill_ext(extx, xp, xc, xn, i, nL, nT)

        @pl.when(i == 0)
        def _():
            dw_ref[...] = jnp.zeros_like(dw_ref)
            db_ref[...] = jnp.zeros_like(db_ref)

        d = dc[...]
        dx = jnp.zeros((TR, 256), F32)
        for k in range(S_CONV):
            dx = dx + w_ref[k:k + 1, :] * extd[pl.ds(_HALO + S_CONV // 2 - k, TR), :]
            dw_ref[k:k + 1, :] += jnp.sum(d * extx[pl.ds(_HALO - S_CONV // 2 + k, TR), :], axis=0, keepdims=True)
        dx_ref[...] = dx.astype(dx_ref.dtype)
        db_ref[0:1, :] += jnp.sum(d, axis=0, keepdims=True)

    def swap(spec):
        f = spec.index_map
        return pl.BlockSpec(spec.block_shape, lambda j, i: f(i, j))

    dprv, dcur, dnxt = [swap(s) for s in _halo_specs(T, 0)]
    xprv, xcur, xnxt = [swap(s) for s in _halo_specs(T, C_XBC // 256)]
    acc = pl.BlockSpec((8, 256), lambda j, i: (0, j))
    return _pc(body, name=name, grid=(4, nT),
               in_specs=[dprv, dcur, dnxt, xprv, xcur, xnxt, acc],
               out_specs=[pl.BlockSpec((TR, 256), lambda j, i: (i, j)), acc, acc],
               out_shape=[_sds((T, 1024), BF16), _sds((8, 1024), F32), _sds((8, 1024), F32)],
               scratch_shapes=[pltpu.VMEM((TR + 2 * _HALO, 256), F32), pltpu.VMEM((TR + 2 * _HALO, 256), F32)],
               compiler_params=_cp(("parallel", "arbitrary"), 16 << 20))(dpre, dpre, dpre, P, P, P, w8)


def _onehot_row(h, n):
    return (lax.broadcasted_iota(jnp.int32, (1, n), 1) == h).astype(F32)


def _onehot_col(h, n):
    return (lax.broadcasted_iota(jnp.int32, (n, 1), 0) == h).astype(F32)


def _ssd_chunk(xs, dtr, dtb, alog, bm, cm, hin, reverse):
    Qn = S_Q
    ii = lax.broadcasted_iota(jnp.int32, (Qn, Qn), 0)
    jj = lax.broadcasted_iota(jnp.int32, (Qn, Qn), 1)
    keep = (ii <= jj) if reverse else (ii >= jj)
    tri = keep.astype(F32)
    triT = ((jj <= ii) if reverse else (jj >= ii)).astype(F32)
    eye = (ii == jj).astype(F32)
    dt = _softplus(dtr + dtb)
    a = dt * (-jnp.exp(alog))
    cs = hdot(tri, a)
    csT = hdot(a, triT, "tn")
    dtT = hdot(dt, eye, "tn")
    last = _onehot_row(0 if reverse else Qn - 1, Qn)
    ys, houts = [], []
    for g in range(S_GROUPS):
        G = bdot(cm[g], bm[g], "nt")
        for r in range(S_HEADS // S_GROUPS):
            h = g * (S_HEADS // S_GROUPS) + r
            eh_r, eh_c = _onehot_row(h, S_HEADS), _onehot_col(h, S_HEADS)
            cs_c = jnp.sum(cs * eh_r, axis=1, keepdims=True)
            dt_c = jnp.sum(dt * eh_r, axis=1, keepdims=True)
            cs_r = jnp.sum(csT * eh_c, axis=0, keepdims=True)
            dt_r = jnp.sum(dtT * eh_c, axis=0, keepdims=True)
            tot = jnp.sum(cs_r * last, axis=1, keepdims=True)
            decay = jnp.exp(jnp.where(keep, cs_c - cs_r, NEG))
            w = G * decay * dt_r
            y = bdot(w, xs[h], "nn") + bdot(cm[g], hin[h], "nt") * jnp.exp(cs_c)
            xsc = xs[h] * (jnp.exp(tot - cs_c) * dt_c)
            hout = hin[h] * jnp.exp(tot) + bdot(xsc, bm[g], "tn")
            ys.append(y)
            houts.append(hout)
    return ys, houts


def _ssd_orders(L, Lc):
    nl, ncx = L // S_Q, Lc // S_Q
    fwd = lambda s: jnp.where(s < ncx, nl + s, s - ncx)
    bwd = lambda s: nl + ncx - 1 - s
    return nl + ncx, fwd, bwd


def _ssd_in_specs(fo, bo, step):
    def at(order, w, col):
        return pl.BlockSpec((S_Q, w), lambda u: (order(step(u)), col))
    specs = []
    for order in (fo, bo):
        specs += [at(order, 512, 0), at(order, 256, 2), at(order, 256, 3), at(order, 128, C_DT // 128)]
    return specs


def ssd_fwd(act, P, dtb, alog, L, Lc, name):
    T = L + Lc
    ns, fo, bo = _ssd_orders(L, Lc)

    def body(xf, bf, cf, df, xb, bb, cb, db, dtb_ref, al_ref, yf, yb, hsf, hsb, Hf, Hb):
        s = pl.program_id(0)

        @pl.when(s == 0)
        def _():
            Hf[...] = jnp.zeros_like(Hf)
            Hb[...] = jnp.zeros_like(Hb)

        for d, (x_r, b_r, c_r, dt_r, y_r, hs_r, H) in enumerate(((xf, bf, cf, df, yf, hsf, Hf), (xb, bb, cb, db, yb, hsb, Hb))):
            hin = [H[h] for h in range(S_HEADS)]
            hs_r[0] = H[...]
            ys, houts = _ssd_chunk(
                [x_r[:, h * S_P:(h + 1) * S_P] for h in range(S_HEADS)], dt_r[:, d * 8:(d + 1) * 8],
                dtb_ref[d:d + 1, 0:8], al_ref[d:d + 1, 0:8],
                [b_r[:, g * S_N:(g + 1) * S_N] for g in range(S_GROUPS)], [c_r[:, g * S_N:(g + 1) * S_N] for g in range(S_GROUPS)],
                hin, reverse=(d == 1))
            for h in range(S_HEADS):
                y_r[:, h * S_P:(h + 1) * S_P] = ys[h]
                H[h] = houts[h]

    ident = lambda u: u
    small = pl.BlockSpec((8, 128), lambda u: (0, 0))
    hspec = pl.BlockSpec((1, S_HEADS, S_P, S_N), lambda u: (u, 0, 0, 0))
    return _pc(body, name=name, grid=(ns,),
               in_specs=_ssd_in_specs(fo, bo, ident) + [small, small],
               out_specs=[pl.BlockSpec((S_Q, 512), lambda u: (fo(u), 0)), pl.BlockSpec((S_Q, 512), lambda u: (bo(u), 0)), hspec, hspec],
               out_shape=[_sds((T, 512), F32), _sds((T, 512), F32), _sds((ns, S_HEADS, S_P, S_N), F32), _sds((ns, S_HEADS, S_P, S_N), F32)],
               scratch_shapes=[pltpu.VMEM((S_HEADS, S_P, S_N), F32), pltpu.VMEM((S_HEADS, S_P, S_N), F32)],
               compiler_params=_cp(("arbitrary",), 32 << 20))(act, act, act, P, act, act, act, P, dtb, alog)


def ssd_bwd(act, P, dtb, alog, hsf, hsb, dy, L, Lc, name):
    T = L + Lc
    ns, fo, bo = _ssd_orders(L, Lc)
    step = lambda u: ns - 1 - u

    def body(xf, bf, cf, df, xb, bb, cb, db, dtb_ref, al_ref, hsf_r, hsb_r, dyf, dyb,
             dxf, dbf, dcf, ddf, dxb, dbb, dcb, ddb, ddtb, dal, dHf, dHb):
        u = pl.program_id(0)

        @pl.when(u == 0)
        def _():
            dHf[...] = jnp.zeros_like(dHf)
            dHb[...] = jnp.zeros_like(dHb)
            ddtb[...] = jnp.zeros_like(ddtb)
            dal[...] = jnp.zeros_like(dal)

        dirs = ((xf, bf, cf, df, hsf_r, dyf, dxf, dbf, dcf, ddf, dHf), (xb, bb, cb, db, hsb_r, dyb, dxb, dbb, dcb, ddb, dHb))
        for d, (x_r, b_r, c_r, dt_r, hs_r, dy_r, dx_o, db_o, dc_o, dd_o, dH) in enumerate(dirs):
            f = functools.partial(_ssd_chunk, reverse=(d == 1))
            _, vjp = jax.vjp(
                f, [x_r[:, h * S_P:(h + 1) * S_P] for h in range(S_HEADS)], dt_r[:, d * 8:(d + 1) * 8],
                dtb_ref[d:d + 1, 0:8], al_ref[d:d + 1, 0:8],
                [b_r[:, g * S_N:(g + 1) * S_N] for g in range(S_GROUPS)], [c_r[:, g * S_N:(g + 1) * S_N] for g in range(S_GROUPS)],
                [hs_r[0, h] for h in range(S_HEADS)])
            gx, gdt, gdtb, gal, gb, gc, gh = vjp(([dy_r[:, h * S_P:(h + 1) * S_P] for h in range(S_HEADS)],
                                                  [dH[h] for h in range(S_HEADS)]))
            for h in range(S_HEADS):
                dx_o[:, h * S_P:(h + 1) * S_P] = gx[h]
                dH[h] = gh[h]
            for g in range(S_GROUPS):
                db_o[:, g * S_N:(g + 1) * S_N] = gb[g]
                dc_o[:, g * S_N:(g + 1) * S_N] = gc[g]
            dd_o[...] = gdt
            ddtb[d:d + 1, 0:8] += gdtb
            dal[d:d + 1, 0:8] += gal

    small = pl.BlockSpec((8, 128), lambda u: (0, 0))
    hspec = pl.BlockSpec((1, S_HEADS, S_P, S_N), lambda u: (step(u), 0, 0, 0))
    at = lambda order, w: pl.BlockSpec((S_Q, w), lambda u: (order(step(u)), 0))
    outs = []
    for order in (fo, bo):
        outs += [at(order, 512), at(order, 256), at(order, 256), at(order, 8)]
    oshape = [_sds((T, 512), F32), _sds((T, 256), F32), _sds((T, 256), F32), _sds((T, 8), F32)]
    return _pc(body, name=name, grid=(ns,),
               in_specs=_ssd_in_specs(fo, bo, step) + [small, small, hspec, hspec, at(fo, 512), at(bo, 512)],
               out_specs=outs + [small, small], out_shape=oshape + oshape + [_sds((8, 128), F32), _sds((8, 128), F32)],
               scratch_shapes=[pltpu.VMEM((S_HEADS, S_P, S_N), F32), pltpu.VMEM((S_HEADS, S_P, S_N), F32)],
               compiler_params=_cp(("arbitrary",), 40 << 20))(act, act, act, P, act, act, act, P, dtb, alog, hsf, hsb, dy, dy)


def _ssm_out(yf, yb, xs, z, dskip, g):
    y = (yf + yb + dskip * xs) * _silu(z)
    return (y * lax.rsqrt(jnp.mean(y * y, axis=-1, keepdims=True) + EPS)) * g


def ssm_out_fwd(yf, yb, act, P, dskip, g, name):
    T = yf.shape[0]

    def body(yf_r, yb_r, xs_r, z_r, d_r, g_r, o_r):
        o_r[...] = _ssm_out(yf_r[...], yb_r[...], xs_r[...], z_r[...], d_r[...], g_r[...]).astype(o_r.dtype)

    row = pl.BlockSpec((TR, 512), lambda i: (i, 0))
    vec = pl.BlockSpec((1, 512), lambda i: (0, 0))
    return _pc(body, name=name, grid=(T // TR,),
               in_specs=[row, row, row, pl.BlockSpec((TR, 512), lambda i: (i, C_Z // 512)), vec, vec],
               out_specs=row, out_shape=_sds((T, 512), BF16),
               compiler_params=_cp(("parallel",), 16 << 20))(yf, yb, act, P, dskip, g)


def ssm_out_bwd(yf, yb, act, P, dskip, g, do_src, name):
    T = yf.shape[0]

    def body(yf_r, yb_r, xs_r, z_r, d_r, g_r, do_r, dy_r, dxs_r, dz_r, dv_r):
        @pl.when(pl.program_id(0) == 0)
        def _():
            dv_r[...] = jnp.zeros_like(dv_r)

        _, vjp = jax.vjp(_ssm_out, yf_r[...], yb_r[...], xs_r[...], z_r[...], d_r[...], g_r[...])
        dyf, _, dxs, dz, dd, dg = vjp(do_r[...].astype(F32))
        dy_r[...] = dyf
        dxs_r[...] = dxs
        dz_r[...] = dz.astype(dz_r.dtype)
        dv_r[0:1, :] += dd
        dv_r[1:2, :] += dg

    row = pl.BlockSpec((TR, 512), lambda i: (i, 0))
    vec = pl.BlockSpec((1, 512), lambda i: (0, 0))
    return _pc(body, name=name, grid=(T // TR,),
               in_specs=[row, row, row, pl.BlockSpec((TR, 512), lambda i: (i, C_Z // 512)), vec, vec,
                         pl.BlockSpec((TR, 512), lambda i: (i, 1))],
               out_specs=[row, row, row, pl.BlockSpec((8, 512), lambda i: (0, 0))],
               out_shape=[_sds((T, 512), F32), _sds((T, 512), F32), _sds((T, 512), BF16), _sds((8, 512), F32)],
               compiler_params=_cp(("arbitrary",), 24 << 20))(yf, yb, act, P, dskip, g, do_src)


def add_halves(xv, got, cvec, name):
    n, r, cdim = xv.shape
    h = r // 2

    def body(c_ref, x_ref, g_ref, o_ref):
        o_ref[...] = (x_ref[...].astype(F32) + g_ref[...].astype(F32)).astype(o_ref.dtype)

    gs = pltpu.PrefetchScalarGridSpec(
        num_scalar_prefetch=1, grid=(n,),
        in_specs=[pl.BlockSpec((1, h, cdim), lambda k, c_ref: (k, c_ref[0], 0)), pl.BlockSpec((1, h, cdim), lambda k, c_ref: (k, 0, 0))],
        out_specs=pl.BlockSpec((1, h, cdim), lambda k, c_ref: (k, 0, 0)))
    return _pc(body, name=name, grid_spec=gs, out_shape=_sds((n, h, cdim), BF16),
               compiler_params=_cp(("arbitrary",), 24 << 20))(cvec, xv, got)


def sum_slots(a, name):
    n, r, cdim = a.shape
    tr = _div_tile(r, 512, 16)

    def body(a_ref, o_ref):
        acc = a_ref[0].astype(F32)
        for k in range(1, n):
            acc = acc + a_ref[k].astype(F32)
        o_ref[...] = acc

    return _pc(body, name=name, grid=(r // tr,), in_specs=[pl.BlockSpec((n, tr, cdim), lambda i: (0, i, 0))],
               out_specs=pl.BlockSpec((tr, cdim), lambda i: (i, 0)), out_shape=_sds((r, cdim), F32),
               compiler_params=_cp(("parallel",), 32 << 20))(a)


def adamw(w, g, m, v, name):
    R, C = w.shape
    tr = _div_tile(R, max(8, (1 << 19) // max(C, 1) // 8 * 8), 8) if R % 8 == 0 else R
    c1 = 1.0 / (1.0 - ADAM_B1 ** ADAM_STEP)
    c2 = 1.0 / (1.0 - ADAM_B2 ** ADAM_STEP)

    def body(w_ref, g_ref, m_ref, v_ref, d_ref, mo_ref, vo_ref):
        gg = g_ref[...]
        mn = ADAM_B1 * m_ref[...] + (1.0 - ADAM_B1) * gg
        vn = ADAM_B2 * v_ref[...] + (1.0 - ADAM_B2) * (gg * gg)
        d_ref[...] = -ADAM_LR * ((mn * c1) / (jnp.sqrt(vn * c2) + ADAM_EPS) + ADAM_WD * w_ref[...])
        mo_ref[...] = mn
        vo_ref[...] = vn

    spec = pl.BlockSpec((tr, C), lambda i: (i, 0))
    return _pc(body, name=name, grid=(R // tr,), in_specs=[spec] * 4, out_specs=[spec] * 3,
               out_shape=[_sds((R, C), F32)] * 3, compiler_params=_cp(("parallel",), 32 << 20))(w, g, m, v)


def _me():
    return lax.axis_index("x"), lax.axis_index("y"), lax.axis_index("c")


def _flip(v, bit):
    return 1 - v if bit else v


def allgather8(xv, name):
    R = xv.shape[0]

    def body(x_ref, out_ref, sum_ref, send_sems, recv_sems):
        mx, my, mc = _me()
        me = 4 * mx + 2 * my + mc
        out_ref[me] = x_ref[...]
        sends, recvs = [], []
        for k in range(1, 8):
            px, py, pc = _flip(mx, k & 4), _flip(my, k & 2), _flip(mc, k & 1)
            peer = 4 * px + 2 * py + pc
            sends.append(pltpu.make_async_remote_copy(src_ref=x_ref, dst_ref=out_ref.at[me], send_sem=send_sems.at[k - 1],
                                                      recv_sem=recv_sems.at[k - 1], device_id=(px, py, pc), device_id_type=MESH))
            recvs.append(pltpu.make_async_remote_copy(src_ref=x_ref, dst_ref=out_ref.at[peer], send_sem=send_sems.at[k - 1],
                                                      recv_sem=recv_sems.at[k - 1], device_id=(px, py, pc), device_id_type=MESH))
        for cp in sends:
            cp.start()
        for cp in recvs:
            cp.wait_recv()
        for cp in sends:
            cp.wait_send()
        acc = out_ref[0]
        for d in range(1, 8):
            acc = acc + out_ref[d]
        sum_ref[...] = acc

    vm = pl.BlockSpec(memory_space=pltpu.VMEM)
    return _pc(body, name=name, pin=False, in_specs=[vm], out_specs=[vm, vm], out_shape=[_sds((8, R, 128), F32), _sds((R, 128), F32)],
               scratch_shapes=[pltpu.SemaphoreType.DMA((7,)), pltpu.SemaphoreType.DMA((7,))],
               compiler_params=_cp(None, 32 << 20))(xv)


def _other_chips(mx, my):
    return [(1 - mx, my), (mx, 1 - my), (1 - mx, 1 - my)]


def _halves(r, mc, mult):
    h = r // 2
    return pl.ds(pl.multiple_of(mc * h, mult), h), pl.ds(pl.multiple_of((1 - mc) * h, mult), h)


def _rcopy(src, dst, send_sems, recv_sems, k, to):
    return pltpu.make_async_remote_copy(src_ref=src, dst_ref=dst, send_sem=send_sems.at[k], recv_sem=recv_sems.at[k],
                                        device_id=to, device_id_type=MESH)


def _gather_body(xs, outs, send_sems, recv_sems, local_sems):
    n = len(xs)
    mx, my, mc = _me()
    chip = 2 * mx + my
    sib = (mx, my, 1 - mc)
    chips = _other_chips(mx, my)
    idx = [2 * cx + cy for cx, cy in chips]
    cp = functools.partial(_rcopy, send_sems=send_sems, recv_sems=recv_sems)
    hv = [_halves(x.shape[0], mc, 16) for x in xs]
    local, first, passed = [], [], []
    for a in range(n):
        local.append(pltpu.make_async_copy(xs[a], outs[a].at[chip], local_sems.at[a]))
        local[-1].start()
        for j, (cx, cy) in enumerate(chips):
            first.append(cp(xs[a].at[hv[a][0]], outs[a].at[chip, hv[a][0]], k=6 * a + j, to=(cx, cy, mc)))
            first[-1].start()
    for a in range(n):
        for j in range(3):
            cp(xs[a].at[hv[a][0]], outs[a].at[idx[j], hv[a][0]], k=6 * a + j, to=sib).wait_recv()
            passed.append(cp(outs[a].at[idx[j], hv[a][0]], outs[a].at[idx[j], hv[a][0]], k=6 * a + 3 + j, to=sib))
            passed[-1].start()
    for a in range(n):
        for j in range(3):
            cp(xs[a].at[hv[a][1]], outs[a].at[idx[j], hv[a][1]], k=6 * a + 3 + j, to=sib).wait_recv()
    for c_ in first + passed:
        c_.wait_send()
    for c_ in local:
        c_.wait()


def gather_weights(shards, name):
    n = len(shards)

    def body(*refs):
        _gather_body(refs[:n], refs[n:2 * n], *refs[2 * n:])

    hbm = pl.BlockSpec(memory_space=pl.ANY)
    return _pc(body, name=name, in_specs=[hbm] * n, out_specs=[hbm] * n, out_shape=[_sds((4,) + x.shape, x.dtype) for x in shards],
               scratch_shapes=[pltpu.SemaphoreType.DMA((6 * n,)), pltpu.SemaphoreType.DMA((6 * n,)), pltpu.SemaphoreType.DMA((n,))])(*shards)


GATHER_REST_ID = 3


def gather_weights_sc(shards, name):
    n = len(shards)
    x_refs = [jax.new_ref(x, memory_space=pltpu.MemorySpace.HBM) for x in shards]
    out_refs = [jax.empty_ref(_sds((4,) + x.shape, x.dtype), memory_space=pltpu.MemorySpace.HBM) for x in shards]

    @pl.kernel(mesh=plsc.ScalarSubcoreMesh(axis_name="sc", num_cores=1), name=name,
               scratch_types=(pltpu.SemaphoreType.DMA((6 * n,)), pltpu.SemaphoreType.DMA((6 * n,)), pltpu.SemaphoreType.DMA((n,))),
               compiler_params=pltpu.CompilerParams(collective_id=GATHER_REST_ID))
    def launch(send_sems, recv_sems, local_sems):
        mx, my, mc = _me()
        barrier = pltpu.get_barrier_semaphore()
        for peer in [(mx, my, 1 - mc)] + [(cx, cy, mc) for cx, cy in _other_chips(mx, my)]:
            pl.semaphore_signal(barrier, inc=1, device_id=peer, device_id_type=MESH)
        pl.semaphore_wait(barrier, 4)
        _gather_body(x_refs, out_refs, send_sems, recv_sems, local_sems)

    launch()
    return [o[...] for o in out_refs]


def swap_halves(arrs, name):
    n = len(arrs)

    def body(*refs):
        xs, outs = refs[:n], refs[n:2 * n]
        send_sems, recv_sems = refs[2 * n:]
        mx, my, mc = _me()
        cps = []
        for a in range(n):
            theirs = _halves(xs[a].shape[1], mc, 16)[1]
            cps.append(_rcopy(xs[a].at[pl.ds(0, 4), theirs], outs[a], send_sems, recv_sems, a, (mx, my, 1 - mc)))
            cps[-1].start()
        for c_ in cps:
            c_.wait()

    hbm = pl.BlockSpec(memory_space=pl.ANY)
    return _pc(body, name=name, in_specs=[hbm] * n, out_specs=[hbm] * n,
               out_shape=[_sds((4, x.shape[1] // 2, x.shape[2]), x.dtype) for x in arrs],
               scratch_shapes=[pltpu.SemaphoreType.DMA((n,)), pltpu.SemaphoreType.DMA((n,))])(*arrs)


def scatter_chips(arrs, name):
    n = len(arrs)

    def body(*refs):
        xs, outs = refs[:n], refs[n:2 * n]
        send_sems, recv_sems, local_sems = refs[2 * n:]
        mx, my, mc = _me()
        chip = 2 * mx + my
        chips = _other_chips(mx, my)
        idx = [2 * cx + cy for cx, cy in chips]
        cp = functools.partial(_rcopy, send_sems=send_sems, recv_sems=recv_sems)
        local, sends = [], []
        for a in range(n):
            local.append(pltpu.make_async_copy(xs[a].at[chip], outs[a].at[chip], local_sems.at[a]))
            local[-1].start()
            for j, (cx, cy) in enumerate(chips):
                sends.append(cp(xs[a].at[idx[j]], outs[a].at[chip], k=3 * a + j, to=(cx, cy, mc)))
                sends[-1].start()
        for a in range(n):
            for j, (cx, cy) in enumerate(chips):
                cp(xs[a].at[idx[j]], outs[a].at[idx[j]], k=3 * a + j, to=(cx, cy, mc)).wait_recv()
        for c_ in sends:
            c_.wait_send()
        for c_ in local:
            c_.wait()

    hbm = pl.BlockSpec(memory_space=pl.ANY)
    return _pc(body, name=name, in_specs=[hbm] * n, out_specs=[hbm] * n, out_shape=[_sds(x.shape, x.dtype) for x in arrs],
               scratch_shapes=[pltpu.SemaphoreType.DMA((3 * n,)), pltpu.SemaphoreType.DMA((3 * n,)), pltpu.SemaphoreType.DMA((n,))])(*arrs)


def share_halves(parts, name):
    flat = [p for w in parts for p in w]
    nw, n = len(parts), len(flat)
    depth = n // nw

    def body(*refs):
        xs, outs = refs[:n], refs[n:n + nw]
        send_sems, recv_sems, local_sems = refs[n + nw:]
        mx, my, mc = _me()
        sib = (mx, my, 1 - mc)
        local, sends, recvs = [], [], []
        for a in range(n):
            w, l = a // depth, a % depth
            mine, theirs = _halves(outs[w].shape[1], mc, 8)
            local.append(pltpu.make_async_copy(xs[a], outs[w].at[l, mine], local_sems.at[a]))
            sends.append(_rcopy(xs[a], outs[w].at[l, mine], send_sems, recv_sems, a, sib))
            recvs.append(_rcopy(xs[a], outs[w].at[l, theirs], send_sems, recv_sems, a, sib))
            local[-1].start()
            sends[-1].start()
        for c_ in recvs:
            c_.wait_recv()
        for c_ in sends:
            c_.wait_send()
        for c_ in local:
            c_.wait()

    hbm = pl.BlockSpec(memory_space=pl.ANY)
    return _pc(body, name=name, in_specs=[hbm] * n, out_specs=[hbm] * nw,
               out_shape=[_sds((depth, 2 * w[0].shape[0], w[0].shape[1]), F32) for w in parts],
               scratch_shapes=[pltpu.SemaphoreType.DMA((n,)), pltpu.SemaphoreType.DMA((n,)), pltpu.SemaphoreType.DMA((n,))])(*flat)


_BIG = ("w_in", "w_out", "w_ffn_in", "w_ffn_out")
N_CHIPS = 4
DEPTH = 2


def _pad_rows(v, mult=8):
    n = v.shape[0]
    rows = -(-n // 128)
    rows = -(-rows // mult) * mult
    return jnp.pad(v, (0, rows * 128 - n)).reshape(rows, 128)


class _Flat:
    def __init__(self):
        self.items = []

    def add(self, name, a):
        self.items.append((name, a.shape, a.reshape(-1).astype(F32)))

    def rows(self):
        return _pad_rows(jnp.concatenate([a for _, _, a in self.items]))

    def split(self, rows):
        flat = rows.reshape(-1)
        out, o = {}, 0
        for name, shape, a in self.items:
            out[name] = flat[o:o + a.shape[0]].reshape(shape)
            o += a.shape[0]
        return out

    def split_lead(self, rows3):
        n = rows3.shape[0]
        flat = rows3.reshape(n, -1)
        out, o = {}, 0
        for name, shape, a in self.items:
            out[name] = flat[:, o:o + a.shape[0]].reshape((n,) + tuple(shape))
            o += a.shape[0]
        return out


def _gsv(rows):
    z = jnp.zeros((2, D), F32)
    r = [z if a is None else a for a in rows] + [z] * 5
    return jnp.stack(r, axis=1)


def _pad8(a, rows=8, cols=128):
    return jnp.zeros((rows, cols), F32).at[:a.shape[0], :a.shape[1]].set(a.astype(F32))


def kernel(x, c, ctx, c_ctx, w_mod, b_mod, g_mix, w_in, wa_sink, na_rpb, ssm_conv_w, ssm_conv_b, ssm_dt_bias, ssm_a_log, ssm_d, ssm_norm_g, w_out, g_ffn, w_ffn_in, w_ffn_out, g_final, loss_target, m_c_ctx, m_w_mod, m_b_mod, m_g_mix, m_w_in, m_wa_sink, m_na_rpb, m_ssm_conv_w, m_ssm_conv_b, m_ssm_dt_bias, m_ssm_a_log, m_ssm_d, m_ssm_norm_g, m_w_out, m_g_ffn, m_w_ffn_in, m_w_ffn_out, m_g_final, v_c_ctx, v_w_mod, v_b_mod, v_g_mix, v_w_in, v_wa_sink, v_na_rpb, v_ssm_conv_w, v_ssm_conv_b, v_ssm_dt_bias, v_ssm_a_log, v_ssm_d, v_ssm_norm_g, v_w_out, v_g_ffn, v_w_ffn_in, v_w_ffn_out, v_g_final):
    L, Lc = x.shape[1], ctx.shape[1]
    T = L + Lc
    nL = L // TR
    mx, my, mc = lax.axis_index("x"), lax.axis_index("y"), lax.axis_index("c")
    dev = 4 * mx + 2 * my + mc
    chip = 2 * mx + my
    MODW = 6 * D // N_CHIPS
    CW = 1024 // N_CHIPS

    sc = _silu(c.astype(F32))
    scc = _silu(c_ctx.astype(F32))[None]
    f1 = _Flat()
    f1.add("sc", sc)
    f1.add("conv_w", ssm_conv_w)
    g1, _ = allgather8(f1.rows(), "gather_cond")
    g1 = f1.split_lead(g1)
    sc_all = g1["sc"][:, 0]
    conv_w = jnp.concatenate([g1["conv_w"][2 * k] for k in range(N_CHIPS)], axis=-1)
    A16 = jnp.concatenate([sc_all, scc, jnp.zeros((7, D), F32)], axis=0)

    mod_part = jnp.stack([matmul(A16, w_mod[l], "nn", F32, f"mod_fwd{l}") for l in range(DEPTH)])
    f2 = _Flat()
    f2.add("mod", mod_part)
    g2, _ = allgather8(f2.rows(), "gather_mod")
    g2 = f2.split_lead(g2)["mod"]
    mods = jnp.concatenate([g2[2 * k] for k in range(N_CHIPS)], axis=-1) + b_mod[:, None, :]
    mod_l = lax.dynamic_index_in_dim(mods, dev, axis=1, keepdims=False).reshape(DEPTH, 6, D)
    mod_c = mods[:, 8].reshape(DEPTH, 6, D)
    mod = jnp.stack([mod_l, mod_c], axis=1)
    mrow = lambda l, j: mod[l, :, j]

    own = {"w_in": w_in, "w_out": w_out, "w_ffn_in": w_ffn_in, "w_ffn_out": w_ffn_out}
    sh16 = [own[n][l].astype(BF16) for n in _BIG for l in range(DEPTH)]
    gath = list(gather_weights(sh16[:1], "gather_first")) + list(gather_weights_sc(sh16[1:], "gather_rest"))
    gw = {n: [gath[DEPTH * i + l] for l in range(DEPTH)] for i, n in enumerate(_BIG)}
    W_in = [jnp.pad(jnp.concatenate([g[k] for k in range(N_CHIPS)], axis=1), ((0, 0), (0, IN_PAD - IN_COLS))) for g in gw["w_in"]]
    W_out = [g.reshape(D, D) for g in gw["w_out"]]
    W_fo = [g.reshape(D_FF, D) for g in gw["w_ffn_out"]]
    W_fi = gw["w_ffn_in"]

    cos, sin, rotm = rope_tables(L, Lc)
    x0 = jnp.concatenate([x[0], ctx[0]], axis=0).astype(F32)

    sv = []
    xin = x0
    gsv_first = _gsv([None, mrow(0, 0), mrow(0, 1)])
    _, h1 = res_norm_mod(x0, None, gsv_first, g_mix[0][None], nL, "norm_first")
    for l in range(DEPTH):
        s = {"xin": xin, "h1": h1}
        P = matmul(h1, W_in[l], "nn", F32, f"in_proj{l}", tn=IN_PAD)
        qr, kr, kb, vb = rope_apply(P, C_QA // 256, P, C_KA // 128, cos, sin, rotm, False, f"rope{l}", kv_src=P)
        sink8 = _pad8(jnp.broadcast_to(wa_sink[l][:, None], (WA_HEADS, 128)))
        oa = win_attn_fwd(qr, kr, P, sink8, L, Lc, f"wa_fwd{l}")
        bias = na_bias_table(na_rpb[l], l)
        ob = na_fwd(P, kb, vb, bias, L, Lc, f"na_fwd{l}")
        w8 = jnp.concatenate([conv_w[l], jnp.zeros((1, 1024), F32)], axis=0)
        pre, act = conv_silu_fwd(P, w8, ssm_conv_b[l][None], nL, f"conv_fwd{l}")
        dtb8, al8 = _pad8(ssm_dt_bias[l]), _pad8(ssm_a_log[l])
        yf, yb, hsf, hsb = ssd_fwd(act, P, dtb8, al8, L, Lc, f"ssd_fwd{l}")
        dskip = jnp.repeat(ssm_d[l], S_P)[None]
        oc = ssm_out_fwd(yf, yb, act, P, dskip, ssm_norm_g[l][None], f"ssm_out_fwd{l}")
        mixin = jnp.concatenate([oa, ob, oc], axis=1)
        mix = matmul(mixin, W_out[l], "nn", F32, f"out_proj{l}")
        gsv_mid = _gsv([mrow(l, 2), mrow(l, 3), mrow(l, 4)])
        x1, h2 = res_norm_mod(xin, mix, gsv_mid, g_ffn[l][None], nL, f"norm_mid{l}")
        gu = matmul_fi(h2, W_fi[l], "nn", BF16, f"ffn_in{l}")
        af = swiglu_fwd(gu, f"swiglu_fwd{l}")
        fo = matmul(af, W_fo[l], "nn", F32, f"ffn_out{l}")
        s.update(P=P, qr=qr, kr=kr, sink8=sink8, kb=kb, vb=vb, bias=bias, w8=w8, pre=pre, act=act, dtb8=dtb8, al8=al8, yf=yf,
                 yb=yb, hsf=hsf, hsb=hsb, dskip=dskip, mixin=mixin, mix=mix, gsv_mid=gsv_mid, x1=x1, h2=h2, gu=gu, af=af, fo=fo)
        if l + 1 < DEPTH:
            s["gsv_end"] = _gsv([mrow(l, 5), mrow(l + 1, 0), mrow(l + 1, 1)])
            xin, h1 = res_norm_mod(x1, fo, s["gsv_end"], g_mix[l + 1][None], nL, f"norm_end{l}")
        else:
            s["gsv_end"] = _gsv([mrow(l, 5), None, None])
        sv.append(s)

    last = sv[-1]
    loss8, dres, dfo, dgsv_end, dg_final = final_loss(last["x1"], last["fo"], last["gsv_end"], g_final[None], loss_target[0].astype(F32), nL, "final_loss")
    loss = lax.psum(loss8[0, 0], ("x", "y", "c"))

    dmod = [[None] * 6 for _ in range(DEPTH)]
    gW = {n: [None] * DEPTH for n in _BIG}
    small = [dict() for _ in range(DEPTH)]
    grad_x = None
    for l in reversed(range(DEPTH)):
        s = sv[l]
        dmod[l][5] = dgsv_end[:, 0]
        if l + 1 < DEPTH:
            dmod[l + 1][0], dmod[l + 1][1] = dgsv_end[:, 1], dgsv_end[:, 2]
        daf = matmul(dfo, W_fo[l], "nt", BF16, f"ffn_out_dx{l}")
        gW["w_ffn_out"][l] = matmul(s["af"], dfo, "tn", BF16, f"ffn_out_dw{l}", tm=1408).reshape(N_CHIPS, D_FF // N_CHIPS, D)
        dgu = swiglu_bwd(s["gu"], daf, f"swiglu_bwd{l}")
        dh2 = matmul_fi(dgu, W_fi[l], "nt", F32, f"ffn_in_dx{l}")
        gW["w_ffn_in"][l] = matmul_fi(s["h2"], dgu, "tn", BF16, f"ffn_in_dw{l}")
        dres, dmix, dgsv_mid, dg_ffn = res_norm_mod_bwd(s["x1"], s["mix"], s["gsv_mid"], g_ffn[l][None], dh2, dres, nL, f"norm_mid_bwd{l}")
        dmod[l][2], dmod[l][3], dmod[l][4] = dgsv_mid[:, 0], dgsv_mid[:, 1], dgsv_mid[:, 2]
        dmixin = matmul(dmix, W_out[l], "nt", F32, f"out_proj_dx{l}")
        gW["w_out"][l] = matmul(s["mixin"], dmix, "tn", BF16, f"out_proj_dw{l}", tm=1024).reshape(N_CHIPS, D // N_CHIPS, D)
        P = s["P"]
        dqr, dkr, dva, dsink = win_attn_bwd(s["qr"], s["kr"], P, s["sink8"], dmixin, L, Lc, f"wa_bwd{l}")
        dqa, dka = rope_apply(dqr, 0, dkr[WA_BLK:WA_BLK + T], 0, cos, sin, rotm, True, f"rope_bwd{l}")
        dqb, dkb, dvb, dbias = na_bwd(P, s["kb"], s["vb"], s["bias"], dmixin, L, Lc, f"na_bwd{l}")
        dy, dxs1, dz, dvec = ssm_out_bwd(s["yf"], s["yb"], s["act"], P, s["dskip"], ssm_norm_g[l][None], dmixin, f"ssm_out_bwd{l}")
        dxf, dbf, dcf, ddf, dxb, dbb, dcb, ddb, ddtb, dal = ssd_bwd(s["act"], P, s["dtb8"], s["al8"], s["hsf"], s["hsb"], dy, L, Lc, f"ssd_bwd{l}")
        dpre = dsilu(s["pre"], [dxf, dxb, dxs1], [dbf, dbb], [dcf, dcb], f"dsilu{l}")
        dxbc, dw8, db8 = conv_bwd(dpre, P, s["w8"], nL, f"conv_bwd{l}")
        dP = jnp.concatenate([dqa, dqb, dz, dka, dva[WA_BLK:WA_BLK + T].astype(BF16), dkb.astype(BF16), dvb.astype(BF16), dxbc,
                              ddf.astype(BF16), ddb.astype(BF16), jnp.zeros((T, IN_PAD - IN_COLS), BF16)], axis=1)
        dh1 = matmul(dP, W_in[l], "nt", F32, f"in_proj_dx{l}", tk=IN_PAD)
        dwin = matmul(s["h1"], dP, "tn", BF16, f"in_proj_dw{l}", tm=512, tn=IN_PAD)
        cw = IN_COLS // N_CHIPS
        gW["w_in"][l] = jnp.stack([dwin[:, k * cw:(k + 1) * cw] for k in range(N_CHIPS)])
        small[l] = dict(g_ffn=dg_ffn[0], wa_sink=dsink[:WA_HEADS, 0], na_rpb=na_rpb_grad(dbias, l), conv_w=dw8[:S_CONV], conv_b=db8[0],
                        dt_bias=ddtb[:2, :8], a_log=dal[:2, :8], ssm_d=dvec[0].reshape(S_HEADS, S_P).sum(axis=1), norm_g=dvec[1])
        if l > 0:
            p = sv[l - 1]
            dres, dfo, dgsv_end, dg_mix = res_norm_mod_bwd(s["xin"], p["fo"], p["gsv_end"], g_mix[l][None], dh1, dres, nL, f"norm_end_bwd{l - 1}")
        else:
            grad_x, _, dgsv_first, dg_mix = res_norm_mod_bwd(s["xin"], None, gsv_first, g_mix[0][None], dh1, dres, nL, "norm_first_bwd")
            dmod[0][0], dmod[0][1] = dgsv_first[:, 1], dgsv_first[:, 2]
        small[l]["g_mix"] = dg_mix[0]
    for l in range(DEPTH):
        for j in range(6):
            if dmod[l][j] is None:
                dmod[l][j] = jnp.zeros((2, D), F32)
    dmod = jnp.stack([jnp.stack(r, axis=1) for r in dmod])

    f3 = _Flat()
    f3.add("dmod_l", dmod[:, 0].reshape(DEPTH, 6 * D))
    f3.add("dmod_c", dmod[:, 1].reshape(DEPTH, 6 * D))
    f3.add("g_final", dg_final[0])
    for n in ("g_mix", "g_ffn", "wa_sink", "na_rpb", "conv_w", "conv_b", "dt_bias", "a_log", "ssm_d", "norm_g"):
        f3.add(n, jnp.stack([small[l][n] for l in range(DEPTH)]))
    g3, s3 = allgather8(f3.rows(), "reduce_small")
    dmod_all = f3.split_lead(g3)["dmod_l"]
    s3 = f3.split(s3)
    dmodc_tot = s3["dmod_c"]
    col0 = chip * MODW
    G16, G16c = [], []
    for l in range(DEPTH):
        rows = jnp.concatenate([dmod_all[:, l], dmodc_tot[l][None], jnp.zeros((7, 6 * D), F32)], axis=0)
        G16.append(lax.dynamic_slice_in_dim(rows, col0, MODW, axis=1))
        rc = jnp.concatenate([dmodc_tot[l][None], jnp.zeros((15, 6 * D), F32)], axis=0)
        G16c.append(lax.dynamic_slice_in_dim(rc, col0, MODW, axis=1))
    grad_w_mod = jnp.stack([matmul(A16, G16[l], "tn", F32, f"mod_dw{l}") for l in range(DEPTH)])
    dscc_part = sum(matmul(G16c[l], w_mod[l], "nt", F32, f"mod_dx{l}")[0] for l in range(DEPTH))
    _, s4 = allgather8(_pad_rows(dscc_part * (mc == 1).astype(F32)), "reduce_cctx")
    dscc = s4.reshape(-1)[:D]
    cc = c_ctx.astype(F32)
    sg = 1.0 / (1.0 + jnp.exp(-cc))
    grad_c_ctx = dscc * (sg * (1.0 + cc * (1.0 - sg)))

    garr = [gW[n][l] for n in _BIG for l in range(DEPTH)]
    got = swap_halves(garr, "reduce_d2d")
    cvec = mc.astype(jnp.int32).reshape(1)
    chip_sum = [add_halves(garr[a], got[a], cvec, f"reduce_add_pair{a}") for a in range(len(garr))]
    parts = scatter_chips(chip_sum, "reduce_ici")
    halves = [sum_slots(parts[a], f"reduce_add_chips{a}") for a in range(len(garr))]
    shared = share_halves([[halves[DEPTH * i + l] for l in range(DEPTH)] for i in range(len(_BIG))], "reduce_share")
    gsh = dict(zip(_BIG, shared))

    grads = {"c_ctx": grad_c_ctx, "w_mod": grad_w_mod, "b_mod": s3["dmod_l"] + s3["dmod_c"], "g_mix": s3["g_mix"], "w_in": gsh["w_in"],
             "wa_sink": s3["wa_sink"], "na_rpb": s3["na_rpb"],
             "ssm_conv_w": lax.dynamic_slice_in_dim(s3["conv_w"], chip * CW, CW, axis=2), "ssm_conv_b": s3["conv_b"],
             "ssm_dt_bias": s3["dt_bias"], "ssm_a_log": s3["a_log"], "ssm_d": s3["ssm_d"], "ssm_norm_g": s3["norm_g"],
             "w_out": gsh["w_out"], "g_ffn": s3["g_ffn"], "w_ffn_in": gsh["w_ffn_in"], "w_ffn_out": gsh["w_ffn_out"], "g_final": s3["g_final"]}
    wts = {"c_ctx": c_ctx, "w_mod": w_mod, "b_mod": b_mod, "g_mix": g_mix, "w_in": w_in, "wa_sink": wa_sink, "na_rpb": na_rpb,
           "ssm_conv_w": ssm_conv_w, "ssm_conv_b": ssm_conv_b, "ssm_dt_bias": ssm_dt_bias, "ssm_a_log": ssm_a_log, "ssm_d": ssm_d,
           "ssm_norm_g": ssm_norm_g, "w_out": w_out, "g_ffn": g_ffn, "w_ffn_in": w_ffn_in, "w_ffn_out": w_ffn_out, "g_final": g_final}
    ms = {"c_ctx": m_c_ctx, "w_mod": m_w_mod, "b_mod": m_b_mod, "g_mix": m_g_mix, "w_in": m_w_in, "wa_sink": m_wa_sink, "na_rpb": m_na_rpb,
          "ssm_conv_w": m_ssm_conv_w, "ssm_conv_b": m_ssm_conv_b, "ssm_dt_bias": m_ssm_dt_bias, "ssm_a_log": m_ssm_a_log, "ssm_d": m_ssm_d,
          "ssm_norm_g": m_ssm_norm_g, "w_out": m_w_out, "g_ffn": m_g_ffn, "w_ffn_in": m_w_ffn_in, "w_ffn_out": m_w_ffn_out, "g_final": m_g_final}
    vs = {"c_ctx": v_c_ctx, "w_mod": v_w_mod, "b_mod": v_b_mod, "g_mix": v_g_mix, "w_in": v_w_in, "wa_sink": v_wa_sink, "na_rpb": v_na_rpb,
          "ssm_conv_w": v_ssm_conv_w, "ssm_conv_b": v_ssm_conv_b, "ssm_dt_bias": v_ssm_dt_bias, "ssm_a_log": v_ssm_a_log, "ssm_d": v_ssm_d,
          "ssm_norm_g": v_ssm_norm_g, "w_out": v_w_out, "g_ffn": v_g_ffn, "w_ffn_in": v_w_ffn_in, "w_ffn_out": v_w_ffn_out, "g_final": v_g_final}
    names = list(wts)
    grads = {n: grads[n].reshape(wts[n].shape).astype(F32) for n in names}
    big = ("w_mod", "w_in", "w_out", "w_ffn_in", "w_ffn_out")
    delta, new_m, new_v = {}, {}, {}
    for n in big:
        shp = wts[n].shape
        two = lambda a: a.reshape(shp[0] * shp[1], shp[2])
        d_, m_, v_ = adamw(two(wts[n]), two(grads[n]), two(ms[n]), two(vs[n]), f"adamw_{n}")
        delta[n], new_m[n], new_v[n] = d_.reshape(shp), m_.reshape(shp), v_.reshape(shp)
    packs = []
    for src in (wts, grads, ms, vs):
        f = _Flat()
        for n in names:
            if n not in big:
                f.add(n, src[n])
        packs.append(f)
    d_, m_, v_ = adamw(*[f.rows() for f in packs], "adamw_small")
    for dst, rows in ((delta, d_), (new_m, m_), (new_v, v_)):
        dst.update(packs[0].split(rows))

    return (loss, grad_x[:L][None], *[grads[n] for n in names], *[delta[n] for n in names],
            *[new_m[n] for n in names], *[new_v[n] for n in names])
```

```python
import functools

import numpy as np
import jax
import jax.numpy as jnp
from jax import lax
from jax.experimental import pallas as pl
from jax.experimental.pallas import tpu as pltpu
from jax.experimental.pallas import tpu_sc as plsc

F32 = jnp.float32
BF16 = jnp.bfloat16
_MXU = jnp.bfloat16
_HI = lax.Precision.HIGHEST
MESH = pl.DeviceIdType.MESH

D = 1024
HD = 64
GRID_W = 64
EPS = 1e-6
ROPE_BASE = 10000.0
WA_HEADS, WA_KV = 4, 2
WA_BLK = 128
NA_HEADS, NA_KH, NA_KW = 4, 8, 16
S_HEADS, S_P, S_INNER, S_GROUPS, S_N, S_CONV, S_Q = 8, 64, 512, 2, 128, 7, 128
D_FF = 2816
IN_COLS = 2832
IN_PAD = 2944
C_QA, C_QB, C_Z, C_KA, C_VA, C_KB, C_VB, C_XBC, C_DT = 0, 256, 512, 1024, 1152, 1280, 1536, 1792, 2816
ADAM_LR, ADAM_B1, ADAM_B2, ADAM_EPS, ADAM_WD, ADAM_STEP = 0.001, 0.9, 0.999, 1e-08, 0.01, 10

TR = 256
NEG = -1e30
VMEM_CAP = 56 * 1024 * 1024


PIN_BYTES = 256 * 1024


def _is_big(a):
    return hasattr(a, "shape") and len(a.shape) >= 2 and int(np.prod(a.shape)) * jnp.dtype(a.dtype).itemsize >= PIN_BYTES


def _pc(body, *, out_shape, pin=True, **kw):
    if not pin:
        return pl.pallas_call(body, out_shape=out_shape, **kw)
    one = isinstance(out_shape, jax.ShapeDtypeStruct)
    outs = [pltpu.HBM(s.shape, s.dtype) if _is_big(s) else s for s in ([out_shape] if one else out_shape)]
    call = pl.pallas_call(body, out_shape=outs[0] if one else outs, **kw)
    return lambda *args: call(*[pltpu.with_memory_space_constraint(a, pltpu.HBM) if _is_big(a) else a for a in args])


def _cp(sem=None, vmem=None):
    kw = {}
    if sem is not None:
        kw["dimension_semantics"] = sem
    if vmem is not None:
        kw["vmem_limit_bytes"] = int(min(max(vmem, 16 * 1024 * 1024), VMEM_CAP))
    return pltpu.CompilerParams(**kw)


def _sds(shape, dtype):
    return jax.ShapeDtypeStruct(tuple(shape), dtype)


_DIMS = {"nn": ((1,), (0,)), "nt": ((1,), (1,)), "tn": ((0,), (0,))}


def _dg(a, b, dims):
    return lax.dot_general(a.astype(_MXU), b.astype(_MXU), (dims, ((), ())), preferred_element_type=F32)


@functools.partial(jax.custom_vjp, nondiff_argnums=(2,))
def bdot(a, b, mode):
    return _dg(a, b, _DIMS[mode])


def _bdot_fwd(a, b, mode):
    return bdot(a, b, mode), (a, b)


def _bdot_bwd(mode, res, g):
    a, b = res
    if mode == "nn":
        return bdot(g, b, "nt"), bdot(a, g, "tn")
    if mode == "nt":
        return bdot(g, b, "nn"), bdot(g, a, "tn")
    return bdot(b, g, "nt"), bdot(a, g, "nn")


bdot.defvjp(_bdot_fwd, _bdot_bwd)


def hdot(a, b, mode="nn"):
    return lax.dot_general(a, b, (_DIMS[mode], ((), ())), precision=_HI, preferred_element_type=F32)


def _silu(x):
    return x / (1.0 + jnp.exp(-x))


def _softplus(x):
    return jnp.maximum(x, 0.0) + jnp.log(1.0 + jnp.exp(-jnp.abs(x)))


def _div_tile(n, cap, mult):
    if n <= cap:
        return n
    best = None
    for t in range(mult, cap + 1, mult):
        if n % t == 0:
            best = t
    assert best is not None, (n, cap, mult)
    return best


def matmul(a, b, mode, out_dtype, name, tm=640, tn=1536, tk=1408, hi=False):
    if mode == "tn":
        K, M = a.shape
    else:
        M, K = a.shape
    N = b.shape[0] if mode == "nt" else b.shape[1]
    tm = _div_tile(M, tm, 128 if mode == "tn" else 16)
    tn = _div_tile(N, tn, 128)
    tk = _div_tile(K, tk, 128 if mode != "tn" else 16)
    nk = K // tk
    dims = _DIMS[mode]

    def body(a_ref, b_ref, o_ref, *acc):
        if hi:
            part = lax.dot_general(a_ref[...], b_ref[...], (dims, ((), ())), precision=_HI, preferred_element_type=F32)
        else:
            part = _dg(a_ref[...], b_ref[...], dims)
        if nk == 1:
            o_ref[...] = part.astype(o_ref.dtype)
        else:
            k = pl.program_id(2)

            @pl.when(k == 0)
            def _():
                acc[0][...] = part

            @pl.when(k > 0)
            def _():
                acc[0][...] += part

            @pl.when(k == nk - 1)
            def _():
                o_ref[...] = acc[0][...].astype(o_ref.dtype)

    if mode == "tn":
        a_spec = pl.BlockSpec((tk, tm), lambda i, j, k: (k, i))
    else:
        a_spec = pl.BlockSpec((tm, tk), lambda i, j, k: (i, k))
    if mode == "nt":
        b_spec = pl.BlockSpec((tn, tk), lambda i, j, k: (j, k))
    else:
        b_spec = pl.BlockSpec((tk, tn), lambda i, j, k: (k, j))
    isz = lambda x: jnp.dtype(x.dtype).itemsize
    vmem = 2 * (tm * tk * isz(a) + tk * tn * isz(b) + tm * tn * jnp.dtype(out_dtype).itemsize) + 3 * tm * tn * 4
    return _pc(
        body, name=name, grid=(M // tm, N // tn, nk),
        in_specs=[a_spec, b_spec], out_specs=pl.BlockSpec((tm, tn), lambda i, j, k: (i, j)),
        out_shape=_sds((M, N), out_dtype),
        scratch_shapes=[pltpu.VMEM((tm, tn), F32)] if nk > 1 else [],
        compiler_params=_cp(("parallel", "parallel", "arbitrary"), vmem + (8 << 20)),
    )(a, b)


def _norm_mod(xo, shift, scale, g):
    r = lax.rsqrt(jnp.mean(xo * xo, axis=-1, keepdims=True) + EPS)
    return (xo * r) * g * (1.0 + scale) + shift


def res_norm_mod(x, y, gsv, g, nL, name):
    T = x.shape[0]
    has_y = y is not None

    def body(*refs):
        if has_y:
            x_ref, y_ref, gsv_ref, g_ref, xo_ref, h_ref = refs
            xo = x_ref[...] + gsv_ref[0, 0:1, :] * y_ref[...]
            xo_ref[...] = xo
        else:
            x_ref, gsv_ref, g_ref, h_ref = refs
            xo = x_ref[...]
        h_ref[...] = _norm_mod(xo, gsv_ref[0, 1:2, :], gsv_ref[0, 2:3, :], g_ref[...]).astype(h_ref.dtype)

    row = pl.BlockSpec((TR, D), lambda i: (i, 0))
    in_specs = [row] + ([row] if has_y else []) + [pl.BlockSpec((1, 8, D), lambda i: (i // nL, 0, 0)),
                                                     pl.BlockSpec((1, D), lambda i: (0, 0))]
    out_specs = ([row] if has_y else []) + [row]
    out_shape = ([_sds((T, D), F32)] if has_y else []) + [_sds((T, D), BF16)]
    args = (x, y, gsv, g) if has_y else (x, gsv, g)
    outs = _pc(body, name=name, grid=(T // TR,), in_specs=in_specs, out_specs=out_specs, out_shape=out_shape,
               compiler_params=_cp(("arbitrary",), 24 << 20))(*args)
    return (outs[0], outs[1]) if has_y else (None, outs[0])


def res_norm_mod_bwd(xo, y, gsv, g, dh, dres, nL, name):
    T = xo.shape[0]
    has_y = y is not None

    def body(*refs):
        if has_y:
            xo_ref, y_ref, gsv_ref, g_ref, dh_ref, dres_ref, dx_ref, dy_ref, dgsv_ref, dg_ref = refs
        else:
            xo_ref, gsv_ref, g_ref, dh_ref, dres_ref, dx_ref, dgsv_ref, dg_ref = refs
        i = pl.program_id(0)

        @pl.when((i == 0) | (i == nL))
        def _():
            dgsv_ref[...] = jnp.zeros_like(dgsv_ref)

        @pl.when(i == 0)
        def _():
            dg_ref[...] = jnp.zeros_like(dg_ref)

        _, vjp = jax.vjp(_norm_mod, xo_ref[...], gsv_ref[0, 1:2, :], gsv_ref[0, 2:3, :], g_ref[...])
        dxn, dshift, dscale, dg = vjp(dh_ref[...].astype(F32))
        dxo = dres_ref[...] + dxn
        dx_ref[...] = dxo
        if has_y:
            dy_ref[...] = (gsv_ref[0, 0:1, :] * dxo).astype(dy_ref.dtype)
            dgsv_ref[0, 0:1, :] += jnp.sum(y_ref[...] * dxo, axis=0, keepdims=True)
        dgsv_ref[0, 1:2, :] += dshift
        dgsv_ref[0, 2:3, :] += dscale
        dg_ref[0:1, :] += dg

    row = pl.BlockSpec((TR, D), lambda i: (i, 0))
    gspec = pl.BlockSpec((1, 8, D), lambda i: (i // nL, 0, 0))
    in_specs = [row] + ([row] if has_y else []) + [gspec, pl.BlockSpec((1, D), lambda i: (0, 0)), row, row]
    out_specs = [row] + ([row] if has_y else []) + [gspec, pl.BlockSpec((8, D), lambda i: (0, 0))]
    out_shape = [_sds((T, D), F32)] + ([_sds((T, D), BF16)] if has_y else []) + [_sds((2, 8, D), F32), _sds((8, D), F32)]
    args = (xo, y, gsv, g, dh, dres) if has_y else (xo, gsv, g, dh, dres)
    outs = _pc(body, name=name, grid=(T // TR,), in_specs=in_specs, out_specs=out_specs, out_shape=out_shape,
               compiler_params=_cp(("arbitrary",), 32 << 20))(*args)
    if has_y:
        return outs
    return outs[0], None, outs[1], outs[2]


def final_loss(x, y, gsv, g, target, nL, name):
    T = x.shape[0]

    def lossf(xo, gv, t):
        yn = (xo * lax.rsqrt(jnp.mean(xo * xo, axis=-1, keepdims=True) + EPS)) * gv
        e = yn - t
        return 0.5 * jnp.sum(jnp.sum(e * e, axis=-1, keepdims=True) * (1.0 / D), axis=0, keepdims=True)

    def body(x_ref, y_ref, gsv_ref, g_ref, t_ref, loss_ref, dx_ref, dy_ref, dgsv_ref, dg_ref):
        i = pl.program_id(0)

        @pl.when(i == 0)
        def _():
            loss_ref[...] = jnp.zeros_like(loss_ref)
            dg_ref[...] = jnp.zeros_like(dg_ref)

        @pl.when((i == 0) | (i == nL))
        def _():
            dgsv_ref[...] = jnp.zeros_like(dgsv_ref)

        @pl.when(i < nL)
        def _():
            gate = gsv_ref[0, 0:1, :]
            yv = y_ref[...]
            xo = x_ref[...] + gate * yv
            lv, vjp = jax.vjp(lossf, xo, g_ref[...], t_ref[...])
            dxo, dg, _ = vjp(jnp.ones((1, 1), F32))
            loss_ref[...] += jnp.broadcast_to(lv, loss_ref.shape)
            dx_ref[...] = dxo
            dy_ref[...] = (gate * dxo).astype(dy_ref.dtype)
            dgsv_ref[0, 0:1, :] += jnp.sum(yv * dxo, axis=0, keepdims=True)
            dg_ref[0:1, :] += dg

        @pl.when(i >= nL)
        def _():
            dx_ref[...] = jnp.zeros_like(dx_ref)
            dy_ref[...] = jnp.zeros_like(dy_ref)

    row = pl.BlockSpec((TR, D), lambda i: (i, 0))
    gspec = pl.BlockSpec((1, 8, D), lambda i: (i // nL, 0, 0))
    return _pc(
        body, name=name, grid=(T // TR,),
        in_specs=[row, row, gspec, pl.BlockSpec((1, D), lambda i: (0, 0)),
                  pl.BlockSpec((TR, D), lambda i: (jnp.minimum(i, nL - 1), 0))],
        out_specs=[pl.BlockSpec((8, 128), lambda i: (0, 0)), row, row, gspec, pl.BlockSpec((8, D), lambda i: (0, 0))],
        out_shape=[_sds((8, 128), F32), _sds((T, D), F32), _sds((T, D), BF16), _sds((2, 8, D), F32), _sds((8, D), F32)],
        compiler_params=_cp(("arbitrary",), 32 << 20),
    )(x, y, gsv, g, target)


FI_BLK = 2 * D_FF // 4


def _fi_chip(j):
    return (j % 2) * 2 + j // 2


def matmul_fi(a, b, mode, out_dtype, name, tm=640, tk=1088):
    T = a.shape[0]
    if mode == "tn":
        tmd = 512
        tk = _div_tile(T, tk, 16)
        nk = T // tk

        def body(a_ref, b_ref, o_ref, acc):
            k = pl.program_id(2)
            part = _dg(a_ref[...], b_ref[...], _DIMS["tn"])

            @pl.when(k == 0)
            def _():
                acc[...] = part

            @pl.when(k > 0)
            def _():
                acc[...] += part

            @pl.when(k == nk - 1)
            def _():
                o_ref[0] = acc[...].astype(o_ref.dtype)

        return _pc(body, name=name, grid=(D // tmd, 4, nk),
                   in_specs=[pl.BlockSpec((tk, tmd), lambda i, j, k: (k, i)), pl.BlockSpec((tk, FI_BLK), lambda i, j, k: (k, j))],
                   out_specs=pl.BlockSpec((1, tmd, FI_BLK), lambda i, j, k: (_fi_chip(j), i, 0)),
                   out_shape=_sds((4, D, FI_BLK), out_dtype), scratch_shapes=[pltpu.VMEM((tmd, FI_BLK), F32)],
                   compiler_params=_cp(("parallel", "parallel", "arbitrary"), 40 << 20))(a, b)
    tm = _div_tile(T, tm, 16)
    if mode == "nn":
        def body(a_ref, b_ref, o_ref):
            o_ref[...] = _dg(a_ref[...], b_ref[0], _DIMS["nn"]).astype(o_ref.dtype)

        return _pc(body, name=name, grid=(T // tm, 4),
                   in_specs=[pl.BlockSpec((tm, D), lambda i, j: (i, 0)), pl.BlockSpec((1, D, FI_BLK), lambda i, j: (_fi_chip(j), 0, 0))],
                   out_specs=pl.BlockSpec((tm, FI_BLK), lambda i, j: (i, j)), out_shape=_sds((T, 4 * FI_BLK), out_dtype),
                   compiler_params=_cp(("parallel", "arbitrary"), 32 << 20))(a, b)

    def body(a_ref, b_ref, o_ref, acc):
        k = pl.program_id(1)
        part = _dg(a_ref[...], b_ref[0], _DIMS["nt"])

        @pl.when(k == 0)
        def _():
            acc[...] = part

        @pl.when(k > 0)
        def _():
            acc[...] += part

        @pl.when(k == 3)
        def _():
            o_ref[...] = acc[...].astype(o_ref.dtype)

    return _pc(body, name=name, grid=(T // tm, 4),
               in_specs=[pl.BlockSpec((tm, FI_BLK), lambda i, k: (i, k)), pl.BlockSpec((1, D, FI_BLK), lambda i, k: (_fi_chip(k), 0, 0))],
               out_specs=pl.BlockSpec((tm, D), lambda i, k: (i, 0)), out_shape=_sds((T, D), out_dtype),
               scratch_shapes=[pltpu.VMEM((tm, D), F32)], compiler_params=_cp(("parallel", "arbitrary"), 32 << 20))(a, b)


def _swiglu(gate, up):
    return _silu(gate) * up


def swiglu_fwd(gu, name):
    T = gu.shape[0]

    def body(x_ref, o_ref):
        o_ref[...] = _swiglu(x_ref[:, :FI_BLK].astype(F32), x_ref[:, FI_BLK:].astype(F32)).astype(o_ref.dtype)

    return _pc(body, name=name, grid=(T // TR, 2), in_specs=[pl.BlockSpec((TR, 2 * FI_BLK), lambda i, j: (i, j))],
               out_specs=pl.BlockSpec((TR, FI_BLK), lambda i, j: (i, j)), out_shape=_sds((T, D_FF), BF16),
               compiler_params=_cp(("parallel", "parallel"), 24 << 20))(gu)


def swiglu_bwd(gu, dact, name):
    T = gu.shape[0]

    def body(x_ref, d_ref, o_ref):
        _, vjp = jax.vjp(_swiglu, x_ref[:, :FI_BLK].astype(F32), x_ref[:, FI_BLK:].astype(F32))
        dg, du = vjp(d_ref[...].astype(F32))
        o_ref[:, :FI_BLK] = dg.astype(o_ref.dtype)
        o_ref[:, FI_BLK:] = du.astype(o_ref.dtype)

    return _pc(body, name=name, grid=(T // TR, 2),
               in_specs=[pl.BlockSpec((TR, 2 * FI_BLK), lambda i, j: (i, j)), pl.BlockSpec((TR, FI_BLK), lambda i, j: (i, j))],
               out_specs=pl.BlockSpec((TR, 2 * FI_BLK), lambda i, j: (i, j)), out_shape=_sds((T, 2 * D_FF), BF16),
               compiler_params=_cp(("parallel", "parallel"), 32 << 20))(gu, dact)


def rope_tables(L, Lc):
    t = np.arange(L)
    rows, cols = t // GRID_W, t % GRID_W
    inv = ROPE_BASE ** (-np.arange(16, dtype=np.float32) / 16)
    lane = np.arange(64)
    pos = np.where((lane // 32)[None, :] == 0, rows[:, None], cols[:, None]).astype(np.float32)
    ang = jnp.asarray(pos) * jnp.asarray(inv[lane % 16])[None, :]
    cos = jnp.concatenate([jnp.cos(ang), jnp.ones((Lc, 64), F32)], axis=0)
    sin = jnp.concatenate([jnp.sin(ang), jnp.zeros((Lc, 64), F32)], axis=0)
    R = np.zeros((128, 128), np.float32)
    for i in range(128):
        if (i % 32) < 16:
            R[i + 16, i] = -1.0
        else:
            R[i - 16, i] = 1.0
    return jnp.tile(cos, (1, 2)), jnp.tile(sin, (1, 2)), jnp.asarray(R)


def rope_apply(q_src, q_col, k_src, k_col, cos, sin, R, transpose, name, kv_src=None):
    T = cos.shape[0]
    with_kv = kv_src is not None

    def rot(x, c, s, Rm):
        if transpose:
            return x * c + hdot(x * s, Rm, "nt")
        return x * c + hdot(x, Rm) * s

    def body(q_ref, k_ref, c_ref, s_ref, R_ref, *rest):
        qo_ref, ko_ref = rest[-4:-2] if with_kv else rest
        c, s, Rm = c_ref[...], s_ref[...], R_ref[...]
        for j in range(2):
            qo_ref[:, j * 128:(j + 1) * 128] = rot(q_ref[:, j * 128:(j + 1) * 128].astype(F32), c, s, Rm).astype(qo_ref.dtype)
        ko_ref[...] = rot(k_ref[...].astype(F32), c, s, Rm).astype(ko_ref.dtype)
        if with_kv:
            rest[-2][...] = rest[0][...].astype(BF16)
            rest[-1][...] = rest[1][...].astype(BF16)

    tab = pl.BlockSpec((TR, 128), lambda i: (i, 0))
    wide = pl.BlockSpec((TR, 256), lambda i: (i, 0))
    kv_in = [pl.BlockSpec((TR, 256), lambda i: (i, C_KB // 256)), pl.BlockSpec((TR, 256), lambda i: (i, C_VB // 256))] if with_kv else []
    return _pc(body, name=name, grid=(T // TR,),
               in_specs=[pl.BlockSpec((TR, 256), lambda i: (i, q_col)), pl.BlockSpec((TR, 128), lambda i: (i, k_col)),
                         tab, tab, pl.BlockSpec((128, 128), lambda i: (0, 0))] + kv_in,
               out_specs=[wide, tab] + ([wide, wide] if with_kv else []),
               out_shape=[_sds((T, 256), BF16), _sds((T, 128), BF16)] + ([_sds((T, 256), BF16)] * 2 if with_kv else []),
               compiler_params=_cp(("parallel",), 16 << 20))(q_src, k_src, cos, sin, R, *([kv_src, kv_src] if with_kv else []))


_SCALE = HD ** -0.5


def _attn_tile(qh, ks, vs, extra):
    ss = []
    for k, add in ks:
        s = bdot(qh, k, "nt") * _SCALE
        ss.append(s if add is None else s + add)
    m = ss[0].max(axis=-1, keepdims=True)
    for s in ss[1:]:
        m = jnp.maximum(m, s.max(axis=-1, keepdims=True))
    if extra is not None:
        m = jnp.maximum(m, extra)
    ps = [jnp.exp(s - m) for s in ss]
    den = ps[0].sum(axis=-1, keepdims=True)
    for p in ps[1:]:
        den = den + p.sum(axis=-1, keepdims=True)
    if extra is not None:
        den = den + jnp.exp(extra - m)
    num = bdot(ps[0], vs[0], "nn")
    for p, v in zip(ps[1:], vs[1:]):
        num = num + bdot(p, v, "nn")
    return num / den


def _wa_mask(n, L):
    qpos = n * WA_BLK + lax.broadcasted_iota(jnp.int32, (WA_BLK, 3 * WA_BLK), 0)
    kpos = (n - 1) * WA_BLK + lax.broadcasted_iota(jnp.int32, (WA_BLK, 3 * WA_BLK), 1)
    ok = (jnp.abs(qpos - kpos) <= WA_BLK) & (kpos >= 0) & (kpos < L)
    return jnp.where(ok, 0.0, NEG).astype(F32)


def _wa_specs(L, Lc):
    nb = L // WA_BLK
    cb = L // Lc
    lat = lambda n: jnp.minimum(n, nb - 1)
    prv = lambda n: jnp.clip(n - 1, 0, nb - 1)
    nxt = lambda n: jnp.minimum(n + 1, nb - 1)
    kspecs = [pl.BlockSpec((WA_BLK, 128), lambda n: (prv(n), 0)), pl.BlockSpec((WA_BLK, 128), lambda n: (lat(n), 0)),
              pl.BlockSpec((WA_BLK, 128), lambda n: (nxt(n), 0)), pl.BlockSpec((Lc, 128), lambda n: (cb, 0))]
    vcol = C_VA // 128
    vspecs = [pl.BlockSpec((WA_BLK, 128), lambda n: (prv(n), vcol)), pl.BlockSpec((WA_BLK, 128), lambda n: (lat(n), vcol)),
              pl.BlockSpec((WA_BLK, 128), lambda n: (nxt(n), vcol)), pl.BlockSpec((Lc, 128), lambda n: (cb, vcol))]
    return nb, kspecs, vspecs


def win_attn_fwd(qr, kr, P, sink, L, Lc, name):
    T = L + Lc
    nb, kspecs, vspecs = _wa_specs(L, Lc)

    def body(q_ref, kp, kc, kn, kx, vp, vc, vn, vx, s_ref, o_ref):
        n = pl.program_id(0)

        @pl.when(n < nb)
        def _():
            mask = _wa_mask(n, L)
            for g in range(WA_KV):
                sl = slice(g * HD, (g + 1) * HD)
                k3 = jnp.concatenate([kp[:, sl], kc[:, sl], kn[:, sl]], axis=0)
                v3 = jnp.concatenate([vp[:, sl], vc[:, sl], vn[:, sl]], axis=0)
                for r in range(2):
                    h = 2 * g + r
                    o = _attn_tile(q_ref[:, h * HD:(h + 1) * HD], [(k3, mask), (kx[:, sl], None)], [v3, vx[:, sl]],
                                   s_ref[h:h + 1, 0:1])
                    o_ref[:, h * HD:(h + 1) * HD] = o.astype(o_ref.dtype)

        @pl.when(n >= nb)
        def _():
            for h in range(WA_HEADS):
                sl = slice((h // 2) * HD, (h // 2 + 1) * HD)
                o = _attn_tile(q_ref[:, h * HD:(h + 1) * HD], [(kx[:, sl], None)], [vx[:, sl]], s_ref[h:h + 1, 0:1])
                o_ref[:, h * HD:(h + 1) * HD] = o.astype(o_ref.dtype)

    qspec = pl.BlockSpec((WA_BLK, 256), lambda n: (n, 0))
    return _pc(body, name=name, grid=(T // WA_BLK,),
               in_specs=[qspec] + kspecs + vspecs + [pl.BlockSpec((8, 128), lambda n: (0, 0))],
               out_specs=qspec, out_shape=_sds((T, 256), BF16),
               compiler_params=_cp(("arbitrary",), 32 << 20))(qr, kr, kr, kr, kr, P, P, P, P, sink)


def win_attn_bwd(qr, kr, P, sink, do_src, L, Lc, name):
    T = L + Lc
    nb, kspecs, vspecs = _wa_specs(L, Lc)
    cx = WA_BLK + L

    def body(q_ref, kp, kc, kn, kx, vp, vc, vn, vx, s_ref, do_ref, dq_ref, dk_ref, dv_ref, ds_ref):
        n = pl.program_id(0)

        @pl.when(n == 0)
        def _():
            dk_ref[...] = jnp.zeros_like(dk_ref)
            dv_ref[...] = jnp.zeros_like(dv_ref)
            ds_ref[...] = jnp.zeros_like(ds_ref)

        @pl.when(n < nb)
        def _():
            mask = _wa_mask(n, L)
            rows = pl.ds(pl.multiple_of(n * WA_BLK, WA_BLK), 3 * WA_BLK)
            for g in range(WA_KV):
                sl = slice(g * HD, (g + 1) * HD)
                k3 = jnp.concatenate([kp[:, sl], kc[:, sl], kn[:, sl]], axis=0)
                v3 = jnp.concatenate([vp[:, sl], vc[:, sl], vn[:, sl]], axis=0)
                kxg, vxg = kx[:, sl], vx[:, sl]
                acc = None
                for r in range(2):
                    h = 2 * g + r
                    hs = slice(h * HD, (h + 1) * HD)
                    f = lambda q, k3_, v3_, kx_, vx_, s_: _attn_tile(q, [(k3_, mask), (kx_, None)], [v3_, vx_], s_)
                    _, vjp = jax.vjp(f, q_ref[:, hs].astype(F32), k3.astype(F32), v3.astype(F32), kxg.astype(F32),
                                     vxg.astype(F32), s_ref[h:h + 1, 0:1])
                    dq, dk3, dv3, dkx, dvx, dsk = vjp(do_ref[:, hs].astype(F32))
                    dq_ref[:, hs] = dq
                    ds_ref[h:h + 1, :] += jnp.broadcast_to(dsk, (1, 128))
                    acc = (dk3, dv3, dkx, dvx) if acc is None else tuple(a + b for a, b in zip(acc, (dk3, dv3, dkx, dvx)))
                dk_ref[rows, sl] += acc[0]
                dv_ref[rows, sl] += acc[1]
                dk_ref[cx:cx + Lc, sl] += acc[2]
                dv_ref[cx:cx + Lc, sl] += acc[3]

        @pl.when(n >= nb)
        def _():
            for h in range(WA_HEADS):
                sl = slice((h // 2) * HD, (h // 2 + 1) * HD)
                hs = slice(h * HD, (h + 1) * HD)
                f = lambda q, kx_, vx_, s_: _attn_tile(q, [(kx_, None)], [vx_], s_)
                _, vjp = jax.vjp(f, q_ref[:, hs].astype(F32), kx[:, sl].astype(F32), vx[:, sl].astype(F32), s_ref[h:h + 1, 0:1])
                dq, dkx, dvx, dsk = vjp(do_ref[:, hs].astype(F32))
                dq_ref[:, hs] = dq
                ds_ref[h:h + 1, :] += jnp.broadcast_to(dsk, (1, 128))
                dk_ref[cx:cx + Lc, sl] += dkx
                dv_ref[cx:cx + Lc, sl] += dvx

    qspec = pl.BlockSpec((WA_BLK, 256), lambda n: (n, 0))
    acc_spec = pl.BlockSpec((T + 2 * WA_BLK, 128), lambda n: (0, 0))
    return _pc(body, name=name, grid=(T // WA_BLK,),
               in_specs=[qspec] + kspecs + vspecs + [pl.BlockSpec((8, 128), lambda n: (0, 0)), qspec],
               out_specs=[qspec, acc_spec, acc_spec, pl.BlockSpec((8, 128), lambda n: (0, 0))],
               out_shape=[_sds((T, 256), F32), _sds((T + 2 * WA_BLK, 128), F32), _sds((T + 2 * WA_BLK, 128), F32), _sds((8, 128), F32)],
               compiler_params=_cp(("arbitrary",), 40 << 20))(qr, kr, kr, kr, kr, P, P, P, P, sink, do_src)


def na_index_tables():
    qc = np.arange(GRID_W)[:, None]
    kc = np.arange(GRID_W)[None, :]
    cstart = np.clip(qc - NA_KW // 2, 0, GRID_W - NA_KW)
    ok = (kc >= cstart) & (kc < cstart + NA_KW)
    dx = np.clip(kc - qc, -(NA_KW - 1), NA_KW - 1) + (NA_KW - 1)
    off = np.arange(NA_KH)[:, None]
    kr = np.arange(NA_KH)[None, :]
    dy = kr - off + (NA_KH - 1)
    return ok, dx, dy


def _na_selectors():
    ok, dx, dy = na_index_tables()
    e1 = np.zeros((GRID_W * GRID_W, 128), np.float32)
    qi, ki = np.nonzero(ok)
    e1[qi * GRID_W + ki, dx[qi, ki]] = 1.0
    e2 = np.zeros((16, NA_KH * NA_KH), np.float32)
    oi, ri = np.meshgrid(np.arange(NA_KH), np.arange(NA_KH), indexing="ij")
    e2[dy[oi, ri].ravel(), (oi * NA_KH + ri).ravel()] = 1.0
    return ok, jnp.asarray(e1), jnp.asarray(np.kron(np.eye(NA_HEADS, dtype=np.float32), e2))


def na_bias_table(rpb, tag):
    ok, e1, e2 = _na_selectors()
    r2 = jnp.pad(rpb.astype(F32), ((0, 0), (0, 1), (0, 128 - (2 * NA_KW - 1)))).reshape(NA_HEADS * 16, 128)
    r1 = matmul(e2, r2, "tn", F32, f"na_bias_sel1_{tag}", hi=True)
    x = matmul(r1, e1, "nt", F32, f"na_bias_sel2_{tag}", hi=True)
    b = x.reshape(NA_HEADS, NA_KH, NA_KH, GRID_W, GRID_W).transpose(0, 1, 3, 2, 4)
    b = b + jnp.asarray(np.where(ok, 0.0, NEG).astype(np.float32))[None, None, :, None, :]
    return b.reshape(NA_HEADS, NA_KH, GRID_W, NA_KH * GRID_W)


def _na_rows(r, GR):
    r0 = jnp.clip(r - NA_KH // 2, 0, GR - NA_KH)
    return r0, jnp.clip(r - r0, 0, NA_KH - 1)


NA_RPS = 2


def na_fwd(P, kb, vb, bias, L, Lc, name):
    T = L + Lc
    GR = L // GRID_W
    W = NA_KH * GRID_W
    QB = GRID_W * NA_RPS
    nlat = GR // NA_RPS

    def body(q_ref, k_ref, v_ref, b_ref, o_ref):
        s = pl.program_id(0)

        @pl.when(s < nlat)
        def _():
            for rr in range(NA_RPS):
                r0, off = _na_rows(s * NA_RPS + rr, GR)
                rows = pl.ds(pl.multiple_of(r0 * GRID_W, GRID_W), W)
                qs = slice(rr * GRID_W, (rr + 1) * GRID_W)
                for h in range(NA_HEADS):
                    hs = slice(h * HD, (h + 1) * HD)
                    o = _attn_tile(q_ref[qs, hs], [(k_ref[rows, hs], b_ref[h, off]), (k_ref[L:T, hs], None)],
                                   [v_ref[rows, hs], v_ref[L:T, hs]], None)
                    o_ref[qs, hs] = o.astype(o_ref.dtype)

        @pl.when(s >= nlat)
        def _():
            for h in range(NA_HEADS):
                hs = slice(h * HD, (h + 1) * HD)
                o = _attn_tile(q_ref[:, hs], [(k_ref[L:T, hs], None)], [v_ref[L:T, hs]], None)
                o_ref[:, hs] = o.astype(o_ref.dtype)

    one = pl.Buffered(1)
    return _pc(body, name=name, grid=(T // QB,),
               in_specs=[pl.BlockSpec((QB, 256), lambda r: (r, C_QB // 256)),
                         pl.BlockSpec((T, 256), lambda r: (0, 0), pipeline_mode=one),
                         pl.BlockSpec((T, 256), lambda r: (0, 0), pipeline_mode=one),
                         pl.BlockSpec((NA_HEADS, NA_KH, GRID_W, W), lambda r: (0, 0, 0, 0), pipeline_mode=one)],
               out_specs=pl.BlockSpec((QB, 256), lambda r: (r, 0)), out_shape=_sds((T, 256), BF16),
               compiler_params=_cp(("arbitrary",), 32 << 20))(P, kb, vb, bias)


def na_bwd(P, kb, vb, bias, do_src, L, Lc, name):
    T = L + Lc
    GR = L // GRID_W
    W = NA_KH * GRID_W
    QB = GRID_W * NA_RPS
    nlat = GR // NA_RPS

    def body(q_ref, k_ref, v_ref, b_ref, do_ref, dq_ref, dk_ref, dv_ref, db_ref):
        s = pl.program_id(0)

        @pl.when(s == 0)
        def _():
            dk_ref[...] = jnp.zeros_like(dk_ref)
            dv_ref[...] = jnp.zeros_like(dv_ref)
            db_ref[...] = jnp.zeros_like(db_ref)

        @pl.when(s < nlat)
        def _():
            for rr in range(NA_RPS):
                r0, off = _na_rows(s * NA_RPS + rr, GR)
                rows = pl.ds(pl.multiple_of(r0 * GRID_W, GRID_W), W)
                qs = slice(rr * GRID_W, (rr + 1) * GRID_W)
                for h in range(NA_HEADS):
                    hs = slice(h * HD, (h + 1) * HD)
                    f = lambda q, kw, vw, kx, vx, b: _attn_tile(q, [(kw, b), (kx, None)], [vw, vx], None)
                    _, vjp = jax.vjp(f, q_ref[qs, hs].astype(F32), k_ref[rows, hs].astype(F32), v_ref[rows, hs].astype(F32),
                                     k_ref[L:T, hs].astype(F32), v_ref[L:T, hs].astype(F32), b_ref[h, off])
                    dq, dkw, dvw, dkx, dvx, db = vjp(do_ref[qs, hs].astype(F32))
                    dq_ref[qs, hs] = dq.astype(dq_ref.dtype)
                    dk_ref[rows, hs] += dkw
                    dv_ref[rows, hs] += dvw
                    dk_ref[L:T, hs] += dkx
                    dv_ref[L:T, hs] += dvx
                    db_ref[h, off] += db

        @pl.when(s >= nlat)
        def _():
            for h in range(NA_HEADS):
                hs = slice(h * HD, (h + 1) * HD)
                f = lambda q, kx, vx: _attn_tile(q, [(kx, None)], [vx], None)
                _, vjp = jax.vjp(f, q_ref[:, hs].astype(F32), k_ref[L:T, hs].astype(F32), v_ref[L:T, hs].astype(F32))
                dq, dkx, dvx = vjp(do_ref[:, hs].astype(F32))
                dq_ref[:, hs] = dq.astype(dq_ref.dtype)
                dk_ref[L:T, hs] += dkx
                dv_ref[L:T, hs] += dvx

    one = pl.Buffered(1)
    full = lambda shape: pl.BlockSpec(shape, lambda r: (0,) * len(shape), pipeline_mode=one)
    return _pc(body, name=name, grid=(T // QB,),
               in_specs=[pl.BlockSpec((QB, 256), lambda r: (r, C_QB // 256)), full((T, 256)), full((T, 256)),
                         full((NA_HEADS, NA_KH, GRID_W, W)), pl.BlockSpec((QB, 256), lambda r: (r, 1))],
               out_specs=[pl.BlockSpec((QB, 256), lambda r: (r, 0)), full((T, 256)), full((T, 256)),
                          full((NA_HEADS, NA_KH, GRID_W, W))],
               out_shape=[_sds((T, 256), BF16), _sds((T, 256), F32), _sds((T, 256), F32), _sds((NA_HEADS, NA_KH, GRID_W, W), F32)],
               compiler_params=_cp(("arbitrary",), 48 << 20))(P, kb, vb, bias, do_src)


def na_rpb_grad(dbias, tag):
    _, e1, e2 = _na_selectors()
    x = dbias.reshape(NA_HEADS, NA_KH, GRID_W, NA_KH, GRID_W).transpose(0, 1, 3, 2, 4).reshape(NA_HEADS * NA_KH * NA_KH, GRID_W * GRID_W)
    r1 = matmul(x, e1, "nn", F32, f"na_rpb_sel1_{tag}", hi=True, tk=1024)
    r2 = matmul(e2, r1, "nn", F32, f"na_rpb_sel2_{tag}", hi=True)
    return r2.reshape(NA_HEADS, 16, 128)[:, :2 * NA_KH - 1, :2 * NA_KW - 1]


_HALO = 8


def _halo_specs(T, col0):
    nh = TR // _HALO
    cur = pl.BlockSpec((TR, 256), lambda i, j: (i, col0 + j))
    prv = pl.BlockSpec((_HALO, 256), lambda i, j: (jnp.maximum(i * nh - 1, 0), col0 + j))
    nxt = pl.BlockSpec((_HALO, 256), lambda i, j: (jnp.minimum((i + 1) * nh, T // _HALO - 1), col0 + j))
    return prv, cur, nxt


def _fill_ext(ext, prv, cur, nxt, i, nL, nT):
    has_prev = jnp.where((i != 0) & (i != nL), 1.0, 0.0)
    has_next = jnp.where((i != nL - 1) & (i != nT - 1), 1.0, 0.0)
    ext[0:_HALO, :] = prv[...].astype(F32) * has_prev
    ext[_HALO:_HALO + TR, :] = cur[...].astype(F32)
    ext[_HALO + TR:, :] = nxt[...].astype(F32) * has_next


def conv_silu_fwd(P, w8, b, nL, name):
    T = P.shape[0]
    nT = T // TR

    def body(prv, cur, nxt, w_ref, b_ref, pre_ref, act_ref, ext):
        i = pl.program_id(0)
        _fill_ext(ext, prv, cur, nxt, i, nL, nT)
        y = jnp.broadcast_to(b_ref[...], (TR, 256))
        for k in range(S_CONV):
            y = y + w_ref[k:k + 1, :] * ext[pl.ds(_HALO - S_CONV // 2 + k, TR), :]
        pre_ref[...] = y
        act_ref[...] = _silu(y)

    prv, cur, nxt = _halo_specs(T, C_XBC // 256)
    out = pl.BlockSpec((TR, 256), lambda i, j: (i, j))
    return _pc(body, name=name, grid=(nT, 4),
               in_specs=[prv, cur, nxt, pl.BlockSpec((8, 256), lambda i, j: (0, j)), pl.BlockSpec((1, 256), lambda i, j: (0, j))],
               out_specs=[out, out], out_shape=[_sds((T, 1024), F32), _sds((T, 1024), F32)],
               scratch_shapes=[pltpu.VMEM((TR + 2 * _HALO, 256), F32)],
               compiler_params=_cp(("parallel", "parallel"), 16 << 20))(P, P, P, w8, b)


def dsilu(pre, dxs_list, db_list, dc_list, name):
    T = pre.shape[0]
    n1, n2, n3 = len(dxs_list), len(db_list), len(dc_list)

    def body(*refs):
        pre_ref = refs[0]
        ins = refs[1:1 + n1 + n2 + n3]
        out = refs[-1]

        def part(rs, lo, hi):
            g = rs[0][...].astype(F32)
            for r in rs[1:]:
                g = g + r[...].astype(F32)
            _, vjp = jax.vjp(_silu, pre_ref[:, lo:hi])
            out[:, lo:hi] = vjp(g)[0]

        part(ins[:n1], 0, 512)
        part(ins[n1:n1 + n2], 512, 768)
        part(ins[n1 + n2:], 768, 1024)

    spec = lambda w: pl.BlockSpec((TR, w), lambda i: (i, 0))
    return _pc(body, name=name, grid=(T // TR,),
               in_specs=[spec(1024)] + [spec(512)] * n1 + [spec(256)] * (n2 + n3),
               out_specs=spec(1024), out_shape=_sds((T, 1024), F32),
               compiler_params=_cp(("parallel",), 32 << 20))(pre, *dxs_list, *db_list, *dc_list)


def conv_bwd(dpre, P, w8, nL, name):
    T = P.shape[0]
    nT = T // TR

    def body(dp, dc, dn, xp, xc, xn, w_ref, dx_ref, dw_ref, db_ref, extd, extx):
        i = pl.program_id(1)
        _fill_ext(extd, dp, dc, dn, i, nL, nT)
        _fill_ext(extx, xp, xc, xn, i, nL, nT)

        @pl.when(i == 0)
        def _():
            dw_ref[...] = jnp.zeros_like(dw_ref)
            db_ref[...] = jnp.zeros_like(db_ref)

        d = dc[...]
        dx = jnp.zeros((TR, 256), F32)
        for k in range(S_CONV):
            dx = dx + w_ref[k:k + 1, :] * extd[pl.ds(_HALO + S_CONV // 2 - k, TR), :]
            dw_ref[k:k + 1, :] += jnp.sum(d * extx[pl.ds(_HALO - S_CONV // 2 + k, TR), :], axis=0, keepdims=True)
        dx_ref[...] = dx.astype(dx_ref.dtype)
        db_ref[0:1, :] += jnp.sum(d, axis=0, keepdims=True)

    def swap(spec):
        f = spec.index_map
        return pl.BlockSpec(spec.block_shape, lambda j, i: f(i, j))

    dprv, dcur, dnxt = [swap(s) for s in _halo_specs(T, 0)]
    xprv, xcur, xnxt = [swap(s) for s in _halo_specs(T, C_XBC // 256)]
    acc = pl.BlockSpec((8, 256), lambda j, i: (0, j))
    return _pc(body, name=name, grid=(4, nT),
               in_specs=[dprv, dcur, dnxt, xprv, xcur, xnxt, acc],
               out_specs=[pl.BlockSpec((TR, 256), lambda j, i: (i, j)), acc, acc],
               out_shape=[_sds((T, 1024), BF16), _sds((8, 1024), F32), _sds((8, 1024), F32)],
               scratch_shapes=[pltpu.VMEM((TR + 2 * _HALO, 256), F32), pltpu.VMEM((TR + 2 * _HALO, 256), F32)],
               compiler_params=_cp(("parallel", "arbitrary"), 16 << 20))(dpre, dpre, dpre, P, P, P, w8)


def _onehot_row(h, n):
    return (lax.broadcasted_iota(jnp.int32, (1, n), 1) == h).astype(F32)


def _onehot_col(h, n):
    return (lax.broadcasted_iota(jnp.int32, (n, 1), 0) == h).astype(F32)


def _ssd_chunk(xs, dtr, dtb, alog, bm, cm, hin, reverse):
    Qn = S_Q
    ii = lax.broadcasted_iota(jnp.int32, (Qn, Qn), 0)
    jj = lax.broadcasted_iota(jnp.int32, (Qn, Qn), 1)
    keep = (ii <= jj) if reverse else (ii >= jj)
    tri = keep.astype(F32)
    triT = ((jj <= ii) if reverse else (jj >= ii)).astype(F32)
    eye = (ii == jj).astype(F32)
    dt = _softplus(dtr + dtb)
    a = dt * (-jnp.exp(alog))
    cs = hdot(tri, a)
    csT = hdot(a, triT, "tn")
    dtT = hdot(dt, eye, "tn")
    last = _onehot_row(0 if reverse else Qn - 1, Qn)
    ys, houts = [], []
    for g in range(S_GROUPS):
        G = bdot(cm[g], bm[g], "nt")
        for r in range(S_HEADS // S_GROUPS):
            h = g * (S_HEADS // S_GROUPS) + r
            eh_r, eh_c = _onehot_row(h, S_HEADS), _onehot_col(h, S_HEADS)
            cs_c = jnp.sum(cs * eh_r, axis=1, keepdims=True)
            dt_c = jnp.sum(dt * eh_r, axis=1, keepdims=True)
            cs_r = jnp.sum(csT * eh_c, axis=0, keepdims=True)
            dt_r = jnp.sum(dtT * eh_c, axis=0, keepdims=True)
            tot = jnp.sum(cs_r * last, axis=1, keepdims=True)
            decay = jnp.exp(jnp.where(keep, cs_c - cs_r, NEG))
            w = G * decay * dt_r
            y = bdot(w, xs[h], "nn") + bdot(cm[g], hin[h], "nt") * jnp.exp(cs_c)
            xsc = xs[h] * (jnp.exp(tot - cs_c) * dt_c)
            hout = hin[h] * jnp.exp(tot) + bdot(xsc, bm[g], "tn")
            ys.append(y)
            houts.append(hout)
    return ys, houts


def _ssd_orders(L, Lc):
    nl, ncx = L // S_Q, Lc // S_Q
    fwd = lambda s: jnp.where(s < ncx, nl + s, s - ncx)
    bwd = lambda s: nl + ncx - 1 - s
    return nl + ncx, fwd, bwd


def _ssd_in_specs(fo, bo, step):
    def at(order, w, col):
        return pl.BlockSpec((S_Q, w), lambda u: (order(step(u)), col))
    specs = []
    for order in (fo, bo):
        specs += [at(order, 512, 0), at(order, 256, 2), at(order, 256, 3), at(order, 128, C_DT // 128)]
    return specs


def ssd_fwd(act, P, dtb, alog, L, Lc, name):
    T = L + Lc
    ns, fo, bo = _ssd_orders(L, Lc)

    def body(xf, bf, cf, df, xb, bb, cb, db, dtb_ref, al_ref, yf, yb, hsf, hsb, Hf, Hb):
        s = pl.program_id(0)

        @pl.when(s == 0)
        def _():
            Hf[...] = jnp.zeros_like(Hf)
            Hb[...] = jnp.zeros_like(Hb)

        for d, (x_r, b_r, c_r, dt_r, y_r, hs_r, H) in enumerate(((xf, bf, cf, df, yf, hsf, Hf), (xb, bb, cb, db, yb, hsb, Hb))):
            hin = [H[h] for h in range(S_HEADS)]
            hs_r[0] = H[...]
            ys, houts = _ssd_chunk(
                [x_r[:, h * S_P:(h + 1) * S_P] for h in range(S_HEADS)], dt_r[:, d * 8:(d + 1) * 8],
                dtb_ref[d:d + 1, 0:8], al_ref[d:d + 1, 0:8],
                [b_r[:, g * S_N:(g + 1) * S_N] for g in range(S_GROUPS)], [c_r[:, g * S_N:(g + 1) * S_N] for g in range(S_GROUPS)],
                hin, reverse=(d == 1))
            for h in range(S_HEADS):
                y_r[:, h * S_P:(h + 1) * S_P] = ys[h]
                H[h] = houts[h]

    ident = lambda u: u
    small = pl.BlockSpec((8, 128), lambda u: (0, 0))
    hspec = pl.BlockSpec((1, S_HEADS, S_P, S_N), lambda u: (u, 0, 0, 0))
    return _pc(body, name=name, grid=(ns,),
               in_specs=_ssd_in_specs(fo, bo, ident) + [small, small],
               out_specs=[pl.BlockSpec((S_Q, 512), lambda u: (fo(u), 0)), pl.BlockSpec((S_Q, 512), lambda u: (bo(u), 0)), hspec, hspec],
               out_shape=[_sds((T, 512), F32), _sds((T, 512), F32), _sds((ns, S_HEADS, S_P, S_N), F32), _sds((ns, S_HEADS, S_P, S_N), F32)],
               scratch_shapes=[pltpu.VMEM((S_HEADS, S_P, S_N), F32), pltpu.VMEM((S_HEADS, S_P, S_N), F32)],
               compiler_params=_cp(("arbitrary",), 32 << 20))(act, act, act, P, act, act, act, P, dtb, alog)


def ssd_bwd(act, P, dtb, alog, hsf, hsb, dy, L, Lc, name):
    T = L + Lc
    ns, fo, bo = _ssd_orders(L, Lc)
    step = lambda u: ns - 1 - u

    def body(xf, bf, cf, df, xb, bb, cb, db, dtb_ref, al_ref, hsf_r, hsb_r, dyf, dyb,
             dxf, dbf, dcf, ddf, dxb, dbb, dcb, ddb, ddtb, dal, dHf, dHb):
        u = pl.program_id(0)

        @pl.when(u == 0)
        def _():
            dHf[...] = jnp.zeros_like(dHf)
            dHb[...] = jnp.zeros_like(dHb)
            ddtb[...] = jnp.zeros_like(ddtb)
            dal[...] = jnp.zeros_like(dal)

        dirs = ((xf, bf, cf, df, hsf_r, dyf, dxf, dbf, dcf, ddf, dHf), (xb, bb, cb, db, hsb_r, dyb, dxb, dbb, dcb, ddb, dHb))
        for d, (x_r, b_r, c_r, dt_r, hs_r, dy_r, dx_o, db_o, dc_o, dd_o, dH) in enumerate(dirs):
            f = functools.partial(_ssd_chunk, reverse=(d == 1))
            _, vjp = jax.vjp(
                f, [x_r[:, h * S_P:(h + 1) * S_P] for h in range(S_HEADS)], dt_r[:, d * 8:(d + 1) * 8],
                dtb_ref[d:d + 1, 0:8], al_ref[d:d + 1, 0:8],
                [b_r[:, g * S_N:(g + 1) * S_N] for g in range(S_GROUPS)], [c_r[:, g * S_N:(g + 1) * S_N] for g in range(S_GROUPS)],
                [hs_r[0, h] for h in range(S_HEADS)])
            gx, gdt, gdtb, gal, gb, gc, gh = vjp(([dy_r[:, h * S_P:(h + 1) * S_P] for h in range(S_HEADS)],
                                                  [dH[h] for h in range(S_HEADS)]))
            for h in range(S_HEADS):
                dx_o[:, h * S_P:(h + 1) * S_P] = gx[h]
                dH[h] = gh[h]
            for g in range(S_GROUPS):
                db_o[:, g * S_N:(g + 1) * S_N] = gb[g]
                dc_o[:, g * S_N:(g + 1) * S_N] = gc[g]
            dd_o[...] = gdt
            ddtb[d:d + 1, 0:8] += gdtb
            dal[d:d + 1, 0:8] += gal

    small = pl.BlockSpec((8, 128), lambda u: (0, 0))
    hspec = pl.BlockSpec((1, S_HEADS, S_P, S_N), lambda u: (step(u), 0, 0, 0))
    at = lambda order, w: pl.BlockSpec((S_Q, w), lambda u: (order(step(u)), 0))
    outs = []
    for order in (fo, bo):
        outs += [at(order, 512), at(order, 256), at(order, 256), at(order, 8)]
    oshape = [_sds((T, 512), F32), _sds((T, 256), F32), _sds((T, 256), F32), _sds((T, 8), F32)]
    return _pc(body, name=name, grid=(ns,),
               in_specs=_ssd_in_specs(fo, bo, step) + [small, small, hspec, hspec, at(fo, 512), at(bo, 512)],
               out_specs=outs + [small, small], out_shape=oshape + oshape + [_sds((8, 128), F32), _sds((8, 128), F32)],
               scratch_shapes=[pltpu.VMEM((S_HEADS, S_P, S_N), F32), pltpu.VMEM((S_HEADS, S_P, S_N), F32)],
               compiler_params=_cp(("arbitrary",), 40 << 20))(act, act, act, P, act, act, act, P, dtb, alog, hsf, hsb, dy, dy)


def _ssm_out(yf, yb, xs, z, dskip, g):
    y = (yf + yb + dskip * xs) * _silu(z)
    return (y * lax.rsqrt(jnp.mean(y * y, axis=-1, keepdims=True) + EPS)) * g


def ssm_out_fwd(yf, yb, act, P, dskip, g, name):
    T = yf.shape[0]

    def body(yf_r, yb_r, xs_r, z_r, d_r, g_r, o_r):
        o_r[...] = _ssm_out(yf_r[...], yb_r[...], xs_r[...], z_r[...], d_r[...], g_r[...]).astype(o_r.dtype)

    row = pl.BlockSpec((TR, 512), lambda i: (i, 0))
    vec = pl.BlockSpec((1, 512), lambda i: (0, 0))
    return _pc(body, name=name, grid=(T // TR,),
               in_specs=[row, row, row, pl.BlockSpec((TR, 512), lambda i: (i, C_Z // 512)), vec, vec],
               out_specs=row, out_shape=_sds((T, 512), BF16),
               compiler_params=_cp(("parallel",), 16 << 20))(yf, yb, act, P, dskip, g)


def ssm_out_bwd(yf, yb, act, P, dskip, g, do_src, name):
    T = yf.shape[0]

    def body(yf_r, yb_r, xs_r, z_r, d_r, g_r, do_r, dy_r, dxs_r, dz_r, dv_r):
        @pl.when(pl.program_id(0) == 0)
        def _():
            dv_r[...] = jnp.zeros_like(dv_r)

        _, vjp = jax.vjp(_ssm_out, yf_r[...], yb_r[...], xs_r[...], z_r[...], d_r[...], g_r[...])
        dyf, _, dxs, dz, dd, dg = vjp(do_r[...].astype(F32))
        dy_r[...] = dyf
        dxs_r[...] = dxs
        dz_r[...] = dz.astype(dz_r.dtype)
        dv_r[0:1, :] += dd
        dv_r[1:2, :] += dg

    row = pl.BlockSpec((TR, 512), lambda i: (i, 0))
    vec = pl.BlockSpec((1, 512), lambda i: (0, 0))
    return _pc(body, name=name, grid=(T // TR,),
               in_specs=[row, row, row, pl.BlockSpec((TR, 512), lambda i: (i, C_Z // 512)), vec, vec,
                         pl.BlockSpec((TR, 512), lambda i: (i, 1))],
               out_specs=[row, row, row, pl.BlockSpec((8, 512), lambda i: (0, 0))],
               out_shape=[_sds((T, 512), F32), _sds((T, 512), F32), _sds((T, 512), BF16), _sds((8, 512), F32)],
               compiler_params=_cp(("arbitrary",), 24 << 20))(yf, yb, act, P, dskip, g, do_src)


def add_halves(xv, got, cvec, name):
    n, r, cdim = xv.shape
    h = r // 2

    def body(c_ref, x_ref, g_ref, o_ref):
        o_ref[...] = (x_ref[...].astype(F32) + g_ref[...].astype(F32)).astype(o_ref.dtype)

    gs = pltpu.PrefetchScalarGridSpec(
        num_scalar_prefetch=1, grid=(n,),
        in_specs=[pl.BlockSpec((1, h, cdim), lambda k, c_ref: (k, c_ref[0], 0)), pl.BlockSpec((1, h, cdim), lambda k, c_ref: (k, 0, 0))],
        out_specs=pl.BlockSpec((1, h, cdim), lambda k, c_ref: (k, 0, 0)))
    return _pc(body, name=name, grid_spec=gs, out_shape=_sds((n, h, cdim), BF16),
               compiler_params=_cp(("arbitrary",), 24 << 20))(cvec, xv, got)


def sum_slots(a, name):
    n, r, cdim = a.shape
    tr = _div_tile(r, 512, 16)

    def body(a_ref, o_ref):
        acc = a_ref[0].astype(F32)
        for k in range(1, n):
            acc = acc + a_ref[k].astype(F32)
        o_ref[...] = acc

    return _pc(body, name=name, grid=(r // tr,), in_specs=[pl.BlockSpec((n, tr, cdim), lambda i: (0, i, 0))],
               out_specs=pl.BlockSpec((tr, cdim), lambda i: (i, 0)), out_shape=_sds((r, cdim), F32),
               compiler_params=_cp(("parallel",), 32 << 20))(a)


def adamw(w, g, m, v, name):
    R, C = w.shape
    tr = _div_tile(R, max(8, (1 << 19) // max(C, 1) // 8 * 8), 8) if R % 8 == 0 else R
    c1 = 1.0 / (1.0 - ADAM_B1 ** ADAM_STEP)
    c2 = 1.0 / (1.0 - ADAM_B2 ** ADAM_STEP)

    def body(w_ref, g_ref, m_ref, v_ref, d_ref, mo_ref, vo_ref):
        gg = g_ref[...]
        mn = ADAM_B1 * m_ref[...] + (1.0 - ADAM_B1) * gg
        vn = ADAM_B2 * v_ref[...] + (1.0 - ADAM_B2) * (gg * gg)
        d_ref[...] = -ADAM_LR * ((mn * c1) / (jnp.sqrt(vn * c2) + ADAM_EPS) + ADAM_WD * w_ref[...])
        mo_ref[...] = mn
        vo_ref[...] = vn

    spec = pl.BlockSpec((tr, C), lambda i: (i, 0))
    return _pc(body, name=name, grid=(R // tr,), in_specs=[spec] * 4, out_specs=[spec] * 3,
               out_shape=[_sds((R, C), F32)] * 3, compiler_params=_cp(("parallel",), 32 << 20))(w, g, m, v)


def _me():
    return lax.axis_index("x"), lax.axis_index("y"), lax.axis_index("c")


def _flip(v, bit):
    return 1 - v if bit else v


def allgather8(xv, name):
    R = xv.shape[0]

    def body(x_ref, out_ref, sum_ref, send_sems, recv_sems):
        mx, my, mc = _me()
        me = 4 * mx + 2 * my + mc
        out_ref[me] = x_ref[...]
        sends, recvs = [], []
        for k in range(1, 8):
            px, py, pc = _flip(mx, k & 4), _flip(my, k & 2), _flip(mc, k & 1)
            peer = 4 * px + 2 * py + pc
            sends.append(pltpu.make_async_remote_copy(src_ref=x_ref, dst_ref=out_ref.at[me], send_sem=send_sems.at[k - 1],
                                                      recv_sem=recv_sems.at[k - 1], device_id=(px, py, pc), device_id_type=MESH))
            recvs.append(pltpu.make_async_remote_copy(src_ref=x_ref, dst_ref=out_ref.at[peer], send_sem=send_sems.at[k - 1],
                                                      recv_sem=recv_sems.at[k - 1], device_id=(px, py, pc), device_id_type=MESH))
        for cp in sends:
            cp.start()
        for cp in recvs:
            cp.wait_recv()
        for cp in sends:
            cp.wait_send()
        acc = out_ref[0]
        for d in range(1, 8):
            acc = acc + out_ref[d]
        sum_ref[...] = acc

    vm = pl.BlockSpec(memory_space=pltpu.VMEM)
    return _pc(body, name=name, pin=False, in_specs=[vm], out_specs=[vm, vm], out_shape=[_sds((8, R, 128), F32), _sds((R, 128), F32)],
               scratch_shapes=[pltpu.SemaphoreType.DMA((7,)), pltpu.SemaphoreType.DMA((7,))],
               compiler_params=_cp(None, 32 << 20))(xv)


def _other_chips(mx, my):
    return [(1 - mx, my), (mx, 1 - my), (1 - mx, 1 - my)]


def _halves(r, mc, mult):
    h = r // 2
    return pl.ds(pl.multiple_of(mc * h, mult), h), pl.ds(pl.multiple_of((1 - mc) * h, mult), h)


def _rcopy(src, dst, send_sems, recv_sems, k, to):
    return pltpu.make_async_remote_copy(src_ref=src, dst_ref=dst, send_sem=send_sems.at[k], recv_sem=recv_sems.at[k],
                                        device_id=to, device_id_type=MESH)


def _gather_body(xs, outs, send_sems, recv_sems):
    n = len(xs)
    mx, my, mc = _me()
    chip = 2 * mx + my
    sib = (mx, my, 1 - mc)
    chips = _other_chips(mx, my)
    idx = [2 * cx + cy for cx, cy in chips]
    cp = functools.partial(_rcopy, send_sems=send_sems, recv_sems=recv_sems)
    hv = [_halves(x.shape[0], mc, 16) for x in xs]
    first, passed = [], []
    for a in range(n):
        for j, (cx, cy) in enumerate(chips):
            first.append(cp(xs[a].at[hv[a][0]], outs[a].at[chip, hv[a][0]], k=6 * a + j, to=(cx, cy, mc)))
            first[-1].start()
    for a in range(n):
        for j in range(3):
            cp(xs[a].at[hv[a][0]], outs[a].at[idx[j], hv[a][0]], k=6 * a + j, to=sib).wait_recv()
            passed.append(cp(outs[a].at[idx[j], hv[a][0]], outs[a].at[idx[j], hv[a][0]], k=6 * a + 3 + j, to=sib))
            passed[-1].start()
    for a in range(n):
        for j in range(3):
            cp(xs[a].at[hv[a][1]], outs[a].at[idx[j], hv[a][1]], k=6 * a + 3 + j, to=sib).wait_recv()
    for c_ in first + passed:
        c_.wait_send()


def _my_chip():
    return 2 * lax.axis_index("x") + lax.axis_index("y")


def _own_slots(outs, shards):
    return [lax.dynamic_update_index_in_dim(o, x, _my_chip(), 0) for o, x in zip(outs, shards)]


def gather_weights(shards, name):
    n = len(shards)

    def body(*refs):
        _gather_body(refs[:n], refs[n:2 * n], *refs[2 * n:])

    hbm = pl.BlockSpec(memory_space=pl.ANY)
    outs = _pc(body, name=name, in_specs=[hbm] * n, out_specs=[hbm] * n, out_shape=[_sds((4,) + x.shape, x.dtype) for x in shards],
               scratch_shapes=[pltpu.SemaphoreType.DMA((6 * n,)), pltpu.SemaphoreType.DMA((6 * n,))])(*shards)
    return _own_slots(outs, shards)


GATHER_REST_ID = 3


def gather_weights_sc(shards, name):
    n = len(shards)
    x_refs = [jax.new_ref(x, memory_space=pltpu.MemorySpace.HBM) for x in shards]
    out_refs = [jax.empty_ref(_sds((4,) + x.shape, x.dtype), memory_space=pltpu.MemorySpace.HBM) for x in shards]

    @pl.kernel(mesh=plsc.ScalarSubcoreMesh(axis_name="sc", num_cores=1), name=name,
               scratch_types=(pltpu.SemaphoreType.DMA((6 * n,)), pltpu.SemaphoreType.DMA((6 * n,))),
               compiler_params=pltpu.CompilerParams(collective_id=GATHER_REST_ID))
    def launch(send_sems, recv_sems):
        mx, my, mc = _me()
        barrier = pltpu.get_barrier_semaphore()
        for peer in [(mx, my, 1 - mc)] + [(cx, cy, mc) for cx, cy in _other_chips(mx, my)]:
            pl.semaphore_signal(barrier, inc=1, device_id=peer, device_id_type=MESH)
        pl.semaphore_wait(barrier, 4)
        _gather_body(x_refs, out_refs, send_sems, recv_sems)

    launch()
    return _own_slots([o[...] for o in out_refs], shards)


def swap_halves(arrs, name):
    n = len(arrs)

    def body(*refs):
        xs, outs = refs[:n], refs[n:2 * n]
        send_sems, recv_sems = refs[2 * n:]
        mx, my, mc = _me()
        cps = []
        for a in range(n):
            theirs = _halves(xs[a].shape[1], mc, 16)[1]
            cps.append(_rcopy(xs[a].at[pl.ds(0, 4), theirs], outs[a], send_sems, recv_sems, a, (mx, my, 1 - mc)))
            cps[-1].start()
        for c_ in cps:
            c_.wait()

    hbm = pl.BlockSpec(memory_space=pl.ANY)
    return _pc(body, name=name, in_specs=[hbm] * n, out_specs=[hbm] * n,
               out_shape=[_sds((4, x.shape[1] // 2, x.shape[2]), x.dtype) for x in arrs],
               scratch_shapes=[pltpu.SemaphoreType.DMA((n,)), pltpu.SemaphoreType.DMA((n,))])(*arrs)


SCATTER_ID = 4


def scatter_chips_sc(arrs, name):
    n = len(arrs)
    x_refs = [jax.new_ref(x, memory_space=pltpu.MemorySpace.HBM) for x in arrs]
    out_refs = [jax.empty_ref(_sds(x.shape, x.dtype), memory_space=pltpu.MemorySpace.HBM) for x in arrs]

    @pl.kernel(mesh=plsc.ScalarSubcoreMesh(axis_name="sc", num_cores=1), name=name,
               scratch_types=(pltpu.SemaphoreType.DMA((3 * n,)), pltpu.SemaphoreType.DMA((3 * n,))),
               compiler_params=pltpu.CompilerParams(collective_id=SCATTER_ID))
    def launch(send_sems, recv_sems):
        mx, my, mc = _me()
        chip = 2 * mx + my
        chips = _other_chips(mx, my)
        idx = [2 * cx + cy for cx, cy in chips]
        barrier = pltpu.get_barrier_semaphore()
        for cx, cy in chips:
            pl.semaphore_signal(barrier, inc=1, device_id=(cx, cy, mc), device_id_type=MESH)
        pl.semaphore_wait(barrier, 3)
        cp = functools.partial(_rcopy, send_sems=send_sems, recv_sems=recv_sems)
        sends = []
        for a in range(n):
            for j, (cx, cy) in enumerate(chips):
                sends.append(cp(x_refs[a].at[idx[j]], out_refs[a].at[chip], k=3 * a + j, to=(cx, cy, mc)))
                sends[-1].start()
        for a in range(n):
            for j, (cx, cy) in enumerate(chips):
                cp(x_refs[a].at[idx[j]], out_refs[a].at[idx[j]], k=3 * a + j, to=(cx, cy, mc)).wait_recv()
        for c_ in sends:
            c_.wait_send()

    launch()
    return _own_slots([o[...] for o in out_refs], [lax.dynamic_index_in_dim(x, _my_chip(), 0, keepdims=False) for x in arrs])


def share_halves(parts, name):
    flat = [p for w in parts for p in w]
    nw, n = len(parts), len(flat)
    depth = n // nw

    def body(*refs):
        xs, outs = refs[:n], refs[n:n + nw]
        send_sems, recv_sems = refs[n + nw:]
        mx, my, mc = _me()
        sib = (mx, my, 1 - mc)
        sends, recvs = [], []
        for a in range(n):
            w, l = a // depth, a % depth
            mine, theirs = _halves(outs[w].shape[1], mc, 8)
            sends.append(_rcopy(xs[a], outs[w].at[l, mine], send_sems, recv_sems, a, sib))
            recvs.append(_rcopy(xs[a], outs[w].at[l, theirs], send_sems, recv_sems, a, sib))
            sends[-1].start()
        for c_ in recvs:
            c_.wait_recv()
        for c_ in sends:
            c_.wait_send()

    hbm = pl.BlockSpec(memory_space=pl.ANY)
    outs = _pc(body, name=name, in_specs=[hbm] * n, out_specs=[hbm] * nw,
               out_shape=[_sds((depth, 2 * w[0].shape[0], w[0].shape[1]), F32) for w in parts],
               scratch_shapes=[pltpu.SemaphoreType.DMA((n,)), pltpu.SemaphoreType.DMA((n,))])(*flat)
    outs = list(outs)
    mc = lax.axis_index("c")
    for w in range(nw):
        for l in range(depth):
            h = parts[w][l].shape[0]
            outs[w] = lax.dynamic_update_slice(outs[w], parts[w][l][None], (l, mc * h, 0))
    return outs


_BIG = ("w_in", "w_out", "w_ffn_in", "w_ffn_out")
N_CHIPS = 4
DEPTH = 2


def _pad_rows(v, mult=8):
    n = v.shape[0]
    rows = -(-n // 128)
    rows = -(-rows // mult) * mult
    return jnp.pad(v, (0, rows * 128 - n)).reshape(rows, 128)


class _Flat:
    def __init__(self):
        self.items = []

    def add(self, name, a):
        self.items.append((name, a.shape, a.reshape(-1).astype(F32)))

    def rows(self):
        return _pad_rows(jnp.concatenate([a for _, _, a in self.items]))

    def split(self, rows):
        flat = rows.reshape(-1)
        out, o = {}, 0
        for name, shape, a in self.items:
            out[name] = flat[o:o + a.shape[0]].reshape(shape)
            o += a.shape[0]
        return out

    def split_lead(self, rows3):
        n = rows3.shape[0]
        flat = rows3.reshape(n, -1)
        out, o = {}, 0
        for name, shape, a in self.items:
            out[name] = flat[:, o:o + a.shape[0]].reshape((n,) + tuple(shape))
            o += a.shape[0]
        return out


def _gsv(rows):
    z = jnp.zeros((2, D), F32)
    r = [z if a is None else a for a in rows] + [z] * 5
    return jnp.stack(r, axis=1)


def _pad8(a, rows=8, cols=128):
    return jnp.zeros((rows, cols), F32).at[:a.shape[0], :a.shape[1]].set(a.astype(F32))


def kernel(x, c, ctx, c_ctx, w_mod, b_mod, g_mix, w_in, wa_sink, na_rpb, ssm_conv_w, ssm_conv_b, ssm_dt_bias, ssm_a_log, ssm_d, ssm_norm_g, w_out, g_ffn, w_ffn_in, w_ffn_out, g_final, loss_target, m_c_ctx, m_w_mod, m_b_mod, m_g_mix, m_w_in, m_wa_sink, m_na_rpb, m_ssm_conv_w, m_ssm_conv_b, m_ssm_dt_bias, m_ssm_a_log, m_ssm_d, m_ssm_norm_g, m_w_out, m_g_ffn, m_w_ffn_in, m_w_ffn_out, m_g_final, v_c_ctx, v_w_mod, v_b_mod, v_g_mix, v_w_in, v_wa_sink, v_na_rpb, v_ssm_conv_w, v_ssm_conv_b, v_ssm_dt_bias, v_ssm_a_log, v_ssm_d, v_ssm_norm_g, v_w_out, v_g_ffn, v_w_ffn_in, v_w_ffn_out, v_g_final):
    L, Lc = x.shape[1], ctx.shape[1]
    T = L + Lc
    nL = L // TR
    mx, my, mc = lax.axis_index("x"), lax.axis_index("y"), lax.axis_index("c")
    dev = 4 * mx + 2 * my + mc
    chip = 2 * mx + my
    MODW = 6 * D // N_CHIPS
    CW = 1024 // N_CHIPS

    sc = _silu(c.astype(F32))
    scc = _silu(c_ctx.astype(F32))[None]
    f1 = _Flat()
    f1.add("sc", sc)
    f1.add("conv_w", ssm_conv_w)
    g1, _ = allgather8(f1.rows(), "gather_cond")
    g1 = f1.split_lead(g1)
    sc_all = g1["sc"][:, 0]
    conv_w = jnp.concatenate([g1["conv_w"][2 * k] for k in range(N_CHIPS)], axis=-1)
    A16 = jnp.concatenate([sc_all, scc, jnp.zeros((7, D), F32)], axis=0)

    mod_part = jnp.stack([matmul(A16, w_mod[l], "nn", F32, f"mod_fwd{l}") for l in range(DEPTH)])
    f2 = _Flat()
    f2.add("mod", mod_part)
    g2, _ = allgather8(f2.rows(), "gather_mod")
    g2 = f2.split_lead(g2)["mod"]
    mods = jnp.concatenate([g2[2 * k] for k in range(N_CHIPS)], axis=-1) + b_mod[:, None, :]
    mod_l = lax.dynamic_index_in_dim(mods, dev, axis=1, keepdims=False).reshape(DEPTH, 6, D)
    mod_c = mods[:, 8].reshape(DEPTH, 6, D)
    mod = jnp.stack([mod_l, mod_c], axis=1)
    mrow = lambda l, j: mod[l, :, j]

    own = {"w_in": w_in, "w_out": w_out, "w_ffn_in": w_ffn_in, "w_ffn_out": w_ffn_out}
    sh16 = [own[n][l].astype(BF16) for n in _BIG for l in range(DEPTH)]
    gath = list(gather_weights(sh16[:1], "gather_first"))
    after_first = (gath[0][0, 0, 0] * 0).astype(BF16)
    gath += list(gather_weights_sc([sh16[1] + after_first] + sh16[2:], "gather_rest"))
    gw = {n: [gath[DEPTH * i + l] for l in range(DEPTH)] for i, n in enumerate(_BIG)}
    W_in = [jnp.pad(jnp.concatenate([g[k] for k in range(N_CHIPS)], axis=1), ((0, 0), (0, IN_PAD - IN_COLS))) for g in gw["w_in"]]
    W_out = [g.reshape(D, D) for g in gw["w_out"]]
    W_fo = [g.reshape(D_FF, D) for g in gw["w_ffn_out"]]
    W_fi = gw["w_ffn_in"]

    cos, sin, rotm = rope_tables(L, Lc)
    x0 = jnp.concatenate([x[0], ctx[0]], axis=0).astype(F32)

    sv = []
    xin = x0
    gsv_first = _gsv([None, mrow(0, 0), mrow(0, 1)])
    _, h1 = res_norm_mod(x0, None, gsv_first, g_mix[0][None], nL, "norm_first")
    for l in range(DEPTH):
        s = {"xin": xin, "h1": h1}
        P = matmul(h1, W_in[l], "nn", F32, f"in_proj{l}", tn=IN_PAD)
        qr, kr, kb, vb = rope_apply(P, C_QA // 256, P, C_KA // 128, cos, sin, rotm, False, f"rope{l}", kv_src=P)
        sink8 = _pad8(jnp.broadcast_to(wa_sink[l][:, None], (WA_HEADS, 128)))
        oa = win_attn_fwd(qr, kr, P, sink8, L, Lc, f"wa_fwd{l}")
        bias = na_bias_table(na_rpb[l], l)
        ob = na_fwd(P, kb, vb, bias, L, Lc, f"na_fwd{l}")
        w8 = jnp.concatenate([conv_w[l], jnp.zeros((1, 1024), F32)], axis=0)
        pre, act = conv_silu_fwd(P, w8, ssm_conv_b[l][None], nL, f"conv_fwd{l}")
        dtb8, al8 = _pad8(ssm_dt_bias[l]), _pad8(ssm_a_log[l])
        yf, yb, hsf, hsb = ssd_fwd(act, P, dtb8, al8, L, Lc, f"ssd_fwd{l}")
        dskip = jnp.repeat(ssm_d[l], S_P)[None]
        oc = ssm_out_fwd(yf, yb, act, P, dskip, ssm_norm_g[l][None], f"ssm_out_fwd{l}")
        mixin = jnp.concatenate([oa, ob, oc], axis=1)
        mix = matmul(mixin, W_out[l], "nn", F32, f"out_proj{l}")
        gsv_mid = _gsv([mrow(l, 2), mrow(l, 3), mrow(l, 4)])
        x1, h2 = res_norm_mod(xin, mix, gsv_mid, g_ffn[l][None], nL, f"norm_mid{l}")
        gu = matmul_fi(h2, W_fi[l], "nn", BF16, f"ffn_in{l}")
        af = swiglu_fwd(gu, f"swiglu_fwd{l}")
        fo = matmul(af, W_fo[l], "nn", F32, f"ffn_out{l}")
        s.update(P=P, qr=qr, kr=kr, sink8=sink8, kb=kb, vb=vb, bias=bias, w8=w8, pre=pre, act=act, dtb8=dtb8, al8=al8, yf=yf,
                 yb=yb, hsf=hsf, hsb=hsb, dskip=dskip, mixin=mixin, mix=mix, gsv_mid=gsv_mid, x1=x1, h2=h2, gu=gu, af=af, fo=fo)
        if l + 1 < DEPTH:
            s["gsv_end"] = _gsv([mrow(l, 5), mrow(l + 1, 0), mrow(l + 1, 1)])
            xin, h1 = res_norm_mod(x1, fo, s["gsv_end"], g_mix[l + 1][None], nL, f"norm_end{l}")
        else:
            s["gsv_end"] = _gsv([mrow(l, 5), None, None])
        sv.append(s)

    last = sv[-1]
    loss8, dres, dfo, dgsv_end, dg_final = final_loss(last["x1"], last["fo"], last["gsv_end"], g_final[None], loss_target[0].astype(F32), nL, "final_loss")
    loss = lax.psum(loss8[0, 0], ("x", "y", "c"))

    dmod = [[None] * 6 for _ in range(DEPTH)]
    gW = {n: [None] * DEPTH for n in _BIG}
    small = [dict() for _ in range(DEPTH)]
    parts = [None] * DEPTH
    cvec = mc.astype(jnp.int32).reshape(1)
    grad_x = None
    for l in reversed(range(DEPTH)):
        s = sv[l]
        dmod[l][5] = dgsv_end[:, 0]
        if l + 1 < DEPTH:
            dmod[l + 1][0], dmod[l + 1][1] = dgsv_end[:, 1], dgsv_end[:, 2]
        daf = matmul(dfo, W_fo[l], "nt", BF16, f"ffn_out_dx{l}")
        gW["w_ffn_out"][l] = matmul(s["af"], dfo, "tn", BF16, f"ffn_out_dw{l}", tm=1408).reshape(N_CHIPS, D_FF // N_CHIPS, D)
        dgu = swiglu_bwd(s["gu"], daf, f"swiglu_bwd{l}")
        dh2 = matmul_fi(dgu, W_fi[l], "nt", F32, f"ffn_in_dx{l}")
        gW["w_ffn_in"][l] = matmul_fi(s["h2"], dgu, "tn", BF16, f"ffn_in_dw{l}")
        dres, dmix, dgsv_mid, dg_ffn = res_norm_mod_bwd(s["x1"], s["mix"], s["gsv_mid"], g_ffn[l][None], dh2, dres, nL, f"norm_mid_bwd{l}")
        dmod[l][2], dmod[l][3], dmod[l][4] = dgsv_mid[:, 0], dgsv_mid[:, 1], dgsv_mid[:, 2]
        dmixin = matmul(dmix, W_out[l], "nt", F32, f"out_proj_dx{l}")
        gW["w_out"][l] = matmul(s["mixin"], dmix, "tn", BF16, f"out_proj_dw{l}", tm=1024).reshape(N_CHIPS, D // N_CHIPS, D)
        P = s["P"]
        dqr, dkr, dva, dsink = win_attn_bwd(s["qr"], s["kr"], P, s["sink8"], dmixin, L, Lc, f"wa_bwd{l}")
        dqa, dka = rope_apply(dqr, 0, dkr[WA_BLK:WA_BLK + T], 0, cos, sin, rotm, True, f"rope_bwd{l}")
        dqb, dkb, dvb, dbias = na_bwd(P, s["kb"], s["vb"], s["bias"], dmixin, L, Lc, f"na_bwd{l}")
        dy, dxs1, dz, dvec = ssm_out_bwd(s["yf"], s["yb"], s["act"], P, s["dskip"], ssm_norm_g[l][None], dmixin, f"ssm_out_bwd{l}")
        dxf, dbf, dcf, ddf, dxb, dbb, dcb, ddb, ddtb, dal = ssd_bwd(s["act"], P, s["dtb8"], s["al8"], s["hsf"], s["hsb"], dy, L, Lc, f"ssd_bwd{l}")
        dpre = dsilu(s["pre"], [dxf, dxb, dxs1], [dbf, dbb], [dcf, dcb], f"dsilu{l}")
        dxbc, dw8, db8 = conv_bwd(dpre, P, s["w8"], nL, f"conv_bwd{l}")
        dP = jnp.concatenate([dqa, dqb, dz, dka, dva[WA_BLK:WA_BLK + T].astype(BF16), dkb.astype(BF16), dvb.astype(BF16), dxbc,
                              ddf.astype(BF16), ddb.astype(BF16), jnp.zeros((T, IN_PAD - IN_COLS), BF16)], axis=1)
        dh1 = matmul(dP, W_in[l], "nt", F32, f"in_proj_dx{l}", tk=IN_PAD)
        dwin = matmul(s["h1"], dP, "tn", BF16, f"in_proj_dw{l}", tm=512, tn=IN_PAD)
        cw = IN_COLS // N_CHIPS
        gW["w_in"][l] = jnp.stack([dwin[:, k * cw:(k + 1) * cw] for k in range(N_CHIPS)])
        garr = [gW[n][l] for n in _BIG]
        got = swap_halves(garr, f"reduce_d2d{l}")
        chip_sum = [add_halves(garr[a], got[a], cvec, f"reduce_add_pair{l}_{a}") for a in range(len(garr))]
        parts[l] = scatter_chips_sc(chip_sum, f"reduce_ici{l}")
        small[l] = dict(g_ffn=dg_ffn[0], wa_sink=dsink[:WA_HEADS, 0], na_rpb=na_rpb_grad(dbias, l), conv_w=dw8[:S_CONV], conv_b=db8[0],
                        dt_bias=ddtb[:2, :8], a_log=dal[:2, :8], ssm_d=dvec[0].reshape(S_HEADS, S_P).sum(axis=1), norm_g=dvec[1])
        if l > 0:
            p = sv[l - 1]
            dres, dfo, dgsv_end, dg_mix = res_norm_mod_bwd(s["xin"], p["fo"], p["gsv_end"], g_mix[l][None], dh1, dres, nL, f"norm_end_bwd{l - 1}")
        else:
            grad_x, _, dgsv_first, dg_mix = res_norm_mod_bwd(s["xin"], None, gsv_first, g_mix[0][None], dh1, dres, nL, "norm_first_bwd")
            dmod[0][0], dmod[0][1] = dgsv_first[:, 1], dgsv_first[:, 2]
        small[l]["g_mix"] = dg_mix[0]
    for l in range(DEPTH):
        for j in range(6):
            if dmod[l][j] is None:
                dmod[l][j] = jnp.zeros((2, D), F32)
    dmod = jnp.stack([jnp.stack(r, axis=1) for r in dmod])

    f3 = _Flat()
    f3.add("dmod_l", dmod[:, 0].reshape(DEPTH, 6 * D))
    f3.add("dmod_c", dmod[:, 1].reshape(DEPTH, 6 * D))
    f3.add("g_final", dg_final[0])
    for n in ("g_mix", "g_ffn", "wa_sink", "na_rpb", "conv_w", "conv_b", "dt_bias", "a_log", "ssm_d", "norm_g"):
        f3.add(n, jnp.stack([small[l][n] for l in range(DEPTH)]))
    g3, s3 = allgather8(f3.rows(), "reduce_small")
    dmod_all = f3.split_lead(g3)["dmod_l"]
    s3 = f3.split(s3)
    dmodc_tot = s3["dmod_c"]
    col0 = chip * MODW
    G16, G16c = [], []
    for l in range(DEPTH):
        rows = jnp.concatenate([dmod_all[:, l], dmodc_tot[l][None], jnp.zeros((7, 6 * D), F32)], axis=0)
        G16.append(lax.dynamic_slice_in_dim(rows, col0, MODW, axis=1))
        rc = jnp.concatenate([dmodc_tot[l][None], jnp.zeros((15, 6 * D), F32)], axis=0)
        G16c.append(lax.dynamic_slice_in_dim(rc, col0, MODW, axis=1))
    grad_w_mod = jnp.stack([matmul(A16, G16[l], "tn", F32, f"mod_dw{l}") for l in range(DEPTH)])
    dscc_part = sum(matmul(G16c[l], w_mod[l], "nt", F32, f"mod_dx{l}")[0] for l in range(DEPTH))
    _, s4 = allgather8(_pad_rows(dscc_part * (mc == 1).astype(F32)), "reduce_cctx")
    dscc = s4.reshape(-1)[:D]
    cc = c_ctx.astype(F32)
    sg = 1.0 / (1.0 + jnp.exp(-cc))
    grad_c_ctx = dscc * (sg * (1.0 + cc * (1.0 - sg)))

    halves = [[sum_slots(parts[l][i], f"reduce_add_chips{l}_{i}") for l in range(DEPTH)] for i in range(len(_BIG))]
    gsh = dict(zip(_BIG, share_halves(halves, "reduce_share")))

    grads = {"c_ctx": grad_c_ctx, "w_mod": grad_w_mod, "b_mod": s3["dmod_l"] + s3["dmod_c"], "g_mix": s3["g_mix"], "w_in": gsh["w_in"],
             "wa_sink": s3["wa_sink"], "na_rpb": s3["na_rpb"],
             "ssm_conv_w": lax.dynamic_slice_in_dim(s3["conv_w"], chip * CW, CW, axis=2), "ssm_conv_b": s3["conv_b"],
             "ssm_dt_bias": s3["dt_bias"], "ssm_a_log": s3["a_log"], "ssm_d": s3["ssm_d"], "ssm_norm_g": s3["norm_g"],
             "w_out": gsh["w_out"], "g_ffn": s3["g_ffn"], "w_ffn_in": gsh["w_ffn_in"], "w_ffn_out": gsh["w_ffn_out"], "g_final": s3["g_final"]}
    wts = {"c_ctx": c_ctx, "w_mod": w_mod, "b_mod": b_mod, "g_mix": g_mix, "w_in": w_in, "wa_sink": wa_sink, "na_rpb": na_rpb,
           "ssm_conv_w": ssm_conv_w, "ssm_conv_b": ssm_conv_b, "ssm_dt_bias": ssm_dt_bias, "ssm_a_log": ssm_a_log, "ssm_d": ssm_d,
           "ssm_norm_g": ssm_norm_g, "w_out": w_out, "g_ffn": g_ffn, "w_ffn_in": w_ffn_in, "w_ffn_out": w_ffn_out, "g_final": g_final}
    ms = {"c_ctx": m_c_ctx, "w_mod": m_w_mod, "b_mod": m_b_mod, "g_mix": m_g_mix, "w_in": m_w_in, "wa_sink": m_wa_sink, "na_rpb": m_na_rpb,
          "ssm_conv_w": m_ssm_conv_w, "ssm_conv_b": m_ssm_conv_b, "ssm_dt_bias": m_ssm_dt_bias, "ssm_a_log": m_ssm_a_log, "ssm_d": m_ssm_d,
          "ssm_norm_g": m_ssm_norm_g, "w_out": m_w_out, "g_ffn": m_g_ffn, "w_ffn_in": m_w_ffn_in, "w_ffn_out": m_w_ffn_out, "g_final": m_g_final}
    vs = {"c_ctx": v_c_ctx, "w_mod": v_w_mod, "b_mod": v_b_mod, "g_mix": v_g_mix, "w_in": v_w_in, "wa_sink": v_wa_sink, "na_rpb": v_na_rpb,
          "ssm_conv_w": v_ssm_conv_w, "ssm_conv_b": v_ssm_conv_b, "ssm_dt_bias": v_ssm_dt_bias, "ssm_a_log": v_ssm_a_log, "ssm_d": v_ssm_d,
          "ssm_norm_g": v_ssm_norm_g, "w_out": v_w_out, "g_ffn": v_g_ffn, "w_ffn_in": v_w_ffn_in, "w_ffn_out": v_w_ffn_out, "g_final": v_g_final}
    names = list(wts)
    grads = {n: grads[n].reshape(wts[n].shape).astype(F32) for n in names}
    big = ("w_mod", "w_in", "w_out", "w_ffn_in", "w_ffn_out")
    delta, new_m, new_v = {}, {}, {}
    for n in big:
        shp = wts[n].shape
        two = lambda a: a.reshape(shp[0] * shp[1], shp[2])
        d_, m_, v_ = adamw(two(wts[n]), two(grads[n]), two(ms[n]), two(vs[n]), f"adamw_{n}")
        delta[n], new_m[n], new_v[n] = d_.reshape(shp), m_.reshape(shp), v_.reshape(shp)
    packs = []
    for src in (wts, grads, ms, vs):
        f = _Flat()
        for n in names:
            if n not in big:
                f.add(n, src[n])
        packs.append(f)
    d_, m_, v_ = adamw(*[f.rows() for f in packs], "adamw_small")
    for dst, rows in ((delta, d_), (new_m, m_), (new_v, v_)):
        dst.update(packs[0].split(rows))

    return (loss, grad_x[:L][None], *[grads[n] for n in names], *[delta[n] for n in names],
            *[new_m[n] for n in names], *[new_v[n] for n in names])
```

```python
import functools

import numpy as np
import jax
import jax.numpy as jnp
from jax import lax
from jax.experimental import pallas as pl
from jax.experimental.pallas import tpu as pltpu
from jax.experimental.pallas import tpu_sc as plsc

F32 = jnp.float32
BF16 = jnp.bfloat16
_MXU = jnp.bfloat16
_HI = lax.Precision.HIGHEST
MESH = pl.DeviceIdType.MESH

D = 1024
HD = 64
GRID_W = 64
EPS = 1e-6
ROPE_BASE = 10000.0
WA_HEADS, WA_KV = 4, 2
WA_BLK = 128
NA_HEADS, NA_KH, NA_KW = 4, 8, 16
S_HEADS, S_P, S_INNER, S_GROUPS, S_N, S_CONV, S_Q = 8, 64, 512, 2, 128, 7, 128
D_FF = 2816
IN_COLS = 2832
IN_PAD = 2944
C_QA, C_QB, C_Z, C_KA, C_VA, C_KB, C_VB, C_XBC, C_DT = 0, 256, 512, 1024, 1152, 1280, 1536, 1792, 2816
ADAM_LR, ADAM_B1, ADAM_B2, ADAM_EPS, ADAM_WD, ADAM_STEP = 0.001, 0.9, 0.999, 1e-08, 0.01, 10

TR = 256
NEG = -1e30
VMEM_CAP = 56 * 1024 * 1024


PIN_BYTES = 256 * 1024


def _is_big(a):
    return hasattr(a, "shape") and len(a.shape) >= 2 and int(np.prod(a.shape)) * jnp.dtype(a.dtype).itemsize >= PIN_BYTES


def _pc(body, *, out_shape, pin=True, **kw):
    if not pin:
        return pl.pallas_call(body, out_shape=out_shape, **kw)
    one = isinstance(out_shape, jax.ShapeDtypeStruct)
    outs = [pltpu.HBM(s.shape, s.dtype) if _is_big(s) else s for s in ([out_shape] if one else out_shape)]
    call = pl.pallas_call(body, out_shape=outs[0] if one else outs, **kw)
    return lambda *args: call(*[pltpu.with_memory_space_constraint(a, pltpu.HBM) if _is_big(a) else a for a in args])


def _cp(sem=None, vmem=None):
    kw = {}
    if sem is not None:
        kw["dimension_semantics"] = sem
    if vmem is not None:
        kw["vmem_limit_bytes"] = int(min(max(vmem, 16 * 1024 * 1024), VMEM_CAP))
    return pltpu.CompilerParams(**kw)


def _sds(shape, dtype):
    return jax.ShapeDtypeStruct(tuple(shape), dtype)


_DIMS = {"nn": ((1,), (0,)), "nt": ((1,), (1,)), "tn": ((0,), (0,))}


def _dg(a, b, dims):
    return lax.dot_general(a.astype(_MXU), b.astype(_MXU), (dims, ((), ())), preferred_element_type=F32)


@functools.partial(jax.custom_vjp, nondiff_argnums=(2,))
def bdot(a, b, mode):
    return _dg(a, b, _DIMS[mode])


def _bdot_fwd(a, b, mode):
    return bdot(a, b, mode), (a, b)


def _bdot_bwd(mode, res, g):
    a, b = res
    if mode == "nn":
        return bdot(g, b, "nt"), bdot(a, g, "tn")
    if mode == "nt":
        return bdot(g, b, "nn"), bdot(g, a, "tn")
    return bdot(b, g, "nt"), bdot(a, g, "nn")


bdot.defvjp(_bdot_fwd, _bdot_bwd)


def hdot(a, b, mode="nn"):
    return lax.dot_general(a, b, (_DIMS[mode], ((), ())), precision=_HI, preferred_element_type=F32)


def _silu(x):
    return x / (1.0 + jnp.exp(-x))


def _softplus(x):
    return jnp.maximum(x, 0.0) + jnp.log(1.0 + jnp.exp(-jnp.abs(x)))


def _div_tile(n, cap, mult):
    if n <= cap:
        return n
    best = None
    for t in range(mult, cap + 1, mult):
        if n % t == 0:
            best = t
    assert best is not None, (n, cap, mult)
    return best


def matmul(a, b, mode, out_dtype, name, tm=640, tn=1536, tk=1408, hi=False):
    if mode == "tn":
        K, M = a.shape
    else:
        M, K = a.shape
    N = b.shape[0] if mode == "nt" else b.shape[1]
    tm = _div_tile(M, tm, 128 if mode == "tn" else 16)
    tn = _div_tile(N, tn, 128)
    tk = _div_tile(K, tk, 128 if mode != "tn" else 16)
    nk = K // tk
    dims = _DIMS[mode]

    def body(a_ref, b_ref, o_ref, *acc):
        if hi:
            part = lax.dot_general(a_ref[...], b_ref[...], (dims, ((), ())), precision=_HI, preferred_element_type=F32)
        else:
            part = _dg(a_ref[...], b_ref[...], dims)
        if nk == 1:
            o_ref[...] = part.astype(o_ref.dtype)
        else:
            k = pl.program_id(2)

            @pl.when(k == 0)
            def _():
                acc[0][...] = part

            @pl.when(k > 0)
            def _():
                acc[0][...] += part

            @pl.when(k == nk - 1)
            def _():
                o_ref[...] = acc[0][...].astype(o_ref.dtype)

    if mode == "tn":
        a_spec = pl.BlockSpec((tk, tm), lambda i, j, k: (k, i))
    else:
        a_spec = pl.BlockSpec((tm, tk), lambda i, j, k: (i, k))
    if mode == "nt":
        b_spec = pl.BlockSpec((tn, tk), lambda i, j, k: (j, k))
    else:
        b_spec = pl.BlockSpec((tk, tn), lambda i, j, k: (k, j))
    isz = lambda x: jnp.dtype(x.dtype).itemsize
    vmem = 2 * (tm * tk * isz(a) + tk * tn * isz(b) + tm * tn * jnp.dtype(out_dtype).itemsize) + 3 * tm * tn * 4
    return _pc(
        body, name=name, grid=(M // tm, N // tn, nk),
        in_specs=[a_spec, b_spec], out_specs=pl.BlockSpec((tm, tn), lambda i, j, k: (i, j)),
        out_shape=_sds((M, N), out_dtype),
        scratch_shapes=[pltpu.VMEM((tm, tn), F32)] if nk > 1 else [],
        compiler_params=_cp(("parallel", "parallel", "arbitrary"), vmem + (8 << 20)),
    )(a, b)


def _norm_mod(xo, shift, scale, g):
    r = lax.rsqrt(jnp.mean(xo * xo, axis=-1, keepdims=True) + EPS)
    return (xo * r) * g * (1.0 + scale) + shift


def res_norm_mod(x, y, gsv, g, nL, name):
    T = x.shape[0]
    has_y = y is not None

    def body(*refs):
        if has_y:
            x_ref, y_ref, gsv_ref, g_ref, xo_ref, h_ref = refs
            xo = x_ref[...] + gsv_ref[0, 0:1, :] * y_ref[...]
            xo_ref[...] = xo
        else:
            x_ref, gsv_ref, g_ref, h_ref = refs
            xo = x_ref[...]
        h_ref[...] = _norm_mod(xo, gsv_ref[0, 1:2, :], gsv_ref[0, 2:3, :], g_ref[...]).astype(h_ref.dtype)

    row = pl.BlockSpec((TR, D), lambda i: (i, 0))
    in_specs = [row] + ([row] if has_y else []) + [pl.BlockSpec((1, 8, D), lambda i: (i // nL, 0, 0)),
                                                     pl.BlockSpec((1, D), lambda i: (0, 0))]
    out_specs = ([row] if has_y else []) + [row]
    out_shape = ([_sds((T, D), F32)] if has_y else []) + [_sds((T, D), BF16)]
    args = (x, y, gsv, g) if has_y else (x, gsv, g)
    outs = _pc(body, name=name, grid=(T // TR,), in_specs=in_specs, out_specs=out_specs, out_shape=out_shape,
               compiler_params=_cp(("arbitrary",), 24 << 20))(*args)
    return (outs[0], outs[1]) if has_y else (None, outs[0])


def res_norm_mod_bwd(xo, y, gsv, g, dh, dres, nL, name):
    T = xo.shape[0]
    has_y = y is not None

    def body(*refs):
        if has_y:
            xo_ref, y_ref, gsv_ref, g_ref, dh_ref, dres_ref, dx_ref, dy_ref, dgsv_ref, dg_ref = refs
        else:
            xo_ref, gsv_ref, g_ref, dh_ref, dres_ref, dx_ref, dgsv_ref, dg_ref = refs
        i = pl.program_id(0)

        @pl.when((i == 0) | (i == nL))
        def _():
            dgsv_ref[...] = jnp.zeros_like(dgsv_ref)

        @pl.when(i == 0)
        def _():
            dg_ref[...] = jnp.zeros_like(dg_ref)

        _, vjp = jax.vjp(_norm_mod, xo_ref[...], gsv_ref[0, 1:2, :], gsv_ref[0, 2:3, :], g_ref[...])
        dxn, dshift, dscale, dg = vjp(dh_ref[...].astype(F32))
        dxo = dres_ref[...] + dxn
        dx_ref[...] = dxo
        if has_y:
            dy_ref[...] = (gsv_ref[0, 0:1, :] * dxo).astype(dy_ref.dtype)
            dgsv_ref[0, 0:1, :] += jnp.sum(y_ref[...] * dxo, axis=0, keepdims=True)
        dgsv_ref[0, 1:2, :] += dshift
        dgsv_ref[0, 2:3, :] += dscale
        dg_ref[0:1, :] += dg

    row = pl.BlockSpec((TR, D), lambda i: (i, 0))
    gspec = pl.BlockSpec((1, 8, D), lambda i: (i // nL, 0, 0))
    in_specs = [row] + ([row] if has_y else []) + [gspec, pl.BlockSpec((1, D), lambda i: (0, 0)), row, row]
    out_specs = [row] + ([row] if has_y else []) + [gspec, pl.BlockSpec((8, D), lambda i: (0, 0))]
    out_shape = [_sds((T, D), F32)] + ([_sds((T, D), BF16)] if has_y else []) + [_sds((2, 8, D), F32), _sds((8, D), F32)]
    args = (xo, y, gsv, g, dh, dres) if has_y else (xo, gsv, g, dh, dres)
    outs = _pc(body, name=name, grid=(T // TR,), in_specs=in_specs, out_specs=out_specs, out_shape=out_shape,
               compiler_params=_cp(("arbitrary",), 32 << 20))(*args)
    if has_y:
        return outs
    return outs[0], None, outs[1], outs[2]


def final_loss(x, y, gsv, g, target, nL, name):
    T = x.shape[0]

    def lossf(xo, gv, t):
        yn = (xo * lax.rsqrt(jnp.mean(xo * xo, axis=-1, keepdims=True) + EPS)) * gv
        e = yn - t
        return 0.5 * jnp.sum(jnp.sum(e * e, axis=-1, keepdims=True) * (1.0 / D), axis=0, keepdims=True)

    def body(x_ref, y_ref, gsv_ref, g_ref, t_ref, loss_ref, dx_ref, dy_ref, dgsv_ref, dg_ref):
        i = pl.program_id(0)

        @pl.when(i == 0)
        def _():
            loss_ref[...] = jnp.zeros_like(loss_ref)
            dg_ref[...] = jnp.zeros_like(dg_ref)

        @pl.when((i == 0) | (i == nL))
        def _():
            dgsv_ref[...] = jnp.zeros_like(dgsv_ref)

        @pl.when(i < nL)
        def _():
            gate = gsv_ref[0, 0:1, :]
            yv = y_ref[...]
            xo = x_ref[...] + gate * yv
            lv, vjp = jax.vjp(lossf, xo, g_ref[...], t_ref[...])
            dxo, dg, _ = vjp(jnp.ones((1, 1), F32))
            loss_ref[...] += jnp.broadcast_to(lv, loss_ref.shape)
            dx_ref[...] = dxo
            dy_ref[...] = (gate * dxo).astype(dy_ref.dtype)
            dgsv_ref[0, 0:1, :] += jnp.sum(yv * dxo, axis=0, keepdims=True)
            dg_ref[0:1, :] += dg

        @pl.when(i >= nL)
        def _():
            dx_ref[...] = jnp.zeros_like(dx_ref)
            dy_ref[...] = jnp.zeros_like(dy_ref)

    row = pl.BlockSpec((TR, D), lambda i: (i, 0))
    gspec = pl.BlockSpec((1, 8, D), lambda i: (i // nL, 0, 0))
    return _pc(
        body, name=name, grid=(T // TR,),
        in_specs=[row, row, gspec, pl.BlockSpec((1, D), lambda i: (0, 0)),
                  pl.BlockSpec((TR, D), lambda i: (jnp.minimum(i, nL - 1), 0))],
        out_specs=[pl.BlockSpec((8, 128), lambda i: (0, 0)), row, row, gspec, pl.BlockSpec((8, D), lambda i: (0, 0))],
        out_shape=[_sds((8, 128), F32), _sds((T, D), F32), _sds((T, D), BF16), _sds((2, 8, D), F32), _sds((8, D), F32)],
        compiler_params=_cp(("arbitrary",), 32 << 20),
    )(x, y, gsv, g, target)


FI_BLK = 2 * D_FF // 4


def _fi_chip(j):
    return (j % 2) * 2 + j // 2


def matmul_fi(a, b, mode, out_dtype, name):
    T = a.shape[0]
    if mode == "tn":
        tmd = 512

        def body(a_ref, b_ref, o_ref):
            o_ref[0] = _dg(a_ref[...], b_ref[...], _DIMS["tn"]).astype(o_ref.dtype)

        return _pc(body, name=name, grid=(D // tmd, 4),
                   in_specs=[pl.BlockSpec((T, tmd), lambda i, j: (0, i)), pl.BlockSpec((T, FI_BLK), lambda i, j: (0, j))],
                   out_specs=pl.BlockSpec((1, tmd, FI_BLK), lambda i, j: (_fi_chip(j), i, 0)),
                   out_shape=_sds((4, D, FI_BLK), out_dtype), compiler_params=_cp(("parallel", "arbitrary"), 48 << 20))(a, b)
    if mode == "nn":
        tm = _div_tile(T, 1280, 16)

        def body(a_ref, b_ref, o_ref):
            o_ref[...] = _dg(a_ref[...], b_ref[0], _DIMS["nn"]).astype(o_ref.dtype)

        return _pc(body, name=name, grid=(T // tm, 4),
                   in_specs=[pl.BlockSpec((tm, D), lambda i, j: (i, 0)), pl.BlockSpec((1, D, FI_BLK), lambda i, j: (_fi_chip(j), 0, 0))],
                   out_specs=pl.BlockSpec((tm, FI_BLK), lambda i, j: (i, j)), out_shape=_sds((T, 4 * FI_BLK), out_dtype),
                   compiler_params=_cp(("parallel", "arbitrary"), 40 << 20))(a, b)
    tm = _div_tile(T, 640, 16)

    def body(a_ref, b_ref, o_ref):
        acc = None
        for k in range(4):
            part = _dg(a_ref[:, k * FI_BLK:(k + 1) * FI_BLK], b_ref[_fi_chip(k)], _DIMS["nt"])
            acc = part if acc is None else acc + part
        o_ref[...] = acc.astype(o_ref.dtype)

    return _pc(body, name=name, grid=(T // tm,),
               in_specs=[pl.BlockSpec((tm, 4 * FI_BLK), lambda i: (i, 0)), pl.BlockSpec((4, D, FI_BLK), lambda i: (0, 0, 0))],
               out_specs=pl.BlockSpec((tm, D), lambda i: (i, 0)), out_shape=_sds((T, D), out_dtype),
               compiler_params=_cp(("parallel",), VMEM_CAP))(a, b)


def _swiglu(gate, up):
    return _silu(gate) * up


def swiglu_fwd(gu, name):
    T = gu.shape[0]

    def body(x_ref, o_ref):
        o_ref[...] = _swiglu(x_ref[:, :FI_BLK].astype(F32), x_ref[:, FI_BLK:].astype(F32)).astype(o_ref.dtype)

    return _pc(body, name=name, grid=(T // TR, 2), in_specs=[pl.BlockSpec((TR, 2 * FI_BLK), lambda i, j: (i, j))],
               out_specs=pl.BlockSpec((TR, FI_BLK), lambda i, j: (i, j)), out_shape=_sds((T, D_FF), BF16),
               compiler_params=_cp(("parallel", "parallel"), 24 << 20))(gu)


def swiglu_bwd(gu, dact, name):
    T = gu.shape[0]

    def body(x_ref, d_ref, o_ref):
        _, vjp = jax.vjp(_swiglu, x_ref[:, :FI_BLK].astype(F32), x_ref[:, FI_BLK:].astype(F32))
        dg, du = vjp(d_ref[...].astype(F32))
        o_ref[:, :FI_BLK] = dg.astype(o_ref.dtype)
        o_ref[:, FI_BLK:] = du.astype(o_ref.dtype)

    return _pc(body, name=name, grid=(T // TR, 2),
               in_specs=[pl.BlockSpec((TR, 2 * FI_BLK), lambda i, j: (i, j)), pl.BlockSpec((TR, FI_BLK), lambda i, j: (i, j))],
               out_specs=pl.BlockSpec((TR, 2 * FI_BLK), lambda i, j: (i, j)), out_shape=_sds((T, 2 * D_FF), BF16),
               compiler_params=_cp(("parallel", "parallel"), 32 << 20))(gu, dact)


def rope_tables(L, Lc):
    t = np.arange(L)
    rows, cols = t // GRID_W, t % GRID_W
    inv = ROPE_BASE ** (-np.arange(16, dtype=np.float32) / 16)
    lane = np.arange(64)
    pos = np.where((lane // 32)[None, :] == 0, rows[:, None], cols[:, None]).astype(np.float32)
    ang = jnp.asarray(pos) * jnp.asarray(inv[lane % 16])[None, :]
    cos = jnp.concatenate([jnp.cos(ang), jnp.ones((Lc, 64), F32)], axis=0)
    sin = jnp.concatenate([jnp.sin(ang), jnp.zeros((Lc, 64), F32)], axis=0)
    R = np.zeros((128, 128), np.float32)
    for i in range(128):
        if (i % 32) < 16:
            R[i + 16, i] = -1.0
        else:
            R[i - 16, i] = 1.0
    return jnp.tile(cos, (1, 2)), jnp.tile(sin, (1, 2)), jnp.asarray(R)


def rope_apply(q_src, q_col, k_src, k_col, cos, sin, R, transpose, name, kv_src=None):
    T = cos.shape[0]
    with_kv = kv_src is not None

    def rot(x, c, s, Rm):
        if transpose:
            return x * c + hdot(x * s, Rm, "nt")
        return x * c + hdot(x, Rm) * s

    def body(q_ref, k_ref, c_ref, s_ref, R_ref, *rest):
        qo_ref, ko_ref = rest[-4:-2] if with_kv else rest
        c, s, Rm = c_ref[...], s_ref[...], R_ref[...]
        for j in range(2):
            qo_ref[:, j * 128:(j + 1) * 128] = rot(q_ref[:, j * 128:(j + 1) * 128].astype(F32), c, s, Rm).astype(qo_ref.dtype)
        ko_ref[...] = rot(k_ref[...].astype(F32), c, s, Rm).astype(ko_ref.dtype)
        if with_kv:
            rest[-2][...] = rest[0][...].astype(BF16)
            rest[-1][...] = rest[1][...].astype(BF16)

    tab = pl.BlockSpec((TR, 128), lambda i: (i, 0))
    wide = pl.BlockSpec((TR, 256), lambda i: (i, 0))
    kv_in = [pl.BlockSpec((TR, 256), lambda i: (i, C_KB // 256)), pl.BlockSpec((TR, 256), lambda i: (i, C_VB // 256))] if with_kv else []
    return _pc(body, name=name, grid=(T // TR,),
               in_specs=[pl.BlockSpec((TR, 256), lambda i: (i, q_col)), pl.BlockSpec((TR, 128), lambda i: (i, k_col)),
                         tab, tab, pl.BlockSpec((128, 128), lambda i: (0, 0))] + kv_in,
               out_specs=[wide, tab] + ([wide, wide] if with_kv else []),
               out_shape=[_sds((T, 256), BF16), _sds((T, 128), BF16)] + ([_sds((T, 256), BF16)] * 2 if with_kv else []),
               compiler_params=_cp(("parallel",), 16 << 20))(q_src, k_src, cos, sin, R, *([kv_src, kv_src] if with_kv else []))


_SCALE = HD ** -0.5


def _attn_tile(qh, ks, vs, extra):
    ss = []
    for k, add in ks:
        s = bdot(qh, k, "nt") * _SCALE
        ss.append(s if add is None else s + add)
    m = ss[0].max(axis=-1, keepdims=True)
    for s in ss[1:]:
        m = jnp.maximum(m, s.max(axis=-1, keepdims=True))
    if extra is not None:
        m = jnp.maximum(m, extra)
    ps = [jnp.exp(s - m) for s in ss]
    den = ps[0].sum(axis=-1, keepdims=True)
    for p in ps[1:]:
        den = den + p.sum(axis=-1, keepdims=True)
    if extra is not None:
        den = den + jnp.exp(extra - m)
    num = bdot(ps[0], vs[0], "nn")
    for p, v in zip(ps[1:], vs[1:]):
        num = num + bdot(p, v, "nn")
    return num / den


def _wa_mask(n, L):
    qpos = n * WA_BLK + lax.broadcasted_iota(jnp.int32, (WA_BLK, 3 * WA_BLK), 0)
    kpos = (n - 1) * WA_BLK + lax.broadcasted_iota(jnp.int32, (WA_BLK, 3 * WA_BLK), 1)
    ok = (jnp.abs(qpos - kpos) <= WA_BLK) & (kpos >= 0) & (kpos < L)
    return jnp.where(ok, 0.0, NEG).astype(F32)


def _wa_specs(L, Lc):
    nb = L // WA_BLK
    cb = L // Lc
    lat = lambda n: jnp.minimum(n, nb - 1)
    prv = lambda n: jnp.clip(n - 1, 0, nb - 1)
    nxt = lambda n: jnp.minimum(n + 1, nb - 1)
    kspecs = [pl.BlockSpec((WA_BLK, 128), lambda n: (prv(n), 0)), pl.BlockSpec((WA_BLK, 128), lambda n: (lat(n), 0)),
              pl.BlockSpec((WA_BLK, 128), lambda n: (nxt(n), 0)), pl.BlockSpec((Lc, 128), lambda n: (cb, 0))]
    vcol = C_VA // 128
    vspecs = [pl.BlockSpec((WA_BLK, 128), lambda n: (prv(n), vcol)), pl.BlockSpec((WA_BLK, 128), lambda n: (lat(n), vcol)),
              pl.BlockSpec((WA_BLK, 128), lambda n: (nxt(n), vcol)), pl.BlockSpec((Lc, 128), lambda n: (cb, vcol))]
    return nb, kspecs, vspecs


def win_attn_fwd(qr, kr, P, sink, L, Lc, name):
    T = L + Lc
    nb, kspecs, vspecs = _wa_specs(L, Lc)

    def body(q_ref, kp, kc, kn, kx, vp, vc, vn, vx, s_ref, o_ref):
        n = pl.program_id(0)

        @pl.when(n < nb)
        def _():
            mask = _wa_mask(n, L)
            for g in range(WA_KV):
                sl = slice(g * HD, (g + 1) * HD)
                k3 = jnp.concatenate([kp[:, sl], kc[:, sl], kn[:, sl]], axis=0)
                v3 = jnp.concatenate([vp[:, sl], vc[:, sl], vn[:, sl]], axis=0)
                for r in range(2):
                    h = 2 * g + r
                    o = _attn_tile(q_ref[:, h * HD:(h + 1) * HD], [(k3, mask), (kx[:, sl], None)], [v3, vx[:, sl]],
                                   s_ref[h:h + 1, 0:1])
                    o_ref[:, h * HD:(h + 1) * HD] = o.astype(o_ref.dtype)

        @pl.when(n >= nb)
        def _():
            for h in range(WA_HEADS):
                sl = slice((h // 2) * HD, (h // 2 + 1) * HD)
                o = _attn_tile(q_ref[:, h * HD:(h + 1) * HD], [(kx[:, sl], None)], [vx[:, sl]], s_ref[h:h + 1, 0:1])
                o_ref[:, h * HD:(h + 1) * HD] = o.astype(o_ref.dtype)

    qspec = pl.BlockSpec((WA_BLK, 256), lambda n: (n, 0))
    return _pc(body, name=name, grid=(T // WA_BLK,),
               in_specs=[qspec] + kspecs + vspecs + [pl.BlockSpec((8, 128), lambda n: (0, 0))],
               out_specs=qspec, out_shape=_sds((T, 256), BF16),
               compiler_params=_cp(("arbitrary",), 32 << 20))(qr, kr, kr, kr, kr, P, P, P, P, sink)


def win_attn_bwd(qr, kr, P, sink, do_src, L, Lc, name):
    T = L + Lc
    nb, kspecs, vspecs = _wa_specs(L, Lc)
    cx = WA_BLK + L

    def body(q_ref, kp, kc, kn, kx, vp, vc, vn, vx, s_ref, do_ref, dq_ref, dk_ref, dv_ref, ds_ref):
        n = pl.program_id(0)

        @pl.when(n == 0)
        def _():
            dk_ref[...] = jnp.zeros_like(dk_ref)
            dv_ref[...] = jnp.zeros_like(dv_ref)
            ds_ref[...] = jnp.zeros_like(ds_ref)

        @pl.when(n < nb)
        def _():
            mask = _wa_mask(n, L)
            rows = pl.ds(pl.multiple_of(n * WA_BLK, WA_BLK), 3 * WA_BLK)
            for g in range(WA_KV):
                sl = slice(g * HD, (g + 1) * HD)
                k3 = jnp.concatenate([kp[:, sl], kc[:, sl], kn[:, sl]], axis=0)
                v3 = jnp.concatenate([vp[:, sl], vc[:, sl], vn[:, sl]], axis=0)
                kxg, vxg = kx[:, sl], vx[:, sl]
                acc = None
                for r in range(2):
                    h = 2 * g + r
                    hs = slice(h * HD, (h + 1) * HD)
                    f = lambda q, k3_, v3_, kx_, vx_, s_: _attn_tile(q, [(k3_, mask), (kx_, None)], [v3_, vx_], s_)
                    _, vjp = jax.vjp(f, q_ref[:, hs].astype(F32), k3.astype(F32), v3.astype(F32), kxg.astype(F32),
                                     vxg.astype(F32), s_ref[h:h + 1, 0:1])
                    dq, dk3, dv3, dkx, dvx, dsk = vjp(do_ref[:, hs].astype(F32))
                    dq_ref[:, hs] = dq
                    ds_ref[h:h + 1, :] += jnp.broadcast_to(dsk, (1, 128))
                    acc = (dk3, dv3, dkx, dvx) if acc is None else tuple(a + b for a, b in zip(acc, (dk3, dv3, dkx, dvx)))
                dk_ref[rows, sl] += acc[0]
                dv_ref[rows, sl] += acc[1]
                dk_ref[cx:cx + Lc, sl] += acc[2]
                dv_ref[cx:cx + Lc, sl] += acc[3]

        @pl.when(n >= nb)
        def _():
            for h in range(WA_HEADS):
                sl = slice((h // 2) * HD, (h // 2 + 1) * HD)
                hs = slice(h * HD, (h + 1) * HD)
                f = lambda q, kx_, vx_, s_: _attn_tile(q, [(kx_, None)], [vx_], s_)
                _, vjp = jax.vjp(f, q_ref[:, hs].astype(F32), kx[:, sl].astype(F32), vx[:, sl].astype(F32), s_ref[h:h + 1, 0:1])
                dq, dkx, dvx, dsk = vjp(do_ref[:, hs].astype(F32))
                dq_ref[:, hs] = dq
                ds_ref[h:h + 1, :] += jnp.broadcast_to(dsk, (1, 128))
                dk_ref[cx:cx + Lc, sl] += dkx
                dv_ref[cx:cx + Lc, sl] += dvx

    qspec = pl.BlockSpec((WA_BLK, 256), lambda n: (n, 0))
    acc_spec = pl.BlockSpec((T + 2 * WA_BLK, 128), lambda n: (0, 0))
    return _pc(body, name=name, grid=(T // WA_BLK,),
               in_specs=[qspec] + kspecs + vspecs + [pl.BlockSpec((8, 128), lambda n: (0, 0)), qspec],
               out_specs=[qspec, acc_spec, acc_spec, pl.BlockSpec((8, 128), lambda n: (0, 0))],
               out_shape=[_sds((T, 256), F32), _sds((T + 2 * WA_BLK, 128), F32), _sds((T + 2 * WA_BLK, 128), F32), _sds((8, 128), F32)],
               compiler_params=_cp(("arbitrary",), 40 << 20))(qr, kr, kr, kr, kr, P, P, P, P, sink, do_src)


def na_index_tables():
    qc = np.arange(GRID_W)[:, None]
    kc = np.arange(GRID_W)[None, :]
    cstart = np.clip(qc - NA_KW // 2, 0, GRID_W - NA_KW)
    ok = (kc >= cstart) & (kc < cstart + NA_KW)
    dx = np.clip(kc - qc, -(NA_KW - 1), NA_KW - 1) + (NA_KW - 1)
    off = np.arange(NA_KH)[:, None]
    kr = np.arange(NA_KH)[None, :]
    dy = kr - off + (NA_KH - 1)
    return ok, dx, dy


def _na_selectors():
    ok, dx, dy = na_index_tables()
    e1 = np.zeros((GRID_W * GRID_W, 128), np.float32)
    qi, ki = np.nonzero(ok)
    e1[qi * GRID_W + ki, dx[qi, ki]] = 1.0
    e2 = np.zeros((16, NA_KH * NA_KH), np.float32)
    oi, ri = np.meshgrid(np.arange(NA_KH), np.arange(NA_KH), indexing="ij")
    e2[dy[oi, ri].ravel(), (oi * NA_KH + ri).ravel()] = 1.0
    return ok, jnp.asarray(e1), jnp.asarray(np.kron(np.eye(NA_HEADS, dtype=np.float32), e2))


def na_bias_table(rpb, tag):
    ok, e1, e2 = _na_selectors()
    r2 = jnp.pad(rpb.astype(F32), ((0, 0), (0, 1), (0, 128 - (2 * NA_KW - 1)))).reshape(NA_HEADS * 16, 128)
    r1 = matmul(e2, r2, "tn", F32, f"na_bias_sel1_{tag}", hi=True)
    x = matmul(r1, e1, "nt", F32, f"na_bias_sel2_{tag}", hi=True)
    b = x.reshape(NA_HEADS, NA_KH, NA_KH, GRID_W, GRID_W).transpose(0, 1, 3, 2, 4)
    b = b + jnp.asarray(np.where(ok, 0.0, NEG).astype(np.float32))[None, None, :, None, :]
    return b.reshape(NA_HEADS, NA_KH, GRID_W, NA_KH * GRID_W)


def _na_rows(r, GR):
    r0 = jnp.clip(r - NA_KH // 2, 0, GR - NA_KH)
    return r0, jnp.clip(r - r0, 0, NA_KH - 1)


NA_RPS = 2


def na_fwd(P, kb, vb, bias, L, Lc, name):
    T = L + Lc
    GR = L // GRID_W
    W = NA_KH * GRID_W
    QB = GRID_W * NA_RPS
    nlat = GR // NA_RPS

    def body(q_ref, k_ref, v_ref, b_ref, o_ref):
        s = pl.program_id(0)

        @pl.when(s < nlat)
        def _():
            for rr in range(NA_RPS):
                r0, off = _na_rows(s * NA_RPS + rr, GR)
                rows = pl.ds(pl.multiple_of(r0 * GRID_W, GRID_W), W)
                qs = slice(rr * GRID_W, (rr + 1) * GRID_W)
                for h in range(NA_HEADS):
                    hs = slice(h * HD, (h + 1) * HD)
                    o = _attn_tile(q_ref[qs, hs], [(k_ref[rows, hs], b_ref[h, off]), (k_ref[L:T, hs], None)],
                                   [v_ref[rows, hs], v_ref[L:T, hs]], None)
                    o_ref[qs, hs] = o.astype(o_ref.dtype)

        @pl.when(s >= nlat)
        def _():
            for h in range(NA_HEADS):
                hs = slice(h * HD, (h + 1) * HD)
                o = _attn_tile(q_ref[:, hs], [(k_ref[L:T, hs], None)], [v_ref[L:T, hs]], None)
                o_ref[:, hs] = o.astype(o_ref.dtype)

    one = pl.Buffered(1)
    return _pc(body, name=name, grid=(T // QB,),
               in_specs=[pl.BlockSpec((QB, 256), lambda r: (r, C_QB // 256)),
                         pl.BlockSpec((T, 256), lambda r: (0, 0), pipeline_mode=one),
                         pl.BlockSpec((T, 256), lambda r: (0, 0), pipeline_mode=one),
                         pl.BlockSpec((NA_HEADS, NA_KH, GRID_W, W), lambda r: (0, 0, 0, 0), pipeline_mode=one)],
               out_specs=pl.BlockSpec((QB, 256), lambda r: (r, 0)), out_shape=_sds((T, 256), BF16),
               compiler_params=_cp(("arbitrary",), 32 << 20))(P, kb, vb, bias)


def na_bwd(P, kb, vb, bias, do_src, L, Lc, name):
    T = L + Lc
    GR = L // GRID_W
    W = NA_KH * GRID_W
    QB = GRID_W * NA_RPS
    nlat = GR // NA_RPS

    def body(q_ref, k_ref, v_ref, b_ref, do_ref, dq_ref, dk_ref, dv_ref, db_ref):
        s = pl.program_id(0)

        @pl.when(s == 0)
        def _():
            dk_ref[...] = jnp.zeros_like(dk_ref)
            dv_ref[...] = jnp.zeros_like(dv_ref)
            db_ref[...] = jnp.zeros_like(db_ref)

        @pl.when(s < nlat)
        def _():
            for rr in range(NA_RPS):
                r0, off = _na_rows(s * NA_RPS + rr, GR)
                rows = pl.ds(pl.multiple_of(r0 * GRID_W, GRID_W), W)
                qs = slice(rr * GRID_W, (rr + 1) * GRID_W)
                for h in range(NA_HEADS):
                    hs = slice(h * HD, (h + 1) * HD)
                    f = lambda q, kw, vw, kx, vx, b: _attn_tile(q, [(kw, b), (kx, None)], [vw, vx], None)
                    _, vjp = jax.vjp(f, q_ref[qs, hs].astype(F32), k_ref[rows, hs].astype(F32), v_ref[rows, hs].astype(F32),
                                     k_ref[L:T, hs].astype(F32), v_ref[L:T, hs].astype(F32), b_ref[h, off])
                    dq, dkw, dvw, dkx, dvx, db = vjp(do_ref[qs, hs].astype(F32))
                    dq_ref[qs, hs] = dq.astype(dq_ref.dtype)
                    dk_ref[rows, hs] += dkw
                    dv_ref[rows, hs] += dvw
                    dk_ref[L:T, hs] += dkx
                    dv_ref[L:T, hs] += dvx
                    db_ref[h, off] += db

        @pl.when(s >= nlat)
        def _():
            for h in range(NA_HEADS):
                hs = slice(h * HD, (h + 1) * HD)
                f = lambda q, kx, vx: _attn_tile(q, [(kx, None)], [vx], None)
                _, vjp = jax.vjp(f, q_ref[:, hs].astype(F32), k_ref[L:T, hs].astype(F32), v_ref[L:T, hs].astype(F32))
                dq, dkx, dvx = vjp(do_ref[:, hs].astype(F32))
                dq_ref[:, hs] = dq.astype(dq_ref.dtype)
                dk_ref[L:T, hs] += dkx
                dv_ref[L:T, hs] += dvx

    one = pl.Buffered(1)
    full = lambda shape: pl.BlockSpec(shape, lambda r: (0,) * len(shape), pipeline_mode=one)
    return _pc(body, name=name, grid=(T // QB,),
               in_specs=[pl.BlockSpec((QB, 256), lambda r: (r, C_QB // 256)), full((T, 256)), full((T, 256)),
                         full((NA_HEADS, NA_KH, GRID_W, W)), pl.BlockSpec((QB, 256), lambda r: (r, 1))],
               out_specs=[pl.BlockSpec((QB, 256), lambda r: (r, 0)), full((T, 256)), full((T, 256)),
                          full((NA_HEADS, NA_KH, GRID_W, W))],
               out_shape=[_sds((T, 256), BF16), _sds((T, 256), F32), _sds((T, 256), F32), _sds((NA_HEADS, NA_KH, GRID_W, W), F32)],
               compiler_params=_cp(("arbitrary",), 48 << 20))(P, kb, vb, bias, do_src)


def na_rpb_grad(dbias, tag):
    _, e1, e2 = _na_selectors()
    x = dbias.reshape(NA_HEADS, NA_KH, GRID_W, NA_KH, GRID_W).transpose(0, 1, 3, 2, 4).reshape(NA_HEADS * NA_KH * NA_KH, GRID_W * GRID_W)
    r1 = matmul(x, e1, "nn", F32, f"na_rpb_sel1_{tag}", hi=True, tk=1024)
    r2 = matmul(e2, r1, "nn", F32, f"na_rpb_sel2_{tag}", hi=True)
    return r2.reshape(NA_HEADS, 16, 128)[:, :2 * NA_KH - 1, :2 * NA_KW - 1]


_HALO = 8


def _halo_specs(T, col0):
    nh = TR // _HALO
    cur = pl.BlockSpec((TR, 256), lambda i, j: (i, col0 + j))
    prv = pl.BlockSpec((_HALO, 256), lambda i, j: (jnp.maximum(i * nh - 1, 0), col0 + j))
    nxt = pl.BlockSpec((_HALO, 256), lambda i, j: (jnp.minimum((i + 1) * nh, T // _HALO - 1), col0 + j))
    return prv, cur, nxt


def _fill_ext(ext, prv, cur, nxt, i, nL, nT):
    has_prev = jnp.where((i != 0) & (i != nL), 1.0, 0.0)
    has_next = jnp.where((i != nL - 1) & (i != nT - 1), 1.0, 0.0)
    ext[0:_HALO, :] = prv[...].astype(F32) * has_prev
    ext[_HALO:_HALO + TR, :] = cur[...].astype(F32)
    ext[_HALO + TR:, :] = nxt[...].astype(F32) * has_next


def conv_silu_fwd(P, w8, b, nL, name):
    T = P.shape[0]
    nT = T // TR

    def body(prv, cur, nxt, w_ref, b_ref, pre_ref, act_ref, ext):
        i = pl.program_id(0)
        _fill_ext(ext, prv, cur, nxt, i, nL, nT)
        y = jnp.broadcast_to(b_ref[...], (TR, 256))
        for k in range(S_CONV):
            y = y + w_ref[k:k + 1, :] * ext[pl.ds(_HALO - S_CONV // 2 + k, TR), :]
        pre_ref[...] = y
        act_ref[...] = _silu(y)

    prv, cur, nxt = _halo_specs(T, C_XBC // 256)
    out = pl.BlockSpec((TR, 256), lambda i, j: (i, j))
    return _pc(body, name=name, grid=(nT, 4),
               in_specs=[prv, cur, nxt, pl.BlockSpec((8, 256), lambda i, j: (0, j)), pl.BlockSpec((1, 256), lambda i, j: (0, j))],
               out_specs=[out, out], out_shape=[_sds((T, 1024), F32), _sds((T, 1024), F32)],
               scratch_shapes=[pltpu.VMEM((TR + 2 * _HALO, 256), F32)],
               compiler_params=_cp(("parallel", "parallel"), 16 << 20))(P, P, P, w8, b)


def dsilu(pre, dxs_list, db_list, dc_list, name):
    T = pre.shape[0]
    n1, n2, n3 = len(dxs_list), len(db_list), len(dc_list)

    def body(*refs):
        pre_ref = refs[0]
        ins = refs[1:1 + n1 + n2 + n3]
        out = refs[-1]

        def part(rs, lo, hi):
            g = rs[0][...].astype(F32)
            for r in rs[1:]:
                g = g + r[...].astype(F32)
            _, vjp = jax.vjp(_silu, pre_ref[:, lo:hi])
            out[:, lo:hi] = vjp(g)[0]

        part(ins[:n1], 0, 512)
        part(ins[n1:n1 + n2], 512, 768)
        part(ins[n1 + n2:], 768, 1024)

    spec = lambda w: pl.BlockSpec((TR, w), lambda i: (i, 0))
    return _pc(body, name=name, grid=(T // TR,),
               in_specs=[spec(1024)] + [spec(512)] * n1 + [spec(256)] * (n2 + n3),
               out_specs=spec(1024), out_shape=_sds((T, 1024), F32),
               compiler_params=_cp(("parallel",), 32 << 20))(pre, *dxs_list, *db_list, *dc_list)


def conv_bwd(dpre, P, w8, nL, name):
    T = P.shape[0]
    nT = T // TR

    def body(dp, dc, dn, xp, xc, xn, w_ref, dx_ref, dw_ref, db_ref, extd, extx):
        i = pl.program_id(1)
        _fill_ext(extd, dp, dc, dn, i, nL, nT)
        _fill_ext(extx, xp, xc, xn, i, nL, nT)

        @pl.when(i == 0)
        def _():
            dw_ref[...] = jnp.zeros_like(dw_ref)
            db_ref[...] = jnp.zeros_like(db_ref)

        d = dc[...]
        dx = jnp.zeros((TR, 256), F32)
        for k in range(S_CONV):
            dx = dx + w_ref[k:k + 1, :] * extd[pl.ds(_HALO + S_CONV // 2 - k, TR), :]
            dw_ref[k:k + 1, :] += jnp.sum(d * extx[pl.ds(_HALO - S_CONV // 2 + k, TR), :], axis=0, keepdims=True)
        dx_ref[...] = dx.astype(dx_ref.dtype)
        db_ref[0:1, :] += jnp.sum(d, axis=0, keepdims=True)

    def swap(spec):
        f = spec.index_map
        return pl.BlockSpec(spec.block_shape, lambda j, i: f(i, j))

    dprv, dcur, dnxt = [swap(s) for s in _halo_specs(T, 0)]
    xprv, xcur, xnxt = [swap(s) for s in _halo_specs(T, C_XBC // 256)]
    acc = pl.BlockSpec((8, 256), lambda j, i: (0, j))
    return _pc(body, name=name, grid=(4, nT),
               in_specs=[dprv, dcur, dnxt, xprv, xcur, xnxt, acc],
               out_specs=[pl.BlockSpec((TR, 256), lambda j, i: (i, j)), acc, acc],
               out_shape=[_sds((T, 1024), BF16), _sds((8, 1024), F32), _sds((8, 1024), F32)],
               scratch_shapes=[pltpu.VMEM((TR + 2 * _HALO, 256), F32), pltpu.VMEM((TR + 2 * _HALO, 256), F32)],
               compiler_params=_cp(("parallel", "arbitrary"), 16 << 20))(dpre, dpre, dpre, P, P, P, w8)


def _onehot_row(h, n):
    return (lax.broadcasted_iota(jnp.int32, (1, n), 1) == h).astype(F32)


def _onehot_col(h, n):
    return (lax.broadcasted_iota(jnp.int32, (n, 1), 0) == h).astype(F32)


def _ssd_chunk(xs, dtr, dtb, alog, bm, cm, hin, reverse):
    Qn = S_Q
    ii = lax.broadcasted_iota(jnp.int32, (Qn, Qn), 0)
    jj = lax.broadcasted_iota(jnp.int32, (Qn, Qn), 1)
    keep = (ii <= jj) if reverse else (ii >= jj)
    tri = keep.astype(F32)
    triT = ((jj <= ii) if reverse else (jj >= ii)).astype(F32)
    eye = (ii == jj).astype(F32)
    dt = _softplus(dtr + dtb)
    a = dt * (-jnp.exp(alog))
    cs = hdot(tri, a)
    csT = hdot(a, triT, "tn")
    dtT = hdot(dt, eye, "tn")
    last = _onehot_row(0 if reverse else Qn - 1, Qn)
    ys, houts = [], []
    for g in range(S_GROUPS):
        G = bdot(cm[g], bm[g], "nt")
        for r in range(S_HEADS // S_GROUPS):
            h = g * (S_HEADS // S_GROUPS) + r
            eh_r, eh_c = _onehot_row(h, S_HEADS), _onehot_col(h, S_HEADS)
            cs_c = jnp.sum(cs * eh_r, axis=1, keepdims=True)
            dt_c = jnp.sum(dt * eh_r, axis=1, keepdims=True)
            cs_r = jnp.sum(csT * eh_c, axis=0, keepdims=True)
            dt_r = jnp.sum(dtT * eh_c, axis=0, keepdims=True)
            tot = jnp.sum(cs_r * last, axis=1, keepdims=True)
            decay = jnp.exp(jnp.where(keep, cs_c - cs_r, NEG))
            w = G * decay * dt_r
            y = bdot(w, xs[h], "nn") + bdot(cm[g], hin[h], "nt") * jnp.exp(cs_c)
            xsc = xs[h] * (jnp.exp(tot - cs_c) * dt_c)
            hout = hin[h] * jnp.exp(tot) + bdot(xsc, bm[g], "tn")
            ys.append(y)
            houts.append(hout)
    return ys, houts


def _ssd_orders(L, Lc):
    nl, ncx = L // S_Q, Lc // S_Q
    fwd = lambda s: jnp.where(s < ncx, nl + s, s - ncx)
    bwd = lambda s: nl + ncx - 1 - s
    return nl + ncx, fwd, bwd


def _ssd_in_specs(fo, bo, step):
    def at(order, w, col):
        return pl.BlockSpec((S_Q, w), lambda u: (order(step(u)), col))
    specs = []
    for order in (fo, bo):
        specs += [at(order, 512, 0), at(order, 256, 2), at(order, 256, 3), at(order, 128, C_DT // 128)]
    return specs


def ssd_fwd(act, P, dtb, alog, L, Lc, name):
    T = L + Lc
    ns, fo, bo = _ssd_orders(L, Lc)

    def body(xf, bf, cf, df, xb, bb, cb, db, dtb_ref, al_ref, yf, yb, hsf, hsb, Hf, Hb):
        s = pl.program_id(0)

        @pl.when(s == 0)
        def _():
            Hf[...] = jnp.zeros_like(Hf)
            Hb[...] = jnp.zeros_like(Hb)

        for d, (x_r, b_r, c_r, dt_r, y_r, hs_r, H) in enumerate(((xf, bf, cf, df, yf, hsf, Hf), (xb, bb, cb, db, yb, hsb, Hb))):
            hin = [H[h] for h in range(S_HEADS)]
            hs_r[0] = H[...]
            ys, houts = _ssd_chunk(
                [x_r[:, h * S_P:(h + 1) * S_P] for h in range(S_HEADS)], dt_r[:, d * 8:(d + 1) * 8],
                dtb_ref[d:d + 1, 0:8], al_ref[d:d + 1, 0:8],
                [b_r[:, g * S_N:(g + 1) * S_N] for g in range(S_GROUPS)], [c_r[:, g * S_N:(g + 1) * S_N] for g in range(S_GROUPS)],
                hin, reverse=(d == 1))
            for h in range(S_HEADS):
                y_r[:, h * S_P:(h + 1) * S_P] = ys[h]
                H[h] = houts[h]

    ident = lambda u: u
    small = pl.BlockSpec((8, 128), lambda u: (0, 0))
    hspec = pl.BlockSpec((1, S_HEADS, S_P, S_N), lambda u: (u, 0, 0, 0))
    return _pc(body, name=name, grid=(ns,),
               in_specs=_ssd_in_specs(fo, bo, ident) + [small, small],
               out_specs=[pl.BlockSpec((S_Q, 512), lambda u: (fo(u), 0)), pl.BlockSpec((S_Q, 512), lambda u: (bo(u), 0)), hspec, hspec],
               out_shape=[_sds((T, 512), F32), _sds((T, 512), F32), _sds((ns, S_HEADS, S_P, S_N), F32), _sds((ns, S_HEADS, S_P, S_N), F32)],
               scratch_shapes=[pltpu.VMEM((S_HEADS, S_P, S_N), F32), pltpu.VMEM((S_HEADS, S_P, S_N), F32)],
               compiler_params=_cp(("arbitrary",), 32 << 20))(act, act, act, P, act, act, act, P, dtb, alog)


def ssd_bwd(act, P, dtb, alog, hsf, hsb, dy, L, Lc, name):
    T = L + Lc
    ns, fo, bo = _ssd_orders(L, Lc)
    step = lambda u: ns - 1 - u

    def body(xf, bf, cf, df, xb, bb, cb, db, dtb_ref, al_ref, hsf_r, hsb_r, dyf, dyb,
             dxf, dbf, dcf, ddf, dxb, dbb, dcb, ddb, ddtb, dal, dHf, dHb):
        u = pl.program_id(0)

        @pl.when(u == 0)
        def _():
            dHf[...] = jnp.zeros_like(dHf)
            dHb[...] = jnp.zeros_like(dHb)
            ddtb[...] = jnp.zeros_like(ddtb)
            dal[...] = jnp.zeros_like(dal)

        dirs = ((xf, bf, cf, df, hsf_r, dyf, dxf, dbf, dcf, ddf, dHf), (xb, bb, cb, db, hsb_r, dyb, dxb, dbb, dcb, ddb, dHb))
        for d, (x_r, b_r, c_r, dt_r, hs_r, dy_r, dx_o, db_o, dc_o, dd_o, dH) in enumerate(dirs):
            f = functools.partial(_ssd_chunk, reverse=(d == 1))
            _, vjp = jax.vjp(
                f, [x_r[:, h * S_P:(h + 1) * S_P] for h in range(S_HEADS)], dt_r[:, d * 8:(d + 1) * 8],
                dtb_ref[d:d + 1, 0:8], al_ref[d:d + 1, 0:8],
                [b_r[:, g * S_N:(g + 1) * S_N] for g in range(S_GROUPS)], [c_r[:, g * S_N:(g + 1) * S_N] for g in range(S_GROUPS)],
                [hs_r[0, h] for h in range(S_HEADS)])
            gx, gdt, gdtb, gal, gb, gc, gh = vjp(([dy_r[:, h * S_P:(h + 1) * S_P] for h in range(S_HEADS)],
                                                  [dH[h] for h in range(S_HEADS)]))
            for h in range(S_HEADS):
                dx_o[:, h * S_P:(h + 1) * S_P] = gx[h]
                dH[h] = gh[h]
            for g in range(S_GROUPS):
                db_o[:, g * S_N:(g + 1) * S_N] = gb[g]
                dc_o[:, g * S_N:(g + 1) * S_N] = gc[g]
            dd_o[...] = gdt
            ddtb[d:d + 1, 0:8] += gdtb
            dal[d:d + 1, 0:8] += gal

    small = pl.BlockSpec((8, 128), lambda u: (0, 0))
    hspec = pl.BlockSpec((1, S_HEADS, S_P, S_N), lambda u: (step(u), 0, 0, 0))
    at = lambda order, w: pl.BlockSpec((S_Q, w), lambda u: (order(step(u)), 0))
    outs = []
    for order in (fo, bo):
        outs += [at(order, 512), at(order, 256), at(order, 256), at(order, 8)]
    oshape = [_sds((T, 512), F32), _sds((T, 256), F32), _sds((T, 256), F32), _sds((T, 8), F32)]
    return _pc(body, name=name, grid=(ns,),
               in_specs=_ssd_in_specs(fo, bo, step) + [small, small, hspec, hspec, at(fo, 512), at(bo, 512)],
               out_specs=outs + [small, small], out_shape=oshape + oshape + [_sds((8, 128), F32), _sds((8, 128), F32)],
               scratch_shapes=[pltpu.VMEM((S_HEADS, S_P, S_N), F32), pltpu.VMEM((S_HEADS, S_P, S_N), F32)],
               compiler_params=_cp(("arbitrary",), 40 << 20))(act, act, act, P, act, act, act, P, dtb, alog, hsf, hsb, dy, dy)


def _ssm_out(yf, yb, xs, z, dskip, g):
    y = (yf + yb + dskip * xs) * _silu(z)
    return (y * lax.rsqrt(jnp.mean(y * y, axis=-1, keepdims=True) + EPS)) * g


def ssm_out_fwd(yf, yb, act, P, dskip, g, name):
    T = yf.shape[0]

    def body(yf_r, yb_r, xs_r, z_r, d_r, g_r, o_r):
        o_r[...] = _ssm_out(yf_r[...], yb_r[...], xs_r[...], z_r[...], d_r[...], g_r[...]).astype(o_r.dtype)

    row = pl.BlockSpec((TR, 512), lambda i: (i, 0))
    vec = pl.BlockSpec((1, 512), lambda i: (0, 0))
    return _pc(body, name=name, grid=(T // TR,),
               in_specs=[row, row, row, pl.BlockSpec((TR, 512), lambda i: (i, C_Z // 512)), vec, vec],
               out_specs=row, out_shape=_sds((T, 512), BF16),
               compiler_params=_cp(("parallel",), 16 << 20))(yf, yb, act, P, dskip, g)


def ssm_out_bwd(yf, yb, act, P, dskip, g, do_src, name):
    T = yf.shape[0]

    def body(yf_r, yb_r, xs_r, z_r, d_r, g_r, do_r, dy_r, dxs_r, dz_r, dv_r):
        @pl.when(pl.program_id(0) == 0)
        def _():
            dv_r[...] = jnp.zeros_like(dv_r)

        _, vjp = jax.vjp(_ssm_out, yf_r[...], yb_r[...], xs_r[...], z_r[...], d_r[...], g_r[...])
        dyf, _, dxs, dz, dd, dg = vjp(do_r[...].astype(F32))
        dy_r[...] = dyf
        dxs_r[...] = dxs
        dz_r[...] = dz.astype(dz_r.dtype)
        dv_r[0:1, :] += dd
        dv_r[1:2, :] += dg

    row = pl.BlockSpec((TR, 512), lambda i: (i, 0))
    vec = pl.BlockSpec((1, 512), lambda i: (0, 0))
    return _pc(body, name=name, grid=(T // TR,),
               in_specs=[row, row, row, pl.BlockSpec((TR, 512), lambda i: (i, C_Z // 512)), vec, vec,
                         pl.BlockSpec((TR, 512), lambda i: (i, 1))],
               out_specs=[row, row, row, pl.BlockSpec((8, 512), lambda i: (0, 0))],
               out_shape=[_sds((T, 512), F32), _sds((T, 512), F32), _sds((T, 512), BF16), _sds((8, 512), F32)],
               compiler_params=_cp(("arbitrary",), 24 << 20))(yf, yb, act, P, dskip, g, do_src)


def add_halves(xv, got, cvec, name):
    n, r, cdim = xv.shape
    h = r // 2

    def body(c_ref, x_ref, g_ref, o_ref):
        o_ref[...] = (x_ref[...].astype(F32) + g_ref[...].astype(F32)).astype(o_ref.dtype)

    gs = pltpu.PrefetchScalarGridSpec(
        num_scalar_prefetch=1, grid=(n,),
        in_specs=[pl.BlockSpec((1, h, cdim), lambda k, c_ref: (k, c_ref[0], 0)), pl.BlockSpec((1, h, cdim), lambda k, c_ref: (k, 0, 0))],
        out_specs=pl.BlockSpec((1, h, cdim), lambda k, c_ref: (k, 0, 0)))
    return _pc(body, name=name, grid_spec=gs, out_shape=_sds((n, h, cdim), BF16),
               compiler_params=_cp(("arbitrary",), 24 << 20))(cvec, xv, got)


def sum_slots(a, name):
    n, r, cdim = a.shape
    tr = _div_tile(r, 512, 16)

    def body(a_ref, o_ref):
        acc = a_ref[0].astype(F32)
        for k in range(1, n):
            acc = acc + a_ref[k].astype(F32)
        o_ref[...] = acc

    return _pc(body, name=name, grid=(r // tr,), in_specs=[pl.BlockSpec((n, tr, cdim), lambda i: (0, i, 0))],
               out_specs=pl.BlockSpec((tr, cdim), lambda i: (i, 0)), out_shape=_sds((r, cdim), F32),
               compiler_params=_cp(("parallel",), 32 << 20))(a)


def adamw(w, g, m, v, name):
    R, C = w.shape
    tr = _div_tile(R, max(8, (1 << 19) // max(C, 1) // 8 * 8), 8) if R % 8 == 0 else R
    c1 = 1.0 / (1.0 - ADAM_B1 ** ADAM_STEP)
    c2 = 1.0 / (1.0 - ADAM_B2 ** ADAM_STEP)

    def body(w_ref, g_ref, m_ref, v_ref, d_ref, mo_ref, vo_ref):
        gg = g_ref[...]
        mn = ADAM_B1 * m_ref[...] + (1.0 - ADAM_B1) * gg
        vn = ADAM_B2 * v_ref[...] + (1.0 - ADAM_B2) * (gg * gg)
        d_ref[...] = -ADAM_LR * ((mn * c1) / (jnp.sqrt(vn * c2) + ADAM_EPS) + ADAM_WD * w_ref[...])
        mo_ref[...] = mn
        vo_ref[...] = vn

    spec = pl.BlockSpec((tr, C), lambda i: (i, 0))
    return _pc(body, name=name, grid=(R // tr,), in_specs=[spec] * 4, out_specs=[spec] * 3,
               out_shape=[_sds((R, C), F32)] * 3, compiler_params=_cp(("parallel",), 32 << 20))(w, g, m, v)


def _me():
    return lax.axis_index("x"), lax.axis_index("y"), lax.axis_index("c")


def _flip(v, bit):
    return 1 - v if bit else v


def allgather8(xv, name):
    R = xv.shape[0]

    def body(x_ref, out_ref, sum_ref, send_sems, recv_sems):
        mx, my, mc = _me()
        me = 4 * mx + 2 * my + mc
        out_ref[me] = x_ref[...]
        sends, recvs = [], []
        for k in range(1, 8):
            px, py, pc = _flip(mx, k & 4), _flip(my, k & 2), _flip(mc, k & 1)
            peer = 4 * px + 2 * py + pc
            sends.append(pltpu.make_async_remote_copy(src_ref=x_ref, dst_ref=out_ref.at[me], send_sem=send_sems.at[k - 1],
                                                      recv_sem=recv_sems.at[k - 1], device_id=(px, py, pc), device_id_type=MESH))
            recvs.append(pltpu.make_async_remote_copy(src_ref=x_ref, dst_ref=out_ref.at[peer], send_sem=send_sems.at[k - 1],
                                                      recv_sem=recv_sems.at[k - 1], device_id=(px, py, pc), device_id_type=MESH))
        for cp in sends:
            cp.start()
        for cp in recvs:
            cp.wait_recv()
        for cp in sends:
            cp.wait_send()
        acc = out_ref[0]
        for d in range(1, 8):
            acc = acc + out_ref[d]
        sum_ref[...] = acc

    vm = pl.BlockSpec(memory_space=pltpu.VMEM)
    return _pc(body, name=name, pin=False, in_specs=[vm], out_specs=[vm, vm], out_shape=[_sds((8, R, 128), F32), _sds((R, 128), F32)],
               scratch_shapes=[pltpu.SemaphoreType.DMA((7,)), pltpu.SemaphoreType.DMA((7,))],
               compiler_params=_cp(None, 32 << 20))(xv)


def _other_chips(mx, my):
    return [(1 - mx, my), (mx, 1 - my), (1 - mx, 1 - my)]


def _halves(r, mc, mult):
    h = r // 2
    return pl.ds(pl.multiple_of(mc * h, mult), h), pl.ds(pl.multiple_of((1 - mc) * h, mult), h)


def _rcopy(src, dst, send_sems, recv_sems, k, to):
    return pltpu.make_async_remote_copy(src_ref=src, dst_ref=dst, send_sem=send_sems.at[k], recv_sem=recv_sems.at[k],
                                        device_id=to, device_id_type=MESH)


def _gather_body(xs, outs, send_sems, recv_sems):
    n = len(xs)
    mx, my, mc = _me()
    chip = 2 * mx + my
    sib = (mx, my, 1 - mc)
    chips = _other_chips(mx, my)
    idx = [2 * cx + cy for cx, cy in chips]
    cp = functools.partial(_rcopy, send_sems=send_sems, recv_sems=recv_sems)
    hv = [_halves(x.shape[0], mc, 16) for x in xs]
    first, passed = [], []
    for a in range(n):
        for j, (cx, cy) in enumerate(chips):
            first.append(cp(xs[a].at[hv[a][0]], outs[a].at[chip, hv[a][0]], k=6 * a + j, to=(cx, cy, mc)))
            first[-1].start()
    for a in range(n):
        for j in range(3):
            cp(xs[a].at[hv[a][0]], outs[a].at[idx[j], hv[a][0]], k=6 * a + j, to=sib).wait_recv()
            passed.append(cp(outs[a].at[idx[j], hv[a][0]], outs[a].at[idx[j], hv[a][0]], k=6 * a + 3 + j, to=sib))
            passed[-1].start()
    for a in range(n):
        for j in range(3):
            cp(xs[a].at[hv[a][1]], outs[a].at[idx[j], hv[a][1]], k=6 * a + 3 + j, to=sib).wait_recv()
    for c_ in first + passed:
        c_.wait_send()


def _my_chip():
    return 2 * lax.axis_index("x") + lax.axis_index("y")


def _own_slots(outs, shards):
    return [lax.dynamic_update_index_in_dim(o, x, _my_chip(), 0) for o, x in zip(outs, shards)]


def gather_weights(shards, name):
    n = len(shards)

    def body(*refs):
        _gather_body(refs[:n], refs[n:2 * n], *refs[2 * n:])

    hbm = pl.BlockSpec(memory_space=pl.ANY)
    outs = _pc(body, name=name, in_specs=[hbm] * n, out_specs=[hbm] * n, out_shape=[_sds((4,) + x.shape, x.dtype) for x in shards],
               scratch_shapes=[pltpu.SemaphoreType.DMA((6 * n,)), pltpu.SemaphoreType.DMA((6 * n,))])(*shards)
    return _own_slots(outs, shards)


GATHER_REST_ID = 3


def gather_weights_sc(shards, name):
    n = len(shards)
    x_refs = [jax.new_ref(x, memory_space=pltpu.MemorySpace.HBM) for x in shards]
    out_refs = [jax.empty_ref(_sds((4,) + x.shape, x.dtype), memory_space=pltpu.MemorySpace.HBM) for x in shards]

    @pl.kernel(mesh=plsc.ScalarSubcoreMesh(axis_name="sc", num_cores=1), name=name,
               scratch_types=(pltpu.SemaphoreType.DMA((6 * n,)), pltpu.SemaphoreType.DMA((6 * n,))),
               compiler_params=pltpu.CompilerParams(collective_id=GATHER_REST_ID))
    def launch(send_sems, recv_sems):
        mx, my, mc = _me()
        barrier = pltpu.get_barrier_semaphore()
        for peer in [(mx, my, 1 - mc)] + [(cx, cy, mc) for cx, cy in _other_chips(mx, my)]:
            pl.semaphore_signal(barrier, inc=1, device_id=peer, device_id_type=MESH)
        pl.semaphore_wait(barrier, 4)
        _gather_body(x_refs, out_refs, send_sems, recv_sems)

    launch()
    return _own_slots([o[...] for o in out_refs], shards)


def swap_halves(arrs, name):
    n = len(arrs)

    def body(*refs):
        xs, outs = refs[:n], refs[n:2 * n]
        send_sems, recv_sems = refs[2 * n:]
        mx, my, mc = _me()
        cps = []
        for a in range(n):
            theirs = _halves(xs[a].shape[1], mc, 16)[1]
            cps.append(_rcopy(xs[a].at[pl.ds(0, 4), theirs], outs[a], send_sems, recv_sems, a, (mx, my, 1 - mc)))
            cps[-1].start()
        for c_ in cps:
            c_.wait()

    hbm = pl.BlockSpec(memory_space=pl.ANY)
    return _pc(body, name=name, in_specs=[hbm] * n, out_specs=[hbm] * n,
               out_shape=[_sds((4, x.shape[1] // 2, x.shape[2]), x.dtype) for x in arrs],
               scratch_shapes=[pltpu.SemaphoreType.DMA((n,)), pltpu.SemaphoreType.DMA((n,))])(*arrs)


SCATTER_ID = 4


def scatter_chips_sc(arrs, name):
    n = len(arrs)
    x_refs = [jax.new_ref(x, memory_space=pltpu.MemorySpace.HBM) for x in arrs]
    out_refs = [jax.empty_ref(_sds(x.shape, x.dtype), memory_space=pltpu.MemorySpace.HBM) for x in arrs]

    @pl.kernel(mesh=plsc.ScalarSubcoreMesh(axis_name="sc", num_cores=1), name=name,
               scratch_types=(pltpu.SemaphoreType.DMA((3 * n,)), pltpu.SemaphoreType.DMA((3 * n,))),
               compiler_params=pltpu.CompilerParams(collective_id=SCATTER_ID))
    def launch(send_sems, recv_sems):
        mx, my, mc = _me()
        chip = 2 * mx + my
        chips = _other_chips(mx, my)
        idx = [2 * cx + cy for cx, cy in chips]
        barrier = pltpu.get_barrier_semaphore()
        for cx, cy in chips:
            pl.semaphore_signal(barrier, inc=1, device_id=(cx, cy, mc), device_id_type=MESH)
        pl.semaphore_wait(barrier, 3)
        cp = functools.partial(_rcopy, send_sems=send_sems, recv_sems=recv_sems)
        sends = []
        for a in range(n):
            for j, (cx, cy) in enumerate(chips):
                sends.append(cp(x_refs[a].at[idx[j]], out_refs[a].at[chip], k=3 * a + j, to=(cx, cy, mc)))
                sends[-1].start()
        for a in range(n):
            for j, (cx, cy) in enumerate(chips):
                cp(x_refs[a].at[idx[j]], out_refs[a].at[idx[j]], k=3 * a + j, to=(cx, cy, mc)).wait_recv()
        for c_ in sends:
            c_.wait_send()

    launch()
    return _own_slots([o[...] for o in out_refs], [lax.dynamic_index_in_dim(x, _my_chip(), 0, keepdims=False) for x in arrs])


def share_halves(parts, name):
    flat = [p for w in parts for p in w]
    nw, n = len(parts), len(flat)
    depth = n // nw

    def body(*refs):
        xs, outs = refs[:n], refs[n:n + nw]
        send_sems, recv_sems = refs[n + nw:]
        mx, my, mc = _me()
        sib = (mx, my, 1 - mc)
        sends, recvs = [], []
        for a in range(n):
            w, l = a // depth, a % depth
            mine, theirs = _halves(outs[w].shape[1], mc, 8)
            sends.append(_rcopy(xs[a], outs[w].at[l, mine], send_sems, recv_sems, a, sib))
            recvs.append(_rcopy(xs[a], outs[w].at[l, theirs], send_sems, recv_sems, a, sib))
            sends[-1].start()
        for c_ in recvs:
            c_.wait_recv()
        for c_ in sends:
            c_.wait_send()

    hbm = pl.BlockSpec(memory_space=pl.ANY)
    outs = _pc(body, name=name, in_specs=[hbm] * n, out_specs=[hbm] * nw,
               out_shape=[_sds((depth, 2 * w[0].shape[0], w[0].shape[1]), F32) for w in parts],
               scratch_shapes=[pltpu.SemaphoreType.DMA((n,)), pltpu.SemaphoreType.DMA((n,))])(*flat)
    outs = list(outs)
    mc = lax.axis_index("c")
    for w in range(nw):
        for l in range(depth):
            h = parts[w][l].shape[0]
            outs[w] = lax.dynamic_update_slice(outs[w], parts[w][l][None], (l, mc * h, 0))
    return outs


_BIG = ("w_in", "w_out", "w_ffn_in", "w_ffn_out")
N_CHIPS = 4
DEPTH = 2


def _pad_rows(v, mult=8):
    n = v.shape[0]
    rows = -(-n // 128)
    rows = -(-rows // mult) * mult
    return jnp.pad(v, (0, rows * 128 - n)).reshape(rows, 128)


class _Flat:
    def __init__(self):
        self.items = []

    def add(self, name, a):
        self.items.append((name, a.shape, a.reshape(-1).astype(F32)))

    def rows(self):
        return _pad_rows(jnp.concatenate([a for _, _, a in self.items]))

    def split(self, rows):
        flat = rows.reshape(-1)
        out, o = {}, 0
        for name, shape, a in self.items:
            out[name] = flat[o:o + a.shape[0]].reshape(shape)
            o += a.shape[0]
        return out

    def split_lead(self, rows3):
        n = rows3.shape[0]
        flat = rows3.reshape(n, -1)
        out, o = {}, 0
        for name, shape, a in self.items:
            out[name] = flat[:, o:o + a.shape[0]].reshape((n,) + tuple(shape))
            o += a.shape[0]
        return out


def _gsv(rows):
    z = jnp.zeros((2, D), F32)
    r = [z if a is None else a for a in rows] + [z] * 5
    return jnp.stack(r, axis=1)


def _pad8(a, rows=8, cols=128):
    return jnp.zeros((rows, cols), F32).at[:a.shape[0], :a.shape[1]].set(a.astype(F32))


def kernel(x, c, ctx, c_ctx, w_mod, b_mod, g_mix, w_in, wa_sink, na_rpb, ssm_conv_w, ssm_conv_b, ssm_dt_bias, ssm_a_log, ssm_d, ssm_norm_g, w_out, g_ffn, w_ffn_in, w_ffn_out, g_final, loss_target, m_c_ctx, m_w_mod, m_b_mod, m_g_mix, m_w_in, m_wa_sink, m_na_rpb, m_ssm_conv_w, m_ssm_conv_b, m_ssm_dt_bias, m_ssm_a_log, m_ssm_d, m_ssm_norm_g, m_w_out, m_g_ffn, m_w_ffn_in, m_w_ffn_out, m_g_final, v_c_ctx, v_w_mod, v_b_mod, v_g_mix, v_w_in, v_wa_sink, v_na_rpb, v_ssm_conv_w, v_ssm_conv_b, v_ssm_dt_bias, v_ssm_a_log, v_ssm_d, v_ssm_norm_g, v_w_out, v_g_ffn, v_w_ffn_in, v_w_ffn_out, v_g_final):
    L, Lc = x.shape[1], ctx.shape[1]
    T = L + Lc
    nL = L // TR
    mx, my, mc = lax.axis_index("x"), lax.axis_index("y"), lax.axis_index("c")
    dev = 4 * mx + 2 * my + mc
    chip = 2 * mx + my
    MODW = 6 * D // N_CHIPS
    CW = 1024 // N_CHIPS

    sc = _silu(c.astype(F32))
    scc = _silu(c_ctx.astype(F32))[None]
    f1 = _Flat()
    f1.add("sc", sc)
    f1.add("conv_w", ssm_conv_w)
    g1, _ = allgather8(f1.rows(), "gather_cond")
    g1 = f1.split_lead(g1)
    sc_all = g1["sc"][:, 0]
    conv_w = jnp.concatenate([g1["conv_w"][2 * k] for k in range(N_CHIPS)], axis=-1)
    A16 = jnp.concatenate([sc_all, scc, jnp.zeros((7, D), F32)], axis=0)

    mod_part = jnp.stack([matmul(A16, w_mod[l], "nn", F32, f"mod_fwd{l}") for l in range(DEPTH)])
    f2 = _Flat()
    f2.add("mod", mod_part)
    g2, _ = allgather8(f2.rows(), "gather_mod")
    g2 = f2.split_lead(g2)["mod"]
    mods = jnp.concatenate([g2[2 * k] for k in range(N_CHIPS)], axis=-1) + b_mod[:, None, :]
    mod_l = lax.dynamic_index_in_dim(mods, dev, axis=1, keepdims=False).reshape(DEPTH, 6, D)
    mod_c = mods[:, 8].reshape(DEPTH, 6, D)
    mod = jnp.stack([mod_l, mod_c], axis=1)
    mrow = lambda l, j: mod[l, :, j]

    own = {"w_in": w_in, "w_out": w_out, "w_ffn_in": w_ffn_in, "w_ffn_out": w_ffn_out}
    sh16 = [own[n][l].astype(BF16) for n in _BIG for l in range(DEPTH)]
    gath = list(gather_weights(sh16[:1], "gather_first"))
    after_first = (gath[0][0, 0, 0] * 0).astype(BF16)
    gath += list(gather_weights_sc([sh16[1] + after_first] + sh16[2:], "gather_rest"))
    gw = {n: [gath[DEPTH * i + l] for l in range(DEPTH)] for i, n in enumerate(_BIG)}
    W_in = [jnp.pad(jnp.concatenate([g[k] for k in range(N_CHIPS)], axis=1), ((0, 0), (0, IN_PAD - IN_COLS))) for g in gw["w_in"]]
    W_out = [g.reshape(D, D) for g in gw["w_out"]]
    W_fo = [g.reshape(D_FF, D) for g in gw["w_ffn_out"]]
    W_fi = gw["w_ffn_in"]

    cos, sin, rotm = rope_tables(L, Lc)
    x0 = jnp.concatenate([x[0], ctx[0]], axis=0).astype(F32)

    sv = []
    xin = x0
    gsv_first = _gsv([None, mrow(0, 0), mrow(0, 1)])
    _, h1 = res_norm_mod(x0, None, gsv_first, g_mix[0][None], nL, "norm_first")
    for l in range(DEPTH):
        s = {"xin": xin, "h1": h1}
        P = matmul(h1, W_in[l], "nn", F32, f"in_proj{l}", tn=IN_PAD)
        qr, kr, kb, vb = rope_apply(P, C_QA // 256, P, C_KA // 128, cos, sin, rotm, False, f"rope{l}", kv_src=P)
        sink8 = _pad8(jnp.broadcast_to(wa_sink[l][:, None], (WA_HEADS, 128)))
        oa = win_attn_fwd(qr, kr, P, sink8, L, Lc, f"wa_fwd{l}")
        bias = na_bias_table(na_rpb[l], l)
        ob = na_fwd(P, kb, vb, bias, L, Lc, f"na_fwd{l}")
        w8 = jnp.concatenate([conv_w[l], jnp.zeros((1, 1024), F32)], axis=0)
        pre, act = conv_silu_fwd(P, w8, ssm_conv_b[l][None], nL, f"conv_fwd{l}")
        dtb8, al8 = _pad8(ssm_dt_bias[l]), _pad8(ssm_a_log[l])
        yf, yb, hsf, hsb = ssd_fwd(act, P, dtb8, al8, L, Lc, f"ssd_fwd{l}")
        dskip = jnp.repeat(ssm_d[l], S_P)[None]
        oc = ssm_out_fwd(yf, yb, act, P, dskip, ssm_norm_g[l][None], f"ssm_out_fwd{l}")
        mixin = jnp.concatenate([oa, ob, oc], axis=1)
        mix = matmul(mixin, W_out[l], "nn", F32, f"out_proj{l}")
        gsv_mid = _gsv([mrow(l, 2), mrow(l, 3), mrow(l, 4)])
        x1, h2 = res_norm_mod(xin, mix, gsv_mid, g_ffn[l][None], nL, f"norm_mid{l}")
        gu = matmul_fi(h2, W_fi[l], "nn", BF16, f"ffn_in{l}")
        af = swiglu_fwd(gu, f"swiglu_fwd{l}")
        fo = matmul(af, W_fo[l], "nn", F32, f"ffn_out{l}", tk=D_FF)
        s.update(P=P, qr=qr, kr=kr, sink8=sink8, kb=kb, vb=vb, bias=bias, w8=w8, pre=pre, act=act, dtb8=dtb8, al8=al8, yf=yf,
                 yb=yb, hsf=hsf, hsb=hsb, dskip=dskip, mixin=mixin, mix=mix, gsv_mid=gsv_mid, x1=x1, h2=h2, gu=gu, af=af, fo=fo)
        if l + 1 < DEPTH:
            s["gsv_end"] = _gsv([mrow(l, 5), mrow(l + 1, 0), mrow(l + 1, 1)])
            xin, h1 = res_norm_mod(x1, fo, s["gsv_end"], g_mix[l + 1][None], nL, f"norm_end{l}")
        else:
            s["gsv_end"] = _gsv([mrow(l, 5), None, None])
        sv.append(s)

    last = sv[-1]
    loss8, dres, dfo, dgsv_end, dg_final = final_loss(last["x1"], last["fo"], last["gsv_end"], g_final[None], loss_target[0].astype(F32), nL, "final_loss")
    loss = lax.psum(loss8[0, 0], ("x", "y", "c"))

    dmod = [[None] * 6 for _ in range(DEPTH)]
    gW = {n: [None] * DEPTH for n in _BIG}
    small = [dict() for _ in range(DEPTH)]
    parts = [None] * DEPTH
    cvec = mc.astype(jnp.int32).reshape(1)
    grad_x = None
    for l in reversed(range(DEPTH)):
        s = sv[l]
        dmod[l][5] = dgsv_end[:, 0]
        if l + 1 < DEPTH:
            dmod[l + 1][0], dmod[l + 1][1] = dgsv_end[:, 1], dgsv_end[:, 2]
        daf = matmul(dfo, W_fo[l], "nt", BF16, f"ffn_out_dx{l}")
        gW["w_ffn_out"][l] = matmul(s["af"], dfo, "tn", BF16, f"ffn_out_dw{l}", tm=1408, tk=T).reshape(N_CHIPS, D_FF // N_CHIPS, D)
        dgu = swiglu_bwd(s["gu"], daf, f"swiglu_bwd{l}")
        dh2 = matmul_fi(dgu, W_fi[l], "nt", F32, f"ffn_in_dx{l}")
        gW["w_ffn_in"][l] = matmul_fi(s["h2"], dgu, "tn", BF16, f"ffn_in_dw{l}")
        dres, dmix, dgsv_mid, dg_ffn = res_norm_mod_bwd(s["x1"], s["mix"], s["gsv_mid"], g_ffn[l][None], dh2, dres, nL, f"norm_mid_bwd{l}")
        dmod[l][2], dmod[l][3], dmod[l][4] = dgsv_mid[:, 0], dgsv_mid[:, 1], dgsv_mid[:, 2]
        dmixin = matmul(dmix, W_out[l], "nt", F32, f"out_proj_dx{l}")
        gW["w_out"][l] = matmul(s["mixin"], dmix, "tn", BF16, f"out_proj_dw{l}", tm=1024, tk=T).reshape(N_CHIPS, D // N_CHIPS, D)
        P = s["P"]
        dqr, dkr, dva, dsink = win_attn_bwd(s["qr"], s["kr"], P, s["sink8"], dmixin, L, Lc, f"wa_bwd{l}")
        dqa, dka = rope_apply(dqr, 0, dkr[WA_BLK:WA_BLK + T], 0, cos, sin, rotm, True, f"rope_bwd{l}")
        dqb, dkb, dvb, dbias = na_bwd(P, s["kb"], s["vb"], s["bias"], dmixin, L, Lc, f"na_bwd{l}")
        dy, dxs1, dz, dvec = ssm_out_bwd(s["yf"], s["yb"], s["act"], P, s["dskip"], ssm_norm_g[l][None], dmixin, f"ssm_out_bwd{l}")
        dxf, dbf, dcf, ddf, dxb, dbb, dcb, ddb, ddtb, dal = ssd_bwd(s["act"], P, s["dtb8"], s["al8"], s["hsf"], s["hsb"], dy, L, Lc, f"ssd_bwd{l}")
        dpre = dsilu(s["pre"], [dxf, dxb, dxs1], [dbf, dbb], [dcf, dcb], f"dsilu{l}")
        dxbc, dw8, db8 = conv_bwd(dpre, P, s["w8"], nL, f"conv_bwd{l}")
        dP = jnp.concatenate([dqa, dqb, dz, dka, dva[WA_BLK:WA_BLK + T].astype(BF16), dkb.astype(BF16), dvb.astype(BF16), dxbc,
                              ddf.astype(BF16), ddb.astype(BF16), jnp.zeros((T, IN_PAD - IN_COLS), BF16)], axis=1)
        dh1 = matmul(dP, W_in[l], "nt", F32, f"in_proj_dx{l}", tk=IN_PAD)
        dwin = matmul(s["h1"], dP, "tn", BF16, f"in_proj_dw{l}", tm=512, tn=IN_PAD, tk=T // 2)
        cw = IN_COLS // N_CHIPS
        gW["w_in"][l] = jnp.stack([dwin[:, k * cw:(k + 1) * cw] for k in range(N_CHIPS)])
        garr = [gW[n][l] for n in _BIG]
        got = swap_halves(garr, f"reduce_d2d{l}")
        chip_sum = [add_halves(garr[a], got[a], cvec, f"reduce_add_pair{l}_{a}") for a in range(len(garr))]
        parts[l] = scatter_chips_sc(chip_sum, f"reduce_ici{l}")
        small[l] = dict(g_ffn=dg_ffn[0], wa_sink=dsink[:WA_HEADS, 0], na_rpb=na_rpb_grad(dbias, l), conv_w=dw8[:S_CONV], conv_b=db8[0],
                        dt_bias=ddtb[:2, :8], a_log=dal[:2, :8], ssm_d=dvec[0].reshape(S_HEADS, S_P).sum(axis=1), norm_g=dvec[1])
        if l > 0:
            p = sv[l - 1]
            dres, dfo, dgsv_end, dg_mix = res_norm_mod_bwd(s["xin"], p["fo"], p["gsv_end"], g_mix[l][None], dh1, dres, nL, f"norm_end_bwd{l - 1}")
        else:
            grad_x, _, dgsv_first, dg_mix = res_norm_mod_bwd(s["xin"], None, gsv_first, g_mix[0][None], dh1, dres, nL, "norm_first_bwd")
            dmod[0][0], dmod[0][1] = dgsv_first[:, 1], dgsv_first[:, 2]
        small[l]["g_mix"] = dg_mix[0]
    for l in range(DEPTH):
        for j in range(6):
            if dmod[l][j] is None:
                dmod[l][j] = jnp.zeros((2, D), F32)
    dmod = jnp.stack([jnp.stack(r, axis=1) for r in dmod])

    f3 = _Flat()
    f3.add("dmod_l", dmod[:, 0].reshape(DEPTH, 6 * D))
    f3.add("dmod_c", dmod[:, 1].reshape(DEPTH, 6 * D))
    f3.add("g_final", dg_final[0])
    for n in ("g_mix", "g_ffn", "wa_sink", "na_rpb", "conv_w", "conv_b", "dt_bias", "a_log", "ssm_d", "norm_g"):
        f3.add(n, jnp.stack([small[l][n] for l in range(DEPTH)]))
    g3, s3 = allgather8(f3.rows(), "reduce_small")
    dmod_all = f3.split_lead(g3)["dmod_l"]
    s3 = f3.split(s3)
    dmodc_tot = s3["dmod_c"]
    col0 = chip * MODW
    G16, G16c = [], []
    for l in range(DEPTH):
        rows = jnp.concatenate([dmod_all[:, l], dmodc_tot[l][None], jnp.zeros((7, 6 * D), F32)], axis=0)
        G16.append(lax.dynamic_slice_in_dim(rows, col0, MODW, axis=1))
        rc = jnp.concatenate([dmodc_tot[l][None], jnp.zeros((15, 6 * D), F32)], axis=0)
        G16c.append(lax.dynamic_slice_in_dim(rc, col0, MODW, axis=1))
    grad_w_mod = jnp.stack([matmul(A16, G16[l], "tn", F32, f"mod_dw{l}") for l in range(DEPTH)])
    dscc_part = sum(matmul(G16c[l], w_mod[l], "nt", F32, f"mod_dx{l}")[0] for l in range(DEPTH))
    _, s4 = allgather8(_pad_rows(dscc_part * (mc == 1).astype(F32)), "reduce_cctx")
    dscc = s4.reshape(-1)[:D]
    cc = c_ctx.astype(F32)
    sg = 1.0 / (1.0 + jnp.exp(-cc))
    grad_c_ctx = dscc * (sg * (1.0 + cc * (1.0 - sg)))

    halves = [[sum_slots(parts[l][i], f"reduce_add_chips{l}_{i}") for l in range(DEPTH)] for i in range(len(_BIG))]
    gsh = dict(zip(_BIG, share_halves(halves, "reduce_share")))

    grads = {"c_ctx": grad_c_ctx, "w_mod": grad_w_mod, "b_mod": s3["dmod_l"] + s3["dmod_c"], "g_mix": s3["g_mix"], "w_in": gsh["w_in"],
             "wa_sink": s3["wa_sink"], "na_rpb": s3["na_rpb"],
             "ssm_conv_w": lax.dynamic_slice_in_dim(s3["conv_w"], chip * CW, CW, axis=2), "ssm_conv_b": s3["conv_b"],
             "ssm_dt_bias": s3["dt_bias"], "ssm_a_log": s3["a_log"], "ssm_d": s3["ssm_d"], "ssm_norm_g": s3["norm_g"],
             "w_out": gsh["w_out"], "g_ffn": s3["g_ffn"], "w_ffn_in": gsh["w_ffn_in"], "w_ffn_out": gsh["w_ffn_out"], "g_final": s3["g_final"]}
    wts = {"c_ctx": c_ctx, "w_mod": w_mod, "b_mod": b_mod, "g_mix": g_mix, "w_in": w_in, "wa_sink": wa_sink, "na_rpb": na_rpb,
           "ssm_conv_w": ssm_conv_w, "ssm_conv_b": ssm_conv_b, "ssm_dt_bias": ssm_dt_bias, "ssm_a_log": ssm_a_log, "ssm_d": ssm_d,
           "ssm_norm_g": ssm_norm_g, "w_out": w_out, "g_ffn": g_ffn, "w_ffn_in": w_ffn_in, "w_ffn_out": w_ffn_out, "g_final": g_final}
    ms = {"c_ctx": m_c_ctx, "w_mod": m_w_mod, "b_mod": m_b_mod, "g_mix": m_g_mix, "w_in": m_w_in, "wa_sink": m_wa_sink, "na_rpb": m_na_rpb,
          "ssm_conv_w": m_ssm_conv_w, "ssm_conv_b": m_ssm_conv_b, "ssm_dt_bias": m_ssm_dt_bias, "ssm_a_log": m_ssm_a_log, "ssm_d": m_ssm_d,
          "ssm_norm_g": m_ssm_norm_g, "w_out": m_w_out, "g_ffn": m_g_ffn, "w_ffn_in": m_w_ffn_in, "w_ffn_out": m_w_ffn_out, "g_final": m_g_final}
    vs = {"c_ctx": v_c_ctx, "w_mod": v_w_mod, "b_mod": v_b_mod, "g_mix": v_g_mix, "w_in": v_w_in, "wa_sink": v_wa_sink, "na_rpb": v_na_rpb,
          "ssm_conv_w": v_ssm_conv_w, "ssm_conv_b": v_ssm_conv_b, "ssm_dt_bias": v_ssm_dt_bias, "ssm_a_log": v_ssm_a_log, "ssm_d": v_ssm_d,
          "ssm_norm_g": v_ssm_norm_g, "w_out": v_w_out, "g_ffn": v_g_ffn, "w_ffn_in": v_w_ffn_in, "w_ffn_out": v_w_ffn_out, "g_final": v_g_final}
    names = list(wts)
    grads = {n: grads[n].reshape(wts[n].shape).astype(F32) for n in names}
    big = ("w_mod", "w_in", "w_out", "w_ffn_in", "w_ffn_out")
    delta, new_m, new_v = {}, {}, {}
    for n in big:
        shp = wts[n].shape
        two = lambda a: a.reshape(shp[0] * shp[1], shp[2])
        d_, m_, v_ = adamw(two(wts[n]), two(grads[n]), two(ms[n]), two(vs[n]), f"adamw_{n}")
        delta[n], new_m[n], new_v[n] = d_.reshape(shp), m_.reshape(shp), v_.reshape(shp)
    packs = []
    for src in (wts, grads, ms, vs):
        f = _Flat()
        for n in names:
            if n not in big:
                f.add(n, src[n])
        packs.append(f)
    d_, m_, v_ = adamw(*[f.rows() for f in packs], "adamw_small")
    for dst, rows in ((delta, d_), (new_m, m_), (new_v, v_)):
        dst.update(packs[0].split(rows))

    return (loss, grad_x[:L][None], *[grads[n] for n in names], *[delta[n] for n in names],
            *[new_m[n] for n in names], *[new_v[n] for n in names])
```

```python
import functools

import numpy as np
import jax
import jax.numpy as jnp
from jax import lax
from jax.experimental import pallas as pl
from jax.experimental.pallas import tpu as pltpu
from jax.experimental.pallas import tpu_sc as plsc

F32 = jnp.float32
BF16 = jnp.bfloat16
_MXU = jnp.bfloat16
_HI = lax.Precision.HIGHEST
MESH = pl.DeviceIdType.MESH

D = 1024
HD = 64
GRID_W = 64
EPS = 1e-6
ROPE_BASE = 10000.0
WA_HEADS, WA_KV = 4, 2
WA_BLK = 128
NA_HEADS, NA_KH, NA_KW = 4, 8, 16
S_HEADS, S_P, S_INNER, S_GROUPS, S_N, S_CONV, S_Q = 8, 64, 512, 2, 128, 7, 128
D_FF = 2816
IN_COLS = 2832
IN_PAD = 2944
C_QA, C_QB, C_Z, C_KA, C_VA, C_KB, C_VB, C_XBC, C_DT = 0, 256, 512, 1024, 1152, 1280, 1536, 1792, 2816
ADAM_LR, ADAM_B1, ADAM_B2, ADAM_EPS, ADAM_WD, ADAM_STEP = 0.001, 0.9, 0.999, 1e-08, 0.01, 10

TR = 256
NEG = -1e30
VMEM_CAP = 56 * 1024 * 1024


PIN_BYTES = 256 * 1024


def _is_big(a):
    return hasattr(a, "shape") and len(a.shape) >= 2 and int(np.prod(a.shape)) * jnp.dtype(a.dtype).itemsize >= PIN_BYTES


def _pc(body, *, out_shape, pin=True, **kw):
    if not pin:
        return pl.pallas_call(body, out_shape=out_shape, **kw)
    one = isinstance(out_shape, jax.ShapeDtypeStruct)
    outs = [pltpu.HBM(s.shape, s.dtype) if _is_big(s) else s for s in ([out_shape] if one else out_shape)]
    call = pl.pallas_call(body, out_shape=outs[0] if one else outs, **kw)
    return lambda *args: call(*[pltpu.with_memory_space_constraint(a, pltpu.HBM) if _is_big(a) else a for a in args])


def _cp(sem=None, vmem=None):
    kw = {}
    if sem is not None:
        kw["dimension_semantics"] = sem
    if vmem is not None:
        kw["vmem_limit_bytes"] = int(min(max(vmem, 16 * 1024 * 1024), VMEM_CAP))
    return pltpu.CompilerParams(**kw)


def _sds(shape, dtype):
    return jax.ShapeDtypeStruct(tuple(shape), dtype)


_DIMS = {"nn": ((1,), (0,)), "nt": ((1,), (1,)), "tn": ((0,), (0,))}


def _dg(a, b, dims):
    return lax.dot_general(a.astype(_MXU), b.astype(_MXU), (dims, ((), ())), preferred_element_type=F32)


@functools.partial(jax.custom_vjp, nondiff_argnums=(2,))
def bdot(a, b, mode):
    return _dg(a, b, _DIMS[mode])


def _bdot_fwd(a, b, mode):
    return bdot(a, b, mode), (a, b)


def _bdot_bwd(mode, res, g):
    a, b = res
    if mode == "nn":
        return bdot(g, b, "nt"), bdot(a, g, "tn")
    if mode == "nt":
        return bdot(g, b, "nn"), bdot(g, a, "tn")
    return bdot(b, g, "nt"), bdot(a, g, "nn")


bdot.defvjp(_bdot_fwd, _bdot_bwd)


def hdot(a, b, mode="nn"):
    return lax.dot_general(a, b, (_DIMS[mode], ((), ())), precision=_HI, preferred_element_type=F32)


def _silu(x):
    return x / (1.0 + jnp.exp(-x))


def _softplus(x):
    return jnp.maximum(x, 0.0) + jnp.log(1.0 + jnp.exp(-jnp.abs(x)))


def _div_tile(n, cap, mult):
    if n <= cap:
        return n
    best = None
    for t in range(mult, cap + 1, mult):
        if n % t == 0:
            best = t
    assert best is not None, (n, cap, mult)
    return best


def matmul(a, b, mode, out_dtype, name, tm=640, tn=1536, tk=1408, hi=False):
    if mode == "tn":
        K, M = a.shape
    else:
        M, K = a.shape
    N = b.shape[0] if mode == "nt" else b.shape[1]
    tm = _div_tile(M, tm, 128 if mode == "tn" else 16)
    tn = _div_tile(N, tn, 128)
    tk = _div_tile(K, tk, 128 if mode != "tn" else 16)
    nk = K // tk
    dims = _DIMS[mode]

    def body(a_ref, b_ref, o_ref, *acc):
        if hi:
            part = lax.dot_general(a_ref[...], b_ref[...], (dims, ((), ())), precision=_HI, preferred_element_type=F32)
        else:
            part = _dg(a_ref[...], b_ref[...], dims)
        if nk == 1:
            o_ref[...] = part.astype(o_ref.dtype)
        else:
            k = pl.program_id(2)

            @pl.when(k == 0)
            def _():
                acc[0][...] = part

            @pl.when(k > 0)
            def _():
                acc[0][...] += part

            @pl.when(k == nk - 1)
            def _():
                o_ref[...] = acc[0][...].astype(o_ref.dtype)

    if mode == "tn":
        a_spec = pl.BlockSpec((tk, tm), lambda i, j, k: (k, i))
    else:
        a_spec = pl.BlockSpec((tm, tk), lambda i, j, k: (i, k))
    if mode == "nt":
        b_spec = pl.BlockSpec((tn, tk), lambda i, j, k: (j, k))
    else:
        b_spec = pl.BlockSpec((tk, tn), lambda i, j, k: (k, j))
    isz = lambda x: jnp.dtype(x.dtype).itemsize
    vmem = 2 * (tm * tk * isz(a) + tk * tn * isz(b) + tm * tn * jnp.dtype(out_dtype).itemsize) + 3 * tm * tn * 4
    return _pc(
        body, name=name, grid=(M // tm, N // tn, nk),
        in_specs=[a_spec, b_spec], out_specs=pl.BlockSpec((tm, tn), lambda i, j, k: (i, j)),
        out_shape=_sds((M, N), out_dtype),
        scratch_shapes=[pltpu.VMEM((tm, tn), F32)] if nk > 1 else [],
        compiler_params=_cp(("parallel", "parallel", "arbitrary"), vmem + (8 << 20)),
    )(a, b)


def _norm_mod(xo, shift, scale, g):
    r = lax.rsqrt(jnp.mean(xo * xo, axis=-1, keepdims=True) + EPS)
    return (xo * r) * g * (1.0 + scale) + shift


def res_norm_mod(x, y, gsv, g, nL, name):
    T = x.shape[0]
    has_y = y is not None

    def body(*refs):
        if has_y:
            x_ref, y_ref, gsv_ref, g_ref, xo_ref, h_ref = refs
            xo = x_ref[...] + gsv_ref[0, 0:1, :] * y_ref[...]
            xo_ref[...] = xo
        else:
            x_ref, gsv_ref, g_ref, h_ref = refs
            xo = x_ref[...]
        h_ref[...] = _norm_mod(xo, gsv_ref[0, 1:2, :], gsv_ref[0, 2:3, :], g_ref[...]).astype(h_ref.dtype)

    row = pl.BlockSpec((TR, D), lambda i: (i, 0))
    in_specs = [row] + ([row] if has_y else []) + [pl.BlockSpec((1, 8, D), lambda i: (i // nL, 0, 0)),
                                                     pl.BlockSpec((1, D), lambda i: (0, 0))]
    out_specs = ([row] if has_y else []) + [row]
    out_shape = ([_sds((T, D), F32)] if has_y else []) + [_sds((T, D), BF16)]
    args = (x, y, gsv, g) if has_y else (x, gsv, g)
    outs = _pc(body, name=name, grid=(T // TR,), in_specs=in_specs, out_specs=out_specs, out_shape=out_shape,
               compiler_params=_cp(("arbitrary",), 24 << 20))(*args)
    return (outs[0], outs[1]) if has_y else (None, outs[0])


def res_norm_mod_bwd(xo, y, gsv, g, dh, dres, nL, name):
    T = xo.shape[0]
    has_y = y is not None

    def body(*refs):
        if has_y:
            xo_ref, y_ref, gsv_ref, g_ref, dh_ref, dres_ref, dx_ref, dy_ref, dgsv_ref, dg_ref = refs
        else:
            xo_ref, gsv_ref, g_ref, dh_ref, dres_ref, dx_ref, dgsv_ref, dg_ref = refs
        i = pl.program_id(0)

        @pl.when((i == 0) | (i == nL))
        def _():
            dgsv_ref[...] = jnp.zeros_like(dgsv_ref)

        @pl.when(i == 0)
        def _():
            dg_ref[...] = jnp.zeros_like(dg_ref)

        _, vjp = jax.vjp(_norm_mod, xo_ref[...], gsv_ref[0, 1:2, :], gsv_ref[0, 2:3, :], g_ref[...])
        dxn, dshift, dscale, dg = vjp(dh_ref[...].astype(F32))
        dxo = dres_ref[...] + dxn
        dx_ref[...] = dxo
        if has_y:
            dy_ref[...] = (gsv_ref[0, 0:1, :] * dxo).astype(dy_ref.dtype)
            dgsv_ref[0, 0:1, :] += jnp.sum(y_ref[...] * dxo, axis=0, keepdims=True)
        dgsv_ref[0, 1:2, :] += dshift
        dgsv_ref[0, 2:3, :] += dscale
        dg_ref[0:1, :] += dg

    row = pl.BlockSpec((TR, D), lambda i: (i, 0))
    gspec = pl.BlockSpec((1, 8, D), lambda i: (i // nL, 0, 0))
    in_specs = [row] + ([row] if has_y else []) + [gspec, pl.BlockSpec((1, D), lambda i: (0, 0)), row, row]
    out_specs = [row] + ([row] if has_y else []) + [gspec, pl.BlockSpec((8, D), lambda i: (0, 0))]
    out_shape = [_sds((T, D), F32)] + ([_sds((T, D), BF16)] if has_y else []) + [_sds((2, 8, D), F32), _sds((8, D), F32)]
    args = (xo, y, gsv, g, dh, dres) if has_y else (xo, gsv, g, dh, dres)
    outs = _pc(body, name=name, grid=(T // TR,), in_specs=in_specs, out_specs=out_specs, out_shape=out_shape,
               compiler_params=_cp(("arbitrary",), 32 << 20))(*args)
    if has_y:
        return outs
    return outs[0], None, outs[1], outs[2]


def final_loss(x, y, gsv, g, target, nL, name):
    T = x.shape[0]

    def lossf(xo, gv, t):
        yn = (xo * lax.rsqrt(jnp.mean(xo * xo, axis=-1, keepdims=True) + EPS)) * gv
        e = yn - t
        return 0.5 * jnp.sum(jnp.sum(e * e, axis=-1, keepdims=True) * (1.0 / D), axis=0, keepdims=True)

    def body(x_ref, y_ref, gsv_ref, g_ref, t_ref, loss_ref, dx_ref, dy_ref, dgsv_ref, dg_ref):
        i = pl.program_id(0)

        @pl.when(i == 0)
        def _():
            loss_ref[...] = jnp.zeros_like(loss_ref)
            dg_ref[...] = jnp.zeros_like(dg_ref)

        @pl.when((i == 0) | (i == nL))
        def _():
            dgsv_ref[...] = jnp.zeros_like(dgsv_ref)

        @pl.when(i < nL)
        def _():
            gate = gsv_ref[0, 0:1, :]
            yv = y_ref[...]
            xo = x_ref[...] + gate * yv
            lv, vjp = jax.vjp(lossf, xo, g_ref[...], t_ref[...])
            dxo, dg, _ = vjp(jnp.ones((1, 1), F32))
            loss_ref[...] += jnp.broadcast_to(lv, loss_ref.shape)
            dx_ref[...] = dxo
            dy_ref[...] = (gate * dxo).astype(dy_ref.dtype)
            dgsv_ref[0, 0:1, :] += jnp.sum(yv * dxo, axis=0, keepdims=True)
            dg_ref[0:1, :] += dg

        @pl.when(i >= nL)
        def _():
            dx_ref[...] = jnp.zeros_like(dx_ref)
            dy_ref[...] = jnp.zeros_like(dy_ref)

    row = pl.BlockSpec((TR, D), lambda i: (i, 0))
    gspec = pl.BlockSpec((1, 8, D), lambda i: (i // nL, 0, 0))
    return _pc(
        body, name=name, grid=(T // TR,),
        in_specs=[row, row, gspec, pl.BlockSpec((1, D), lambda i: (0, 0)),
                  pl.BlockSpec((TR, D), lambda i: (jnp.minimum(i, nL - 1), 0))],
        out_specs=[pl.BlockSpec((8, 128), lambda i: (0, 0)), row, row, gspec, pl.BlockSpec((8, D), lambda i: (0, 0))],
        out_shape=[_sds((8, 128), F32), _sds((T, D), F32), _sds((T, D), BF16), _sds((2, 8, D), F32), _sds((8, D), F32)],
        compiler_params=_cp(("arbitrary",), 32 << 20),
    )(x, y, gsv, g, target)


FI_BLK = 2 * D_FF // 4


def _fi_chip(j):
    return (j % 2) * 2 + j // 2


def matmul_fi(a, b, mode, out_dtype, name):
    T = a.shape[0]
    if mode == "tn":
        tmd = 512

        def body(a_ref, b_ref, o_ref):
            o_ref[0] = _dg(a_ref[...], b_ref[...], _DIMS["tn"]).astype(o_ref.dtype)

        return _pc(body, name=name, grid=(D // tmd, 4),
                   in_specs=[pl.BlockSpec((T, tmd), lambda i, j: (0, i)), pl.BlockSpec((T, FI_BLK), lambda i, j: (0, j))],
                   out_specs=pl.BlockSpec((1, tmd, FI_BLK), lambda i, j: (_fi_chip(j), i, 0)),
                   out_shape=_sds((4, D, FI_BLK), out_dtype), compiler_params=_cp(("parallel", "arbitrary"), 48 << 20))(a, b)
    if mode == "nn":
        tm = _div_tile(T, 1280, 16)

        def body(a_ref, b_ref, o_ref):
            o_ref[...] = _dg(a_ref[...], b_ref[0], _DIMS["nn"]).astype(o_ref.dtype)

        return _pc(body, name=name, grid=(T // tm, 4),
                   in_specs=[pl.BlockSpec((tm, D), lambda i, j: (i, 0)), pl.BlockSpec((1, D, FI_BLK), lambda i, j: (_fi_chip(j), 0, 0))],
                   out_specs=pl.BlockSpec((tm, FI_BLK), lambda i, j: (i, j)), out_shape=_sds((T, 4 * FI_BLK), out_dtype),
                   compiler_params=_cp(("parallel", "arbitrary"), 40 << 20))(a, b)
    tm = _div_tile(T, 640, 16)

    def body(a_ref, b_ref, o_ref):
        acc = None
        for k in range(4):
            part = _dg(a_ref[:, k * FI_BLK:(k + 1) * FI_BLK], b_ref[_fi_chip(k)], _DIMS["nt"])
            acc = part if acc is None else acc + part
        o_ref[...] = acc.astype(o_ref.dtype)

    return _pc(body, name=name, grid=(T // tm,),
               in_specs=[pl.BlockSpec((tm, 4 * FI_BLK), lambda i: (i, 0)), pl.BlockSpec((4, D, FI_BLK), lambda i: (0, 0, 0))],
               out_specs=pl.BlockSpec((tm, D), lambda i: (i, 0)), out_shape=_sds((T, D), out_dtype),
               compiler_params=_cp(("parallel",), VMEM_CAP))(a, b)


def _swiglu(gate, up):
    return _silu(gate) * up


def swiglu_fwd(gu, name):
    T = gu.shape[0]

    def body(x_ref, o_ref):
        o_ref[...] = _swiglu(x_ref[:, :FI_BLK].astype(F32), x_ref[:, FI_BLK:].astype(F32)).astype(o_ref.dtype)

    return _pc(body, name=name, grid=(T // TR, 2), in_specs=[pl.BlockSpec((TR, 2 * FI_BLK), lambda i, j: (i, j))],
               out_specs=pl.BlockSpec((TR, FI_BLK), lambda i, j: (i, j)), out_shape=_sds((T, D_FF), BF16),
               compiler_params=_cp(("parallel", "parallel"), 24 << 20))(gu)


def swiglu_bwd(gu, dact, name):
    T = gu.shape[0]

    def body(x_ref, d_ref, o_ref):
        _, vjp = jax.vjp(_swiglu, x_ref[:, :FI_BLK].astype(F32), x_ref[:, FI_BLK:].astype(F32))
        dg, du = vjp(d_ref[...].astype(F32))
        o_ref[:, :FI_BLK] = dg.astype(o_ref.dtype)
        o_ref[:, FI_BLK:] = du.astype(o_ref.dtype)

    return _pc(body, name=name, grid=(T // TR, 2),
               in_specs=[pl.BlockSpec((TR, 2 * FI_BLK), lambda i, j: (i, j)), pl.BlockSpec((TR, FI_BLK), lambda i, j: (i, j))],
               out_specs=pl.BlockSpec((TR, 2 * FI_BLK), lambda i, j: (i, j)), out_shape=_sds((T, 2 * D_FF), BF16),
               compiler_params=_cp(("parallel", "parallel"), 32 << 20))(gu, dact)


def rope_tables(L, Lc):
    t = np.arange(L)
    rows, cols = t // GRID_W, t % GRID_W
    inv = ROPE_BASE ** (-np.arange(16, dtype=np.float32) / 16)
    lane = np.arange(64)
    pos = np.where((lane // 32)[None, :] == 0, rows[:, None], cols[:, None]).astype(np.float32)
    ang = jnp.asarray(pos) * jnp.asarray(inv[lane % 16])[None, :]
    cos = jnp.concatenate([jnp.cos(ang), jnp.ones((Lc, 64), F32)], axis=0)
    sin = jnp.concatenate([jnp.sin(ang), jnp.zeros((Lc, 64), F32)], axis=0)
    R = np.zeros((128, 128), np.float32)
    for i in range(128):
        if (i % 32) < 16:
            R[i + 16, i] = -1.0
        else:
            R[i - 16, i] = 1.0
    return jnp.tile(cos, (1, 2)), jnp.tile(sin, (1, 2)), jnp.asarray(R)


def rope_apply(q_src, q_col, k_src, k_col, cos, sin, R, transpose, name, kv_src=None):
    T = cos.shape[0]
    with_kv = kv_src is not None

    def rot(x, c, s, Rm):
        if transpose:
            return x * c + hdot(x * s, Rm, "nt")
        return x * c + hdot(x, Rm) * s

    def body(q_ref, k_ref, c_ref, s_ref, R_ref, *rest):
        qo_ref, ko_ref = rest[-4:-2] if with_kv else rest
        c, s, Rm = c_ref[...], s_ref[...], R_ref[...]
        for j in range(2):
            qo_ref[:, j * 128:(j + 1) * 128] = rot(q_ref[:, j * 128:(j + 1) * 128].astype(F32), c, s, Rm).astype(qo_ref.dtype)
        ko_ref[...] = rot(k_ref[...].astype(F32), c, s, Rm).astype(ko_ref.dtype)
        if with_kv:
            rest[-2][...] = rest[0][...].astype(BF16)
            rest[-1][...] = rest[1][...].astype(BF16)

    tab = pl.BlockSpec((TR, 128), lambda i: (i, 0))
    wide = pl.BlockSpec((TR, 256), lambda i: (i, 0))
    kv_in = [pl.BlockSpec((TR, 256), lambda i: (i, C_KB // 256)), pl.BlockSpec((TR, 256), lambda i: (i, C_VB // 256))] if with_kv else []
    return _pc(body, name=name, grid=(T // TR,),
               in_specs=[pl.BlockSpec((TR, 256), lambda i: (i, q_col)), pl.BlockSpec((TR, 128), lambda i: (i, k_col)),
                         tab, tab, pl.BlockSpec((128, 128), lambda i: (0, 0))] + kv_in,
               out_specs=[wide, tab] + ([wide, wide] if with_kv else []),
               out_shape=[_sds((T, 256), BF16), _sds((T, 128), BF16)] + ([_sds((T, 256), BF16)] * 2 if with_kv else []),
               compiler_params=_cp(("parallel",), 16 << 20))(q_src, k_src, cos, sin, R, *([kv_src, kv_src] if with_kv else []))


_SCALE = HD ** -0.5


def _attn_tile(qh, ks, vs, extra):
    ss = []
    for k, add in ks:
        s = bdot(qh, k, "nt") * _SCALE
        ss.append(s if add is None else s + add)
    m = ss[0].max(axis=-1, keepdims=True)
    for s in ss[1:]:
        m = jnp.maximum(m, s.max(axis=-1, keepdims=True))
    if extra is not None:
        m = jnp.maximum(m, extra)
    ps = [jnp.exp(s - m) for s in ss]
    den = ps[0].sum(axis=-1, keepdims=True)
    for p in ps[1:]:
        den = den + p.sum(axis=-1, keepdims=True)
    if extra is not None:
        den = den + jnp.exp(extra - m)
    num = bdot(ps[0], vs[0], "nn")
    for p, v in zip(ps[1:], vs[1:]):
        num = num + bdot(p, v, "nn")
    return num / den


def _wa_mask(n, L):
    qpos = n * WA_BLK + lax.broadcasted_iota(jnp.int32, (WA_BLK, 3 * WA_BLK), 0)
    kpos = (n - 1) * WA_BLK + lax.broadcasted_iota(jnp.int32, (WA_BLK, 3 * WA_BLK), 1)
    ok = (jnp.abs(qpos - kpos) <= WA_BLK) & (kpos >= 0) & (kpos < L)
    return jnp.where(ok, 0.0, NEG).astype(F32)


WA_BPS = 2


def _wa_specs(L, Lc):
    nb = L // WA_BLK
    cb = L // Lc

    def blk(j, col):
        return pl.BlockSpec((WA_BLK, 128), lambda s: (jnp.clip(s * WA_BPS - 1 + j, 0, nb - 1), col))

    vcol = C_VA // 128
    kspecs = [blk(j, 0) for j in range(WA_BPS + 2)] + [pl.BlockSpec((Lc, 128), lambda s: (cb, 0))]
    vspecs = [blk(j, vcol) for j in range(WA_BPS + 2)] + [pl.BlockSpec((Lc, 128), lambda s: (cb, vcol))]
    return nb, kspecs, vspecs


def win_attn_fwd(qr, kr, P, sink, L, Lc, name):
    T = L + Lc
    nb, kspecs, vspecs = _wa_specs(L, Lc)
    nk = WA_BPS + 2
    QB = WA_BPS * WA_BLK
    nlat = nb // WA_BPS

    def body(q_ref, *refs):
        kbs, kx, vbs, vx, s_ref, o_ref = refs[:nk], refs[nk], refs[nk + 1:2 * nk + 1], refs[2 * nk + 1], refs[-2], refs[-1]
        s = pl.program_id(0)

        @pl.when(s < nlat)
        def _():
            for b in range(WA_BPS):
                mask = _wa_mask(s * WA_BPS + b, L)
                qs = slice(b * WA_BLK, (b + 1) * WA_BLK)
                for g in range(WA_KV):
                    sl = slice(g * HD, (g + 1) * HD)
                    k3 = jnp.concatenate([kbs[b + j][:, sl] for j in range(3)], axis=0)
                    v3 = jnp.concatenate([vbs[b + j][:, sl] for j in range(3)], axis=0)
                    for r in range(2):
                        hs = slice((2 * g + r) * HD, (2 * g + r + 1) * HD)
                        o = _attn_tile(q_ref[qs, hs], [(k3, mask), (kx[:, sl], None)], [v3, vx[:, sl]], s_ref[2 * g + r:2 * g + r + 1, 0:1])
                        o_ref[qs, hs] = o.astype(o_ref.dtype)

        @pl.when(s >= nlat)
        def _():
            for h in range(WA_HEADS):
                sl = slice((h // 2) * HD, (h // 2 + 1) * HD)
                o = _attn_tile(q_ref[:, h * HD:(h + 1) * HD], [(kx[:, sl], None)], [vx[:, sl]], s_ref[h:h + 1, 0:1])
                o_ref[:, h * HD:(h + 1) * HD] = o.astype(o_ref.dtype)

    qspec = pl.BlockSpec((QB, 256), lambda s: (s, 0))
    return _pc(body, name=name, grid=(T // QB,),
               in_specs=[qspec] + kspecs + vspecs + [pl.BlockSpec((8, 128), lambda s: (0, 0))],
               out_specs=qspec, out_shape=_sds((T, 256), BF16),
               compiler_params=_cp(("arbitrary",), 32 << 20))(qr, *([kr] * (nk + 1)), *([P] * (nk + 1)), sink)


def win_attn_bwd(qr, kr, P, sink, do_src, L, Lc, name):
    T = L + Lc
    nb, kspecs, vspecs = _wa_specs(L, Lc)
    nk = WA_BPS + 2
    QB = WA_BPS * WA_BLK
    nlat = nb // WA_BPS
    cx = WA_BLK + L

    def body(q_ref, *refs):
        kbs, kx, vbs, vx = refs[:nk], refs[nk], refs[nk + 1:2 * nk + 1], refs[2 * nk + 1]
        s_ref, do_ref, dq_ref, dk_ref, dv_ref, ds_ref = refs[2 * nk + 2:]
        s = pl.program_id(0)

        @pl.when(s == 0)
        def _():
            dk_ref[...] = jnp.zeros_like(dk_ref)
            dv_ref[...] = jnp.zeros_like(dv_ref)
            ds_ref[...] = jnp.zeros_like(ds_ref)

        @pl.when(s < nlat)
        def _():
            for b in range(WA_BPS):
                n = s * WA_BPS + b
                mask = _wa_mask(n, L)
                rows = pl.ds(pl.multiple_of(n * WA_BLK, WA_BLK), 3 * WA_BLK)
                qs = slice(b * WA_BLK, (b + 1) * WA_BLK)
                for g in range(WA_KV):
                    sl = slice(g * HD, (g + 1) * HD)
                    k3 = jnp.concatenate([kbs[b + j][:, sl] for j in range(3)], axis=0)
                    v3 = jnp.concatenate([vbs[b + j][:, sl] for j in range(3)], axis=0)
                    kxg, vxg = kx[:, sl], vx[:, sl]
                    acc = None
                    for r in range(2):
                        h = 2 * g + r
                        hs = slice(h * HD, (h + 1) * HD)
                        f = lambda q, k3_, v3_, kx_, vx_, s_: _attn_tile(q, [(k3_, mask), (kx_, None)], [v3_, vx_], s_)
                        _, vjp = jax.vjp(f, q_ref[qs, hs].astype(F32), k3.astype(F32), v3.astype(F32), kxg.astype(F32),
                                         vxg.astype(F32), s_ref[h:h + 1, 0:1])
                        dq, dk3, dv3, dkx, dvx, dsk = vjp(do_ref[qs, hs].astype(F32))
                        dq_ref[qs, hs] = dq
                        ds_ref[h:h + 1, :] += jnp.broadcast_to(dsk, (1, 128))
                        acc = (dk3, dv3, dkx, dvx) if acc is None else tuple(a + b_ for a, b_ in zip(acc, (dk3, dv3, dkx, dvx)))
                    dk_ref[rows, sl] += acc[0]
                    dv_ref[rows, sl] += acc[1]
                    dk_ref[cx:cx + Lc, sl] += acc[2]
                    dv_ref[cx:cx + Lc, sl] += acc[3]

        @pl.when(s >= nlat)
        def _():
            for h in range(WA_HEADS):
                sl = slice((h // 2) * HD, (h // 2 + 1) * HD)
                hs = slice(h * HD, (h + 1) * HD)
                f = lambda q, kx_, vx_, s_: _attn_tile(q, [(kx_, None)], [vx_], s_)
                _, vjp = jax.vjp(f, q_ref[:, hs].astype(F32), kx[:, sl].astype(F32), vx[:, sl].astype(F32), s_ref[h:h + 1, 0:1])
                dq, dkx, dvx, dsk = vjp(do_ref[:, hs].astype(F32))
                dq_ref[:, hs] = dq
                ds_ref[h:h + 1, :] += jnp.broadcast_to(dsk, (1, 128))
                dk_ref[cx:cx + Lc, sl] += dkx
                dv_ref[cx:cx + Lc, sl] += dvx

    qspec = pl.BlockSpec((QB, 256), lambda s: (s, 0))
    acc_spec = pl.BlockSpec((T + 2 * WA_BLK, 128), lambda s: (0, 0))
    return _pc(body, name=name, grid=(T // QB,),
               in_specs=[qspec] + kspecs + vspecs + [pl.BlockSpec((8, 128), lambda s: (0, 0)), qspec],
               out_specs=[qspec, acc_spec, acc_spec, pl.BlockSpec((8, 128), lambda s: (0, 0))],
               out_shape=[_sds((T, 256), F32), _sds((T + 2 * WA_BLK, 128), F32), _sds((T + 2 * WA_BLK, 128), F32), _sds((8, 128), F32)],
               compiler_params=_cp(("arbitrary",), 40 << 20))(qr, *([kr] * (nk + 1)), *([P] * (nk + 1)), sink, do_src)


def na_index_tables():
    qc = np.arange(GRID_W)[:, None]
    kc = np.arange(GRID_W)[None, :]
    cstart = np.clip(qc - NA_KW // 2, 0, GRID_W - NA_KW)
    ok = (kc >= cstart) & (kc < cstart + NA_KW)
    dx = np.clip(kc - qc, -(NA_KW - 1), NA_KW - 1) + (NA_KW - 1)
    off = np.arange(NA_KH)[:, None]
    kr = np.arange(NA_KH)[None, :]
    dy = kr - off + (NA_KH - 1)
    return ok, dx, dy


def _na_selectors():
    ok, dx, dy = na_index_tables()
    e1 = np.zeros((GRID_W * GRID_W, 128), np.float32)
    qi, ki = np.nonzero(ok)
    e1[qi * GRID_W + ki, dx[qi, ki]] = 1.0
    e2 = np.zeros((16, NA_KH * NA_KH), np.float32)
    oi, ri = np.meshgrid(np.arange(NA_KH), np.arange(NA_KH), indexing="ij")
    e2[dy[oi, ri].ravel(), (oi * NA_KH + ri).ravel()] = 1.0
    return ok, jnp.asarray(e1), jnp.asarray(np.kron(np.eye(NA_HEADS, dtype=np.float32), e2))


def na_bias_table(rpb, tag):
    ok, e1, e2 = _na_selectors()
    r2 = jnp.pad(rpb.astype(F32), ((0, 0), (0, 1), (0, 128 - (2 * NA_KW - 1)))).reshape(NA_HEADS * 16, 128)
    r1 = matmul(e2, r2, "tn", F32, f"na_bias_sel1_{tag}", hi=True)
    x = matmul(r1, e1, "nt", F32, f"na_bias_sel2_{tag}", hi=True)
    b = x.reshape(NA_HEADS, NA_KH, NA_KH, GRID_W, GRID_W).transpose(0, 1, 3, 2, 4)
    b = b + jnp.asarray(np.where(ok, 0.0, NEG).astype(np.float32))[None, None, :, None, :]
    return b.reshape(NA_HEADS, NA_KH, GRID_W, NA_KH * GRID_W)


def _na_rows(r, GR):
    r0 = jnp.clip(r - NA_KH // 2, 0, GR - NA_KH)
    return r0, jnp.clip(r - r0, 0, NA_KH - 1)


NA_RPS = 4


def na_fwd(P, kb, vb, bias, L, Lc, name):
    T = L + Lc
    GR = L // GRID_W
    W = NA_KH * GRID_W
    QB = GRID_W * NA_RPS
    nlat = GR // NA_RPS

    def body(q_ref, k_ref, v_ref, b_ref, o_ref):
        s = pl.program_id(0)

        @pl.when(s < nlat)
        def _():
            for rr in range(NA_RPS):
                r0, off = _na_rows(s * NA_RPS + rr, GR)
                rows = pl.ds(pl.multiple_of(r0 * GRID_W, GRID_W), W)
                qs = slice(rr * GRID_W, (rr + 1) * GRID_W)
                for h in range(NA_HEADS):
                    hs = slice(h * HD, (h + 1) * HD)
                    o = _attn_tile(q_ref[qs, hs], [(k_ref[rows, hs], b_ref[h, off]), (k_ref[L:T, hs], None)],
                                   [v_ref[rows, hs], v_ref[L:T, hs]], None)
                    o_ref[qs, hs] = o.astype(o_ref.dtype)

        @pl.when(s >= nlat)
        def _():
            for h in range(NA_HEADS):
                hs = slice(h * HD, (h + 1) * HD)
                o = _attn_tile(q_ref[:, hs], [(k_ref[L:T, hs], None)], [v_ref[L:T, hs]], None)
                o_ref[:, hs] = o.astype(o_ref.dtype)

    one = pl.Buffered(1)
    return _pc(body, name=name, grid=(T // QB,),
               in_specs=[pl.BlockSpec((QB, 256), lambda r: (r, C_QB // 256)),
                         pl.BlockSpec((T, 256), lambda r: (0, 0), pipeline_mode=one),
                         pl.BlockSpec((T, 256), lambda r: (0, 0), pipeline_mode=one),
                         pl.BlockSpec((NA_HEADS, NA_KH, GRID_W, W), lambda r: (0, 0, 0, 0), pipeline_mode=one)],
               out_specs=pl.BlockSpec((QB, 256), lambda r: (r, 0)), out_shape=_sds((T, 256), BF16),
               compiler_params=_cp(("arbitrary",), 32 << 20))(P, kb, vb, bias)


def na_bwd(P, kb, vb, bias, do_src, L, Lc, name):
    T = L + Lc
    GR = L // GRID_W
    W = NA_KH * GRID_W
    QB = GRID_W * NA_RPS
    nlat = GR // NA_RPS

    def body(q_ref, k_ref, v_ref, b_ref, do_ref, dq_ref, dk_ref, dv_ref, db_ref):
        s = pl.program_id(0)

        @pl.when(s == 0)
        def _():
            dk_ref[...] = jnp.zeros_like(dk_ref)
            dv_ref[...] = jnp.zeros_like(dv_ref)
            db_ref[...] = jnp.zeros_like(db_ref)

        @pl.when(s < nlat)
        def _():
            for rr in range(NA_RPS):
                r0, off = _na_rows(s * NA_RPS + rr, GR)
                rows = pl.ds(pl.multiple_of(r0 * GRID_W, GRID_W), W)
                qs = slice(rr * GRID_W, (rr + 1) * GRID_W)
                for h in range(NA_HEADS):
                    hs = slice(h * HD, (h + 1) * HD)
                    f = lambda q, kw, vw, kx, vx, b: _attn_tile(q, [(kw, b), (kx, None)], [vw, vx], None)
                    _, vjp = jax.vjp(f, q_ref[qs, hs].astype(F32), k_ref[rows, hs].astype(F32), v_ref[rows, hs].astype(F32),
                                     k_ref[L:T, hs].astype(F32), v_ref[L:T, hs].astype(F32), b_ref[h, off])
                    dq, dkw, dvw, dkx, dvx, db = vjp(do_ref[qs, hs].astype(F32))
                    dq_ref[qs, hs] = dq.astype(dq_ref.dtype)
                    dk_ref[rows, hs] += dkw
                    dv_ref[rows, hs] += dvw
                    dk_ref[L:T, hs] += dkx
                    dv_ref[L:T, hs] += dvx
                    db_ref[h, off] += db

        @pl.when(s >= nlat)
        def _():
            for h in range(NA_HEADS):
                hs = slice(h * HD, (h + 1) * HD)
                f = lambda q, kx, vx: _attn_tile(q, [(kx, None)], [vx], None)
                _, vjp = jax.vjp(f, q_ref[:, hs].astype(F32), k_ref[L:T, hs].astype(F32), v_ref[L:T, hs].astype(F32))
                dq, dkx, dvx = vjp(do_ref[:, hs].astype(F32))
                dq_ref[:, hs] = dq.astype(dq_ref.dtype)
                dk_ref[L:T, hs] += dkx
                dv_ref[L:T, hs] += dvx

    one = pl.Buffered(1)
    full = lambda shape: pl.BlockSpec(shape, lambda r: (0,) * len(shape), pipeline_mode=one)
    return _pc(body, name=name, grid=(T // QB,),
               in_specs=[pl.BlockSpec((QB, 256), lambda r: (r, C_QB // 256)), full((T, 256)), full((T, 256)),
                         full((NA_HEADS, NA_KH, GRID_W, W)), pl.BlockSpec((QB, 256), lambda r: (r, 1))],
               out_specs=[pl.BlockSpec((QB, 256), lambda r: (r, 0)), full((T, 256)), full((T, 256)),
                          full((NA_HEADS, NA_KH, GRID_W, W))],
               out_shape=[_sds((T, 256), BF16), _sds((T, 256), F32), _sds((T, 256), F32), _sds((NA_HEADS, NA_KH, GRID_W, W), F32)],
               compiler_params=_cp(("arbitrary",), 48 << 20))(P, kb, vb, bias, do_src)


def na_rpb_grad(dbias, tag):
    _, e1, e2 = _na_selectors()
    x = dbias.reshape(NA_HEADS, NA_KH, GRID_W, NA_KH, GRID_W).transpose(0, 1, 3, 2, 4).reshape(NA_HEADS * NA_KH * NA_KH, GRID_W * GRID_W)
    r1 = matmul(x, e1, "nn", F32, f"na_rpb_sel1_{tag}", hi=True, tk=1024)
    r2 = matmul(e2, r1, "nn", F32, f"na_rpb_sel2_{tag}", hi=True)
    return r2.reshape(NA_HEADS, 16, 128)[:, :2 * NA_KH - 1, :2 * NA_KW - 1]


_HALO = 8


def _halo_specs(T, col0):
    nh = TR // _HALO
    cur = pl.BlockSpec((TR, 256), lambda i, j: (i, col0 + j))
    prv = pl.BlockSpec((_HALO, 256), lambda i, j: (jnp.maximum(i * nh - 1, 0), col0 + j))
    nxt = pl.BlockSpec((_HALO, 256), lambda i, j: (jnp.minimum((i + 1) * nh, T // _HALO - 1), col0 + j))
    return prv, cur, nxt


def _fill_ext(ext, prv, cur, nxt, i, nL, nT):
    has_prev = jnp.where((i != 0) & (i != nL), 1.0, 0.0)
    has_next = jnp.where((i != nL - 1) & (i != nT - 1), 1.0, 0.0)
    ext[0:_HALO, :] = prv[...].astype(F32) * has_prev
    ext[_HALO:_HALO + TR, :] = cur[...].astype(F32)
    ext[_HALO + TR:, :] = nxt[...].astype(F32) * has_next


def conv_silu_fwd(P, w8, b, nL, name):
    T = P.shape[0]
    nT = T // TR

    def body(prv, cur, nxt, w_ref, b_ref, pre_ref, act_ref, ext):
        i = pl.program_id(0)
        _fill_ext(ext, prv, cur, nxt, i, nL, nT)
        y = jnp.broadcast_to(b_ref[...], (TR, 256))
        for k in range(S_CONV):
            y = y + w_ref[k:k + 1, :] * ext[pl.ds(_HALO - S_CONV // 2 + k, TR), :]
        pre_ref[...] = y
        act_ref[...] = _silu(y)

    prv, cur, nxt = _halo_specs(T, C_XBC // 256)
    out = pl.BlockSpec((TR, 256), lambda i, j: (i, j))
    return _pc(body, name=name, grid=(nT, 4),
               in_specs=[prv, cur, nxt, pl.BlockSpec((8, 256), lambda i, j: (0, j)), pl.BlockSpec((1, 256), lambda i, j: (0, j))],
               out_specs=[out, out], out_shape=[_sds((T, 1024), F32), _sds((T, 1024), F32)],
               scratch_shapes=[pltpu.VMEM((TR + 2 * _HALO, 256), F32)],
               compiler_params=_cp(("parallel", "parallel"), 16 << 20))(P, P, P, w8, b)


def dsilu(pre, dxs_list, db_list, dc_list, name):
    T = pre.shape[0]
    n1, n2, n3 = len(dxs_list), len(db_list), len(dc_list)

    def body(*refs):
        pre_ref = refs[0]
        ins = refs[1:1 + n1 + n2 + n3]
        out = refs[-1]

        def part(rs, lo, hi):
            g = rs[0][...].astype(F32)
            for r in rs[1:]:
                g = g + r[...].astype(F32)
            _, vjp = jax.vjp(_silu, pre_ref[:, lo:hi])
            out[:, lo:hi] = vjp(g)[0]

        part(ins[:n1], 0, 512)
        part(ins[n1:n1 + n2], 512, 768)
        part(ins[n1 + n2:], 768, 1024)

    spec = lambda w: pl.BlockSpec((TR, w), lambda i: (i, 0))
    return _pc(body, name=name, grid=(T // TR,),
               in_specs=[spec(1024)] + [spec(512)] * n1 + [spec(256)] * (n2 + n3),
               out_specs=spec(1024), out_shape=_sds((T, 1024), F32),
               compiler_params=_cp(("parallel",), 32 << 20))(pre, *dxs_list, *db_list, *dc_list)


def conv_bwd(dpre, P, w8, nL, name):
    T = P.shape[0]
    nT = T // TR

    def body(dp, dc, dn, xp, xc, xn, w_ref, dx_ref, dw_ref, db_ref, extd, extx):
        i = pl.program_id(1)
        _fill_ext(extd, dp, dc, dn, i, nL, nT)
        _fill_ext(extx, xp, xc, xn, i, nL, nT)

        @pl.when(i == 0)
        def _():
            dw_ref[...] = jnp.zeros_like(dw_ref)
            db_ref[...] = jnp.zeros_like(db_ref)

        d = dc[...]
        dx = jnp.zeros((TR, 256), F32)
        for k in range(S_CONV):
            dx = dx + w_ref[k:k + 1, :] * extd[pl.ds(_HALO + S_CONV // 2 - k, TR), :]
            dw_ref[k:k + 1, :] += jnp.sum(d * extx[pl.ds(_HALO - S_CONV // 2 + k, TR), :], axis=0, keepdims=True)
        dx_ref[...] = dx.astype(dx_ref.dtype)
        db_ref[0:1, :] += jnp.sum(d, axis=0, keepdims=True)

    def swap(spec):
        f = spec.index_map
        return pl.BlockSpec(spec.block_shape, lambda j, i: f(i, j))

    dprv, dcur, dnxt = [swap(s) for s in _halo_specs(T, 0)]
    xprv, xcur, xnxt = [swap(s) for s in _halo_specs(T, C_XBC // 256)]
    acc = pl.BlockSpec((8, 256), lambda j, i: (0, j))
    return _pc(body, name=name, grid=(4, nT),
               in_specs=[dprv, dcur, dnxt, xprv, xcur, xnxt, acc],
               out_specs=[pl.BlockSpec((TR, 256), lambda j, i: (i, j)), acc, acc],
               out_shape=[_sds((T, 1024), BF16), _sds((8, 1024), F32), _sds((8, 1024), F32)],
               scratch_shapes=[pltpu.VMEM((TR + 2 * _HALO, 256), F32), pltpu.VMEM((TR + 2 * _HALO, 256), F32)],
               compiler_params=_cp(("parallel", "arbitrary"), 16 << 20))(dpre, dpre, dpre, P, P, P, w8)


def _onehot_row(h, n):
    return (lax.broadcasted_iota(jnp.int32, (1, n), 1) == h).astype(F32)


def _onehot_col(h, n):
    return (lax.broadcasted_iota(jnp.int32, (n, 1), 0) == h).astype(F32)


def _ssd_chunk(xs, dtr, dtb, alog, bm, cm, hin, reverse):
    Qn = S_Q
    ii = lax.broadcasted_iota(jnp.int32, (Qn, Qn), 0)
    jj = lax.broadcasted_iota(jnp.int32, (Qn, Qn), 1)
    keep = (ii <= jj) if reverse else (ii >= jj)
    tri = keep.astype(F32)
    triT = ((jj <= ii) if reverse else (jj >= ii)).astype(F32)
    eye = (ii == jj).astype(F32)
    dt = _softplus(dtr + dtb)
    a = dt * (-jnp.exp(alog))
    cs = hdot(tri, a)
    csT = hdot(a, triT, "tn")
    dtT = hdot(dt, eye, "tn")
    last = _onehot_row(0 if reverse else Qn - 1, Qn)
    ys, houts = [], []
    for g in range(S_GROUPS):
        G = bdot(cm[g], bm[g], "nt")
        for r in range(S_HEADS // S_GROUPS):
            h = g * (S_HEADS // S_GROUPS) + r
            eh_r, eh_c = _onehot_row(h, S_HEADS), _onehot_col(h, S_HEADS)
            cs_c = jnp.sum(cs * eh_r, axis=1, keepdims=True)
            dt_c = jnp.sum(dt * eh_r, axis=1, keepdims=True)
            cs_r = jnp.sum(csT * eh_c, axis=0, keepdims=True)
            dt_r = jnp.sum(dtT * eh_c, axis=0, keepdims=True)
            tot = jnp.sum(cs_r * last, axis=1, keepdims=True)
            decay = jnp.exp(jnp.where(keep, cs_c - cs_r, NEG))
            w = G * decay * dt_r
            y = bdot(w, xs[h], "nn") + bdot(cm[g], hin[h], "nt") * jnp.exp(cs_c)
            xsc = xs[h] * (jnp.exp(tot - cs_c) * dt_c)
            hout = hin[h] * jnp.exp(tot) + bdot(xsc, bm[g], "tn")
            ys.append(y)
            houts.append(hout)
    return ys, houts


def _ssd_orders(L, Lc):
    nl, ncx = L // S_Q, Lc // S_Q
    fwd = lambda s: jnp.where(s < ncx, nl + s, s - ncx)
    bwd = lambda s: nl + ncx - 1 - s
    return nl + ncx, fwd, bwd


def _ssd_in_specs(fo, bo, step):
    def at(order, w, col):
        return pl.BlockSpec((S_Q, w), lambda u: (order(step(u)), col))
    specs = []
    for order in (fo, bo):
        specs += [at(order, 512, 0), at(order, 256, 2), at(order, 256, 3), at(order, 128, C_DT // 128)]
    return specs


def ssd_fwd(act, P, dtb, alog, L, Lc, name):
    T = L + Lc
    ns, fo, bo = _ssd_orders(L, Lc)

    def body(xf, bf, cf, df, xb, bb, cb, db, dtb_ref, al_ref, yf, yb, hsf, hsb, Hf, Hb):
        s = pl.program_id(0)

        @pl.when(s == 0)
        def _():
            Hf[...] = jnp.zeros_like(Hf)
            Hb[...] = jnp.zeros_like(Hb)

        for d, (x_r, b_r, c_r, dt_r, y_r, hs_r, H) in enumerate(((xf, bf, cf, df, yf, hsf, Hf), (xb, bb, cb, db, yb, hsb, Hb))):
            hin = [H[h] for h in range(S_HEADS)]
            hs_r[0] = H[...]
            ys, houts = _ssd_chunk(
                [x_r[:, h * S_P:(h + 1) * S_P] for h in range(S_HEADS)], dt_r[:, d * 8:(d + 1) * 8],
                dtb_ref[d:d + 1, 0:8], al_ref[d:d + 1, 0:8],
                [b_r[:, g * S_N:(g + 1) * S_N] for g in range(S_GROUPS)], [c_r[:, g * S_N:(g + 1) * S_N] for g in range(S_GROUPS)],
                hin, reverse=(d == 1))
            for h in range(S_HEADS):
                y_r[:, h * S_P:(h + 1) * S_P] = ys[h]
                H[h] = houts[h]

    ident = lambda u: u
    small = pl.BlockSpec((8, 128), lambda u: (0, 0))
    hspec = pl.BlockSpec((1, S_HEADS, S_P, S_N), lambda u: (u, 0, 0, 0))
    return _pc(body, name=name, grid=(ns,),
               in_specs=_ssd_in_specs(fo, bo, ident) + [small, small],
               out_specs=[pl.BlockSpec((S_Q, 512), lambda u: (fo(u), 0)), pl.BlockSpec((S_Q, 512), lambda u: (bo(u), 0)), hspec, hspec],
               out_shape=[_sds((T, 512), F32), _sds((T, 512), F32), _sds((ns, S_HEADS, S_P, S_N), F32), _sds((ns, S_HEADS, S_P, S_N), F32)],
               scratch_shapes=[pltpu.VMEM((S_HEADS, S_P, S_N), F32), pltpu.VMEM((S_HEADS, S_P, S_N), F32)],
               compiler_params=_cp(("arbitrary",), 32 << 20))(act, act, act, P, act, act, act, P, dtb, alog)


def ssd_bwd(act, P, dtb, alog, hsf, hsb, dy, L, Lc, name):
    T = L + Lc
    ns, fo, bo = _ssd_orders(L, Lc)
    step = lambda u: ns - 1 - u

    def body(xf, bf, cf, df, xb, bb, cb, db, dtb_ref, al_ref, hsf_r, hsb_r, dyf, dyb,
             dxf, dbf, dcf, ddf, dxb, dbb, dcb, ddb, ddtb, dal, dHf, dHb):
        u = pl.program_id(0)

        @pl.when(u == 0)
        def _():
            dHf[...] = jnp.zeros_like(dHf)
            dHb[...] = jnp.zeros_like(dHb)
            ddtb[...] = jnp.zeros_like(ddtb)
            dal[...] = jnp.zeros_like(dal)

        dirs = ((xf, bf, cf, df, hsf_r, dyf, dxf, dbf, dcf, ddf, dHf), (xb, bb, cb, db, hsb_r, dyb, dxb, dbb, dcb, ddb, dHb))
        for d, (x_r, b_r, c_r, dt_r, hs_r, dy_r, dx_o, db_o, dc_o, dd_o, dH) in enumerate(dirs):
            f = functools.partial(_ssd_chunk, reverse=(d == 1))
            _, vjp = jax.vjp(
                f, [x_r[:, h * S_P:(h + 1) * S_P] for h in range(S_HEADS)], dt_r[:, d * 8:(d + 1) * 8],
                dtb_ref[d:d + 1, 0:8], al_ref[d:d + 1, 0:8],
                [b_r[:, g * S_N:(g + 1) * S_N] for g in range(S_GROUPS)], [c_r[:, g * S_N:(g + 1) * S_N] for g in range(S_GROUPS)],
                [hs_r[0, h] for h in range(S_HEADS)])
            gx, gdt, gdtb, gal, gb, gc, gh = vjp(([dy_r[:, h * S_P:(h + 1) * S_P] for h in range(S_HEADS)],
                                                  [dH[h] for h in range(S_HEADS)]))
            for h in range(S_HEADS):
                dx_o[:, h * S_P:(h + 1) * S_P] = gx[h]
                dH[h] = gh[h]
            for g in range(S_GROUPS):
                db_o[:, g * S_N:(g + 1) * S_N] = gb[g]
                dc_o[:, g * S_N:(g + 1) * S_N] = gc[g]
            dd_o[...] = gdt
            ddtb[d:d + 1, 0:8] += gdtb
            dal[d:d + 1, 0:8] += gal

    small = pl.BlockSpec((8, 128), lambda u: (0, 0))
    hspec = pl.BlockSpec((1, S_HEADS, S_P, S_N), lambda u: (step(u), 0, 0, 0))
    at = lambda order, w: pl.BlockSpec((S_Q, w), lambda u: (order(step(u)), 0))
    outs = []
    for order in (fo, bo):
        outs += [at(order, 512), at(order, 256), at(order, 256), at(order, 8)]
    oshape = [_sds((T, 512), F32), _sds((T, 256), F32), _sds((T, 256), F32), _sds((T, 8), F32)]
    return _pc(body, name=name, grid=(ns,),
               in_specs=_ssd_in_specs(fo, bo, step) + [small, small, hspec, hspec, at(fo, 512), at(bo, 512)],
               out_specs=outs + [small, small], out_shape=oshape + oshape + [_sds((8, 128), F32), _sds((8, 128), F32)],
               scratch_shapes=[pltpu.VMEM((S_HEADS, S_P, S_N), F32), pltpu.VMEM((S_HEADS, S_P, S_N), F32)],
               compiler_params=_cp(("arbitrary",), 40 << 20))(act, act, act, P, act, act, act, P, dtb, alog, hsf, hsb, dy, dy)


def _ssm_out(yf, yb, xs, z, dskip, g):
    y = (yf + yb + dskip * xs) * _silu(z)
    return (y * lax.rsqrt(jnp.mean(y * y, axis=-1, keepdims=True) + EPS)) * g


def ssm_out_fwd(yf, yb, act, P, dskip, g, name):
    T = yf.shape[0]

    def body(yf_r, yb_r, xs_r, z_r, d_r, g_r, o_r):
        o_r[...] = _ssm_out(yf_r[...], yb_r[...], xs_r[...], z_r[...], d_r[...], g_r[...]).astype(o_r.dtype)

    row = pl.BlockSpec((TR, 512), lambda i: (i, 0))
    vec = pl.BlockSpec((1, 512), lambda i: (0, 0))
    return _pc(body, name=name, grid=(T // TR,),
               in_specs=[row, row, row, pl.BlockSpec((TR, 512), lambda i: (i, C_Z // 512)), vec, vec],
               out_specs=row, out_shape=_sds((T, 512), BF16),
               compiler_params=_cp(("parallel",), 16 << 20))(yf, yb, act, P, dskip, g)


def ssm_out_bwd(yf, yb, act, P, dskip, g, do_src, name):
    T = yf.shape[0]

    def body(yf_r, yb_r, xs_r, z_r, d_r, g_r, do_r, dy_r, dxs_r, dz_r, dv_r):
        @pl.when(pl.program_id(0) == 0)
        def _():
            dv_r[...] = jnp.zeros_like(dv_r)

        _, vjp = jax.vjp(_ssm_out, yf_r[...], yb_r[...], xs_r[...], z_r[...], d_r[...], g_r[...])
        dyf, _, dxs, dz, dd, dg = vjp(do_r[...].astype(F32))
        dy_r[...] = dyf
        dxs_r[...] = dxs
        dz_r[...] = dz.astype(dz_r.dtype)
        dv_r[0:1, :] += dd
        dv_r[1:2, :] += dg

    row = pl.BlockSpec((TR, 512), lambda i: (i, 0))
    vec = pl.BlockSpec((1, 512), lambda i: (0, 0))
    return _pc(body, name=name, grid=(T // TR,),
               in_specs=[row, row, row, pl.BlockSpec((TR, 512), lambda i: (i, C_Z // 512)), vec, vec,
                         pl.BlockSpec((TR, 512), lambda i: (i, 1))],
               out_specs=[row, row, row, pl.BlockSpec((8, 512), lambda i: (0, 0))],
               out_shape=[_sds((T, 512), F32), _sds((T, 512), F32), _sds((T, 512), BF16), _sds((8, 512), F32)],
               compiler_params=_cp(("arbitrary",), 24 << 20))(yf, yb, act, P, dskip, g, do_src)


def add_halves(xv, got, cvec, name):
    n, r, cdim = xv.shape
    h = r // 2

    def body(c_ref, x_ref, g_ref, o_ref):
        o_ref[...] = (x_ref[...].astype(F32) + g_ref[...].astype(F32)).astype(o_ref.dtype)

    gs = pltpu.PrefetchScalarGridSpec(
        num_scalar_prefetch=1, grid=(n,),
        in_specs=[pl.BlockSpec((1, h, cdim), lambda k, c_ref: (k, c_ref[0], 0)), pl.BlockSpec((1, h, cdim), lambda k, c_ref: (k, 0, 0))],
        out_specs=pl.BlockSpec((1, h, cdim), lambda k, c_ref: (k, 0, 0)))
    return _pc(body, name=name, grid_spec=gs, out_shape=_sds((n, h, cdim), BF16),
               compiler_params=_cp(("arbitrary",), 24 << 20))(cvec, xv, got)


def sum_slots(a, name):
    n, r, cdim = a.shape
    tr = _div_tile(r, 512, 16)

    def body(a_ref, o_ref):
        acc = a_ref[0].astype(F32)
        for k in range(1, n):
            acc = acc + a_ref[k].astype(F32)
        o_ref[...] = acc

    return _pc(body, name=name, grid=(r // tr,), in_specs=[pl.BlockSpec((n, tr, cdim), lambda i: (0, i, 0))],
               out_specs=pl.BlockSpec((tr, cdim), lambda i: (i, 0)), out_shape=_sds((r, cdim), F32),
               compiler_params=_cp(("parallel",), 32 << 20))(a)


def adamw(w, g, m, v, name):
    B, R, C = w.shape
    tr = _div_tile(R, max(8, (1 << 19) // max(C, 1) // 8 * 8), 8) if R % 8 == 0 else R
    c1 = 1.0 / (1.0 - ADAM_B1 ** ADAM_STEP)
    c2 = 1.0 / (1.0 - ADAM_B2 ** ADAM_STEP)

    def body(w_ref, g_ref, m_ref, v_ref, d_ref, mo_ref, vo_ref):
        gg = g_ref[...]
        mn = ADAM_B1 * m_ref[...] + (1.0 - ADAM_B1) * gg
        vn = ADAM_B2 * v_ref[...] + (1.0 - ADAM_B2) * (gg * gg)
        d_ref[...] = -ADAM_LR * ((mn * c1) / (jnp.sqrt(vn * c2) + ADAM_EPS) + ADAM_WD * w_ref[...])
        mo_ref[...] = mn
        vo_ref[...] = vn

    spec = pl.BlockSpec((1, tr, C), lambda b, i: (b, i, 0))
    return _pc(body, name=name, grid=(B, R // tr), in_specs=[spec] * 4, out_specs=[spec] * 3,
               out_shape=[_sds((B, R, C), F32)] * 3, compiler_params=_cp(("parallel", "parallel"), 32 << 20))(w, g, m, v)


def _me():
    return lax.axis_index("x"), lax.axis_index("y"), lax.axis_index("c")


def _flip(v, bit):
    return 1 - v if bit else v


def allgather8(xv, name):
    R = xv.shape[0]

    def body(x_ref, out_ref, sum_ref, send_sems, recv_sems):
        mx, my, mc = _me()
        me = 4 * mx + 2 * my + mc
        out_ref[me] = x_ref[...]
        sends, recvs = [], []
        for k in range(1, 8):
            px, py, pc = _flip(mx, k & 4), _flip(my, k & 2), _flip(mc, k & 1)
            peer = 4 * px + 2 * py + pc
            sends.append(pltpu.make_async_remote_copy(src_ref=x_ref, dst_ref=out_ref.at[me], send_sem=send_sems.at[k - 1],
                                                      recv_sem=recv_sems.at[k - 1], device_id=(px, py, pc), device_id_type=MESH))
            recvs.append(pltpu.make_async_remote_copy(src_ref=x_ref, dst_ref=out_ref.at[peer], send_sem=send_sems.at[k - 1],
                                                      recv_sem=recv_sems.at[k - 1], device_id=(px, py, pc), device_id_type=MESH))
        for cp in sends:
            cp.start()
        for cp in recvs:
            cp.wait_recv()
        for cp in sends:
            cp.wait_send()
        acc = out_ref[0]
        for d in range(1, 8):
            acc = acc + out_ref[d]
        sum_ref[...] = acc

    vm = pl.BlockSpec(memory_space=pltpu.VMEM)
    return _pc(body, name=name, pin=False, in_specs=[vm], out_specs=[vm, vm], out_shape=[_sds((8, R, 128), F32), _sds((R, 128), F32)],
               scratch_shapes=[pltpu.SemaphoreType.DMA((7,)), pltpu.SemaphoreType.DMA((7,))],
               compiler_params=_cp(None, 32 << 20))(xv)


def _other_chips(mx, my):
    return [(1 - mx, my), (mx, 1 - my), (1 - mx, 1 - my)]


def _halves(r, mc, mult):
    h = r // 2
    return pl.ds(pl.multiple_of(mc * h, mult), h), pl.ds(pl.multiple_of((1 - mc) * h, mult), h)


def _rcopy(src, dst, send_sems, recv_sems, k, to):
    return pltpu.make_async_remote_copy(src_ref=src, dst_ref=dst, send_sem=send_sems.at[k], recv_sem=recv_sems.at[k],
                                        device_id=to, device_id_type=MESH)


def _gather_body(xs, outs, send_sems, recv_sems):
    n = len(xs)
    mx, my, mc = _me()
    chip = 2 * mx + my
    sib = (mx, my, 1 - mc)
    chips = _other_chips(mx, my)
    idx = [2 * cx + cy for cx, cy in chips]
    cp = functools.partial(_rcopy, send_sems=send_sems, recv_sems=recv_sems)
    hv = [_halves(x.shape[0], mc, 16) for x in xs]
    first, passed = [], []
    for a in range(n):
        for j, (cx, cy) in enumerate(chips):
            first.append(cp(xs[a].at[hv[a][0]], outs[a].at[chip, hv[a][0]], k=6 * a + j, to=(cx, cy, mc)))
            first[-1].start()
    for a in range(n):
        for j in range(3):
            cp(xs[a].at[hv[a][0]], outs[a].at[idx[j], hv[a][0]], k=6 * a + j, to=sib).wait_recv()
            passed.append(cp(outs[a].at[idx[j], hv[a][0]], outs[a].at[idx[j], hv[a][0]], k=6 * a + 3 + j, to=sib))
            passed[-1].start()
    for a in range(n):
        for j in range(3):
            cp(xs[a].at[hv[a][1]], outs[a].at[idx[j], hv[a][1]], k=6 * a + 3 + j, to=sib).wait_recv()
    for c_ in first + passed:
        c_.wait_send()


def _my_chip():
    return 2 * lax.axis_index("x") + lax.axis_index("y")


def _own_slots(outs, shards):
    return [lax.dynamic_update_index_in_dim(o, x, _my_chip(), 0) for o, x in zip(outs, shards)]


def gather_weights(shards, name):
    n = len(shards)

    def body(*refs):
        _gather_body(refs[:n], refs[n:2 * n], *refs[2 * n:])

    hbm = pl.BlockSpec(memory_space=pl.ANY)
    outs = _pc(body, name=name, in_specs=[hbm] * n, out_specs=[hbm] * n, out_shape=[_sds((4,) + x.shape, x.dtype) for x in shards],
               scratch_shapes=[pltpu.SemaphoreType.DMA((6 * n,)), pltpu.SemaphoreType.DMA((6 * n,))])(*shards)
    return _own_slots(outs, shards)


GATHER_REST_ID = 3


def gather_weights_sc(shards, name):
    n = len(shards)
    x_refs = [jax.new_ref(x, memory_space=pltpu.MemorySpace.HBM) for x in shards]
    out_refs = [jax.empty_ref(_sds((4,) + x.shape, x.dtype), memory_space=pltpu.MemorySpace.HBM) for x in shards]

    @pl.kernel(mesh=plsc.ScalarSubcoreMesh(axis_name="sc", num_cores=1), name=name,
               scratch_types=(pltpu.SemaphoreType.DMA((6 * n,)), pltpu.SemaphoreType.DMA((6 * n,))),
               compiler_params=pltpu.CompilerParams(collective_id=GATHER_REST_ID))
    def launch(send_sems, recv_sems):
        mx, my, mc = _me()
        barrier = pltpu.get_barrier_semaphore()
        for peer in [(mx, my, 1 - mc)] + [(cx, cy, mc) for cx, cy in _other_chips(mx, my)]:
            pl.semaphore_signal(barrier, inc=1, device_id=peer, device_id_type=MESH)
        pl.semaphore_wait(barrier, 4)
        _gather_body(x_refs, out_refs, send_sems, recv_sems)

    launch()
    return _own_slots([o[...] for o in out_refs], shards)


def swap_halves(arrs, name):
    n = len(arrs)

    def body(*refs):
        xs, outs = refs[:n], refs[n:2 * n]
        send_sems, recv_sems = refs[2 * n:]
        mx, my, mc = _me()
        cps = []
        for a in range(n):
            theirs = _halves(xs[a].shape[1], mc, 16)[1]
            cps.append(_rcopy(xs[a].at[pl.ds(0, 4), theirs], outs[a], send_sems, recv_sems, a, (mx, my, 1 - mc)))
            cps[-1].start()
        for c_ in cps:
            c_.wait()

    hbm = pl.BlockSpec(memory_space=pl.ANY)
    return _pc(body, name=name, in_specs=[hbm] * n, out_specs=[hbm] * n,
               out_shape=[_sds((4, x.shape[1] // 2, x.shape[2]), x.dtype) for x in arrs],
               scratch_shapes=[pltpu.SemaphoreType.DMA((n,)), pltpu.SemaphoreType.DMA((n,))])(*arrs)


SCATTER_ID = 4


def scatter_chips_sc(arrs, name):
    n = len(arrs)
    x_refs = [jax.new_ref(x, memory_space=pltpu.MemorySpace.HBM) for x in arrs]
    out_refs = [jax.empty_ref(_sds(x.shape, x.dtype), memory_space=pltpu.MemorySpace.HBM) for x in arrs]

    @pl.kernel(mesh=plsc.ScalarSubcoreMesh(axis_name="sc", num_cores=1), name=name,
               scratch_types=(pltpu.SemaphoreType.DMA((3 * n,)), pltpu.SemaphoreType.DMA((3 * n,))),
               compiler_params=pltpu.CompilerParams(collective_id=SCATTER_ID))
    def launch(send_sems, recv_sems):
        mx, my, mc = _me()
        chip = 2 * mx + my
        chips = _other_chips(mx, my)
        idx = [2 * cx + cy for cx, cy in chips]
        barrier = pltpu.get_barrier_semaphore()
        for cx, cy in chips:
            pl.semaphore_signal(barrier, inc=1, device_id=(cx, cy, mc), device_id_type=MESH)
        pl.semaphore_wait(barrier, 3)
        cp = functools.partial(_rcopy, send_sems=send_sems, recv_sems=recv_sems)
        sends = []
        for a in range(n):
            for j, (cx, cy) in enumerate(chips):
                sends.append(cp(x_refs[a].at[idx[j]], out_refs[a].at[chip], k=3 * a + j, to=(cx, cy, mc)))
                sends[-1].start()
        for a in range(n):
            for j, (cx, cy) in enumerate(chips):
                cp(x_refs[a].at[idx[j]], out_refs[a].at[idx[j]], k=3 * a + j, to=(cx, cy, mc)).wait_recv()
        for c_ in sends:
            c_.wait_send()

    launch()
    return _own_slots([o[...] for o in out_refs], [lax.dynamic_index_in_dim(x, _my_chip(), 0, keepdims=False) for x in arrs])


def share_halves(parts, name):
    flat = [p for w in parts for p in w]
    nw, n = len(parts), len(flat)
    depth = n // nw

    def body(*refs):
        xs, outs = refs[:n], refs[n:n + nw]
        send_sems, recv_sems = refs[n + nw:]
        mx, my, mc = _me()
        sib = (mx, my, 1 - mc)
        sends, recvs = [], []
        for a in range(n):
            w, l = a // depth, a % depth
            mine, theirs = _halves(outs[w].shape[1], mc, 8)
            sends.append(_rcopy(xs[a], outs[w].at[l, mine], send_sems, recv_sems, a, sib))
            recvs.append(_rcopy(xs[a], outs[w].at[l, theirs], send_sems, recv_sems, a, sib))
            sends[-1].start()
        for c_ in recvs:
            c_.wait_recv()
        for c_ in sends:
            c_.wait_send()

    hbm = pl.BlockSpec(memory_space=pl.ANY)
    outs = _pc(body, name=name, in_specs=[hbm] * n, out_specs=[hbm] * nw,
               out_shape=[_sds((depth, 2 * w[0].shape[0], w[0].shape[1]), F32) for w in parts],
               scratch_shapes=[pltpu.SemaphoreType.DMA((n,)), pltpu.SemaphoreType.DMA((n,))])(*flat)
    outs = list(outs)
    mc = lax.axis_index("c")
    for w in range(nw):
        for l in range(depth):
            h = parts[w][l].shape[0]
            outs[w] = lax.dynamic_update_slice(outs[w], parts[w][l][None], (l, mc * h, 0))
    return outs


_BIG = ("w_in", "w_out", "w_ffn_in", "w_ffn_out")
N_CHIPS = 4
DEPTH = 2


def _pad_rows(v, mult=8):
    n = v.shape[0]
    rows = -(-n // 128)
    rows = -(-rows // mult) * mult
    return jnp.pad(v, (0, rows * 128 - n)).reshape(rows, 128)


class _Flat:
    def __init__(self):
        self.items = []

    def add(self, name, a):
        self.items.append((name, a.shape, a.reshape(-1).astype(F32)))

    def rows(self):
        return _pad_rows(jnp.concatenate([a for _, _, a in self.items]))

    def split(self, rows):
        flat = rows.reshape(-1)
        out, o = {}, 0
        for name, shape, a in self.items:
            out[name] = flat[o:o + a.shape[0]].reshape(shape)
            o += a.shape[0]
        return out

    def split_lead(self, rows3):
        n = rows3.shape[0]
        flat = rows3.reshape(n, -1)
        out, o = {}, 0
        for name, shape, a in self.items:
            out[name] = flat[:, o:o + a.shape[0]].reshape((n,) + tuple(shape))
            o += a.shape[0]
        return out


def _gsv(rows):
    z = jnp.zeros((2, D), F32)
    r = [z if a is None else a for a in rows] + [z] * 5
    return jnp.stack(r, axis=1)


def _pad8(a, rows=8, cols=128):
    return jnp.zeros((rows, cols), F32).at[:a.shape[0], :a.shape[1]].set(a.astype(F32))


def kernel(x, c, ctx, c_ctx, w_mod, b_mod, g_mix, w_in, wa_sink, na_rpb, ssm_conv_w, ssm_conv_b, ssm_dt_bias, ssm_a_log, ssm_d, ssm_norm_g, w_out, g_ffn, w_ffn_in, w_ffn_out, g_final, loss_target, m_c_ctx, m_w_mod, m_b_mod, m_g_mix, m_w_in, m_wa_sink, m_na_rpb, m_ssm_conv_w, m_ssm_conv_b, m_ssm_dt_bias, m_ssm_a_log, m_ssm_d, m_ssm_norm_g, m_w_out, m_g_ffn, m_w_ffn_in, m_w_ffn_out, m_g_final, v_c_ctx, v_w_mod, v_b_mod, v_g_mix, v_w_in, v_wa_sink, v_na_rpb, v_ssm_conv_w, v_ssm_conv_b, v_ssm_dt_bias, v_ssm_a_log, v_ssm_d, v_ssm_norm_g, v_w_out, v_g_ffn, v_w_ffn_in, v_w_ffn_out, v_g_final):
    L, Lc = x.shape[1], ctx.shape[1]
    T = L + Lc
    nL = L // TR
    mx, my, mc = lax.axis_index("x"), lax.axis_index("y"), lax.axis_index("c")
    dev = 4 * mx + 2 * my + mc
    chip = 2 * mx + my
    MODW = 6 * D // N_CHIPS
    CW = 1024 // N_CHIPS

    sc = _silu(c.astype(F32))
    scc = _silu(c_ctx.astype(F32))[None]
    f1 = _Flat()
    f1.add("sc", sc)
    f1.add("conv_w", ssm_conv_w)
    g1, _ = allgather8(f1.rows(), "gather_cond")
    g1 = f1.split_lead(g1)
    sc_all = g1["sc"][:, 0]
    conv_w = jnp.concatenate([g1["conv_w"][2 * k] for k in range(N_CHIPS)], axis=-1)
    A16 = jnp.concatenate([sc_all, scc, jnp.zeros((7, D), F32)], axis=0)

    mod_part = jnp.stack([matmul(A16, w_mod[l], "nn", F32, f"mod_fwd{l}") for l in range(DEPTH)])
    f2 = _Flat()
    f2.add("mod", mod_part)
    g2, _ = allgather8(f2.rows(), "gather_mod")
    g2 = f2.split_lead(g2)["mod"]
    mods = jnp.concatenate([g2[2 * k] for k in range(N_CHIPS)], axis=-1) + b_mod[:, None, :]
    mod_l = lax.dynamic_index_in_dim(mods, dev, axis=1, keepdims=False).reshape(DEPTH, 6, D)
    mod_c = mods[:, 8].reshape(DEPTH, 6, D)
    mod = jnp.stack([mod_l, mod_c], axis=1)
    mrow = lambda l, j: mod[l, :, j]

    own = {"w_in": w_in, "w_out": w_out, "w_ffn_in": w_ffn_in, "w_ffn_out": w_ffn_out}
    sh16 = [own[n][l].astype(BF16) for n in _BIG for l in range(DEPTH)]
    gath = list(gather_weights(sh16[:1], "gather_first"))
    after_first = (gath[0][0, 0, 0] * 0).astype(BF16)
    gath += list(gather_weights_sc([sh16[1] + after_first] + sh16[2:], "gather_rest"))
    gw = {n: [gath[DEPTH * i + l] for l in range(DEPTH)] for i, n in enumerate(_BIG)}
    W_in = [jnp.pad(jnp.concatenate([g[k] for k in range(N_CHIPS)], axis=1), ((0, 0), (0, IN_PAD - IN_COLS))) for g in gw["w_in"]]
    W_out = [g.reshape(D, D) for g in gw["w_out"]]
    W_fo = [g.reshape(D_FF, D) for g in gw["w_ffn_out"]]
    W_fi = gw["w_ffn_in"]

    cos, sin, rotm = rope_tables(L, Lc)
    x0 = jnp.concatenate([x[0], ctx[0]], axis=0).astype(F32)

    sv = []
    xin = x0
    gsv_first = _gsv([None, mrow(0, 0), mrow(0, 1)])
    _, h1 = res_norm_mod(x0, None, gsv_first, g_mix[0][None], nL, "norm_first")
    for l in range(DEPTH):
        s = {"xin": xin, "h1": h1}
        P = matmul(h1, W_in[l], "nn", F32, f"in_proj{l}", tn=IN_PAD)
        qr, kr, kb, vb = rope_apply(P, C_QA // 256, P, C_KA // 128, cos, sin, rotm, False, f"rope{l}", kv_src=P)
        sink8 = _pad8(jnp.broadcast_to(wa_sink[l][:, None], (WA_HEADS, 128)))
        oa = win_attn_fwd(qr, kr, P, sink8, L, Lc, f"wa_fwd{l}")
        bias = na_bias_table(na_rpb[l], l)
        ob = na_fwd(P, kb, vb, bias, L, Lc, f"na_fwd{l}")
        w8 = jnp.concatenate([conv_w[l], jnp.zeros((1, 1024), F32)], axis=0)
        pre, act = conv_silu_fwd(P, w8, ssm_conv_b[l][None], nL, f"conv_fwd{l}")
        dtb8, al8 = _pad8(ssm_dt_bias[l]), _pad8(ssm_a_log[l])
        yf, yb, hsf, hsb = ssd_fwd(act, P, dtb8, al8, L, Lc, f"ssd_fwd{l}")
        dskip = jnp.repeat(ssm_d[l], S_P)[None]
        oc = ssm_out_fwd(yf, yb, act, P, dskip, ssm_norm_g[l][None], f"ssm_out_fwd{l}")
        mixin = jnp.concatenate([oa, ob, oc], axis=1)
        mix = matmul(mixin, W_out[l], "nn", F32, f"out_proj{l}")
        gsv_mid = _gsv([mrow(l, 2), mrow(l, 3), mrow(l, 4)])
        x1, h2 = res_norm_mod(xin, mix, gsv_mid, g_ffn[l][None], nL, f"norm_mid{l}")
        gu = matmul_fi(h2, W_fi[l], "nn", BF16, f"ffn_in{l}")
        af = swiglu_fwd(gu, f"swiglu_fwd{l}")
        fo = matmul(af, W_fo[l], "nn", F32, f"ffn_out{l}", tk=D_FF)
        s.update(P=P, qr=qr, kr=kr, sink8=sink8, kb=kb, vb=vb, bias=bias, w8=w8, pre=pre, act=act, dtb8=dtb8, al8=al8, yf=yf,
                 yb=yb, hsf=hsf, hsb=hsb, dskip=dskip, mixin=mixin, mix=mix, gsv_mid=gsv_mid, x1=x1, h2=h2, gu=gu, af=af, fo=fo)
        if l + 1 < DEPTH:
            s["gsv_end"] = _gsv([mrow(l, 5), mrow(l + 1, 0), mrow(l + 1, 1)])
            xin, h1 = res_norm_mod(x1, fo, s["gsv_end"], g_mix[l + 1][None], nL, f"norm_end{l}")
        else:
            s["gsv_end"] = _gsv([mrow(l, 5), None, None])
        sv.append(s)

    last = sv[-1]
    loss8, dres, dfo, dgsv_end, dg_final = final_loss(last["x1"], last["fo"], last["gsv_end"], g_final[None], loss_target[0].astype(F32), nL, "final_loss")
    loss = lax.psum(loss8[0, 0], ("x", "y", "c"))

    dmod = [[None] * 6 for _ in range(DEPTH)]
    gW = {n: [None] * DEPTH for n in _BIG}
    small = [dict() for _ in range(DEPTH)]
    parts = [None] * DEPTH
    cvec = mc.astype(jnp.int32).reshape(1)
    grad_x = None
    for l in reversed(range(DEPTH)):
        s = sv[l]
        dmod[l][5] = dgsv_end[:, 0]
        if l + 1 < DEPTH:
            dmod[l + 1][0], dmod[l + 1][1] = dgsv_end[:, 1], dgsv_end[:, 2]
        daf = matmul(dfo, W_fo[l], "nt", BF16, f"ffn_out_dx{l}")
        gW["w_ffn_out"][l] = matmul(s["af"], dfo, "tn", BF16, f"ffn_out_dw{l}", tm=1408, tk=T).reshape(N_CHIPS, D_FF // N_CHIPS, D)
        dgu = swiglu_bwd(s["gu"], daf, f"swiglu_bwd{l}")
        dh2 = matmul_fi(dgu, W_fi[l], "nt", F32, f"ffn_in_dx{l}")
        gW["w_ffn_in"][l] = matmul_fi(s["h2"], dgu, "tn", BF16, f"ffn_in_dw{l}")
        dres, dmix, dgsv_mid, dg_ffn = res_norm_mod_bwd(s["x1"], s["mix"], s["gsv_mid"], g_ffn[l][None], dh2, dres, nL, f"norm_mid_bwd{l}")
        dmod[l][2], dmod[l][3], dmod[l][4] = dgsv_mid[:, 0], dgsv_mid[:, 1], dgsv_mid[:, 2]
        dmixin = matmul(dmix, W_out[l], "nt", F32, f"out_proj_dx{l}")
        gW["w_out"][l] = matmul(s["mixin"], dmix, "tn", BF16, f"out_proj_dw{l}", tm=1024, tk=T).reshape(N_CHIPS, D // N_CHIPS, D)
        P = s["P"]
        dqr, dkr, dva, dsink = win_attn_bwd(s["qr"], s["kr"], P, s["sink8"], dmixin, L, Lc, f"wa_bwd{l}")
        dqa, dka = rope_apply(dqr, 0, dkr[WA_BLK:WA_BLK + T], 0, cos, sin, rotm, True, f"rope_bwd{l}")
        dqb, dkb, dvb, dbias = na_bwd(P, s["kb"], s["vb"], s["bias"], dmixin, L, Lc, f"na_bwd{l}")
        dy, dxs1, dz, dvec = ssm_out_bwd(s["yf"], s["yb"], s["act"], P, s["dskip"], ssm_norm_g[l][None], dmixin, f"ssm_out_bwd{l}")
        dxf, dbf, dcf, ddf, dxb, dbb, dcb, ddb, ddtb, dal = ssd_bwd(s["act"], P, s["dtb8"], s["al8"], s["hsf"], s["hsb"], dy, L, Lc, f"ssd_bwd{l}")
        dpre = dsilu(s["pre"], [dxf, dxb, dxs1], [dbf, dbb], [dcf, dcb], f"dsilu{l}")
        dxbc, dw8, db8 = conv_bwd(dpre, P, s["w8"], nL, f"conv_bwd{l}")
        dP = jnp.concatenate([dqa, dqb, dz, dka, dva[WA_BLK:WA_BLK + T].astype(BF16), dkb.astype(BF16), dvb.astype(BF16), dxbc,
                              ddf.astype(BF16), ddb.astype(BF16), jnp.zeros((T, IN_PAD - IN_COLS), BF16)], axis=1)
        dh1 = matmul(dP, W_in[l], "nt", F32, f"in_proj_dx{l}", tk=IN_PAD)
        dwin = matmul(s["h1"], dP, "tn", BF16, f"in_proj_dw{l}", tm=512, tn=IN_PAD, tk=T // 2)
        cw = IN_COLS // N_CHIPS
        gW["w_in"][l] = jnp.stack([dwin[:, k * cw:(k + 1) * cw] for k in range(N_CHIPS)])
        garr = [gW[n][l] for n in _BIG]
        got = swap_halves(garr, f"reduce_d2d{l}")
        chip_sum = [add_halves(garr[a], got[a], cvec, f"reduce_add_pair{l}_{a}") for a in range(len(garr))]
        parts[l] = scatter_chips_sc(chip_sum, f"reduce_ici{l}")
        small[l] = dict(g_ffn=dg_ffn[0], wa_sink=dsink[:WA_HEADS, 0], na_rpb=na_rpb_grad(dbias, l), conv_w=dw8[:S_CONV], conv_b=db8[0],
                        dt_bias=ddtb[:2, :8], a_log=dal[:2, :8], ssm_d=dvec[0].reshape(S_HEADS, S_P).sum(axis=1), norm_g=dvec[1])
        if l > 0:
            p = sv[l - 1]
            dres, dfo, dgsv_end, dg_mix = res_norm_mod_bwd(s["xin"], p["fo"], p["gsv_end"], g_mix[l][None], dh1, dres, nL, f"norm_end_bwd{l - 1}")
        else:
            grad_x, _, dgsv_first, dg_mix = res_norm_mod_bwd(s["xin"], None, gsv_first, g_mix[0][None], dh1, dres, nL, "norm_first_bwd")
            dmod[0][0], dmod[0][1] = dgsv_first[:, 1], dgsv_first[:, 2]
        small[l]["g_mix"] = dg_mix[0]
    for l in range(DEPTH):
        for j in range(6):
            if dmod[l][j] is None:
                dmod[l][j] = jnp.zeros((2, D), F32)
    dmod = jnp.stack([jnp.stack(r, axis=1) for r in dmod])

    f3 = _Flat()
    f3.add("dmod_l", dmod[:, 0].reshape(DEPTH, 6 * D))
    f3.add("dmod_c", dmod[:, 1].reshape(DEPTH, 6 * D))
    f3.add("g_final", dg_final[0])
    for n in ("g_mix", "g_ffn", "wa_sink", "na_rpb", "conv_w", "conv_b", "dt_bias", "a_log", "ssm_d", "norm_g"):
        f3.add(n, jnp.stack([small[l][n] for l in range(DEPTH)]))
    g3, s3 = allgather8(f3.rows(), "reduce_small")
    dmod_all = f3.split_lead(g3)["dmod_l"]
    s3 = f3.split(s3)
    dmodc_tot = s3["dmod_c"]
    col0 = chip * MODW
    G16, G16c = [], []
    for l in range(DEPTH):
        rows = jnp.concatenate([dmod_all[:, l], dmodc_tot[l][None], jnp.zeros((7, 6 * D), F32)], axis=0)
        G16.append(lax.dynamic_slice_in_dim(rows, col0, MODW, axis=1))
        rc = jnp.concatenate([dmodc_tot[l][None], jnp.zeros((15, 6 * D), F32)], axis=0)
        G16c.append(lax.dynamic_slice_in_dim(rc, col0, MODW, axis=1))
    grad_w_mod = jnp.stack([matmul(A16, G16[l], "tn", F32, f"mod_dw{l}") for l in range(DEPTH)])
    dscc_part = sum(matmul(G16c[l], w_mod[l], "nt", F32, f"mod_dx{l}")[0] for l in range(DEPTH))
    _, s4 = allgather8(_pad_rows(dscc_part * (mc == 1).astype(F32)), "reduce_cctx")
    dscc = s4.reshape(-1)[:D]
    cc = c_ctx.astype(F32)
    sg = 1.0 / (1.0 + jnp.exp(-cc))
    grad_c_ctx = dscc * (sg * (1.0 + cc * (1.0 - sg)))

    halves = [[sum_slots(parts[l][i], f"reduce_add_chips{l}_{i}") for l in range(DEPTH)] for i in range(len(_BIG))]
    gsh = dict(zip(_BIG, share_halves(halves, "reduce_share")))

    grads = {"c_ctx": grad_c_ctx, "w_mod": grad_w_mod, "b_mod": s3["dmod_l"] + s3["dmod_c"], "g_mix": s3["g_mix"], "w_in": gsh["w_in"],
             "wa_sink": s3["wa_sink"], "na_rpb": s3["na_rpb"],
             "ssm_conv_w": lax.dynamic_slice_in_dim(s3["conv_w"], chip * CW, CW, axis=2), "ssm_conv_b": s3["conv_b"],
             "ssm_dt_bias": s3["dt_bias"], "ssm_a_log": s3["a_log"], "ssm_d": s3["ssm_d"], "ssm_norm_g": s3["norm_g"],
             "w_out": gsh["w_out"], "g_ffn": s3["g_ffn"], "w_ffn_in": gsh["w_ffn_in"], "w_ffn_out": gsh["w_ffn_out"], "g_final": s3["g_final"]}
    wts = {"c_ctx": c_ctx, "w_mod": w_mod, "b_mod": b_mod, "g_mix": g_mix, "w_in": w_in, "wa_sink": wa_sink, "na_rpb": na_rpb,
           "ssm_conv_w": ssm_conv_w, "ssm_conv_b": ssm_conv_b, "ssm_dt_bias": ssm_dt_bias, "ssm_a_log": ssm_a_log, "ssm_d": ssm_d,
           "ssm_norm_g": ssm_norm_g, "w_out": w_out, "g_ffn": g_ffn, "w_ffn_in": w_ffn_in, "w_ffn_out": w_ffn_out, "g_final": g_final}
    ms = {"c_ctx": m_c_ctx, "w_mod": m_w_mod, "b_mod": m_b_mod, "g_mix": m_g_mix, "w_in": m_w_in, "wa_sink": m_wa_sink, "na_rpb": m_na_rpb,
          "ssm_conv_w": m_ssm_conv_w, "ssm_conv_b": m_ssm_conv_b, "ssm_dt_bias": m_ssm_dt_bias, "ssm_a_log": m_ssm_a_log, "ssm_d": m_ssm_d,
          "ssm_norm_g": m_ssm_norm_g, "w_out": m_w_out, "g_ffn": m_g_ffn, "w_ffn_in": m_w_ffn_in, "w_ffn_out": m_w_ffn_out, "g_final": m_g_final}
    vs = {"c_ctx": v_c_ctx, "w_mod": v_w_mod, "b_mod": v_b_mod, "g_mix": v_g_mix, "w_in": v_w_in, "wa_sink": v_wa_sink, "na_rpb": v_na_rpb,
          "ssm_conv_w": v_ssm_conv_w, "ssm_conv_b": v_ssm_conv_b, "ssm_dt_bias": v_ssm_dt_bias, "ssm_a_log": v_ssm_a_log, "ssm_d": v_ssm_d,
          "ssm_norm_g": v_ssm_norm_g, "w_out": v_w_out, "g_ffn": v_g_ffn, "w_ffn_in": v_w_ffn_in, "w_ffn_out": v_w_ffn_out, "g_final": v_g_final}
    names = list(wts)
    grads = {n: grads[n].reshape(wts[n].shape).astype(F32) for n in names}
    big = ("w_mod", "w_in", "w_out", "w_ffn_in", "w_ffn_out")
    delta, new_m, new_v = {}, {}, {}
    for n in big:
        delta[n], new_m[n], new_v[n] = adamw(wts[n], grads[n], ms[n], vs[n], f"adamw_{n}")
    packs = []
    for src in (wts, grads, ms, vs):
        f = _Flat()
        for n in names:
            if n not in big:
                f.add(n, src[n])
        packs.append(f)
    d_, m_, v_ = adamw(*[f.rows()[None] for f in packs], "adamw_small")
    for dst, rows in ((delta, d_), (new_m, m_), (new_v, v_)):
        dst.update(packs[0].split(rows[0]))

    return (loss, grad_x[:L][None], *[grads[n] for n in names], *[delta[n] for n in names],
            *[new_m[n] for n in names], *[new_v[n] for n in names])
```

```python
import functools

import numpy as np
import jax
import jax.numpy as jnp
from jax import lax
from jax.experimental import pallas as pl
from jax.experimental.pallas import tpu as pltpu
from jax.experimental.pallas import tpu_sc as plsc

F32 = jnp.float32
BF16 = jnp.bfloat16
_MXU = jnp.bfloat16
_HI = lax.Precision.HIGHEST
MESH = pl.DeviceIdType.MESH

D = 1024
HD = 64
GRID_W = 64
EPS = 1e-6
ROPE_BASE = 10000.0
WA_HEADS, WA_KV = 4, 2
WA_BLK = 128
NA_HEADS, NA_KH, NA_KW = 4, 8, 16
S_HEADS, S_P, S_INNER, S_GROUPS, S_N, S_CONV, S_Q = 8, 64, 512, 2, 128, 7, 128
D_FF = 2816
IN_COLS = 2832
IN_PAD = 2944
C_QA, C_QB, C_Z, C_KA, C_VA, C_KB, C_VB, C_XBC, C_DT = 0, 256, 512, 1024, 1152, 1280, 1536, 1792, 2816
ADAM_LR, ADAM_B1, ADAM_B2, ADAM_EPS, ADAM_WD, ADAM_STEP = 0.001, 0.9, 0.999, 1e-08, 0.01, 10

TR = 256
NEG = -1e30
VMEM_CAP = 56 * 1024 * 1024


PIN_BYTES = 256 * 1024


def _is_big(a):
    return hasattr(a, "shape") and len(a.shape) >= 2 and int(np.prod(a.shape)) * jnp.dtype(a.dtype).itemsize >= PIN_BYTES


def _pc(body, *, out_shape, pin=True, **kw):
    if not pin:
        return pl.pallas_call(body, out_shape=out_shape, **kw)
    one = isinstance(out_shape, jax.ShapeDtypeStruct)
    outs = [pltpu.HBM(s.shape, s.dtype) if _is_big(s) else s for s in ([out_shape] if one else out_shape)]
    call = pl.pallas_call(body, out_shape=outs[0] if one else outs, **kw)
    return lambda *args: call(*[pltpu.with_memory_space_constraint(a, pltpu.HBM) if _is_big(a) else a for a in args])


def _cp(sem=None, vmem=None):
    kw = {}
    if sem is not None:
        kw["dimension_semantics"] = sem
    if vmem is not None:
        kw["vmem_limit_bytes"] = int(min(max(vmem, 16 * 1024 * 1024), VMEM_CAP))
    return pltpu.CompilerParams(**kw)


def _sds(shape, dtype):
    return jax.ShapeDtypeStruct(tuple(shape), dtype)


_DIMS = {"nn": ((1,), (0,)), "nt": ((1,), (1,)), "tn": ((0,), (0,))}


def _dg(a, b, dims):
    return lax.dot_general(a.astype(_MXU), b.astype(_MXU), (dims, ((), ())), preferred_element_type=F32)


@functools.partial(jax.custom_vjp, nondiff_argnums=(2,))
def bdot(a, b, mode):
    return _dg(a, b, _DIMS[mode])


def _bdot_fwd(a, b, mode):
    return bdot(a, b, mode), (a, b)


def _bdot_bwd(mode, res, g):
    a, b = res
    if mode == "nn":
        return bdot(g, b, "nt"), bdot(a, g, "tn")
    if mode == "nt":
        return bdot(g, b, "nn"), bdot(g, a, "tn")
    return bdot(b, g, "nt"), bdot(a, g, "nn")


bdot.defvjp(_bdot_fwd, _bdot_bwd)


def hdot(a, b, mode="nn"):
    return lax.dot_general(a, b, (_DIMS[mode], ((), ())), precision=_HI, preferred_element_type=F32)


def _silu(x):
    return x / (1.0 + jnp.exp(-x))


def _softplus(x):
    return jnp.maximum(x, 0.0) + jnp.log(1.0 + jnp.exp(-jnp.abs(x)))


def _div_tile(n, cap, mult):
    if n <= cap:
        return n
    best = None
    for t in range(mult, cap + 1, mult):
        if n % t == 0:
            best = t
    assert best is not None, (n, cap, mult)
    return best


def matmul(a, b, mode, out_dtype, name, tm=640, tn=1536, tk=1408, hi=False):
    if mode == "tn":
        K, M = a.shape
    else:
        M, K = a.shape
    N = b.shape[0] if mode == "nt" else b.shape[1]
    tm = _div_tile(M, tm, 128 if mode == "tn" else 16)
    tn = _div_tile(N, tn, 128)
    tk = _div_tile(K, tk, 128 if mode != "tn" else 16)
    nk = K // tk
    dims = _DIMS[mode]

    def body(a_ref, b_ref, o_ref, *acc):
        if hi:
            part = lax.dot_general(a_ref[...], b_ref[...], (dims, ((), ())), precision=_HI, preferred_element_type=F32)
        else:
            part = _dg(a_ref[...], b_ref[...], dims)
        if nk == 1:
            o_ref[...] = part.astype(o_ref.dtype)
        else:
            k = pl.program_id(2)

            @pl.when(k == 0)
            def _():
                acc[0][...] = part

            @pl.when(k > 0)
            def _():
                acc[0][...] += part

            @pl.when(k == nk - 1)
            def _():
                o_ref[...] = acc[0][...].astype(o_ref.dtype)

    if mode == "tn":
        a_spec = pl.BlockSpec((tk, tm), lambda i, j, k: (k, i))
    else:
        a_spec = pl.BlockSpec((tm, tk), lambda i, j, k: (i, k))
    if mode == "nt":
        b_spec = pl.BlockSpec((tn, tk), lambda i, j, k: (j, k))
    else:
        b_spec = pl.BlockSpec((tk, tn), lambda i, j, k: (k, j))
    isz = lambda x: jnp.dtype(x.dtype).itemsize
    vmem = 2 * (tm * tk * isz(a) + tk * tn * isz(b) + tm * tn * jnp.dtype(out_dtype).itemsize) + 3 * tm * tn * 4
    return _pc(
        body, name=name, grid=(M // tm, N // tn, nk),
        in_specs=[a_spec, b_spec], out_specs=pl.BlockSpec((tm, tn), lambda i, j, k: (i, j)),
        out_shape=_sds((M, N), out_dtype),
        scratch_shapes=[pltpu.VMEM((tm, tn), F32)] if nk > 1 else [],
        compiler_params=_cp(("parallel", "parallel", "arbitrary"), vmem + (8 << 20)),
    )(a, b)


def _norm_mod(xo, shift, scale, g):
    r = lax.rsqrt(jnp.mean(xo * xo, axis=-1, keepdims=True) + EPS)
    return (xo * r) * g * (1.0 + scale) + shift


def res_norm_mod(x, y, gsv, g, nL, name):
    T = x.shape[0]
    has_y = y is not None

    def body(*refs):
        if has_y:
            x_ref, y_ref, gsv_ref, g_ref, xo_ref, h_ref = refs
            xo = x_ref[...] + gsv_ref[0, 0:1, :] * y_ref[...]
            xo_ref[...] = xo
        else:
            x_ref, gsv_ref, g_ref, h_ref = refs
            xo = x_ref[...]
        h_ref[...] = _norm_mod(xo, gsv_ref[0, 1:2, :], gsv_ref[0, 2:3, :], g_ref[...]).astype(h_ref.dtype)

    row = pl.BlockSpec((TR, D), lambda i: (i, 0))
    in_specs = [row] + ([row] if has_y else []) + [pl.BlockSpec((1, 8, D), lambda i: (i // nL, 0, 0)),
                                                     pl.BlockSpec((1, D), lambda i: (0, 0))]
    out_specs = ([row] if has_y else []) + [row]
    out_shape = ([_sds((T, D), F32)] if has_y else []) + [_sds((T, D), BF16)]
    args = (x, y, gsv, g) if has_y else (x, gsv, g)
    outs = _pc(body, name=name, grid=(T // TR,), in_specs=in_specs, out_specs=out_specs, out_shape=out_shape,
               compiler_params=_cp(("arbitrary",), 24 << 20))(*args)
    return (outs[0], outs[1]) if has_y else (None, outs[0])


def res_norm_mod_bwd(xo, y, gsv, g, dh, dres, nL, name):
    T = xo.shape[0]
    has_y = y is not None

    def body(*refs):
        if has_y:
            xo_ref, y_ref, gsv_ref, g_ref, dh_ref, dres_ref, dx_ref, dy_ref, dgsv_ref, dg_ref = refs
        else:
            xo_ref, gsv_ref, g_ref, dh_ref, dres_ref, dx_ref, dgsv_ref, dg_ref = refs
        i = pl.program_id(0)

        @pl.when((i == 0) | (i == nL))
        def _():
            dgsv_ref[...] = jnp.zeros_like(dgsv_ref)

        @pl.when(i == 0)
        def _():
            dg_ref[...] = jnp.zeros_like(dg_ref)

        _, vjp = jax.vjp(_norm_mod, xo_ref[...], gsv_ref[0, 1:2, :], gsv_ref[0, 2:3, :], g_ref[...])
        dxn, dshift, dscale, dg = vjp(dh_ref[...].astype(F32))
        dxo = dres_ref[...] + dxn
        dx_ref[...] = dxo
        if has_y:
            dy_ref[...] = (gsv_ref[0, 0:1, :] * dxo).astype(dy_ref.dtype)
            dgsv_ref[0, 0:1, :] += jnp.sum(y_ref[...] * dxo, axis=0, keepdims=True)
        dgsv_ref[0, 1:2, :] += dshift
        dgsv_ref[0, 2:3, :] += dscale
        dg_ref[0:1, :] += dg

    row = pl.BlockSpec((TR, D), lambda i: (i, 0))
    gspec = pl.BlockSpec((1, 8, D), lambda i: (i // nL, 0, 0))
    in_specs = [row] + ([row] if has_y else []) + [gspec, pl.BlockSpec((1, D), lambda i: (0, 0)), row, row]
    out_specs = [row] + ([row] if has_y else []) + [gspec, pl.BlockSpec((8, D), lambda i: (0, 0))]
    out_shape = [_sds((T, D), F32)] + ([_sds((T, D), BF16)] if has_y else []) + [_sds((2, 8, D), F32), _sds((8, D), F32)]
    args = (xo, y, gsv, g, dh, dres) if has_y else (xo, gsv, g, dh, dres)
    outs = _pc(body, name=name, grid=(T // TR,), in_specs=in_specs, out_specs=out_specs, out_shape=out_shape,
               compiler_params=_cp(("arbitrary",), 32 << 20))(*args)
    if has_y:
        return outs
    return outs[0], None, outs[1], outs[2]


def final_loss(x, y, gsv, g, target, nL, name):
    T = x.shape[0]

    def lossf(xo, gv, t):
        yn = (xo * lax.rsqrt(jnp.mean(xo * xo, axis=-1, keepdims=True) + EPS)) * gv
        e = yn - t
        return 0.5 * jnp.sum(jnp.sum(e * e, axis=-1, keepdims=True) * (1.0 / D), axis=0, keepdims=True)

    def body(x_ref, y_ref, gsv_ref, g_ref, t_ref, loss_ref, dx_ref, dy_ref, dgsv_ref, dg_ref):
        i = pl.program_id(0)

        @pl.when(i == 0)
        def _():
            loss_ref[...] = jnp.zeros_like(loss_ref)
            dg_ref[...] = jnp.zeros_like(dg_ref)

        @pl.when((i == 0) | (i == nL))
        def _():
            dgsv_ref[...] = jnp.zeros_like(dgsv_ref)

        @pl.when(i < nL)
        def _():
            gate = gsv_ref[0, 0:1, :]
            yv = y_ref[...]
            xo = x_ref[...] + gate * yv
            lv, vjp = jax.vjp(lossf, xo, g_ref[...], t_ref[...])
            dxo, dg, _ = vjp(jnp.ones((1, 1), F32))
            loss_ref[...] += jnp.broadcast_to(lv, loss_ref.shape)
            dx_ref[...] = dxo
            dy_ref[...] = (gate * dxo).astype(dy_ref.dtype)
            dgsv_ref[0, 0:1, :] += jnp.sum(yv * dxo, axis=0, keepdims=True)
            dg_ref[0:1, :] += dg

        @pl.when(i >= nL)
        def _():
            dx_ref[...] = jnp.zeros_like(dx_ref)
            dy_ref[...] = jnp.zeros_like(dy_ref)

    row = pl.BlockSpec((TR, D), lambda i: (i, 0))
    gspec = pl.BlockSpec((1, 8, D), lambda i: (i // nL, 0, 0))
    return _pc(
        body, name=name, grid=(T // TR,),
        in_specs=[row, row, gspec, pl.BlockSpec((1, D), lambda i: (0, 0)),
                  pl.BlockSpec((TR, D), lambda i: (jnp.minimum(i, nL - 1), 0))],
        out_specs=[pl.BlockSpec((8, 128), lambda i: (0, 0)), row, row, gspec, pl.BlockSpec((8, D), lambda i: (0, 0))],
        out_shape=[_sds((8, 128), F32), _sds((T, D), F32), _sds((T, D), BF16), _sds((2, 8, D), F32), _sds((8, D), F32)],
        compiler_params=_cp(("arbitrary",), 32 << 20),
    )(x, y, gsv, g, target)


FI_BLK = 2 * D_FF // 4


def _fi_chip(j):
    return (j % 2) * 2 + j // 2


def matmul_fi(a, b, mode, out_dtype, name):
    T = a.shape[0]
    if mode == "tn":
        tmd = 512

        def body(a_ref, b_ref, o_ref):
            o_ref[0] = _dg(a_ref[...], b_ref[...], _DIMS["tn"]).astype(o_ref.dtype)

        return _pc(body, name=name, grid=(D // tmd, 4),
                   in_specs=[pl.BlockSpec((T, tmd), lambda i, j: (0, i)), pl.BlockSpec((T, FI_BLK), lambda i, j: (0, j))],
                   out_specs=pl.BlockSpec((1, tmd, FI_BLK), lambda i, j: (_fi_chip(j), i, 0)),
                   out_shape=_sds((4, D, FI_BLK), out_dtype), compiler_params=_cp(("parallel", "arbitrary"), 48 << 20))(a, b)
    if mode == "nn":
        tm = _div_tile(T, 1280, 16)

        def body(a_ref, b_ref, o_ref):
            o_ref[...] = _dg(a_ref[...], b_ref[0], _DIMS["nn"]).astype(o_ref.dtype)

        return _pc(body, name=name, grid=(T // tm, 4),
                   in_specs=[pl.BlockSpec((tm, D), lambda i, j: (i, 0)), pl.BlockSpec((1, D, FI_BLK), lambda i, j: (_fi_chip(j), 0, 0))],
                   out_specs=pl.BlockSpec((tm, FI_BLK), lambda i, j: (i, j)), out_shape=_sds((T, 4 * FI_BLK), out_dtype),
                   compiler_params=_cp(("parallel", "arbitrary"), 40 << 20))(a, b)
    tm = _div_tile(T, 640, 16)

    def body(a_ref, b_ref, o_ref):
        acc = None
        for k in range(4):
            part = _dg(a_ref[:, k * FI_BLK:(k + 1) * FI_BLK], b_ref[_fi_chip(k)], _DIMS["nt"])
            acc = part if acc is None else acc + part
        o_ref[...] = acc.astype(o_ref.dtype)

    return _pc(body, name=name, grid=(T // tm,),
               in_specs=[pl.BlockSpec((tm, 4 * FI_BLK), lambda i: (i, 0)), pl.BlockSpec((4, D, FI_BLK), lambda i: (0, 0, 0))],
               out_specs=pl.BlockSpec((tm, D), lambda i: (i, 0)), out_shape=_sds((T, D), out_dtype),
               compiler_params=_cp(("parallel",), VMEM_CAP))(a, b)


def _swiglu(gate, up):
    return _silu(gate) * up


def swiglu_fwd(gu, name):
    T = gu.shape[0]

    def body(x_ref, o_ref):
        o_ref[...] = _swiglu(x_ref[:, :FI_BLK].astype(F32), x_ref[:, FI_BLK:].astype(F32)).astype(o_ref.dtype)

    return _pc(body, name=name, grid=(T // TR, 2), in_specs=[pl.BlockSpec((TR, 2 * FI_BLK), lambda i, j: (i, j))],
               out_specs=pl.BlockSpec((TR, FI_BLK), lambda i, j: (i, j)), out_shape=_sds((T, D_FF), BF16),
               compiler_params=_cp(("parallel", "parallel"), 24 << 20))(gu)


def swiglu_bwd(gu, dact, name):
    T = gu.shape[0]

    def body(x_ref, d_ref, o_ref):
        _, vjp = jax.vjp(_swiglu, x_ref[:, :FI_BLK].astype(F32), x_ref[:, FI_BLK:].astype(F32))
        dg, du = vjp(d_ref[...].astype(F32))
        o_ref[:, :FI_BLK] = dg.astype(o_ref.dtype)
        o_ref[:, FI_BLK:] = du.astype(o_ref.dtype)

    return _pc(body, name=name, grid=(T // TR, 2),
               in_specs=[pl.BlockSpec((TR, 2 * FI_BLK), lambda i, j: (i, j)), pl.BlockSpec((TR, FI_BLK), lambda i, j: (i, j))],
               out_specs=pl.BlockSpec((TR, 2 * FI_BLK), lambda i, j: (i, j)), out_shape=_sds((T, 2 * D_FF), BF16),
               compiler_params=_cp(("parallel", "parallel"), 32 << 20))(gu, dact)


def rope_tables(L, Lc):
    t = np.arange(L)
    rows, cols = t // GRID_W, t % GRID_W
    inv = ROPE_BASE ** (-np.arange(16, dtype=np.float32) / 16)
    lane = np.arange(64)
    pos = np.where((lane // 32)[None, :] == 0, rows[:, None], cols[:, None]).astype(np.float32)
    ang = jnp.asarray(pos) * jnp.asarray(inv[lane % 16])[None, :]
    cos = jnp.concatenate([jnp.cos(ang), jnp.ones((Lc, 64), F32)], axis=0)
    sin = jnp.concatenate([jnp.sin(ang), jnp.zeros((Lc, 64), F32)], axis=0)
    R = np.zeros((128, 128), np.float32)
    for i in range(128):
        if (i % 32) < 16:
            R[i + 16, i] = -1.0
        else:
            R[i - 16, i] = 1.0
    return jnp.tile(cos, (1, 2)), jnp.tile(sin, (1, 2)), jnp.asarray(R)


def rope_apply(q_src, q_col, k_src, k_col, cos, sin, R, transpose, name, kv_src=None):
    T = cos.shape[0]
    with_kv = kv_src is not None

    def rot(x, c, s, Rm):
        if transpose:
            return x * c + hdot(x * s, Rm, "nt")
        return x * c + hdot(x, Rm) * s

    def body(q_ref, k_ref, c_ref, s_ref, R_ref, *rest):
        qo_ref, ko_ref = rest[-4:-2] if with_kv else rest
        c, s, Rm = c_ref[...], s_ref[...], R_ref[...]
        for j in range(2):
            qo_ref[:, j * 128:(j + 1) * 128] = rot(q_ref[:, j * 128:(j + 1) * 128].astype(F32), c, s, Rm).astype(qo_ref.dtype)
        ko_ref[...] = rot(k_ref[...].astype(F32), c, s, Rm).astype(ko_ref.dtype)
        if with_kv:
            rest[-2][...] = rest[0][...].astype(BF16)
            rest[-1][...] = rest[1][...].astype(BF16)

    tab = pl.BlockSpec((TR, 128), lambda i: (i, 0))
    wide = pl.BlockSpec((TR, 256), lambda i: (i, 0))
    kv_in = [pl.BlockSpec((TR, 256), lambda i: (i, C_KB // 256)), pl.BlockSpec((TR, 256), lambda i: (i, C_VB // 256))] if with_kv else []
    return _pc(body, name=name, grid=(T // TR,),
               in_specs=[pl.BlockSpec((TR, 256), lambda i: (i, q_col)), pl.BlockSpec((TR, 128), lambda i: (i, k_col)),
                         tab, tab, pl.BlockSpec((128, 128), lambda i: (0, 0))] + kv_in,
               out_specs=[wide, tab] + ([wide, wide] if with_kv else []),
               out_shape=[_sds((T, 256), BF16), _sds((T, 128), BF16)] + ([_sds((T, 256), BF16)] * 2 if with_kv else []),
               compiler_params=_cp(("parallel",), 16 << 20))(q_src, k_src, cos, sin, R, *([kv_src, kv_src] if with_kv else []))


_SCALE = HD ** -0.5


def _attn_tile(qh, ks, vs, extra):
    ss = []
    for k, add in ks:
        s = bdot(qh, k, "nt") * _SCALE
        ss.append(s if add is None else s + add)
    m = ss[0].max(axis=-1, keepdims=True)
    for s in ss[1:]:
        m = jnp.maximum(m, s.max(axis=-1, keepdims=True))
    if extra is not None:
        m = jnp.maximum(m, extra)
    m = lax.stop_gradient(m)
    ps = [jnp.exp(s - m) for s in ss]
    den = ps[0].sum(axis=-1, keepdims=True)
    for p in ps[1:]:
        den = den + p.sum(axis=-1, keepdims=True)
    if extra is not None:
        den = den + jnp.exp(extra - m)
    num = bdot(ps[0], vs[0], "nn")
    for p, v in zip(ps[1:], vs[1:]):
        num = num + bdot(p, v, "nn")
    return num * (1.0 / den)


def _wa_mask(n, L):
    qpos = n * WA_BLK + lax.broadcasted_iota(jnp.int32, (WA_BLK, 3 * WA_BLK), 0)
    kpos = (n - 1) * WA_BLK + lax.broadcasted_iota(jnp.int32, (WA_BLK, 3 * WA_BLK), 1)
    ok = (jnp.abs(qpos - kpos) <= WA_BLK) & (kpos >= 0) & (kpos < L)
    return jnp.where(ok, 0.0, NEG).astype(F32)


WA_BPS = 2


def _wa_specs(L, Lc):
    nb = L // WA_BLK
    cb = L // Lc

    def blk(j, col):
        return pl.BlockSpec((WA_BLK, 128), lambda s: (jnp.clip(s * WA_BPS - 1 + j, 0, nb - 1), col))

    vcol = C_VA // 128
    kspecs = [blk(j, 0) for j in range(WA_BPS + 2)] + [pl.BlockSpec((Lc, 128), lambda s: (cb, 0))]
    vspecs = [blk(j, vcol) for j in range(WA_BPS + 2)] + [pl.BlockSpec((Lc, 128), lambda s: (cb, vcol))]
    return nb, kspecs, vspecs


def win_attn_fwd(qr, kr, P, sink, L, Lc, name):
    T = L + Lc
    nb, kspecs, vspecs = _wa_specs(L, Lc)
    nk = WA_BPS + 2
    QB = WA_BPS * WA_BLK
    nlat = nb // WA_BPS

    def body(q_ref, *refs):
        kbs, kx, vbs, vx, s_ref, o_ref = refs[:nk], refs[nk], refs[nk + 1:2 * nk + 1], refs[2 * nk + 1], refs[-2], refs[-1]
        s = pl.program_id(0)

        @pl.when(s < nlat)
        def _():
            for b in range(WA_BPS):
                mask = _wa_mask(s * WA_BPS + b, L)
                qs = slice(b * WA_BLK, (b + 1) * WA_BLK)
                for g in range(WA_KV):
                    sl = slice(g * HD, (g + 1) * HD)
                    k3 = jnp.concatenate([kbs[b + j][:, sl] for j in range(3)], axis=0)
                    v3 = jnp.concatenate([vbs[b + j][:, sl] for j in range(3)], axis=0)
                    for r in range(2):
                        hs = slice((2 * g + r) * HD, (2 * g + r + 1) * HD)
                        o = _attn_tile(q_ref[qs, hs], [(k3, mask), (kx[:, sl], None)], [v3, vx[:, sl]], s_ref[2 * g + r:2 * g + r + 1, 0:1])
                        o_ref[qs, hs] = o.astype(o_ref.dtype)

        @pl.when(s >= nlat)
        def _():
            for h in range(WA_HEADS):
                sl = slice((h // 2) * HD, (h // 2 + 1) * HD)
                o = _attn_tile(q_ref[:, h * HD:(h + 1) * HD], [(kx[:, sl], None)], [vx[:, sl]], s_ref[h:h + 1, 0:1])
                o_ref[:, h * HD:(h + 1) * HD] = o.astype(o_ref.dtype)

    qspec = pl.BlockSpec((QB, 256), lambda s: (s, 0))
    return _pc(body, name=name, grid=(T // QB,),
               in_specs=[qspec] + kspecs + vspecs + [pl.BlockSpec((8, 128), lambda s: (0, 0))],
               out_specs=qspec, out_shape=_sds((T, 256), BF16),
               compiler_params=_cp(("arbitrary",), 32 << 20))(qr, *([kr] * (nk + 1)), *([P] * (nk + 1)), sink)


def win_attn_bwd(qr, kr, P, sink, do_src, L, Lc, name):
    T = L + Lc
    nb, kspecs, vspecs = _wa_specs(L, Lc)
    nk = WA_BPS + 2
    QB = WA_BPS * WA_BLK
    nlat = nb // WA_BPS
    cx = WA_BLK + L

    def body(q_ref, *refs):
        kbs, kx, vbs, vx = refs[:nk], refs[nk], refs[nk + 1:2 * nk + 1], refs[2 * nk + 1]
        s_ref, do_ref, dq_ref, dk_ref, dv_ref, ds_ref = refs[2 * nk + 2:]
        s = pl.program_id(0)

        @pl.when(s == 0)
        def _():
            dk_ref[...] = jnp.zeros_like(dk_ref)
            dv_ref[...] = jnp.zeros_like(dv_ref)
            ds_ref[...] = jnp.zeros_like(ds_ref)

        @pl.when(s < nlat)
        def _():
            for b in range(WA_BPS):
                n = s * WA_BPS + b
                mask = _wa_mask(n, L)
                rows = pl.ds(pl.multiple_of(n * WA_BLK, WA_BLK), 3 * WA_BLK)
                qs = slice(b * WA_BLK, (b + 1) * WA_BLK)
                for g in range(WA_KV):
                    sl = slice(g * HD, (g + 1) * HD)
                    k3 = jnp.concatenate([kbs[b + j][:, sl] for j in range(3)], axis=0)
                    v3 = jnp.concatenate([vbs[b + j][:, sl] for j in range(3)], axis=0)
                    kxg, vxg = kx[:, sl], vx[:, sl]
                    acc = None
                    for r in range(2):
                        h = 2 * g + r
                        hs = slice(h * HD, (h + 1) * HD)
                        f = lambda q, k3_, v3_, kx_, vx_, s_: _attn_tile(q, [(k3_, mask), (kx_, None)], [v3_, vx_], s_)
                        _, vjp = jax.vjp(f, q_ref[qs, hs].astype(F32), k3.astype(F32), v3.astype(F32), kxg.astype(F32),
                                         vxg.astype(F32), s_ref[h:h + 1, 0:1])
                        dq, dk3, dv3, dkx, dvx, dsk = vjp(do_ref[qs, hs].astype(F32))
                        dq_ref[qs, hs] = dq
                        ds_ref[h:h + 1, :] += jnp.broadcast_to(dsk, (1, 128))
                        acc = (dk3, dv3, dkx, dvx) if acc is None else tuple(a + b_ for a, b_ in zip(acc, (dk3, dv3, dkx, dvx)))
                    dk_ref[rows, sl] += acc[0]
                    dv_ref[rows, sl] += acc[1]
                    dk_ref[cx:cx + Lc, sl] += acc[2]
                    dv_ref[cx:cx + Lc, sl] += acc[3]

        @pl.when(s >= nlat)
        def _():
            for h in range(WA_HEADS):
                sl = slice((h // 2) * HD, (h // 2 + 1) * HD)
                hs = slice(h * HD, (h + 1) * HD)
                f = lambda q, kx_, vx_, s_: _attn_tile(q, [(kx_, None)], [vx_], s_)
                _, vjp = jax.vjp(f, q_ref[:, hs].astype(F32), kx[:, sl].astype(F32), vx[:, sl].astype(F32), s_ref[h:h + 1, 0:1])
                dq, dkx, dvx, dsk = vjp(do_ref[:, hs].astype(F32))
                dq_ref[:, hs] = dq
                ds_ref[h:h + 1, :] += jnp.broadcast_to(dsk, (1, 128))
                dk_ref[cx:cx + Lc, sl] += dkx
                dv_ref[cx:cx + Lc, sl] += dvx

    qspec = pl.BlockSpec((QB, 256), lambda s: (s, 0))
    acc_spec = pl.BlockSpec((T + 2 * WA_BLK, 128), lambda s: (0, 0))
    return _pc(body, name=name, grid=(T // QB,),
               in_specs=[qspec] + kspecs + vspecs + [pl.BlockSpec((8, 128), lambda s: (0, 0)), qspec],
               out_specs=[qspec, acc_spec, acc_spec, pl.BlockSpec((8, 128), lambda s: (0, 0))],
               out_shape=[_sds((T, 256), F32), _sds((T + 2 * WA_BLK, 128), F32), _sds((T + 2 * WA_BLK, 128), F32), _sds((8, 128), F32)],
               compiler_params=_cp(("arbitrary",), 40 << 20))(qr, *([kr] * (nk + 1)), *([P] * (nk + 1)), sink, do_src)


def na_index_tables():
    qc = np.arange(GRID_W)[:, None]
    kc = np.arange(GRID_W)[None, :]
    cstart = np.clip(qc - NA_KW // 2, 0, GRID_W - NA_KW)
    ok = (kc >= cstart) & (kc < cstart + NA_KW)
    dx = np.clip(kc - qc, -(NA_KW - 1), NA_KW - 1) + (NA_KW - 1)
    off = np.arange(NA_KH)[:, None]
    kr = np.arange(NA_KH)[None, :]
    dy = kr - off + (NA_KH - 1)
    return ok, dx, dy


def _na_selectors():
    ok, dx, dy = na_index_tables()
    e1 = np.zeros((GRID_W * GRID_W, 128), np.float32)
    qi, ki = np.nonzero(ok)
    e1[qi * GRID_W + ki, dx[qi, ki]] = 1.0
    e2 = np.zeros((16, NA_KH * NA_KH), np.float32)
    oi, ri = np.meshgrid(np.arange(NA_KH), np.arange(NA_KH), indexing="ij")
    e2[dy[oi, ri].ravel(), (oi * NA_KH + ri).ravel()] = 1.0
    return ok, jnp.asarray(e1), jnp.asarray(np.kron(np.eye(NA_HEADS, dtype=np.float32), e2))


def na_bias_table(rpb, tag):
    ok, e1, e2 = _na_selectors()
    r2 = jnp.pad(rpb.astype(F32), ((0, 0), (0, 1), (0, 128 - (2 * NA_KW - 1)))).reshape(NA_HEADS * 16, 128)
    r1 = matmul(e2, r2, "tn", F32, f"na_bias_sel1_{tag}", hi=True)
    x = matmul(r1, e1, "nt", F32, f"na_bias_sel2_{tag}", hi=True)
    b = x.reshape(NA_HEADS, NA_KH, NA_KH, GRID_W, GRID_W).transpose(0, 1, 3, 2, 4)
    b = b + jnp.asarray(np.where(ok, 0.0, NEG).astype(np.float32))[None, None, :, None, :]
    return b.reshape(NA_HEADS, NA_KH, GRID_W, NA_KH * GRID_W)


def _na_rows(r, GR):
    r0 = jnp.clip(r - NA_KH // 2, 0, GR - NA_KH)
    return r0, jnp.clip(r - r0, 0, NA_KH - 1)


NA_RPS = 4


def na_fwd(P, kb, vb, bias, L, Lc, name):
    T = L + Lc
    GR = L // GRID_W
    W = NA_KH * GRID_W
    QB = GRID_W * NA_RPS
    nlat = GR // NA_RPS

    def body(q_ref, k_ref, v_ref, b_ref, o_ref):
        s = pl.program_id(0)

        @pl.when(s < nlat)
        def _():
            for rr in range(NA_RPS):
                r0, off = _na_rows(s * NA_RPS + rr, GR)
                rows = pl.ds(pl.multiple_of(r0 * GRID_W, GRID_W), W)
                qs = slice(rr * GRID_W, (rr + 1) * GRID_W)
                for h in range(NA_HEADS):
                    hs = slice(h * HD, (h + 1) * HD)
                    o = _attn_tile(q_ref[qs, hs], [(k_ref[rows, hs], b_ref[h, off]), (k_ref[L:T, hs], None)],
                                   [v_ref[rows, hs], v_ref[L:T, hs]], None)
                    o_ref[qs, hs] = o.astype(o_ref.dtype)

        @pl.when(s >= nlat)
        def _():
            for h in range(NA_HEADS):
                hs = slice(h * HD, (h + 1) * HD)
                o = _attn_tile(q_ref[:, hs], [(k_ref[L:T, hs], None)], [v_ref[L:T, hs]], None)
                o_ref[:, hs] = o.astype(o_ref.dtype)

    one = pl.Buffered(1)
    return _pc(body, name=name, grid=(T // QB,),
               in_specs=[pl.BlockSpec((QB, 256), lambda r: (r, C_QB // 256)),
                         pl.BlockSpec((T, 256), lambda r: (0, 0), pipeline_mode=one),
                         pl.BlockSpec((T, 256), lambda r: (0, 0), pipeline_mode=one),
                         pl.BlockSpec((NA_HEADS, NA_KH, GRID_W, W), lambda r: (0, 0, 0, 0), pipeline_mode=one)],
               out_specs=pl.BlockSpec((QB, 256), lambda r: (r, 0)), out_shape=_sds((T, 256), BF16),
               compiler_params=_cp(("arbitrary",), 32 << 20))(P, kb, vb, bias)


def na_bwd(P, kb, vb, bias, do_src, L, Lc, name):
    T = L + Lc
    GR = L // GRID_W
    W = NA_KH * GRID_W
    QB = GRID_W * NA_RPS
    nlat = GR // NA_RPS

    def body(q_ref, k_ref, v_ref, b_ref, do_ref, dq_ref, dk_ref, dv_ref, db_ref):
        s = pl.program_id(0)

        @pl.when(s == 0)
        def _():
            dk_ref[...] = jnp.zeros_like(dk_ref)
            dv_ref[...] = jnp.zeros_like(dv_ref)
            db_ref[...] = jnp.zeros_like(db_ref)

        @pl.when(s < nlat)
        def _():
            for rr in range(NA_RPS):
                r0, off = _na_rows(s * NA_RPS + rr, GR)
                rows = pl.ds(pl.multiple_of(r0 * GRID_W, GRID_W), W)
                qs = slice(rr * GRID_W, (rr + 1) * GRID_W)
                for h in range(NA_HEADS):
                    hs = slice(h * HD, (h + 1) * HD)
                    f = lambda q, kw, vw, kx, vx, b: _attn_tile(q, [(kw, b), (kx, None)], [vw, vx], None)
                    _, vjp = jax.vjp(f, q_ref[qs, hs].astype(F32), k_ref[rows, hs].astype(F32), v_ref[rows, hs].astype(F32),
                                     k_ref[L:T, hs].astype(F32), v_ref[L:T, hs].astype(F32), b_ref[h, off])
                    dq, dkw, dvw, dkx, dvx, db = vjp(do_ref[qs, hs].astype(F32))
                    dq_ref[qs, hs] = dq.astype(dq_ref.dtype)
                    dk_ref[rows, hs] += dkw
                    dv_ref[rows, hs] += dvw
                    dk_ref[L:T, hs] += dkx
                    dv_ref[L:T, hs] += dvx
                    db_ref[h, off] += db

        @pl.when(s >= nlat)
        def _():
            for h in range(NA_HEADS):
                hs = slice(h * HD, (h + 1) * HD)
                f = lambda q, kx, vx: _attn_tile(q, [(kx, None)], [vx], None)
                _, vjp = jax.vjp(f, q_ref[:, hs].astype(F32), k_ref[L:T, hs].astype(F32), v_ref[L:T, hs].astype(F32))
                dq, dkx, dvx = vjp(do_ref[:, hs].astype(F32))
                dq_ref[:, hs] = dq.astype(dq_ref.dtype)
                dk_ref[L:T, hs] += dkx
                dv_ref[L:T, hs] += dvx

    one = pl.Buffered(1)
    full = lambda shape: pl.BlockSpec(shape, lambda r: (0,) * len(shape), pipeline_mode=one)
    return _pc(body, name=name, grid=(T // QB,),
               in_specs=[pl.BlockSpec((QB, 256), lambda r: (r, C_QB // 256)), full((T, 256)), full((T, 256)),
                         full((NA_HEADS, NA_KH, GRID_W, W)), pl.BlockSpec((QB, 256), lambda r: (r, 1))],
               out_specs=[pl.BlockSpec((QB, 256), lambda r: (r, 0)), full((T, 256)), full((T, 256)),
                          full((NA_HEADS, NA_KH, GRID_W, W))],
               out_shape=[_sds((T, 256), BF16), _sds((T, 256), F32), _sds((T, 256), F32), _sds((NA_HEADS, NA_KH, GRID_W, W), F32)],
               compiler_params=_cp(("arbitrary",), 48 << 20))(P, kb, vb, bias, do_src)


def na_rpb_grad(dbias, tag):
    _, e1, e2 = _na_selectors()
    x = dbias.reshape(NA_HEADS, NA_KH, GRID_W, NA_KH, GRID_W).transpose(0, 1, 3, 2, 4).reshape(NA_HEADS * NA_KH * NA_KH, GRID_W * GRID_W)
    r1 = matmul(x, e1, "nn", F32, f"na_rpb_sel1_{tag}", hi=True, tk=1024)
    r2 = matmul(e2, r1, "nn", F32, f"na_rpb_sel2_{tag}", hi=True)
    return r2.reshape(NA_HEADS, 16, 128)[:, :2 * NA_KH - 1, :2 * NA_KW - 1]


_HALO = 8


def _halo_specs(T, col0):
    nh = TR // _HALO
    cur = pl.BlockSpec((TR, 256), lambda i, j: (i, col0 + j))
    prv = pl.BlockSpec((_HALO, 256), lambda i, j: (jnp.maximum(i * nh - 1, 0), col0 + j))
    nxt = pl.BlockSpec((_HALO, 256), lambda i, j: (jnp.minimum((i + 1) * nh, T // _HALO - 1), col0 + j))
    return prv, cur, nxt


def _fill_ext(ext, prv, cur, nxt, i, nL, nT):
    has_prev = jnp.where((i != 0) & (i != nL), 1.0, 0.0)
    has_next = jnp.where((i != nL - 1) & (i != nT - 1), 1.0, 0.0)
    ext[0:_HALO, :] = prv[...].astype(F32) * has_prev
    ext[_HALO:_HALO + TR, :] = cur[...].astype(F32)
    ext[_HALO + TR:, :] = nxt[...].astype(F32) * has_next


def conv_silu_fwd(P, w8, b, nL, name):
    T = P.shape[0]
    nT = T // TR

    def body(prv, cur, nxt, w_ref, b_ref, pre_ref, act_ref, ext):
        i = pl.program_id(0)
        _fill_ext(ext, prv, cur, nxt, i, nL, nT)
        y = jnp.broadcast_to(b_ref[...], (TR, 256))
        for k in range(S_CONV):
            y = y + w_ref[k:k + 1, :] * ext[pl.ds(_HALO - S_CONV // 2 + k, TR), :]
        pre_ref[...] = y
        act_ref[...] = _silu(y)

    prv, cur, nxt = _halo_specs(T, C_XBC // 256)
    out = pl.BlockSpec((TR, 256), lambda i, j: (i, j))
    return _pc(body, name=name, grid=(nT, 4),
               in_specs=[prv, cur, nxt, pl.BlockSpec((8, 256), lambda i, j: (0, j)), pl.BlockSpec((1, 256), lambda i, j: (0, j))],
               out_specs=[out, out], out_shape=[_sds((T, 1024), F32), _sds((T, 1024), F32)],
               scratch_shapes=[pltpu.VMEM((TR + 2 * _HALO, 256), F32)],
               compiler_params=_cp(("parallel", "parallel"), 16 << 20))(P, P, P, w8, b)


def dsilu(pre, dxs_list, db_list, dc_list, name):
    T = pre.shape[0]
    n1, n2, n3 = len(dxs_list), len(db_list), len(dc_list)

    def body(*refs):
        pre_ref = refs[0]
        ins = refs[1:1 + n1 + n2 + n3]
        out = refs[-1]

        def part(rs, lo, hi):
            g = rs[0][...].astype(F32)
            for r in rs[1:]:
                g = g + r[...].astype(F32)
            _, vjp = jax.vjp(_silu, pre_ref[:, lo:hi])
            out[:, lo:hi] = vjp(g)[0]

        part(ins[:n1], 0, 512)
        part(ins[n1:n1 + n2], 512, 768)
        part(ins[n1 + n2:], 768, 1024)

    spec = lambda w: pl.BlockSpec((TR, w), lambda i: (i, 0))
    return _pc(body, name=name, grid=(T // TR,),
               in_specs=[spec(1024)] + [spec(512)] * n1 + [spec(256)] * (n2 + n3),
               out_specs=spec(1024), out_shape=_sds((T, 1024), F32),
               compiler_params=_cp(("parallel",), 32 << 20))(pre, *dxs_list, *db_list, *dc_list)


def conv_bwd(dpre, P, w8, nL, name):
    T = P.shape[0]
    nT = T // TR

    def body(dp, dc, dn, xp, xc, xn, w_ref, dx_ref, dw_ref, db_ref, extd, extx):
        i = pl.program_id(1)
        _fill_ext(extd, dp, dc, dn, i, nL, nT)
        _fill_ext(extx, xp, xc, xn, i, nL, nT)

        @pl.when(i == 0)
        def _():
            dw_ref[...] = jnp.zeros_like(dw_ref)
            db_ref[...] = jnp.zeros_like(db_ref)

        d = dc[...]
        dx = jnp.zeros((TR, 256), F32)
        for k in range(S_CONV):
            dx = dx + w_ref[k:k + 1, :] * extd[pl.ds(_HALO + S_CONV // 2 - k, TR), :]
            dw_ref[k:k + 1, :] += jnp.sum(d * extx[pl.ds(_HALO - S_CONV // 2 + k, TR), :], axis=0, keepdims=True)
        dx_ref[...] = dx.astype(dx_ref.dtype)
        db_ref[0:1, :] += jnp.sum(d, axis=0, keepdims=True)

    def swap(spec):
        f = spec.index_map
        return pl.BlockSpec(spec.block_shape, lambda j, i: f(i, j))

    dprv, dcur, dnxt = [swap(s) for s in _halo_specs(T, 0)]
    xprv, xcur, xnxt = [swap(s) for s in _halo_specs(T, C_XBC // 256)]
    acc = pl.BlockSpec((8, 256), lambda j, i: (0, j))
    return _pc(body, name=name, grid=(4, nT),
               in_specs=[dprv, dcur, dnxt, xprv, xcur, xnxt, acc],
               out_specs=[pl.BlockSpec((TR, 256), lambda j, i: (i, j)), acc, acc],
               out_shape=[_sds((T, 1024), BF16), _sds((8, 1024), F32), _sds((8, 1024), F32)],
               scratch_shapes=[pltpu.VMEM((TR + 2 * _HALO, 256), F32), pltpu.VMEM((TR + 2 * _HALO, 256), F32)],
               compiler_params=_cp(("parallel", "arbitrary"), 16 << 20))(dpre, dpre, dpre, P, P, P, w8)


def _onehot_row(h, n):
    return (lax.broadcasted_iota(jnp.int32, (1, n), 1) == h).astype(F32)


def _onehot_col(h, n):
    return (lax.broadcasted_iota(jnp.int32, (n, 1), 0) == h).astype(F32)


def _ssd_chunk(xs, dtr, dtb, alog, bm, cm, hin, reverse):
    Qn = S_Q
    ii = lax.broadcasted_iota(jnp.int32, (Qn, Qn), 0)
    jj = lax.broadcasted_iota(jnp.int32, (Qn, Qn), 1)
    keep = (ii <= jj) if reverse else (ii >= jj)
    tri = keep.astype(F32)
    triT = ((jj <= ii) if reverse else (jj >= ii)).astype(F32)
    eye = (ii == jj).astype(F32)
    dt = _softplus(dtr + dtb)
    a = dt * (-jnp.exp(alog))
    cs = hdot(tri, a)
    csT = hdot(a, triT, "tn")
    dtT = hdot(dt, eye, "tn")
    last = _onehot_row(0 if reverse else Qn - 1, Qn)
    ys, houts = [], []
    for g in range(S_GROUPS):
        G = bdot(cm[g], bm[g], "nt")
        for r in range(S_HEADS // S_GROUPS):
            h = g * (S_HEADS // S_GROUPS) + r
            eh_r, eh_c = _onehot_row(h, S_HEADS), _onehot_col(h, S_HEADS)
            cs_c = jnp.sum(cs * eh_r, axis=1, keepdims=True)
            dt_c = jnp.sum(dt * eh_r, axis=1, keepdims=True)
            cs_r = jnp.sum(csT * eh_c, axis=0, keepdims=True)
            dt_r = jnp.sum(dtT * eh_c, axis=0, keepdims=True)
            tot = jnp.sum(cs_r * last, axis=1, keepdims=True)
            decay = jnp.exp(jnp.where(keep, cs_c - cs_r, NEG))
            w = G * decay * dt_r
            y = bdot(w, xs[h], "nn") + bdot(cm[g], hin[h], "nt") * jnp.exp(cs_c)
            xsc = xs[h] * (jnp.exp(tot - cs_c) * dt_c)
            hout = hin[h] * jnp.exp(tot) + bdot(xsc, bm[g], "tn")
            ys.append(y)
            houts.append(hout)
    return ys, houts


def _ssd_orders(L, Lc):
    nl, ncx = L // S_Q, Lc // S_Q
    fwd = lambda s: jnp.where(s < ncx, nl + s, s - ncx)
    bwd = lambda s: nl + ncx - 1 - s
    return nl + ncx, fwd, bwd


def _ssd_in_specs(fo, bo, step):
    def at(order, w, col):
        return pl.BlockSpec((S_Q, w), lambda u: (order(step(u)), col))
    specs = []
    for order in (fo, bo):
        specs += [at(order, 512, 0), at(order, 256, 2), at(order, 256, 3), at(order, 128, C_DT // 128)]
    return specs


def ssd_fwd(act, P, dtb, alog, L, Lc, name):
    T = L + Lc
    ns, fo, bo = _ssd_orders(L, Lc)

    def body(xf, bf, cf, df, xb, bb, cb, db, dtb_ref, al_ref, yf, yb, hsf, hsb, Hf, Hb):
        s = pl.program_id(0)

        @pl.when(s == 0)
        def _():
            Hf[...] = jnp.zeros_like(Hf)
            Hb[...] = jnp.zeros_like(Hb)

        for d, (x_r, b_r, c_r, dt_r, y_r, hs_r, H) in enumerate(((xf, bf, cf, df, yf, hsf, Hf), (xb, bb, cb, db, yb, hsb, Hb))):
            hin = [H[h] for h in range(S_HEADS)]
            hs_r[0] = H[...]
            ys, houts = _ssd_chunk(
                [x_r[:, h * S_P:(h + 1) * S_P] for h in range(S_HEADS)], dt_r[:, d * 8:(d + 1) * 8],
                dtb_ref[d:d + 1, 0:8], al_ref[d:d + 1, 0:8],
                [b_r[:, g * S_N:(g + 1) * S_N] for g in range(S_GROUPS)], [c_r[:, g * S_N:(g + 1) * S_N] for g in range(S_GROUPS)],
                hin, reverse=(d == 1))
            for h in range(S_HEADS):
                y_r[:, h * S_P:(h + 1) * S_P] = ys[h]
                H[h] = houts[h]

    ident = lambda u: u
    small = pl.BlockSpec((8, 128), lambda u: (0, 0))
    hspec = pl.BlockSpec((1, S_HEADS, S_P, S_N), lambda u: (u, 0, 0, 0))
    return _pc(body, name=name, grid=(ns,),
               in_specs=_ssd_in_specs(fo, bo, ident) + [small, small],
               out_specs=[pl.BlockSpec((S_Q, 512), lambda u: (fo(u), 0)), pl.BlockSpec((S_Q, 512), lambda u: (bo(u), 0)), hspec, hspec],
               out_shape=[_sds((T, 512), F32), _sds((T, 512), F32), _sds((ns, S_HEADS, S_P, S_N), F32), _sds((ns, S_HEADS, S_P, S_N), F32)],
               scratch_shapes=[pltpu.VMEM((S_HEADS, S_P, S_N), F32), pltpu.VMEM((S_HEADS, S_P, S_N), F32)],
               compiler_params=_cp(("arbitrary",), 32 << 20))(act, act, act, P, act, act, act, P, dtb, alog)


def ssd_bwd(act, P, dtb, alog, hsf, hsb, dy, L, Lc, name):
    T = L + Lc
    ns, fo, bo = _ssd_orders(L, Lc)
    step = lambda u: ns - 1 - u

    def body(xf, bf, cf, df, xb, bb, cb, db, dtb_ref, al_ref, hsf_r, hsb_r, dyf, dyb,
             dxf, dbf, dcf, ddf, dxb, dbb, dcb, ddb, ddtb, dal, dHf, dHb):
        u = pl.program_id(0)

        @pl.when(u == 0)
        def _():
            dHf[...] = jnp.zeros_like(dHf)
            dHb[...] = jnp.zeros_like(dHb)
            ddtb[...] = jnp.zeros_like(ddtb)
            dal[...] = jnp.zeros_like(dal)

        dirs = ((xf, bf, cf, df, hsf_r, dyf, dxf, dbf, dcf, ddf, dHf), (xb, bb, cb, db, hsb_r, dyb, dxb, dbb, dcb, ddb, dHb))
        for d, (x_r, b_r, c_r, dt_r, hs_r, dy_r, dx_o, db_o, dc_o, dd_o, dH) in enumerate(dirs):
            f = functools.partial(_ssd_chunk, reverse=(d == 1))
            _, vjp = jax.vjp(
                f, [x_r[:, h * S_P:(h + 1) * S_P] for h in range(S_HEADS)], dt_r[:, d * 8:(d + 1) * 8],
                dtb_ref[d:d + 1, 0:8], al_ref[d:d + 1, 0:8],
                [b_r[:, g * S_N:(g + 1) * S_N] for g in range(S_GROUPS)], [c_r[:, g * S_N:(g + 1) * S_N] for g in range(S_GROUPS)],
                [hs_r[0, h] for h in range(S_HEADS)])
            gx, gdt, gdtb, gal, gb, gc, gh = vjp(([dy_r[:, h * S_P:(h + 1) * S_P] for h in range(S_HEADS)],
                                                  [dH[h] for h in range(S_HEADS)]))
            for h in range(S_HEADS):
                dx_o[:, h * S_P:(h + 1) * S_P] = gx[h]
                dH[h] = gh[h]
            for g in range(S_GROUPS):
                db_o[:, g * S_N:(g + 1) * S_N] = gb[g]
                dc_o[:, g * S_N:(g + 1) * S_N] = gc[g]
            dd_o[...] = gdt
            ddtb[d:d + 1, 0:8] += gdtb
            dal[d:d + 1, 0:8] += gal

    small = pl.BlockSpec((8, 128), lambda u: (0, 0))
    hspec = pl.BlockSpec((1, S_HEADS, S_P, S_N), lambda u: (step(u), 0, 0, 0))
    at = lambda order, w: pl.BlockSpec((S_Q, w), lambda u: (order(step(u)), 0))
    outs = []
    for order in (fo, bo):
        outs += [at(order, 512), at(order, 256), at(order, 256), at(order, 8)]
    oshape = [_sds((T, 512), F32), _sds((T, 256), F32), _sds((T, 256), F32), _sds((T, 8), F32)]
    return _pc(body, name=name, grid=(ns,),
               in_specs=_ssd_in_specs(fo, bo, step) + [small, small, hspec, hspec, at(fo, 512), at(bo, 512)],
               out_specs=outs + [small, small], out_shape=oshape + oshape + [_sds((8, 128), F32), _sds((8, 128), F32)],
               scratch_shapes=[pltpu.VMEM((S_HEADS, S_P, S_N), F32), pltpu.VMEM((S_HEADS, S_P, S_N), F32)],
               compiler_params=_cp(("arbitrary",), 40 << 20))(act, act, act, P, act, act, act, P, dtb, alog, hsf, hsb, dy, dy)


def _ssm_out(yf, yb, xs, z, dskip, g):
    y = (yf + yb + dskip * xs) * _silu(z)
    return (y * lax.rsqrt(jnp.mean(y * y, axis=-1, keepdims=True) + EPS)) * g


def ssm_out_fwd(yf, yb, act, P, dskip, g, name):
    T = yf.shape[0]

    def body(yf_r, yb_r, xs_r, z_r, d_r, g_r, o_r):
        o_r[...] = _ssm_out(yf_r[...], yb_r[...], xs_r[...], z_r[...], d_r[...], g_r[...]).astype(o_r.dtype)

    row = pl.BlockSpec((TR, 512), lambda i: (i, 0))
    vec = pl.BlockSpec((1, 512), lambda i: (0, 0))
    return _pc(body, name=name, grid=(T // TR,),
               in_specs=[row, row, row, pl.BlockSpec((TR, 512), lambda i: (i, C_Z // 512)), vec, vec],
               out_specs=row, out_shape=_sds((T, 512), BF16),
               compiler_params=_cp(("parallel",), 16 << 20))(yf, yb, act, P, dskip, g)


def ssm_out_bwd(yf, yb, act, P, dskip, g, do_src, name):
    T = yf.shape[0]

    def body(yf_r, yb_r, xs_r, z_r, d_r, g_r, do_r, dy_r, dxs_r, dz_r, dv_r):
        @pl.when(pl.program_id(0) == 0)
        def _():
            dv_r[...] = jnp.zeros_like(dv_r)

        _, vjp = jax.vjp(_ssm_out, yf_r[...], yb_r[...], xs_r[...], z_r[...], d_r[...], g_r[...])
        dyf, _, dxs, dz, dd, dg = vjp(do_r[...].astype(F32))
        dy_r[...] = dyf
        dxs_r[...] = dxs
        dz_r[...] = dz.astype(dz_r.dtype)
        dv_r[0:1, :] += dd
        dv_r[1:2, :] += dg

    row = pl.BlockSpec((TR, 512), lambda i: (i, 0))
    vec = pl.BlockSpec((1, 512), lambda i: (0, 0))
    return _pc(body, name=name, grid=(T // TR,),
               in_specs=[row, row, row, pl.BlockSpec((TR, 512), lambda i: (i, C_Z // 512)), vec, vec,
                         pl.BlockSpec((TR, 512), lambda i: (i, 1))],
               out_specs=[row, row, row, pl.BlockSpec((8, 512), lambda i: (0, 0))],
               out_shape=[_sds((T, 512), F32), _sds((T, 512), F32), _sds((T, 512), BF16), _sds((8, 512), F32)],
               compiler_params=_cp(("arbitrary",), 24 << 20))(yf, yb, act, P, dskip, g, do_src)


def add_halves(xv, got, cvec, name):
    n, r, cdim = xv.shape
    h = r // 2

    def body(c_ref, x_ref, g_ref, o_ref):
        o_ref[...] = (x_ref[...].astype(F32) + g_ref[...].astype(F32)).astype(o_ref.dtype)

    gs = pltpu.PrefetchScalarGridSpec(
        num_scalar_prefetch=1, grid=(n,),
        in_specs=[pl.BlockSpec((1, h, cdim), lambda k, c_ref: (k, c_ref[0], 0)), pl.BlockSpec((1, h, cdim), lambda k, c_ref: (k, 0, 0))],
        out_specs=pl.BlockSpec((1, h, cdim), lambda k, c_ref: (k, 0, 0)))
    return _pc(body, name=name, grid_spec=gs, out_shape=_sds((n, h, cdim), BF16),
               compiler_params=_cp(("arbitrary",), 24 << 20))(cvec, xv, got)


def sum_slots(a, name):
    n, r, cdim = a.shape
    tr = _div_tile(r, 512, 16)

    def body(a_ref, o_ref):
        acc = a_ref[0].astype(F32)
        for k in range(1, n):
            acc = acc + a_ref[k].astype(F32)
        o_ref[...] = acc

    return _pc(body, name=name, grid=(r // tr,), in_specs=[pl.BlockSpec((n, tr, cdim), lambda i: (0, i, 0))],
               out_specs=pl.BlockSpec((tr, cdim), lambda i: (i, 0)), out_shape=_sds((r, cdim), F32),
               compiler_params=_cp(("parallel",), 32 << 20))(a)


def adamw(w, g, m, v, name):
    B, R, C = w.shape
    tr = _div_tile(R, max(8, (1 << 19) // max(C, 1) // 8 * 8), 8) if R % 8 == 0 else R
    c1 = 1.0 / (1.0 - ADAM_B1 ** ADAM_STEP)
    c2 = 1.0 / (1.0 - ADAM_B2 ** ADAM_STEP)

    def body(w_ref, g_ref, m_ref, v_ref, d_ref, mo_ref, vo_ref):
        gg = g_ref[...]
        mn = ADAM_B1 * m_ref[...] + (1.0 - ADAM_B1) * gg
        vn = ADAM_B2 * v_ref[...] + (1.0 - ADAM_B2) * (gg * gg)
        d_ref[...] = -ADAM_LR * ((mn * c1) / (jnp.sqrt(vn * c2) + ADAM_EPS) + ADAM_WD * w_ref[...])
        mo_ref[...] = mn
        vo_ref[...] = vn

    spec = pl.BlockSpec((1, tr, C), lambda b, i: (b, i, 0))
    return _pc(body, name=name, grid=(B, R // tr), in_specs=[spec] * 4, out_specs=[spec] * 3,
               out_shape=[_sds((B, R, C), F32)] * 3, compiler_params=_cp(("parallel", "parallel"), 32 << 20))(w, g, m, v)


def _me():
    return lax.axis_index("x"), lax.axis_index("y"), lax.axis_index("c")


def _flip(v, bit):
    return 1 - v if bit else v


def allgather8(xv, name):
    R = xv.shape[0]

    def body(x_ref, out_ref, sum_ref, send_sems, recv_sems):
        mx, my, mc = _me()
        me = 4 * mx + 2 * my + mc
        out_ref[me] = x_ref[...]
        sends, recvs = [], []
        for k in range(1, 8):
            px, py, pc = _flip(mx, k & 4), _flip(my, k & 2), _flip(mc, k & 1)
            peer = 4 * px + 2 * py + pc
            sends.append(pltpu.make_async_remote_copy(src_ref=x_ref, dst_ref=out_ref.at[me], send_sem=send_sems.at[k - 1],
                                                      recv_sem=recv_sems.at[k - 1], device_id=(px, py, pc), device_id_type=MESH))
            recvs.append(pltpu.make_async_remote_copy(src_ref=x_ref, dst_ref=out_ref.at[peer], send_sem=send_sems.at[k - 1],
                                                      recv_sem=recv_sems.at[k - 1], device_id=(px, py, pc), device_id_type=MESH))
        for cp in sends:
            cp.start()
        for cp in recvs:
            cp.wait_recv()
        for cp in sends:
            cp.wait_send()
        acc = out_ref[0]
        for d in range(1, 8):
            acc = acc + out_ref[d]
        sum_ref[...] = acc

    vm = pl.BlockSpec(memory_space=pltpu.VMEM)
    return _pc(body, name=name, pin=False, in_specs=[vm], out_specs=[vm, vm], out_shape=[_sds((8, R, 128), F32), _sds((R, 128), F32)],
               scratch_shapes=[pltpu.SemaphoreType.DMA((7,)), pltpu.SemaphoreType.DMA((7,))],
               compiler_params=_cp(None, 32 << 20))(xv)


def _other_chips(mx, my):
    return [(1 - mx, my), (mx, 1 - my), (1 - mx, 1 - my)]


def _halves(r, mc, mult):
    h = r // 2
    return pl.ds(pl.multiple_of(mc * h, mult), h), pl.ds(pl.multiple_of((1 - mc) * h, mult), h)


def _rcopy(src, dst, send_sems, recv_sems, k, to):
    return pltpu.make_async_remote_copy(src_ref=src, dst_ref=dst, send_sem=send_sems.at[k], recv_sem=recv_sems.at[k],
                                        device_id=to, device_id_type=MESH)


def _gather_body(xs, outs, send_sems, recv_sems):
    n = len(xs)
    mx, my, mc = _me()
    chip = 2 * mx + my
    sib = (mx, my, 1 - mc)
    chips = _other_chips(mx, my)
    idx = [2 * cx + cy for cx, cy in chips]
    cp = functools.partial(_rcopy, send_sems=send_sems, recv_sems=recv_sems)
    hv = [_halves(x.shape[0], mc, 16) for x in xs]
    first, passed = [], []
    for a in range(n):
        for j, (cx, cy) in enumerate(chips):
            first.append(cp(xs[a].at[hv[a][0]], outs[a].at[chip, hv[a][0]], k=6 * a + j, to=(cx, cy, mc)))
            first[-1].start()
    for a in range(n):
        for j in range(3):
            cp(xs[a].at[hv[a][0]], outs[a].at[idx[j], hv[a][0]], k=6 * a + j, to=sib).wait_recv()
            passed.append(cp(outs[a].at[idx[j], hv[a][0]], outs[a].at[idx[j], hv[a][0]], k=6 * a + 3 + j, to=sib))
            passed[-1].start()
    for a in range(n):
        for j in range(3):
            cp(xs[a].at[hv[a][1]], outs[a].at[idx[j], hv[a][1]], k=6 * a + 3 + j, to=sib).wait_recv()
    for c_ in first + passed:
        c_.wait_send()


def _my_chip():
    return 2 * lax.axis_index("x") + lax.axis_index("y")


def _own_slots(outs, shards):
    return [lax.dynamic_update_index_in_dim(o, x, _my_chip(), 0) for o, x in zip(outs, shards)]


def gather_weights(shards, name):
    n = len(shards)

    def body(*refs):
        _gather_body(refs[:n], refs[n:2 * n], *refs[2 * n:])

    hbm = pl.BlockSpec(memory_space=pl.ANY)
    outs = _pc(body, name=name, in_specs=[hbm] * n, out_specs=[hbm] * n, out_shape=[_sds((4,) + x.shape, x.dtype) for x in shards],
               scratch_shapes=[pltpu.SemaphoreType.DMA((6 * n,)), pltpu.SemaphoreType.DMA((6 * n,))])(*shards)
    return _own_slots(outs, shards)


GATHER_REST_ID = 3


def gather_weights_sc(shards, name):
    n = len(shards)
    x_refs = [jax.new_ref(x, memory_space=pltpu.MemorySpace.HBM) for x in shards]
    out_refs = [jax.empty_ref(_sds((4,) + x.shape, x.dtype), memory_space=pltpu.MemorySpace.HBM) for x in shards]

    @pl.kernel(mesh=plsc.ScalarSubcoreMesh(axis_name="sc", num_cores=1), name=name,
               scratch_types=(pltpu.SemaphoreType.DMA((6 * n,)), pltpu.SemaphoreType.DMA((6 * n,))),
               compiler_params=pltpu.CompilerParams(collective_id=GATHER_REST_ID))
    def launch(send_sems, recv_sems):
        mx, my, mc = _me()
        barrier = pltpu.get_barrier_semaphore()
        for peer in [(mx, my, 1 - mc)] + [(cx, cy, mc) for cx, cy in _other_chips(mx, my)]:
            pl.semaphore_signal(barrier, inc=1, device_id=peer, device_id_type=MESH)
        pl.semaphore_wait(barrier, 4)
        _gather_body(x_refs, out_refs, send_sems, recv_sems)

    launch()
    return _own_slots([o[...] for o in out_refs], shards)


def swap_halves(arrs, name):
    n = len(arrs)

    def body(*refs):
        xs, outs = refs[:n], refs[n:2 * n]
        send_sems, recv_sems = refs[2 * n:]
        mx, my, mc = _me()
        cps = []
        for a in range(n):
            theirs = _halves(xs[a].shape[1], mc, 16)[1]
            cps.append(_rcopy(xs[a].at[pl.ds(0, 4), theirs], outs[a], send_sems, recv_sems, a, (mx, my, 1 - mc)))
            cps[-1].start()
        for c_ in cps:
            c_.wait()

    hbm = pl.BlockSpec(memory_space=pl.ANY)
    return _pc(body, name=name, in_specs=[hbm] * n, out_specs=[hbm] * n,
               out_shape=[_sds((4, x.shape[1] // 2, x.shape[2]), x.dtype) for x in arrs],
               scratch_shapes=[pltpu.SemaphoreType.DMA((n,)), pltpu.SemaphoreType.DMA((n,))])(*arrs)


SCATTER_ID = 4


def scatter_chips_sc(arrs, name):
    n = len(arrs)
    x_refs = [jax.new_ref(x, memory_space=pltpu.MemorySpace.HBM) for x in arrs]
    out_refs = [jax.empty_ref(_sds(x.shape, x.dtype), memory_space=pltpu.MemorySpace.HBM) for x in arrs]

    @pl.kernel(mesh=plsc.ScalarSubcoreMesh(axis_name="sc", num_cores=1), name=name,
               scratch_types=(pltpu.SemaphoreType.DMA((3 * n,)), pltpu.SemaphoreType.DMA((3 * n,))),
               compiler_params=pltpu.CompilerParams(collective_id=SCATTER_ID))
    def launch(send_sems, recv_sems):
        mx, my, mc = _me()
        chip = 2 * mx + my
        chips = _other_chips(mx, my)
        idx = [2 * cx + cy for cx, cy in chips]
        barrier = pltpu.get_barrier_semaphore()
        for cx, cy in chips:
            pl.semaphore_signal(barrier, inc=1, device_id=(cx, cy, mc), device_id_type=MESH)
        pl.semaphore_wait(barrier, 3)
        cp = functools.partial(_rcopy, send_sems=send_sems, recv_sems=recv_sems)
        sends = []
        for a in range(n):
            for j, (cx, cy) in enumerate(chips):
                sends.append(cp(x_refs[a].at[idx[j]], out_refs[a].at[chip], k=3 * a + j, to=(cx, cy, mc)))
                sends[-1].start()
        for a in range(n):
            for j, (cx, cy) in enumerate(chips):
                cp(x_refs[a].at[idx[j]], out_refs[a].at[idx[j]], k=3 * a + j, to=(cx, cy, mc)).wait_recv()
        for c_ in sends:
            c_.wait_send()

    launch()
    return _own_slots([o[...] for o in out_refs], [lax.dynamic_index_in_dim(x, _my_chip(), 0, keepdims=False) for x in arrs])


def share_halves(parts, name):
    flat = [p for w in parts for p in w]
    nw, n = len(parts), len(flat)
    depth = n // nw

    def body(*refs):
        xs, outs = refs[:n], refs[n:n + nw]
        send_sems, recv_sems = refs[n + nw:]
        mx, my, mc = _me()
        sib = (mx, my, 1 - mc)
        sends, recvs = [], []
        for a in range(n):
            w, l = a // depth, a % depth
            mine, theirs = _halves(outs[w].shape[1], mc, 8)
            sends.append(_rcopy(xs[a], outs[w].at[l, mine], send_sems, recv_sems, a, sib))
            recvs.append(_rcopy(xs[a], outs[w].at[l, theirs], send_sems, recv_sems, a, sib))
            sends[-1].start()
        for c_ in recvs:
            c_.wait_recv()
        for c_ in sends:
            c_.wait_send()

    hbm = pl.BlockSpec(memory_space=pl.ANY)
    outs = _pc(body, name=name, in_specs=[hbm] * n, out_specs=[hbm] * nw,
               out_shape=[_sds((depth, 2 * w[0].shape[0], w[0].shape[1]), F32) for w in parts],
               scratch_shapes=[pltpu.SemaphoreType.DMA((n,)), pltpu.SemaphoreType.DMA((n,))])(*flat)
    outs = list(outs)
    mc = lax.axis_index("c")
    for w in range(nw):
        for l in range(depth):
            h = parts[w][l].shape[0]
            outs[w] = lax.dynamic_update_slice(outs[w], parts[w][l][None], (l, mc * h, 0))
    return outs


_BIG = ("w_in", "w_out", "w_ffn_in", "w_ffn_out")
N_CHIPS = 4
DEPTH = 2


def _pad_rows(v, mult=8):
    n = v.shape[0]
    rows = -(-n // 128)
    rows = -(-rows // mult) * mult
    return jnp.pad(v, (0, rows * 128 - n)).reshape(rows, 128)


class _Flat:
    def __init__(self):
        self.items = []

    def add(self, name, a):
        self.items.append((name, a.shape, a.reshape(-1).astype(F32)))

    def rows(self):
        return _pad_rows(jnp.concatenate([a for _, _, a in self.items]))

    def split(self, rows):
        flat = rows.reshape(-1)
        out, o = {}, 0
        for name, shape, a in self.items:
            out[name] = flat[o:o + a.shape[0]].reshape(shape)
            o += a.shape[0]
        return out

    def split_lead(self, rows3):
        n = rows3.shape[0]
        flat = rows3.reshape(n, -1)
        out, o = {}, 0
        for name, shape, a in self.items:
            out[name] = flat[:, o:o + a.shape[0]].reshape((n,) + tuple(shape))
            o += a.shape[0]
        return out


def _gsv(rows):
    z = jnp.zeros((2, D), F32)
    r = [z if a is None else a for a in rows] + [z] * 5
    return jnp.stack(r, axis=1)


def _pad8(a, rows=8, cols=128):
    return jnp.zeros((rows, cols), F32).at[:a.shape[0], :a.shape[1]].set(a.astype(F32))


def kernel(x, c, ctx, c_ctx, w_mod, b_mod, g_mix, w_in, wa_sink, na_rpb, ssm_conv_w, ssm_conv_b, ssm_dt_bias, ssm_a_log, ssm_d, ssm_norm_g, w_out, g_ffn, w_ffn_in, w_ffn_out, g_final, loss_target, m_c_ctx, m_w_mod, m_b_mod, m_g_mix, m_w_in, m_wa_sink, m_na_rpb, m_ssm_conv_w, m_ssm_conv_b, m_ssm_dt_bias, m_ssm_a_log, m_ssm_d, m_ssm_norm_g, m_w_out, m_g_ffn, m_w_ffn_in, m_w_ffn_out, m_g_final, v_c_ctx, v_w_mod, v_b_mod, v_g_mix, v_w_in, v_wa_sink, v_na_rpb, v_ssm_conv_w, v_ssm_conv_b, v_ssm_dt_bias, v_ssm_a_log, v_ssm_d, v_ssm_norm_g, v_w_out, v_g_ffn, v_w_ffn_in, v_w_ffn_out, v_g_final):
    L, Lc = x.shape[1], ctx.shape[1]
    T = L + Lc
    nL = L // TR
    mx, my, mc = lax.axis_index("x"), lax.axis_index("y"), lax.axis_index("c")
    dev = 4 * mx + 2 * my + mc
    chip = 2 * mx + my
    MODW = 6 * D // N_CHIPS
    CW = 1024 // N_CHIPS

    sc = _silu(c.astype(F32))
    scc = _silu(c_ctx.astype(F32))[None]
    f1 = _Flat()
    f1.add("sc", sc)
    f1.add("conv_w", ssm_conv_w)
    g1, _ = allgather8(f1.rows(), "gather_cond")
    g1 = f1.split_lead(g1)
    sc_all = g1["sc"][:, 0]
    conv_w = jnp.concatenate([g1["conv_w"][2 * k] for k in range(N_CHIPS)], axis=-1)
    A16 = jnp.concatenate([sc_all, scc, jnp.zeros((7, D), F32)], axis=0)

    mod_part = jnp.stack([matmul(A16, w_mod[l], "nn", F32, f"mod_fwd{l}") for l in range(DEPTH)])
    f2 = _Flat()
    f2.add("mod", mod_part)
    g2, _ = allgather8(f2.rows(), "gather_mod")
    g2 = f2.split_lead(g2)["mod"]
    mods = jnp.concatenate([g2[2 * k] for k in range(N_CHIPS)], axis=-1) + b_mod[:, None, :]
    mod_l = lax.dynamic_index_in_dim(mods, dev, axis=1, keepdims=False).reshape(DEPTH, 6, D)
    mod_c = mods[:, 8].reshape(DEPTH, 6, D)
    mod = jnp.stack([mod_l, mod_c], axis=1)
    mrow = lambda l, j: mod[l, :, j]

    own = {"w_in": w_in, "w_out": w_out, "w_ffn_in": w_ffn_in, "w_ffn_out": w_ffn_out}
    sh16 = [own[n][l].astype(BF16) for n in _BIG for l in range(DEPTH)]
    gath = list(gather_weights(sh16[:1], "gather_first"))
    after_first = (gath[0][0, 0, 0] * 0).astype(BF16)
    gath += list(gather_weights_sc([sh16[1] + after_first] + sh16[2:], "gather_rest"))
    gw = {n: [gath[DEPTH * i + l] for l in range(DEPTH)] for i, n in enumerate(_BIG)}
    W_in = [jnp.pad(jnp.concatenate([g[k] for k in range(N_CHIPS)], axis=1), ((0, 0), (0, IN_PAD - IN_COLS))) for g in gw["w_in"]]
    W_out = [g.reshape(D, D) for g in gw["w_out"]]
    W_fo = [g.reshape(D_FF, D) for g in gw["w_ffn_out"]]
    W_fi = gw["w_ffn_in"]

    cos, sin, rotm = rope_tables(L, Lc)
    x0 = jnp.concatenate([x[0], ctx[0]], axis=0).astype(F32)

    sv = []
    xin = x0
    gsv_first = _gsv([None, mrow(0, 0), mrow(0, 1)])
    _, h1 = res_norm_mod(x0, None, gsv_first, g_mix[0][None], nL, "norm_first")
    for l in range(DEPTH):
        s = {"xin": xin, "h1": h1}
        P = matmul(h1, W_in[l], "nn", F32, f"in_proj{l}", tn=IN_PAD)
        qr, kr, kb, vb = rope_apply(P, C_QA // 256, P, C_KA // 128, cos, sin, rotm, False, f"rope{l}", kv_src=P)
        sink8 = _pad8(jnp.broadcast_to(wa_sink[l][:, None], (WA_HEADS, 128)))
        oa = win_attn_fwd(qr, kr, P, sink8, L, Lc, f"wa_fwd{l}")
        bias = na_bias_table(na_rpb[l], l)
        ob = na_fwd(P, kb, vb, bias, L, Lc, f"na_fwd{l}")
        w8 = jnp.concatenate([conv_w[l], jnp.zeros((1, 1024), F32)], axis=0)
        pre, act = conv_silu_fwd(P, w8, ssm_conv_b[l][None], nL, f"conv_fwd{l}")
        dtb8, al8 = _pad8(ssm_dt_bias[l]), _pad8(ssm_a_log[l])
        yf, yb, hsf, hsb = ssd_fwd(act, P, dtb8, al8, L, Lc, f"ssd_fwd{l}")
        dskip = jnp.repeat(ssm_d[l], S_P)[None]
        oc = ssm_out_fwd(yf, yb, act, P, dskip, ssm_norm_g[l][None], f"ssm_out_fwd{l}")
        mixin = jnp.concatenate([oa, ob, oc], axis=1)
        mix = matmul(mixin, W_out[l], "nn", F32, f"out_proj{l}")
        gsv_mid = _gsv([mrow(l, 2), mrow(l, 3), mrow(l, 4)])
        x1, h2 = res_norm_mod(xin, mix, gsv_mid, g_ffn[l][None], nL, f"norm_mid{l}")
        gu = matmul_fi(h2, W_fi[l], "nn", BF16, f"ffn_in{l}")
        af = swiglu_fwd(gu, f"swiglu_fwd{l}")
        fo = matmul(af, W_fo[l], "nn", F32, f"ffn_out{l}", tk=D_FF)
        s.update(P=P, qr=qr, kr=kr, sink8=sink8, kb=kb, vb=vb, bias=bias, w8=w8, pre=pre, act=act, dtb8=dtb8, al8=al8, yf=yf,
                 yb=yb, hsf=hsf, hsb=hsb, dskip=dskip, mixin=mixin, mix=mix, gsv_mid=gsv_mid, x1=x1, h2=h2, gu=gu, af=af, fo=fo)
        if l + 1 < DEPTH:
            s["gsv_end"] = _gsv([mrow(l, 5), mrow(l + 1, 0), mrow(l + 1, 1)])
            xin, h1 = res_norm_mod(x1, fo, s["gsv_end"], g_mix[l + 1][None], nL, f"norm_end{l}")
        else:
            s["gsv_end"] = _gsv([mrow(l, 5), None, None])
        sv.append(s)

    last = sv[-1]
    loss8, dres, dfo, dgsv_end, dg_final = final_loss(last["x1"], last["fo"], last["gsv_end"], g_final[None], loss_target[0].astype(F32), nL, "final_loss")
    loss = lax.psum(loss8[0, 0], ("x", "y", "c"))

    dmod = [[None] * 6 for _ in range(DEPTH)]
    gW = {n: [None] * DEPTH for n in _BIG}
    small = [dict() for _ in range(DEPTH)]
    parts = [None] * DEPTH
    cvec = mc.astype(jnp.int32).reshape(1)
    grad_x = None
    for l in reversed(range(DEPTH)):
        s = sv[l]
        dmod[l][5] = dgsv_end[:, 0]
        if l + 1 < DEPTH:
            dmod[l + 1][0], dmod[l + 1][1] = dgsv_end[:, 1], dgsv_end[:, 2]
        daf = matmul(dfo, W_fo[l], "nt", BF16, f"ffn_out_dx{l}")
        gW["w_ffn_out"][l] = matmul(s["af"], dfo, "tn", BF16, f"ffn_out_dw{l}", tm=1408, tk=T).reshape(N_CHIPS, D_FF // N_CHIPS, D)
        dgu = swiglu_bwd(s["gu"], daf, f"swiglu_bwd{l}")
        dh2 = matmul_fi(dgu, W_fi[l], "nt", F32, f"ffn_in_dx{l}")
        gW["w_ffn_in"][l] = matmul_fi(s["h2"], dgu, "tn", BF16, f"ffn_in_dw{l}")
        dres, dmix, dgsv_mid, dg_ffn = res_norm_mod_bwd(s["x1"], s["mix"], s["gsv_mid"], g_ffn[l][None], dh2, dres, nL, f"norm_mid_bwd{l}")
        dmod[l][2], dmod[l][3], dmod[l][4] = dgsv_mid[:, 0], dgsv_mid[:, 1], dgsv_mid[:, 2]
        dmixin = matmul(dmix, W_out[l], "nt", F32, f"out_proj_dx{l}")
        gW["w_out"][l] = matmul(s["mixin"], dmix, "tn", BF16, f"out_proj_dw{l}", tm=1024, tk=T).reshape(N_CHIPS, D // N_CHIPS, D)
        P = s["P"]
        dqr, dkr, dva, dsink = win_attn_bwd(s["qr"], s["kr"], P, s["sink8"], dmixin, L, Lc, f"wa_bwd{l}")
        dqa, dka = rope_apply(dqr, 0, dkr[WA_BLK:WA_BLK + T], 0, cos, sin, rotm, True, f"rope_bwd{l}")
        dqb, dkb, dvb, dbias = na_bwd(P, s["kb"], s["vb"], s["bias"], dmixin, L, Lc, f"na_bwd{l}")
        dy, dxs1, dz, dvec = ssm_out_bwd(s["yf"], s["yb"], s["act"], P, s["dskip"], ssm_norm_g[l][None], dmixin, f"ssm_out_bwd{l}")
        dxf, dbf, dcf, ddf, dxb, dbb, dcb, ddb, ddtb, dal = ssd_bwd(s["act"], P, s["dtb8"], s["al8"], s["hsf"], s["hsb"], dy, L, Lc, f"ssd_bwd{l}")
        dpre = dsilu(s["pre"], [dxf, dxb, dxs1], [dbf, dbb], [dcf, dcb], f"dsilu{l}")
        dxbc, dw8, db8 = conv_bwd(dpre, P, s["w8"], nL, f"conv_bwd{l}")
        dP = jnp.concatenate([dqa, dqb, dz, dka, dva[WA_BLK:WA_BLK + T].astype(BF16), dkb.astype(BF16), dvb.astype(BF16), dxbc,
                              ddf.astype(BF16), ddb.astype(BF16), jnp.zeros((T, IN_PAD - IN_COLS), BF16)], axis=1)
        dh1 = matmul(dP, W_in[l], "nt", F32, f"in_proj_dx{l}", tk=IN_PAD)
        dwin = matmul(s["h1"], dP, "tn", BF16, f"in_proj_dw{l}", tm=512, tn=IN_PAD, tk=T // 2)
        cw = IN_COLS // N_CHIPS
        gW["w_in"][l] = jnp.stack([dwin[:, k * cw:(k + 1) * cw] for k in range(N_CHIPS)])
        garr = [gW[n][l] for n in _BIG]
        got = swap_halves(garr, f"reduce_d2d{l}")
        chip_sum = [add_halves(garr[a], got[a], cvec, f"reduce_add_pair{l}_{a}") for a in range(len(garr))]
        parts[l] = scatter_chips_sc(chip_sum, f"reduce_ici{l}")
        small[l] = dict(g_ffn=dg_ffn[0], wa_sink=dsink[:WA_HEADS, 0], na_rpb=na_rpb_grad(dbias, l), conv_w=dw8[:S_CONV], conv_b=db8[0],
                        dt_bias=ddtb[:2, :8], a_log=dal[:2, :8], ssm_d=dvec[0].reshape(S_HEADS, S_P).sum(axis=1), norm_g=dvec[1])
        if l > 0:
            p = sv[l - 1]
            dres, dfo, dgsv_end, dg_mix = res_norm_mod_bwd(s["xin"], p["fo"], p["gsv_end"], g_mix[l][None], dh1, dres, nL, f"norm_end_bwd{l - 1}")
        else:
            grad_x, _, dgsv_first, dg_mix = res_norm_mod_bwd(s["xin"], None, gsv_first, g_mix[0][None], dh1, dres, nL, "norm_first_bwd")
            dmod[0][0], dmod[0][1] = dgsv_first[:, 1], dgsv_first[:, 2]
        small[l]["g_mix"] = dg_mix[0]
    for l in range(DEPTH):
        for j in range(6):
            if dmod[l][j] is None:
                dmod[l][j] = jnp.zeros((2, D), F32)
    dmod = jnp.stack([jnp.stack(r, axis=1) for r in dmod])

    f3 = _Flat()
    f3.add("dmod_l", dmod[:, 0].reshape(DEPTH, 6 * D))
    f3.add("dmod_c", dmod[:, 1].reshape(DEPTH, 6 * D))
    f3.add("g_final", dg_final[0])
    for n in ("g_mix", "g_ffn", "wa_sink", "na_rpb", "conv_w", "conv_b", "dt_bias", "a_log", "ssm_d", "norm_g"):
        f3.add(n, jnp.stack([small[l][n] for l in range(DEPTH)]))
    g3, s3 = allgather8(f3.rows(), "reduce_small")
    dmod_all = f3.split_lead(g3)["dmod_l"]
    s3 = f3.split(s3)
    dmodc_tot = s3["dmod_c"]
    col0 = chip * MODW
    G16, G16c = [], []
    for l in range(DEPTH):
        rows = jnp.concatenate([dmod_all[:, l], dmodc_tot[l][None], jnp.zeros((7, 6 * D), F32)], axis=0)
        G16.append(lax.dynamic_slice_in_dim(rows, col0, MODW, axis=1))
        rc = jnp.concatenate([dmodc_tot[l][None], jnp.zeros((15, 6 * D), F32)], axis=0)
        G16c.append(lax.dynamic_slice_in_dim(rc, col0, MODW, axis=1))
    grad_w_mod = jnp.stack([matmul(A16, G16[l], "tn", F32, f"mod_dw{l}") for l in range(DEPTH)])
    dscc_part = sum(matmul(G16c[l], w_mod[l], "nt", F32, f"mod_dx{l}")[0] for l in range(DEPTH))
    _, s4 = allgather8(_pad_rows(dscc_part * (mc == 1).astype(F32)), "reduce_cctx")
    dscc = s4.reshape(-1)[:D]
    cc = c_ctx.astype(F32)
    sg = 1.0 / (1.0 + jnp.exp(-cc))
    grad_c_ctx = dscc * (sg * (1.0 + cc * (1.0 - sg)))

    halves = [[sum_slots(parts[l][i], f"reduce_add_chips{l}_{i}") for l in range(DEPTH)] for i in range(len(_BIG))]
    gsh = dict(zip(_BIG, share_halves(halves, "reduce_share")))

    grads = {"c_ctx": grad_c_ctx, "w_mod": grad_w_mod, "b_mod": s3["dmod_l"] + s3["dmod_c"], "g_mix": s3["g_mix"], "w_in": gsh["w_in"],
             "wa_sink": s3["wa_sink"], "na_rpb": s3["na_rpb"],
             "ssm_conv_w": lax.dynamic_slice_in_dim(s3["conv_w"], chip * CW, CW, axis=2), "ssm_conv_b": s3["conv_b"],
             "ssm_dt_bias": s3["dt_bias"], "ssm_a_log": s3["a_log"], "ssm_d": s3["ssm_d"], "ssm_norm_g": s3["norm_g"],
             "w_out": gsh["w_out"], "g_ffn": s3["g_ffn"], "w_ffn_in": gsh["w_ffn_in"], "w_ffn_out": gsh["w_ffn_out"], "g_final": s3["g_final"]}
    wts = {"c_ctx": c_ctx, "w_mod": w_mod, "b_mod": b_mod, "g_mix": g_mix, "w_in": w_in, "wa_sink": wa_sink, "na_rpb": na_rpb,
           "ssm_conv_w": ssm_conv_w, "ssm_conv_b": ssm_conv_b, "ssm_dt_bias": ssm_dt_bias, "ssm_a_log": ssm_a_log, "ssm_d": ssm_d,
           "ssm_norm_g": ssm_norm_g, "w_out": w_out, "g_ffn": g_ffn, "w_ffn_in": w_ffn_in, "w_ffn_out": w_ffn_out, "g_final": g_final}
    ms = {"c_ctx": m_c_ctx, "w_mod": m_w_mod, "b_mod": m_b_mod, "g_mix": m_g_mix, "w_in": m_w_in, "wa_sink": m_wa_sink, "na_rpb": m_na_rpb,
          "ssm_conv_w": m_ssm_conv_w, "ssm_conv_b": m_ssm_conv_b, "ssm_dt_bias": m_ssm_dt_bias, "ssm_a_log": m_ssm_a_log, "ssm_d": m_ssm_d,
          "ssm_norm_g": m_ssm_norm_g, "w_out": m_w_out, "g_ffn": m_g_ffn, "w_ffn_in": m_w_ffn_in, "w_ffn_out": m_w_ffn_out, "g_final": m_g_final}
    vs = {"c_ctx": v_c_ctx, "w_mod": v_w_mod, "b_mod": v_b_mod, "g_mix": v_g_mix, "w_in": v_w_in, "wa_sink": v_wa_sink, "na_rpb": v_na_rpb,
          "ssm_conv_w": v_ssm_conv_w, "ssm_conv_b": v_ssm_conv_b, "ssm_dt_bias": v_ssm_dt_bias, "ssm_a_log": v_ssm_a_log, "ssm_d": v_ssm_d,
          "ssm_norm_g": v_ssm_norm_g, "w_out": v_w_out, "g_ffn": v_g_ffn, "w_ffn_in": v_w_ffn_in, "w_ffn_out": v_w_ffn_out, "g_final": v_g_final}
    names = list(wts)
    grads = {n: grads[n].reshape(wts[n].shape).astype(F32) for n in names}
    big = ("w_mod", "w_in", "w_out", "w_ffn_in", "w_ffn_out")
    delta, new_m, new_v = {}, {}, {}
    for n in big:
        delta[n], new_m[n], new_v[n] = adamw(wts[n], grads[n], ms[n], vs[n], f"adamw_{n}")
    packs = []
    for src in (wts, grads, ms, vs):
        f = _Flat()
        for n in names:
            if n not in big:
                f.add(n, src[n])
        packs.append(f)
    d_, m_, v_ = adamw(*[f.rows()[None] for f in packs], "adamw_small")
    for dst, rows in ((delta, d_), (new_m, m_), (new_v, v_)):
        dst.update(packs[0].split(rows[0]))

    return (loss, grad_x[:L][None], *[grads[n] for n in names], *[delta[n] for n in names],
            *[new_m[n] for n in names], *[new_v[n] for n in names])
```

```python
import functools

import numpy as np
import jax
import jax.numpy as jnp
from jax import lax
from jax.experimental import pallas as pl
from jax.experimental.pallas import tpu as pltpu
from jax.experimental.pallas import tpu_sc as plsc

F32 = jnp.float32
BF16 = jnp.bfloat16
_MXU = jnp.bfloat16
_HI = lax.Precision.HIGHEST
MESH = pl.DeviceIdType.MESH

D = 1024
HD = 64
GRID_W = 64
EPS = 1e-6
ROPE_BASE = 10000.0
WA_HEADS, WA_KV = 4, 2
WA_BLK = 128
NA_HEADS, NA_KH, NA_KW = 4, 8, 16
S_HEADS, S_P, S_INNER, S_GROUPS, S_N, S_CONV, S_Q = 8, 64, 512, 2, 128, 7, 128
D_FF = 2816
IN_COLS = 2832
IN_PAD = 2944
C_QA, C_QB, C_Z, C_KA, C_VA, C_KB, C_VB, C_XBC, C_DT = 0, 256, 512, 1024, 1152, 1280, 1536, 1792, 2816
ADAM_LR, ADAM_B1, ADAM_B2, ADAM_EPS, ADAM_WD, ADAM_STEP = 0.001, 0.9, 0.999, 1e-08, 0.01, 10

TR = 256
NEG = -1e30
VMEM_CAP = 56 * 1024 * 1024


PIN_BYTES = 256 * 1024


def _is_big(a):
    return hasattr(a, "shape") and len(a.shape) >= 2 and int(np.prod(a.shape)) * jnp.dtype(a.dtype).itemsize >= PIN_BYTES


def _pc(body, *, out_shape, pin=True, **kw):
    if not pin:
        return pl.pallas_call(body, out_shape=out_shape, **kw)
    one = isinstance(out_shape, jax.ShapeDtypeStruct)
    outs = [pltpu.HBM(s.shape, s.dtype) if _is_big(s) else s for s in ([out_shape] if one else out_shape)]
    call = pl.pallas_call(body, out_shape=outs[0] if one else outs, **kw)
    return lambda *args: call(*[pltpu.with_memory_space_constraint(a, pltpu.HBM) if _is_big(a) else a for a in args])


def _cp(sem=None, vmem=None):
    kw = {}
    if sem is not None:
        kw["dimension_semantics"] = sem
    if vmem is not None:
        kw["vmem_limit_bytes"] = int(min(max(vmem, 16 * 1024 * 1024), VMEM_CAP))
    return pltpu.CompilerParams(**kw)


def _sds(shape, dtype):
    return jax.ShapeDtypeStruct(tuple(shape), dtype)


_DIMS = {"nn": ((1,), (0,)), "nt": ((1,), (1,)), "tn": ((0,), (0,))}


def _dg(a, b, dims):
    return lax.dot_general(a.astype(_MXU), b.astype(_MXU), (dims, ((), ())), preferred_element_type=F32)


@functools.partial(jax.custom_vjp, nondiff_argnums=(2,))
def bdot(a, b, mode):
    return _dg(a, b, _DIMS[mode])


def _bdot_fwd(a, b, mode):
    return bdot(a, b, mode), (a, b)


def _bdot_bwd(mode, res, g):
    a, b = res
    if mode == "nn":
        return bdot(g, b, "nt"), bdot(a, g, "tn")
    if mode == "nt":
        return bdot(g, b, "nn"), bdot(g, a, "tn")
    return bdot(b, g, "nt"), bdot(a, g, "nn")


bdot.defvjp(_bdot_fwd, _bdot_bwd)


def hdot(a, b, mode="nn"):
    return lax.dot_general(a, b, (_DIMS[mode], ((), ())), precision=_HI, preferred_element_type=F32)


def _silu(x):
    return x / (1.0 + jnp.exp(-x))


def _softplus(x):
    return jnp.maximum(x, 0.0) + jnp.log(1.0 + jnp.exp(-jnp.abs(x)))


def _div_tile(n, cap, mult):
    if n <= cap:
        return n
    best = None
    for t in range(mult, cap + 1, mult):
        if n % t == 0:
            best = t
    assert best is not None, (n, cap, mult)
    return best


def matmul(a, b, mode, out_dtype, name, tm=640, tn=1536, tk=1408, hi=False):
    if mode == "tn":
        K, M = a.shape
    else:
        M, K = a.shape
    N = b.shape[0] if mode == "nt" else b.shape[1]
    tm = _div_tile(M, tm, 128 if mode == "tn" else 16)
    tn = _div_tile(N, tn, 128)
    tk = _div_tile(K, tk, 128 if mode != "tn" else 16)
    nk = K // tk
    dims = _DIMS[mode]

    def body(a_ref, b_ref, o_ref, *acc):
        if hi:
            part = lax.dot_general(a_ref[...], b_ref[...], (dims, ((), ())), precision=_HI, preferred_element_type=F32)
        else:
            part = _dg(a_ref[...], b_ref[...], dims)
        if nk == 1:
            o_ref[...] = part.astype(o_ref.dtype)
        else:
            k = pl.program_id(2)

            @pl.when(k == 0)
            def _():
                acc[0][...] = part

            @pl.when(k > 0)
            def _():
                acc[0][...] += part

            @pl.when(k == nk - 1)
            def _():
                o_ref[...] = acc[0][...].astype(o_ref.dtype)

    if mode == "tn":
        a_spec = pl.BlockSpec((tk, tm), lambda i, j, k: (k, i))
    else:
        a_spec = pl.BlockSpec((tm, tk), lambda i, j, k: (i, k))
    if mode == "nt":
        b_spec = pl.BlockSpec((tn, tk), lambda i, j, k: (j, k))
    else:
        b_spec = pl.BlockSpec((tk, tn), lambda i, j, k: (k, j))
    isz = lambda x: jnp.dtype(x.dtype).itemsize
    vmem = 2 * (tm * tk * isz(a) + tk * tn * isz(b) + tm * tn * jnp.dtype(out_dtype).itemsize) + 3 * tm * tn * 4
    return _pc(
        body, name=name, grid=(M // tm, N // tn, nk),
        in_specs=[a_spec, b_spec], out_specs=pl.BlockSpec((tm, tn), lambda i, j, k: (i, j)),
        out_shape=_sds((M, N), out_dtype),
        scratch_shapes=[pltpu.VMEM((tm, tn), F32)] if nk > 1 else [],
        compiler_params=_cp(("parallel", "parallel", "arbitrary"), vmem + (8 << 20)),
    )(a, b)


def _norm_mod(xo, shift, scale, g):
    r = lax.rsqrt(jnp.mean(xo * xo, axis=-1, keepdims=True) + EPS)
    return (xo * r) * g * (1.0 + scale) + shift


def res_norm_mod(x, y, gsv, g, nL, name):
    T = x.shape[0]
    has_y = y is not None

    def body(*refs):
        if has_y:
            x_ref, y_ref, gsv_ref, g_ref, xo_ref, h_ref = refs
            xo = x_ref[...] + gsv_ref[0, 0:1, :] * y_ref[...]
            xo_ref[...] = xo
        else:
            x_ref, gsv_ref, g_ref, h_ref = refs
            xo = x_ref[...]
        h_ref[...] = _norm_mod(xo, gsv_ref[0, 1:2, :], gsv_ref[0, 2:3, :], g_ref[...]).astype(h_ref.dtype)

    row = pl.BlockSpec((TR, D), lambda i: (i, 0))
    in_specs = [row] + ([row] if has_y else []) + [pl.BlockSpec((1, 8, D), lambda i: (i // nL, 0, 0)),
                                                     pl.BlockSpec((1, D), lambda i: (0, 0))]
    out_specs = ([row] if has_y else []) + [row]
    out_shape = ([_sds((T, D), F32)] if has_y else []) + [_sds((T, D), BF16)]
    args = (x, y, gsv, g) if has_y else (x, gsv, g)
    outs = _pc(body, name=name, grid=(T // TR,), in_specs=in_specs, out_specs=out_specs, out_shape=out_shape,
               compiler_params=_cp(("arbitrary",), 24 << 20))(*args)
    return (outs[0], outs[1]) if has_y else (None, outs[0])


def res_norm_mod_bwd(xo, y, gsv, g, dh, dres, nL, name):
    T = xo.shape[0]
    has_y = y is not None

    def body(*refs):
        if has_y:
            xo_ref, y_ref, gsv_ref, g_ref, dh_ref, dres_ref, dx_ref, dy_ref, dgsv_ref, dg_ref = refs
        else:
            xo_ref, gsv_ref, g_ref, dh_ref, dres_ref, dx_ref, dgsv_ref, dg_ref = refs
        i = pl.program_id(0)

        @pl.when((i == 0) | (i == nL))
        def _():
            dgsv_ref[...] = jnp.zeros_like(dgsv_ref)

        @pl.when(i == 0)
        def _():
            dg_ref[...] = jnp.zeros_like(dg_ref)

        _, vjp = jax.vjp(_norm_mod, xo_ref[...], gsv_ref[0, 1:2, :], gsv_ref[0, 2:3, :], g_ref[...])
        dxn, dshift, dscale, dg = vjp(dh_ref[...].astype(F32))
        dxo = dres_ref[...] + dxn
        dx_ref[...] = dxo
        if has_y:
            dy_ref[...] = (gsv_ref[0, 0:1, :] * dxo).astype(dy_ref.dtype)
            dgsv_ref[0, 0:1, :] += jnp.sum(y_ref[...] * dxo, axis=0, keepdims=True)
        dgsv_ref[0, 1:2, :] += dshift
        dgsv_ref[0, 2:3, :] += dscale
        dg_ref[0:1, :] += dg

    row = pl.BlockSpec((TR, D), lambda i: (i, 0))
    gspec = pl.BlockSpec((1, 8, D), lambda i: (i // nL, 0, 0))
    in_specs = [row] + ([row] if has_y else []) + [gspec, pl.BlockSpec((1, D), lambda i: (0, 0)), row, row]
    out_specs = [row] + ([row] if has_y else []) + [gspec, pl.BlockSpec((8, D), lambda i: (0, 0))]
    out_shape = [_sds((T, D), F32)] + ([_sds((T, D), BF16)] if has_y else []) + [_sds((2, 8, D), F32), _sds((8, D), F32)]
    args = (xo, y, gsv, g, dh, dres) if has_y else (xo, gsv, g, dh, dres)
    outs = _pc(body, name=name, grid=(T // TR,), in_specs=in_specs, out_specs=out_specs, out_shape=out_shape,
               compiler_params=_cp(("arbitrary",), 32 << 20))(*args)
    if has_y:
        return outs
    return outs[0], None, outs[1], outs[2]


def final_loss(x, y, gsv, g, target, nL, name):
    T = x.shape[0]

    def lossf(xo, gv, t):
        yn = (xo * lax.rsqrt(jnp.mean(xo * xo, axis=-1, keepdims=True) + EPS)) * gv
        e = yn - t
        return 0.5 * jnp.sum(jnp.sum(e * e, axis=-1, keepdims=True) * (1.0 / D), axis=0, keepdims=True)

    def body(x_ref, y_ref, gsv_ref, g_ref, t_ref, loss_ref, dx_ref, dy_ref, dgsv_ref, dg_ref):
        i = pl.program_id(0)

        @pl.when(i == 0)
        def _():
            loss_ref[...] = jnp.zeros_like(loss_ref)
            dg_ref[...] = jnp.zeros_like(dg_ref)

        @pl.when((i == 0) | (i == nL))
        def _():
            dgsv_ref[...] = jnp.zeros_like(dgsv_ref)

        @pl.when(i < nL)
        def _():
            gate = gsv_ref[0, 0:1, :]
            yv = y_ref[...]
            xo = x_ref[...] + gate * yv
            lv, vjp = jax.vjp(lossf, xo, g_ref[...], t_ref[...])
            dxo, dg, _ = vjp(jnp.ones((1, 1), F32))
            loss_ref[...] += jnp.broadcast_to(lv, loss_ref.shape)
            dx_ref[...] = dxo
            dy_ref[...] = (gate * dxo).astype(dy_ref.dtype)
            dgsv_ref[0, 0:1, :] += jnp.sum(yv * dxo, axis=0, keepdims=True)
            dg_ref[0:1, :] += dg

        @pl.when(i >= nL)
        def _():
            dx_ref[...] = jnp.zeros_like(dx_ref)
            dy_ref[...] = jnp.zeros_like(dy_ref)

    row = pl.BlockSpec((TR, D), lambda i: (i, 0))
    gspec = pl.BlockSpec((1, 8, D), lambda i: (i // nL, 0, 0))
    return _pc(
        body, name=name, grid=(T // TR,),
        in_specs=[row, row, gspec, pl.BlockSpec((1, D), lambda i: (0, 0)),
                  pl.BlockSpec((TR, D), lambda i: (jnp.minimum(i, nL - 1), 0))],
        out_specs=[pl.BlockSpec((8, 128), lambda i: (0, 0)), row, row, gspec, pl.BlockSpec((8, D), lambda i: (0, 0))],
        out_shape=[_sds((8, 128), F32), _sds((T, D), F32), _sds((T, D), BF16), _sds((2, 8, D), F32), _sds((8, D), F32)],
        compiler_params=_cp(("arbitrary",), 32 << 20),
    )(x, y, gsv, g, target)


FI_BLK = 2 * D_FF // 4


def _fi_chip(j):
    return (j % 2) * 2 + j // 2


def matmul_fi(a, b, mode, out_dtype, name):
    T = a.shape[0]
    if mode == "tn":
        tmd = 512

        def body(a_ref, b_ref, o_ref):
            o_ref[0] = _dg(a_ref[...], b_ref[...], _DIMS["tn"]).astype(o_ref.dtype)

        return _pc(body, name=name, grid=(D // tmd, 4),
                   in_specs=[pl.BlockSpec((T, tmd), lambda i, j: (0, i)), pl.BlockSpec((T, FI_BLK), lambda i, j: (0, j))],
                   out_specs=pl.BlockSpec((1, tmd, FI_BLK), lambda i, j: (_fi_chip(j), i, 0)),
                   out_shape=_sds((4, D, FI_BLK), out_dtype), compiler_params=_cp(("parallel", "arbitrary"), 48 << 20))(a, b)
    if mode == "nn":
        tm = _div_tile(T, 1280, 16)

        def body(a_ref, b_ref, o_ref):
            o_ref[...] = _dg(a_ref[...], b_ref[0], _DIMS["nn"]).astype(o_ref.dtype)

        return _pc(body, name=name, grid=(T // tm, 4),
                   in_specs=[pl.BlockSpec((tm, D), lambda i, j: (i, 0)), pl.BlockSpec((1, D, FI_BLK), lambda i, j: (_fi_chip(j), 0, 0))],
                   out_specs=pl.BlockSpec((tm, FI_BLK), lambda i, j: (i, j)), out_shape=_sds((T, 4 * FI_BLK), out_dtype),
                   compiler_params=_cp(("parallel", "arbitrary"), 40 << 20))(a, b)
    tm = _div_tile(T, 640, 16)

    def body(a_ref, b_ref, o_ref):
        acc = None
        for k in range(4):
            part = _dg(a_ref[:, k * FI_BLK:(k + 1) * FI_BLK], b_ref[_fi_chip(k)], _DIMS["nt"])
            acc = part if acc is None else acc + part
        o_ref[...] = acc.astype(o_ref.dtype)

    return _pc(body, name=name, grid=(T // tm,),
               in_specs=[pl.BlockSpec((tm, 4 * FI_BLK), lambda i: (i, 0)), pl.BlockSpec((4, D, FI_BLK), lambda i: (0, 0, 0))],
               out_specs=pl.BlockSpec((tm, D), lambda i: (i, 0)), out_shape=_sds((T, D), out_dtype),
               compiler_params=_cp(("parallel",), VMEM_CAP))(a, b)


def _swiglu(gate, up):
    return _silu(gate) * up


def swiglu_fwd(gu, name):
    T = gu.shape[0]

    def body(x_ref, o_ref):
        o_ref[...] = _swiglu(x_ref[:, :FI_BLK].astype(F32), x_ref[:, FI_BLK:].astype(F32)).astype(o_ref.dtype)

    return _pc(body, name=name, grid=(T // TR, 2), in_specs=[pl.BlockSpec((TR, 2 * FI_BLK), lambda i, j: (i, j))],
               out_specs=pl.BlockSpec((TR, FI_BLK), lambda i, j: (i, j)), out_shape=_sds((T, D_FF), BF16),
               compiler_params=_cp(("parallel", "parallel"), 24 << 20))(gu)


def swiglu_bwd(gu, dact, name):
    T = gu.shape[0]

    def body(x_ref, d_ref, o_ref):
        _, vjp = jax.vjp(_swiglu, x_ref[:, :FI_BLK].astype(F32), x_ref[:, FI_BLK:].astype(F32))
        dg, du = vjp(d_ref[...].astype(F32))
        o_ref[:, :FI_BLK] = dg.astype(o_ref.dtype)
        o_ref[:, FI_BLK:] = du.astype(o_ref.dtype)

    return _pc(body, name=name, grid=(T // TR, 2),
               in_specs=[pl.BlockSpec((TR, 2 * FI_BLK), lambda i, j: (i, j)), pl.BlockSpec((TR, FI_BLK), lambda i, j: (i, j))],
               out_specs=pl.BlockSpec((TR, 2 * FI_BLK), lambda i, j: (i, j)), out_shape=_sds((T, 2 * D_FF), BF16),
               compiler_params=_cp(("parallel", "parallel"), 32 << 20))(gu, dact)


def rope_tables(L, Lc):
    t = np.arange(L)
    rows, cols = t // GRID_W, t % GRID_W
    inv = ROPE_BASE ** (-np.arange(16, dtype=np.float32) / 16)
    lane = np.arange(64)
    pos = np.where((lane // 32)[None, :] == 0, rows[:, None], cols[:, None]).astype(np.float32)
    ang = jnp.asarray(pos) * jnp.asarray(inv[lane % 16])[None, :]
    cos = jnp.concatenate([jnp.cos(ang), jnp.ones((Lc, 64), F32)], axis=0)
    sin = jnp.concatenate([jnp.sin(ang), jnp.zeros((Lc, 64), F32)], axis=0)
    R = np.zeros((128, 128), np.float32)
    for i in range(128):
        if (i % 32) < 16:
            R[i + 16, i] = -1.0
        else:
            R[i - 16, i] = 1.0
    return jnp.tile(cos, (1, 2)), jnp.tile(sin, (1, 2)), jnp.asarray(R)


def rope_apply(q_src, q_col, k_src, k_col, cos, sin, R, transpose, name, kv_src=None):
    T = cos.shape[0]
    with_kv = kv_src is not None

    def rot(x, c, s, Rm):
        if transpose:
            return x * c + hdot(x * s, Rm, "nt")
        return x * c + hdot(x, Rm) * s

    def body(q_ref, k_ref, c_ref, s_ref, R_ref, *rest):
        qo_ref, ko_ref = rest[-4:-2] if with_kv else rest
        c, s, Rm = c_ref[...], s_ref[...], R_ref[...]
        for j in range(2):
            qo_ref[:, j * 128:(j + 1) * 128] = rot(q_ref[:, j * 128:(j + 1) * 128].astype(F32), c, s, Rm).astype(qo_ref.dtype)
        ko_ref[...] = rot(k_ref[...].astype(F32), c, s, Rm).astype(ko_ref.dtype)
        if with_kv:
            rest[-2][...] = rest[0][...].astype(BF16)
            rest[-1][...] = rest[1][...].astype(BF16)

    tab = pl.BlockSpec((TR, 128), lambda i: (i, 0))
    wide = pl.BlockSpec((TR, 256), lambda i: (i, 0))
    kv_in = [pl.BlockSpec((TR, 256), lambda i: (i, C_KB // 256)), pl.BlockSpec((TR, 256), lambda i: (i, C_VB // 256))] if with_kv else []
    return _pc(body, name=name, grid=(T // TR,),
               in_specs=[pl.BlockSpec((TR, 256), lambda i: (i, q_col)), pl.BlockSpec((TR, 128), lambda i: (i, k_col)),
                         tab, tab, pl.BlockSpec((128, 128), lambda i: (0, 0))] + kv_in,
               out_specs=[wide, tab] + ([wide, wide] if with_kv else []),
               out_shape=[_sds((T, 256), BF16), _sds((T, 128), BF16)] + ([_sds((T, 256), BF16)] * 2 if with_kv else []),
               compiler_params=_cp(("parallel",), 16 << 20))(q_src, k_src, cos, sin, R, *([kv_src, kv_src] if with_kv else []))


_SCALE = HD ** -0.5


def _attn_tile(qh, ks, vs, extra):
    ss = []
    for k, add in ks:
        s = _dg(qh, k, _DIMS["nt"]) * _SCALE
        ss.append(s if add is None else s + add)
    m = ss[0].max(axis=-1, keepdims=True)
    for s in ss[1:]:
        m = jnp.maximum(m, s.max(axis=-1, keepdims=True))
    if extra is not None:
        m = jnp.maximum(m, extra)
    ps = [jnp.exp(s - m) for s in ss]
    den = ps[0].sum(axis=-1, keepdims=True)
    for p in ps[1:]:
        den = den + p.sum(axis=-1, keepdims=True)
    if extra is not None:
        den = den + jnp.exp(extra - m)
    num = _dg(ps[0], vs[0], _DIMS["nn"])
    for p, v in zip(ps[1:], vs[1:]):
        num = num + _dg(p, v, _DIMS["nn"])
    linv = 1.0 / den
    return num * linv, m, linv


def _attn_bwd_tile(qh, ks, vs, extra, m, linv, oh, doh):
    delta = jnp.sum(doh * oh, axis=-1, keepdims=True)
    dq = None
    dks, dvs, dss = [], [], []
    for (k, add), v in zip(ks, vs):
        s = _dg(qh, k, _DIMS["nt"]) * _SCALE
        if add is not None:
            s = s + add
        p = jnp.exp(s - m) * linv
        dvs.append(_dg(p, doh, _DIMS["tn"]))
        ds = p * (_dg(doh, v, _DIMS["nt"]) - delta)
        dss.append(ds)
        dsq = ds * _SCALE
        part = _dg(dsq, k, _DIMS["nn"])
        dq = part if dq is None else dq + part
        dks.append(_dg(dsq, qh, _DIMS["tn"]))
    dextra = None
    if extra is not None:
        dextra = -jnp.sum(jnp.exp(extra - m) * linv * delta, axis=0, keepdims=True)
    return dq, dks, dvs, dss, dextra


def _wa_mask(n, L):
    qpos = n * WA_BLK + lax.broadcasted_iota(jnp.int32, (WA_BLK, 3 * WA_BLK), 0)
    kpos = (n - 1) * WA_BLK + lax.broadcasted_iota(jnp.int32, (WA_BLK, 3 * WA_BLK), 1)
    ok = (jnp.abs(qpos - kpos) <= WA_BLK) & (kpos >= 0) & (kpos < L)
    return jnp.where(ok, 0.0, NEG).astype(F32)


WA_BPS = 2


def _wa_specs(L, Lc):
    nb = L // WA_BLK
    cb = L // Lc

    def blk(j, col):
        return pl.BlockSpec((WA_BLK, 128), lambda s: (jnp.clip(s * WA_BPS - 1 + j, 0, nb - 1), col))

    vcol = C_VA // 128
    kspecs = [blk(j, 0) for j in range(WA_BPS + 2)] + [pl.BlockSpec((Lc, 128), lambda s: (cb, 0))]
    vspecs = [blk(j, vcol) for j in range(WA_BPS + 2)] + [pl.BlockSpec((Lc, 128), lambda s: (cb, vcol))]
    return nb, kspecs, vspecs


def win_attn_fwd(qr, kr, P, sink, L, Lc, name):
    T = L + Lc
    nb, kspecs, vspecs = _wa_specs(L, Lc)
    nk = WA_BPS + 2
    QB = WA_BPS * WA_BLK
    nlat = nb // WA_BPS

    def body(q_ref, *refs):
        kbs, kx, vbs, vx, s_ref, o_ref, st_ref = refs[:nk], refs[nk], refs[nk + 1:2 * nk + 1], refs[2 * nk + 1], refs[-3], refs[-2], refs[-1]
        s = pl.program_id(0)

        def put(qs, h, res):
            o, m, linv = res
            o_ref[qs, h * HD:(h + 1) * HD] = o.astype(o_ref.dtype)
            st_ref[qs, h:h + 1] = m
            st_ref[qs, WA_HEADS + h:WA_HEADS + h + 1] = linv

        @pl.when(s < nlat)
        def _():
            for b in range(WA_BPS):
                mask = _wa_mask(s * WA_BPS + b, L)
                qs = slice(b * WA_BLK, (b + 1) * WA_BLK)
                for g in range(WA_KV):
                    sl = slice(g * HD, (g + 1) * HD)
                    k3 = jnp.concatenate([kbs[b + j][:, sl] for j in range(3)], axis=0)
                    v3 = jnp.concatenate([vbs[b + j][:, sl] for j in range(3)], axis=0)
                    for r in range(2):
                        h = 2 * g + r
                        put(qs, h, _attn_tile(q_ref[qs, h * HD:(h + 1) * HD], [(k3, mask), (kx[:, sl], None)], [v3, vx[:, sl]], s_ref[h:h + 1, 0:1]))

        @pl.when(s >= nlat)
        def _():
            for h in range(WA_HEADS):
                sl = slice((h // 2) * HD, (h // 2 + 1) * HD)
                put(slice(None), h, _attn_tile(q_ref[:, h * HD:(h + 1) * HD], [(kx[:, sl], None)], [vx[:, sl]], s_ref[h:h + 1, 0:1]))

    qspec = pl.BlockSpec((QB, 256), lambda s: (s, 0))
    return _pc(body, name=name, grid=(T // QB,),
               in_specs=[qspec] + kspecs + vspecs + [pl.BlockSpec((8, 128), lambda s: (0, 0))],
               out_specs=[qspec, pl.BlockSpec((QB, 8), lambda s: (s, 0))], out_shape=[_sds((T, 256), BF16), _sds((T, 8), F32)],
               compiler_params=_cp(("arbitrary",), 32 << 20))(qr, *([kr] * (nk + 1)), *([P] * (nk + 1)), sink)


def win_attn_bwd(qr, kr, P, sink, do_src, o, stats, L, Lc, name):
    T = L + Lc
    nb, kspecs, vspecs = _wa_specs(L, Lc)
    nk = WA_BPS + 2
    QB = WA_BPS * WA_BLK
    nlat = nb // WA_BPS
    cx = WA_BLK + L

    def body(q_ref, *refs):
        kbs, kx, vbs, vx = refs[:nk], refs[nk], refs[nk + 1:2 * nk + 1], refs[2 * nk + 1]
        s_ref, do_ref, o_ref, st_ref, dq_ref, dk_ref, dv_ref, ds_ref = refs[2 * nk + 2:]
        s = pl.program_id(0)

        @pl.when(s == 0)
        def _():
            dk_ref[...] = jnp.zeros_like(dk_ref)
            dv_ref[...] = jnp.zeros_like(dv_ref)
            ds_ref[...] = jnp.zeros_like(ds_ref)

        def tile(qs, h, ks, vs):
            hs = slice(h * HD, (h + 1) * HD)
            dq, dks, dvs, _, dsk = _attn_bwd_tile(q_ref[qs, hs], ks, vs, s_ref[h:h + 1, 0:1], st_ref[qs, h:h + 1],
                                                  st_ref[qs, WA_HEADS + h:WA_HEADS + h + 1], o_ref[qs, hs].astype(F32), do_ref[qs, hs].astype(F32))
            dq_ref[qs, hs] = dq
            ds_ref[h:h + 1, :] += jnp.broadcast_to(dsk, (1, 128))
            return dks, dvs

        @pl.when(s < nlat)
        def _():
            for b in range(WA_BPS):
                n = s * WA_BPS + b
                mask = _wa_mask(n, L)
                rows = pl.ds(pl.multiple_of(n * WA_BLK, WA_BLK), 3 * WA_BLK)
                qs = slice(b * WA_BLK, (b + 1) * WA_BLK)
                for g in range(WA_KV):
                    sl = slice(g * HD, (g + 1) * HD)
                    k3 = jnp.concatenate([kbs[b + j][:, sl] for j in range(3)], axis=0)
                    v3 = jnp.concatenate([vbs[b + j][:, sl] for j in range(3)], axis=0)
                    acc = None
                    for r in range(2):
                        dks, dvs = tile(qs, 2 * g + r, [(k3, mask), (kx[:, sl], None)], [v3, vx[:, sl]])
                        acc = dks + dvs if acc is None else [a + b_ for a, b_ in zip(acc, dks + dvs)]
                    dk_ref[rows, sl] += acc[0]
                    dk_ref[cx:cx + Lc, sl] += acc[1]
                    dv_ref[rows, sl] += acc[2]
                    dv_ref[cx:cx + Lc, sl] += acc[3]

        @pl.when(s >= nlat)
        def _():
            for h in range(WA_HEADS):
                sl = slice((h // 2) * HD, (h // 2 + 1) * HD)
                dks, dvs = tile(slice(None), h, [(kx[:, sl], None)], [vx[:, sl]])
                dk_ref[cx:cx + Lc, sl] += dks[0]
                dv_ref[cx:cx + Lc, sl] += dvs[0]

    qspec = pl.BlockSpec((QB, 256), lambda s: (s, 0))
    acc_spec = pl.BlockSpec((T + 2 * WA_BLK, 128), lambda s: (0, 0))
    return _pc(body, name=name, grid=(T // QB,),
               in_specs=[qspec] + kspecs + vspecs + [pl.BlockSpec((8, 128), lambda s: (0, 0)), qspec, qspec, pl.BlockSpec((QB, 8), lambda s: (s, 0))],
               out_specs=[qspec, acc_spec, acc_spec, pl.BlockSpec((8, 128), lambda s: (0, 0))],
               out_shape=[_sds((T, 256), F32), _sds((T + 2 * WA_BLK, 128), F32), _sds((T + 2 * WA_BLK, 128), F32), _sds((8, 128), F32)],
               compiler_params=_cp(("arbitrary",), 40 << 20))(qr, *([kr] * (nk + 1)), *([P] * (nk + 1)), sink, do_src, o, stats)


def na_index_tables():
    qc = np.arange(GRID_W)[:, None]
    kc = np.arange(GRID_W)[None, :]
    cstart = np.clip(qc - NA_KW // 2, 0, GRID_W - NA_KW)
    ok = (kc >= cstart) & (kc < cstart + NA_KW)
    dx = np.clip(kc - qc, -(NA_KW - 1), NA_KW - 1) + (NA_KW - 1)
    off = np.arange(NA_KH)[:, None]
    kr = np.arange(NA_KH)[None, :]
    dy = kr - off + (NA_KH - 1)
    return ok, dx, dy


def _na_selectors():
    ok, dx, dy = na_index_tables()
    e1 = np.zeros((GRID_W * GRID_W, 128), np.float32)
    qi, ki = np.nonzero(ok)
    e1[qi * GRID_W + ki, dx[qi, ki]] = 1.0
    e2 = np.zeros((16, NA_KH * NA_KH), np.float32)
    oi, ri = np.meshgrid(np.arange(NA_KH), np.arange(NA_KH), indexing="ij")
    e2[dy[oi, ri].ravel(), (oi * NA_KH + ri).ravel()] = 1.0
    return ok, jnp.asarray(e1), jnp.asarray(np.kron(np.eye(NA_HEADS, dtype=np.float32), e2))


def na_bias_table(rpb, tag):
    ok, e1, e2 = _na_selectors()
    r2 = jnp.pad(rpb.astype(F32), ((0, 0), (0, 1), (0, 128 - (2 * NA_KW - 1)))).reshape(NA_HEADS * 16, 128)
    r1 = matmul(e2, r2, "tn", F32, f"na_bias_sel1_{tag}", hi=True)
    x = matmul(r1, e1, "nt", F32, f"na_bias_sel2_{tag}", hi=True)
    b = x.reshape(NA_HEADS, NA_KH, NA_KH, GRID_W, GRID_W).transpose(0, 1, 3, 2, 4)
    b = b + jnp.asarray(np.where(ok, 0.0, NEG).astype(np.float32))[None, None, :, None, :]
    return b.reshape(NA_HEADS, NA_KH, GRID_W, NA_KH * GRID_W)


def _na_rows(r, GR):
    r0 = jnp.clip(r - NA_KH // 2, 0, GR - NA_KH)
    return r0, jnp.clip(r - r0, 0, NA_KH - 1)


NA_RPS = 4


def na_fwd(P, kb, vb, bias, L, Lc, name):
    T = L + Lc
    GR = L // GRID_W
    W = NA_KH * GRID_W
    QB = GRID_W * NA_RPS
    nlat = GR // NA_RPS

    def body(q_ref, k_ref, v_ref, b_ref, o_ref, st_ref):
        s = pl.program_id(0)

        def put(qs, h, res):
            o, m, linv = res
            o_ref[qs, h * HD:(h + 1) * HD] = o.astype(o_ref.dtype)
            st_ref[qs, h:h + 1] = m
            st_ref[qs, NA_HEADS + h:NA_HEADS + h + 1] = linv

        @pl.when(s < nlat)
        def _():
            for rr in range(NA_RPS):
                r0, off = _na_rows(s * NA_RPS + rr, GR)
                rows = pl.ds(pl.multiple_of(r0 * GRID_W, GRID_W), W)
                qs = slice(rr * GRID_W, (rr + 1) * GRID_W)
                for h in range(NA_HEADS):
                    hs = slice(h * HD, (h + 1) * HD)
                    put(qs, h, _attn_tile(q_ref[qs, hs], [(k_ref[rows, hs], b_ref[h, off]), (k_ref[L:T, hs], None)],
                                          [v_ref[rows, hs], v_ref[L:T, hs]], None))

        @pl.when(s >= nlat)
        def _():
            for h in range(NA_HEADS):
                hs = slice(h * HD, (h + 1) * HD)
                put(slice(None), h, _attn_tile(q_ref[:, hs], [(k_ref[L:T, hs], None)], [v_ref[L:T, hs]], None))

    one = pl.Buffered(1)
    return _pc(body, name=name, grid=(T // QB,),
               in_specs=[pl.BlockSpec((QB, 256), lambda r: (r, C_QB // 256)),
                         pl.BlockSpec((T, 256), lambda r: (0, 0), pipeline_mode=one),
                         pl.BlockSpec((T, 256), lambda r: (0, 0), pipeline_mode=one),
                         pl.BlockSpec((NA_HEADS, NA_KH, GRID_W, W), lambda r: (0, 0, 0, 0), pipeline_mode=one)],
               out_specs=[pl.BlockSpec((QB, 256), lambda r: (r, 0)), pl.BlockSpec((QB, 8), lambda r: (r, 0))],
               out_shape=[_sds((T, 256), BF16), _sds((T, 8), F32)],
               compiler_params=_cp(("arbitrary",), 32 << 20))(P, kb, vb, bias)


def na_bwd(P, kb, vb, bias, do_src, o, stats, L, Lc, name):
    T = L + Lc
    GR = L // GRID_W
    W = NA_KH * GRID_W
    QB = GRID_W * NA_RPS
    nlat = GR // NA_RPS

    def body(q_ref, k_ref, v_ref, b_ref, do_ref, o_ref, st_ref, dq_ref, dk_ref, dv_ref, db_ref):
        s = pl.program_id(0)

        @pl.when(s == 0)
        def _():
            dk_ref[...] = jnp.zeros_like(dk_ref)
            dv_ref[...] = jnp.zeros_like(dv_ref)
            db_ref[...] = jnp.zeros_like(db_ref)

        def tile(qs, h, ks, vs):
            hs = slice(h * HD, (h + 1) * HD)
            dq, dks, dvs, dss, _ = _attn_bwd_tile(q_ref[qs, hs], ks, vs, None, st_ref[qs, h:h + 1], st_ref[qs, NA_HEADS + h:NA_HEADS + h + 1],
                                                  o_ref[qs, hs].astype(F32), do_ref[qs, hs].astype(F32))
            dq_ref[qs, hs] = dq.astype(dq_ref.dtype)
            return dks, dvs, dss

        @pl.when(s < nlat)
        def _():
            for rr in range(NA_RPS):
                r0, off = _na_rows(s * NA_RPS + rr, GR)
                rows = pl.ds(pl.multiple_of(r0 * GRID_W, GRID_W), W)
                qs = slice(rr * GRID_W, (rr + 1) * GRID_W)
                for h in range(NA_HEADS):
                    hs = slice(h * HD, (h + 1) * HD)
                    dks, dvs, dss = tile(qs, h, [(k_ref[rows, hs], b_ref[h, off]), (k_ref[L:T, hs], None)], [v_ref[rows, hs], v_ref[L:T, hs]])
                    dk_ref[rows, hs] += dks[0]
                    dv_ref[rows, hs] += dvs[0]
                    dk_ref[L:T, hs] += dks[1]
                    dv_ref[L:T, hs] += dvs[1]
                    db_ref[h, off] += dss[0]

        @pl.when(s >= nlat)
        def _():
            for h in range(NA_HEADS):
                hs = slice(h * HD, (h + 1) * HD)
                dks, dvs, _ = tile(slice(None), h, [(k_ref[L:T, hs], None)], [v_ref[L:T, hs]])
                dk_ref[L:T, hs] += dks[0]
                dv_ref[L:T, hs] += dvs[0]

    one = pl.Buffered(1)
    full = lambda shape: pl.BlockSpec(shape, lambda r: (0,) * len(shape), pipeline_mode=one)
    qspec = pl.BlockSpec((QB, 256), lambda r: (r, 0))
    return _pc(body, name=name, grid=(T // QB,),
               in_specs=[pl.BlockSpec((QB, 256), lambda r: (r, C_QB // 256)), full((T, 256)), full((T, 256)),
                         full((NA_HEADS, NA_KH, GRID_W, W)), pl.BlockSpec((QB, 256), lambda r: (r, 1)), qspec, pl.BlockSpec((QB, 8), lambda r: (r, 0))],
               out_specs=[qspec, full((T, 256)), full((T, 256)), full((NA_HEADS, NA_KH, GRID_W, W))],
               out_shape=[_sds((T, 256), BF16), _sds((T, 256), F32), _sds((T, 256), F32), _sds((NA_HEADS, NA_KH, GRID_W, W), F32)],
               compiler_params=_cp(("arbitrary",), 48 << 20))(P, kb, vb, bias, do_src, o, stats)


def na_rpb_grad(dbias, tag):
    _, e1, e2 = _na_selectors()
    x = dbias.reshape(NA_HEADS, NA_KH, GRID_W, NA_KH, GRID_W).transpose(0, 1, 3, 2, 4).reshape(NA_HEADS * NA_KH * NA_KH, GRID_W * GRID_W)
    r1 = matmul(x, e1, "nn", F32, f"na_rpb_sel1_{tag}", hi=True, tk=1024)
    r2 = matmul(e2, r1, "nn", F32, f"na_rpb_sel2_{tag}", hi=True)
    return r2.reshape(NA_HEADS, 16, 128)[:, :2 * NA_KH - 1, :2 * NA_KW - 1]


_HALO = 8


def _halo_specs(T, col0):
    nh = TR // _HALO
    cur = pl.BlockSpec((TR, 256), lambda i, j: (i, col0 + j))
    prv = pl.BlockSpec((_HALO, 256), lambda i, j: (jnp.maximum(i * nh - 1, 0), col0 + j))
    nxt = pl.BlockSpec((_HALO, 256), lambda i, j: (jnp.minimum((i + 1) * nh, T // _HALO - 1), col0 + j))
    return prv, cur, nxt


def _fill_ext(ext, prv, cur, nxt, i, nL, nT):
    has_prev = jnp.where((i != 0) & (i != nL), 1.0, 0.0)
    has_next = jnp.where((i != nL - 1) & (i != nT - 1), 1.0, 0.0)
    ext[0:_HALO, :] = prv[...].astype(F32) * has_prev
    ext[_HALO:_HALO + TR, :] = cur[...].astype(F32)
    ext[_HALO + TR:, :] = nxt[...].astype(F32) * has_next


def conv_silu_fwd(P, w8, b, nL, name):
    T = P.shape[0]
    nT = T // TR

    def body(prv, cur, nxt, w_ref, b_ref, pre_ref, act_ref, ext):
        i = pl.program_id(0)
        _fill_ext(ext, prv, cur, nxt, i, nL, nT)
        y = jnp.broadcast_to(b_ref[...], (TR, 256))
        for k in range(S_CONV):
            y = y + w_ref[k:k + 1, :] * ext[pl.ds(_HALO - S_CONV // 2 + k, TR), :]
        pre_ref[...] = y
        act_ref[...] = _silu(y)

    prv, cur, nxt = _halo_specs(T, C_XBC // 256)
    out = pl.BlockSpec((TR, 256), lambda i, j: (i, j))
    return _pc(body, name=name, grid=(nT, 4),
               in_specs=[prv, cur, nxt, pl.BlockSpec((8, 256), lambda i, j: (0, j)), pl.BlockSpec((1, 256), lambda i, j: (0, j))],
               out_specs=[out, out], out_shape=[_sds((T, 1024), F32), _sds((T, 1024), F32)],
               scratch_shapes=[pltpu.VMEM((TR + 2 * _HALO, 256), F32)],
               compiler_params=_cp(("parallel", "parallel"), 16 << 20))(P, P, P, w8, b)


def dsilu(pre, dxs_list, db_list, dc_list, name):
    T = pre.shape[0]
    n1, n2, n3 = len(dxs_list), len(db_list), len(dc_list)

    def body(*refs):
        pre_ref = refs[0]
        ins = refs[1:1 + n1 + n2 + n3]
        out = refs[-1]

        def part(rs, lo, hi):
            g = rs[0][...].astype(F32)
            for r in rs[1:]:
                g = g + r[...].astype(F32)
            _, vjp = jax.vjp(_silu, pre_ref[:, lo:hi])
            out[:, lo:hi] = vjp(g)[0]

        part(ins[:n1], 0, 512)
        part(ins[n1:n1 + n2], 512, 768)
        part(ins[n1 + n2:], 768, 1024)

    spec = lambda w: pl.BlockSpec((TR, w), lambda i: (i, 0))
    return _pc(body, name=name, grid=(T // TR,),
               in_specs=[spec(1024)] + [spec(512)] * n1 + [spec(256)] * (n2 + n3),
               out_specs=spec(1024), out_shape=_sds((T, 1024), F32),
               compiler_params=_cp(("parallel",), 32 << 20))(pre, *dxs_list, *db_list, *dc_list)


def conv_bwd(dpre, P, w8, nL, name):
    T = P.shape[0]
    nT = T // TR

    def body(dp, dc, dn, xp, xc, xn, w_ref, dx_ref, dw_ref, db_ref, extd, extx):
        i = pl.program_id(1)
        _fill_ext(extd, dp, dc, dn, i, nL, nT)
        _fill_ext(extx, xp, xc, xn, i, nL, nT)

        @pl.when(i == 0)
        def _():
            dw_ref[...] = jnp.zeros_like(dw_ref)
            db_ref[...] = jnp.zeros_like(db_ref)

        d = dc[...]
        dx = jnp.zeros((TR, 256), F32)
        for k in range(S_CONV):
            dx = dx + w_ref[k:k + 1, :] * extd[pl.ds(_HALO + S_CONV // 2 - k, TR), :]
            dw_ref[k:k + 1, :] += jnp.sum(d * extx[pl.ds(_HALO - S_CONV // 2 + k, TR), :], axis=0, keepdims=True)
        dx_ref[...] = dx.astype(dx_ref.dtype)
        db_ref[0:1, :] += jnp.sum(d, axis=0, keepdims=True)

    def swap(spec):
        f = spec.index_map
        return pl.BlockSpec(spec.block_shape, lambda j, i: f(i, j))

    dprv, dcur, dnxt = [swap(s) for s in _halo_specs(T, 0)]
    xprv, xcur, xnxt = [swap(s) for s in _halo_specs(T, C_XBC // 256)]
    acc = pl.BlockSpec((8, 256), lambda j, i: (0, j))
    return _pc(body, name=name, grid=(4, nT),
               in_specs=[dprv, dcur, dnxt, xprv, xcur, xnxt, acc],
               out_specs=[pl.BlockSpec((TR, 256), lambda j, i: (i, j)), acc, acc],
               out_shape=[_sds((T, 1024), BF16), _sds((8, 1024), F32), _sds((8, 1024), F32)],
               scratch_shapes=[pltpu.VMEM((TR + 2 * _HALO, 256), F32), pltpu.VMEM((TR + 2 * _HALO, 256), F32)],
               compiler_params=_cp(("parallel", "arbitrary"), 16 << 20))(dpre, dpre, dpre, P, P, P, w8)


def _onehot_row(h, n):
    return (lax.broadcasted_iota(jnp.int32, (1, n), 1) == h).astype(F32)


def _onehot_col(h, n):
    return (lax.broadcasted_iota(jnp.int32, (n, 1), 0) == h).astype(F32)


def _ssd_chunk(xs, dtr, dtb, alog, bm, cm, hin, reverse):
    Qn = S_Q
    ii = lax.broadcasted_iota(jnp.int32, (Qn, Qn), 0)
    jj = lax.broadcasted_iota(jnp.int32, (Qn, Qn), 1)
    keep = (ii <= jj) if reverse else (ii >= jj)
    tri = keep.astype(F32)
    triT = ((jj <= ii) if reverse else (jj >= ii)).astype(F32)
    eye = (ii == jj).astype(F32)
    dt = _softplus(dtr + dtb)
    a = dt * (-jnp.exp(alog))
    cs = hdot(tri, a)
    csT = hdot(a, triT, "tn")
    dtT = hdot(dt, eye, "tn")
    last = _onehot_row(0 if reverse else Qn - 1, Qn)
    ys, houts = [], []
    for g in range(S_GROUPS):
        G = bdot(cm[g], bm[g], "nt")
        for r in range(S_HEADS // S_GROUPS):
            h = g * (S_HEADS // S_GROUPS) + r
            eh_r, eh_c = _onehot_row(h, S_HEADS), _onehot_col(h, S_HEADS)
            cs_c = jnp.sum(cs * eh_r, axis=1, keepdims=True)
            dt_c = jnp.sum(dt * eh_r, axis=1, keepdims=True)
            cs_r = jnp.sum(csT * eh_c, axis=0, keepdims=True)
            dt_r = jnp.sum(dtT * eh_c, axis=0, keepdims=True)
            tot = jnp.sum(cs_r * last, axis=1, keepdims=True)
            decay = jnp.exp(jnp.where(keep, cs_c - cs_r, NEG))
            w = G * decay * dt_r
            y = bdot(w, xs[h], "nn") + bdot(cm[g], hin[h], "nt") * jnp.exp(cs_c)
            xsc = xs[h] * (jnp.exp(tot - cs_c) * dt_c)
            hout = hin[h] * jnp.exp(tot) + bdot(xsc, bm[g], "tn")
            ys.append(y)
            houts.append(hout)
    return ys, houts


def _ssd_orders(L, Lc):
    nl, ncx = L // S_Q, Lc // S_Q
    fwd = lambda s: jnp.where(s < ncx, nl + s, s - ncx)
    bwd = lambda s: nl + ncx - 1 - s
    return nl + ncx, fwd, bwd


def _ssd_in_specs(fo, bo, step):
    def at(order, w, col):
        return pl.BlockSpec((S_Q, w), lambda u: (order(step(u)), col))
    specs = []
    for order in (fo, bo):
        specs += [at(order, 512, 0), at(order, 256, 2), at(order, 256, 3), at(order, 128, C_DT // 128)]
    return specs


def ssd_fwd(act, P, dtb, alog, L, Lc, name):
    T = L + Lc
    ns, fo, bo = _ssd_orders(L, Lc)

    def body(xf, bf, cf, df, xb, bb, cb, db, dtb_ref, al_ref, yf, yb, hsf, hsb, Hf, Hb):
        s = pl.program_id(0)

        @pl.when(s == 0)
        def _():
            Hf[...] = jnp.zeros_like(Hf)
            Hb[...] = jnp.zeros_like(Hb)

        for d, (x_r, b_r, c_r, dt_r, y_r, hs_r, H) in enumerate(((xf, bf, cf, df, yf, hsf, Hf), (xb, bb, cb, db, yb, hsb, Hb))):
            hin = [H[h] for h in range(S_HEADS)]
            hs_r[0] = H[...]
            ys, houts = _ssd_chunk(
                [x_r[:, h * S_P:(h + 1) * S_P] for h in range(S_HEADS)], dt_r[:, d * 8:(d + 1) * 8],
                dtb_ref[d:d + 1, 0:8], al_ref[d:d + 1, 0:8],
                [b_r[:, g * S_N:(g + 1) * S_N] for g in range(S_GROUPS)], [c_r[:, g * S_N:(g + 1) * S_N] for g in range(S_GROUPS)],
                hin, reverse=(d == 1))
            for h in range(S_HEADS):
                y_r[:, h * S_P:(h + 1) * S_P] = ys[h]
                H[h] = houts[h]

    ident = lambda u: u
    small = pl.BlockSpec((8, 128), lambda u: (0, 0))
    hspec = pl.BlockSpec((1, S_HEADS, S_P, S_N), lambda u: (u, 0, 0, 0))
    return _pc(body, name=name, grid=(ns,),
               in_specs=_ssd_in_specs(fo, bo, ident) + [small, small],
               out_specs=[pl.BlockSpec((S_Q, 512), lambda u: (fo(u), 0)), pl.BlockSpec((S_Q, 512), lambda u: (bo(u), 0)), hspec, hspec],
               out_shape=[_sds((T, 512), F32), _sds((T, 512), F32), _sds((ns, S_HEADS, S_P, S_N), F32), _sds((ns, S_HEADS, S_P, S_N), F32)],
               scratch_shapes=[pltpu.VMEM((S_HEADS, S_P, S_N), F32), pltpu.VMEM((S_HEADS, S_P, S_N), F32)],
               compiler_params=_cp(("arbitrary",), 32 << 20))(act, act, act, P, act, act, act, P, dtb, alog)


def ssd_bwd(act, P, dtb, alog, hsf, hsb, dy, L, Lc, name):
    T = L + Lc
    ns, fo, bo = _ssd_orders(L, Lc)
    step = lambda u: ns - 1 - u

    def body(xf, bf, cf, df, xb, bb, cb, db, dtb_ref, al_ref, hsf_r, hsb_r, dyf, dyb,
             dxf, dbf, dcf, ddf, dxb, dbb, dcb, ddb, ddtb, dal, dHf, dHb):
        u = pl.program_id(0)

        @pl.when(u == 0)
        def _():
            dHf[...] = jnp.zeros_like(dHf)
            dHb[...] = jnp.zeros_like(dHb)
            ddtb[...] = jnp.zeros_like(ddtb)
            dal[...] = jnp.zeros_like(dal)

        dirs = ((xf, bf, cf, df, hsf_r, dyf, dxf, dbf, dcf, ddf, dHf), (xb, bb, cb, db, hsb_r, dyb, dxb, dbb, dcb, ddb, dHb))
        for d, (x_r, b_r, c_r, dt_r, hs_r, dy_r, dx_o, db_o, dc_o, dd_o, dH) in enumerate(dirs):
            f = functools.partial(_ssd_chunk, reverse=(d == 1))
            _, vjp = jax.vjp(
                f, [x_r[:, h * S_P:(h + 1) * S_P] for h in range(S_HEADS)], dt_r[:, d * 8:(d + 1) * 8],
                dtb_ref[d:d + 1, 0:8], al_ref[d:d + 1, 0:8],
                [b_r[:, g * S_N:(g + 1) * S_N] for g in range(S_GROUPS)], [c_r[:, g * S_N:(g + 1) * S_N] for g in range(S_GROUPS)],
                [hs_r[0, h] for h in range(S_HEADS)])
            gx, gdt, gdtb, gal, gb, gc, gh = vjp(([dy_r[:, h * S_P:(h + 1) * S_P] for h in range(S_HEADS)],
                                                  [dH[h] for h in range(S_HEADS)]))
            for h in range(S_HEADS):
                dx_o[:, h * S_P:(h + 1) * S_P] = gx[h]
                dH[h] = gh[h]
            for g in range(S_GROUPS):
                db_o[:, g * S_N:(g + 1) * S_N] = gb[g]
                dc_o[:, g * S_N:(g + 1) * S_N] = gc[g]
            dd_o[...] = gdt
            ddtb[d:d + 1, 0:8] += gdtb
            dal[d:d + 1, 0:8] += gal

    small = pl.BlockSpec((8, 128), lambda u: (0, 0))
    hspec = pl.BlockSpec((1, S_HEADS, S_P, S_N), lambda u: (step(u), 0, 0, 0))
    at = lambda order, w: pl.BlockSpec((S_Q, w), lambda u: (order(step(u)), 0))
    outs = []
    for order in (fo, bo):
        outs += [at(order, 512), at(order, 256), at(order, 256), at(order, 8)]
    oshape = [_sds((T, 512), F32), _sds((T, 256), F32), _sds((T, 256), F32), _sds((T, 8), F32)]
    return _pc(body, name=name, grid=(ns,),
               in_specs=_ssd_in_specs(fo, bo, step) + [small, small, hspec, hspec, at(fo, 512), at(bo, 512)],
               out_specs=outs + [small, small], out_shape=oshape + oshape + [_sds((8, 128), F32), _sds((8, 128), F32)],
               scratch_shapes=[pltpu.VMEM((S_HEADS, S_P, S_N), F32), pltpu.VMEM((S_HEADS, S_P, S_N), F32)],
               compiler_params=_cp(("arbitrary",), 40 << 20))(act, act, act, P, act, act, act, P, dtb, alog, hsf, hsb, dy, dy)


def _ssm_out(yf, yb, xs, z, dskip, g):
    y = (yf + yb + dskip * xs) * _silu(z)
    return (y * lax.rsqrt(jnp.mean(y * y, axis=-1, keepdims=True) + EPS)) * g


def ssm_out_fwd(yf, yb, act, P, dskip, g, name):
    T = yf.shape[0]

    def body(yf_r, yb_r, xs_r, z_r, d_r, g_r, o_r):
        o_r[...] = _ssm_out(yf_r[...], yb_r[...], xs_r[...], z_r[...], d_r[...], g_r[...]).astype(o_r.dtype)

    row = pl.BlockSpec((TR, 512), lambda i: (i, 0))
    vec = pl.BlockSpec((1, 512), lambda i: (0, 0))
    return _pc(body, name=name, grid=(T // TR,),
               in_specs=[row, row, row, pl.BlockSpec((TR, 512), lambda i: (i, C_Z // 512)), vec, vec],
               out_specs=row, out_shape=_sds((T, 512), BF16),
               compiler_params=_cp(("parallel",), 16 << 20))(yf, yb, act, P, dskip, g)


def ssm_out_bwd(yf, yb, act, P, dskip, g, do_src, name):
    T = yf.shape[0]

    def body(yf_r, yb_r, xs_r, z_r, d_r, g_r, do_r, dy_r, dxs_r, dz_r, dv_r):
        @pl.when(pl.program_id(0) == 0)
        def _():
            dv_r[...] = jnp.zeros_like(dv_r)

        _, vjp = jax.vjp(_ssm_out, yf_r[...], yb_r[...], xs_r[...], z_r[...], d_r[...], g_r[...])
        dyf, _, dxs, dz, dd, dg = vjp(do_r[...].astype(F32))
        dy_r[...] = dyf
        dxs_r[...] = dxs
        dz_r[...] = dz.astype(dz_r.dtype)
        dv_r[0:1, :] += dd
        dv_r[1:2, :] += dg

    row = pl.BlockSpec((TR, 512), lambda i: (i, 0))
    vec = pl.BlockSpec((1, 512), lambda i: (0, 0))
    return _pc(body, name=name, grid=(T // TR,),
               in_specs=[row, row, row, pl.BlockSpec((TR, 512), lambda i: (i, C_Z // 512)), vec, vec,
                         pl.BlockSpec((TR, 512), lambda i: (i, 1))],
               out_specs=[row, row, row, pl.BlockSpec((8, 512), lambda i: (0, 0))],
               out_shape=[_sds((T, 512), F32), _sds((T, 512), F32), _sds((T, 512), BF16), _sds((8, 512), F32)],
               compiler_params=_cp(("arbitrary",), 24 << 20))(yf, yb, act, P, dskip, g, do_src)


def add_halves(xv, got, cvec, name):
    n, r, cdim = xv.shape
    h = r // 2

    def body(c_ref, x_ref, g_ref, o_ref):
        o_ref[...] = (x_ref[...].astype(F32) + g_ref[...].astype(F32)).astype(o_ref.dtype)

    gs = pltpu.PrefetchScalarGridSpec(
        num_scalar_prefetch=1, grid=(n,),
        in_specs=[pl.BlockSpec((1, h, cdim), lambda k, c_ref: (k, c_ref[0], 0)), pl.BlockSpec((1, h, cdim), lambda k, c_ref: (k, 0, 0))],
        out_specs=pl.BlockSpec((1, h, cdim), lambda k, c_ref: (k, 0, 0)))
    return _pc(body, name=name, grid_spec=gs, out_shape=_sds((n, h, cdim), BF16),
               compiler_params=_cp(("arbitrary",), 24 << 20))(cvec, xv, got)


def sum_slots(a, name):
    n, r, cdim = a.shape
    tr = _div_tile(r, 512, 16)

    def body(a_ref, o_ref):
        acc = a_ref[0].astype(F32)
        for k in range(1, n):
            acc = acc + a_ref[k].astype(F32)
        o_ref[...] = acc

    return _pc(body, name=name, grid=(r // tr,), in_specs=[pl.BlockSpec((n, tr, cdim), lambda i: (0, i, 0))],
               out_specs=pl.BlockSpec((tr, cdim), lambda i: (i, 0)), out_shape=_sds((r, cdim), F32),
               compiler_params=_cp(("parallel",), 32 << 20))(a)


def adamw(w, g, m, v, name):
    B, R, C = w.shape
    tr = _div_tile(R, max(8, (1 << 19) // max(C, 1) // 8 * 8), 8) if R % 8 == 0 else R
    c1 = 1.0 / (1.0 - ADAM_B1 ** ADAM_STEP)
    c2 = 1.0 / (1.0 - ADAM_B2 ** ADAM_STEP)

    def body(w_ref, g_ref, m_ref, v_ref, d_ref, mo_ref, vo_ref):
        gg = g_ref[...]
        mn = ADAM_B1 * m_ref[...] + (1.0 - ADAM_B1) * gg
        vn = ADAM_B2 * v_ref[...] + (1.0 - ADAM_B2) * (gg * gg)
        d_ref[...] = -ADAM_LR * ((mn * c1) / (jnp.sqrt(vn * c2) + ADAM_EPS) + ADAM_WD * w_ref[...])
        mo_ref[...] = mn
        vo_ref[...] = vn

    spec = pl.BlockSpec((1, tr, C), lambda b, i: (b, i, 0))
    return _pc(body, name=name, grid=(B, R // tr), in_specs=[spec] * 4, out_specs=[spec] * 3,
               out_shape=[_sds((B, R, C), F32)] * 3, compiler_params=_cp(("parallel", "parallel"), 32 << 20))(w, g, m, v)


def _me():
    return lax.axis_index("x"), lax.axis_index("y"), lax.axis_index("c")


def _flip(v, bit):
    return 1 - v if bit else v


def allgather8(xv, name):
    R = xv.shape[0]

    def body(x_ref, out_ref, sum_ref, send_sems, recv_sems):
        mx, my, mc = _me()
        me = 4 * mx + 2 * my + mc
        out_ref[me] = x_ref[...]
        sends, recvs = [], []
        for k in range(1, 8):
            px, py, pc = _flip(mx, k & 4), _flip(my, k & 2), _flip(mc, k & 1)
            peer = 4 * px + 2 * py + pc
            sends.append(pltpu.make_async_remote_copy(src_ref=x_ref, dst_ref=out_ref.at[me], send_sem=send_sems.at[k - 1],
                                                      recv_sem=recv_sems.at[k - 1], device_id=(px, py, pc), device_id_type=MESH))
            recvs.append(pltpu.make_async_remote_copy(src_ref=x_ref, dst_ref=out_ref.at[peer], send_sem=send_sems.at[k - 1],
                                                      recv_sem=recv_sems.at[k - 1], device_id=(px, py, pc), device_id_type=MESH))
        for cp in sends:
            cp.start()
        for cp in recvs:
            cp.wait_recv()
        for cp in sends:
            cp.wait_send()
        acc = out_ref[0]
        for d in range(1, 8):
            acc = acc + out_ref[d]
        sum_ref[...] = acc

    vm = pl.BlockSpec(memory_space=pltpu.VMEM)
    return _pc(body, name=name, pin=False, in_specs=[vm], out_specs=[vm, vm], out_shape=[_sds((8, R, 128), F32), _sds((R, 128), F32)],
               scratch_shapes=[pltpu.SemaphoreType.DMA((7,)), pltpu.SemaphoreType.DMA((7,))],
               compiler_params=_cp(None, 32 << 20))(xv)


def _other_chips(mx, my):
    return [(1 - mx, my), (mx, 1 - my), (1 - mx, 1 - my)]


def _halves(r, mc, mult):
    h = r // 2
    return pl.ds(pl.multiple_of(mc * h, mult), h), pl.ds(pl.multiple_of((1 - mc) * h, mult), h)


def _rcopy(src, dst, send_sems, recv_sems, k, to):
    return pltpu.make_async_remote_copy(src_ref=src, dst_ref=dst, send_sem=send_sems.at[k], recv_sem=recv_sems.at[k],
                                        device_id=to, device_id_type=MESH)


def _gather_body(xs, outs, send_sems, recv_sems):
    n = len(xs)
    mx, my, mc = _me()
    chip = 2 * mx + my
    sib = (mx, my, 1 - mc)
    chips = _other_chips(mx, my)
    idx = [2 * cx + cy for cx, cy in chips]
    cp = functools.partial(_rcopy, send_sems=send_sems, recv_sems=recv_sems)
    hv = [_halves(x.shape[0], mc, 16) for x in xs]
    first, passed = [], []
    for a in range(n):
        for j, (cx, cy) in enumerate(chips):
            first.append(cp(xs[a].at[hv[a][0]], outs[a].at[chip, hv[a][0]], k=6 * a + j, to=(cx, cy, mc)))
            first[-1].start()
    for a in range(n):
        for j in range(3):
            cp(xs[a].at[hv[a][0]], outs[a].at[idx[j], hv[a][0]], k=6 * a + j, to=sib).wait_recv()
            passed.append(cp(outs[a].at[idx[j], hv[a][0]], outs[a].at[idx[j], hv[a][0]], k=6 * a + 3 + j, to=sib))
            passed[-1].start()
    for a in range(n):
        for j in range(3):
            cp(xs[a].at[hv[a][1]], outs[a].at[idx[j], hv[a][1]], k=6 * a + 3 + j, to=sib).wait_recv()
    for c_ in first + passed:
        c_.wait_send()


def _my_chip():
    return 2 * lax.axis_index("x") + lax.axis_index("y")


def _own_slots(outs, shards):
    return [lax.dynamic_update_index_in_dim(o, x, _my_chip(), 0) for o, x in zip(outs, shards)]


def gather_weights(shards, name):
    n = len(shards)

    def body(*refs):
        _gather_body(refs[:n], refs[n:2 * n], *refs[2 * n:])

    hbm = pl.BlockSpec(memory_space=pl.ANY)
    outs = _pc(body, name=name, in_specs=[hbm] * n, out_specs=[hbm] * n, out_shape=[_sds((4,) + x.shape, x.dtype) for x in shards],
               scratch_shapes=[pltpu.SemaphoreType.DMA((6 * n,)), pltpu.SemaphoreType.DMA((6 * n,))])(*shards)
    return _own_slots(outs, shards)


GATHER_REST_ID = 3


def gather_weights_sc(shards, name):
    n = len(shards)
    x_refs = [jax.new_ref(x, memory_space=pltpu.MemorySpace.HBM) for x in shards]
    out_refs = [jax.empty_ref(_sds((4,) + x.shape, x.dtype), memory_space=pltpu.MemorySpace.HBM) for x in shards]

    @pl.kernel(mesh=plsc.ScalarSubcoreMesh(axis_name="sc", num_cores=1), name=name,
               scratch_types=(pltpu.SemaphoreType.DMA((6 * n,)), pltpu.SemaphoreType.DMA((6 * n,))),
               compiler_params=pltpu.CompilerParams(collective_id=GATHER_REST_ID))
    def launch(send_sems, recv_sems):
        mx, my, mc = _me()
        barrier = pltpu.get_barrier_semaphore()
        for peer in [(mx, my, 1 - mc)] + [(cx, cy, mc) for cx, cy in _other_chips(mx, my)]:
            pl.semaphore_signal(barrier, inc=1, device_id=peer, device_id_type=MESH)
        pl.semaphore_wait(barrier, 4)
        _gather_body(x_refs, out_refs, send_sems, recv_sems)

    launch()
    return _own_slots([o[...] for o in out_refs], shards)


def swap_halves(arrs, name):
    n = len(arrs)

    def body(*refs):
        xs, outs = refs[:n], refs[n:2 * n]
        send_sems, recv_sems = refs[2 * n:]
        mx, my, mc = _me()
        cps = []
        for a in range(n):
            theirs = _halves(xs[a].shape[1], mc, 16)[1]
            cps.append(_rcopy(xs[a].at[pl.ds(0, 4), theirs], outs[a], send_sems, recv_sems, a, (mx, my, 1 - mc)))
            cps[-1].start()
        for c_ in cps:
            c_.wait()

    hbm = pl.BlockSpec(memory_space=pl.ANY)
    return _pc(body, name=name, in_specs=[hbm] * n, out_specs=[hbm] * n,
               out_shape=[_sds((4, x.shape[1] // 2, x.shape[2]), x.dtype) for x in arrs],
               scratch_shapes=[pltpu.SemaphoreType.DMA((n,)), pltpu.SemaphoreType.DMA((n,))])(*arrs)


SCATTER_ID = 4


def scatter_chips_sc(arrs, name):
    n = len(arrs)
    x_refs = [jax.new_ref(x, memory_space=pltpu.MemorySpace.HBM) for x in arrs]
    out_refs = [jax.empty_ref(_sds(x.shape, x.dtype), memory_space=pltpu.MemorySpace.HBM) for x in arrs]

    @pl.kernel(mesh=plsc.ScalarSubcoreMesh(axis_name="sc", num_cores=1), name=name,
               scratch_types=(pltpu.SemaphoreType.DMA((3 * n,)), pltpu.SemaphoreType.DMA((3 * n,))),
               compiler_params=pltpu.CompilerParams(collective_id=SCATTER_ID))
    def launch(send_sems, recv_sems):
        mx, my, mc = _me()
        chip = 2 * mx + my
        chips = _other_chips(mx, my)
        idx = [2 * cx + cy for cx, cy in chips]
        barrier = pltpu.get_barrier_semaphore()
        for cx, cy in chips:
            pl.semaphore_signal(barrier, inc=1, device_id=(cx, cy, mc), device_id_type=MESH)
        pl.semaphore_wait(barrier, 3)
        cp = functools.partial(_rcopy, send_sems=send_sems, recv_sems=recv_sems)
        sends = []
        for a in range(n):
            for j, (cx, cy) in enumerate(chips):
                sends.append(cp(x_refs[a].at[idx[j]], out_refs[a].at[chip], k=3 * a + j, to=(cx, cy, mc)))
                sends[-1].start()
        for a in range(n):
            for j, (cx, cy) in enumerate(chips):
                cp(x_refs[a].at[idx[j]], out_refs[a].at[idx[j]], k=3 * a + j, to=(cx, cy, mc)).wait_recv()
        for c_ in sends:
            c_.wait_send()

    launch()
    return _own_slots([o[...] for o in out_refs], [lax.dynamic_index_in_dim(x, _my_chip(), 0, keepdims=False) for x in arrs])


def share_halves(parts, name):
    flat = [p for w in parts for p in w]
    nw, n = len(parts), len(flat)
    depth = n // nw

    def body(*refs):
        xs, outs = refs[:n], refs[n:n + nw]
        send_sems, recv_sems = refs[n + nw:]
        mx, my, mc = _me()
        sib = (mx, my, 1 - mc)
        sends, recvs = [], []
        for a in range(n):
            w, l = a // depth, a % depth
            mine, theirs = _halves(outs[w].shape[1], mc, 8)
            sends.append(_rcopy(xs[a], outs[w].at[l, mine], send_sems, recv_sems, a, sib))
            recvs.append(_rcopy(xs[a], outs[w].at[l, theirs], send_sems, recv_sems, a, sib))
            sends[-1].start()
        for c_ in recvs:
            c_.wait_recv()
        for c_ in sends:
            c_.wait_send()

    hbm = pl.BlockSpec(memory_space=pl.ANY)
    outs = _pc(body, name=name, in_specs=[hbm] * n, out_specs=[hbm] * nw,
               out_shape=[_sds((depth, 2 * w[0].shape[0], w[0].shape[1]), F32) for w in parts],
               scratch_shapes=[pltpu.SemaphoreType.DMA((n,)), pltpu.SemaphoreType.DMA((n,))])(*flat)
    outs = list(outs)
    mc = lax.axis_index("c")
    for w in range(nw):
        for l in range(depth):
            h = parts[w][l].shape[0]
            outs[w] = lax.dynamic_update_slice(outs[w], parts[w][l][None], (l, mc * h, 0))
    return outs


_BIG = ("w_in", "w_out", "w_ffn_in", "w_ffn_out")
N_CHIPS = 4
DEPTH = 2


def _pad_rows(v, mult=8):
    n = v.shape[0]
    rows = -(-n // 128)
    rows = -(-rows // mult) * mult
    return jnp.pad(v, (0, rows * 128 - n)).reshape(rows, 128)


class _Flat:
    def __init__(self):
        self.items = []

    def add(self, name, a):
        self.items.append((name, a.shape, a.reshape(-1).astype(F32)))

    def rows(self):
        return _pad_rows(jnp.concatenate([a for _, _, a in self.items]))

    def split(self, rows):
        flat = rows.reshape(-1)
        out, o = {}, 0
        for name, shape, a in self.items:
            out[name] = flat[o:o + a.shape[0]].reshape(shape)
            o += a.shape[0]
        return out

    def split_lead(self, rows3):
        n = rows3.shape[0]
        flat = rows3.reshape(n, -1)
        out, o = {}, 0
        for name, shape, a in self.items:
            out[name] = flat[:, o:o + a.shape[0]].reshape((n,) + tuple(shape))
            o += a.shape[0]
        return out


def _gsv(rows):
    z = jnp.zeros((2, D), F32)
    r = [z if a is None else a for a in rows] + [z] * 5
    return jnp.stack(r, axis=1)


def _pad8(a, rows=8, cols=128):
    return jnp.zeros((rows, cols), F32).at[:a.shape[0], :a.shape[1]].set(a.astype(F32))


def kernel(x, c, ctx, c_ctx, w_mod, b_mod, g_mix, w_in, wa_sink, na_rpb, ssm_conv_w, ssm_conv_b, ssm_dt_bias, ssm_a_log, ssm_d, ssm_norm_g, w_out, g_ffn, w_ffn_in, w_ffn_out, g_final, loss_target, m_c_ctx, m_w_mod, m_b_mod, m_g_mix, m_w_in, m_wa_sink, m_na_rpb, m_ssm_conv_w, m_ssm_conv_b, m_ssm_dt_bias, m_ssm_a_log, m_ssm_d, m_ssm_norm_g, m_w_out, m_g_ffn, m_w_ffn_in, m_w_ffn_out, m_g_final, v_c_ctx, v_w_mod, v_b_mod, v_g_mix, v_w_in, v_wa_sink, v_na_rpb, v_ssm_conv_w, v_ssm_conv_b, v_ssm_dt_bias, v_ssm_a_log, v_ssm_d, v_ssm_norm_g, v_w_out, v_g_ffn, v_w_ffn_in, v_w_ffn_out, v_g_final):
    L, Lc = x.shape[1], ctx.shape[1]
    T = L + Lc
    nL = L // TR
    mx, my, mc = lax.axis_index("x"), lax.axis_index("y"), lax.axis_index("c")
    dev = 4 * mx + 2 * my + mc
    chip = 2 * mx + my
    MODW = 6 * D // N_CHIPS
    CW = 1024 // N_CHIPS

    sc = _silu(c.astype(F32))
    scc = _silu(c_ctx.astype(F32))[None]
    f1 = _Flat()
    f1.add("sc", sc)
    f1.add("conv_w", ssm_conv_w)
    g1, _ = allgather8(f1.rows(), "gather_cond")
    g1 = f1.split_lead(g1)
    sc_all = g1["sc"][:, 0]
    conv_w = jnp.concatenate([g1["conv_w"][2 * k] for k in range(N_CHIPS)], axis=-1)
    A16 = jnp.concatenate([sc_all, scc, jnp.zeros((7, D), F32)], axis=0)

    mod_part = jnp.stack([matmul(A16, w_mod[l], "nn", F32, f"mod_fwd{l}") for l in range(DEPTH)])
    f2 = _Flat()
    f2.add("mod", mod_part)
    g2, _ = allgather8(f2.rows(), "gather_mod")
    g2 = f2.split_lead(g2)["mod"]
    mods = jnp.concatenate([g2[2 * k] for k in range(N_CHIPS)], axis=-1) + b_mod[:, None, :]
    mod_l = lax.dynamic_index_in_dim(mods, dev, axis=1, keepdims=False).reshape(DEPTH, 6, D)
    mod_c = mods[:, 8].reshape(DEPTH, 6, D)
    mod = jnp.stack([mod_l, mod_c], axis=1)
    mrow = lambda l, j: mod[l, :, j]

    own = {"w_in": w_in, "w_out": w_out, "w_ffn_in": w_ffn_in, "w_ffn_out": w_ffn_out}
    sh16 = [own[n][l].astype(BF16) for n in _BIG for l in range(DEPTH)]
    gath = list(gather_weights(sh16[:1], "gather_first"))
    after_first = (gath[0][0, 0, 0] * 0).astype(BF16)
    gath += list(gather_weights_sc([sh16[1] + after_first] + sh16[2:], "gather_rest"))
    gw = {n: [gath[DEPTH * i + l] for l in range(DEPTH)] for i, n in enumerate(_BIG)}
    W_in = [jnp.pad(jnp.concatenate([g[k] for k in range(N_CHIPS)], axis=1), ((0, 0), (0, IN_PAD - IN_COLS))) for g in gw["w_in"]]
    W_out = [g.reshape(D, D) for g in gw["w_out"]]
    W_fo = [g.reshape(D_FF, D) for g in gw["w_ffn_out"]]
    W_fi = gw["w_ffn_in"]

    cos, sin, rotm = rope_tables(L, Lc)
    x0 = jnp.concatenate([x[0], ctx[0]], axis=0).astype(F32)

    sv = []
    xin = x0
    gsv_first = _gsv([None, mrow(0, 0), mrow(0, 1)])
    _, h1 = res_norm_mod(x0, None, gsv_first, g_mix[0][None], nL, "norm_first")
    for l in range(DEPTH):
        s = {"xin": xin, "h1": h1}
        P = matmul(h1, W_in[l], "nn", F32, f"in_proj{l}", tn=IN_PAD)
        qr, kr, kb, vb = rope_apply(P, C_QA // 256, P, C_KA // 128, cos, sin, rotm, False, f"rope{l}", kv_src=P)
        sink8 = _pad8(jnp.broadcast_to(wa_sink[l][:, None], (WA_HEADS, 128)))
        oa, sta = win_attn_fwd(qr, kr, P, sink8, L, Lc, f"wa_fwd{l}")
        bias = na_bias_table(na_rpb[l], l)
        ob, stb = na_fwd(P, kb, vb, bias, L, Lc, f"na_fwd{l}")
        w8 = jnp.concatenate([conv_w[l], jnp.zeros((1, 1024), F32)], axis=0)
        pre, act = conv_silu_fwd(P, w8, ssm_conv_b[l][None], nL, f"conv_fwd{l}")
        dtb8, al8 = _pad8(ssm_dt_bias[l]), _pad8(ssm_a_log[l])
        yf, yb, hsf, hsb = ssd_fwd(act, P, dtb8, al8, L, Lc, f"ssd_fwd{l}")
        dskip = jnp.repeat(ssm_d[l], S_P)[None]
        oc = ssm_out_fwd(yf, yb, act, P, dskip, ssm_norm_g[l][None], f"ssm_out_fwd{l}")
        mixin = jnp.concatenate([oa, ob, oc], axis=1)
        mix = matmul(mixin, W_out[l], "nn", F32, f"out_proj{l}")
        gsv_mid = _gsv([mrow(l, 2), mrow(l, 3), mrow(l, 4)])
        x1, h2 = res_norm_mod(xin, mix, gsv_mid, g_ffn[l][None], nL, f"norm_mid{l}")
        gu = matmul_fi(h2, W_fi[l], "nn", BF16, f"ffn_in{l}")
        af = swiglu_fwd(gu, f"swiglu_fwd{l}")
        fo = matmul(af, W_fo[l], "nn", F32, f"ffn_out{l}", tk=D_FF)
        s.update(P=P, qr=qr, kr=kr, sink8=sink8, oa=oa, sta=sta, ob=ob, stb=stb, kb=kb, vb=vb, bias=bias, w8=w8, pre=pre, act=act, dtb8=dtb8, al8=al8, yf=yf,
                 yb=yb, hsf=hsf, hsb=hsb, dskip=dskip, mixin=mixin, mix=mix, gsv_mid=gsv_mid, x1=x1, h2=h2, gu=gu, af=af, fo=fo)
        if l + 1 < DEPTH:
            s["gsv_end"] = _gsv([mrow(l, 5), mrow(l + 1, 0), mrow(l + 1, 1)])
            xin, h1 = res_norm_mod(x1, fo, s["gsv_end"], g_mix[l + 1][None], nL, f"norm_end{l}")
        else:
            s["gsv_end"] = _gsv([mrow(l, 5), None, None])
        sv.append(s)

    last = sv[-1]
    loss8, dres, dfo, dgsv_end, dg_final = final_loss(last["x1"], last["fo"], last["gsv_end"], g_final[None], loss_target[0].astype(F32), nL, "final_loss")
    loss = lax.psum(loss8[0, 0], ("x", "y", "c"))

    dmod = [[None] * 6 for _ in range(DEPTH)]
    gW = {n: [None] * DEPTH for n in _BIG}
    small = [dict() for _ in range(DEPTH)]
    parts = [None] * DEPTH
    cvec = mc.astype(jnp.int32).reshape(1)
    grad_x = None
    for l in reversed(range(DEPTH)):
        s = sv[l]
        dmod[l][5] = dgsv_end[:, 0]
        if l + 1 < DEPTH:
            dmod[l + 1][0], dmod[l + 1][1] = dgsv_end[:, 1], dgsv_end[:, 2]
        daf = matmul(dfo, W_fo[l], "nt", BF16, f"ffn_out_dx{l}")
        gW["w_ffn_out"][l] = matmul(s["af"], dfo, "tn", BF16, f"ffn_out_dw{l}", tm=1408, tk=T).reshape(N_CHIPS, D_FF // N_CHIPS, D)
        dgu = swiglu_bwd(s["gu"], daf, f"swiglu_bwd{l}")
        dh2 = matmul_fi(dgu, W_fi[l], "nt", F32, f"ffn_in_dx{l}")
        gW["w_ffn_in"][l] = matmul_fi(s["h2"], dgu, "tn", BF16, f"ffn_in_dw{l}")
        dres, dmix, dgsv_mid, dg_ffn = res_norm_mod_bwd(s["x1"], s["mix"], s["gsv_mid"], g_ffn[l][None], dh2, dres, nL, f"norm_mid_bwd{l}")
        dmod[l][2], dmod[l][3], dmod[l][4] = dgsv_mid[:, 0], dgsv_mid[:, 1], dgsv_mid[:, 2]
        dmixin = matmul(dmix, W_out[l], "nt", F32, f"out_proj_dx{l}")
        gW["w_out"][l] = matmul(s["mixin"], dmix, "tn", BF16, f"out_proj_dw{l}", tm=1024, tk=T).reshape(N_CHIPS, D // N_CHIPS, D)
        P = s["P"]
        dqr, dkr, dva, dsink = win_attn_bwd(s["qr"], s["kr"], P, s["sink8"], dmixin, s["oa"], s["sta"], L, Lc, f"wa_bwd{l}")
        dqa, dka = rope_apply(dqr, 0, dkr[WA_BLK:WA_BLK + T], 0, cos, sin, rotm, True, f"rope_bwd{l}")
        dqb, dkb, dvb, dbias = na_bwd(P, s["kb"], s["vb"], s["bias"], dmixin, s["ob"], s["stb"], L, Lc, f"na_bwd{l}")
        dy, dxs1, dz, dvec = ssm_out_bwd(s["yf"], s["yb"], s["act"], P, s["dskip"], ssm_norm_g[l][None], dmixin, f"ssm_out_bwd{l}")
        dxf, dbf, dcf, ddf, dxb, dbb, dcb, ddb, ddtb, dal = ssd_bwd(s["act"], P, s["dtb8"], s["al8"], s["hsf"], s["hsb"], dy, L, Lc, f"ssd_bwd{l}")
        dpre = dsilu(s["pre"], [dxf, dxb, dxs1], [dbf, dbb], [dcf, dcb], f"dsilu{l}")
        dxbc, dw8, db8 = conv_bwd(dpre, P, s["w8"], nL, f"conv_bwd{l}")
        dP = jnp.concatenate([dqa, dqb, dz, dka, dva[WA_BLK:WA_BLK + T].astype(BF16), dkb.astype(BF16), dvb.astype(BF16), dxbc,
                              ddf.astype(BF16), ddb.astype(BF16), jnp.zeros((T, IN_PAD - IN_COLS), BF16)], axis=1)
        dh1 = matmul(dP, W_in[l], "nt", F32, f"in_proj_dx{l}", tk=IN_PAD)
        dwin = matmul(s["h1"], dP, "tn", BF16, f"in_proj_dw{l}", tm=512, tn=IN_PAD, tk=T // 2)
        cw = IN_COLS // N_CHIPS
        gW["w_in"][l] = jnp.stack([dwin[:, k * cw:(k + 1) * cw] for k in range(N_CHIPS)])
        garr = [gW[n][l] for n in _BIG]
        got = swap_halves(garr, f"reduce_d2d{l}")
        chip_sum = [add_halves(garr[a], got[a], cvec, f"reduce_add_pair{l}_{a}") for a in range(len(garr))]
        parts[l] = scatter_chips_sc(chip_sum, f"reduce_ici{l}")
        small[l] = dict(g_ffn=dg_ffn[0], wa_sink=dsink[:WA_HEADS, 0], na_rpb=na_rpb_grad(dbias, l), conv_w=dw8[:S_CONV], conv_b=db8[0],
                        dt_bias=ddtb[:2, :8], a_log=dal[:2, :8], ssm_d=dvec[0].reshape(S_HEADS, S_P).sum(axis=1), norm_g=dvec[1])
        if l > 0:
            p = sv[l - 1]
            dres, dfo, dgsv_end, dg_mix = res_norm_mod_bwd(s["xin"], p["fo"], p["gsv_end"], g_mix[l][None], dh1, dres, nL, f"norm_end_bwd{l - 1}")
        else:
            grad_x, _, dgsv_first, dg_mix = res_norm_mod_bwd(s["xin"], None, gsv_first, g_mix[0][None], dh1, dres, nL, "norm_first_bwd")
            dmod[0][0], dmod[0][1] = dgsv_first[:, 1], dgsv_first[:, 2]
        small[l]["g_mix"] = dg_mix[0]
    for l in range(DEPTH):
        for j in range(6):
            if dmod[l][j] is None:
                dmod[l][j] = jnp.zeros((2, D), F32)
    dmod = jnp.stack([jnp.stack(r, axis=1) for r in dmod])

    f3 = _Flat()
    f3.add("dmod_l", dmod[:, 0].reshape(DEPTH, 6 * D))
    f3.add("dmod_c", dmod[:, 1].reshape(DEPTH, 6 * D))
    f3.add("g_final", dg_final[0])
    for n in ("g_mix", "g_ffn", "wa_sink", "na_rpb", "conv_w", "conv_b", "dt_bias", "a_log", "ssm_d", "norm_g"):
        f3.add(n, jnp.stack([small[l][n] for l in range(DEPTH)]))
    g3, s3 = allgather8(f3.rows(), "reduce_small")
    dmod_all = f3.split_lead(g3)["dmod_l"]
    s3 = f3.split(s3)
    dmodc_tot = s3["dmod_c"]
    col0 = chip * MODW
    G16, G16c = [], []
    for l in range(DEPTH):
        rows = jnp.concatenate([dmod_all[:, l], dmodc_tot[l][None], jnp.zeros((7, 6 * D), F32)], axis=0)
        G16.append(lax.dynamic_slice_in_dim(rows, col0, MODW, axis=1))
        rc = jnp.concatenate([dmodc_tot[l][None], jnp.zeros((15, 6 * D), F32)], axis=0)
        G16c.append(lax.dynamic_slice_in_dim(rc, col0, MODW, axis=1))
    grad_w_mod = jnp.stack([matmul(A16, G16[l], "tn", F32, f"mod_dw{l}") for l in range(DEPTH)])
    dscc_part = sum(matmul(G16c[l], w_mod[l], "nt", F32, f"mod_dx{l}")[0] for l in range(DEPTH))
    _, s4 = allgather8(_pad_rows(dscc_part * (mc == 1).astype(F32)), "reduce_cctx")
    dscc = s4.reshape(-1)[:D]
    cc = c_ctx.astype(F32)
    sg = 1.0 / (1.0 + jnp.exp(-cc))
    grad_c_ctx = dscc * (sg * (1.0 + cc * (1.0 - sg)))

    halves = [[sum_slots(parts[l][i], f"reduce_add_chips{l}_{i}") for l in range(DEPTH)] for i in range(len(_BIG))]
    gsh = dict(zip(_BIG, share_halves(halves, "reduce_share")))

    grads = {"c_ctx": grad_c_ctx, "w_mod": grad_w_mod, "b_mod": s3["dmod_l"] + s3["dmod_c"], "g_mix": s3["g_mix"], "w_in": gsh["w_in"],
             "wa_sink": s3["wa_sink"], "na_rpb": s3["na_rpb"],
             "ssm_conv_w": lax.dynamic_slice_in_dim(s3["conv_w"], chip * CW, CW, axis=2), "ssm_conv_b": s3["conv_b"],
             "ssm_dt_bias": s3["dt_bias"], "ssm_a_log": s3["a_log"], "ssm_d": s3["ssm_d"], "ssm_norm_g": s3["norm_g"],
             "w_out": gsh["w_out"], "g_ffn": s3["g_ffn"], "w_ffn_in": gsh["w_ffn_in"], "w_ffn_out": gsh["w_ffn_out"], "g_final": s3["g_final"]}
    wts = {"c_ctx": c_ctx, "w_mod": w_mod, "b_mod": b_mod, "g_mix": g_mix, "w_in": w_in, "wa_sink": wa_sink, "na_rpb": na_rpb,
           "ssm_conv_w": ssm_conv_w, "ssm_conv_b": ssm_conv_b, "ssm_dt_bias": ssm_dt_bias, "ssm_a_log": ssm_a_log, "ssm_d": ssm_d,
           "ssm_norm_g": ssm_norm_g, "w_out": w_out, "g_ffn": g_ffn, "w_ffn_in": w_ffn_in, "w_ffn_out": w_ffn_out, "g_final": g_final}
    ms = {"c_ctx": m_c_ctx, "w_mod": m_w_mod, "b_mod": m_b_mod, "g_mix": m_g_mix, "w_in": m_w_in, "wa_sink": m_wa_sink, "na_rpb": m_na_rpb,
          "ssm_conv_w": m_ssm_conv_w, "ssm_conv_b": m_ssm_conv_b, "ssm_dt_bias": m_ssm_dt_bias, "ssm_a_log": m_ssm_a_log, "ssm_d": m_ssm_d,
          "ssm_norm_g": m_ssm_norm_g, "w_out": m_w_out, "g_ffn": m_g_ffn, "w_ffn_in": m_w_ffn_in, "w_ffn_out": m_w_ffn_out, "g_final": m_g_final}
    vs = {"c_ctx": v_c_ctx, "w_mod": v_w_mod, "b_mod": v_b_mod, "g_mix": v_g_mix, "w_in": v_w_in, "wa_sink": v_wa_sink, "na_rpb": v_na_rpb,
          "ssm_conv_w": v_ssm_conv_w, "ssm_conv_b": v_ssm_conv_b, "ssm_dt_bias": v_ssm_dt_bias, "ssm_a_log": v_ssm_a_log, "ssm_d": v_ssm_d,
          "ssm_norm_g": v_ssm_norm_g, "w_out": v_w_out, "g_ffn": v_g_ffn, "w_ffn_in": v_w_ffn_in, "w_ffn_out": v_w_ffn_out, "g_final": v_g_final}
    names = list(wts)
    grads = {n: grads[n].reshape(wts[n].shape).astype(F32) for n in names}
    big = ("w_mod", "w_in", "w_out", "w_ffn_in", "w_ffn_out")
    delta, new_m, new_v = {}, {}, {}
    for n in big:
        delta[n], new_m[n], new_v[n] = adamw(wts[n], grads[n], ms[n], vs[n], f"adamw_{n}")
    packs = []
    for src in (wts, grads, ms, vs):
        f = _Flat()
        for n in names:
            if n not in big:
                f.add(n, src[n])
        packs.append(f)
    d_, m_, v_ = adamw(*[f.rows()[None] for f in packs], "adamw_small")
    for dst, rows in ((delta, d_), (new_m, m_), (new_v, v_)):
        dst.update(packs[0].split(rows[0]))

    return (loss, grad_x[:L][None], *[grads[n] for n in names], *[delta[n] for n in names],
            *[new_m[n] for n in names], *[new_v[n] for n in names])
```

```python
import functools

import numpy as np
import jax
import jax.numpy as jnp
from jax import lax
from jax.experimental import pallas as pl
from jax.experimental.pallas import tpu as pltpu
from jax.experimental.pallas import tpu_sc as plsc

F32 = jnp.float32
BF16 = jnp.bfloat16
_MXU = jnp.bfloat16
_HI = lax.Precision.HIGHEST
MESH = pl.DeviceIdType.MESH

D = 1024
HD = 64
GRID_W = 64
EPS = 1e-6
ROPE_BASE = 10000.0
WA_HEADS, WA_KV = 4, 2
WA_BLK = 128
NA_HEADS, NA_KH, NA_KW = 4, 8, 16
S_HEADS, S_P, S_INNER, S_GROUPS, S_N, S_CONV, S_Q = 8, 64, 512, 2, 128, 7, 128
D_FF = 2816
IN_COLS = 2832
IN_PAD = 2944
C_QA, C_QB, C_Z, C_KA, C_VA, C_KB, C_VB, C_XBC, C_DT = 0, 256, 512, 1024, 1152, 1280, 1536, 1792, 2816
ADAM_LR, ADAM_B1, ADAM_B2, ADAM_EPS, ADAM_WD, ADAM_STEP = 0.001, 0.9, 0.999, 1e-08, 0.01, 10

TR = 256
NEG = -1e30
VMEM_CAP = 56 * 1024 * 1024


PIN_BYTES = 256 * 1024


def _is_big(a):
    return hasattr(a, "shape") and len(a.shape) >= 2 and int(np.prod(a.shape)) * jnp.dtype(a.dtype).itemsize >= PIN_BYTES


def _pc(body, *, out_shape, pin=True, **kw):
    if not pin:
        return pl.pallas_call(body, out_shape=out_shape, **kw)
    one = isinstance(out_shape, jax.ShapeDtypeStruct)
    outs = [pltpu.HBM(s.shape, s.dtype) if _is_big(s) else s for s in ([out_shape] if one else out_shape)]
    call = pl.pallas_call(body, out_shape=outs[0] if one else outs, **kw)
    return lambda *args: call(*[pltpu.with_memory_space_constraint(a, pltpu.HBM) if _is_big(a) else a for a in args])


def _cp(sem=None, vmem=None):
    kw = {}
    if sem is not None:
        kw["dimension_semantics"] = sem
    if vmem is not None:
        kw["vmem_limit_bytes"] = int(min(max(vmem, 16 * 1024 * 1024), VMEM_CAP))
    return pltpu.CompilerParams(**kw)


def _sds(shape, dtype):
    return jax.ShapeDtypeStruct(tuple(shape), dtype)


_DIMS = {"nn": ((1,), (0,)), "nt": ((1,), (1,)), "tn": ((0,), (0,))}


def _dg(a, b, dims):
    return lax.dot_general(a.astype(_MXU), b.astype(_MXU), (dims, ((), ())), preferred_element_type=F32)


@functools.partial(jax.custom_vjp, nondiff_argnums=(2,))
def bdot(a, b, mode):
    return _dg(a, b, _DIMS[mode])


def _bdot_fwd(a, b, mode):
    return bdot(a, b, mode), (a, b)


def _bdot_bwd(mode, res, g):
    a, b = res
    if mode == "nn":
        return bdot(g, b, "nt"), bdot(a, g, "tn")
    if mode == "nt":
        return bdot(g, b, "nn"), bdot(g, a, "tn")
    return bdot(b, g, "nt"), bdot(a, g, "nn")


bdot.defvjp(_bdot_fwd, _bdot_bwd)


def hdot(a, b, mode="nn"):
    return lax.dot_general(a, b, (_DIMS[mode], ((), ())), precision=_HI, preferred_element_type=F32)


def _silu(x):
    return x / (1.0 + jnp.exp(-x))


def _softplus(x):
    return jnp.maximum(x, 0.0) + jnp.log(1.0 + jnp.exp(-jnp.abs(x)))


def _div_tile(n, cap, mult):
    if n <= cap:
        return n
    best = None
    for t in range(mult, cap + 1, mult):
        if n % t == 0:
            best = t
    assert best is not None, (n, cap, mult)
    return best


def matmul(a, b, mode, out_dtype, name, tm=640, tn=1536, tk=1408, hi=False):
    if mode == "tn":
        K, M = a.shape
    else:
        M, K = a.shape
    N = b.shape[0] if mode == "nt" else b.shape[1]
    tm = _div_tile(M, tm, 128 if mode == "tn" else 16)
    tn = _div_tile(N, tn, 128)
    tk = _div_tile(K, tk, 128 if mode != "tn" else 16)
    nk = K // tk
    dims = _DIMS[mode]

    def body(a_ref, b_ref, o_ref, *acc):
        if hi:
            part = lax.dot_general(a_ref[...], b_ref[...], (dims, ((), ())), precision=_HI, preferred_element_type=F32)
        else:
            part = _dg(a_ref[...], b_ref[...], dims)
        if nk == 1:
            o_ref[...] = part.astype(o_ref.dtype)
        else:
            k = pl.program_id(2)

            @pl.when(k == 0)
            def _():
                acc[0][...] = part

            @pl.when(k > 0)
            def _():
                acc[0][...] += part

            @pl.when(k == nk - 1)
            def _():
                o_ref[...] = acc[0][...].astype(o_ref.dtype)

    if mode == "tn":
        a_spec = pl.BlockSpec((tk, tm), lambda i, j, k: (k, i))
    else:
        a_spec = pl.BlockSpec((tm, tk), lambda i, j, k: (i, k))
    if mode == "nt":
        b_spec = pl.BlockSpec((tn, tk), lambda i, j, k: (j, k))
    else:
        b_spec = pl.BlockSpec((tk, tn), lambda i, j, k: (k, j))
    isz = lambda x: jnp.dtype(x.dtype).itemsize
    vmem = 2 * (tm * tk * isz(a) + tk * tn * isz(b) + tm * tn * jnp.dtype(out_dtype).itemsize) + 3 * tm * tn * 4
    return _pc(
        body, name=name, grid=(M // tm, N // tn, nk),
        in_specs=[a_spec, b_spec], out_specs=pl.BlockSpec((tm, tn), lambda i, j, k: (i, j)),
        out_shape=_sds((M, N), out_dtype),
        scratch_shapes=[pltpu.VMEM((tm, tn), F32)] if nk > 1 else [],
        compiler_params=_cp(("parallel", "parallel", "arbitrary"), vmem + (8 << 20)),
    )(a, b)


def _norm_mod(xo, shift, scale, g):
    r = lax.rsqrt(jnp.mean(xo * xo, axis=-1, keepdims=True) + EPS)
    return (xo * r) * g * (1.0 + scale) + shift


def res_norm_mod(x, y, gsv, g, nL, name):
    T = x.shape[0]
    has_y = y is not None

    def body(*refs):
        if has_y:
            x_ref, y_ref, gsv_ref, g_ref, xo_ref, h_ref = refs
            xo = x_ref[...] + gsv_ref[0, 0:1, :] * y_ref[...]
            xo_ref[...] = xo
        else:
            x_ref, gsv_ref, g_ref, h_ref = refs
            xo = x_ref[...]
        h_ref[...] = _norm_mod(xo, gsv_ref[0, 1:2, :], gsv_ref[0, 2:3, :], g_ref[...]).astype(h_ref.dtype)

    row = pl.BlockSpec((TR, D), lambda i: (i, 0))
    in_specs = [row] + ([row] if has_y else []) + [pl.BlockSpec((1, 8, D), lambda i: (i // nL, 0, 0)),
                                                     pl.BlockSpec((1, D), lambda i: (0, 0))]
    out_specs = ([row] if has_y else []) + [row]
    out_shape = ([_sds((T, D), F32)] if has_y else []) + [_sds((T, D), BF16)]
    args = (x, y, gsv, g) if has_y else (x, gsv, g)
    outs = _pc(body, name=name, grid=(T // TR,), in_specs=in_specs, out_specs=out_specs, out_shape=out_shape,
               compiler_params=_cp(("arbitrary",), 24 << 20))(*args)
    return (outs[0], outs[1]) if has_y else (None, outs[0])


def res_norm_mod_bwd(xo, y, gsv, g, dh, dres, nL, name):
    T = xo.shape[0]
    has_y = y is not None

    def body(*refs):
        if has_y:
            xo_ref, y_ref, gsv_ref, g_ref, dh_ref, dres_ref, dx_ref, dy_ref, dgsv_ref, dg_ref = refs
        else:
            xo_ref, gsv_ref, g_ref, dh_ref, dres_ref, dx_ref, dgsv_ref, dg_ref = refs
        i = pl.program_id(0)

        @pl.when((i == 0) | (i == nL))
        def _():
            dgsv_ref[...] = jnp.zeros_like(dgsv_ref)

        @pl.when(i == 0)
        def _():
            dg_ref[...] = jnp.zeros_like(dg_ref)

        _, vjp = jax.vjp(_norm_mod, xo_ref[...], gsv_ref[0, 1:2, :], gsv_ref[0, 2:3, :], g_ref[...])
        dxn, dshift, dscale, dg = vjp(dh_ref[...].astype(F32))
        dxo = dres_ref[...] + dxn
        dx_ref[...] = dxo
        if has_y:
            dy_ref[...] = (gsv_ref[0, 0:1, :] * dxo).astype(dy_ref.dtype)
            dgsv_ref[0, 0:1, :] += jnp.sum(y_ref[...] * dxo, axis=0, keepdims=True)
        dgsv_ref[0, 1:2, :] += dshift
        dgsv_ref[0, 2:3, :] += dscale
        dg_ref[0:1, :] += dg

    row = pl.BlockSpec((TR, D), lambda i: (i, 0))
    gspec = pl.BlockSpec((1, 8, D), lambda i: (i // nL, 0, 0))
    in_specs = [row] + ([row] if has_y else []) + [gspec, pl.BlockSpec((1, D), lambda i: (0, 0)), row, row]
    out_specs = [row] + ([row] if has_y else []) + [gspec, pl.BlockSpec((8, D), lambda i: (0, 0))]
    out_shape = [_sds((T, D), F32)] + ([_sds((T, D), BF16)] if has_y else []) + [_sds((2, 8, D), F32), _sds((8, D), F32)]
    args = (xo, y, gsv, g, dh, dres) if has_y else (xo, gsv, g, dh, dres)
    outs = _pc(body, name=name, grid=(T // TR,), in_specs=in_specs, out_specs=out_specs, out_shape=out_shape,
               compiler_params=_cp(("arbitrary",), 32 << 20))(*args)
    if has_y:
        return outs
    return outs[0], None, outs[1], outs[2]


def final_loss(x, y, gsv, g, target, nL, name):
    T = x.shape[0]

    def lossf(xo, gv, t):
        yn = (xo * lax.rsqrt(jnp.mean(xo * xo, axis=-1, keepdims=True) + EPS)) * gv
        e = yn - t
        return 0.5 * jnp.sum(jnp.sum(e * e, axis=-1, keepdims=True) * (1.0 / D), axis=0, keepdims=True)

    def body(x_ref, y_ref, gsv_ref, g_ref, t_ref, loss_ref, dx_ref, dy_ref, dgsv_ref, dg_ref):
        i = pl.program_id(0)

        @pl.when(i == 0)
        def _():
            loss_ref[...] = jnp.zeros_like(loss_ref)
            dg_ref[...] = jnp.zeros_like(dg_ref)

        @pl.when((i == 0) | (i == nL))
        def _():
            dgsv_ref[...] = jnp.zeros_like(dgsv_ref)

        @pl.when(i < nL)
        def _():
            gate = gsv_ref[0, 0:1, :]
            yv = y_ref[...]
            xo = x_ref[...] + gate * yv
            lv, vjp = jax.vjp(lossf, xo, g_ref[...], t_ref[...])
            dxo, dg, _ = vjp(jnp.ones((1, 1), F32))
            loss_ref[...] += jnp.broadcast_to(lv, loss_ref.shape)
            dx_ref[...] = dxo
            dy_ref[...] = (gate * dxo).astype(dy_ref.dtype)
            dgsv_ref[0, 0:1, :] += jnp.sum(yv * dxo, axis=0, keepdims=True)
            dg_ref[0:1, :] += dg

        @pl.when(i >= nL)
        def _():
            dx_ref[...] = jnp.zeros_like(dx_ref)
            dy_ref[...] = jnp.zeros_like(dy_ref)

    row = pl.BlockSpec((TR, D), lambda i: (i, 0))
    gspec = pl.BlockSpec((1, 8, D), lambda i: (i // nL, 0, 0))
    return _pc(
        body, name=name, grid=(T // TR,),
        in_specs=[row, row, gspec, pl.BlockSpec((1, D), lambda i: (0, 0)),
                  pl.BlockSpec((TR, D), lambda i: (jnp.minimum(i, nL - 1), 0))],
        out_specs=[pl.BlockSpec((8, 128), lambda i: (0, 0)), row, row, gspec, pl.BlockSpec((8, D), lambda i: (0, 0))],
        out_shape=[_sds((8, 128), F32), _sds((T, D), F32), _sds((T, D), BF16), _sds((2, 8, D), F32), _sds((8, D), F32)],
        compiler_params=_cp(("arbitrary",), 32 << 20),
    )(x, y, gsv, g, target)


FI_BLK = 2 * D_FF // 4


def _fi_chip(j):
    return (j % 2) * 2 + j // 2


def matmul_fi(a, b, mode, out_dtype, name):
    T = a.shape[0]
    if mode == "tn":
        tmd = 512

        def body(a_ref, b_ref, o_ref):
            o_ref[0] = _dg(a_ref[...], b_ref[...], _DIMS["tn"]).astype(o_ref.dtype)

        return _pc(body, name=name, grid=(D // tmd, 4),
                   in_specs=[pl.BlockSpec((T, tmd), lambda i, j: (0, i)), pl.BlockSpec((T, FI_BLK), lambda i, j: (0, j))],
                   out_specs=pl.BlockSpec((1, tmd, FI_BLK), lambda i, j: (_fi_chip(j), i, 0)),
                   out_shape=_sds((4, D, FI_BLK), out_dtype), compiler_params=_cp(("parallel", "arbitrary"), 48 << 20))(a, b)
    if mode == "nn":
        tm = _div_tile(T, 1280, 16)

        def body(a_ref, b_ref, o_ref):
            o_ref[...] = _dg(a_ref[...], b_ref[0], _DIMS["nn"]).astype(o_ref.dtype)

        return _pc(body, name=name, grid=(T // tm, 4),
                   in_specs=[pl.BlockSpec((tm, D), lambda i, j: (i, 0)), pl.BlockSpec((1, D, FI_BLK), lambda i, j: (_fi_chip(j), 0, 0))],
                   out_specs=pl.BlockSpec((tm, FI_BLK), lambda i, j: (i, j)), out_shape=_sds((T, 4 * FI_BLK), out_dtype),
                   compiler_params=_cp(("parallel", "arbitrary"), 40 << 20))(a, b)
    tm = _div_tile(T, 640, 16)

    def body(a_ref, b_ref, o_ref):
        acc = None
        for k in range(4):
            part = _dg(a_ref[:, k * FI_BLK:(k + 1) * FI_BLK], b_ref[_fi_chip(k)], _DIMS["nt"])
            acc = part if acc is None else acc + part
        o_ref[...] = acc.astype(o_ref.dtype)

    return _pc(body, name=name, grid=(T // tm,),
               in_specs=[pl.BlockSpec((tm, 4 * FI_BLK), lambda i: (i, 0)), pl.BlockSpec((4, D, FI_BLK), lambda i: (0, 0, 0))],
               out_specs=pl.BlockSpec((tm, D), lambda i: (i, 0)), out_shape=_sds((T, D), out_dtype),
               compiler_params=_cp(("parallel",), VMEM_CAP))(a, b)


def _swiglu(gate, up):
    return _silu(gate) * up


def swiglu_fwd(gu, name):
    T = gu.shape[0]

    def body(x_ref, o_ref):
        o_ref[...] = _swiglu(x_ref[:, :FI_BLK].astype(F32), x_ref[:, FI_BLK:].astype(F32)).astype(o_ref.dtype)

    return _pc(body, name=name, grid=(T // TR, 2), in_specs=[pl.BlockSpec((TR, 2 * FI_BLK), lambda i, j: (i, j))],
               out_specs=pl.BlockSpec((TR, FI_BLK), lambda i, j: (i, j)), out_shape=_sds((T, D_FF), BF16),
               compiler_params=_cp(("parallel", "parallel"), 24 << 20))(gu)


def swiglu_bwd(gu, dact, name):
    T = gu.shape[0]

    def body(x_ref, d_ref, o_ref):
        g, u, d = x_ref[:, :FI_BLK].astype(F32), x_ref[:, FI_BLK:].astype(F32), d_ref[...].astype(F32)
        sg = 1.0 / (1.0 + jnp.exp(-g))
        sl = g * sg
        o_ref[:, :FI_BLK] = (d * u * (sg + sl * (1.0 - sg))).astype(o_ref.dtype)
        o_ref[:, FI_BLK:] = (d * sl).astype(o_ref.dtype)

    return _pc(body, name=name, grid=(T // TR, 2),
               in_specs=[pl.BlockSpec((TR, 2 * FI_BLK), lambda i, j: (i, j)), pl.BlockSpec((TR, FI_BLK), lambda i, j: (i, j))],
               out_specs=pl.BlockSpec((TR, 2 * FI_BLK), lambda i, j: (i, j)), out_shape=_sds((T, 2 * D_FF), BF16),
               compiler_params=_cp(("parallel", "parallel"), 32 << 20))(gu, dact)


def rope_tables(L, Lc):
    t = np.arange(L)
    rows, cols = t // GRID_W, t % GRID_W
    inv = ROPE_BASE ** (-np.arange(16, dtype=np.float32) / 16)
    lane = np.arange(64)
    pos = np.where((lane // 32)[None, :] == 0, rows[:, None], cols[:, None]).astype(np.float32)
    ang = jnp.asarray(pos) * jnp.asarray(inv[lane % 16])[None, :]
    cos = jnp.concatenate([jnp.cos(ang), jnp.ones((Lc, 64), F32)], axis=0)
    sin = jnp.concatenate([jnp.sin(ang), jnp.zeros((Lc, 64), F32)], axis=0)
    R = np.zeros((128, 128), np.float32)
    for i in range(128):
        if (i % 32) < 16:
            R[i + 16, i] = -1.0
        else:
            R[i - 16, i] = 1.0
    return jnp.tile(cos, (1, 2)), jnp.tile(sin, (1, 2)), jnp.asarray(R)


def rope_apply(q_src, q_col, k_src, k_col, cos, sin, R, transpose, name, kv_src=None):
    T = cos.shape[0]
    with_kv = kv_src is not None

    def rot(x, c, s, Rm):
        if transpose:
            return x * c + hdot(x * s, Rm, "nt")
        return x * c + hdot(x, Rm) * s

    def body(q_ref, k_ref, c_ref, s_ref, R_ref, *rest):
        qo_ref, ko_ref = rest[-4:-2] if with_kv else rest
        c, s, Rm = c_ref[...], s_ref[...], R_ref[...]
        for j in range(2):
            qo_ref[:, j * 128:(j + 1) * 128] = rot(q_ref[:, j * 128:(j + 1) * 128].astype(F32), c, s, Rm).astype(qo_ref.dtype)
        ko_ref[...] = rot(k_ref[...].astype(F32), c, s, Rm).astype(ko_ref.dtype)
        if with_kv:
            rest[-2][...] = rest[0][...].astype(BF16)
            rest[-1][...] = rest[1][...].astype(BF16)

    tab = pl.BlockSpec((TR, 128), lambda i: (i, 0))
    wide = pl.BlockSpec((TR, 256), lambda i: (i, 0))
    kv_in = [pl.BlockSpec((TR, 256), lambda i: (i, C_KB // 256)), pl.BlockSpec((TR, 256), lambda i: (i, C_VB // 256))] if with_kv else []
    return _pc(body, name=name, grid=(T // TR,),
               in_specs=[pl.BlockSpec((TR, 256), lambda i: (i, q_col)), pl.BlockSpec((TR, 128), lambda i: (i, k_col)),
                         tab, tab, pl.BlockSpec((128, 128), lambda i: (0, 0))] + kv_in,
               out_specs=[wide, tab] + ([wide, wide] if with_kv else []),
               out_shape=[_sds((T, 256), BF16), _sds((T, 128), BF16)] + ([_sds((T, 256), BF16)] * 2 if with_kv else []),
               compiler_params=_cp(("parallel",), 16 << 20))(q_src, k_src, cos, sin, R, *([kv_src, kv_src] if with_kv else []))


_SCALE = HD ** -0.5


def _attn_tile(qh, ks, vs, extra):
    ss = []
    for k, add in ks:
        s = _dg(qh, k, _DIMS["nt"]) * _SCALE
        ss.append(s if add is None else s + add)
    m = ss[0].max(axis=-1, keepdims=True)
    for s in ss[1:]:
        m = jnp.maximum(m, s.max(axis=-1, keepdims=True))
    if extra is not None:
        m = jnp.maximum(m, extra)
    ps = [jnp.exp(s - m) for s in ss]
    den = ps[0].sum(axis=-1, keepdims=True)
    for p in ps[1:]:
        den = den + p.sum(axis=-1, keepdims=True)
    if extra is not None:
        den = den + jnp.exp(extra - m)
    num = _dg(ps[0], vs[0], _DIMS["nn"])
    for p, v in zip(ps[1:], vs[1:]):
        num = num + _dg(p, v, _DIMS["nn"])
    linv = 1.0 / den
    return num * linv, m, linv


def _attn_bwd_tile(qh, ks, vs, extra, m, linv, oh, doh):
    delta = jnp.sum(doh * oh, axis=-1, keepdims=True)
    dq = None
    dks, dvs, dss = [], [], []
    for (k, add), v in zip(ks, vs):
        s = _dg(qh, k, _DIMS["nt"]) * _SCALE
        if add is not None:
            s = s + add
        p = jnp.exp(s - m) * linv
        dvs.append(_dg(p, doh, _DIMS["tn"]))
        ds = p * (_dg(doh, v, _DIMS["nt"]) - delta)
        dss.append(ds)
        dsq = ds * _SCALE
        part = _dg(dsq, k, _DIMS["nn"])
        dq = part if dq is None else dq + part
        dks.append(_dg(dsq, qh, _DIMS["tn"]))
    dextra = None
    if extra is not None:
        dextra = -jnp.sum(jnp.exp(extra - m) * linv * delta, axis=0, keepdims=True)
    return dq, dks, dvs, dss, dextra


def _wa_mask(n, L):
    qpos = n * WA_BLK + lax.broadcasted_iota(jnp.int32, (WA_BLK, 3 * WA_BLK), 0)
    kpos = (n - 1) * WA_BLK + lax.broadcasted_iota(jnp.int32, (WA_BLK, 3 * WA_BLK), 1)
    ok = (jnp.abs(qpos - kpos) <= WA_BLK) & (kpos >= 0) & (kpos < L)
    return jnp.where(ok, 0.0, NEG).astype(F32)


WA_BPS = 2


def _wa_specs(L, Lc):
    nb = L // WA_BLK
    cb = L // Lc

    def blk(j, col):
        return pl.BlockSpec((WA_BLK, 128), lambda s: (jnp.clip(s * WA_BPS - 1 + j, 0, nb - 1), col))

    vcol = C_VA // 128
    kspecs = [blk(j, 0) for j in range(WA_BPS + 2)] + [pl.BlockSpec((Lc, 128), lambda s: (cb, 0))]
    vspecs = [blk(j, vcol) for j in range(WA_BPS + 2)] + [pl.BlockSpec((Lc, 128), lambda s: (cb, vcol))]
    return nb, kspecs, vspecs


def win_attn_fwd(qr, kr, P, sink, L, Lc, name):
    T = L + Lc
    nb, kspecs, vspecs = _wa_specs(L, Lc)
    nk = WA_BPS + 2
    QB = WA_BPS * WA_BLK
    nlat = nb // WA_BPS

    def body(q_ref, *refs):
        kbs, kx, vbs, vx, s_ref, o_ref, st_ref = refs[:nk], refs[nk], refs[nk + 1:2 * nk + 1], refs[2 * nk + 1], refs[-3], refs[-2], refs[-1]
        s = pl.program_id(0)

        def put(qs, h, res):
            o, m, linv = res
            o_ref[qs, h * HD:(h + 1) * HD] = o.astype(o_ref.dtype)
            st_ref[qs, h:h + 1] = m
            st_ref[qs, WA_HEADS + h:WA_HEADS + h + 1] = linv

        @pl.when(s < nlat)
        def _():
            for b in range(WA_BPS):
                mask = _wa_mask(s * WA_BPS + b, L)
                qs = slice(b * WA_BLK, (b + 1) * WA_BLK)
                for g in range(WA_KV):
                    sl = slice(g * HD, (g + 1) * HD)
                    k3 = jnp.concatenate([kbs[b + j][:, sl] for j in range(3)], axis=0)
                    v3 = jnp.concatenate([vbs[b + j][:, sl] for j in range(3)], axis=0)
                    for r in range(2):
                        h = 2 * g + r
                        put(qs, h, _attn_tile(q_ref[qs, h * HD:(h + 1) * HD], [(k3, mask), (kx[:, sl], None)], [v3, vx[:, sl]], s_ref[h:h + 1, 0:1]))

        @pl.when(s >= nlat)
        def _():
            for h in range(WA_HEADS):
                sl = slice((h // 2) * HD, (h // 2 + 1) * HD)
                put(slice(None), h, _attn_tile(q_ref[:, h * HD:(h + 1) * HD], [(kx[:, sl], None)], [vx[:, sl]], s_ref[h:h + 1, 0:1]))

    qspec = pl.BlockSpec((QB, 256), lambda s: (s, 0))
    return _pc(body, name=name, grid=(T // QB,),
               in_specs=[qspec] + kspecs + vspecs + [pl.BlockSpec((8, 128), lambda s: (0, 0))],
               out_specs=[qspec, pl.BlockSpec((QB, 8), lambda s: (s, 0))], out_shape=[_sds((T, 256), BF16), _sds((T, 8), F32)],
               compiler_params=_cp(("arbitrary",), 32 << 20))(qr, *([kr] * (nk + 1)), *([P] * (nk + 1)), sink)


def win_attn_bwd(qr, kr, P, sink, do_src, o, stats, L, Lc, name):
    T = L + Lc
    nb, kspecs, vspecs = _wa_specs(L, Lc)
    nk = WA_BPS + 2
    QB = WA_BPS * WA_BLK
    nlat = nb // WA_BPS
    cx = WA_BLK + L

    def body(q_ref, *refs):
        kbs, kx, vbs, vx = refs[:nk], refs[nk], refs[nk + 1:2 * nk + 1], refs[2 * nk + 1]
        s_ref, do_ref, o_ref, st_ref, dq_ref, dk_ref, dv_ref, ds_ref = refs[2 * nk + 2:]
        s = pl.program_id(0)

        @pl.when(s == 0)
        def _():
            dk_ref[...] = jnp.zeros_like(dk_ref)
            dv_ref[...] = jnp.zeros_like(dv_ref)
            ds_ref[...] = jnp.zeros_like(ds_ref)

        def tile(qs, h, ks, vs):
            hs = slice(h * HD, (h + 1) * HD)
            dq, dks, dvs, _, dsk = _attn_bwd_tile(q_ref[qs, hs], ks, vs, s_ref[h:h + 1, 0:1], st_ref[qs, h:h + 1],
                                                  st_ref[qs, WA_HEADS + h:WA_HEADS + h + 1], o_ref[qs, hs].astype(F32), do_ref[qs, hs].astype(F32))
            dq_ref[qs, hs] = dq
            ds_ref[h:h + 1, :] += jnp.broadcast_to(dsk, (1, 128))
            return dks, dvs

        @pl.when(s < nlat)
        def _():
            for b in range(WA_BPS):
                n = s * WA_BPS + b
                mask = _wa_mask(n, L)
                rows = pl.ds(pl.multiple_of(n * WA_BLK, WA_BLK), 3 * WA_BLK)
                qs = slice(b * WA_BLK, (b + 1) * WA_BLK)
                for g in range(WA_KV):
                    sl = slice(g * HD, (g + 1) * HD)
                    k3 = jnp.concatenate([kbs[b + j][:, sl] for j in range(3)], axis=0)
                    v3 = jnp.concatenate([vbs[b + j][:, sl] for j in range(3)], axis=0)
                    acc = None
                    for r in range(2):
                        dks, dvs = tile(qs, 2 * g + r, [(k3, mask), (kx[:, sl], None)], [v3, vx[:, sl]])
                        acc = dks + dvs if acc is None else [a + b_ for a, b_ in zip(acc, dks + dvs)]
                    dk_ref[rows, sl] += acc[0]
                    dk_ref[cx:cx + Lc, sl] += acc[1]
                    dv_ref[rows, sl] += acc[2]
                    dv_ref[cx:cx + Lc, sl] += acc[3]

        @pl.when(s >= nlat)
        def _():
            for h in range(WA_HEADS):
                sl = slice((h // 2) * HD, (h // 2 + 1) * HD)
                dks, dvs = tile(slice(None), h, [(kx[:, sl], None)], [vx[:, sl]])
                dk_ref[cx:cx + Lc, sl] += dks[0]
                dv_ref[cx:cx + Lc, sl] += dvs[0]

    qspec = pl.BlockSpec((QB, 256), lambda s: (s, 0))
    acc_spec = pl.BlockSpec((T + 2 * WA_BLK, 128), lambda s: (0, 0))
    return _pc(body, name=name, grid=(T // QB,),
               in_specs=[qspec] + kspecs + vspecs + [pl.BlockSpec((8, 128), lambda s: (0, 0)), qspec, qspec, pl.BlockSpec((QB, 8), lambda s: (s, 0))],
               out_specs=[qspec, acc_spec, acc_spec, pl.BlockSpec((8, 128), lambda s: (0, 0))],
               out_shape=[_sds((T, 256), F32), _sds((T + 2 * WA_BLK, 128), F32), _sds((T + 2 * WA_BLK, 128), F32), _sds((8, 128), F32)],
               compiler_params=_cp(("arbitrary",), 40 << 20))(qr, *([kr] * (nk + 1)), *([P] * (nk + 1)), sink, do_src, o, stats)


def na_index_tables():
    qc = np.arange(GRID_W)[:, None]
    kc = np.arange(GRID_W)[None, :]
    cstart = np.clip(qc - NA_KW // 2, 0, GRID_W - NA_KW)
    ok = (kc >= cstart) & (kc < cstart + NA_KW)
    dx = np.clip(kc - qc, -(NA_KW - 1), NA_KW - 1) + (NA_KW - 1)
    off = np.arange(NA_KH)[:, None]
    kr = np.arange(NA_KH)[None, :]
    dy = kr - off + (NA_KH - 1)
    return ok, dx, dy


def _na_selectors():
    ok, dx, dy = na_index_tables()
    e1 = np.zeros((GRID_W * GRID_W, 128), np.float32)
    qi, ki = np.nonzero(ok)
    e1[qi * GRID_W + ki, dx[qi, ki]] = 1.0
    e2 = np.zeros((16, NA_KH * NA_KH), np.float32)
    oi, ri = np.meshgrid(np.arange(NA_KH), np.arange(NA_KH), indexing="ij")
    e2[dy[oi, ri].ravel(), (oi * NA_KH + ri).ravel()] = 1.0
    return ok, jnp.asarray(e1), jnp.asarray(np.kron(np.eye(NA_HEADS, dtype=np.float32), e2))


def na_bias_table(rpb, tag):
    ok, e1, e2 = _na_selectors()
    r2 = jnp.pad(rpb.astype(F32), ((0, 0), (0, 1), (0, 128 - (2 * NA_KW - 1)))).reshape(NA_HEADS * 16, 128)
    r1 = matmul(e2, r2, "tn", F32, f"na_bias_sel1_{tag}", hi=True)
    x = matmul(r1, e1, "nt", F32, f"na_bias_sel2_{tag}", hi=True)
    b = x.reshape(NA_HEADS, NA_KH, NA_KH, GRID_W, GRID_W).transpose(0, 1, 3, 2, 4)
    b = b + jnp.asarray(np.where(ok, 0.0, NEG).astype(np.float32))[None, None, :, None, :]
    return b.reshape(NA_HEADS, NA_KH, GRID_W, NA_KH * GRID_W)


def _na_rows(r, GR):
    r0 = jnp.clip(r - NA_KH // 2, 0, GR - NA_KH)
    return r0, jnp.clip(r - r0, 0, NA_KH - 1)


NA_RPS = 4


def na_fwd(P, kb, vb, bias, L, Lc, name):
    T = L + Lc
    GR = L // GRID_W
    W = NA_KH * GRID_W
    QB = GRID_W * NA_RPS
    nlat = GR // NA_RPS

    def body(q_ref, k_ref, v_ref, b_ref, o_ref, st_ref):
        s = pl.program_id(0)

        def put(qs, h, res):
            o, m, linv = res
            o_ref[qs, h * HD:(h + 1) * HD] = o.astype(o_ref.dtype)
            st_ref[qs, h:h + 1] = m
            st_ref[qs, NA_HEADS + h:NA_HEADS + h + 1] = linv

        @pl.when(s < nlat)
        def _():
            for rr in range(NA_RPS):
                r0, off = _na_rows(s * NA_RPS + rr, GR)
                rows = pl.ds(pl.multiple_of(r0 * GRID_W, GRID_W), W)
                qs = slice(rr * GRID_W, (rr + 1) * GRID_W)
                for h in range(NA_HEADS):
                    hs = slice(h * HD, (h + 1) * HD)
                    put(qs, h, _attn_tile(q_ref[qs, hs], [(k_ref[rows, hs], b_ref[h, off]), (k_ref[L:T, hs], None)],
                                          [v_ref[rows, hs], v_ref[L:T, hs]], None))

        @pl.when(s >= nlat)
        def _():
            for h in range(NA_HEADS):
                hs = slice(h * HD, (h + 1) * HD)
                put(slice(None), h, _attn_tile(q_ref[:, hs], [(k_ref[L:T, hs], None)], [v_ref[L:T, hs]], None))

    one = pl.Buffered(1)
    return _pc(body, name=name, grid=(T // QB,),
               in_specs=[pl.BlockSpec((QB, 256), lambda r: (r, C_QB // 256)),
                         pl.BlockSpec((T, 256), lambda r: (0, 0), pipeline_mode=one),
                         pl.BlockSpec((T, 256), lambda r: (0, 0), pipeline_mode=one),
                         pl.BlockSpec((NA_HEADS, NA_KH, GRID_W, W), lambda r: (0, 0, 0, 0), pipeline_mode=one)],
               out_specs=[pl.BlockSpec((QB, 256), lambda r: (r, 0)), pl.BlockSpec((QB, 8), lambda r: (r, 0))],
               out_shape=[_sds((T, 256), BF16), _sds((T, 8), F32)],
               compiler_params=_cp(("arbitrary",), 32 << 20))(P, kb, vb, bias)


def na_bwd(P, kb, vb, bias, do_src, o, stats, L, Lc, name):
    T = L + Lc
    GR = L // GRID_W
    W = NA_KH * GRID_W
    QB = GRID_W * NA_RPS
    nlat = GR // NA_RPS

    def body(q_ref, k_ref, v_ref, b_ref, do_ref, o_ref, st_ref, dq_ref, dk_ref, dv_ref, db_ref):
        s = pl.program_id(0)

        @pl.when(s == 0)
        def _():
            dk_ref[...] = jnp.zeros_like(dk_ref)
            dv_ref[...] = jnp.zeros_like(dv_ref)
            db_ref[...] = jnp.zeros_like(db_ref)

        def tile(qs, h, ks, vs):
            hs = slice(h * HD, (h + 1) * HD)
            dq, dks, dvs, dss, _ = _attn_bwd_tile(q_ref[qs, hs], ks, vs, None, st_ref[qs, h:h + 1], st_ref[qs, NA_HEADS + h:NA_HEADS + h + 1],
                                                  o_ref[qs, hs].astype(F32), do_ref[qs, hs].astype(F32))
            dq_ref[qs, hs] = dq.astype(dq_ref.dtype)
            return dks, dvs, dss

        @pl.when(s < nlat)
        def _():
            for rr in range(NA_RPS):
                r0, off = _na_rows(s * NA_RPS + rr, GR)
                rows = pl.ds(pl.multiple_of(r0 * GRID_W, GRID_W), W)
                qs = slice(rr * GRID_W, (rr + 1) * GRID_W)
                for h in range(NA_HEADS):
                    hs = slice(h * HD, (h + 1) * HD)
                    dks, dvs, dss = tile(qs, h, [(k_ref[rows, hs], b_ref[h, off]), (k_ref[L:T, hs], None)], [v_ref[rows, hs], v_ref[L:T, hs]])
                    dk_ref[rows, hs] += dks[0]
                    dv_ref[rows, hs] += dvs[0]
                    dk_ref[L:T, hs] += dks[1]
                    dv_ref[L:T, hs] += dvs[1]
                    db_ref[h, off] += dss[0]

        @pl.when(s >= nlat)
        def _():
            for h in range(NA_HEADS):
                hs = slice(h * HD, (h + 1) * HD)
                dks, dvs, _ = tile(slice(None), h, [(k_ref[L:T, hs], None)], [v_ref[L:T, hs]])
                dk_ref[L:T, hs] += dks[0]
                dv_ref[L:T, hs] += dvs[0]

    one = pl.Buffered(1)
    full = lambda shape: pl.BlockSpec(shape, lambda r: (0,) * len(shape), pipeline_mode=one)
    qspec = pl.BlockSpec((QB, 256), lambda r: (r, 0))
    return _pc(body, name=name, grid=(T // QB,),
               in_specs=[pl.BlockSpec((QB, 256), lambda r: (r, C_QB // 256)), full((T, 256)), full((T, 256)),
                         full((NA_HEADS, NA_KH, GRID_W, W)), pl.BlockSpec((QB, 256), lambda r: (r, 1)), qspec, pl.BlockSpec((QB, 8), lambda r: (r, 0))],
               out_specs=[qspec, full((T, 256)), full((T, 256)), full((NA_HEADS, NA_KH, GRID_W, W))],
               out_shape=[_sds((T, 256), BF16), _sds((T, 256), F32), _sds((T, 256), F32), _sds((NA_HEADS, NA_KH, GRID_W, W), F32)],
               compiler_params=_cp(("arbitrary",), 48 << 20))(P, kb, vb, bias, do_src, o, stats)


def na_rpb_grad(dbias, tag):
    _, e1, e2 = _na_selectors()
    x = dbias.reshape(NA_HEADS, NA_KH, GRID_W, NA_KH, GRID_W).transpose(0, 1, 3, 2, 4).reshape(NA_HEADS * NA_KH * NA_KH, GRID_W * GRID_W)
    r1 = matmul(x, e1, "nn", F32, f"na_rpb_sel1_{tag}", hi=True, tk=1024)
    r2 = matmul(e2, r1, "nn", F32, f"na_rpb_sel2_{tag}", hi=True)
    return r2.reshape(NA_HEADS, 16, 128)[:, :2 * NA_KH - 1, :2 * NA_KW - 1]


_HALO = 8


def _halo_specs(T, col0):
    nh = TR // _HALO
    cur = pl.BlockSpec((TR, 256), lambda i, j: (i, col0 + j))
    prv = pl.BlockSpec((_HALO, 256), lambda i, j: (jnp.maximum(i * nh - 1, 0), col0 + j))
    nxt = pl.BlockSpec((_HALO, 256), lambda i, j: (jnp.minimum((i + 1) * nh, T // _HALO - 1), col0 + j))
    return prv, cur, nxt


def _fill_ext(ext, prv, cur, nxt, i, nL, nT):
    has_prev = jnp.where((i != 0) & (i != nL), 1.0, 0.0)
    has_next = jnp.where((i != nL - 1) & (i != nT - 1), 1.0, 0.0)
    ext[0:_HALO, :] = prv[...].astype(F32) * has_prev
    ext[_HALO:_HALO + TR, :] = cur[...].astype(F32)
    ext[_HALO + TR:, :] = nxt[...].astype(F32) * has_next


def conv_silu_fwd(P, w8, b, nL, name):
    T = P.shape[0]
    nT = T // TR

    def body(prv, cur, nxt, w_ref, b_ref, pre_ref, act_ref, ext):
        i = pl.program_id(0)
        _fill_ext(ext, prv, cur, nxt, i, nL, nT)
        y = jnp.broadcast_to(b_ref[...], (TR, 256))
        for k in range(S_CONV):
            y = y + w_ref[k:k + 1, :] * ext[pl.ds(_HALO - S_CONV // 2 + k, TR), :]
        pre_ref[...] = y
        act_ref[...] = _silu(y)

    prv, cur, nxt = _halo_specs(T, C_XBC // 256)
    out = pl.BlockSpec((TR, 256), lambda i, j: (i, j))
    return _pc(body, name=name, grid=(nT, 4),
               in_specs=[prv, cur, nxt, pl.BlockSpec((8, 256), lambda i, j: (0, j)), pl.BlockSpec((1, 256), lambda i, j: (0, j))],
               out_specs=[out, out], out_shape=[_sds((T, 1024), F32), _sds((T, 1024), F32)],
               scratch_shapes=[pltpu.VMEM((TR + 2 * _HALO, 256), F32)],
               compiler_params=_cp(("parallel", "parallel"), 16 << 20))(P, P, P, w8, b)


def dsilu(pre, dxs_list, db_list, dc_list, name):
    T = pre.shape[0]
    n1, n2, n3 = len(dxs_list), len(db_list), len(dc_list)

    def body(*refs):
        pre_ref = refs[0]
        ins = refs[1:1 + n1 + n2 + n3]
        out = refs[-1]

        def part(rs, lo, hi):
            g = rs[0][...].astype(F32)
            for r in rs[1:]:
                g = g + r[...].astype(F32)
            x = pre_ref[:, lo:hi]
            sg = 1.0 / (1.0 + jnp.exp(-x))
            sl = x * sg
            out[:, lo:hi] = g * (sg + sl * (1.0 - sg))

        part(ins[:n1], 0, 512)
        part(ins[n1:n1 + n2], 512, 768)
        part(ins[n1 + n2:], 768, 1024)

    spec = lambda w: pl.BlockSpec((TR, w), lambda i: (i, 0))
    return _pc(body, name=name, grid=(T // TR,),
               in_specs=[spec(1024)] + [spec(512)] * n1 + [spec(256)] * (n2 + n3),
               out_specs=spec(1024), out_shape=_sds((T, 1024), F32),
               compiler_params=_cp(("parallel",), 32 << 20))(pre, *dxs_list, *db_list, *dc_list)


def conv_bwd(dpre, P, w8, nL, name):
    T = P.shape[0]
    nT = T // TR

    def body(dp, dc, dn, xp, xc, xn, w_ref, dx_ref, dw_ref, db_ref, extd, extx):
        i = pl.program_id(1)
        _fill_ext(extd, dp, dc, dn, i, nL, nT)
        _fill_ext(extx, xp, xc, xn, i, nL, nT)

        @pl.when(i == 0)
        def _():
            dw_ref[...] = jnp.zeros_like(dw_ref)
            db_ref[...] = jnp.zeros_like(db_ref)

        d = dc[...]
        dx = jnp.zeros((TR, 256), F32)
        for k in range(S_CONV):
            dx = dx + w_ref[k:k + 1, :] * extd[pl.ds(_HALO + S_CONV // 2 - k, TR), :]
            dw_ref[k:k + 1, :] += jnp.sum(d * extx[pl.ds(_HALO - S_CONV // 2 + k, TR), :], axis=0, keepdims=True)
        dx_ref[...] = dx.astype(dx_ref.dtype)
        db_ref[0:1, :] += jnp.sum(d, axis=0, keepdims=True)

    def swap(spec):
        f = spec.index_map
        return pl.BlockSpec(spec.block_shape, lambda j, i: f(i, j))

    dprv, dcur, dnxt = [swap(s) for s in _halo_specs(T, 0)]
    xprv, xcur, xnxt = [swap(s) for s in _halo_specs(T, C_XBC // 256)]
    acc = pl.BlockSpec((8, 256), lambda j, i: (0, j))
    return _pc(body, name=name, grid=(4, nT),
               in_specs=[dprv, dcur, dnxt, xprv, xcur, xnxt, acc],
               out_specs=[pl.BlockSpec((TR, 256), lambda j, i: (i, j)), acc, acc],
               out_shape=[_sds((T, 1024), BF16), _sds((8, 1024), F32), _sds((8, 1024), F32)],
               scratch_shapes=[pltpu.VMEM((TR + 2 * _HALO, 256), F32), pltpu.VMEM((TR + 2 * _HALO, 256), F32)],
               compiler_params=_cp(("parallel", "arbitrary"), 16 << 20))(dpre, dpre, dpre, P, P, P, w8)


def _onehot_row(h, n):
    return (lax.broadcasted_iota(jnp.int32, (1, n), 1) == h).astype(F32)


def _onehot_col(h, n):
    return (lax.broadcasted_iota(jnp.int32, (n, 1), 0) == h).astype(F32)


def _ssd_chunk(xs, dtr, dtb, alog, bm, cm, hin, reverse):
    Qn = S_Q
    ii = lax.broadcasted_iota(jnp.int32, (Qn, Qn), 0)
    jj = lax.broadcasted_iota(jnp.int32, (Qn, Qn), 1)
    keep = (ii <= jj) if reverse else (ii >= jj)
    tri = keep.astype(F32)
    triT = ((jj <= ii) if reverse else (jj >= ii)).astype(F32)
    eye = (ii == jj).astype(F32)
    dt = _softplus(dtr + dtb)
    a = dt * (-jnp.exp(alog))
    cs = hdot(tri, a)
    csT = hdot(a, triT, "tn")
    dtT = hdot(dt, eye, "tn")
    last = _onehot_row(0 if reverse else Qn - 1, Qn)
    ys, houts = [], []
    for g in range(S_GROUPS):
        G = bdot(cm[g], bm[g], "nt")
        for r in range(S_HEADS // S_GROUPS):
            h = g * (S_HEADS // S_GROUPS) + r
            eh_r, eh_c = _onehot_row(h, S_HEADS), _onehot_col(h, S_HEADS)
            cs_c = jnp.sum(cs * eh_r, axis=1, keepdims=True)
            dt_c = jnp.sum(dt * eh_r, axis=1, keepdims=True)
            cs_r = jnp.sum(csT * eh_c, axis=0, keepdims=True)
            dt_r = jnp.sum(dtT * eh_c, axis=0, keepdims=True)
            tot = jnp.sum(cs_r * last, axis=1, keepdims=True)
            decay = jnp.exp(jnp.where(keep, cs_c - cs_r, NEG))
            w = G * decay * dt_r
            y = bdot(w, xs[h], "nn") + bdot(cm[g], hin[h], "nt") * jnp.exp(cs_c)
            xsc = xs[h] * (jnp.exp(tot - cs_c) * dt_c)
            hout = hin[h] * jnp.exp(tot) + bdot(xsc, bm[g], "tn")
            ys.append(y)
            houts.append(hout)
    return ys, houts


def _ssd_orders(L, Lc):
    nl, ncx = L // S_Q, Lc // S_Q
    fwd = lambda s: jnp.where(s < ncx, nl + s, s - ncx)
    bwd = lambda s: nl + ncx - 1 - s
    return nl + ncx, fwd, bwd


def _ssd_in_specs(fo, bo, step):
    def at(order, w, col):
        return pl.BlockSpec((S_Q, w), lambda u: (order(step(u)), col))
    specs = []
    for order in (fo, bo):
        specs += [at(order, 512, 0), at(order, 256, 2), at(order, 256, 3), at(order, 128, C_DT // 128)]
    return specs


def ssd_fwd(act, P, dtb, alog, L, Lc, name):
    T = L + Lc
    ns, fo, bo = _ssd_orders(L, Lc)

    def body(xf, bf, cf, df, xb, bb, cb, db, dtb_ref, al_ref, yf, yb, hsf, hsb, Hf, Hb):
        s = pl.program_id(0)

        @pl.when(s == 0)
        def _():
            Hf[...] = jnp.zeros_like(Hf)
            Hb[...] = jnp.zeros_like(Hb)

        for d, (x_r, b_r, c_r, dt_r, y_r, hs_r, H) in enumerate(((xf, bf, cf, df, yf, hsf, Hf), (xb, bb, cb, db, yb, hsb, Hb))):
            hin = [H[h] for h in range(S_HEADS)]
            hs_r[0] = H[...]
            ys, houts = _ssd_chunk(
                [x_r[:, h * S_P:(h + 1) * S_P] for h in range(S_HEADS)], dt_r[:, d * 8:(d + 1) * 8],
                dtb_ref[d:d + 1, 0:8], al_ref[d:d + 1, 0:8],
                [b_r[:, g * S_N:(g + 1) * S_N] for g in range(S_GROUPS)], [c_r[:, g * S_N:(g + 1) * S_N] for g in range(S_GROUPS)],
                hin, reverse=(d == 1))
            for h in range(S_HEADS):
                y_r[:, h * S_P:(h + 1) * S_P] = ys[h]
                H[h] = houts[h]

    ident = lambda u: u
    small = pl.BlockSpec((8, 128), lambda u: (0, 0))
    hspec = pl.BlockSpec((1, S_HEADS, S_P, S_N), lambda u: (u, 0, 0, 0))
    return _pc(body, name=name, grid=(ns,),
               in_specs=_ssd_in_specs(fo, bo, ident) + [small, small],
               out_specs=[pl.BlockSpec((S_Q, 512), lambda u: (fo(u), 0)), pl.BlockSpec((S_Q, 512), lambda u: (bo(u), 0)), hspec, hspec],
               out_shape=[_sds((T, 512), F32), _sds((T, 512), F32), _sds((ns, S_HEADS, S_P, S_N), F32), _sds((ns, S_HEADS, S_P, S_N), F32)],
               scratch_shapes=[pltpu.VMEM((S_HEADS, S_P, S_N), F32), pltpu.VMEM((S_HEADS, S_P, S_N), F32)],
               compiler_params=_cp(("arbitrary",), 32 << 20))(act, act, act, P, act, act, act, P, dtb, alog)


def ssd_bwd(act, P, dtb, alog, hsf, hsb, dy, L, Lc, name):
    T = L + Lc
    ns, fo, bo = _ssd_orders(L, Lc)
    step = lambda u: ns - 1 - u

    def body(xf, bf, cf, df, xb, bb, cb, db, dtb_ref, al_ref, hsf_r, hsb_r, dyf, dyb,
             dxf, dbf, dcf, ddf, dxb, dbb, dcb, ddb, ddtb, dal, dHf, dHb):
        u = pl.program_id(0)

        @pl.when(u == 0)
        def _():
            dHf[...] = jnp.zeros_like(dHf)
            dHb[...] = jnp.zeros_like(dHb)
            ddtb[...] = jnp.zeros_like(ddtb)
            dal[...] = jnp.zeros_like(dal)

        dirs = ((xf, bf, cf, df, hsf_r, dyf, dxf, dbf, dcf, ddf, dHf), (xb, bb, cb, db, hsb_r, dyb, dxb, dbb, dcb, ddb, dHb))
        for d, (x_r, b_r, c_r, dt_r, hs_r, dy_r, dx_o, db_o, dc_o, dd_o, dH) in enumerate(dirs):
            f = functools.partial(_ssd_chunk, reverse=(d == 1))
            _, vjp = jax.vjp(
                f, [x_r[:, h * S_P:(h + 1) * S_P] for h in range(S_HEADS)], dt_r[:, d * 8:(d + 1) * 8],
                dtb_ref[d:d + 1, 0:8], al_ref[d:d + 1, 0:8],
                [b_r[:, g * S_N:(g + 1) * S_N] for g in range(S_GROUPS)], [c_r[:, g * S_N:(g + 1) * S_N] for g in range(S_GROUPS)],
                [hs_r[0, h] for h in range(S_HEADS)])
            gx, gdt, gdtb, gal, gb, gc, gh = vjp(([dy_r[:, h * S_P:(h + 1) * S_P] for h in range(S_HEADS)],
                                                  [dH[h] for h in range(S_HEADS)]))
            for h in range(S_HEADS):
                dx_o[:, h * S_P:(h + 1) * S_P] = gx[h]
                dH[h] = gh[h]
            for g in range(S_GROUPS):
                db_o[:, g * S_N:(g + 1) * S_N] = gb[g]
                dc_o[:, g * S_N:(g + 1) * S_N] = gc[g]
            dd_o[...] = gdt
            ddtb[d:d + 1, 0:8] += gdtb
            dal[d:d + 1, 0:8] += gal

    small = pl.BlockSpec((8, 128), lambda u: (0, 0))
    hspec = pl.BlockSpec((1, S_HEADS, S_P, S_N), lambda u: (step(u), 0, 0, 0))
    at = lambda order, w: pl.BlockSpec((S_Q, w), lambda u: (order(step(u)), 0))
    outs = []
    for order in (fo, bo):
        outs += [at(order, 512), at(order, 256), at(order, 256), at(order, 8)]
    oshape = [_sds((T, 512), F32), _sds((T, 256), F32), _sds((T, 256), F32), _sds((T, 8), F32)]
    return _pc(body, name=name, grid=(ns,),
               in_specs=_ssd_in_specs(fo, bo, step) + [small, small, hspec, hspec, at(fo, 512), at(bo, 512)],
               out_specs=outs + [small, small], out_shape=oshape + oshape + [_sds((8, 128), F32), _sds((8, 128), F32)],
               scratch_shapes=[pltpu.VMEM((S_HEADS, S_P, S_N), F32), pltpu.VMEM((S_HEADS, S_P, S_N), F32)],
               compiler_params=_cp(("arbitrary",), 40 << 20))(act, act, act, P, act, act, act, P, dtb, alog, hsf, hsb, dy, dy)


def _ssm_out(yf, yb, xs, z, dskip, g):
    y = (yf + yb + dskip * xs) * _silu(z)
    return (y * lax.rsqrt(jnp.mean(y * y, axis=-1, keepdims=True) + EPS)) * g


def ssm_out_fwd(yf, yb, act, P, dskip, g, name):
    T = yf.shape[0]

    def body(yf_r, yb_r, xs_r, z_r, d_r, g_r, o_r):
        o_r[...] = _ssm_out(yf_r[...], yb_r[...], xs_r[...], z_r[...], d_r[...], g_r[...]).astype(o_r.dtype)

    row = pl.BlockSpec((TR, 512), lambda i: (i, 0))
    vec = pl.BlockSpec((1, 512), lambda i: (0, 0))
    return _pc(body, name=name, grid=(T // TR,),
               in_specs=[row, row, row, pl.BlockSpec((TR, 512), lambda i: (i, C_Z // 512)), vec, vec],
               out_specs=row, out_shape=_sds((T, 512), BF16),
               compiler_params=_cp(("parallel",), 16 << 20))(yf, yb, act, P, dskip, g)


def ssm_out_bwd(yf, yb, act, P, dskip, g, do_src, name):
    T = yf.shape[0]

    def body(yf_r, yb_r, xs_r, z_r, d_r, g_r, do_r, dy_r, dxs_r, dz_r, dv_r):
        @pl.when(pl.program_id(0) == 0)
        def _():
            dv_r[...] = jnp.zeros_like(dv_r)

        _, vjp = jax.vjp(_ssm_out, yf_r[...], yb_r[...], xs_r[...], z_r[...], d_r[...], g_r[...])
        dyf, _, dxs, dz, dd, dg = vjp(do_r[...].astype(F32))
        dy_r[...] = dyf
        dxs_r[...] = dxs
        dz_r[...] = dz.astype(dz_r.dtype)
        dv_r[0:1, :] += dd
        dv_r[1:2, :] += dg

    row = pl.BlockSpec((TR, 512), lambda i: (i, 0))
    vec = pl.BlockSpec((1, 512), lambda i: (0, 0))
    return _pc(body, name=name, grid=(T // TR,),
               in_specs=[row, row, row, pl.BlockSpec((TR, 512), lambda i: (i, C_Z // 512)), vec, vec,
                         pl.BlockSpec((TR, 512), lambda i: (i, 1))],
               out_specs=[row, row, row, pl.BlockSpec((8, 512), lambda i: (0, 0))],
               out_shape=[_sds((T, 512), F32), _sds((T, 512), F32), _sds((T, 512), BF16), _sds((8, 512), F32)],
               compiler_params=_cp(("arbitrary",), 24 << 20))(yf, yb, act, P, dskip, g, do_src)


def add_halves(xv, got, cvec, name):
    n, r, cdim = xv.shape
    h = r // 2

    def body(c_ref, x_ref, g_ref, o_ref):
        o_ref[...] = (x_ref[...].astype(F32) + g_ref[...].astype(F32)).astype(o_ref.dtype)

    gs = pltpu.PrefetchScalarGridSpec(
        num_scalar_prefetch=1, grid=(n,),
        in_specs=[pl.BlockSpec((1, h, cdim), lambda k, c_ref: (k, c_ref[0], 0)), pl.BlockSpec((1, h, cdim), lambda k, c_ref: (k, 0, 0))],
        out_specs=pl.BlockSpec((1, h, cdim), lambda k, c_ref: (k, 0, 0)))
    return _pc(body, name=name, grid_spec=gs, out_shape=_sds((n, h, cdim), BF16),
               compiler_params=_cp(("arbitrary",), 24 << 20))(cvec, xv, got)


def sum_slots(a, name):
    n, r, cdim = a.shape
    tr = _div_tile(r, 512, 16)

    def body(a_ref, o_ref):
        acc = a_ref[0].astype(F32)
        for k in range(1, n):
            acc = acc + a_ref[k].astype(F32)
        o_ref[...] = acc

    return _pc(body, name=name, grid=(r // tr,), in_specs=[pl.BlockSpec((n, tr, cdim), lambda i: (0, i, 0))],
               out_specs=pl.BlockSpec((tr, cdim), lambda i: (i, 0)), out_shape=_sds((r, cdim), F32),
               compiler_params=_cp(("parallel",), 32 << 20))(a)


def adamw(w, g, m, v, name):
    B, R, C = w.shape
    tr = _div_tile(R, max(8, (1 << 19) // max(C, 1) // 8 * 8), 8) if R % 8 == 0 else R
    c1 = 1.0 / (1.0 - ADAM_B1 ** ADAM_STEP)
    c2 = 1.0 / (1.0 - ADAM_B2 ** ADAM_STEP)

    def body(w_ref, g_ref, m_ref, v_ref, d_ref, mo_ref, vo_ref):
        gg = g_ref[...]
        mn = ADAM_B1 * m_ref[...] + (1.0 - ADAM_B1) * gg
        vn = ADAM_B2 * v_ref[...] + (1.0 - ADAM_B2) * (gg * gg)
        d_ref[...] = -ADAM_LR * ((mn * c1) / (jnp.sqrt(vn * c2) + ADAM_EPS) + ADAM_WD * w_ref[...])
        mo_ref[...] = mn
        vo_ref[...] = vn

    spec = pl.BlockSpec((1, tr, C), lambda b, i: (b, i, 0))
    return _pc(body, name=name, grid=(B, R // tr), in_specs=[spec] * 4, out_specs=[spec] * 3,
               out_shape=[_sds((B, R, C), F32)] * 3, compiler_params=_cp(("parallel", "parallel"), 32 << 20))(w, g, m, v)


def _me():
    return lax.axis_index("x"), lax.axis_index("y"), lax.axis_index("c")


def _flip(v, bit):
    return 1 - v if bit else v


def allgather8(xv, name):
    R = xv.shape[0]

    def body(x_ref, out_ref, sum_ref, send_sems, recv_sems):
        mx, my, mc = _me()
        me = 4 * mx + 2 * my + mc
        out_ref[me] = x_ref[...]
        sends, recvs = [], []
        for k in range(1, 8):
            px, py, pc = _flip(mx, k & 4), _flip(my, k & 2), _flip(mc, k & 1)
            peer = 4 * px + 2 * py + pc
            sends.append(pltpu.make_async_remote_copy(src_ref=x_ref, dst_ref=out_ref.at[me], send_sem=send_sems.at[k - 1],
                                                      recv_sem=recv_sems.at[k - 1], device_id=(px, py, pc), device_id_type=MESH))
            recvs.append(pltpu.make_async_remote_copy(src_ref=x_ref, dst_ref=out_ref.at[peer], send_sem=send_sems.at[k - 1],
                                                      recv_sem=recv_sems.at[k - 1], device_id=(px, py, pc), device_id_type=MESH))
        for cp in sends:
            cp.start()
        for cp in recvs:
            cp.wait_recv()
        for cp in sends:
            cp.wait_send()
        acc = out_ref[0]
        for d in range(1, 8):
            acc = acc + out_ref[d]
        sum_ref[...] = acc

    vm = pl.BlockSpec(memory_space=pltpu.VMEM)
    return _pc(body, name=name, pin=False, in_specs=[vm], out_specs=[vm, vm], out_shape=[_sds((8, R, 128), F32), _sds((R, 128), F32)],
               scratch_shapes=[pltpu.SemaphoreType.DMA((7,)), pltpu.SemaphoreType.DMA((7,))],
               compiler_params=_cp(None, 32 << 20))(xv)


def _other_chips(mx, my):
    return [(1 - mx, my), (mx, 1 - my), (1 - mx, 1 - my)]


def _halves(r, mc, mult):
    h = r // 2
    return pl.ds(pl.multiple_of(mc * h, mult), h), pl.ds(pl.multiple_of((1 - mc) * h, mult), h)


def _rcopy(src, dst, send_sems, recv_sems, k, to):
    return pltpu.make_async_remote_copy(src_ref=src, dst_ref=dst, send_sem=send_sems.at[k], recv_sem=recv_sems.at[k],
                                        device_id=to, device_id_type=MESH)


def _gather_body(xs, outs, send_sems, recv_sems):
    n = len(xs)
    mx, my, mc = _me()
    chip = 2 * mx + my
    sib = (mx, my, 1 - mc)
    chips = _other_chips(mx, my)
    idx = [2 * cx + cy for cx, cy in chips]
    cp = functools.partial(_rcopy, send_sems=send_sems, recv_sems=recv_sems)
    hv = [_halves(x.shape[0], mc, 16) for x in xs]
    first, passed = [], []
    for a in range(n):
        for j, (cx, cy) in enumerate(chips):
            first.append(cp(xs[a].at[hv[a][0]], outs[a].at[chip, hv[a][0]], k=6 * a + j, to=(cx, cy, mc)))
            first[-1].start()
    for a in range(n):
        for j in range(3):
            cp(xs[a].at[hv[a][0]], outs[a].at[idx[j], hv[a][0]], k=6 * a + j, to=sib).wait_recv()
            passed.append(cp(outs[a].at[idx[j], hv[a][0]], outs[a].at[idx[j], hv[a][0]], k=6 * a + 3 + j, to=sib))
            passed[-1].start()
    for a in range(n):
        for j in range(3):
            cp(xs[a].at[hv[a][1]], outs[a].at[idx[j], hv[a][1]], k=6 * a + 3 + j, to=sib).wait_recv()
    for c_ in first + passed:
        c_.wait_send()


def _my_chip():
    return 2 * lax.axis_index("x") + lax.axis_index("y")


def _own_slots(outs, shards):
    return [lax.dynamic_update_index_in_dim(o, x, _my_chip(), 0) for o, x in zip(outs, shards)]


def gather_weights(shards, name):
    n = len(shards)

    def body(*refs):
        _gather_body(refs[:n], refs[n:2 * n], *refs[2 * n:])

    hbm = pl.BlockSpec(memory_space=pl.ANY)
    outs = _pc(body, name=name, in_specs=[hbm] * n, out_specs=[hbm] * n, out_shape=[_sds((4,) + x.shape, x.dtype) for x in shards],
               scratch_shapes=[pltpu.SemaphoreType.DMA((6 * n,)), pltpu.SemaphoreType.DMA((6 * n,))])(*shards)
    return _own_slots(outs, shards)


GATHER_REST_ID = 3


def gather_weights_sc(shards, name):
    n = len(shards)
    x_refs = [jax.new_ref(x, memory_space=pltpu.MemorySpace.HBM) for x in shards]
    out_refs = [jax.empty_ref(_sds((4,) + x.shape, x.dtype), memory_space=pltpu.MemorySpace.HBM) for x in shards]

    @pl.kernel(mesh=plsc.ScalarSubcoreMesh(axis_name="sc", num_cores=1), name=name,
               scratch_types=(pltpu.SemaphoreType.DMA((6 * n,)), pltpu.SemaphoreType.DMA((6 * n,))),
               compiler_params=pltpu.CompilerParams(collective_id=GATHER_REST_ID))
    def launch(send_sems, recv_sems):
        mx, my, mc = _me()
        barrier = pltpu.get_barrier_semaphore()
        for peer in [(mx, my, 1 - mc)] + [(cx, cy, mc) for cx, cy in _other_chips(mx, my)]:
            pl.semaphore_signal(barrier, inc=1, device_id=peer, device_id_type=MESH)
        pl.semaphore_wait(barrier, 4)
        _gather_body(x_refs, out_refs, send_sems, recv_sems)

    launch()
    return _own_slots([o[...] for o in out_refs], shards)


def swap_halves(arrs, name):
    n = len(arrs)

    def body(*refs):
        xs, outs = refs[:n], refs[n:2 * n]
        send_sems, recv_sems = refs[2 * n:]
        mx, my, mc = _me()
        cps = []
        for a in range(n):
            theirs = _halves(xs[a].shape[1], mc, 16)[1]
            cps.append(_rcopy(xs[a].at[pl.ds(0, 4), theirs], outs[a], send_sems, recv_sems, a, (mx, my, 1 - mc)))
            cps[-1].start()
        for c_ in cps:
            c_.wait()

    hbm = pl.BlockSpec(memory_space=pl.ANY)
    return _pc(body, name=name, in_specs=[hbm] * n, out_specs=[hbm] * n,
               out_shape=[_sds((4, x.shape[1] // 2, x.shape[2]), x.dtype) for x in arrs],
               scratch_shapes=[pltpu.SemaphoreType.DMA((n,)), pltpu.SemaphoreType.DMA((n,))])(*arrs)


SCATTER_ID = 4


def scatter_chips_sc(arrs, name):
    n = len(arrs)
    x_refs = [jax.new_ref(x, memory_space=pltpu.MemorySpace.HBM) for x in arrs]
    out_refs = [jax.empty_ref(_sds(x.shape, x.dtype), memory_space=pltpu.MemorySpace.HBM) for x in arrs]

    @pl.kernel(mesh=plsc.ScalarSubcoreMesh(axis_name="sc", num_cores=1), name=name,
               scratch_types=(pltpu.SemaphoreType.DMA((3 * n,)), pltpu.SemaphoreType.DMA((3 * n,))),
               compiler_params=pltpu.CompilerParams(collective_id=SCATTER_ID))
    def launch(send_sems, recv_sems):
        mx, my, mc = _me()
        chip = 2 * mx + my
        chips = _other_chips(mx, my)
        idx = [2 * cx + cy for cx, cy in chips]
        barrier = pltpu.get_barrier_semaphore()
        for cx, cy in chips:
            pl.semaphore_signal(barrier, inc=1, device_id=(cx, cy, mc), device_id_type=MESH)
        pl.semaphore_wait(barrier, 3)
        cp = functools.partial(_rcopy, send_sems=send_sems, recv_sems=recv_sems)
        sends = []
        for a in range(n):
            for j, (cx, cy) in enumerate(chips):
                sends.append(cp(x_refs[a].at[idx[j]], out_refs[a].at[chip], k=3 * a + j, to=(cx, cy, mc)))
                sends[-1].start()
        for a in range(n):
            for j, (cx, cy) in enumerate(chips):
                cp(x_refs[a].at[idx[j]], out_refs[a].at[idx[j]], k=3 * a + j, to=(cx, cy, mc)).wait_recv()
        for c_ in sends:
            c_.wait_send()

    launch()
    return _own_slots([o[...] for o in out_refs], [lax.dynamic_index_in_dim(x, _my_chip(), 0, keepdims=False) for x in arrs])


def share_halves(parts, name):
    flat = [p for w in parts for p in w]
    nw, n = len(parts), len(flat)
    depth = n // nw

    def body(*refs):
        xs, outs = refs[:n], refs[n:n + nw]
        send_sems, recv_sems = refs[n + nw:]
        mx, my, mc = _me()
        sib = (mx, my, 1 - mc)
        sends, recvs = [], []
        for a in range(n):
            w, l = a // depth, a % depth
            mine, theirs = _halves(outs[w].shape[1], mc, 8)
            sends.append(_rcopy(xs[a], outs[w].at[l, mine], send_sems, recv_sems, a, sib))
            recvs.append(_rcopy(xs[a], outs[w].at[l, theirs], send_sems, recv_sems, a, sib))
            sends[-1].start()
        for c_ in recvs:
            c_.wait_recv()
        for c_ in sends:
            c_.wait_send()

    hbm = pl.BlockSpec(memory_space=pl.ANY)
    outs = _pc(body, name=name, in_specs=[hbm] * n, out_specs=[hbm] * nw,
               out_shape=[_sds((depth, 2 * w[0].shape[0], w[0].shape[1]), F32) for w in parts],
               scratch_shapes=[pltpu.SemaphoreType.DMA((n,)), pltpu.SemaphoreType.DMA((n,))])(*flat)
    outs = list(outs)
    mc = lax.axis_index("c")
    for w in range(nw):
        for l in range(depth):
            h = parts[w][l].shape[0]
            outs[w] = lax.dynamic_update_slice(outs[w], parts[w][l][None], (l, mc * h, 0))
    return outs


_BIG = ("w_in", "w_out", "w_ffn_in", "w_ffn_out")
N_CHIPS = 4
DEPTH = 2


def _pad_rows(v, mult=8):
    n = v.shape[0]
    rows = -(-n // 128)
    rows = -(-rows // mult) * mult
    return jnp.pad(v, (0, rows * 128 - n)).reshape(rows, 128)


class _Flat:
    def __init__(self):
        self.items = []

    def add(self, name, a):
        self.items.append((name, a.shape, a.reshape(-1).astype(F32)))

    def rows(self):
        return _pad_rows(jnp.concatenate([a for _, _, a in self.items]))

    def split(self, rows):
        flat = rows.reshape(-1)
        out, o = {}, 0
        for name, shape, a in self.items:
            out[name] = flat[o:o + a.shape[0]].reshape(shape)
            o += a.shape[0]
        return out

    def split_lead(self, rows3):
        n = rows3.shape[0]
        flat = rows3.reshape(n, -1)
        out, o = {}, 0
        for name, shape, a in self.items:
            out[name] = flat[:, o:o + a.shape[0]].reshape((n,) + tuple(shape))
            o += a.shape[0]
        return out


def _gsv(rows):
    z = jnp.zeros((2, D), F32)
    r = [z if a is None else a for a in rows] + [z] * 5
    return jnp.stack(r, axis=1)


def _pad8(a, rows=8, cols=128):
    return jnp.zeros((rows, cols), F32).at[:a.shape[0], :a.shape[1]].set(a.astype(F32))


def kernel(x, c, ctx, c_ctx, w_mod, b_mod, g_mix, w_in, wa_sink, na_rpb, ssm_conv_w, ssm_conv_b, ssm_dt_bias, ssm_a_log, ssm_d, ssm_norm_g, w_out, g_ffn, w_ffn_in, w_ffn_out, g_final, loss_target, m_c_ctx, m_w_mod, m_b_mod, m_g_mix, m_w_in, m_wa_sink, m_na_rpb, m_ssm_conv_w, m_ssm_conv_b, m_ssm_dt_bias, m_ssm_a_log, m_ssm_d, m_ssm_norm_g, m_w_out, m_g_ffn, m_w_ffn_in, m_w_ffn_out, m_g_final, v_c_ctx, v_w_mod, v_b_mod, v_g_mix, v_w_in, v_wa_sink, v_na_rpb, v_ssm_conv_w, v_ssm_conv_b, v_ssm_dt_bias, v_ssm_a_log, v_ssm_d, v_ssm_norm_g, v_w_out, v_g_ffn, v_w_ffn_in, v_w_ffn_out, v_g_final):
    L, Lc = x.shape[1], ctx.shape[1]
    T = L + Lc
    nL = L // TR
    mx, my, mc = lax.axis_index("x"), lax.axis_index("y"), lax.axis_index("c")
    dev = 4 * mx + 2 * my + mc
    chip = 2 * mx + my
    MODW = 6 * D // N_CHIPS
    CW = 1024 // N_CHIPS

    sc = _silu(c.astype(F32))
    scc = _silu(c_ctx.astype(F32))[None]
    f1 = _Flat()
    f1.add("sc", sc)
    f1.add("conv_w", ssm_conv_w)
    g1, _ = allgather8(f1.rows(), "gather_cond")
    g1 = f1.split_lead(g1)
    sc_all = g1["sc"][:, 0]
    conv_w = jnp.concatenate([g1["conv_w"][2 * k] for k in range(N_CHIPS)], axis=-1)
    A16 = jnp.concatenate([sc_all, scc, jnp.zeros((7, D), F32)], axis=0)

    mod_part = jnp.stack([matmul(A16, w_mod[l], "nn", F32, f"mod_fwd{l}") for l in range(DEPTH)])
    f2 = _Flat()
    f2.add("mod", mod_part)
    g2, _ = allgather8(f2.rows(), "gather_mod")
    g2 = f2.split_lead(g2)["mod"]
    mods = jnp.concatenate([g2[2 * k] for k in range(N_CHIPS)], axis=-1) + b_mod[:, None, :]
    mod_l = lax.dynamic_index_in_dim(mods, dev, axis=1, keepdims=False).reshape(DEPTH, 6, D)
    mod_c = mods[:, 8].reshape(DEPTH, 6, D)
    mod = jnp.stack([mod_l, mod_c], axis=1)
    mrow = lambda l, j: mod[l, :, j]

    own = {"w_in": w_in, "w_out": w_out, "w_ffn_in": w_ffn_in, "w_ffn_out": w_ffn_out}
    sh16 = [own[n][l].astype(BF16) for n in _BIG for l in range(DEPTH)]
    gath = list(gather_weights(sh16[:1], "gather_first"))
    after_first = (gath[0][0, 0, 0] * 0).astype(BF16)
    gath += list(gather_weights_sc([sh16[1] + after_first] + sh16[2:], "gather_rest"))
    gw = {n: [gath[DEPTH * i + l] for l in range(DEPTH)] for i, n in enumerate(_BIG)}
    W_in = [jnp.pad(jnp.concatenate([g[k] for k in range(N_CHIPS)], axis=1), ((0, 0), (0, IN_PAD - IN_COLS))) for g in gw["w_in"]]
    W_out = [g.reshape(D, D) for g in gw["w_out"]]
    W_fo = [g.reshape(D_FF, D) for g in gw["w_ffn_out"]]
    W_fi = gw["w_ffn_in"]

    cos, sin, rotm = rope_tables(L, Lc)
    x0 = jnp.concatenate([x[0], ctx[0]], axis=0).astype(F32)

    sv = []
    xin = x0
    gsv_first = _gsv([None, mrow(0, 0), mrow(0, 1)])
    _, h1 = res_norm_mod(x0, None, gsv_first, g_mix[0][None], nL, "norm_first")
    for l in range(DEPTH):
        s = {"xin": xin, "h1": h1}
        P = matmul(h1, W_in[l], "nn", F32, f"in_proj{l}", tn=IN_PAD)
        qr, kr, kb, vb = rope_apply(P, C_QA // 256, P, C_KA // 128, cos, sin, rotm, False, f"rope{l}", kv_src=P)
        sink8 = _pad8(jnp.broadcast_to(wa_sink[l][:, None], (WA_HEADS, 128)))
        oa, sta = win_attn_fwd(qr, kr, P, sink8, L, Lc, f"wa_fwd{l}")
        bias = na_bias_table(na_rpb[l], l)
        ob, stb = na_fwd(P, kb, vb, bias, L, Lc, f"na_fwd{l}")
        w8 = jnp.concatenate([conv_w[l], jnp.zeros((1, 1024), F32)], axis=0)
        pre, act = conv_silu_fwd(P, w8, ssm_conv_b[l][None], nL, f"conv_fwd{l}")
        dtb8, al8 = _pad8(ssm_dt_bias[l]), _pad8(ssm_a_log[l])
        yf, yb, hsf, hsb = ssd_fwd(act, P, dtb8, al8, L, Lc, f"ssd_fwd{l}")
        dskip = jnp.repeat(ssm_d[l], S_P)[None]
        oc = ssm_out_fwd(yf, yb, act, P, dskip, ssm_norm_g[l][None], f"ssm_out_fwd{l}")
        mixin = jnp.concatenate([oa, ob, oc], axis=1)
        mix = matmul(mixin, W_out[l], "nn", BF16, f"out_proj{l}")
        gsv_mid = _gsv([mrow(l, 2), mrow(l, 3), mrow(l, 4)])
        x1, h2 = res_norm_mod(xin, mix, gsv_mid, g_ffn[l][None], nL, f"norm_mid{l}")
        gu = matmul_fi(h2, W_fi[l], "nn", BF16, f"ffn_in{l}")
        af = swiglu_fwd(gu, f"swiglu_fwd{l}")
        fo = matmul(af, W_fo[l], "nn", BF16, f"ffn_out{l}", tk=D_FF)
        s.update(P=P, qr=qr, kr=kr, sink8=sink8, oa=oa, sta=sta, ob=ob, stb=stb, kb=kb, vb=vb, bias=bias, w8=w8, pre=pre, act=act, dtb8=dtb8, al8=al8, yf=yf,
                 yb=yb, hsf=hsf, hsb=hsb, dskip=dskip, mixin=mixin, mix=mix, gsv_mid=gsv_mid, x1=x1, h2=h2, gu=gu, af=af, fo=fo)
        if l + 1 < DEPTH:
            s["gsv_end"] = _gsv([mrow(l, 5), mrow(l + 1, 0), mrow(l + 1, 1)])
            xin, h1 = res_norm_mod(x1, fo, s["gsv_end"], g_mix[l + 1][None], nL, f"norm_end{l}")
        else:
            s["gsv_end"] = _gsv([mrow(l, 5), None, None])
        sv.append(s)

    last = sv[-1]
    loss8, dres, dfo, dgsv_end, dg_final = final_loss(last["x1"], last["fo"], last["gsv_end"], g_final[None], loss_target[0].astype(F32), nL, "final_loss")
    loss = lax.psum(loss8[0, 0], ("x", "y", "c"))

    dmod = [[None] * 6 for _ in range(DEPTH)]
    gW = {n: [None] * DEPTH for n in _BIG}
    small = [dict() for _ in range(DEPTH)]
    parts = [None] * DEPTH
    cvec = mc.astype(jnp.int32).reshape(1)
    grad_x = None
    for l in reversed(range(DEPTH)):
        s = sv[l]
        dmod[l][5] = dgsv_end[:, 0]
        if l + 1 < DEPTH:
            dmod[l + 1][0], dmod[l + 1][1] = dgsv_end[:, 1], dgsv_end[:, 2]
        daf = matmul(dfo, W_fo[l], "nt", BF16, f"ffn_out_dx{l}")
        gW["w_ffn_out"][l] = matmul(s["af"], dfo, "tn", BF16, f"ffn_out_dw{l}", tm=1408, tk=T).reshape(N_CHIPS, D_FF // N_CHIPS, D)
        dgu = swiglu_bwd(s["gu"], daf, f"swiglu_bwd{l}")
        dh2 = matmul_fi(dgu, W_fi[l], "nt", BF16, f"ffn_in_dx{l}")
        gW["w_ffn_in"][l] = matmul_fi(s["h2"], dgu, "tn", BF16, f"ffn_in_dw{l}")
        dres, dmix, dgsv_mid, dg_ffn = res_norm_mod_bwd(s["x1"], s["mix"], s["gsv_mid"], g_ffn[l][None], dh2, dres, nL, f"norm_mid_bwd{l}")
        dmod[l][2], dmod[l][3], dmod[l][4] = dgsv_mid[:, 0], dgsv_mid[:, 1], dgsv_mid[:, 2]
        dmixin = matmul(dmix, W_out[l], "nt", BF16, f"out_proj_dx{l}")
        gW["w_out"][l] = matmul(s["mixin"], dmix, "tn", BF16, f"out_proj_dw{l}", tm=1024, tk=T).reshape(N_CHIPS, D // N_CHIPS, D)
        P = s["P"]
        dqr, dkr, dva, dsink = win_attn_bwd(s["qr"], s["kr"], P, s["sink8"], dmixin, s["oa"], s["sta"], L, Lc, f"wa_bwd{l}")
        dqa, dka = rope_apply(dqr, 0, dkr[WA_BLK:WA_BLK + T], 0, cos, sin, rotm, True, f"rope_bwd{l}")
        dqb, dkb, dvb, dbias = na_bwd(P, s["kb"], s["vb"], s["bias"], dmixin, s["ob"], s["stb"], L, Lc, f"na_bwd{l}")
        dy, dxs1, dz, dvec = ssm_out_bwd(s["yf"], s["yb"], s["act"], P, s["dskip"], ssm_norm_g[l][None], dmixin, f"ssm_out_bwd{l}")
        dxf, dbf, dcf, ddf, dxb, dbb, dcb, ddb, ddtb, dal = ssd_bwd(s["act"], P, s["dtb8"], s["al8"], s["hsf"], s["hsb"], dy, L, Lc, f"ssd_bwd{l}")
        dpre = dsilu(s["pre"], [dxf, dxb, dxs1], [dbf, dbb], [dcf, dcb], f"dsilu{l}")
        dxbc, dw8, db8 = conv_bwd(dpre, P, s["w8"], nL, f"conv_bwd{l}")
        dP = jnp.concatenate([dqa, dqb, dz, dka, dva[WA_BLK:WA_BLK + T].astype(BF16), dkb.astype(BF16), dvb.astype(BF16), dxbc,
                              ddf.astype(BF16), ddb.astype(BF16), jnp.zeros((T, IN_PAD - IN_COLS), BF16)], axis=1)
        dh1 = matmul(dP, W_in[l], "nt", BF16, f"in_proj_dx{l}", tk=IN_PAD)
        dwin = matmul(s["h1"], dP, "tn", BF16, f"in_proj_dw{l}", tm=512, tn=IN_PAD, tk=T // 2)
        cw = IN_COLS // N_CHIPS
        gW["w_in"][l] = jnp.stack([dwin[:, k * cw:(k + 1) * cw] for k in range(N_CHIPS)])
        garr = [gW[n][l] for n in _BIG]
        got = swap_halves(garr, f"reduce_d2d{l}")
        chip_sum = [add_halves(garr[a], got[a], cvec, f"reduce_add_pair{l}_{a}") for a in range(len(garr))]
        parts[l] = scatter_chips_sc(chip_sum, f"reduce_ici{l}")
        small[l] = dict(g_ffn=dg_ffn[0], wa_sink=dsink[:WA_HEADS, 0], na_rpb=na_rpb_grad(dbias, l), conv_w=dw8[:S_CONV], conv_b=db8[0],
                        dt_bias=ddtb[:2, :8], a_log=dal[:2, :8], ssm_d=dvec[0].reshape(S_HEADS, S_P).sum(axis=1), norm_g=dvec[1])
        if l > 0:
            p = sv[l - 1]
            dres, dfo, dgsv_end, dg_mix = res_norm_mod_bwd(s["xin"], p["fo"], p["gsv_end"], g_mix[l][None], dh1, dres, nL, f"norm_end_bwd{l - 1}")
        else:
            grad_x, _, dgsv_first, dg_mix = res_norm_mod_bwd(s["xin"], None, gsv_first, g_mix[0][None], dh1, dres, nL, "norm_first_bwd")
            dmod[0][0], dmod[0][1] = dgsv_first[:, 1], dgsv_first[:, 2]
        small[l]["g_mix"] = dg_mix[0]
    for l in range(DEPTH):
        for j in range(6):
            if dmod[l][j] is None:
                dmod[l][j] = jnp.zeros((2, D), F32)
    dmod = jnp.stack([jnp.stack(r, axis=1) for r in dmod])

    f3 = _Flat()
    f3.add("dmod_l", dmod[:, 0].reshape(DEPTH, 6 * D))
    f3.add("dmod_c", dmod[:, 1].reshape(DEPTH, 6 * D))
    f3.add("g_final", dg_final[0])
    for n in ("g_mix", "g_ffn", "wa_sink", "na_rpb", "conv_w", "conv_b", "dt_bias", "a_log", "ssm_d", "norm_g"):
        f3.add(n, jnp.stack([small[l][n] for l in range(DEPTH)]))
    g3, s3 = allgather8(f3.rows(), "reduce_small")
    dmod_all = f3.split_lead(g3)["dmod_l"]
    s3 = f3.split(s3)
    dmodc_tot = s3["dmod_c"]
    col0 = chip * MODW
    G16, G16c = [], []
    for l in range(DEPTH):
        rows = jnp.concatenate([dmod_all[:, l], dmodc_tot[l][None], jnp.zeros((7, 6 * D), F32)], axis=0)
        G16.append(lax.dynamic_slice_in_dim(rows, col0, MODW, axis=1))
        rc = jnp.concatenate([dmodc_tot[l][None], jnp.zeros((15, 6 * D), F32)], axis=0)
        G16c.append(lax.dynamic_slice_in_dim(rc, col0, MODW, axis=1))
    grad_w_mod = jnp.stack([matmul(A16, G16[l], "tn", F32, f"mod_dw{l}") for l in range(DEPTH)])
    dscc_part = sum(matmul(G16c[l], w_mod[l], "nt", F32, f"mod_dx{l}")[0] for l in range(DEPTH))
    _, s4 = allgather8(_pad_rows(dscc_part * (mc == 1).astype(F32)), "reduce_cctx")
    dscc = s4.reshape(-1)[:D]
    cc = c_ctx.astype(F32)
    sg = 1.0 / (1.0 + jnp.exp(-cc))
    grad_c_ctx = dscc * (sg * (1.0 + cc * (1.0 - sg)))

    halves = [[sum_slots(parts[l][i], f"reduce_add_chips{l}_{i}") for l in range(DEPTH)] for i in range(len(_BIG))]
    gsh = dict(zip(_BIG, share_halves(halves, "reduce_share")))

    grads = {"c_ctx": grad_c_ctx, "w_mod": grad_w_mod, "b_mod": s3["dmod_l"] + s3["dmod_c"], "g_mix": s3["g_mix"], "w_in": gsh["w_in"],
             "wa_sink": s3["wa_sink"], "na_rpb": s3["na_rpb"],
             "ssm_conv_w": lax.dynamic_slice_in_dim(s3["conv_w"], chip * CW, CW, axis=2), "ssm_conv_b": s3["conv_b"],
             "ssm_dt_bias": s3["dt_bias"], "ssm_a_log": s3["a_log"], "ssm_d": s3["ssm_d"], "ssm_norm_g": s3["norm_g"],
             "w_out": gsh["w_out"], "g_ffn": s3["g_ffn"], "w_ffn_in": gsh["w_ffn_in"], "w_ffn_out": gsh["w_ffn_out"], "g_final": s3["g_final"]}
    wts = {"c_ctx": c_ctx, "w_mod": w_mod, "b_mod": b_mod, "g_mix": g_mix, "w_in": w_in, "wa_sink": wa_sink, "na_rpb": na_rpb,
           "ssm_conv_w": ssm_conv_w, "ssm_conv_b": ssm_conv_b, "ssm_dt_bias": ssm_dt_bias, "ssm_a_log": ssm_a_log, "ssm_d": ssm_d,
           "ssm_norm_g": ssm_norm_g, "w_out": w_out, "g_ffn": g_ffn, "w_ffn_in": w_ffn_in, "w_ffn_out": w_ffn_out, "g_final": g_final}
    ms = {"c_ctx": m_c_ctx, "w_mod": m_w_mod, "b_mod": m_b_mod, "g_mix": m_g_mix, "w_in": m_w_in, "wa_sink": m_wa_sink, "na_rpb": m_na_rpb,
          "ssm_conv_w": m_ssm_conv_w, "ssm_conv_b": m_ssm_conv_b, "ssm_dt_bias": m_ssm_dt_bias, "ssm_a_log": m_ssm_a_log, "ssm_d": m_ssm_d,
          "ssm_norm_g": m_ssm_norm_g, "w_out": m_w_out, "g_ffn": m_g_ffn, "w_ffn_in": m_w_ffn_in, "w_ffn_out": m_w_ffn_out, "g_final": m_g_final}
    vs = {"c_ctx": v_c_ctx, "w_mod": v_w_mod, "b_mod": v_b_mod, "g_mix": v_g_mix, "w_in": v_w_in, "wa_sink": v_wa_sink, "na_rpb": v_na_rpb,
          "ssm_conv_w": v_ssm_conv_w, "ssm_conv_b": v_ssm_conv_b, "ssm_dt_bias": v_ssm_dt_bias, "ssm_a_log": v_ssm_a_log, "ssm_d": v_ssm_d,
          "ssm_norm_g": v_ssm_norm_g, "w_out": v_w_out, "g_ffn": v_g_ffn, "w_ffn_in": v_w_ffn_in, "w_ffn_out": v_w_ffn_out, "g_final": v_g_final}
    names = list(wts)
    grads = {n: grads[n].reshape(wts[n].shape).astype(F32) for n in names}
    big = ("w_mod", "w_in", "w_out", "w_ffn_in", "w_ffn_out")
    delta, new_m, new_v = {}, {}, {}
    for n in big:
        delta[n], new_m[n], new_v[n] = adamw(wts[n], grads[n], ms[n], vs[n], f"adamw_{n}")
    packs = []
    for src in (wts, grads, ms, vs):
        f = _Flat()
        for n in names:
            if n not in big:
                f.add(n, src[n])
        packs.append(f)
    d_, m_, v_ = adamw(*[f.rows()[None] for f in packs], "adamw_small")
    for dst, rows in ((delta, d_), (new_m, m_), (new_v, v_)):
        dst.update(packs[0].split(rows[0]))

    return (loss, grad_x[:L][None], *[grads[n] for n in names], *[delta[n] for n in names],
            *[new_m[n] for n in names], *[new_v[n] for n in names])
```

```python
import functools

import numpy as np
import jax
import jax.numpy as jnp
from jax import lax
from jax.experimental import pallas as pl
from jax.experimental.pallas import tpu as pltpu
from jax.experimental.pallas import tpu_sc as plsc

F32 = jnp.float32
BF16 = jnp.bfloat16
_MXU = jnp.bfloat16
_HI = lax.Precision.HIGHEST
MESH = pl.DeviceIdType.MESH

D = 1024
HD = 64
GRID_W = 64
EPS = 1e-6
ROPE_BASE = 10000.0
WA_HEADS, WA_KV = 4, 2
WA_BLK = 128
NA_HEADS, NA_KH, NA_KW = 4, 8, 16
S_HEADS, S_P, S_INNER, S_GROUPS, S_N, S_CONV, S_Q = 8, 64, 512, 2, 128, 7, 128
D_FF = 2816
IN_COLS = 2832
IN_PAD = 2944
C_QA, C_QB, C_Z, C_KA, C_VA, C_KB, C_VB, C_XBC, C_DT = 0, 256, 512, 1024, 1152, 1280, 1536, 1792, 2816
ADAM_LR, ADAM_B1, ADAM_B2, ADAM_EPS, ADAM_WD, ADAM_STEP = 0.001, 0.9, 0.999, 1e-08, 0.01, 10

TR = 256
NEG = -1e30
VMEM_CAP = 56 * 1024 * 1024


PIN_BYTES = 256 * 1024


def _is_big(a):
    return hasattr(a, "shape") and len(a.shape) >= 2 and int(np.prod(a.shape)) * jnp.dtype(a.dtype).itemsize >= PIN_BYTES


def _pc(body, *, out_shape, pin=True, **kw):
    if not pin:
        return pl.pallas_call(body, out_shape=out_shape, **kw)
    one = isinstance(out_shape, jax.ShapeDtypeStruct)
    outs = [pltpu.HBM(s.shape, s.dtype) if _is_big(s) else s for s in ([out_shape] if one else out_shape)]
    call = pl.pallas_call(body, out_shape=outs[0] if one else outs, **kw)
    return lambda *args: call(*[pltpu.with_memory_space_constraint(a, pltpu.HBM) if _is_big(a) else a for a in args])


def _cp(sem=None, vmem=None):
    kw = {}
    if sem is not None:
        kw["dimension_semantics"] = sem
    if vmem is not None:
        kw["vmem_limit_bytes"] = int(min(max(vmem, 16 * 1024 * 1024), VMEM_CAP))
    return pltpu.CompilerParams(**kw)


def _sds(shape, dtype):
    return jax.ShapeDtypeStruct(tuple(shape), dtype)


_DIMS = {"nn": ((1,), (0,)), "nt": ((1,), (1,)), "tn": ((0,), (0,))}


def _dg(a, b, dims):
    return lax.dot_general(a.astype(_MXU), b.astype(_MXU), (dims, ((), ())), preferred_element_type=F32)


@functools.partial(jax.custom_vjp, nondiff_argnums=(2,))
def bdot(a, b, mode):
    return _dg(a, b, _DIMS[mode])


def _bdot_fwd(a, b, mode):
    return bdot(a, b, mode), (a, b)


def _bdot_bwd(mode, res, g):
    a, b = res
    if mode == "nn":
        return bdot(g, b, "nt"), bdot(a, g, "tn")
    if mode == "nt":
        return bdot(g, b, "nn"), bdot(g, a, "tn")
    return bdot(b, g, "nt"), bdot(a, g, "nn")


bdot.defvjp(_bdot_fwd, _bdot_bwd)


def hdot(a, b, mode="nn"):
    return lax.dot_general(a, b, (_DIMS[mode], ((), ())), precision=_HI, preferred_element_type=F32)


def _silu(x):
    return x / (1.0 + jnp.exp(-x))


def _softplus(x):
    return jnp.maximum(x, 0.0) + jnp.log(1.0 + jnp.exp(-jnp.abs(x)))


def _div_tile(n, cap, mult):
    if n <= cap:
        return n
    best = None
    for t in range(mult, cap + 1, mult):
        if n % t == 0:
            best = t
    assert best is not None, (n, cap, mult)
    return best


def matmul(a, b, mode, out_dtype, name, tm=640, tn=1536, tk=1408, hi=False):
    if mode == "tn":
        K, M = a.shape
    else:
        M, K = a.shape
    N = b.shape[0] if mode == "nt" else b.shape[1]
    tm = _div_tile(M, tm, 128 if mode == "tn" else 16)
    tn = _div_tile(N, tn, 128)
    tk = _div_tile(K, tk, 128 if mode != "tn" else 16)
    nk = K // tk
    dims = _DIMS[mode]

    def body(a_ref, b_ref, o_ref, *acc):
        if hi:
            part = lax.dot_general(a_ref[...], b_ref[...], (dims, ((), ())), precision=_HI, preferred_element_type=F32)
        else:
            part = _dg(a_ref[...], b_ref[...], dims)
        if nk == 1:
            o_ref[...] = part.astype(o_ref.dtype)
        else:
            k = pl.program_id(2)

            @pl.when(k == 0)
            def _():
                acc[0][...] = part

            @pl.when(k > 0)
            def _():
                acc[0][...] += part

            @pl.when(k == nk - 1)
            def _():
                o_ref[...] = acc[0][...].astype(o_ref.dtype)

    if mode == "tn":
        a_spec = pl.BlockSpec((tk, tm), lambda i, j, k: (k, i))
    else:
        a_spec = pl.BlockSpec((tm, tk), lambda i, j, k: (i, k))
    if mode == "nt":
        b_spec = pl.BlockSpec((tn, tk), lambda i, j, k: (j, k))
    else:
        b_spec = pl.BlockSpec((tk, tn), lambda i, j, k: (k, j))
    isz = lambda x: jnp.dtype(x.dtype).itemsize
    vmem = 2 * (tm * tk * isz(a) + tk * tn * isz(b) + tm * tn * jnp.dtype(out_dtype).itemsize) + 3 * tm * tn * 4
    return _pc(
        body, name=name, grid=(M // tm, N // tn, nk),
        in_specs=[a_spec, b_spec], out_specs=pl.BlockSpec((tm, tn), lambda i, j, k: (i, j)),
        out_shape=_sds((M, N), out_dtype),
        scratch_shapes=[pltpu.VMEM((tm, tn), F32)] if nk > 1 else [],
        compiler_params=_cp(("parallel", "parallel", "arbitrary"), vmem + (8 << 20)),
    )(a, b)


def in_proj_bwd(pieces, h1, w, name):
    T = h1.shape[0]
    arrs = [a for a, _ in pieces]
    offs = [o for _, o in pieces]
    wid = [a.shape[1] for a in arrs]
    n = len(arrs)
    assert sum(wid) == IN_PAD, "the pieces must tile all columns of P"
    tm = _div_tile(T, 640, 16)

    def dx_body(*refs):
        w_ref, o_ref = refs[n], refs[n + 1]
        acc = None
        for j in range(n):
            part = _dg(refs[j][...], w_ref[:, offs[j]:offs[j] + wid[j]], _DIMS["nt"])
            acc = part if acc is None else acc + part
        o_ref[...] = acc.astype(o_ref.dtype)

    dh1 = _pc(dx_body, name=name + "_dx", grid=(T // tm,),
              in_specs=[pl.BlockSpec((tm, wj), lambda i: (i, 0)) for wj in wid] + [pl.BlockSpec((D, IN_PAD), lambda i: (0, 0))],
              out_specs=pl.BlockSpec((tm, D), lambda i: (i, 0)), out_shape=_sds((T, D), BF16),
              compiler_params=_cp(("parallel",), 40 << 20))(*arrs, w)

    tmd, nk = 512, 4
    tk = T // nk

    def dw_body(h_ref, *refs):
        o_ref, acc = refs[n], refs[n + 1]
        k = pl.program_id(1)

        @pl.when(k == 0)
        def _():
            acc[...] = jnp.zeros_like(acc)

        for j in range(n):
            acc[:, offs[j]:offs[j] + wid[j]] += _dg(h_ref[...], refs[j][...], _DIMS["tn"])

        @pl.when(k == nk - 1)
        def _():
            o_ref[...] = acc[...].astype(o_ref.dtype)

    dw = _pc(dw_body, name=name + "_dw", grid=(D // tmd, nk),
             in_specs=[pl.BlockSpec((tk, tmd), lambda i, k: (k, i))] + [pl.BlockSpec((tk, wj), lambda i, k: (k, 0)) for wj in wid],
             out_specs=pl.BlockSpec((tmd, IN_PAD), lambda i, k: (i, 0)), out_shape=_sds((D, IN_PAD), BF16),
             scratch_shapes=[pltpu.VMEM((tmd, IN_PAD), F32)], compiler_params=_cp(("parallel", "arbitrary"), 48 << 20))(h1, *arrs)
    return dh1, dw


def _norm_mod(xo, shift, scale, g):
    r = lax.rsqrt(jnp.mean(xo * xo, axis=-1, keepdims=True) + EPS)
    return (xo * r) * g * (1.0 + scale) + shift


def res_norm_mod(x, y, gsv, g, nL, name):
    T = x.shape[0]
    has_y = y is not None

    def body(*refs):
        if has_y:
            x_ref, y_ref, gsv_ref, g_ref, xo_ref, h_ref = refs
            xo = x_ref[...] + gsv_ref[0, 0:1, :] * y_ref[...]
            xo_ref[...] = xo
        else:
            x_ref, gsv_ref, g_ref, h_ref = refs
            xo = x_ref[...]
        h_ref[...] = _norm_mod(xo, gsv_ref[0, 1:2, :], gsv_ref[0, 2:3, :], g_ref[...]).astype(h_ref.dtype)

    row = pl.BlockSpec((TR, D), lambda i: (i, 0))
    in_specs = [row] + ([row] if has_y else []) + [pl.BlockSpec((1, 8, D), lambda i: (i // nL, 0, 0)),
                                                     pl.BlockSpec((1, D), lambda i: (0, 0))]
    out_specs = ([row] if has_y else []) + [row]
    out_shape = ([_sds((T, D), F32)] if has_y else []) + [_sds((T, D), BF16)]
    args = (x, y, gsv, g) if has_y else (x, gsv, g)
    outs = _pc(body, name=name, grid=(T // TR,), in_specs=in_specs, out_specs=out_specs, out_shape=out_shape,
               compiler_params=_cp(("arbitrary",), 24 << 20))(*args)
    return (outs[0], outs[1]) if has_y else (None, outs[0])


def res_norm_mod_bwd(xo, y, gsv, g, dh, dres, nL, name):
    T = xo.shape[0]
    has_y = y is not None

    def body(*refs):
        if has_y:
            xo_ref, y_ref, gsv_ref, g_ref, dh_ref, dres_ref, dx_ref, dy_ref, dgsv_ref, dg_ref = refs
        else:
            xo_ref, gsv_ref, g_ref, dh_ref, dres_ref, dx_ref, dgsv_ref, dg_ref = refs
        i = pl.program_id(0)

        @pl.when((i == 0) | (i == nL))
        def _():
            dgsv_ref[...] = jnp.zeros_like(dgsv_ref)

        @pl.when(i == 0)
        def _():
            dg_ref[...] = jnp.zeros_like(dg_ref)

        _, vjp = jax.vjp(_norm_mod, xo_ref[...], gsv_ref[0, 1:2, :], gsv_ref[0, 2:3, :], g_ref[...])
        dxn, dshift, dscale, dg = vjp(dh_ref[...].astype(F32))
        dxo = dres_ref[...] + dxn
        dx_ref[...] = dxo
        if has_y:
            dy_ref[...] = (gsv_ref[0, 0:1, :] * dxo).astype(dy_ref.dtype)
            dgsv_ref[0, 0:1, :] += jnp.sum(y_ref[...] * dxo, axis=0, keepdims=True)
        dgsv_ref[0, 1:2, :] += dshift
        dgsv_ref[0, 2:3, :] += dscale
        dg_ref[0:1, :] += dg

    row = pl.BlockSpec((TR, D), lambda i: (i, 0))
    gspec = pl.BlockSpec((1, 8, D), lambda i: (i // nL, 0, 0))
    in_specs = [row] + ([row] if has_y else []) + [gspec, pl.BlockSpec((1, D), lambda i: (0, 0)), row, row]
    out_specs = [row] + ([row] if has_y else []) + [gspec, pl.BlockSpec((8, D), lambda i: (0, 0))]
    out_shape = [_sds((T, D), F32)] + ([_sds((T, D), BF16)] if has_y else []) + [_sds((2, 8, D), F32), _sds((8, D), F32)]
    args = (xo, y, gsv, g, dh, dres) if has_y else (xo, gsv, g, dh, dres)
    outs = _pc(body, name=name, grid=(T // TR,), in_specs=in_specs, out_specs=out_specs, out_shape=out_shape,
               compiler_params=_cp(("arbitrary",), 32 << 20))(*args)
    if has_y:
        return outs
    return outs[0], None, outs[1], outs[2]


def final_loss(x, y, gsv, g, target, nL, name):
    T = x.shape[0]

    def lossf(xo, gv, t):
        yn = (xo * lax.rsqrt(jnp.mean(xo * xo, axis=-1, keepdims=True) + EPS)) * gv
        e = yn - t
        return 0.5 * jnp.sum(jnp.sum(e * e, axis=-1, keepdims=True) * (1.0 / D), axis=0, keepdims=True)

    def body(x_ref, y_ref, gsv_ref, g_ref, t_ref, loss_ref, dx_ref, dy_ref, dgsv_ref, dg_ref):
        i = pl.program_id(0)

        @pl.when(i == 0)
        def _():
            loss_ref[...] = jnp.zeros_like(loss_ref)
            dg_ref[...] = jnp.zeros_like(dg_ref)

        @pl.when((i == 0) | (i == nL))
        def _():
            dgsv_ref[...] = jnp.zeros_like(dgsv_ref)

        @pl.when(i < nL)
        def _():
            gate = gsv_ref[0, 0:1, :]
            yv = y_ref[...]
            xo = x_ref[...] + gate * yv
            lv, vjp = jax.vjp(lossf, xo, g_ref[...], t_ref[...])
            dxo, dg, _ = vjp(jnp.ones((1, 1), F32))
            loss_ref[...] += jnp.broadcast_to(lv, loss_ref.shape)
            dx_ref[...] = dxo
            dy_ref[...] = (gate * dxo).astype(dy_ref.dtype)
            dgsv_ref[0, 0:1, :] += jnp.sum(yv * dxo, axis=0, keepdims=True)
            dg_ref[0:1, :] += dg

        @pl.when(i >= nL)
        def _():
            dx_ref[...] = jnp.zeros_like(dx_ref)
            dy_ref[...] = jnp.zeros_like(dy_ref)

    row = pl.BlockSpec((TR, D), lambda i: (i, 0))
    gspec = pl.BlockSpec((1, 8, D), lambda i: (i // nL, 0, 0))
    return _pc(
        body, name=name, grid=(T // TR,),
        in_specs=[row, row, gspec, pl.BlockSpec((1, D), lambda i: (0, 0)),
                  pl.BlockSpec((TR, D), lambda i: (jnp.minimum(i, nL - 1), 0))],
        out_specs=[pl.BlockSpec((8, 128), lambda i: (0, 0)), row, row, gspec, pl.BlockSpec((8, D), lambda i: (0, 0))],
        out_shape=[_sds((8, 128), F32), _sds((T, D), F32), _sds((T, D), BF16), _sds((2, 8, D), F32), _sds((8, D), F32)],
        compiler_params=_cp(("arbitrary",), 32 << 20),
    )(x, y, gsv, g, target)


FI_BLK = 2 * D_FF // 4


def _fi_chip(j):
    return (j % 2) * 2 + j // 2


def matmul_fi(a, b, mode, out_dtype, name):
    T = a.shape[0]
    if mode == "tn":
        tmd = 512

        def body(a_ref, b_ref, o_ref):
            o_ref[0] = _dg(a_ref[...], b_ref[...], _DIMS["tn"]).astype(o_ref.dtype)

        return _pc(body, name=name, grid=(D // tmd, 4),
                   in_specs=[pl.BlockSpec((T, tmd), lambda i, j: (0, i)), pl.BlockSpec((T, FI_BLK), lambda i, j: (0, j))],
                   out_specs=pl.BlockSpec((1, tmd, FI_BLK), lambda i, j: (_fi_chip(j), i, 0)),
                   out_shape=_sds((4, D, FI_BLK), out_dtype), compiler_params=_cp(("parallel", "arbitrary"), 48 << 20))(a, b)
    if mode == "nn":
        tm = _div_tile(T, 1280, 16)

        def body(a_ref, b_ref, o_ref):
            o_ref[...] = _dg(a_ref[...], b_ref[0], _DIMS["nn"]).astype(o_ref.dtype)

        return _pc(body, name=name, grid=(T // tm, 4),
                   in_specs=[pl.BlockSpec((tm, D), lambda i, j: (i, 0)), pl.BlockSpec((1, D, FI_BLK), lambda i, j: (_fi_chip(j), 0, 0))],
                   out_specs=pl.BlockSpec((tm, FI_BLK), lambda i, j: (i, j)), out_shape=_sds((T, 4 * FI_BLK), out_dtype),
                   compiler_params=_cp(("parallel", "arbitrary"), 40 << 20))(a, b)
    tm = _div_tile(T, 640, 16)

    def body(a_ref, b_ref, o_ref):
        acc = None
        for k in range(4):
            part = _dg(a_ref[:, k * FI_BLK:(k + 1) * FI_BLK], b_ref[_fi_chip(k)], _DIMS["nt"])
            acc = part if acc is None else acc + part
        o_ref[...] = acc.astype(o_ref.dtype)

    return _pc(body, name=name, grid=(T // tm,),
               in_specs=[pl.BlockSpec((tm, 4 * FI_BLK), lambda i: (i, 0)), pl.BlockSpec((4, D, FI_BLK), lambda i: (0, 0, 0))],
               out_specs=pl.BlockSpec((tm, D), lambda i: (i, 0)), out_shape=_sds((T, D), out_dtype),
               compiler_params=_cp(("parallel",), VMEM_CAP))(a, b)


def _swiglu(gate, up):
    return _silu(gate) * up


def swiglu_fwd(gu, name):
    T = gu.shape[0]

    def body(x_ref, o_ref):
        o_ref[...] = _swiglu(x_ref[:, :FI_BLK].astype(F32), x_ref[:, FI_BLK:].astype(F32)).astype(o_ref.dtype)

    return _pc(body, name=name, grid=(T // TR, 2), in_specs=[pl.BlockSpec((TR, 2 * FI_BLK), lambda i, j: (i, j))],
               out_specs=pl.BlockSpec((TR, FI_BLK), lambda i, j: (i, j)), out_shape=_sds((T, D_FF), BF16),
               compiler_params=_cp(("parallel", "parallel"), 24 << 20))(gu)


def swiglu_bwd(gu, dact, name):
    T = gu.shape[0]

    def body(x_ref, d_ref, o_ref):
        g, u, d = x_ref[:, :FI_BLK].astype(F32), x_ref[:, FI_BLK:].astype(F32), d_ref[...].astype(F32)
        sg = 1.0 / (1.0 + jnp.exp(-g))
        sl = g * sg
        o_ref[:, :FI_BLK] = (d * u * (sg + sl * (1.0 - sg))).astype(o_ref.dtype)
        o_ref[:, FI_BLK:] = (d * sl).astype(o_ref.dtype)

    return _pc(body, name=name, grid=(T // TR, 2),
               in_specs=[pl.BlockSpec((TR, 2 * FI_BLK), lambda i, j: (i, j)), pl.BlockSpec((TR, FI_BLK), lambda i, j: (i, j))],
               out_specs=pl.BlockSpec((TR, 2 * FI_BLK), lambda i, j: (i, j)), out_shape=_sds((T, 2 * D_FF), BF16),
               compiler_params=_cp(("parallel", "parallel"), 32 << 20))(gu, dact)


def rope_tables(L, Lc):
    t = np.arange(L)
    rows, cols = t // GRID_W, t % GRID_W
    inv = ROPE_BASE ** (-np.arange(16, dtype=np.float32) / 16)
    lane = np.arange(64)
    pos = np.where((lane // 32)[None, :] == 0, rows[:, None], cols[:, None]).astype(np.float32)
    ang = jnp.asarray(pos) * jnp.asarray(inv[lane % 16])[None, :]
    cos = jnp.concatenate([jnp.cos(ang), jnp.ones((Lc, 64), F32)], axis=0)
    sin = jnp.concatenate([jnp.sin(ang), jnp.zeros((Lc, 64), F32)], axis=0)
    R = np.zeros((128, 128), np.float32)
    for i in range(128):
        if (i % 32) < 16:
            R[i + 16, i] = -1.0
        else:
            R[i - 16, i] = 1.0
    return jnp.tile(cos, (1, 2)), jnp.tile(sin, (1, 2)), jnp.asarray(R)


def rope_apply(q_src, q_col, k_src, k_col, cos, sin, R, transpose, name, kv_src=None):
    T = cos.shape[0]
    with_kv = kv_src is not None

    def rot(x, c, s, Rm):
        if transpose:
            return x * c + hdot(x * s, Rm, "nt")
        return x * c + hdot(x, Rm) * s

    def body(q_ref, k_ref, c_ref, s_ref, R_ref, *rest):
        qo_ref, ko_ref = rest[-4:-2] if with_kv else rest
        c, s, Rm = c_ref[...], s_ref[...], R_ref[...]
        for j in range(2):
            qo_ref[:, j * 128:(j + 1) * 128] = rot(q_ref[:, j * 128:(j + 1) * 128].astype(F32), c, s, Rm).astype(qo_ref.dtype)
        ko_ref[...] = rot(k_ref[...].astype(F32), c, s, Rm).astype(ko_ref.dtype)
        if with_kv:
            rest[-2][...] = rest[0][...].astype(BF16)
            rest[-1][...] = rest[1][...].astype(BF16)

    tab = pl.BlockSpec((TR, 128), lambda i: (i, 0))
    wide = pl.BlockSpec((TR, 256), lambda i: (i, 0))
    kv_in = [pl.BlockSpec((TR, 256), lambda i: (i, C_KB // 256)), pl.BlockSpec((TR, 256), lambda i: (i, C_VB // 256))] if with_kv else []
    return _pc(body, name=name, grid=(T // TR,),
               in_specs=[pl.BlockSpec((TR, 256), lambda i: (i, q_col)), pl.BlockSpec((TR, 128), lambda i: (i, k_col)),
                         tab, tab, pl.BlockSpec((128, 128), lambda i: (0, 0))] + kv_in,
               out_specs=[wide, tab] + ([wide, wide] if with_kv else []),
               out_shape=[_sds((T, 256), BF16), _sds((T, 128), BF16)] + ([_sds((T, 256), BF16)] * 2 if with_kv else []),
               compiler_params=_cp(("parallel",), 16 << 20))(q_src, k_src, cos, sin, R, *([kv_src, kv_src] if with_kv else []))


_SCALE = HD ** -0.5


def _attn_tile(qh, ks, vs, extra):
    ss = []
    for k, add in ks:
        s = _dg(qh, k, _DIMS["nt"]) * _SCALE
        ss.append(s if add is None else s + add)
    m = ss[0].max(axis=-1, keepdims=True)
    for s in ss[1:]:
        m = jnp.maximum(m, s.max(axis=-1, keepdims=True))
    if extra is not None:
        m = jnp.maximum(m, extra)
    ps = [jnp.exp(s - m) for s in ss]
    den = ps[0].sum(axis=-1, keepdims=True)
    for p in ps[1:]:
        den = den + p.sum(axis=-1, keepdims=True)
    if extra is not None:
        den = den + jnp.exp(extra - m)
    num = _dg(ps[0], vs[0], _DIMS["nn"])
    for p, v in zip(ps[1:], vs[1:]):
        num = num + _dg(p, v, _DIMS["nn"])
    linv = 1.0 / den
    return num * linv, m, linv


def _attn_bwd_tile(qh, ks, vs, extra, m, linv, oh, doh):
    delta = jnp.sum(doh * oh, axis=-1, keepdims=True)
    dq = None
    dks, dvs, dss = [], [], []
    for (k, add), v in zip(ks, vs):
        s = _dg(qh, k, _DIMS["nt"]) * _SCALE
        if add is not None:
            s = s + add
        p = jnp.exp(s - m) * linv
        dvs.append(_dg(p, doh, _DIMS["tn"]))
        ds = p * (_dg(doh, v, _DIMS["nt"]) - delta)
        dss.append(ds)
        dsq = ds * _SCALE
        part = _dg(dsq, k, _DIMS["nn"])
        dq = part if dq is None else dq + part
        dks.append(_dg(dsq, qh, _DIMS["tn"]))
    dextra = None
    if extra is not None:
        dextra = -jnp.sum(jnp.exp(extra - m) * linv * delta, axis=0, keepdims=True)
    return dq, dks, dvs, dss, dextra


def _wa_mask(n, L):
    qpos = n * WA_BLK + lax.broadcasted_iota(jnp.int32, (WA_BLK, 3 * WA_BLK), 0)
    kpos = (n - 1) * WA_BLK + lax.broadcasted_iota(jnp.int32, (WA_BLK, 3 * WA_BLK), 1)
    ok = (jnp.abs(qpos - kpos) <= WA_BLK) & (kpos >= 0) & (kpos < L)
    return jnp.where(ok, 0.0, NEG).astype(F32)


WA_BPS = 2


def _wa_specs(L, Lc):
    nb = L // WA_BLK
    cb = L // Lc

    def blk(j, col):
        return pl.BlockSpec((WA_BLK, 128), lambda s: (jnp.clip(s * WA_BPS - 1 + j, 0, nb - 1), col))

    vcol = C_VA // 128
    kspecs = [blk(j, 0) for j in range(WA_BPS + 2)] + [pl.BlockSpec((Lc, 128), lambda s: (cb, 0))]
    vspecs = [blk(j, vcol) for j in range(WA_BPS + 2)] + [pl.BlockSpec((Lc, 128), lambda s: (cb, vcol))]
    return nb, kspecs, vspecs


def win_attn_fwd(qr, kr, P, sink, L, Lc, name):
    T = L + Lc
    nb, kspecs, vspecs = _wa_specs(L, Lc)
    nk = WA_BPS + 2
    QB = WA_BPS * WA_BLK
    nlat = nb // WA_BPS

    def body(q_ref, *refs):
        kbs, kx, vbs, vx, s_ref, o_ref, st_ref = refs[:nk], refs[nk], refs[nk + 1:2 * nk + 1], refs[2 * nk + 1], refs[-3], refs[-2], refs[-1]
        s = pl.program_id(0)

        def put(qs, h, res):
            o, m, linv = res
            o_ref[qs, h * HD:(h + 1) * HD] = o.astype(o_ref.dtype)
            st_ref[qs, h:h + 1] = m
            st_ref[qs, WA_HEADS + h:WA_HEADS + h + 1] = linv

        @pl.when(s < nlat)
        def _():
            for b in range(WA_BPS):
                mask = _wa_mask(s * WA_BPS + b, L)
                qs = slice(b * WA_BLK, (b + 1) * WA_BLK)
                for g in range(WA_KV):
                    sl = slice(g * HD, (g + 1) * HD)
                    k3 = jnp.concatenate([kbs[b + j][:, sl] for j in range(3)], axis=0)
                    v3 = jnp.concatenate([vbs[b + j][:, sl] for j in range(3)], axis=0)
                    for r in range(2):
                        h = 2 * g + r
                        put(qs, h, _attn_tile(q_ref[qs, h * HD:(h + 1) * HD], [(k3, mask), (kx[:, sl], None)], [v3, vx[:, sl]], s_ref[h:h + 1, 0:1]))

        @pl.when(s >= nlat)
        def _():
            for h in range(WA_HEADS):
                sl = slice((h // 2) * HD, (h // 2 + 1) * HD)
                put(slice(None), h, _attn_tile(q_ref[:, h * HD:(h + 1) * HD], [(kx[:, sl], None)], [vx[:, sl]], s_ref[h:h + 1, 0:1]))

    qspec = pl.BlockSpec((QB, 256), lambda s: (s, 0))
    return _pc(body, name=name, grid=(T // QB,),
               in_specs=[qspec] + kspecs + vspecs + [pl.BlockSpec((8, 128), lambda s: (0, 0))],
               out_specs=[qspec, pl.BlockSpec((QB, 8), lambda s: (s, 0))], out_shape=[_sds((T, 256), BF16), _sds((T, 8), F32)],
               compiler_params=_cp(("arbitrary",), 32 << 20))(qr, *([kr] * (nk + 1)), *([P] * (nk + 1)), sink)


def win_attn_bwd(qr, kr, P, sink, do_src, o, stats, L, Lc, name):
    T = L + Lc
    nb, kspecs, vspecs = _wa_specs(L, Lc)
    nk = WA_BPS + 2
    QB = WA_BPS * WA_BLK
    nlat = nb // WA_BPS
    cx = WA_BLK + L

    def body(q_ref, *refs):
        kbs, kx, vbs, vx = refs[:nk], refs[nk], refs[nk + 1:2 * nk + 1], refs[2 * nk + 1]
        s_ref, do_ref, o_ref, st_ref, dq_ref, dk_ref, dv_ref, ds_ref = refs[2 * nk + 2:]
        s = pl.program_id(0)

        @pl.when(s == 0)
        def _():
            dk_ref[...] = jnp.zeros_like(dk_ref)
            dv_ref[...] = jnp.zeros_like(dv_ref)
            ds_ref[...] = jnp.zeros_like(ds_ref)

        def tile(qs, h, ks, vs):
            hs = slice(h * HD, (h + 1) * HD)
            dq, dks, dvs, _, dsk = _attn_bwd_tile(q_ref[qs, hs], ks, vs, s_ref[h:h + 1, 0:1], st_ref[qs, h:h + 1],
                                                  st_ref[qs, WA_HEADS + h:WA_HEADS + h + 1], o_ref[qs, hs].astype(F32), do_ref[qs, hs].astype(F32))
            dq_ref[qs, hs] = dq
            ds_ref[h:h + 1, :] += jnp.broadcast_to(dsk, (1, 128))
            return dks, dvs

        @pl.when(s < nlat)
        def _():
            for b in range(WA_BPS):
                n = s * WA_BPS + b
                mask = _wa_mask(n, L)
                rows = pl.ds(pl.multiple_of(n * WA_BLK, WA_BLK), 3 * WA_BLK)
                qs = slice(b * WA_BLK, (b + 1) * WA_BLK)
                for g in range(WA_KV):
                    sl = slice(g * HD, (g + 1) * HD)
                    k3 = jnp.concatenate([kbs[b + j][:, sl] for j in range(3)], axis=0)
                    v3 = jnp.concatenate([vbs[b + j][:, sl] for j in range(3)], axis=0)
                    acc = None
                    for r in range(2):
                        dks, dvs = tile(qs, 2 * g + r, [(k3, mask), (kx[:, sl], None)], [v3, vx[:, sl]])
                        acc = dks + dvs if acc is None else [a + b_ for a, b_ in zip(acc, dks + dvs)]
                    dk_ref[rows, sl] += acc[0]
                    dk_ref[cx:cx + Lc, sl] += acc[1]
                    dv_ref[rows, sl] += acc[2]
                    dv_ref[cx:cx + Lc, sl] += acc[3]

        @pl.when(s >= nlat)
        def _():
            for h in range(WA_HEADS):
                sl = slice((h // 2) * HD, (h // 2 + 1) * HD)
                dks, dvs = tile(slice(None), h, [(kx[:, sl], None)], [vx[:, sl]])
                dk_ref[cx:cx + Lc, sl] += dks[0]
                dv_ref[cx:cx + Lc, sl] += dvs[0]

    qspec = pl.BlockSpec((QB, 256), lambda s: (s, 0))
    acc_spec = pl.BlockSpec((T + 2 * WA_BLK, 128), lambda s: (0, 0))
    return _pc(body, name=name, grid=(T // QB,),
               in_specs=[qspec] + kspecs + vspecs + [pl.BlockSpec((8, 128), lambda s: (0, 0)), qspec, qspec, pl.BlockSpec((QB, 8), lambda s: (s, 0))],
               out_specs=[qspec, acc_spec, acc_spec, pl.BlockSpec((8, 128), lambda s: (0, 0))],
               out_shape=[_sds((T, 256), F32), _sds((T + 2 * WA_BLK, 128), F32), _sds((T + 2 * WA_BLK, 128), F32), _sds((8, 128), F32)],
               compiler_params=_cp(("arbitrary",), 40 << 20))(qr, *([kr] * (nk + 1)), *([P] * (nk + 1)), sink, do_src, o, stats)


def na_index_tables():
    qc = np.arange(GRID_W)[:, None]
    kc = np.arange(GRID_W)[None, :]
    cstart = np.clip(qc - NA_KW // 2, 0, GRID_W - NA_KW)
    ok = (kc >= cstart) & (kc < cstart + NA_KW)
    dx = np.clip(kc - qc, -(NA_KW - 1), NA_KW - 1) + (NA_KW - 1)
    off = np.arange(NA_KH)[:, None]
    kr = np.arange(NA_KH)[None, :]
    dy = kr - off + (NA_KH - 1)
    return ok, dx, dy


def _na_selectors():
    ok, dx, dy = na_index_tables()
    e1 = np.zeros((GRID_W * GRID_W, 128), np.float32)
    qi, ki = np.nonzero(ok)
    e1[qi * GRID_W + ki, dx[qi, ki]] = 1.0
    e2 = np.zeros((16, NA_KH * NA_KH), np.float32)
    oi, ri = np.meshgrid(np.arange(NA_KH), np.arange(NA_KH), indexing="ij")
    e2[dy[oi, ri].ravel(), (oi * NA_KH + ri).ravel()] = 1.0
    return ok, jnp.asarray(e1), jnp.asarray(np.kron(np.eye(NA_HEADS, dtype=np.float32), e2))


def na_bias_table(rpb, tag):
    ok, e1, e2 = _na_selectors()
    r2 = jnp.pad(rpb.astype(F32), ((0, 0), (0, 1), (0, 128 - (2 * NA_KW - 1)))).reshape(NA_HEADS * 16, 128)
    r1 = matmul(e2, r2, "tn", F32, f"na_bias_sel1_{tag}", hi=True)
    x = matmul(r1, e1, "nt", F32, f"na_bias_sel2_{tag}", hi=True)
    b = x.reshape(NA_HEADS, NA_KH, NA_KH, GRID_W, GRID_W).transpose(0, 1, 3, 2, 4)
    b = b + jnp.asarray(np.where(ok, 0.0, NEG).astype(np.float32))[None, None, :, None, :]
    return b.reshape(NA_HEADS, NA_KH, GRID_W, NA_KH * GRID_W)


def _na_rows(r, GR):
    r0 = jnp.clip(r - NA_KH // 2, 0, GR - NA_KH)
    return r0, jnp.clip(r - r0, 0, NA_KH - 1)


NA_RPS = 4


def na_fwd(P, kb, vb, bias, L, Lc, name):
    T = L + Lc
    GR = L // GRID_W
    W = NA_KH * GRID_W
    QB = GRID_W * NA_RPS
    nlat = GR // NA_RPS

    def body(q_ref, k_ref, v_ref, b_ref, o_ref, st_ref):
        s = pl.program_id(0)

        def put(qs, h, res):
            o, m, linv = res
            o_ref[qs, h * HD:(h + 1) * HD] = o.astype(o_ref.dtype)
            st_ref[qs, h:h + 1] = m
            st_ref[qs, NA_HEADS + h:NA_HEADS + h + 1] = linv

        @pl.when(s < nlat)
        def _():
            for rr in range(NA_RPS):
                r0, off = _na_rows(s * NA_RPS + rr, GR)
                rows = pl.ds(pl.multiple_of(r0 * GRID_W, GRID_W), W)
                qs = slice(rr * GRID_W, (rr + 1) * GRID_W)
                for h in range(NA_HEADS):
                    hs = slice(h * HD, (h + 1) * HD)
                    put(qs, h, _attn_tile(q_ref[qs, hs], [(k_ref[rows, hs], b_ref[h, off]), (k_ref[L:T, hs], None)],
                                          [v_ref[rows, hs], v_ref[L:T, hs]], None))

        @pl.when(s >= nlat)
        def _():
            for h in range(NA_HEADS):
                hs = slice(h * HD, (h + 1) * HD)
                put(slice(None), h, _attn_tile(q_ref[:, hs], [(k_ref[L:T, hs], None)], [v_ref[L:T, hs]], None))

    one = pl.Buffered(1)
    return _pc(body, name=name, grid=(T // QB,),
               in_specs=[pl.BlockSpec((QB, 256), lambda r: (r, C_QB // 256)),
                         pl.BlockSpec((T, 256), lambda r: (0, 0), pipeline_mode=one),
                         pl.BlockSpec((T, 256), lambda r: (0, 0), pipeline_mode=one),
                         pl.BlockSpec((NA_HEADS, NA_KH, GRID_W, W), lambda r: (0, 0, 0, 0), pipeline_mode=one)],
               out_specs=[pl.BlockSpec((QB, 256), lambda r: (r, 0)), pl.BlockSpec((QB, 8), lambda r: (r, 0))],
               out_shape=[_sds((T, 256), BF16), _sds((T, 8), F32)],
               compiler_params=_cp(("arbitrary",), 32 << 20))(P, kb, vb, bias)


def na_bwd(P, kb, vb, bias, do_src, o, stats, L, Lc, name):
    T = L + Lc
    GR = L // GRID_W
    W = NA_KH * GRID_W
    QB = GRID_W * NA_RPS
    nlat = GR // NA_RPS

    def body(q_ref, k_ref, v_ref, b_ref, do_ref, o_ref, st_ref, dq_ref, dk_ref, dv_ref, db_ref):
        s = pl.program_id(0)

        @pl.when(s == 0)
        def _():
            dk_ref[...] = jnp.zeros_like(dk_ref)
            dv_ref[...] = jnp.zeros_like(dv_ref)
            db_ref[...] = jnp.zeros_like(db_ref)

        def tile(qs, h, ks, vs):
            hs = slice(h * HD, (h + 1) * HD)
            dq, dks, dvs, dss, _ = _attn_bwd_tile(q_ref[qs, hs], ks, vs, None, st_ref[qs, h:h + 1], st_ref[qs, NA_HEADS + h:NA_HEADS + h + 1],
                                                  o_ref[qs, hs].astype(F32), do_ref[qs, hs].astype(F32))
            dq_ref[qs, hs] = dq.astype(dq_ref.dtype)
            return dks, dvs, dss

        @pl.when(s < nlat)
        def _():
            for rr in range(NA_RPS):
                r0, off = _na_rows(s * NA_RPS + rr, GR)
                rows = pl.ds(pl.multiple_of(r0 * GRID_W, GRID_W), W)
                qs = slice(rr * GRID_W, (rr + 1) * GRID_W)
                for h in range(NA_HEADS):
                    hs = slice(h * HD, (h + 1) * HD)
                    dks, dvs, dss = tile(qs, h, [(k_ref[rows, hs], b_ref[h, off]), (k_ref[L:T, hs], None)], [v_ref[rows, hs], v_ref[L:T, hs]])
                    dk_ref[rows, hs] += dks[0]
                    dv_ref[rows, hs] += dvs[0]
                    dk_ref[L:T, hs] += dks[1]
                    dv_ref[L:T, hs] += dvs[1]
                    db_ref[h, off] += dss[0]

        @pl.when(s >= nlat)
        def _():
            for h in range(NA_HEADS):
                hs = slice(h * HD, (h + 1) * HD)
                dks, dvs, _ = tile(slice(None), h, [(k_ref[L:T, hs], None)], [v_ref[L:T, hs]])
                dk_ref[L:T, hs] += dks[0]
                dv_ref[L:T, hs] += dvs[0]

    one = pl.Buffered(1)
    full = lambda shape: pl.BlockSpec(shape, lambda r: (0,) * len(shape), pipeline_mode=one)
    qspec = pl.BlockSpec((QB, 256), lambda r: (r, 0))
    return _pc(body, name=name, grid=(T // QB,),
               in_specs=[pl.BlockSpec((QB, 256), lambda r: (r, C_QB // 256)), full((T, 256)), full((T, 256)),
                         full((NA_HEADS, NA_KH, GRID_W, W)), pl.BlockSpec((QB, 256), lambda r: (r, 1)), qspec, pl.BlockSpec((QB, 8), lambda r: (r, 0))],
               out_specs=[qspec, full((T, 256)), full((T, 256)), full((NA_HEADS, NA_KH, GRID_W, W))],
               out_shape=[_sds((T, 256), BF16), _sds((T, 256), F32), _sds((T, 256), F32), _sds((NA_HEADS, NA_KH, GRID_W, W), F32)],
               compiler_params=_cp(("arbitrary",), 48 << 20))(P, kb, vb, bias, do_src, o, stats)


def na_rpb_grad(dbias, tag):
    _, e1, e2 = _na_selectors()
    x = dbias.reshape(NA_HEADS, NA_KH, GRID_W, NA_KH, GRID_W).transpose(0, 1, 3, 2, 4).reshape(NA_HEADS * NA_KH * NA_KH, GRID_W * GRID_W)
    r1 = matmul(x, e1, "nn", F32, f"na_rpb_sel1_{tag}", hi=True, tk=1024)
    r2 = matmul(e2, r1, "nn", F32, f"na_rpb_sel2_{tag}", hi=True)
    return r2.reshape(NA_HEADS, 16, 128)[:, :2 * NA_KH - 1, :2 * NA_KW - 1]


_HALO = 8


def _halo_specs(T, col0):
    nh = TR // _HALO
    cur = pl.BlockSpec((TR, 256), lambda i, j: (i, col0 + j))
    prv = pl.BlockSpec((_HALO, 256), lambda i, j: (jnp.maximum(i * nh - 1, 0), col0 + j))
    nxt = pl.BlockSpec((_HALO, 256), lambda i, j: (jnp.minimum((i + 1) * nh, T // _HALO - 1), col0 + j))
    return prv, cur, nxt


def _fill_ext(ext, prv, cur, nxt, i, nL, nT):
    has_prev = jnp.where((i != 0) & (i != nL), 1.0, 0.0)
    has_next = jnp.where((i != nL - 1) & (i != nT - 1), 1.0, 0.0)
    ext[0:_HALO, :] = prv[...].astype(F32) * has_prev
    ext[_HALO:_HALO + TR, :] = cur[...].astype(F32)
    ext[_HALO + TR:, :] = nxt[...].astype(F32) * has_next


def conv_silu_fwd(P, w8, b, nL, name):
    T = P.shape[0]
    nT = T // TR

    def body(prv, cur, nxt, w_ref, b_ref, pre_ref, act_ref, ext):
        i = pl.program_id(0)
        _fill_ext(ext, prv, cur, nxt, i, nL, nT)
        y = jnp.broadcast_to(b_ref[...], (TR, 256))
        for k in range(S_CONV):
            y = y + w_ref[k:k + 1, :] * ext[pl.ds(_HALO - S_CONV // 2 + k, TR), :]
        pre_ref[...] = y
        act_ref[...] = _silu(y)

    prv, cur, nxt = _halo_specs(T, C_XBC // 256)
    out = pl.BlockSpec((TR, 256), lambda i, j: (i, j))
    return _pc(body, name=name, grid=(nT, 4),
               in_specs=[prv, cur, nxt, pl.BlockSpec((8, 256), lambda i, j: (0, j)), pl.BlockSpec((1, 256), lambda i, j: (0, j))],
               out_specs=[out, out], out_shape=[_sds((T, 1024), F32), _sds((T, 1024), F32)],
               scratch_shapes=[pltpu.VMEM((TR + 2 * _HALO, 256), F32)],
               compiler_params=_cp(("parallel", "parallel"), 16 << 20))(P, P, P, w8, b)


def dsilu(pre, dxs_list, db_list, dc_list, name):
    T = pre.shape[0]
    n1, n2, n3 = len(dxs_list), len(db_list), len(dc_list)

    def body(*refs):
        pre_ref = refs[0]
        ins = refs[1:1 + n1 + n2 + n3]
        out = refs[-1]

        def part(rs, lo, hi):
            g = rs[0][...].astype(F32)
            for r in rs[1:]:
                g = g + r[...].astype(F32)
            x = pre_ref[:, lo:hi]
            sg = 1.0 / (1.0 + jnp.exp(-x))
            sl = x * sg
            out[:, lo:hi] = g * (sg + sl * (1.0 - sg))

        part(ins[:n1], 0, 512)
        part(ins[n1:n1 + n2], 512, 768)
        part(ins[n1 + n2:], 768, 1024)

    spec = lambda w: pl.BlockSpec((TR, w), lambda i: (i, 0))
    return _pc(body, name=name, grid=(T // TR,),
               in_specs=[spec(1024)] + [spec(512)] * n1 + [spec(256)] * (n2 + n3),
               out_specs=spec(1024), out_shape=_sds((T, 1024), F32),
               compiler_params=_cp(("parallel",), 32 << 20))(pre, *dxs_list, *db_list, *dc_list)


def conv_bwd(dpre, P, w8, nL, name):
    T = P.shape[0]
    nT = T // TR

    def body(dp, dc, dn, xp, xc, xn, w_ref, dx_ref, dw_ref, db_ref, extd, extx):
        i = pl.program_id(1)
        _fill_ext(extd, dp, dc, dn, i, nL, nT)
        _fill_ext(extx, xp, xc, xn, i, nL, nT)

        @pl.when(i == 0)
        def _():
            dw_ref[...] = jnp.zeros_like(dw_ref)
            db_ref[...] = jnp.zeros_like(db_ref)

        d = dc[...]
        dx = jnp.zeros((TR, 256), F32)
        for k in range(S_CONV):
            dx = dx + w_ref[k:k + 1, :] * extd[pl.ds(_HALO + S_CONV // 2 - k, TR), :]
            dw_ref[k:k + 1, :] += jnp.sum(d * extx[pl.ds(_HALO - S_CONV // 2 + k, TR), :], axis=0, keepdims=True)
        dx_ref[...] = dx.astype(dx_ref.dtype)
        db_ref[0:1, :] += jnp.sum(d, axis=0, keepdims=True)

    def swap(spec):
        f = spec.index_map
        return pl.BlockSpec(spec.block_shape, lambda j, i: f(i, j))

    dprv, dcur, dnxt = [swap(s) for s in _halo_specs(T, 0)]
    xprv, xcur, xnxt = [swap(s) for s in _halo_specs(T, C_XBC // 256)]
    acc = pl.BlockSpec((8, 256), lambda j, i: (0, j))
    return _pc(body, name=name, grid=(4, nT),
               in_specs=[dprv, dcur, dnxt, xprv, xcur, xnxt, acc],
               out_specs=[pl.BlockSpec((TR, 256), lambda j, i: (i, j)), acc, acc],
               out_shape=[_sds((T, 1024), BF16), _sds((8, 1024), F32), _sds((8, 1024), F32)],
               scratch_shapes=[pltpu.VMEM((TR + 2 * _HALO, 256), F32), pltpu.VMEM((TR + 2 * _HALO, 256), F32)],
               compiler_params=_cp(("parallel", "arbitrary"), 16 << 20))(dpre, dpre, dpre, P, P, P, w8)


def _onehot_row(h, n):
    return (lax.broadcasted_iota(jnp.int32, (1, n), 1) == h).astype(F32)


def _onehot_col(h, n):
    return (lax.broadcasted_iota(jnp.int32, (n, 1), 0) == h).astype(F32)


def _ssd_chunk(xs, dtr, dtb, alog, bm, cm, hin, reverse):
    Qn = S_Q
    ii = lax.broadcasted_iota(jnp.int32, (Qn, Qn), 0)
    jj = lax.broadcasted_iota(jnp.int32, (Qn, Qn), 1)
    keep = (ii <= jj) if reverse else (ii >= jj)
    tri = keep.astype(F32)
    triT = ((jj <= ii) if reverse else (jj >= ii)).astype(F32)
    eye = (ii == jj).astype(F32)
    dt = _softplus(dtr + dtb)
    a = dt * (-jnp.exp(alog))
    cs = hdot(tri, a)
    csT = hdot(a, triT, "tn")
    dtT = hdot(dt, eye, "tn")
    last = _onehot_row(0 if reverse else Qn - 1, Qn)
    ys, houts = [], []
    for g in range(S_GROUPS):
        G = bdot(cm[g], bm[g], "nt")
        for r in range(S_HEADS // S_GROUPS):
            h = g * (S_HEADS // S_GROUPS) + r
            eh_r, eh_c = _onehot_row(h, S_HEADS), _onehot_col(h, S_HEADS)
            cs_c = jnp.sum(cs * eh_r, axis=1, keepdims=True)
            dt_c = jnp.sum(dt * eh_r, axis=1, keepdims=True)
            cs_r = jnp.sum(csT * eh_c, axis=0, keepdims=True)
            dt_r = jnp.sum(dtT * eh_c, axis=0, keepdims=True)
            tot = jnp.sum(cs_r * last, axis=1, keepdims=True)
            decay = jnp.exp(jnp.where(keep, cs_c - cs_r, NEG))
            w = G * decay * dt_r
            y = bdot(w, xs[h], "nn") + bdot(cm[g], hin[h], "nt") * jnp.exp(cs_c)
            xsc = xs[h] * (jnp.exp(tot - cs_c) * dt_c)
            hout = hin[h] * jnp.exp(tot) + bdot(xsc, bm[g], "tn")
            ys.append(y)
            houts.append(hout)
    return ys, houts


def _ssd_orders(L, Lc):
    nl, ncx = L // S_Q, Lc // S_Q
    fwd = lambda s: jnp.where(s < ncx, nl + s, s - ncx)
    bwd = lambda s: nl + ncx - 1 - s
    return nl + ncx, fwd, bwd


def _ssd_in_specs(fo, bo, step):
    def at(order, w, col):
        return pl.BlockSpec((S_Q, w), lambda u: (order(step(u)), col))
    specs = []
    for order in (fo, bo):
        specs += [at(order, 512, 0), at(order, 256, 2), at(order, 256, 3), at(order, 128, C_DT // 128)]
    return specs


def ssd_fwd(act, P, dtb, alog, L, Lc, name):
    T = L + Lc
    ns, fo, bo = _ssd_orders(L, Lc)

    def body(xf, bf, cf, df, xb, bb, cb, db, dtb_ref, al_ref, yf, yb, hsf, hsb, Hf, Hb):
        s = pl.program_id(0)

        @pl.when(s == 0)
        def _():
            Hf[...] = jnp.zeros_like(Hf)
            Hb[...] = jnp.zeros_like(Hb)

        for d, (x_r, b_r, c_r, dt_r, y_r, hs_r, H) in enumerate(((xf, bf, cf, df, yf, hsf, Hf), (xb, bb, cb, db, yb, hsb, Hb))):
            hin = [H[h] for h in range(S_HEADS)]
            hs_r[0] = H[...]
            ys, houts = _ssd_chunk(
                [x_r[:, h * S_P:(h + 1) * S_P] for h in range(S_HEADS)], dt_r[:, d * 8:(d + 1) * 8],
                dtb_ref[d:d + 1, 0:8], al_ref[d:d + 1, 0:8],
                [b_r[:, g * S_N:(g + 1) * S_N] for g in range(S_GROUPS)], [c_r[:, g * S_N:(g + 1) * S_N] for g in range(S_GROUPS)],
                hin, reverse=(d == 1))
            for h in range(S_HEADS):
                y_r[:, h * S_P:(h + 1) * S_P] = ys[h]
                H[h] = houts[h]

    ident = lambda u: u
    small = pl.BlockSpec((8, 128), lambda u: (0, 0))
    hspec = pl.BlockSpec((1, S_HEADS, S_P, S_N), lambda u: (u, 0, 0, 0))
    return _pc(body, name=name, grid=(ns,),
               in_specs=_ssd_in_specs(fo, bo, ident) + [small, small],
               out_specs=[pl.BlockSpec((S_Q, 512), lambda u: (fo(u), 0)), pl.BlockSpec((S_Q, 512), lambda u: (bo(u), 0)), hspec, hspec],
               out_shape=[_sds((T, 512), F32), _sds((T, 512), F32), _sds((ns, S_HEADS, S_P, S_N), F32), _sds((ns, S_HEADS, S_P, S_N), F32)],
               scratch_shapes=[pltpu.VMEM((S_HEADS, S_P, S_N), F32), pltpu.VMEM((S_HEADS, S_P, S_N), F32)],
               compiler_params=_cp(("arbitrary",), 32 << 20))(act, act, act, P, act, act, act, P, dtb, alog)


def ssd_bwd(act, P, dtb, alog, hsf, hsb, dy, L, Lc, name):
    T = L + Lc
    ns, fo, bo = _ssd_orders(L, Lc)
    step = lambda u: ns - 1 - u

    def body(xf, bf, cf, df, xb, bb, cb, db, dtb_ref, al_ref, hsf_r, hsb_r, dyf, dyb,
             dxf, dbf, dcf, ddf, dxb, dbb, dcb, ddb, ddtb, dal, dHf, dHb):
        u = pl.program_id(0)

        @pl.when(u == 0)
        def _():
            dHf[...] = jnp.zeros_like(dHf)
            dHb[...] = jnp.zeros_like(dHb)
            ddtb[...] = jnp.zeros_like(ddtb)
            dal[...] = jnp.zeros_like(dal)

        dirs = ((xf, bf, cf, df, hsf_r, dyf, dxf, dbf, dcf, ddf, dHf), (xb, bb, cb, db, hsb_r, dyb, dxb, dbb, dcb, ddb, dHb))
        for d, (x_r, b_r, c_r, dt_r, hs_r, dy_r, dx_o, db_o, dc_o, dd_o, dH) in enumerate(dirs):
            f = functools.partial(_ssd_chunk, reverse=(d == 1))
            _, vjp = jax.vjp(
                f, [x_r[:, h * S_P:(h + 1) * S_P] for h in range(S_HEADS)], dt_r[:, d * 8:(d + 1) * 8],
                dtb_ref[d:d + 1, 0:8], al_ref[d:d + 1, 0:8],
                [b_r[:, g * S_N:(g + 1) * S_N] for g in range(S_GROUPS)], [c_r[:, g * S_N:(g + 1) * S_N] for g in range(S_GROUPS)],
                [hs_r[0, h] for h in range(S_HEADS)])
            gx, gdt, gdtb, gal, gb, gc, gh = vjp(([dy_r[:, h * S_P:(h + 1) * S_P] for h in range(S_HEADS)],
                                                  [dH[h] for h in range(S_HEADS)]))
            for h in range(S_HEADS):
                dx_o[:, h * S_P:(h + 1) * S_P] = gx[h]
                dH[h] = gh[h]
            for g in range(S_GROUPS):
                db_o[:, g * S_N:(g + 1) * S_N] = gb[g]
                dc_o[:, g * S_N:(g + 1) * S_N] = gc[g]
            dd_o[...] = gdt
            ddtb[d:d + 1, 0:8] += gdtb
            dal[d:d + 1, 0:8] += gal

    small = pl.BlockSpec((8, 128), lambda u: (0, 0))
    hspec = pl.BlockSpec((1, S_HEADS, S_P, S_N), lambda u: (step(u), 0, 0, 0))
    at = lambda order, w: pl.BlockSpec((S_Q, w), lambda u: (order(step(u)), 0))
    outs = []
    for order in (fo, bo):
        outs += [at(order, 512), at(order, 256), at(order, 256), at(order, 8)]
    oshape = [_sds((T, 512), F32), _sds((T, 256), F32), _sds((T, 256), F32), _sds((T, 8), F32)]
    return _pc(body, name=name, grid=(ns,),
               in_specs=_ssd_in_specs(fo, bo, step) + [small, small, hspec, hspec, at(fo, 512), at(bo, 512)],
               out_specs=outs + [small, small], out_shape=oshape + oshape + [_sds((8, 128), F32), _sds((8, 128), F32)],
               scratch_shapes=[pltpu.VMEM((S_HEADS, S_P, S_N), F32), pltpu.VMEM((S_HEADS, S_P, S_N), F32)],
               compiler_params=_cp(("arbitrary",), 40 << 20))(act, act, act, P, act, act, act, P, dtb, alog, hsf, hsb, dy, dy)


def _ssm_out(yf, yb, xs, z, dskip, g):
    y = (yf + yb + dskip * xs) * _silu(z)
    return (y * lax.rsqrt(jnp.mean(y * y, axis=-1, keepdims=True) + EPS)) * g


def ssm_out_fwd(yf, yb, act, P, dskip, g, name):
    T = yf.shape[0]

    def body(yf_r, yb_r, xs_r, z_r, d_r, g_r, o_r):
        o_r[...] = _ssm_out(yf_r[...], yb_r[...], xs_r[...], z_r[...], d_r[...], g_r[...]).astype(o_r.dtype)

    row = pl.BlockSpec((TR, 512), lambda i: (i, 0))
    vec = pl.BlockSpec((1, 512), lambda i: (0, 0))
    return _pc(body, name=name, grid=(T // TR,),
               in_specs=[row, row, row, pl.BlockSpec((TR, 512), lambda i: (i, C_Z // 512)), vec, vec],
               out_specs=row, out_shape=_sds((T, 512), BF16),
               compiler_params=_cp(("parallel",), 16 << 20))(yf, yb, act, P, dskip, g)


def ssm_out_bwd(yf, yb, act, P, dskip, g, do_src, name):
    T = yf.shape[0]

    def body(yf_r, yb_r, xs_r, z_r, d_r, g_r, do_r, dy_r, dxs_r, dz_r, dv_r):
        @pl.when(pl.program_id(0) == 0)
        def _():
            dv_r[...] = jnp.zeros_like(dv_r)

        _, vjp = jax.vjp(_ssm_out, yf_r[...], yb_r[...], xs_r[...], z_r[...], d_r[...], g_r[...])
        dyf, _, dxs, dz, dd, dg = vjp(do_r[...].astype(F32))
        dy_r[...] = dyf
        dxs_r[...] = dxs
        dz_r[...] = dz.astype(dz_r.dtype)
        dv_r[0:1, :] += dd
        dv_r[1:2, :] += dg

    row = pl.BlockSpec((TR, 512), lambda i: (i, 0))
    vec = pl.BlockSpec((1, 512), lambda i: (0, 0))
    return _pc(body, name=name, grid=(T // TR,),
               in_specs=[row, row, row, pl.BlockSpec((TR, 512), lambda i: (i, C_Z // 512)), vec, vec,
                         pl.BlockSpec((TR, 512), lambda i: (i, 1))],
               out_specs=[row, row, row, pl.BlockSpec((8, 512), lambda i: (0, 0))],
               out_shape=[_sds((T, 512), F32), _sds((T, 512), F32), _sds((T, 512), BF16), _sds((8, 512), F32)],
               compiler_params=_cp(("arbitrary",), 24 << 20))(yf, yb, act, P, dskip, g, do_src)


def add_halves(xv, got, cvec, name):
    n, r, cdim = xv.shape
    h = r // 2

    def body(c_ref, x_ref, g_ref, o_ref):
        o_ref[...] = (x_ref[...].astype(F32) + g_ref[...].astype(F32)).astype(o_ref.dtype)

    gs = pltpu.PrefetchScalarGridSpec(
        num_scalar_prefetch=1, grid=(n,),
        in_specs=[pl.BlockSpec((1, h, cdim), lambda k, c_ref: (k, c_ref[0], 0)), pl.BlockSpec((1, h, cdim), lambda k, c_ref: (k, 0, 0))],
        out_specs=pl.BlockSpec((1, h, cdim), lambda k, c_ref: (k, 0, 0)))
    return _pc(body, name=name, grid_spec=gs, out_shape=_sds((n, h, cdim), BF16),
               compiler_params=_cp(("arbitrary",), 24 << 20))(cvec, xv, got)


def sum_slots(a, name):
    n, r, cdim = a.shape
    tr = _div_tile(r, 512, 16)

    def body(a_ref, o_ref):
        acc = a_ref[0].astype(F32)
        for k in range(1, n):
            acc = acc + a_ref[k].astype(F32)
        o_ref[...] = acc

    return _pc(body, name=name, grid=(r // tr,), in_specs=[pl.BlockSpec((n, tr, cdim), lambda i: (0, i, 0))],
               out_specs=pl.BlockSpec((tr, cdim), lambda i: (i, 0)), out_shape=_sds((r, cdim), F32),
               compiler_params=_cp(("parallel",), 32 << 20))(a)


def adamw(w, g, m, v, name):
    B, R, C = w.shape
    tr = _div_tile(R, max(8, (1 << 19) // max(C, 1) // 8 * 8), 8) if R % 8 == 0 else R
    c1 = 1.0 / (1.0 - ADAM_B1 ** ADAM_STEP)
    c2 = 1.0 / (1.0 - ADAM_B2 ** ADAM_STEP)

    def body(w_ref, g_ref, m_ref, v_ref, d_ref, mo_ref, vo_ref):
        gg = g_ref[...]
        mn = ADAM_B1 * m_ref[...] + (1.0 - ADAM_B1) * gg
        vn = ADAM_B2 * v_ref[...] + (1.0 - ADAM_B2) * (gg * gg)
        d_ref[...] = -ADAM_LR * ((mn * c1) / (jnp.sqrt(vn * c2) + ADAM_EPS) + ADAM_WD * w_ref[...])
        mo_ref[...] = mn
        vo_ref[...] = vn

    spec = pl.BlockSpec((1, tr, C), lambda b, i: (b, i, 0))
    return _pc(body, name=name, grid=(B, R // tr), in_specs=[spec] * 4, out_specs=[spec] * 3,
               out_shape=[_sds((B, R, C), F32)] * 3, compiler_params=_cp(("parallel", "parallel"), 32 << 20))(w, g, m, v)


def _me():
    return lax.axis_index("x"), lax.axis_index("y"), lax.axis_index("c")


def _flip(v, bit):
    return 1 - v if bit else v


def allgather8(xv, name):
    R = xv.shape[0]

    def body(x_ref, out_ref, sum_ref, send_sems, recv_sems):
        mx, my, mc = _me()
        me = 4 * mx + 2 * my + mc
        out_ref[me] = x_ref[...]
        sends, recvs = [], []
        for k in range(1, 8):
            px, py, pc = _flip(mx, k & 4), _flip(my, k & 2), _flip(mc, k & 1)
            peer = 4 * px + 2 * py + pc
            sends.append(pltpu.make_async_remote_copy(src_ref=x_ref, dst_ref=out_ref.at[me], send_sem=send_sems.at[k - 1],
                                                      recv_sem=recv_sems.at[k - 1], device_id=(px, py, pc), device_id_type=MESH))
            recvs.append(pltpu.make_async_remote_copy(src_ref=x_ref, dst_ref=out_ref.at[peer], send_sem=send_sems.at[k - 1],
                                                      recv_sem=recv_sems.at[k - 1], device_id=(px, py, pc), device_id_type=MESH))
        for cp in sends:
            cp.start()
        for cp in recvs:
            cp.wait_recv()
        for cp in sends:
            cp.wait_send()
        acc = out_ref[0]
        for d in range(1, 8):
            acc = acc + out_ref[d]
        sum_ref[...] = acc

    vm = pl.BlockSpec(memory_space=pltpu.VMEM)
    return _pc(body, name=name, pin=False, in_specs=[vm], out_specs=[vm, vm], out_shape=[_sds((8, R, 128), F32), _sds((R, 128), F32)],
               scratch_shapes=[pltpu.SemaphoreType.DMA((7,)), pltpu.SemaphoreType.DMA((7,))],
               compiler_params=_cp(None, 32 << 20))(xv)


def _other_chips(mx, my):
    return [(1 - mx, my), (mx, 1 - my), (1 - mx, 1 - my)]


def _halves(r, mc, mult):
    h = r // 2
    return pl.ds(pl.multiple_of(mc * h, mult), h), pl.ds(pl.multiple_of((1 - mc) * h, mult), h)


def _rcopy(src, dst, send_sems, recv_sems, k, to):
    return pltpu.make_async_remote_copy(src_ref=src, dst_ref=dst, send_sem=send_sems.at[k], recv_sem=recv_sems.at[k],
                                        device_id=to, device_id_type=MESH)


def _gather_body(xs, outs, send_sems, recv_sems):
    n = len(xs)
    mx, my, mc = _me()
    chip = 2 * mx + my
    sib = (mx, my, 1 - mc)
    chips = _other_chips(mx, my)
    idx = [2 * cx + cy for cx, cy in chips]
    cp = functools.partial(_rcopy, send_sems=send_sems, recv_sems=recv_sems)
    hv = [_halves(x.shape[0], mc, 16) for x in xs]
    first, passed = [], []
    for a in range(n):
        for j, (cx, cy) in enumerate(chips):
            first.append(cp(xs[a].at[hv[a][0]], outs[a].at[chip, hv[a][0]], k=6 * a + j, to=(cx, cy, mc)))
            first[-1].start()
    for a in range(n):
        for j in range(3):
            cp(xs[a].at[hv[a][0]], outs[a].at[idx[j], hv[a][0]], k=6 * a + j, to=sib).wait_recv()
            passed.append(cp(outs[a].at[idx[j], hv[a][0]], outs[a].at[idx[j], hv[a][0]], k=6 * a + 3 + j, to=sib))
            passed[-1].start()
    for a in range(n):
        for j in range(3):
            cp(xs[a].at[hv[a][1]], outs[a].at[idx[j], hv[a][1]], k=6 * a + 3 + j, to=sib).wait_recv()
    for c_ in first + passed:
        c_.wait_send()


def _my_chip():
    return 2 * lax.axis_index("x") + lax.axis_index("y")


def _own_slots(outs, shards):
    return [lax.dynamic_update_index_in_dim(o, x, _my_chip(), 0) for o, x in zip(outs, shards)]


def gather_weights(shards, name):
    n = len(shards)

    def body(*refs):
        _gather_body(refs[:n], refs[n:2 * n], *refs[2 * n:])

    hbm = pl.BlockSpec(memory_space=pl.ANY)
    outs = _pc(body, name=name, in_specs=[hbm] * n, out_specs=[hbm] * n, out_shape=[_sds((4,) + x.shape, x.dtype) for x in shards],
               scratch_shapes=[pltpu.SemaphoreType.DMA((6 * n,)), pltpu.SemaphoreType.DMA((6 * n,))])(*shards)
    return _own_slots(outs, shards)


GATHER_REST_ID = 3


def gather_weights_sc(shards, name):
    n = len(shards)
    x_refs = [jax.new_ref(x, memory_space=pltpu.MemorySpace.HBM) for x in shards]
    out_refs = [jax.empty_ref(_sds((4,) + x.shape, x.dtype), memory_space=pltpu.MemorySpace.HBM) for x in shards]

    @pl.kernel(mesh=plsc.ScalarSubcoreMesh(axis_name="sc", num_cores=1), name=name,
               scratch_types=(pltpu.SemaphoreType.DMA((6 * n,)), pltpu.SemaphoreType.DMA((6 * n,))),
               compiler_params=pltpu.CompilerParams(collective_id=GATHER_REST_ID))
    def launch(send_sems, recv_sems):
        mx, my, mc = _me()
        barrier = pltpu.get_barrier_semaphore()
        for peer in [(mx, my, 1 - mc)] + [(cx, cy, mc) for cx, cy in _other_chips(mx, my)]:
            pl.semaphore_signal(barrier, inc=1, device_id=peer, device_id_type=MESH)
        pl.semaphore_wait(barrier, 4)
        _gather_body(x_refs, out_refs, send_sems, recv_sems)

    launch()
    return _own_slots([o[...] for o in out_refs], shards)


def swap_halves(arrs, name):
    n = len(arrs)

    def body(*refs):
        xs, outs = refs[:n], refs[n:2 * n]
        send_sems, recv_sems = refs[2 * n:]
        mx, my, mc = _me()
        cps = []
        for a in range(n):
            theirs = _halves(xs[a].shape[1], mc, 16)[1]
            cps.append(_rcopy(xs[a].at[pl.ds(0, 4), theirs], outs[a], send_sems, recv_sems, a, (mx, my, 1 - mc)))
            cps[-1].start()
        for c_ in cps:
            c_.wait()

    hbm = pl.BlockSpec(memory_space=pl.ANY)
    return _pc(body, name=name, in_specs=[hbm] * n, out_specs=[hbm] * n,
               out_shape=[_sds((4, x.shape[1] // 2, x.shape[2]), x.dtype) for x in arrs],
               scratch_shapes=[pltpu.SemaphoreType.DMA((n,)), pltpu.SemaphoreType.DMA((n,))])(*arrs)


SCATTER_ID = 4


def scatter_chips_sc(arrs, name):
    n = len(arrs)
    x_refs = [jax.new_ref(x, memory_space=pltpu.MemorySpace.HBM) for x in arrs]
    out_refs = [jax.empty_ref(_sds(x.shape, x.dtype), memory_space=pltpu.MemorySpace.HBM) for x in arrs]

    @pl.kernel(mesh=plsc.ScalarSubcoreMesh(axis_name="sc", num_cores=1), name=name,
               scratch_types=(pltpu.SemaphoreType.DMA((3 * n,)), pltpu.SemaphoreType.DMA((3 * n,))),
               compiler_params=pltpu.CompilerParams(collective_id=SCATTER_ID))
    def launch(send_sems, recv_sems):
        mx, my, mc = _me()
        chip = 2 * mx + my
        chips = _other_chips(mx, my)
        idx = [2 * cx + cy for cx, cy in chips]
        barrier = pltpu.get_barrier_semaphore()
        for cx, cy in chips:
            pl.semaphore_signal(barrier, inc=1, device_id=(cx, cy, mc), device_id_type=MESH)
        pl.semaphore_wait(barrier, 3)
        cp = functools.partial(_rcopy, send_sems=send_sems, recv_sems=recv_sems)
        sends = []
        for a in range(n):
            for j, (cx, cy) in enumerate(chips):
                sends.append(cp(x_refs[a].at[idx[j]], out_refs[a].at[chip], k=3 * a + j, to=(cx, cy, mc)))
                sends[-1].start()
        for a in range(n):
            for j, (cx, cy) in enumerate(chips):
                cp(x_refs[a].at[idx[j]], out_refs[a].at[idx[j]], k=3 * a + j, to=(cx, cy, mc)).wait_recv()
        for c_ in sends:
            c_.wait_send()

    launch()
    return _own_slots([o[...] for o in out_refs], [lax.dynamic_index_in_dim(x, _my_chip(), 0, keepdims=False) for x in arrs])


def share_halves(parts, name):
    flat = [p for w in parts for p in w]
    nw, n = len(parts), len(flat)
    depth = n // nw

    def body(*refs):
        xs, outs = refs[:n], refs[n:n + nw]
        send_sems, recv_sems = refs[n + nw:]
        mx, my, mc = _me()
        sib = (mx, my, 1 - mc)
        sends, recvs = [], []
        for a in range(n):
            w, l = a // depth, a % depth
            mine, theirs = _halves(outs[w].shape[1], mc, 8)
            sends.append(_rcopy(xs[a], outs[w].at[l, mine], send_sems, recv_sems, a, sib))
            recvs.append(_rcopy(xs[a], outs[w].at[l, theirs], send_sems, recv_sems, a, sib))
            sends[-1].start()
        for c_ in recvs:
            c_.wait_recv()
        for c_ in sends:
            c_.wait_send()

    hbm = pl.BlockSpec(memory_space=pl.ANY)
    outs = _pc(body, name=name, in_specs=[hbm] * n, out_specs=[hbm] * nw,
               out_shape=[_sds((depth, 2 * w[0].shape[0], w[0].shape[1]), F32) for w in parts],
               scratch_shapes=[pltpu.SemaphoreType.DMA((n,)), pltpu.SemaphoreType.DMA((n,))])(*flat)
    outs = list(outs)
    mc = lax.axis_index("c")
    for w in range(nw):
        for l in range(depth):
            h = parts[w][l].shape[0]
            outs[w] = lax.dynamic_update_slice(outs[w], parts[w][l][None], (l, mc * h, 0))
    return outs


_BIG = ("w_in", "w_out", "w_ffn_in", "w_ffn_out")
N_CHIPS = 4
DEPTH = 2


def _pad_rows(v, mult=8):
    n = v.shape[0]
    rows = -(-n // 128)
    rows = -(-rows // mult) * mult
    return jnp.pad(v, (0, rows * 128 - n)).reshape(rows, 128)


class _Flat:
    def __init__(self):
        self.items = []

    def add(self, name, a):
        self.items.append((name, a.shape, a.reshape(-1).astype(F32)))

    def rows(self):
        return _pad_rows(jnp.concatenate([a for _, _, a in self.items]))

    def split(self, rows):
        flat = rows.reshape(-1)
        out, o = {}, 0
        for name, shape, a in self.items:
            out[name] = flat[o:o + a.shape[0]].reshape(shape)
            o += a.shape[0]
        return out

    def split_lead(self, rows3):
        n = rows3.shape[0]
        flat = rows3.reshape(n, -1)
        out, o = {}, 0
        for name, shape, a in self.items:
            out[name] = flat[:, o:o + a.shape[0]].reshape((n,) + tuple(shape))
            o += a.shape[0]
        return out


def _gsv(rows):
    z = jnp.zeros((2, D), F32)
    r = [z if a is None else a for a in rows] + [z] * 5
    return jnp.stack(r, axis=1)


def _pad8(a, rows=8, cols=128):
    return jnp.zeros((rows, cols), F32).at[:a.shape[0], :a.shape[1]].set(a.astype(F32))


def kernel(x, c, ctx, c_ctx, w_mod, b_mod, g_mix, w_in, wa_sink, na_rpb, ssm_conv_w, ssm_conv_b, ssm_dt_bias, ssm_a_log, ssm_d, ssm_norm_g, w_out, g_ffn, w_ffn_in, w_ffn_out, g_final, loss_target, m_c_ctx, m_w_mod, m_b_mod, m_g_mix, m_w_in, m_wa_sink, m_na_rpb, m_ssm_conv_w, m_ssm_conv_b, m_ssm_dt_bias, m_ssm_a_log, m_ssm_d, m_ssm_norm_g, m_w_out, m_g_ffn, m_w_ffn_in, m_w_ffn_out, m_g_final, v_c_ctx, v_w_mod, v_b_mod, v_g_mix, v_w_in, v_wa_sink, v_na_rpb, v_ssm_conv_w, v_ssm_conv_b, v_ssm_dt_bias, v_ssm_a_log, v_ssm_d, v_ssm_norm_g, v_w_out, v_g_ffn, v_w_ffn_in, v_w_ffn_out, v_g_final):
    L, Lc = x.shape[1], ctx.shape[1]
    T = L + Lc
    nL = L // TR
    mx, my, mc = lax.axis_index("x"), lax.axis_index("y"), lax.axis_index("c")
    dev = 4 * mx + 2 * my + mc
    chip = 2 * mx + my
    MODW = 6 * D // N_CHIPS
    CW = 1024 // N_CHIPS

    sc = _silu(c.astype(F32))
    scc = _silu(c_ctx.astype(F32))[None]
    f1 = _Flat()
    f1.add("sc", sc)
    f1.add("conv_w", ssm_conv_w)
    g1, _ = allgather8(f1.rows(), "gather_cond")
    g1 = f1.split_lead(g1)
    sc_all = g1["sc"][:, 0]
    conv_w = jnp.concatenate([g1["conv_w"][2 * k] for k in range(N_CHIPS)], axis=-1)
    A16 = jnp.concatenate([sc_all, scc, jnp.zeros((7, D), F32)], axis=0)

    mod_part = jnp.stack([matmul(A16, w_mod[l], "nn", F32, f"mod_fwd{l}") for l in range(DEPTH)])
    f2 = _Flat()
    f2.add("mod", mod_part)
    g2, _ = allgather8(f2.rows(), "gather_mod")
    g2 = f2.split_lead(g2)["mod"]
    mods = jnp.concatenate([g2[2 * k] for k in range(N_CHIPS)], axis=-1) + b_mod[:, None, :]
    mod_l = lax.dynamic_index_in_dim(mods, dev, axis=1, keepdims=False).reshape(DEPTH, 6, D)
    mod_c = mods[:, 8].reshape(DEPTH, 6, D)
    mod = jnp.stack([mod_l, mod_c], axis=1)
    mrow = lambda l, j: mod[l, :, j]

    own = {"w_in": w_in, "w_out": w_out, "w_ffn_in": w_ffn_in, "w_ffn_out": w_ffn_out}
    sh16 = [own[n][l].astype(BF16) for n in _BIG for l in range(DEPTH)]
    gath = list(gather_weights_sc(sh16[:1], "gather_first"))
    after_mod = (g2[0, 0, 0, 0] * 0).astype(BF16)
    gath += list(gather_weights_sc([sh16[1] + after_mod] + sh16[2:], "gather_rest"))
    gw = {n: [gath[DEPTH * i + l] for l in range(DEPTH)] for i, n in enumerate(_BIG)}
    W_in = [jnp.pad(jnp.concatenate([g[k] for k in range(N_CHIPS)], axis=1), ((0, 0), (0, IN_PAD - IN_COLS))) for g in gw["w_in"]]
    W_out = [g.reshape(D, D) for g in gw["w_out"]]
    W_fo = [g.reshape(D_FF, D) for g in gw["w_ffn_out"]]
    W_fi = gw["w_ffn_in"]

    cos, sin, rotm = rope_tables(L, Lc)
    x0 = jnp.concatenate([x[0], ctx[0]], axis=0).astype(F32)

    sv = []
    xin = x0
    gsv_first = _gsv([None, mrow(0, 0), mrow(0, 1)])
    _, h1 = res_norm_mod(x0, None, gsv_first, g_mix[0][None], nL, "norm_first")
    for l in range(DEPTH):
        s = {"xin": xin, "h1": h1}
        P = matmul(h1, W_in[l], "nn", F32, f"in_proj{l}", tn=IN_PAD)
        qr, kr, kb, vb = rope_apply(P, C_QA // 256, P, C_KA // 128, cos, sin, rotm, False, f"rope{l}", kv_src=P)
        sink8 = _pad8(jnp.broadcast_to(wa_sink[l][:, None], (WA_HEADS, 128)))
        oa, sta = win_attn_fwd(qr, kr, P, sink8, L, Lc, f"wa_fwd{l}")
        bias = na_bias_table(na_rpb[l], l)
        ob, stb = na_fwd(P, kb, vb, bias, L, Lc, f"na_fwd{l}")
        w8 = jnp.concatenate([conv_w[l], jnp.zeros((1, 1024), F32)], axis=0)
        pre, act = conv_silu_fwd(P, w8, ssm_conv_b[l][None], nL, f"conv_fwd{l}")
        dtb8, al8 = _pad8(ssm_dt_bias[l]), _pad8(ssm_a_log[l])
        yf, yb, hsf, hsb = ssd_fwd(act, P, dtb8, al8, L, Lc, f"ssd_fwd{l}")
        dskip = jnp.repeat(ssm_d[l], S_P)[None]
        oc = ssm_out_fwd(yf, yb, act, P, dskip, ssm_norm_g[l][None], f"ssm_out_fwd{l}")
        mixin = jnp.concatenate([oa, ob, oc], axis=1)
        mix = matmul(mixin, W_out[l], "nn", BF16, f"out_proj{l}")
        gsv_mid = _gsv([mrow(l, 2), mrow(l, 3), mrow(l, 4)])
        x1, h2 = res_norm_mod(xin, mix, gsv_mid, g_ffn[l][None], nL, f"norm_mid{l}")
        gu = matmul_fi(h2, W_fi[l], "nn", BF16, f"ffn_in{l}")
        af = swiglu_fwd(gu, f"swiglu_fwd{l}")
        fo = matmul(af, W_fo[l], "nn", BF16, f"ffn_out{l}", tk=D_FF)
        s.update(P=P, qr=qr, kr=kr, sink8=sink8, oa=oa, sta=sta, ob=ob, stb=stb, kb=kb, vb=vb, bias=bias, w8=w8, pre=pre, act=act, dtb8=dtb8, al8=al8, yf=yf,
                 yb=yb, hsf=hsf, hsb=hsb, dskip=dskip, mixin=mixin, mix=mix, gsv_mid=gsv_mid, x1=x1, h2=h2, gu=gu, af=af, fo=fo)
        if l + 1 < DEPTH:
            s["gsv_end"] = _gsv([mrow(l, 5), mrow(l + 1, 0), mrow(l + 1, 1)])
            xin, h1 = res_norm_mod(x1, fo, s["gsv_end"], g_mix[l + 1][None], nL, f"norm_end{l}")
        else:
            s["gsv_end"] = _gsv([mrow(l, 5), None, None])
        sv.append(s)

    last = sv[-1]
    loss8, dres, dfo, dgsv_end, dg_final = final_loss(last["x1"], last["fo"], last["gsv_end"], g_final[None], loss_target[0].astype(F32), nL, "final_loss")
    loss = lax.psum(loss8[0, 0], ("x", "y", "c"))

    dmod = [[None] * 6 for _ in range(DEPTH)]
    gW = {n: [None] * DEPTH for n in _BIG}
    small = [dict() for _ in range(DEPTH)]
    parts = [None] * DEPTH
    cvec = mc.astype(jnp.int32).reshape(1)
    grad_x = None
    for l in reversed(range(DEPTH)):
        s = sv[l]
        dmod[l][5] = dgsv_end[:, 0]
        if l + 1 < DEPTH:
            dmod[l + 1][0], dmod[l + 1][1] = dgsv_end[:, 1], dgsv_end[:, 2]
        daf = matmul(dfo, W_fo[l], "nt", BF16, f"ffn_out_dx{l}")
        gW["w_ffn_out"][l] = matmul(s["af"], dfo, "tn", BF16, f"ffn_out_dw{l}", tm=1408, tk=T).reshape(N_CHIPS, D_FF // N_CHIPS, D)
        dgu = swiglu_bwd(s["gu"], daf, f"swiglu_bwd{l}")
        dh2 = matmul_fi(dgu, W_fi[l], "nt", BF16, f"ffn_in_dx{l}")
        gW["w_ffn_in"][l] = matmul_fi(s["h2"], dgu, "tn", BF16, f"ffn_in_dw{l}")
        dres, dmix, dgsv_mid, dg_ffn = res_norm_mod_bwd(s["x1"], s["mix"], s["gsv_mid"], g_ffn[l][None], dh2, dres, nL, f"norm_mid_bwd{l}")
        dmod[l][2], dmod[l][3], dmod[l][4] = dgsv_mid[:, 0], dgsv_mid[:, 1], dgsv_mid[:, 2]
        dmixin = matmul(dmix, W_out[l], "nt", BF16, f"out_proj_dx{l}")
        gW["w_out"][l] = matmul(s["mixin"], dmix, "tn", BF16, f"out_proj_dw{l}", tm=1024, tk=T).reshape(N_CHIPS, D // N_CHIPS, D)
        P = s["P"]
        dqr, dkr, dva, dsink = win_attn_bwd(s["qr"], s["kr"], P, s["sink8"], dmixin, s["oa"], s["sta"], L, Lc, f"wa_bwd{l}")
        dqa, dka = rope_apply(dqr, 0, dkr[WA_BLK:WA_BLK + T], 0, cos, sin, rotm, True, f"rope_bwd{l}")
        dqb, dkb, dvb, dbias = na_bwd(P, s["kb"], s["vb"], s["bias"], dmixin, s["ob"], s["stb"], L, Lc, f"na_bwd{l}")
        dy, dxs1, dz, dvec = ssm_out_bwd(s["yf"], s["yb"], s["act"], P, s["dskip"], ssm_norm_g[l][None], dmixin, f"ssm_out_bwd{l}")
        dxf, dbf, dcf, ddf, dxb, dbb, dcb, ddb, ddtb, dal = ssd_bwd(s["act"], P, s["dtb8"], s["al8"], s["hsf"], s["hsb"], dy, L, Lc, f"ssd_bwd{l}")
        dpre = dsilu(s["pre"], [dxf, dxb, dxs1], [dbf, dbb], [dcf, dcb], f"dsilu{l}")
        dxbc, dw8, db8 = conv_bwd(dpre, P, s["w8"], nL, f"conv_bwd{l}")
        ddt = jnp.concatenate([ddf, ddb, jnp.zeros((T, IN_PAD - IN_COLS), F32)], axis=1)
        pieces = [(dqa, C_QA), (dqb, C_QB), (dz, C_Z), (dka, C_KA), (dva[WA_BLK:WA_BLK + T], C_VA), (dkb, C_KB), (dvb, C_VB),
                  (dxbc, C_XBC), (ddt, C_DT)]
        dh1, dwin = in_proj_bwd(pieces, s["h1"], W_in[l], f"in_proj_bwd{l}")
        cw = IN_COLS // N_CHIPS
        gW["w_in"][l] = jnp.stack([dwin[:, k * cw:(k + 1) * cw] for k in range(N_CHIPS)])
        garr = [gW[n][l] for n in _BIG]
        got = swap_halves(garr, f"reduce_d2d{l}")
        chip_sum = [add_halves(garr[a], got[a], cvec, f"reduce_add_pair{l}_{a}") for a in range(len(garr))]
        parts[l] = scatter_chips_sc(chip_sum, f"reduce_ici{l}")
        small[l] = dict(g_ffn=dg_ffn[0], wa_sink=dsink[:WA_HEADS, 0], na_rpb=na_rpb_grad(dbias, l), conv_w=dw8[:S_CONV], conv_b=db8[0],
                        dt_bias=ddtb[:2, :8], a_log=dal[:2, :8], ssm_d=dvec[0].reshape(S_HEADS, S_P).sum(axis=1), norm_g=dvec[1])
        if l > 0:
            p = sv[l - 1]
            dres, dfo, dgsv_end, dg_mix = res_norm_mod_bwd(s["xin"], p["fo"], p["gsv_end"], g_mix[l][None], dh1, dres, nL, f"norm_end_bwd{l - 1}")
        else:
            grad_x, _, dgsv_first, dg_mix = res_norm_mod_bwd(s["xin"], None, gsv_first, g_mix[0][None], dh1, dres, nL, "norm_first_bwd")
            dmod[0][0], dmod[0][1] = dgsv_first[:, 1], dgsv_first[:, 2]
        small[l]["g_mix"] = dg_mix[0]
    for l in range(DEPTH):
        for j in range(6):
            if dmod[l][j] is None:
                dmod[l][j] = jnp.zeros((2, D), F32)
    dmod = jnp.stack([jnp.stack(r, axis=1) for r in dmod])

    f3 = _Flat()
    f3.add("dmod_l", dmod[:, 0].reshape(DEPTH, 6 * D))
    f3.add("dmod_c", dmod[:, 1].reshape(DEPTH, 6 * D))
    f3.add("g_final", dg_final[0])
    for n in ("g_mix", "g_ffn", "wa_sink", "na_rpb", "conv_w", "conv_b", "dt_bias", "a_log", "ssm_d", "norm_g"):
        f3.add(n, jnp.stack([small[l][n] for l in range(DEPTH)]))
    g3, s3 = allgather8(f3.rows(), "reduce_small")
    dmod_all = f3.split_lead(g3)["dmod_l"]
    s3 = f3.split(s3)
    dmodc_tot = s3["dmod_c"]
    col0 = chip * MODW
    G16, G16c = [], []
    for l in range(DEPTH):
        rows = jnp.concatenate([dmod_all[:, l], dmodc_tot[l][None], jnp.zeros((7, 6 * D), F32)], axis=0)
        G16.append(lax.dynamic_slice_in_dim(rows, col0, MODW, axis=1))
        rc = jnp.concatenate([dmodc_tot[l][None], jnp.zeros((15, 6 * D), F32)], axis=0)
        G16c.append(lax.dynamic_slice_in_dim(rc, col0, MODW, axis=1))
    grad_w_mod = jnp.stack([matmul(A16, G16[l], "tn", F32, f"mod_dw{l}") for l in range(DEPTH)])
    dscc_part = sum(matmul(G16c[l], w_mod[l], "nt", F32, f"mod_dx{l}")[0] for l in range(DEPTH))
    _, s4 = allgather8(_pad_rows(dscc_part * (mc == 1).astype(F32)), "reduce_cctx")
    dscc = s4.reshape(-1)[:D]
    cc = c_ctx.astype(F32)
    sg = 1.0 / (1.0 + jnp.exp(-cc))
    grad_c_ctx = dscc * (sg * (1.0 + cc * (1.0 - sg)))

    halves = [[sum_slots(parts[l][i], f"reduce_add_chips{l}_{i}") for l in range(DEPTH)] for i in range(len(_BIG))]
    gsh = dict(zip(_BIG, share_halves(halves, "reduce_share")))

    grads = {"c_ctx": grad_c_ctx, "w_mod": grad_w_mod, "b_mod": s3["dmod_l"] + s3["dmod_c"], "g_mix": s3["g_mix"], "w_in": gsh["w_in"],
             "wa_sink": s3["wa_sink"], "na_rpb": s3["na_rpb"],
             "ssm_conv_w": lax.dynamic_slice_in_dim(s3["conv_w"], chip * CW, CW, axis=2), "ssm_conv_b": s3["conv_b"],
             "ssm_dt_bias": s3["dt_bias"], "ssm_a_log": s3["a_log"], "ssm_d": s3["ssm_d"], "ssm_norm_g": s3["norm_g"],
             "w_out": gsh["w_out"], "g_ffn": s3["g_ffn"], "w_ffn_in": gsh["w_ffn_in"], "w_ffn_out": gsh["w_ffn_out"], "g_final": s3["g_final"]}
    wts = {"c_ctx": c_ctx, "w_mod": w_mod, "b_mod": b_mod, "g_mix": g_mix, "w_in": w_in, "wa_sink": wa_sink, "na_rpb": na_rpb,
           "ssm_conv_w": ssm_conv_w, "ssm_conv_b": ssm_conv_b, "ssm_dt_bias": ssm_dt_bias, "ssm_a_log": ssm_a_log, "ssm_d": ssm_d,
           "ssm_norm_g": ssm_norm_g, "w_out": w_out, "g_ffn": g_ffn, "w_ffn_in": w_ffn_in, "w_ffn_out": w_ffn_out, "g_final": g_final}
    ms = {"c_ctx": m_c_ctx, "w_mod": m_w_mod, "b_mod": m_b_mod, "g_mix": m_g_mix, "w_in": m_w_in, "wa_sink": m_wa_sink, "na_rpb": m_na_rpb,
          "ssm_conv_w": m_ssm_conv_w, "ssm_conv_b": m_ssm_conv_b, "ssm_dt_bias": m_ssm_dt_bias, "ssm_a_log": m_ssm_a_log, "ssm_d": m_ssm_d,
          "ssm_norm_g": m_ssm_norm_g, "w_out": m_w_out, "g_ffn": m_g_ffn, "w_ffn_in": m_w_ffn_in, "w_ffn_out": m_w_ffn_out, "g_final": m_g_final}
    vs = {"c_ctx": v_c_ctx, "w_mod": v_w_mod, "b_mod": v_b_mod, "g_mix": v_g_mix, "w_in": v_w_in, "wa_sink": v_wa_sink, "na_rpb": v_na_rpb,
          "ssm_conv_w": v_ssm_conv_w, "ssm_conv_b": v_ssm_conv_b, "ssm_dt_bias": v_ssm_dt_bias, "ssm_a_log": v_ssm_a_log, "ssm_d": v_ssm_d,
          "ssm_norm_g": v_ssm_norm_g, "w_out": v_w_out, "g_ffn": v_g_ffn, "w_ffn_in": v_w_ffn_in, "w_ffn_out": v_w_ffn_out, "g_final": v_g_final}
    names = list(wts)
    grads = {n: grads[n].reshape(wts[n].shape).astype(F32) for n in names}
    big = ("w_mod", "w_in", "w_out", "w_ffn_in", "w_ffn_out")
    delta, new_m, new_v = {}, {}, {}
    for n in big:
        delta[n], new_m[n], new_v[n] = adamw(wts[n], grads[n], ms[n], vs[n], f"adamw_{n}")
    packs = []
    for src in (wts, grads, ms, vs):
        f = _Flat()
        for n in names:
            if n not in big:
                f.add(n, src[n])
        packs.append(f)
    d_, m_, v_ = adamw(*[f.rows()[None] for f in packs], "adamw_small")
    for dst, rows in ((delta, d_), (new_m, m_), (new_v, v_)):
        dst.update(packs[0].split(rows[0]))

    return (loss, grad_x[:L][None], *[grads[n] for n in names], *[delta[n] for n in names],
            *[new_m[n] for n in names], *[new_v[n] for n in names])
```

```python
import functools

import numpy as np
import jax
import jax.numpy as jnp
from jax import lax
from jax.experimental import pallas as pl
from jax.experimental.pallas import tpu as pltpu
from jax.experimental.pallas import tpu_sc as plsc

F32 = jnp.float32
BF16 = jnp.bfloat16
_MXU = jnp.bfloat16
_HI = lax.Precision.HIGHEST
MESH = pl.DeviceIdType.MESH

D = 1024
HD = 64
GRID_W = 64
EPS = 1e-6
ROPE_BASE = 10000.0
WA_HEADS, WA_KV = 4, 2
WA_BLK = 128
NA_HEADS, NA_KH, NA_KW = 4, 8, 16
S_HEADS, S_P, S_INNER, S_GROUPS, S_N, S_CONV, S_Q = 8, 64, 512, 2, 128, 7, 128
D_FF = 2816
IN_COLS = 2832
IN_PAD = 2944
C_QA, C_QB, C_Z, C_KA, C_VA, C_KB, C_VB, C_XBC, C_DT = 0, 256, 512, 1024, 1152, 1280, 1536, 1792, 2816
ADAM_LR, ADAM_B1, ADAM_B2, ADAM_EPS, ADAM_WD, ADAM_STEP = 0.001, 0.9, 0.999, 1e-08, 0.01, 10

TR = 256
NEG = -1e30
VMEM_CAP = 56 * 1024 * 1024


PIN_BYTES = 256 * 1024


def _is_big(a):
    return hasattr(a, "shape") and len(a.shape) >= 2 and int(np.prod(a.shape)) * jnp.dtype(a.dtype).itemsize >= PIN_BYTES


def _pc(body, *, out_shape, pin=True, **kw):
    if not pin:
        return pl.pallas_call(body, out_shape=out_shape, **kw)
    one = isinstance(out_shape, jax.ShapeDtypeStruct)
    outs = [pltpu.HBM(s.shape, s.dtype) if _is_big(s) else s for s in ([out_shape] if one else out_shape)]
    call = pl.pallas_call(body, out_shape=outs[0] if one else outs, **kw)
    return lambda *args: call(*[pltpu.with_memory_space_constraint(a, pltpu.HBM) if _is_big(a) else a for a in args])


def _cp(sem=None, vmem=None):
    kw = {}
    if sem is not None:
        kw["dimension_semantics"] = sem
    if vmem is not None:
        kw["vmem_limit_bytes"] = int(min(max(vmem, 16 * 1024 * 1024), VMEM_CAP))
    return pltpu.CompilerParams(**kw)


def _sds(shape, dtype):
    return jax.ShapeDtypeStruct(tuple(shape), dtype)


_DIMS = {"nn": ((1,), (0,)), "nt": ((1,), (1,)), "tn": ((0,), (0,))}


def _dg(a, b, dims):
    return lax.dot_general(a.astype(_MXU), b.astype(_MXU), (dims, ((), ())), preferred_element_type=F32)


@functools.partial(jax.custom_vjp, nondiff_argnums=(2,))
def bdot(a, b, mode):
    return _dg(a, b, _DIMS[mode])


def _bdot_fwd(a, b, mode):
    return bdot(a, b, mode), (a, b)


def _bdot_bwd(mode, res, g):
    a, b = res
    if mode == "nn":
        return bdot(g, b, "nt"), bdot(a, g, "tn")
    if mode == "nt":
        return bdot(g, b, "nn"), bdot(g, a, "tn")
    return bdot(b, g, "nt"), bdot(a, g, "nn")


bdot.defvjp(_bdot_fwd, _bdot_bwd)


def hdot(a, b, mode="nn"):
    return lax.dot_general(a, b, (_DIMS[mode], ((), ())), precision=_HI, preferred_element_type=F32)


def _silu(x):
    return x / (1.0 + jnp.exp(-x))


def _softplus(x):
    return jnp.maximum(x, 0.0) + jnp.log(1.0 + jnp.exp(-jnp.abs(x)))


def _div_tile(n, cap, mult):
    if n <= cap:
        return n
    best = None
    for t in range(mult, cap + 1, mult):
        if n % t == 0:
            best = t
    assert best is not None, (n, cap, mult)
    return best


def matmul(a, b, mode, out_dtype, name, tm=640, tn=1536, tk=1408, hi=False):
    if mode == "tn":
        K, M = a.shape
    else:
        M, K = a.shape
    N = b.shape[0] if mode == "nt" else b.shape[1]
    tm = _div_tile(M, tm, 128 if mode == "tn" else 16)
    tn = _div_tile(N, tn, 128)
    tk = _div_tile(K, tk, 128 if mode != "tn" else 16)
    nk = K // tk
    dims = _DIMS[mode]

    def body(a_ref, b_ref, o_ref, *acc):
        if hi:
            part = lax.dot_general(a_ref[...], b_ref[...], (dims, ((), ())), precision=_HI, preferred_element_type=F32)
        else:
            part = _dg(a_ref[...], b_ref[...], dims)
        if nk == 1:
            o_ref[...] = part.astype(o_ref.dtype)
        else:
            k = pl.program_id(2)

            @pl.when(k == 0)
            def _():
                acc[0][...] = part

            @pl.when(k > 0)
            def _():
                acc[0][...] += part

            @pl.when(k == nk - 1)
            def _():
                o_ref[...] = acc[0][...].astype(o_ref.dtype)

    if mode == "tn":
        a_spec = pl.BlockSpec((tk, tm), lambda i, j, k: (k, i))
    else:
        a_spec = pl.BlockSpec((tm, tk), lambda i, j, k: (i, k))
    if mode == "nt":
        b_spec = pl.BlockSpec((tn, tk), lambda i, j, k: (j, k))
    else:
        b_spec = pl.BlockSpec((tk, tn), lambda i, j, k: (k, j))
    isz = lambda x: jnp.dtype(x.dtype).itemsize
    vmem = 2 * (tm * tk * isz(a) + tk * tn * isz(b) + tm * tn * jnp.dtype(out_dtype).itemsize) + 3 * tm * tn * 4
    return _pc(
        body, name=name, grid=(M // tm, N // tn, nk),
        in_specs=[a_spec, b_spec], out_specs=pl.BlockSpec((tm, tn), lambda i, j, k: (i, j)),
        out_shape=_sds((M, N), out_dtype),
        scratch_shapes=[pltpu.VMEM((tm, tn), F32)] if nk > 1 else [],
        compiler_params=_cp(("parallel", "parallel", "arbitrary"), vmem + (8 << 20)),
    )(a, b)


def in_proj_bwd(pieces, h1, w, name):
    T = h1.shape[0]
    arrs = [a for a, _ in pieces]
    offs = [o for _, o in pieces]
    wid = [a.shape[1] for a in arrs]
    n = len(arrs)
    assert sum(wid) == IN_PAD, "the pieces must tile all columns of P"
    tm = _div_tile(T, 640, 16)

    def dx_body(*refs):
        w_ref, o_ref = refs[n], refs[n + 1]
        acc = None
        for j in range(n):
            part = _dg(refs[j][...], w_ref[:, offs[j]:offs[j] + wid[j]], _DIMS["nt"])
            acc = part if acc is None else acc + part
        o_ref[...] = acc.astype(o_ref.dtype)

    dh1 = _pc(dx_body, name=name + "_dx", grid=(T // tm,),
              in_specs=[pl.BlockSpec((tm, wj), lambda i: (i, 0)) for wj in wid] + [pl.BlockSpec((D, IN_PAD), lambda i: (0, 0))],
              out_specs=pl.BlockSpec((tm, D), lambda i: (i, 0)), out_shape=_sds((T, D), BF16),
              compiler_params=_cp(("parallel",), 40 << 20))(*arrs, w)

    tmd, nk = 512, 4
    tk = T // nk

    def dw_body(h_ref, *refs):
        o_ref, acc = refs[n], refs[n + 1]
        k = pl.program_id(1)

        @pl.when(k == 0)
        def _():
            acc[...] = jnp.zeros_like(acc)

        for j in range(n):
            acc[:, offs[j]:offs[j] + wid[j]] += _dg(h_ref[...], refs[j][...], _DIMS["tn"])

        @pl.when(k == nk - 1)
        def _():
            o_ref[...] = acc[...].astype(o_ref.dtype)

    dw = _pc(dw_body, name=name + "_dw", grid=(D // tmd, nk),
             in_specs=[pl.BlockSpec((tk, tmd), lambda i, k: (k, i))] + [pl.BlockSpec((tk, wj), lambda i, k: (k, 0)) for wj in wid],
             out_specs=pl.BlockSpec((tmd, IN_PAD), lambda i, k: (i, 0)), out_shape=_sds((D, IN_PAD), BF16),
             scratch_shapes=[pltpu.VMEM((tmd, IN_PAD), F32)], compiler_params=_cp(("parallel", "arbitrary"), 48 << 20))(h1, *arrs)
    return dh1, dw


def _norm_mod(xo, shift, scale, g):
    r = lax.rsqrt(jnp.mean(xo * xo, axis=-1, keepdims=True) + EPS)
    return (xo * r) * g * (1.0 + scale) + shift


def res_norm_mod(x, y, gsv, g, nL, name):
    T = x.shape[0]
    has_y = y is not None

    def body(*refs):
        if has_y:
            x_ref, y_ref, gsv_ref, g_ref, xo_ref, h_ref = refs
            xo = x_ref[...] + gsv_ref[0, 0:1, :] * y_ref[...]
            xo_ref[...] = xo
        else:
            x_ref, gsv_ref, g_ref, h_ref = refs
            xo = x_ref[...]
        h_ref[...] = _norm_mod(xo, gsv_ref[0, 1:2, :], gsv_ref[0, 2:3, :], g_ref[...]).astype(h_ref.dtype)

    row = pl.BlockSpec((TR, D), lambda i: (i, 0))
    in_specs = [row] + ([row] if has_y else []) + [pl.BlockSpec((1, 8, D), lambda i: (i // nL, 0, 0)),
                                                     pl.BlockSpec((1, D), lambda i: (0, 0))]
    out_specs = ([row] if has_y else []) + [row]
    out_shape = ([_sds((T, D), F32)] if has_y else []) + [_sds((T, D), BF16)]
    args = (x, y, gsv, g) if has_y else (x, gsv, g)
    outs = _pc(body, name=name, grid=(T // TR,), in_specs=in_specs, out_specs=out_specs, out_shape=out_shape,
               compiler_params=_cp(("arbitrary",), 24 << 20))(*args)
    return (outs[0], outs[1]) if has_y else (None, outs[0])


def res_norm_mod_bwd(xo, y, gsv, g, dh, dres, nL, name):
    T = xo.shape[0]
    has_y = y is not None

    def body(*refs):
        if has_y:
            xo_ref, y_ref, gsv_ref, g_ref, dh_ref, dres_ref, dx_ref, dy_ref, dgsv_ref, dg_ref = refs
        else:
            xo_ref, gsv_ref, g_ref, dh_ref, dres_ref, dx_ref, dgsv_ref, dg_ref = refs
        i = pl.program_id(0)

        @pl.when((i == 0) | (i == nL))
        def _():
            dgsv_ref[...] = jnp.zeros_like(dgsv_ref)

        @pl.when(i == 0)
        def _():
            dg_ref[...] = jnp.zeros_like(dg_ref)

        _, vjp = jax.vjp(_norm_mod, xo_ref[...], gsv_ref[0, 1:2, :], gsv_ref[0, 2:3, :], g_ref[...])
        dxn, dshift, dscale, dg = vjp(dh_ref[...].astype(F32))
        dxo = dres_ref[...] + dxn
        dx_ref[...] = dxo
        if has_y:
            dy_ref[...] = (gsv_ref[0, 0:1, :] * dxo).astype(dy_ref.dtype)
            dgsv_ref[0, 0:1, :] += jnp.sum(y_ref[...] * dxo, axis=0, keepdims=True)
        dgsv_ref[0, 1:2, :] += dshift
        dgsv_ref[0, 2:3, :] += dscale
        dg_ref[0:1, :] += dg

    row = pl.BlockSpec((TR, D), lambda i: (i, 0))
    gspec = pl.BlockSpec((1, 8, D), lambda i: (i // nL, 0, 0))
    in_specs = [row] + ([row] if has_y else []) + [gspec, pl.BlockSpec((1, D), lambda i: (0, 0)), row, row]
    out_specs = [row] + ([row] if has_y else []) + [gspec, pl.BlockSpec((8, D), lambda i: (0, 0))]
    out_shape = [_sds((T, D), F32)] + ([_sds((T, D), BF16)] if has_y else []) + [_sds((2, 8, D), F32), _sds((8, D), F32)]
    args = (xo, y, gsv, g, dh, dres) if has_y else (xo, gsv, g, dh, dres)
    outs = _pc(body, name=name, grid=(T // TR,), in_specs=in_specs, out_specs=out_specs, out_shape=out_shape,
               compiler_params=_cp(("arbitrary",), 32 << 20))(*args)
    if has_y:
        return outs
    return outs[0], None, outs[1], outs[2]


def final_loss(x, y, gsv, g, target, nL, name):
    T = x.shape[0]

    def lossf(xo, gv, t):
        yn = (xo * lax.rsqrt(jnp.mean(xo * xo, axis=-1, keepdims=True) + EPS)) * gv
        e = yn - t
        return 0.5 * jnp.sum(jnp.sum(e * e, axis=-1, keepdims=True) * (1.0 / D), axis=0, keepdims=True)

    def body(x_ref, y_ref, gsv_ref, g_ref, t_ref, loss_ref, dx_ref, dy_ref, dgsv_ref, dg_ref):
        i = pl.program_id(0)

        @pl.when(i == 0)
        def _():
            loss_ref[...] = jnp.zeros_like(loss_ref)
            dg_ref[...] = jnp.zeros_like(dg_ref)

        @pl.when((i == 0) | (i == nL))
        def _():
            dgsv_ref[...] = jnp.zeros_like(dgsv_ref)

        @pl.when(i < nL)
        def _():
            gate = gsv_ref[0, 0:1, :]
            yv = y_ref[...]
            xo = x_ref[...] + gate * yv
            lv, vjp = jax.vjp(lossf, xo, g_ref[...], t_ref[...])
            dxo, dg, _ = vjp(jnp.ones((1, 1), F32))
            loss_ref[...] += jnp.broadcast_to(lv, loss_ref.shape)
            dx_ref[...] = dxo
            dy_ref[...] = (gate * dxo).astype(dy_ref.dtype)
            dgsv_ref[0, 0:1, :] += jnp.sum(yv * dxo, axis=0, keepdims=True)
            dg_ref[0:1, :] += dg

        @pl.when(i >= nL)
        def _():
            dx_ref[...] = jnp.zeros_like(dx_ref)
            dy_ref[...] = jnp.zeros_like(dy_ref)

    row = pl.BlockSpec((TR, D), lambda i: (i, 0))
    gspec = pl.BlockSpec((1, 8, D), lambda i: (i // nL, 0, 0))
    return _pc(
        body, name=name, grid=(T // TR,),
        in_specs=[row, row, gspec, pl.BlockSpec((1, D), lambda i: (0, 0)),
                  pl.BlockSpec((TR, D), lambda i: (jnp.minimum(i, nL - 1), 0))],
        out_specs=[pl.BlockSpec((8, 128), lambda i: (0, 0)), row, row, gspec, pl.BlockSpec((8, D), lambda i: (0, 0))],
        out_shape=[_sds((8, 128), F32), _sds((T, D), F32), _sds((T, D), BF16), _sds((2, 8, D), F32), _sds((8, D), F32)],
        compiler_params=_cp(("arbitrary",), 32 << 20),
    )(x, y, gsv, g, target)


FI_BLK = 2 * D_FF // 4


def _fi_chip(j):
    return (j % 2) * 2 + j // 2


def matmul_fi(a, b, mode, out_dtype, name):
    T = a.shape[0]
    if mode == "tn":
        tmd = 512

        def body(a_ref, b_ref, o_ref):
            o_ref[0] = _dg(a_ref[...], b_ref[...], _DIMS["tn"]).astype(o_ref.dtype)

        return _pc(body, name=name, grid=(D // tmd, 4),
                   in_specs=[pl.BlockSpec((T, tmd), lambda i, j: (0, i)), pl.BlockSpec((T, FI_BLK), lambda i, j: (0, j))],
                   out_specs=pl.BlockSpec((1, tmd, FI_BLK), lambda i, j: (_fi_chip(j), i, 0)),
                   out_shape=_sds((4, D, FI_BLK), out_dtype), compiler_params=_cp(("parallel", "arbitrary"), 48 << 20))(a, b)
    if mode == "nn":
        tm = _div_tile(T, 1280, 16)

        def body(a_ref, b_ref, o_ref):
            o_ref[...] = _dg(a_ref[...], b_ref[0], _DIMS["nn"]).astype(o_ref.dtype)

        return _pc(body, name=name, grid=(T // tm, 4),
                   in_specs=[pl.BlockSpec((tm, D), lambda i, j: (i, 0)), pl.BlockSpec((1, D, FI_BLK), lambda i, j: (_fi_chip(j), 0, 0))],
                   out_specs=pl.BlockSpec((tm, FI_BLK), lambda i, j: (i, j)), out_shape=_sds((T, 4 * FI_BLK), out_dtype),
                   compiler_params=_cp(("parallel", "arbitrary"), 40 << 20))(a, b)
    tm = _div_tile(T, 640, 16)

    def body(a_ref, b_ref, o_ref):
        acc = None
        for k in range(4):
            part = _dg(a_ref[:, k * FI_BLK:(k + 1) * FI_BLK], b_ref[_fi_chip(k)], _DIMS["nt"])
            acc = part if acc is None else acc + part
        o_ref[...] = acc.astype(o_ref.dtype)

    return _pc(body, name=name, grid=(T // tm,),
               in_specs=[pl.BlockSpec((tm, 4 * FI_BLK), lambda i: (i, 0)), pl.BlockSpec((4, D, FI_BLK), lambda i: (0, 0, 0))],
               out_specs=pl.BlockSpec((tm, D), lambda i: (i, 0)), out_shape=_sds((T, D), out_dtype),
               compiler_params=_cp(("parallel",), VMEM_CAP))(a, b)


def _swiglu(gate, up):
    return _silu(gate) * up


def swiglu_fwd(gu, name):
    T = gu.shape[0]

    def body(x_ref, o_ref):
        o_ref[...] = _swiglu(x_ref[:, :FI_BLK].astype(F32), x_ref[:, FI_BLK:].astype(F32)).astype(o_ref.dtype)

    return _pc(body, name=name, grid=(T // TR, 2), in_specs=[pl.BlockSpec((TR, 2 * FI_BLK), lambda i, j: (i, j))],
               out_specs=pl.BlockSpec((TR, FI_BLK), lambda i, j: (i, j)), out_shape=_sds((T, D_FF), BF16),
               compiler_params=_cp(("parallel", "parallel"), 24 << 20))(gu)


def swiglu_bwd(gu, dact, name):
    T = gu.shape[0]

    def body(x_ref, d_ref, o_ref):
        g, u, d = x_ref[:, :FI_BLK].astype(F32), x_ref[:, FI_BLK:].astype(F32), d_ref[...].astype(F32)
        sg = 1.0 / (1.0 + jnp.exp(-g))
        sl = g * sg
        o_ref[:, :FI_BLK] = (d * u * (sg + sl * (1.0 - sg))).astype(o_ref.dtype)
        o_ref[:, FI_BLK:] = (d * sl).astype(o_ref.dtype)

    return _pc(body, name=name, grid=(T // TR, 2),
               in_specs=[pl.BlockSpec((TR, 2 * FI_BLK), lambda i, j: (i, j)), pl.BlockSpec((TR, FI_BLK), lambda i, j: (i, j))],
               out_specs=pl.BlockSpec((TR, 2 * FI_BLK), lambda i, j: (i, j)), out_shape=_sds((T, 2 * D_FF), BF16),
               compiler_params=_cp(("parallel", "parallel"), 32 << 20))(gu, dact)


def rope_tables(L, Lc):
    t = np.arange(L)
    rows, cols = t // GRID_W, t % GRID_W
    inv = ROPE_BASE ** (-np.arange(16, dtype=np.float32) / 16)
    lane = np.arange(64)
    pos = np.where((lane // 32)[None, :] == 0, rows[:, None], cols[:, None]).astype(np.float32)
    ang = jnp.asarray(pos) * jnp.asarray(inv[lane % 16])[None, :]
    cos = jnp.concatenate([jnp.cos(ang), jnp.ones((Lc, 64), F32)], axis=0)
    sin = jnp.concatenate([jnp.sin(ang), jnp.zeros((Lc, 64), F32)], axis=0)
    R = np.zeros((128, 128), np.float32)
    for i in range(128):
        if (i % 32) < 16:
            R[i + 16, i] = -1.0
        else:
            R[i - 16, i] = 1.0
    return jnp.tile(cos, (1, 2)), jnp.tile(sin, (1, 2)), jnp.asarray(R)


def rope_apply(q_src, q_col, k_src, k_col, cos, sin, R, transpose, name, kv_src=None):
    T = cos.shape[0]
    with_kv = kv_src is not None

    def rot(x, c, s, Rm):
        if transpose:
            return x * c + hdot(x * s, Rm, "nt")
        return x * c + hdot(x, Rm) * s

    def body(q_ref, k_ref, c_ref, s_ref, R_ref, *rest):
        qo_ref, ko_ref = rest[-4:-2] if with_kv else rest
        c, s, Rm = c_ref[...], s_ref[...], R_ref[...]
        for j in range(2):
            qo_ref[:, j * 128:(j + 1) * 128] = rot(q_ref[:, j * 128:(j + 1) * 128].astype(F32), c, s, Rm).astype(qo_ref.dtype)
        ko_ref[...] = rot(k_ref[...].astype(F32), c, s, Rm).astype(ko_ref.dtype)
        if with_kv:
            rest[-2][...] = rest[0][...].astype(BF16)
            rest[-1][...] = rest[1][...].astype(BF16)

    tab = pl.BlockSpec((TR, 128), lambda i: (i, 0))
    wide = pl.BlockSpec((TR, 256), lambda i: (i, 0))
    kv_in = [pl.BlockSpec((TR, 256), lambda i: (i, C_KB // 256)), pl.BlockSpec((TR, 256), lambda i: (i, C_VB // 256))] if with_kv else []
    return _pc(body, name=name, grid=(T // TR,),
               in_specs=[pl.BlockSpec((TR, 256), lambda i: (i, q_col)), pl.BlockSpec((TR, 128), lambda i: (i, k_col)),
                         tab, tab, pl.BlockSpec((128, 128), lambda i: (0, 0))] + kv_in,
               out_specs=[wide, tab] + ([wide, wide] if with_kv else []),
               out_shape=[_sds((T, 256), BF16), _sds((T, 128), BF16)] + ([_sds((T, 256), BF16)] * 2 if with_kv else []),
               compiler_params=_cp(("parallel",), 16 << 20))(q_src, k_src, cos, sin, R, *([kv_src, kv_src] if with_kv else []))


_SCALE = HD ** -0.5


def _attn_tile(qh, ks, vs, extra):
    ss = []
    for k, add in ks:
        s = _dg(qh, k, _DIMS["nt"]) * _SCALE
        ss.append(s if add is None else s + add)
    m = ss[0].max(axis=-1, keepdims=True)
    for s in ss[1:]:
        m = jnp.maximum(m, s.max(axis=-1, keepdims=True))
    if extra is not None:
        m = jnp.maximum(m, extra)
    ps = [jnp.exp(s - m) for s in ss]
    den = ps[0].sum(axis=-1, keepdims=True)
    for p in ps[1:]:
        den = den + p.sum(axis=-1, keepdims=True)
    if extra is not None:
        den = den + jnp.exp(extra - m)
    num = _dg(ps[0], vs[0], _DIMS["nn"])
    for p, v in zip(ps[1:], vs[1:]):
        num = num + _dg(p, v, _DIMS["nn"])
    linv = 1.0 / den
    return num * linv, m, linv


def _attn_bwd_tile(qh, ks, vs, extra, m, linv, oh, doh):
    delta = jnp.sum(doh * oh, axis=-1, keepdims=True)
    dq = None
    dks, dvs, dss = [], [], []
    for (k, add), v in zip(ks, vs):
        s = _dg(qh, k, _DIMS["nt"]) * _SCALE
        if add is not None:
            s = s + add
        p = jnp.exp(s - m) * linv
        dvs.append(_dg(p, doh, _DIMS["tn"]))
        ds = p * (_dg(doh, v, _DIMS["nt"]) - delta)
        dss.append(ds)
        dsq = ds * _SCALE
        part = _dg(dsq, k, _DIMS["nn"])
        dq = part if dq is None else dq + part
        dks.append(_dg(dsq, qh, _DIMS["tn"]))
    dextra = None
    if extra is not None:
        dextra = -jnp.sum(jnp.exp(extra - m) * linv * delta, axis=0, keepdims=True)
    return dq, dks, dvs, dss, dextra


def _wa_mask(n, L):
    qpos = n * WA_BLK + lax.broadcasted_iota(jnp.int32, (WA_BLK, 3 * WA_BLK), 0)
    kpos = (n - 1) * WA_BLK + lax.broadcasted_iota(jnp.int32, (WA_BLK, 3 * WA_BLK), 1)
    ok = (jnp.abs(qpos - kpos) <= WA_BLK) & (kpos >= 0) & (kpos < L)
    return jnp.where(ok, 0.0, NEG).astype(F32)


WA_BPS = 2


def _wa_specs(L, Lc):
    nb = L // WA_BLK
    cb = L // Lc

    def blk(j, col):
        return pl.BlockSpec((WA_BLK, 128), lambda s: (jnp.clip(s * WA_BPS - 1 + j, 0, nb - 1), col))

    vcol = C_VA // 128
    kspecs = [blk(j, 0) for j in range(WA_BPS + 2)] + [pl.BlockSpec((Lc, 128), lambda s: (cb, 0))]
    vspecs = [blk(j, vcol) for j in range(WA_BPS + 2)] + [pl.BlockSpec((Lc, 128), lambda s: (cb, vcol))]
    return nb, kspecs, vspecs


def win_attn_fwd(qr, kr, P, sink, L, Lc, name):
    T = L + Lc
    nb, kspecs, vspecs = _wa_specs(L, Lc)
    nk = WA_BPS + 2
    QB = WA_BPS * WA_BLK
    nlat = nb // WA_BPS

    def body(q_ref, *refs):
        kbs, kx, vbs, vx, s_ref, o_ref, st_ref = refs[:nk], refs[nk], refs[nk + 1:2 * nk + 1], refs[2 * nk + 1], refs[-3], refs[-2], refs[-1]
        s = pl.program_id(0)

        def put(qs, h, res):
            o, m, linv = res
            o_ref[qs, h * HD:(h + 1) * HD] = o.astype(o_ref.dtype)
            st_ref[qs, h:h + 1] = m
            st_ref[qs, WA_HEADS + h:WA_HEADS + h + 1] = linv

        @pl.when(s < nlat)
        def _():
            for b in range(WA_BPS):
                mask = _wa_mask(s * WA_BPS + b, L)
                qs = slice(b * WA_BLK, (b + 1) * WA_BLK)
                for g in range(WA_KV):
                    sl = slice(g * HD, (g + 1) * HD)
                    k3 = jnp.concatenate([kbs[b + j][:, sl] for j in range(3)], axis=0)
                    v3 = jnp.concatenate([vbs[b + j][:, sl] for j in range(3)], axis=0)
                    for r in range(2):
                        h = 2 * g + r
                        put(qs, h, _attn_tile(q_ref[qs, h * HD:(h + 1) * HD], [(k3, mask), (kx[:, sl], None)], [v3, vx[:, sl]], s_ref[h:h + 1, 0:1]))

        @pl.when(s >= nlat)
        def _():
            for h in range(WA_HEADS):
                sl = slice((h // 2) * HD, (h // 2 + 1) * HD)
                put(slice(None), h, _attn_tile(q_ref[:, h * HD:(h + 1) * HD], [(kx[:, sl], None)], [vx[:, sl]], s_ref[h:h + 1, 0:1]))

    qspec = pl.BlockSpec((QB, 256), lambda s: (s, 0))
    return _pc(body, name=name, grid=(T // QB,),
               in_specs=[qspec] + kspecs + vspecs + [pl.BlockSpec((8, 128), lambda s: (0, 0))],
               out_specs=[qspec, pl.BlockSpec((QB, 8), lambda s: (s, 0))], out_shape=[_sds((T, 256), BF16), _sds((T, 8), F32)],
               compiler_params=_cp(("arbitrary",), 32 << 20))(qr, *([kr] * (nk + 1)), *([P] * (nk + 1)), sink)


def win_attn_bwd(qr, kr, P, sink, do_src, o, stats, L, Lc, name):
    T = L + Lc
    nb, kspecs, vspecs = _wa_specs(L, Lc)
    nk = WA_BPS + 2
    QB = WA_BPS * WA_BLK
    nlat = nb // WA_BPS
    cx = WA_BLK + L

    def body(q_ref, *refs):
        kbs, kx, vbs, vx = refs[:nk], refs[nk], refs[nk + 1:2 * nk + 1], refs[2 * nk + 1]
        s_ref, do_ref, o_ref, st_ref, dq_ref, dk_ref, dv_ref, ds_ref = refs[2 * nk + 2:]
        s = pl.program_id(0)

        @pl.when(s == 0)
        def _():
            dk_ref[...] = jnp.zeros_like(dk_ref)
            dv_ref[...] = jnp.zeros_like(dv_ref)
            ds_ref[...] = jnp.zeros_like(ds_ref)

        def tile(qs, h, ks, vs):
            hs = slice(h * HD, (h + 1) * HD)
            dq, dks, dvs, _, dsk = _attn_bwd_tile(q_ref[qs, hs], ks, vs, s_ref[h:h + 1, 0:1], st_ref[qs, h:h + 1],
                                                  st_ref[qs, WA_HEADS + h:WA_HEADS + h + 1], o_ref[qs, hs].astype(F32), do_ref[qs, hs].astype(F32))
            dq_ref[qs, hs] = dq
            ds_ref[h:h + 1, :] += jnp.broadcast_to(dsk, (1, 128))
            return dks, dvs

        @pl.when(s < nlat)
        def _():
            for b in range(WA_BPS):
                n = s * WA_BPS + b
                mask = _wa_mask(n, L)
                rows = pl.ds(pl.multiple_of(n * WA_BLK, WA_BLK), 3 * WA_BLK)
                qs = slice(b * WA_BLK, (b + 1) * WA_BLK)
                for g in range(WA_KV):
                    sl = slice(g * HD, (g + 1) * HD)
                    k3 = jnp.concatenate([kbs[b + j][:, sl] for j in range(3)], axis=0)
                    v3 = jnp.concatenate([vbs[b + j][:, sl] for j in range(3)], axis=0)
                    acc = None
                    for r in range(2):
                        dks, dvs = tile(qs, 2 * g + r, [(k3, mask), (kx[:, sl], None)], [v3, vx[:, sl]])
                        acc = dks + dvs if acc is None else [a + b_ for a, b_ in zip(acc, dks + dvs)]
                    dk_ref[rows, sl] += acc[0]
                    dk_ref[cx:cx + Lc, sl] += acc[1]
                    dv_ref[rows, sl] += acc[2]
                    dv_ref[cx:cx + Lc, sl] += acc[3]

        @pl.when(s >= nlat)
        def _():
            for h in range(WA_HEADS):
                sl = slice((h // 2) * HD, (h // 2 + 1) * HD)
                dks, dvs = tile(slice(None), h, [(kx[:, sl], None)], [vx[:, sl]])
                dk_ref[cx:cx + Lc, sl] += dks[0]
                dv_ref[cx:cx + Lc, sl] += dvs[0]

    qspec = pl.BlockSpec((QB, 256), lambda s: (s, 0))
    acc_spec = pl.BlockSpec((T + 2 * WA_BLK, 128), lambda s: (0, 0))
    return _pc(body, name=name, grid=(T // QB,),
               in_specs=[qspec] + kspecs + vspecs + [pl.BlockSpec((8, 128), lambda s: (0, 0)), qspec, qspec, pl.BlockSpec((QB, 8), lambda s: (s, 0))],
               out_specs=[qspec, acc_spec, acc_spec, pl.BlockSpec((8, 128), lambda s: (0, 0))],
               out_shape=[_sds((T, 256), F32), _sds((T + 2 * WA_BLK, 128), F32), _sds((T + 2 * WA_BLK, 128), F32), _sds((8, 128), F32)],
               compiler_params=_cp(("arbitrary",), 40 << 20))(qr, *([kr] * (nk + 1)), *([P] * (nk + 1)), sink, do_src, o, stats)


def na_index_tables():
    qc = np.arange(GRID_W)[:, None]
    kc = np.arange(GRID_W)[None, :]
    cstart = np.clip(qc - NA_KW // 2, 0, GRID_W - NA_KW)
    ok = (kc >= cstart) & (kc < cstart + NA_KW)
    dx = np.clip(kc - qc, -(NA_KW - 1), NA_KW - 1) + (NA_KW - 1)
    off = np.arange(NA_KH)[:, None]
    kr = np.arange(NA_KH)[None, :]
    dy = kr - off + (NA_KH - 1)
    return ok, dx, dy


def _na_selectors():
    ok, dx, dy = na_index_tables()
    e1 = np.zeros((GRID_W * GRID_W, 128), np.float32)
    qi, ki = np.nonzero(ok)
    e1[qi * GRID_W + ki, dx[qi, ki]] = 1.0
    e2 = np.zeros((16, NA_KH * NA_KH), np.float32)
    oi, ri = np.meshgrid(np.arange(NA_KH), np.arange(NA_KH), indexing="ij")
    e2[dy[oi, ri].ravel(), (oi * NA_KH + ri).ravel()] = 1.0
    return ok, jnp.asarray(e1), jnp.asarray(np.kron(np.eye(NA_HEADS, dtype=np.float32), e2))


def na_bias_table(rpb, tag):
    ok, e1, e2 = _na_selectors()
    r2 = jnp.pad(rpb.astype(F32), ((0, 0), (0, 1), (0, 128 - (2 * NA_KW - 1)))).reshape(NA_HEADS * 16, 128)
    r1 = matmul(e2, r2, "tn", F32, f"na_bias_sel1_{tag}", hi=True)
    x = matmul(r1, e1, "nt", F32, f"na_bias_sel2_{tag}", hi=True)
    b = x.reshape(NA_HEADS, NA_KH, NA_KH, GRID_W, GRID_W).transpose(0, 1, 3, 2, 4)
    b = b + jnp.asarray(np.where(ok, 0.0, NEG).astype(np.float32))[None, None, :, None, :]
    return b.reshape(NA_HEADS, NA_KH, GRID_W, NA_KH * GRID_W)


def _na_rows(r, GR):
    r0 = jnp.clip(r - NA_KH // 2, 0, GR - NA_KH)
    return r0, jnp.clip(r - r0, 0, NA_KH - 1)


NA_RPS = 4


def na_fwd(P, kb, vb, bias, L, Lc, name):
    T = L + Lc
    GR = L // GRID_W
    W = NA_KH * GRID_W
    QB = GRID_W * NA_RPS
    nlat = GR // NA_RPS

    def body(q_ref, k_ref, v_ref, b_ref, o_ref, st_ref):
        s = pl.program_id(0)

        def put(qs, h, res):
            o, m, linv = res
            o_ref[qs, h * HD:(h + 1) * HD] = o.astype(o_ref.dtype)
            st_ref[qs, h:h + 1] = m
            st_ref[qs, NA_HEADS + h:NA_HEADS + h + 1] = linv

        @pl.when(s < nlat)
        def _():
            for rr in range(NA_RPS):
                r0, off = _na_rows(s * NA_RPS + rr, GR)
                rows = pl.ds(pl.multiple_of(r0 * GRID_W, GRID_W), W)
                qs = slice(rr * GRID_W, (rr + 1) * GRID_W)
                for h in range(NA_HEADS):
                    hs = slice(h * HD, (h + 1) * HD)
                    put(qs, h, _attn_tile(q_ref[qs, hs], [(k_ref[rows, hs], b_ref[h, off]), (k_ref[L:T, hs], None)],
                                          [v_ref[rows, hs], v_ref[L:T, hs]], None))

        @pl.when(s >= nlat)
        def _():
            for h in range(NA_HEADS):
                hs = slice(h * HD, (h + 1) * HD)
                put(slice(None), h, _attn_tile(q_ref[:, hs], [(k_ref[L:T, hs], None)], [v_ref[L:T, hs]], None))

    one = pl.Buffered(1)
    return _pc(body, name=name, grid=(T // QB,),
               in_specs=[pl.BlockSpec((QB, 256), lambda r: (r, C_QB // 256)),
                         pl.BlockSpec((T, 256), lambda r: (0, 0), pipeline_mode=one),
                         pl.BlockSpec((T, 256), lambda r: (0, 0), pipeline_mode=one),
                         pl.BlockSpec((NA_HEADS, NA_KH, GRID_W, W), lambda r: (0, 0, 0, 0), pipeline_mode=one)],
               out_specs=[pl.BlockSpec((QB, 256), lambda r: (r, 0)), pl.BlockSpec((QB, 8), lambda r: (r, 0))],
               out_shape=[_sds((T, 256), BF16), _sds((T, 8), F32)],
               compiler_params=_cp(("arbitrary",), 32 << 20))(P, kb, vb, bias)


def na_bwd(P, kb, vb, bias, do_src, o, stats, L, Lc, name):
    T = L + Lc
    GR = L // GRID_W
    W = NA_KH * GRID_W
    QB = GRID_W * NA_RPS
    nlat = GR // NA_RPS

    def body(q_ref, k_ref, v_ref, b_ref, do_ref, o_ref, st_ref, dq_ref, dk_ref, dv_ref, db_ref):
        s = pl.program_id(0)

        @pl.when(s == 0)
        def _():
            dk_ref[...] = jnp.zeros_like(dk_ref)
            dv_ref[...] = jnp.zeros_like(dv_ref)
            db_ref[...] = jnp.zeros_like(db_ref)

        def tile(qs, h, ks, vs):
            hs = slice(h * HD, (h + 1) * HD)
            dq, dks, dvs, dss, _ = _attn_bwd_tile(q_ref[qs, hs], ks, vs, None, st_ref[qs, h:h + 1], st_ref[qs, NA_HEADS + h:NA_HEADS + h + 1],
                                                  o_ref[qs, hs].astype(F32), do_ref[qs, hs].astype(F32))
            dq_ref[qs, hs] = dq.astype(dq_ref.dtype)
            return dks, dvs, dss

        @pl.when(s < nlat)
        def _():
            for rr in range(NA_RPS):
                r0, off = _na_rows(s * NA_RPS + rr, GR)
                rows = pl.ds(pl.multiple_of(r0 * GRID_W, GRID_W), W)
                qs = slice(rr * GRID_W, (rr + 1) * GRID_W)
                for h in range(NA_HEADS):
                    hs = slice(h * HD, (h + 1) * HD)
                    dks, dvs, dss = tile(qs, h, [(k_ref[rows, hs], b_ref[h, off]), (k_ref[L:T, hs], None)], [v_ref[rows, hs], v_ref[L:T, hs]])
                    dk_ref[rows, hs] += dks[0]
                    dv_ref[rows, hs] += dvs[0]
                    dk_ref[L:T, hs] += dks[1]
                    dv_ref[L:T, hs] += dvs[1]
                    db_ref[h, off] += dss[0]

        @pl.when(s >= nlat)
        def _():
            for h in range(NA_HEADS):
                hs = slice(h * HD, (h + 1) * HD)
                dks, dvs, _ = tile(slice(None), h, [(k_ref[L:T, hs], None)], [v_ref[L:T, hs]])
                dk_ref[L:T, hs] += dks[0]
                dv_ref[L:T, hs] += dvs[0]

    one = pl.Buffered(1)
    full = lambda shape: pl.BlockSpec(shape, lambda r: (0,) * len(shape), pipeline_mode=one)
    qspec = pl.BlockSpec((QB, 256), lambda r: (r, 0))
    return _pc(body, name=name, grid=(T // QB,),
               in_specs=[pl.BlockSpec((QB, 256), lambda r: (r, C_QB // 256)), full((T, 256)), full((T, 256)),
                         full((NA_HEADS, NA_KH, GRID_W, W)), pl.BlockSpec((QB, 256), lambda r: (r, 1)), qspec, pl.BlockSpec((QB, 8), lambda r: (r, 0))],
               out_specs=[qspec, full((T, 256)), full((T, 256)), full((NA_HEADS, NA_KH, GRID_W, W))],
               out_shape=[_sds((T, 256), BF16), _sds((T, 256), F32), _sds((T, 256), F32), _sds((NA_HEADS, NA_KH, GRID_W, W), F32)],
               compiler_params=_cp(("arbitrary",), 48 << 20))(P, kb, vb, bias, do_src, o, stats)


def na_rpb_grad(dbias, tag):
    _, e1, e2 = _na_selectors()
    x = dbias.reshape(NA_HEADS, NA_KH, GRID_W, NA_KH, GRID_W).transpose(0, 1, 3, 2, 4).reshape(NA_HEADS * NA_KH * NA_KH, GRID_W * GRID_W)
    r1 = matmul(x, e1, "nn", F32, f"na_rpb_sel1_{tag}", hi=True, tk=1024)
    r2 = matmul(e2, r1, "nn", F32, f"na_rpb_sel2_{tag}", hi=True)
    return r2.reshape(NA_HEADS, 16, 128)[:, :2 * NA_KH - 1, :2 * NA_KW - 1]


_HALO = 8


def _halo_specs(T, col0):
    nh = TR // _HALO
    cur = pl.BlockSpec((TR, 256), lambda i, j: (i, col0 + j))
    prv = pl.BlockSpec((_HALO, 256), lambda i, j: (jnp.maximum(i * nh - 1, 0), col0 + j))
    nxt = pl.BlockSpec((_HALO, 256), lambda i, j: (jnp.minimum((i + 1) * nh, T // _HALO - 1), col0 + j))
    return prv, cur, nxt


def _fill_ext(ext, prv, cur, nxt, i, nL, nT):
    has_prev = jnp.where((i != 0) & (i != nL), 1.0, 0.0)
    has_next = jnp.where((i != nL - 1) & (i != nT - 1), 1.0, 0.0)
    ext[0:_HALO, :] = prv[...].astype(F32) * has_prev
    ext[_HALO:_HALO + TR, :] = cur[...].astype(F32)
    ext[_HALO + TR:, :] = nxt[...].astype(F32) * has_next


def conv_silu_fwd(P, w8, b, nL, name):
    T = P.shape[0]
    nT = T // TR

    def body(prv, cur, nxt, w_ref, b_ref, pre_ref, act_ref, ext):
        i = pl.program_id(0)
        _fill_ext(ext, prv, cur, nxt, i, nL, nT)
        y = jnp.broadcast_to(b_ref[...], (TR, 256))
        for k in range(S_CONV):
            y = y + w_ref[k:k + 1, :] * ext[pl.ds(_HALO - S_CONV // 2 + k, TR), :]
        pre_ref[...] = y
        act_ref[...] = _silu(y)

    prv, cur, nxt = _halo_specs(T, C_XBC // 256)
    out = pl.BlockSpec((TR, 256), lambda i, j: (i, j))
    return _pc(body, name=name, grid=(nT, 4),
               in_specs=[prv, cur, nxt, pl.BlockSpec((8, 256), lambda i, j: (0, j)), pl.BlockSpec((1, 256), lambda i, j: (0, j))],
               out_specs=[out, out], out_shape=[_sds((T, 1024), F32), _sds((T, 1024), F32)],
               scratch_shapes=[pltpu.VMEM((TR + 2 * _HALO, 256), F32)],
               compiler_params=_cp(("parallel", "parallel"), 16 << 20))(P, P, P, w8, b)


def dsilu(pre, dxs_list, db_list, dc_list, name):
    T = pre.shape[0]
    n1, n2, n3 = len(dxs_list), len(db_list), len(dc_list)

    def body(*refs):
        pre_ref = refs[0]
        ins = refs[1:1 + n1 + n2 + n3]
        out = refs[-1]

        def part(rs, lo, hi):
            g = rs[0][...].astype(F32)
            for r in rs[1:]:
                g = g + r[...].astype(F32)
            x = pre_ref[:, lo:hi]
            sg = 1.0 / (1.0 + jnp.exp(-x))
            sl = x * sg
            out[:, lo:hi] = g * (sg + sl * (1.0 - sg))

        part(ins[:n1], 0, 512)
        part(ins[n1:n1 + n2], 512, 768)
        part(ins[n1 + n2:], 768, 1024)

    spec = lambda w: pl.BlockSpec((TR, w), lambda i: (i, 0))
    return _pc(body, name=name, grid=(T // TR,),
               in_specs=[spec(1024)] + [spec(512)] * n1 + [spec(256)] * (n2 + n3),
               out_specs=spec(1024), out_shape=_sds((T, 1024), F32),
               compiler_params=_cp(("parallel",), 32 << 20))(pre, *dxs_list, *db_list, *dc_list)


def conv_bwd(dpre, P, w8, nL, name):
    T = P.shape[0]
    nT = T // TR

    def body(dp, dc, dn, xp, xc, xn, w_ref, dx_ref, dw_ref, db_ref, extd, extx):
        i = pl.program_id(1)
        _fill_ext(extd, dp, dc, dn, i, nL, nT)
        _fill_ext(extx, xp, xc, xn, i, nL, nT)

        @pl.when(i == 0)
        def _():
            dw_ref[...] = jnp.zeros_like(dw_ref)
            db_ref[...] = jnp.zeros_like(db_ref)

        d = dc[...]
        dx = jnp.zeros((TR, 256), F32)
        for k in range(S_CONV):
            dx = dx + w_ref[k:k + 1, :] * extd[pl.ds(_HALO + S_CONV // 2 - k, TR), :]
            dw_ref[k:k + 1, :] += jnp.sum(d * extx[pl.ds(_HALO - S_CONV // 2 + k, TR), :], axis=0, keepdims=True)
        dx_ref[...] = dx.astype(dx_ref.dtype)
        db_ref[0:1, :] += jnp.sum(d, axis=0, keepdims=True)

    def swap(spec):
        f = spec.index_map
        return pl.BlockSpec(spec.block_shape, lambda j, i: f(i, j))

    dprv, dcur, dnxt = [swap(s) for s in _halo_specs(T, 0)]
    xprv, xcur, xnxt = [swap(s) for s in _halo_specs(T, C_XBC // 256)]
    acc = pl.BlockSpec((8, 256), lambda j, i: (0, j))
    return _pc(body, name=name, grid=(4, nT),
               in_specs=[dprv, dcur, dnxt, xprv, xcur, xnxt, acc],
               out_specs=[pl.BlockSpec((TR, 256), lambda j, i: (i, j)), acc, acc],
               out_shape=[_sds((T, 1024), BF16), _sds((8, 1024), F32), _sds((8, 1024), F32)],
               scratch_shapes=[pltpu.VMEM((TR + 2 * _HALO, 256), F32), pltpu.VMEM((TR + 2 * _HALO, 256), F32)],
               compiler_params=_cp(("parallel", "arbitrary"), 16 << 20))(dpre, dpre, dpre, P, P, P, w8)


def _onehot_row(h, n):
    return (lax.broadcasted_iota(jnp.int32, (1, n), 1) == h).astype(F32)


def _onehot_col(h, n):
    return (lax.broadcasted_iota(jnp.int32, (n, 1), 0) == h).astype(F32)


def _ssd_chunk(xs, dtr, dtb, alog, bm, cm, hin, reverse):
    Qn = S_Q
    ii = lax.broadcasted_iota(jnp.int32, (Qn, Qn), 0)
    jj = lax.broadcasted_iota(jnp.int32, (Qn, Qn), 1)
    keep = (ii <= jj) if reverse else (ii >= jj)
    tri = keep.astype(F32)
    triT = ((jj <= ii) if reverse else (jj >= ii)).astype(F32)
    eye = (ii == jj).astype(F32)
    dt = _softplus(dtr + dtb)
    a = dt * (-jnp.exp(alog))
    cs = hdot(tri, a)
    csT = hdot(a, triT, "tn")
    dtT = hdot(dt, eye, "tn")
    last = _onehot_row(0 if reverse else Qn - 1, Qn)
    ys, houts = [], []
    for g in range(S_GROUPS):
        G = bdot(cm[g], bm[g], "nt")
        for r in range(S_HEADS // S_GROUPS):
            h = g * (S_HEADS // S_GROUPS) + r
            eh_r, eh_c = _onehot_row(h, S_HEADS), _onehot_col(h, S_HEADS)
            cs_c = jnp.sum(cs * eh_r, axis=1, keepdims=True)
            dt_c = jnp.sum(dt * eh_r, axis=1, keepdims=True)
            cs_r = jnp.sum(csT * eh_c, axis=0, keepdims=True)
            dt_r = jnp.sum(dtT * eh_c, axis=0, keepdims=True)
            tot = jnp.sum(cs_r * last, axis=1, keepdims=True)
            decay = jnp.exp(jnp.where(keep, cs_c - cs_r, NEG))
            w = G * decay * dt_r
            y = bdot(w, xs[h], "nn") + bdot(cm[g], hin[h], "nt") * jnp.exp(cs_c)
            xsc = xs[h] * (jnp.exp(tot - cs_c) * dt_c)
            hout = hin[h] * jnp.exp(tot) + bdot(xsc, bm[g], "tn")
            ys.append(y)
            houts.append(hout)
    return ys, houts


def _ssd_orders(L, Lc):
    nl, ncx = L // S_Q, Lc // S_Q
    fwd = lambda s: jnp.where(s < ncx, nl + s, s - ncx)
    bwd = lambda s: nl + ncx - 1 - s
    return nl + ncx, fwd, bwd


def _ssd_in_specs(fo, bo, step):
    def at(order, w, col):
        return pl.BlockSpec((S_Q, w), lambda u: (order(step(u)), col))
    specs = []
    for order in (fo, bo):
        specs += [at(order, 512, 0), at(order, 256, 2), at(order, 256, 3), at(order, 128, C_DT // 128)]
    return specs


def ssd_fwd(act, P, dtb, alog, L, Lc, name):
    T = L + Lc
    ns, fo, bo = _ssd_orders(L, Lc)

    def body(xf, bf, cf, df, xb, bb, cb, db, dtb_ref, al_ref, yf, yb, hsf, hsb, Hf, Hb):
        s = pl.program_id(0)

        @pl.when(s == 0)
        def _():
            Hf[...] = jnp.zeros_like(Hf)
            Hb[...] = jnp.zeros_like(Hb)

        for d, (x_r, b_r, c_r, dt_r, y_r, hs_r, H) in enumerate(((xf, bf, cf, df, yf, hsf, Hf), (xb, bb, cb, db, yb, hsb, Hb))):
            hin = [H[h] for h in range(S_HEADS)]
            hs_r[0] = H[...]
            ys, houts = _ssd_chunk(
                [x_r[:, h * S_P:(h + 1) * S_P] for h in range(S_HEADS)], dt_r[:, d * 8:(d + 1) * 8],
                dtb_ref[d:d + 1, 0:8], al_ref[d:d + 1, 0:8],
                [b_r[:, g * S_N:(g + 1) * S_N] for g in range(S_GROUPS)], [c_r[:, g * S_N:(g + 1) * S_N] for g in range(S_GROUPS)],
                hin, reverse=(d == 1))
            for h in range(S_HEADS):
                y_r[:, h * S_P:(h + 1) * S_P] = ys[h]
                H[h] = houts[h]

    ident = lambda u: u
    small = pl.BlockSpec((8, 128), lambda u: (0, 0))
    hspec = pl.BlockSpec((1, S_HEADS, S_P, S_N), lambda u: (u, 0, 0, 0))
    return _pc(body, name=name, grid=(ns,),
               in_specs=_ssd_in_specs(fo, bo, ident) + [small, small],
               out_specs=[pl.BlockSpec((S_Q, 512), lambda u: (fo(u), 0)), pl.BlockSpec((S_Q, 512), lambda u: (bo(u), 0)), hspec, hspec],
               out_shape=[_sds((T, 512), F32), _sds((T, 512), F32), _sds((ns, S_HEADS, S_P, S_N), F32), _sds((ns, S_HEADS, S_P, S_N), F32)],
               scratch_shapes=[pltpu.VMEM((S_HEADS, S_P, S_N), F32), pltpu.VMEM((S_HEADS, S_P, S_N), F32)],
               compiler_params=_cp(("arbitrary",), 32 << 20))(act, act, act, P, act, act, act, P, dtb, alog)


def ssd_bwd(act, P, dtb, alog, hsf, hsb, dy, L, Lc, name):
    T = L + Lc
    ns, fo, bo = _ssd_orders(L, Lc)
    step = lambda u: ns - 1 - u

    def body(xf, bf, cf, df, xb, bb, cb, db, dtb_ref, al_ref, hsf_r, hsb_r, dyf, dyb,
             dxf, dbf, dcf, ddf, dxb, dbb, dcb, ddb, ddtb, dal, dHf, dHb):
        u = pl.program_id(0)

        @pl.when(u == 0)
        def _():
            dHf[...] = jnp.zeros_like(dHf)
            dHb[...] = jnp.zeros_like(dHb)
            ddtb[...] = jnp.zeros_like(ddtb)
            dal[...] = jnp.zeros_like(dal)

        dirs = ((xf, bf, cf, df, hsf_r, dyf, dxf, dbf, dcf, ddf, dHf), (xb, bb, cb, db, hsb_r, dyb, dxb, dbb, dcb, ddb, dHb))
        for d, (x_r, b_r, c_r, dt_r, hs_r, dy_r, dx_o, db_o, dc_o, dd_o, dH) in enumerate(dirs):
            f = functools.partial(_ssd_chunk, reverse=(d == 1))
            _, vjp = jax.vjp(
                f, [x_r[:, h * S_P:(h + 1) * S_P] for h in range(S_HEADS)], dt_r[:, d * 8:(d + 1) * 8],
                dtb_ref[d:d + 1, 0:8], al_ref[d:d + 1, 0:8],
                [b_r[:, g * S_N:(g + 1) * S_N] for g in range(S_GROUPS)], [c_r[:, g * S_N:(g + 1) * S_N] for g in range(S_GROUPS)],
                [hs_r[0, h] for h in range(S_HEADS)])
            gx, gdt, gdtb, gal, gb, gc, gh = vjp(([dy_r[:, h * S_P:(h + 1) * S_P] for h in range(S_HEADS)],
                                                  [dH[h] for h in range(S_HEADS)]))
            for h in range(S_HEADS):
                dx_o[:, h * S_P:(h + 1) * S_P] = gx[h]
                dH[h] = gh[h]
            for g in range(S_GROUPS):
                db_o[:, g * S_N:(g + 1) * S_N] = gb[g]
                dc_o[:, g * S_N:(g + 1) * S_N] = gc[g]
            dd_o[...] = gdt
            ddtb[d:d + 1, 0:8] += gdtb
            dal[d:d + 1, 0:8] += gal

    small = pl.BlockSpec((8, 128), lambda u: (0, 0))
    hspec = pl.BlockSpec((1, S_HEADS, S_P, S_N), lambda u: (step(u), 0, 0, 0))
    at = lambda order, w: pl.BlockSpec((S_Q, w), lambda u: (order(step(u)), 0))
    outs = []
    for order in (fo, bo):
        outs += [at(order, 512), at(order, 256), at(order, 256), at(order, 8)]
    oshape = [_sds((T, 512), F32), _sds((T, 256), F32), _sds((T, 256), F32), _sds((T, 8), F32)]
    return _pc(body, name=name, grid=(ns,),
               in_specs=_ssd_in_specs(fo, bo, step) + [small, small, hspec, hspec, at(fo, 512), at(bo, 512)],
               out_specs=outs + [small, small], out_shape=oshape + oshape + [_sds((8, 128), F32), _sds((8, 128), F32)],
               scratch_shapes=[pltpu.VMEM((S_HEADS, S_P, S_N), F32), pltpu.VMEM((S_HEADS, S_P, S_N), F32)],
               compiler_params=_cp(("arbitrary",), 40 << 20))(act, act, act, P, act, act, act, P, dtb, alog, hsf, hsb, dy, dy)


def _ssm_out(yf, yb, xs, z, dskip, g):
    y = (yf + yb + dskip * xs) * _silu(z)
    return (y * lax.rsqrt(jnp.mean(y * y, axis=-1, keepdims=True) + EPS)) * g


def ssm_out_fwd(yf, yb, act, P, dskip, g, name):
    T = yf.shape[0]

    def body(yf_r, yb_r, xs_r, z_r, d_r, g_r, o_r):
        o_r[...] = _ssm_out(yf_r[...], yb_r[...], xs_r[...], z_r[...], d_r[...], g_r[...]).astype(o_r.dtype)

    row = pl.BlockSpec((TR, 512), lambda i: (i, 0))
    vec = pl.BlockSpec((1, 512), lambda i: (0, 0))
    return _pc(body, name=name, grid=(T // TR,),
               in_specs=[row, row, row, pl.BlockSpec((TR, 512), lambda i: (i, C_Z // 512)), vec, vec],
               out_specs=row, out_shape=_sds((T, 512), BF16),
               compiler_params=_cp(("parallel",), 16 << 20))(yf, yb, act, P, dskip, g)


def ssm_out_bwd(yf, yb, act, P, dskip, g, do_src, name):
    T = yf.shape[0]

    def body(yf_r, yb_r, xs_r, z_r, d_r, g_r, do_r, dy_r, dxs_r, dz_r, dv_r):
        @pl.when(pl.program_id(0) == 0)
        def _():
            dv_r[...] = jnp.zeros_like(dv_r)

        _, vjp = jax.vjp(_ssm_out, yf_r[...], yb_r[...], xs_r[...], z_r[...], d_r[...], g_r[...])
        dyf, _, dxs, dz, dd, dg = vjp(do_r[...].astype(F32))
        dy_r[...] = dyf
        dxs_r[...] = dxs
        dz_r[...] = dz.astype(dz_r.dtype)
        dv_r[0:1, :] += dd
        dv_r[1:2, :] += dg

    row = pl.BlockSpec((TR, 512), lambda i: (i, 0))
    vec = pl.BlockSpec((1, 512), lambda i: (0, 0))
    return _pc(body, name=name, grid=(T // TR,),
               in_specs=[row, row, row, pl.BlockSpec((TR, 512), lambda i: (i, C_Z // 512)), vec, vec,
                         pl.BlockSpec((TR, 512), lambda i: (i, 1))],
               out_specs=[row, row, row, pl.BlockSpec((8, 512), lambda i: (0, 0))],
               out_shape=[_sds((T, 512), F32), _sds((T, 512), F32), _sds((T, 512), BF16), _sds((8, 512), F32)],
               compiler_params=_cp(("arbitrary",), 24 << 20))(yf, yb, act, P, dskip, g, do_src)


def add_halves(xv, got, cvec, name):
    n, r, cdim = xv.shape
    h = r // 2

    def body(c_ref, x_ref, g_ref, o_ref):
        o_ref[...] = (x_ref[...].astype(F32) + g_ref[...].astype(F32)).astype(o_ref.dtype)

    gs = pltpu.PrefetchScalarGridSpec(
        num_scalar_prefetch=1, grid=(n,),
        in_specs=[pl.BlockSpec((1, h, cdim), lambda k, c_ref: (k, c_ref[0], 0)), pl.BlockSpec((1, h, cdim), lambda k, c_ref: (k, 0, 0))],
        out_specs=pl.BlockSpec((1, h, cdim), lambda k, c_ref: (k, 0, 0)))
    return _pc(body, name=name, grid_spec=gs, out_shape=_sds((n, h, cdim), BF16),
               compiler_params=_cp(("arbitrary",), 24 << 20))(cvec, xv, got)


def sum_slots(a, name):
    n, r, cdim = a.shape
    tr = _div_tile(r, 512, 16)

    def body(a_ref, o_ref):
        acc = a_ref[0].astype(F32)
        for k in range(1, n):
            acc = acc + a_ref[k].astype(F32)
        o_ref[...] = acc

    return _pc(body, name=name, grid=(r // tr,), in_specs=[pl.BlockSpec((n, tr, cdim), lambda i: (0, i, 0))],
               out_specs=pl.BlockSpec((tr, cdim), lambda i: (i, 0)), out_shape=_sds((r, cdim), F32),
               compiler_params=_cp(("parallel",), 32 << 20))(a)


def adamw(w, g, m, v, name):
    B, R, C = w.shape
    tr = _div_tile(R, max(8, (1 << 19) // max(C, 1) // 8 * 8), 8) if R % 8 == 0 else R
    c1 = 1.0 / (1.0 - ADAM_B1 ** ADAM_STEP)
    c2 = 1.0 / (1.0 - ADAM_B2 ** ADAM_STEP)

    def body(w_ref, g_ref, m_ref, v_ref, d_ref, mo_ref, vo_ref):
        gg = g_ref[...]
        mn = ADAM_B1 * m_ref[...] + (1.0 - ADAM_B1) * gg
        vn = ADAM_B2 * v_ref[...] + (1.0 - ADAM_B2) * (gg * gg)
        d_ref[...] = -ADAM_LR * ((mn * c1) / (jnp.sqrt(vn * c2) + ADAM_EPS) + ADAM_WD * w_ref[...])
        mo_ref[...] = mn
        vo_ref[...] = vn

    spec = pl.BlockSpec((1, tr, C), lambda b, i: (b, i, 0))
    return _pc(body, name=name, grid=(B, R // tr), in_specs=[spec] * 4, out_specs=[spec] * 3,
               out_shape=[_sds((B, R, C), F32)] * 3, compiler_params=_cp(("parallel", "parallel"), 32 << 20))(w, g, m, v)


def _me():
    return lax.axis_index("x"), lax.axis_index("y"), lax.axis_index("c")


def _flip(v, bit):
    return 1 - v if bit else v


def allgather8(xv, name):
    R = xv.shape[0]

    def body(x_ref, out_ref, sum_ref, send_sems, recv_sems):
        mx, my, mc = _me()
        me = 4 * mx + 2 * my + mc
        out_ref[me] = x_ref[...]
        sends, recvs = [], []
        for k in range(1, 8):
            px, py, pc = _flip(mx, k & 4), _flip(my, k & 2), _flip(mc, k & 1)
            peer = 4 * px + 2 * py + pc
            sends.append(pltpu.make_async_remote_copy(src_ref=x_ref, dst_ref=out_ref.at[me], send_sem=send_sems.at[k - 1],
                                                      recv_sem=recv_sems.at[k - 1], device_id=(px, py, pc), device_id_type=MESH))
            recvs.append(pltpu.make_async_remote_copy(src_ref=x_ref, dst_ref=out_ref.at[peer], send_sem=send_sems.at[k - 1],
                                                      recv_sem=recv_sems.at[k - 1], device_id=(px, py, pc), device_id_type=MESH))
        for cp in sends:
            cp.start()
        for cp in recvs:
            cp.wait_recv()
        for cp in sends:
            cp.wait_send()
        acc = out_ref[0]
        for d in range(1, 8):
            acc = acc + out_ref[d]
        sum_ref[...] = acc

    vm = pl.BlockSpec(memory_space=pltpu.VMEM)
    return _pc(body, name=name, pin=False, in_specs=[vm], out_specs=[vm, vm], out_shape=[_sds((8, R, 128), F32), _sds((R, 128), F32)],
               scratch_shapes=[pltpu.SemaphoreType.DMA((7,)), pltpu.SemaphoreType.DMA((7,))],
               compiler_params=_cp(None, 32 << 20))(xv)


def _other_chips(mx, my):
    return [(1 - mx, my), (mx, 1 - my), (1 - mx, 1 - my)]


def _halves(r, mc, mult):
    h = r // 2
    return pl.ds(pl.multiple_of(mc * h, mult), h), pl.ds(pl.multiple_of((1 - mc) * h, mult), h)


def _rcopy(src, dst, send_sems, recv_sems, k, to):
    return pltpu.make_async_remote_copy(src_ref=src, dst_ref=dst, send_sem=send_sems.at[k], recv_sem=recv_sems.at[k],
                                        device_id=to, device_id_type=MESH)


def _gather_body(xs, outs, send_sems, recv_sems):
    n = len(xs)
    mx, my, mc = _me()
    chip = 2 * mx + my
    sib = (mx, my, 1 - mc)
    chips = _other_chips(mx, my)
    idx = [2 * cx + cy for cx, cy in chips]
    cp = functools.partial(_rcopy, send_sems=send_sems, recv_sems=recv_sems)
    hv = [_halves(x.shape[0], mc, 16) for x in xs]
    first, passed = [], []
    for a in range(n):
        for j, (cx, cy) in enumerate(chips):
            first.append(cp(xs[a].at[hv[a][0]], outs[a].at[chip, hv[a][0]], k=6 * a + j, to=(cx, cy, mc)))
            first[-1].start()
    for a in range(n):
        for j in range(3):
            cp(xs[a].at[hv[a][0]], outs[a].at[idx[j], hv[a][0]], k=6 * a + j, to=sib).wait_recv()
            passed.append(cp(outs[a].at[idx[j], hv[a][0]], outs[a].at[idx[j], hv[a][0]], k=6 * a + 3 + j, to=sib))
            passed[-1].start()
    for a in range(n):
        for j in range(3):
            cp(xs[a].at[hv[a][1]], outs[a].at[idx[j], hv[a][1]], k=6 * a + 3 + j, to=sib).wait_recv()
    for c_ in first + passed:
        c_.wait_send()


def _my_chip():
    return 2 * lax.axis_index("x") + lax.axis_index("y")


def _own_slots(outs, shards):
    return [lax.dynamic_update_index_in_dim(o, x, _my_chip(), 0) for o, x in zip(outs, shards)]


def gather_weights(shards, name):
    n = len(shards)

    def body(*refs):
        _gather_body(refs[:n], refs[n:2 * n], *refs[2 * n:])

    hbm = pl.BlockSpec(memory_space=pl.ANY)
    outs = _pc(body, name=name, in_specs=[hbm] * n, out_specs=[hbm] * n, out_shape=[_sds((4,) + x.shape, x.dtype) for x in shards],
               scratch_shapes=[pltpu.SemaphoreType.DMA((6 * n,)), pltpu.SemaphoreType.DMA((6 * n,))])(*shards)
    return _own_slots(outs, shards)


GATHER_REST_ID = 3


def gather_weights_sc(shards, name):
    n = len(shards)
    x_refs = [jax.new_ref(x, memory_space=pltpu.MemorySpace.HBM) for x in shards]
    out_refs = [jax.empty_ref(_sds((4,) + x.shape, x.dtype), memory_space=pltpu.MemorySpace.HBM) for x in shards]

    @pl.kernel(mesh=plsc.ScalarSubcoreMesh(axis_name="sc", num_cores=1), name=name,
               scratch_types=(pltpu.SemaphoreType.DMA((6 * n,)), pltpu.SemaphoreType.DMA((6 * n,))),
               compiler_params=pltpu.CompilerParams(collective_id=GATHER_REST_ID))
    def launch(send_sems, recv_sems):
        mx, my, mc = _me()
        barrier = pltpu.get_barrier_semaphore()
        for peer in [(mx, my, 1 - mc)] + [(cx, cy, mc) for cx, cy in _other_chips(mx, my)]:
            pl.semaphore_signal(barrier, inc=1, device_id=peer, device_id_type=MESH)
        pl.semaphore_wait(barrier, 4)
        _gather_body(x_refs, out_refs, send_sems, recv_sems)

    launch()
    return _own_slots([o[...] for o in out_refs], shards)


def swap_halves(arrs, name):
    n = len(arrs)

    def body(*refs):
        xs, outs = refs[:n], refs[n:2 * n]
        send_sems, recv_sems = refs[2 * n:]
        mx, my, mc = _me()
        cps = []
        for a in range(n):
            theirs = _halves(xs[a].shape[1], mc, 16)[1]
            cps.append(_rcopy(xs[a].at[pl.ds(0, 4), theirs], outs[a], send_sems, recv_sems, a, (mx, my, 1 - mc)))
            cps[-1].start()
        for c_ in cps:
            c_.wait()

    hbm = pl.BlockSpec(memory_space=pl.ANY)
    return _pc(body, name=name, in_specs=[hbm] * n, out_specs=[hbm] * n,
               out_shape=[_sds((4, x.shape[1] // 2, x.shape[2]), x.dtype) for x in arrs],
               scratch_shapes=[pltpu.SemaphoreType.DMA((n,)), pltpu.SemaphoreType.DMA((n,))])(*arrs)


SCATTER_ID = 4


def scatter_chips_sc(arrs, name):
    n = len(arrs)
    x_refs = [jax.new_ref(x, memory_space=pltpu.MemorySpace.HBM) for x in arrs]
    out_refs = [jax.empty_ref(_sds(x.shape, x.dtype), memory_space=pltpu.MemorySpace.HBM) for x in arrs]

    @pl.kernel(mesh=plsc.ScalarSubcoreMesh(axis_name="sc", num_cores=1), name=name,
               scratch_types=(pltpu.SemaphoreType.DMA((3 * n,)), pltpu.SemaphoreType.DMA((3 * n,))),
               compiler_params=pltpu.CompilerParams(collective_id=SCATTER_ID))
    def launch(send_sems, recv_sems):
        mx, my, mc = _me()
        chip = 2 * mx + my
        chips = _other_chips(mx, my)
        idx = [2 * cx + cy for cx, cy in chips]
        barrier = pltpu.get_barrier_semaphore()
        for cx, cy in chips:
            pl.semaphore_signal(barrier, inc=1, device_id=(cx, cy, mc), device_id_type=MESH)
        pl.semaphore_wait(barrier, 3)
        cp = functools.partial(_rcopy, send_sems=send_sems, recv_sems=recv_sems)
        sends = []
        for a in range(n):
            for j, (cx, cy) in enumerate(chips):
                sends.append(cp(x_refs[a].at[idx[j]], out_refs[a].at[chip], k=3 * a + j, to=(cx, cy, mc)))
                sends[-1].start()
        for a in range(n):
            for j, (cx, cy) in enumerate(chips):
                cp(x_refs[a].at[idx[j]], out_refs[a].at[idx[j]], k=3 * a + j, to=(cx, cy, mc)).wait_recv()
        for c_ in sends:
            c_.wait_send()

    launch()
    return _own_slots([o[...] for o in out_refs], [lax.dynamic_index_in_dim(x, _my_chip(), 0, keepdims=False) for x in arrs])


def share_halves(parts, name):
    flat = [p for w in parts for p in w]
    nw, n = len(parts), len(flat)
    depth = n // nw

    def body(*refs):
        xs, outs = refs[:n], refs[n:n + nw]
        send_sems, recv_sems = refs[n + nw:]
        mx, my, mc = _me()
        sib = (mx, my, 1 - mc)
        sends, recvs = [], []
        for a in range(n):
            w, l = a // depth, a % depth
            mine, theirs = _halves(outs[w].shape[1], mc, 8)
            sends.append(_rcopy(xs[a], outs[w].at[l, mine], send_sems, recv_sems, a, sib))
            recvs.append(_rcopy(xs[a], outs[w].at[l, theirs], send_sems, recv_sems, a, sib))
            sends[-1].start()
        for c_ in recvs:
            c_.wait_recv()
        for c_ in sends:
            c_.wait_send()

    hbm = pl.BlockSpec(memory_space=pl.ANY)
    outs = _pc(body, name=name, in_specs=[hbm] * n, out_specs=[hbm] * nw,
               out_shape=[_sds((depth, 2 * w[0].shape[0], w[0].shape[1]), F32) for w in parts],
               scratch_shapes=[pltpu.SemaphoreType.DMA((n,)), pltpu.SemaphoreType.DMA((n,))])(*flat)
    outs = list(outs)
    mc = lax.axis_index("c")
    for w in range(nw):
        for l in range(depth):
            h = parts[w][l].shape[0]
            outs[w] = lax.dynamic_update_slice(outs[w], parts[w][l][None], (l, mc * h, 0))
    return outs


_BIG = ("w_in", "w_out", "w_ffn_in", "w_ffn_out")
N_CHIPS = 4
DEPTH = 2


def _pad_rows(v, mult=8):
    n = v.shape[0]
    rows = -(-n // 128)
    rows = -(-rows // mult) * mult
    return jnp.pad(v, (0, rows * 128 - n)).reshape(rows, 128)


class _Flat:
    def __init__(self):
        self.items = []

    def add(self, name, a):
        self.items.append((name, a.shape, a.reshape(-1).astype(F32)))

    def rows(self):
        return _pad_rows(jnp.concatenate([a for _, _, a in self.items]))

    def split(self, rows):
        flat = rows.reshape(-1)
        out, o = {}, 0
        for name, shape, a in self.items:
            out[name] = flat[o:o + a.shape[0]].reshape(shape)
            o += a.shape[0]
        return out

    def split_lead(self, rows3):
        n = rows3.shape[0]
        flat = rows3.reshape(n, -1)
        out, o = {}, 0
        for name, shape, a in self.items:
            out[name] = flat[:, o:o + a.shape[0]].reshape((n,) + tuple(shape))
            o += a.shape[0]
        return out


def _gsv(rows):
    z = jnp.zeros((2, D), F32)
    r = [z if a is None else a for a in rows] + [z] * 5
    return jnp.stack(r, axis=1)


def _pad8(a, rows=8, cols=128):
    return jnp.zeros((rows, cols), F32).at[:a.shape[0], :a.shape[1]].set(a.astype(F32))


def kernel(x, c, ctx, c_ctx, w_mod, b_mod, g_mix, w_in, wa_sink, na_rpb, ssm_conv_w, ssm_conv_b, ssm_dt_bias, ssm_a_log, ssm_d, ssm_norm_g, w_out, g_ffn, w_ffn_in, w_ffn_out, g_final, loss_target, m_c_ctx, m_w_mod, m_b_mod, m_g_mix, m_w_in, m_wa_sink, m_na_rpb, m_ssm_conv_w, m_ssm_conv_b, m_ssm_dt_bias, m_ssm_a_log, m_ssm_d, m_ssm_norm_g, m_w_out, m_g_ffn, m_w_ffn_in, m_w_ffn_out, m_g_final, v_c_ctx, v_w_mod, v_b_mod, v_g_mix, v_w_in, v_wa_sink, v_na_rpb, v_ssm_conv_w, v_ssm_conv_b, v_ssm_dt_bias, v_ssm_a_log, v_ssm_d, v_ssm_norm_g, v_w_out, v_g_ffn, v_w_ffn_in, v_w_ffn_out, v_g_final):
    L, Lc = x.shape[1], ctx.shape[1]
    T = L + Lc
    nL = L // TR
    mx, my, mc = lax.axis_index("x"), lax.axis_index("y"), lax.axis_index("c")
    dev = 4 * mx + 2 * my + mc
    chip = 2 * mx + my
    MODW = 6 * D // N_CHIPS
    CW = 1024 // N_CHIPS

    sc = _silu(c.astype(F32))
    scc = _silu(c_ctx.astype(F32))[None]
    f1 = _Flat()
    f1.add("sc", sc)
    f1.add("conv_w", ssm_conv_w)
    g1, _ = allgather8(f1.rows(), "gather_cond")
    g1 = f1.split_lead(g1)
    sc_all = g1["sc"][:, 0]
    conv_w = jnp.concatenate([g1["conv_w"][2 * k] for k in range(N_CHIPS)], axis=-1)
    A16 = jnp.concatenate([sc_all, scc, jnp.zeros((7, D), F32)], axis=0)

    mod_part = jnp.stack([matmul(A16, w_mod[l], "nn", F32, f"mod_fwd{l}") for l in range(DEPTH)])
    f2 = _Flat()
    f2.add("mod", mod_part)
    g2, _ = allgather8(f2.rows(), "gather_mod")
    g2 = f2.split_lead(g2)["mod"]
    mods = jnp.concatenate([g2[2 * k] for k in range(N_CHIPS)], axis=-1) + b_mod[:, None, :]
    mod_l = lax.dynamic_index_in_dim(mods, dev, axis=1, keepdims=False).reshape(DEPTH, 6, D)
    mod_c = mods[:, 8].reshape(DEPTH, 6, D)
    mod = jnp.stack([mod_l, mod_c], axis=1)
    mrow = lambda l, j: mod[l, :, j]

    own = {"w_in": w_in, "w_out": w_out, "w_ffn_in": w_ffn_in, "w_ffn_out": w_ffn_out}
    sh16 = [own[n][l].astype(BF16) for n in _BIG for l in range(DEPTH)]
    gath = list(gather_weights(sh16[:1], "gather_first"))
    after_first = (gath[0][0, 0, 0] * 0).astype(BF16)
    gath += list(gather_weights_sc([sh16[1] + after_first] + sh16[2:], "gather_rest"))
    gw = {n: [gath[DEPTH * i + l] for l in range(DEPTH)] for i, n in enumerate(_BIG)}
    W_in = [jnp.pad(jnp.concatenate([g[k] for k in range(N_CHIPS)], axis=1), ((0, 0), (0, IN_PAD - IN_COLS))) for g in gw["w_in"]]
    W_out = [g.reshape(D, D) for g in gw["w_out"]]
    W_fo = [g.reshape(D_FF, D) for g in gw["w_ffn_out"]]
    W_fi = gw["w_ffn_in"]

    cos, sin, rotm = rope_tables(L, Lc)
    x0 = jnp.concatenate([x[0], ctx[0]], axis=0).astype(F32)

    sv = []
    xin = x0
    gsv_first = _gsv([None, mrow(0, 0), mrow(0, 1)])
    _, h1 = res_norm_mod(x0, None, gsv_first, g_mix[0][None], nL, "norm_first")
    for l in range(DEPTH):
        s = {"xin": xin, "h1": h1}
        P = matmul(h1, W_in[l], "nn", F32, f"in_proj{l}", tn=IN_PAD)
        qr, kr, kb, vb = rope_apply(P, C_QA // 256, P, C_KA // 128, cos, sin, rotm, False, f"rope{l}", kv_src=P)
        sink8 = _pad8(jnp.broadcast_to(wa_sink[l][:, None], (WA_HEADS, 128)))
        oa, sta = win_attn_fwd(qr, kr, P, sink8, L, Lc, f"wa_fwd{l}")
        bias = na_bias_table(na_rpb[l], l)
        ob, stb = na_fwd(P, kb, vb, bias, L, Lc, f"na_fwd{l}")
        w8 = jnp.concatenate([conv_w[l], jnp.zeros((1, 1024), F32)], axis=0)
        pre, act = conv_silu_fwd(P, w8, ssm_conv_b[l][None], nL, f"conv_fwd{l}")
        dtb8, al8 = _pad8(ssm_dt_bias[l]), _pad8(ssm_a_log[l])
        yf, yb, hsf, hsb = ssd_fwd(act, P, dtb8, al8, L, Lc, f"ssd_fwd{l}")
        dskip = jnp.repeat(ssm_d[l], S_P)[None]
        oc = ssm_out_fwd(yf, yb, act, P, dskip, ssm_norm_g[l][None], f"ssm_out_fwd{l}")
        mixin = jnp.concatenate([oa, ob, oc], axis=1)
        mix = matmul(mixin, W_out[l], "nn", BF16, f"out_proj{l}")
        gsv_mid = _gsv([mrow(l, 2), mrow(l, 3), mrow(l, 4)])
        x1, h2 = res_norm_mod(xin, mix, gsv_mid, g_ffn[l][None], nL, f"norm_mid{l}")
        gu = matmul_fi(h2, W_fi[l], "nn", BF16, f"ffn_in{l}")
        af = swiglu_fwd(gu, f"swiglu_fwd{l}")
        fo = matmul(af, W_fo[l], "nn", BF16, f"ffn_out{l}", tk=D_FF)
        s.update(P=P, qr=qr, kr=kr, sink8=sink8, oa=oa, sta=sta, ob=ob, stb=stb, kb=kb, vb=vb, bias=bias, w8=w8, pre=pre, act=act, dtb8=dtb8, al8=al8, yf=yf,
                 yb=yb, hsf=hsf, hsb=hsb, dskip=dskip, mixin=mixin, mix=mix, gsv_mid=gsv_mid, x1=x1, h2=h2, gu=gu, af=af, fo=fo)
        if l + 1 < DEPTH:
            s["gsv_end"] = _gsv([mrow(l, 5), mrow(l + 1, 0), mrow(l + 1, 1)])
            xin, h1 = res_norm_mod(x1, fo, s["gsv_end"], g_mix[l + 1][None], nL, f"norm_end{l}")
        else:
            s["gsv_end"] = _gsv([mrow(l, 5), None, None])
        sv.append(s)

    last = sv[-1]
    loss8, dres, dfo, dgsv_end, dg_final = final_loss(last["x1"], last["fo"], last["gsv_end"], g_final[None], loss_target[0].astype(F32), nL, "final_loss")
    loss = lax.psum(loss8[0, 0], ("x", "y", "c"))

    dmod = [[None] * 6 for _ in range(DEPTH)]
    gW = {n: [None] * DEPTH for n in _BIG}
    small = [dict() for _ in range(DEPTH)]
    parts = [None] * DEPTH
    cvec = mc.astype(jnp.int32).reshape(1)
    grad_x = None
    for l in reversed(range(DEPTH)):
        s = sv[l]
        dmod[l][5] = dgsv_end[:, 0]
        if l + 1 < DEPTH:
            dmod[l + 1][0], dmod[l + 1][1] = dgsv_end[:, 1], dgsv_end[:, 2]
        daf = matmul(dfo, W_fo[l], "nt", BF16, f"ffn_out_dx{l}")
        gW["w_ffn_out"][l] = matmul(s["af"], dfo, "tn", BF16, f"ffn_out_dw{l}", tm=1408, tk=T).reshape(N_CHIPS, D_FF // N_CHIPS, D)
        dgu = swiglu_bwd(s["gu"], daf, f"swiglu_bwd{l}")
        dh2 = matmul_fi(dgu, W_fi[l], "nt", BF16, f"ffn_in_dx{l}")
        gW["w_ffn_in"][l] = matmul_fi(s["h2"], dgu, "tn", BF16, f"ffn_in_dw{l}")
        dres, dmix, dgsv_mid, dg_ffn = res_norm_mod_bwd(s["x1"], s["mix"], s["gsv_mid"], g_ffn[l][None], dh2, dres, nL, f"norm_mid_bwd{l}")
        dmod[l][2], dmod[l][3], dmod[l][4] = dgsv_mid[:, 0], dgsv_mid[:, 1], dgsv_mid[:, 2]
        dmixin = matmul(dmix, W_out[l], "nt", BF16, f"out_proj_dx{l}")
        gW["w_out"][l] = matmul(s["mixin"], dmix, "tn", BF16, f"out_proj_dw{l}", tm=1024, tk=T).reshape(N_CHIPS, D // N_CHIPS, D)
        P = s["P"]
        dqr, dkr, dva, dsink = win_attn_bwd(s["qr"], s["kr"], P, s["sink8"], dmixin, s["oa"], s["sta"], L, Lc, f"wa_bwd{l}")
        dqa, dka = rope_apply(dqr, 0, dkr[WA_BLK:WA_BLK + T], 0, cos, sin, rotm, True, f"rope_bwd{l}")
        dqb, dkb, dvb, dbias = na_bwd(P, s["kb"], s["vb"], s["bias"], dmixin, s["ob"], s["stb"], L, Lc, f"na_bwd{l}")
        dy, dxs1, dz, dvec = ssm_out_bwd(s["yf"], s["yb"], s["act"], P, s["dskip"], ssm_norm_g[l][None], dmixin, f"ssm_out_bwd{l}")
        dxf, dbf, dcf, ddf, dxb, dbb, dcb, ddb, ddtb, dal = ssd_bwd(s["act"], P, s["dtb8"], s["al8"], s["hsf"], s["hsb"], dy, L, Lc, f"ssd_bwd{l}")
        dpre = dsilu(s["pre"], [dxf, dxb, dxs1], [dbf, dbb], [dcf, dcb], f"dsilu{l}")
        dxbc, dw8, db8 = conv_bwd(dpre, P, s["w8"], nL, f"conv_bwd{l}")
        ddt = jnp.concatenate([ddf, ddb, jnp.zeros((T, IN_PAD - IN_COLS), F32)], axis=1)
        pieces = [(dqa, C_QA), (dqb, C_QB), (dz, C_Z), (dka, C_KA), (dva[WA_BLK:WA_BLK + T], C_VA), (dkb, C_KB), (dvb, C_VB),
                  (dxbc, C_XBC), (ddt, C_DT)]
        dh1, dwin = in_proj_bwd(pieces, s["h1"], W_in[l], f"in_proj_bwd{l}")
        cw = IN_COLS // N_CHIPS
        gW["w_in"][l] = jnp.stack([dwin[:, k * cw:(k + 1) * cw] for k in range(N_CHIPS)])
        garr = [gW[n][l] for n in _BIG]
        got = swap_halves(garr, f"reduce_d2d{l}")
        chip_sum = [add_halves(garr[a], got[a], cvec, f"reduce_add_pair{l}_{a}") for a in range(len(garr))]
        parts[l] = scatter_chips_sc(chip_sum, f"reduce_ici{l}")
        small[l] = dict(g_ffn=dg_ffn[0], wa_sink=dsink[:WA_HEADS, 0], na_rpb=na_rpb_grad(dbias, l), conv_w=dw8[:S_CONV], conv_b=db8[0],
                        dt_bias=ddtb[:2, :8], a_log=dal[:2, :8], ssm_d=dvec[0].reshape(S_HEADS, S_P).sum(axis=1), norm_g=dvec[1])
        if l > 0:
            p = sv[l - 1]
            dres, dfo, dgsv_end, dg_mix = res_norm_mod_bwd(s["xin"], p["fo"], p["gsv_end"], g_mix[l][None], dh1, dres, nL, f"norm_end_bwd{l - 1}")
        else:
            grad_x, _, dgsv_first, dg_mix = res_norm_mod_bwd(s["xin"], None, gsv_first, g_mix[0][None], dh1, dres, nL, "norm_first_bwd")
            dmod[0][0], dmod[0][1] = dgsv_first[:, 1], dgsv_first[:, 2]
        small[l]["g_mix"] = dg_mix[0]
    for l in range(DEPTH):
        for j in range(6):
            if dmod[l][j] is None:
                dmod[l][j] = jnp.zeros((2, D), F32)
    dmod = jnp.stack([jnp.stack(r, axis=1) for r in dmod])

    f3 = _Flat()
    f3.add("dmod_l", dmod[:, 0].reshape(DEPTH, 6 * D))
    f3.add("dmod_c", dmod[:, 1].reshape(DEPTH, 6 * D))
    f3.add("g_final", dg_final[0])
    for n in ("g_mix", "g_ffn", "wa_sink", "na_rpb", "conv_w", "conv_b", "dt_bias", "a_log", "ssm_d", "norm_g"):
        f3.add(n, jnp.stack([small[l][n] for l in range(DEPTH)]))
    g3, s3 = allgather8(f3.rows(), "reduce_small")
    dmod_all = f3.split_lead(g3)["dmod_l"]
    s3 = f3.split(s3)
    dmodc_tot = s3["dmod_c"]
    col0 = chip * MODW
    G16, G16c = [], []
    for l in range(DEPTH):
        rows = jnp.concatenate([dmod_all[:, l], dmodc_tot[l][None], jnp.zeros((7, 6 * D), F32)], axis=0)
        G16.append(lax.dynamic_slice_in_dim(rows, col0, MODW, axis=1))
        rc = jnp.concatenate([dmodc_tot[l][None], jnp.zeros((15, 6 * D), F32)], axis=0)
        G16c.append(lax.dynamic_slice_in_dim(rc, col0, MODW, axis=1))
    grad_w_mod = jnp.stack([matmul(A16, G16[l], "tn", F32, f"mod_dw{l}") for l in range(DEPTH)])
    dscc_part = sum(matmul(G16c[l], w_mod[l], "nt", F32, f"mod_dx{l}")[0] for l in range(DEPTH))
    _, s4 = allgather8(_pad_rows(dscc_part * (mc == 1).astype(F32)), "reduce_cctx")
    dscc = s4.reshape(-1)[:D]
    cc = c_ctx.astype(F32)
    sg = 1.0 / (1.0 + jnp.exp(-cc))
    grad_c_ctx = dscc * (sg * (1.0 + cc * (1.0 - sg)))

    halves = [[sum_slots(parts[l][i], f"reduce_add_chips{l}_{i}") for l in range(DEPTH)] for i in range(len(_BIG))]
    gsh = dict(zip(_BIG, share_halves(halves, "reduce_share")))

    grads = {"c_ctx": grad_c_ctx, "w_mod": grad_w_mod, "b_mod": s3["dmod_l"] + s3["dmod_c"], "g_mix": s3["g_mix"], "w_in": gsh["w_in"],
             "wa_sink": s3["wa_sink"], "na_rpb": s3["na_rpb"],
             "ssm_conv_w": lax.dynamic_slice_in_dim(s3["conv_w"], chip * CW, CW, axis=2), "ssm_conv_b": s3["conv_b"],
             "ssm_dt_bias": s3["dt_bias"], "ssm_a_log": s3["a_log"], "ssm_d": s3["ssm_d"], "ssm_norm_g": s3["norm_g"],
             "w_out": gsh["w_out"], "g_ffn": s3["g_ffn"], "w_ffn_in": gsh["w_ffn_in"], "w_ffn_out": gsh["w_ffn_out"], "g_final": s3["g_final"]}
    wts = {"c_ctx": c_ctx, "w_mod": w_mod, "b_mod": b_mod, "g_mix": g_mix, "w_in": w_in, "wa_sink": wa_sink, "na_rpb": na_rpb,
           "ssm_conv_w": ssm_conv_w, "ssm_conv_b": ssm_conv_b, "ssm_dt_bias": ssm_dt_bias, "ssm_a_log": ssm_a_log, "ssm_d": ssm_d,
           "ssm_norm_g": ssm_norm_g, "w_out": w_out, "g_ffn": g_ffn, "w_ffn_in": w_ffn_in, "w_ffn_out": w_ffn_out, "g_final": g_final}
    ms = {"c_ctx": m_c_ctx, "w_mod": m_w_mod, "b_mod": m_b_mod, "g_mix": m_g_mix, "w_in": m_w_in, "wa_sink": m_wa_sink, "na_rpb": m_na_rpb,
          "ssm_conv_w": m_ssm_conv_w, "ssm_conv_b": m_ssm_conv_b, "ssm_dt_bias": m_ssm_dt_bias, "ssm_a_log": m_ssm_a_log, "ssm_d": m_ssm_d,
          "ssm_norm_g": m_ssm_norm_g, "w_out": m_w_out, "g_ffn": m_g_ffn, "w_ffn_in": m_w_ffn_in, "w_ffn_out": m_w_ffn_out, "g_final": m_g_final}
    vs = {"c_ctx": v_c_ctx, "w_mod": v_w_mod, "b_mod": v_b_mod, "g_mix": v_g_mix, "w_in": v_w_in, "wa_sink": v_wa_sink, "na_rpb": v_na_rpb,
          "ssm_conv_w": v_ssm_conv_w, "ssm_conv_b": v_ssm_conv_b, "ssm_dt_bias": v_ssm_dt_bias, "ssm_a_log": v_ssm_a_log, "ssm_d": v_ssm_d,
          "ssm_norm_g": v_ssm_norm_g, "w_out": v_w_out, "g_ffn": v_g_ffn, "w_ffn_in": v_w_ffn_in, "w_ffn_out": v_w_ffn_out, "g_final": v_g_final}
    names = list(wts)
    grads = {n: grads[n].reshape(wts[n].shape).astype(F32) for n in names}
    big = ("w_mod", "w_in", "w_out", "w_ffn_in", "w_ffn_out")
    delta, new_m, new_v = {}, {}, {}
    for n in big:
        delta[n], new_m[n], new_v[n] = adamw(wts[n], grads[n], ms[n], vs[n], f"adamw_{n}")
    packs = []
    for src in (wts, grads, ms, vs):
        f = _Flat()
        for n in names:
            if n not in big:
                f.add(n, src[n])
        packs.append(f)
    d_, m_, v_ = adamw(*[f.rows()[None] for f in packs], "adamw_small")
    for dst, rows in ((delta, d_), (new_m, m_), (new_v, v_)):
        dst.update(packs[0].split(rows[0]))

    return (loss, grad_x[:L][None], *[grads[n] for n in names], *[delta[n] for n in names],
            *[new_m[n] for n in names], *[new_v[n] for n in names])
```

```python
import functools

import numpy as np
import jax
import jax.numpy as jnp
from jax import lax
from jax.experimental import pallas as pl
from jax.experimental.pallas import tpu as pltpu
from jax.experimental.pallas import tpu_sc as plsc

F32 = jnp.float32
BF16 = jnp.bfloat16
_MXU = jnp.bfloat16
_HI = lax.Precision.HIGHEST
MESH = pl.DeviceIdType.MESH

D = 1024
HD = 64
GRID_W = 64
EPS = 1e-6
ROPE_BASE = 10000.0
WA_HEADS, WA_KV = 4, 2
WA_BLK = 128
NA_HEADS, NA_KH, NA_KW = 4, 8, 16
S_HEADS, S_P, S_INNER, S_GROUPS, S_N, S_CONV, S_Q = 8, 64, 512, 2, 128, 7, 128
D_FF = 2816
IN_COLS = 2832
IN_PAD = 2944
C_QA, C_QB, C_Z, C_KA, C_VA, C_KB, C_VB, C_XBC, C_DT = 0, 256, 512, 1024, 1152, 1280, 1536, 1792, 2816
ADAM_LR, ADAM_B1, ADAM_B2, ADAM_EPS, ADAM_WD, ADAM_STEP = 0.001, 0.9, 0.999, 1e-08, 0.01, 10

TR = 256
NEG = -1e30
VMEM_CAP = 56 * 1024 * 1024


PIN_BYTES = 256 * 1024


def _is_big(a):
    return hasattr(a, "shape") and len(a.shape) >= 2 and int(np.prod(a.shape)) * jnp.dtype(a.dtype).itemsize >= PIN_BYTES


def _pc(body, *, out_shape, pin=True, **kw):
    if not pin:
        return pl.pallas_call(body, out_shape=out_shape, **kw)
    one = isinstance(out_shape, jax.ShapeDtypeStruct)
    outs = [pltpu.HBM(s.shape, s.dtype) if _is_big(s) else s for s in ([out_shape] if one else out_shape)]
    call = pl.pallas_call(body, out_shape=outs[0] if one else outs, **kw)
    return lambda *args: call(*[pltpu.with_memory_space_constraint(a, pltpu.HBM) if _is_big(a) else a for a in args])


def _cp(sem=None, vmem=None):
    kw = {}
    if sem is not None:
        kw["dimension_semantics"] = sem
    if vmem is not None:
        kw["vmem_limit_bytes"] = int(min(max(vmem, 16 * 1024 * 1024), VMEM_CAP))
    return pltpu.CompilerParams(**kw)


def _sds(shape, dtype):
    return jax.ShapeDtypeStruct(tuple(shape), dtype)


_DIMS = {"nn": ((1,), (0,)), "nt": ((1,), (1,)), "tn": ((0,), (0,))}


def _dg(a, b, dims):
    return lax.dot_general(a.astype(_MXU), b.astype(_MXU), (dims, ((), ())), preferred_element_type=F32)


@functools.partial(jax.custom_vjp, nondiff_argnums=(2,))
def bdot(a, b, mode):
    return _dg(a, b, _DIMS[mode])


def _bdot_fwd(a, b, mode):
    return bdot(a, b, mode), (a, b)


def _bdot_bwd(mode, res, g):
    a, b = res
    if mode == "nn":
        return bdot(g, b, "nt"), bdot(a, g, "tn")
    if mode == "nt":
        return bdot(g, b, "nn"), bdot(g, a, "tn")
    return bdot(b, g, "nt"), bdot(a, g, "nn")


bdot.defvjp(_bdot_fwd, _bdot_bwd)


def hdot(a, b, mode="nn"):
    return lax.dot_general(a, b, (_DIMS[mode], ((), ())), precision=_HI, preferred_element_type=F32)


def _silu(x):
    return x / (1.0 + jnp.exp(-x))


def _softplus(x):
    return jnp.maximum(x, 0.0) + jnp.log(1.0 + jnp.exp(-jnp.abs(x)))


def _div_tile(n, cap, mult):
    if n <= cap:
        return n
    best = None
    for t in range(mult, cap + 1, mult):
        if n % t == 0:
            best = t
    assert best is not None, (n, cap, mult)
    return best


def matmul(a, b, mode, out_dtype, name, tm=640, tn=1536, tk=1408, hi=False):
    if mode == "tn":
        K, M = a.shape
    else:
        M, K = a.shape
    N = b.shape[0] if mode == "nt" else b.shape[1]
    tm = _div_tile(M, tm, 128 if mode == "tn" else 16)
    tn = _div_tile(N, tn, 128)
    tk = _div_tile(K, tk, 128 if mode != "tn" else 16)
    nk = K // tk
    dims = _DIMS[mode]

    def body(a_ref, b_ref, o_ref, *acc):
        if hi:
            part = lax.dot_general(a_ref[...], b_ref[...], (dims, ((), ())), precision=_HI, preferred_element_type=F32)
        else:
            part = _dg(a_ref[...], b_ref[...], dims)
        if nk == 1:
            o_ref[...] = part.astype(o_ref.dtype)
        else:
            k = pl.program_id(2)

            @pl.when(k == 0)
            def _():
                acc[0][...] = part

            @pl.when(k > 0)
            def _():
                acc[0][...] += part

            @pl.when(k == nk - 1)
            def _():
                o_ref[...] = acc[0][...].astype(o_ref.dtype)

    if mode == "tn":
        a_spec = pl.BlockSpec((tk, tm), lambda i, j, k: (k, i))
    else:
        a_spec = pl.BlockSpec((tm, tk), lambda i, j, k: (i, k))
    if mode == "nt":
        b_spec = pl.BlockSpec((tn, tk), lambda i, j, k: (j, k))
    else:
        b_spec = pl.BlockSpec((tk, tn), lambda i, j, k: (k, j))
    isz = lambda x: jnp.dtype(x.dtype).itemsize
    vmem = 2 * (tm * tk * isz(a) + tk * tn * isz(b) + tm * tn * jnp.dtype(out_dtype).itemsize) + 3 * tm * tn * 4
    return _pc(
        body, name=name, grid=(M // tm, N // tn, nk),
        in_specs=[a_spec, b_spec], out_specs=pl.BlockSpec((tm, tn), lambda i, j, k: (i, j)),
        out_shape=_sds((M, N), out_dtype),
        scratch_shapes=[pltpu.VMEM((tm, tn), F32)] if nk > 1 else [],
        compiler_params=_cp(("parallel", "parallel", "arbitrary"), vmem + (8 << 20)),
    )(a, b)


def matmul_layers(a, b, mode, name):
    nl = b.shape[0]
    a3 = a if a.ndim == 3 else a[None]
    shared = a3.shape[0] == 1
    M = a3.shape[2] if mode == "tn" else a3.shape[1]
    N = b.shape[1] if mode == "nt" else b.shape[2]

    def body(a_ref, b_ref, o_ref):
        o_ref[0] = _dg(a_ref[0], b_ref[0], _DIMS[mode])

    return _pc(body, name=name, grid=(nl,),
               in_specs=[pl.BlockSpec((1,) + a3.shape[1:], (lambda l: (0, 0, 0)) if shared else (lambda l: (l, 0, 0))),
                         pl.BlockSpec((1,) + b.shape[1:], lambda l: (l, 0, 0))],
               out_specs=pl.BlockSpec((1, M, N), lambda l: (l, 0, 0)), out_shape=_sds((nl, M, N), F32),
               compiler_params=_cp(("parallel",), 48 << 20))(a3, b)


def out_proj_fwd(pieces, w, name):
    T = pieces[0][0].shape[0]
    arrs, offs = [a for a, _ in pieces], [o for _, o in pieces]
    n = len(arrs)
    tm = _div_tile(T, 640, 16)

    def body(*refs):
        w_ref, o_ref = refs[n], refs[n + 1]
        acc = None
        for j in range(n):
            part = _dg(refs[j][...], w_ref[offs[j]:offs[j] + arrs[j].shape[1], :], _DIMS["nn"])
            acc = part if acc is None else acc + part
        o_ref[...] = acc.astype(o_ref.dtype)

    return _pc(body, name=name, grid=(T // tm,),
               in_specs=[pl.BlockSpec((tm, a.shape[1]), lambda i: (i, 0)) for a in arrs] + [pl.BlockSpec(w.shape, lambda i: (0, 0))],
               out_specs=pl.BlockSpec((tm, w.shape[1]), lambda i: (i, 0)), out_shape=_sds((T, w.shape[1]), BF16),
               compiler_params=_cp(("parallel",), 32 << 20))(*arrs, w)


def out_proj_dw(pieces, dy, name):
    T, N = dy.shape
    arrs, offs = [a for a, _ in pieces], [o for _, o in pieces]
    n = len(arrs)
    rows = sum(a.shape[1] for a in arrs)
    tn = 512

    def body(*refs):
        d_ref, o_ref = refs[n], refs[n + 1]
        for j in range(n):
            o_ref[offs[j]:offs[j] + arrs[j].shape[1], :] = _dg(refs[j][...], d_ref[...], _DIMS["tn"]).astype(o_ref.dtype)

    return _pc(body, name=name, grid=(N // tn,),
               in_specs=[pl.BlockSpec(a.shape, lambda j: (0, 0)) for a in arrs] + [pl.BlockSpec((T, tn), lambda j: (0, j))],
               out_specs=pl.BlockSpec((rows, tn), lambda j: (0, j)), out_shape=_sds((rows, N), BF16),
               compiler_params=_cp(("parallel",), 48 << 20))(*arrs, dy)


def in_proj_bwd(pieces, h1, w, name):
    T = h1.shape[0]
    arrs = [a for a, _ in pieces]
    offs = [o for _, o in pieces]
    wid = [a.shape[1] for a in arrs]
    n = len(arrs)
    assert sum(wid) == IN_PAD, "the pieces must tile all columns of P"
    tm = _div_tile(T, 640, 16)

    def dx_body(*refs):
        w_ref, o_ref = refs[n], refs[n + 1]
        acc = None
        for j in range(n):
            part = _dg(refs[j][...], w_ref[:, offs[j]:offs[j] + wid[j]], _DIMS["nt"])
            acc = part if acc is None else acc + part
        o_ref[...] = acc.astype(o_ref.dtype)

    dh1 = _pc(dx_body, name=name + "_dx", grid=(T // tm,),
              in_specs=[pl.BlockSpec((tm, wj), lambda i: (i, 0)) for wj in wid] + [pl.BlockSpec((D, IN_PAD), lambda i: (0, 0))],
              out_specs=pl.BlockSpec((tm, D), lambda i: (i, 0)), out_shape=_sds((T, D), BF16),
              compiler_params=_cp(("parallel",), 40 << 20))(*arrs, w)

    tmd, nk = 512, 4
    tk = T // nk

    def dw_body(h_ref, *refs):
        o_ref, acc = refs[n], refs[n + 1]
        k = pl.program_id(1)

        @pl.when(k == 0)
        def _():
            acc[...] = jnp.zeros_like(acc)

        for j in range(n):
            acc[:, offs[j]:offs[j] + wid[j]] += _dg(h_ref[...], refs[j][...], _DIMS["tn"])

        @pl.when(k == nk - 1)
        def _():
            o_ref[...] = acc[...].astype(o_ref.dtype)

    dw = _pc(dw_body, name=name + "_dw", grid=(D // tmd, nk),
             in_specs=[pl.BlockSpec((tk, tmd), lambda i, k: (k, i))] + [pl.BlockSpec((tk, wj), lambda i, k: (k, 0)) for wj in wid],
             out_specs=pl.BlockSpec((tmd, IN_PAD), lambda i, k: (i, 0)), out_shape=_sds((D, IN_PAD), BF16),
             scratch_shapes=[pltpu.VMEM((tmd, IN_PAD), F32)], compiler_params=_cp(("parallel", "arbitrary"), 48 << 20))(h1, *arrs)
    return dh1, dw


def _norm_mod(xo, shift, scale, g):
    r = lax.rsqrt(jnp.mean(xo * xo, axis=-1, keepdims=True) + EPS)
    return (xo * r) * g * (1.0 + scale) + shift


def res_norm_mod(x, y, gsv, g, nL, name):
    T = x.shape[0]
    has_y = y is not None

    def body(*refs):
        if has_y:
            x_ref, y_ref, gsv_ref, g_ref, xo_ref, h_ref = refs
            xo = x_ref[...] + gsv_ref[0, 0:1, :] * y_ref[...]
            xo_ref[...] = xo
        else:
            x_ref, gsv_ref, g_ref, h_ref = refs
            xo = x_ref[...]
        h_ref[...] = _norm_mod(xo, gsv_ref[0, 1:2, :], gsv_ref[0, 2:3, :], g_ref[...]).astype(h_ref.dtype)

    row = pl.BlockSpec((TR, D), lambda i: (i, 0))
    in_specs = [row] + ([row] if has_y else []) + [pl.BlockSpec((1, 8, D), lambda i: (i // nL, 0, 0)),
                                                     pl.BlockSpec((1, D), lambda i: (0, 0))]
    out_specs = ([row] if has_y else []) + [row]
    out_shape = ([_sds((T, D), F32)] if has_y else []) + [_sds((T, D), BF16)]
    args = (x, y, gsv, g) if has_y else (x, gsv, g)
    outs = _pc(body, name=name, grid=(T // TR,), in_specs=in_specs, out_specs=out_specs, out_shape=out_shape,
               compiler_params=_cp(("arbitrary",), 24 << 20))(*args)
    return (outs[0], outs[1]) if has_y else (None, outs[0])


def res_norm_mod_bwd(xo, y, gsv, g, dh, dres, nL, name):
    T = xo.shape[0]
    has_y = y is not None

    def body(*refs):
        if has_y:
            xo_ref, y_ref, gsv_ref, g_ref, dh_ref, dres_ref, dx_ref, dy_ref, dgsv_ref, dg_ref = refs
        else:
            xo_ref, gsv_ref, g_ref, dh_ref, dres_ref, dx_ref, dgsv_ref, dg_ref = refs
        i = pl.program_id(0)

        @pl.when((i == 0) | (i == nL))
        def _():
            dgsv_ref[...] = jnp.zeros_like(dgsv_ref)

        @pl.when(i == 0)
        def _():
            dg_ref[...] = jnp.zeros_like(dg_ref)

        _, vjp = jax.vjp(_norm_mod, xo_ref[...], gsv_ref[0, 1:2, :], gsv_ref[0, 2:3, :], g_ref[...])
        dxn, dshift, dscale, dg = vjp(dh_ref[...].astype(F32))
        dxo = dres_ref[...] + dxn
        dx_ref[...] = dxo
        if has_y:
            dy_ref[...] = (gsv_ref[0, 0:1, :] * dxo).astype(dy_ref.dtype)
            dgsv_ref[0, 0:1, :] += jnp.sum(y_ref[...] * dxo, axis=0, keepdims=True)
        dgsv_ref[0, 1:2, :] += dshift
        dgsv_ref[0, 2:3, :] += dscale
        dg_ref[0:1, :] += dg

    row = pl.BlockSpec((TR, D), lambda i: (i, 0))
    gspec = pl.BlockSpec((1, 8, D), lambda i: (i // nL, 0, 0))
    in_specs = [row] + ([row] if has_y else []) + [gspec, pl.BlockSpec((1, D), lambda i: (0, 0)), row, row]
    out_specs = [row] + ([row] if has_y else []) + [gspec, pl.BlockSpec((8, D), lambda i: (0, 0))]
    out_shape = [_sds((T, D), F32)] + ([_sds((T, D), BF16)] if has_y else []) + [_sds((2, 8, D), F32), _sds((8, D), F32)]
    args = (xo, y, gsv, g, dh, dres) if has_y else (xo, gsv, g, dh, dres)
    outs = _pc(body, name=name, grid=(T // TR,), in_specs=in_specs, out_specs=out_specs, out_shape=out_shape,
               compiler_params=_cp(("arbitrary",), 32 << 20))(*args)
    if has_y:
        return outs
    return outs[0], None, outs[1], outs[2]


def final_loss(x, y, gsv, g, target, nL, name):
    T = x.shape[0]

    def lossf(xo, gv, t):
        yn = (xo * lax.rsqrt(jnp.mean(xo * xo, axis=-1, keepdims=True) + EPS)) * gv
        e = yn - t
        return 0.5 * jnp.sum(jnp.sum(e * e, axis=-1, keepdims=True) * (1.0 / D), axis=0, keepdims=True)

    def body(x_ref, y_ref, gsv_ref, g_ref, t_ref, loss_ref, dx_ref, dy_ref, dgsv_ref, dg_ref):
        i = pl.program_id(0)

        @pl.when(i == 0)
        def _():
            loss_ref[...] = jnp.zeros_like(loss_ref)
            dg_ref[...] = jnp.zeros_like(dg_ref)

        @pl.when((i == 0) | (i == nL))
        def _():
            dgsv_ref[...] = jnp.zeros_like(dgsv_ref)

        @pl.when(i < nL)
        def _():
            gate = gsv_ref[0, 0:1, :]
            yv = y_ref[...]
            xo = x_ref[...] + gate * yv
            lv, vjp = jax.vjp(lossf, xo, g_ref[...], t_ref[...])
            dxo, dg, _ = vjp(jnp.ones((1, 1), F32))
            loss_ref[...] += jnp.broadcast_to(lv, loss_ref.shape)
            dx_ref[...] = dxo
            dy_ref[...] = (gate * dxo).astype(dy_ref.dtype)
            dgsv_ref[0, 0:1, :] += jnp.sum(yv * dxo, axis=0, keepdims=True)
            dg_ref[0:1, :] += dg

        @pl.when(i >= nL)
        def _():
            dx_ref[...] = jnp.zeros_like(dx_ref)
            dy_ref[...] = jnp.zeros_like(dy_ref)

    row = pl.BlockSpec((TR, D), lambda i: (i, 0))
    gspec = pl.BlockSpec((1, 8, D), lambda i: (i // nL, 0, 0))
    return _pc(
        body, name=name, grid=(T // TR,),
        in_specs=[row, row, gspec, pl.BlockSpec((1, D), lambda i: (0, 0)),
                  pl.BlockSpec((TR, D), lambda i: (jnp.minimum(i, nL - 1), 0))],
        out_specs=[pl.BlockSpec((8, 128), lambda i: (0, 0)), row, row, gspec, pl.BlockSpec((8, D), lambda i: (0, 0))],
        out_shape=[_sds((8, 128), F32), _sds((T, D), F32), _sds((T, D), BF16), _sds((2, 8, D), F32), _sds((8, D), F32)],
        compiler_params=_cp(("arbitrary",), 32 << 20),
    )(x, y, gsv, g, target)


FI_BLK = 2 * D_FF // 4


def _fi_chip(j):
    return (j % 2) * 2 + j // 2


def matmul_fi(a, b, mode, out_dtype, name):
    T = a.shape[0]
    if mode == "tn":
        tmd = 512

        def body(a_ref, b_ref, o_ref):
            o_ref[0] = _dg(a_ref[...], b_ref[...], _DIMS["tn"]).astype(o_ref.dtype)

        return _pc(body, name=name, grid=(D // tmd, 4),
                   in_specs=[pl.BlockSpec((T, tmd), lambda i, j: (0, i)), pl.BlockSpec((T, FI_BLK), lambda i, j: (0, j))],
                   out_specs=pl.BlockSpec((1, tmd, FI_BLK), lambda i, j: (_fi_chip(j), i, 0)),
                   out_shape=_sds((4, D, FI_BLK), out_dtype), compiler_params=_cp(("parallel", "arbitrary"), 48 << 20))(a, b)
    if mode == "nn":
        tm = _div_tile(T, 1280, 16)

        def body(a_ref, b_ref, o_ref):
            o_ref[...] = _dg(a_ref[...], b_ref[0], _DIMS["nn"]).astype(o_ref.dtype)

        return _pc(body, name=name, grid=(T // tm, 4),
                   in_specs=[pl.BlockSpec((tm, D), lambda i, j: (i, 0)), pl.BlockSpec((1, D, FI_BLK), lambda i, j: (_fi_chip(j), 0, 0))],
                   out_specs=pl.BlockSpec((tm, FI_BLK), lambda i, j: (i, j)), out_shape=_sds((T, 4 * FI_BLK), out_dtype),
                   compiler_params=_cp(("parallel", "arbitrary"), 40 << 20))(a, b)
    tm = _div_tile(T, 640, 16)

    def body(a_ref, b_ref, o_ref):
        acc = None
        for k in range(4):
            part = _dg(a_ref[:, k * FI_BLK:(k + 1) * FI_BLK], b_ref[_fi_chip(k)], _DIMS["nt"])
            acc = part if acc is None else acc + part
        o_ref[...] = acc.astype(o_ref.dtype)

    return _pc(body, name=name, grid=(T // tm,),
               in_specs=[pl.BlockSpec((tm, 4 * FI_BLK), lambda i: (i, 0)), pl.BlockSpec((4, D, FI_BLK), lambda i: (0, 0, 0))],
               out_specs=pl.BlockSpec((tm, D), lambda i: (i, 0)), out_shape=_sds((T, D), out_dtype),
               compiler_params=_cp(("parallel",), VMEM_CAP))(a, b)


def _swiglu(gate, up):
    return _silu(gate) * up


def swiglu_fwd(gu, name):
    T = gu.shape[0]

    def body(x_ref, o_ref):
        o_ref[...] = _swiglu(x_ref[:, :FI_BLK].astype(F32), x_ref[:, FI_BLK:].astype(F32)).astype(o_ref.dtype)

    return _pc(body, name=name, grid=(T // TR, 2), in_specs=[pl.BlockSpec((TR, 2 * FI_BLK), lambda i, j: (i, j))],
               out_specs=pl.BlockSpec((TR, FI_BLK), lambda i, j: (i, j)), out_shape=_sds((T, D_FF), BF16),
               compiler_params=_cp(("parallel", "parallel"), 24 << 20))(gu)


def swiglu_bwd(gu, dact, name):
    T = gu.shape[0]

    def body(x_ref, d_ref, o_ref):
        g, u, d = x_ref[:, :FI_BLK].astype(F32), x_ref[:, FI_BLK:].astype(F32), d_ref[...].astype(F32)
        sg = 1.0 / (1.0 + jnp.exp(-g))
        sl = g * sg
        o_ref[:, :FI_BLK] = (d * u * (sg + sl * (1.0 - sg))).astype(o_ref.dtype)
        o_ref[:, FI_BLK:] = (d * sl).astype(o_ref.dtype)

    return _pc(body, name=name, grid=(T // TR, 2),
               in_specs=[pl.BlockSpec((TR, 2 * FI_BLK), lambda i, j: (i, j)), pl.BlockSpec((TR, FI_BLK), lambda i, j: (i, j))],
               out_specs=pl.BlockSpec((TR, 2 * FI_BLK), lambda i, j: (i, j)), out_shape=_sds((T, 2 * D_FF), BF16),
               compiler_params=_cp(("parallel", "parallel"), 32 << 20))(gu, dact)


def rope_tables(L, Lc):
    t = np.arange(L)
    rows, cols = t // GRID_W, t % GRID_W
    inv = ROPE_BASE ** (-np.arange(16, dtype=np.float32) / 16)
    lane = np.arange(64)
    pos = np.where((lane // 32)[None, :] == 0, rows[:, None], cols[:, None]).astype(np.float32)
    ang = jnp.asarray(pos) * jnp.asarray(inv[lane % 16])[None, :]
    cos = jnp.concatenate([jnp.cos(ang), jnp.ones((Lc, 64), F32)], axis=0)
    sin = jnp.concatenate([jnp.sin(ang), jnp.zeros((Lc, 64), F32)], axis=0)
    R = np.zeros((128, 128), np.float32)
    for i in range(128):
        if (i % 32) < 16:
            R[i + 16, i] = -1.0
        else:
            R[i - 16, i] = 1.0
    return jnp.tile(cos, (1, 2)), jnp.tile(sin, (1, 2)), jnp.asarray(R)


def rope_apply(q_src, q_col, k_src, k_col, cos, sin, R, transpose, name, kv_src=None):
    T = cos.shape[0]
    with_kv = kv_src is not None

    def rot(x, c, s, Rm):
        if transpose:
            return x * c + hdot(x * s, Rm, "nt")
        return x * c + hdot(x, Rm) * s

    def body(q_ref, k_ref, c_ref, s_ref, R_ref, *rest):
        qo_ref, ko_ref = rest[-4:-2] if with_kv else rest
        c, s, Rm = c_ref[...], s_ref[...], R_ref[...]
        for j in range(2):
            qo_ref[:, j * 128:(j + 1) * 128] = rot(q_ref[:, j * 128:(j + 1) * 128].astype(F32), c, s, Rm).astype(qo_ref.dtype)
        ko_ref[...] = rot(k_ref[...].astype(F32), c, s, Rm).astype(ko_ref.dtype)
        if with_kv:
            rest[-2][...] = rest[0][...].astype(BF16)
            rest[-1][...] = rest[1][...].astype(BF16)

    tab = pl.BlockSpec((TR, 128), lambda i: (i, 0))
    wide = pl.BlockSpec((TR, 256), lambda i: (i, 0))
    kv_in = [pl.BlockSpec((TR, 256), lambda i: (i, C_KB // 256)), pl.BlockSpec((TR, 256), lambda i: (i, C_VB // 256))] if with_kv else []
    return _pc(body, name=name, grid=(T // TR,),
               in_specs=[pl.BlockSpec((TR, 256), lambda i: (i, q_col)), pl.BlockSpec((TR, 128), lambda i: (i, k_col)),
                         tab, tab, pl.BlockSpec((128, 128), lambda i: (0, 0))] + kv_in,
               out_specs=[wide, tab] + ([wide, wide] if with_kv else []),
               out_shape=[_sds((T, 256), BF16), _sds((T, 128), BF16)] + ([_sds((T, 256), BF16)] * 2 if with_kv else []),
               compiler_params=_cp(("parallel",), 16 << 20))(q_src, k_src, cos, sin, R, *([kv_src, kv_src] if with_kv else []))


_SCALE = HD ** -0.5


def _attn_tile(qh, ks, vs, extra):
    ss = []
    for k, add in ks:
        s = _dg(qh, k, _DIMS["nt"]) * _SCALE
        ss.append(s if add is None else s + add)
    m = ss[0].max(axis=-1, keepdims=True)
    for s in ss[1:]:
        m = jnp.maximum(m, s.max(axis=-1, keepdims=True))
    if extra is not None:
        m = jnp.maximum(m, extra)
    ps = [jnp.exp(s - m) for s in ss]
    den = ps[0].sum(axis=-1, keepdims=True)
    for p in ps[1:]:
        den = den + p.sum(axis=-1, keepdims=True)
    if extra is not None:
        den = den + jnp.exp(extra - m)
    num = _dg(ps[0], vs[0], _DIMS["nn"])
    for p, v in zip(ps[1:], vs[1:]):
        num = num + _dg(p, v, _DIMS["nn"])
    linv = 1.0 / den
    return num * linv, m, linv


def _attn_bwd_tile(qh, ks, vs, extra, m, linv, oh, doh):
    delta = jnp.sum(doh * oh, axis=-1, keepdims=True)
    dq = None
    dks, dvs, dss = [], [], []
    for (k, add), v in zip(ks, vs):
        s = _dg(qh, k, _DIMS["nt"]) * _SCALE
        if add is not None:
            s = s + add
        p = jnp.exp(s - m) * linv
        dvs.append(_dg(p, doh, _DIMS["tn"]))
        ds = p * (_dg(doh, v, _DIMS["nt"]) - delta)
        dss.append(ds)
        dsq = ds * _SCALE
        part = _dg(dsq, k, _DIMS["nn"])
        dq = part if dq is None else dq + part
        dks.append(_dg(dsq, qh, _DIMS["tn"]))
    dextra = None
    if extra is not None:
        dextra = -jnp.sum(jnp.exp(extra - m) * linv * delta, axis=0, keepdims=True)
    return dq, dks, dvs, dss, dextra


def _wa_mask(n, L):
    qpos = n * WA_BLK + lax.broadcasted_iota(jnp.int32, (WA_BLK, 3 * WA_BLK), 0)
    kpos = (n - 1) * WA_BLK + lax.broadcasted_iota(jnp.int32, (WA_BLK, 3 * WA_BLK), 1)
    ok = (jnp.abs(qpos - kpos) <= WA_BLK) & (kpos >= 0) & (kpos < L)
    return jnp.where(ok, 0.0, NEG).astype(F32)


WA_BPS = 2


def _wa_specs(L, Lc):
    nb = L // WA_BLK
    cb = L // Lc

    def blk(j, col):
        return pl.BlockSpec((WA_BLK, 128), lambda s: (jnp.clip(s * WA_BPS - 1 + j, 0, nb - 1), col))

    vcol = C_VA // 128
    kspecs = [blk(j, 0) for j in range(WA_BPS + 2)] + [pl.BlockSpec((Lc, 128), lambda s: (cb, 0))]
    vspecs = [blk(j, vcol) for j in range(WA_BPS + 2)] + [pl.BlockSpec((Lc, 128), lambda s: (cb, vcol))]
    return nb, kspecs, vspecs


def win_attn_fwd(qr, kr, P, sink, L, Lc, name):
    T = L + Lc
    nb, kspecs, vspecs = _wa_specs(L, Lc)
    nk = WA_BPS + 2
    QB = WA_BPS * WA_BLK
    nlat = nb // WA_BPS

    def body(q_ref, *refs):
        kbs, kx, vbs, vx, s_ref, o_ref, st_ref = refs[:nk], refs[nk], refs[nk + 1:2 * nk + 1], refs[2 * nk + 1], refs[-3], refs[-2], refs[-1]
        s = pl.program_id(0)

        def put(qs, h, res):
            o, m, linv = res
            o_ref[qs, h * HD:(h + 1) * HD] = o.astype(o_ref.dtype)
            st_ref[qs, h:h + 1] = m
            st_ref[qs, WA_HEADS + h:WA_HEADS + h + 1] = linv

        @pl.when(s < nlat)
        def _():
            for b in range(WA_BPS):
                mask = _wa_mask(s * WA_BPS + b, L)
                qs = slice(b * WA_BLK, (b + 1) * WA_BLK)
                for g in range(WA_KV):
                    sl = slice(g * HD, (g + 1) * HD)
                    k3 = jnp.concatenate([kbs[b + j][:, sl] for j in range(3)], axis=0)
                    v3 = jnp.concatenate([vbs[b + j][:, sl] for j in range(3)], axis=0)
                    for r in range(2):
                        h = 2 * g + r
                        put(qs, h, _attn_tile(q_ref[qs, h * HD:(h + 1) * HD], [(k3, mask), (kx[:, sl], None)], [v3, vx[:, sl]], s_ref[h:h + 1, 0:1]))

        @pl.when(s >= nlat)
        def _():
            for h in range(WA_HEADS):
                sl = slice((h // 2) * HD, (h // 2 + 1) * HD)
                put(slice(None), h, _attn_tile(q_ref[:, h * HD:(h + 1) * HD], [(kx[:, sl], None)], [vx[:, sl]], s_ref[h:h + 1, 0:1]))

    qspec = pl.BlockSpec((QB, 256), lambda s: (s, 0))
    return _pc(body, name=name, grid=(T // QB,),
               in_specs=[qspec] + kspecs + vspecs + [pl.BlockSpec((8, 128), lambda s: (0, 0))],
               out_specs=[qspec, pl.BlockSpec((QB, 8), lambda s: (s, 0))], out_shape=[_sds((T, 256), BF16), _sds((T, 8), F32)],
               compiler_params=_cp(("arbitrary",), 32 << 20))(qr, *([kr] * (nk + 1)), *([P] * (nk + 1)), sink)


def win_attn_bwd(qr, kr, P, sink, do_src, o, stats, L, Lc, name):
    T = L + Lc
    nb, kspecs, vspecs = _wa_specs(L, Lc)
    nk = WA_BPS + 2
    QB = WA_BPS * WA_BLK
    nlat = nb // WA_BPS
    cx = WA_BLK + L

    def body(q_ref, *refs):
        kbs, kx, vbs, vx = refs[:nk], refs[nk], refs[nk + 1:2 * nk + 1], refs[2 * nk + 1]
        s_ref, do_ref, o_ref, st_ref, dq_ref, dk_ref, dv_ref, ds_ref = refs[2 * nk + 2:]
        s = pl.program_id(0)

        @pl.when(s == 0)
        def _():
            dk_ref[...] = jnp.zeros_like(dk_ref)
            dv_ref[...] = jnp.zeros_like(dv_ref)
            ds_ref[...] = jnp.zeros_like(ds_ref)

        def tile(qs, h, ks, vs):
            hs = slice(h * HD, (h + 1) * HD)
            dq, dks, dvs, _, dsk = _attn_bwd_tile(q_ref[qs, hs], ks, vs, s_ref[h:h + 1, 0:1], st_ref[qs, h:h + 1],
                                                  st_ref[qs, WA_HEADS + h:WA_HEADS + h + 1], o_ref[qs, hs].astype(F32), do_ref[qs, hs].astype(F32))
            dq_ref[qs, hs] = dq
            ds_ref[h:h + 1, :] += jnp.broadcast_to(dsk, (1, 128))
            return dks, dvs

        @pl.when(s < nlat)
        def _():
            for b in range(WA_BPS):
                n = s * WA_BPS + b
                mask = _wa_mask(n, L)
                rows = pl.ds(pl.multiple_of(n * WA_BLK, WA_BLK), 3 * WA_BLK)
                qs = slice(b * WA_BLK, (b + 1) * WA_BLK)
                for g in range(WA_KV):
                    sl = slice(g * HD, (g + 1) * HD)
                    k3 = jnp.concatenate([kbs[b + j][:, sl] for j in range(3)], axis=0)
                    v3 = jnp.concatenate([vbs[b + j][:, sl] for j in range(3)], axis=0)
                    acc = None
                    for r in range(2):
                        dks, dvs = tile(qs, 2 * g + r, [(k3, mask), (kx[:, sl], None)], [v3, vx[:, sl]])
                        acc = dks + dvs if acc is None else [a + b_ for a, b_ in zip(acc, dks + dvs)]
                    dk_ref[rows, sl] += acc[0]
                    dk_ref[cx:cx + Lc, sl] += acc[1]
                    dv_ref[rows, sl] += acc[2]
                    dv_ref[cx:cx + Lc, sl] += acc[3]

        @pl.when(s >= nlat)
        def _():
            for h in range(WA_HEADS):
                sl = slice((h // 2) * HD, (h // 2 + 1) * HD)
                dks, dvs = tile(slice(None), h, [(kx[:, sl], None)], [vx[:, sl]])
                dk_ref[cx:cx + Lc, sl] += dks[0]
                dv_ref[cx:cx + Lc, sl] += dvs[0]

    qspec = pl.BlockSpec((QB, 256), lambda s: (s, 0))
    acc_spec = pl.BlockSpec((T + 2 * WA_BLK, 128), lambda s: (0, 0))
    return _pc(body, name=name, grid=(T // QB,),
               in_specs=[qspec] + kspecs + vspecs + [pl.BlockSpec((8, 128), lambda s: (0, 0)), qspec, qspec, pl.BlockSpec((QB, 8), lambda s: (s, 0))],
               out_specs=[qspec, acc_spec, acc_spec, pl.BlockSpec((8, 128), lambda s: (0, 0))],
               out_shape=[_sds((T, 256), F32), _sds((T + 2 * WA_BLK, 128), F32), _sds((T + 2 * WA_BLK, 128), F32), _sds((8, 128), F32)],
               compiler_params=_cp(("arbitrary",), 40 << 20))(qr, *([kr] * (nk + 1)), *([P] * (nk + 1)), sink, do_src, o, stats)


def na_index_tables():
    qc = np.arange(GRID_W)[:, None]
    kc = np.arange(GRID_W)[None, :]
    cstart = np.clip(qc - NA_KW // 2, 0, GRID_W - NA_KW)
    ok = (kc >= cstart) & (kc < cstart + NA_KW)
    dx = np.clip(kc - qc, -(NA_KW - 1), NA_KW - 1) + (NA_KW - 1)
    off = np.arange(NA_KH)[:, None]
    kr = np.arange(NA_KH)[None, :]
    dy = kr - off + (NA_KH - 1)
    return ok, dx, dy


def _na_selectors():
    ok, dx, dy = na_index_tables()
    e1 = np.zeros((GRID_W * GRID_W, 128), np.float32)
    qi, ki = np.nonzero(ok)
    e1[qi * GRID_W + ki, dx[qi, ki]] = 1.0
    e2 = np.zeros((16, NA_KH * NA_KH), np.float32)
    oi, ri = np.meshgrid(np.arange(NA_KH), np.arange(NA_KH), indexing="ij")
    e2[dy[oi, ri].ravel(), (oi * NA_KH + ri).ravel()] = 1.0
    return ok, jnp.asarray(e1), jnp.asarray(np.kron(np.eye(NA_HEADS, dtype=np.float32), e2))


def na_bias_table(rpb, tag):
    ok, e1, e2 = _na_selectors()
    r2 = jnp.pad(rpb.astype(F32), ((0, 0), (0, 1), (0, 128 - (2 * NA_KW - 1)))).reshape(NA_HEADS * 16, 128)
    r1 = matmul(e2, r2, "tn", F32, f"na_bias_sel1_{tag}", hi=True)
    x = matmul(r1, e1, "nt", F32, f"na_bias_sel2_{tag}", hi=True)
    b = x.reshape(NA_HEADS, NA_KH, NA_KH, GRID_W, GRID_W).transpose(0, 1, 3, 2, 4)
    b = b + jnp.asarray(np.where(ok, 0.0, NEG).astype(np.float32))[None, None, :, None, :]
    return b.reshape(NA_HEADS, NA_KH, GRID_W, NA_KH * GRID_W)


def _na_rows(r, GR):
    r0 = jnp.clip(r - NA_KH // 2, 0, GR - NA_KH)
    return r0, jnp.clip(r - r0, 0, NA_KH - 1)


NA_RPS = 4


def na_fwd(P, kb, vb, bias, L, Lc, name):
    T = L + Lc
    GR = L // GRID_W
    W = NA_KH * GRID_W
    QB = GRID_W * NA_RPS
    nlat = GR // NA_RPS

    def body(q_ref, k_ref, v_ref, b_ref, o_ref, st_ref):
        s = pl.program_id(0)

        def put(qs, h, res):
            o, m, linv = res
            o_ref[qs, h * HD:(h + 1) * HD] = o.astype(o_ref.dtype)
            st_ref[qs, h:h + 1] = m
            st_ref[qs, NA_HEADS + h:NA_HEADS + h + 1] = linv

        @pl.when(s < nlat)
        def _():
            for rr in range(NA_RPS):
                r0, off = _na_rows(s * NA_RPS + rr, GR)
                rows = pl.ds(pl.multiple_of(r0 * GRID_W, GRID_W), W)
                qs = slice(rr * GRID_W, (rr + 1) * GRID_W)
                for h in range(NA_HEADS):
                    hs = slice(h * HD, (h + 1) * HD)
                    put(qs, h, _attn_tile(q_ref[qs, hs], [(k_ref[rows, hs], b_ref[h, off]), (k_ref[L:T, hs], None)],
                                          [v_ref[rows, hs], v_ref[L:T, hs]], None))

        @pl.when(s >= nlat)
        def _():
            for h in range(NA_HEADS):
                hs = slice(h * HD, (h + 1) * HD)
                put(slice(None), h, _attn_tile(q_ref[:, hs], [(k_ref[L:T, hs], None)], [v_ref[L:T, hs]], None))

    one = pl.Buffered(1)
    return _pc(body, name=name, grid=(T // QB,),
               in_specs=[pl.BlockSpec((QB, 256), lambda r: (r, C_QB // 256)),
                         pl.BlockSpec((T, 256), lambda r: (0, 0), pipeline_mode=one),
                         pl.BlockSpec((T, 256), lambda r: (0, 0), pipeline_mode=one),
                         pl.BlockSpec((NA_HEADS, NA_KH, GRID_W, W), lambda r: (0, 0, 0, 0), pipeline_mode=one)],
               out_specs=[pl.BlockSpec((QB, 256), lambda r: (r, 0)), pl.BlockSpec((QB, 8), lambda r: (r, 0))],
               out_shape=[_sds((T, 256), BF16), _sds((T, 8), F32)],
               compiler_params=_cp(("arbitrary",), 32 << 20))(P, kb, vb, bias)


def na_bwd(P, kb, vb, bias, do_src, o, stats, L, Lc, name):
    T = L + Lc
    GR = L // GRID_W
    W = NA_KH * GRID_W
    QB = GRID_W * NA_RPS
    nlat = GR // NA_RPS

    def body(q_ref, k_ref, v_ref, b_ref, do_ref, o_ref, st_ref, dq_ref, dk_ref, dv_ref, db_ref):
        s = pl.program_id(0)

        @pl.when(s == 0)
        def _():
            dk_ref[...] = jnp.zeros_like(dk_ref)
            dv_ref[...] = jnp.zeros_like(dv_ref)
            db_ref[...] = jnp.zeros_like(db_ref)

        def tile(qs, h, ks, vs):
            hs = slice(h * HD, (h + 1) * HD)
            dq, dks, dvs, dss, _ = _attn_bwd_tile(q_ref[qs, hs], ks, vs, None, st_ref[qs, h:h + 1], st_ref[qs, NA_HEADS + h:NA_HEADS + h + 1],
                                                  o_ref[qs, hs].astype(F32), do_ref[qs, hs].astype(F32))
            dq_ref[qs, hs] = dq.astype(dq_ref.dtype)
            return dks, dvs, dss

        @pl.when(s < nlat)
        def _():
            for rr in range(NA_RPS):
                r0, off = _na_rows(s * NA_RPS + rr, GR)
                rows = pl.ds(pl.multiple_of(r0 * GRID_W, GRID_W), W)
                qs = slice(rr * GRID_W, (rr + 1) * GRID_W)
                for h in range(NA_HEADS):
                    hs = slice(h * HD, (h + 1) * HD)
                    dks, dvs, dss = tile(qs, h, [(k_ref[rows, hs], b_ref[h, off]), (k_ref[L:T, hs], None)], [v_ref[rows, hs], v_ref[L:T, hs]])
                    dk_ref[rows, hs] += dks[0]
                    dv_ref[rows, hs] += dvs[0]
                    dk_ref[L:T, hs] += dks[1]
                    dv_ref[L:T, hs] += dvs[1]
                    db_ref[h, off] += dss[0]

        @pl.when(s >= nlat)
        def _():
            for h in range(NA_HEADS):
                hs = slice(h * HD, (h + 1) * HD)
                dks, dvs, _ = tile(slice(None), h, [(k_ref[L:T, hs], None)], [v_ref[L:T, hs]])
                dk_ref[L:T, hs] += dks[0]
                dv_ref[L:T, hs] += dvs[0]

    one = pl.Buffered(1)
    full = lambda shape: pl.BlockSpec(shape, lambda r: (0,) * len(shape), pipeline_mode=one)
    qspec = pl.BlockSpec((QB, 256), lambda r: (r, 0))
    return _pc(body, name=name, grid=(T // QB,),
               in_specs=[pl.BlockSpec((QB, 256), lambda r: (r, C_QB // 256)), full((T, 256)), full((T, 256)),
                         full((NA_HEADS, NA_KH, GRID_W, W)), pl.BlockSpec((QB, 256), lambda r: (r, 1)), qspec, pl.BlockSpec((QB, 8), lambda r: (r, 0))],
               out_specs=[qspec, full((T, 256)), full((T, 256)), full((NA_HEADS, NA_KH, GRID_W, W))],
               out_shape=[_sds((T, 256), BF16), _sds((T, 256), F32), _sds((T, 256), F32), _sds((NA_HEADS, NA_KH, GRID_W, W), F32)],
               compiler_params=_cp(("arbitrary",), 48 << 20))(P, kb, vb, bias, do_src, o, stats)


def na_rpb_grad(dbias, tag):
    _, e1, e2 = _na_selectors()
    x = dbias.reshape(NA_HEADS, NA_KH, GRID_W, NA_KH, GRID_W).transpose(0, 1, 3, 2, 4).reshape(NA_HEADS * NA_KH * NA_KH, GRID_W * GRID_W)
    r1 = matmul(x, e1, "nn", F32, f"na_rpb_sel1_{tag}", hi=True, tk=1024)
    r2 = matmul(e2, r1, "nn", F32, f"na_rpb_sel2_{tag}", hi=True)
    return r2.reshape(NA_HEADS, 16, 128)[:, :2 * NA_KH - 1, :2 * NA_KW - 1]


_HALO = 8


def _halo_specs(T, col0):
    nh = TR // _HALO
    cur = pl.BlockSpec((TR, 256), lambda i, j: (i, col0 + j))
    prv = pl.BlockSpec((_HALO, 256), lambda i, j: (jnp.maximum(i * nh - 1, 0), col0 + j))
    nxt = pl.BlockSpec((_HALO, 256), lambda i, j: (jnp.minimum((i + 1) * nh, T // _HALO - 1), col0 + j))
    return prv, cur, nxt


def _fill_ext(ext, prv, cur, nxt, i, nL, nT):
    has_prev = jnp.where((i != 0) & (i != nL), 1.0, 0.0)
    has_next = jnp.where((i != nL - 1) & (i != nT - 1), 1.0, 0.0)
    ext[0:_HALO, :] = prv[...].astype(F32) * has_prev
    ext[_HALO:_HALO + TR, :] = cur[...].astype(F32)
    ext[_HALO + TR:, :] = nxt[...].astype(F32) * has_next


def conv_silu_fwd(P, w8, b, nL, name):
    T = P.shape[0]
    nT = T // TR

    def body(prv, cur, nxt, w_ref, b_ref, pre_ref, act_ref, ext):
        i = pl.program_id(0)
        _fill_ext(ext, prv, cur, nxt, i, nL, nT)
        y = jnp.broadcast_to(b_ref[...], (TR, 256))
        for k in range(S_CONV):
            y = y + w_ref[k:k + 1, :] * ext[pl.ds(_HALO - S_CONV // 2 + k, TR), :]
        pre_ref[...] = y
        act_ref[...] = _silu(y)

    prv, cur, nxt = _halo_specs(T, C_XBC // 256)
    out = pl.BlockSpec((TR, 256), lambda i, j: (i, j))
    return _pc(body, name=name, grid=(nT, 4),
               in_specs=[prv, cur, nxt, pl.BlockSpec((8, 256), lambda i, j: (0, j)), pl.BlockSpec((1, 256), lambda i, j: (0, j))],
               out_specs=[out, out], out_shape=[_sds((T, 1024), F32), _sds((T, 1024), F32)],
               scratch_shapes=[pltpu.VMEM((TR + 2 * _HALO, 256), F32)],
               compiler_params=_cp(("parallel", "parallel"), 16 << 20))(P, P, P, w8, b)


def dsilu(pre, dxs_list, db_list, dc_list, name):
    T = pre.shape[0]
    n1, n2, n3 = len(dxs_list), len(db_list), len(dc_list)

    def body(*refs):
        pre_ref = refs[0]
        ins = refs[1:1 + n1 + n2 + n3]
        out = refs[-1]

        def part(rs, lo, hi):
            g = rs[0][...].astype(F32)
            for r in rs[1:]:
                g = g + r[...].astype(F32)
            x = pre_ref[:, lo:hi]
            sg = 1.0 / (1.0 + jnp.exp(-x))
            sl = x * sg
            out[:, lo:hi] = g * (sg + sl * (1.0 - sg))

        part(ins[:n1], 0, 512)
        part(ins[n1:n1 + n2], 512, 768)
        part(ins[n1 + n2:], 768, 1024)

    spec = lambda w: pl.BlockSpec((TR, w), lambda i: (i, 0))
    return _pc(body, name=name, grid=(T // TR,),
               in_specs=[spec(1024)] + [spec(512)] * n1 + [spec(256)] * (n2 + n3),
               out_specs=spec(1024), out_shape=_sds((T, 1024), F32),
               compiler_params=_cp(("parallel",), 32 << 20))(pre, *dxs_list, *db_list, *dc_list)


def conv_bwd(dpre, P, w8, nL, name):
    T = P.shape[0]
    nT = T // TR

    def body(dp, dc, dn, xp, xc, xn, w_ref, dx_ref, dw_ref, db_ref, extd, extx):
        i = pl.program_id(1)
        _fill_ext(extd, dp, dc, dn, i, nL, nT)
        _fill_ext(extx, xp, xc, xn, i, nL, nT)

        @pl.when(i == 0)
        def _():
            dw_ref[...] = jnp.zeros_like(dw_ref)
            db_ref[...] = jnp.zeros_like(db_ref)

        d = dc[...]
        dx = jnp.zeros((TR, 256), F32)
        for k in range(S_CONV):
            dx = dx + w_ref[k:k + 1, :] * extd[pl.ds(_HALO + S_CONV // 2 - k, TR), :]
            dw_ref[k:k + 1, :] += jnp.sum(d * extx[pl.ds(_HALO - S_CONV // 2 + k, TR), :], axis=0, keepdims=True)
        dx_ref[...] = dx.astype(dx_ref.dtype)
        db_ref[0:1, :] += jnp.sum(d, axis=0, keepdims=True)

    def swap(spec):
        f = spec.index_map
        return pl.BlockSpec(spec.block_shape, lambda j, i: f(i, j))

    dprv, dcur, dnxt = [swap(s) for s in _halo_specs(T, 0)]
    xprv, xcur, xnxt = [swap(s) for s in _halo_specs(T, C_XBC // 256)]
    acc = pl.BlockSpec((8, 256), lambda j, i: (0, j))
    return _pc(body, name=name, grid=(4, nT),
               in_specs=[dprv, dcur, dnxt, xprv, xcur, xnxt, acc],
               out_specs=[pl.BlockSpec((TR, 256), lambda j, i: (i, j)), acc, acc],
               out_shape=[_sds((T, 1024), BF16), _sds((8, 1024), F32), _sds((8, 1024), F32)],
               scratch_shapes=[pltpu.VMEM((TR + 2 * _HALO, 256), F32), pltpu.VMEM((TR + 2 * _HALO, 256), F32)],
               compiler_params=_cp(("parallel", "arbitrary"), 16 << 20))(dpre, dpre, dpre, P, P, P, w8)


def _onehot_row(h, n):
    return (lax.broadcasted_iota(jnp.int32, (1, n), 1) == h).astype(F32)


def _onehot_col(h, n):
    return (lax.broadcasted_iota(jnp.int32, (n, 1), 0) == h).astype(F32)


def _ssd_chunk(xs, dtr, dtb, alog, bm, cm, hin, reverse):
    Qn = S_Q
    ii = lax.broadcasted_iota(jnp.int32, (Qn, Qn), 0)
    jj = lax.broadcasted_iota(jnp.int32, (Qn, Qn), 1)
    keep = (ii <= jj) if reverse else (ii >= jj)
    tri = keep.astype(F32)
    triT = ((jj <= ii) if reverse else (jj >= ii)).astype(F32)
    eye = (ii == jj).astype(F32)
    dt = _softplus(dtr + dtb)
    a = dt * (-jnp.exp(alog))
    cs = hdot(tri, a)
    csT = hdot(a, triT, "tn")
    dtT = hdot(dt, eye, "tn")
    last = _onehot_row(0 if reverse else Qn - 1, Qn)
    ys, houts = [], []
    for g in range(S_GROUPS):
        G = bdot(cm[g], bm[g], "nt")
        for r in range(S_HEADS // S_GROUPS):
            h = g * (S_HEADS // S_GROUPS) + r
            eh_r, eh_c = _onehot_row(h, S_HEADS), _onehot_col(h, S_HEADS)
            cs_c = jnp.sum(cs * eh_r, axis=1, keepdims=True)
            dt_c = jnp.sum(dt * eh_r, axis=1, keepdims=True)
            cs_r = jnp.sum(csT * eh_c, axis=0, keepdims=True)
            dt_r = jnp.sum(dtT * eh_c, axis=0, keepdims=True)
            tot = jnp.sum(cs_r * last, axis=1, keepdims=True)
            decay = jnp.exp(jnp.where(keep, cs_c - cs_r, NEG))
            w = G * decay * dt_r
            y = bdot(w, xs[h], "nn") + bdot(cm[g], hin[h], "nt") * jnp.exp(cs_c)
            xsc = xs[h] * (jnp.exp(tot - cs_c) * dt_c)
            hout = hin[h] * jnp.exp(tot) + bdot(xsc, bm[g], "tn")
            ys.append(y)
            houts.append(hout)
    return ys, houts


def _ssd_orders(L, Lc):
    nl, ncx = L // S_Q, Lc // S_Q
    fwd = lambda s: jnp.where(s < ncx, nl + s, s - ncx)
    bwd = lambda s: nl + ncx - 1 - s
    return nl + ncx, fwd, bwd


def _ssd_in_specs(fo, bo, step):
    def at(order, w, col):
        return pl.BlockSpec((S_Q, w), lambda u: (order(step(u)), col))
    specs = []
    for order in (fo, bo):
        specs += [at(order, 512, 0), at(order, 256, 2), at(order, 256, 3), at(order, 128, C_DT // 128)]
    return specs


def ssd_fwd(act, P, dtb, alog, L, Lc, name):
    T = L + Lc
    ns, fo, bo = _ssd_orders(L, Lc)

    def body(xf, bf, cf, df, xb, bb, cb, db, dtb_ref, al_ref, yf, yb, hsf, hsb, Hf, Hb):
        s = pl.program_id(0)

        @pl.when(s == 0)
        def _():
            Hf[...] = jnp.zeros_like(Hf)
            Hb[...] = jnp.zeros_like(Hb)

        for d, (x_r, b_r, c_r, dt_r, y_r, hs_r, H) in enumerate(((xf, bf, cf, df, yf, hsf, Hf), (xb, bb, cb, db, yb, hsb, Hb))):
            hin = [H[h] for h in range(S_HEADS)]
            hs_r[0] = H[...]
            ys, houts = _ssd_chunk(
                [x_r[:, h * S_P:(h + 1) * S_P] for h in range(S_HEADS)], dt_r[:, d * 8:(d + 1) * 8],
                dtb_ref[d:d + 1, 0:8], al_ref[d:d + 1, 0:8],
                [b_r[:, g * S_N:(g + 1) * S_N] for g in range(S_GROUPS)], [c_r[:, g * S_N:(g + 1) * S_N] for g in range(S_GROUPS)],
                hin, reverse=(d == 1))
            for h in range(S_HEADS):
                y_r[:, h * S_P:(h + 1) * S_P] = ys[h]
                H[h] = houts[h]

    ident = lambda u: u
    small = pl.BlockSpec((8, 128), lambda u: (0, 0))
    hspec = pl.BlockSpec((1, S_HEADS, S_P, S_N), lambda u: (u, 0, 0, 0))
    return _pc(body, name=name, grid=(ns,),
               in_specs=_ssd_in_specs(fo, bo, ident) + [small, small],
               out_specs=[pl.BlockSpec((S_Q, 512), lambda u: (fo(u), 0)), pl.BlockSpec((S_Q, 512), lambda u: (bo(u), 0)), hspec, hspec],
               out_shape=[_sds((T, 512), F32), _sds((T, 512), F32), _sds((ns, S_HEADS, S_P, S_N), F32), _sds((ns, S_HEADS, S_P, S_N), F32)],
               scratch_shapes=[pltpu.VMEM((S_HEADS, S_P, S_N), F32), pltpu.VMEM((S_HEADS, S_P, S_N), F32)],
               compiler_params=_cp(("arbitrary",), 32 << 20))(act, act, act, P, act, act, act, P, dtb, alog)


def ssd_bwd(act, P, dtb, alog, hsf, hsb, dy, L, Lc, name):
    T = L + Lc
    ns, fo, bo = _ssd_orders(L, Lc)
    step = lambda u: ns - 1 - u

    def body(xf, bf, cf, df, xb, bb, cb, db, dtb_ref, al_ref, hsf_r, hsb_r, dyf, dyb,
             dxf, dbf, dcf, ddf, dxb, dbb, dcb, ddb, ddtb, dal, dHf, dHb):
        u = pl.program_id(0)

        @pl.when(u == 0)
        def _():
            dHf[...] = jnp.zeros_like(dHf)
            dHb[...] = jnp.zeros_like(dHb)
            ddtb[...] = jnp.zeros_like(ddtb)
            dal[...] = jnp.zeros_like(dal)

        dirs = ((xf, bf, cf, df, hsf_r, dyf, dxf, dbf, dcf, ddf, dHf), (xb, bb, cb, db, hsb_r, dyb, dxb, dbb, dcb, ddb, dHb))
        for d, (x_r, b_r, c_r, dt_r, hs_r, dy_r, dx_o, db_o, dc_o, dd_o, dH) in enumerate(dirs):
            f = functools.partial(_ssd_chunk, reverse=(d == 1))
            _, vjp = jax.vjp(
                f, [x_r[:, h * S_P:(h + 1) * S_P] for h in range(S_HEADS)], dt_r[:, d * 8:(d + 1) * 8],
                dtb_ref[d:d + 1, 0:8], al_ref[d:d + 1, 0:8],
                [b_r[:, g * S_N:(g + 1) * S_N] for g in range(S_GROUPS)], [c_r[:, g * S_N:(g + 1) * S_N] for g in range(S_GROUPS)],
                [hs_r[0, h] for h in range(S_HEADS)])
            gx, gdt, gdtb, gal, gb, gc, gh = vjp(([dy_r[:, h * S_P:(h + 1) * S_P] for h in range(S_HEADS)],
                                                  [dH[h] for h in range(S_HEADS)]))
            for h in range(S_HEADS):
                dx_o[:, h * S_P:(h + 1) * S_P] = gx[h]
                dH[h] = gh[h]
            for g in range(S_GROUPS):
                db_o[:, g * S_N:(g + 1) * S_N] = gb[g]
                dc_o[:, g * S_N:(g + 1) * S_N] = gc[g]
            dd_o[...] = gdt
            ddtb[d:d + 1, 0:8] += gdtb
            dal[d:d + 1, 0:8] += gal

    small = pl.BlockSpec((8, 128), lambda u: (0, 0))
    hspec = pl.BlockSpec((1, S_HEADS, S_P, S_N), lambda u: (step(u), 0, 0, 0))
    at = lambda order, w: pl.BlockSpec((S_Q, w), lambda u: (order(step(u)), 0))
    outs = []
    for order in (fo, bo):
        outs += [at(order, 512), at(order, 256), at(order, 256), at(order, 8)]
    oshape = [_sds((T, 512), F32), _sds((T, 256), F32), _sds((T, 256), F32), _sds((T, 8), F32)]
    return _pc(body, name=name, grid=(ns,),
               in_specs=_ssd_in_specs(fo, bo, step) + [small, small, hspec, hspec, at(fo, 512), at(bo, 512)],
               out_specs=outs + [small, small], out_shape=oshape + oshape + [_sds((8, 128), F32), _sds((8, 128), F32)],
               scratch_shapes=[pltpu.VMEM((S_HEADS, S_P, S_N), F32), pltpu.VMEM((S_HEADS, S_P, S_N), F32)],
               compiler_params=_cp(("arbitrary",), 40 << 20))(act, act, act, P, act, act, act, P, dtb, alog, hsf, hsb, dy, dy)


def _ssm_out(yf, yb, xs, z, dskip, g):
    y = (yf + yb + dskip * xs) * _silu(z)
    return (y * lax.rsqrt(jnp.mean(y * y, axis=-1, keepdims=True) + EPS)) * g


def ssm_out_fwd(yf, yb, act, P, dskip, g, name):
    T = yf.shape[0]

    def body(yf_r, yb_r, xs_r, z_r, d_r, g_r, o_r):
        o_r[...] = _ssm_out(yf_r[...], yb_r[...], xs_r[...], z_r[...], d_r[...], g_r[...]).astype(o_r.dtype)

    row = pl.BlockSpec((TR, 512), lambda i: (i, 0))
    vec = pl.BlockSpec((1, 512), lambda i: (0, 0))
    return _pc(body, name=name, grid=(T // TR,),
               in_specs=[row, row, row, pl.BlockSpec((TR, 512), lambda i: (i, C_Z // 512)), vec, vec],
               out_specs=row, out_shape=_sds((T, 512), BF16),
               compiler_params=_cp(("parallel",), 16 << 20))(yf, yb, act, P, dskip, g)


def ssm_out_bwd(yf, yb, act, P, dskip, g, do_src, name):
    T = yf.shape[0]

    def body(yf_r, yb_r, xs_r, z_r, d_r, g_r, do_r, dy_r, dxs_r, dz_r, dv_r):
        @pl.when(pl.program_id(0) == 0)
        def _():
            dv_r[...] = jnp.zeros_like(dv_r)

        _, vjp = jax.vjp(_ssm_out, yf_r[...], yb_r[...], xs_r[...], z_r[...], d_r[...], g_r[...])
        dyf, _, dxs, dz, dd, dg = vjp(do_r[...].astype(F32))
        dy_r[...] = dyf
        dxs_r[...] = dxs
        dz_r[...] = dz.astype(dz_r.dtype)
        dv_r[0:1, :] += dd
        dv_r[1:2, :] += dg

    row = pl.BlockSpec((TR, 512), lambda i: (i, 0))
    vec = pl.BlockSpec((1, 512), lambda i: (0, 0))
    return _pc(body, name=name, grid=(T // TR,),
               in_specs=[row, row, row, pl.BlockSpec((TR, 512), lambda i: (i, C_Z // 512)), vec, vec,
                         pl.BlockSpec((TR, 512), lambda i: (i, 1))],
               out_specs=[row, row, row, pl.BlockSpec((8, 512), lambda i: (0, 0))],
               out_shape=[_sds((T, 512), F32), _sds((T, 512), F32), _sds((T, 512), BF16), _sds((8, 512), F32)],
               compiler_params=_cp(("arbitrary",), 24 << 20))(yf, yb, act, P, dskip, g, do_src)


def add_halves(xv, got, cvec, name):
    n, r, cdim = xv.shape
    h = r // 2

    def body(c_ref, x_ref, g_ref, o_ref):
        o_ref[...] = (x_ref[...].astype(F32) + g_ref[...].astype(F32)).astype(o_ref.dtype)

    gs = pltpu.PrefetchScalarGridSpec(
        num_scalar_prefetch=1, grid=(n,),
        in_specs=[pl.BlockSpec((1, h, cdim), lambda k, c_ref: (k, c_ref[0], 0)), pl.BlockSpec((1, h, cdim), lambda k, c_ref: (k, 0, 0))],
        out_specs=pl.BlockSpec((1, h, cdim), lambda k, c_ref: (k, 0, 0)))
    return _pc(body, name=name, grid_spec=gs, out_shape=_sds((n, h, cdim), BF16),
               compiler_params=_cp(("arbitrary",), 24 << 20))(cvec, xv, got)


def sum_slots(a, name):
    n, r, cdim = a.shape
    tr = _div_tile(r, 512, 16)

    def body(a_ref, o_ref):
        acc = a_ref[0].astype(F32)
        for k in range(1, n):
            acc = acc + a_ref[k].astype(F32)
        o_ref[...] = acc

    return _pc(body, name=name, grid=(r // tr,), in_specs=[pl.BlockSpec((n, tr, cdim), lambda i: (0, i, 0))],
               out_specs=pl.BlockSpec((tr, cdim), lambda i: (i, 0)), out_shape=_sds((r, cdim), F32),
               compiler_params=_cp(("parallel",), 32 << 20))(a)


def adamw(w, g, m, v, name):
    B, R, C = w.shape
    tr = _div_tile(R, max(8, (1 << 19) // max(C, 1) // 8 * 8), 8) if R % 8 == 0 else R
    c1 = 1.0 / (1.0 - ADAM_B1 ** ADAM_STEP)
    c2 = 1.0 / (1.0 - ADAM_B2 ** ADAM_STEP)

    def body(w_ref, g_ref, m_ref, v_ref, d_ref, mo_ref, vo_ref):
        gg = g_ref[...]
        mn = ADAM_B1 * m_ref[...] + (1.0 - ADAM_B1) * gg
        vn = ADAM_B2 * v_ref[...] + (1.0 - ADAM_B2) * (gg * gg)
        d_ref[...] = -ADAM_LR * ((mn * c1) / (jnp.sqrt(vn * c2) + ADAM_EPS) + ADAM_WD * w_ref[...])
        mo_ref[...] = mn
        vo_ref[...] = vn

    spec = pl.BlockSpec((1, tr, C), lambda b, i: (b, i, 0))
    return _pc(body, name=name, grid=(B, R // tr), in_specs=[spec] * 4, out_specs=[spec] * 3,
               out_shape=[_sds((B, R, C), F32)] * 3, compiler_params=_cp(("parallel", "parallel"), 32 << 20))(w, g, m, v)


def _me():
    return lax.axis_index("x"), lax.axis_index("y"), lax.axis_index("c")


def _flip(v, bit):
    return 1 - v if bit else v


def allgather8(xv, name):
    R = xv.shape[0]

    def body(x_ref, out_ref, sum_ref, send_sems, recv_sems):
        mx, my, mc = _me()
        me = 4 * mx + 2 * my + mc
        out_ref[me] = x_ref[...]
        sends, recvs = [], []
        for k in range(1, 8):
            px, py, pc = _flip(mx, k & 4), _flip(my, k & 2), _flip(mc, k & 1)
            peer = 4 * px + 2 * py + pc
            sends.append(pltpu.make_async_remote_copy(src_ref=x_ref, dst_ref=out_ref.at[me], send_sem=send_sems.at[k - 1],
                                                      recv_sem=recv_sems.at[k - 1], device_id=(px, py, pc), device_id_type=MESH))
            recvs.append(pltpu.make_async_remote_copy(src_ref=x_ref, dst_ref=out_ref.at[peer], send_sem=send_sems.at[k - 1],
                                                      recv_sem=recv_sems.at[k - 1], device_id=(px, py, pc), device_id_type=MESH))
        for cp in sends:
            cp.start()
        for cp in recvs:
            cp.wait_recv()
        for cp in sends:
            cp.wait_send()
        acc = out_ref[0]
        for d in range(1, 8):
            acc = acc + out_ref[d]
        sum_ref[...] = acc

    vm = pl.BlockSpec(memory_space=pltpu.VMEM)
    return _pc(body, name=name, pin=False, in_specs=[vm], out_specs=[vm, vm], out_shape=[_sds((8, R, 128), F32), _sds((R, 128), F32)],
               scratch_shapes=[pltpu.SemaphoreType.DMA((7,)), pltpu.SemaphoreType.DMA((7,))],
               compiler_params=_cp(None, 32 << 20))(xv)


def _other_chips(mx, my):
    return [(1 - mx, my), (mx, 1 - my), (1 - mx, 1 - my)]


def _halves(r, mc, mult):
    h = r // 2
    return pl.ds(pl.multiple_of(mc * h, mult), h), pl.ds(pl.multiple_of((1 - mc) * h, mult), h)


def _rcopy(src, dst, send_sems, recv_sems, k, to):
    return pltpu.make_async_remote_copy(src_ref=src, dst_ref=dst, send_sem=send_sems.at[k], recv_sem=recv_sems.at[k],
                                        device_id=to, device_id_type=MESH)


def _gather_body(xs, outs, send_sems, recv_sems):
    n = len(xs)
    mx, my, mc = _me()
    chip = 2 * mx + my
    sib = (mx, my, 1 - mc)
    chips = _other_chips(mx, my)
    idx = [2 * cx + cy for cx, cy in chips]
    cp = functools.partial(_rcopy, send_sems=send_sems, recv_sems=recv_sems)
    hv = [_halves(x.shape[0], mc, 16) for x in xs]
    first, passed = [], []
    for a in range(n):
        for j, (cx, cy) in enumerate(chips):
            first.append(cp(xs[a].at[hv[a][0]], outs[a].at[chip, hv[a][0]], k=6 * a + j, to=(cx, cy, mc)))
            first[-1].start()
    for a in range(n):
        for j in range(3):
            cp(xs[a].at[hv[a][0]], outs[a].at[idx[j], hv[a][0]], k=6 * a + j, to=sib).wait_recv()
            passed.append(cp(outs[a].at[idx[j], hv[a][0]], outs[a].at[idx[j], hv[a][0]], k=6 * a + 3 + j, to=sib))
            passed[-1].start()
    for a in range(n):
        for j in range(3):
            cp(xs[a].at[hv[a][1]], outs[a].at[idx[j], hv[a][1]], k=6 * a + 3 + j, to=sib).wait_recv()
    for c_ in first + passed:
        c_.wait_send()


def _my_chip():
    return 2 * lax.axis_index("x") + lax.axis_index("y")


def _own_slots(outs, shards):
    return [lax.dynamic_update_index_in_dim(o, x, _my_chip(), 0) for o, x in zip(outs, shards)]


def gather_weights(shards, name):
    n = len(shards)

    def body(*refs):
        _gather_body(refs[:n], refs[n:2 * n], *refs[2 * n:])

    hbm = pl.BlockSpec(memory_space=pl.ANY)
    outs = _pc(body, name=name, in_specs=[hbm] * n, out_specs=[hbm] * n, out_shape=[_sds((4,) + x.shape, x.dtype) for x in shards],
               scratch_shapes=[pltpu.SemaphoreType.DMA((6 * n,)), pltpu.SemaphoreType.DMA((6 * n,))])(*shards)
    return _own_slots(outs, shards)


GATHER_REST_ID = 3


def gather_weights_sc(shards, name):
    n = len(shards)
    x_refs = [jax.new_ref(x, memory_space=pltpu.MemorySpace.HBM) for x in shards]
    out_refs = [jax.empty_ref(_sds((4,) + x.shape, x.dtype), memory_space=pltpu.MemorySpace.HBM) for x in shards]

    @pl.kernel(mesh=plsc.ScalarSubcoreMesh(axis_name="sc", num_cores=1), name=name,
               scratch_types=(pltpu.SemaphoreType.DMA((6 * n,)), pltpu.SemaphoreType.DMA((6 * n,))),
               compiler_params=pltpu.CompilerParams(collective_id=GATHER_REST_ID))
    def launch(send_sems, recv_sems):
        mx, my, mc = _me()
        barrier = pltpu.get_barrier_semaphore()
        for peer in [(mx, my, 1 - mc)] + [(cx, cy, mc) for cx, cy in _other_chips(mx, my)]:
            pl.semaphore_signal(barrier, inc=1, device_id=peer, device_id_type=MESH)
        pl.semaphore_wait(barrier, 4)
        _gather_body(x_refs, out_refs, send_sems, recv_sems)

    launch()
    return _own_slots([o[...] for o in out_refs], shards)


def swap_halves(arrs, name):
    n = len(arrs)

    def body(*refs):
        xs, outs = refs[:n], refs[n:2 * n]
        send_sems, recv_sems = refs[2 * n:]
        mx, my, mc = _me()
        cps = []
        for a in range(n):
            theirs = _halves(xs[a].shape[1], mc, 16)[1]
            cps.append(_rcopy(xs[a].at[pl.ds(0, 4), theirs], outs[a], send_sems, recv_sems, a, (mx, my, 1 - mc)))
            cps[-1].start()
        for c_ in cps:
            c_.wait()

    hbm = pl.BlockSpec(memory_space=pl.ANY)
    return _pc(body, name=name, in_specs=[hbm] * n, out_specs=[hbm] * n,
               out_shape=[_sds((4, x.shape[1] // 2, x.shape[2]), x.dtype) for x in arrs],
               scratch_shapes=[pltpu.SemaphoreType.DMA((n,)), pltpu.SemaphoreType.DMA((n,))])(*arrs)


SCATTER_ID = 4


def scatter_chips_sc(arrs, name):
    n = len(arrs)
    x_refs = [jax.new_ref(x, memory_space=pltpu.MemorySpace.HBM) for x in arrs]
    out_refs = [jax.empty_ref(_sds(x.shape, x.dtype), memory_space=pltpu.MemorySpace.HBM) for x in arrs]

    @pl.kernel(mesh=plsc.ScalarSubcoreMesh(axis_name="sc", num_cores=1), name=name,
               scratch_types=(pltpu.SemaphoreType.DMA((3 * n,)), pltpu.SemaphoreType.DMA((3 * n,))),
               compiler_params=pltpu.CompilerParams(collective_id=SCATTER_ID))
    def launch(send_sems, recv_sems):
        mx, my, mc = _me()
        chip = 2 * mx + my
        chips = _other_chips(mx, my)
        idx = [2 * cx + cy for cx, cy in chips]
        barrier = pltpu.get_barrier_semaphore()
        for cx, cy in chips:
            pl.semaphore_signal(barrier, inc=1, device_id=(cx, cy, mc), device_id_type=MESH)
        pl.semaphore_wait(barrier, 3)
        cp = functools.partial(_rcopy, send_sems=send_sems, recv_sems=recv_sems)
        sends = []
        for a in range(n):
            for j, (cx, cy) in enumerate(chips):
                sends.append(cp(x_refs[a].at[idx[j]], out_refs[a].at[chip], k=3 * a + j, to=(cx, cy, mc)))
                sends[-1].start()
        for a in range(n):
            for j, (cx, cy) in enumerate(chips):
                cp(x_refs[a].at[idx[j]], out_refs[a].at[idx[j]], k=3 * a + j, to=(cx, cy, mc)).wait_recv()
        for c_ in sends:
            c_.wait_send()

    launch()
    return _own_slots([o[...] for o in out_refs], [lax.dynamic_index_in_dim(x, _my_chip(), 0, keepdims=False) for x in arrs])


def share_halves(parts, name):
    flat = [p for w in parts for p in w]
    nw, n = len(parts), len(flat)
    depth = n // nw

    def body(*refs):
        xs, outs = refs[:n], refs[n:n + nw]
        send_sems, recv_sems = refs[n + nw:]
        mx, my, mc = _me()
        sib = (mx, my, 1 - mc)
        sends, recvs = [], []
        for a in range(n):
            w, l = a // depth, a % depth
            mine, theirs = _halves(outs[w].shape[1], mc, 8)
            sends.append(_rcopy(xs[a], outs[w].at[l, mine], send_sems, recv_sems, a, sib))
            recvs.append(_rcopy(xs[a], outs[w].at[l, theirs], send_sems, recv_sems, a, sib))
            sends[-1].start()
        for c_ in recvs:
            c_.wait_recv()
        for c_ in sends:
            c_.wait_send()

    hbm = pl.BlockSpec(memory_space=pl.ANY)
    outs = _pc(body, name=name, in_specs=[hbm] * n, out_specs=[hbm] * nw,
               out_shape=[_sds((depth, 2 * w[0].shape[0], w[0].shape[1]), F32) for w in parts],
               scratch_shapes=[pltpu.SemaphoreType.DMA((n,)), pltpu.SemaphoreType.DMA((n,))])(*flat)
    outs = list(outs)
    mc = lax.axis_index("c")
    for w in range(nw):
        for l in range(depth):
            h = parts[w][l].shape[0]
            outs[w] = lax.dynamic_update_slice(outs[w], parts[w][l][None], (l, mc * h, 0))
    return outs


_BIG = ("w_in", "w_out", "w_ffn_in", "w_ffn_out")
N_CHIPS = 4
DEPTH = 2


def _pad_rows(v, mult=8):
    n = v.shape[0]
    rows = -(-n // 128)
    rows = -(-rows // mult) * mult
    return jnp.pad(v, (0, rows * 128 - n)).reshape(rows, 128)


class _Flat:
    def __init__(self):
        self.items = []

    def add(self, name, a):
        self.items.append((name, a.shape, a.reshape(-1).astype(F32)))

    def rows(self):
        return _pad_rows(jnp.concatenate([a for _, _, a in self.items]))

    def split(self, rows):
        flat = rows.reshape(-1)
        out, o = {}, 0
        for name, shape, a in self.items:
            out[name] = flat[o:o + a.shape[0]].reshape(shape)
            o += a.shape[0]
        return out

    def split_lead(self, rows3):
        n = rows3.shape[0]
        flat = rows3.reshape(n, -1)
        out, o = {}, 0
        for name, shape, a in self.items:
            out[name] = flat[:, o:o + a.shape[0]].reshape((n,) + tuple(shape))
            o += a.shape[0]
        return out


def _gsv(rows):
    z = jnp.zeros((2, D), F32)
    r = [z if a is None else a for a in rows] + [z] * 5
    return jnp.stack(r, axis=1)


def _pad8(a, rows=8, cols=128):
    return jnp.zeros((rows, cols), F32).at[:a.shape[0], :a.shape[1]].set(a.astype(F32))


def kernel(x, c, ctx, c_ctx, w_mod, b_mod, g_mix, w_in, wa_sink, na_rpb, ssm_conv_w, ssm_conv_b, ssm_dt_bias, ssm_a_log, ssm_d, ssm_norm_g, w_out, g_ffn, w_ffn_in, w_ffn_out, g_final, loss_target, m_c_ctx, m_w_mod, m_b_mod, m_g_mix, m_w_in, m_wa_sink, m_na_rpb, m_ssm_conv_w, m_ssm_conv_b, m_ssm_dt_bias, m_ssm_a_log, m_ssm_d, m_ssm_norm_g, m_w_out, m_g_ffn, m_w_ffn_in, m_w_ffn_out, m_g_final, v_c_ctx, v_w_mod, v_b_mod, v_g_mix, v_w_in, v_wa_sink, v_na_rpb, v_ssm_conv_w, v_ssm_conv_b, v_ssm_dt_bias, v_ssm_a_log, v_ssm_d, v_ssm_norm_g, v_w_out, v_g_ffn, v_w_ffn_in, v_w_ffn_out, v_g_final):
    L, Lc = x.shape[1], ctx.shape[1]
    T = L + Lc
    nL = L // TR
    mx, my, mc = lax.axis_index("x"), lax.axis_index("y"), lax.axis_index("c")
    dev = 4 * mx + 2 * my + mc
    chip = 2 * mx + my
    MODW = 6 * D // N_CHIPS
    CW = 1024 // N_CHIPS

    sc = _silu(c.astype(F32))
    scc = _silu(c_ctx.astype(F32))[None]
    f1 = _Flat()
    f1.add("sc", sc)
    f1.add("conv_w", ssm_conv_w)
    g1, _ = allgather8(f1.rows(), "gather_cond")
    g1 = f1.split_lead(g1)
    sc_all = g1["sc"][:, 0]
    conv_w = jnp.concatenate([g1["conv_w"][2 * k] for k in range(N_CHIPS)], axis=-1)
    A16 = jnp.concatenate([sc_all, scc, jnp.zeros((7, D), F32)], axis=0)

    mod_part = matmul_layers(A16, w_mod, "nn", "mod_fwd")
    f2 = _Flat()
    f2.add("mod", mod_part)
    g2, _ = allgather8(f2.rows(), "gather_mod")
    g2 = f2.split_lead(g2)["mod"]
    mods = jnp.concatenate([g2[2 * k] for k in range(N_CHIPS)], axis=-1) + b_mod[:, None, :]
    mod_l = lax.dynamic_index_in_dim(mods, dev, axis=1, keepdims=False).reshape(DEPTH, 6, D)
    mod_c = mods[:, 8].reshape(DEPTH, 6, D)
    mod = jnp.stack([mod_l, mod_c], axis=1)
    mrow = lambda l, j: mod[l, :, j]

    own = {"w_in": w_in, "w_out": w_out, "w_ffn_in": w_ffn_in, "w_ffn_out": w_ffn_out}
    sh16 = [own[n][l].astype(BF16) for n in _BIG for l in range(DEPTH)]
    gath = list(gather_weights(sh16[:1], "gather_first"))
    after_first = (gath[0][0, 0, 0] * 0).astype(BF16)
    gath += list(gather_weights_sc([sh16[1] + after_first] + sh16[2:], "gather_rest"))
    gw = {n: [gath[DEPTH * i + l] for l in range(DEPTH)] for i, n in enumerate(_BIG)}
    W_in = [jnp.pad(jnp.concatenate([g[k] for k in range(N_CHIPS)], axis=1), ((0, 0), (0, IN_PAD - IN_COLS))) for g in gw["w_in"]]
    W_out = [g.reshape(D, D) for g in gw["w_out"]]
    W_fo = [g.reshape(D_FF, D) for g in gw["w_ffn_out"]]
    W_fi = gw["w_ffn_in"]

    cos, sin, rotm = rope_tables(L, Lc)
    x0 = jnp.concatenate([x[0], ctx[0]], axis=0).astype(F32)

    sv = []
    xin = x0
    gsv_first = _gsv([None, mrow(0, 0), mrow(0, 1)])
    _, h1 = res_norm_mod(x0, None, gsv_first, g_mix[0][None], nL, "norm_first")
    for l in range(DEPTH):
        s = {"xin": xin, "h1": h1}
        P = matmul(h1, W_in[l], "nn", F32, f"in_proj{l}", tn=IN_PAD)
        qr, kr, kb, vb = rope_apply(P, C_QA // 256, P, C_KA // 128, cos, sin, rotm, False, f"rope{l}", kv_src=P)
        sink8 = _pad8(jnp.broadcast_to(wa_sink[l][:, None], (WA_HEADS, 128)))
        oa, sta = win_attn_fwd(qr, kr, P, sink8, L, Lc, f"wa_fwd{l}")
        bias = na_bias_table(na_rpb[l], l)
        ob, stb = na_fwd(P, kb, vb, bias, L, Lc, f"na_fwd{l}")
        w8 = jnp.concatenate([conv_w[l], jnp.zeros((1, 1024), F32)], axis=0)
        pre, act = conv_silu_fwd(P, w8, ssm_conv_b[l][None], nL, f"conv_fwd{l}")
        dtb8, al8 = _pad8(ssm_dt_bias[l]), _pad8(ssm_a_log[l])
        yf, yb, hsf, hsb = ssd_fwd(act, P, dtb8, al8, L, Lc, f"ssd_fwd{l}")
        dskip = jnp.repeat(ssm_d[l], S_P)[None]
        oc = ssm_out_fwd(yf, yb, act, P, dskip, ssm_norm_g[l][None], f"ssm_out_fwd{l}")
        mixin = [(oa, 0), (ob, 256), (oc, 512)]
        mix = out_proj_fwd(mixin, W_out[l], f"out_proj{l}")
        gsv_mid = _gsv([mrow(l, 2), mrow(l, 3), mrow(l, 4)])
        x1, h2 = res_norm_mod(xin, mix, gsv_mid, g_ffn[l][None], nL, f"norm_mid{l}")
        gu = matmul_fi(h2, W_fi[l], "nn", BF16, f"ffn_in{l}")
        af = swiglu_fwd(gu, f"swiglu_fwd{l}")
        fo = matmul(af, W_fo[l], "nn", BF16, f"ffn_out{l}", tk=D_FF)
        s.update(P=P, qr=qr, kr=kr, sink8=sink8, oa=oa, sta=sta, ob=ob, stb=stb, kb=kb, vb=vb, bias=bias, w8=w8, pre=pre, act=act, dtb8=dtb8, al8=al8, yf=yf,
                 yb=yb, hsf=hsf, hsb=hsb, dskip=dskip, mixin=mixin, mix=mix, gsv_mid=gsv_mid, x1=x1, h2=h2, gu=gu, af=af, fo=fo)
        if l + 1 < DEPTH:
            s["gsv_end"] = _gsv([mrow(l, 5), mrow(l + 1, 0), mrow(l + 1, 1)])
            xin, h1 = res_norm_mod(x1, fo, s["gsv_end"], g_mix[l + 1][None], nL, f"norm_end{l}")
        else:
            s["gsv_end"] = _gsv([mrow(l, 5), None, None])
        sv.append(s)

    last = sv[-1]
    loss8, dres, dfo, dgsv_end, dg_final = final_loss(last["x1"], last["fo"], last["gsv_end"], g_final[None], loss_target[0].astype(F32), nL, "final_loss")
    loss = lax.psum(loss8[0, 0], ("x", "y", "c"))

    dmod = [[None] * 6 for _ in range(DEPTH)]
    gW = {n: [None] * DEPTH for n in _BIG}
    small = [dict() for _ in range(DEPTH)]
    parts = [None] * DEPTH
    cvec = mc.astype(jnp.int32).reshape(1)
    grad_x = None
    for l in reversed(range(DEPTH)):
        s = sv[l]
        dmod[l][5] = dgsv_end[:, 0]
        if l + 1 < DEPTH:
            dmod[l + 1][0], dmod[l + 1][1] = dgsv_end[:, 1], dgsv_end[:, 2]
        daf = matmul(dfo, W_fo[l], "nt", BF16, f"ffn_out_dx{l}")
        gW["w_ffn_out"][l] = matmul(s["af"], dfo, "tn", BF16, f"ffn_out_dw{l}", tm=1408, tk=T).reshape(N_CHIPS, D_FF // N_CHIPS, D)
        dgu = swiglu_bwd(s["gu"], daf, f"swiglu_bwd{l}")
        dh2 = matmul_fi(dgu, W_fi[l], "nt", BF16, f"ffn_in_dx{l}")
        gW["w_ffn_in"][l] = matmul_fi(s["h2"], dgu, "tn", BF16, f"ffn_in_dw{l}")
        dres, dmix, dgsv_mid, dg_ffn = res_norm_mod_bwd(s["x1"], s["mix"], s["gsv_mid"], g_ffn[l][None], dh2, dres, nL, f"norm_mid_bwd{l}")
        dmod[l][2], dmod[l][3], dmod[l][4] = dgsv_mid[:, 0], dgsv_mid[:, 1], dgsv_mid[:, 2]
        dmixin = matmul(dmix, W_out[l], "nt", BF16, f"out_proj_dx{l}")
        gW["w_out"][l] = out_proj_dw(s["mixin"], dmix, f"out_proj_dw{l}").reshape(N_CHIPS, D // N_CHIPS, D)
        P = s["P"]
        dqr, dkr, dva, dsink = win_attn_bwd(s["qr"], s["kr"], P, s["sink8"], dmixin, s["oa"], s["sta"], L, Lc, f"wa_bwd{l}")
        dqa, dka = rope_apply(dqr, 0, dkr[WA_BLK:WA_BLK + T], 0, cos, sin, rotm, True, f"rope_bwd{l}")
        dqb, dkb, dvb, dbias = na_bwd(P, s["kb"], s["vb"], s["bias"], dmixin, s["ob"], s["stb"], L, Lc, f"na_bwd{l}")
        dy, dxs1, dz, dvec = ssm_out_bwd(s["yf"], s["yb"], s["act"], P, s["dskip"], ssm_norm_g[l][None], dmixin, f"ssm_out_bwd{l}")
        dxf, dbf, dcf, ddf, dxb, dbb, dcb, ddb, ddtb, dal = ssd_bwd(s["act"], P, s["dtb8"], s["al8"], s["hsf"], s["hsb"], dy, L, Lc, f"ssd_bwd{l}")
        dpre = dsilu(s["pre"], [dxf, dxb, dxs1], [dbf, dbb], [dcf, dcb], f"dsilu{l}")
        dxbc, dw8, db8 = conv_bwd(dpre, P, s["w8"], nL, f"conv_bwd{l}")
        ddt = jnp.concatenate([ddf, ddb, jnp.zeros((T, IN_PAD - IN_COLS), F32)], axis=1)
        pieces = [(dqa, C_QA), (dqb, C_QB), (dz, C_Z), (dka, C_KA), (dva[WA_BLK:WA_BLK + T], C_VA), (dkb, C_KB), (dvb, C_VB),
                  (dxbc, C_XBC), (ddt, C_DT)]
        dh1, dwin = in_proj_bwd(pieces, s["h1"], W_in[l], f"in_proj_bwd{l}")
        cw = IN_COLS // N_CHIPS
        gW["w_in"][l] = jnp.stack([dwin[:, k * cw:(k + 1) * cw] for k in range(N_CHIPS)])
        garr = [gW[n][l] for n in _BIG]
        got = swap_halves(garr, f"reduce_d2d{l}")
        chip_sum = [add_halves(garr[a], got[a], cvec, f"reduce_add_pair{l}_{a}") for a in range(len(garr))]
        parts[l] = scatter_chips_sc(chip_sum, f"reduce_ici{l}")
        small[l] = dict(g_ffn=dg_ffn[0], wa_sink=dsink[:WA_HEADS, 0], na_rpb=na_rpb_grad(dbias, l), conv_w=dw8[:S_CONV], conv_b=db8[0],
                        dt_bias=ddtb[:2, :8], a_log=dal[:2, :8], ssm_d=dvec[0].reshape(S_HEADS, S_P).sum(axis=1), norm_g=dvec[1])
        if l > 0:
            p = sv[l - 1]
            dres, dfo, dgsv_end, dg_mix = res_norm_mod_bwd(s["xin"], p["fo"], p["gsv_end"], g_mix[l][None], dh1, dres, nL, f"norm_end_bwd{l - 1}")
        else:
            grad_x, _, dgsv_first, dg_mix = res_norm_mod_bwd(s["xin"], None, gsv_first, g_mix[0][None], dh1, dres, nL, "norm_first_bwd")
            dmod[0][0], dmod[0][1] = dgsv_first[:, 1], dgsv_first[:, 2]
        small[l]["g_mix"] = dg_mix[0]
    for l in range(DEPTH):
        for j in range(6):
            if dmod[l][j] is None:
                dmod[l][j] = jnp.zeros((2, D), F32)
    dmod = jnp.stack([jnp.stack(r, axis=1) for r in dmod])

    f3 = _Flat()
    f3.add("dmod_l", dmod[:, 0].reshape(DEPTH, 6 * D))
    f3.add("dmod_c", dmod[:, 1].reshape(DEPTH, 6 * D))
    f3.add("g_final", dg_final[0])
    for n in ("g_mix", "g_ffn", "wa_sink", "na_rpb", "conv_w", "conv_b", "dt_bias", "a_log", "ssm_d", "norm_g"):
        f3.add(n, jnp.stack([small[l][n] for l in range(DEPTH)]))
    g3, s3 = allgather8(f3.rows(), "reduce_small")
    dmod_all = f3.split_lead(g3)["dmod_l"]
    s3 = f3.split(s3)
    dmodc_tot = s3["dmod_c"]
    col0 = chip * MODW
    G16, G16c = [], []
    for l in range(DEPTH):
        rows = jnp.concatenate([dmod_all[:, l], dmodc_tot[l][None], jnp.zeros((7, 6 * D), F32)], axis=0)
        G16.append(lax.dynamic_slice_in_dim(rows, col0, MODW, axis=1))
        rc = jnp.concatenate([dmodc_tot[l][None], jnp.zeros((15, 6 * D), F32)], axis=0)
        G16c.append(lax.dynamic_slice_in_dim(rc, col0, MODW, axis=1))
    grad_w_mod = matmul_layers(A16, jnp.stack(G16), "tn", "mod_dw")
    dscc_part = matmul_layers(jnp.stack(G16c), w_mod, "nt", "mod_dx")[:, 0].sum(axis=0)
    _, s4 = allgather8(_pad_rows(dscc_part * (mc == 1).astype(F32)), "reduce_cctx")
    dscc = s4.reshape(-1)[:D]
    cc = c_ctx.astype(F32)
    sg = 1.0 / (1.0 + jnp.exp(-cc))
    grad_c_ctx = dscc * (sg * (1.0 + cc * (1.0 - sg)))

    halves = [[sum_slots(parts[l][i], f"reduce_add_chips{l}_{i}") for l in range(DEPTH)] for i in range(len(_BIG))]
    gsh = dict(zip(_BIG, share_halves(halves, "reduce_share")))

    grads = {"c_ctx": grad_c_ctx, "w_mod": grad_w_mod, "b_mod": s3["dmod_l"] + s3["dmod_c"], "g_mix": s3["g_mix"], "w_in": gsh["w_in"],
             "wa_sink": s3["wa_sink"], "na_rpb": s3["na_rpb"],
             "ssm_conv_w": lax.dynamic_slice_in_dim(s3["conv_w"], chip * CW, CW, axis=2), "ssm_conv_b": s3["conv_b"],
             "ssm_dt_bias": s3["dt_bias"], "ssm_a_log": s3["a_log"], "ssm_d": s3["ssm_d"], "ssm_norm_g": s3["norm_g"],
             "w_out": gsh["w_out"], "g_ffn": s3["g_ffn"], "w_ffn_in": gsh["w_ffn_in"], "w_ffn_out": gsh["w_ffn_out"], "g_final": s3["g_final"]}
    wts = {"c_ctx": c_ctx, "w_mod": w_mod, "b_mod": b_mod, "g_mix": g_mix, "w_in": w_in, "wa_sink": wa_sink, "na_rpb": na_rpb,
           "ssm_conv_w": ssm_conv_w, "ssm_conv_b": ssm_conv_b, "ssm_dt_bias": ssm_dt_bias, "ssm_a_log": ssm_a_log, "ssm_d": ssm_d,
           "ssm_norm_g": ssm_norm_g, "w_out": w_out, "g_ffn": g_ffn, "w_ffn_in": w_ffn_in, "w_ffn_out": w_ffn_out, "g_final": g_final}
    ms = {"c_ctx": m_c_ctx, "w_mod": m_w_mod, "b_mod": m_b_mod, "g_mix": m_g_mix, "w_in": m_w_in, "wa_sink": m_wa_sink, "na_rpb": m_na_rpb,
          "ssm_conv_w": m_ssm_conv_w, "ssm_conv_b": m_ssm_conv_b, "ssm_dt_bias": m_ssm_dt_bias, "ssm_a_log": m_ssm_a_log, "ssm_d": m_ssm_d,
          "ssm_norm_g": m_ssm_norm_g, "w_out": m_w_out, "g_ffn": m_g_ffn, "w_ffn_in": m_w_ffn_in, "w_ffn_out": m_w_ffn_out, "g_final": m_g_final}
    vs = {"c_ctx": v_c_ctx, "w_mod": v_w_mod, "b_mod": v_b_mod, "g_mix": v_g_mix, "w_in": v_w_in, "wa_sink": v_wa_sink, "na_rpb": v_na_rpb,
          "ssm_conv_w": v_ssm_conv_w, "ssm_conv_b": v_ssm_conv_b, "ssm_dt_bias": v_ssm_dt_bias, "ssm_a_log": v_ssm_a_log, "ssm_d": v_ssm_d,
          "ssm_norm_g": v_ssm_norm_g, "w_out": v_w_out, "g_ffn": v_g_ffn, "w_ffn_in": v_w_ffn_in, "w_ffn_out": v_w_ffn_out, "g_final": v_g_final}
    names = list(wts)
    grads = {n: grads[n].reshape(wts[n].shape).astype(F32) for n in names}
    big = ("w_mod", "w_in", "w_out", "w_ffn_in", "w_ffn_out")
    delta, new_m, new_v = {}, {}, {}
    for n in big:
        delta[n], new_m[n], new_v[n] = adamw(wts[n], grads[n], ms[n], vs[n], f"adamw_{n}")
    packs = []
    for src in (wts, grads, ms, vs):
        f = _Flat()
        for n in names:
            if n not in big:
                f.add(n, src[n])
        packs.append(f)
    d_, m_, v_ = adamw(*[f.rows()[None] for f in packs], "adamw_small")
    for dst, rows in ((delta, d_), (new_m, m_), (new_v, v_)):
        dst.update(packs[0].split(rows[0]))

    return (loss, grad_x[:L][None], *[grads[n] for n in names], *[delta[n] for n in names],
            *[new_m[n] for n in names], *[new_v[n] for n in names])
```

```python
import functools

import numpy as np
import jax
import jax.numpy as jnp
from jax import lax
from jax.experimental import pallas as pl
from jax.experimental.pallas import tpu as pltpu
from jax.experimental.pallas import tpu_sc as plsc

F32 = jnp.float32
BF16 = jnp.bfloat16
_MXU = jnp.bfloat16
_HI = lax.Precision.HIGHEST
MESH = pl.DeviceIdType.MESH

D = 1024
HD = 64
GRID_W = 64
EPS = 1e-6
ROPE_BASE = 10000.0
WA_HEADS, WA_KV = 4, 2
WA_BLK = 128
NA_HEADS, NA_KH, NA_KW = 4, 8, 16
S_HEADS, S_P, S_INNER, S_GROUPS, S_N, S_CONV, S_Q = 8, 64, 512, 2, 128, 7, 128
D_FF = 2816
IN_COLS = 2832
IN_PAD = 2944
C_QA, C_QB, C_Z, C_KA, C_VA, C_KB, C_VB, C_XBC, C_DT = 0, 256, 512, 1024, 1152, 1280, 1536, 1792, 2816
ADAM_LR, ADAM_B1, ADAM_B2, ADAM_EPS, ADAM_WD, ADAM_STEP = 0.001, 0.9, 0.999, 1e-08, 0.01, 10

TR = 256
NEG = -1e30
VMEM_CAP = 56 * 1024 * 1024


PIN_BYTES = 256 * 1024


def _is_big(a):
    return hasattr(a, "shape") and len(a.shape) >= 2 and int(np.prod(a.shape)) * jnp.dtype(a.dtype).itemsize >= PIN_BYTES


def _pc(body, *, out_shape, pin=True, **kw):
    if not pin:
        return pl.pallas_call(body, out_shape=out_shape, **kw)
    one = isinstance(out_shape, jax.ShapeDtypeStruct)
    outs = [pltpu.HBM(s.shape, s.dtype) if _is_big(s) else s for s in ([out_shape] if one else out_shape)]
    call = pl.pallas_call(body, out_shape=outs[0] if one else outs, **kw)
    return lambda *args: call(*[pltpu.with_memory_space_constraint(a, pltpu.HBM) if _is_big(a) else a for a in args])


def _cp(sem=None, vmem=None):
    kw = {}
    if sem is not None:
        kw["dimension_semantics"] = sem
    if vmem is not None:
        kw["vmem_limit_bytes"] = int(min(max(vmem, 16 * 1024 * 1024), VMEM_CAP))
    return pltpu.CompilerParams(**kw)


def _sds(shape, dtype):
    return jax.ShapeDtypeStruct(tuple(shape), dtype)


_DIMS = {"nn": ((1,), (0,)), "nt": ((1,), (1,)), "tn": ((0,), (0,))}


def _dg(a, b, dims):
    return lax.dot_general(a.astype(_MXU), b.astype(_MXU), (dims, ((), ())), preferred_element_type=F32)


@functools.partial(jax.custom_vjp, nondiff_argnums=(2,))
def bdot(a, b, mode):
    return _dg(a, b, _DIMS[mode])


def _bdot_fwd(a, b, mode):
    return bdot(a, b, mode), (a, b)


def _bdot_bwd(mode, res, g):
    a, b = res
    if mode == "nn":
        return bdot(g, b, "nt"), bdot(a, g, "tn")
    if mode == "nt":
        return bdot(g, b, "nn"), bdot(g, a, "tn")
    return bdot(b, g, "nt"), bdot(a, g, "nn")


bdot.defvjp(_bdot_fwd, _bdot_bwd)


def hdot(a, b, mode="nn"):
    return lax.dot_general(a, b, (_DIMS[mode], ((), ())), precision=_HI, preferred_element_type=F32)


def _silu(x):
    return x / (1.0 + jnp.exp(-x))


def _softplus(x):
    return jnp.maximum(x, 0.0) + jnp.log(1.0 + jnp.exp(-jnp.abs(x)))


def _div_tile(n, cap, mult):
    if n <= cap:
        return n
    best = None
    for t in range(mult, cap + 1, mult):
        if n % t == 0:
            best = t
    assert best is not None, (n, cap, mult)
    return best


def matmul(a, b, mode, out_dtype, name, tm=640, tn=1536, tk=1408, hi=False):
    if mode == "tn":
        K, M = a.shape
    else:
        M, K = a.shape
    N = b.shape[0] if mode == "nt" else b.shape[1]
    tm = _div_tile(M, tm, 128 if mode == "tn" else 16)
    tn = _div_tile(N, tn, 128)
    tk = _div_tile(K, tk, 128 if mode != "tn" else 16)
    nk = K // tk
    dims = _DIMS[mode]

    def body(a_ref, b_ref, o_ref, *acc):
        if hi:
            part = lax.dot_general(a_ref[...], b_ref[...], (dims, ((), ())), precision=_HI, preferred_element_type=F32)
        else:
            part = _dg(a_ref[...], b_ref[...], dims)
        if nk == 1:
            o_ref[...] = part.astype(o_ref.dtype)
        else:
            k = pl.program_id(2)

            @pl.when(k == 0)
            def _():
                acc[0][...] = part

            @pl.when(k > 0)
            def _():
                acc[0][...] += part

            @pl.when(k == nk - 1)
            def _():
                o_ref[...] = acc[0][...].astype(o_ref.dtype)

    if mode == "tn":
        a_spec = pl.BlockSpec((tk, tm), lambda i, j, k: (k, i))
    else:
        a_spec = pl.BlockSpec((tm, tk), lambda i, j, k: (i, k))
    if mode == "nt":
        b_spec = pl.BlockSpec((tn, tk), lambda i, j, k: (j, k))
    else:
        b_spec = pl.BlockSpec((tk, tn), lambda i, j, k: (k, j))
    isz = lambda x: jnp.dtype(x.dtype).itemsize
    vmem = 2 * (tm * tk * isz(a) + tk * tn * isz(b) + tm * tn * jnp.dtype(out_dtype).itemsize) + 3 * tm * tn * 4
    return _pc(
        body, name=name, grid=(M // tm, N // tn, nk),
        in_specs=[a_spec, b_spec], out_specs=pl.BlockSpec((tm, tn), lambda i, j, k: (i, j)),
        out_shape=_sds((M, N), out_dtype),
        scratch_shapes=[pltpu.VMEM((tm, tn), F32)] if nk > 1 else [],
        compiler_params=_cp(("parallel", "parallel", "arbitrary"), vmem + (8 << 20)),
    )(a, b)


def matmul_layers(a, b, mode, name):
    nl = b.shape[0]
    a3 = a if a.ndim == 3 else a[None]
    shared = a3.shape[0] == 1
    M = a3.shape[2] if mode == "tn" else a3.shape[1]
    N = b.shape[1] if mode == "nt" else b.shape[2]

    def body(a_ref, b_ref, o_ref):
        o_ref[0] = _dg(a_ref[0], b_ref[0], _DIMS[mode])

    return _pc(body, name=name, grid=(nl,),
               in_specs=[pl.BlockSpec((1,) + a3.shape[1:], (lambda l: (0, 0, 0)) if shared else (lambda l: (l, 0, 0))),
                         pl.BlockSpec((1,) + b.shape[1:], lambda l: (l, 0, 0))],
               out_specs=pl.BlockSpec((1, M, N), lambda l: (l, 0, 0)), out_shape=_sds((nl, M, N), F32),
               compiler_params=_cp(("parallel",), 48 << 20))(a3, b)


def out_proj_fwd(pieces, w, name):
    T = pieces[0][0].shape[0]
    arrs, offs = [a for a, _ in pieces], [o for _, o in pieces]
    n = len(arrs)
    tm = _div_tile(T, 640, 16)

    def body(*refs):
        w_ref, o_ref = refs[n], refs[n + 1]
        acc = None
        for j in range(n):
            part = _dg(refs[j][...], w_ref[offs[j]:offs[j] + arrs[j].shape[1], :], _DIMS["nn"])
            acc = part if acc is None else acc + part
        o_ref[...] = acc.astype(o_ref.dtype)

    return _pc(body, name=name, grid=(T // tm,),
               in_specs=[pl.BlockSpec((tm, a.shape[1]), lambda i: (i, 0)) for a in arrs] + [pl.BlockSpec(w.shape, lambda i: (0, 0))],
               out_specs=pl.BlockSpec((tm, w.shape[1]), lambda i: (i, 0)), out_shape=_sds((T, w.shape[1]), BF16),
               compiler_params=_cp(("parallel",), 32 << 20))(*arrs, w)


def out_proj_dw(pieces, dy, name):
    T, N = dy.shape
    arrs, offs = [a for a, _ in pieces], [o for _, o in pieces]
    n = len(arrs)
    rows = sum(a.shape[1] for a in arrs)
    tn = 512

    def body(*refs):
        d_ref, o_ref = refs[n], refs[n + 1]
        for j in range(n):
            o_ref[offs[j]:offs[j] + arrs[j].shape[1], :] = _dg(refs[j][...], d_ref[...], _DIMS["tn"]).astype(o_ref.dtype)

    return _pc(body, name=name, grid=(N // tn,),
               in_specs=[pl.BlockSpec(a.shape, lambda j: (0, 0)) for a in arrs] + [pl.BlockSpec((T, tn), lambda j: (0, j))],
               out_specs=pl.BlockSpec((rows, tn), lambda j: (0, j)), out_shape=_sds((rows, N), BF16),
               compiler_params=_cp(("parallel",), 48 << 20))(*arrs, dy)


def in_proj_bwd(pieces, h1, w, name):
    T = h1.shape[0]
    arrs = [a for a, _ in pieces]
    offs = [o for _, o in pieces]
    wid = [a.shape[1] for a in arrs]
    n = len(arrs)
    assert sum(wid) == IN_PAD, "the pieces must tile all columns of P"
    tm = _div_tile(T, 640, 16)

    def dx_body(*refs):
        w_ref, o_ref = refs[n], refs[n + 1]
        acc = None
        for j in range(n):
            part = _dg(refs[j][...], w_ref[:, offs[j]:offs[j] + wid[j]], _DIMS["nt"])
            acc = part if acc is None else acc + part
        o_ref[...] = acc.astype(o_ref.dtype)

    dh1 = _pc(dx_body, name=name + "_dx", grid=(T // tm,),
              in_specs=[pl.BlockSpec((tm, wj), lambda i: (i, 0)) for wj in wid] + [pl.BlockSpec((D, IN_PAD), lambda i: (0, 0))],
              out_specs=pl.BlockSpec((tm, D), lambda i: (i, 0)), out_shape=_sds((T, D), BF16),
              compiler_params=_cp(("parallel",), 40 << 20))(*arrs, w)

    tmd, nk = 512, 4
    tk = T // nk

    def dw_body(h_ref, *refs):
        o_ref, acc = refs[n], refs[n + 1]
        k = pl.program_id(1)

        @pl.when(k == 0)
        def _():
            acc[...] = jnp.zeros_like(acc)

        for j in range(n):
            acc[:, offs[j]:offs[j] + wid[j]] += _dg(h_ref[...], refs[j][...], _DIMS["tn"])

        @pl.when(k == nk - 1)
        def _():
            o_ref[...] = acc[...].astype(o_ref.dtype)

    dw = _pc(dw_body, name=name + "_dw", grid=(D // tmd, nk),
             in_specs=[pl.BlockSpec((tk, tmd), lambda i, k: (k, i))] + [pl.BlockSpec((tk, wj), lambda i, k: (k, 0)) for wj in wid],
             out_specs=pl.BlockSpec((tmd, IN_PAD), lambda i, k: (i, 0)), out_shape=_sds((D, IN_PAD), BF16),
             scratch_shapes=[pltpu.VMEM((tmd, IN_PAD), F32)], compiler_params=_cp(("parallel", "arbitrary"), 48 << 20))(h1, *arrs)
    return dh1, dw


def _norm_mod(xo, shift, scale, g):
    r = lax.rsqrt(jnp.mean(xo * xo, axis=-1, keepdims=True) + EPS)
    return (xo * r) * g * (1.0 + scale) + shift


def res_norm_mod(x, y, gsv, g, nL, name):
    T = x.shape[0]
    has_y = y is not None

    def body(*refs):
        if has_y:
            x_ref, y_ref, gsv_ref, g_ref, xo_ref, h_ref = refs
            xo = x_ref[...] + gsv_ref[0, 0:1, :] * y_ref[...]
            xo_ref[...] = xo
        else:
            x_ref, gsv_ref, g_ref, h_ref = refs
            xo = x_ref[...]
        h_ref[...] = _norm_mod(xo, gsv_ref[0, 1:2, :], gsv_ref[0, 2:3, :], g_ref[...]).astype(h_ref.dtype)

    row = pl.BlockSpec((TR, D), lambda i: (i, 0))
    in_specs = [row] + ([row] if has_y else []) + [pl.BlockSpec((1, 8, D), lambda i: (i // nL, 0, 0)),
                                                     pl.BlockSpec((1, D), lambda i: (0, 0))]
    out_specs = ([row] if has_y else []) + [row]
    out_shape = ([_sds((T, D), F32)] if has_y else []) + [_sds((T, D), BF16)]
    args = (x, y, gsv, g) if has_y else (x, gsv, g)
    outs = _pc(body, name=name, grid=(T // TR,), in_specs=in_specs, out_specs=out_specs, out_shape=out_shape,
               compiler_params=_cp(("arbitrary",), 24 << 20))(*args)
    return (outs[0], outs[1]) if has_y else (None, outs[0])


def res_norm_mod_bwd(xo, y, gsv, g, dh, dres, nL, name):
    T = xo.shape[0]
    has_y = y is not None

    def body(*refs):
        if has_y:
            xo_ref, y_ref, gsv_ref, g_ref, dh_ref, dres_ref, dx_ref, dy_ref, dgsv_ref, dg_ref = refs
        else:
            xo_ref, gsv_ref, g_ref, dh_ref, dres_ref, dx_ref, dgsv_ref, dg_ref = refs
        i = pl.program_id(0)

        @pl.when((i == 0) | (i == nL))
        def _():
            dgsv_ref[...] = jnp.zeros_like(dgsv_ref)

        @pl.when(i == 0)
        def _():
            dg_ref[...] = jnp.zeros_like(dg_ref)

        _, vjp = jax.vjp(_norm_mod, xo_ref[...], gsv_ref[0, 1:2, :], gsv_ref[0, 2:3, :], g_ref[...])
        dxn, dshift, dscale, dg = vjp(dh_ref[...].astype(F32))
        dxo = dres_ref[...] + dxn
        dx_ref[...] = dxo
        if has_y:
            dy_ref[...] = (gsv_ref[0, 0:1, :] * dxo).astype(dy_ref.dtype)
            dgsv_ref[0, 0:1, :] += jnp.sum(y_ref[...] * dxo, axis=0, keepdims=True)
        dgsv_ref[0, 1:2, :] += dshift
        dgsv_ref[0, 2:3, :] += dscale
        dg_ref[0:1, :] += dg

    row = pl.BlockSpec((TR, D), lambda i: (i, 0))
    gspec = pl.BlockSpec((1, 8, D), lambda i: (i // nL, 0, 0))
    in_specs = [row] + ([row] if has_y else []) + [gspec, pl.BlockSpec((1, D), lambda i: (0, 0)), row, row]
    out_specs = [row] + ([row] if has_y else []) + [gspec, pl.BlockSpec((8, D), lambda i: (0, 0))]
    out_shape = [_sds((T, D), F32)] + ([_sds((T, D), BF16)] if has_y else []) + [_sds((2, 8, D), F32), _sds((8, D), F32)]
    args = (xo, y, gsv, g, dh, dres) if has_y else (xo, gsv, g, dh, dres)
    outs = _pc(body, name=name, grid=(T // TR,), in_specs=in_specs, out_specs=out_specs, out_shape=out_shape,
               compiler_params=_cp(("arbitrary",), 32 << 20))(*args)
    if has_y:
        return outs
    return outs[0], None, outs[1], outs[2]


def final_loss(x, y, gsv, g, target, nL, name):
    T = x.shape[0]

    def lossf(xo, gv, t):
        yn = (xo * lax.rsqrt(jnp.mean(xo * xo, axis=-1, keepdims=True) + EPS)) * gv
        e = yn - t
        return 0.5 * jnp.sum(jnp.sum(e * e, axis=-1, keepdims=True) * (1.0 / D), axis=0, keepdims=True)

    def body(x_ref, y_ref, gsv_ref, g_ref, t_ref, loss_ref, dx_ref, dy_ref, dgsv_ref, dg_ref):
        i = pl.program_id(0)

        @pl.when(i == 0)
        def _():
            loss_ref[...] = jnp.zeros_like(loss_ref)
            dg_ref[...] = jnp.zeros_like(dg_ref)

        @pl.when((i == 0) | (i == nL))
        def _():
            dgsv_ref[...] = jnp.zeros_like(dgsv_ref)

        @pl.when(i < nL)
        def _():
            gate = gsv_ref[0, 0:1, :]
            yv = y_ref[...]
            xo = x_ref[...] + gate * yv
            lv, vjp = jax.vjp(lossf, xo, g_ref[...], t_ref[...])
            dxo, dg, _ = vjp(jnp.ones((1, 1), F32))
            loss_ref[...] += jnp.broadcast_to(lv, loss_ref.shape)
            dx_ref[...] = dxo
            dy_ref[...] = (gate * dxo).astype(dy_ref.dtype)
            dgsv_ref[0, 0:1, :] += jnp.sum(yv * dxo, axis=0, keepdims=True)
            dg_ref[0:1, :] += dg

        @pl.when(i >= nL)
        def _():
            dx_ref[...] = jnp.zeros_like(dx_ref)
            dy_ref[...] = jnp.zeros_like(dy_ref)

    row = pl.BlockSpec((TR, D), lambda i: (i, 0))
    gspec = pl.BlockSpec((1, 8, D), lambda i: (i // nL, 0, 0))
    return _pc(
        body, name=name, grid=(T // TR,),
        in_specs=[row, row, gspec, pl.BlockSpec((1, D), lambda i: (0, 0)),
                  pl.BlockSpec((TR, D), lambda i: (jnp.minimum(i, nL - 1), 0))],
        out_specs=[pl.BlockSpec((8, 128), lambda i: (0, 0)), row, row, gspec, pl.BlockSpec((8, D), lambda i: (0, 0))],
        out_shape=[_sds((8, 128), F32), _sds((T, D), F32), _sds((T, D), BF16), _sds((2, 8, D), F32), _sds((8, D), F32)],
        compiler_params=_cp(("arbitrary",), 32 << 20),
    )(x, y, gsv, g, target)


FI_BLK = 2 * D_FF // 4


def _fi_chip(j):
    return (j % 2) * 2 + j // 2


def matmul_fi(a, b, mode, out_dtype, name):
    T = a.shape[0]
    if mode == "tn":
        tmd = 512

        def body(a_ref, b_ref, o_ref):
            o_ref[0] = _dg(a_ref[...], b_ref[...], _DIMS["tn"]).astype(o_ref.dtype)

        return _pc(body, name=name, grid=(D // tmd, 4),
                   in_specs=[pl.BlockSpec((T, tmd), lambda i, j: (0, i)), pl.BlockSpec((T, FI_BLK), lambda i, j: (0, j))],
                   out_specs=pl.BlockSpec((1, tmd, FI_BLK), lambda i, j: (_fi_chip(j), i, 0)),
                   out_shape=_sds((4, D, FI_BLK), out_dtype), compiler_params=_cp(("parallel", "arbitrary"), 48 << 20))(a, b)
    if mode == "nn":
        tm = _div_tile(T, 1280, 16)

        def body(a_ref, b_ref, o_ref):
            o_ref[...] = _dg(a_ref[...], b_ref[0], _DIMS["nn"]).astype(o_ref.dtype)

        return _pc(body, name=name, grid=(T // tm, 4),
                   in_specs=[pl.BlockSpec((tm, D), lambda i, j: (i, 0)), pl.BlockSpec((1, D, FI_BLK), lambda i, j: (_fi_chip(j), 0, 0))],
                   out_specs=pl.BlockSpec((tm, FI_BLK), lambda i, j: (i, j)), out_shape=_sds((T, 4 * FI_BLK), out_dtype),
                   compiler_params=_cp(("parallel", "arbitrary"), 40 << 20))(a, b)
    tm = _div_tile(T, 640, 16)

    def body(a_ref, b_ref, o_ref):
        acc = None
        for k in range(4):
            part = _dg(a_ref[:, k * FI_BLK:(k + 1) * FI_BLK], b_ref[_fi_chip(k)], _DIMS["nt"])
            acc = part if acc is None else acc + part
        o_ref[...] = acc.astype(o_ref.dtype)

    return _pc(body, name=name, grid=(T // tm,),
               in_specs=[pl.BlockSpec((tm, 4 * FI_BLK), lambda i: (i, 0)), pl.BlockSpec((4, D, FI_BLK), lambda i: (0, 0, 0))],
               out_specs=pl.BlockSpec((tm, D), lambda i: (i, 0)), out_shape=_sds((T, D), out_dtype),
               compiler_params=_cp(("parallel",), VMEM_CAP))(a, b)


def _swiglu(gate, up):
    return _silu(gate) * up


def swiglu_fwd(gu, name):
    T = gu.shape[0]

    def body(x_ref, o_ref):
        o_ref[...] = _swiglu(x_ref[:, :FI_BLK].astype(F32), x_ref[:, FI_BLK:].astype(F32)).astype(o_ref.dtype)

    return _pc(body, name=name, grid=(T // TR, 2), in_specs=[pl.BlockSpec((TR, 2 * FI_BLK), lambda i, j: (i, j))],
               out_specs=pl.BlockSpec((TR, FI_BLK), lambda i, j: (i, j)), out_shape=_sds((T, D_FF), BF16),
               compiler_params=_cp(("parallel", "parallel"), 24 << 20))(gu)


def swiglu_bwd(gu, dact, name):
    T = gu.shape[0]

    def body(x_ref, d_ref, o_ref):
        g, u, d = x_ref[:, :FI_BLK].astype(F32), x_ref[:, FI_BLK:].astype(F32), d_ref[...].astype(F32)
        sg = 1.0 / (1.0 + jnp.exp(-g))
        sl = g * sg
        o_ref[:, :FI_BLK] = (d * u * (sg + sl * (1.0 - sg))).astype(o_ref.dtype)
        o_ref[:, FI_BLK:] = (d * sl).astype(o_ref.dtype)

    return _pc(body, name=name, grid=(T // TR, 2),
               in_specs=[pl.BlockSpec((TR, 2 * FI_BLK), lambda i, j: (i, j)), pl.BlockSpec((TR, FI_BLK), lambda i, j: (i, j))],
               out_specs=pl.BlockSpec((TR, 2 * FI_BLK), lambda i, j: (i, j)), out_shape=_sds((T, 2 * D_FF), BF16),
               compiler_params=_cp(("parallel", "parallel"), 32 << 20))(gu, dact)


def rope_tables(L, Lc):
    t = np.arange(L)
    rows, cols = t // GRID_W, t % GRID_W
    inv = ROPE_BASE ** (-np.arange(16, dtype=np.float32) / 16)
    lane = np.arange(64)
    pos = np.where((lane // 32)[None, :] == 0, rows[:, None], cols[:, None]).astype(np.float32)
    ang = jnp.asarray(pos) * jnp.asarray(inv[lane % 16])[None, :]
    cos = jnp.concatenate([jnp.cos(ang), jnp.ones((Lc, 64), F32)], axis=0)
    sin = jnp.concatenate([jnp.sin(ang), jnp.zeros((Lc, 64), F32)], axis=0)
    R = np.zeros((128, 128), np.float32)
    for i in range(128):
        if (i % 32) < 16:
            R[i + 16, i] = -1.0
        else:
            R[i - 16, i] = 1.0
    return jnp.tile(cos, (1, 2)), jnp.tile(sin, (1, 2)), jnp.asarray(R)


def rope_apply(q_src, q_col, k_src, k_col, cos, sin, R, transpose, name, kv_src=None):
    T = cos.shape[0]
    with_kv = kv_src is not None

    def rot(x, c, s, Rm):
        if transpose:
            return x * c + hdot(x * s, Rm, "nt")
        return x * c + hdot(x, Rm) * s

    def body(q_ref, k_ref, c_ref, s_ref, R_ref, *rest):
        qo_ref, ko_ref = rest[-4:-2] if with_kv else rest
        c, s, Rm = c_ref[...], s_ref[...], R_ref[...]
        for j in range(2):
            qo_ref[:, j * 128:(j + 1) * 128] = rot(q_ref[:, j * 128:(j + 1) * 128].astype(F32), c, s, Rm).astype(qo_ref.dtype)
        ko_ref[...] = rot(k_ref[...].astype(F32), c, s, Rm).astype(ko_ref.dtype)
        if with_kv:
            rest[-2][...] = rest[0][...].astype(BF16)
            rest[-1][...] = rest[1][...].astype(BF16)

    tab = pl.BlockSpec((TR, 128), lambda i: (i, 0))
    wide = pl.BlockSpec((TR, 256), lambda i: (i, 0))
    kv_in = [pl.BlockSpec((TR, 256), lambda i: (i, C_KB // 256)), pl.BlockSpec((TR, 256), lambda i: (i, C_VB // 256))] if with_kv else []
    return _pc(body, name=name, grid=(T // TR,),
               in_specs=[pl.BlockSpec((TR, 256), lambda i: (i, q_col)), pl.BlockSpec((TR, 128), lambda i: (i, k_col)),
                         tab, tab, pl.BlockSpec((128, 128), lambda i: (0, 0))] + kv_in,
               out_specs=[wide, tab] + ([wide, wide] if with_kv else []),
               out_shape=[_sds((T, 256), BF16), _sds((T, 128), BF16)] + ([_sds((T, 256), BF16)] * 2 if with_kv else []),
               compiler_params=_cp(("parallel",), 16 << 20))(q_src, k_src, cos, sin, R, *([kv_src, kv_src] if with_kv else []))


_SCALE = HD ** -0.5


def _attn_tile(qh, ks, vs, extra):
    ss = []
    for k, add in ks:
        s = _dg(qh, k, _DIMS["nt"]) * _SCALE
        ss.append(s if add is None else s + add)
    m = ss[0].max(axis=-1, keepdims=True)
    for s in ss[1:]:
        m = jnp.maximum(m, s.max(axis=-1, keepdims=True))
    if extra is not None:
        m = jnp.maximum(m, extra)
    ps = [jnp.exp(s - m) for s in ss]
    den = ps[0].sum(axis=-1, keepdims=True)
    for p in ps[1:]:
        den = den + p.sum(axis=-1, keepdims=True)
    if extra is not None:
        den = den + jnp.exp(extra - m)
    num = _dg(ps[0], vs[0], _DIMS["nn"])
    for p, v in zip(ps[1:], vs[1:]):
        num = num + _dg(p, v, _DIMS["nn"])
    linv = 1.0 / den
    return num * linv, m, linv


def _attn_bwd_tile(qh, ks, vs, extra, m, linv, oh, doh):
    delta = jnp.sum(doh * oh, axis=-1, keepdims=True)
    dq = None
    dks, dvs, dss = [], [], []
    for (k, add), v in zip(ks, vs):
        s = _dg(qh, k, _DIMS["nt"]) * _SCALE
        if add is not None:
            s = s + add
        p = jnp.exp(s - m) * linv
        dvs.append(_dg(p, doh, _DIMS["tn"]))
        ds = p * (_dg(doh, v, _DIMS["nt"]) - delta)
        dss.append(ds)
        dsq = ds * _SCALE
        part = _dg(dsq, k, _DIMS["nn"])
        dq = part if dq is None else dq + part
        dks.append(_dg(dsq, qh, _DIMS["tn"]))
    dextra = None
    if extra is not None:
        dextra = -jnp.sum(jnp.exp(extra - m) * linv * delta, axis=0, keepdims=True)
    return dq, dks, dvs, dss, dextra


def _wa_mask(n, L):
    qpos = n * WA_BLK + lax.broadcasted_iota(jnp.int32, (WA_BLK, 3 * WA_BLK), 0)
    kpos = (n - 1) * WA_BLK + lax.broadcasted_iota(jnp.int32, (WA_BLK, 3 * WA_BLK), 1)
    ok = (jnp.abs(qpos - kpos) <= WA_BLK) & (kpos >= 0) & (kpos < L)
    return jnp.where(ok, 0.0, NEG).astype(F32)


WA_BPS = 2


def _wa_specs(L, Lc):
    nb = L // WA_BLK
    cb = L // Lc

    def blk(j, col):
        return pl.BlockSpec((WA_BLK, 128), lambda s: (jnp.clip(s * WA_BPS - 1 + j, 0, nb - 1), col))

    vcol = C_VA // 128
    kspecs = [blk(j, 0) for j in range(WA_BPS + 2)] + [pl.BlockSpec((Lc, 128), lambda s: (cb, 0))]
    vspecs = [blk(j, vcol) for j in range(WA_BPS + 2)] + [pl.BlockSpec((Lc, 128), lambda s: (cb, vcol))]
    return nb, kspecs, vspecs


def win_attn_fwd(qr, kr, P, sink, L, Lc, name):
    T = L + Lc
    nb, kspecs, vspecs = _wa_specs(L, Lc)
    nk = WA_BPS + 2
    QB = WA_BPS * WA_BLK
    nlat = nb // WA_BPS

    def body(q_ref, *refs):
        kbs, kx, vbs, vx, s_ref, o_ref, st_ref = refs[:nk], refs[nk], refs[nk + 1:2 * nk + 1], refs[2 * nk + 1], refs[-3], refs[-2], refs[-1]
        s = pl.program_id(0)

        def put(qs, h, res):
            o, m, linv = res
            o_ref[qs, h * HD:(h + 1) * HD] = o.astype(o_ref.dtype)
            st_ref[qs, h:h + 1] = m
            st_ref[qs, WA_HEADS + h:WA_HEADS + h + 1] = linv

        @pl.when(s < nlat)
        def _():
            for b in range(WA_BPS):
                mask = _wa_mask(s * WA_BPS + b, L)
                qs = slice(b * WA_BLK, (b + 1) * WA_BLK)
                for g in range(WA_KV):
                    sl = slice(g * HD, (g + 1) * HD)
                    k3 = jnp.concatenate([kbs[b + j][:, sl] for j in range(3)], axis=0)
                    v3 = jnp.concatenate([vbs[b + j][:, sl] for j in range(3)], axis=0)
                    for r in range(2):
                        h = 2 * g + r
                        put(qs, h, _attn_tile(q_ref[qs, h * HD:(h + 1) * HD], [(k3, mask), (kx[:, sl], None)], [v3, vx[:, sl]], s_ref[h:h + 1, 0:1]))

        @pl.when(s >= nlat)
        def _():
            for h in range(WA_HEADS):
                sl = slice((h // 2) * HD, (h // 2 + 1) * HD)
                put(slice(None), h, _attn_tile(q_ref[:, h * HD:(h + 1) * HD], [(kx[:, sl], None)], [vx[:, sl]], s_ref[h:h + 1, 0:1]))

    qspec = pl.BlockSpec((QB, 256), lambda s: (s, 0))
    return _pc(body, name=name, grid=(T // QB,),
               in_specs=[qspec] + kspecs + vspecs + [pl.BlockSpec((8, 128), lambda s: (0, 0))],
               out_specs=[qspec, pl.BlockSpec((QB, 8), lambda s: (s, 0))], out_shape=[_sds((T, 256), BF16), _sds((T, 8), F32)],
               compiler_params=_cp(("arbitrary",), 32 << 20))(qr, *([kr] * (nk + 1)), *([P] * (nk + 1)), sink)


def win_attn_bwd(qr, kr, P, sink, do_src, o, stats, L, Lc, name):
    T = L + Lc
    nb, kspecs, vspecs = _wa_specs(L, Lc)
    nk = WA_BPS + 2
    QB = WA_BPS * WA_BLK
    nlat = nb // WA_BPS
    cx = WA_BLK + L

    def body(q_ref, *refs):
        kbs, kx, vbs, vx = refs[:nk], refs[nk], refs[nk + 1:2 * nk + 1], refs[2 * nk + 1]
        s_ref, do_ref, o_ref, st_ref, dq_ref, dk_ref, dv_ref, ds_ref = refs[2 * nk + 2:]
        s = pl.program_id(0)

        @pl.when(s == 0)
        def _():
            dk_ref[...] = jnp.zeros_like(dk_ref)
            dv_ref[...] = jnp.zeros_like(dv_ref)
            ds_ref[...] = jnp.zeros_like(ds_ref)

        def tile(qs, h, ks, vs):
            hs = slice(h * HD, (h + 1) * HD)
            dq, dks, dvs, _, dsk = _attn_bwd_tile(q_ref[qs, hs], ks, vs, s_ref[h:h + 1, 0:1], st_ref[qs, h:h + 1],
                                                  st_ref[qs, WA_HEADS + h:WA_HEADS + h + 1], o_ref[qs, hs].astype(F32), do_ref[qs, hs].astype(F32))
            dq_ref[qs, hs] = dq
            ds_ref[h:h + 1, :] += jnp.broadcast_to(dsk, (1, 128))
            return dks, dvs

        @pl.when(s < nlat)
        def _():
            for b in range(WA_BPS):
                n = s * WA_BPS + b
                mask = _wa_mask(n, L)
                rows = pl.ds(pl.multiple_of(n * WA_BLK, WA_BLK), 3 * WA_BLK)
                qs = slice(b * WA_BLK, (b + 1) * WA_BLK)
                for g in range(WA_KV):
                    sl = slice(g * HD, (g + 1) * HD)
                    k3 = jnp.concatenate([kbs[b + j][:, sl] for j in range(3)], axis=0)
                    v3 = jnp.concatenate([vbs[b + j][:, sl] for j in range(3)], axis=0)
                    acc = None
                    for r in range(2):
                        dks, dvs = tile(qs, 2 * g + r, [(k3, mask), (kx[:, sl], None)], [v3, vx[:, sl]])
                        acc = dks + dvs if acc is None else [a + b_ for a, b_ in zip(acc, dks + dvs)]
                    dk_ref[rows, sl] += acc[0]
                    dk_ref[cx:cx + Lc, sl] += acc[1]
                    dv_ref[rows, sl] += acc[2]
                    dv_ref[cx:cx + Lc, sl] += acc[3]

        @pl.when(s >= nlat)
        def _():
            for h in range(WA_HEADS):
                sl = slice((h // 2) * HD, (h // 2 + 1) * HD)
                dks, dvs = tile(slice(None), h, [(kx[:, sl], None)], [vx[:, sl]])
                dk_ref[cx:cx + Lc, sl] += dks[0]
                dv_ref[cx:cx + Lc, sl] += dvs[0]

    qspec = pl.BlockSpec((QB, 256), lambda s: (s, 0))
    acc_spec = pl.BlockSpec((T + 2 * WA_BLK, 128), lambda s: (0, 0))
    return _pc(body, name=name, grid=(T // QB,),
               in_specs=[qspec] + kspecs + vspecs + [pl.BlockSpec((8, 128), lambda s: (0, 0)), qspec, qspec, pl.BlockSpec((QB, 8), lambda s: (s, 0))],
               out_specs=[qspec, acc_spec, acc_spec, pl.BlockSpec((8, 128), lambda s: (0, 0))],
               out_shape=[_sds((T, 256), F32), _sds((T + 2 * WA_BLK, 128), F32), _sds((T + 2 * WA_BLK, 128), F32), _sds((8, 128), F32)],
               compiler_params=_cp(("arbitrary",), 40 << 20))(qr, *([kr] * (nk + 1)), *([P] * (nk + 1)), sink, do_src, o, stats)


def na_index_tables():
    qc = np.arange(GRID_W)[:, None]
    kc = np.arange(GRID_W)[None, :]
    cstart = np.clip(qc - NA_KW // 2, 0, GRID_W - NA_KW)
    ok = (kc >= cstart) & (kc < cstart + NA_KW)
    dx = np.clip(kc - qc, -(NA_KW - 1), NA_KW - 1) + (NA_KW - 1)
    off = np.arange(NA_KH)[:, None]
    kr = np.arange(NA_KH)[None, :]
    dy = kr - off + (NA_KH - 1)
    return ok, dx, dy


def _na_selectors():
    ok, dx, dy = na_index_tables()
    e1 = np.zeros((GRID_W * GRID_W, 128), np.float32)
    qi, ki = np.nonzero(ok)
    e1[qi * GRID_W + ki, dx[qi, ki]] = 1.0
    e2 = np.zeros((16, NA_KH * NA_KH), np.float32)
    oi, ri = np.meshgrid(np.arange(NA_KH), np.arange(NA_KH), indexing="ij")
    e2[dy[oi, ri].ravel(), (oi * NA_KH + ri).ravel()] = 1.0
    return ok, jnp.asarray(e1), jnp.asarray(np.kron(np.eye(NA_HEADS, dtype=np.float32), e2))


def na_bias_table(rpb, tag):
    ok, e1, e2 = _na_selectors()
    r2 = jnp.pad(rpb.astype(F32), ((0, 0), (0, 1), (0, 128 - (2 * NA_KW - 1)))).reshape(NA_HEADS * 16, 128)
    r1 = matmul(e2, r2, "tn", F32, f"na_bias_sel1_{tag}", hi=True)
    x = matmul(r1, e1, "nt", F32, f"na_bias_sel2_{tag}", hi=True)
    b = x.reshape(NA_HEADS, NA_KH, NA_KH, GRID_W, GRID_W).transpose(0, 1, 3, 2, 4)
    b = b + jnp.asarray(np.where(ok, 0.0, NEG).astype(np.float32))[None, None, :, None, :]
    return b.reshape(NA_HEADS, NA_KH, GRID_W, NA_KH * GRID_W)


def _na_rows(r, GR):
    r0 = jnp.clip(r - NA_KH // 2, 0, GR - NA_KH)
    return r0, jnp.clip(r - r0, 0, NA_KH - 1)


NA_RPS = 4


def na_fwd(P, kb, vb, bias, L, Lc, name):
    T = L + Lc
    GR = L // GRID_W
    W = NA_KH * GRID_W
    QB = GRID_W * NA_RPS
    nlat = GR // NA_RPS

    def body(q_ref, k_ref, v_ref, b_ref, o_ref, st_ref):
        s = pl.program_id(0)

        def put(qs, h, res):
            o, m, linv = res
            o_ref[qs, h * HD:(h + 1) * HD] = o.astype(o_ref.dtype)
            st_ref[qs, h:h + 1] = m
            st_ref[qs, NA_HEADS + h:NA_HEADS + h + 1] = linv

        @pl.when(s < nlat)
        def _():
            for rr in range(NA_RPS):
                r0, off = _na_rows(s * NA_RPS + rr, GR)
                rows = pl.ds(pl.multiple_of(r0 * GRID_W, GRID_W), W)
                qs = slice(rr * GRID_W, (rr + 1) * GRID_W)
                for h in range(NA_HEADS):
                    hs = slice(h * HD, (h + 1) * HD)
                    put(qs, h, _attn_tile(q_ref[qs, hs], [(k_ref[rows, hs], b_ref[h, off]), (k_ref[L:T, hs], None)],
                                          [v_ref[rows, hs], v_ref[L:T, hs]], None))

        @pl.when(s >= nlat)
        def _():
            for h in range(NA_HEADS):
                hs = slice(h * HD, (h + 1) * HD)
                put(slice(None), h, _attn_tile(q_ref[:, hs], [(k_ref[L:T, hs], None)], [v_ref[L:T, hs]], None))

    one = pl.Buffered(1)
    return _pc(body, name=name, grid=(T // QB,),
               in_specs=[pl.BlockSpec((QB, 256), lambda r: (r, C_QB // 256)),
                         pl.BlockSpec((T, 256), lambda r: (0, 0), pipeline_mode=one),
                         pl.BlockSpec((T, 256), lambda r: (0, 0), pipeline_mode=one),
                         pl.BlockSpec((NA_HEADS, NA_KH, GRID_W, W), lambda r: (0, 0, 0, 0), pipeline_mode=one)],
               out_specs=[pl.BlockSpec((QB, 256), lambda r: (r, 0)), pl.BlockSpec((QB, 8), lambda r: (r, 0))],
               out_shape=[_sds((T, 256), BF16), _sds((T, 8), F32)],
               compiler_params=_cp(("arbitrary",), 32 << 20))(P, kb, vb, bias)


def na_bwd(P, kb, vb, bias, do_src, o, stats, L, Lc, name):
    T = L + Lc
    GR = L // GRID_W
    W = NA_KH * GRID_W
    QB = GRID_W * NA_RPS
    nlat = GR // NA_RPS

    def body(q_ref, k_ref, v_ref, b_ref, do_ref, o_ref, st_ref, dq_ref, dk_ref, dv_ref, db_ref):
        s = pl.program_id(0)

        @pl.when(s == 0)
        def _():
            dk_ref[...] = jnp.zeros_like(dk_ref)
            dv_ref[...] = jnp.zeros_like(dv_ref)
            db_ref[...] = jnp.zeros_like(db_ref)

        def tile(qs, h, ks, vs):
            hs = slice(h * HD, (h + 1) * HD)
            dq, dks, dvs, dss, _ = _attn_bwd_tile(q_ref[qs, hs], ks, vs, None, st_ref[qs, h:h + 1], st_ref[qs, NA_HEADS + h:NA_HEADS + h + 1],
                                                  o_ref[qs, hs].astype(F32), do_ref[qs, hs].astype(F32))
            dq_ref[qs, hs] = dq.astype(dq_ref.dtype)
            return dks, dvs, dss

        @pl.when(s < nlat)
        def _():
            for rr in range(NA_RPS):
                r0, off = _na_rows(s * NA_RPS + rr, GR)
                rows = pl.ds(pl.multiple_of(r0 * GRID_W, GRID_W), W)
                qs = slice(rr * GRID_W, (rr + 1) * GRID_W)
                for h in range(NA_HEADS):
                    hs = slice(h * HD, (h + 1) * HD)
                    dks, dvs, dss = tile(qs, h, [(k_ref[rows, hs], b_ref[h, off]), (k_ref[L:T, hs], None)], [v_ref[rows, hs], v_ref[L:T, hs]])
                    dk_ref[rows, hs] += dks[0]
                    dv_ref[rows, hs] += dvs[0]
                    dk_ref[L:T, hs] += dks[1]
                    dv_ref[L:T, hs] += dvs[1]
                    db_ref[h, off] += dss[0]

        @pl.when(s >= nlat)
        def _():
            for h in range(NA_HEADS):
                hs = slice(h * HD, (h + 1) * HD)
                dks, dvs, _ = tile(slice(None), h, [(k_ref[L:T, hs], None)], [v_ref[L:T, hs]])
                dk_ref[L:T, hs] += dks[0]
                dv_ref[L:T, hs] += dvs[0]

    one = pl.Buffered(1)
    full = lambda shape: pl.BlockSpec(shape, lambda r: (0,) * len(shape), pipeline_mode=one)
    qspec = pl.BlockSpec((QB, 256), lambda r: (r, 0))
    return _pc(body, name=name, grid=(T // QB,),
               in_specs=[pl.BlockSpec((QB, 256), lambda r: (r, C_QB // 256)), full((T, 256)), full((T, 256)),
                         full((NA_HEADS, NA_KH, GRID_W, W)), pl.BlockSpec((QB, 256), lambda r: (r, 1)), qspec, pl.BlockSpec((QB, 8), lambda r: (r, 0))],
               out_specs=[qspec, full((T, 256)), full((T, 256)), full((NA_HEADS, NA_KH, GRID_W, W))],
               out_shape=[_sds((T, 256), BF16), _sds((T, 256), F32), _sds((T, 256), F32), _sds((NA_HEADS, NA_KH, GRID_W, W), F32)],
               compiler_params=_cp(("arbitrary",), 48 << 20))(P, kb, vb, bias, do_src, o, stats)


def na_rpb_grad(dbias, tag):
    _, e1, e2 = _na_selectors()
    x = dbias.reshape(NA_HEADS, NA_KH, GRID_W, NA_KH, GRID_W).transpose(0, 1, 3, 2, 4).reshape(NA_HEADS * NA_KH * NA_KH, GRID_W * GRID_W)
    r1 = matmul(x, e1, "nn", F32, f"na_rpb_sel1_{tag}", hi=True, tk=1024)
    r2 = matmul(e2, r1, "nn", F32, f"na_rpb_sel2_{tag}", hi=True)
    return r2.reshape(NA_HEADS, 16, 128)[:, :2 * NA_KH - 1, :2 * NA_KW - 1]


_HALO = 8


def _halo_specs(T, col0):
    nh = TR // _HALO
    cur = pl.BlockSpec((TR, 256), lambda i, j: (i, col0 + j))
    prv = pl.BlockSpec((_HALO, 256), lambda i, j: (jnp.maximum(i * nh - 1, 0), col0 + j))
    nxt = pl.BlockSpec((_HALO, 256), lambda i, j: (jnp.minimum((i + 1) * nh, T // _HALO - 1), col0 + j))
    return prv, cur, nxt


def _fill_ext(ext, prv, cur, nxt, i, nL, nT):
    has_prev = jnp.where((i != 0) & (i != nL), 1.0, 0.0)
    has_next = jnp.where((i != nL - 1) & (i != nT - 1), 1.0, 0.0)
    ext[0:_HALO, :] = prv[...].astype(F32) * has_prev
    ext[_HALO:_HALO + TR, :] = cur[...].astype(F32)
    ext[_HALO + TR:, :] = nxt[...].astype(F32) * has_next


def conv_silu_fwd(P, w8, b, nL, name):
    T = P.shape[0]
    nT = T // TR

    def body(prv, cur, nxt, w_ref, b_ref, pre_ref, act_ref, ext):
        i = pl.program_id(0)
        _fill_ext(ext, prv, cur, nxt, i, nL, nT)
        y = jnp.broadcast_to(b_ref[...], (TR, 256))
        for k in range(S_CONV):
            y = y + w_ref[k:k + 1, :] * ext[pl.ds(_HALO - S_CONV // 2 + k, TR), :]
        pre_ref[...] = y
        act_ref[...] = _silu(y)

    prv, cur, nxt = _halo_specs(T, C_XBC // 256)
    out = pl.BlockSpec((TR, 256), lambda i, j: (i, j))
    return _pc(body, name=name, grid=(nT, 4),
               in_specs=[prv, cur, nxt, pl.BlockSpec((8, 256), lambda i, j: (0, j)), pl.BlockSpec((1, 256), lambda i, j: (0, j))],
               out_specs=[out, out], out_shape=[_sds((T, 1024), F32), _sds((T, 1024), F32)],
               scratch_shapes=[pltpu.VMEM((TR + 2 * _HALO, 256), F32)],
               compiler_params=_cp(("parallel", "parallel"), 16 << 20))(P, P, P, w8, b)


def dsilu(pre, dxs_list, db_list, dc_list, name):
    T = pre.shape[0]
    n1, n2, n3 = len(dxs_list), len(db_list), len(dc_list)

    def body(*refs):
        pre_ref = refs[0]
        ins = refs[1:1 + n1 + n2 + n3]
        out = refs[-1]

        def part(rs, lo, hi):
            g = rs[0][...].astype(F32)
            for r in rs[1:]:
                g = g + r[...].astype(F32)
            x = pre_ref[:, lo:hi]
            sg = 1.0 / (1.0 + jnp.exp(-x))
            sl = x * sg
            out[:, lo:hi] = g * (sg + sl * (1.0 - sg))

        part(ins[:n1], 0, 512)
        part(ins[n1:n1 + n2], 512, 768)
        part(ins[n1 + n2:], 768, 1024)

    spec = lambda w: pl.BlockSpec((TR, w), lambda i: (i, 0))
    return _pc(body, name=name, grid=(T // TR,),
               in_specs=[spec(1024)] + [spec(512)] * n1 + [spec(256)] * (n2 + n3),
               out_specs=spec(1024), out_shape=_sds((T, 1024), F32),
               compiler_params=_cp(("parallel",), 32 << 20))(pre, *dxs_list, *db_list, *dc_list)


def conv_bwd(dpre, P, w8, nL, name):
    T = P.shape[0]
    nT = T // TR

    def body(dp, dc, dn, xp, xc, xn, w_ref, dx_ref, dw_ref, db_ref, extd, extx):
        i = pl.program_id(1)
        _fill_ext(extd, dp, dc, dn, i, nL, nT)
        _fill_ext(extx, xp, xc, xn, i, nL, nT)

        @pl.when(i == 0)
        def _():
            dw_ref[...] = jnp.zeros_like(dw_ref)
            db_ref[...] = jnp.zeros_like(db_ref)

        d = dc[...]
        dx = jnp.zeros((TR, 256), F32)
        for k in range(S_CONV):
            dx = dx + w_ref[k:k + 1, :] * extd[pl.ds(_HALO + S_CONV // 2 - k, TR), :]
            dw_ref[k:k + 1, :] += jnp.sum(d * extx[pl.ds(_HALO - S_CONV // 2 + k, TR), :], axis=0, keepdims=True)
        dx_ref[...] = dx.astype(dx_ref.dtype)
        db_ref[0:1, :] += jnp.sum(d, axis=0, keepdims=True)

    def swap(spec):
        f = spec.index_map
        return pl.BlockSpec(spec.block_shape, lambda j, i: f(i, j))

    dprv, dcur, dnxt = [swap(s) for s in _halo_specs(T, 0)]
    xprv, xcur, xnxt = [swap(s) for s in _halo_specs(T, C_XBC // 256)]
    acc = pl.BlockSpec((8, 256), lambda j, i: (0, j))
    return _pc(body, name=name, grid=(4, nT),
               in_specs=[dprv, dcur, dnxt, xprv, xcur, xnxt, acc],
               out_specs=[pl.BlockSpec((TR, 256), lambda j, i: (i, j)), acc, acc],
               out_shape=[_sds((T, 1024), BF16), _sds((8, 1024), F32), _sds((8, 1024), F32)],
               scratch_shapes=[pltpu.VMEM((TR + 2 * _HALO, 256), F32), pltpu.VMEM((TR + 2 * _HALO, 256), F32)],
               compiler_params=_cp(("parallel", "arbitrary"), 16 << 20))(dpre, dpre, dpre, P, P, P, w8)


def _onehot_row(h, n):
    return (lax.broadcasted_iota(jnp.int32, (1, n), 1) == h).astype(F32)


def _onehot_col(h, n):
    return (lax.broadcasted_iota(jnp.int32, (n, 1), 0) == h).astype(F32)


def _ssd_chunk(xs, dtr, dtb, alog, bm, cm, hin, reverse):
    Qn = S_Q
    ii = lax.broadcasted_iota(jnp.int32, (Qn, Qn), 0)
    jj = lax.broadcasted_iota(jnp.int32, (Qn, Qn), 1)
    keep = (ii <= jj) if reverse else (ii >= jj)
    tri = keep.astype(F32)
    triT = ((jj <= ii) if reverse else (jj >= ii)).astype(F32)
    eye = (ii == jj).astype(F32)
    dt = _softplus(dtr + dtb)
    a = dt * (-jnp.exp(alog))
    cs = hdot(tri, a)
    csT = hdot(a, triT, "tn")
    dtT = hdot(dt, eye, "tn")
    last = _onehot_row(0 if reverse else Qn - 1, Qn)
    ys, houts = [], []
    for g in range(S_GROUPS):
        G = bdot(cm[g], bm[g], "nt")
        for r in range(S_HEADS // S_GROUPS):
            h = g * (S_HEADS // S_GROUPS) + r
            eh_r, eh_c = _onehot_row(h, S_HEADS), _onehot_col(h, S_HEADS)
            cs_c = jnp.sum(cs * eh_r, axis=1, keepdims=True)
            dt_c = jnp.sum(dt * eh_r, axis=1, keepdims=True)
            cs_r = jnp.sum(csT * eh_c, axis=0, keepdims=True)
            dt_r = jnp.sum(dtT * eh_c, axis=0, keepdims=True)
            tot = jnp.sum(cs_r * last, axis=1, keepdims=True)
            decay = jnp.exp(jnp.where(keep, cs_c - cs_r, NEG))
            w = G * decay * dt_r
            y = bdot(w, xs[h], "nn") + bdot(cm[g], hin[h], "nt") * jnp.exp(cs_c)
            xsc = xs[h] * (jnp.exp(tot - cs_c) * dt_c)
            hout = hin[h] * jnp.exp(tot) + bdot(xsc, bm[g], "tn")
            ys.append(y)
            houts.append(hout)
    return ys, houts


def _ssd_orders(L, Lc):
    nl, ncx = L // S_Q, Lc // S_Q
    fwd = lambda s: jnp.where(s < ncx, nl + s, s - ncx)
    bwd = lambda s: nl + ncx - 1 - s
    return nl + ncx, fwd, bwd


def _ssd_in_specs(fo, bo, step):
    def at(order, w, col):
        return pl.BlockSpec((S_Q, w), lambda u: (order(step(u)), col))
    specs = []
    for order in (fo, bo):
        specs += [at(order, 512, 0), at(order, 256, 2), at(order, 256, 3), at(order, 128, C_DT // 128)]
    return specs


def ssd_fwd(act, P, dtb, alog, L, Lc, name):
    T = L + Lc
    ns, fo, bo = _ssd_orders(L, Lc)

    def body(xf, bf, cf, df, xb, bb, cb, db, dtb_ref, al_ref, yf, yb, hsf, hsb, Hf, Hb):
        s = pl.program_id(0)

        @pl.when(s == 0)
        def _():
            Hf[...] = jnp.zeros_like(Hf)
            Hb[...] = jnp.zeros_like(Hb)

        for d, (x_r, b_r, c_r, dt_r, y_r, hs_r, H) in enumerate(((xf, bf, cf, df, yf, hsf, Hf), (xb, bb, cb, db, yb, hsb, Hb))):
            hin = [H[h] for h in range(S_HEADS)]
            hs_r[0] = H[...]
            ys, houts = _ssd_chunk(
                [x_r[:, h * S_P:(h + 1) * S_P] for h in range(S_HEADS)], dt_r[:, d * 8:(d + 1) * 8],
                dtb_ref[d:d + 1, 0:8], al_ref[d:d + 1, 0:8],
                [b_r[:, g * S_N:(g + 1) * S_N] for g in range(S_GROUPS)], [c_r[:, g * S_N:(g + 1) * S_N] for g in range(S_GROUPS)],
                hin, reverse=(d == 1))
            for h in range(S_HEADS):
                y_r[:, h * S_P:(h + 1) * S_P] = ys[h]
                H[h] = houts[h]

    ident = lambda u: u
    small = pl.BlockSpec((8, 128), lambda u: (0, 0))
    hspec = pl.BlockSpec((1, S_HEADS, S_P, S_N), lambda u: (u, 0, 0, 0))
    return _pc(body, name=name, grid=(ns,),
               in_specs=_ssd_in_specs(fo, bo, ident) + [small, small],
               out_specs=[pl.BlockSpec((S_Q, 512), lambda u: (fo(u), 0)), pl.BlockSpec((S_Q, 512), lambda u: (bo(u), 0)), hspec, hspec],
               out_shape=[_sds((T, 512), F32), _sds((T, 512), F32), _sds((ns, S_HEADS, S_P, S_N), F32), _sds((ns, S_HEADS, S_P, S_N), F32)],
               scratch_shapes=[pltpu.VMEM((S_HEADS, S_P, S_N), F32), pltpu.VMEM((S_HEADS, S_P, S_N), F32)],
               compiler_params=_cp(("arbitrary",), 32 << 20))(act, act, act, P, act, act, act, P, dtb, alog)


def ssd_bwd(act, P, dtb, alog, hsf, hsb, dy, L, Lc, name):
    T = L + Lc
    ns, fo, bo = _ssd_orders(L, Lc)
    step = lambda u: ns - 1 - u

    def body(xf, bf, cf, df, xb, bb, cb, db, dtb_ref, al_ref, hsf_r, hsb_r, dyf, dyb,
             dxf, dbf, dcf, ddf, dxb, dbb, dcb, ddb, ddtb, dal, dHf, dHb):
        u = pl.program_id(0)

        @pl.when(u == 0)
        def _():
            dHf[...] = jnp.zeros_like(dHf)
            dHb[...] = jnp.zeros_like(dHb)
            ddtb[...] = jnp.zeros_like(ddtb)
            dal[...] = jnp.zeros_like(dal)

        dirs = ((xf, bf, cf, df, hsf_r, dyf, dxf, dbf, dcf, ddf, dHf), (xb, bb, cb, db, hsb_r, dyb, dxb, dbb, dcb, ddb, dHb))
        for d, (x_r, b_r, c_r, dt_r, hs_r, dy_r, dx_o, db_o, dc_o, dd_o, dH) in enumerate(dirs):
            f = functools.partial(_ssd_chunk, reverse=(d == 1))
            _, vjp = jax.vjp(
                f, [x_r[:, h * S_P:(h + 1) * S_P] for h in range(S_HEADS)], dt_r[:, d * 8:(d + 1) * 8],
                dtb_ref[d:d + 1, 0:8], al_ref[d:d + 1, 0:8],
                [b_r[:, g * S_N:(g + 1) * S_N] for g in range(S_GROUPS)], [c_r[:, g * S_N:(g + 1) * S_N] for g in range(S_GROUPS)],
                [hs_r[0, h] for h in range(S_HEADS)])
            gx, gdt, gdtb, gal, gb, gc, gh = vjp(([dy_r[:, h * S_P:(h + 1) * S_P] for h in range(S_HEADS)],
                                                  [dH[h] for h in range(S_HEADS)]))
            for h in range(S_HEADS):
                dx_o[:, h * S_P:(h + 1) * S_P] = gx[h]
                dH[h] = gh[h]
            for g in range(S_GROUPS):
                db_o[:, g * S_N:(g + 1) * S_N] = gb[g]
                dc_o[:, g * S_N:(g + 1) * S_N] = gc[g]
            dd_o[...] = gdt
            ddtb[d:d + 1, 0:8] += gdtb
            dal[d:d + 1, 0:8] += gal

    small = pl.BlockSpec((8, 128), lambda u: (0, 0))
    hspec = pl.BlockSpec((1, S_HEADS, S_P, S_N), lambda u: (step(u), 0, 0, 0))
    at = lambda order, w: pl.BlockSpec((S_Q, w), lambda u: (order(step(u)), 0))
    outs = []
    for order in (fo, bo):
        outs += [at(order, 512), at(order, 256), at(order, 256), at(order, 8)]
    oshape = [_sds((T, 512), F32), _sds((T, 256), F32), _sds((T, 256), F32), _sds((T, 8), F32)]
    return _pc(body, name=name, grid=(ns,),
               in_specs=_ssd_in_specs(fo, bo, step) + [small, small, hspec, hspec, at(fo, 512), at(bo, 512)],
               out_specs=outs + [small, small], out_shape=oshape + oshape + [_sds((8, 128), F32), _sds((8, 128), F32)],
               scratch_shapes=[pltpu.VMEM((S_HEADS, S_P, S_N), F32), pltpu.VMEM((S_HEADS, S_P, S_N), F32)],
               compiler_params=_cp(("arbitrary",), 40 << 20))(act, act, act, P, act, act, act, P, dtb, alog, hsf, hsb, dy, dy)


def _ssm_out(yf, yb, xs, z, dskip, g):
    y = (yf + yb + dskip * xs) * _silu(z)
    return (y * lax.rsqrt(jnp.mean(y * y, axis=-1, keepdims=True) + EPS)) * g


def ssm_out_fwd(yf, yb, act, P, dskip, g, name):
    T = yf.shape[0]

    def body(yf_r, yb_r, xs_r, z_r, d_r, g_r, o_r):
        o_r[...] = _ssm_out(yf_r[...], yb_r[...], xs_r[...], z_r[...], d_r[...], g_r[...]).astype(o_r.dtype)

    row = pl.BlockSpec((TR, 512), lambda i: (i, 0))
    vec = pl.BlockSpec((1, 512), lambda i: (0, 0))
    return _pc(body, name=name, grid=(T // TR,),
               in_specs=[row, row, row, pl.BlockSpec((TR, 512), lambda i: (i, C_Z // 512)), vec, vec],
               out_specs=row, out_shape=_sds((T, 512), BF16),
               compiler_params=_cp(("parallel",), 16 << 20))(yf, yb, act, P, dskip, g)


def ssm_out_bwd(yf, yb, act, P, dskip, g, do_src, name):
    T = yf.shape[0]

    def body(yf_r, yb_r, xs_r, z_r, d_r, g_r, do_r, dy_r, dxs_r, dz_r, dv_r):
        @pl.when(pl.program_id(0) == 0)
        def _():
            dv_r[...] = jnp.zeros_like(dv_r)

        _, vjp = jax.vjp(_ssm_out, yf_r[...], yb_r[...], xs_r[...], z_r[...], d_r[...], g_r[...])
        dyf, _, dxs, dz, dd, dg = vjp(do_r[...].astype(F32))
        dy_r[...] = dyf
        dxs_r[...] = dxs
        dz_r[...] = dz.astype(dz_r.dtype)
        dv_r[0:1, :] += dd
        dv_r[1:2, :] += dg

    row = pl.BlockSpec((TR, 512), lambda i: (i, 0))
    vec = pl.BlockSpec((1, 512), lambda i: (0, 0))
    return _pc(body, name=name, grid=(T // TR,),
               in_specs=[row, row, row, pl.BlockSpec((TR, 512), lambda i: (i, C_Z // 512)), vec, vec,
                         pl.BlockSpec((TR, 512), lambda i: (i, 1))],
               out_specs=[row, row, row, pl.BlockSpec((8, 512), lambda i: (0, 0))],
               out_shape=[_sds((T, 512), F32), _sds((T, 512), F32), _sds((T, 512), BF16), _sds((8, 512), F32)],
               compiler_params=_cp(("arbitrary",), 24 << 20))(yf, yb, act, P, dskip, g, do_src)


def add_halves(xv, got, cvec, name):
    n, r, cdim = xv.shape
    h = r // 2

    def body(c_ref, x_ref, g_ref, o_ref):
        o_ref[...] = (x_ref[...].astype(F32) + g_ref[...].astype(F32)).astype(o_ref.dtype)

    gs = pltpu.PrefetchScalarGridSpec(
        num_scalar_prefetch=1, grid=(n,),
        in_specs=[pl.BlockSpec((1, h, cdim), lambda k, c_ref: (k, c_ref[0], 0)), pl.BlockSpec((1, h, cdim), lambda k, c_ref: (k, 0, 0))],
        out_specs=pl.BlockSpec((1, h, cdim), lambda k, c_ref: (k, 0, 0)))
    return _pc(body, name=name, grid_spec=gs, out_shape=_sds((n, h, cdim), BF16),
               compiler_params=_cp(("arbitrary",), 24 << 20))(cvec, xv, got)


def sum_slots(a, name):
    n, r, cdim = a.shape
    tr = _div_tile(r, 512, 16)

    def body(a_ref, o_ref):
        acc = a_ref[0].astype(F32)
        for k in range(1, n):
            acc = acc + a_ref[k].astype(F32)
        o_ref[...] = acc

    return _pc(body, name=name, grid=(r // tr,), in_specs=[pl.BlockSpec((n, tr, cdim), lambda i: (0, i, 0))],
               out_specs=pl.BlockSpec((tr, cdim), lambda i: (i, 0)), out_shape=_sds((r, cdim), F32),
               compiler_params=_cp(("parallel",), 32 << 20))(a)


def adamw(w, g, m, v, name):
    B, R, C = w.shape
    tr = _div_tile(R, max(8, (1 << 19) // max(C, 1) // 8 * 8), 8) if R % 8 == 0 else R
    c1 = 1.0 / (1.0 - ADAM_B1 ** ADAM_STEP)
    c2 = 1.0 / (1.0 - ADAM_B2 ** ADAM_STEP)

    def body(w_ref, g_ref, m_ref, v_ref, d_ref, mo_ref, vo_ref):
        gg = g_ref[...]
        mn = ADAM_B1 * m_ref[...] + (1.0 - ADAM_B1) * gg
        vn = ADAM_B2 * v_ref[...] + (1.0 - ADAM_B2) * (gg * gg)
        d_ref[...] = -ADAM_LR * ((mn * c1) / (jnp.sqrt(vn * c2) + ADAM_EPS) + ADAM_WD * w_ref[...])
        mo_ref[...] = mn
        vo_ref[...] = vn

    spec = pl.BlockSpec((1, tr, C), lambda b, i: (b, i, 0))
    return _pc(body, name=name, grid=(B, R // tr), in_specs=[spec] * 4, out_specs=[spec] * 3,
               out_shape=[_sds((B, R, C), F32)] * 3, compiler_params=_cp(("parallel", "parallel"), 32 << 20))(w, g, m, v)


def _me():
    return lax.axis_index("x"), lax.axis_index("y"), lax.axis_index("c")


def _flip(v, bit):
    return 1 - v if bit else v


def allgather8(xv, name):
    R = xv.shape[0]

    def body(x_ref, out_ref, sum_ref, send_sems, recv_sems):
        mx, my, mc = _me()
        me = 4 * mx + 2 * my + mc
        out_ref[me] = x_ref[...]
        sends, recvs = [], []
        for k in range(1, 8):
            px, py, pc = _flip(mx, k & 4), _flip(my, k & 2), _flip(mc, k & 1)
            peer = 4 * px + 2 * py + pc
            sends.append(pltpu.make_async_remote_copy(src_ref=x_ref, dst_ref=out_ref.at[me], send_sem=send_sems.at[k - 1],
                                                      recv_sem=recv_sems.at[k - 1], device_id=(px, py, pc), device_id_type=MESH))
            recvs.append(pltpu.make_async_remote_copy(src_ref=x_ref, dst_ref=out_ref.at[peer], send_sem=send_sems.at[k - 1],
                                                      recv_sem=recv_sems.at[k - 1], device_id=(px, py, pc), device_id_type=MESH))
        for cp in sends:
            cp.start()
        for cp in recvs:
            cp.wait_recv()
        for cp in sends:
            cp.wait_send()
        acc = out_ref[0]
        for d in range(1, 8):
            acc = acc + out_ref[d]
        sum_ref[...] = acc

    vm = pl.BlockSpec(memory_space=pltpu.VMEM)
    return _pc(body, name=name, pin=False, in_specs=[vm], out_specs=[vm, vm], out_shape=[_sds((8, R, 128), F32), _sds((R, 128), F32)],
               scratch_shapes=[pltpu.SemaphoreType.DMA((7,)), pltpu.SemaphoreType.DMA((7,))],
               compiler_params=_cp(None, 32 << 20))(xv)


def _other_chips(mx, my):
    return [(1 - mx, my), (mx, 1 - my), (1 - mx, 1 - my)]


def _halves(r, mc, mult):
    h = r // 2
    return pl.ds(pl.multiple_of(mc * h, mult), h), pl.ds(pl.multiple_of((1 - mc) * h, mult), h)


def _rcopy(src, dst, send_sems, recv_sems, k, to):
    return pltpu.make_async_remote_copy(src_ref=src, dst_ref=dst, send_sem=send_sems.at[k], recv_sem=recv_sems.at[k],
                                        device_id=to, device_id_type=MESH)


def _gather_body(xs, outs, send_sems, recv_sems):
    n = len(xs)
    mx, my, mc = _me()
    chip = 2 * mx + my
    sib = (mx, my, 1 - mc)
    chips = _other_chips(mx, my)
    idx = [2 * cx + cy for cx, cy in chips]
    cp = functools.partial(_rcopy, send_sems=send_sems, recv_sems=recv_sems)
    hv = [_halves(x.shape[0], mc, 16) for x in xs]
    first, passed = [], []
    for a in range(n):
        for j, (cx, cy) in enumerate(chips):
            first.append(cp(xs[a].at[hv[a][0]], outs[a].at[chip, hv[a][0]], k=6 * a + j, to=(cx, cy, mc)))
            first[-1].start()
    for a in range(n):
        for j in range(3):
            cp(xs[a].at[hv[a][0]], outs[a].at[idx[j], hv[a][0]], k=6 * a + j, to=sib).wait_recv()
            passed.append(cp(outs[a].at[idx[j], hv[a][0]], outs[a].at[idx[j], hv[a][0]], k=6 * a + 3 + j, to=sib))
            passed[-1].start()
    for a in range(n):
        for j in range(3):
            cp(xs[a].at[hv[a][1]], outs[a].at[idx[j], hv[a][1]], k=6 * a + 3 + j, to=sib).wait_recv()
    for c_ in first + passed:
        c_.wait_send()


def _my_chip():
    return 2 * lax.axis_index("x") + lax.axis_index("y")


def _own_slots(outs, shards):
    return [lax.dynamic_update_index_in_dim(o, x, _my_chip(), 0) for o, x in zip(outs, shards)]


def gather_weights(shards, name):
    n = len(shards)

    def body(*refs):
        _gather_body(refs[:n], refs[n:2 * n], *refs[2 * n:])

    hbm = pl.BlockSpec(memory_space=pl.ANY)
    outs = _pc(body, name=name, in_specs=[hbm] * n, out_specs=[hbm] * n, out_shape=[_sds((4,) + x.shape, x.dtype) for x in shards],
               scratch_shapes=[pltpu.SemaphoreType.DMA((6 * n,)), pltpu.SemaphoreType.DMA((6 * n,))])(*shards)
    return _own_slots(outs, shards)


GATHER_REST_ID = 3


def gather_weights_sc(shards, name):
    n = len(shards)
    x_refs = [jax.new_ref(x, memory_space=pltpu.MemorySpace.HBM) for x in shards]
    out_refs = [jax.empty_ref(_sds((4,) + x.shape, x.dtype), memory_space=pltpu.MemorySpace.HBM) for x in shards]

    @pl.kernel(mesh=plsc.ScalarSubcoreMesh(axis_name="sc", num_cores=1), name=name,
               scratch_types=(pltpu.SemaphoreType.DMA((6 * n,)), pltpu.SemaphoreType.DMA((6 * n,))),
               compiler_params=pltpu.CompilerParams(collective_id=GATHER_REST_ID))
    def launch(send_sems, recv_sems):
        mx, my, mc = _me()
        barrier = pltpu.get_barrier_semaphore()
        for peer in [(mx, my, 1 - mc)] + [(cx, cy, mc) for cx, cy in _other_chips(mx, my)]:
            pl.semaphore_signal(barrier, inc=1, device_id=peer, device_id_type=MESH)
        pl.semaphore_wait(barrier, 4)
        _gather_body(x_refs, out_refs, send_sems, recv_sems)

    launch()
    return _own_slots([o[...] for o in out_refs], shards)


def swap_halves(arrs, name):
    n = len(arrs)

    def body(*refs):
        xs, outs = refs[:n], refs[n:2 * n]
        send_sems, recv_sems = refs[2 * n:]
        mx, my, mc = _me()
        cps = []
        for a in range(n):
            theirs = _halves(xs[a].shape[1], mc, 16)[1]
            cps.append(_rcopy(xs[a].at[pl.ds(0, 4), theirs], outs[a], send_sems, recv_sems, a, (mx, my, 1 - mc)))
            cps[-1].start()
        for c_ in cps:
            c_.wait()

    hbm = pl.BlockSpec(memory_space=pl.ANY)
    return _pc(body, name=name, in_specs=[hbm] * n, out_specs=[hbm] * n,
               out_shape=[_sds((4, x.shape[1] // 2, x.shape[2]), x.dtype) for x in arrs],
               scratch_shapes=[pltpu.SemaphoreType.DMA((n,)), pltpu.SemaphoreType.DMA((n,))])(*arrs)


SCATTER_ID = 4


def scatter_chips_sc(arrs, name):
    n = len(arrs)
    x_refs = [jax.new_ref(x, memory_space=pltpu.MemorySpace.HBM) for x in arrs]
    out_refs = [jax.empty_ref(_sds(x.shape, x.dtype), memory_space=pltpu.MemorySpace.HBM) for x in arrs]

    @pl.kernel(mesh=plsc.ScalarSubcoreMesh(axis_name="sc", num_cores=1), name=name,
               scratch_types=(pltpu.SemaphoreType.DMA((3 * n,)), pltpu.SemaphoreType.DMA((3 * n,))),
               compiler_params=pltpu.CompilerParams(collective_id=SCATTER_ID))
    def launch(send_sems, recv_sems):
        mx, my, mc = _me()
        chip = 2 * mx + my
        chips = _other_chips(mx, my)
        idx = [2 * cx + cy for cx, cy in chips]
        barrier = pltpu.get_barrier_semaphore()
        for cx, cy in chips:
            pl.semaphore_signal(barrier, inc=1, device_id=(cx, cy, mc), device_id_type=MESH)
        pl.semaphore_wait(barrier, 3)
        cp = functools.partial(_rcopy, send_sems=send_sems, recv_sems=recv_sems)
        sends = []
        for a in range(n):
            for j, (cx, cy) in enumerate(chips):
                sends.append(cp(x_refs[a].at[idx[j]], out_refs[a].at[chip], k=3 * a + j, to=(cx, cy, mc)))
                sends[-1].start()
        for a in range(n):
            for j, (cx, cy) in enumerate(chips):
                cp(x_refs[a].at[idx[j]], out_refs[a].at[idx[j]], k=3 * a + j, to=(cx, cy, mc)).wait_recv()
        for c_ in sends:
            c_.wait_send()

    launch()
    return _own_slots([o[...] for o in out_refs], [lax.dynamic_index_in_dim(x, _my_chip(), 0, keepdims=False) for x in arrs])


def share_halves(parts, name):
    flat = [p for w in parts for p in w]
    nw, n = len(parts), len(flat)
    depth = n // nw

    def body(*refs):
        xs, outs = refs[:n], refs[n:n + nw]
        send_sems, recv_sems = refs[n + nw:]
        mx, my, mc = _me()
        sib = (mx, my, 1 - mc)
        sends, recvs = [], []
        for a in range(n):
            w, l = a // depth, a % depth
            mine, theirs = _halves(outs[w].shape[1], mc, 8)
            sends.append(_rcopy(xs[a], outs[w].at[l, mine], send_sems, recv_sems, a, sib))
            recvs.append(_rcopy(xs[a], outs[w].at[l, theirs], send_sems, recv_sems, a, sib))
            sends[-1].start()
        for c_ in recvs:
            c_.wait_recv()
        for c_ in sends:
            c_.wait_send()

    hbm = pl.BlockSpec(memory_space=pl.ANY)
    outs = _pc(body, name=name, in_specs=[hbm] * n, out_specs=[hbm] * nw,
               out_shape=[_sds((depth, 2 * w[0].shape[0], w[0].shape[1]), F32) for w in parts],
               scratch_shapes=[pltpu.SemaphoreType.DMA((n,)), pltpu.SemaphoreType.DMA((n,))])(*flat)
    outs = list(outs)
    mc = lax.axis_index("c")
    for w in range(nw):
        for l in range(depth):
            h = parts[w][l].shape[0]
            outs[w] = lax.dynamic_update_slice(outs[w], parts[w][l][None], (l, mc * h, 0))
    return outs


_BIG = ("w_in", "w_out", "w_ffn_in", "w_ffn_out")
N_CHIPS = 4
DEPTH = 2


def _pad_rows(v, mult=8):
    n = v.shape[0]
    rows = -(-n // 128)
    rows = -(-rows // mult) * mult
    return jnp.pad(v, (0, rows * 128 - n)).reshape(rows, 128)


class _Flat:
    def __init__(self):
        self.items = []

    def add(self, name, a):
        self.items.append((name, a.shape, a.reshape(-1).astype(F32)))

    def rows(self):
        return _pad_rows(jnp.concatenate([a for _, _, a in self.items]))

    def split(self, rows):
        flat = rows.reshape(-1)
        out, o = {}, 0
        for name, shape, a in self.items:
            out[name] = flat[o:o + a.shape[0]].reshape(shape)
            o += a.shape[0]
        return out

    def split_lead(self, rows3):
        n = rows3.shape[0]
        flat = rows3.reshape(n, -1)
        out, o = {}, 0
        for name, shape, a in self.items:
            out[name] = flat[:, o:o + a.shape[0]].reshape((n,) + tuple(shape))
            o += a.shape[0]
        return out


def _gsv(rows):
    z = jnp.zeros((2, D), F32)
    r = [z if a is None else a for a in rows] + [z] * 5
    return jnp.stack(r, axis=1)


def _pad8(a, rows=8, cols=128):
    return jnp.zeros((rows, cols), F32).at[:a.shape[0], :a.shape[1]].set(a.astype(F32))


def kernel(x, c, ctx, c_ctx, w_mod, b_mod, g_mix, w_in, wa_sink, na_rpb, ssm_conv_w, ssm_conv_b, ssm_dt_bias, ssm_a_log, ssm_d, ssm_norm_g, w_out, g_ffn, w_ffn_in, w_ffn_out, g_final, loss_target, m_c_ctx, m_w_mod, m_b_mod, m_g_mix, m_w_in, m_wa_sink, m_na_rpb, m_ssm_conv_w, m_ssm_conv_b, m_ssm_dt_bias, m_ssm_a_log, m_ssm_d, m_ssm_norm_g, m_w_out, m_g_ffn, m_w_ffn_in, m_w_ffn_out, m_g_final, v_c_ctx, v_w_mod, v_b_mod, v_g_mix, v_w_in, v_wa_sink, v_na_rpb, v_ssm_conv_w, v_ssm_conv_b, v_ssm_dt_bias, v_ssm_a_log, v_ssm_d, v_ssm_norm_g, v_w_out, v_g_ffn, v_w_ffn_in, v_w_ffn_out, v_g_final):
    L, Lc = x.shape[1], ctx.shape[1]
    T = L + Lc
    nL = L // TR
    mx, my, mc = lax.axis_index("x"), lax.axis_index("y"), lax.axis_index("c")
    dev = 4 * mx + 2 * my + mc
    chip = 2 * mx + my
    MODW = 6 * D // N_CHIPS
    CW = 1024 // N_CHIPS

    sc = _silu(c.astype(F32))
    scc = _silu(c_ctx.astype(F32))[None]
    f1 = _Flat()
    f1.add("sc", sc)
    f1.add("conv_w", ssm_conv_w)
    g1, _ = allgather8(f1.rows(), "gather_cond")
    g1 = f1.split_lead(g1)
    sc_all = g1["sc"][:, 0]
    conv_w = jnp.concatenate([g1["conv_w"][2 * k] for k in range(N_CHIPS)], axis=-1)
    A16 = jnp.concatenate([sc_all, scc, jnp.zeros((7, D), F32)], axis=0)

    mod_part = matmul_layers(A16, w_mod, "nn", "mod_fwd")
    f2 = _Flat()
    f2.add("mod", mod_part)
    g2, _ = allgather8(f2.rows(), "gather_mod")
    g2 = f2.split_lead(g2)["mod"]
    mods = jnp.concatenate([g2[2 * k] for k in range(N_CHIPS)], axis=-1) + b_mod[:, None, :]
    mod_l = lax.dynamic_index_in_dim(mods, dev, axis=1, keepdims=False).reshape(DEPTH, 6, D)
    mod_c = mods[:, 8].reshape(DEPTH, 6, D)
    mod = jnp.stack([mod_l, mod_c], axis=1)
    mrow = lambda l, j: mod[l, :, j]

    own = {"w_in": w_in, "w_out": w_out, "w_ffn_in": w_ffn_in, "w_ffn_out": w_ffn_out}
    sh16 = [own[n][l].astype(BF16) for n in _BIG for l in range(DEPTH)]
    after_mod = (g2[0, 0, 0, 0] * 0).astype(BF16)
    gath = list(gather_weights([sh16[0] + after_mod], "gather_first"))
    after_first = (gath[0][0, 0, 0] * 0).astype(BF16)
    gath += list(gather_weights_sc([sh16[1] + after_first] + sh16[2:], "gather_rest"))
    gw = {n: [gath[DEPTH * i + l] for l in range(DEPTH)] for i, n in enumerate(_BIG)}
    W_in = [jnp.pad(jnp.concatenate([g[k] for k in range(N_CHIPS)], axis=1), ((0, 0), (0, IN_PAD - IN_COLS))) for g in gw["w_in"]]
    W_out = [g.reshape(D, D) for g in gw["w_out"]]
    W_fo = [g.reshape(D_FF, D) for g in gw["w_ffn_out"]]
    W_fi = gw["w_ffn_in"]

    cos, sin, rotm = rope_tables(L, Lc)
    x0 = jnp.concatenate([x[0], ctx[0]], axis=0).astype(F32)

    sv = []
    xin = x0
    gsv_first = _gsv([None, mrow(0, 0), mrow(0, 1)])
    _, h1 = res_norm_mod(x0, None, gsv_first, g_mix[0][None], nL, "norm_first")
    for l in range(DEPTH):
        s = {"xin": xin, "h1": h1}
        P = matmul(h1, W_in[l], "nn", F32, f"in_proj{l}", tn=IN_PAD)
        qr, kr, kb, vb = rope_apply(P, C_QA // 256, P, C_KA // 128, cos, sin, rotm, False, f"rope{l}", kv_src=P)
        sink8 = _pad8(jnp.broadcast_to(wa_sink[l][:, None], (WA_HEADS, 128)))
        oa, sta = win_attn_fwd(qr, kr, P, sink8, L, Lc, f"wa_fwd{l}")
        bias = na_bias_table(na_rpb[l], l)
        ob, stb = na_fwd(P, kb, vb, bias, L, Lc, f"na_fwd{l}")
        w8 = jnp.concatenate([conv_w[l], jnp.zeros((1, 1024), F32)], axis=0)
        pre, act = conv_silu_fwd(P, w8, ssm_conv_b[l][None], nL, f"conv_fwd{l}")
        dtb8, al8 = _pad8(ssm_dt_bias[l]), _pad8(ssm_a_log[l])
        yf, yb, hsf, hsb = ssd_fwd(act, P, dtb8, al8, L, Lc, f"ssd_fwd{l}")
        dskip = jnp.repeat(ssm_d[l], S_P)[None]
        oc = ssm_out_fwd(yf, yb, act, P, dskip, ssm_norm_g[l][None], f"ssm_out_fwd{l}")
        mixin = [(oa, 0), (ob, 256), (oc, 512)]
        mix = out_proj_fwd(mixin, W_out[l], f"out_proj{l}")
        gsv_mid = _gsv([mrow(l, 2), mrow(l, 3), mrow(l, 4)])
        x1, h2 = res_norm_mod(xin, mix, gsv_mid, g_ffn[l][None], nL, f"norm_mid{l}")
        gu = matmul_fi(h2, W_fi[l], "nn", BF16, f"ffn_in{l}")
        af = swiglu_fwd(gu, f"swiglu_fwd{l}")
        fo = matmul(af, W_fo[l], "nn", BF16, f"ffn_out{l}", tk=D_FF)
        s.update(P=P, qr=qr, kr=kr, sink8=sink8, oa=oa, sta=sta, ob=ob, stb=stb, kb=kb, vb=vb, bias=bias, w8=w8, pre=pre, act=act, dtb8=dtb8, al8=al8, yf=yf,
                 yb=yb, hsf=hsf, hsb=hsb, dskip=dskip, mixin=mixin, mix=mix, gsv_mid=gsv_mid, x1=x1, h2=h2, gu=gu, af=af, fo=fo)
        if l + 1 < DEPTH:
            s["gsv_end"] = _gsv([mrow(l, 5), mrow(l + 1, 0), mrow(l + 1, 1)])
            xin, h1 = res_norm_mod(x1, fo, s["gsv_end"], g_mix[l + 1][None], nL, f"norm_end{l}")
        else:
            s["gsv_end"] = _gsv([mrow(l, 5), None, None])
        sv.append(s)

    last = sv[-1]
    loss8, dres, dfo, dgsv_end, dg_final = final_loss(last["x1"], last["fo"], last["gsv_end"], g_final[None], loss_target[0].astype(F32), nL, "final_loss")
    loss = lax.psum(loss8[0, 0], ("x", "y", "c"))

    dmod = [[None] * 6 for _ in range(DEPTH)]
    gW = {n: [None] * DEPTH for n in _BIG}
    small = [dict() for _ in range(DEPTH)]
    parts = [None] * DEPTH
    cvec = mc.astype(jnp.int32).reshape(1)
    grad_x = None
    for l in reversed(range(DEPTH)):
        s = sv[l]
        dmod[l][5] = dgsv_end[:, 0]
        if l + 1 < DEPTH:
            dmod[l + 1][0], dmod[l + 1][1] = dgsv_end[:, 1], dgsv_end[:, 2]
        daf = matmul(dfo, W_fo[l], "nt", BF16, f"ffn_out_dx{l}")
        gW["w_ffn_out"][l] = matmul(s["af"], dfo, "tn", BF16, f"ffn_out_dw{l}", tm=1408, tk=T).reshape(N_CHIPS, D_FF // N_CHIPS, D)
        dgu = swiglu_bwd(s["gu"], daf, f"swiglu_bwd{l}")
        dh2 = matmul_fi(dgu, W_fi[l], "nt", BF16, f"ffn_in_dx{l}")
        gW["w_ffn_in"][l] = matmul_fi(s["h2"], dgu, "tn", BF16, f"ffn_in_dw{l}")
        dres, dmix, dgsv_mid, dg_ffn = res_norm_mod_bwd(s["x1"], s["mix"], s["gsv_mid"], g_ffn[l][None], dh2, dres, nL, f"norm_mid_bwd{l}")
        dmod[l][2], dmod[l][3], dmod[l][4] = dgsv_mid[:, 0], dgsv_mid[:, 1], dgsv_mid[:, 2]
        dmixin = matmul(dmix, W_out[l], "nt", BF16, f"out_proj_dx{l}")
        gW["w_out"][l] = out_proj_dw(s["mixin"], dmix, f"out_proj_dw{l}").reshape(N_CHIPS, D // N_CHIPS, D)
        P = s["P"]
        dqr, dkr, dva, dsink = win_attn_bwd(s["qr"], s["kr"], P, s["sink8"], dmixin, s["oa"], s["sta"], L, Lc, f"wa_bwd{l}")
        dqa, dka = rope_apply(dqr, 0, dkr[WA_BLK:WA_BLK + T], 0, cos, sin, rotm, True, f"rope_bwd{l}")
        dqb, dkb, dvb, dbias = na_bwd(P, s["kb"], s["vb"], s["bias"], dmixin, s["ob"], s["stb"], L, Lc, f"na_bwd{l}")
        dy, dxs1, dz, dvec = ssm_out_bwd(s["yf"], s["yb"], s["act"], P, s["dskip"], ssm_norm_g[l][None], dmixin, f"ssm_out_bwd{l}")
        dxf, dbf, dcf, ddf, dxb, dbb, dcb, ddb, ddtb, dal = ssd_bwd(s["act"], P, s["dtb8"], s["al8"], s["hsf"], s["hsb"], dy, L, Lc, f"ssd_bwd{l}")
        dpre = dsilu(s["pre"], [dxf, dxb, dxs1], [dbf, dbb], [dcf, dcb], f"dsilu{l}")
        dxbc, dw8, db8 = conv_bwd(dpre, P, s["w8"], nL, f"conv_bwd{l}")
        ddt = jnp.concatenate([ddf, ddb, jnp.zeros((T, IN_PAD - IN_COLS), F32)], axis=1)
        pieces = [(dqa, C_QA), (dqb, C_QB), (dz, C_Z), (dka, C_KA), (dva[WA_BLK:WA_BLK + T], C_VA), (dkb, C_KB), (dvb, C_VB),
                  (dxbc, C_XBC), (ddt, C_DT)]
        dh1, dwin = in_proj_bwd(pieces, s["h1"], W_in[l], f"in_proj_bwd{l}")
        cw = IN_COLS // N_CHIPS
        gW["w_in"][l] = jnp.stack([dwin[:, k * cw:(k + 1) * cw] for k in range(N_CHIPS)])
        garr = [gW[n][l] for n in _BIG]
        got = swap_halves(garr, f"reduce_d2d{l}")
        chip_sum = [add_halves(garr[a], got[a], cvec, f"reduce_add_pair{l}_{a}") for a in range(len(garr))]
        parts[l] = scatter_chips_sc(chip_sum, f"reduce_ici{l}")
        small[l] = dict(g_ffn=dg_ffn[0], wa_sink=dsink[:WA_HEADS, 0], na_rpb=na_rpb_grad(dbias, l), conv_w=dw8[:S_CONV], conv_b=db8[0],
                        dt_bias=ddtb[:2, :8], a_log=dal[:2, :8], ssm_d=dvec[0].reshape(S_HEADS, S_P).sum(axis=1), norm_g=dvec[1])
        if l > 0:
            p = sv[l - 1]
            dres, dfo, dgsv_end, dg_mix = res_norm_mod_bwd(s["xin"], p["fo"], p["gsv_end"], g_mix[l][None], dh1, dres, nL, f"norm_end_bwd{l - 1}")
        else:
            grad_x, _, dgsv_first, dg_mix = res_norm_mod_bwd(s["xin"], None, gsv_first, g_mix[0][None], dh1, dres, nL, "norm_first_bwd")
            dmod[0][0], dmod[0][1] = dgsv_first[:, 1], dgsv_first[:, 2]
        small[l]["g_mix"] = dg_mix[0]
    for l in range(DEPTH):
        for j in range(6):
            if dmod[l][j] is None:
                dmod[l][j] = jnp.zeros((2, D), F32)
    dmod = jnp.stack([jnp.stack(r, axis=1) for r in dmod])

    f3 = _Flat()
    f3.add("dmod_l", dmod[:, 0].reshape(DEPTH, 6 * D))
    f3.add("dmod_c", dmod[:, 1].reshape(DEPTH, 6 * D))
    f3.add("g_final", dg_final[0])
    for n in ("g_mix", "g_ffn", "wa_sink", "na_rpb", "conv_w", "conv_b", "dt_bias", "a_log", "ssm_d", "norm_g"):
        f3.add(n, jnp.stack([small[l][n] for l in range(DEPTH)]))
    g3, s3 = allgather8(f3.rows(), "reduce_small")
    dmod_all = f3.split_lead(g3)["dmod_l"]
    s3 = f3.split(s3)
    dmodc_tot = s3["dmod_c"]
    col0 = chip * MODW
    G16, G16c = [], []
    for l in range(DEPTH):
        rows = jnp.concatenate([dmod_all[:, l], dmodc_tot[l][None], jnp.zeros((7, 6 * D), F32)], axis=0)
        G16.append(lax.dynamic_slice_in_dim(rows, col0, MODW, axis=1))
        rc = jnp.concatenate([dmodc_tot[l][None], jnp.zeros((15, 6 * D), F32)], axis=0)
        G16c.append(lax.dynamic_slice_in_dim(rc, col0, MODW, axis=1))
    grad_w_mod = matmul_layers(A16, jnp.stack(G16), "tn", "mod_dw")
    dscc_part = matmul_layers(jnp.stack(G16c), w_mod, "nt", "mod_dx")[:, 0].sum(axis=0)
    _, s4 = allgather8(_pad_rows(dscc_part * (mc == 1).astype(F32)), "reduce_cctx")
    dscc = s4.reshape(-1)[:D]
    cc = c_ctx.astype(F32)
    sg = 1.0 / (1.0 + jnp.exp(-cc))
    grad_c_ctx = dscc * (sg * (1.0 + cc * (1.0 - sg)))

    halves = [[sum_slots(parts[l][i], f"reduce_add_chips{l}_{i}") for l in range(DEPTH)] for i in range(len(_BIG))]
    gsh = dict(zip(_BIG, share_halves(halves, "reduce_share")))

    grads = {"c_ctx": grad_c_ctx, "w_mod": grad_w_mod, "b_mod": s3["dmod_l"] + s3["dmod_c"], "g_mix": s3["g_mix"], "w_in": gsh["w_in"],
             "wa_sink": s3["wa_sink"], "na_rpb": s3["na_rpb"],
             "ssm_conv_w": lax.dynamic_slice_in_dim(s3["conv_w"], chip * CW, CW, axis=2), "ssm_conv_b": s3["conv_b"],
             "ssm_dt_bias": s3["dt_bias"], "ssm_a_log": s3["a_log"], "ssm_d": s3["ssm_d"], "ssm_norm_g": s3["norm_g"],
             "w_out": gsh["w_out"], "g_ffn": s3["g_ffn"], "w_ffn_in": gsh["w_ffn_in"], "w_ffn_out": gsh["w_ffn_out"], "g_final": s3["g_final"]}
    wts = {"c_ctx": c_ctx, "w_mod": w_mod, "b_mod": b_mod, "g_mix": g_mix, "w_in": w_in, "wa_sink": wa_sink, "na_rpb": na_rpb,
           "ssm_conv_w": ssm_conv_w, "ssm_conv_b": ssm_conv_b, "ssm_dt_bias": ssm_dt_bias, "ssm_a_log": ssm_a_log, "ssm_d": ssm_d,
           "ssm_norm_g": ssm_norm_g, "w_out": w_out, "g_ffn": g_ffn, "w_ffn_in": w_ffn_in, "w_ffn_out": w_ffn_out, "g_final": g_final}
    ms = {"c_ctx": m_c_ctx, "w_mod": m_w_mod, "b_mod": m_b_mod, "g_mix": m_g_mix, "w_in": m_w_in, "wa_sink": m_wa_sink, "na_rpb": m_na_rpb,
          "ssm_conv_w": m_ssm_conv_w, "ssm_conv_b": m_ssm_conv_b, "ssm_dt_bias": m_ssm_dt_bias, "ssm_a_log": m_ssm_a_log, "ssm_d": m_ssm_d,
          "ssm_norm_g": m_ssm_norm_g, "w_out": m_w_out, "g_ffn": m_g_ffn, "w_ffn_in": m_w_ffn_in, "w_ffn_out": m_w_ffn_out, "g_final": m_g_final}
    vs = {"c_ctx": v_c_ctx, "w_mod": v_w_mod, "b_mod": v_b_mod, "g_mix": v_g_mix, "w_in": v_w_in, "wa_sink": v_wa_sink, "na_rpb": v_na_rpb,
          "ssm_conv_w": v_ssm_conv_w, "ssm_conv_b": v_ssm_conv_b, "ssm_dt_bias": v_ssm_dt_bias, "ssm_a_log": v_ssm_a_log, "ssm_d": v_ssm_d,
          "ssm_norm_g": v_ssm_norm_g, "w_out": v_w_out, "g_ffn": v_g_ffn, "w_ffn_in": v_w_ffn_in, "w_ffn_out": v_w_ffn_out, "g_final": v_g_final}
    names = list(wts)
    grads = {n: grads[n].reshape(wts[n].shape).astype(F32) for n in names}
    big = ("w_mod", "w_in", "w_out", "w_ffn_in", "w_ffn_out")
    delta, new_m, new_v = {}, {}, {}
    for n in big:
        delta[n], new_m[n], new_v[n] = adamw(wts[n], grads[n], ms[n], vs[n], f"adamw_{n}")
    packs = []
    for src in (wts, grads, ms, vs):
        f = _Flat()
        for n in names:
            if n not in big:
                f.add(n, src[n])
        packs.append(f)
    d_, m_, v_ = adamw(*[f.rows()[None] for f in packs], "adamw_small")
    for dst, rows in ((delta, d_), (new_m, m_), (new_v, v_)):
        dst.update(packs[0].split(rows[0]))

    return (loss, grad_x[:L][None], *[grads[n] for n in names], *[delta[n] for n in names],
            *[new_m[n] for n in names], *[new_v[n] for n in names])
```

```python
import functools

import numpy as np
import jax
import jax.numpy as jnp
from jax import lax
from jax.experimental import pallas as pl
from jax.experimental.pallas import tpu as pltpu
from jax.experimental.pallas import tpu_sc as plsc

F32 = jnp.float32
BF16 = jnp.bfloat16
_MXU = jnp.bfloat16
_HI = lax.Precision.HIGHEST
MESH = pl.DeviceIdType.MESH

D = 1024
HD = 64
GRID_W = 64
EPS = 1e-6
ROPE_BASE = 10000.0
WA_HEADS, WA_KV = 4, 2
WA_BLK = 128
NA_HEADS, NA_KH, NA_KW = 4, 8, 16
S_HEADS, S_P, S_INNER, S_GROUPS, S_N, S_CONV, S_Q = 8, 64, 512, 2, 128, 7, 128
D_FF = 2816
IN_COLS = 2832
IN_PAD = 2944
C_QA, C_QB, C_Z, C_KA, C_VA, C_KB, C_VB, C_XBC, C_DT = 0, 256, 512, 1024, 1152, 1280, 1536, 1792, 2816
ADAM_LR, ADAM_B1, ADAM_B2, ADAM_EPS, ADAM_WD, ADAM_STEP = 0.001, 0.9, 0.999, 1e-08, 0.01, 10

TR = 256
NEG = -1e30
VMEM_CAP = 56 * 1024 * 1024


PIN_BYTES = 256 * 1024


def _is_big(a):
    return hasattr(a, "shape") and len(a.shape) >= 2 and int(np.prod(a.shape)) * jnp.dtype(a.dtype).itemsize >= PIN_BYTES


def _pc(body, *, out_shape, pin=True, **kw):
    if not pin:
        return pl.pallas_call(body, out_shape=out_shape, **kw)
    one = isinstance(out_shape, jax.ShapeDtypeStruct)
    outs = [pltpu.HBM(s.shape, s.dtype) if _is_big(s) else s for s in ([out_shape] if one else out_shape)]
    call = pl.pallas_call(body, out_shape=outs[0] if one else outs, **kw)
    return lambda *args: call(*[pltpu.with_memory_space_constraint(a, pltpu.HBM) if _is_big(a) else a for a in args])


def _cp(sem=None, vmem=None):
    kw = {}
    if sem is not None:
        kw["dimension_semantics"] = sem
    if vmem is not None:
        kw["vmem_limit_bytes"] = int(min(max(vmem, 16 * 1024 * 1024), VMEM_CAP))
    return pltpu.CompilerParams(**kw)


def _sds(shape, dtype):
    return jax.ShapeDtypeStruct(tuple(shape), dtype)


_DIMS = {"nn": ((1,), (0,)), "nt": ((1,), (1,)), "tn": ((0,), (0,))}


def _dg(a, b, dims):
    return lax.dot_general(a.astype(_MXU), b.astype(_MXU), (dims, ((), ())), preferred_element_type=F32)


@functools.partial(jax.custom_vjp, nondiff_argnums=(2,))
def bdot(a, b, mode):
    return _dg(a, b, _DIMS[mode])


def _bdot_fwd(a, b, mode):
    return bdot(a, b, mode), (a, b)


def _bdot_bwd(mode, res, g):
    a, b = res
    if mode == "nn":
        return bdot(g, b, "nt"), bdot(a, g, "tn")
    if mode == "nt":
        return bdot(g, b, "nn"), bdot(g, a, "tn")
    return bdot(b, g, "nt"), bdot(a, g, "nn")


bdot.defvjp(_bdot_fwd, _bdot_bwd)


def hdot(a, b, mode="nn"):
    return lax.dot_general(a, b, (_DIMS[mode], ((), ())), precision=_HI, preferred_element_type=F32)


def _silu(x):
    return x / (1.0 + jnp.exp(-x))


def _softplus(x):
    return jnp.maximum(x, 0.0) + jnp.log(1.0 + jnp.exp(-jnp.abs(x)))


def _div_tile(n, cap, mult):
    if n <= cap:
        return n
    best = None
    for t in range(mult, cap + 1, mult):
        if n % t == 0:
            best = t
    assert best is not None, (n, cap, mult)
    return best


def matmul(a, b, mode, out_dtype, name, tm=640, tn=1536, tk=1408, hi=False):
    if mode == "tn":
        K, M = a.shape
    else:
        M, K = a.shape
    N = b.shape[0] if mode == "nt" else b.shape[1]
    tm = _div_tile(M, tm, 128 if mode == "tn" else 16)
    tn = _div_tile(N, tn, 128)
    tk = _div_tile(K, tk, 128 if mode != "tn" else 16)
    nk = K // tk
    dims = _DIMS[mode]

    def body(a_ref, b_ref, o_ref, *acc):
        if hi:
            part = lax.dot_general(a_ref[...], b_ref[...], (dims, ((), ())), precision=_HI, preferred_element_type=F32)
        else:
            part = _dg(a_ref[...], b_ref[...], dims)
        if nk == 1:
            o_ref[...] = part.astype(o_ref.dtype)
        else:
            k = pl.program_id(2)

            @pl.when(k == 0)
            def _():
                acc[0][...] = part

            @pl.when(k > 0)
            def _():
                acc[0][...] += part

            @pl.when(k == nk - 1)
            def _():
                o_ref[...] = acc[0][...].astype(o_ref.dtype)

    if mode == "tn":
        a_spec = pl.BlockSpec((tk, tm), lambda i, j, k: (k, i))
    else:
        a_spec = pl.BlockSpec((tm, tk), lambda i, j, k: (i, k))
    if mode == "nt":
        b_spec = pl.BlockSpec((tn, tk), lambda i, j, k: (j, k))
    else:
        b_spec = pl.BlockSpec((tk, tn), lambda i, j, k: (k, j))
    isz = lambda x: jnp.dtype(x.dtype).itemsize
    vmem = 2 * (tm * tk * isz(a) + tk * tn * isz(b) + tm * tn * jnp.dtype(out_dtype).itemsize) + 3 * tm * tn * 4
    return _pc(
        body, name=name, grid=(M // tm, N // tn, nk),
        in_specs=[a_spec, b_spec], out_specs=pl.BlockSpec((tm, tn), lambda i, j, k: (i, j)),
        out_shape=_sds((M, N), out_dtype),
        scratch_shapes=[pltpu.VMEM((tm, tn), F32)] if nk > 1 else [],
        compiler_params=_cp(("parallel", "parallel", "arbitrary"), vmem + (8 << 20)),
    )(a, b)


def matmul_layers(a, b, mode, name):
    nl = b.shape[0]
    a3 = a if a.ndim == 3 else a[None]
    shared = a3.shape[0] == 1
    M = a3.shape[2] if mode == "tn" else a3.shape[1]
    N = b.shape[1] if mode == "nt" else b.shape[2]

    def body(a_ref, b_ref, o_ref):
        o_ref[0] = _dg(a_ref[0], b_ref[0], _DIMS[mode])

    return _pc(body, name=name, grid=(nl,),
               in_specs=[pl.BlockSpec((1,) + a3.shape[1:], (lambda l: (0, 0, 0)) if shared else (lambda l: (l, 0, 0))),
                         pl.BlockSpec((1,) + b.shape[1:], lambda l: (l, 0, 0))],
               out_specs=pl.BlockSpec((1, M, N), lambda l: (l, 0, 0)), out_shape=_sds((nl, M, N), F32),
               compiler_params=_cp(("parallel",), 48 << 20))(a3, b)


def out_proj_fwd(pieces, w, name):
    T = pieces[0][0].shape[0]
    arrs, offs = [a for a, _ in pieces], [o for _, o in pieces]
    n = len(arrs)
    tm = _div_tile(T, 640, 16)

    def body(*refs):
        w_ref, o_ref = refs[n], refs[n + 1]
        acc = None
        for j in range(n):
            part = _dg(refs[j][...], w_ref[offs[j]:offs[j] + arrs[j].shape[1], :], _DIMS["nn"])
            acc = part if acc is None else acc + part
        o_ref[...] = acc.astype(o_ref.dtype)

    return _pc(body, name=name, grid=(T // tm,),
               in_specs=[pl.BlockSpec((tm, a.shape[1]), lambda i: (i, 0)) for a in arrs] + [pl.BlockSpec(w.shape, lambda i: (0, 0))],
               out_specs=pl.BlockSpec((tm, w.shape[1]), lambda i: (i, 0)), out_shape=_sds((T, w.shape[1]), BF16),
               compiler_params=_cp(("parallel",), 32 << 20))(*arrs, w)


def out_proj_dw(pieces, dy, name):
    T, N = dy.shape
    arrs, offs = [a for a, _ in pieces], [o for _, o in pieces]
    n = len(arrs)
    rows = sum(a.shape[1] for a in arrs)
    tn = 512

    def body(*refs):
        d_ref, o_ref = refs[n], refs[n + 1]
        for j in range(n):
            o_ref[offs[j]:offs[j] + arrs[j].shape[1], :] = _dg(refs[j][...], d_ref[...], _DIMS["tn"]).astype(o_ref.dtype)

    return _pc(body, name=name, grid=(N // tn,),
               in_specs=[pl.BlockSpec(a.shape, lambda j: (0, 0)) for a in arrs] + [pl.BlockSpec((T, tn), lambda j: (0, j))],
               out_specs=pl.BlockSpec((rows, tn), lambda j: (0, j)), out_shape=_sds((rows, N), BF16),
               compiler_params=_cp(("parallel",), 48 << 20))(*arrs, dy)


def in_proj_bwd(pieces, h1, w, name):
    T = h1.shape[0]
    arrs = [a for a, _ in pieces]
    offs = [o for _, o in pieces]
    wid = [a.shape[1] for a in arrs]
    n = len(arrs)
    assert sum(wid) == IN_PAD, "the pieces must tile all columns of P"
    tm = _div_tile(T, 640, 16)

    def dx_body(*refs):
        w_ref, o_ref = refs[n], refs[n + 1]
        acc = None
        for j in range(n):
            part = _dg(refs[j][...], w_ref[:, offs[j]:offs[j] + wid[j]], _DIMS["nt"])
            acc = part if acc is None else acc + part
        o_ref[...] = acc.astype(o_ref.dtype)

    dh1 = _pc(dx_body, name=name + "_dx", grid=(T // tm,),
              in_specs=[pl.BlockSpec((tm, wj), lambda i: (i, 0)) for wj in wid] + [pl.BlockSpec((D, IN_PAD), lambda i: (0, 0))],
              out_specs=pl.BlockSpec((tm, D), lambda i: (i, 0)), out_shape=_sds((T, D), BF16),
              compiler_params=_cp(("parallel",), 40 << 20))(*arrs, w)

    tmd, nk = 512, 4
    tk = T // nk

    def dw_body(h_ref, *refs):
        o_ref, acc = refs[n], refs[n + 1]
        k = pl.program_id(1)

        @pl.when(k == 0)
        def _():
            acc[...] = jnp.zeros_like(acc)

        for j in range(n):
            acc[:, offs[j]:offs[j] + wid[j]] += _dg(h_ref[...], refs[j][...], _DIMS["tn"])

        @pl.when(k == nk - 1)
        def _():
            o_ref[...] = acc[...].astype(o_ref.dtype)

    dw = _pc(dw_body, name=name + "_dw", grid=(D // tmd, nk),
             in_specs=[pl.BlockSpec((tk, tmd), lambda i, k: (k, i))] + [pl.BlockSpec((tk, wj), lambda i, k: (k, 0)) for wj in wid],
             out_specs=pl.BlockSpec((tmd, IN_PAD), lambda i, k: (i, 0)), out_shape=_sds((D, IN_PAD), BF16),
             scratch_shapes=[pltpu.VMEM((tmd, IN_PAD), F32)], compiler_params=_cp(("parallel", "arbitrary"), 48 << 20))(h1, *arrs)
    return dh1, dw


def _norm_mod(xo, shift, scale, g):
    r = lax.rsqrt(jnp.mean(xo * xo, axis=-1, keepdims=True) + EPS)
    return (xo * r) * g * (1.0 + scale) + shift


def res_norm_mod(x, y, gsv, g, nL, name):
    T = x.shape[0]
    has_y = y is not None

    def body(*refs):
        if has_y:
            x_ref, y_ref, gsv_ref, g_ref, xo_ref, h_ref = refs
            xo = x_ref[...] + gsv_ref[0, 0:1, :] * y_ref[...]
            xo_ref[...] = xo
        else:
            x_ref, gsv_ref, g_ref, h_ref = refs
            xo = x_ref[...]
        h_ref[...] = _norm_mod(xo, gsv_ref[0, 1:2, :], gsv_ref[0, 2:3, :], g_ref[...]).astype(h_ref.dtype)

    row = pl.BlockSpec((TR, D), lambda i: (i, 0))
    in_specs = [row] + ([row] if has_y else []) + [pl.BlockSpec((1, 8, D), lambda i: (i // nL, 0, 0)),
                                                     pl.BlockSpec((1, D), lambda i: (0, 0))]
    out_specs = ([row] if has_y else []) + [row]
    out_shape = ([_sds((T, D), F32)] if has_y else []) + [_sds((T, D), BF16)]
    args = (x, y, gsv, g) if has_y else (x, gsv, g)
    outs = _pc(body, name=name, grid=(T // TR,), in_specs=in_specs, out_specs=out_specs, out_shape=out_shape,
               compiler_params=_cp(("arbitrary",), 24 << 20))(*args)
    return (outs[0], outs[1]) if has_y else (None, outs[0])


def res_norm_mod_bwd(xo, y, gsv, g, dh, dres, nL, name):
    T = xo.shape[0]
    has_y = y is not None

    def body(*refs):
        if has_y:
            xo_ref, y_ref, gsv_ref, g_ref, dh_ref, dres_ref, dx_ref, dy_ref, dgsv_ref, dg_ref = refs
        else:
            xo_ref, gsv_ref, g_ref, dh_ref, dres_ref, dx_ref, dgsv_ref, dg_ref = refs
        i = pl.program_id(0)

        @pl.when((i == 0) | (i == nL))
        def _():
            dgsv_ref[...] = jnp.zeros_like(dgsv_ref)

        @pl.when(i == 0)
        def _():
            dg_ref[...] = jnp.zeros_like(dg_ref)

        _, vjp = jax.vjp(_norm_mod, xo_ref[...], gsv_ref[0, 1:2, :], gsv_ref[0, 2:3, :], g_ref[...])
        dxn, dshift, dscale, dg = vjp(dh_ref[...].astype(F32))
        dxo = dres_ref[...] + dxn
        dx_ref[...] = dxo
        if has_y:
            dy_ref[...] = (gsv_ref[0, 0:1, :] * dxo).astype(dy_ref.dtype)
            dgsv_ref[0, 0:1, :] += jnp.sum(y_ref[...] * dxo, axis=0, keepdims=True)
        dgsv_ref[0, 1:2, :] += dshift
        dgsv_ref[0, 2:3, :] += dscale
        dg_ref[0:1, :] += dg

    row = pl.BlockSpec((TR, D), lambda i: (i, 0))
    gspec = pl.BlockSpec((1, 8, D), lambda i: (i // nL, 0, 0))
    in_specs = [row] + ([row] if has_y else []) + [gspec, pl.BlockSpec((1, D), lambda i: (0, 0)), row, row]
    out_specs = [row] + ([row] if has_y else []) + [gspec, pl.BlockSpec((8, D), lambda i: (0, 0))]
    out_shape = [_sds((T, D), F32)] + ([_sds((T, D), BF16)] if has_y else []) + [_sds((2, 8, D), F32), _sds((8, D), F32)]
    args = (xo, y, gsv, g, dh, dres) if has_y else (xo, gsv, g, dh, dres)
    outs = _pc(body, name=name, grid=(T // TR,), in_specs=in_specs, out_specs=out_specs, out_shape=out_shape,
               compiler_params=_cp(("arbitrary",), 32 << 20))(*args)
    if has_y:
        return outs
    return outs[0], None, outs[1], outs[2]


def final_loss(x, y, gsv, g, target, nL, name):
    T = x.shape[0]

    def lossf(xo, gv, t):
        yn = (xo * lax.rsqrt(jnp.mean(xo * xo, axis=-1, keepdims=True) + EPS)) * gv
        e = yn - t
        return 0.5 * jnp.sum(jnp.sum(e * e, axis=-1, keepdims=True) * (1.0 / D), axis=0, keepdims=True)

    def body(x_ref, y_ref, gsv_ref, g_ref, t_ref, loss_ref, dx_ref, dy_ref, dgsv_ref, dg_ref):
        i = pl.program_id(0)

        @pl.when(i == 0)
        def _():
            loss_ref[...] = jnp.zeros_like(loss_ref)
            dg_ref[...] = jnp.zeros_like(dg_ref)

        @pl.when((i == 0) | (i == nL))
        def _():
            dgsv_ref[...] = jnp.zeros_like(dgsv_ref)

        @pl.when(i < nL)
        def _():
            gate = gsv_ref[0, 0:1, :]
            yv = y_ref[...]
            xo = x_ref[...] + gate * yv
            lv, vjp = jax.vjp(lossf, xo, g_ref[...], t_ref[...])
            dxo, dg, _ = vjp(jnp.ones((1, 1), F32))
            loss_ref[...] += jnp.broadcast_to(lv, loss_ref.shape)
            dx_ref[...] = dxo
            dy_ref[...] = (gate * dxo).astype(dy_ref.dtype)
            dgsv_ref[0, 0:1, :] += jnp.sum(yv * dxo, axis=0, keepdims=True)
            dg_ref[0:1, :] += dg

        @pl.when(i >= nL)
        def _():
            dx_ref[...] = jnp.zeros_like(dx_ref)
            dy_ref[...] = jnp.zeros_like(dy_ref)

    row = pl.BlockSpec((TR, D), lambda i: (i, 0))
    gspec = pl.BlockSpec((1, 8, D), lambda i: (i // nL, 0, 0))
    return _pc(
        body, name=name, grid=(T // TR,),
        in_specs=[row, row, gspec, pl.BlockSpec((1, D), lambda i: (0, 0)),
                  pl.BlockSpec((TR, D), lambda i: (jnp.minimum(i, nL - 1), 0))],
        out_specs=[pl.BlockSpec((8, 128), lambda i: (0, 0)), row, row, gspec, pl.BlockSpec((8, D), lambda i: (0, 0))],
        out_shape=[_sds((8, 128), F32), _sds((T, D), F32), _sds((T, D), BF16), _sds((2, 8, D), F32), _sds((8, D), F32)],
        compiler_params=_cp(("arbitrary",), 32 << 20),
    )(x, y, gsv, g, target)


FI_BLK = 2 * D_FF // 4


def _fi_chip(j):
    return (j % 2) * 2 + j // 2


def matmul_fi(a, b, mode, out_dtype, name):
    T = a.shape[0]
    if mode == "tn":
        tmd = 512

        def body(a_ref, b_ref, o_ref):
            o_ref[0] = _dg(a_ref[...], b_ref[...], _DIMS["tn"]).astype(o_ref.dtype)

        return _pc(body, name=name, grid=(D // tmd, 4),
                   in_specs=[pl.BlockSpec((T, tmd), lambda i, j: (0, i)), pl.BlockSpec((T, FI_BLK), lambda i, j: (0, j))],
                   out_specs=pl.BlockSpec((1, tmd, FI_BLK), lambda i, j: (_fi_chip(j), i, 0)),
                   out_shape=_sds((4, D, FI_BLK), out_dtype), compiler_params=_cp(("parallel", "arbitrary"), 48 << 20))(a, b)
    if mode == "nn":
        tm = _div_tile(T, 1280, 16)

        def body(a_ref, b_ref, o_ref):
            o_ref[...] = _dg(a_ref[...], b_ref[0], _DIMS["nn"]).astype(o_ref.dtype)

        return _pc(body, name=name, grid=(T // tm, 4),
                   in_specs=[pl.BlockSpec((tm, D), lambda i, j: (i, 0)), pl.BlockSpec((1, D, FI_BLK), lambda i, j: (_fi_chip(j), 0, 0))],
                   out_specs=pl.BlockSpec((tm, FI_BLK), lambda i, j: (i, j)), out_shape=_sds((T, 4 * FI_BLK), out_dtype),
                   compiler_params=_cp(("parallel", "arbitrary"), 40 << 20))(a, b)
    tm = _div_tile(T, 640, 16)

    def body(a_ref, b_ref, o_ref):
        acc = None
        for k in range(4):
            part = _dg(a_ref[:, k * FI_BLK:(k + 1) * FI_BLK], b_ref[_fi_chip(k)], _DIMS["nt"])
            acc = part if acc is None else acc + part
        o_ref[...] = acc.astype(o_ref.dtype)

    return _pc(body, name=name, grid=(T // tm,),
               in_specs=[pl.BlockSpec((tm, 4 * FI_BLK), lambda i: (i, 0)), pl.BlockSpec((4, D, FI_BLK), lambda i: (0, 0, 0))],
               out_specs=pl.BlockSpec((tm, D), lambda i: (i, 0)), out_shape=_sds((T, D), out_dtype),
               compiler_params=_cp(("parallel",), VMEM_CAP))(a, b)


def _swiglu(gate, up):
    return _silu(gate) * up


def swiglu_fwd(gu, name):
    T = gu.shape[0]

    def body(x_ref, o_ref):
        o_ref[...] = _swiglu(x_ref[:, :FI_BLK].astype(F32), x_ref[:, FI_BLK:].astype(F32)).astype(o_ref.dtype)

    return _pc(body, name=name, grid=(T // TR, 2), in_specs=[pl.BlockSpec((TR, 2 * FI_BLK), lambda i, j: (i, j))],
               out_specs=pl.BlockSpec((TR, FI_BLK), lambda i, j: (i, j)), out_shape=_sds((T, D_FF), BF16),
               compiler_params=_cp(("parallel", "parallel"), 24 << 20))(gu)


def swiglu_bwd(gu, dact, name):
    T = gu.shape[0]

    def body(x_ref, d_ref, o_ref):
        g, u, d = x_ref[:, :FI_BLK].astype(F32), x_ref[:, FI_BLK:].astype(F32), d_ref[...].astype(F32)
        sg = 1.0 / (1.0 + jnp.exp(-g))
        sl = g * sg
        o_ref[:, :FI_BLK] = (d * u * (sg + sl * (1.0 - sg))).astype(o_ref.dtype)
        o_ref[:, FI_BLK:] = (d * sl).astype(o_ref.dtype)

    return _pc(body, name=name, grid=(T // TR, 2),
               in_specs=[pl.BlockSpec((TR, 2 * FI_BLK), lambda i, j: (i, j)), pl.BlockSpec((TR, FI_BLK), lambda i, j: (i, j))],
               out_specs=pl.BlockSpec((TR, 2 * FI_BLK), lambda i, j: (i, j)), out_shape=_sds((T, 2 * D_FF), BF16),
               compiler_params=_cp(("parallel", "parallel"), 32 << 20))(gu, dact)


def rope_tables(L, Lc):
    t = np.arange(L)
    rows, cols = t // GRID_W, t % GRID_W
    inv = ROPE_BASE ** (-np.arange(16, dtype=np.float32) / 16)
    lane = np.arange(64)
    pos = np.where((lane // 32)[None, :] == 0, rows[:, None], cols[:, None]).astype(np.float32)
    ang = jnp.asarray(pos) * jnp.asarray(inv[lane % 16])[None, :]
    cos = jnp.concatenate([jnp.cos(ang), jnp.ones((Lc, 64), F32)], axis=0)
    sin = jnp.concatenate([jnp.sin(ang), jnp.zeros((Lc, 64), F32)], axis=0)
    R = np.zeros((128, 128), np.float32)
    for i in range(128):
        if (i % 32) < 16:
            R[i + 16, i] = -1.0
        else:
            R[i - 16, i] = 1.0
    return jnp.tile(cos, (1, 2)), jnp.tile(sin, (1, 2)), jnp.asarray(R)


def rope_apply(q_src, q_col, k_src, k_col, cos, sin, R, transpose, name, kv_src=None):
    T = cos.shape[0]
    with_kv = kv_src is not None

    def rot(x, c, s, Rm):
        if transpose:
            return x * c + hdot(x * s, Rm, "nt")
        return x * c + hdot(x, Rm) * s

    def body(q_ref, k_ref, c_ref, s_ref, R_ref, *rest):
        qo_ref, ko_ref = rest[-4:-2] if with_kv else rest
        c, s, Rm = c_ref[...], s_ref[...], R_ref[...]
        for j in range(2):
            qo_ref[:, j * 128:(j + 1) * 128] = rot(q_ref[:, j * 128:(j + 1) * 128].astype(F32), c, s, Rm).astype(qo_ref.dtype)
        ko_ref[...] = rot(k_ref[...].astype(F32), c, s, Rm).astype(ko_ref.dtype)
        if with_kv:
            rest[-2][...] = rest[0][...].astype(BF16)
            rest[-1][...] = rest[1][...].astype(BF16)

    tab = pl.BlockSpec((TR, 128), lambda i: (i, 0))
    wide = pl.BlockSpec((TR, 256), lambda i: (i, 0))
    kv_in = [pl.BlockSpec((TR, 256), lambda i: (i, C_KB // 256)), pl.BlockSpec((TR, 256), lambda i: (i, C_VB // 256))] if with_kv else []
    return _pc(body, name=name, grid=(T // TR,),
               in_specs=[pl.BlockSpec((TR, 256), lambda i: (i, q_col)), pl.BlockSpec((TR, 128), lambda i: (i, k_col)),
                         tab, tab, pl.BlockSpec((128, 128), lambda i: (0, 0))] + kv_in,
               out_specs=[wide, tab] + ([wide, wide] if with_kv else []),
               out_shape=[_sds((T, 256), BF16), _sds((T, 128), BF16)] + ([_sds((T, 256), BF16)] * 2 if with_kv else []),
               compiler_params=_cp(("parallel",), 16 << 20))(q_src, k_src, cos, sin, R, *([kv_src, kv_src] if with_kv else []))


_SCALE = HD ** -0.5


def _attn_tile(qh, ks, vs, extra):
    ss = []
    for k, add in ks:
        s = _dg(qh, k, _DIMS["nt"]) * _SCALE
        ss.append(s if add is None else s + add)
    m = ss[0].max(axis=-1, keepdims=True)
    for s in ss[1:]:
        m = jnp.maximum(m, s.max(axis=-1, keepdims=True))
    if extra is not None:
        m = jnp.maximum(m, extra)
    ps = [jnp.exp(s - m) for s in ss]
    den = ps[0].sum(axis=-1, keepdims=True)
    for p in ps[1:]:
        den = den + p.sum(axis=-1, keepdims=True)
    if extra is not None:
        den = den + jnp.exp(extra - m)
    num = _dg(ps[0], vs[0], _DIMS["nn"])
    for p, v in zip(ps[1:], vs[1:]):
        num = num + _dg(p, v, _DIMS["nn"])
    linv = 1.0 / den
    return num * linv, m, linv


def _attn_bwd_tile(qh, ks, vs, extra, m, linv, oh, doh):
    delta = jnp.sum(doh * oh, axis=-1, keepdims=True)
    dq = None
    dks, dvs, dss = [], [], []
    for (k, add), v in zip(ks, vs):
        s = _dg(qh, k, _DIMS["nt"]) * _SCALE
        if add is not None:
            s = s + add
        p = jnp.exp(s - m) * linv
        dvs.append(_dg(p, doh, _DIMS["tn"]))
        ds = p * (_dg(doh, v, _DIMS["nt"]) - delta)
        dss.append(ds)
        dsq = ds * _SCALE
        part = _dg(dsq, k, _DIMS["nn"])
        dq = part if dq is None else dq + part
        dks.append(_dg(dsq, qh, _DIMS["tn"]))
    dextra = None
    if extra is not None:
        dextra = -jnp.sum(jnp.exp(extra - m) * linv * delta, axis=0, keepdims=True)
    return dq, dks, dvs, dss, dextra


def _wa_mask(n, L):
    qpos = n * WA_BLK + lax.broadcasted_iota(jnp.int32, (WA_BLK, 3 * WA_BLK), 0)
    kpos = (n - 1) * WA_BLK + lax.broadcasted_iota(jnp.int32, (WA_BLK, 3 * WA_BLK), 1)
    ok = (jnp.abs(qpos - kpos) <= WA_BLK) & (kpos >= 0) & (kpos < L)
    return jnp.where(ok, 0.0, NEG).astype(F32)


WA_BPS = 2


def _wa_specs(L, Lc):
    nb = L // WA_BLK
    cb = L // Lc

    def blk(j, col):
        return pl.BlockSpec((WA_BLK, 128), lambda s: (jnp.clip(s * WA_BPS - 1 + j, 0, nb - 1), col))

    vcol = C_VA // 128
    kspecs = [blk(j, 0) for j in range(WA_BPS + 2)] + [pl.BlockSpec((Lc, 128), lambda s: (cb, 0))]
    vspecs = [blk(j, vcol) for j in range(WA_BPS + 2)] + [pl.BlockSpec((Lc, 128), lambda s: (cb, vcol))]
    return nb, kspecs, vspecs


def win_attn_fwd(qr, kr, P, sink, L, Lc, name):
    T = L + Lc
    nb, kspecs, vspecs = _wa_specs(L, Lc)
    nk = WA_BPS + 2
    QB = WA_BPS * WA_BLK
    nlat = nb // WA_BPS

    def body(q_ref, *refs):
        kbs, kx, vbs, vx, s_ref, o_ref, st_ref = refs[:nk], refs[nk], refs[nk + 1:2 * nk + 1], refs[2 * nk + 1], refs[-3], refs[-2], refs[-1]
        s = pl.program_id(0)

        def put(qs, h, res):
            o, m, linv = res
            o_ref[qs, h * HD:(h + 1) * HD] = o.astype(o_ref.dtype)
            st_ref[qs, h:h + 1] = m
            st_ref[qs, WA_HEADS + h:WA_HEADS + h + 1] = linv

        @pl.when(s < nlat)
        def _():
            for b in range(WA_BPS):
                mask = _wa_mask(s * WA_BPS + b, L)
                qs = slice(b * WA_BLK, (b + 1) * WA_BLK)
                for g in range(WA_KV):
                    sl = slice(g * HD, (g + 1) * HD)
                    k3 = jnp.concatenate([kbs[b + j][:, sl] for j in range(3)], axis=0)
                    v3 = jnp.concatenate([vbs[b + j][:, sl] for j in range(3)], axis=0)
                    for r in range(2):
                        h = 2 * g + r
                        put(qs, h, _attn_tile(q_ref[qs, h * HD:(h + 1) * HD], [(k3, mask), (kx[:, sl], None)], [v3, vx[:, sl]], s_ref[h:h + 1, 0:1]))

        @pl.when(s >= nlat)
        def _():
            for h in range(WA_HEADS):
                sl = slice((h // 2) * HD, (h // 2 + 1) * HD)
                put(slice(None), h, _attn_tile(q_ref[:, h * HD:(h + 1) * HD], [(kx[:, sl], None)], [vx[:, sl]], s_ref[h:h + 1, 0:1]))

    qspec = pl.BlockSpec((QB, 256), lambda s: (s, 0))
    return _pc(body, name=name, grid=(T // QB,),
               in_specs=[qspec] + kspecs + vspecs + [pl.BlockSpec((8, 128), lambda s: (0, 0))],
               out_specs=[qspec, pl.BlockSpec((QB, 8), lambda s: (s, 0))], out_shape=[_sds((T, 256), BF16), _sds((T, 8), F32)],
               compiler_params=_cp(("arbitrary",), 32 << 20))(qr, *([kr] * (nk + 1)), *([P] * (nk + 1)), sink)


def win_attn_bwd(qr, kr, P, sink, do_src, o, stats, L, Lc, name):
    T = L + Lc
    nb, kspecs, vspecs = _wa_specs(L, Lc)
    nk = WA_BPS + 2
    QB = WA_BPS * WA_BLK
    nlat = nb // WA_BPS
    cx = WA_BLK + L

    def body(q_ref, *refs):
        kbs, kx, vbs, vx = refs[:nk], refs[nk], refs[nk + 1:2 * nk + 1], refs[2 * nk + 1]
        s_ref, do_ref, o_ref, st_ref, dq_ref, dk_ref, dv_ref, ds_ref = refs[2 * nk + 2:]
        s = pl.program_id(0)

        @pl.when(s == 0)
        def _():
            dk_ref[...] = jnp.zeros_like(dk_ref)
            dv_ref[...] = jnp.zeros_like(dv_ref)
            ds_ref[...] = jnp.zeros_like(ds_ref)

        def tile(qs, h, ks, vs):
            hs = slice(h * HD, (h + 1) * HD)
            dq, dks, dvs, _, dsk = _attn_bwd_tile(q_ref[qs, hs], ks, vs, s_ref[h:h + 1, 0:1], st_ref[qs, h:h + 1],
                                                  st_ref[qs, WA_HEADS + h:WA_HEADS + h + 1], o_ref[qs, hs].astype(F32), do_ref[qs, hs].astype(F32))
            dq_ref[qs, hs] = dq
            ds_ref[h:h + 1, :] += jnp.broadcast_to(dsk, (1, 128))
            return dks, dvs

        @pl.when(s < nlat)
        def _():
            for b in range(WA_BPS):
                n = s * WA_BPS + b
                mask = _wa_mask(n, L)
                rows = pl.ds(pl.multiple_of(n * WA_BLK, WA_BLK), 3 * WA_BLK)
                qs = slice(b * WA_BLK, (b + 1) * WA_BLK)
                for g in range(WA_KV):
                    sl = slice(g * HD, (g + 1) * HD)
                    k3 = jnp.concatenate([kbs[b + j][:, sl] for j in range(3)], axis=0)
                    v3 = jnp.concatenate([vbs[b + j][:, sl] for j in range(3)], axis=0)
                    acc = None
                    for r in range(2):
                        dks, dvs = tile(qs, 2 * g + r, [(k3, mask), (kx[:, sl], None)], [v3, vx[:, sl]])
                        acc = dks + dvs if acc is None else [a + b_ for a, b_ in zip(acc, dks + dvs)]
                    dk_ref[rows, sl] += acc[0]
                    dk_ref[cx:cx + Lc, sl] += acc[1]
                    dv_ref[rows, sl] += acc[2]
                    dv_ref[cx:cx + Lc, sl] += acc[3]

        @pl.when(s >= nlat)
        def _():
            for h in range(WA_HEADS):
                sl = slice((h // 2) * HD, (h // 2 + 1) * HD)
                dks, dvs = tile(slice(None), h, [(kx[:, sl], None)], [vx[:, sl]])
                dk_ref[cx:cx + Lc, sl] += dks[0]
                dv_ref[cx:cx + Lc, sl] += dvs[0]

    qspec = pl.BlockSpec((QB, 256), lambda s: (s, 0))
    acc_spec = pl.BlockSpec((T + 2 * WA_BLK, 128), lambda s: (0, 0))
    return _pc(body, name=name, grid=(T // QB,),
               in_specs=[qspec] + kspecs + vspecs + [pl.BlockSpec((8, 128), lambda s: (0, 0)), qspec, qspec, pl.BlockSpec((QB, 8), lambda s: (s, 0))],
               out_specs=[qspec, acc_spec, acc_spec, pl.BlockSpec((8, 128), lambda s: (0, 0))],
               out_shape=[_sds((T, 256), F32), _sds((T + 2 * WA_BLK, 128), F32), _sds((T + 2 * WA_BLK, 128), F32), _sds((8, 128), F32)],
               compiler_params=_cp(("arbitrary",), 40 << 20))(qr, *([kr] * (nk + 1)), *([P] * (nk + 1)), sink, do_src, o, stats)


def na_index_tables():
    qc = np.arange(GRID_W)[:, None]
    kc = np.arange(GRID_W)[None, :]
    cstart = np.clip(qc - NA_KW // 2, 0, GRID_W - NA_KW)
    ok = (kc >= cstart) & (kc < cstart + NA_KW)
    dx = np.clip(kc - qc, -(NA_KW - 1), NA_KW - 1) + (NA_KW - 1)
    off = np.arange(NA_KH)[:, None]
    kr = np.arange(NA_KH)[None, :]
    dy = kr - off + (NA_KH - 1)
    return ok, dx, dy


def _na_selectors():
    ok, dx, dy = na_index_tables()
    e1 = np.zeros((GRID_W * GRID_W, 128), np.float32)
    qi, ki = np.nonzero(ok)
    e1[qi * GRID_W + ki, dx[qi, ki]] = 1.0
    e2 = np.zeros((16, NA_KH * NA_KH), np.float32)
    oi, ri = np.meshgrid(np.arange(NA_KH), np.arange(NA_KH), indexing="ij")
    e2[dy[oi, ri].ravel(), (oi * NA_KH + ri).ravel()] = 1.0
    return ok, jnp.asarray(e1), jnp.asarray(np.kron(np.eye(NA_HEADS, dtype=np.float32), e2))


def na_bias_table(rpb, tag):
    ok, e1, e2 = _na_selectors()
    r2 = jnp.pad(rpb.astype(F32), ((0, 0), (0, 1), (0, 128 - (2 * NA_KW - 1)))).reshape(NA_HEADS * 16, 128)
    r1 = matmul(e2, r2, "tn", F32, f"na_bias_sel1_{tag}", hi=True)
    x = matmul(r1, e1, "nt", F32, f"na_bias_sel2_{tag}", hi=True)
    b = x.reshape(NA_HEADS, NA_KH, NA_KH, GRID_W, GRID_W).transpose(0, 1, 3, 2, 4)
    b = b + jnp.asarray(np.where(ok, 0.0, NEG).astype(np.float32))[None, None, :, None, :]
    return b.reshape(NA_HEADS, NA_KH, GRID_W, NA_KH * GRID_W)


def _na_rows(r, GR):
    r0 = jnp.clip(r - NA_KH // 2, 0, GR - NA_KH)
    return r0, jnp.clip(r - r0, 0, NA_KH - 1)


NA_RPS = 4


def _pair_rows(x):
    lane = lax.broadcasted_iota(jnp.int32, x.shape, 1)
    zero = jnp.zeros_like(x)
    return jnp.concatenate([jnp.where(lane < HD, x, zero), jnp.where(lane >= HD, x, zero)], axis=0)


def _unpair_rows(x2):
    n = x2.shape[0] // 2
    lane = lax.broadcasted_iota(jnp.int32, (n, 128), 1)
    return jnp.where(lane < HD, x2[:n], x2[n:])


def na_fwd(P, kb, vb, bias, L, Lc, name):
    T = L + Lc
    GR = L // GRID_W
    W = NA_KH * GRID_W
    QB = GRID_W * NA_RPS
    nlat = GR // NA_RPS

    def body(q_ref, k_ref, v_ref, b_ref, o_ref, st_ref):
        s = pl.program_id(0)

        def put(qs, p, res):
            o2, m2, l2 = res
            n = o2.shape[0] // 2
            o_ref[qs, p * 128:(p + 1) * 128] = _unpair_rows(o2).astype(o_ref.dtype)
            for r in range(2):
                st_ref[qs, 2 * p + r:2 * p + r + 1] = m2[r * n:(r + 1) * n]
                st_ref[qs, NA_HEADS + 2 * p + r:NA_HEADS + 2 * p + r + 1] = l2[r * n:(r + 1) * n]

        @pl.when(s < nlat)
        def _():
            for rr in range(NA_RPS):
                r0, off = _na_rows(s * NA_RPS + rr, GR)
                rows = pl.ds(pl.multiple_of(r0 * GRID_W, GRID_W), W)
                qs = slice(rr * GRID_W, (rr + 1) * GRID_W)
                for p in range(NA_HEADS // 2):
                    ps = slice(p * 128, (p + 1) * 128)
                    b2 = jnp.concatenate([b_ref[2 * p, off], b_ref[2 * p + 1, off]], axis=0)
                    put(qs, p, _attn_tile(_pair_rows(q_ref[qs, ps]), [(k_ref[rows, ps], b2), (k_ref[L:T, ps], None)],
                                          [v_ref[rows, ps], v_ref[L:T, ps]], None))

        @pl.when(s >= nlat)
        def _():
            for p in range(NA_HEADS // 2):
                ps = slice(p * 128, (p + 1) * 128)
                put(slice(None), p, _attn_tile(_pair_rows(q_ref[:, ps]), [(k_ref[L:T, ps], None)], [v_ref[L:T, ps]], None))

    one = pl.Buffered(1)
    return _pc(body, name=name, grid=(T // QB,),
               in_specs=[pl.BlockSpec((QB, 256), lambda r: (r, C_QB // 256)),
                         pl.BlockSpec((T, 256), lambda r: (0, 0), pipeline_mode=one),
                         pl.BlockSpec((T, 256), lambda r: (0, 0), pipeline_mode=one),
                         pl.BlockSpec((NA_HEADS, NA_KH, GRID_W, W), lambda r: (0, 0, 0, 0), pipeline_mode=one)],
               out_specs=[pl.BlockSpec((QB, 256), lambda r: (r, 0)), pl.BlockSpec((QB, 8), lambda r: (r, 0))],
               out_shape=[_sds((T, 256), BF16), _sds((T, 8), F32)],
               compiler_params=_cp(("arbitrary",), 32 << 20))(P, kb, vb, bias)


def na_bwd(P, kb, vb, bias, do_src, o, stats, L, Lc, name):
    T = L + Lc
    GR = L // GRID_W
    W = NA_KH * GRID_W
    QB = GRID_W * NA_RPS
    nlat = GR // NA_RPS

    def body(q_ref, k_ref, v_ref, b_ref, do_ref, o_ref, st_ref, dq_ref, dk_ref, dv_ref, db_ref):
        s = pl.program_id(0)

        @pl.when(s == 0)
        def _():
            dk_ref[...] = jnp.zeros_like(dk_ref)
            dv_ref[...] = jnp.zeros_like(dv_ref)
            db_ref[...] = jnp.zeros_like(db_ref)

        def tile(qs, p, ks, vs):
            ps = slice(p * 128, (p + 1) * 128)
            m2 = jnp.concatenate([st_ref[qs, 2 * p:2 * p + 1], st_ref[qs, 2 * p + 1:2 * p + 2]], axis=0)
            l2 = jnp.concatenate([st_ref[qs, NA_HEADS + 2 * p:NA_HEADS + 2 * p + 1], st_ref[qs, NA_HEADS + 2 * p + 1:NA_HEADS + 2 * p + 2]], axis=0)
            dq2, dks, dvs, dss, _ = _attn_bwd_tile(_pair_rows(q_ref[qs, ps]), ks, vs, None, m2, l2,
                                                   _pair_rows(o_ref[qs, ps].astype(F32)), _pair_rows(do_ref[qs, ps].astype(F32)))
            dq_ref[qs, ps] = _unpair_rows(dq2).astype(dq_ref.dtype)
            return dks, dvs, dss

        @pl.when(s < nlat)
        def _():
            for rr in range(NA_RPS):
                r0, off = _na_rows(s * NA_RPS + rr, GR)
                rows = pl.ds(pl.multiple_of(r0 * GRID_W, GRID_W), W)
                qs = slice(rr * GRID_W, (rr + 1) * GRID_W)
                for p in range(NA_HEADS // 2):
                    ps = slice(p * 128, (p + 1) * 128)
                    b2 = jnp.concatenate([b_ref[2 * p, off], b_ref[2 * p + 1, off]], axis=0)
                    dks, dvs, dss = tile(qs, p, [(k_ref[rows, ps], b2), (k_ref[L:T, ps], None)], [v_ref[rows, ps], v_ref[L:T, ps]])
                    dk_ref[rows, ps] += dks[0]
                    dv_ref[rows, ps] += dvs[0]
                    dk_ref[L:T, ps] += dks[1]
                    dv_ref[L:T, ps] += dvs[1]
                    db_ref[2 * p, off] += dss[0][:GRID_W]
                    db_ref[2 * p + 1, off] += dss[0][GRID_W:]

        @pl.when(s >= nlat)
        def _():
            for p in range(NA_HEADS // 2):
                ps = slice(p * 128, (p + 1) * 128)
                dks, dvs, _ = tile(slice(None), p, [(k_ref[L:T, ps], None)], [v_ref[L:T, ps]])
                dk_ref[L:T, ps] += dks[0]
                dv_ref[L:T, ps] += dvs[0]

    one = pl.Buffered(1)
    full = lambda shape: pl.BlockSpec(shape, lambda r: (0,) * len(shape), pipeline_mode=one)
    qspec = pl.BlockSpec((QB, 256), lambda r: (r, 0))
    return _pc(body, name=name, grid=(T // QB,),
               in_specs=[pl.BlockSpec((QB, 256), lambda r: (r, C_QB // 256)), full((T, 256)), full((T, 256)),
                         full((NA_HEADS, NA_KH, GRID_W, W)), pl.BlockSpec((QB, 256), lambda r: (r, 1)), qspec, pl.BlockSpec((QB, 8), lambda r: (r, 0))],
               out_specs=[qspec, full((T, 256)), full((T, 256)), full((NA_HEADS, NA_KH, GRID_W, W))],
               out_shape=[_sds((T, 256), BF16), _sds((T, 256), F32), _sds((T, 256), F32), _sds((NA_HEADS, NA_KH, GRID_W, W), F32)],
               compiler_params=_cp(("arbitrary",), 48 << 20))(P, kb, vb, bias, do_src, o, stats)


def na_rpb_grad(dbias, tag):
    _, e1, e2 = _na_selectors()
    x = dbias.reshape(NA_HEADS, NA_KH, GRID_W, NA_KH, GRID_W).transpose(0, 1, 3, 2, 4).reshape(NA_HEADS * NA_KH * NA_KH, GRID_W * GRID_W)
    r1 = matmul(x, e1, "nn", F32, f"na_rpb_sel1_{tag}", hi=True, tk=1024)
    r2 = matmul(e2, r1, "nn", F32, f"na_rpb_sel2_{tag}", hi=True)
    return r2.reshape(NA_HEADS, 16, 128)[:, :2 * NA_KH - 1, :2 * NA_KW - 1]


_HALO = 8


def _halo_specs(T, col0):
    nh = TR // _HALO
    cur = pl.BlockSpec((TR, 256), lambda i, j: (i, col0 + j))
    prv = pl.BlockSpec((_HALO, 256), lambda i, j: (jnp.maximum(i * nh - 1, 0), col0 + j))
    nxt = pl.BlockSpec((_HALO, 256), lambda i, j: (jnp.minimum((i + 1) * nh, T // _HALO - 1), col0 + j))
    return prv, cur, nxt


def _fill_ext(ext, prv, cur, nxt, i, nL, nT):
    has_prev = jnp.where((i != 0) & (i != nL), 1.0, 0.0)
    has_next = jnp.where((i != nL - 1) & (i != nT - 1), 1.0, 0.0)
    ext[0:_HALO, :] = prv[...].astype(F32) * has_prev
    ext[_HALO:_HALO + TR, :] = cur[...].astype(F32)
    ext[_HALO + TR:, :] = nxt[...].astype(F32) * has_next


def conv_silu_fwd(P, w8, b, nL, name):
    T = P.shape[0]
    nT = T // TR

    def body(prv, cur, nxt, w_ref, b_ref, pre_ref, act_ref, ext):
        i = pl.program_id(0)
        _fill_ext(ext, prv, cur, nxt, i, nL, nT)
        y = jnp.broadcast_to(b_ref[...], (TR, 256))
        for k in range(S_CONV):
            y = y + w_ref[k:k + 1, :] * ext[pl.ds(_HALO - S_CONV // 2 + k, TR), :]
        pre_ref[...] = y
        act_ref[...] = _silu(y)

    prv, cur, nxt = _halo_specs(T, C_XBC // 256)
    out = pl.BlockSpec((TR, 256), lambda i, j: (i, j))
    return _pc(body, name=name, grid=(nT, 4),
               in_specs=[prv, cur, nxt, pl.BlockSpec((8, 256), lambda i, j: (0, j)), pl.BlockSpec((1, 256), lambda i, j: (0, j))],
               out_specs=[out, out], out_shape=[_sds((T, 1024), F32), _sds((T, 1024), F32)],
               scratch_shapes=[pltpu.VMEM((TR + 2 * _HALO, 256), F32)],
               compiler_params=_cp(("parallel", "parallel"), 16 << 20))(P, P, P, w8, b)


def dsilu(pre, dxs_list, db_list, dc_list, name):
    T = pre.shape[0]
    n1, n2, n3 = len(dxs_list), len(db_list), len(dc_list)

    def body(*refs):
        pre_ref = refs[0]
        ins = refs[1:1 + n1 + n2 + n3]
        out = refs[-1]

        def part(rs, lo, hi):
            g = rs[0][...].astype(F32)
            for r in rs[1:]:
                g = g + r[...].astype(F32)
            x = pre_ref[:, lo:hi]
            sg = 1.0 / (1.0 + jnp.exp(-x))
            sl = x * sg
            out[:, lo:hi] = g * (sg + sl * (1.0 - sg))

        part(ins[:n1], 0, 512)
        part(ins[n1:n1 + n2], 512, 768)
        part(ins[n1 + n2:], 768, 1024)

    spec = lambda w: pl.BlockSpec((TR, w), lambda i: (i, 0))
    return _pc(body, name=name, grid=(T // TR,),
               in_specs=[spec(1024)] + [spec(512)] * n1 + [spec(256)] * (n2 + n3),
               out_specs=spec(1024), out_shape=_sds((T, 1024), F32),
               compiler_params=_cp(("parallel",), 32 << 20))(pre, *dxs_list, *db_list, *dc_list)


def conv_bwd(dpre, P, w8, nL, name):
    T = P.shape[0]
    nT = T // TR

    def body(dp, dc, dn, xp, xc, xn, w_ref, dx_ref, dw_ref, db_ref, extd, extx):
        i = pl.program_id(1)
        _fill_ext(extd, dp, dc, dn, i, nL, nT)
        _fill_ext(extx, xp, xc, xn, i, nL, nT)

        @pl.when(i == 0)
        def _():
            dw_ref[...] = jnp.zeros_like(dw_ref)
            db_ref[...] = jnp.zeros_like(db_ref)

        d = dc[...]
        dx = jnp.zeros((TR, 256), F32)
        for k in range(S_CONV):
            dx = dx + w_ref[k:k + 1, :] * extd[pl.ds(_HALO + S_CONV // 2 - k, TR), :]
            dw_ref[k:k + 1, :] += jnp.sum(d * extx[pl.ds(_HALO - S_CONV // 2 + k, TR), :], axis=0, keepdims=True)
        dx_ref[...] = dx.astype(dx_ref.dtype)
        db_ref[0:1, :] += jnp.sum(d, axis=0, keepdims=True)

    def swap(spec):
        f = spec.index_map
        return pl.BlockSpec(spec.block_shape, lambda j, i: f(i, j))

    dprv, dcur, dnxt = [swap(s) for s in _halo_specs(T, 0)]
    xprv, xcur, xnxt = [swap(s) for s in _halo_specs(T, C_XBC // 256)]
    acc = pl.BlockSpec((8, 256), lambda j, i: (0, j))
    return _pc(body, name=name, grid=(4, nT),
               in_specs=[dprv, dcur, dnxt, xprv, xcur, xnxt, acc],
               out_specs=[pl.BlockSpec((TR, 256), lambda j, i: (i, j)), acc, acc],
               out_shape=[_sds((T, 1024), BF16), _sds((8, 1024), F32), _sds((8, 1024), F32)],
               scratch_shapes=[pltpu.VMEM((TR + 2 * _HALO, 256), F32), pltpu.VMEM((TR + 2 * _HALO, 256), F32)],
               compiler_params=_cp(("parallel", "arbitrary"), 16 << 20))(dpre, dpre, dpre, P, P, P, w8)


def _onehot_row(h, n):
    return (lax.broadcasted_iota(jnp.int32, (1, n), 1) == h).astype(F32)


def _onehot_col(h, n):
    return (lax.broadcasted_iota(jnp.int32, (n, 1), 0) == h).astype(F32)


def _ssd_chunk(xs, dtr, dtb, alog, bm, cm, hin, reverse):
    Qn = S_Q
    ii = lax.broadcasted_iota(jnp.int32, (Qn, Qn), 0)
    jj = lax.broadcasted_iota(jnp.int32, (Qn, Qn), 1)
    keep = (ii <= jj) if reverse else (ii >= jj)
    tri = keep.astype(F32)
    triT = ((jj <= ii) if reverse else (jj >= ii)).astype(F32)
    eye = (ii == jj).astype(F32)
    dt = _softplus(dtr + dtb)
    a = dt * (-jnp.exp(alog))
    cs = hdot(tri, a)
    csT = hdot(a, triT, "tn")
    dtT = hdot(dt, eye, "tn")
    last = _onehot_row(0 if reverse else Qn - 1, Qn)
    ys, houts = [], []
    for g in range(S_GROUPS):
        G = bdot(cm[g], bm[g], "nt")
        for r in range(S_HEADS // S_GROUPS):
            h = g * (S_HEADS // S_GROUPS) + r
            eh_r, eh_c = _onehot_row(h, S_HEADS), _onehot_col(h, S_HEADS)
            cs_c = jnp.sum(cs * eh_r, axis=1, keepdims=True)
            dt_c = jnp.sum(dt * eh_r, axis=1, keepdims=True)
            cs_r = jnp.sum(csT * eh_c, axis=0, keepdims=True)
            dt_r = jnp.sum(dtT * eh_c, axis=0, keepdims=True)
            tot = jnp.sum(cs_r * last, axis=1, keepdims=True)
            decay = jnp.exp(jnp.where(keep, cs_c - cs_r, NEG))
            w = G * decay * dt_r
            y = bdot(w, xs[h], "nn") + bdot(cm[g], hin[h], "nt") * jnp.exp(cs_c)
            xsc = xs[h] * (jnp.exp(tot - cs_c) * dt_c)
            hout = hin[h] * jnp.exp(tot) + bdot(xsc, bm[g], "tn")
            ys.append(y)
            houts.append(hout)
    return ys, houts


def _ssd_orders(L, Lc):
    nl, ncx = L // S_Q, Lc // S_Q
    fwd = lambda s: jnp.where(s < ncx, nl + s, s - ncx)
    bwd = lambda s: nl + ncx - 1 - s
    return nl + ncx, fwd, bwd


def _ssd_in_specs(fo, bo, step):
    def at(order, w, col):
        return pl.BlockSpec((S_Q, w), lambda u: (order(step(u)), col))
    specs = []
    for order in (fo, bo):
        specs += [at(order, 512, 0), at(order, 256, 2), at(order, 256, 3), at(order, 128, C_DT // 128)]
    return specs


def ssd_fwd(act, P, dtb, alog, L, Lc, name):
    T = L + Lc
    ns, fo, bo = _ssd_orders(L, Lc)

    def body(xf, bf, cf, df, xb, bb, cb, db, dtb_ref, al_ref, yf, yb, hsf, hsb, Hf, Hb):
        s = pl.program_id(0)

        @pl.when(s == 0)
        def _():
            Hf[...] = jnp.zeros_like(Hf)
            Hb[...] = jnp.zeros_like(Hb)

        for d, (x_r, b_r, c_r, dt_r, y_r, hs_r, H) in enumerate(((xf, bf, cf, df, yf, hsf, Hf), (xb, bb, cb, db, yb, hsb, Hb))):
            hin = [H[h] for h in range(S_HEADS)]
            hs_r[0] = H[...]
            ys, houts = _ssd_chunk(
                [x_r[:, h * S_P:(h + 1) * S_P] for h in range(S_HEADS)], dt_r[:, d * 8:(d + 1) * 8],
                dtb_ref[d:d + 1, 0:8], al_ref[d:d + 1, 0:8],
                [b_r[:, g * S_N:(g + 1) * S_N] for g in range(S_GROUPS)], [c_r[:, g * S_N:(g + 1) * S_N] for g in range(S_GROUPS)],
                hin, reverse=(d == 1))
            for h in range(S_HEADS):
                y_r[:, h * S_P:(h + 1) * S_P] = ys[h]
                H[h] = houts[h]

    ident = lambda u: u
    small = pl.BlockSpec((8, 128), lambda u: (0, 0))
    hspec = pl.BlockSpec((1, S_HEADS, S_P, S_N), lambda u: (u, 0, 0, 0))
    return _pc(body, name=name, grid=(ns,),
               in_specs=_ssd_in_specs(fo, bo, ident) + [small, small],
               out_specs=[pl.BlockSpec((S_Q, 512), lambda u: (fo(u), 0)), pl.BlockSpec((S_Q, 512), lambda u: (bo(u), 0)), hspec, hspec],
               out_shape=[_sds((T, 512), F32), _sds((T, 512), F32), _sds((ns, S_HEADS, S_P, S_N), F32), _sds((ns, S_HEADS, S_P, S_N), F32)],
               scratch_shapes=[pltpu.VMEM((S_HEADS, S_P, S_N), F32), pltpu.VMEM((S_HEADS, S_P, S_N), F32)],
               compiler_params=_cp(("arbitrary",), 32 << 20))(act, act, act, P, act, act, act, P, dtb, alog)


def ssd_bwd(act, P, dtb, alog, hsf, hsb, dy, L, Lc, name):
    T = L + Lc
    ns, fo, bo = _ssd_orders(L, Lc)
    step = lambda u: ns - 1 - u

    def body(xf, bf, cf, df, xb, bb, cb, db, dtb_ref, al_ref, hsf_r, hsb_r, dyf, dyb,
             dxf, dbf, dcf, ddf, dxb, dbb, dcb, ddb, ddtb, dal, dHf, dHb):
        u = pl.program_id(0)

        @pl.when(u == 0)
        def _():
            dHf[...] = jnp.zeros_like(dHf)
            dHb[...] = jnp.zeros_like(dHb)
            ddtb[...] = jnp.zeros_like(ddtb)
            dal[...] = jnp.zeros_like(dal)

        dirs = ((xf, bf, cf, df, hsf_r, dyf, dxf, dbf, dcf, ddf, dHf), (xb, bb, cb, db, hsb_r, dyb, dxb, dbb, dcb, ddb, dHb))
        for d, (x_r, b_r, c_r, dt_r, hs_r, dy_r, dx_o, db_o, dc_o, dd_o, dH) in enumerate(dirs):
            f = functools.partial(_ssd_chunk, reverse=(d == 1))
            _, vjp = jax.vjp(
                f, [x_r[:, h * S_P:(h + 1) * S_P] for h in range(S_HEADS)], dt_r[:, d * 8:(d + 1) * 8],
                dtb_ref[d:d + 1, 0:8], al_ref[d:d + 1, 0:8],
                [b_r[:, g * S_N:(g + 1) * S_N] for g in range(S_GROUPS)], [c_r[:, g * S_N:(g + 1) * S_N] for g in range(S_GROUPS)],
                [hs_r[0, h] for h in range(S_HEADS)])
            gx, gdt, gdtb, gal, gb, gc, gh = vjp(([dy_r[:, h * S_P:(h + 1) * S_P] for h in range(S_HEADS)],
                                                  [dH[h] for h in range(S_HEADS)]))
            for h in range(S_HEADS):
                dx_o[:, h * S_P:(h + 1) * S_P] = gx[h]
                dH[h] = gh[h]
            for g in range(S_GROUPS):
                db_o[:, g * S_N:(g + 1) * S_N] = gb[g]
                dc_o[:, g * S_N:(g + 1) * S_N] = gc[g]
            dd_o[...] = gdt
            ddtb[d:d + 1, 0:8] += gdtb
            dal[d:d + 1, 0:8] += gal

    small = pl.BlockSpec((8, 128), lambda u: (0, 0))
    hspec = pl.BlockSpec((1, S_HEADS, S_P, S_N), lambda u: (step(u), 0, 0, 0))
    at = lambda order, w: pl.BlockSpec((S_Q, w), lambda u: (order(step(u)), 0))
    outs = []
    for order in (fo, bo):
        outs += [at(order, 512), at(order, 256), at(order, 256), at(order, 8)]
    oshape = [_sds((T, 512), F32), _sds((T, 256), F32), _sds((T, 256), F32), _sds((T, 8), F32)]
    return _pc(body, name=name, grid=(ns,),
               in_specs=_ssd_in_specs(fo, bo, step) + [small, small, hspec, hspec, at(fo, 512), at(bo, 512)],
               out_specs=outs + [small, small], out_shape=oshape + oshape + [_sds((8, 128), F32), _sds((8, 128), F32)],
               scratch_shapes=[pltpu.VMEM((S_HEADS, S_P, S_N), F32), pltpu.VMEM((S_HEADS, S_P, S_N), F32)],
               compiler_params=_cp(("arbitrary",), 40 << 20))(act, act, act, P, act, act, act, P, dtb, alog, hsf, hsb, dy, dy)


def _ssm_out(yf, yb, xs, z, dskip, g):
    y = (yf + yb + dskip * xs) * _silu(z)
    return (y * lax.rsqrt(jnp.mean(y * y, axis=-1, keepdims=True) + EPS)) * g


def ssm_out_fwd(yf, yb, act, P, dskip, g, name):
    T = yf.shape[0]

    def body(yf_r, yb_r, xs_r, z_r, d_r, g_r, o_r):
        o_r[...] = _ssm_out(yf_r[...], yb_r[...], xs_r[...], z_r[...], d_r[...], g_r[...]).astype(o_r.dtype)

    row = pl.BlockSpec((TR, 512), lambda i: (i, 0))
    vec = pl.BlockSpec((1, 512), lambda i: (0, 0))
    return _pc(body, name=name, grid=(T // TR,),
               in_specs=[row, row, row, pl.BlockSpec((TR, 512), lambda i: (i, C_Z // 512)), vec, vec],
               out_specs=row, out_shape=_sds((T, 512), BF16),
               compiler_params=_cp(("parallel",), 16 << 20))(yf, yb, act, P, dskip, g)


def ssm_out_bwd(yf, yb, act, P, dskip, g, do_src, name):
    T = yf.shape[0]

    def body(yf_r, yb_r, xs_r, z_r, d_r, g_r, do_r, dy_r, dxs_r, dz_r, dv_r):
        @pl.when(pl.program_id(0) == 0)
        def _():
            dv_r[...] = jnp.zeros_like(dv_r)

        _, vjp = jax.vjp(_ssm_out, yf_r[...], yb_r[...], xs_r[...], z_r[...], d_r[...], g_r[...])
        dyf, _, dxs, dz, dd, dg = vjp(do_r[...].astype(F32))
        dy_r[...] = dyf
        dxs_r[...] = dxs
        dz_r[...] = dz.astype(dz_r.dtype)
        dv_r[0:1, :] += dd
        dv_r[1:2, :] += dg

    row = pl.BlockSpec((TR, 512), lambda i: (i, 0))
    vec = pl.BlockSpec((1, 512), lambda i: (0, 0))
    return _pc(body, name=name, grid=(T // TR,),
               in_specs=[row, row, row, pl.BlockSpec((TR, 512), lambda i: (i, C_Z // 512)), vec, vec,
                         pl.BlockSpec((TR, 512), lambda i: (i, 1))],
               out_specs=[row, row, row, pl.BlockSpec((8, 512), lambda i: (0, 0))],
               out_shape=[_sds((T, 512), F32), _sds((T, 512), F32), _sds((T, 512), BF16), _sds((8, 512), F32)],
               compiler_params=_cp(("arbitrary",), 24 << 20))(yf, yb, act, P, dskip, g, do_src)


def add_halves(xv, got, cvec, name):
    n, r, cdim = xv.shape
    h = r // 2

    def body(c_ref, x_ref, g_ref, o_ref):
        o_ref[...] = (x_ref[...].astype(F32) + g_ref[...].astype(F32)).astype(o_ref.dtype)

    gs = pltpu.PrefetchScalarGridSpec(
        num_scalar_prefetch=1, grid=(n,),
        in_specs=[pl.BlockSpec((1, h, cdim), lambda k, c_ref: (k, c_ref[0], 0)), pl.BlockSpec((1, h, cdim), lambda k, c_ref: (k, 0, 0))],
        out_specs=pl.BlockSpec((1, h, cdim), lambda k, c_ref: (k, 0, 0)))
    return _pc(body, name=name, grid_spec=gs, out_shape=_sds((n, h, cdim), BF16),
               compiler_params=_cp(("arbitrary",), 24 << 20))(cvec, xv, got)


def sum_slots(a, name):
    n, r, cdim = a.shape
    tr = _div_tile(r, 512, 16)

    def body(a_ref, o_ref):
        acc = a_ref[0].astype(F32)
        for k in range(1, n):
            acc = acc + a_ref[k].astype(F32)
        o_ref[...] = acc

    return _pc(body, name=name, grid=(r // tr,), in_specs=[pl.BlockSpec((n, tr, cdim), lambda i: (0, i, 0))],
               out_specs=pl.BlockSpec((tr, cdim), lambda i: (i, 0)), out_shape=_sds((r, cdim), F32),
               compiler_params=_cp(("parallel",), 32 << 20))(a)


def adamw(w, g, m, v, name):
    B, R, C = w.shape
    tr = _div_tile(R, max(8, (1 << 19) // max(C, 1) // 8 * 8), 8) if R % 8 == 0 else R
    c1 = 1.0 / (1.0 - ADAM_B1 ** ADAM_STEP)
    c2 = 1.0 / (1.0 - ADAM_B2 ** ADAM_STEP)

    def body(w_ref, g_ref, m_ref, v_ref, d_ref, mo_ref, vo_ref):
        gg = g_ref[...]
        mn = ADAM_B1 * m_ref[...] + (1.0 - ADAM_B1) * gg
        vn = ADAM_B2 * v_ref[...] + (1.0 - ADAM_B2) * (gg * gg)
        d_ref[...] = -ADAM_LR * ((mn * c1) / (jnp.sqrt(vn * c2) + ADAM_EPS) + ADAM_WD * w_ref[...])
        mo_ref[...] = mn
        vo_ref[...] = vn

    spec = pl.BlockSpec((1, tr, C), lambda b, i: (b, i, 0))
    return _pc(body, name=name, grid=(B, R // tr), in_specs=[spec] * 4, out_specs=[spec] * 3,
               out_shape=[_sds((B, R, C), F32)] * 3, compiler_params=_cp(("parallel", "parallel"), 32 << 20))(w, g, m, v)


def _me():
    return lax.axis_index("x"), lax.axis_index("y"), lax.axis_index("c")


def _flip(v, bit):
    return 1 - v if bit else v


def allgather8(xv, name):
    R = xv.shape[0]

    def body(x_ref, out_ref, sum_ref, send_sems, recv_sems):
        mx, my, mc = _me()
        me = 4 * mx + 2 * my + mc
        out_ref[me] = x_ref[...]
        sends, recvs = [], []
        for k in range(1, 8):
            px, py, pc = _flip(mx, k & 4), _flip(my, k & 2), _flip(mc, k & 1)
            peer = 4 * px + 2 * py + pc
            sends.append(pltpu.make_async_remote_copy(src_ref=x_ref, dst_ref=out_ref.at[me], send_sem=send_sems.at[k - 1],
                                                      recv_sem=recv_sems.at[k - 1], device_id=(px, py, pc), device_id_type=MESH))
            recvs.append(pltpu.make_async_remote_copy(src_ref=x_ref, dst_ref=out_ref.at[peer], send_sem=send_sems.at[k - 1],
                                                      recv_sem=recv_sems.at[k - 1], device_id=(px, py, pc), device_id_type=MESH))
        for cp in sends:
            cp.start()
        for cp in recvs:
            cp.wait_recv()
        for cp in sends:
            cp.wait_send()
        acc = out_ref[0]
        for d in range(1, 8):
            acc = acc + out_ref[d]
        sum_ref[...] = acc

    vm = pl.BlockSpec(memory_space=pltpu.VMEM)
    return _pc(body, name=name, pin=False, in_specs=[vm], out_specs=[vm, vm], out_shape=[_sds((8, R, 128), F32), _sds((R, 128), F32)],
               scratch_shapes=[pltpu.SemaphoreType.DMA((7,)), pltpu.SemaphoreType.DMA((7,))],
               compiler_params=_cp(None, 32 << 20))(xv)


def _other_chips(mx, my):
    return [(1 - mx, my), (mx, 1 - my), (1 - mx, 1 - my)]


def _halves(r, mc, mult):
    h = r // 2
    return pl.ds(pl.multiple_of(mc * h, mult), h), pl.ds(pl.multiple_of((1 - mc) * h, mult), h)


def _rcopy(src, dst, send_sems, recv_sems, k, to):
    return pltpu.make_async_remote_copy(src_ref=src, dst_ref=dst, send_sem=send_sems.at[k], recv_sem=recv_sems.at[k],
                                        device_id=to, device_id_type=MESH)


def _gather_body(xs, outs, send_sems, recv_sems):
    n = len(xs)
    mx, my, mc = _me()
    chip = 2 * mx + my
    sib = (mx, my, 1 - mc)
    chips = _other_chips(mx, my)
    idx = [2 * cx + cy for cx, cy in chips]
    cp = functools.partial(_rcopy, send_sems=send_sems, recv_sems=recv_sems)
    hv = [_halves(x.shape[0], mc, 16) for x in xs]
    first, passed = [], []
    for a in range(n):
        for j, (cx, cy) in enumerate(chips):
            first.append(cp(xs[a].at[hv[a][0]], outs[a].at[chip, hv[a][0]], k=6 * a + j, to=(cx, cy, mc)))
            first[-1].start()
    for a in range(n):
        for j in range(3):
            cp(xs[a].at[hv[a][0]], outs[a].at[idx[j], hv[a][0]], k=6 * a + j, to=sib).wait_recv()
            passed.append(cp(outs[a].at[idx[j], hv[a][0]], outs[a].at[idx[j], hv[a][0]], k=6 * a + 3 + j, to=sib))
            passed[-1].start()
    for a in range(n):
        for j in range(3):
            cp(xs[a].at[hv[a][1]], outs[a].at[idx[j], hv[a][1]], k=6 * a + 3 + j, to=sib).wait_recv()
    for c_ in first + passed:
        c_.wait_send()


def _my_chip():
    return 2 * lax.axis_index("x") + lax.axis_index("y")


def _own_slots(outs, shards):
    return [lax.dynamic_update_index_in_dim(o, x, _my_chip(), 0) for o, x in zip(outs, shards)]


def gather_weights(shards, name):
    n = len(shards)

    def body(*refs):
        _gather_body(refs[:n], refs[n:2 * n], *refs[2 * n:])

    hbm = pl.BlockSpec(memory_space=pl.ANY)
    outs = _pc(body, name=name, in_specs=[hbm] * n, out_specs=[hbm] * n, out_shape=[_sds((4,) + x.shape, x.dtype) for x in shards],
               scratch_shapes=[pltpu.SemaphoreType.DMA((6 * n,)), pltpu.SemaphoreType.DMA((6 * n,))])(*shards)
    return _own_slots(outs, shards)


GATHER_REST_ID = 3


def gather_weights_sc(shards, name):
    n = len(shards)
    x_refs = [jax.new_ref(x, memory_space=pltpu.MemorySpace.HBM) for x in shards]
    out_refs = [jax.empty_ref(_sds((4,) + x.shape, x.dtype), memory_space=pltpu.MemorySpace.HBM) for x in shards]

    @pl.kernel(mesh=plsc.ScalarSubcoreMesh(axis_name="sc", num_cores=1), name=name,
               scratch_types=(pltpu.SemaphoreType.DMA((6 * n,)), pltpu.SemaphoreType.DMA((6 * n,))),
               compiler_params=pltpu.CompilerParams(collective_id=GATHER_REST_ID))
    def launch(send_sems, recv_sems):
        mx, my, mc = _me()
        barrier = pltpu.get_barrier_semaphore()
        for peer in [(mx, my, 1 - mc)] + [(cx, cy, mc) for cx, cy in _other_chips(mx, my)]:
            pl.semaphore_signal(barrier, inc=1, device_id=peer, device_id_type=MESH)
        pl.semaphore_wait(barrier, 4)
        _gather_body(x_refs, out_refs, send_sems, recv_sems)

    launch()
    return _own_slots([o[...] for o in out_refs], shards)


def swap_halves(arrs, name):
    n = len(arrs)

    def body(*refs):
        xs, outs = refs[:n], refs[n:2 * n]
        send_sems, recv_sems = refs[2 * n:]
        mx, my, mc = _me()
        cps = []
        for a in range(n):
            theirs = _halves(xs[a].shape[1], mc, 16)[1]
            cps.append(_rcopy(xs[a].at[pl.ds(0, 4), theirs], outs[a], send_sems, recv_sems, a, (mx, my, 1 - mc)))
            cps[-1].start()
        for c_ in cps:
            c_.wait()

    hbm = pl.BlockSpec(memory_space=pl.ANY)
    return _pc(body, name=name, in_specs=[hbm] * n, out_specs=[hbm] * n,
               out_shape=[_sds((4, x.shape[1] // 2, x.shape[2]), x.dtype) for x in arrs],
               scratch_shapes=[pltpu.SemaphoreType.DMA((n,)), pltpu.SemaphoreType.DMA((n,))])(*arrs)


SCATTER_ID = 4


def scatter_chips_sc(arrs, name):
    n = len(arrs)
    x_refs = [jax.new_ref(x, memory_space=pltpu.MemorySpace.HBM) for x in arrs]
    out_refs = [jax.empty_ref(_sds(x.shape, x.dtype), memory_space=pltpu.MemorySpace.HBM) for x in arrs]

    @pl.kernel(mesh=plsc.ScalarSubcoreMesh(axis_name="sc", num_cores=1), name=name,
               scratch_types=(pltpu.SemaphoreType.DMA((3 * n,)), pltpu.SemaphoreType.DMA((3 * n,))),
               compiler_params=pltpu.CompilerParams(collective_id=SCATTER_ID))
    def launch(send_sems, recv_sems):
        mx, my, mc = _me()
        chip = 2 * mx + my
        chips = _other_chips(mx, my)
        idx = [2 * cx + cy for cx, cy in chips]
        barrier = pltpu.get_barrier_semaphore()
        for cx, cy in chips:
            pl.semaphore_signal(barrier, inc=1, device_id=(cx, cy, mc), device_id_type=MESH)
        pl.semaphore_wait(barrier, 3)
        cp = functools.partial(_rcopy, send_sems=send_sems, recv_sems=recv_sems)
        sends = []
        for a in range(n):
            for j, (cx, cy) in enumerate(chips):
                sends.append(cp(x_refs[a].at[idx[j]], out_refs[a].at[chip], k=3 * a + j, to=(cx, cy, mc)))
                sends[-1].start()
        for a in range(n):
            for j, (cx, cy) in enumerate(chips):
                cp(x_refs[a].at[idx[j]], out_refs[a].at[idx[j]], k=3 * a + j, to=(cx, cy, mc)).wait_recv()
        for c_ in sends:
            c_.wait_send()

    launch()
    return _own_slots([o[...] for o in out_refs], [lax.dynamic_index_in_dim(x, _my_chip(), 0, keepdims=False) for x in arrs])


def share_halves(parts, name):
    flat = [p for w in parts for p in w]
    nw, n = len(parts), len(flat)
    depth = n // nw

    def body(*refs):
        xs, outs = refs[:n], refs[n:n + nw]
        send_sems, recv_sems = refs[n + nw:]
        mx, my, mc = _me()
        sib = (mx, my, 1 - mc)
        sends, recvs = [], []
        for a in range(n):
            w, l = a // depth, a % depth
            mine, theirs = _halves(outs[w].shape[1], mc, 8)
            sends.append(_rcopy(xs[a], outs[w].at[l, mine], send_sems, recv_sems, a, sib))
            recvs.append(_rcopy(xs[a], outs[w].at[l, theirs], send_sems, recv_sems, a, sib))
            sends[-1].start()
        for c_ in recvs:
            c_.wait_recv()
        for c_ in sends:
            c_.wait_send()

    hbm = pl.BlockSpec(memory_space=pl.ANY)
    outs = _pc(body, name=name, in_specs=[hbm] * n, out_specs=[hbm] * nw,
               out_shape=[_sds((depth, 2 * w[0].shape[0], w[0].shape[1]), F32) for w in parts],
               scratch_shapes=[pltpu.SemaphoreType.DMA((n,)), pltpu.SemaphoreType.DMA((n,))])(*flat)
    outs = list(outs)
    mc = lax.axis_index("c")
    for w in range(nw):
        for l in range(depth):
            h = parts[w][l].shape[0]
            outs[w] = lax.dynamic_update_slice(outs[w], parts[w][l][None], (l, mc * h, 0))
    return outs


_BIG = ("w_in", "w_out", "w_ffn_in", "w_ffn_out")
N_CHIPS = 4
DEPTH = 2


def _pad_rows(v, mult=8):
    n = v.shape[0]
    rows = -(-n // 128)
    rows = -(-rows // mult) * mult
    return jnp.pad(v, (0, rows * 128 - n)).reshape(rows, 128)


class _Flat:
    def __init__(self):
        self.items = []

    def add(self, name, a):
        self.items.append((name, a.shape, a.reshape(-1).astype(F32)))

    def rows(self):
        return _pad_rows(jnp.concatenate([a for _, _, a in self.items]))

    def split(self, rows):
        flat = rows.reshape(-1)
        out, o = {}, 0
        for name, shape, a in self.items:
            out[name] = flat[o:o + a.shape[0]].reshape(shape)
            o += a.shape[0]
        return out

    def split_lead(self, rows3):
        n = rows3.shape[0]
        flat = rows3.reshape(n, -1)
        out, o = {}, 0
        for name, shape, a in self.items:
            out[name] = flat[:, o:o + a.shape[0]].reshape((n,) + tuple(shape))
            o += a.shape[0]
        return out


def _gsv(rows):
    z = jnp.zeros((2, D), F32)
    r = [z if a is None else a for a in rows] + [z] * 5
    return jnp.stack(r, axis=1)


def _pad8(a, rows=8, cols=128):
    return jnp.zeros((rows, cols), F32).at[:a.shape[0], :a.shape[1]].set(a.astype(F32))


def kernel(x, c, ctx, c_ctx, w_mod, b_mod, g_mix, w_in, wa_sink, na_rpb, ssm_conv_w, ssm_conv_b, ssm_dt_bias, ssm_a_log, ssm_d, ssm_norm_g, w_out, g_ffn, w_ffn_in, w_ffn_out, g_final, loss_target, m_c_ctx, m_w_mod, m_b_mod, m_g_mix, m_w_in, m_wa_sink, m_na_rpb, m_ssm_conv_w, m_ssm_conv_b, m_ssm_dt_bias, m_ssm_a_log, m_ssm_d, m_ssm_norm_g, m_w_out, m_g_ffn, m_w_ffn_in, m_w_ffn_out, m_g_final, v_c_ctx, v_w_mod, v_b_mod, v_g_mix, v_w_in, v_wa_sink, v_na_rpb, v_ssm_conv_w, v_ssm_conv_b, v_ssm_dt_bias, v_ssm_a_log, v_ssm_d, v_ssm_norm_g, v_w_out, v_g_ffn, v_w_ffn_in, v_w_ffn_out, v_g_final):
    L, Lc = x.shape[1], ctx.shape[1]
    T = L + Lc
    nL = L // TR
    mx, my, mc = lax.axis_index("x"), lax.axis_index("y"), lax.axis_index("c")
    dev = 4 * mx + 2 * my + mc
    chip = 2 * mx + my
    MODW = 6 * D // N_CHIPS
    CW = 1024 // N_CHIPS

    sc = _silu(c.astype(F32))
    scc = _silu(c_ctx.astype(F32))[None]
    f1 = _Flat()
    f1.add("sc", sc)
    f1.add("conv_w", ssm_conv_w)
    g1, _ = allgather8(f1.rows(), "gather_cond")
    g1 = f1.split_lead(g1)
    sc_all = g1["sc"][:, 0]
    conv_w = jnp.concatenate([g1["conv_w"][2 * k] for k in range(N_CHIPS)], axis=-1)
    A16 = jnp.concatenate([sc_all, scc, jnp.zeros((7, D), F32)], axis=0)

    mod_part = matmul_layers(A16, w_mod, "nn", "mod_fwd")
    f2 = _Flat()
    f2.add("mod", mod_part)
    g2, _ = allgather8(f2.rows(), "gather_mod")
    g2 = f2.split_lead(g2)["mod"]
    mods = jnp.concatenate([g2[2 * k] for k in range(N_CHIPS)], axis=-1) + b_mod[:, None, :]
    mod_l = lax.dynamic_index_in_dim(mods, dev, axis=1, keepdims=False).reshape(DEPTH, 6, D)
    mod_c = mods[:, 8].reshape(DEPTH, 6, D)
    mod = jnp.stack([mod_l, mod_c], axis=1)
    mrow = lambda l, j: mod[l, :, j]

    own = {"w_in": w_in, "w_out": w_out, "w_ffn_in": w_ffn_in, "w_ffn_out": w_ffn_out}
    sh16 = [own[n][l].astype(BF16) for n in _BIG for l in range(DEPTH)]
    after_mod = (g2[0, 0, 0, 0] * 0).astype(BF16)
    gath = list(gather_weights([sh16[0] + after_mod], "gather_first"))
    after_first = (gath[0][0, 0, 0] * 0).astype(BF16)
    gath += list(gather_weights_sc([sh16[1] + after_first] + sh16[2:], "gather_rest"))
    gw = {n: [gath[DEPTH * i + l] for l in range(DEPTH)] for i, n in enumerate(_BIG)}
    W_in = [jnp.pad(jnp.concatenate([g[k] for k in range(N_CHIPS)], axis=1), ((0, 0), (0, IN_PAD - IN_COLS))) for g in gw["w_in"]]
    W_out = [g.reshape(D, D) for g in gw["w_out"]]
    W_fo = [g.reshape(D_FF, D) for g in gw["w_ffn_out"]]
    W_fi = gw["w_ffn_in"]

    cos, sin, rotm = rope_tables(L, Lc)
    x0 = jnp.concatenate([x[0], ctx[0]], axis=0).astype(F32)

    sv = []
    xin = x0
    gsv_first = _gsv([None, mrow(0, 0), mrow(0, 1)])
    _, h1 = res_norm_mod(x0, None, gsv_first, g_mix[0][None], nL, "norm_first")
    for l in range(DEPTH):
        s = {"xin": xin, "h1": h1}
        P = matmul(h1, W_in[l], "nn", F32, f"in_proj{l}", tn=IN_PAD)
        qr, kr, kb, vb = rope_apply(P, C_QA // 256, P, C_KA // 128, cos, sin, rotm, False, f"rope{l}", kv_src=P)
        sink8 = _pad8(jnp.broadcast_to(wa_sink[l][:, None], (WA_HEADS, 128)))
        oa, sta = win_attn_fwd(qr, kr, P, sink8, L, Lc, f"wa_fwd{l}")
        bias = na_bias_table(na_rpb[l], l)
        ob, stb = na_fwd(P, kb, vb, bias, L, Lc, f"na_fwd{l}")
        w8 = jnp.concatenate([conv_w[l], jnp.zeros((1, 1024), F32)], axis=0)
        pre, act = conv_silu_fwd(P, w8, ssm_conv_b[l][None], nL, f"conv_fwd{l}")
        dtb8, al8 = _pad8(ssm_dt_bias[l]), _pad8(ssm_a_log[l])
        yf, yb, hsf, hsb = ssd_fwd(act, P, dtb8, al8, L, Lc, f"ssd_fwd{l}")
        dskip = jnp.repeat(ssm_d[l], S_P)[None]
        oc = ssm_out_fwd(yf, yb, act, P, dskip, ssm_norm_g[l][None], f"ssm_out_fwd{l}")
        mixin = [(oa, 0), (ob, 256), (oc, 512)]
        mix = out_proj_fwd(mixin, W_out[l], f"out_proj{l}")
        gsv_mid = _gsv([mrow(l, 2), mrow(l, 3), mrow(l, 4)])
        x1, h2 = res_norm_mod(xin, mix, gsv_mid, g_ffn[l][None], nL, f"norm_mid{l}")
        gu = matmul_fi(h2, W_fi[l], "nn", BF16, f"ffn_in{l}")
        af = swiglu_fwd(gu, f"swiglu_fwd{l}")
        fo = matmul(af, W_fo[l], "nn", BF16, f"ffn_out{l}", tk=D_FF)
        s.update(P=P, qr=qr, kr=kr, sink8=sink8, oa=oa, sta=sta, ob=ob, stb=stb, kb=kb, vb=vb, bias=bias, w8=w8, pre=pre, act=act, dtb8=dtb8, al8=al8, yf=yf,
                 yb=yb, hsf=hsf, hsb=hsb, dskip=dskip, mixin=mixin, mix=mix, gsv_mid=gsv_mid, x1=x1, h2=h2, gu=gu, af=af, fo=fo)
        if l + 1 < DEPTH:
            s["gsv_end"] = _gsv([mrow(l, 5), mrow(l + 1, 0), mrow(l + 1, 1)])
            xin, h1 = res_norm_mod(x1, fo, s["gsv_end"], g_mix[l + 1][None], nL, f"norm_end{l}")
        else:
            s["gsv_end"] = _gsv([mrow(l, 5), None, None])
        sv.append(s)

    last = sv[-1]
    loss8, dres, dfo, dgsv_end, dg_final = final_loss(last["x1"], last["fo"], last["gsv_end"], g_final[None], loss_target[0].astype(F32), nL, "final_loss")
    loss = lax.psum(loss8[0, 0], ("x", "y", "c"))

    dmod = [[None] * 6 for _ in range(DEPTH)]
    gW = {n: [None] * DEPTH for n in _BIG}
    small = [dict() for _ in range(DEPTH)]
    parts = [None] * DEPTH
    cvec = mc.astype(jnp.int32).reshape(1)
    grad_x = None
    for l in reversed(range(DEPTH)):
        s = sv[l]
        dmod[l][5] = dgsv_end[:, 0]
        if l + 1 < DEPTH:
            dmod[l + 1][0], dmod[l + 1][1] = dgsv_end[:, 1], dgsv_end[:, 2]
        daf = matmul(dfo, W_fo[l], "nt", BF16, f"ffn_out_dx{l}")
        gW["w_ffn_out"][l] = matmul(s["af"], dfo, "tn", BF16, f"ffn_out_dw{l}", tm=1408, tk=T).reshape(N_CHIPS, D_FF // N_CHIPS, D)
        dgu = swiglu_bwd(s["gu"], daf, f"swiglu_bwd{l}")
        dh2 = matmul_fi(dgu, W_fi[l], "nt", BF16, f"ffn_in_dx{l}")
        gW["w_ffn_in"][l] = matmul_fi(s["h2"], dgu, "tn", BF16, f"ffn_in_dw{l}")
        dres, dmix, dgsv_mid, dg_ffn = res_norm_mod_bwd(s["x1"], s["mix"], s["gsv_mid"], g_ffn[l][None], dh2, dres, nL, f"norm_mid_bwd{l}")
        dmod[l][2], dmod[l][3], dmod[l][4] = dgsv_mid[:, 0], dgsv_mid[:, 1], dgsv_mid[:, 2]
        dmixin = matmul(dmix, W_out[l], "nt", BF16, f"out_proj_dx{l}")
        gW["w_out"][l] = out_proj_dw(s["mixin"], dmix, f"out_proj_dw{l}").reshape(N_CHIPS, D // N_CHIPS, D)
        P = s["P"]
        dqr, dkr, dva, dsink = win_attn_bwd(s["qr"], s["kr"], P, s["sink8"], dmixin, s["oa"], s["sta"], L, Lc, f"wa_bwd{l}")
        dqa, dka = rope_apply(dqr, 0, dkr[WA_BLK:WA_BLK + T], 0, cos, sin, rotm, True, f"rope_bwd{l}")
        dqb, dkb, dvb, dbias = na_bwd(P, s["kb"], s["vb"], s["bias"], dmixin, s["ob"], s["stb"], L, Lc, f"na_bwd{l}")
        dy, dxs1, dz, dvec = ssm_out_bwd(s["yf"], s["yb"], s["act"], P, s["dskip"], ssm_norm_g[l][None], dmixin, f"ssm_out_bwd{l}")
        dxf, dbf, dcf, ddf, dxb, dbb, dcb, ddb, ddtb, dal = ssd_bwd(s["act"], P, s["dtb8"], s["al8"], s["hsf"], s["hsb"], dy, L, Lc, f"ssd_bwd{l}")
        dpre = dsilu(s["pre"], [dxf, dxb, dxs1], [dbf, dbb], [dcf, dcb], f"dsilu{l}")
        dxbc, dw8, db8 = conv_bwd(dpre, P, s["w8"], nL, f"conv_bwd{l}")
        ddt = jnp.concatenate([ddf, ddb, jnp.zeros((T, IN_PAD - IN_COLS), F32)], axis=1)
        pieces = [(dqa, C_QA), (dqb, C_QB), (dz, C_Z), (dka, C_KA), (dva[WA_BLK:WA_BLK + T], C_VA), (dkb, C_KB), (dvb, C_VB),
                  (dxbc, C_XBC), (ddt, C_DT)]
        dh1, dwin = in_proj_bwd(pieces, s["h1"], W_in[l], f"in_proj_bwd{l}")
        cw = IN_COLS // N_CHIPS
        gW["w_in"][l] = jnp.stack([dwin[:, k * cw:(k + 1) * cw] for k in range(N_CHIPS)])
        garr = [gW[n][l] for n in _BIG]
        got = swap_halves(garr, f"reduce_d2d{l}")
        chip_sum = [add_halves(garr[a], got[a], cvec, f"reduce_add_pair{l}_{a}") for a in range(len(garr))]
        parts[l] = scatter_chips_sc(chip_sum, f"reduce_ici{l}")
        small[l] = dict(g_ffn=dg_ffn[0], wa_sink=dsink[:WA_HEADS, 0], na_rpb=na_rpb_grad(dbias, l), conv_w=dw8[:S_CONV], conv_b=db8[0],
                        dt_bias=ddtb[:2, :8], a_log=dal[:2, :8], ssm_d=dvec[0].reshape(S_HEADS, S_P).sum(axis=1), norm_g=dvec[1])
        if l > 0:
            p = sv[l - 1]
            dres, dfo, dgsv_end, dg_mix = res_norm_mod_bwd(s["xin"], p["fo"], p["gsv_end"], g_mix[l][None], dh1, dres, nL, f"norm_end_bwd{l - 1}")
        else:
            grad_x, _, dgsv_first, dg_mix = res_norm_mod_bwd(s["xin"], None, gsv_first, g_mix[0][None], dh1, dres, nL, "norm_first_bwd")
            dmod[0][0], dmod[0][1] = dgsv_first[:, 1], dgsv_first[:, 2]
        small[l]["g_mix"] = dg_mix[0]
    for l in range(DEPTH):
        for j in range(6):
            if dmod[l][j] is None:
                dmod[l][j] = jnp.zeros((2, D), F32)
    dmod = jnp.stack([jnp.stack(r, axis=1) for r in dmod])

    f3 = _Flat()
    f3.add("dmod_l", dmod[:, 0].reshape(DEPTH, 6 * D))
    f3.add("dmod_c", dmod[:, 1].reshape(DEPTH, 6 * D))
    f3.add("g_final", dg_final[0])
    for n in ("g_mix", "g_ffn", "wa_sink", "na_rpb", "conv_w", "conv_b", "dt_bias", "a_log", "ssm_d", "norm_g"):
        f3.add(n, jnp.stack([small[l][n] for l in range(DEPTH)]))
    g3, s3 = allgather8(f3.rows(), "reduce_small")
    dmod_all = f3.split_lead(g3)["dmod_l"]
    s3 = f3.split(s3)
    dmodc_tot = s3["dmod_c"]
    col0 = chip * MODW
    G16, G16c = [], []
    for l in range(DEPTH):
        rows = jnp.concatenate([dmod_all[:, l], dmodc_tot[l][None], jnp.zeros((7, 6 * D), F32)], axis=0)
        G16.append(lax.dynamic_slice_in_dim(rows, col0, MODW, axis=1))
        rc = jnp.concatenate([dmodc_tot[l][None], jnp.zeros((15, 6 * D), F32)], axis=0)
        G16c.append(lax.dynamic_slice_in_dim(rc, col0, MODW, axis=1))
    grad_w_mod = matmul_layers(A16, jnp.stack(G16), "tn", "mod_dw")
    dscc_part = matmul_layers(jnp.stack(G16c), w_mod, "nt", "mod_dx")[:, 0].sum(axis=0)
    _, s4 = allgather8(_pad_rows(dscc_part * (mc == 1).astype(F32)), "reduce_cctx")
    dscc = s4.reshape(-1)[:D]
    cc = c_ctx.astype(F32)
    sg = 1.0 / (1.0 + jnp.exp(-cc))
    grad_c_ctx = dscc * (sg * (1.0 + cc * (1.0 - sg)))

    halves = [[sum_slots(parts[l][i], f"reduce_add_chips{l}_{i}") for l in range(DEPTH)] for i in range(len(_BIG))]
    gsh = dict(zip(_BIG, share_halves(halves, "reduce_share")))

    grads = {"c_ctx": grad_c_ctx, "w_mod": grad_w_mod, "b_mod": s3["dmod_l"] + s3["dmod_c"], "g_mix": s3["g_mix"], "w_in": gsh["w_in"],
             "wa_sink": s3["wa_sink"], "na_rpb": s3["na_rpb"],
             "ssm_conv_w": lax.dynamic_slice_in_dim(s3["conv_w"], chip * CW, CW, axis=2), "ssm_conv_b": s3["conv_b"],
             "ssm_dt_bias": s3["dt_bias"], "ssm_a_log": s3["a_log"], "ssm_d": s3["ssm_d"], "ssm_norm_g": s3["norm_g"],
             "w_out": gsh["w_out"], "g_ffn": s3["g_ffn"], "w_ffn_in": gsh["w_ffn_in"], "w_ffn_out": gsh["w_ffn_out"], "g_final": s3["g_final"]}
    wts = {"c_ctx": c_ctx, "w_mod": w_mod, "b_mod": b_mod, "g_mix": g_mix, "w_in": w_in, "wa_sink": wa_sink, "na_rpb": na_rpb,
           "ssm_conv_w": ssm_conv_w, "ssm_conv_b": ssm_conv_b, "ssm_dt_bias": ssm_dt_bias, "ssm_a_log": ssm_a_log, "ssm_d": ssm_d,
           "ssm_norm_g": ssm_norm_g, "w_out": w_out, "g_ffn": g_ffn, "w_ffn_in": w_ffn_in, "w_ffn_out": w_ffn_out, "g_final": g_final}
    ms = {"c_ctx": m_c_ctx, "w_mod": m_w_mod, "b_mod": m_b_mod, "g_mix": m_g_mix, "w_in": m_w_in, "wa_sink": m_wa_sink, "na_rpb": m_na_rpb,
          "ssm_conv_w": m_ssm_conv_w, "ssm_conv_b": m_ssm_conv_b, "ssm_dt_bias": m_ssm_dt_bias, "ssm_a_log": m_ssm_a_log, "ssm_d": m_ssm_d,
          "ssm_norm_g": m_ssm_norm_g, "w_out": m_w_out, "g_ffn": m_g_ffn, "w_ffn_in": m_w_ffn_in, "w_ffn_out": m_w_ffn_out, "g_final": m_g_final}
    vs = {"c_ctx": v_c_ctx, "w_mod": v_w_mod, "b_mod": v_b_mod, "g_mix": v_g_mix, "w_in": v_w_in, "wa_sink": v_wa_sink, "na_rpb": v_na_rpb,
          "ssm_conv_w": v_ssm_conv_w, "ssm_conv_b": v_ssm_conv_b, "ssm_dt_bias": v_ssm_dt_bias, "ssm_a_log": v_ssm_a_log, "ssm_d": v_ssm_d,
          "ssm_norm_g": v_ssm_norm_g, "w_out": v_w_out, "g_ffn": v_g_ffn, "w_ffn_in": v_w_ffn_in, "w_ffn_out": v_w_ffn_out, "g_final": v_g_final}
    names = list(wts)
    grads = {n: grads[n].reshape(wts[n].shape).astype(F32) for n in names}
    big = ("w_mod", "w_in", "w_out", "w_ffn_in", "w_ffn_out")
    delta, new_m, new_v = {}, {}, {}
    for n in big:
        delta[n], new_m[n], new_v[n] = adamw(wts[n], grads[n], ms[n], vs[n], f"adamw_{n}")
    packs = []
    for src in (wts, grads, ms, vs):
        f = _Flat()
        for n in names:
            if n not in big:
                f.add(n, src[n])
        packs.append(f)
    d_, m_, v_ = adamw(*[f.rows()[None] for f in packs], "adamw_small")
    for dst, rows in ((delta, d_), (new_m, m_), (new_v, v_)):
        dst.update(packs[0].split(rows[0]))

    return (loss, grad_x[:L][None], *[grads[n] for n in names], *[delta[n] for n in names],
            *[new_m[n] for n in names], *[new_v[n] for n in names])
```

```python
import functools

import numpy as np
import jax
import jax.numpy as jnp
from jax import lax
from jax.experimental import pallas as pl
from jax.experimental.pallas import tpu as pltpu
from jax.experimental.pallas import tpu_sc as plsc

F32 = jnp.float32
BF16 = jnp.bfloat16
_MXU = jnp.bfloat16
_HI = lax.Precision.HIGHEST
MESH = pl.DeviceIdType.MESH

D = 1024
HD = 64
GRID_W = 64
EPS = 1e-6
ROPE_BASE = 10000.0
WA_HEADS, WA_KV = 4, 2
WA_BLK = 128
NA_HEADS, NA_KH, NA_KW = 4, 8, 16
S_HEADS, S_P, S_INNER, S_GROUPS, S_N, S_CONV, S_Q = 8, 64, 512, 2, 128, 7, 128
D_FF = 2816
IN_COLS = 2832
IN_PAD = 2944
C_QA, C_QB, C_Z, C_KA, C_VA, C_KB, C_VB, C_XBC, C_DT = 0, 256, 512, 1024, 1152, 1280, 1536, 1792, 2816
ADAM_LR, ADAM_B1, ADAM_B2, ADAM_EPS, ADAM_WD, ADAM_STEP = 0.001, 0.9, 0.999, 1e-08, 0.01, 10

TR = 256
NEG = -1e30
VMEM_CAP = 56 * 1024 * 1024


PIN_BYTES = 256 * 1024


def _is_big(a):
    return hasattr(a, "shape") and len(a.shape) >= 2 and int(np.prod(a.shape)) * jnp.dtype(a.dtype).itemsize >= PIN_BYTES


def _pc(body, *, out_shape, pin=True, **kw):
    if not pin:
        return pl.pallas_call(body, out_shape=out_shape, **kw)
    one = isinstance(out_shape, jax.ShapeDtypeStruct)
    outs = [pltpu.HBM(s.shape, s.dtype) if _is_big(s) else s for s in ([out_shape] if one else out_shape)]
    call = pl.pallas_call(body, out_shape=outs[0] if one else outs, **kw)
    return lambda *args: call(*[pltpu.with_memory_space_constraint(a, pltpu.HBM) if _is_big(a) else a for a in args])


def _cp(sem=None, vmem=None):
    kw = {}
    if sem is not None:
        kw["dimension_semantics"] = sem
    if vmem is not None:
        kw["vmem_limit_bytes"] = int(min(max(vmem, 16 * 1024 * 1024), VMEM_CAP))
    return pltpu.CompilerParams(**kw)


def _sds(shape, dtype):
    return jax.ShapeDtypeStruct(tuple(shape), dtype)


_DIMS = {"nn": ((1,), (0,)), "nt": ((1,), (1,)), "tn": ((0,), (0,))}


def _dg(a, b, dims):
    return lax.dot_general(a.astype(_MXU), b.astype(_MXU), (dims, ((), ())), preferred_element_type=F32)


@functools.partial(jax.custom_vjp, nondiff_argnums=(2,))
def bdot(a, b, mode):
    return _dg(a, b, _DIMS[mode])


def _bdot_fwd(a, b, mode):
    return bdot(a, b, mode), (a, b)


def _bdot_bwd(mode, res, g):
    a, b = res
    if mode == "nn":
        return bdot(g, b, "nt"), bdot(a, g, "tn")
    if mode == "nt":
        return bdot(g, b, "nn"), bdot(g, a, "tn")
    return bdot(b, g, "nt"), bdot(a, g, "nn")


bdot.defvjp(_bdot_fwd, _bdot_bwd)


def hdot(a, b, mode="nn"):
    return lax.dot_general(a, b, (_DIMS[mode], ((), ())), precision=_HI, preferred_element_type=F32)


def _silu(x):
    return x / (1.0 + jnp.exp(-x))


def _softplus(x):
    return jnp.maximum(x, 0.0) + jnp.log(1.0 + jnp.exp(-jnp.abs(x)))


def _div_tile(n, cap, mult):
    if n <= cap:
        return n
    best = None
    for t in range(mult, cap + 1, mult):
        if n % t == 0:
            best = t
    assert best is not None, (n, cap, mult)
    return best


def matmul(a, b, mode, out_dtype, name, tm=640, tn=1536, tk=1408, hi=False):
    if mode == "tn":
        K, M = a.shape
    else:
        M, K = a.shape
    N = b.shape[0] if mode == "nt" else b.shape[1]
    tm = _div_tile(M, tm, 128 if mode == "tn" else 16)
    tn = _div_tile(N, tn, 128)
    tk = _div_tile(K, tk, 128 if mode != "tn" else 16)
    nk = K // tk
    dims = _DIMS[mode]

    def body(a_ref, b_ref, o_ref, *acc):
        if hi:
            part = lax.dot_general(a_ref[...], b_ref[...], (dims, ((), ())), precision=_HI, preferred_element_type=F32)
        else:
            part = _dg(a_ref[...], b_ref[...], dims)
        if nk == 1:
            o_ref[...] = part.astype(o_ref.dtype)
        else:
            k = pl.program_id(2)

            @pl.when(k == 0)
            def _():
                acc[0][...] = part

            @pl.when(k > 0)
            def _():
                acc[0][...] += part

            @pl.when(k == nk - 1)
            def _():
                o_ref[...] = acc[0][...].astype(o_ref.dtype)

    if mode == "tn":
        a_spec = pl.BlockSpec((tk, tm), lambda i, j, k: (k, i))
    else:
        a_spec = pl.BlockSpec((tm, tk), lambda i, j, k: (i, k))
    if mode == "nt":
        b_spec = pl.BlockSpec((tn, tk), lambda i, j, k: (j, k))
    else:
        b_spec = pl.BlockSpec((tk, tn), lambda i, j, k: (k, j))
    isz = lambda x: jnp.dtype(x.dtype).itemsize
    vmem = 2 * (tm * tk * isz(a) + tk * tn * isz(b) + tm * tn * jnp.dtype(out_dtype).itemsize) + 3 * tm * tn * 4
    return _pc(
        body, name=name, grid=(M // tm, N // tn, nk),
        in_specs=[a_spec, b_spec], out_specs=pl.BlockSpec((tm, tn), lambda i, j, k: (i, j)),
        out_shape=_sds((M, N), out_dtype),
        scratch_shapes=[pltpu.VMEM((tm, tn), F32)] if nk > 1 else [],
        compiler_params=_cp(("parallel", "parallel", "arbitrary"), vmem + (8 << 20)),
    )(a, b)


def matmul_layers(a, b, mode, name):
    nl = b.shape[0]
    a3 = a if a.ndim == 3 else a[None]
    shared = a3.shape[0] == 1
    M = a3.shape[2] if mode == "tn" else a3.shape[1]
    N = b.shape[1] if mode == "nt" else b.shape[2]

    def body(a_ref, b_ref, o_ref):
        o_ref[0] = _dg(a_ref[0], b_ref[0], _DIMS[mode])

    return _pc(body, name=name, grid=(nl,),
               in_specs=[pl.BlockSpec((1,) + a3.shape[1:], (lambda l: (0, 0, 0)) if shared else (lambda l: (l, 0, 0))),
                         pl.BlockSpec((1,) + b.shape[1:], lambda l: (l, 0, 0))],
               out_specs=pl.BlockSpec((1, M, N), lambda l: (l, 0, 0)), out_shape=_sds((nl, M, N), F32),
               compiler_params=_cp(("parallel",), 48 << 20))(a3, b)


def out_proj_fwd(pieces, w, name):
    T = pieces[0][0].shape[0]
    arrs, offs = [a for a, _ in pieces], [o for _, o in pieces]
    n = len(arrs)
    tm = _div_tile(T, 640, 16)

    def body(*refs):
        w_ref, o_ref = refs[n], refs[n + 1]
        acc = None
        for j in range(n):
            part = _dg(refs[j][...], w_ref[offs[j]:offs[j] + arrs[j].shape[1], :], _DIMS["nn"])
            acc = part if acc is None else acc + part
        o_ref[...] = acc.astype(o_ref.dtype)

    return _pc(body, name=name, grid=(T // tm,),
               in_specs=[pl.BlockSpec((tm, a.shape[1]), lambda i: (i, 0)) for a in arrs] + [pl.BlockSpec(w.shape, lambda i: (0, 0))],
               out_specs=pl.BlockSpec((tm, w.shape[1]), lambda i: (i, 0)), out_shape=_sds((T, w.shape[1]), BF16),
               compiler_params=_cp(("parallel",), 32 << 20))(*arrs, w)


def out_proj_dw(pieces, dy, name):
    T, N = dy.shape
    arrs, offs = [a for a, _ in pieces], [o for _, o in pieces]
    n = len(arrs)
    rows = sum(a.shape[1] for a in arrs)
    tn = 512

    def body(*refs):
        d_ref, o_ref = refs[n], refs[n + 1]
        for j in range(n):
            o_ref[offs[j]:offs[j] + arrs[j].shape[1], :] = _dg(refs[j][...], d_ref[...], _DIMS["tn"]).astype(o_ref.dtype)

    return _pc(body, name=name, grid=(N // tn,),
               in_specs=[pl.BlockSpec(a.shape, lambda j: (0, 0)) for a in arrs] + [pl.BlockSpec((T, tn), lambda j: (0, j))],
               out_specs=pl.BlockSpec((rows, tn), lambda j: (0, j)), out_shape=_sds((rows, N), BF16),
               compiler_params=_cp(("parallel",), 48 << 20))(*arrs, dy)


def in_proj_bwd(pieces, h1, w, name):
    T = h1.shape[0]
    arrs = [a for a, _ in pieces]
    offs = [o for _, o in pieces]
    wid = [a.shape[1] for a in arrs]
    n = len(arrs)
    assert sum(wid) == IN_PAD, "the pieces must tile all columns of P"
    tm = _div_tile(T, 640, 16)

    def dx_body(*refs):
        w_ref, o_ref = refs[n], refs[n + 1]
        acc = None
        for j in range(n):
            part = _dg(refs[j][...], w_ref[:, offs[j]:offs[j] + wid[j]], _DIMS["nt"])
            acc = part if acc is None else acc + part
        o_ref[...] = acc.astype(o_ref.dtype)

    dh1 = _pc(dx_body, name=name + "_dx", grid=(T // tm,),
              in_specs=[pl.BlockSpec((tm, wj), lambda i: (i, 0)) for wj in wid] + [pl.BlockSpec((D, IN_PAD), lambda i: (0, 0))],
              out_specs=pl.BlockSpec((tm, D), lambda i: (i, 0)), out_shape=_sds((T, D), BF16),
              compiler_params=_cp(("parallel",), 40 << 20))(*arrs, w)

    tmd, nk = 512, 4
    tk = T // nk

    def dw_body(h_ref, *refs):
        o_ref, acc = refs[n], refs[n + 1]
        k = pl.program_id(1)

        @pl.when(k == 0)
        def _():
            acc[...] = jnp.zeros_like(acc)

        for j in range(n):
            acc[:, offs[j]:offs[j] + wid[j]] += _dg(h_ref[...], refs[j][...], _DIMS["tn"])

        @pl.when(k == nk - 1)
        def _():
            o_ref[...] = acc[...].astype(o_ref.dtype)

    dw = _pc(dw_body, name=name + "_dw", grid=(D // tmd, nk),
             in_specs=[pl.BlockSpec((tk, tmd), lambda i, k: (k, i))] + [pl.BlockSpec((tk, wj), lambda i, k: (k, 0)) for wj in wid],
             out_specs=pl.BlockSpec((tmd, IN_PAD), lambda i, k: (i, 0)), out_shape=_sds((D, IN_PAD), BF16),
             scratch_shapes=[pltpu.VMEM((tmd, IN_PAD), F32)], compiler_params=_cp(("parallel", "arbitrary"), 48 << 20))(h1, *arrs)
    return dh1, dw


def _norm_mod(xo, shift, scale, g):
    r = lax.rsqrt(jnp.mean(xo * xo, axis=-1, keepdims=True) + EPS)
    return (xo * r) * g * (1.0 + scale) + shift


def res_norm_mod(x, y, gsv, g, nL, name):
    T = x.shape[0]
    has_y = y is not None

    def body(*refs):
        if has_y:
            x_ref, y_ref, gsv_ref, g_ref, xo_ref, h_ref = refs
            xo = x_ref[...] + gsv_ref[0, 0:1, :] * y_ref[...]
            xo_ref[...] = xo
        else:
            x_ref, gsv_ref, g_ref, h_ref = refs
            xo = x_ref[...]
        h_ref[...] = _norm_mod(xo, gsv_ref[0, 1:2, :], gsv_ref[0, 2:3, :], g_ref[...]).astype(h_ref.dtype)

    row = pl.BlockSpec((TR, D), lambda i: (i, 0))
    in_specs = [row] + ([row] if has_y else []) + [pl.BlockSpec((1, 8, D), lambda i: (i // nL, 0, 0)),
                                                     pl.BlockSpec((1, D), lambda i: (0, 0))]
    out_specs = ([row] if has_y else []) + [row]
    out_shape = ([_sds((T, D), F32)] if has_y else []) + [_sds((T, D), BF16)]
    args = (x, y, gsv, g) if has_y else (x, gsv, g)
    outs = _pc(body, name=name, grid=(T // TR,), in_specs=in_specs, out_specs=out_specs, out_shape=out_shape,
               compiler_params=_cp(("arbitrary",), 24 << 20))(*args)
    return (outs[0], outs[1]) if has_y else (None, outs[0])


def res_norm_mod_bwd(xo, y, gsv, g, dh, dres, nL, name):
    T = xo.shape[0]
    has_y = y is not None

    def body(*refs):
        if has_y:
            xo_ref, y_ref, gsv_ref, g_ref, dh_ref, dres_ref, dx_ref, dy_ref, dgsv_ref, dg_ref = refs
        else:
            xo_ref, gsv_ref, g_ref, dh_ref, dres_ref, dx_ref, dgsv_ref, dg_ref = refs
        i = pl.program_id(0)

        @pl.when((i == 0) | (i == nL))
        def _():
            dgsv_ref[...] = jnp.zeros_like(dgsv_ref)

        @pl.when(i == 0)
        def _():
            dg_ref[...] = jnp.zeros_like(dg_ref)

        _, vjp = jax.vjp(_norm_mod, xo_ref[...], gsv_ref[0, 1:2, :], gsv_ref[0, 2:3, :], g_ref[...])
        dxn, dshift, dscale, dg = vjp(dh_ref[...].astype(F32))
        dxo = dres_ref[...] + dxn
        dx_ref[...] = dxo
        if has_y:
            dy_ref[...] = (gsv_ref[0, 0:1, :] * dxo).astype(dy_ref.dtype)
            dgsv_ref[0, 0:1, :] += jnp.sum(y_ref[...] * dxo, axis=0, keepdims=True)
        dgsv_ref[0, 1:2, :] += dshift
        dgsv_ref[0, 2:3, :] += dscale
        dg_ref[0:1, :] += dg

    row = pl.BlockSpec((TR, D), lambda i: (i, 0))
    gspec = pl.BlockSpec((1, 8, D), lambda i: (i // nL, 0, 0))
    in_specs = [row] + ([row] if has_y else []) + [gspec, pl.BlockSpec((1, D), lambda i: (0, 0)), row, row]
    out_specs = [row] + ([row] if has_y else []) + [gspec, pl.BlockSpec((8, D), lambda i: (0, 0))]
    out_shape = [_sds((T, D), F32)] + ([_sds((T, D), BF16)] if has_y else []) + [_sds((2, 8, D), F32), _sds((8, D), F32)]
    args = (xo, y, gsv, g, dh, dres) if has_y else (xo, gsv, g, dh, dres)
    outs = _pc(body, name=name, grid=(T // TR,), in_specs=in_specs, out_specs=out_specs, out_shape=out_shape,
               compiler_params=_cp(("arbitrary",), 32 << 20))(*args)
    if has_y:
        return outs
    return outs[0], None, outs[1], outs[2]


def final_loss(x, y, gsv, g, target, nL, name):
    T = x.shape[0]

    def lossf(xo, gv, t):
        yn = (xo * lax.rsqrt(jnp.mean(xo * xo, axis=-1, keepdims=True) + EPS)) * gv
        e = yn - t
        return 0.5 * jnp.sum(jnp.sum(e * e, axis=-1, keepdims=True) * (1.0 / D), axis=0, keepdims=True)

    def body(x_ref, y_ref, gsv_ref, g_ref, t_ref, loss_ref, dx_ref, dy_ref, dgsv_ref, dg_ref):
        i = pl.program_id(0)

        @pl.when(i == 0)
        def _():
            loss_ref[...] = jnp.zeros_like(loss_ref)
            dg_ref[...] = jnp.zeros_like(dg_ref)

        @pl.when((i == 0) | (i == nL))
        def _():
            dgsv_ref[...] = jnp.zeros_like(dgsv_ref)

        @pl.when(i < nL)
        def _():
            gate = gsv_ref[0, 0:1, :]
            yv = y_ref[...]
            xo = x_ref[...] + gate * yv
            lv, vjp = jax.vjp(lossf, xo, g_ref[...], t_ref[...])
            dxo, dg, _ = vjp(jnp.ones((1, 1), F32))
            loss_ref[...] += jnp.broadcast_to(lv, loss_ref.shape)
            dx_ref[...] = dxo
            dy_ref[...] = (gate * dxo).astype(dy_ref.dtype)
            dgsv_ref[0, 0:1, :] += jnp.sum(yv * dxo, axis=0, keepdims=True)
            dg_ref[0:1, :] += dg

        @pl.when(i >= nL)
        def _():
            dx_ref[...] = jnp.zeros_like(dx_ref)
            dy_ref[...] = jnp.zeros_like(dy_ref)

    row = pl.BlockSpec((TR, D), lambda i: (i, 0))
    gspec = pl.BlockSpec((1, 8, D), lambda i: (i // nL, 0, 0))
    return _pc(
        body, name=name, grid=(T // TR,),
        in_specs=[row, row, gspec, pl.BlockSpec((1, D), lambda i: (0, 0)),
                  pl.BlockSpec((TR, D), lambda i: (jnp.minimum(i, nL - 1), 0))],
        out_specs=[pl.BlockSpec((8, 128), lambda i: (0, 0)), row, row, gspec, pl.BlockSpec((8, D), lambda i: (0, 0))],
        out_shape=[_sds((8, 128), F32), _sds((T, D), F32), _sds((T, D), BF16), _sds((2, 8, D), F32), _sds((8, D), F32)],
        compiler_params=_cp(("arbitrary",), 32 << 20),
    )(x, y, gsv, g, target)


FI_BLK = 2 * D_FF // 4


def _fi_chip(j):
    return (j % 2) * 2 + j // 2


def matmul_fi(a, b, mode, out_dtype, name):
    T = a.shape[0]
    if mode == "tn":
        tmd = 512

        def body(a_ref, b_ref, o_ref):
            o_ref[0] = _dg(a_ref[...], b_ref[...], _DIMS["tn"]).astype(o_ref.dtype)

        return _pc(body, name=name, grid=(D // tmd, 4),
                   in_specs=[pl.BlockSpec((T, tmd), lambda i, j: (0, i)), pl.BlockSpec((T, FI_BLK), lambda i, j: (0, j))],
                   out_specs=pl.BlockSpec((1, tmd, FI_BLK), lambda i, j: (_fi_chip(j), i, 0)),
                   out_shape=_sds((4, D, FI_BLK), out_dtype), compiler_params=_cp(("parallel", "arbitrary"), 48 << 20))(a, b)
    if mode == "nn":
        tm = _div_tile(T, 1280, 16)

        def body(a_ref, b_ref, o_ref):
            o_ref[...] = _dg(a_ref[...], b_ref[0], _DIMS["nn"]).astype(o_ref.dtype)

        return _pc(body, name=name, grid=(T // tm, 4),
                   in_specs=[pl.BlockSpec((tm, D), lambda i, j: (i, 0)), pl.BlockSpec((1, D, FI_BLK), lambda i, j: (_fi_chip(j), 0, 0))],
                   out_specs=pl.BlockSpec((tm, FI_BLK), lambda i, j: (i, j)), out_shape=_sds((T, 4 * FI_BLK), out_dtype),
                   compiler_params=_cp(("parallel", "arbitrary"), 40 << 20))(a, b)
    tm = _div_tile(T, 640, 16)

    def body(a_ref, b_ref, o_ref):
        acc = None
        for k in range(4):
            part = _dg(a_ref[:, k * FI_BLK:(k + 1) * FI_BLK], b_ref[_fi_chip(k)], _DIMS["nt"])
            acc = part if acc is None else acc + part
        o_ref[...] = acc.astype(o_ref.dtype)

    return _pc(body, name=name, grid=(T // tm,),
               in_specs=[pl.BlockSpec((tm, 4 * FI_BLK), lambda i: (i, 0)), pl.BlockSpec((4, D, FI_BLK), lambda i: (0, 0, 0))],
               out_specs=pl.BlockSpec((tm, D), lambda i: (i, 0)), out_shape=_sds((T, D), out_dtype),
               compiler_params=_cp(("parallel",), VMEM_CAP))(a, b)


def _swiglu(gate, up):
    return _silu(gate) * up


def swiglu_fwd(gu, name):
    T = gu.shape[0]

    def body(x_ref, o_ref):
        o_ref[...] = _swiglu(x_ref[:, :FI_BLK].astype(F32), x_ref[:, FI_BLK:].astype(F32)).astype(o_ref.dtype)

    return _pc(body, name=name, grid=(T // TR, 2), in_specs=[pl.BlockSpec((TR, 2 * FI_BLK), lambda i, j: (i, j))],
               out_specs=pl.BlockSpec((TR, FI_BLK), lambda i, j: (i, j)), out_shape=_sds((T, D_FF), BF16),
               compiler_params=_cp(("parallel", "parallel"), 24 << 20))(gu)


def swiglu_bwd(gu, dact, name):
    T = gu.shape[0]

    def body(x_ref, d_ref, o_ref):
        g, u, d = x_ref[:, :FI_BLK].astype(F32), x_ref[:, FI_BLK:].astype(F32), d_ref[...].astype(F32)
        sg = 1.0 / (1.0 + jnp.exp(-g))
        sl = g * sg
        o_ref[:, :FI_BLK] = (d * u * (sg + sl * (1.0 - sg))).astype(o_ref.dtype)
        o_ref[:, FI_BLK:] = (d * sl).astype(o_ref.dtype)

    return _pc(body, name=name, grid=(T // TR, 2),
               in_specs=[pl.BlockSpec((TR, 2 * FI_BLK), lambda i, j: (i, j)), pl.BlockSpec((TR, FI_BLK), lambda i, j: (i, j))],
               out_specs=pl.BlockSpec((TR, 2 * FI_BLK), lambda i, j: (i, j)), out_shape=_sds((T, 2 * D_FF), BF16),
               compiler_params=_cp(("parallel", "parallel"), 32 << 20))(gu, dact)


def rope_tables(L, Lc):
    t = np.arange(L)
    rows, cols = t // GRID_W, t % GRID_W
    inv = ROPE_BASE ** (-np.arange(16, dtype=np.float32) / 16)
    lane = np.arange(64)
    pos = np.where((lane // 32)[None, :] == 0, rows[:, None], cols[:, None]).astype(np.float32)
    ang = jnp.asarray(pos) * jnp.asarray(inv[lane % 16])[None, :]
    cos = jnp.concatenate([jnp.cos(ang), jnp.ones((Lc, 64), F32)], axis=0)
    sin = jnp.concatenate([jnp.sin(ang), jnp.zeros((Lc, 64), F32)], axis=0)
    R = np.zeros((128, 128), np.float32)
    for i in range(128):
        if (i % 32) < 16:
            R[i + 16, i] = -1.0
        else:
            R[i - 16, i] = 1.0
    return jnp.tile(cos, (1, 2)), jnp.tile(sin, (1, 2)), jnp.asarray(R)


def rope_apply(q_src, q_col, k_src, k_col, cos, sin, R, transpose, name, kv_src=None):
    T = cos.shape[0]
    with_kv = kv_src is not None

    def rot(x, c, s, Rm):
        if transpose:
            return x * c + hdot(x * s, Rm, "nt")
        return x * c + hdot(x, Rm) * s

    def body(q_ref, k_ref, c_ref, s_ref, R_ref, *rest):
        qo_ref, ko_ref = rest[-4:-2] if with_kv else rest
        c, s, Rm = c_ref[...], s_ref[...], R_ref[...]
        for j in range(2):
            qo_ref[:, j * 128:(j + 1) * 128] = rot(q_ref[:, j * 128:(j + 1) * 128].astype(F32), c, s, Rm).astype(qo_ref.dtype)
        ko_ref[...] = rot(k_ref[...].astype(F32), c, s, Rm).astype(ko_ref.dtype)
        if with_kv:
            rest[-2][...] = rest[0][...].astype(BF16)
            rest[-1][...] = rest[1][...].astype(BF16)

    tab = pl.BlockSpec((TR, 128), lambda i: (i, 0))
    wide = pl.BlockSpec((TR, 256), lambda i: (i, 0))
    kv_in = [pl.BlockSpec((TR, 256), lambda i: (i, C_KB // 256)), pl.BlockSpec((TR, 256), lambda i: (i, C_VB // 256))] if with_kv else []
    return _pc(body, name=name, grid=(T // TR,),
               in_specs=[pl.BlockSpec((TR, 256), lambda i: (i, q_col)), pl.BlockSpec((TR, 128), lambda i: (i, k_col)),
                         tab, tab, pl.BlockSpec((128, 128), lambda i: (0, 0))] + kv_in,
               out_specs=[wide, tab] + ([wide, wide] if with_kv else []),
               out_shape=[_sds((T, 256), BF16), _sds((T, 128), BF16)] + ([_sds((T, 256), BF16)] * 2 if with_kv else []),
               compiler_params=_cp(("parallel",), 16 << 20))(q_src, k_src, cos, sin, R, *([kv_src, kv_src] if with_kv else []))


_SCALE = HD ** -0.5


def _attn_tile(qh, ks, vs, extra):
    ss = []
    for k, add in ks:
        s = _dg(qh, k, _DIMS["nt"]) * _SCALE
        ss.append(s if add is None else s + add)
    m = ss[0].max(axis=-1, keepdims=True)
    for s in ss[1:]:
        m = jnp.maximum(m, s.max(axis=-1, keepdims=True))
    if extra is not None:
        m = jnp.maximum(m, extra)
    ps = [jnp.exp(s - m) for s in ss]
    den = ps[0].sum(axis=-1, keepdims=True)
    for p in ps[1:]:
        den = den + p.sum(axis=-1, keepdims=True)
    if extra is not None:
        den = den + jnp.exp(extra - m)
    num = _dg(ps[0], vs[0], _DIMS["nn"])
    for p, v in zip(ps[1:], vs[1:]):
        num = num + _dg(p, v, _DIMS["nn"])
    linv = 1.0 / den
    return num * linv, m, linv


def _attn_bwd_tile(qh, ks, vs, extra, m, linv, oh, doh):
    delta = jnp.sum(doh * oh, axis=-1, keepdims=True)
    dq = None
    dks, dvs, dss = [], [], []
    for (k, add), v in zip(ks, vs):
        s = _dg(qh, k, _DIMS["nt"]) * _SCALE
        if add is not None:
            s = s + add
        p = jnp.exp(s - m) * linv
        dvs.append(_dg(p, doh, _DIMS["tn"]))
        ds = p * (_dg(doh, v, _DIMS["nt"]) - delta)
        dss.append(ds)
        dsq = ds * _SCALE
        part = _dg(dsq, k, _DIMS["nn"])
        dq = part if dq is None else dq + part
        dks.append(_dg(dsq, qh, _DIMS["tn"]))
    dextra = None
    if extra is not None:
        dextra = -(jnp.exp(extra - m) * linv * delta)
    return dq, dks, dvs, dss, dextra


def _wa_mask(n, L):
    qpos = n * WA_BLK + lax.broadcasted_iota(jnp.int32, (WA_BLK, 3 * WA_BLK), 0)
    kpos = (n - 1) * WA_BLK + lax.broadcasted_iota(jnp.int32, (WA_BLK, 3 * WA_BLK), 1)
    ok = (jnp.abs(qpos - kpos) <= WA_BLK) & (kpos >= 0) & (kpos < L)
    return jnp.where(ok, 0.0, NEG).astype(F32)


WA_BPS = 2
_WA_PAIRS = (((0, 0), (1, 3), False), ((1, 2), (0, 1), True))


def _swap_halves_lanes(a):
    return jnp.concatenate([a[:, HD:], a[:, :HD]], axis=1)


def _wa_specs(L, Lc):
    nb = L // WA_BLK
    cb = L // Lc

    def blk(j):
        return pl.BlockSpec((WA_BLK, 128), lambda s: (jnp.clip(s * WA_BPS - 1 + j, 0, nb - 1), 0))

    return nb, [blk(j) for j in range(WA_BPS + 2)] + [pl.BlockSpec((Lc, 128), lambda s: (cb, 0))]


def _wa_pair_q(q_ref, qs, lo, hi):
    a = q_ref[qs, lo[0] * 128:(lo[0] + 1) * 128]
    b = q_ref[qs, hi[0] * 128:(hi[0] + 1) * 128]
    lane = lax.broadcasted_iota(jnp.int32, a.shape, 1)
    zero = jnp.zeros_like(a)
    return jnp.concatenate([jnp.where(lane < HD, a, zero), jnp.where(lane >= HD, b, zero)], axis=0)


def _wa_pair_vec(ref, qs, lo, hi, base=0):
    return jnp.concatenate([ref[qs, base + lo[1]:base + lo[1] + 1], ref[qs, base + hi[1]:base + hi[1] + 1]], axis=0)


def _wa_pair_sink(s_ref, n, lo, hi):
    return jnp.concatenate([jnp.broadcast_to(s_ref[lo[1]:lo[1] + 1, 0:1], (n, 1)), jnp.broadcast_to(s_ref[hi[1]:hi[1] + 1, 0:1], (n, 1))], axis=0)


def win_attn_fwd(qr, kr, krs, v, vs, sink, L, Lc, name):
    T = L + Lc
    nb, specs = _wa_specs(L, Lc)
    nk = WA_BPS + 2
    QB = WA_BPS * WA_BLK
    nlat = nb // WA_BPS

    def body(q_ref, *refs):
        groups = [refs[g * (nk + 1):(g + 1) * (nk + 1)] for g in range(4)]
        s_ref, o_ref, st_ref = refs[-3], refs[-2], refs[-1]
        s = pl.program_id(0)

        def run(qs, n, ks_of, vs_of):
            outs = []
            for lo, hi, swapped in _WA_PAIRS:
                kb, vb = groups[1 if swapped else 0], groups[3 if swapped else 2]
                o2, m2, l2 = _attn_tile(_wa_pair_q(q_ref, qs, lo, hi), ks_of(kb), vs_of(vb), _wa_pair_sink(s_ref, n, lo, hi))
                outs.append(o2)
                for r, (_, h) in enumerate((lo, hi)):
                    st_ref[qs, h:h + 1] = m2[r * n:(r + 1) * n]
                    st_ref[qs, WA_HEADS + h:WA_HEADS + h + 1] = l2[r * n:(r + 1) * n]
            lane = lax.broadcasted_iota(jnp.int32, (n, 128), 1)
            o_ref[qs, 0:128] = jnp.where(lane < HD, outs[0][:n], outs[1][n:]).astype(o_ref.dtype)
            o_ref[qs, 128:256] = jnp.where(lane < HD, outs[1][:n], outs[0][n:]).astype(o_ref.dtype)

        @pl.when(s < nlat)
        def _():
            for b in range(WA_BPS):
                m1 = _wa_mask(s * WA_BPS + b, L)
                mask = jnp.concatenate([m1, m1], axis=0)
                cat = lambda g: jnp.concatenate([g[b + j][...] for j in range(3)], axis=0)
                run(slice(b * WA_BLK, (b + 1) * WA_BLK), WA_BLK,
                    lambda kb: [(cat(kb), mask), (kb[nk][...], None)], lambda vb: [cat(vb), vb[nk][...]])

        @pl.when(s >= nlat)
        def _():
            run(slice(None), QB, lambda kb: [(kb[nk][...], None)], lambda vb: [vb[nk][...]])

    qspec = pl.BlockSpec((QB, 256), lambda s: (s, 0))
    return _pc(body, name=name, grid=(T // QB,),
               in_specs=[qspec] + specs * 4 + [pl.BlockSpec((8, 128), lambda s: (0, 0))],
               out_specs=[qspec, pl.BlockSpec((QB, 8), lambda s: (s, 0))], out_shape=[_sds((T, 256), BF16), _sds((T, 8), F32)],
               compiler_params=_cp(("arbitrary",), 40 << 20))(qr, *([kr] * (nk + 1)), *([krs] * (nk + 1)), *([v] * (nk + 1)), *([vs] * (nk + 1)), sink)


def win_attn_bwd(qr, kr, krs, v, vs, sink, do_src, o, stats, L, Lc, name):
    T = L + Lc
    nb, specs = _wa_specs(L, Lc)
    nk = WA_BPS + 2
    QB = WA_BPS * WA_BLK
    nlat = nb // WA_BPS
    cx = WA_BLK + L

    def body(q_ref, *refs):
        groups = [refs[g * (nk + 1):(g + 1) * (nk + 1)] for g in range(4)]
        s_ref, do_ref, o_ref, st_ref, dq_ref, dk_ref, dks_ref, dv_ref, dvs_ref, ds_ref = refs[4 * (nk + 1):]
        s = pl.program_id(0)

        @pl.when(s == 0)
        def _():
            for r in (dk_ref, dks_ref, dv_ref, dvs_ref, ds_ref):
                r[...] = jnp.zeros_like(r)

        def run(qs, n, ks_of, vs_of, rows):
            lane = lax.broadcasted_iota(jnp.int32, (n, 128), 1)
            dqs = []
            for lo, hi, swapped in _WA_PAIRS:
                kb, vb = groups[1 if swapped else 0], groups[3 if swapped else 2]
                dka, dva = (dks_ref, dvs_ref) if swapped else (dk_ref, dv_ref)
                pair = lambda ref: jnp.concatenate([jnp.where(lane < HD, ref[qs, lo[0] * 128:(lo[0] + 1) * 128].astype(F32), 0.0),
                                                    jnp.where(lane >= HD, ref[qs, hi[0] * 128:(hi[0] + 1) * 128].astype(F32), 0.0)], axis=0)
                dq2, dks, dvs, _, dex = _attn_bwd_tile(_wa_pair_q(q_ref, qs, lo, hi), ks_of(kb), vs_of(vb), _wa_pair_sink(s_ref, n, lo, hi),
                                                       _wa_pair_vec(st_ref, qs, lo, hi), _wa_pair_vec(st_ref, qs, lo, hi, WA_HEADS), pair(o_ref), pair(do_ref))
                dqs.append(dq2)
                for r, (_, h) in enumerate((lo, hi)):
                    ds_ref[h:h + 1, :] += jnp.broadcast_to(jnp.sum(dex[r * n:(r + 1) * n], axis=0, keepdims=True), (1, 128))
                if rows is not None:
                    dka[rows, :] += dks[0]
                    dva[rows, :] += dvs[0]
                dka[cx:cx + Lc, :] += dks[-1]
                dva[cx:cx + Lc, :] += dvs[-1]
            dq_ref[qs, 0:128] = jnp.where(lane < HD, dqs[0][:n], dqs[1][n:])
            dq_ref[qs, 128:256] = jnp.where(lane < HD, dqs[1][:n], dqs[0][n:])

        @pl.when(s < nlat)
        def _():
            for b in range(WA_BPS):
                nblk = s * WA_BPS + b
                m1 = _wa_mask(nblk, L)
                mask = jnp.concatenate([m1, m1], axis=0)
                cat = lambda g: jnp.concatenate([g[b + j][...] for j in range(3)], axis=0)
                run(slice(b * WA_BLK, (b + 1) * WA_BLK), WA_BLK, lambda kb: [(cat(kb), mask), (kb[nk][...], None)],
                    lambda vb: [cat(vb), vb[nk][...]], pl.ds(pl.multiple_of(nblk * WA_BLK, WA_BLK), 3 * WA_BLK))

        @pl.when(s >= nlat)
        def _():
            run(slice(None), QB, lambda kb: [(kb[nk][...], None)], lambda vb: [vb[nk][...]], None)

    qspec = pl.BlockSpec((QB, 256), lambda s: (s, 0))
    acc_spec = pl.BlockSpec((T + 2 * WA_BLK, 128), lambda s: (0, 0))
    acc_shape = _sds((T + 2 * WA_BLK, 128), F32)
    return _pc(body, name=name, grid=(T // QB,),
               in_specs=[qspec] + specs * 4 + [pl.BlockSpec((8, 128), lambda s: (0, 0)), qspec, qspec, pl.BlockSpec((QB, 8), lambda s: (s, 0))],
               out_specs=[qspec, acc_spec, acc_spec, acc_spec, acc_spec, pl.BlockSpec((8, 128), lambda s: (0, 0))],
               out_shape=[_sds((T, 256), F32), acc_shape, acc_shape, acc_shape, acc_shape, _sds((8, 128), F32)],
               compiler_params=_cp(("arbitrary",), 48 << 20))(qr, *([kr] * (nk + 1)), *([krs] * (nk + 1)), *([v] * (nk + 1)), *([vs] * (nk + 1)),
                                                              sink, do_src, o, stats)


def na_index_tables():
    qc = np.arange(GRID_W)[:, None]
    kc = np.arange(GRID_W)[None, :]
    cstart = np.clip(qc - NA_KW // 2, 0, GRID_W - NA_KW)
    ok = (kc >= cstart) & (kc < cstart + NA_KW)
    dx = np.clip(kc - qc, -(NA_KW - 1), NA_KW - 1) + (NA_KW - 1)
    off = np.arange(NA_KH)[:, None]
    kr = np.arange(NA_KH)[None, :]
    dy = kr - off + (NA_KH - 1)
    return ok, dx, dy


def _na_selectors():
    ok, dx, dy = na_index_tables()
    e1 = np.zeros((GRID_W * GRID_W, 128), np.float32)
    qi, ki = np.nonzero(ok)
    e1[qi * GRID_W + ki, dx[qi, ki]] = 1.0
    e2 = np.zeros((16, NA_KH * NA_KH), np.float32)
    oi, ri = np.meshgrid(np.arange(NA_KH), np.arange(NA_KH), indexing="ij")
    e2[dy[oi, ri].ravel(), (oi * NA_KH + ri).ravel()] = 1.0
    return ok, jnp.asarray(e1), jnp.asarray(np.kron(np.eye(NA_HEADS, dtype=np.float32), e2))


def na_bias_table(rpb, tag):
    ok, e1, e2 = _na_selectors()
    r2 = jnp.pad(rpb.astype(F32), ((0, 0), (0, 1), (0, 128 - (2 * NA_KW - 1)))).reshape(NA_HEADS * 16, 128)
    r1 = matmul(e2, r2, "tn", F32, f"na_bias_sel1_{tag}", hi=True)
    x = matmul(r1, e1, "nt", F32, f"na_bias_sel2_{tag}", hi=True)
    b = x.reshape(NA_HEADS, NA_KH, NA_KH, GRID_W, GRID_W).transpose(0, 1, 3, 2, 4)
    b = b + jnp.asarray(np.where(ok, 0.0, NEG).astype(np.float32))[None, None, :, None, :]
    return b.reshape(NA_HEADS, NA_KH, GRID_W, NA_KH * GRID_W)


def _na_rows(r, GR):
    r0 = jnp.clip(r - NA_KH // 2, 0, GR - NA_KH)
    return r0, jnp.clip(r - r0, 0, NA_KH - 1)


NA_RPS = 4


def _pair_rows(x):
    lane = lax.broadcasted_iota(jnp.int32, x.shape, 1)
    zero = jnp.zeros_like(x)
    return jnp.concatenate([jnp.where(lane < HD, x, zero), jnp.where(lane >= HD, x, zero)], axis=0)


def _unpair_rows(x2):
    n = x2.shape[0] // 2
    lane = lax.broadcasted_iota(jnp.int32, (n, 128), 1)
    return jnp.where(lane < HD, x2[:n], x2[n:])


def na_fwd(P, kb, vb, bias, L, Lc, name):
    T = L + Lc
    GR = L // GRID_W
    W = NA_KH * GRID_W
    QB = GRID_W * NA_RPS
    nlat = GR // NA_RPS

    def body(q_ref, k_ref, v_ref, b_ref, o_ref, st_ref):
        s = pl.program_id(0)

        def put(qs, p, res):
            o2, m2, l2 = res
            n = o2.shape[0] // 2
            o_ref[qs, p * 128:(p + 1) * 128] = _unpair_rows(o2).astype(o_ref.dtype)
            for r in range(2):
                st_ref[qs, 2 * p + r:2 * p + r + 1] = m2[r * n:(r + 1) * n]
                st_ref[qs, NA_HEADS + 2 * p + r:NA_HEADS + 2 * p + r + 1] = l2[r * n:(r + 1) * n]

        @pl.when(s < nlat)
        def _():
            for rr in range(NA_RPS):
                r0, off = _na_rows(s * NA_RPS + rr, GR)
                rows = pl.ds(pl.multiple_of(r0 * GRID_W, GRID_W), W)
                qs = slice(rr * GRID_W, (rr + 1) * GRID_W)
                for p in range(NA_HEADS // 2):
                    ps = slice(p * 128, (p + 1) * 128)
                    b2 = jnp.concatenate([b_ref[2 * p, off], b_ref[2 * p + 1, off]], axis=0)
                    put(qs, p, _attn_tile(_pair_rows(q_ref[qs, ps]), [(k_ref[rows, ps], b2), (k_ref[L:T, ps], None)],
                                          [v_ref[rows, ps], v_ref[L:T, ps]], None))

        @pl.when(s >= nlat)
        def _():
            for p in range(NA_HEADS // 2):
                ps = slice(p * 128, (p + 1) * 128)
                put(slice(None), p, _attn_tile(_pair_rows(q_ref[:, ps]), [(k_ref[L:T, ps], None)], [v_ref[L:T, ps]], None))

    one = pl.Buffered(1)
    return _pc(body, name=name, grid=(T // QB,),
               in_specs=[pl.BlockSpec((QB, 256), lambda r: (r, C_QB // 256)),
                         pl.BlockSpec((T, 256), lambda r: (0, 0), pipeline_mode=one),
                         pl.BlockSpec((T, 256), lambda r: (0, 0), pipeline_mode=one),
                         pl.BlockSpec((NA_HEADS, NA_KH, GRID_W, W), lambda r: (0, 0, 0, 0), pipeline_mode=one)],
               out_specs=[pl.BlockSpec((QB, 256), lambda r: (r, 0)), pl.BlockSpec((QB, 8), lambda r: (r, 0))],
               out_shape=[_sds((T, 256), BF16), _sds((T, 8), F32)],
               compiler_params=_cp(("arbitrary",), 32 << 20))(P, kb, vb, bias)


def na_bwd(P, kb, vb, bias, do_src, o, stats, L, Lc, name):
    T = L + Lc
    GR = L // GRID_W
    W = NA_KH * GRID_W
    QB = GRID_W * NA_RPS
    nlat = GR // NA_RPS

    def body(q_ref, k_ref, v_ref, b_ref, do_ref, o_ref, st_ref, dq_ref, dk_ref, dv_ref, db_ref):
        s = pl.program_id(0)

        @pl.when(s == 0)
        def _():
            dk_ref[...] = jnp.zeros_like(dk_ref)
            dv_ref[...] = jnp.zeros_like(dv_ref)
            db_ref[...] = jnp.zeros_like(db_ref)

        def tile(qs, p, ks, vs):
            ps = slice(p * 128, (p + 1) * 128)
            m2 = jnp.concatenate([st_ref[qs, 2 * p:2 * p + 1], st_ref[qs, 2 * p + 1:2 * p + 2]], axis=0)
            l2 = jnp.concatenate([st_ref[qs, NA_HEADS + 2 * p:NA_HEADS + 2 * p + 1], st_ref[qs, NA_HEADS + 2 * p + 1:NA_HEADS + 2 * p + 2]], axis=0)
            dq2, dks, dvs, dss, _ = _attn_bwd_tile(_pair_rows(q_ref[qs, ps]), ks, vs, None, m2, l2,
                                                   _pair_rows(o_ref[qs, ps].astype(F32)), _pair_rows(do_ref[qs, ps].astype(F32)))
            dq_ref[qs, ps] = _unpair_rows(dq2).astype(dq_ref.dtype)
            return dks, dvs, dss

        @pl.when(s < nlat)
        def _():
            for rr in range(NA_RPS):
                r0, off = _na_rows(s * NA_RPS + rr, GR)
                rows = pl.ds(pl.multiple_of(r0 * GRID_W, GRID_W), W)
                qs = slice(rr * GRID_W, (rr + 1) * GRID_W)
                for p in range(NA_HEADS // 2):
                    ps = slice(p * 128, (p + 1) * 128)
                    b2 = jnp.concatenate([b_ref[2 * p, off], b_ref[2 * p + 1, off]], axis=0)
                    dks, dvs, dss = tile(qs, p, [(k_ref[rows, ps], b2), (k_ref[L:T, ps], None)], [v_ref[rows, ps], v_ref[L:T, ps]])
                    dk_ref[rows, ps] += dks[0]
                    dv_ref[rows, ps] += dvs[0]
                    dk_ref[L:T, ps] += dks[1]
                    dv_ref[L:T, ps] += dvs[1]
                    db_ref[2 * p, off] += dss[0][:GRID_W]
                    db_ref[2 * p + 1, off] += dss[0][GRID_W:]

        @pl.when(s >= nlat)
        def _():
            for p in range(NA_HEADS // 2):
                ps = slice(p * 128, (p + 1) * 128)
                dks, dvs, _ = tile(slice(None), p, [(k_ref[L:T, ps], None)], [v_ref[L:T, ps]])
                dk_ref[L:T, ps] += dks[0]
                dv_ref[L:T, ps] += dvs[0]

    one = pl.Buffered(1)
    full = lambda shape: pl.BlockSpec(shape, lambda r: (0,) * len(shape), pipeline_mode=one)
    qspec = pl.BlockSpec((QB, 256), lambda r: (r, 0))
    return _pc(body, name=name, grid=(T // QB,),
               in_specs=[pl.BlockSpec((QB, 256), lambda r: (r, C_QB // 256)), full((T, 256)), full((T, 256)),
                         full((NA_HEADS, NA_KH, GRID_W, W)), pl.BlockSpec((QB, 256), lambda r: (r, 1)), qspec, pl.BlockSpec((QB, 8), lambda r: (r, 0))],
               out_specs=[qspec, full((T, 256)), full((T, 256)), full((NA_HEADS, NA_KH, GRID_W, W))],
               out_shape=[_sds((T, 256), BF16), _sds((T, 256), F32), _sds((T, 256), F32), _sds((NA_HEADS, NA_KH, GRID_W, W), F32)],
               compiler_params=_cp(("arbitrary",), 48 << 20))(P, kb, vb, bias, do_src, o, stats)


def na_rpb_grad(dbias, tag):
    _, e1, e2 = _na_selectors()
    x = dbias.reshape(NA_HEADS, NA_KH, GRID_W, NA_KH, GRID_W).transpose(0, 1, 3, 2, 4).reshape(NA_HEADS * NA_KH * NA_KH, GRID_W * GRID_W)
    r1 = matmul(x, e1, "nn", F32, f"na_rpb_sel1_{tag}", hi=True, tk=1024)
    r2 = matmul(e2, r1, "nn", F32, f"na_rpb_sel2_{tag}", hi=True)
    return r2.reshape(NA_HEADS, 16, 128)[:, :2 * NA_KH - 1, :2 * NA_KW - 1]


_HALO = 8


def _halo_specs(T, col0):
    nh = TR // _HALO
    cur = pl.BlockSpec((TR, 256), lambda i, j: (i, col0 + j))
    prv = pl.BlockSpec((_HALO, 256), lambda i, j: (jnp.maximum(i * nh - 1, 0), col0 + j))
    nxt = pl.BlockSpec((_HALO, 256), lambda i, j: (jnp.minimum((i + 1) * nh, T // _HALO - 1), col0 + j))
    return prv, cur, nxt


def _fill_ext(ext, prv, cur, nxt, i, nL, nT):
    has_prev = jnp.where((i != 0) & (i != nL), 1.0, 0.0)
    has_next = jnp.where((i != nL - 1) & (i != nT - 1), 1.0, 0.0)
    ext[0:_HALO, :] = prv[...].astype(F32) * has_prev
    ext[_HALO:_HALO + TR, :] = cur[...].astype(F32)
    ext[_HALO + TR:, :] = nxt[...].astype(F32) * has_next


def conv_silu_fwd(P, w8, b, nL, name):
    T = P.shape[0]
    nT = T // TR

    def body(prv, cur, nxt, w_ref, b_ref, pre_ref, act_ref, ext):
        i = pl.program_id(0)
        _fill_ext(ext, prv, cur, nxt, i, nL, nT)
        y = jnp.broadcast_to(b_ref[...], (TR, 256))
        for k in range(S_CONV):
            y = y + w_ref[k:k + 1, :] * ext[pl.ds(_HALO - S_CONV // 2 + k, TR), :]
        pre_ref[...] = y
        act_ref[...] = _silu(y)

    prv, cur, nxt = _halo_specs(T, C_XBC // 256)
    out = pl.BlockSpec((TR, 256), lambda i, j: (i, j))
    return _pc(body, name=name, grid=(nT, 4),
               in_specs=[prv, cur, nxt, pl.BlockSpec((8, 256), lambda i, j: (0, j)), pl.BlockSpec((1, 256), lambda i, j: (0, j))],
               out_specs=[out, out], out_shape=[_sds((T, 1024), F32), _sds((T, 1024), F32)],
               scratch_shapes=[pltpu.VMEM((TR + 2 * _HALO, 256), F32)],
               compiler_params=_cp(("parallel", "parallel"), 16 << 20))(P, P, P, w8, b)


def dsilu(pre, dxs_list, db_list, dc_list, name):
    T = pre.shape[0]
    n1, n2, n3 = len(dxs_list), len(db_list), len(dc_list)

    def body(*refs):
        pre_ref = refs[0]
        ins = refs[1:1 + n1 + n2 + n3]
        out = refs[-1]

        def part(rs, lo, hi):
            g = rs[0][...].astype(F32)
            for r in rs[1:]:
                g = g + r[...].astype(F32)
            x = pre_ref[:, lo:hi]
            sg = 1.0 / (1.0 + jnp.exp(-x))
            sl = x * sg
            out[:, lo:hi] = g * (sg + sl * (1.0 - sg))

        part(ins[:n1], 0, 512)
        part(ins[n1:n1 + n2], 512, 768)
        part(ins[n1 + n2:], 768, 1024)

    spec = lambda w: pl.BlockSpec((TR, w), lambda i: (i, 0))
    return _pc(body, name=name, grid=(T // TR,),
               in_specs=[spec(1024)] + [spec(512)] * n1 + [spec(256)] * (n2 + n3),
               out_specs=spec(1024), out_shape=_sds((T, 1024), F32),
               compiler_params=_cp(("parallel",), 32 << 20))(pre, *dxs_list, *db_list, *dc_list)


def conv_bwd(dpre, P, w8, nL, name):
    T = P.shape[0]
    nT = T // TR

    def body(dp, dc, dn, xp, xc, xn, w_ref, dx_ref, dw_ref, db_ref, extd, extx):
        i = pl.program_id(1)
        _fill_ext(extd, dp, dc, dn, i, nL, nT)
        _fill_ext(extx, xp, xc, xn, i, nL, nT)

        @pl.when(i == 0)
        def _():
            dw_ref[...] = jnp.zeros_like(dw_ref)
            db_ref[...] = jnp.zeros_like(db_ref)

        d = dc[...]
        dx = jnp.zeros((TR, 256), F32)
        for k in range(S_CONV):
            dx = dx + w_ref[k:k + 1, :] * extd[pl.ds(_HALO + S_CONV // 2 - k, TR), :]
            dw_ref[k:k + 1, :] += jnp.sum(d * extx[pl.ds(_HALO - S_CONV // 2 + k, TR), :], axis=0, keepdims=True)
        dx_ref[...] = dx.astype(dx_ref.dtype)
        db_ref[0:1, :] += jnp.sum(d, axis=0, keepdims=True)

    def swap(spec):
        f = spec.index_map
        return pl.BlockSpec(spec.block_shape, lambda j, i: f(i, j))

    dprv, dcur, dnxt = [swap(s) for s in _halo_specs(T, 0)]
    xprv, xcur, xnxt = [swap(s) for s in _halo_specs(T, C_XBC // 256)]
    acc = pl.BlockSpec((8, 256), lambda j, i: (0, j))
    return _pc(body, name=name, grid=(4, nT),
               in_specs=[dprv, dcur, dnxt, xprv, xcur, xnxt, acc],
               out_specs=[pl.BlockSpec((TR, 256), lambda j, i: (i, j)), acc, acc],
               out_shape=[_sds((T, 1024), BF16), _sds((8, 1024), F32), _sds((8, 1024), F32)],
               scratch_shapes=[pltpu.VMEM((TR + 2 * _HALO, 256), F32), pltpu.VMEM((TR + 2 * _HALO, 256), F32)],
               compiler_params=_cp(("parallel", "arbitrary"), 16 << 20))(dpre, dpre, dpre, P, P, P, w8)


def _onehot_row(h, n):
    return (lax.broadcasted_iota(jnp.int32, (1, n), 1) == h).astype(F32)


def _onehot_col(h, n):
    return (lax.broadcasted_iota(jnp.int32, (n, 1), 0) == h).astype(F32)


def _ssd_chunk(xs, dtr, dtb, alog, bm, cm, hin, reverse):
    Qn = S_Q
    ii = lax.broadcasted_iota(jnp.int32, (Qn, Qn), 0)
    jj = lax.broadcasted_iota(jnp.int32, (Qn, Qn), 1)
    keep = (ii <= jj) if reverse else (ii >= jj)
    tri = keep.astype(F32)
    triT = ((jj <= ii) if reverse else (jj >= ii)).astype(F32)
    eye = (ii == jj).astype(F32)
    dt = _softplus(dtr + dtb)
    a = dt * (-jnp.exp(alog))
    cs = hdot(tri, a)
    csT = hdot(a, triT, "tn")
    dtT = hdot(dt, eye, "tn")
    last = _onehot_row(0 if reverse else Qn - 1, Qn)
    ys, houts = [], []
    for g in range(S_GROUPS):
        G = bdot(cm[g], bm[g], "nt")
        for r in range(S_HEADS // S_GROUPS):
            h = g * (S_HEADS // S_GROUPS) + r
            eh_r, eh_c = _onehot_row(h, S_HEADS), _onehot_col(h, S_HEADS)
            cs_c = jnp.sum(cs * eh_r, axis=1, keepdims=True)
            dt_c = jnp.sum(dt * eh_r, axis=1, keepdims=True)
            cs_r = jnp.sum(csT * eh_c, axis=0, keepdims=True)
            dt_r = jnp.sum(dtT * eh_c, axis=0, keepdims=True)
            tot = jnp.sum(cs_r * last, axis=1, keepdims=True)
            decay = jnp.exp(jnp.where(keep, cs_c - cs_r, NEG))
            w = G * decay * dt_r
            y = bdot(w, xs[h], "nn") + bdot(cm[g], hin[h], "nt") * jnp.exp(cs_c)
            xsc = xs[h] * (jnp.exp(tot - cs_c) * dt_c)
            hout = hin[h] * jnp.exp(tot) + bdot(xsc, bm[g], "tn")
            ys.append(y)
            houts.append(hout)
    return ys, houts


def _ssd_orders(L, Lc):
    nl, ncx = L // S_Q, Lc // S_Q
    fwd = lambda s: jnp.where(s < ncx, nl + s, s - ncx)
    bwd = lambda s: nl + ncx - 1 - s
    return nl + ncx, fwd, bwd


def _ssd_in_specs(fo, bo, step):
    def at(order, w, col):
        return pl.BlockSpec((S_Q, w), lambda u: (order(step(u)), col))
    specs = []
    for order in (fo, bo):
        specs += [at(order, 512, 0), at(order, 256, 2), at(order, 256, 3), at(order, 128, C_DT // 128)]
    return specs


def ssd_fwd(act, P, dtb, alog, L, Lc, name):
    T = L + Lc
    ns, fo, bo = _ssd_orders(L, Lc)

    def body(xf, bf, cf, df, xb, bb, cb, db, dtb_ref, al_ref, yf, yb, hsf, hsb, Hf, Hb):
        s = pl.program_id(0)

        @pl.when(s == 0)
        def _():
            Hf[...] = jnp.zeros_like(Hf)
            Hb[...] = jnp.zeros_like(Hb)

        for d, (x_r, b_r, c_r, dt_r, y_r, hs_r, H) in enumerate(((xf, bf, cf, df, yf, hsf, Hf), (xb, bb, cb, db, yb, hsb, Hb))):
            hin = [H[h] for h in range(S_HEADS)]
            hs_r[0] = H[...]
            ys, houts = _ssd_chunk(
                [x_r[:, h * S_P:(h + 1) * S_P] for h in range(S_HEADS)], dt_r[:, d * 8:(d + 1) * 8],
                dtb_ref[d:d + 1, 0:8], al_ref[d:d + 1, 0:8],
                [b_r[:, g * S_N:(g + 1) * S_N] for g in range(S_GROUPS)], [c_r[:, g * S_N:(g + 1) * S_N] for g in range(S_GROUPS)],
                hin, reverse=(d == 1))
            for h in range(S_HEADS):
                y_r[:, h * S_P:(h + 1) * S_P] = ys[h]
                H[h] = houts[h]

    ident = lambda u: u
    small = pl.BlockSpec((8, 128), lambda u: (0, 0))
    hspec = pl.BlockSpec((1, S_HEADS, S_P, S_N), lambda u: (u, 0, 0, 0))
    return _pc(body, name=name, grid=(ns,),
               in_specs=_ssd_in_specs(fo, bo, ident) + [small, small],
               out_specs=[pl.BlockSpec((S_Q, 512), lambda u: (fo(u), 0)), pl.BlockSpec((S_Q, 512), lambda u: (bo(u), 0)), hspec, hspec],
               out_shape=[_sds((T, 512), F32), _sds((T, 512), F32), _sds((ns, S_HEADS, S_P, S_N), F32), _sds((ns, S_HEADS, S_P, S_N), F32)],
               scratch_shapes=[pltpu.VMEM((S_HEADS, S_P, S_N), F32), pltpu.VMEM((S_HEADS, S_P, S_N), F32)],
               compiler_params=_cp(("arbitrary",), 32 << 20))(act, act, act, P, act, act, act, P, dtb, alog)


def ssd_bwd(act, P, dtb, alog, hsf, hsb, dy, L, Lc, name):
    T = L + Lc
    ns, fo, bo = _ssd_orders(L, Lc)
    step = lambda u: ns - 1 - u

    def body(xf, bf, cf, df, xb, bb, cb, db, dtb_ref, al_ref, hsf_r, hsb_r, dyf, dyb,
             dxf, dbf, dcf, ddf, dxb, dbb, dcb, ddb, ddtb, dal, dHf, dHb):
        u = pl.program_id(0)

        @pl.when(u == 0)
        def _():
            dHf[...] = jnp.zeros_like(dHf)
            dHb[...] = jnp.zeros_like(dHb)
            ddtb[...] = jnp.zeros_like(ddtb)
            dal[...] = jnp.zeros_like(dal)

        dirs = ((xf, bf, cf, df, hsf_r, dyf, dxf, dbf, dcf, ddf, dHf), (xb, bb, cb, db, hsb_r, dyb, dxb, dbb, dcb, ddb, dHb))
        for d, (x_r, b_r, c_r, dt_r, hs_r, dy_r, dx_o, db_o, dc_o, dd_o, dH) in enumerate(dirs):
            f = functools.partial(_ssd_chunk, reverse=(d == 1))
            _, vjp = jax.vjp(
                f, [x_r[:, h * S_P:(h + 1) * S_P] for h in range(S_HEADS)], dt_r[:, d * 8:(d + 1) * 8],
                dtb_ref[d:d + 1, 0:8], al_ref[d:d + 1, 0:8],
                [b_r[:, g * S_N:(g + 1) * S_N] for g in range(S_GROUPS)], [c_r[:, g * S_N:(g + 1) * S_N] for g in range(S_GROUPS)],
                [hs_r[0, h] for h in range(S_HEADS)])
            gx, gdt, gdtb, gal, gb, gc, gh = vjp(([dy_r[:, h * S_P:(h + 1) * S_P] for h in range(S_HEADS)],
                                                  [dH[h] for h in range(S_HEADS)]))
            for h in range(S_HEADS):
                dx_o[:, h * S_P:(h + 1) * S_P] = gx[h]
                dH[h] = gh[h]
            for g in range(S_GROUPS):
                db_o[:, g * S_N:(g + 1) * S_N] = gb[g]
                dc_o[:, g * S_N:(g + 1) * S_N] = gc[g]
            dd_o[...] = gdt
            ddtb[d:d + 1, 0:8] += gdtb
            dal[d:d + 1, 0:8] += gal

    small = pl.BlockSpec((8, 128), lambda u: (0, 0))
    hspec = pl.BlockSpec((1, S_HEADS, S_P, S_N), lambda u: (step(u), 0, 0, 0))
    at = lambda order, w: pl.BlockSpec((S_Q, w), lambda u: (order(step(u)), 0))
    outs = []
    for order in (fo, bo):
        outs += [at(order, 512), at(order, 256), at(order, 256), at(order, 8)]
    oshape = [_sds((T, 512), F32), _sds((T, 256), F32), _sds((T, 256), F32), _sds((T, 8), F32)]
    return _pc(body, name=name, grid=(ns,),
               in_specs=_ssd_in_specs(fo, bo, step) + [small, small, hspec, hspec, at(fo, 512), at(bo, 512)],
               out_specs=outs + [small, small], out_shape=oshape + oshape + [_sds((8, 128), F32), _sds((8, 128), F32)],
               scratch_shapes=[pltpu.VMEM((S_HEADS, S_P, S_N), F32), pltpu.VMEM((S_HEADS, S_P, S_N), F32)],
               compiler_params=_cp(("arbitrary",), 40 << 20))(act, act, act, P, act, act, act, P, dtb, alog, hsf, hsb, dy, dy)


def _ssm_out(yf, yb, xs, z, dskip, g):
    y = (yf + yb + dskip * xs) * _silu(z)
    return (y * lax.rsqrt(jnp.mean(y * y, axis=-1, keepdims=True) + EPS)) * g


def ssm_out_fwd(yf, yb, act, P, dskip, g, name):
    T = yf.shape[0]

    def body(yf_r, yb_r, xs_r, z_r, d_r, g_r, o_r):
        o_r[...] = _ssm_out(yf_r[...], yb_r[...], xs_r[...], z_r[...], d_r[...], g_r[...]).astype(o_r.dtype)

    row = pl.BlockSpec((TR, 512), lambda i: (i, 0))
    vec = pl.BlockSpec((1, 512), lambda i: (0, 0))
    return _pc(body, name=name, grid=(T // TR,),
               in_specs=[row, row, row, pl.BlockSpec((TR, 512), lambda i: (i, C_Z // 512)), vec, vec],
               out_specs=row, out_shape=_sds((T, 512), BF16),
               compiler_params=_cp(("parallel",), 16 << 20))(yf, yb, act, P, dskip, g)


def ssm_out_bwd(yf, yb, act, P, dskip, g, do_src, name):
    T = yf.shape[0]

    def body(yf_r, yb_r, xs_r, z_r, d_r, g_r, do_r, dy_r, dxs_r, dz_r, dv_r):
        @pl.when(pl.program_id(0) == 0)
        def _():
            dv_r[...] = jnp.zeros_like(dv_r)

        _, vjp = jax.vjp(_ssm_out, yf_r[...], yb_r[...], xs_r[...], z_r[...], d_r[...], g_r[...])
        dyf, _, dxs, dz, dd, dg = vjp(do_r[...].astype(F32))
        dy_r[...] = dyf
        dxs_r[...] = dxs
        dz_r[...] = dz.astype(dz_r.dtype)
        dv_r[0:1, :] += dd
        dv_r[1:2, :] += dg

    row = pl.BlockSpec((TR, 512), lambda i: (i, 0))
    vec = pl.BlockSpec((1, 512), lambda i: (0, 0))
    return _pc(body, name=name, grid=(T // TR,),
               in_specs=[row, row, row, pl.BlockSpec((TR, 512), lambda i: (i, C_Z // 512)), vec, vec,
                         pl.BlockSpec((TR, 512), lambda i: (i, 1))],
               out_specs=[row, row, row, pl.BlockSpec((8, 512), lambda i: (0, 0))],
               out_shape=[_sds((T, 512), F32), _sds((T, 512), F32), _sds((T, 512), BF16), _sds((8, 512), F32)],
               compiler_params=_cp(("arbitrary",), 24 << 20))(yf, yb, act, P, dskip, g, do_src)


def add_halves(xv, got, cvec, name):
    n, r, cdim = xv.shape
    h = r // 2

    def body(c_ref, x_ref, g_ref, o_ref):
        o_ref[...] = (x_ref[...].astype(F32) + g_ref[...].astype(F32)).astype(o_ref.dtype)

    gs = pltpu.PrefetchScalarGridSpec(
        num_scalar_prefetch=1, grid=(n,),
        in_specs=[pl.BlockSpec((1, h, cdim), lambda k, c_ref: (k, c_ref[0], 0)), pl.BlockSpec((1, h, cdim), lambda k, c_ref: (k, 0, 0))],
        out_specs=pl.BlockSpec((1, h, cdim), lambda k, c_ref: (k, 0, 0)))
    return _pc(body, name=name, grid_spec=gs, out_shape=_sds((n, h, cdim), BF16),
               compiler_params=_cp(("arbitrary",), 24 << 20))(cvec, xv, got)


def sum_slots(a, name):
    n, r, cdim = a.shape
    tr = _div_tile(r, 512, 16)

    def body(a_ref, o_ref):
        acc = a_ref[0].astype(F32)
        for k in range(1, n):
            acc = acc + a_ref[k].astype(F32)
        o_ref[...] = acc

    return _pc(body, name=name, grid=(r // tr,), in_specs=[pl.BlockSpec((n, tr, cdim), lambda i: (0, i, 0))],
               out_specs=pl.BlockSpec((tr, cdim), lambda i: (i, 0)), out_shape=_sds((r, cdim), F32),
               compiler_params=_cp(("parallel",), 32 << 20))(a)


def adamw(w, g, m, v, name):
    B, R, C = w.shape
    tr = _div_tile(R, max(8, (1 << 19) // max(C, 1) // 8 * 8), 8) if R % 8 == 0 else R
    c1 = 1.0 / (1.0 - ADAM_B1 ** ADAM_STEP)
    c2 = 1.0 / (1.0 - ADAM_B2 ** ADAM_STEP)

    def body(w_ref, g_ref, m_ref, v_ref, d_ref, mo_ref, vo_ref):
        gg = g_ref[...]
        mn = ADAM_B1 * m_ref[...] + (1.0 - ADAM_B1) * gg
        vn = ADAM_B2 * v_ref[...] + (1.0 - ADAM_B2) * (gg * gg)
        d_ref[...] = -ADAM_LR * ((mn * c1) / (jnp.sqrt(vn * c2) + ADAM_EPS) + ADAM_WD * w_ref[...])
        mo_ref[...] = mn
        vo_ref[...] = vn

    spec = pl.BlockSpec((1, tr, C), lambda b, i: (b, i, 0))
    return _pc(body, name=name, grid=(B, R // tr), in_specs=[spec] * 4, out_specs=[spec] * 3,
               out_shape=[_sds((B, R, C), F32)] * 3, compiler_params=_cp(("parallel", "parallel"), 32 << 20))(w, g, m, v)


def _me():
    return lax.axis_index("x"), lax.axis_index("y"), lax.axis_index("c")


def _flip(v, bit):
    return 1 - v if bit else v


def allgather8(xv, name):
    R = xv.shape[0]

    def body(x_ref, out_ref, sum_ref, send_sems, recv_sems):
        mx, my, mc = _me()
        me = 4 * mx + 2 * my + mc
        out_ref[me] = x_ref[...]
        sends, recvs = [], []
        for k in range(1, 8):
            px, py, pc = _flip(mx, k & 4), _flip(my, k & 2), _flip(mc, k & 1)
            peer = 4 * px + 2 * py + pc
            sends.append(pltpu.make_async_remote_copy(src_ref=x_ref, dst_ref=out_ref.at[me], send_sem=send_sems.at[k - 1],
                                                      recv_sem=recv_sems.at[k - 1], device_id=(px, py, pc), device_id_type=MESH))
            recvs.append(pltpu.make_async_remote_copy(src_ref=x_ref, dst_ref=out_ref.at[peer], send_sem=send_sems.at[k - 1],
                                                      recv_sem=recv_sems.at[k - 1], device_id=(px, py, pc), device_id_type=MESH))
        for cp in sends:
            cp.start()
        for cp in recvs:
            cp.wait_recv()
        for cp in sends:
            cp.wait_send()
        acc = out_ref[0]
        for d in range(1, 8):
            acc = acc + out_ref[d]
        sum_ref[...] = acc

    vm = pl.BlockSpec(memory_space=pltpu.VMEM)
    return _pc(body, name=name, pin=False, in_specs=[vm], out_specs=[vm, vm], out_shape=[_sds((8, R, 128), F32), _sds((R, 128), F32)],
               scratch_shapes=[pltpu.SemaphoreType.DMA((7,)), pltpu.SemaphoreType.DMA((7,))],
               compiler_params=_cp(None, 32 << 20))(xv)


def _other_chips(mx, my):
    return [(1 - mx, my), (mx, 1 - my), (1 - mx, 1 - my)]


def _halves(r, mc, mult):
    h = r // 2
    return pl.ds(pl.multiple_of(mc * h, mult), h), pl.ds(pl.multiple_of((1 - mc) * h, mult), h)


def _rcopy(src, dst, send_sems, recv_sems, k, to):
    return pltpu.make_async_remote_copy(src_ref=src, dst_ref=dst, send_sem=send_sems.at[k], recv_sem=recv_sems.at[k],
                                        device_id=to, device_id_type=MESH)


def _gather_body(xs, outs, send_sems, recv_sems):
    n = len(xs)
    mx, my, mc = _me()
    chip = 2 * mx + my
    sib = (mx, my, 1 - mc)
    chips = _other_chips(mx, my)
    idx = [2 * cx + cy for cx, cy in chips]
    cp = functools.partial(_rcopy, send_sems=send_sems, recv_sems=recv_sems)
    hv = [_halves(x.shape[0], mc, 16) for x in xs]
    first, passed = [], []
    for a in range(n):
        for j, (cx, cy) in enumerate(chips):
            first.append(cp(xs[a].at[hv[a][0]], outs[a].at[chip, hv[a][0]], k=6 * a + j, to=(cx, cy, mc)))
            first[-1].start()
    for a in range(n):
        for j in range(3):
            cp(xs[a].at[hv[a][0]], outs[a].at[idx[j], hv[a][0]], k=6 * a + j, to=sib).wait_recv()
            passed.append(cp(outs[a].at[idx[j], hv[a][0]], outs[a].at[idx[j], hv[a][0]], k=6 * a + 3 + j, to=sib))
            passed[-1].start()
    for a in range(n):
        for j in range(3):
            cp(xs[a].at[hv[a][1]], outs[a].at[idx[j], hv[a][1]], k=6 * a + 3 + j, to=sib).wait_recv()
    for c_ in first + passed:
        c_.wait_send()


def _my_chip():
    return 2 * lax.axis_index("x") + lax.axis_index("y")


def _own_slots(outs, shards):
    return [lax.dynamic_update_index_in_dim(o, x, _my_chip(), 0) for o, x in zip(outs, shards)]


def gather_weights(shards, name):
    n = len(shards)

    def body(*refs):
        _gather_body(refs[:n], refs[n:2 * n], *refs[2 * n:])

    hbm = pl.BlockSpec(memory_space=pl.ANY)
    outs = _pc(body, name=name, in_specs=[hbm] * n, out_specs=[hbm] * n, out_shape=[_sds((4,) + x.shape, x.dtype) for x in shards],
               scratch_shapes=[pltpu.SemaphoreType.DMA((6 * n,)), pltpu.SemaphoreType.DMA((6 * n,))])(*shards)
    return _own_slots(outs, shards)


GATHER_REST_ID = 3


def gather_weights_sc(shards, name):
    n = len(shards)
    x_refs = [jax.new_ref(x, memory_space=pltpu.MemorySpace.HBM) for x in shards]
    out_refs = [jax.empty_ref(_sds((4,) + x.shape, x.dtype), memory_space=pltpu.MemorySpace.HBM) for x in shards]

    @pl.kernel(mesh=plsc.ScalarSubcoreMesh(axis_name="sc", num_cores=1), name=name,
               scratch_types=(pltpu.SemaphoreType.DMA((6 * n,)), pltpu.SemaphoreType.DMA((6 * n,))),
               compiler_params=pltpu.CompilerParams(collective_id=GATHER_REST_ID))
    def launch(send_sems, recv_sems):
        mx, my, mc = _me()
        barrier = pltpu.get_barrier_semaphore()
        for peer in [(mx, my, 1 - mc)] + [(cx, cy, mc) for cx, cy in _other_chips(mx, my)]:
            pl.semaphore_signal(barrier, inc=1, device_id=peer, device_id_type=MESH)
        pl.semaphore_wait(barrier, 4)
        _gather_body(x_refs, out_refs, send_sems, recv_sems)

    launch()
    return _own_slots([o[...] for o in out_refs], shards)


def swap_halves(arrs, name):
    n = len(arrs)

    def body(*refs):
        xs, outs = refs[:n], refs[n:2 * n]
        send_sems, recv_sems = refs[2 * n:]
        mx, my, mc = _me()
        cps = []
        for a in range(n):
            theirs = _halves(xs[a].shape[1], mc, 16)[1]
            cps.append(_rcopy(xs[a].at[pl.ds(0, 4), theirs], outs[a], send_sems, recv_sems, a, (mx, my, 1 - mc)))
            cps[-1].start()
        for c_ in cps:
            c_.wait()

    hbm = pl.BlockSpec(memory_space=pl.ANY)
    return _pc(body, name=name, in_specs=[hbm] * n, out_specs=[hbm] * n,
               out_shape=[_sds((4, x.shape[1] // 2, x.shape[2]), x.dtype) for x in arrs],
               scratch_shapes=[pltpu.SemaphoreType.DMA((n,)), pltpu.SemaphoreType.DMA((n,))])(*arrs)


SCATTER_ID = 4


def scatter_chips_sc(arrs, name):
    n = len(arrs)
    x_refs = [jax.new_ref(x, memory_space=pltpu.MemorySpace.HBM) for x in arrs]
    out_refs = [jax.empty_ref(_sds(x.shape, x.dtype), memory_space=pltpu.MemorySpace.HBM) for x in arrs]

    @pl.kernel(mesh=plsc.ScalarSubcoreMesh(axis_name="sc", num_cores=1), name=name,
               scratch_types=(pltpu.SemaphoreType.DMA((3 * n,)), pltpu.SemaphoreType.DMA((3 * n,))),
               compiler_params=pltpu.CompilerParams(collective_id=SCATTER_ID))
    def launch(send_sems, recv_sems):
        mx, my, mc = _me()
        chip = 2 * mx + my
        chips = _other_chips(mx, my)
        idx = [2 * cx + cy for cx, cy in chips]
        barrier = pltpu.get_barrier_semaphore()
        for cx, cy in chips:
            pl.semaphore_signal(barrier, inc=1, device_id=(cx, cy, mc), device_id_type=MESH)
        pl.semaphore_wait(barrier, 3)
        cp = functools.partial(_rcopy, send_sems=send_sems, recv_sems=recv_sems)
        sends = []
        for a in range(n):
            for j, (cx, cy) in enumerate(chips):
                sends.append(cp(x_refs[a].at[idx[j]], out_refs[a].at[chip], k=3 * a + j, to=(cx, cy, mc)))
                sends[-1].start()
        for a in range(n):
            for j, (cx, cy) in enumerate(chips):
                cp(x_refs[a].at[idx[j]], out_refs[a].at[idx[j]], k=3 * a + j, to=(cx, cy, mc)).wait_recv()
        for c_ in sends:
            c_.wait_send()

    launch()
    return _own_slots([o[...] for o in out_refs], [lax.dynamic_index_in_dim(x, _my_chip(), 0, keepdims=False) for x in arrs])


def share_halves(parts, name):
    flat = [p for w in parts for p in w]
    nw, n = len(parts), len(flat)
    depth = n // nw

    def body(*refs):
        xs, outs = refs[:n], refs[n:n + nw]
        send_sems, recv_sems = refs[n + nw:]
        mx, my, mc = _me()
        sib = (mx, my, 1 - mc)
        sends, recvs = [], []
        for a in range(n):
            w, l = a // depth, a % depth
            mine, theirs = _halves(outs[w].shape[1], mc, 8)
            sends.append(_rcopy(xs[a], outs[w].at[l, mine], send_sems, recv_sems, a, sib))
            recvs.append(_rcopy(xs[a], outs[w].at[l, theirs], send_sems, recv_sems, a, sib))
            sends[-1].start()
        for c_ in recvs:
            c_.wait_recv()
        for c_ in sends:
            c_.wait_send()

    hbm = pl.BlockSpec(memory_space=pl.ANY)
    outs = _pc(body, name=name, in_specs=[hbm] * n, out_specs=[hbm] * nw,
               out_shape=[_sds((depth, 2 * w[0].shape[0], w[0].shape[1]), F32) for w in parts],
               scratch_shapes=[pltpu.SemaphoreType.DMA((n,)), pltpu.SemaphoreType.DMA((n,))])(*flat)
    outs = list(outs)
    mc = lax.axis_index("c")
    for w in range(nw):
        for l in range(depth):
            h = parts[w][l].shape[0]
            outs[w] = lax.dynamic_update_slice(outs[w], parts[w][l][None], (l, mc * h, 0))
    return outs


_BIG = ("w_in", "w_out", "w_ffn_in", "w_ffn_out")
N_CHIPS = 4
DEPTH = 2


def _pad_rows(v, mult=8):
    n = v.shape[0]
    rows = -(-n // 128)
    rows = -(-rows // mult) * mult
    return jnp.pad(v, (0, rows * 128 - n)).reshape(rows, 128)


class _Flat:
    def __init__(self):
        self.items = []

    def add(self, name, a):
        self.items.append((name, a.shape, a.reshape(-1).astype(F32)))

    def rows(self):
        return _pad_rows(jnp.concatenate([a for _, _, a in self.items]))

    def split(self, rows):
        flat = rows.reshape(-1)
        out, o = {}, 0
        for name, shape, a in self.items:
            out[name] = flat[o:o + a.shape[0]].reshape(shape)
            o += a.shape[0]
        return out

    def split_lead(self, rows3):
        n = rows3.shape[0]
        flat = rows3.reshape(n, -1)
        out, o = {}, 0
        for name, shape, a in self.items:
            out[name] = flat[:, o:o + a.shape[0]].reshape((n,) + tuple(shape))
            o += a.shape[0]
        return out


def _gsv(rows):
    z = jnp.zeros((2, D), F32)
    r = [z if a is None else a for a in rows] + [z] * 5
    return jnp.stack(r, axis=1)


def _pad8(a, rows=8, cols=128):
    return jnp.zeros((rows, cols), F32).at[:a.shape[0], :a.shape[1]].set(a.astype(F32))


def kernel(x, c, ctx, c_ctx, w_mod, b_mod, g_mix, w_in, wa_sink, na_rpb, ssm_conv_w, ssm_conv_b, ssm_dt_bias, ssm_a_log, ssm_d, ssm_norm_g, w_out, g_ffn, w_ffn_in, w_ffn_out, g_final, loss_target, m_c_ctx, m_w_mod, m_b_mod, m_g_mix, m_w_in, m_wa_sink, m_na_rpb, m_ssm_conv_w, m_ssm_conv_b, m_ssm_dt_bias, m_ssm_a_log, m_ssm_d, m_ssm_norm_g, m_w_out, m_g_ffn, m_w_ffn_in, m_w_ffn_out, m_g_final, v_c_ctx, v_w_mod, v_b_mod, v_g_mix, v_w_in, v_wa_sink, v_na_rpb, v_ssm_conv_w, v_ssm_conv_b, v_ssm_dt_bias, v_ssm_a_log, v_ssm_d, v_ssm_norm_g, v_w_out, v_g_ffn, v_w_ffn_in, v_w_ffn_out, v_g_final):
    L, Lc = x.shape[1], ctx.shape[1]
    T = L + Lc
    nL = L // TR
    mx, my, mc = lax.axis_index("x"), lax.axis_index("y"), lax.axis_index("c")
    dev = 4 * mx + 2 * my + mc
    chip = 2 * mx + my
    MODW = 6 * D // N_CHIPS
    CW = 1024 // N_CHIPS

    sc = _silu(c.astype(F32))
    scc = _silu(c_ctx.astype(F32))[None]
    f1 = _Flat()
    f1.add("sc", sc)
    f1.add("conv_w", ssm_conv_w)
    g1, _ = allgather8(f1.rows(), "gather_cond")
    g1 = f1.split_lead(g1)
    sc_all = g1["sc"][:, 0]
    conv_w = jnp.concatenate([g1["conv_w"][2 * k] for k in range(N_CHIPS)], axis=-1)
    A16 = jnp.concatenate([sc_all, scc, jnp.zeros((7, D), F32)], axis=0)

    mod_part = matmul_layers(A16, w_mod, "nn", "mod_fwd")
    f2 = _Flat()
    f2.add("mod", mod_part)
    g2, _ = allgather8(f2.rows(), "gather_mod")
    g2 = f2.split_lead(g2)["mod"]
    mods = jnp.concatenate([g2[2 * k] for k in range(N_CHIPS)], axis=-1) + b_mod[:, None, :]
    mod_l = lax.dynamic_index_in_dim(mods, dev, axis=1, keepdims=False).reshape(DEPTH, 6, D)
    mod_c = mods[:, 8].reshape(DEPTH, 6, D)
    mod = jnp.stack([mod_l, mod_c], axis=1)
    mrow = lambda l, j: mod[l, :, j]

    own = {"w_in": w_in, "w_out": w_out, "w_ffn_in": w_ffn_in, "w_ffn_out": w_ffn_out}
    sh16 = [own[n][l].astype(BF16) for n in _BIG for l in range(DEPTH)]
    after_mod = (g2[0, 0, 0, 0] * 0).astype(BF16)
    gath = list(gather_weights([sh16[0] + after_mod], "gather_first"))
    after_first = (gath[0][0, 0, 0] * 0).astype(BF16)
    gath += list(gather_weights_sc([sh16[1] + after_first] + sh16[2:], "gather_rest"))
    gw = {n: [gath[DEPTH * i + l] for l in range(DEPTH)] for i, n in enumerate(_BIG)}
    W_in = [jnp.pad(jnp.concatenate([g[k] for k in range(N_CHIPS)], axis=1), ((0, 0), (0, IN_PAD - IN_COLS))) for g in gw["w_in"]]
    W_out = [g.reshape(D, D) for g in gw["w_out"]]
    W_fo = [g.reshape(D_FF, D) for g in gw["w_ffn_out"]]
    W_fi = gw["w_ffn_in"]

    cos, sin, rotm = rope_tables(L, Lc)
    x0 = jnp.concatenate([x[0], ctx[0]], axis=0).astype(F32)

    sv = []
    xin = x0
    gsv_first = _gsv([None, mrow(0, 0), mrow(0, 1)])
    _, h1 = res_norm_mod(x0, None, gsv_first, g_mix[0][None], nL, "norm_first")
    for l in range(DEPTH):
        s = {"xin": xin, "h1": h1}
        P = matmul(h1, W_in[l], "nn", F32, f"in_proj{l}", tn=IN_PAD)
        qr, kr, kb, vb = rope_apply(P, C_QA // 256, P, C_KA // 128, cos, sin, rotm, False, f"rope{l}", kv_src=P)
        sink8 = _pad8(jnp.broadcast_to(wa_sink[l][:, None], (WA_HEADS, 128)))
        krs, va = _swap_halves_lanes(kr), P[:, C_VA:C_VA + 128]
        vas = _swap_halves_lanes(va)
        oa, sta = win_attn_fwd(qr, kr, krs, va, vas, sink8, L, Lc, f"wa_fwd{l}")
        bias = na_bias_table(na_rpb[l], l)
        ob, stb = na_fwd(P, kb, vb, bias, L, Lc, f"na_fwd{l}")
        w8 = jnp.concatenate([conv_w[l], jnp.zeros((1, 1024), F32)], axis=0)
        pre, act = conv_silu_fwd(P, w8, ssm_conv_b[l][None], nL, f"conv_fwd{l}")
        dtb8, al8 = _pad8(ssm_dt_bias[l]), _pad8(ssm_a_log[l])
        yf, yb, hsf, hsb = ssd_fwd(act, P, dtb8, al8, L, Lc, f"ssd_fwd{l}")
        dskip = jnp.repeat(ssm_d[l], S_P)[None]
        oc = ssm_out_fwd(yf, yb, act, P, dskip, ssm_norm_g[l][None], f"ssm_out_fwd{l}")
        mixin = [(oa, 0), (ob, 256), (oc, 512)]
        mix = out_proj_fwd(mixin, W_out[l], f"out_proj{l}")
        gsv_mid = _gsv([mrow(l, 2), mrow(l, 3), mrow(l, 4)])
        x1, h2 = res_norm_mod(xin, mix, gsv_mid, g_ffn[l][None], nL, f"norm_mid{l}")
        gu = matmul_fi(h2, W_fi[l], "nn", BF16, f"ffn_in{l}")
        af = swiglu_fwd(gu, f"swiglu_fwd{l}")
        fo = matmul(af, W_fo[l], "nn", BF16, f"ffn_out{l}", tk=D_FF)
        s.update(P=P, qr=qr, kr=kr, krs=krs, va=va, vas=vas, sink8=sink8, oa=oa, sta=sta, ob=ob, stb=stb, kb=kb, vb=vb, bias=bias, w8=w8, pre=pre, act=act, dtb8=dtb8, al8=al8, yf=yf,
                 yb=yb, hsf=hsf, hsb=hsb, dskip=dskip, mixin=mixin, mix=mix, gsv_mid=gsv_mid, x1=x1, h2=h2, gu=gu, af=af, fo=fo)
        if l + 1 < DEPTH:
            s["gsv_end"] = _gsv([mrow(l, 5), mrow(l + 1, 0), mrow(l + 1, 1)])
            xin, h1 = res_norm_mod(x1, fo, s["gsv_end"], g_mix[l + 1][None], nL, f"norm_end{l}")
        else:
            s["gsv_end"] = _gsv([mrow(l, 5), None, None])
        sv.append(s)

    last = sv[-1]
    loss8, dres, dfo, dgsv_end, dg_final = final_loss(last["x1"], last["fo"], last["gsv_end"], g_final[None], loss_target[0].astype(F32), nL, "final_loss")
    loss = lax.psum(loss8[0, 0], ("x", "y", "c"))

    dmod = [[None] * 6 for _ in range(DEPTH)]
    gW = {n: [None] * DEPTH for n in _BIG}
    small = [dict() for _ in range(DEPTH)]
    parts = [None] * DEPTH
    cvec = mc.astype(jnp.int32).reshape(1)
    grad_x = None
    for l in reversed(range(DEPTH)):
        s = sv[l]
        dmod[l][5] = dgsv_end[:, 0]
        if l + 1 < DEPTH:
            dmod[l + 1][0], dmod[l + 1][1] = dgsv_end[:, 1], dgsv_end[:, 2]
        daf = matmul(dfo, W_fo[l], "nt", BF16, f"ffn_out_dx{l}")
        gW["w_ffn_out"][l] = matmul(s["af"], dfo, "tn", BF16, f"ffn_out_dw{l}", tm=1408, tk=T).reshape(N_CHIPS, D_FF // N_CHIPS, D)
        dgu = swiglu_bwd(s["gu"], daf, f"swiglu_bwd{l}")
        dh2 = matmul_fi(dgu, W_fi[l], "nt", BF16, f"ffn_in_dx{l}")
        gW["w_ffn_in"][l] = matmul_fi(s["h2"], dgu, "tn", BF16, f"ffn_in_dw{l}")
        dres, dmix, dgsv_mid, dg_ffn = res_norm_mod_bwd(s["x1"], s["mix"], s["gsv_mid"], g_ffn[l][None], dh2, dres, nL, f"norm_mid_bwd{l}")
        dmod[l][2], dmod[l][3], dmod[l][4] = dgsv_mid[:, 0], dgsv_mid[:, 1], dgsv_mid[:, 2]
        dmixin = matmul(dmix, W_out[l], "nt", BF16, f"out_proj_dx{l}")
        gW["w_out"][l] = out_proj_dw(s["mixin"], dmix, f"out_proj_dw{l}").reshape(N_CHIPS, D // N_CHIPS, D)
        P = s["P"]
        dqr, dkr, dkrs, dva, dvas, dsink = win_attn_bwd(s["qr"], s["kr"], s["krs"], s["va"], s["vas"], s["sink8"], dmixin, s["oa"], s["sta"], L, Lc,
                                                        f"wa_bwd{l}")
        dkr, dva = dkr + _swap_halves_lanes(dkrs), dva + _swap_halves_lanes(dvas)
        dqa, dka = rope_apply(dqr, 0, dkr[WA_BLK:WA_BLK + T], 0, cos, sin, rotm, True, f"rope_bwd{l}")
        dqb, dkb, dvb, dbias = na_bwd(P, s["kb"], s["vb"], s["bias"], dmixin, s["ob"], s["stb"], L, Lc, f"na_bwd{l}")
        dy, dxs1, dz, dvec = ssm_out_bwd(s["yf"], s["yb"], s["act"], P, s["dskip"], ssm_norm_g[l][None], dmixin, f"ssm_out_bwd{l}")
        dxf, dbf, dcf, ddf, dxb, dbb, dcb, ddb, ddtb, dal = ssd_bwd(s["act"], P, s["dtb8"], s["al8"], s["hsf"], s["hsb"], dy, L, Lc, f"ssd_bwd{l}")
        dpre = dsilu(s["pre"], [dxf, dxb, dxs1], [dbf, dbb], [dcf, dcb], f"dsilu{l}")
        dxbc, dw8, db8 = conv_bwd(dpre, P, s["w8"], nL, f"conv_bwd{l}")
        ddt = jnp.concatenate([ddf, ddb, jnp.zeros((T, IN_PAD - IN_COLS), F32)], axis=1)
        pieces = [(dqa, C_QA), (dqb, C_QB), (dz, C_Z), (dka, C_KA), (dva[WA_BLK:WA_BLK + T], C_VA), (dkb, C_KB), (dvb, C_VB),
                  (dxbc, C_XBC), (ddt, C_DT)]
        dh1, dwin = in_proj_bwd(pieces, s["h1"], W_in[l], f"in_proj_bwd{l}")
        cw = IN_COLS // N_CHIPS
        gW["w_in"][l] = jnp.stack([dwin[:, k * cw:(k + 1) * cw] for k in range(N_CHIPS)])
        garr = [gW[n][l] for n in _BIG]
        got = swap_halves(garr, f"reduce_d2d{l}")
        chip_sum = [add_halves(garr[a], got[a], cvec, f"reduce_add_pair{l}_{a}") for a in range(len(garr))]
        parts[l] = scatter_chips_sc(chip_sum, f"reduce_ici{l}")
        small[l] = dict(g_ffn=dg_ffn[0], wa_sink=dsink[:WA_HEADS, 0], na_rpb=na_rpb_grad(dbias, l), conv_w=dw8[:S_CONV], conv_b=db8[0],
                        dt_bias=ddtb[:2, :8], a_log=dal[:2, :8], ssm_d=dvec[0].reshape(S_HEADS, S_P).sum(axis=1), norm_g=dvec[1])
        if l > 0:
            p = sv[l - 1]
            dres, dfo, dgsv_end, dg_mix = res_norm_mod_bwd(s["xin"], p["fo"], p["gsv_end"], g_mix[l][None], dh1, dres, nL, f"norm_end_bwd{l - 1}")
        else:
            grad_x, _, dgsv_first, dg_mix = res_norm_mod_bwd(s["xin"], None, gsv_first, g_mix[0][None], dh1, dres, nL, "norm_first_bwd")
            dmod[0][0], dmod[0][1] = dgsv_first[:, 1], dgsv_first[:, 2]
        small[l]["g_mix"] = dg_mix[0]
    for l in range(DEPTH):
        for j in range(6):
            if dmod[l][j] is None:
                dmod[l][j] = jnp.zeros((2, D), F32)
    dmod = jnp.stack([jnp.stack(r, axis=1) for r in dmod])

    f3 = _Flat()
    f3.add("dmod_l", dmod[:, 0].reshape(DEPTH, 6 * D))
    f3.add("dmod_c", dmod[:, 1].reshape(DEPTH, 6 * D))
    f3.add("g_final", dg_final[0])
    for n in ("g_mix", "g_ffn", "wa_sink", "na_rpb", "conv_w", "conv_b", "dt_bias", "a_log", "ssm_d", "norm_g"):
        f3.add(n, jnp.stack([small[l][n] for l in range(DEPTH)]))
    g3, s3 = allgather8(f3.rows(), "reduce_small")
    dmod_all = f3.split_lead(g3)["dmod_l"]
    s3 = f3.split(s3)
    dmodc_tot = s3["dmod_c"]
    col0 = chip * MODW
    G16, G16c = [], []
    for l in range(DEPTH):
        rows = jnp.concatenate([dmod_all[:, l], dmodc_tot[l][None], jnp.zeros((7, 6 * D), F32)], axis=0)
        G16.append(lax.dynamic_slice_in_dim(rows, col0, MODW, axis=1))
        rc = jnp.concatenate([dmodc_tot[l][None], jnp.zeros((15, 6 * D), F32)], axis=0)
        G16c.append(lax.dynamic_slice_in_dim(rc, col0, MODW, axis=1))
    grad_w_mod = matmul_layers(A16, jnp.stack(G16), "tn", "mod_dw")
    dscc_part = matmul_layers(jnp.stack(G16c), w_mod, "nt", "mod_dx")[:, 0].sum(axis=0)
    _, s4 = allgather8(_pad_rows(dscc_part * (mc == 1).astype(F32)), "reduce_cctx")
    dscc = s4.reshape(-1)[:D]
    cc = c_ctx.astype(F32)
    sg = 1.0 / (1.0 + jnp.exp(-cc))
    grad_c_ctx = dscc * (sg * (1.0 + cc * (1.0 - sg)))

    halves = [[sum_slots(parts[l][i], f"reduce_add_chips{l}_{i}") for l in range(DEPTH)] for i in range(len(_BIG))]
    gsh = dict(zip(_BIG, share_halves(halves, "reduce_share")))

    grads = {"c_ctx": grad_c_ctx, "w_mod": grad_w_mod, "b_mod": s3["dmod_l"] + s3["dmod_c"], "g_mix": s3["g_mix"], "w_in": gsh["w_in"],
             "wa_sink": s3["wa_sink"], "na_rpb": s3["na_rpb"],
             "ssm_conv_w": lax.dynamic_slice_in_dim(s3["conv_w"], chip * CW, CW, axis=2), "ssm_conv_b": s3["conv_b"],
             "ssm_dt_bias": s3["dt_bias"], "ssm_a_log": s3["a_log"], "ssm_d": s3["ssm_d"], "ssm_norm_g": s3["norm_g"],
             "w_out": gsh["w_out"], "g_ffn": s3["g_ffn"], "w_ffn_in": gsh["w_ffn_in"], "w_ffn_out": gsh["w_ffn_out"], "g_final": s3["g_final"]}
    wts = {"c_ctx": c_ctx, "w_mod": w_mod, "b_mod": b_mod, "g_mix": g_mix, "w_in": w_in, "wa_sink": wa_sink, "na_rpb": na_rpb,
           "ssm_conv_w": ssm_conv_w, "ssm_conv_b": ssm_conv_b, "ssm_dt_bias": ssm_dt_bias, "ssm_a_log": ssm_a_log, "ssm_d": ssm_d,
           "ssm_norm_g": ssm_norm_g, "w_out": w_out, "g_ffn": g_ffn, "w_ffn_in": w_ffn_in, "w_ffn_out": w_ffn_out, "g_final": g_final}
    ms = {"c_ctx": m_c_ctx, "w_mod": m_w_mod, "b_mod": m_b_mod, "g_mix": m_g_mix, "w_in": m_w_in, "wa_sink": m_wa_sink, "na_rpb": m_na_rpb,
          "ssm_conv_w": m_ssm_conv_w, "ssm_conv_b": m_ssm_conv_b, "ssm_dt_bias": m_ssm_dt_bias, "ssm_a_log": m_ssm_a_log, "ssm_d": m_ssm_d,
          "ssm_norm_g": m_ssm_norm_g, "w_out": m_w_out, "g_ffn": m_g_ffn, "w_ffn_in": m_w_ffn_in, "w_ffn_out": m_w_ffn_out, "g_final": m_g_final}
    vs = {"c_ctx": v_c_ctx, "w_mod": v_w_mod, "b_mod": v_b_mod, "g_mix": v_g_mix, "w_in": v_w_in, "wa_sink": v_wa_sink, "na_rpb": v_na_rpb,
          "ssm_conv_w": v_ssm_conv_w, "ssm_conv_b": v_ssm_conv_b, "ssm_dt_bias": v_ssm_dt_bias, "ssm_a_log": v_ssm_a_log, "ssm_d": v_ssm_d,
          "ssm_norm_g": v_ssm_norm_g, "w_out": v_w_out, "g_ffn": v_g_ffn, "w_ffn_in": v_w_ffn_in, "w_ffn_out": v_w_ffn_out, "g_final": v_g_final}
    names = list(wts)
    grads = {n: grads[n].reshape(wts[n].shape).astype(F32) for n in names}
    big = ("w_mod", "w_in", "w_out", "w_ffn_in", "w_ffn_out")
    delta, new_m, new_v = {}, {}, {}
    for n in big:
        delta[n], new_m[n], new_v[n] = adamw(wts[n], grads[n], ms[n], vs[n], f"adamw_{n}")
    packs = []
    for src in (wts, grads, ms, vs):
        f = _Flat()
        for n in names:
            if n not in big:
                f.add(n, src[n])
        packs.append(f)
    d_, m_, v_ = adamw(*[f.rows()[None] for f in packs], "adamw_small")
    for dst, rows in ((delta, d_), (new_m, m_), (new_v, v_)):
        dst.update(packs[0].split(rows[0]))

    return (loss, grad_x[:L][None], *[grads[n] for n in names], *[delta[n] for n in names],
            *[new_m[n] for n in names], *[new_v[n] for n in names])
```

```python
import functools

import numpy as np
import jax
import jax.numpy as jnp
from jax import lax
from jax.experimental import pallas as pl
from jax.experimental.pallas import tpu as pltpu
from jax.experimental.pallas import tpu_sc as plsc

F32 = jnp.float32
BF16 = jnp.bfloat16
_MXU = jnp.bfloat16
_HI = lax.Precision.HIGHEST
MESH = pl.DeviceIdType.MESH

D = 1024
HD = 64
GRID_W = 64
EPS = 1e-6
ROPE_BASE = 10000.0
WA_HEADS, WA_KV = 4, 2
WA_BLK = 128
NA_HEADS, NA_KH, NA_KW = 4, 8, 16
S_HEADS, S_P, S_INNER, S_GROUPS, S_N, S_CONV, S_Q = 8, 64, 512, 2, 128, 7, 128
D_FF = 2816
IN_COLS = 2832
IN_PAD = 2944
C_QA, C_QB, C_Z, C_KA, C_VA, C_KB, C_VB, C_XBC, C_DT = 0, 256, 512, 1024, 1152, 1280, 1536, 1792, 2816
ADAM_LR, ADAM_B1, ADAM_B2, ADAM_EPS, ADAM_WD, ADAM_STEP = 0.001, 0.9, 0.999, 1e-08, 0.01, 10

TR = 256
NEG = -1e30
VMEM_CAP = 56 * 1024 * 1024


PIN_BYTES = 256 * 1024


def _is_big(a):
    return hasattr(a, "shape") and len(a.shape) >= 2 and int(np.prod(a.shape)) * jnp.dtype(a.dtype).itemsize >= PIN_BYTES


def _pc(body, *, out_shape, pin=True, **kw):
    if not pin:
        return pl.pallas_call(body, out_shape=out_shape, **kw)
    one = isinstance(out_shape, jax.ShapeDtypeStruct)
    outs = [pltpu.HBM(s.shape, s.dtype) if _is_big(s) else s for s in ([out_shape] if one else out_shape)]
    call = pl.pallas_call(body, out_shape=outs[0] if one else outs, **kw)
    return lambda *args: call(*[pltpu.with_memory_space_constraint(a, pltpu.HBM) if _is_big(a) else a for a in args])


def _cp(sem=None, vmem=None):
    kw = {}
    if sem is not None:
        kw["dimension_semantics"] = sem
    if vmem is not None:
        kw["vmem_limit_bytes"] = int(min(max(vmem, 16 * 1024 * 1024), VMEM_CAP))
    return pltpu.CompilerParams(**kw)


def _sds(shape, dtype):
    return jax.ShapeDtypeStruct(tuple(shape), dtype)


_DIMS = {"nn": ((1,), (0,)), "nt": ((1,), (1,)), "tn": ((0,), (0,))}


def _dg(a, b, dims):
    return lax.dot_general(a.astype(_MXU), b.astype(_MXU), (dims, ((), ())), preferred_element_type=F32)


@functools.partial(jax.custom_vjp, nondiff_argnums=(2,))
def bdot(a, b, mode):
    return _dg(a, b, _DIMS[mode])


def _bdot_fwd(a, b, mode):
    return bdot(a, b, mode), (a, b)


def _bdot_bwd(mode, res, g):
    a, b = res
    if mode == "nn":
        return bdot(g, b, "nt"), bdot(a, g, "tn")
    if mode == "nt":
        return bdot(g, b, "nn"), bdot(g, a, "tn")
    return bdot(b, g, "nt"), bdot(a, g, "nn")


bdot.defvjp(_bdot_fwd, _bdot_bwd)


def hdot(a, b, mode="nn"):
    return lax.dot_general(a, b, (_DIMS[mode], ((), ())), precision=_HI, preferred_element_type=F32)


def _silu(x):
    return x / (1.0 + jnp.exp(-x))


def _softplus(x):
    return jnp.maximum(x, 0.0) + jnp.log(1.0 + jnp.exp(-jnp.abs(x)))


def _div_tile(n, cap, mult):
    if n <= cap:
        return n
    best = None
    for t in range(mult, cap + 1, mult):
        if n % t == 0:
            best = t
    assert best is not None, (n, cap, mult)
    return best


def matmul(a, b, mode, out_dtype, name, tm=640, tn=1536, tk=1408, hi=False):
    if mode == "tn":
        K, M = a.shape
    else:
        M, K = a.shape
    N = b.shape[0] if mode == "nt" else b.shape[1]
    tm = _div_tile(M, tm, 128 if mode == "tn" else 16)
    tn = _div_tile(N, tn, 128)
    tk = _div_tile(K, tk, 128 if mode != "tn" else 16)
    nk = K // tk
    dims = _DIMS[mode]

    def body(a_ref, b_ref, o_ref, *acc):
        if hi:
            part = lax.dot_general(a_ref[...], b_ref[...], (dims, ((), ())), precision=_HI, preferred_element_type=F32)
        else:
            part = _dg(a_ref[...], b_ref[...], dims)
        if nk == 1:
            o_ref[...] = part.astype(o_ref.dtype)
        else:
            k = pl.program_id(2)

            @pl.when(k == 0)
            def _():
                acc[0][...] = part

            @pl.when(k > 0)
            def _():
                acc[0][...] += part

            @pl.when(k == nk - 1)
            def _():
                o_ref[...] = acc[0][...].astype(o_ref.dtype)

    if mode == "tn":
        a_spec = pl.BlockSpec((tk, tm), lambda i, j, k: (k, i))
    else:
        a_spec = pl.BlockSpec((tm, tk), lambda i, j, k: (i, k))
    if mode == "nt":
        b_spec = pl.BlockSpec((tn, tk), lambda i, j, k: (j, k))
    else:
        b_spec = pl.BlockSpec((tk, tn), lambda i, j, k: (k, j))
    isz = lambda x: jnp.dtype(x.dtype).itemsize
    vmem = 2 * (tm * tk * isz(a) + tk * tn * isz(b) + tm * tn * jnp.dtype(out_dtype).itemsize) + 3 * tm * tn * 4
    return _pc(
        body, name=name, grid=(M // tm, N // tn, nk),
        in_specs=[a_spec, b_spec], out_specs=pl.BlockSpec((tm, tn), lambda i, j, k: (i, j)),
        out_shape=_sds((M, N), out_dtype),
        scratch_shapes=[pltpu.VMEM((tm, tn), F32)] if nk > 1 else [],
        compiler_params=_cp(("parallel", "parallel", "arbitrary"), vmem + (8 << 20)),
    )(a, b)


def matmul_layers(a, b, mode, name):
    nl = b.shape[0]
    a3 = a if a.ndim == 3 else a[None]
    shared = a3.shape[0] == 1
    M = a3.shape[2] if mode == "tn" else a3.shape[1]
    N = b.shape[1] if mode == "nt" else b.shape[2]

    def body(a_ref, b_ref, o_ref):
        o_ref[0] = _dg(a_ref[0], b_ref[0], _DIMS[mode])

    return _pc(body, name=name, grid=(nl,),
               in_specs=[pl.BlockSpec((1,) + a3.shape[1:], (lambda l: (0, 0, 0)) if shared else (lambda l: (l, 0, 0))),
                         pl.BlockSpec((1,) + b.shape[1:], lambda l: (l, 0, 0))],
               out_specs=pl.BlockSpec((1, M, N), lambda l: (l, 0, 0)), out_shape=_sds((nl, M, N), F32),
               compiler_params=_cp(("parallel",), 48 << 20))(a3, b)


def out_proj_fwd(pieces, w, name):
    T = pieces[0][0].shape[0]
    arrs, offs = [a for a, _ in pieces], [o for _, o in pieces]
    n = len(arrs)
    tm = _div_tile(T, 640, 16)

    def body(*refs):
        w_ref, o_ref = refs[n], refs[n + 1]
        acc = None
        for j in range(n):
            part = _dg(refs[j][...], w_ref[offs[j]:offs[j] + arrs[j].shape[1], :], _DIMS["nn"])
            acc = part if acc is None else acc + part
        o_ref[...] = acc.astype(o_ref.dtype)

    return _pc(body, name=name, grid=(T // tm,),
               in_specs=[pl.BlockSpec((tm, a.shape[1]), lambda i: (i, 0)) for a in arrs] + [pl.BlockSpec(w.shape, lambda i: (0, 0))],
               out_specs=pl.BlockSpec((tm, w.shape[1]), lambda i: (i, 0)), out_shape=_sds((T, w.shape[1]), BF16),
               compiler_params=_cp(("parallel",), 32 << 20))(*arrs, w)


def out_proj_dw(pieces, dy, name):
    T, N = dy.shape
    arrs, offs = [a for a, _ in pieces], [o for _, o in pieces]
    n = len(arrs)
    rows = sum(a.shape[1] for a in arrs)
    tn = 512

    def body(*refs):
        d_ref, o_ref = refs[n], refs[n + 1]
        for j in range(n):
            o_ref[offs[j]:offs[j] + arrs[j].shape[1], :] = _dg(refs[j][...], d_ref[...], _DIMS["tn"]).astype(o_ref.dtype)

    return _pc(body, name=name, grid=(N // tn,),
               in_specs=[pl.BlockSpec(a.shape, lambda j: (0, 0)) for a in arrs] + [pl.BlockSpec((T, tn), lambda j: (0, j))],
               out_specs=pl.BlockSpec((rows, tn), lambda j: (0, j)), out_shape=_sds((rows, N), BF16),
               compiler_params=_cp(("parallel",), 48 << 20))(*arrs, dy)


def in_proj_bwd(pieces, h1, w, name):
    T = h1.shape[0]
    arrs = [a for a, _ in pieces]
    offs = [o for _, o in pieces]
    wid = [a.shape[1] for a in arrs]
    n = len(arrs)
    assert sum(wid) == IN_PAD, "the pieces must tile all columns of P"
    tm = _div_tile(T, 640, 16)

    def dx_body(*refs):
        w_ref, o_ref = refs[n], refs[n + 1]
        acc = None
        for j in range(n):
            part = _dg(refs[j][...], w_ref[:, offs[j]:offs[j] + wid[j]], _DIMS["nt"])
            acc = part if acc is None else acc + part
        o_ref[...] = acc.astype(o_ref.dtype)

    dh1 = _pc(dx_body, name=name + "_dx", grid=(T // tm,),
              in_specs=[pl.BlockSpec((tm, wj), lambda i: (i, 0)) for wj in wid] + [pl.BlockSpec((D, IN_PAD), lambda i: (0, 0))],
              out_specs=pl.BlockSpec((tm, D), lambda i: (i, 0)), out_shape=_sds((T, D), BF16),
              compiler_params=_cp(("parallel",), 40 << 20))(*arrs, w)

    tmd, nk = 512, 4
    tk = T // nk

    def dw_body(h_ref, *refs):
        o_ref, acc = refs[n], refs[n + 1]
        k = pl.program_id(1)

        @pl.when(k == 0)
        def _():
            acc[...] = jnp.zeros_like(acc)

        for j in range(n):
            acc[:, offs[j]:offs[j] + wid[j]] += _dg(h_ref[...], refs[j][...], _DIMS["tn"])

        @pl.when(k == nk - 1)
        def _():
            o_ref[...] = acc[...].astype(o_ref.dtype)

    dw = _pc(dw_body, name=name + "_dw", grid=(D // tmd, nk),
             in_specs=[pl.BlockSpec((tk, tmd), lambda i, k: (k, i))] + [pl.BlockSpec((tk, wj), lambda i, k: (k, 0)) for wj in wid],
             out_specs=pl.BlockSpec((tmd, IN_PAD), lambda i, k: (i, 0)), out_shape=_sds((D, IN_PAD), BF16),
             scratch_shapes=[pltpu.VMEM((tmd, IN_PAD), F32)], compiler_params=_cp(("parallel", "arbitrary"), 48 << 20))(h1, *arrs)
    return dh1, dw


def _norm_mod(xo, shift, scale, g):
    r = lax.rsqrt(jnp.mean(xo * xo, axis=-1, keepdims=True) + EPS)
    return (xo * r) * g * (1.0 + scale) + shift


def res_norm_mod(x, y, gsv, g, nL, name):
    T = x.shape[0]
    has_y = y is not None

    def body(*refs):
        if has_y:
            x_ref, y_ref, gsv_ref, g_ref, xo_ref, h_ref = refs
            xo = x_ref[...] + gsv_ref[0, 0:1, :] * y_ref[...]
            xo_ref[...] = xo
        else:
            x_ref, gsv_ref, g_ref, h_ref = refs
            xo = x_ref[...]
        h_ref[...] = _norm_mod(xo, gsv_ref[0, 1:2, :], gsv_ref[0, 2:3, :], g_ref[...]).astype(h_ref.dtype)

    row = pl.BlockSpec((TR, D), lambda i: (i, 0))
    in_specs = [row] + ([row] if has_y else []) + [pl.BlockSpec((1, 8, D), lambda i: (i // nL, 0, 0)),
                                                     pl.BlockSpec((1, D), lambda i: (0, 0))]
    out_specs = ([row] if has_y else []) + [row]
    out_shape = ([_sds((T, D), F32)] if has_y else []) + [_sds((T, D), BF16)]
    args = (x, y, gsv, g) if has_y else (x, gsv, g)
    outs = _pc(body, name=name, grid=(T // TR,), in_specs=in_specs, out_specs=out_specs, out_shape=out_shape,
               compiler_params=_cp(("arbitrary",), 24 << 20))(*args)
    return (outs[0], outs[1]) if has_y else (None, outs[0])


def res_norm_mod_bwd(xo, y, gsv, g, dh, dres, nL, name):
    T = xo.shape[0]
    has_y = y is not None

    def body(*refs):
        if has_y:
            xo_ref, y_ref, gsv_ref, g_ref, dh_ref, dres_ref, dx_ref, dy_ref, dgsv_ref, dg_ref = refs
        else:
            xo_ref, gsv_ref, g_ref, dh_ref, dres_ref, dx_ref, dgsv_ref, dg_ref = refs
        i = pl.program_id(0)

        @pl.when((i == 0) | (i == nL))
        def _():
            dgsv_ref[...] = jnp.zeros_like(dgsv_ref)

        @pl.when(i == 0)
        def _():
            dg_ref[...] = jnp.zeros_like(dg_ref)

        _, vjp = jax.vjp(_norm_mod, xo_ref[...], gsv_ref[0, 1:2, :], gsv_ref[0, 2:3, :], g_ref[...])
        dxn, dshift, dscale, dg = vjp(dh_ref[...].astype(F32))
        dxo = dres_ref[...] + dxn
        dx_ref[...] = dxo
        if has_y:
            dy_ref[...] = (gsv_ref[0, 0:1, :] * dxo).astype(dy_ref.dtype)
            dgsv_ref[0, 0:1, :] += jnp.sum(y_ref[...] * dxo, axis=0, keepdims=True)
        dgsv_ref[0, 1:2, :] += dshift
        dgsv_ref[0, 2:3, :] += dscale
        dg_ref[0:1, :] += dg

    row = pl.BlockSpec((TR, D), lambda i: (i, 0))
    gspec = pl.BlockSpec((1, 8, D), lambda i: (i // nL, 0, 0))
    in_specs = [row] + ([row] if has_y else []) + [gspec, pl.BlockSpec((1, D), lambda i: (0, 0)), row, row]
    out_specs = [row] + ([row] if has_y else []) + [gspec, pl.BlockSpec((8, D), lambda i: (0, 0))]
    out_shape = [_sds((T, D), F32)] + ([_sds((T, D), BF16)] if has_y else []) + [_sds((2, 8, D), F32), _sds((8, D), F32)]
    args = (xo, y, gsv, g, dh, dres) if has_y else (xo, gsv, g, dh, dres)
    outs = _pc(body, name=name, grid=(T // TR,), in_specs=in_specs, out_specs=out_specs, out_shape=out_shape,
               compiler_params=_cp(("arbitrary",), 32 << 20))(*args)
    if has_y:
        return outs
    return outs[0], None, outs[1], outs[2]


def final_loss(x, y, gsv, g, target, nL, name):
    T = x.shape[0]

    def lossf(xo, gv, t):
        yn = (xo * lax.rsqrt(jnp.mean(xo * xo, axis=-1, keepdims=True) + EPS)) * gv
        e = yn - t
        return 0.5 * jnp.sum(jnp.sum(e * e, axis=-1, keepdims=True) * (1.0 / D), axis=0, keepdims=True)

    def body(x_ref, y_ref, gsv_ref, g_ref, t_ref, loss_ref, dx_ref, dy_ref, dgsv_ref, dg_ref):
        i = pl.program_id(0)

        @pl.when(i == 0)
        def _():
            loss_ref[...] = jnp.zeros_like(loss_ref)
            dg_ref[...] = jnp.zeros_like(dg_ref)

        @pl.when((i == 0) | (i == nL))
        def _():
            dgsv_ref[...] = jnp.zeros_like(dgsv_ref)

        @pl.when(i < nL)
        def _():
            gate = gsv_ref[0, 0:1, :]
            yv = y_ref[...]
            xo = x_ref[...] + gate * yv
            lv, vjp = jax.vjp(lossf, xo, g_ref[...], t_ref[...])
            dxo, dg, _ = vjp(jnp.ones((1, 1), F32))
            loss_ref[...] += jnp.broadcast_to(lv, loss_ref.shape)
            dx_ref[...] = dxo
            dy_ref[...] = (gate * dxo).astype(dy_ref.dtype)
            dgsv_ref[0, 0:1, :] += jnp.sum(yv * dxo, axis=0, keepdims=True)
            dg_ref[0:1, :] += dg

        @pl.when(i >= nL)
        def _():
            dx_ref[...] = jnp.zeros_like(dx_ref)
            dy_ref[...] = jnp.zeros_like(dy_ref)

    row = pl.BlockSpec((TR, D), lambda i: (i, 0))
    gspec = pl.BlockSpec((1, 8, D), lambda i: (i // nL, 0, 0))
    return _pc(
        body, name=name, grid=(T // TR,),
        in_specs=[row, row, gspec, pl.BlockSpec((1, D), lambda i: (0, 0)),
                  pl.BlockSpec((TR, D), lambda i: (jnp.minimum(i, nL - 1), 0))],
        out_specs=[pl.BlockSpec((8, 128), lambda i: (0, 0)), row, row, gspec, pl.BlockSpec((8, D), lambda i: (0, 0))],
        out_shape=[_sds((8, 128), F32), _sds((T, D), F32), _sds((T, D), BF16), _sds((2, 8, D), F32), _sds((8, D), F32)],
        compiler_params=_cp(("arbitrary",), 32 << 20),
    )(x, y, gsv, g, target)


FI_BLK = 2 * D_FF // 4


def _fi_chip(j):
    return (j % 2) * 2 + j // 2


def matmul_fi(a, b, mode, out_dtype, name):
    T = a.shape[0]
    if mode == "tn":
        tmd = 512

        def body(a_ref, b_ref, o_ref):
            o_ref[0] = _dg(a_ref[...], b_ref[...], _DIMS["tn"]).astype(o_ref.dtype)

        return _pc(body, name=name, grid=(D // tmd, 4),
                   in_specs=[pl.BlockSpec((T, tmd), lambda i, j: (0, i)), pl.BlockSpec((T, FI_BLK), lambda i, j: (0, j))],
                   out_specs=pl.BlockSpec((1, tmd, FI_BLK), lambda i, j: (_fi_chip(j), i, 0)),
                   out_shape=_sds((4, D, FI_BLK), out_dtype), compiler_params=_cp(("parallel", "arbitrary"), 48 << 20))(a, b)
    if mode == "nn":
        tm = _div_tile(T, 1280, 16)

        def body(a_ref, b_ref, o_ref):
            o_ref[...] = _dg(a_ref[...], b_ref[0], _DIMS["nn"]).astype(o_ref.dtype)

        return _pc(body, name=name, grid=(T // tm, 4),
                   in_specs=[pl.BlockSpec((tm, D), lambda i, j: (i, 0)), pl.BlockSpec((1, D, FI_BLK), lambda i, j: (_fi_chip(j), 0, 0))],
                   out_specs=pl.BlockSpec((tm, FI_BLK), lambda i, j: (i, j)), out_shape=_sds((T, 4 * FI_BLK), out_dtype),
                   compiler_params=_cp(("parallel", "arbitrary"), 40 << 20))(a, b)
    tm = _div_tile(T, 640, 16)

    def body(a_ref, b_ref, o_ref):
        acc = None
        for k in range(4):
            part = _dg(a_ref[:, k * FI_BLK:(k + 1) * FI_BLK], b_ref[_fi_chip(k)], _DIMS["nt"])
            acc = part if acc is None else acc + part
        o_ref[...] = acc.astype(o_ref.dtype)

    return _pc(body, name=name, grid=(T // tm,),
               in_specs=[pl.BlockSpec((tm, 4 * FI_BLK), lambda i: (i, 0)), pl.BlockSpec((4, D, FI_BLK), lambda i: (0, 0, 0))],
               out_specs=pl.BlockSpec((tm, D), lambda i: (i, 0)), out_shape=_sds((T, D), out_dtype),
               compiler_params=_cp(("parallel",), VMEM_CAP))(a, b)


def _swiglu(gate, up):
    return _silu(gate) * up


def swiglu_fwd(gu, name):
    T = gu.shape[0]

    def body(x_ref, o_ref):
        o_ref[...] = _swiglu(x_ref[:, :FI_BLK].astype(F32), x_ref[:, FI_BLK:].astype(F32)).astype(o_ref.dtype)

    return _pc(body, name=name, grid=(T // TR, 2), in_specs=[pl.BlockSpec((TR, 2 * FI_BLK), lambda i, j: (i, j))],
               out_specs=pl.BlockSpec((TR, FI_BLK), lambda i, j: (i, j)), out_shape=_sds((T, D_FF), BF16),
               compiler_params=_cp(("parallel", "parallel"), 24 << 20))(gu)


def swiglu_bwd(gu, dact, name):
    T = gu.shape[0]

    def body(x_ref, d_ref, o_ref):
        g, u, d = x_ref[:, :FI_BLK].astype(F32), x_ref[:, FI_BLK:].astype(F32), d_ref[...].astype(F32)
        sg = 1.0 / (1.0 + jnp.exp(-g))
        sl = g * sg
        o_ref[:, :FI_BLK] = (d * u * (sg + sl * (1.0 - sg))).astype(o_ref.dtype)
        o_ref[:, FI_BLK:] = (d * sl).astype(o_ref.dtype)

    return _pc(body, name=name, grid=(T // TR, 2),
               in_specs=[pl.BlockSpec((TR, 2 * FI_BLK), lambda i, j: (i, j)), pl.BlockSpec((TR, FI_BLK), lambda i, j: (i, j))],
               out_specs=pl.BlockSpec((TR, 2 * FI_BLK), lambda i, j: (i, j)), out_shape=_sds((T, 2 * D_FF), BF16),
               compiler_params=_cp(("parallel", "parallel"), 32 << 20))(gu, dact)


def rope_tables(L, Lc):
    t = np.arange(L)
    rows, cols = t // GRID_W, t % GRID_W
    inv = ROPE_BASE ** (-np.arange(16, dtype=np.float32) / 16)
    lane = np.arange(64)
    pos = np.where((lane // 32)[None, :] == 0, rows[:, None], cols[:, None]).astype(np.float32)
    ang = jnp.asarray(pos) * jnp.asarray(inv[lane % 16])[None, :]
    cos = jnp.concatenate([jnp.cos(ang), jnp.ones((Lc, 64), F32)], axis=0)
    sin = jnp.concatenate([jnp.sin(ang), jnp.zeros((Lc, 64), F32)], axis=0)
    R = np.zeros((128, 128), np.float32)
    for i in range(128):
        if (i % 32) < 16:
            R[i + 16, i] = -1.0
        else:
            R[i - 16, i] = 1.0
    return jnp.tile(cos, (1, 2)), jnp.tile(sin, (1, 2)), jnp.asarray(R)


def rope_apply(q_src, q_col, k_src, k_col, cos, sin, R, transpose, name, kv_src=None):
    T = cos.shape[0]
    with_kv = kv_src is not None

    def rot(x, c, s, Rm):
        if transpose:
            return x * c + hdot(x * s, Rm, "nt")
        return x * c + hdot(x, Rm) * s

    def body(q_ref, k_ref, c_ref, s_ref, R_ref, *rest):
        qo_ref, ko_ref = rest[-4:-2] if with_kv else rest
        c, s, Rm = c_ref[...], s_ref[...], R_ref[...]
        for j in range(2):
            qo_ref[:, j * 128:(j + 1) * 128] = rot(q_ref[:, j * 128:(j + 1) * 128].astype(F32), c, s, Rm).astype(qo_ref.dtype)
        ko_ref[...] = rot(k_ref[...].astype(F32), c, s, Rm).astype(ko_ref.dtype)
        if with_kv:
            rest[-2][...] = rest[0][...].astype(BF16)
            rest[-1][...] = rest[1][...].astype(BF16)

    tab = pl.BlockSpec((TR, 128), lambda i: (i, 0))
    wide = pl.BlockSpec((TR, 256), lambda i: (i, 0))
    kv_in = [pl.BlockSpec((TR, 256), lambda i: (i, C_KB // 256)), pl.BlockSpec((TR, 256), lambda i: (i, C_VB // 256))] if with_kv else []
    return _pc(body, name=name, grid=(T // TR,),
               in_specs=[pl.BlockSpec((TR, 256), lambda i: (i, q_col)), pl.BlockSpec((TR, 128), lambda i: (i, k_col)),
                         tab, tab, pl.BlockSpec((128, 128), lambda i: (0, 0))] + kv_in,
               out_specs=[wide, tab] + ([wide, wide] if with_kv else []),
               out_shape=[_sds((T, 256), BF16), _sds((T, 128), BF16)] + ([_sds((T, 256), BF16)] * 2 if with_kv else []),
               compiler_params=_cp(("parallel",), 16 << 20))(q_src, k_src, cos, sin, R, *([kv_src, kv_src] if with_kv else []))


_SCALE = HD ** -0.5


def _attn_tile(qh, ks, vs, extra):
    ss = []
    for k, add in ks:
        s = _dg(qh, k, _DIMS["nt"]) * _SCALE
        ss.append(s if add is None else s + add)
    m = ss[0].max(axis=-1, keepdims=True)
    for s in ss[1:]:
        m = jnp.maximum(m, s.max(axis=-1, keepdims=True))
    if extra is not None:
        m = jnp.maximum(m, extra)
    ps = [jnp.exp(s - m) for s in ss]
    den = ps[0].sum(axis=-1, keepdims=True)
    for p in ps[1:]:
        den = den + p.sum(axis=-1, keepdims=True)
    if extra is not None:
        den = den + jnp.exp(extra - m)
    num = _dg(ps[0], vs[0], _DIMS["nn"])
    for p, v in zip(ps[1:], vs[1:]):
        num = num + _dg(p, v, _DIMS["nn"])
    linv = 1.0 / den
    return num * linv, m, linv


def _attn_bwd_tile(qh, ks, vs, extra, m, linv, oh, doh):
    delta = jnp.sum(doh * oh, axis=-1, keepdims=True)
    dq = None
    dks, dvs, dss = [], [], []
    for (k, add), v in zip(ks, vs):
        s = _dg(qh, k, _DIMS["nt"]) * _SCALE
        if add is not None:
            s = s + add
        p = jnp.exp(s - m) * linv
        dvs.append(_dg(p, doh, _DIMS["tn"]))
        ds = p * (_dg(doh, v, _DIMS["nt"]) - delta)
        dss.append(ds)
        dsq = ds * _SCALE
        part = _dg(dsq, k, _DIMS["nn"])
        dq = part if dq is None else dq + part
        dks.append(_dg(dsq, qh, _DIMS["tn"]))
    dextra = None
    if extra is not None:
        dextra = -(jnp.exp(extra - m) * linv * delta)
    return dq, dks, dvs, dss, dextra


def _wa_mask(n, L):
    qpos = n * WA_BLK + lax.broadcasted_iota(jnp.int32, (WA_BLK, 3 * WA_BLK), 0)
    kpos = (n - 1) * WA_BLK + lax.broadcasted_iota(jnp.int32, (WA_BLK, 3 * WA_BLK), 1)
    ok = (jnp.abs(qpos - kpos) <= WA_BLK) & (kpos >= 0) & (kpos < L)
    return jnp.where(ok, 0.0, NEG).astype(F32)


WA_BPS = 2
_WA_PAIRS = (((0, 0), (1, 3), False), ((1, 2), (0, 1), True))


def _swap_halves_lanes(a):
    return jnp.concatenate([a[:, HD:], a[:, :HD]], axis=1)


def _wa_specs(L, Lc):
    nb = L // WA_BLK
    cb = L // Lc

    def blk(j):
        return pl.BlockSpec((WA_BLK, 128), lambda s: (jnp.clip(s * WA_BPS - 1 + j, 0, nb - 1), 0))

    return nb, [blk(j) for j in range(WA_BPS + 2)] + [pl.BlockSpec((Lc, 128), lambda s: (cb, 0))]


def _wa_pair_q(q_ref, qs, lo, hi):
    a = q_ref[qs, lo[0] * 128:(lo[0] + 1) * 128]
    b = q_ref[qs, hi[0] * 128:(hi[0] + 1) * 128]
    lane = lax.broadcasted_iota(jnp.int32, a.shape, 1)
    zero = jnp.zeros_like(a)
    return jnp.concatenate([jnp.where(lane < HD, a, zero), jnp.where(lane >= HD, b, zero)], axis=0)


def _wa_pair_vec(ref, qs, lo, hi, base=0):
    return jnp.concatenate([ref[qs, base + lo[1]:base + lo[1] + 1], ref[qs, base + hi[1]:base + hi[1] + 1]], axis=0)


def _wa_pair_sink(s_ref, n, lo, hi):
    return jnp.concatenate([jnp.broadcast_to(s_ref[lo[1]:lo[1] + 1, 0:1], (n, 1)), jnp.broadcast_to(s_ref[hi[1]:hi[1] + 1, 0:1], (n, 1))], axis=0)


def win_attn_fwd(qr, kr, krs, v, vs, sink, L, Lc, name):
    T = L + Lc
    nb, specs = _wa_specs(L, Lc)
    nk = WA_BPS + 2
    QB = WA_BPS * WA_BLK
    nlat = nb // WA_BPS

    def body(q_ref, *refs):
        groups = [refs[g * (nk + 1):(g + 1) * (nk + 1)] for g in range(4)]
        s_ref, o_ref, st_ref = refs[-3], refs[-2], refs[-1]
        s = pl.program_id(0)

        def run(qs, n, ks_of, vs_of):
            outs = []
            for lo, hi, swapped in _WA_PAIRS:
                kb, vb = groups[1 if swapped else 0], groups[3 if swapped else 2]
                o2, m2, l2 = _attn_tile(_wa_pair_q(q_ref, qs, lo, hi), ks_of(kb), vs_of(vb), _wa_pair_sink(s_ref, n, lo, hi))
                outs.append(o2)
                for r, (_, h) in enumerate((lo, hi)):
                    st_ref[qs, h:h + 1] = m2[r * n:(r + 1) * n]
                    st_ref[qs, WA_HEADS + h:WA_HEADS + h + 1] = l2[r * n:(r + 1) * n]
            lane = lax.broadcasted_iota(jnp.int32, (n, 128), 1)
            o_ref[qs, 0:128] = jnp.where(lane < HD, outs[0][:n], outs[1][n:]).astype(o_ref.dtype)
            o_ref[qs, 128:256] = jnp.where(lane < HD, outs[1][:n], outs[0][n:]).astype(o_ref.dtype)

        @pl.when(s < nlat)
        def _():
            for b in range(WA_BPS):
                m1 = _wa_mask(s * WA_BPS + b, L)
                mask = jnp.concatenate([m1, m1], axis=0)
                cat = lambda g: jnp.concatenate([g[b + j][...] for j in range(3)], axis=0)
                run(slice(b * WA_BLK, (b + 1) * WA_BLK), WA_BLK,
                    lambda kb: [(cat(kb), mask), (kb[nk][...], None)], lambda vb: [cat(vb), vb[nk][...]])

        @pl.when(s >= nlat)
        def _():
            run(slice(None), QB, lambda kb: [(kb[nk][...], None)], lambda vb: [vb[nk][...]])

    qspec = pl.BlockSpec((QB, 256), lambda s: (s, 0))
    return _pc(body, name=name, grid=(T // QB,),
               in_specs=[qspec] + specs * 4 + [pl.BlockSpec((8, 128), lambda s: (0, 0))],
               out_specs=[qspec, pl.BlockSpec((QB, 8), lambda s: (s, 0))], out_shape=[_sds((T, 256), BF16), _sds((T, 8), F32)],
               compiler_params=_cp(("arbitrary",), 40 << 20))(qr, *([kr] * (nk + 1)), *([krs] * (nk + 1)), *([v] * (nk + 1)), *([vs] * (nk + 1)), sink)


def win_attn_bwd(qr, kr, krs, v, vs, sink, do_src, o, stats, L, Lc, name):
    T = L + Lc
    nb, specs = _wa_specs(L, Lc)
    nk = WA_BPS + 2
    QB = WA_BPS * WA_BLK
    nlat = nb // WA_BPS
    cx = WA_BLK + L

    def body(q_ref, *refs):
        groups = [refs[g * (nk + 1):(g + 1) * (nk + 1)] for g in range(4)]
        s_ref, do_ref, o_ref, st_ref, dq_ref, dk_ref, dks_ref, dv_ref, dvs_ref, ds_ref = refs[4 * (nk + 1):]
        s = pl.program_id(0)

        @pl.when(s == 0)
        def _():
            for r in (dk_ref, dks_ref, dv_ref, dvs_ref, ds_ref):
                r[...] = jnp.zeros_like(r)

        def run(qs, n, ks_of, vs_of, rows):
            lane = lax.broadcasted_iota(jnp.int32, (n, 128), 1)
            dqs = []
            for lo, hi, swapped in _WA_PAIRS:
                kb, vb = groups[1 if swapped else 0], groups[3 if swapped else 2]
                dka, dva = (dks_ref, dvs_ref) if swapped else (dk_ref, dv_ref)
                pair = lambda ref: jnp.concatenate([jnp.where(lane < HD, ref[qs, lo[0] * 128:(lo[0] + 1) * 128].astype(F32), 0.0),
                                                    jnp.where(lane >= HD, ref[qs, hi[0] * 128:(hi[0] + 1) * 128].astype(F32), 0.0)], axis=0)
                dq2, dks, dvs, _, dex = _attn_bwd_tile(_wa_pair_q(q_ref, qs, lo, hi), ks_of(kb), vs_of(vb), _wa_pair_sink(s_ref, n, lo, hi),
                                                       _wa_pair_vec(st_ref, qs, lo, hi), _wa_pair_vec(st_ref, qs, lo, hi, WA_HEADS), pair(o_ref), pair(do_ref))
                dqs.append(dq2)
                for r, (_, h) in enumerate((lo, hi)):
                    ds_ref[h:h + 1, :] += jnp.broadcast_to(jnp.sum(dex[r * n:(r + 1) * n], axis=0, keepdims=True), (1, 128))
                if rows is not None:
                    dka[rows, :] += dks[0]
                    dva[rows, :] += dvs[0]
                dka[cx:cx + Lc, :] += dks[-1]
                dva[cx:cx + Lc, :] += dvs[-1]
            dq_ref[qs, 0:128] = jnp.where(lane < HD, dqs[0][:n], dqs[1][n:])
            dq_ref[qs, 128:256] = jnp.where(lane < HD, dqs[1][:n], dqs[0][n:])

        @pl.when(s < nlat)
        def _():
            for b in range(WA_BPS):
                nblk = s * WA_BPS + b
                m1 = _wa_mask(nblk, L)
                mask = jnp.concatenate([m1, m1], axis=0)
                cat = lambda g: jnp.concatenate([g[b + j][...] for j in range(3)], axis=0)
                run(slice(b * WA_BLK, (b + 1) * WA_BLK), WA_BLK, lambda kb: [(cat(kb), mask), (kb[nk][...], None)],
                    lambda vb: [cat(vb), vb[nk][...]], pl.ds(pl.multiple_of(nblk * WA_BLK, WA_BLK), 3 * WA_BLK))

        @pl.when(s >= nlat)
        def _():
            run(slice(None), QB, lambda kb: [(kb[nk][...], None)], lambda vb: [vb[nk][...]], None)

    qspec = pl.BlockSpec((QB, 256), lambda s: (s, 0))
    acc_spec = pl.BlockSpec((T + 2 * WA_BLK, 128), lambda s: (0, 0))
    acc_shape = _sds((T + 2 * WA_BLK, 128), F32)
    return _pc(body, name=name, grid=(T // QB,),
               in_specs=[qspec] + specs * 4 + [pl.BlockSpec((8, 128), lambda s: (0, 0)), qspec, qspec, pl.BlockSpec((QB, 8), lambda s: (s, 0))],
               out_specs=[qspec, acc_spec, acc_spec, acc_spec, acc_spec, pl.BlockSpec((8, 128), lambda s: (0, 0))],
               out_shape=[_sds((T, 256), F32), acc_shape, acc_shape, acc_shape, acc_shape, _sds((8, 128), F32)],
               compiler_params=_cp(("arbitrary",), 48 << 20))(qr, *([kr] * (nk + 1)), *([krs] * (nk + 1)), *([v] * (nk + 1)), *([vs] * (nk + 1)),
                                                              sink, do_src, o, stats)


def na_index_tables():
    qc = np.arange(GRID_W)[:, None]
    kc = np.arange(GRID_W)[None, :]
    cstart = np.clip(qc - NA_KW // 2, 0, GRID_W - NA_KW)
    ok = (kc >= cstart) & (kc < cstart + NA_KW)
    dx = np.clip(kc - qc, -(NA_KW - 1), NA_KW - 1) + (NA_KW - 1)
    off = np.arange(NA_KH)[:, None]
    kr = np.arange(NA_KH)[None, :]
    dy = kr - off + (NA_KH - 1)
    return ok, dx, dy


def _na_selectors():
    ok, dx, dy = na_index_tables()
    e1 = np.zeros((GRID_W * GRID_W, 128), np.float32)
    qi, ki = np.nonzero(ok)
    e1[qi * GRID_W + ki, dx[qi, ki]] = 1.0
    e2 = np.zeros((16, NA_KH * NA_KH), np.float32)
    oi, ri = np.meshgrid(np.arange(NA_KH), np.arange(NA_KH), indexing="ij")
    e2[dy[oi, ri].ravel(), (oi * NA_KH + ri).ravel()] = 1.0
    return ok, jnp.asarray(e1), jnp.asarray(np.kron(np.eye(NA_HEADS, dtype=np.float32), e2))


def na_bias_table(rpb, tag):
    ok, e1, e2 = _na_selectors()
    r2 = jnp.pad(rpb.astype(F32), ((0, 0), (0, 1), (0, 128 - (2 * NA_KW - 1)))).reshape(NA_HEADS * 16, 128)
    r1 = matmul(e2, r2, "tn", F32, f"na_bias_sel1_{tag}", hi=True)
    x = matmul(r1, e1, "nt", F32, f"na_bias_sel2_{tag}", hi=True)
    b = x.reshape(NA_HEADS, NA_KH, NA_KH, GRID_W, GRID_W).transpose(0, 1, 3, 2, 4)
    b = b + jnp.asarray(np.where(ok, 0.0, NEG).astype(np.float32))[None, None, :, None, :]
    return b.reshape(NA_HEADS, NA_KH, GRID_W, NA_KH * GRID_W)


def _na_rows(r, GR):
    r0 = jnp.clip(r - NA_KH // 2, 0, GR - NA_KH)
    return r0, jnp.clip(r - r0, 0, NA_KH - 1)


NA_RPS = 4


def _pair_rows(x):
    lane = lax.broadcasted_iota(jnp.int32, x.shape, 1)
    zero = jnp.zeros_like(x)
    return jnp.concatenate([jnp.where(lane < HD, x, zero), jnp.where(lane >= HD, x, zero)], axis=0)


def _unpair_rows(x2):
    n = x2.shape[0] // 2
    lane = lax.broadcasted_iota(jnp.int32, (n, 128), 1)
    return jnp.where(lane < HD, x2[:n], x2[n:])


def na_fwd(P, kb, vb, bias, L, Lc, name):
    T = L + Lc
    GR = L // GRID_W
    W = NA_KH * GRID_W
    QB = GRID_W * NA_RPS
    nlat = GR // NA_RPS

    def body(q_ref, k_ref, v_ref, b_ref, o_ref, st_ref):
        s = pl.program_id(0)

        def put(qs, p, res):
            o2, m2, l2 = res
            n = o2.shape[0] // 2
            o_ref[qs, p * 128:(p + 1) * 128] = _unpair_rows(o2).astype(o_ref.dtype)
            for r in range(2):
                st_ref[qs, 2 * p + r:2 * p + r + 1] = m2[r * n:(r + 1) * n]
                st_ref[qs, NA_HEADS + 2 * p + r:NA_HEADS + 2 * p + r + 1] = l2[r * n:(r + 1) * n]

        @pl.when(s < nlat)
        def _():
            for rr in range(NA_RPS):
                r0, off = _na_rows(s * NA_RPS + rr, GR)
                rows = pl.ds(pl.multiple_of(r0 * GRID_W, GRID_W), W)
                qs = slice(rr * GRID_W, (rr + 1) * GRID_W)
                for p in range(NA_HEADS // 2):
                    ps = slice(p * 128, (p + 1) * 128)
                    b2 = jnp.concatenate([b_ref[2 * p, off], b_ref[2 * p + 1, off]], axis=0)
                    put(qs, p, _attn_tile(_pair_rows(q_ref[qs, ps]), [(k_ref[rows, ps], b2), (k_ref[L:T, ps], None)],
                                          [v_ref[rows, ps], v_ref[L:T, ps]], None))

        @pl.when(s >= nlat)
        def _():
            for p in range(NA_HEADS // 2):
                ps = slice(p * 128, (p + 1) * 128)
                put(slice(None), p, _attn_tile(_pair_rows(q_ref[:, ps]), [(k_ref[L:T, ps], None)], [v_ref[L:T, ps]], None))

    one = pl.Buffered(1)
    return _pc(body, name=name, grid=(T // QB,),
               in_specs=[pl.BlockSpec((QB, 256), lambda r: (r, C_QB // 256)),
                         pl.BlockSpec((T, 256), lambda r: (0, 0), pipeline_mode=one),
                         pl.BlockSpec((T, 256), lambda r: (0, 0), pipeline_mode=one),
                         pl.BlockSpec((NA_HEADS, NA_KH, GRID_W, W), lambda r: (0, 0, 0, 0), pipeline_mode=one)],
               out_specs=[pl.BlockSpec((QB, 256), lambda r: (r, 0)), pl.BlockSpec((QB, 8), lambda r: (r, 0))],
               out_shape=[_sds((T, 256), BF16), _sds((T, 8), F32)],
               compiler_params=_cp(("arbitrary",), 32 << 20))(P, kb, vb, bias)


def na_bwd(P, kb, vb, bias, do_src, o, stats, L, Lc, name):
    T = L + Lc
    GR = L // GRID_W
    W = NA_KH * GRID_W
    QB = GRID_W * NA_RPS
    nlat = GR // NA_RPS

    def body(q_ref, k_ref, v_ref, b_ref, do_ref, o_ref, st_ref, dq_ref, dk_ref, dv_ref, db_ref):
        s = pl.program_id(0)

        @pl.when(s == 0)
        def _():
            dk_ref[...] = jnp.zeros_like(dk_ref)
            dv_ref[...] = jnp.zeros_like(dv_ref)
            db_ref[...] = jnp.zeros_like(db_ref)

        def tile(qs, p, ks, vs):
            ps = slice(p * 128, (p + 1) * 128)
            m2 = jnp.concatenate([st_ref[qs, 2 * p:2 * p + 1], st_ref[qs, 2 * p + 1:2 * p + 2]], axis=0)
            l2 = jnp.concatenate([st_ref[qs, NA_HEADS + 2 * p:NA_HEADS + 2 * p + 1], st_ref[qs, NA_HEADS + 2 * p + 1:NA_HEADS + 2 * p + 2]], axis=0)
            dq2, dks, dvs, dss, _ = _attn_bwd_tile(_pair_rows(q_ref[qs, ps]), ks, vs, None, m2, l2,
                                                   _pair_rows(o_ref[qs, ps].astype(F32)), _pair_rows(do_ref[qs, ps].astype(F32)))
            dq_ref[qs, ps] = _unpair_rows(dq2).astype(dq_ref.dtype)
            return dks, dvs, dss

        @pl.when(s < nlat)
        def _():
            for rr in range(NA_RPS):
                r0, off = _na_rows(s * NA_RPS + rr, GR)
                rows = pl.ds(pl.multiple_of(r0 * GRID_W, GRID_W), W)
                qs = slice(rr * GRID_W, (rr + 1) * GRID_W)
                for p in range(NA_HEADS // 2):
                    ps = slice(p * 128, (p + 1) * 128)
                    b2 = jnp.concatenate([b_ref[2 * p, off], b_ref[2 * p + 1, off]], axis=0)
                    dks, dvs, dss = tile(qs, p, [(k_ref[rows, ps], b2), (k_ref[L:T, ps], None)], [v_ref[rows, ps], v_ref[L:T, ps]])
                    dk_ref[rows, ps] += dks[0]
                    dv_ref[rows, ps] += dvs[0]
                    dk_ref[L:T, ps] += dks[1]
                    dv_ref[L:T, ps] += dvs[1]
                    db_ref[2 * p, off] += dss[0][:GRID_W]
                    db_ref[2 * p + 1, off] += dss[0][GRID_W:]

        @pl.when(s >= nlat)
        def _():
            for p in range(NA_HEADS // 2):
                ps = slice(p * 128, (p + 1) * 128)
                dks, dvs, _ = tile(slice(None), p, [(k_ref[L:T, ps], None)], [v_ref[L:T, ps]])
                dk_ref[L:T, ps] += dks[0]
                dv_ref[L:T, ps] += dvs[0]

    one = pl.Buffered(1)
    full = lambda shape: pl.BlockSpec(shape, lambda r: (0,) * len(shape), pipeline_mode=one)
    qspec = pl.BlockSpec((QB, 256), lambda r: (r, 0))
    return _pc(body, name=name, grid=(T // QB,),
               in_specs=[pl.BlockSpec((QB, 256), lambda r: (r, C_QB // 256)), full((T, 256)), full((T, 256)),
                         full((NA_HEADS, NA_KH, GRID_W, W)), pl.BlockSpec((QB, 256), lambda r: (r, 1)), qspec, pl.BlockSpec((QB, 8), lambda r: (r, 0))],
               out_specs=[qspec, full((T, 256)), full((T, 256)), full((NA_HEADS, NA_KH, GRID_W, W))],
               out_shape=[_sds((T, 256), BF16), _sds((T, 256), F32), _sds((T, 256), F32), _sds((NA_HEADS, NA_KH, GRID_W, W), F32)],
               compiler_params=_cp(("arbitrary",), 48 << 20))(P, kb, vb, bias, do_src, o, stats)


def na_rpb_grad(dbias, tag):
    _, e1, e2 = _na_selectors()
    x = dbias.reshape(NA_HEADS, NA_KH, GRID_W, NA_KH, GRID_W).transpose(0, 1, 3, 2, 4).reshape(NA_HEADS * NA_KH * NA_KH, GRID_W * GRID_W)
    r1 = matmul(x, e1, "nn", F32, f"na_rpb_sel1_{tag}", hi=True, tk=1024)
    r2 = matmul(e2, r1, "nn", F32, f"na_rpb_sel2_{tag}", hi=True)
    return r2.reshape(NA_HEADS, 16, 128)[:, :2 * NA_KH - 1, :2 * NA_KW - 1]


_HALO = 8


def _halo_specs(T, col0):
    nh = TR // _HALO
    cur = pl.BlockSpec((TR, 256), lambda i, j: (i, col0 + j))
    prv = pl.BlockSpec((_HALO, 256), lambda i, j: (jnp.maximum(i * nh - 1, 0), col0 + j))
    nxt = pl.BlockSpec((_HALO, 256), lambda i, j: (jnp.minimum((i + 1) * nh, T // _HALO - 1), col0 + j))
    return prv, cur, nxt


def _fill_ext(ext, prv, cur, nxt, i, nL, nT):
    has_prev = jnp.where((i != 0) & (i != nL), 1.0, 0.0)
    has_next = jnp.where((i != nL - 1) & (i != nT - 1), 1.0, 0.0)
    ext[0:_HALO, :] = prv[...].astype(F32) * has_prev
    ext[_HALO:_HALO + TR, :] = cur[...].astype(F32)
    ext[_HALO + TR:, :] = nxt[...].astype(F32) * has_next


def conv_silu_fwd(P, w8, b, nL, name):
    T = P.shape[0]
    nT = T // TR

    def body(prv, cur, nxt, w_ref, b_ref, pre_ref, act_ref, ext):
        i = pl.program_id(0)
        _fill_ext(ext, prv, cur, nxt, i, nL, nT)
        y = jnp.broadcast_to(b_ref[...], (TR, 256))
        for k in range(S_CONV):
            y = y + w_ref[k:k + 1, :] * ext[pl.ds(_HALO - S_CONV // 2 + k, TR), :]
        pre_ref[...] = y
        act_ref[...] = _silu(y)

    prv, cur, nxt = _halo_specs(T, C_XBC // 256)
    out = pl.BlockSpec((TR, 256), lambda i, j: (i, j))
    return _pc(body, name=name, grid=(nT, 4),
               in_specs=[prv, cur, nxt, pl.BlockSpec((8, 256), lambda i, j: (0, j)), pl.BlockSpec((1, 256), lambda i, j: (0, j))],
               out_specs=[out, out], out_shape=[_sds((T, 1024), F32), _sds((T, 1024), F32)],
               scratch_shapes=[pltpu.VMEM((TR + 2 * _HALO, 256), F32)],
               compiler_params=_cp(("parallel", "parallel"), 16 << 20))(P, P, P, w8, b)


def dsilu(pre, dxs_list, db_list, dc_list, name):
    T = pre.shape[0]
    n1, n2, n3 = len(dxs_list), len(db_list), len(dc_list)

    def body(*refs):
        pre_ref = refs[0]
        ins = refs[1:1 + n1 + n2 + n3]
        out = refs[-1]

        def part(rs, lo, hi):
            g = rs[0][...].astype(F32)
            for r in rs[1:]:
                g = g + r[...].astype(F32)
            x = pre_ref[:, lo:hi]
            sg = 1.0 / (1.0 + jnp.exp(-x))
            sl = x * sg
            out[:, lo:hi] = g * (sg + sl * (1.0 - sg))

        part(ins[:n1], 0, 512)
        part(ins[n1:n1 + n2], 512, 768)
        part(ins[n1 + n2:], 768, 1024)

    spec = lambda w: pl.BlockSpec((TR, w), lambda i: (i, 0))
    return _pc(body, name=name, grid=(T // TR,),
               in_specs=[spec(1024)] + [spec(512)] * n1 + [spec(256)] * (n2 + n3),
               out_specs=spec(1024), out_shape=_sds((T, 1024), F32),
               compiler_params=_cp(("parallel",), 32 << 20))(pre, *dxs_list, *db_list, *dc_list)


def conv_bwd(dpre, P, w8, nL, name):
    T = P.shape[0]
    nT = T // TR

    def body(dp, dc, dn, xp, xc, xn, w_ref, dx_ref, dw_ref, db_ref, extd, extx):
        i = pl.program_id(1)
        _fill_ext(extd, dp, dc, dn, i, nL, nT)
        _fill_ext(extx, xp, xc, xn, i, nL, nT)

        @pl.when(i == 0)
        def _():
            dw_ref[...] = jnp.zeros_like(dw_ref)
            db_ref[...] = jnp.zeros_like(db_ref)

        d = dc[...]
        dx = jnp.zeros((TR, 256), F32)
        for k in range(S_CONV):
            dx = dx + w_ref[k:k + 1, :] * extd[pl.ds(_HALO + S_CONV // 2 - k, TR), :]
            dw_ref[k:k + 1, :] += jnp.sum(d * extx[pl.ds(_HALO - S_CONV // 2 + k, TR), :], axis=0, keepdims=True)
        dx_ref[...] = dx.astype(dx_ref.dtype)
        db_ref[0:1, :] += jnp.sum(d, axis=0, keepdims=True)

    def swap(spec):
        f = spec.index_map
        return pl.BlockSpec(spec.block_shape, lambda j, i: f(i, j))

    dprv, dcur, dnxt = [swap(s) for s in _halo_specs(T, 0)]
    xprv, xcur, xnxt = [swap(s) for s in _halo_specs(T, C_XBC // 256)]
    acc = pl.BlockSpec((8, 256), lambda j, i: (0, j))
    return _pc(body, name=name, grid=(4, nT),
               in_specs=[dprv, dcur, dnxt, xprv, xcur, xnxt, acc],
               out_specs=[pl.BlockSpec((TR, 256), lambda j, i: (i, j)), acc, acc],
               out_shape=[_sds((T, 1024), BF16), _sds((8, 1024), F32), _sds((8, 1024), F32)],
               scratch_shapes=[pltpu.VMEM((TR + 2 * _HALO, 256), F32), pltpu.VMEM((TR + 2 * _HALO, 256), F32)],
               compiler_params=_cp(("parallel", "arbitrary"), 16 << 20))(dpre, dpre, dpre, P, P, P, w8)


def _onehot_row(h, n):
    return (lax.broadcasted_iota(jnp.int32, (1, n), 1) == h).astype(F32)


def _onehot_col(h, n):
    return (lax.broadcasted_iota(jnp.int32, (n, 1), 0) == h).astype(F32)


S_PAIRS = S_HEADS // 2


def _ssd_chunk(xs, dtr, dtb, alog, bm, cm, hin, reverse):
    Qn = S_Q
    ii = lax.broadcasted_iota(jnp.int32, (Qn, Qn), 0)
    jj = lax.broadcasted_iota(jnp.int32, (Qn, Qn), 1)
    keep = (ii <= jj) if reverse else (ii >= jj)
    tri = keep.astype(F32)
    triT = ((jj <= ii) if reverse else (jj >= ii)).astype(F32)
    eye = (ii == jj).astype(F32)
    low = jj < S_P
    top = ii < S_P
    dt = _softplus(dtr + dtb)
    a = dt * (-jnp.exp(alog))
    cs = hdot(tri, a)
    csT = hdot(a, triT, "tn")
    dtT = hdot(dt, eye, "tn")
    last = _onehot_row(0 if reverse else Qn - 1, Qn)
    ys, houts = [], []
    for p in range(S_PAIRS):
        g = p // (S_PAIRS // S_GROUPS)
        if p % (S_PAIRS // S_GROUPS) == 0:
            G = bdot(cm[g], bm[g], "nt")
        per_head = []
        for h in (2 * p, 2 * p + 1):
            eh_r, eh_c = _onehot_row(h, S_HEADS), _onehot_col(h, S_HEADS)
            cs_c = jnp.sum(cs * eh_r, axis=1, keepdims=True)
            dt_c = jnp.sum(dt * eh_r, axis=1, keepdims=True)
            cs_r = jnp.sum(csT * eh_c, axis=0, keepdims=True)
            dt_r = jnp.sum(dtT * eh_c, axis=0, keepdims=True)
            tot = jnp.sum(cs_r * last, axis=1, keepdims=True)
            w = G * jnp.exp(jnp.where(keep, cs_c - cs_r, NEG)) * dt_r
            per_head.append((bdot(w, xs[p], "nn"), jnp.exp(cs_c), jnp.exp(tot - cs_c) * dt_c, jnp.exp(tot)))
        (y0, e0, f0, d0), (y1, e1, f1, d1) = per_head
        y = jnp.where(low, y0, y1) + bdot(cm[g], hin[p], "nt") * jnp.where(low, e0, e1)
        hout = hin[p] * jnp.where(top, d0, d1) + bdot(xs[p] * jnp.where(low, f0, f1), bm[g], "tn")
        ys.append(y)
        houts.append(hout)
    return ys, houts


def _ssd_orders(L, Lc):
    nl, ncx = L // S_Q, Lc // S_Q
    fwd = lambda s: jnp.where(s < ncx, nl + s, s - ncx)
    bwd = lambda s: nl + ncx - 1 - s
    return nl + ncx, fwd, bwd


def _ssd_in_specs(fo, bo, step):
    def at(order, w, col):
        return pl.BlockSpec((S_Q, w), lambda u: (order(step(u)), col))
    specs = []
    for order in (fo, bo):
        specs += [at(order, 512, 0), at(order, 256, 2), at(order, 256, 3), at(order, 128, C_DT // 128)]
    return specs


def ssd_fwd(act, P, dtb, alog, L, Lc, name):
    T = L + Lc
    ns, fo, bo = _ssd_orders(L, Lc)

    def body(xf, bf, cf, df, xb, bb, cb, db, dtb_ref, al_ref, yf, yb, hsf, hsb, Hf, Hb):
        s = pl.program_id(0)

        @pl.when(s == 0)
        def _():
            Hf[...] = jnp.zeros_like(Hf)
            Hb[...] = jnp.zeros_like(Hb)

        for d, (x_r, b_r, c_r, dt_r, y_r, hs_r, H) in enumerate(((xf, bf, cf, df, yf, hsf, Hf), (xb, bb, cb, db, yb, hsb, Hb))):
            hin = [H[p] for p in range(S_PAIRS)]
            hs_r[0] = H[...]
            ys, houts = _ssd_chunk(
                [x_r[:, p * 128:(p + 1) * 128] for p in range(S_PAIRS)], dt_r[:, d * 8:(d + 1) * 8],
                dtb_ref[d:d + 1, 0:8], al_ref[d:d + 1, 0:8],
                [b_r[:, g * S_N:(g + 1) * S_N] for g in range(S_GROUPS)], [c_r[:, g * S_N:(g + 1) * S_N] for g in range(S_GROUPS)],
                hin, reverse=(d == 1))
            for p in range(S_PAIRS):
                y_r[:, p * 128:(p + 1) * 128] = ys[p]
                H[p] = houts[p]

    ident = lambda u: u
    small = pl.BlockSpec((8, 128), lambda u: (0, 0))
    hspec = pl.BlockSpec((1, S_PAIRS, 2 * S_P, S_N), lambda u: (u, 0, 0, 0))
    return _pc(body, name=name, grid=(ns,),
               in_specs=_ssd_in_specs(fo, bo, ident) + [small, small],
               out_specs=[pl.BlockSpec((S_Q, 512), lambda u: (fo(u), 0)), pl.BlockSpec((S_Q, 512), lambda u: (bo(u), 0)), hspec, hspec],
               out_shape=[_sds((T, 512), F32), _sds((T, 512), F32), _sds((ns, S_PAIRS, 2 * S_P, S_N), F32), _sds((ns, S_PAIRS, 2 * S_P, S_N), F32)],
               scratch_shapes=[pltpu.VMEM((S_PAIRS, 2 * S_P, S_N), F32), pltpu.VMEM((S_PAIRS, 2 * S_P, S_N), F32)],
               compiler_params=_cp(("arbitrary",), 32 << 20))(act, act, act, P, act, act, act, P, dtb, alog)


def ssd_bwd(act, P, dtb, alog, hsf, hsb, dy, L, Lc, name):
    T = L + Lc
    ns, fo, bo = _ssd_orders(L, Lc)
    step = lambda u: ns - 1 - u

    def body(xf, bf, cf, df, xb, bb, cb, db, dtb_ref, al_ref, hsf_r, hsb_r, dyf, dyb,
             dxf, dbf, dcf, ddf, dxb, dbb, dcb, ddb, ddtb, dal, dHf, dHb):
        u = pl.program_id(0)

        @pl.when(u == 0)
        def _():
            dHf[...] = jnp.zeros_like(dHf)
            dHb[...] = jnp.zeros_like(dHb)
            ddtb[...] = jnp.zeros_like(ddtb)
            dal[...] = jnp.zeros_like(dal)

        dirs = ((xf, bf, cf, df, hsf_r, dyf, dxf, dbf, dcf, ddf, dHf), (xb, bb, cb, db, hsb_r, dyb, dxb, dbb, dcb, ddb, dHb))
        for d, (x_r, b_r, c_r, dt_r, hs_r, dy_r, dx_o, db_o, dc_o, dd_o, dH) in enumerate(dirs):
            f = functools.partial(_ssd_chunk, reverse=(d == 1))
            _, vjp = jax.vjp(
                f, [x_r[:, p * 128:(p + 1) * 128] for p in range(S_PAIRS)], dt_r[:, d * 8:(d + 1) * 8],
                dtb_ref[d:d + 1, 0:8], al_ref[d:d + 1, 0:8],
                [b_r[:, g * S_N:(g + 1) * S_N] for g in range(S_GROUPS)], [c_r[:, g * S_N:(g + 1) * S_N] for g in range(S_GROUPS)],
                [hs_r[0, p] for p in range(S_PAIRS)])
            gx, gdt, gdtb, gal, gb, gc, gh = vjp(([dy_r[:, p * 128:(p + 1) * 128] for p in range(S_PAIRS)],
                                                  [dH[p] for p in range(S_PAIRS)]))
            for p in range(S_PAIRS):
                dx_o[:, p * 128:(p + 1) * 128] = gx[p]
                dH[p] = gh[p]
            for g in range(S_GROUPS):
                db_o[:, g * S_N:(g + 1) * S_N] = gb[g]
                dc_o[:, g * S_N:(g + 1) * S_N] = gc[g]
            dd_o[...] = gdt
            ddtb[d:d + 1, 0:8] += gdtb
            dal[d:d + 1, 0:8] += gal

    small = pl.BlockSpec((8, 128), lambda u: (0, 0))
    hspec = pl.BlockSpec((1, S_PAIRS, 2 * S_P, S_N), lambda u: (step(u), 0, 0, 0))
    at = lambda order, w: pl.BlockSpec((S_Q, w), lambda u: (order(step(u)), 0))
    outs = []
    for order in (fo, bo):
        outs += [at(order, 512), at(order, 256), at(order, 256), at(order, 8)]
    oshape = [_sds((T, 512), F32), _sds((T, 256), F32), _sds((T, 256), F32), _sds((T, 8), F32)]
    return _pc(body, name=name, grid=(ns,),
               in_specs=_ssd_in_specs(fo, bo, step) + [small, small, hspec, hspec, at(fo, 512), at(bo, 512)],
               out_specs=outs + [small, small], out_shape=oshape + oshape + [_sds((8, 128), F32), _sds((8, 128), F32)],
               scratch_shapes=[pltpu.VMEM((S_PAIRS, 2 * S_P, S_N), F32), pltpu.VMEM((S_PAIRS, 2 * S_P, S_N), F32)],
               compiler_params=_cp(("arbitrary",), 40 << 20))(act, act, act, P, act, act, act, P, dtb, alog, hsf, hsb, dy, dy)


def _ssm_out(yf, yb, xs, z, dskip, g):
    y = (yf + yb + dskip * xs) * _silu(z)
    return (y * lax.rsqrt(jnp.mean(y * y, axis=-1, keepdims=True) + EPS)) * g


def ssm_out_fwd(yf, yb, act, P, dskip, g, name):
    T = yf.shape[0]

    def body(yf_r, yb_r, xs_r, z_r, d_r, g_r, o_r):
        o_r[...] = _ssm_out(yf_r[...], yb_r[...], xs_r[...], z_r[...], d_r[...], g_r[...]).astype(o_r.dtype)

    row = pl.BlockSpec((TR, 512), lambda i: (i, 0))
    vec = pl.BlockSpec((1, 512), lambda i: (0, 0))
    return _pc(body, name=name, grid=(T // TR,),
               in_specs=[row, row, row, pl.BlockSpec((TR, 512), lambda i: (i, C_Z // 512)), vec, vec],
               out_specs=row, out_shape=_sds((T, 512), BF16),
               compiler_params=_cp(("parallel",), 16 << 20))(yf, yb, act, P, dskip, g)


def ssm_out_bwd(yf, yb, act, P, dskip, g, do_src, name):
    T = yf.shape[0]

    def body(yf_r, yb_r, xs_r, z_r, d_r, g_r, do_r, dy_r, dxs_r, dz_r, dv_r):
        @pl.when(pl.program_id(0) == 0)
        def _():
            dv_r[...] = jnp.zeros_like(dv_r)

        _, vjp = jax.vjp(_ssm_out, yf_r[...], yb_r[...], xs_r[...], z_r[...], d_r[...], g_r[...])
        dyf, _, dxs, dz, dd, dg = vjp(do_r[...].astype(F32))
        dy_r[...] = dyf
        dxs_r[...] = dxs
        dz_r[...] = dz.astype(dz_r.dtype)
        dv_r[0:1, :] += dd
        dv_r[1:2, :] += dg

    row = pl.BlockSpec((TR, 512), lambda i: (i, 0))
    vec = pl.BlockSpec((1, 512), lambda i: (0, 0))
    return _pc(body, name=name, grid=(T // TR,),
               in_specs=[row, row, row, pl.BlockSpec((TR, 512), lambda i: (i, C_Z // 512)), vec, vec,
                         pl.BlockSpec((TR, 512), lambda i: (i, 1))],
               out_specs=[row, row, row, pl.BlockSpec((8, 512), lambda i: (0, 0))],
               out_shape=[_sds((T, 512), F32), _sds((T, 512), F32), _sds((T, 512), BF16), _sds((8, 512), F32)],
               compiler_params=_cp(("arbitrary",), 24 << 20))(yf, yb, act, P, dskip, g, do_src)


def add_halves(xv, got, cvec, name):
    n, r, cdim = xv.shape
    h = r // 2

    def body(c_ref, x_ref, g_ref, o_ref):
        o_ref[...] = (x_ref[...].astype(F32) + g_ref[...].astype(F32)).astype(o_ref.dtype)

    gs = pltpu.PrefetchScalarGridSpec(
        num_scalar_prefetch=1, grid=(n,),
        in_specs=[pl.BlockSpec((1, h, cdim), lambda k, c_ref: (k, c_ref[0], 0)), pl.BlockSpec((1, h, cdim), lambda k, c_ref: (k, 0, 0))],
        out_specs=pl.BlockSpec((1, h, cdim), lambda k, c_ref: (k, 0, 0)))
    return _pc(body, name=name, grid_spec=gs, out_shape=_sds((n, h, cdim), BF16),
               compiler_params=_cp(("arbitrary",), 24 << 20))(cvec, xv, got)


def sum_slots(a, name):
    n, r, cdim = a.shape
    tr = _div_tile(r, 512, 16)

    def body(a_ref, o_ref):
        acc = a_ref[0].astype(F32)
        for k in range(1, n):
            acc = acc + a_ref[k].astype(F32)
        o_ref[...] = acc

    return _pc(body, name=name, grid=(r // tr,), in_specs=[pl.BlockSpec((n, tr, cdim), lambda i: (0, i, 0))],
               out_specs=pl.BlockSpec((tr, cdim), lambda i: (i, 0)), out_shape=_sds((r, cdim), F32),
               compiler_params=_cp(("parallel",), 32 << 20))(a)


def adamw(w, g, m, v, name):
    B, R, C = w.shape
    tr = _div_tile(R, max(8, (1 << 19) // max(C, 1) // 8 * 8), 8) if R % 8 == 0 else R
    c1 = 1.0 / (1.0 - ADAM_B1 ** ADAM_STEP)
    c2 = 1.0 / (1.0 - ADAM_B2 ** ADAM_STEP)

    def body(w_ref, g_ref, m_ref, v_ref, d_ref, mo_ref, vo_ref):
        gg = g_ref[...]
        mn = ADAM_B1 * m_ref[...] + (1.0 - ADAM_B1) * gg
        vn = ADAM_B2 * v_ref[...] + (1.0 - ADAM_B2) * (gg * gg)
        d_ref[...] = -ADAM_LR * ((mn * c1) / (jnp.sqrt(vn * c2) + ADAM_EPS) + ADAM_WD * w_ref[...])
        mo_ref[...] = mn
        vo_ref[...] = vn

    spec = pl.BlockSpec((1, tr, C), lambda b, i: (b, i, 0))
    return _pc(body, name=name, grid=(B, R // tr), in_specs=[spec] * 4, out_specs=[spec] * 3,
               out_shape=[_sds((B, R, C), F32)] * 3, compiler_params=_cp(("parallel", "parallel"), 32 << 20))(w, g, m, v)


def _me():
    return lax.axis_index("x"), lax.axis_index("y"), lax.axis_index("c")


def _flip(v, bit):
    return 1 - v if bit else v


def allgather8(xv, name):
    R = xv.shape[0]

    def body(x_ref, out_ref, sum_ref, send_sems, recv_sems):
        mx, my, mc = _me()
        me = 4 * mx + 2 * my + mc
        out_ref[me] = x_ref[...]
        sends, recvs = [], []
        for k in range(1, 8):
            px, py, pc = _flip(mx, k & 4), _flip(my, k & 2), _flip(mc, k & 1)
            peer = 4 * px + 2 * py + pc
            sends.append(pltpu.make_async_remote_copy(src_ref=x_ref, dst_ref=out_ref.at[me], send_sem=send_sems.at[k - 1],
                                                      recv_sem=recv_sems.at[k - 1], device_id=(px, py, pc), device_id_type=MESH))
            recvs.append(pltpu.make_async_remote_copy(src_ref=x_ref, dst_ref=out_ref.at[peer], send_sem=send_sems.at[k - 1],
                                                      recv_sem=recv_sems.at[k - 1], device_id=(px, py, pc), device_id_type=MESH))
        for cp in sends:
            cp.start()
        for cp in recvs:
            cp.wait_recv()
        for cp in sends:
            cp.wait_send()
        acc = out_ref[0]
        for d in range(1, 8):
            acc = acc + out_ref[d]
        sum_ref[...] = acc

    vm = pl.BlockSpec(memory_space=pltpu.VMEM)
    return _pc(body, name=name, pin=False, in_specs=[vm], out_specs=[vm, vm], out_shape=[_sds((8, R, 128), F32), _sds((R, 128), F32)],
               scratch_shapes=[pltpu.SemaphoreType.DMA((7,)), pltpu.SemaphoreType.DMA((7,))],
               compiler_params=_cp(None, 32 << 20))(xv)


def _other_chips(mx, my):
    return [(1 - mx, my), (mx, 1 - my), (1 - mx, 1 - my)]


def _halves(r, mc, mult):
    h = r // 2
    return pl.ds(pl.multiple_of(mc * h, mult), h), pl.ds(pl.multiple_of((1 - mc) * h, mult), h)


def _rcopy(src, dst, send_sems, recv_sems, k, to):
    return pltpu.make_async_remote_copy(src_ref=src, dst_ref=dst, send_sem=send_sems.at[k], recv_sem=recv_sems.at[k],
                                        device_id=to, device_id_type=MESH)


def _gather_body(xs, outs, send_sems, recv_sems):
    n = len(xs)
    mx, my, mc = _me()
    chip = 2 * mx + my
    sib = (mx, my, 1 - mc)
    chips = _other_chips(mx, my)
    idx = [2 * cx + cy for cx, cy in chips]
    cp = functools.partial(_rcopy, send_sems=send_sems, recv_sems=recv_sems)
    hv = [_halves(x.shape[0], mc, 16) for x in xs]
    first, passed = [], []
    for a in range(n):
        for j, (cx, cy) in enumerate(chips):
            first.append(cp(xs[a].at[hv[a][0]], outs[a].at[chip, hv[a][0]], k=6 * a + j, to=(cx, cy, mc)))
            first[-1].start()
    for a in range(n):
        for j in range(3):
            cp(xs[a].at[hv[a][0]], outs[a].at[idx[j], hv[a][0]], k=6 * a + j, to=sib).wait_recv()
            passed.append(cp(outs[a].at[idx[j], hv[a][0]], outs[a].at[idx[j], hv[a][0]], k=6 * a + 3 + j, to=sib))
            passed[-1].start()
    for a in range(n):
        for j in range(3):
            cp(xs[a].at[hv[a][1]], outs[a].at[idx[j], hv[a][1]], k=6 * a + 3 + j, to=sib).wait_recv()
    for c_ in first + passed:
        c_.wait_send()


def _my_chip():
    return 2 * lax.axis_index("x") + lax.axis_index("y")


def _own_slots(outs, shards):
    return [lax.dynamic_update_index_in_dim(o, x, _my_chip(), 0) for o, x in zip(outs, shards)]


def gather_weights(shards, name):
    n = len(shards)

    def body(*refs):
        _gather_body(refs[:n], refs[n:2 * n], *refs[2 * n:])

    hbm = pl.BlockSpec(memory_space=pl.ANY)
    outs = _pc(body, name=name, in_specs=[hbm] * n, out_specs=[hbm] * n, out_shape=[_sds((4,) + x.shape, x.dtype) for x in shards],
               scratch_shapes=[pltpu.SemaphoreType.DMA((6 * n,)), pltpu.SemaphoreType.DMA((6 * n,))])(*shards)
    return _own_slots(outs, shards)


GATHER_REST_ID = 3


def gather_weights_sc(shards, name):
    n = len(shards)
    x_refs = [jax.new_ref(x, memory_space=pltpu.MemorySpace.HBM) for x in shards]
    out_refs = [jax.empty_ref(_sds((4,) + x.shape, x.dtype), memory_space=pltpu.MemorySpace.HBM) for x in shards]

    @pl.kernel(mesh=plsc.ScalarSubcoreMesh(axis_name="sc", num_cores=1), name=name,
               scratch_types=(pltpu.SemaphoreType.DMA((6 * n,)), pltpu.SemaphoreType.DMA((6 * n,))),
               compiler_params=pltpu.CompilerParams(collective_id=GATHER_REST_ID))
    def launch(send_sems, recv_sems):
        mx, my, mc = _me()
        barrier = pltpu.get_barrier_semaphore()
        for peer in [(mx, my, 1 - mc)] + [(cx, cy, mc) for cx, cy in _other_chips(mx, my)]:
            pl.semaphore_signal(barrier, inc=1, device_id=peer, device_id_type=MESH)
        pl.semaphore_wait(barrier, 4)
        _gather_body(x_refs, out_refs, send_sems, recv_sems)

    launch()
    return _own_slots([o[...] for o in out_refs], shards)


def swap_halves(arrs, name):
    n = len(arrs)

    def body(*refs):
        xs, outs = refs[:n], refs[n:2 * n]
        send_sems, recv_sems = refs[2 * n:]
        mx, my, mc = _me()
        cps = []
        for a in range(n):
            theirs = _halves(xs[a].shape[1], mc, 16)[1]
            cps.append(_rcopy(xs[a].at[pl.ds(0, 4), theirs], outs[a], send_sems, recv_sems, a, (mx, my, 1 - mc)))
            cps[-1].start()
        for c_ in cps:
            c_.wait()

    hbm = pl.BlockSpec(memory_space=pl.ANY)
    return _pc(body, name=name, in_specs=[hbm] * n, out_specs=[hbm] * n,
               out_shape=[_sds((4, x.shape[1] // 2, x.shape[2]), x.dtype) for x in arrs],
               scratch_shapes=[pltpu.SemaphoreType.DMA((n,)), pltpu.SemaphoreType.DMA((n,))])(*arrs)


SCATTER_ID = 4


def scatter_chips_sc(arrs, name):
    n = len(arrs)
    x_refs = [jax.new_ref(x, memory_space=pltpu.MemorySpace.HBM) for x in arrs]
    out_refs = [jax.empty_ref(_sds(x.shape, x.dtype), memory_space=pltpu.MemorySpace.HBM) for x in arrs]

    @pl.kernel(mesh=plsc.ScalarSubcoreMesh(axis_name="sc", num_cores=1), name=name,
               scratch_types=(pltpu.SemaphoreType.DMA((3 * n,)), pltpu.SemaphoreType.DMA((3 * n,))),
               compiler_params=pltpu.CompilerParams(collective_id=SCATTER_ID))
    def launch(send_sems, recv_sems):
        mx, my, mc = _me()
        chip = 2 * mx + my
        chips = _other_chips(mx, my)
        idx = [2 * cx + cy for cx, cy in chips]
        barrier = pltpu.get_barrier_semaphore()
        for cx, cy in chips:
            pl.semaphore_signal(barrier, inc=1, device_id=(cx, cy, mc), device_id_type=MESH)
        pl.semaphore_wait(barrier, 3)
        cp = functools.partial(_rcopy, send_sems=send_sems, recv_sems=recv_sems)
        sends = []
        for a in range(n):
            for j, (cx, cy) in enumerate(chips):
                sends.append(cp(x_refs[a].at[idx[j]], out_refs[a].at[chip], k=3 * a + j, to=(cx, cy, mc)))
                sends[-1].start()
        for a in range(n):
            for j, (cx, cy) in enumerate(chips):
                cp(x_refs[a].at[idx[j]], out_refs[a].at[idx[j]], k=3 * a + j, to=(cx, cy, mc)).wait_recv()
        for c_ in sends:
            c_.wait_send()

    launch()
    return _own_slots([o[...] for o in out_refs], [lax.dynamic_index_in_dim(x, _my_chip(), 0, keepdims=False) for x in arrs])


def share_halves(parts, name):
    flat = [p for w in parts for p in w]
    nw, n = len(parts), len(flat)
    depth = n // nw

    def body(*refs):
        xs, outs = refs[:n], refs[n:n + nw]
        send_sems, recv_sems = refs[n + nw:]
        mx, my, mc = _me()
        sib = (mx, my, 1 - mc)
        sends, recvs = [], []
        for a in range(n):
            w, l = a // depth, a % depth
            mine, theirs = _halves(outs[w].shape[1], mc, 8)
            sends.append(_rcopy(xs[a], outs[w].at[l, mine], send_sems, recv_sems, a, sib))
            recvs.append(_rcopy(xs[a], outs[w].at[l, theirs], send_sems, recv_sems, a, sib))
            sends[-1].start()
        for c_ in recvs:
            c_.wait_recv()
        for c_ in sends:
            c_.wait_send()

    hbm = pl.BlockSpec(memory_space=pl.ANY)
    outs = _pc(body, name=name, in_specs=[hbm] * n, out_specs=[hbm] * nw,
               out_shape=[_sds((depth, 2 * w[0].shape[0], w[0].shape[1]), F32) for w in parts],
               scratch_shapes=[pltpu.SemaphoreType.DMA((n,)), pltpu.SemaphoreType.DMA((n,))])(*flat)
    outs = list(outs)
    mc = lax.axis_index("c")
    for w in range(nw):
        for l in range(depth):
            h = parts[w][l].shape[0]
            outs[w] = lax.dynamic_update_slice(outs[w], parts[w][l][None], (l, mc * h, 0))
    return outs


_BIG = ("w_in", "w_out", "w_ffn_in", "w_ffn_out")
N_CHIPS = 4
DEPTH = 2


def _pad_rows(v, mult=8):
    n = v.shape[0]
    rows = -(-n // 128)
    rows = -(-rows // mult) * mult
    return jnp.pad(v, (0, rows * 128 - n)).reshape(rows, 128)


class _Flat:
    def __init__(self):
        self.items = []

    def add(self, name, a):
        self.items.append((name, a.shape, a.reshape(-1).astype(F32)))

    def rows(self):
        return _pad_rows(jnp.concatenate([a for _, _, a in self.items]))

    def split(self, rows):
        flat = rows.reshape(-1)
        out, o = {}, 0
        for name, shape, a in self.items:
            out[name] = flat[o:o + a.shape[0]].reshape(shape)
            o += a.shape[0]
        return out

    def split_lead(self, rows3):
        n = rows3.shape[0]
        flat = rows3.reshape(n, -1)
        out, o = {}, 0
        for name, shape, a in self.items:
            out[name] = flat[:, o:o + a.shape[0]].reshape((n,) + tuple(shape))
            o += a.shape[0]
        return out


def _gsv(rows):
    z = jnp.zeros((2, D), F32)
    r = [z if a is None else a for a in rows] + [z] * 5
    return jnp.stack(r, axis=1)


def _pad8(a, rows=8, cols=128):
    return jnp.zeros((rows, cols), F32).at[:a.shape[0], :a.shape[1]].set(a.astype(F32))


def kernel(x, c, ctx, c_ctx, w_mod, b_mod, g_mix, w_in, wa_sink, na_rpb, ssm_conv_w, ssm_conv_b, ssm_dt_bias, ssm_a_log, ssm_d, ssm_norm_g, w_out, g_ffn, w_ffn_in, w_ffn_out, g_final, loss_target, m_c_ctx, m_w_mod, m_b_mod, m_g_mix, m_w_in, m_wa_sink, m_na_rpb, m_ssm_conv_w, m_ssm_conv_b, m_ssm_dt_bias, m_ssm_a_log, m_ssm_d, m_ssm_norm_g, m_w_out, m_g_ffn, m_w_ffn_in, m_w_ffn_out, m_g_final, v_c_ctx, v_w_mod, v_b_mod, v_g_mix, v_w_in, v_wa_sink, v_na_rpb, v_ssm_conv_w, v_ssm_conv_b, v_ssm_dt_bias, v_ssm_a_log, v_ssm_d, v_ssm_norm_g, v_w_out, v_g_ffn, v_w_ffn_in, v_w_ffn_out, v_g_final):
    L, Lc = x.shape[1], ctx.shape[1]
    T = L + Lc
    nL = L // TR
    mx, my, mc = lax.axis_index("x"), lax.axis_index("y"), lax.axis_index("c")
    dev = 4 * mx + 2 * my + mc
    chip = 2 * mx + my
    MODW = 6 * D // N_CHIPS
    CW = 1024 // N_CHIPS

    sc = _silu(c.astype(F32))
    scc = _silu(c_ctx.astype(F32))[None]
    f1 = _Flat()
    f1.add("sc", sc)
    f1.add("conv_w", ssm_conv_w)
    g1, _ = allgather8(f1.rows(), "gather_cond")
    g1 = f1.split_lead(g1)
    sc_all = g1["sc"][:, 0]
    conv_w = jnp.concatenate([g1["conv_w"][2 * k] for k in range(N_CHIPS)], axis=-1)
    A16 = jnp.concatenate([sc_all, scc, jnp.zeros((7, D), F32)], axis=0)

    mod_part = matmul_layers(A16, w_mod, "nn", "mod_fwd")
    f2 = _Flat()
    f2.add("mod", mod_part)
    g2, _ = allgather8(f2.rows(), "gather_mod")
    g2 = f2.split_lead(g2)["mod"]
    mods = jnp.concatenate([g2[2 * k] for k in range(N_CHIPS)], axis=-1) + b_mod[:, None, :]
    mod_l = lax.dynamic_index_in_dim(mods, dev, axis=1, keepdims=False).reshape(DEPTH, 6, D)
    mod_c = mods[:, 8].reshape(DEPTH, 6, D)
    mod = jnp.stack([mod_l, mod_c], axis=1)
    mrow = lambda l, j: mod[l, :, j]

    own = {"w_in": w_in, "w_out": w_out, "w_ffn_in": w_ffn_in, "w_ffn_out": w_ffn_out}
    sh16 = [own[n][l].astype(BF16) for n in _BIG for l in range(DEPTH)]
    after_mod = (g2[0, 0, 0, 0] * 0).astype(BF16)
    gath = list(gather_weights([sh16[0] + after_mod], "gather_first"))
    after_first = (gath[0][0, 0, 0] * 0).astype(BF16)
    gath += list(gather_weights_sc([sh16[1] + after_first] + sh16[2:], "gather_rest"))
    gw = {n: [gath[DEPTH * i + l] for l in range(DEPTH)] for i, n in enumerate(_BIG)}
    W_in = [jnp.pad(jnp.concatenate([g[k] for k in range(N_CHIPS)], axis=1), ((0, 0), (0, IN_PAD - IN_COLS))) for g in gw["w_in"]]
    W_out = [g.reshape(D, D) for g in gw["w_out"]]
    W_fo = [g.reshape(D_FF, D) for g in gw["w_ffn_out"]]
    W_fi = gw["w_ffn_in"]

    cos, sin, rotm = rope_tables(L, Lc)
    x0 = jnp.concatenate([x[0], ctx[0]], axis=0).astype(F32)

    sv = []
    xin = x0
    gsv_first = _gsv([None, mrow(0, 0), mrow(0, 1)])
    _, h1 = res_norm_mod(x0, None, gsv_first, g_mix[0][None], nL, "norm_first")
    for l in range(DEPTH):
        s = {"xin": xin, "h1": h1}
        P = matmul(h1, W_in[l], "nn", F32, f"in_proj{l}", tn=IN_PAD)
        qr, kr, kb, vb = rope_apply(P, C_QA // 256, P, C_KA // 128, cos, sin, rotm, False, f"rope{l}", kv_src=P)
        sink8 = _pad8(jnp.broadcast_to(wa_sink[l][:, None], (WA_HEADS, 128)))
        krs, va = _swap_halves_lanes(kr), P[:, C_VA:C_VA + 128]
        vas = _swap_halves_lanes(va)
        oa, sta = win_attn_fwd(qr, kr, krs, va, vas, sink8, L, Lc, f"wa_fwd{l}")
        bias = na_bias_table(na_rpb[l], l)
        ob, stb = na_fwd(P, kb, vb, bias, L, Lc, f"na_fwd{l}")
        w8 = jnp.concatenate([conv_w[l], jnp.zeros((1, 1024), F32)], axis=0)
        pre, act = conv_silu_fwd(P, w8, ssm_conv_b[l][None], nL, f"conv_fwd{l}")
        dtb8, al8 = _pad8(ssm_dt_bias[l]), _pad8(ssm_a_log[l])
        yf, yb, hsf, hsb = ssd_fwd(act, P, dtb8, al8, L, Lc, f"ssd_fwd{l}")
        dskip = jnp.repeat(ssm_d[l], S_P)[None]
        oc = ssm_out_fwd(yf, yb, act, P, dskip, ssm_norm_g[l][None], f"ssm_out_fwd{l}")
        mixin = [(oa, 0), (ob, 256), (oc, 512)]
        mix = out_proj_fwd(mixin, W_out[l], f"out_proj{l}")
        gsv_mid = _gsv([mrow(l, 2), mrow(l, 3), mrow(l, 4)])
        x1, h2 = res_norm_mod(xin, mix, gsv_mid, g_ffn[l][None], nL, f"norm_mid{l}")
        gu = matmul_fi(h2, W_fi[l], "nn", BF16, f"ffn_in{l}")
        af = swiglu_fwd(gu, f"swiglu_fwd{l}")
        fo = matmul(af, W_fo[l], "nn", BF16, f"ffn_out{l}", tk=D_FF)
        s.update(P=P, qr=qr, kr=kr, krs=krs, va=va, vas=vas, sink8=sink8, oa=oa, sta=sta, ob=ob, stb=stb, kb=kb, vb=vb, bias=bias, w8=w8, pre=pre, act=act, dtb8=dtb8, al8=al8, yf=yf,
                 yb=yb, hsf=hsf, hsb=hsb, dskip=dskip, mixin=mixin, mix=mix, gsv_mid=gsv_mid, x1=x1, h2=h2, gu=gu, af=af, fo=fo)
        if l + 1 < DEPTH:
            s["gsv_end"] = _gsv([mrow(l, 5), mrow(l + 1, 0), mrow(l + 1, 1)])
            xin, h1 = res_norm_mod(x1, fo, s["gsv_end"], g_mix[l + 1][None], nL, f"norm_end{l}")
        else:
            s["gsv_end"] = _gsv([mrow(l, 5), None, None])
        sv.append(s)

    last = sv[-1]
    loss8, dres, dfo, dgsv_end, dg_final = final_loss(last["x1"], last["fo"], last["gsv_end"], g_final[None], loss_target[0].astype(F32), nL, "final_loss")
    loss = lax.psum(loss8[0, 0], ("x", "y", "c"))

    dmod = [[None] * 6 for _ in range(DEPTH)]
    gW = {n: [None] * DEPTH for n in _BIG}
    small = [dict() for _ in range(DEPTH)]
    parts = [None] * DEPTH
    cvec = mc.astype(jnp.int32).reshape(1)
    grad_x = None
    for l in reversed(range(DEPTH)):
        s = sv[l]
        dmod[l][5] = dgsv_end[:, 0]
        if l + 1 < DEPTH:
            dmod[l + 1][0], dmod[l + 1][1] = dgsv_end[:, 1], dgsv_end[:, 2]
        daf = matmul(dfo, W_fo[l], "nt", BF16, f"ffn_out_dx{l}")
        gW["w_ffn_out"][l] = matmul(s["af"], dfo, "tn", BF16, f"ffn_out_dw{l}", tm=1408, tk=T).reshape(N_CHIPS, D_FF // N_CHIPS, D)
        dgu = swiglu_bwd(s["gu"], daf, f"swiglu_bwd{l}")
        dh2 = matmul_fi(dgu, W_fi[l], "nt", BF16, f"ffn_in_dx{l}")
        gW["w_ffn_in"][l] = matmul_fi(s["h2"], dgu, "tn", BF16, f"ffn_in_dw{l}")
        dres, dmix, dgsv_mid, dg_ffn = res_norm_mod_bwd(s["x1"], s["mix"], s["gsv_mid"], g_ffn[l][None], dh2, dres, nL, f"norm_mid_bwd{l}")
        dmod[l][2], dmod[l][3], dmod[l][4] = dgsv_mid[:, 0], dgsv_mid[:, 1], dgsv_mid[:, 2]
        dmixin = matmul(dmix, W_out[l], "nt", BF16, f"out_proj_dx{l}")
        gW["w_out"][l] = out_proj_dw(s["mixin"], dmix, f"out_proj_dw{l}").reshape(N_CHIPS, D // N_CHIPS, D)
        P = s["P"]
        dqr, dkr, dkrs, dva, dvas, dsink = win_attn_bwd(s["qr"], s["kr"], s["krs"], s["va"], s["vas"], s["sink8"], dmixin, s["oa"], s["sta"], L, Lc,
                                                        f"wa_bwd{l}")
        dkr, dva = dkr + _swap_halves_lanes(dkrs), dva + _swap_halves_lanes(dvas)
        dqa, dka = rope_apply(dqr, 0, dkr[WA_BLK:WA_BLK + T], 0, cos, sin, rotm, True, f"rope_bwd{l}")
        dqb, dkb, dvb, dbias = na_bwd(P, s["kb"], s["vb"], s["bias"], dmixin, s["ob"], s["stb"], L, Lc, f"na_bwd{l}")
        dy, dxs1, dz, dvec = ssm_out_bwd(s["yf"], s["yb"], s["act"], P, s["dskip"], ssm_norm_g[l][None], dmixin, f"ssm_out_bwd{l}")
        dxf, dbf, dcf, ddf, dxb, dbb, dcb, ddb, ddtb, dal = ssd_bwd(s["act"], P, s["dtb8"], s["al8"], s["hsf"], s["hsb"], dy, L, Lc, f"ssd_bwd{l}")
        dpre = dsilu(s["pre"], [dxf, dxb, dxs1], [dbf, dbb], [dcf, dcb], f"dsilu{l}")
        dxbc, dw8, db8 = conv_bwd(dpre, P, s["w8"], nL, f"conv_bwd{l}")
        ddt = jnp.concatenate([ddf, ddb, jnp.zeros((T, IN_PAD - IN_COLS), F32)], axis=1)
        pieces = [(dqa, C_QA), (dqb, C_QB), (dz, C_Z), (dka, C_KA), (dva[WA_BLK:WA_BLK + T], C_VA), (dkb, C_KB), (dvb, C_VB),
                  (dxbc, C_XBC), (ddt, C_DT)]
        dh1, dwin = in_proj_bwd(pieces, s["h1"], W_in[l], f"in_proj_bwd{l}")
        cw = IN_COLS // N_CHIPS
        gW["w_in"][l] = jnp.stack([dwin[:, k * cw:(k + 1) * cw] for k in range(N_CHIPS)])
        garr = [gW[n][l] for n in _BIG]
        got = swap_halves(garr, f"reduce_d2d{l}")
        chip_sum = [add_halves(garr[a], got[a], cvec, f"reduce_add_pair{l}_{a}") for a in range(len(garr))]
        parts[l] = scatter_chips_sc(chip_sum, f"reduce_ici{l}")
        small[l] = dict(g_ffn=dg_ffn[0], wa_sink=dsink[:WA_HEADS, 0], na_rpb=na_rpb_grad(dbias, l), conv_w=dw8[:S_CONV], conv_b=db8[0],
                        dt_bias=ddtb[:2, :8], a_log=dal[:2, :8], ssm_d=dvec[0].reshape(S_HEADS, S_P).sum(axis=1), norm_g=dvec[1])
        if l > 0:
            p = sv[l - 1]
            dres, dfo, dgsv_end, dg_mix = res_norm_mod_bwd(s["xin"], p["fo"], p["gsv_end"], g_mix[l][None], dh1, dres, nL, f"norm_end_bwd{l - 1}")
        else:
            grad_x, _, dgsv_first, dg_mix = res_norm_mod_bwd(s["xin"], None, gsv_first, g_mix[0][None], dh1, dres, nL, "norm_first_bwd")
            dmod[0][0], dmod[0][1] = dgsv_first[:, 1], dgsv_first[:, 2]
        small[l]["g_mix"] = dg_mix[0]
    for l in range(DEPTH):
        for j in range(6):
            if dmod[l][j] is None:
                dmod[l][j] = jnp.zeros((2, D), F32)
    dmod = jnp.stack([jnp.stack(r, axis=1) for r in dmod])

    f3 = _Flat()
    f3.add("dmod_l", dmod[:, 0].reshape(DEPTH, 6 * D))
    f3.add("dmod_c", dmod[:, 1].reshape(DEPTH, 6 * D))
    f3.add("g_final", dg_final[0])
    for n in ("g_mix", "g_ffn", "wa_sink", "na_rpb", "conv_w", "conv_b", "dt_bias", "a_log", "ssm_d", "norm_g"):
        f3.add(n, jnp.stack([small[l][n] for l in range(DEPTH)]))
    g3, s3 = allgather8(f3.rows(), "reduce_small")
    dmod_all = f3.split_lead(g3)["dmod_l"]
    s3 = f3.split(s3)
    dmodc_tot = s3["dmod_c"]
    col0 = chip * MODW
    G16, G16c = [], []
    for l in range(DEPTH):
        rows = jnp.concatenate([dmod_all[:, l], dmodc_tot[l][None], jnp.zeros((7, 6 * D), F32)], axis=0)
        G16.append(lax.dynamic_slice_in_dim(rows, col0, MODW, axis=1))
        rc = jnp.concatenate([dmodc_tot[l][None], jnp.zeros((15, 6 * D), F32)], axis=0)
        G16c.append(lax.dynamic_slice_in_dim(rc, col0, MODW, axis=1))
    grad_w_mod = matmul_layers(A16, jnp.stack(G16), "tn", "mod_dw")
    dscc_part = matmul_layers(jnp.stack(G16c), w_mod, "nt", "mod_dx")[:, 0].sum(axis=0)
    _, s4 = allgather8(_pad_rows(dscc_part * (mc == 1).astype(F32)), "reduce_cctx")
    dscc = s4.reshape(-1)[:D]
    cc = c_ctx.astype(F32)
    sg = 1.0 / (1.0 + jnp.exp(-cc))
    grad_c_ctx = dscc * (sg * (1.0 + cc * (1.0 - sg)))

    halves = [[sum_slots(parts[l][i], f"reduce_add_chips{l}_{i}") for l in range(DEPTH)] for i in range(len(_BIG))]
    gsh = dict(zip(_BIG, share_halves(halves, "reduce_share")))

    grads = {"c_ctx": grad_c_ctx, "w_mod": grad_w_mod, "b_mod": s3["dmod_l"] + s3["dmod_c"], "g_mix": s3["g_mix"], "w_in": gsh["w_in"],
             "wa_sink": s3["wa_sink"], "na_rpb": s3["na_rpb"],
             "ssm_conv_w": lax.dynamic_slice_in_dim(s3["conv_w"], chip * CW, CW, axis=2), "ssm_conv_b": s3["conv_b"],
             "ssm_dt_bias": s3["dt_bias"], "ssm_a_log": s3["a_log"], "ssm_d": s3["ssm_d"], "ssm_norm_g": s3["norm_g"],
             "w_out": gsh["w_out"], "g_ffn": s3["g_ffn"], "w_ffn_in": gsh["w_ffn_in"], "w_ffn_out": gsh["w_ffn_out"], "g_final": s3["g_final"]}
    wts = {"c_ctx": c_ctx, "w_mod": w_mod, "b_mod": b_mod, "g_mix": g_mix, "w_in": w_in, "wa_sink": wa_sink, "na_rpb": na_rpb,
           "ssm_conv_w": ssm_conv_w, "ssm_conv_b": ssm_conv_b, "ssm_dt_bias": ssm_dt_bias, "ssm_a_log": ssm_a_log, "ssm_d": ssm_d,
           "ssm_norm_g": ssm_norm_g, "w_out": w_out, "g_ffn": g_ffn, "w_ffn_in": w_ffn_in, "w_ffn_out": w_ffn_out, "g_final": g_final}
    ms = {"c_ctx": m_c_ctx, "w_mod": m_w_mod, "b_mod": m_b_mod, "g_mix": m_g_mix, "w_in": m_w_in, "wa_sink": m_wa_sink, "na_rpb": m_na_rpb,
          "ssm_conv_w": m_ssm_conv_w, "ssm_conv_b": m_ssm_conv_b, "ssm_dt_bias": m_ssm_dt_bias, "ssm_a_log": m_ssm_a_log, "ssm_d": m_ssm_d,
          "ssm_norm_g": m_ssm_norm_g, "w_out": m_w_out, "g_ffn": m_g_ffn, "w_ffn_in": m_w_ffn_in, "w_ffn_out": m_w_ffn_out, "g_final": m_g_final}
    vs = {"c_ctx": v_c_ctx, "w_mod": v_w_mod, "b_mod": v_b_mod, "g_mix": v_g_mix, "w_in": v_w_in, "wa_sink": v_wa_sink, "na_rpb": v_na_rpb,
          "ssm_conv_w": v_ssm_conv_w, "ssm_conv_b": v_ssm_conv_b, "ssm_dt_bias": v_ssm_dt_bias, "ssm_a_log": v_ssm_a_log, "ssm_d": v_ssm_d,
          "ssm_norm_g": v_ssm_norm_g, "w_out": v_w_out, "g_ffn": v_g_ffn, "w_ffn_in": v_w_ffn_in, "w_ffn_out": v_w_ffn_out, "g_final": v_g_final}
    names = list(wts)
    grads = {n: grads[n].reshape(wts[n].shape).astype(F32) for n in names}
    big = ("w_mod", "w_in", "w_out", "w_ffn_in", "w_ffn_out")
    delta, new_m, new_v = {}, {}, {}
    for n in big:
        delta[n], new_m[n], new_v[n] = adamw(wts[n], grads[n], ms[n], vs[n], f"adamw_{n}")
    packs = []
    for src in (wts, grads, ms, vs):
        f = _Flat()
        for n in names:
            if n not in big:
                f.add(n, src[n])
        packs.append(f)
    d_, m_, v_ = adamw(*[f.rows()[None] for f in packs], "adamw_small")
    for dst, rows in ((delta, d_), (new_m, m_), (new_v, v_)):
        dst.update(packs[0].split(rows[0]))

    return (loss, grad_x[:L][None], *[grads[n] for n in names], *[delta[n] for n in names],
            *[new_m[n] for n in names], *[new_v[n] for n in names])
```

```python
import functools

import numpy as np
import jax
import jax.numpy as jnp
from jax import lax
from jax.experimental import pallas as pl
from jax.experimental.pallas import tpu as pltpu
from jax.experimental.pallas import tpu_sc as plsc

F32 = jnp.float32
BF16 = jnp.bfloat16
_MXU = jnp.bfloat16
_HI = lax.Precision.HIGHEST
MESH = pl.DeviceIdType.MESH

D = 1024
HD = 64
GRID_W = 64
EPS = 1e-6
ROPE_BASE = 10000.0
WA_HEADS, WA_KV = 4, 2
WA_BLK = 128
NA_HEADS, NA_KH, NA_KW = 4, 8, 16
S_HEADS, S_P, S_INNER, S_GROUPS, S_N, S_CONV, S_Q = 8, 64, 512, 2, 128, 7, 128
D_FF = 2816
IN_COLS = 2832
IN_PAD = 2944
C_QA, C_QB, C_Z, C_KA, C_VA, C_KB, C_VB, C_XBC, C_DT = 0, 256, 512, 1024, 1152, 1280, 1536, 1792, 2816
ADAM_LR, ADAM_B1, ADAM_B2, ADAM_EPS, ADAM_WD, ADAM_STEP = 0.001, 0.9, 0.999, 1e-08, 0.01, 10

TR = 256
NEG = -1e30
VMEM_CAP = 56 * 1024 * 1024


PIN_BYTES = 256 * 1024


def _is_big(a):
    return hasattr(a, "shape") and len(a.shape) >= 2 and int(np.prod(a.shape)) * jnp.dtype(a.dtype).itemsize >= PIN_BYTES


def _pc(body, *, out_shape, pin=True, **kw):
    if not pin:
        return pl.pallas_call(body, out_shape=out_shape, **kw)
    one = isinstance(out_shape, jax.ShapeDtypeStruct)
    outs = [pltpu.HBM(s.shape, s.dtype) if _is_big(s) else s for s in ([out_shape] if one else out_shape)]
    call = pl.pallas_call(body, out_shape=outs[0] if one else outs, **kw)
    return lambda *args: call(*[pltpu.with_memory_space_constraint(a, pltpu.HBM) if _is_big(a) else a for a in args])


def _cp(sem=None, vmem=None):
    kw = {}
    if sem is not None:
        kw["dimension_semantics"] = sem
    if vmem is not None:
        kw["vmem_limit_bytes"] = int(min(max(vmem, 16 * 1024 * 1024), VMEM_CAP))
    return pltpu.CompilerParams(**kw)


def _sds(shape, dtype):
    return jax.ShapeDtypeStruct(tuple(shape), dtype)


_DIMS = {"nn": ((1,), (0,)), "nt": ((1,), (1,)), "tn": ((0,), (0,))}


def _dg(a, b, dims):
    return lax.dot_general(a.astype(_MXU), b.astype(_MXU), (dims, ((), ())), preferred_element_type=F32)


@functools.partial(jax.custom_vjp, nondiff_argnums=(2,))
def bdot(a, b, mode):
    return _dg(a, b, _DIMS[mode])


def _bdot_fwd(a, b, mode):
    return bdot(a, b, mode), (a, b)


def _bdot_bwd(mode, res, g):
    a, b = res
    if mode == "nn":
        return bdot(g, b, "nt"), bdot(a, g, "tn")
    if mode == "nt":
        return bdot(g, b, "nn"), bdot(g, a, "tn")
    return bdot(b, g, "nt"), bdot(a, g, "nn")


bdot.defvjp(_bdot_fwd, _bdot_bwd)


def hdot(a, b, mode="nn"):
    return lax.dot_general(a, b, (_DIMS[mode], ((), ())), precision=_HI, preferred_element_type=F32)


def _silu(x):
    return x / (1.0 + jnp.exp(-x))


def _softplus(x):
    return jnp.maximum(x, 0.0) + jnp.log(1.0 + jnp.exp(-jnp.abs(x)))


def _div_tile(n, cap, mult):
    if n <= cap:
        return n
    best = None
    for t in range(mult, cap + 1, mult):
        if n % t == 0:
            best = t
    assert best is not None, (n, cap, mult)
    return best


def matmul(a, b, mode, out_dtype, name, tm=640, tn=1536, tk=1408, hi=False):
    if mode == "tn":
        K, M = a.shape
    else:
        M, K = a.shape
    N = b.shape[0] if mode == "nt" else b.shape[1]
    tm = _div_tile(M, tm, 128 if mode == "tn" else 16)
    tn = _div_tile(N, tn, 128)
    tk = _div_tile(K, tk, 128 if mode != "tn" else 16)
    nk = K // tk
    dims = _DIMS[mode]

    def body(a_ref, b_ref, o_ref, *acc):
        if hi:
            part = lax.dot_general(a_ref[...], b_ref[...], (dims, ((), ())), precision=_HI, preferred_element_type=F32)
        else:
            part = _dg(a_ref[...], b_ref[...], dims)
        if nk == 1:
            o_ref[...] = part.astype(o_ref.dtype)
        else:
            k = pl.program_id(2)

            @pl.when(k == 0)
            def _():
                acc[0][...] = part

            @pl.when(k > 0)
            def _():
                acc[0][...] += part

            @pl.when(k == nk - 1)
            def _():
                o_ref[...] = acc[0][...].astype(o_ref.dtype)

    if mode == "tn":
        a_spec = pl.BlockSpec((tk, tm), lambda i, j, k: (k, i))
    else:
        a_spec = pl.BlockSpec((tm, tk), lambda i, j, k: (i, k))
    if mode == "nt":
        b_spec = pl.BlockSpec((tn, tk), lambda i, j, k: (j, k))
    else:
        b_spec = pl.BlockSpec((tk, tn), lambda i, j, k: (k, j))
    isz = lambda x: jnp.dtype(x.dtype).itemsize
    vmem = 2 * (tm * tk * isz(a) + tk * tn * isz(b) + tm * tn * jnp.dtype(out_dtype).itemsize) + 3 * tm * tn * 4
    return _pc(
        body, name=name, grid=(M // tm, N // tn, nk),
        in_specs=[a_spec, b_spec], out_specs=pl.BlockSpec((tm, tn), lambda i, j, k: (i, j)),
        out_shape=_sds((M, N), out_dtype),
        scratch_shapes=[pltpu.VMEM((tm, tn), F32)] if nk > 1 else [],
        compiler_params=_cp(("parallel", "parallel", "arbitrary"), vmem + (8 << 20)),
    )(a, b)


def matmul_layers(a, b, mode, name):
    nl = b.shape[0]
    a3 = a if a.ndim == 3 else a[None]
    shared = a3.shape[0] == 1
    M = a3.shape[2] if mode == "tn" else a3.shape[1]
    N = b.shape[1] if mode == "nt" else b.shape[2]

    def body(a_ref, b_ref, o_ref):
        o_ref[0] = _dg(a_ref[0], b_ref[0], _DIMS[mode])

    return _pc(body, name=name, grid=(nl,),
               in_specs=[pl.BlockSpec((1,) + a3.shape[1:], (lambda l: (0, 0, 0)) if shared else (lambda l: (l, 0, 0))),
                         pl.BlockSpec((1,) + b.shape[1:], lambda l: (l, 0, 0))],
               out_specs=pl.BlockSpec((1, M, N), lambda l: (l, 0, 0)), out_shape=_sds((nl, M, N), F32),
               compiler_params=_cp(("parallel",), 48 << 20))(a3, b)


def out_proj_fwd(pieces, w, name):
    T = pieces[0][0].shape[0]
    arrs, offs = [a for a, _ in pieces], [o for _, o in pieces]
    n = len(arrs)
    tm = _div_tile(T, 640, 16)

    def body(*refs):
        w_ref, o_ref = refs[n], refs[n + 1]
        acc = None
        for j in range(n):
            part = _dg(refs[j][...], w_ref[offs[j]:offs[j] + arrs[j].shape[1], :], _DIMS["nn"])
            acc = part if acc is None else acc + part
        o_ref[...] = acc.astype(o_ref.dtype)

    return _pc(body, name=name, grid=(T // tm,),
               in_specs=[pl.BlockSpec((tm, a.shape[1]), lambda i: (i, 0)) for a in arrs] + [pl.BlockSpec(w.shape, lambda i: (0, 0))],
               out_specs=pl.BlockSpec((tm, w.shape[1]), lambda i: (i, 0)), out_shape=_sds((T, w.shape[1]), BF16),
               compiler_params=_cp(("parallel",), 32 << 20))(*arrs, w)


def out_proj_dw(pieces, dy, name):
    T, N = dy.shape
    arrs, offs = [a for a, _ in pieces], [o for _, o in pieces]
    n = len(arrs)
    rows = sum(a.shape[1] for a in arrs)
    tn = 512

    def body(*refs):
        d_ref, o_ref = refs[n], refs[n + 1]
        for j in range(n):
            o_ref[offs[j]:offs[j] + arrs[j].shape[1], :] = _dg(refs[j][...], d_ref[...], _DIMS["tn"]).astype(o_ref.dtype)

    return _pc(body, name=name, grid=(N // tn,),
               in_specs=[pl.BlockSpec(a.shape, lambda j: (0, 0)) for a in arrs] + [pl.BlockSpec((T, tn), lambda j: (0, j))],
               out_specs=pl.BlockSpec((rows, tn), lambda j: (0, j)), out_shape=_sds((rows, N), BF16),
               compiler_params=_cp(("parallel",), 48 << 20))(*arrs, dy)


def in_proj_bwd(pieces, h1, w, name):
    T = h1.shape[0]
    arrs = [a for a, _ in pieces]
    offs = [o for _, o in pieces]
    wid = [a.shape[1] for a in arrs]
    n = len(arrs)
    assert sum(wid) == IN_PAD, "the pieces must tile all columns of P"
    tm = _div_tile(T, 640, 16)

    def dx_body(*refs):
        w_ref, o_ref = refs[n], refs[n + 1]
        acc = None
        for j in range(n):
            part = _dg(refs[j][...], w_ref[:, offs[j]:offs[j] + wid[j]], _DIMS["nt"])
            acc = part if acc is None else acc + part
        o_ref[...] = acc.astype(o_ref.dtype)

    dh1 = _pc(dx_body, name=name + "_dx", grid=(T // tm,),
              in_specs=[pl.BlockSpec((tm, wj), lambda i: (i, 0)) for wj in wid] + [pl.BlockSpec((D, IN_PAD), lambda i: (0, 0))],
              out_specs=pl.BlockSpec((tm, D), lambda i: (i, 0)), out_shape=_sds((T, D), BF16),
              compiler_params=_cp(("parallel",), 40 << 20))(*arrs, w)

    tmd, nk = 512, 4
    tk = T // nk

    def dw_body(h_ref, *refs):
        o_ref, acc = refs[n], refs[n + 1]
        k = pl.program_id(1)

        @pl.when(k == 0)
        def _():
            acc[...] = jnp.zeros_like(acc)

        for j in range(n):
            acc[:, offs[j]:offs[j] + wid[j]] += _dg(h_ref[...], refs[j][...], _DIMS["tn"])

        @pl.when(k == nk - 1)
        def _():
            o_ref[...] = acc[...].astype(o_ref.dtype)

    dw = _pc(dw_body, name=name + "_dw", grid=(D // tmd, nk),
             in_specs=[pl.BlockSpec((tk, tmd), lambda i, k: (k, i))] + [pl.BlockSpec((tk, wj), lambda i, k: (k, 0)) for wj in wid],
             out_specs=pl.BlockSpec((tmd, IN_PAD), lambda i, k: (i, 0)), out_shape=_sds((D, IN_PAD), BF16),
             scratch_shapes=[pltpu.VMEM((tmd, IN_PAD), F32)], compiler_params=_cp(("parallel", "arbitrary"), 48 << 20))(h1, *arrs)
    return dh1, dw


def _norm_mod(xo, shift, scale, g):
    r = lax.rsqrt(jnp.mean(xo * xo, axis=-1, keepdims=True) + EPS)
    return (xo * r) * g * (1.0 + scale) + shift


def res_norm_mod(x, y, gsv, g, nL, name):
    T = x.shape[0]
    has_y = y is not None

    def body(*refs):
        if has_y:
            x_ref, y_ref, gsv_ref, g_ref, xo_ref, h_ref = refs
            xo = x_ref[...] + gsv_ref[0, 0:1, :] * y_ref[...]
            xo_ref[...] = xo
        else:
            x_ref, gsv_ref, g_ref, h_ref = refs
            xo = x_ref[...]
        h_ref[...] = _norm_mod(xo, gsv_ref[0, 1:2, :], gsv_ref[0, 2:3, :], g_ref[...]).astype(h_ref.dtype)

    row = pl.BlockSpec((TR, D), lambda i: (i, 0))
    in_specs = [row] + ([row] if has_y else []) + [pl.BlockSpec((1, 8, D), lambda i: (i // nL, 0, 0)),
                                                     pl.BlockSpec((1, D), lambda i: (0, 0))]
    out_specs = ([row] if has_y else []) + [row]
    out_shape = ([_sds((T, D), F32)] if has_y else []) + [_sds((T, D), BF16)]
    args = (x, y, gsv, g) if has_y else (x, gsv, g)
    outs = _pc(body, name=name, grid=(T // TR,), in_specs=in_specs, out_specs=out_specs, out_shape=out_shape,
               compiler_params=_cp(("arbitrary",), 24 << 20))(*args)
    return (outs[0], outs[1]) if has_y else (None, outs[0])


def res_norm_mod_bwd(xo, y, gsv, g, dh, dres, nL, name):
    T = xo.shape[0]
    has_y = y is not None

    def body(*refs):
        if has_y:
            xo_ref, y_ref, gsv_ref, g_ref, dh_ref, dres_ref, dx_ref, dy_ref, dgsv_ref, dg_ref = refs
        else:
            xo_ref, gsv_ref, g_ref, dh_ref, dres_ref, dx_ref, dgsv_ref, dg_ref = refs
        i = pl.program_id(0)

        @pl.when((i == 0) | (i == nL))
        def _():
            dgsv_ref[...] = jnp.zeros_like(dgsv_ref)

        @pl.when(i == 0)
        def _():
            dg_ref[...] = jnp.zeros_like(dg_ref)

        _, vjp = jax.vjp(_norm_mod, xo_ref[...], gsv_ref[0, 1:2, :], gsv_ref[0, 2:3, :], g_ref[...])
        dxn, dshift, dscale, dg = vjp(dh_ref[...].astype(F32))
        dxo = dres_ref[...] + dxn
        dx_ref[...] = dxo
        if has_y:
            dy_ref[...] = (gsv_ref[0, 0:1, :] * dxo).astype(dy_ref.dtype)
            dgsv_ref[0, 0:1, :] += jnp.sum(y_ref[...] * dxo, axis=0, keepdims=True)
        dgsv_ref[0, 1:2, :] += dshift
        dgsv_ref[0, 2:3, :] += dscale
        dg_ref[0:1, :] += dg

    row = pl.BlockSpec((TR, D), lambda i: (i, 0))
    gspec = pl.BlockSpec((1, 8, D), lambda i: (i // nL, 0, 0))
    in_specs = [row] + ([row] if has_y else []) + [gspec, pl.BlockSpec((1, D), lambda i: (0, 0)), row, row]
    out_specs = [row] + ([row] if has_y else []) + [gspec, pl.BlockSpec((8, D), lambda i: (0, 0))]
    out_shape = [_sds((T, D), F32)] + ([_sds((T, D), BF16)] if has_y else []) + [_sds((2, 8, D), F32), _sds((8, D), F32)]
    args = (xo, y, gsv, g, dh, dres) if has_y else (xo, gsv, g, dh, dres)
    outs = _pc(body, name=name, grid=(T // TR,), in_specs=in_specs, out_specs=out_specs, out_shape=out_shape,
               compiler_params=_cp(("arbitrary",), 32 << 20))(*args)
    if has_y:
        return outs
    return outs[0], None, outs[1], outs[2]


def final_loss(x, y, gsv, g, target, nL, name):
    T = x.shape[0]

    def lossf(xo, gv, t):
        yn = (xo * lax.rsqrt(jnp.mean(xo * xo, axis=-1, keepdims=True) + EPS)) * gv
        e = yn - t
        return 0.5 * jnp.sum(jnp.sum(e * e, axis=-1, keepdims=True) * (1.0 / D), axis=0, keepdims=True)

    def body(x_ref, y_ref, gsv_ref, g_ref, t_ref, loss_ref, dx_ref, dy_ref, dgsv_ref, dg_ref):
        i = pl.program_id(0)

        @pl.when(i == 0)
        def _():
            loss_ref[...] = jnp.zeros_like(loss_ref)
            dg_ref[...] = jnp.zeros_like(dg_ref)

        @pl.when((i == 0) | (i == nL))
        def _():
            dgsv_ref[...] = jnp.zeros_like(dgsv_ref)

        @pl.when(i < nL)
        def _():
            gate = gsv_ref[0, 0:1, :]
            yv = y_ref[...]
            xo = x_ref[...] + gate * yv
            lv, vjp = jax.vjp(lossf, xo, g_ref[...], t_ref[...])
            dxo, dg, _ = vjp(jnp.ones((1, 1), F32))
            loss_ref[...] += jnp.broadcast_to(lv, loss_ref.shape)
            dx_ref[...] = dxo
            dy_ref[...] = (gate * dxo).astype(dy_ref.dtype)
            dgsv_ref[0, 0:1, :] += jnp.sum(yv * dxo, axis=0, keepdims=True)
            dg_ref[0:1, :] += dg

        @pl.when(i >= nL)
        def _():
            dx_ref[...] = jnp.zeros_like(dx_ref)
            dy_ref[...] = jnp.zeros_like(dy_ref)

    row = pl.BlockSpec((TR, D), lambda i: (i, 0))
    gspec = pl.BlockSpec((1, 8, D), lambda i: (i // nL, 0, 0))
    return _pc(
        body, name=name, grid=(T // TR,),
        in_specs=[row, row, gspec, pl.BlockSpec((1, D), lambda i: (0, 0)),
                  pl.BlockSpec((TR, D), lambda i: (jnp.minimum(i, nL - 1), 0))],
        out_specs=[pl.BlockSpec((8, 128), lambda i: (0, 0)), row, row, gspec, pl.BlockSpec((8, D), lambda i: (0, 0))],
        out_shape=[_sds((8, 128), F32), _sds((T, D), F32), _sds((T, D), BF16), _sds((2, 8, D), F32), _sds((8, D), F32)],
        compiler_params=_cp(("arbitrary",), 32 << 20),
    )(x, y, gsv, g, target)


FI_BLK = 2 * D_FF // 4


def _fi_chip(j):
    return (j % 2) * 2 + j // 2


def matmul_fi(a, b, mode, out_dtype, name):
    T = a.shape[0]
    if mode == "tn":
        tmd = 512

        def body(a_ref, b_ref, o_ref):
            o_ref[0] = _dg(a_ref[...], b_ref[...], _DIMS["tn"]).astype(o_ref.dtype)

        return _pc(body, name=name, grid=(D // tmd, 4),
                   in_specs=[pl.BlockSpec((T, tmd), lambda i, j: (0, i)), pl.BlockSpec((T, FI_BLK), lambda i, j: (0, j))],
                   out_specs=pl.BlockSpec((1, tmd, FI_BLK), lambda i, j: (_fi_chip(j), i, 0)),
                   out_shape=_sds((4, D, FI_BLK), out_dtype), compiler_params=_cp(("parallel", "arbitrary"), 48 << 20))(a, b)
    if mode == "nn":
        tm = _div_tile(T, 1280, 16)

        def body(a_ref, b_ref, o_ref):
            o_ref[...] = _dg(a_ref[...], b_ref[0], _DIMS["nn"]).astype(o_ref.dtype)

        return _pc(body, name=name, grid=(T // tm, 4),
                   in_specs=[pl.BlockSpec((tm, D), lambda i, j: (i, 0)), pl.BlockSpec((1, D, FI_BLK), lambda i, j: (_fi_chip(j), 0, 0))],
                   out_specs=pl.BlockSpec((tm, FI_BLK), lambda i, j: (i, j)), out_shape=_sds((T, 4 * FI_BLK), out_dtype),
                   compiler_params=_cp(("parallel", "arbitrary"), 40 << 20))(a, b)
    tm = _div_tile(T, 640, 16)

    def body(a_ref, b_ref, o_ref):
        acc = None
        for k in range(4):
            part = _dg(a_ref[:, k * FI_BLK:(k + 1) * FI_BLK], b_ref[_fi_chip(k)], _DIMS["nt"])
            acc = part if acc is None else acc + part
        o_ref[...] = acc.astype(o_ref.dtype)

    return _pc(body, name=name, grid=(T // tm,),
               in_specs=[pl.BlockSpec((tm, 4 * FI_BLK), lambda i: (i, 0)), pl.BlockSpec((4, D, FI_BLK), lambda i: (0, 0, 0))],
               out_specs=pl.BlockSpec((tm, D), lambda i: (i, 0)), out_shape=_sds((T, D), out_dtype),
               compiler_params=_cp(("parallel",), VMEM_CAP))(a, b)


def _swiglu(gate, up):
    return _silu(gate) * up


def swiglu_fwd(gu, name):
    T = gu.shape[0]

    def body(x_ref, o_ref):
        o_ref[...] = _swiglu(x_ref[:, :FI_BLK].astype(F32), x_ref[:, FI_BLK:].astype(F32)).astype(o_ref.dtype)

    return _pc(body, name=name, grid=(T // TR, 2), in_specs=[pl.BlockSpec((TR, 2 * FI_BLK), lambda i, j: (i, j))],
               out_specs=pl.BlockSpec((TR, FI_BLK), lambda i, j: (i, j)), out_shape=_sds((T, D_FF), BF16),
               compiler_params=_cp(("parallel", "parallel"), 24 << 20))(gu)


def swiglu_bwd(gu, dact, name):
    T = gu.shape[0]

    def body(x_ref, d_ref, o_ref):
        g, u, d = x_ref[:, :FI_BLK].astype(F32), x_ref[:, FI_BLK:].astype(F32), d_ref[...].astype(F32)
        sg = 1.0 / (1.0 + jnp.exp(-g))
        sl = g * sg
        o_ref[:, :FI_BLK] = (d * u * (sg + sl * (1.0 - sg))).astype(o_ref.dtype)
        o_ref[:, FI_BLK:] = (d * sl).astype(o_ref.dtype)

    return _pc(body, name=name, grid=(T // TR, 2),
               in_specs=[pl.BlockSpec((TR, 2 * FI_BLK), lambda i, j: (i, j)), pl.BlockSpec((TR, FI_BLK), lambda i, j: (i, j))],
               out_specs=pl.BlockSpec((TR, 2 * FI_BLK), lambda i, j: (i, j)), out_shape=_sds((T, 2 * D_FF), BF16),
               compiler_params=_cp(("parallel", "parallel"), 32 << 20))(gu, dact)


def rope_tables(L, Lc):
    t = np.arange(L)
    rows, cols = t // GRID_W, t % GRID_W
    inv = ROPE_BASE ** (-np.arange(16, dtype=np.float32) / 16)
    lane = np.arange(64)
    pos = np.where((lane // 32)[None, :] == 0, rows[:, None], cols[:, None]).astype(np.float32)
    ang = jnp.asarray(pos) * jnp.asarray(inv[lane % 16])[None, :]
    cos = jnp.concatenate([jnp.cos(ang), jnp.ones((Lc, 64), F32)], axis=0)
    sin = jnp.concatenate([jnp.sin(ang), jnp.zeros((Lc, 64), F32)], axis=0)
    R = np.zeros((128, 128), np.float32)
    for i in range(128):
        if (i % 32) < 16:
            R[i + 16, i] = -1.0
        else:
            R[i - 16, i] = 1.0
    return jnp.tile(cos, (1, 2)), jnp.tile(sin, (1, 2)), jnp.asarray(R)


def rope_apply(q_src, q_col, k_src, k_col, cos, sin, R, transpose, name, kv_src=None):
    T = cos.shape[0]
    with_kv = kv_src is not None

    def rot(x, c, s, Rm):
        if transpose:
            return x * c + hdot(x * s, Rm, "nt")
        return x * c + hdot(x, Rm) * s

    def body(q_ref, k_ref, c_ref, s_ref, R_ref, *rest):
        qo_ref, ko_ref = rest[-4:-2] if with_kv else rest
        c, s, Rm = c_ref[...], s_ref[...], R_ref[...]
        for j in range(2):
            qo_ref[:, j * 128:(j + 1) * 128] = rot(q_ref[:, j * 128:(j + 1) * 128].astype(F32), c, s, Rm).astype(qo_ref.dtype)
        ko_ref[...] = rot(k_ref[...].astype(F32), c, s, Rm).astype(ko_ref.dtype)
        if with_kv:
            rest[-2][...] = rest[0][...].astype(BF16)
            rest[-1][...] = rest[1][...].astype(BF16)

    tab = pl.BlockSpec((TR, 128), lambda i: (i, 0))
    wide = pl.BlockSpec((TR, 256), lambda i: (i, 0))
    kv_in = [pl.BlockSpec((TR, 256), lambda i: (i, C_KB // 256)), pl.BlockSpec((TR, 256), lambda i: (i, C_VB // 256))] if with_kv else []
    return _pc(body, name=name, grid=(T // TR,),
               in_specs=[pl.BlockSpec((TR, 256), lambda i: (i, q_col)), pl.BlockSpec((TR, 128), lambda i: (i, k_col)),
                         tab, tab, pl.BlockSpec((128, 128), lambda i: (0, 0))] + kv_in,
               out_specs=[wide, tab] + ([wide, wide] if with_kv else []),
               out_shape=[_sds((T, 256), BF16), _sds((T, 128), BF16)] + ([_sds((T, 256), BF16)] * 2 if with_kv else []),
               compiler_params=_cp(("parallel",), 16 << 20))(q_src, k_src, cos, sin, R, *([kv_src, kv_src] if with_kv else []))


_SCALE = HD ** -0.5


def _attn_tile(qh, ks, vs, extra):
    ss = []
    for k, add in ks:
        s = _dg(qh, k, _DIMS["nt"]) * _SCALE
        ss.append(s if add is None else s + add)
    m = ss[0].max(axis=-1, keepdims=True)
    for s in ss[1:]:
        m = jnp.maximum(m, s.max(axis=-1, keepdims=True))
    if extra is not None:
        m = jnp.maximum(m, extra)
    ps = [jnp.exp(s - m) for s in ss]
    den = ps[0].sum(axis=-1, keepdims=True)
    for p in ps[1:]:
        den = den + p.sum(axis=-1, keepdims=True)
    if extra is not None:
        den = den + jnp.exp(extra - m)
    num = _dg(ps[0], vs[0], _DIMS["nn"])
    for p, v in zip(ps[1:], vs[1:]):
        num = num + _dg(p, v, _DIMS["nn"])
    linv = 1.0 / den
    return num * linv, m, linv


def _attn_bwd_tile(qh, ks, vs, extra, m, linv, oh, doh):
    delta = jnp.sum(doh * oh, axis=-1, keepdims=True)
    dq = None
    dks, dvs, dss = [], [], []
    for (k, add), v in zip(ks, vs):
        s = _dg(qh, k, _DIMS["nt"]) * _SCALE
        if add is not None:
            s = s + add
        p = jnp.exp(s - m) * linv
        dvs.append(_dg(p, doh, _DIMS["tn"]))
        ds = p * (_dg(doh, v, _DIMS["nt"]) - delta)
        dss.append(ds)
        dsq = ds * _SCALE
        part = _dg(dsq, k, _DIMS["nn"])
        dq = part if dq is None else dq + part
        dks.append(_dg(dsq, qh, _DIMS["tn"]))
    dextra = None
    if extra is not None:
        dextra = -(jnp.exp(extra - m) * linv * delta)
    return dq, dks, dvs, dss, dextra


def _wa_mask(n, L):
    qpos = n * WA_BLK + lax.broadcasted_iota(jnp.int32, (WA_BLK, 3 * WA_BLK), 0)
    kpos = (n - 1) * WA_BLK + lax.broadcasted_iota(jnp.int32, (WA_BLK, 3 * WA_BLK), 1)
    ok = (jnp.abs(qpos - kpos) <= WA_BLK) & (kpos >= 0) & (kpos < L)
    return jnp.where(ok, 0.0, NEG).astype(F32)


WA_BPS = 2
_WA_PAIRS = (((0, 0), (1, 3), False), ((1, 2), (0, 1), True))


def _swap_halves_lanes(a):
    return jnp.concatenate([a[:, HD:], a[:, :HD]], axis=1)


def _wa_specs(L, Lc):
    nb = L // WA_BLK
    cb = L // Lc

    def blk(j):
        return pl.BlockSpec((WA_BLK, 128), lambda s: (jnp.clip(s * WA_BPS - 1 + j, 0, nb - 1), 0))

    return nb, [blk(j) for j in range(WA_BPS + 2)] + [pl.BlockSpec((Lc, 128), lambda s: (cb, 0))]


def _wa_pair_q(q_ref, qs, lo, hi):
    a = q_ref[qs, lo[0] * 128:(lo[0] + 1) * 128]
    b = q_ref[qs, hi[0] * 128:(hi[0] + 1) * 128]
    lane = lax.broadcasted_iota(jnp.int32, a.shape, 1)
    zero = jnp.zeros_like(a)
    return jnp.concatenate([jnp.where(lane < HD, a, zero), jnp.where(lane >= HD, b, zero)], axis=0)


def _wa_pair_vec(ref, qs, lo, hi, base=0):
    return jnp.concatenate([ref[qs, base + lo[1]:base + lo[1] + 1], ref[qs, base + hi[1]:base + hi[1] + 1]], axis=0)


def _wa_pair_sink(s_ref, n, lo, hi):
    return jnp.concatenate([jnp.broadcast_to(s_ref[lo[1]:lo[1] + 1, 0:1], (n, 1)), jnp.broadcast_to(s_ref[hi[1]:hi[1] + 1, 0:1], (n, 1))], axis=0)


def win_attn_fwd(qr, kr, krs, v, vs, sink, L, Lc, name):
    T = L + Lc
    nb, specs = _wa_specs(L, Lc)
    nk = WA_BPS + 2
    QB = WA_BPS * WA_BLK
    nlat = nb // WA_BPS

    def body(q_ref, *refs):
        groups = [refs[g * (nk + 1):(g + 1) * (nk + 1)] for g in range(4)]
        s_ref, o_ref, st_ref = refs[-3], refs[-2], refs[-1]
        s = pl.program_id(0)

        def run(qs, n, ks_of, vs_of):
            outs = []
            for lo, hi, swapped in _WA_PAIRS:
                kb, vb = groups[1 if swapped else 0], groups[3 if swapped else 2]
                o2, m2, l2 = _attn_tile(_wa_pair_q(q_ref, qs, lo, hi), ks_of(kb), vs_of(vb), _wa_pair_sink(s_ref, n, lo, hi))
                outs.append(o2)
                for r, (_, h) in enumerate((lo, hi)):
                    st_ref[qs, h:h + 1] = m2[r * n:(r + 1) * n]
                    st_ref[qs, WA_HEADS + h:WA_HEADS + h + 1] = l2[r * n:(r + 1) * n]
            lane = lax.broadcasted_iota(jnp.int32, (n, 128), 1)
            o_ref[qs, 0:128] = jnp.where(lane < HD, outs[0][:n], outs[1][n:]).astype(o_ref.dtype)
            o_ref[qs, 128:256] = jnp.where(lane < HD, outs[1][:n], outs[0][n:]).astype(o_ref.dtype)

        @pl.when(s < nlat)
        def _():
            for b in range(WA_BPS):
                m1 = _wa_mask(s * WA_BPS + b, L)
                mask = jnp.concatenate([m1, m1], axis=0)
                cat = lambda g: jnp.concatenate([g[b + j][...] for j in range(3)], axis=0)
                run(slice(b * WA_BLK, (b + 1) * WA_BLK), WA_BLK,
                    lambda kb: [(cat(kb), mask), (kb[nk][...], None)], lambda vb: [cat(vb), vb[nk][...]])

        @pl.when(s >= nlat)
        def _():
            run(slice(None), QB, lambda kb: [(kb[nk][...], None)], lambda vb: [vb[nk][...]])

    qspec = pl.BlockSpec((QB, 256), lambda s: (s, 0))
    return _pc(body, name=name, grid=(T // QB,),
               in_specs=[qspec] + specs * 4 + [pl.BlockSpec((8, 128), lambda s: (0, 0))],
               out_specs=[qspec, pl.BlockSpec((QB, 8), lambda s: (s, 0))], out_shape=[_sds((T, 256), BF16), _sds((T, 8), F32)],
               compiler_params=_cp(("arbitrary",), 40 << 20))(qr, *([kr] * (nk + 1)), *([krs] * (nk + 1)), *([v] * (nk + 1)), *([vs] * (nk + 1)), sink)


def win_attn_bwd(qr, kr, krs, v, vs, sink, do_src, o, stats, L, Lc, name):
    T = L + Lc
    nb, specs = _wa_specs(L, Lc)
    nk = WA_BPS + 2
    QB = WA_BPS * WA_BLK
    nlat = nb // WA_BPS
    cx = WA_BLK + L

    def body(q_ref, *refs):
        groups = [refs[g * (nk + 1):(g + 1) * (nk + 1)] for g in range(4)]
        s_ref, do_ref, o_ref, st_ref, dq_ref, dk_ref, dks_ref, dv_ref, dvs_ref, ds_ref = refs[4 * (nk + 1):]
        s = pl.program_id(0)

        @pl.when(s == 0)
        def _():
            for r in (dk_ref, dks_ref, dv_ref, dvs_ref, ds_ref):
                r[...] = jnp.zeros_like(r)

        def run(qs, n, ks_of, vs_of, rows):
            lane = lax.broadcasted_iota(jnp.int32, (n, 128), 1)
            dqs = []
            for lo, hi, swapped in _WA_PAIRS:
                kb, vb = groups[1 if swapped else 0], groups[3 if swapped else 2]
                dka, dva = (dks_ref, dvs_ref) if swapped else (dk_ref, dv_ref)
                pair = lambda ref: jnp.concatenate([jnp.where(lane < HD, ref[qs, lo[0] * 128:(lo[0] + 1) * 128].astype(F32), 0.0),
                                                    jnp.where(lane >= HD, ref[qs, hi[0] * 128:(hi[0] + 1) * 128].astype(F32), 0.0)], axis=0)
                dq2, dks, dvs, _, dex = _attn_bwd_tile(_wa_pair_q(q_ref, qs, lo, hi), ks_of(kb), vs_of(vb), _wa_pair_sink(s_ref, n, lo, hi),
                                                       _wa_pair_vec(st_ref, qs, lo, hi), _wa_pair_vec(st_ref, qs, lo, hi, WA_HEADS), pair(o_ref), pair(do_ref))
                dqs.append(dq2)
                for r, (_, h) in enumerate((lo, hi)):
                    ds_ref[h:h + 1, :] += jnp.broadcast_to(jnp.sum(dex[r * n:(r + 1) * n], axis=0, keepdims=True), (1, 128))
                if rows is not None:
                    dka[rows, :] += dks[0]
                    dva[rows, :] += dvs[0]
                dka[cx:cx + Lc, :] += dks[-1]
                dva[cx:cx + Lc, :] += dvs[-1]
            dq_ref[qs, 0:128] = jnp.where(lane < HD, dqs[0][:n], dqs[1][n:])
            dq_ref[qs, 128:256] = jnp.where(lane < HD, dqs[1][:n], dqs[0][n:])

        @pl.when(s < nlat)
        def _():
            for b in range(WA_BPS):
                nblk = s * WA_BPS + b
                m1 = _wa_mask(nblk, L)
                mask = jnp.concatenate([m1, m1], axis=0)
                cat = lambda g: jnp.concatenate([g[b + j][...] for j in range(3)], axis=0)
                run(slice(b * WA_BLK, (b + 1) * WA_BLK), WA_BLK, lambda kb: [(cat(kb), mask), (kb[nk][...], None)],
                    lambda vb: [cat(vb), vb[nk][...]], pl.ds(pl.multiple_of(nblk * WA_BLK, WA_BLK), 3 * WA_BLK))

        @pl.when(s >= nlat)
        def _():
            run(slice(None), QB, lambda kb: [(kb[nk][...], None)], lambda vb: [vb[nk][...]], None)

    qspec = pl.BlockSpec((QB, 256), lambda s: (s, 0))
    acc_spec = pl.BlockSpec((T + 2 * WA_BLK, 128), lambda s: (0, 0))
    acc_shape = _sds((T + 2 * WA_BLK, 128), F32)
    return _pc(body, name=name, grid=(T // QB,),
               in_specs=[qspec] + specs * 4 + [pl.BlockSpec((8, 128), lambda s: (0, 0)), qspec, qspec, pl.BlockSpec((QB, 8), lambda s: (s, 0))],
               out_specs=[qspec, acc_spec, acc_spec, acc_spec, acc_spec, pl.BlockSpec((8, 128), lambda s: (0, 0))],
               out_shape=[_sds((T, 256), F32), acc_shape, acc_shape, acc_shape, acc_shape, _sds((8, 128), F32)],
               compiler_params=_cp(("arbitrary",), 48 << 20))(qr, *([kr] * (nk + 1)), *([krs] * (nk + 1)), *([v] * (nk + 1)), *([vs] * (nk + 1)),
                                                              sink, do_src, o, stats)


def na_index_tables():
    qc = np.arange(GRID_W)[:, None]
    kc = np.arange(GRID_W)[None, :]
    cstart = np.clip(qc - NA_KW // 2, 0, GRID_W - NA_KW)
    ok = (kc >= cstart) & (kc < cstart + NA_KW)
    dx = np.clip(kc - qc, -(NA_KW - 1), NA_KW - 1) + (NA_KW - 1)
    off = np.arange(NA_KH)[:, None]
    kr = np.arange(NA_KH)[None, :]
    dy = kr - off + (NA_KH - 1)
    return ok, dx, dy


def _na_selectors():
    ok, dx, dy = na_index_tables()
    e1 = np.zeros((GRID_W * GRID_W, 128), np.float32)
    qi, ki = np.nonzero(ok)
    e1[qi * GRID_W + ki, dx[qi, ki]] = 1.0
    e2 = np.zeros((16, NA_KH * NA_KH), np.float32)
    oi, ri = np.meshgrid(np.arange(NA_KH), np.arange(NA_KH), indexing="ij")
    e2[dy[oi, ri].ravel(), (oi * NA_KH + ri).ravel()] = 1.0
    return ok, jnp.asarray(e1), jnp.asarray(np.kron(np.eye(NA_HEADS, dtype=np.float32), e2))


def na_bias_table(rpb, tag):
    ok, e1, e2 = _na_selectors()
    r2 = jnp.pad(rpb.astype(F32), ((0, 0), (0, 1), (0, 128 - (2 * NA_KW - 1)))).reshape(NA_HEADS * 16, 128)
    r1 = matmul(e2, r2, "tn", F32, f"na_bias_sel1_{tag}", hi=True)
    x = matmul(r1, e1, "nt", F32, f"na_bias_sel2_{tag}", hi=True)
    b = x.reshape(NA_HEADS, NA_KH, NA_KH, GRID_W, GRID_W).transpose(0, 1, 3, 2, 4)
    b = b + jnp.asarray(np.where(ok, 0.0, NEG).astype(np.float32))[None, None, :, None, :]
    return b.reshape(NA_HEADS, NA_KH, GRID_W, NA_KH * GRID_W)


def _na_rows(r, GR):
    r0 = jnp.clip(r - NA_KH // 2, 0, GR - NA_KH)
    return r0, jnp.clip(r - r0, 0, NA_KH - 1)


NA_RPS = 4


def _pair_rows(x):
    lane = lax.broadcasted_iota(jnp.int32, x.shape, 1)
    zero = jnp.zeros_like(x)
    return jnp.concatenate([jnp.where(lane < HD, x, zero), jnp.where(lane >= HD, x, zero)], axis=0)


def _unpair_rows(x2):
    n = x2.shape[0] // 2
    lane = lax.broadcasted_iota(jnp.int32, (n, 128), 1)
    return jnp.where(lane < HD, x2[:n], x2[n:])


def na_fwd(P, kb, vb, bias, L, Lc, name):
    T = L + Lc
    GR = L // GRID_W
    W = NA_KH * GRID_W
    QB = GRID_W * NA_RPS
    nlat = GR // NA_RPS

    def body(q_ref, k_ref, v_ref, b_ref, o_ref, st_ref):
        s = pl.program_id(0)

        def put(qs, p, res):
            o2, m2, l2 = res
            n = o2.shape[0] // 2
            o_ref[qs, p * 128:(p + 1) * 128] = _unpair_rows(o2).astype(o_ref.dtype)
            for r in range(2):
                st_ref[qs, 2 * p + r:2 * p + r + 1] = m2[r * n:(r + 1) * n]
                st_ref[qs, NA_HEADS + 2 * p + r:NA_HEADS + 2 * p + r + 1] = l2[r * n:(r + 1) * n]

        @pl.when(s < nlat)
        def _():
            for rr in range(NA_RPS):
                r0, off = _na_rows(s * NA_RPS + rr, GR)
                rows = pl.ds(pl.multiple_of(r0 * GRID_W, GRID_W), W)
                qs = slice(rr * GRID_W, (rr + 1) * GRID_W)
                for p in range(NA_HEADS // 2):
                    ps = slice(p * 128, (p + 1) * 128)
                    b2 = jnp.concatenate([b_ref[2 * p, off], b_ref[2 * p + 1, off]], axis=0)
                    put(qs, p, _attn_tile(_pair_rows(q_ref[qs, ps]), [(k_ref[rows, ps], b2), (k_ref[L:T, ps], None)],
                                          [v_ref[rows, ps], v_ref[L:T, ps]], None))

        @pl.when(s >= nlat)
        def _():
            for p in range(NA_HEADS // 2):
                ps = slice(p * 128, (p + 1) * 128)
                put(slice(None), p, _attn_tile(_pair_rows(q_ref[:, ps]), [(k_ref[L:T, ps], None)], [v_ref[L:T, ps]], None))

    one = pl.Buffered(1)
    return _pc(body, name=name, grid=(T // QB,),
               in_specs=[pl.BlockSpec((QB, 256), lambda r: (r, C_QB // 256)),
                         pl.BlockSpec((T, 256), lambda r: (0, 0), pipeline_mode=one),
                         pl.BlockSpec((T, 256), lambda r: (0, 0), pipeline_mode=one),
                         pl.BlockSpec((NA_HEADS, NA_KH, GRID_W, W), lambda r: (0, 0, 0, 0), pipeline_mode=one)],
               out_specs=[pl.BlockSpec((QB, 256), lambda r: (r, 0)), pl.BlockSpec((QB, 8), lambda r: (r, 0))],
               out_shape=[_sds((T, 256), BF16), _sds((T, 8), F32)],
               compiler_params=_cp(("arbitrary",), 32 << 20))(P, kb, vb, bias)


def na_bwd(P, kb, vb, bias, do_src, o, stats, L, Lc, name):
    T = L + Lc
    GR = L // GRID_W
    W = NA_KH * GRID_W
    QB = GRID_W * NA_RPS
    nlat = GR // NA_RPS

    def body(q_ref, k_ref, v_ref, b_ref, do_ref, o_ref, st_ref, dq_ref, dk_ref, dv_ref, db_ref):
        s = pl.program_id(0)

        @pl.when(s == 0)
        def _():
            dk_ref[...] = jnp.zeros_like(dk_ref)
            dv_ref[...] = jnp.zeros_like(dv_ref)
            db_ref[...] = jnp.zeros_like(db_ref)

        def tile(qs, p, ks, vs):
            ps = slice(p * 128, (p + 1) * 128)
            m2 = jnp.concatenate([st_ref[qs, 2 * p:2 * p + 1], st_ref[qs, 2 * p + 1:2 * p + 2]], axis=0)
            l2 = jnp.concatenate([st_ref[qs, NA_HEADS + 2 * p:NA_HEADS + 2 * p + 1], st_ref[qs, NA_HEADS + 2 * p + 1:NA_HEADS + 2 * p + 2]], axis=0)
            dq2, dks, dvs, dss, _ = _attn_bwd_tile(_pair_rows(q_ref[qs, ps]), ks, vs, None, m2, l2,
                                                   _pair_rows(o_ref[qs, ps].astype(F32)), _pair_rows(do_ref[qs, ps].astype(F32)))
            dq_ref[qs, ps] = _unpair_rows(dq2).astype(dq_ref.dtype)
            return dks, dvs, dss

        @pl.when(s < nlat)
        def _():
            for rr in range(NA_RPS):
                r0, off = _na_rows(s * NA_RPS + rr, GR)
                rows = pl.ds(pl.multiple_of(r0 * GRID_W, GRID_W), W)
                qs = slice(rr * GRID_W, (rr + 1) * GRID_W)
                for p in range(NA_HEADS // 2):
                    ps = slice(p * 128, (p + 1) * 128)
                    b2 = jnp.concatenate([b_ref[2 * p, off], b_ref[2 * p + 1, off]], axis=0)
                    dks, dvs, dss = tile(qs, p, [(k_ref[rows, ps], b2), (k_ref[L:T, ps], None)], [v_ref[rows, ps], v_ref[L:T, ps]])
                    dk_ref[rows, ps] += dks[0]
                    dv_ref[rows, ps] += dvs[0]
                    dk_ref[L:T, ps] += dks[1]
                    dv_ref[L:T, ps] += dvs[1]
                    db_ref[2 * p, off] += dss[0][:GRID_W]
                    db_ref[2 * p + 1, off] += dss[0][GRID_W:]

        @pl.when(s >= nlat)
        def _():
            for p in range(NA_HEADS // 2):
                ps = slice(p * 128, (p + 1) * 128)
                dks, dvs, _ = tile(slice(None), p, [(k_ref[L:T, ps], None)], [v_ref[L:T, ps]])
                dk_ref[L:T, ps] += dks[0]
                dv_ref[L:T, ps] += dvs[0]

    one = pl.Buffered(1)
    full = lambda shape: pl.BlockSpec(shape, lambda r: (0,) * len(shape), pipeline_mode=one)
    qspec = pl.BlockSpec((QB, 256), lambda r: (r, 0))
    return _pc(body, name=name, grid=(T // QB,),
               in_specs=[pl.BlockSpec((QB, 256), lambda r: (r, C_QB // 256)), full((T, 256)), full((T, 256)),
                         full((NA_HEADS, NA_KH, GRID_W, W)), pl.BlockSpec((QB, 256), lambda r: (r, 1)), qspec, pl.BlockSpec((QB, 8), lambda r: (r, 0))],
               out_specs=[qspec, full((T, 256)), full((T, 256)), full((NA_HEADS, NA_KH, GRID_W, W))],
               out_shape=[_sds((T, 256), BF16), _sds((T, 256), F32), _sds((T, 256), F32), _sds((NA_HEADS, NA_KH, GRID_W, W), F32)],
               compiler_params=_cp(("arbitrary",), 48 << 20))(P, kb, vb, bias, do_src, o, stats)


def na_rpb_grad(dbias, tag):
    _, e1, e2 = _na_selectors()
    x = dbias.reshape(NA_HEADS, NA_KH, GRID_W, NA_KH, GRID_W).transpose(0, 1, 3, 2, 4).reshape(NA_HEADS * NA_KH * NA_KH, GRID_W * GRID_W)
    r1 = matmul(x, e1, "nn", F32, f"na_rpb_sel1_{tag}", hi=True, tk=1024)
    r2 = matmul(e2, r1, "nn", F32, f"na_rpb_sel2_{tag}", hi=True)
    return r2.reshape(NA_HEADS, 16, 128)[:, :2 * NA_KH - 1, :2 * NA_KW - 1]


_HALO = 8
CONV_CB = 4
CONV_RB = 32


def _halo_specs(T, col0):
    nh = TR // _HALO
    specs = []
    for j in range(CONV_CB):
        specs.append(pl.BlockSpec((_HALO, 256), lambda i, j=j: (jnp.maximum(i * nh - 1, 0), col0 + j)))
        specs.append(pl.BlockSpec((TR, 256), lambda i, j=j: (i, col0 + j)))
        specs.append(pl.BlockSpec((_HALO, 256), lambda i, j=j: (jnp.minimum((i + 1) * nh, T // _HALO - 1), col0 + j)))
    return specs


def _fill_ext(ext, prv, cur, nxt, i, nL, nT):
    has_prev = jnp.where((i != 0) & (i != nL), 1.0, 0.0)
    has_next = jnp.where((i != nL - 1) & (i != nT - 1), 1.0, 0.0)
    ext[0:_HALO, :] = prv[...].astype(F32) * has_prev
    ext[_HALO:_HALO + TR, :] = cur[...].astype(F32)
    ext[_HALO + TR:, :] = nxt[...].astype(F32) * has_next


def conv_silu_fwd(P, w8, b, nL, name):
    T = P.shape[0]
    nT = T // TR

    def body(*refs):
        xin, (w_ref, b_ref, pre_ref, act_ref, ext) = refs[:3 * CONV_CB], refs[3 * CONV_CB:]
        i = pl.program_id(0)
        for j in range(CONV_CB):
            cs = slice(j * 256, (j + 1) * 256)
            _fill_ext(ext, *xin[3 * j:3 * j + 3], i, nL, nT)
            for r in range(0, TR, CONV_RB):
                y = jnp.broadcast_to(b_ref[:, cs], (CONV_RB, 256))
                for k in range(S_CONV):
                    y = y + w_ref[k:k + 1, cs] * ext[pl.ds(_HALO - S_CONV // 2 + k + r, CONV_RB), :]
                pre_ref[r:r + CONV_RB, cs] = y
                act_ref[r:r + CONV_RB, cs] = _silu(y)

    out = pl.BlockSpec((TR, 1024), lambda i: (i, 0))
    return _pc(body, name=name, grid=(nT,),
               in_specs=_halo_specs(T, C_XBC // 256) + [pl.BlockSpec((8, 1024), lambda i: (0, 0)), pl.BlockSpec((1, 1024), lambda i: (0, 0))],
               out_specs=[out, out], out_shape=[_sds((T, 1024), F32), _sds((T, 1024), F32)],
               scratch_shapes=[pltpu.VMEM((TR + 2 * _HALO, 256), F32)],
               compiler_params=_cp(("parallel",), 24 << 20))(*([P] * (3 * CONV_CB)), w8, b)


def dsilu(pre, dxs_list, db_list, dc_list, name):
    T = pre.shape[0]
    n1, n2, n3 = len(dxs_list), len(db_list), len(dc_list)

    def body(*refs):
        pre_ref = refs[0]
        ins = refs[1:1 + n1 + n2 + n3]
        out = refs[-1]

        def part(rs, lo, hi):
            g = rs[0][...].astype(F32)
            for r in rs[1:]:
                g = g + r[...].astype(F32)
            x = pre_ref[:, lo:hi]
            sg = 1.0 / (1.0 + jnp.exp(-x))
            sl = x * sg
            out[:, lo:hi] = g * (sg + sl * (1.0 - sg))

        part(ins[:n1], 0, 512)
        part(ins[n1:n1 + n2], 512, 768)
        part(ins[n1 + n2:], 768, 1024)

    spec = lambda w: pl.BlockSpec((TR, w), lambda i: (i, 0))
    return _pc(body, name=name, grid=(T // TR,),
               in_specs=[spec(1024)] + [spec(512)] * n1 + [spec(256)] * (n2 + n3),
               out_specs=spec(1024), out_shape=_sds((T, 1024), F32),
               compiler_params=_cp(("parallel",), 32 << 20))(pre, *dxs_list, *db_list, *dc_list)


def conv_bwd(dpre, P, w8, nL, name):
    T = P.shape[0]
    nT = T // TR

    def body(*refs):
        din, xin, (w_ref, dx_ref, dw_ref, db_ref, extd) = refs[:3 * CONV_CB], refs[3 * CONV_CB:4 * CONV_CB], refs[4 * CONV_CB:]
        i = pl.program_id(0)

        @pl.when(i == 0)
        def _():
            dw_ref[...] = jnp.zeros_like(dw_ref)
            db_ref[...] = jnp.zeros_like(db_ref)

        fold = lambda a: a.reshape(CONV_RB // 8, 8, 256).sum(axis=0)
        for j in range(CONV_CB):
            cs = slice(j * 256, (j + 1) * 256)
            _fill_ext(extd, *din[3 * j:3 * j + 3], i, nL, nT)
            dws = [jnp.zeros((8, 256), F32) for _ in range(S_CONV)]
            dbs = jnp.zeros((8, 256), F32)
            for r in range(0, TR, CONV_RB):
                x = xin[j][r:r + CONV_RB, :]
                dx = jnp.zeros((CONV_RB, 256), F32)
                for k in range(S_CONV):
                    sd = extd[pl.ds(_HALO + S_CONV // 2 - k + r, CONV_RB), :]
                    dx = dx + w_ref[k:k + 1, cs] * sd
                    dws[k] = dws[k] + fold(sd * x)
                dx_ref[r:r + CONV_RB, cs] = dx.astype(dx_ref.dtype)
                dbs = dbs + fold(din[3 * j + 1][r:r + CONV_RB, :])
            for k in range(S_CONV):
                dw_ref[k:k + 1, cs] += jnp.sum(dws[k], axis=0, keepdims=True)
            db_ref[0:1, cs] += jnp.sum(dbs, axis=0, keepdims=True)

    acc = pl.BlockSpec((8, 1024), lambda i: (0, 0))
    xspecs = [pl.BlockSpec((TR, 256), lambda i, j=j: (i, C_XBC // 256 + j)) for j in range(CONV_CB)]
    return _pc(body, name=name, grid=(nT,),
               in_specs=_halo_specs(T, 0) + xspecs + [acc],
               out_specs=[pl.BlockSpec((TR, 1024), lambda i: (i, 0)), acc, acc],
               out_shape=[_sds((T, 1024), BF16), _sds((8, 1024), F32), _sds((8, 1024), F32)],
               scratch_shapes=[pltpu.VMEM((TR + 2 * _HALO, 256), F32)],
               compiler_params=_cp(("arbitrary",), 24 << 20))(*([dpre] * (3 * CONV_CB)), *([P] * CONV_CB), w8)


def _onehot_row(h, n):
    return (lax.broadcasted_iota(jnp.int32, (1, n), 1) == h).astype(F32)


def _onehot_col(h, n):
    return (lax.broadcasted_iota(jnp.int32, (n, 1), 0) == h).astype(F32)


S_PAIRS = S_HEADS // 2


def _ssd_chunk(xs, dtr, dtb, alog, bm, cm, hin, reverse):
    Qn = S_Q
    ii = lax.broadcasted_iota(jnp.int32, (Qn, Qn), 0)
    jj = lax.broadcasted_iota(jnp.int32, (Qn, Qn), 1)
    keep = (ii <= jj) if reverse else (ii >= jj)
    tri = keep.astype(F32)
    triT = ((jj <= ii) if reverse else (jj >= ii)).astype(F32)
    eye = (ii == jj).astype(F32)
    low = jj < S_P
    top = ii < S_P
    dt = _softplus(dtr + dtb)
    a = dt * (-jnp.exp(alog))
    cs = hdot(tri, a)
    csT = hdot(a, triT, "tn")
    dtT = hdot(dt, eye, "tn")
    last = _onehot_row(0 if reverse else Qn - 1, Qn)
    ys, houts = [], []
    for p in range(S_PAIRS):
        g = p // (S_PAIRS // S_GROUPS)
        if p % (S_PAIRS // S_GROUPS) == 0:
            G = bdot(cm[g], bm[g], "nt")
        per_head = []
        for h in (2 * p, 2 * p + 1):
            eh_r, eh_c = _onehot_row(h, S_HEADS), _onehot_col(h, S_HEADS)
            cs_c = jnp.sum(cs * eh_r, axis=1, keepdims=True)
            dt_c = jnp.sum(dt * eh_r, axis=1, keepdims=True)
            cs_r = jnp.sum(csT * eh_c, axis=0, keepdims=True)
            dt_r = jnp.sum(dtT * eh_c, axis=0, keepdims=True)
            tot = jnp.sum(cs_r * last, axis=1, keepdims=True)
            w = G * jnp.exp(jnp.where(keep, cs_c - cs_r, NEG)) * dt_r
            per_head.append((bdot(w, xs[p], "nn"), jnp.exp(cs_c), jnp.exp(tot - cs_c) * dt_c, jnp.exp(tot)))
        (y0, e0, f0, d0), (y1, e1, f1, d1) = per_head
        y = jnp.where(low, y0, y1) + bdot(cm[g], hin[p], "nt") * jnp.where(low, e0, e1)
        hout = hin[p] * jnp.where(top, d0, d1) + bdot(xs[p] * jnp.where(low, f0, f1), bm[g], "tn")
        ys.append(y)
        houts.append(hout)
    return ys, houts


def _ssd_orders(L, Lc):
    nl, ncx = L // S_Q, Lc // S_Q
    fwd = lambda s: jnp.where(s < ncx, nl + s, s - ncx)
    bwd = lambda s: nl + ncx - 1 - s
    return nl + ncx, fwd, bwd


def _ssd_in_specs(fo, bo, step):
    def at(order, w, col):
        return pl.BlockSpec((S_Q, w), lambda u: (order(step(u)), col))
    specs = []
    for order in (fo, bo):
        specs += [at(order, 512, 0), at(order, 256, 2), at(order, 256, 3), at(order, 128, C_DT // 128)]
    return specs


def ssd_fwd(act, P, dtb, alog, L, Lc, name):
    T = L + Lc
    ns, fo, bo = _ssd_orders(L, Lc)

    def body(xf, bf, cf, df, xb, bb, cb, db, dtb_ref, al_ref, yf, yb, hsf, hsb, Hf, Hb):
        s = pl.program_id(0)

        @pl.when(s == 0)
        def _():
            Hf[...] = jnp.zeros_like(Hf)
            Hb[...] = jnp.zeros_like(Hb)

        for d, (x_r, b_r, c_r, dt_r, y_r, hs_r, H) in enumerate(((xf, bf, cf, df, yf, hsf, Hf), (xb, bb, cb, db, yb, hsb, Hb))):
            hin = [H[p] for p in range(S_PAIRS)]
            hs_r[0] = H[...]
            ys, houts = _ssd_chunk(
                [x_r[:, p * 128:(p + 1) * 128] for p in range(S_PAIRS)], dt_r[:, d * 8:(d + 1) * 8],
                dtb_ref[d:d + 1, 0:8], al_ref[d:d + 1, 0:8],
                [b_r[:, g * S_N:(g + 1) * S_N] for g in range(S_GROUPS)], [c_r[:, g * S_N:(g + 1) * S_N] for g in range(S_GROUPS)],
                hin, reverse=(d == 1))
            for p in range(S_PAIRS):
                y_r[:, p * 128:(p + 1) * 128] = ys[p]
                H[p] = houts[p]

    ident = lambda u: u
    small = pl.BlockSpec((8, 128), lambda u: (0, 0))
    hspec = pl.BlockSpec((1, S_PAIRS, 2 * S_P, S_N), lambda u: (u, 0, 0, 0))
    return _pc(body, name=name, grid=(ns,),
               in_specs=_ssd_in_specs(fo, bo, ident) + [small, small],
               out_specs=[pl.BlockSpec((S_Q, 512), lambda u: (fo(u), 0)), pl.BlockSpec((S_Q, 512), lambda u: (bo(u), 0)), hspec, hspec],
               out_shape=[_sds((T, 512), F32), _sds((T, 512), F32), _sds((ns, S_PAIRS, 2 * S_P, S_N), F32), _sds((ns, S_PAIRS, 2 * S_P, S_N), F32)],
               scratch_shapes=[pltpu.VMEM((S_PAIRS, 2 * S_P, S_N), F32), pltpu.VMEM((S_PAIRS, 2 * S_P, S_N), F32)],
               compiler_params=_cp(("arbitrary",), 32 << 20))(act, act, act, P, act, act, act, P, dtb, alog)


def ssd_bwd(act, P, dtb, alog, hsf, hsb, dy, L, Lc, name):
    T = L + Lc
    ns, fo, bo = _ssd_orders(L, Lc)
    step = lambda u: ns - 1 - u

    def body(xf, bf, cf, df, xb, bb, cb, db, dtb_ref, al_ref, hsf_r, hsb_r, dyf, dyb,
             dxf, dbf, dcf, ddf, dxb, dbb, dcb, ddb, ddtb, dal, dHf, dHb):
        u = pl.program_id(0)

        @pl.when(u == 0)
        def _():
            dHf[...] = jnp.zeros_like(dHf)
            dHb[...] = jnp.zeros_like(dHb)
            ddtb[...] = jnp.zeros_like(ddtb)
            dal[...] = jnp.zeros_like(dal)

        dirs = ((xf, bf, cf, df, hsf_r, dyf, dxf, dbf, dcf, ddf, dHf), (xb, bb, cb, db, hsb_r, dyb, dxb, dbb, dcb, ddb, dHb))
        for d, (x_r, b_r, c_r, dt_r, hs_r, dy_r, dx_o, db_o, dc_o, dd_o, dH) in enumerate(dirs):
            f = functools.partial(_ssd_chunk, reverse=(d == 1))
            _, vjp = jax.vjp(
                f, [x_r[:, p * 128:(p + 1) * 128] for p in range(S_PAIRS)], dt_r[:, d * 8:(d + 1) * 8],
                dtb_ref[d:d + 1, 0:8], al_ref[d:d + 1, 0:8],
                [b_r[:, g * S_N:(g + 1) * S_N] for g in range(S_GROUPS)], [c_r[:, g * S_N:(g + 1) * S_N] for g in range(S_GROUPS)],
                [hs_r[0, p] for p in range(S_PAIRS)])
            gx, gdt, gdtb, gal, gb, gc, gh = vjp(([dy_r[:, p * 128:(p + 1) * 128] for p in range(S_PAIRS)],
                                                  [dH[p] for p in range(S_PAIRS)]))
            for p in range(S_PAIRS):
                dx_o[:, p * 128:(p + 1) * 128] = gx[p]
                dH[p] = gh[p]
            for g in range(S_GROUPS):
                db_o[:, g * S_N:(g + 1) * S_N] = gb[g]
                dc_o[:, g * S_N:(g + 1) * S_N] = gc[g]
            dd_o[...] = gdt
            ddtb[d:d + 1, 0:8] += gdtb
            dal[d:d + 1, 0:8] += gal

    small = pl.BlockSpec((8, 128), lambda u: (0, 0))
    hspec = pl.BlockSpec((1, S_PAIRS, 2 * S_P, S_N), lambda u: (step(u), 0, 0, 0))
    at = lambda order, w: pl.BlockSpec((S_Q, w), lambda u: (order(step(u)), 0))
    outs = []
    for order in (fo, bo):
        outs += [at(order, 512), at(order, 256), at(order, 256), at(order, 8)]
    oshape = [_sds((T, 512), F32), _sds((T, 256), F32), _sds((T, 256), F32), _sds((T, 8), F32)]
    return _pc(body, name=name, grid=(ns,),
               in_specs=_ssd_in_specs(fo, bo, step) + [small, small, hspec, hspec, at(fo, 512), at(bo, 512)],
               out_specs=outs + [small, small], out_shape=oshape + oshape + [_sds((8, 128), F32), _sds((8, 128), F32)],
               scratch_shapes=[pltpu.VMEM((S_PAIRS, 2 * S_P, S_N), F32), pltpu.VMEM((S_PAIRS, 2 * S_P, S_N), F32)],
               compiler_params=_cp(("arbitrary",), 40 << 20))(act, act, act, P, act, act, act, P, dtb, alog, hsf, hsb, dy, dy)


def _ssm_out(yf, yb, xs, z, dskip, g):
    y = (yf + yb + dskip * xs) * _silu(z)
    return (y * lax.rsqrt(jnp.mean(y * y, axis=-1, keepdims=True) + EPS)) * g


def ssm_out_fwd(yf, yb, act, P, dskip, g, name):
    T = yf.shape[0]

    def body(yf_r, yb_r, xs_r, z_r, d_r, g_r, o_r):
        o_r[...] = _ssm_out(yf_r[...], yb_r[...], xs_r[...], z_r[...], d_r[...], g_r[...]).astype(o_r.dtype)

    row = pl.BlockSpec((TR, 512), lambda i: (i, 0))
    vec = pl.BlockSpec((1, 512), lambda i: (0, 0))
    return _pc(body, name=name, grid=(T // TR,),
               in_specs=[row, row, row, pl.BlockSpec((TR, 512), lambda i: (i, C_Z // 512)), vec, vec],
               out_specs=row, out_shape=_sds((T, 512), BF16),
               compiler_params=_cp(("parallel",), 16 << 20))(yf, yb, act, P, dskip, g)


def ssm_out_bwd(yf, yb, act, P, dskip, g, do_src, name):
    T = yf.shape[0]

    def body(yf_r, yb_r, xs_r, z_r, d_r, g_r, do_r, dy_r, dxs_r, dz_r, dv_r):
        @pl.when(pl.program_id(0) == 0)
        def _():
            dv_r[...] = jnp.zeros_like(dv_r)

        _, vjp = jax.vjp(_ssm_out, yf_r[...], yb_r[...], xs_r[...], z_r[...], d_r[...], g_r[...])
        dyf, _, dxs, dz, dd, dg = vjp(do_r[...].astype(F32))
        dy_r[...] = dyf
        dxs_r[...] = dxs
        dz_r[...] = dz.astype(dz_r.dtype)
        dv_r[0:1, :] += dd
        dv_r[1:2, :] += dg

    row = pl.BlockSpec((TR, 512), lambda i: (i, 0))
    vec = pl.BlockSpec((1, 512), lambda i: (0, 0))
    return _pc(body, name=name, grid=(T // TR,),
               in_specs=[row, row, row, pl.BlockSpec((TR, 512), lambda i: (i, C_Z // 512)), vec, vec,
                         pl.BlockSpec((TR, 512), lambda i: (i, 1))],
               out_specs=[row, row, row, pl.BlockSpec((8, 512), lambda i: (0, 0))],
               out_shape=[_sds((T, 512), F32), _sds((T, 512), F32), _sds((T, 512), BF16), _sds((8, 512), F32)],
               compiler_params=_cp(("arbitrary",), 24 << 20))(yf, yb, act, P, dskip, g, do_src)


def add_halves(xv, got, cvec, name):
    n, r, cdim = xv.shape
    h = r // 2

    def body(c_ref, x_ref, g_ref, o_ref):
        o_ref[...] = (x_ref[...].astype(F32) + g_ref[...].astype(F32)).astype(o_ref.dtype)

    gs = pltpu.PrefetchScalarGridSpec(
        num_scalar_prefetch=1, grid=(n,),
        in_specs=[pl.BlockSpec((1, h, cdim), lambda k, c_ref: (k, c_ref[0], 0)), pl.BlockSpec((1, h, cdim), lambda k, c_ref: (k, 0, 0))],
        out_specs=pl.BlockSpec((1, h, cdim), lambda k, c_ref: (k, 0, 0)))
    return _pc(body, name=name, grid_spec=gs, out_shape=_sds((n, h, cdim), BF16),
               compiler_params=_cp(("arbitrary",), 24 << 20))(cvec, xv, got)


def sum_slots(a, name):
    n, r, cdim = a.shape
    tr = _div_tile(r, 512, 16)

    def body(a_ref, o_ref):
        acc = a_ref[0].astype(F32)
        for k in range(1, n):
            acc = acc + a_ref[k].astype(F32)
        o_ref[...] = acc

    return _pc(body, name=name, grid=(r // tr,), in_specs=[pl.BlockSpec((n, tr, cdim), lambda i: (0, i, 0))],
               out_specs=pl.BlockSpec((tr, cdim), lambda i: (i, 0)), out_shape=_sds((r, cdim), F32),
               compiler_params=_cp(("parallel",), 32 << 20))(a)


def adamw(w, g, m, v, name):
    B, R, C = w.shape
    tr = _div_tile(R, max(8, (1 << 19) // max(C, 1) // 8 * 8), 8) if R % 8 == 0 else R
    c1 = 1.0 / (1.0 - ADAM_B1 ** ADAM_STEP)
    c2 = 1.0 / (1.0 - ADAM_B2 ** ADAM_STEP)

    def body(w_ref, g_ref, m_ref, v_ref, d_ref, mo_ref, vo_ref):
        gg = g_ref[...]
        mn = ADAM_B1 * m_ref[...] + (1.0 - ADAM_B1) * gg
        vn = ADAM_B2 * v_ref[...] + (1.0 - ADAM_B2) * (gg * gg)
        d_ref[...] = -ADAM_LR * ((mn * c1) / (jnp.sqrt(vn * c2) + ADAM_EPS) + ADAM_WD * w_ref[...])
        mo_ref[...] = mn
        vo_ref[...] = vn

    spec = pl.BlockSpec((1, tr, C), lambda b, i: (b, i, 0))
    return _pc(body, name=name, grid=(B, R // tr), in_specs=[spec] * 4, out_specs=[spec] * 3,
               out_shape=[_sds((B, R, C), F32)] * 3, compiler_params=_cp(("parallel", "parallel"), 32 << 20))(w, g, m, v)


def _me():
    return lax.axis_index("x"), lax.axis_index("y"), lax.axis_index("c")


def _flip(v, bit):
    return 1 - v if bit else v


def allgather8(xv, name):
    R = xv.shape[0]

    def body(x_ref, out_ref, sum_ref, send_sems, recv_sems):
        mx, my, mc = _me()
        me = 4 * mx + 2 * my + mc
        out_ref[me] = x_ref[...]
        sends, recvs = [], []
        for k in range(1, 8):
            px, py, pc = _flip(mx, k & 4), _flip(my, k & 2), _flip(mc, k & 1)
            peer = 4 * px + 2 * py + pc
            sends.append(pltpu.make_async_remote_copy(src_ref=x_ref, dst_ref=out_ref.at[me], send_sem=send_sems.at[k - 1],
                                                      recv_sem=recv_sems.at[k - 1], device_id=(px, py, pc), device_id_type=MESH))
            recvs.append(pltpu.make_async_remote_copy(src_ref=x_ref, dst_ref=out_ref.at[peer], send_sem=send_sems.at[k - 1],
                                                      recv_sem=recv_sems.at[k - 1], device_id=(px, py, pc), device_id_type=MESH))
        for cp in sends:
            cp.start()
        for cp in recvs:
            cp.wait_recv()
        for cp in sends:
            cp.wait_send()
        acc = out_ref[0]
        for d in range(1, 8):
            acc = acc + out_ref[d]
        sum_ref[...] = acc

    vm = pl.BlockSpec(memory_space=pltpu.VMEM)
    return _pc(body, name=name, pin=False, in_specs=[vm], out_specs=[vm, vm], out_shape=[_sds((8, R, 128), F32), _sds((R, 128), F32)],
               scratch_shapes=[pltpu.SemaphoreType.DMA((7,)), pltpu.SemaphoreType.DMA((7,))],
               compiler_params=_cp(None, 32 << 20))(xv)


def _other_chips(mx, my):
    return [(1 - mx, my), (mx, 1 - my), (1 - mx, 1 - my)]


def _halves(r, mc, mult):
    h = r // 2
    return pl.ds(pl.multiple_of(mc * h, mult), h), pl.ds(pl.multiple_of((1 - mc) * h, mult), h)


def _rcopy(src, dst, send_sems, recv_sems, k, to):
    return pltpu.make_async_remote_copy(src_ref=src, dst_ref=dst, send_sem=send_sems.at[k], recv_sem=recv_sems.at[k],
                                        device_id=to, device_id_type=MESH)


def _gather_body(xs, outs, send_sems, recv_sems):
    n = len(xs)
    mx, my, mc = _me()
    chip = 2 * mx + my
    sib = (mx, my, 1 - mc)
    chips = _other_chips(mx, my)
    idx = [2 * cx + cy for cx, cy in chips]
    cp = functools.partial(_rcopy, send_sems=send_sems, recv_sems=recv_sems)
    hv = [_halves(x.shape[0], mc, 16) for x in xs]
    first, passed = [], []
    for a in range(n):
        for j, (cx, cy) in enumerate(chips):
            first.append(cp(xs[a].at[hv[a][0]], outs[a].at[chip, hv[a][0]], k=6 * a + j, to=(cx, cy, mc)))
            first[-1].start()
    for a in range(n):
        for j in range(3):
            cp(xs[a].at[hv[a][0]], outs[a].at[idx[j], hv[a][0]], k=6 * a + j, to=sib).wait_recv()
            passed.append(cp(outs[a].at[idx[j], hv[a][0]], outs[a].at[idx[j], hv[a][0]], k=6 * a + 3 + j, to=sib))
            passed[-1].start()
    for a in range(n):
        for j in range(3):
            cp(xs[a].at[hv[a][1]], outs[a].at[idx[j], hv[a][1]], k=6 * a + 3 + j, to=sib).wait_recv()
    for c_ in first + passed:
        c_.wait_send()


def _my_chip():
    return 2 * lax.axis_index("x") + lax.axis_index("y")


def _own_slots(outs, shards):
    return [lax.dynamic_update_index_in_dim(o, x, _my_chip(), 0) for o, x in zip(outs, shards)]


def gather_weights(shards, name):
    n = len(shards)

    def body(*refs):
        _gather_body(refs[:n], refs[n:2 * n], *refs[2 * n:])

    hbm = pl.BlockSpec(memory_space=pl.ANY)
    outs = _pc(body, name=name, in_specs=[hbm] * n, out_specs=[hbm] * n, out_shape=[_sds((4,) + x.shape, x.dtype) for x in shards],
               scratch_shapes=[pltpu.SemaphoreType.DMA((6 * n,)), pltpu.SemaphoreType.DMA((6 * n,))])(*shards)
    return _own_slots(outs, shards)


GATHER_REST_ID = 3


def gather_weights_sc(shards, name):
    n = len(shards)
    x_refs = [jax.new_ref(x, memory_space=pltpu.MemorySpace.HBM) for x in shards]
    out_refs = [jax.empty_ref(_sds((4,) + x.shape, x.dtype), memory_space=pltpu.MemorySpace.HBM) for x in shards]

    @pl.kernel(mesh=plsc.ScalarSubcoreMesh(axis_name="sc", num_cores=1), name=name,
               scratch_types=(pltpu.SemaphoreType.DMA((6 * n,)), pltpu.SemaphoreType.DMA((6 * n,))),
               compiler_params=pltpu.CompilerParams(collective_id=GATHER_REST_ID))
    def launch(send_sems, recv_sems):
        mx, my, mc = _me()
        barrier = pltpu.get_barrier_semaphore()
        for peer in [(mx, my, 1 - mc)] + [(cx, cy, mc) for cx, cy in _other_chips(mx, my)]:
            pl.semaphore_signal(barrier, inc=1, device_id=peer, device_id_type=MESH)
        pl.semaphore_wait(barrier, 4)
        _gather_body(x_refs, out_refs, send_sems, recv_sems)

    launch()
    return _own_slots([o[...] for o in out_refs], shards)


def swap_halves(arrs, name):
    n = len(arrs)

    def body(*refs):
        xs, outs = refs[:n], refs[n:2 * n]
        send_sems, recv_sems = refs[2 * n:]
        mx, my, mc = _me()
        cps = []
        for a in range(n):
            theirs = _halves(xs[a].shape[1], mc, 16)[1]
            cps.append(_rcopy(xs[a].at[pl.ds(0, 4), theirs], outs[a], send_sems, recv_sems, a, (mx, my, 1 - mc)))
            cps[-1].start()
        for c_ in cps:
            c_.wait()

    hbm = pl.BlockSpec(memory_space=pl.ANY)
    return _pc(body, name=name, in_specs=[hbm] * n, out_specs=[hbm] * n,
               out_shape=[_sds((4, x.shape[1] // 2, x.shape[2]), x.dtype) for x in arrs],
               scratch_shapes=[pltpu.SemaphoreType.DMA((n,)), pltpu.SemaphoreType.DMA((n,))])(*arrs)


SCATTER_ID = 4


def scatter_chips_sc(arrs, name):
    n = len(arrs)
    x_refs = [jax.new_ref(x, memory_space=pltpu.MemorySpace.HBM) for x in arrs]
    out_refs = [jax.empty_ref(_sds(x.shape, x.dtype), memory_space=pltpu.MemorySpace.HBM) for x in arrs]

    @pl.kernel(mesh=plsc.ScalarSubcoreMesh(axis_name="sc", num_cores=1), name=name,
               scratch_types=(pltpu.SemaphoreType.DMA((3 * n,)), pltpu.SemaphoreType.DMA((3 * n,))),
               compiler_params=pltpu.CompilerParams(collective_id=SCATTER_ID))
    def launch(send_sems, recv_sems):
        mx, my, mc = _me()
        chip = 2 * mx + my
        chips = _other_chips(mx, my)
        idx = [2 * cx + cy for cx, cy in chips]
        barrier = pltpu.get_barrier_semaphore()
        for cx, cy in chips:
            pl.semaphore_signal(barrier, inc=1, device_id=(cx, cy, mc), device_id_type=MESH)
        pl.semaphore_wait(barrier, 3)
        cp = functools.partial(_rcopy, send_sems=send_sems, recv_sems=recv_sems)
        sends = []
        for a in range(n):
            for j, (cx, cy) in enumerate(chips):
                sends.append(cp(x_refs[a].at[idx[j]], out_refs[a].at[chip], k=3 * a + j, to=(cx, cy, mc)))
                sends[-1].start()
        for a in range(n):
            for j, (cx, cy) in enumerate(chips):
                cp(x_refs[a].at[idx[j]], out_refs[a].at[idx[j]], k=3 * a + j, to=(cx, cy, mc)).wait_recv()
        for c_ in sends:
            c_.wait_send()

    launch()
    return _own_slots([o[...] for o in out_refs], [lax.dynamic_index_in_dim(x, _my_chip(), 0, keepdims=False) for x in arrs])


def share_halves(parts, name):
    flat = [p for w in parts for p in w]
    nw, n = len(parts), len(flat)
    depth = n // nw

    def body(*refs):
        xs, outs = refs[:n], refs[n:n + nw]
        send_sems, recv_sems = refs[n + nw:]
        mx, my, mc = _me()
        sib = (mx, my, 1 - mc)
        sends, recvs = [], []
        for a in range(n):
            w, l = a // depth, a % depth
            mine, theirs = _halves(outs[w].shape[1], mc, 8)
            sends.append(_rcopy(xs[a], outs[w].at[l, mine], send_sems, recv_sems, a, sib))
            recvs.append(_rcopy(xs[a], outs[w].at[l, theirs], send_sems, recv_sems, a, sib))
            sends[-1].start()
        for c_ in recvs:
            c_.wait_recv()
        for c_ in sends:
            c_.wait_send()

    hbm = pl.BlockSpec(memory_space=pl.ANY)
    outs = _pc(body, name=name, in_specs=[hbm] * n, out_specs=[hbm] * nw,
               out_shape=[_sds((depth, 2 * w[0].shape[0], w[0].shape[1]), F32) for w in parts],
               scratch_shapes=[pltpu.SemaphoreType.DMA((n,)), pltpu.SemaphoreType.DMA((n,))])(*flat)
    outs = list(outs)
    mc = lax.axis_index("c")
    for w in range(nw):
        for l in range(depth):
            h = parts[w][l].shape[0]
            outs[w] = lax.dynamic_update_slice(outs[w], parts[w][l][None], (l, mc * h, 0))
    return outs


_BIG = ("w_in", "w_out", "w_ffn_in", "w_ffn_out")
N_CHIPS = 4
DEPTH = 2


def _pad_rows(v, mult=8):
    n = v.shape[0]
    rows = -(-n // 128)
    rows = -(-rows // mult) * mult
    return jnp.pad(v, (0, rows * 128 - n)).reshape(rows, 128)


class _Flat:
    def __init__(self):
        self.items = []

    def add(self, name, a):
        self.items.append((name, a.shape, a.reshape(-1).astype(F32)))

    def rows(self):
        return _pad_rows(jnp.concatenate([a for _, _, a in self.items]))

    def split(self, rows):
        flat = rows.reshape(-1)
        out, o = {}, 0
        for name, shape, a in self.items:
            out[name] = flat[o:o + a.shape[0]].reshape(shape)
            o += a.shape[0]
        return out

    def split_lead(self, rows3):
        n = rows3.shape[0]
        flat = rows3.reshape(n, -1)
        out, o = {}, 0
        for name, shape, a in self.items:
            out[name] = flat[:, o:o + a.shape[0]].reshape((n,) + tuple(shape))
            o += a.shape[0]
        return out


def _gsv(rows):
    z = jnp.zeros((2, D), F32)
    r = [z if a is None else a for a in rows] + [z] * 5
    return jnp.stack(r, axis=1)


def _pad8(a, rows=8, cols=128):
    return jnp.zeros((rows, cols), F32).at[:a.shape[0], :a.shape[1]].set(a.astype(F32))


def kernel(x, c, ctx, c_ctx, w_mod, b_mod, g_mix, w_in, wa_sink, na_rpb, ssm_conv_w, ssm_conv_b, ssm_dt_bias, ssm_a_log, ssm_d, ssm_norm_g, w_out, g_ffn, w_ffn_in, w_ffn_out, g_final, loss_target, m_c_ctx, m_w_mod, m_b_mod, m_g_mix, m_w_in, m_wa_sink, m_na_rpb, m_ssm_conv_w, m_ssm_conv_b, m_ssm_dt_bias, m_ssm_a_log, m_ssm_d, m_ssm_norm_g, m_w_out, m_g_ffn, m_w_ffn_in, m_w_ffn_out, m_g_final, v_c_ctx, v_w_mod, v_b_mod, v_g_mix, v_w_in, v_wa_sink, v_na_rpb, v_ssm_conv_w, v_ssm_conv_b, v_ssm_dt_bias, v_ssm_a_log, v_ssm_d, v_ssm_norm_g, v_w_out, v_g_ffn, v_w_ffn_in, v_w_ffn_out, v_g_final):
    L, Lc = x.shape[1], ctx.shape[1]
    T = L + Lc
    nL = L // TR
    mx, my, mc = lax.axis_index("x"), lax.axis_index("y"), lax.axis_index("c")
    dev = 4 * mx + 2 * my + mc
    chip = 2 * mx + my
    MODW = 6 * D // N_CHIPS
    CW = 1024 // N_CHIPS

    sc = _silu(c.astype(F32))
    scc = _silu(c_ctx.astype(F32))[None]
    f1 = _Flat()
    f1.add("sc", sc)
    f1.add("conv_w", ssm_conv_w)
    g1, _ = allgather8(f1.rows(), "gather_cond")
    g1 = f1.split_lead(g1)
    sc_all = g1["sc"][:, 0]
    conv_w = jnp.concatenate([g1["conv_w"][2 * k] for k in range(N_CHIPS)], axis=-1)
    A16 = jnp.concatenate([sc_all, scc, jnp.zeros((7, D), F32)], axis=0)

    mod_part = matmul_layers(A16, w_mod, "nn", "mod_fwd")
    f2 = _Flat()
    f2.add("mod", mod_part)
    g2, _ = allgather8(f2.rows(), "gather_mod")
    g2 = f2.split_lead(g2)["mod"]
    mods = jnp.concatenate([g2[2 * k] for k in range(N_CHIPS)], axis=-1) + b_mod[:, None, :]
    mod_l = lax.dynamic_index_in_dim(mods, dev, axis=1, keepdims=False).reshape(DEPTH, 6, D)
    mod_c = mods[:, 8].reshape(DEPTH, 6, D)
    mod = jnp.stack([mod_l, mod_c], axis=1)
    mrow = lambda l, j: mod[l, :, j]

    own = {"w_in": w_in, "w_out": w_out, "w_ffn_in": w_ffn_in, "w_ffn_out": w_ffn_out}
    sh16 = [own[n][l].astype(BF16) for n in _BIG for l in range(DEPTH)]
    after_mod = (g2[0, 0, 0, 0] * 0).astype(BF16)
    gath = list(gather_weights([sh16[0] + after_mod], "gather_first"))
    after_first = (gath[0][0, 0, 0] * 0).astype(BF16)
    gath += list(gather_weights_sc([sh16[1] + after_first] + sh16[2:], "gather_rest"))
    gw = {n: [gath[DEPTH * i + l] for l in range(DEPTH)] for i, n in enumerate(_BIG)}
    W_in = [jnp.pad(jnp.concatenate([g[k] for k in range(N_CHIPS)], axis=1), ((0, 0), (0, IN_PAD - IN_COLS))) for g in gw["w_in"]]
    W_out = [g.reshape(D, D) for g in gw["w_out"]]
    W_fo = [g.reshape(D_FF, D) for g in gw["w_ffn_out"]]
    W_fi = gw["w_ffn_in"]

    cos, sin, rotm = rope_tables(L, Lc)
    x0 = jnp.concatenate([x[0], ctx[0]], axis=0).astype(F32)

    sv = []
    xin = x0
    gsv_first = _gsv([None, mrow(0, 0), mrow(0, 1)])
    _, h1 = res_norm_mod(x0, None, gsv_first, g_mix[0][None], nL, "norm_first")
    for l in range(DEPTH):
        s = {"xin": xin, "h1": h1}
        P = matmul(h1, W_in[l], "nn", F32, f"in_proj{l}", tn=IN_PAD)
        qr, kr, kb, vb = rope_apply(P, C_QA // 256, P, C_KA // 128, cos, sin, rotm, False, f"rope{l}", kv_src=P)
        sink8 = _pad8(jnp.broadcast_to(wa_sink[l][:, None], (WA_HEADS, 128)))
        krs, va = _swap_halves_lanes(kr), P[:, C_VA:C_VA + 128]
        vas = _swap_halves_lanes(va)
        oa, sta = win_attn_fwd(qr, kr, krs, va, vas, sink8, L, Lc, f"wa_fwd{l}")
        bias = na_bias_table(na_rpb[l], l)
        ob, stb = na_fwd(P, kb, vb, bias, L, Lc, f"na_fwd{l}")
        w8 = jnp.concatenate([conv_w[l], jnp.zeros((1, 1024), F32)], axis=0)
        pre, act = conv_silu_fwd(P, w8, ssm_conv_b[l][None], nL, f"conv_fwd{l}")
        dtb8, al8 = _pad8(ssm_dt_bias[l]), _pad8(ssm_a_log[l])
        yf, yb, hsf, hsb = ssd_fwd(act, P, dtb8, al8, L, Lc, f"ssd_fwd{l}")
        dskip = jnp.repeat(ssm_d[l], S_P)[None]
        oc = ssm_out_fwd(yf, yb, act, P, dskip, ssm_norm_g[l][None], f"ssm_out_fwd{l}")
        mixin = [(oa, 0), (ob, 256), (oc, 512)]
        mix = out_proj_fwd(mixin, W_out[l], f"out_proj{l}")
        gsv_mid = _gsv([mrow(l, 2), mrow(l, 3), mrow(l, 4)])
        x1, h2 = res_norm_mod(xin, mix, gsv_mid, g_ffn[l][None], nL, f"norm_mid{l}")
        gu = matmul_fi(h2, W_fi[l], "nn", BF16, f"ffn_in{l}")
        af = swiglu_fwd(gu, f"swiglu_fwd{l}")
        fo = matmul(af, W_fo[l], "nn", BF16, f"ffn_out{l}", tk=D_FF)
        s.update(P=P, qr=qr, kr=kr, krs=krs, va=va, vas=vas, sink8=sink8, oa=oa, sta=sta, ob=ob, stb=stb, kb=kb, vb=vb, bias=bias, w8=w8, pre=pre, act=act, dtb8=dtb8, al8=al8, yf=yf,
                 yb=yb, hsf=hsf, hsb=hsb, dskip=dskip, mixin=mixin, mix=mix, gsv_mid=gsv_mid, x1=x1, h2=h2, gu=gu, af=af, fo=fo)
        if l + 1 < DEPTH:
            s["gsv_end"] = _gsv([mrow(l, 5), mrow(l + 1, 0), mrow(l + 1, 1)])
            xin, h1 = res_norm_mod(x1, fo, s["gsv_end"], g_mix[l + 1][None], nL, f"norm_end{l}")
        else:
            s["gsv_end"] = _gsv([mrow(l, 5), None, None])
        sv.append(s)

    last = sv[-1]
    loss8, dres, dfo, dgsv_end, dg_final = final_loss(last["x1"], last["fo"], last["gsv_end"], g_final[None], loss_target[0].astype(F32), nL, "final_loss")
    loss = lax.psum(loss8[0, 0], ("x", "y", "c"))

    dmod = [[None] * 6 for _ in range(DEPTH)]
    gW = {n: [None] * DEPTH for n in _BIG}
    small = [dict() for _ in range(DEPTH)]
    parts = [None] * DEPTH
    cvec = mc.astype(jnp.int32).reshape(1)
    grad_x = None
    for l in reversed(range(DEPTH)):
        s = sv[l]
        dmod[l][5] = dgsv_end[:, 0]
        if l + 1 < DEPTH:
            dmod[l + 1][0], dmod[l + 1][1] = dgsv_end[:, 1], dgsv_end[:, 2]
        daf = matmul(dfo, W_fo[l], "nt", BF16, f"ffn_out_dx{l}")
        gW["w_ffn_out"][l] = matmul(s["af"], dfo, "tn", BF16, f"ffn_out_dw{l}", tm=1408, tk=T).reshape(N_CHIPS, D_FF // N_CHIPS, D)
        dgu = swiglu_bwd(s["gu"], daf, f"swiglu_bwd{l}")
        dh2 = matmul_fi(dgu, W_fi[l], "nt", BF16, f"ffn_in_dx{l}")
        gW["w_ffn_in"][l] = matmul_fi(s["h2"], dgu, "tn", BF16, f"ffn_in_dw{l}")
        dres, dmix, dgsv_mid, dg_ffn = res_norm_mod_bwd(s["x1"], s["mix"], s["gsv_mid"], g_ffn[l][None], dh2, dres, nL, f"norm_mid_bwd{l}")
        dmod[l][2], dmod[l][3], dmod[l][4] = dgsv_mid[:, 0], dgsv_mid[:, 1], dgsv_mid[:, 2]
        dmixin = matmul(dmix, W_out[l], "nt", BF16, f"out_proj_dx{l}")
        gW["w_out"][l] = out_proj_dw(s["mixin"], dmix, f"out_proj_dw{l}").reshape(N_CHIPS, D // N_CHIPS, D)
        P = s["P"]
        dqr, dkr, dkrs, dva, dvas, dsink = win_attn_bwd(s["qr"], s["kr"], s["krs"], s["va"], s["vas"], s["sink8"], dmixin, s["oa"], s["sta"], L, Lc,
                                                        f"wa_bwd{l}")
        dkr, dva = dkr + _swap_halves_lanes(dkrs), dva + _swap_halves_lanes(dvas)
        dqa, dka = rope_apply(dqr, 0, dkr[WA_BLK:WA_BLK + T], 0, cos, sin, rotm, True, f"rope_bwd{l}")
        dqb, dkb, dvb, dbias = na_bwd(P, s["kb"], s["vb"], s["bias"], dmixin, s["ob"], s["stb"], L, Lc, f"na_bwd{l}")
        dy, dxs1, dz, dvec = ssm_out_bwd(s["yf"], s["yb"], s["act"], P, s["dskip"], ssm_norm_g[l][None], dmixin, f"ssm_out_bwd{l}")
        dxf, dbf, dcf, ddf, dxb, dbb, dcb, ddb, ddtb, dal = ssd_bwd(s["act"], P, s["dtb8"], s["al8"], s["hsf"], s["hsb"], dy, L, Lc, f"ssd_bwd{l}")
        dpre = dsilu(s["pre"], [dxf, dxb, dxs1], [dbf, dbb], [dcf, dcb], f"dsilu{l}")
        dxbc, dw8, db8 = conv_bwd(dpre, P, s["w8"], nL, f"conv_bwd{l}")
        ddt = jnp.concatenate([ddf, ddb, jnp.zeros((T, IN_PAD - IN_COLS), F32)], axis=1)
        pieces = [(dqa, C_QA), (dqb, C_QB), (dz, C_Z), (dka, C_KA), (dva[WA_BLK:WA_BLK + T], C_VA), (dkb, C_KB), (dvb, C_VB),
                  (dxbc, C_XBC), (ddt, C_DT)]
        dh1, dwin = in_proj_bwd(pieces, s["h1"], W_in[l], f"in_proj_bwd{l}")
        cw = IN_COLS // N_CHIPS
        gW["w_in"][l] = jnp.stack([dwin[:, k * cw:(k + 1) * cw] for k in range(N_CHIPS)])
        garr = [gW[n][l] for n in _BIG]
        got = swap_halves(garr, f"reduce_d2d{l}")
        chip_sum = [add_halves(garr[a], got[a], cvec, f"reduce_add_pair{l}_{a}") for a in range(len(garr))]
        parts[l] = scatter_chips_sc(chip_sum, f"reduce_ici{l}")
        small[l] = dict(g_ffn=dg_ffn[0], wa_sink=dsink[:WA_HEADS, 0], na_rpb=na_rpb_grad(dbias, l), conv_w=dw8[:S_CONV], conv_b=db8[0],
                        dt_bias=ddtb[:2, :8], a_log=dal[:2, :8], ssm_d=dvec[0].reshape(S_HEADS, S_P).sum(axis=1), norm_g=dvec[1])
        if l > 0:
            p = sv[l - 1]
            dres, dfo, dgsv_end, dg_mix = res_norm_mod_bwd(s["xin"], p["fo"], p["gsv_end"], g_mix[l][None], dh1, dres, nL, f"norm_end_bwd{l - 1}")
        else:
            grad_x, _, dgsv_first, dg_mix = res_norm_mod_bwd(s["xin"], None, gsv_first, g_mix[0][None], dh1, dres, nL, "norm_first_bwd")
            dmod[0][0], dmod[0][1] = dgsv_first[:, 1], dgsv_first[:, 2]
        small[l]["g_mix"] = dg_mix[0]
    for l in range(DEPTH):
        for j in range(6):
            if dmod[l][j] is None:
                dmod[l][j] = jnp.zeros((2, D), F32)
    dmod = jnp.stack([jnp.stack(r, axis=1) for r in dmod])

    f3 = _Flat()
    f3.add("dmod_l", dmod[:, 0].reshape(DEPTH, 6 * D))
    f3.add("dmod_c", dmod[:, 1].reshape(DEPTH, 6 * D))
    f3.add("g_final", dg_final[0])
    for n in ("g_mix", "g_ffn", "wa_sink", "na_rpb", "conv_w", "conv_b", "dt_bias", "a_log", "ssm_d", "norm_g"):
        f3.add(n, jnp.stack([small[l][n] for l in range(DEPTH)]))
    g3, s3 = allgather8(f3.rows(), "reduce_small")
    dmod_all = f3.split_lead(g3)["dmod_l"]
    s3 = f3.split(s3)
    dmodc_tot = s3["dmod_c"]
    col0 = chip * MODW
    G16, G16c = [], []
    for l in range(DEPTH):
        rows = jnp.concatenate([dmod_all[:, l], dmodc_tot[l][None], jnp.zeros((7, 6 * D), F32)], axis=0)
        G16.append(lax.dynamic_slice_in_dim(rows, col0, MODW, axis=1))
        rc = jnp.concatenate([dmodc_tot[l][None], jnp.zeros((15, 6 * D), F32)], axis=0)
        G16c.append(lax.dynamic_slice_in_dim(rc, col0, MODW, axis=1))
    grad_w_mod = matmul_layers(A16, jnp.stack(G16), "tn", "mod_dw")
    dscc_part = matmul_layers(jnp.stack(G16c), w_mod, "nt", "mod_dx")[:, 0].sum(axis=0)
    _, s4 = allgather8(_pad_rows(dscc_part * (mc == 1).astype(F32)), "reduce_cctx")
    dscc = s4.reshape(-1)[:D]
    cc = c_ctx.astype(F32)
    sg = 1.0 / (1.0 + jnp.exp(-cc))
    grad_c_ctx = dscc * (sg * (1.0 + cc * (1.0 - sg)))

    halves = [[sum_slots(parts[l][i], f"reduce_add_chips{l}_{i}") for l in range(DEPTH)] for i in range(len(_BIG))]
    gsh = dict(zip(_BIG, share_halves(halves, "reduce_share")))

    grads = {"c_ctx": grad_c_ctx, "w_mod": grad_w_mod, "b_mod": s3["dmod_l"] + s3["dmod_c"], "g_mix": s3["g_mix"], "w_in": gsh["w_in"],
             "wa_sink": s3["wa_sink"], "na_rpb": s3["na_rpb"],
             "ssm_conv_w": lax.dynamic_slice_in_dim(s3["conv_w"], chip * CW, CW, axis=2), "ssm_conv_b": s3["conv_b"],
             "ssm_dt_bias": s3["dt_bias"], "ssm_a_log": s3["a_log"], "ssm_d": s3["ssm_d"], "ssm_norm_g": s3["norm_g"],
             "w_out": gsh["w_out"], "g_ffn": s3["g_ffn"], "w_ffn_in": gsh["w_ffn_in"], "w_ffn_out": gsh["w_ffn_out"], "g_final": s3["g_final"]}
    wts = {"c_ctx": c_ctx, "w_mod": w_mod, "b_mod": b_mod, "g_mix": g_mix, "w_in": w_in, "wa_sink": wa_sink, "na_rpb": na_rpb,
           "ssm_conv_w": ssm_conv_w, "ssm_conv_b": ssm_conv_b, "ssm_dt_bias": ssm_dt_bias, "ssm_a_log": ssm_a_log, "ssm_d": ssm_d,
           "ssm_norm_g": ssm_norm_g, "w_out": w_out, "g_ffn": g_ffn, "w_ffn_in": w_ffn_in, "w_ffn_out": w_ffn_out, "g_final": g_final}
    ms = {"c_ctx": m_c_ctx, "w_mod": m_w_mod, "b_mod": m_b_mod, "g_mix": m_g_mix, "w_in": m_w_in, "wa_sink": m_wa_sink, "na_rpb": m_na_rpb,
          "ssm_conv_w": m_ssm_conv_w, "ssm_conv_b": m_ssm_conv_b, "ssm_dt_bias": m_ssm_dt_bias, "ssm_a_log": m_ssm_a_log, "ssm_d": m_ssm_d,
          "ssm_norm_g": m_ssm_norm_g, "w_out": m_w_out, "g_ffn": m_g_ffn, "w_ffn_in": m_w_ffn_in, "w_ffn_out": m_w_ffn_out, "g_final": m_g_final}
    vs = {"c_ctx": v_c_ctx, "w_mod": v_w_mod, "b_mod": v_b_mod, "g_mix": v_g_mix, "w_in": v_w_in, "wa_sink": v_wa_sink, "na_rpb": v_na_rpb,
          "ssm_conv_w": v_ssm_conv_w, "ssm_conv_b": v_ssm_conv_b, "ssm_dt_bias": v_ssm_dt_bias, "ssm_a_log": v_ssm_a_log, "ssm_d": v_ssm_d,
          "ssm_norm_g": v_ssm_norm_g, "w_out": v_w_out, "g_ffn": v_g_ffn, "w_ffn_in": v_w_ffn_in, "w_ffn_out": v_w_ffn_out, "g_final": v_g_final}
    names = list(wts)
    grads = {n: grads[n].reshape(wts[n].shape).astype(F32) for n in names}
    big = ("w_mod", "w_in", "w_out", "w_ffn_in", "w_ffn_out")
    delta, new_m, new_v = {}, {}, {}
    for n in big:
        delta[n], new_m[n], new_v[n] = adamw(wts[n], grads[n], ms[n], vs[n], f"adamw_{n}")
    packs = []
    for src in (wts, grads, ms, vs):
        f = _Flat()
        for n in names:
            if n not in big:
                f.add(n, src[n])
        packs.append(f)
    d_, m_, v_ = adamw(*[f.rows()[None] for f in packs], "adamw_small")
    for dst, rows in ((delta, d_), (new_m, m_), (new_v, v_)):
        dst.update(packs[0].split(rows[0]))

    return (loss, grad_x[:L][None], *[grads[n] for n in names], *[delta[n] for n in names],
            *[new_m[n] for n in names], *[new_v[n] for n in names])
```

```python
import functools

import numpy as np
import jax
import jax.numpy as jnp
from jax import lax
from jax.experimental import pallas as pl
from jax.experimental.pallas import tpu as pltpu
from jax.experimental.pallas import tpu_sc as plsc

F32 = jnp.float32
BF16 = jnp.bfloat16
_MXU = jnp.bfloat16
_HI = lax.Precision.HIGHEST
MESH = pl.DeviceIdType.MESH

D = 1024
HD = 64
GRID_W = 64
EPS = 1e-6
ROPE_BASE = 10000.0
WA_HEADS, WA_KV = 4, 2
WA_BLK = 128
NA_HEADS, NA_KH, NA_KW = 4, 8, 16
S_HEADS, S_P, S_INNER, S_GROUPS, S_N, S_CONV, S_Q = 8, 64, 512, 2, 128, 7, 128
D_FF = 2816
IN_COLS = 2832
IN_PAD = 2944
C_QA, C_QB, C_Z, C_KA, C_VA, C_KB, C_VB, C_XBC, C_DT = 0, 256, 512, 1024, 1152, 1280, 1536, 1792, 2816
ADAM_LR, ADAM_B1, ADAM_B2, ADAM_EPS, ADAM_WD, ADAM_STEP = 0.001, 0.9, 0.999, 1e-08, 0.01, 10

TR = 256
NEG = -1e30
VMEM_CAP = 56 * 1024 * 1024


PIN_BYTES = 256 * 1024


def _is_big(a):
    return hasattr(a, "shape") and len(a.shape) >= 2 and int(np.prod(a.shape)) * jnp.dtype(a.dtype).itemsize >= PIN_BYTES


def _pc(body, *, out_shape, pin=True, **kw):
    if not pin:
        return pl.pallas_call(body, out_shape=out_shape, **kw)
    one = isinstance(out_shape, jax.ShapeDtypeStruct)
    outs = [pltpu.HBM(s.shape, s.dtype) if _is_big(s) else s for s in ([out_shape] if one else out_shape)]
    call = pl.pallas_call(body, out_shape=outs[0] if one else outs, **kw)
    return lambda *args: call(*[pltpu.with_memory_space_constraint(a, pltpu.HBM) if _is_big(a) else a for a in args])


def _cp(sem=None, vmem=None):
    kw = {}
    if sem is not None:
        kw["dimension_semantics"] = sem
    if vmem is not None:
        kw["vmem_limit_bytes"] = int(min(max(vmem, 16 * 1024 * 1024), VMEM_CAP))
    return pltpu.CompilerParams(**kw)


def _sds(shape, dtype):
    return jax.ShapeDtypeStruct(tuple(shape), dtype)


_DIMS = {"nn": ((1,), (0,)), "nt": ((1,), (1,)), "tn": ((0,), (0,))}


def _dg(a, b, dims):
    return lax.dot_general(a.astype(_MXU), b.astype(_MXU), (dims, ((), ())), preferred_element_type=F32)


@functools.partial(jax.custom_vjp, nondiff_argnums=(2,))
def bdot(a, b, mode):
    return _dg(a, b, _DIMS[mode])


def _bdot_fwd(a, b, mode):
    return bdot(a, b, mode), (a, b)


def _bdot_bwd(mode, res, g):
    a, b = res
    if mode == "nn":
        return bdot(g, b, "nt"), bdot(a, g, "tn")
    if mode == "nt":
        return bdot(g, b, "nn"), bdot(g, a, "tn")
    return bdot(b, g, "nt"), bdot(a, g, "nn")


bdot.defvjp(_bdot_fwd, _bdot_bwd)


def hdot(a, b, mode="nn"):
    return lax.dot_general(a, b, (_DIMS[mode], ((), ())), precision=_HI, preferred_element_type=F32)


def _silu(x):
    return x / (1.0 + jnp.exp(-x))


def _softplus(x):
    return jnp.maximum(x, 0.0) + jnp.log(1.0 + jnp.exp(-jnp.abs(x)))


def _div_tile(n, cap, mult):
    if n <= cap:
        return n
    best = None
    for t in range(mult, cap + 1, mult):
        if n % t == 0:
            best = t
    assert best is not None, (n, cap, mult)
    return best


def matmul(a, b, mode, out_dtype, name, tm=640, tn=1536, tk=1408, hi=False):
    if mode == "tn":
        K, M = a.shape
    else:
        M, K = a.shape
    N = b.shape[0] if mode == "nt" else b.shape[1]
    tm = _div_tile(M, tm, 128 if mode == "tn" else 16)
    tn = _div_tile(N, tn, 128)
    tk = _div_tile(K, tk, 128 if mode != "tn" else 16)
    nk = K // tk
    dims = _DIMS[mode]

    def body(a_ref, b_ref, o_ref, *acc):
        if hi:
            part = lax.dot_general(a_ref[...], b_ref[...], (dims, ((), ())), precision=_HI, preferred_element_type=F32)
        else:
            part = _dg(a_ref[...], b_ref[...], dims)
        if nk == 1:
            o_ref[...] = part.astype(o_ref.dtype)
        else:
            k = pl.program_id(2)

            @pl.when(k == 0)
            def _():
                acc[0][...] = part

            @pl.when(k > 0)
            def _():
                acc[0][...] += part

            @pl.when(k == nk - 1)
            def _():
                o_ref[...] = acc[0][...].astype(o_ref.dtype)

    if mode == "tn":
        a_spec = pl.BlockSpec((tk, tm), lambda i, j, k: (k, i))
    else:
        a_spec = pl.BlockSpec((tm, tk), lambda i, j, k: (i, k))
    if mode == "nt":
        b_spec = pl.BlockSpec((tn, tk), lambda i, j, k: (j, k))
    else:
        b_spec = pl.BlockSpec((tk, tn), lambda i, j, k: (k, j))
    isz = lambda x: jnp.dtype(x.dtype).itemsize
    vmem = 2 * (tm * tk * isz(a) + tk * tn * isz(b) + tm * tn * jnp.dtype(out_dtype).itemsize) + 3 * tm * tn * 4
    return _pc(
        body, name=name, grid=(M // tm, N // tn, nk),
        in_specs=[a_spec, b_spec], out_specs=pl.BlockSpec((tm, tn), lambda i, j, k: (i, j)),
        out_shape=_sds((M, N), out_dtype),
        scratch_shapes=[pltpu.VMEM((tm, tn), F32)] if nk > 1 else [],
        compiler_params=_cp(("parallel", "parallel", "arbitrary"), vmem + (8 << 20)),
    )(a, b)


def matmul_layers(a, b, mode, name):
    nl = b.shape[0]
    a3 = a if a.ndim == 3 else a[None]
    shared = a3.shape[0] == 1
    M = a3.shape[2] if mode == "tn" else a3.shape[1]
    N = b.shape[1] if mode == "nt" else b.shape[2]

    def body(a_ref, b_ref, o_ref):
        o_ref[0] = _dg(a_ref[0], b_ref[0], _DIMS[mode])

    return _pc(body, name=name, grid=(nl,),
               in_specs=[pl.BlockSpec((1,) + a3.shape[1:], (lambda l: (0, 0, 0)) if shared else (lambda l: (l, 0, 0))),
                         pl.BlockSpec((1,) + b.shape[1:], lambda l: (l, 0, 0))],
               out_specs=pl.BlockSpec((1, M, N), lambda l: (l, 0, 0)), out_shape=_sds((nl, M, N), F32),
               compiler_params=_cp(("parallel",), 48 << 20))(a3, b)


def out_proj_fwd(pieces, w, name):
    T = pieces[0][0].shape[0]
    arrs, offs = [a for a, _ in pieces], [o for _, o in pieces]
    n = len(arrs)
    tm = _div_tile(T, 640, 16)

    def body(*refs):
        w_ref, o_ref = refs[n], refs[n + 1]
        acc = None
        for j in range(n):
            part = _dg(refs[j][...], w_ref[offs[j]:offs[j] + arrs[j].shape[1], :], _DIMS["nn"])
            acc = part if acc is None else acc + part
        o_ref[...] = acc.astype(o_ref.dtype)

    return _pc(body, name=name, grid=(T // tm,),
               in_specs=[pl.BlockSpec((tm, a.shape[1]), lambda i: (i, 0)) for a in arrs] + [pl.BlockSpec(w.shape, lambda i: (0, 0))],
               out_specs=pl.BlockSpec((tm, w.shape[1]), lambda i: (i, 0)), out_shape=_sds((T, w.shape[1]), BF16),
               compiler_params=_cp(("parallel",), 32 << 20))(*arrs, w)


def out_proj_dw(pieces, dy, name):
    T, N = dy.shape
    arrs, offs = [a for a, _ in pieces], [o for _, o in pieces]
    n = len(arrs)
    rows = sum(a.shape[1] for a in arrs)
    tn = 512

    def body(*refs):
        d_ref, o_ref = refs[n], refs[n + 1]
        for j in range(n):
            o_ref[offs[j]:offs[j] + arrs[j].shape[1], :] = _dg(refs[j][...], d_ref[...], _DIMS["tn"]).astype(o_ref.dtype)

    return _pc(body, name=name, grid=(N // tn,),
               in_specs=[pl.BlockSpec(a.shape, lambda j: (0, 0)) for a in arrs] + [pl.BlockSpec((T, tn), lambda j: (0, j))],
               out_specs=pl.BlockSpec((rows, tn), lambda j: (0, j)), out_shape=_sds((rows, N), BF16),
               compiler_params=_cp(("parallel",), 48 << 20))(*arrs, dy)


def in_proj_bwd(pieces, h1, w, name):
    T = h1.shape[0]
    arrs = [a for a, _ in pieces]
    offs = [o for _, o in pieces]
    wid = [a.shape[1] for a in arrs]
    n = len(arrs)
    assert sum(wid) == IN_PAD, "the pieces must tile all columns of P"
    tm = _div_tile(T, 640, 16)

    def dx_body(*refs):
        w_ref, o_ref = refs[n], refs[n + 1]
        acc = None
        for j in range(n):
            part = _dg(refs[j][...], w_ref[:, offs[j]:offs[j] + wid[j]], _DIMS["nt"])
            acc = part if acc is None else acc + part
        o_ref[...] = acc.astype(o_ref.dtype)

    dh1 = _pc(dx_body, name=name + "_dx", grid=(T // tm,),
              in_specs=[pl.BlockSpec((tm, wj), lambda i: (i, 0)) for wj in wid] + [pl.BlockSpec((D, IN_PAD), lambda i: (0, 0))],
              out_specs=pl.BlockSpec((tm, D), lambda i: (i, 0)), out_shape=_sds((T, D), BF16),
              compiler_params=_cp(("parallel",), 40 << 20))(*arrs, w)

    tmd, nk = 512, 4
    tk = T // nk

    def dw_body(h_ref, *refs):
        o_ref, acc = refs[n], refs[n + 1]
        k = pl.program_id(1)

        @pl.when(k == 0)
        def _():
            acc[...] = jnp.zeros_like(acc)

        for j in range(n):
            acc[:, offs[j]:offs[j] + wid[j]] += _dg(h_ref[...], refs[j][...], _DIMS["tn"])

        @pl.when(k == nk - 1)
        def _():
            o_ref[...] = acc[...].astype(o_ref.dtype)

    dw = _pc(dw_body, name=name + "_dw", grid=(D // tmd, nk),
             in_specs=[pl.BlockSpec((tk, tmd), lambda i, k: (k, i))] + [pl.BlockSpec((tk, wj), lambda i, k: (k, 0)) for wj in wid],
             out_specs=pl.BlockSpec((tmd, IN_PAD), lambda i, k: (i, 0)), out_shape=_sds((D, IN_PAD), BF16),
             scratch_shapes=[pltpu.VMEM((tmd, IN_PAD), F32)], compiler_params=_cp(("parallel", "arbitrary"), 48 << 20))(h1, *arrs)
    return dh1, dw


def _norm_mod(xo, shift, scale, g):
    r = lax.rsqrt(jnp.mean(xo * xo, axis=-1, keepdims=True) + EPS)
    return (xo * r) * g * (1.0 + scale) + shift


def res_norm_mod(x, y, gsv, g, nL, name):
    T = x.shape[0]
    has_y = y is not None

    def body(*refs):
        if has_y:
            x_ref, y_ref, gsv_ref, g_ref, xo_ref, h_ref = refs
            xo = x_ref[...] + gsv_ref[0, 0:1, :] * y_ref[...]
            xo_ref[...] = xo
        else:
            x_ref, gsv_ref, g_ref, h_ref = refs
            xo = x_ref[...]
        h_ref[...] = _norm_mod(xo, gsv_ref[0, 1:2, :], gsv_ref[0, 2:3, :], g_ref[...]).astype(h_ref.dtype)

    row = pl.BlockSpec((TR, D), lambda i: (i, 0))
    in_specs = [row] + ([row] if has_y else []) + [pl.BlockSpec((1, 8, D), lambda i: (i // nL, 0, 0)),
                                                     pl.BlockSpec((1, D), lambda i: (0, 0))]
    out_specs = ([row] if has_y else []) + [row]
    out_shape = ([_sds((T, D), F32)] if has_y else []) + [_sds((T, D), BF16)]
    args = (x, y, gsv, g) if has_y else (x, gsv, g)
    outs = _pc(body, name=name, grid=(T // TR,), in_specs=in_specs, out_specs=out_specs, out_shape=out_shape,
               compiler_params=_cp(("arbitrary",), 24 << 20))(*args)
    return (outs[0], outs[1]) if has_y else (None, outs[0])


def res_norm_mod_bwd(xo, y, gsv, g, dh, dres, nL, name):
    T = xo.shape[0]
    has_y = y is not None

    def body(*refs):
        if has_y:
            xo_ref, y_ref, gsv_ref, g_ref, dh_ref, dres_ref, dx_ref, dy_ref, dgsv_ref, dg_ref = refs
        else:
            xo_ref, gsv_ref, g_ref, dh_ref, dres_ref, dx_ref, dgsv_ref, dg_ref = refs
        i = pl.program_id(0)

        @pl.when((i == 0) | (i == nL))
        def _():
            dgsv_ref[...] = jnp.zeros_like(dgsv_ref)

        @pl.when(i == 0)
        def _():
            dg_ref[...] = jnp.zeros_like(dg_ref)

        _, vjp = jax.vjp(_norm_mod, xo_ref[...], gsv_ref[0, 1:2, :], gsv_ref[0, 2:3, :], g_ref[...])
        dxn, dshift, dscale, dg = vjp(dh_ref[...].astype(F32))
        dxo = dres_ref[...] + dxn
        dx_ref[...] = dxo
        if has_y:
            dy_ref[...] = (gsv_ref[0, 0:1, :] * dxo).astype(dy_ref.dtype)
            dgsv_ref[0, 0:1, :] += jnp.sum(y_ref[...] * dxo, axis=0, keepdims=True)
        dgsv_ref[0, 1:2, :] += dshift
        dgsv_ref[0, 2:3, :] += dscale
        dg_ref[0:1, :] += dg

    row = pl.BlockSpec((TR, D), lambda i: (i, 0))
    gspec = pl.BlockSpec((1, 8, D), lambda i: (i // nL, 0, 0))
    in_specs = [row] + ([row] if has_y else []) + [gspec, pl.BlockSpec((1, D), lambda i: (0, 0)), row, row]
    out_specs = [row] + ([row] if has_y else []) + [gspec, pl.BlockSpec((8, D), lambda i: (0, 0))]
    out_shape = [_sds((T, D), F32)] + ([_sds((T, D), BF16)] if has_y else []) + [_sds((2, 8, D), F32), _sds((8, D), F32)]
    args = (xo, y, gsv, g, dh, dres) if has_y else (xo, gsv, g, dh, dres)
    outs = _pc(body, name=name, grid=(T // TR,), in_specs=in_specs, out_specs=out_specs, out_shape=out_shape,
               compiler_params=_cp(("arbitrary",), 32 << 20))(*args)
    if has_y:
        return outs
    return outs[0], None, outs[1], outs[2]


def final_loss(x, y, gsv, g, target, nL, name):
    T = x.shape[0]

    def lossf(xo, gv, t):
        yn = (xo * lax.rsqrt(jnp.mean(xo * xo, axis=-1, keepdims=True) + EPS)) * gv
        e = yn - t
        return 0.5 * jnp.sum(jnp.sum(e * e, axis=-1, keepdims=True) * (1.0 / D), axis=0, keepdims=True)

    def body(x_ref, y_ref, gsv_ref, g_ref, t_ref, loss_ref, dx_ref, dy_ref, dgsv_ref, dg_ref):
        i = pl.program_id(0)

        @pl.when(i == 0)
        def _():
            loss_ref[...] = jnp.zeros_like(loss_ref)
            dg_ref[...] = jnp.zeros_like(dg_ref)

        @pl.when((i == 0) | (i == nL))
        def _():
            dgsv_ref[...] = jnp.zeros_like(dgsv_ref)

        @pl.when(i < nL)
        def _():
            gate = gsv_ref[0, 0:1, :]
            yv = y_ref[...]
            xo = x_ref[...] + gate * yv
            lv, vjp = jax.vjp(lossf, xo, g_ref[...], t_ref[...])
            dxo, dg, _ = vjp(jnp.ones((1, 1), F32))
            loss_ref[...] += jnp.broadcast_to(lv, loss_ref.shape)
            dx_ref[...] = dxo
            dy_ref[...] = (gate * dxo).astype(dy_ref.dtype)
            dgsv_ref[0, 0:1, :] += jnp.sum(yv * dxo, axis=0, keepdims=True)
            dg_ref[0:1, :] += dg

        @pl.when(i >= nL)
        def _():
            dx_ref[...] = jnp.zeros_like(dx_ref)
            dy_ref[...] = jnp.zeros_like(dy_ref)

    row = pl.BlockSpec((TR, D), lambda i: (i, 0))
    gspec = pl.BlockSpec((1, 8, D), lambda i: (i // nL, 0, 0))
    return _pc(
        body, name=name, grid=(T // TR,),
        in_specs=[row, row, gspec, pl.BlockSpec((1, D), lambda i: (0, 0)),
                  pl.BlockSpec((TR, D), lambda i: (jnp.minimum(i, nL - 1), 0))],
        out_specs=[pl.BlockSpec((8, 128), lambda i: (0, 0)), row, row, gspec, pl.BlockSpec((8, D), lambda i: (0, 0))],
        out_shape=[_sds((8, 128), F32), _sds((T, D), F32), _sds((T, D), BF16), _sds((2, 8, D), F32), _sds((8, D), F32)],
        compiler_params=_cp(("arbitrary",), 32 << 20),
    )(x, y, gsv, g, target)


FI_BLK = 2 * D_FF // 4


def _fi_chip(j):
    return (j % 2) * 2 + j // 2


def matmul_fi(a, b, mode, out_dtype, name):
    T = a.shape[0]
    if mode == "tn":
        tmd = 512

        def body(a_ref, b_ref, o_ref):
            o_ref[0] = _dg(a_ref[...], b_ref[...], _DIMS["tn"]).astype(o_ref.dtype)

        return _pc(body, name=name, grid=(D // tmd, 4),
                   in_specs=[pl.BlockSpec((T, tmd), lambda i, j: (0, i)), pl.BlockSpec((T, FI_BLK), lambda i, j: (0, j))],
                   out_specs=pl.BlockSpec((1, tmd, FI_BLK), lambda i, j: (_fi_chip(j), i, 0)),
                   out_shape=_sds((4, D, FI_BLK), out_dtype), compiler_params=_cp(("parallel", "arbitrary"), 48 << 20))(a, b)
    if mode == "nn":
        tm = _div_tile(T, 1280, 16)

        def body(a_ref, b_ref, o_ref):
            o_ref[...] = _dg(a_ref[...], b_ref[0], _DIMS["nn"]).astype(o_ref.dtype)

        return _pc(body, name=name, grid=(T // tm, 4),
                   in_specs=[pl.BlockSpec((tm, D), lambda i, j: (i, 0)), pl.BlockSpec((1, D, FI_BLK), lambda i, j: (_fi_chip(j), 0, 0))],
                   out_specs=pl.BlockSpec((tm, FI_BLK), lambda i, j: (i, j)), out_shape=_sds((T, 4 * FI_BLK), out_dtype),
                   compiler_params=_cp(("parallel", "arbitrary"), 40 << 20))(a, b)
    tm = _div_tile(T, 640, 16)

    def body(a_ref, b_ref, o_ref):
        acc = None
        for k in range(4):
            part = _dg(a_ref[:, k * FI_BLK:(k + 1) * FI_BLK], b_ref[_fi_chip(k)], _DIMS["nt"])
            acc = part if acc is None else acc + part
        o_ref[...] = acc.astype(o_ref.dtype)

    return _pc(body, name=name, grid=(T // tm,),
               in_specs=[pl.BlockSpec((tm, 4 * FI_BLK), lambda i: (i, 0)), pl.BlockSpec((4, D, FI_BLK), lambda i: (0, 0, 0))],
               out_specs=pl.BlockSpec((tm, D), lambda i: (i, 0)), out_shape=_sds((T, D), out_dtype),
               compiler_params=_cp(("parallel",), VMEM_CAP))(a, b)


def _swiglu(gate, up):
    return _silu(gate) * up


def swiglu_fwd(gu, name):
    T = gu.shape[0]

    def body(x_ref, o_ref):
        o_ref[...] = _swiglu(x_ref[:, :FI_BLK].astype(F32), x_ref[:, FI_BLK:].astype(F32)).astype(o_ref.dtype)

    return _pc(body, name=name, grid=(T // TR, 2), in_specs=[pl.BlockSpec((TR, 2 * FI_BLK), lambda i, j: (i, j))],
               out_specs=pl.BlockSpec((TR, FI_BLK), lambda i, j: (i, j)), out_shape=_sds((T, D_FF), BF16),
               compiler_params=_cp(("parallel", "parallel"), 24 << 20))(gu)


def swiglu_bwd(gu, dact, name):
    T = gu.shape[0]

    def body(x_ref, d_ref, o_ref):
        g, u, d = x_ref[:, :FI_BLK].astype(F32), x_ref[:, FI_BLK:].astype(F32), d_ref[...].astype(F32)
        sg = 1.0 / (1.0 + jnp.exp(-g))
        sl = g * sg
        o_ref[:, :FI_BLK] = (d * u * (sg + sl * (1.0 - sg))).astype(o_ref.dtype)
        o_ref[:, FI_BLK:] = (d * sl).astype(o_ref.dtype)

    return _pc(body, name=name, grid=(T // TR, 2),
               in_specs=[pl.BlockSpec((TR, 2 * FI_BLK), lambda i, j: (i, j)), pl.BlockSpec((TR, FI_BLK), lambda i, j: (i, j))],
               out_specs=pl.BlockSpec((TR, 2 * FI_BLK), lambda i, j: (i, j)), out_shape=_sds((T, 2 * D_FF), BF16),
               compiler_params=_cp(("parallel", "parallel"), 32 << 20))(gu, dact)


def rope_tables(L, Lc):
    t = np.arange(L)
    rows, cols = t // GRID_W, t % GRID_W
    inv = ROPE_BASE ** (-np.arange(16, dtype=np.float32) / 16)
    lane = np.arange(64)
    pos = np.where((lane // 32)[None, :] == 0, rows[:, None], cols[:, None]).astype(np.float32)
    ang = jnp.asarray(pos) * jnp.asarray(inv[lane % 16])[None, :]
    cos = jnp.concatenate([jnp.cos(ang), jnp.ones((Lc, 64), F32)], axis=0)
    sin = jnp.concatenate([jnp.sin(ang), jnp.zeros((Lc, 64), F32)], axis=0)
    R = np.zeros((128, 128), np.float32)
    for i in range(128):
        if (i % 32) < 16:
            R[i + 16, i] = -1.0
        else:
            R[i - 16, i] = 1.0
    return jnp.tile(cos, (1, 2)), jnp.tile(sin, (1, 2)), jnp.asarray(R)


def rope_apply(q_src, q_col, k_src, k_col, cos, sin, R, transpose, name, kv_src=None):
    T = cos.shape[0]
    with_kv = kv_src is not None

    def rot(x, c, s, Rm):
        if transpose:
            return x * c + hdot(x * s, Rm, "nt")
        return x * c + hdot(x, Rm) * s

    def body(q_ref, k_ref, c_ref, s_ref, R_ref, *rest):
        qo_ref, ko_ref = rest[-4:-2] if with_kv else rest
        c, s, Rm = c_ref[...], s_ref[...], R_ref[...]
        for j in range(2):
            qo_ref[:, j * 128:(j + 1) * 128] = rot(q_ref[:, j * 128:(j + 1) * 128].astype(F32), c, s, Rm).astype(qo_ref.dtype)
        ko_ref[...] = rot(k_ref[...].astype(F32), c, s, Rm).astype(ko_ref.dtype)
        if with_kv:
            rest[-2][...] = rest[0][...].astype(BF16)
            rest[-1][...] = rest[1][...].astype(BF16)

    tab = pl.BlockSpec((TR, 128), lambda i: (i, 0))
    wide = pl.BlockSpec((TR, 256), lambda i: (i, 0))
    kv_in = [pl.BlockSpec((TR, 256), lambda i: (i, C_KB // 256)), pl.BlockSpec((TR, 256), lambda i: (i, C_VB // 256))] if with_kv else []
    return _pc(body, name=name, grid=(T // TR,),
               in_specs=[pl.BlockSpec((TR, 256), lambda i: (i, q_col)), pl.BlockSpec((TR, 128), lambda i: (i, k_col)),
                         tab, tab, pl.BlockSpec((128, 128), lambda i: (0, 0))] + kv_in,
               out_specs=[wide, tab] + ([wide, wide] if with_kv else []),
               out_shape=[_sds((T, 256), BF16), _sds((T, 128), BF16)] + ([_sds((T, 256), BF16)] * 2 if with_kv else []),
               compiler_params=_cp(("parallel",), 16 << 20))(q_src, k_src, cos, sin, R, *([kv_src, kv_src] if with_kv else []))


_SCALE = HD ** -0.5


def _attn_tile(qh, ks, vs, extra):
    ss = []
    for k, add in ks:
        s = _dg(qh, k, _DIMS["nt"]) * _SCALE
        ss.append(s if add is None else s + add)
    m = ss[0].max(axis=-1, keepdims=True)
    for s in ss[1:]:
        m = jnp.maximum(m, s.max(axis=-1, keepdims=True))
    if extra is not None:
        m = jnp.maximum(m, extra)
    ps = [jnp.exp(s - m) for s in ss]
    den = ps[0].sum(axis=-1, keepdims=True)
    for p in ps[1:]:
        den = den + p.sum(axis=-1, keepdims=True)
    if extra is not None:
        den = den + jnp.exp(extra - m)
    num = _dg(ps[0], vs[0], _DIMS["nn"])
    for p, v in zip(ps[1:], vs[1:]):
        num = num + _dg(p, v, _DIMS["nn"])
    linv = 1.0 / den
    return num * linv, m, linv


def _attn_bwd_tile(qh, ks, vs, extra, m, linv, oh, doh):
    delta = jnp.sum(doh * oh, axis=-1, keepdims=True)
    dq = None
    dks, dvs, dss = [], [], []
    for (k, add), v in zip(ks, vs):
        s = _dg(qh, k, _DIMS["nt"]) * _SCALE
        if add is not None:
            s = s + add
        p = jnp.exp(s - m) * linv
        dvs.append(_dg(p, doh, _DIMS["tn"]))
        ds = p * (_dg(doh, v, _DIMS["nt"]) - delta)
        dss.append(ds)
        dsq = ds * _SCALE
        part = _dg(dsq, k, _DIMS["nn"])
        dq = part if dq is None else dq + part
        dks.append(_dg(dsq, qh, _DIMS["tn"]))
    dextra = None
    if extra is not None:
        dextra = -(jnp.exp(extra - m) * linv * delta)
    return dq, dks, dvs, dss, dextra


def _wa_mask(n, L):
    qpos = n * WA_BLK + lax.broadcasted_iota(jnp.int32, (WA_BLK, 3 * WA_BLK), 0)
    kpos = (n - 1) * WA_BLK + lax.broadcasted_iota(jnp.int32, (WA_BLK, 3 * WA_BLK), 1)
    ok = (jnp.abs(qpos - kpos) <= WA_BLK) & (kpos >= 0) & (kpos < L)
    return jnp.where(ok, 0.0, NEG).astype(F32)


WA_BPS = 2
_WA_PAIRS = (((0, 0), (1, 3), False), ((1, 2), (0, 1), True))


def _swap_halves_lanes(a):
    return jnp.concatenate([a[:, HD:], a[:, :HD]], axis=1)


def _wa_specs(L, Lc):
    nb = L // WA_BLK
    cb = L // Lc

    def blk(j):
        return pl.BlockSpec((WA_BLK, 128), lambda s: (jnp.clip(s * WA_BPS - 1 + j, 0, nb - 1), 0))

    return nb, [blk(j) for j in range(WA_BPS + 2)] + [pl.BlockSpec((Lc, 128), lambda s: (cb, 0))]


def _wa_pair_q(q_ref, qs, lo, hi):
    a = q_ref[qs, lo[0] * 128:(lo[0] + 1) * 128]
    b = q_ref[qs, hi[0] * 128:(hi[0] + 1) * 128]
    lane = lax.broadcasted_iota(jnp.int32, a.shape, 1)
    zero = jnp.zeros_like(a)
    return jnp.concatenate([jnp.where(lane < HD, a, zero), jnp.where(lane >= HD, b, zero)], axis=0)


def _wa_pair_vec(ref, qs, lo, hi, base=0):
    return jnp.concatenate([ref[qs, base + lo[1]:base + lo[1] + 1], ref[qs, base + hi[1]:base + hi[1] + 1]], axis=0)


def _wa_pair_sink(s_ref, n, lo, hi):
    return jnp.concatenate([jnp.broadcast_to(s_ref[lo[1]:lo[1] + 1, 0:1], (n, 1)), jnp.broadcast_to(s_ref[hi[1]:hi[1] + 1, 0:1], (n, 1))], axis=0)


def win_attn_fwd(qr, kr, krs, v, vs, sink, L, Lc, name):
    T = L + Lc
    nb, specs = _wa_specs(L, Lc)
    nk = WA_BPS + 2
    QB = WA_BPS * WA_BLK
    nlat = nb // WA_BPS

    def body(q_ref, *refs):
        groups = [refs[g * (nk + 1):(g + 1) * (nk + 1)] for g in range(4)]
        s_ref, o_ref, st_ref = refs[-3], refs[-2], refs[-1]
        s = pl.program_id(0)

        def run(qs, n, ks_of, vs_of):
            outs = []
            for lo, hi, swapped in _WA_PAIRS:
                kb, vb = groups[1 if swapped else 0], groups[3 if swapped else 2]
                o2, m2, l2 = _attn_tile(_wa_pair_q(q_ref, qs, lo, hi), ks_of(kb), vs_of(vb), _wa_pair_sink(s_ref, n, lo, hi))
                outs.append(o2)
                for r, (_, h) in enumerate((lo, hi)):
                    st_ref[qs, h:h + 1] = m2[r * n:(r + 1) * n]
                    st_ref[qs, WA_HEADS + h:WA_HEADS + h + 1] = l2[r * n:(r + 1) * n]
            lane = lax.broadcasted_iota(jnp.int32, (n, 128), 1)
            o_ref[qs, 0:128] = jnp.where(lane < HD, outs[0][:n], outs[1][n:]).astype(o_ref.dtype)
            o_ref[qs, 128:256] = jnp.where(lane < HD, outs[1][:n], outs[0][n:]).astype(o_ref.dtype)

        @pl.when(s < nlat)
        def _():
            for b in range(WA_BPS):
                m1 = _wa_mask(s * WA_BPS + b, L)
                mask = jnp.concatenate([m1, m1], axis=0)
                cat = lambda g: jnp.concatenate([g[b + j][...] for j in range(3)], axis=0)
                run(slice(b * WA_BLK, (b + 1) * WA_BLK), WA_BLK,
                    lambda kb: [(cat(kb), mask), (kb[nk][...], None)], lambda vb: [cat(vb), vb[nk][...]])

        @pl.when(s >= nlat)
        def _():
            run(slice(None), QB, lambda kb: [(kb[nk][...], None)], lambda vb: [vb[nk][...]])

    qspec = pl.BlockSpec((QB, 256), lambda s: (s, 0))
    return _pc(body, name=name, grid=(T // QB,),
               in_specs=[qspec] + specs * 4 + [pl.BlockSpec((8, 128), lambda s: (0, 0))],
               out_specs=[qspec, pl.BlockSpec((QB, 8), lambda s: (s, 0))], out_shape=[_sds((T, 256), BF16), _sds((T, 8), F32)],
               compiler_params=_cp(("arbitrary",), 40 << 20))(qr, *([kr] * (nk + 1)), *([krs] * (nk + 1)), *([v] * (nk + 1)), *([vs] * (nk + 1)), sink)


def win_attn_bwd(qr, kr, krs, v, vs, sink, do_src, o, stats, L, Lc, name):
    T = L + Lc
    nb, specs = _wa_specs(L, Lc)
    nk = WA_BPS + 2
    QB = WA_BPS * WA_BLK
    nlat = nb // WA_BPS
    cx = WA_BLK + L

    def body(q_ref, *refs):
        groups = [refs[g * (nk + 1):(g + 1) * (nk + 1)] for g in range(4)]
        s_ref, do_ref, o_ref, st_ref, dq_ref, dk_ref, dks_ref, dv_ref, dvs_ref, ds_ref = refs[4 * (nk + 1):]
        s = pl.program_id(0)

        @pl.when(s == 0)
        def _():
            for r in (dk_ref, dks_ref, dv_ref, dvs_ref, ds_ref):
                r[...] = jnp.zeros_like(r)

        def run(qs, n, ks_of, vs_of, rows):
            lane = lax.broadcasted_iota(jnp.int32, (n, 128), 1)
            dqs = []
            for lo, hi, swapped in _WA_PAIRS:
                kb, vb = groups[1 if swapped else 0], groups[3 if swapped else 2]
                dka, dva = (dks_ref, dvs_ref) if swapped else (dk_ref, dv_ref)
                pair = lambda ref: jnp.concatenate([jnp.where(lane < HD, ref[qs, lo[0] * 128:(lo[0] + 1) * 128].astype(F32), 0.0),
                                                    jnp.where(lane >= HD, ref[qs, hi[0] * 128:(hi[0] + 1) * 128].astype(F32), 0.0)], axis=0)
                dq2, dks, dvs, _, dex = _attn_bwd_tile(_wa_pair_q(q_ref, qs, lo, hi), ks_of(kb), vs_of(vb), _wa_pair_sink(s_ref, n, lo, hi),
                                                       _wa_pair_vec(st_ref, qs, lo, hi), _wa_pair_vec(st_ref, qs, lo, hi, WA_HEADS), pair(o_ref), pair(do_ref))
                dqs.append(dq2)
                for r, (_, h) in enumerate((lo, hi)):
                    ds_ref[h:h + 1, :] += jnp.broadcast_to(jnp.sum(dex[r * n:(r + 1) * n], axis=0, keepdims=True), (1, 128))
                if rows is not None:
                    dka[rows, :] += dks[0]
                    dva[rows, :] += dvs[0]
                dka[cx:cx + Lc, :] += dks[-1]
                dva[cx:cx + Lc, :] += dvs[-1]
            dq_ref[qs, 0:128] = jnp.where(lane < HD, dqs[0][:n], dqs[1][n:])
            dq_ref[qs, 128:256] = jnp.where(lane < HD, dqs[1][:n], dqs[0][n:])

        @pl.when(s < nlat)
        def _():
            for b in range(WA_BPS):
                nblk = s * WA_BPS + b
                m1 = _wa_mask(nblk, L)
                mask = jnp.concatenate([m1, m1], axis=0)
                cat = lambda g: jnp.concatenate([g[b + j][...] for j in range(3)], axis=0)
                run(slice(b * WA_BLK, (b + 1) * WA_BLK), WA_BLK, lambda kb: [(cat(kb), mask), (kb[nk][...], None)],
                    lambda vb: [cat(vb), vb[nk][...]], pl.ds(pl.multiple_of(nblk * WA_BLK, WA_BLK), 3 * WA_BLK))

        @pl.when(s >= nlat)
        def _():
            run(slice(None), QB, lambda kb: [(kb[nk][...], None)], lambda vb: [vb[nk][...]], None)

    qspec = pl.BlockSpec((QB, 256), lambda s: (s, 0))
    acc_spec = pl.BlockSpec((T + 2 * WA_BLK, 128), lambda s: (0, 0))
    acc_shape = _sds((T + 2 * WA_BLK, 128), F32)
    return _pc(body, name=name, grid=(T // QB,),
               in_specs=[qspec] + specs * 4 + [pl.BlockSpec((8, 128), lambda s: (0, 0)), qspec, qspec, pl.BlockSpec((QB, 8), lambda s: (s, 0))],
               out_specs=[qspec, acc_spec, acc_spec, acc_spec, acc_spec, pl.BlockSpec((8, 128), lambda s: (0, 0))],
               out_shape=[_sds((T, 256), F32), acc_shape, acc_shape, acc_shape, acc_shape, _sds((8, 128), F32)],
               compiler_params=_cp(("arbitrary",), 48 << 20))(qr, *([kr] * (nk + 1)), *([krs] * (nk + 1)), *([v] * (nk + 1)), *([vs] * (nk + 1)),
                                                              sink, do_src, o, stats)


def na_index_tables():
    qc = np.arange(GRID_W)[:, None]
    kc = np.arange(GRID_W)[None, :]
    cstart = np.clip(qc - NA_KW // 2, 0, GRID_W - NA_KW)
    ok = (kc >= cstart) & (kc < cstart + NA_KW)
    dx = np.clip(kc - qc, -(NA_KW - 1), NA_KW - 1) + (NA_KW - 1)
    off = np.arange(NA_KH)[:, None]
    kr = np.arange(NA_KH)[None, :]
    dy = kr - off + (NA_KH - 1)
    return ok, dx, dy


def _na_selectors():
    ok, dx, dy = na_index_tables()
    e1 = np.zeros((GRID_W * GRID_W, 128), np.float32)
    qi, ki = np.nonzero(ok)
    e1[qi * GRID_W + ki, dx[qi, ki]] = 1.0
    e2 = np.zeros((16, NA_KH * NA_KH), np.float32)
    oi, ri = np.meshgrid(np.arange(NA_KH), np.arange(NA_KH), indexing="ij")
    e2[dy[oi, ri].ravel(), (oi * NA_KH + ri).ravel()] = 1.0
    return ok, jnp.asarray(e1), jnp.asarray(np.kron(np.eye(NA_HEADS, dtype=np.float32), e2))


def na_bias_table(rpb, tag):
    ok, e1, e2 = _na_selectors()
    r2 = jnp.pad(rpb.astype(F32), ((0, 0), (0, 1), (0, 128 - (2 * NA_KW - 1)))).reshape(NA_HEADS * 16, 128)
    r1 = matmul(e2, r2, "tn", F32, f"na_bias_sel1_{tag}", hi=True)
    x = matmul(r1, e1, "nt", F32, f"na_bias_sel2_{tag}", hi=True)
    b = x.reshape(NA_HEADS, NA_KH, NA_KH, GRID_W, GRID_W).transpose(0, 1, 3, 2, 4)
    b = b + jnp.asarray(np.where(ok, 0.0, NEG).astype(np.float32))[None, None, :, None, :]
    return b.reshape(NA_HEADS, NA_KH, GRID_W, NA_KH * GRID_W)


def _na_rows(r, GR):
    r0 = jnp.clip(r - NA_KH // 2, 0, GR - NA_KH)
    return r0, jnp.clip(r - r0, 0, NA_KH - 1)


NA_RPS = 4


def _pair_rows(x):
    lane = lax.broadcasted_iota(jnp.int32, x.shape, 1)
    zero = jnp.zeros_like(x)
    return jnp.concatenate([jnp.where(lane < HD, x, zero), jnp.where(lane >= HD, x, zero)], axis=0)


def _unpair_rows(x2):
    n = x2.shape[0] // 2
    lane = lax.broadcasted_iota(jnp.int32, (n, 128), 1)
    return jnp.where(lane < HD, x2[:n], x2[n:])


def na_fwd(P, kb, vb, bias, L, Lc, name):
    T = L + Lc
    GR = L // GRID_W
    W = NA_KH * GRID_W
    QB = GRID_W * NA_RPS
    nlat = GR // NA_RPS

    def body(q_ref, k_ref, v_ref, b_ref, o_ref, st_ref):
        s = pl.program_id(0)

        def put(qs, p, res):
            o2, m2, l2 = res
            n = o2.shape[0] // 2
            o_ref[qs, p * 128:(p + 1) * 128] = _unpair_rows(o2).astype(o_ref.dtype)
            for r in range(2):
                st_ref[qs, 2 * p + r:2 * p + r + 1] = m2[r * n:(r + 1) * n]
                st_ref[qs, NA_HEADS + 2 * p + r:NA_HEADS + 2 * p + r + 1] = l2[r * n:(r + 1) * n]

        @pl.when(s < nlat)
        def _():
            for rr in range(NA_RPS):
                r0, off = _na_rows(s * NA_RPS + rr, GR)
                rows = pl.ds(pl.multiple_of(r0 * GRID_W, GRID_W), W)
                qs = slice(rr * GRID_W, (rr + 1) * GRID_W)
                for p in range(NA_HEADS // 2):
                    ps = slice(p * 128, (p + 1) * 128)
                    b2 = jnp.concatenate([b_ref[2 * p, off], b_ref[2 * p + 1, off]], axis=0)
                    put(qs, p, _attn_tile(_pair_rows(q_ref[qs, ps]), [(k_ref[rows, ps], b2), (k_ref[L:T, ps], None)],
                                          [v_ref[rows, ps], v_ref[L:T, ps]], None))

        @pl.when(s >= nlat)
        def _():
            for p in range(NA_HEADS // 2):
                ps = slice(p * 128, (p + 1) * 128)
                put(slice(None), p, _attn_tile(_pair_rows(q_ref[:, ps]), [(k_ref[L:T, ps], None)], [v_ref[L:T, ps]], None))

    one = pl.Buffered(1)
    return _pc(body, name=name, grid=(T // QB,),
               in_specs=[pl.BlockSpec((QB, 256), lambda r: (r, C_QB // 256)),
                         pl.BlockSpec((T, 256), lambda r: (0, 0), pipeline_mode=one),
                         pl.BlockSpec((T, 256), lambda r: (0, 0), pipeline_mode=one),
                         pl.BlockSpec((NA_HEADS, NA_KH, GRID_W, W), lambda r: (0, 0, 0, 0), pipeline_mode=one)],
               out_specs=[pl.BlockSpec((QB, 256), lambda r: (r, 0)), pl.BlockSpec((QB, 8), lambda r: (r, 0))],
               out_shape=[_sds((T, 256), BF16), _sds((T, 8), F32)],
               compiler_params=_cp(("arbitrary",), 32 << 20))(P, kb, vb, bias)


def na_bwd(P, kb, vb, bias, do_src, o, stats, L, Lc, name):
    T = L + Lc
    GR = L // GRID_W
    W = NA_KH * GRID_W
    QB = GRID_W * NA_RPS
    nlat = GR // NA_RPS

    def body(q_ref, k_ref, v_ref, b_ref, do_ref, o_ref, st_ref, dq_ref, dk_ref, dv_ref, db_ref):
        s = pl.program_id(0)

        @pl.when(s == 0)
        def _():
            dk_ref[...] = jnp.zeros_like(dk_ref)
            dv_ref[...] = jnp.zeros_like(dv_ref)
            db_ref[...] = jnp.zeros_like(db_ref)

        def tile(qs, p, ks, vs):
            ps = slice(p * 128, (p + 1) * 128)
            m2 = jnp.concatenate([st_ref[qs, 2 * p:2 * p + 1], st_ref[qs, 2 * p + 1:2 * p + 2]], axis=0)
            l2 = jnp.concatenate([st_ref[qs, NA_HEADS + 2 * p:NA_HEADS + 2 * p + 1], st_ref[qs, NA_HEADS + 2 * p + 1:NA_HEADS + 2 * p + 2]], axis=0)
            dq2, dks, dvs, dss, _ = _attn_bwd_tile(_pair_rows(q_ref[qs, ps]), ks, vs, None, m2, l2,
                                                   _pair_rows(o_ref[qs, ps].astype(F32)), _pair_rows(do_ref[qs, ps].astype(F32)))
            dq_ref[qs, ps] = _unpair_rows(dq2).astype(dq_ref.dtype)
            return dks, dvs, dss

        @pl.when(s < nlat)
        def _():
            for rr in range(NA_RPS):
                r0, off = _na_rows(s * NA_RPS + rr, GR)
                rows = pl.ds(pl.multiple_of(r0 * GRID_W, GRID_W), W)
                qs = slice(rr * GRID_W, (rr + 1) * GRID_W)
                for p in range(NA_HEADS // 2):
                    ps = slice(p * 128, (p + 1) * 128)
                    b2 = jnp.concatenate([b_ref[2 * p, off], b_ref[2 * p + 1, off]], axis=0)
                    dks, dvs, dss = tile(qs, p, [(k_ref[rows, ps], b2), (k_ref[L:T, ps], None)], [v_ref[rows, ps], v_ref[L:T, ps]])
                    dk_ref[rows, ps] += dks[0]
                    dv_ref[rows, ps] += dvs[0]
                    dk_ref[L:T, ps] += dks[1]
                    dv_ref[L:T, ps] += dvs[1]
                    db_ref[2 * p, off] += dss[0][:GRID_W]
                    db_ref[2 * p + 1, off] += dss[0][GRID_W:]

        @pl.when(s >= nlat)
        def _():
            for p in range(NA_HEADS // 2):
                ps = slice(p * 128, (p + 1) * 128)
                dks, dvs, _ = tile(slice(None), p, [(k_ref[L:T, ps], None)], [v_ref[L:T, ps]])
                dk_ref[L:T, ps] += dks[0]
                dv_ref[L:T, ps] += dvs[0]

    one = pl.Buffered(1)
    full = lambda shape: pl.BlockSpec(shape, lambda r: (0,) * len(shape), pipeline_mode=one)
    qspec = pl.BlockSpec((QB, 256), lambda r: (r, 0))
    return _pc(body, name=name, grid=(T // QB,),
               in_specs=[pl.BlockSpec((QB, 256), lambda r: (r, C_QB // 256)), full((T, 256)), full((T, 256)),
                         full((NA_HEADS, NA_KH, GRID_W, W)), pl.BlockSpec((QB, 256), lambda r: (r, 1)), qspec, pl.BlockSpec((QB, 8), lambda r: (r, 0))],
               out_specs=[qspec, full((T, 256)), full((T, 256)), full((NA_HEADS, NA_KH, GRID_W, W))],
               out_shape=[_sds((T, 256), BF16), _sds((T, 256), F32), _sds((T, 256), F32), _sds((NA_HEADS, NA_KH, GRID_W, W), F32)],
               compiler_params=_cp(("arbitrary",), 48 << 20))(P, kb, vb, bias, do_src, o, stats)


def na_rpb_grad(dbias, tag):
    _, e1, e2 = _na_selectors()
    x = dbias.reshape(NA_HEADS, NA_KH, GRID_W, NA_KH, GRID_W).transpose(0, 1, 3, 2, 4).reshape(NA_HEADS * NA_KH * NA_KH, GRID_W * GRID_W)
    r1 = matmul(x, e1, "nn", F32, f"na_rpb_sel1_{tag}", hi=True, tk=1024)
    r2 = matmul(e2, r1, "nn", F32, f"na_rpb_sel2_{tag}", hi=True)
    return r2.reshape(NA_HEADS, 16, 128)[:, :2 * NA_KH - 1, :2 * NA_KW - 1]


_HALO = 8
CONV_CB = 4
CONV_RB = 32


def _halo_specs(T, col0):
    nh = TR // _HALO
    specs = []
    for j in range(CONV_CB):
        specs.append(pl.BlockSpec((_HALO, 256), lambda i, j=j: (jnp.maximum(i * nh - 1, 0), col0 + j)))
        specs.append(pl.BlockSpec((TR, 256), lambda i, j=j: (i, col0 + j)))
        specs.append(pl.BlockSpec((_HALO, 256), lambda i, j=j: (jnp.minimum((i + 1) * nh, T // _HALO - 1), col0 + j)))
    return specs


def _fill_ext(ext, prv, cur, nxt, i, nL, nT):
    has_prev = jnp.where((i != 0) & (i != nL), 1.0, 0.0)
    has_next = jnp.where((i != nL - 1) & (i != nT - 1), 1.0, 0.0)
    ext[0:_HALO, :] = prv[...].astype(F32) * has_prev
    ext[_HALO:_HALO + TR, :] = cur[...].astype(F32)
    ext[_HALO + TR:, :] = nxt[...].astype(F32) * has_next


def conv_silu_fwd(P, w8, b, nL, name):
    T = P.shape[0]
    nT = T // TR

    def body(*refs):
        xin, (w_ref, b_ref, pre_ref, act_ref, ext) = refs[:3 * CONV_CB], refs[3 * CONV_CB:]
        i = pl.program_id(0)
        for j in range(CONV_CB):
            cs = slice(j * 256, (j + 1) * 256)
            _fill_ext(ext, *xin[3 * j:3 * j + 3], i, nL, nT)
            for r in range(0, TR, CONV_RB):
                y = jnp.broadcast_to(b_ref[:, cs], (CONV_RB, 256))
                for k in range(S_CONV):
                    y = y + w_ref[k:k + 1, cs] * ext[pl.ds(_HALO - S_CONV // 2 + k + r, CONV_RB), :]
                pre_ref[r:r + CONV_RB, cs] = y
                act_ref[r:r + CONV_RB, cs] = _silu(y)

    out = pl.BlockSpec((TR, 1024), lambda i: (i, 0))
    return _pc(body, name=name, grid=(nT,),
               in_specs=_halo_specs(T, C_XBC // 256) + [pl.BlockSpec((8, 1024), lambda i: (0, 0)), pl.BlockSpec((1, 1024), lambda i: (0, 0))],
               out_specs=[out, out], out_shape=[_sds((T, 1024), F32), _sds((T, 1024), F32)],
               scratch_shapes=[pltpu.VMEM((TR + 2 * _HALO, 256), F32)],
               compiler_params=_cp(("parallel",), 24 << 20))(*([P] * (3 * CONV_CB)), w8, b)


def dsilu(pre, dxs_list, db_list, dc_list, name):
    T = pre.shape[0]
    n1, n2, n3 = len(dxs_list), len(db_list), len(dc_list)

    def body(*refs):
        pre_ref = refs[0]
        ins = refs[1:1 + n1 + n2 + n3]
        out = refs[-1]

        def part(rs, lo, hi):
            g = rs[0][...].astype(F32)
            for r in rs[1:]:
                g = g + r[...].astype(F32)
            x = pre_ref[:, lo:hi]
            sg = 1.0 / (1.0 + jnp.exp(-x))
            sl = x * sg
            out[:, lo:hi] = g * (sg + sl * (1.0 - sg))

        part(ins[:n1], 0, 512)
        part(ins[n1:n1 + n2], 512, 768)
        part(ins[n1 + n2:], 768, 1024)

    spec = lambda w: pl.BlockSpec((TR, w), lambda i: (i, 0))
    return _pc(body, name=name, grid=(T // TR,),
               in_specs=[spec(1024)] + [spec(512)] * n1 + [spec(256)] * (n2 + n3),
               out_specs=spec(1024), out_shape=_sds((T, 1024), F32),
               compiler_params=_cp(("parallel",), 32 << 20))(pre, *dxs_list, *db_list, *dc_list)


def conv_bwd(dpre, P, w8, nL, name):
    T = P.shape[0]
    nT = T // TR

    def body(*refs):
        din, xin, (w_ref, dx_ref, dw_ref, db_ref, extd) = refs[:3 * CONV_CB], refs[3 * CONV_CB:4 * CONV_CB], refs[4 * CONV_CB:]
        i = pl.program_id(0)

        @pl.when(i == 0)
        def _():
            dw_ref[...] = jnp.zeros_like(dw_ref)
            db_ref[...] = jnp.zeros_like(db_ref)

        fold = lambda a: functools.reduce(lambda p, q: p + q, [a[q:q + 8] for q in range(0, CONV_RB, 8)])
        for j in range(CONV_CB):
            cs = slice(j * 256, (j + 1) * 256)
            _fill_ext(extd, *din[3 * j:3 * j + 3], i, nL, nT)
            dws = [jnp.zeros((8, 256), F32) for _ in range(S_CONV)]
            dbs = jnp.zeros((8, 256), F32)
            for r in range(0, TR, CONV_RB):
                x = xin[j][r:r + CONV_RB, :]
                dx = jnp.zeros((CONV_RB, 256), F32)
                for k in range(S_CONV):
                    sd = extd[pl.ds(_HALO + S_CONV // 2 - k + r, CONV_RB), :]
                    dx = dx + w_ref[k:k + 1, cs] * sd
                    dws[k] = dws[k] + fold(sd * x)
                dx_ref[r:r + CONV_RB, cs] = dx.astype(dx_ref.dtype)
                dbs = dbs + fold(din[3 * j + 1][r:r + CONV_RB, :])
            for k in range(S_CONV):
                dw_ref[k:k + 1, cs] += jnp.sum(dws[k], axis=0, keepdims=True)
            db_ref[0:1, cs] += jnp.sum(dbs, axis=0, keepdims=True)

    acc = pl.BlockSpec((8, 1024), lambda i: (0, 0))
    xspecs = [pl.BlockSpec((TR, 256), lambda i, j=j: (i, C_XBC // 256 + j)) for j in range(CONV_CB)]
    return _pc(body, name=name, grid=(nT,),
               in_specs=_halo_specs(T, 0) + xspecs + [acc],
               out_specs=[pl.BlockSpec((TR, 1024), lambda i: (i, 0)), acc, acc],
               out_shape=[_sds((T, 1024), BF16), _sds((8, 1024), F32), _sds((8, 1024), F32)],
               scratch_shapes=[pltpu.VMEM((TR + 2 * _HALO, 256), F32)],
               compiler_params=_cp(("arbitrary",), 24 << 20))(*([dpre] * (3 * CONV_CB)), *([P] * CONV_CB), w8)


def _onehot_row(h, n):
    return (lax.broadcasted_iota(jnp.int32, (1, n), 1) == h).astype(F32)


def _onehot_col(h, n):
    return (lax.broadcasted_iota(jnp.int32, (n, 1), 0) == h).astype(F32)


S_PAIRS = S_HEADS // 2


def _ssd_chunk(xs, dtr, dtb, alog, bm, cm, hin, reverse):
    Qn = S_Q
    ii = lax.broadcasted_iota(jnp.int32, (Qn, Qn), 0)
    jj = lax.broadcasted_iota(jnp.int32, (Qn, Qn), 1)
    keep = (ii <= jj) if reverse else (ii >= jj)
    tri = keep.astype(F32)
    triT = ((jj <= ii) if reverse else (jj >= ii)).astype(F32)
    eye = (ii == jj).astype(F32)
    low = jj < S_P
    top = ii < S_P
    dt = _softplus(dtr + dtb)
    a = dt * (-jnp.exp(alog))
    cs = hdot(tri, a)
    csT = hdot(a, triT, "tn")
    dtT = hdot(dt, eye, "tn")
    last = _onehot_row(0 if reverse else Qn - 1, Qn)
    ys, houts = [], []
    for p in range(S_PAIRS):
        g = p // (S_PAIRS // S_GROUPS)
        if p % (S_PAIRS // S_GROUPS) == 0:
            G = bdot(cm[g], bm[g], "nt")
        per_head = []
        for h in (2 * p, 2 * p + 1):
            eh_r, eh_c = _onehot_row(h, S_HEADS), _onehot_col(h, S_HEADS)
            cs_c = jnp.sum(cs * eh_r, axis=1, keepdims=True)
            dt_c = jnp.sum(dt * eh_r, axis=1, keepdims=True)
            cs_r = jnp.sum(csT * eh_c, axis=0, keepdims=True)
            dt_r = jnp.sum(dtT * eh_c, axis=0, keepdims=True)
            tot = jnp.sum(cs_r * last, axis=1, keepdims=True)
            w = G * jnp.exp(jnp.where(keep, cs_c - cs_r, NEG)) * dt_r
            per_head.append((bdot(w, xs[p], "nn"), jnp.exp(cs_c), jnp.exp(tot - cs_c) * dt_c, jnp.exp(tot)))
        (y0, e0, f0, d0), (y1, e1, f1, d1) = per_head
        y = jnp.where(low, y0, y1) + bdot(cm[g], hin[p], "nt") * jnp.where(low, e0, e1)
        hout = hin[p] * jnp.where(top, d0, d1) + bdot(xs[p] * jnp.where(low, f0, f1), bm[g], "tn")
        ys.append(y)
        houts.append(hout)
    return ys, houts


def _ssd_orders(L, Lc):
    nl, ncx = L // S_Q, Lc // S_Q
    fwd = lambda s: jnp.where(s < ncx, nl + s, s - ncx)
    bwd = lambda s: nl + ncx - 1 - s
    return nl + ncx, fwd, bwd


def _ssd_in_specs(fo, bo, step):
    def at(order, w, col):
        return pl.BlockSpec((S_Q, w), lambda u: (order(step(u)), col))
    specs = []
    for order in (fo, bo):
        specs += [at(order, 512, 0), at(order, 256, 2), at(order, 256, 3), at(order, 128, C_DT // 128)]
    return specs


def ssd_fwd(act, P, dtb, alog, L, Lc, name):
    T = L + Lc
    ns, fo, bo = _ssd_orders(L, Lc)

    def body(xf, bf, cf, df, xb, bb, cb, db, dtb_ref, al_ref, yf, yb, hsf, hsb, Hf, Hb):
        s = pl.program_id(0)

        @pl.when(s == 0)
        def _():
            Hf[...] = jnp.zeros_like(Hf)
            Hb[...] = jnp.zeros_like(Hb)

        for d, (x_r, b_r, c_r, dt_r, y_r, hs_r, H) in enumerate(((xf, bf, cf, df, yf, hsf, Hf), (xb, bb, cb, db, yb, hsb, Hb))):
            hin = [H[p] for p in range(S_PAIRS)]
            hs_r[0] = H[...]
            ys, houts = _ssd_chunk(
                [x_r[:, p * 128:(p + 1) * 128] for p in range(S_PAIRS)], dt_r[:, d * 8:(d + 1) * 8],
                dtb_ref[d:d + 1, 0:8], al_ref[d:d + 1, 0:8],
                [b_r[:, g * S_N:(g + 1) * S_N] for g in range(S_GROUPS)], [c_r[:, g * S_N:(g + 1) * S_N] for g in range(S_GROUPS)],
                hin, reverse=(d == 1))
            for p in range(S_PAIRS):
                y_r[:, p * 128:(p + 1) * 128] = ys[p]
                H[p] = houts[p]

    ident = lambda u: u
    small = pl.BlockSpec((8, 128), lambda u: (0, 0))
    hspec = pl.BlockSpec((1, S_PAIRS, 2 * S_P, S_N), lambda u: (u, 0, 0, 0))
    return _pc(body, name=name, grid=(ns,),
               in_specs=_ssd_in_specs(fo, bo, ident) + [small, small],
               out_specs=[pl.BlockSpec((S_Q, 512), lambda u: (fo(u), 0)), pl.BlockSpec((S_Q, 512), lambda u: (bo(u), 0)), hspec, hspec],
               out_shape=[_sds((T, 512), F32), _sds((T, 512), F32), _sds((ns, S_PAIRS, 2 * S_P, S_N), F32), _sds((ns, S_PAIRS, 2 * S_P, S_N), F32)],
               scratch_shapes=[pltpu.VMEM((S_PAIRS, 2 * S_P, S_N), F32), pltpu.VMEM((S_PAIRS, 2 * S_P, S_N), F32)],
               compiler_params=_cp(("arbitrary",), 32 << 20))(act, act, act, P, act, act, act, P, dtb, alog)


def ssd_bwd(act, P, dtb, alog, hsf, hsb, dy, L, Lc, name):
    T = L + Lc
    ns, fo, bo = _ssd_orders(L, Lc)
    step = lambda u: ns - 1 - u

    def body(xf, bf, cf, df, xb, bb, cb, db, dtb_ref, al_ref, hsf_r, hsb_r, dyf, dyb,
             dxf, dbf, dcf, ddf, dxb, dbb, dcb, ddb, ddtb, dal, dHf, dHb):
        u = pl.program_id(0)

        @pl.when(u == 0)
        def _():
            dHf[...] = jnp.zeros_like(dHf)
            dHb[...] = jnp.zeros_like(dHb)
            ddtb[...] = jnp.zeros_like(ddtb)
            dal[...] = jnp.zeros_like(dal)

        dirs = ((xf, bf, cf, df, hsf_r, dyf, dxf, dbf, dcf, ddf, dHf), (xb, bb, cb, db, hsb_r, dyb, dxb, dbb, dcb, ddb, dHb))
        for d, (x_r, b_r, c_r, dt_r, hs_r, dy_r, dx_o, db_o, dc_o, dd_o, dH) in enumerate(dirs):
            f = functools.partial(_ssd_chunk, reverse=(d == 1))
            _, vjp = jax.vjp(
                f, [x_r[:, p * 128:(p + 1) * 128] for p in range(S_PAIRS)], dt_r[:, d * 8:(d + 1) * 8],
                dtb_ref[d:d + 1, 0:8], al_ref[d:d + 1, 0:8],
                [b_r[:, g * S_N:(g + 1) * S_N] for g in range(S_GROUPS)], [c_r[:, g * S_N:(g + 1) * S_N] for g in range(S_GROUPS)],
                [hs_r[0, p] for p in range(S_PAIRS)])
            gx, gdt, gdtb, gal, gb, gc, gh = vjp(([dy_r[:, p * 128:(p + 1) * 128] for p in range(S_PAIRS)],
                                                  [dH[p] for p in range(S_PAIRS)]))
            for p in range(S_PAIRS):
                dx_o[:, p * 128:(p + 1) * 128] = gx[p]
                dH[p] = gh[p]
            for g in range(S_GROUPS):
                db_o[:, g * S_N:(g + 1) * S_N] = gb[g]
                dc_o[:, g * S_N:(g + 1) * S_N] = gc[g]
            dd_o[...] = gdt
            ddtb[d:d + 1, 0:8] += gdtb
            dal[d:d + 1, 0:8] += gal

    small = pl.BlockSpec((8, 128), lambda u: (0, 0))
    hspec = pl.BlockSpec((1, S_PAIRS, 2 * S_P, S_N), lambda u: (step(u), 0, 0, 0))
    at = lambda order, w: pl.BlockSpec((S_Q, w), lambda u: (order(step(u)), 0))
    outs = []
    for order in (fo, bo):
        outs += [at(order, 512), at(order, 256), at(order, 256), at(order, 8)]
    oshape = [_sds((T, 512), F32), _sds((T, 256), F32), _sds((T, 256), F32), _sds((T, 8), F32)]
    return _pc(body, name=name, grid=(ns,),
               in_specs=_ssd_in_specs(fo, bo, step) + [small, small, hspec, hspec, at(fo, 512), at(bo, 512)],
               out_specs=outs + [small, small], out_shape=oshape + oshape + [_sds((8, 128), F32), _sds((8, 128), F32)],
               scratch_shapes=[pltpu.VMEM((S_PAIRS, 2 * S_P, S_N), F32), pltpu.VMEM((S_PAIRS, 2 * S_P, S_N), F32)],
               compiler_params=_cp(("arbitrary",), 40 << 20))(act, act, act, P, act, act, act, P, dtb, alog, hsf, hsb, dy, dy)


def _ssm_out(yf, yb, xs, z, dskip, g):
    y = (yf + yb + dskip * xs) * _silu(z)
    return (y * lax.rsqrt(jnp.mean(y * y, axis=-1, keepdims=True) + EPS)) * g


def ssm_out_fwd(yf, yb, act, P, dskip, g, name):
    T = yf.shape[0]

    def body(yf_r, yb_r, xs_r, z_r, d_r, g_r, o_r):
        o_r[...] = _ssm_out(yf_r[...], yb_r[...], xs_r[...], z_r[...], d_r[...], g_r[...]).astype(o_r.dtype)

    row = pl.BlockSpec((TR, 512), lambda i: (i, 0))
    vec = pl.BlockSpec((1, 512), lambda i: (0, 0))
    return _pc(body, name=name, grid=(T // TR,),
               in_specs=[row, row, row, pl.BlockSpec((TR, 512), lambda i: (i, C_Z // 512)), vec, vec],
               out_specs=row, out_shape=_sds((T, 512), BF16),
               compiler_params=_cp(("parallel",), 16 << 20))(yf, yb, act, P, dskip, g)


def ssm_out_bwd(yf, yb, act, P, dskip, g, do_src, name):
    T = yf.shape[0]

    def body(yf_r, yb_r, xs_r, z_r, d_r, g_r, do_r, dy_r, dxs_r, dz_r, dv_r):
        @pl.when(pl.program_id(0) == 0)
        def _():
            dv_r[...] = jnp.zeros_like(dv_r)

        _, vjp = jax.vjp(_ssm_out, yf_r[...], yb_r[...], xs_r[...], z_r[...], d_r[...], g_r[...])
        dyf, _, dxs, dz, dd, dg = vjp(do_r[...].astype(F32))
        dy_r[...] = dyf
        dxs_r[...] = dxs
        dz_r[...] = dz.astype(dz_r.dtype)
        dv_r[0:1, :] += dd
        dv_r[1:2, :] += dg

    row = pl.BlockSpec((TR, 512), lambda i: (i, 0))
    vec = pl.BlockSpec((1, 512), lambda i: (0, 0))
    return _pc(body, name=name, grid=(T // TR,),
               in_specs=[row, row, row, pl.BlockSpec((TR, 512), lambda i: (i, C_Z // 512)), vec, vec,
                         pl.BlockSpec((TR, 512), lambda i: (i, 1))],
               out_specs=[row, row, row, pl.BlockSpec((8, 512), lambda i: (0, 0))],
               out_shape=[_sds((T, 512), F32), _sds((T, 512), F32), _sds((T, 512), BF16), _sds((8, 512), F32)],
               compiler_params=_cp(("arbitrary",), 24 << 20))(yf, yb, act, P, dskip, g, do_src)


def add_halves(xv, got, cvec, name):
    n, r, cdim = xv.shape
    h = r // 2

    def body(c_ref, x_ref, g_ref, o_ref):
        o_ref[...] = (x_ref[...].astype(F32) + g_ref[...].astype(F32)).astype(o_ref.dtype)

    gs = pltpu.PrefetchScalarGridSpec(
        num_scalar_prefetch=1, grid=(n,),
        in_specs=[pl.BlockSpec((1, h, cdim), lambda k, c_ref: (k, c_ref[0], 0)), pl.BlockSpec((1, h, cdim), lambda k, c_ref: (k, 0, 0))],
        out_specs=pl.BlockSpec((1, h, cdim), lambda k, c_ref: (k, 0, 0)))
    return _pc(body, name=name, grid_spec=gs, out_shape=_sds((n, h, cdim), BF16),
               compiler_params=_cp(("arbitrary",), 24 << 20))(cvec, xv, got)


def sum_slots(a, name):
    n, r, cdim = a.shape
    tr = _div_tile(r, 512, 16)

    def body(a_ref, o_ref):
        acc = a_ref[0].astype(F32)
        for k in range(1, n):
            acc = acc + a_ref[k].astype(F32)
        o_ref[...] = acc

    return _pc(body, name=name, grid=(r // tr,), in_specs=[pl.BlockSpec((n, tr, cdim), lambda i: (0, i, 0))],
               out_specs=pl.BlockSpec((tr, cdim), lambda i: (i, 0)), out_shape=_sds((r, cdim), F32),
               compiler_params=_cp(("parallel",), 32 << 20))(a)


def adamw(w, g, m, v, name):
    B, R, C = w.shape
    tr = _div_tile(R, max(8, (1 << 19) // max(C, 1) // 8 * 8), 8) if R % 8 == 0 else R
    c1 = 1.0 / (1.0 - ADAM_B1 ** ADAM_STEP)
    c2 = 1.0 / (1.0 - ADAM_B2 ** ADAM_STEP)

    def body(w_ref, g_ref, m_ref, v_ref, d_ref, mo_ref, vo_ref):
        gg = g_ref[...]
        mn = ADAM_B1 * m_ref[...] + (1.0 - ADAM_B1) * gg
        vn = ADAM_B2 * v_ref[...] + (1.0 - ADAM_B2) * (gg * gg)
        d_ref[...] = -ADAM_LR * ((mn * c1) / (jnp.sqrt(vn * c2) + ADAM_EPS) + ADAM_WD * w_ref[...])
        mo_ref[...] = mn
        vo_ref[...] = vn

    spec = pl.BlockSpec((1, tr, C), lambda b, i: (b, i, 0))
    return _pc(body, name=name, grid=(B, R // tr), in_specs=[spec] * 4, out_specs=[spec] * 3,
               out_shape=[_sds((B, R, C), F32)] * 3, compiler_params=_cp(("parallel", "parallel"), 32 << 20))(w, g, m, v)


def _me():
    return lax.axis_index("x"), lax.axis_index("y"), lax.axis_index("c")


def _flip(v, bit):
    return 1 - v if bit else v


def allgather8(xv, name):
    R = xv.shape[0]

    def body(x_ref, out_ref, sum_ref, send_sems, recv_sems):
        mx, my, mc = _me()
        me = 4 * mx + 2 * my + mc
        out_ref[me] = x_ref[...]
        sends, recvs = [], []
        for k in range(1, 8):
            px, py, pc = _flip(mx, k & 4), _flip(my, k & 2), _flip(mc, k & 1)
            peer = 4 * px + 2 * py + pc
            sends.append(pltpu.make_async_remote_copy(src_ref=x_ref, dst_ref=out_ref.at[me], send_sem=send_sems.at[k - 1],
                                                      recv_sem=recv_sems.at[k - 1], device_id=(px, py, pc), device_id_type=MESH))
            recvs.append(pltpu.make_async_remote_copy(src_ref=x_ref, dst_ref=out_ref.at[peer], send_sem=send_sems.at[k - 1],
                                                      recv_sem=recv_sems.at[k - 1], device_id=(px, py, pc), device_id_type=MESH))
        for cp in sends:
            cp.start()
        for cp in recvs:
            cp.wait_recv()
        for cp in sends:
            cp.wait_send()
        acc = out_ref[0]
        for d in range(1, 8):
            acc = acc + out_ref[d]
        sum_ref[...] = acc

    vm = pl.BlockSpec(memory_space=pltpu.VMEM)
    return _pc(body, name=name, pin=False, in_specs=[vm], out_specs=[vm, vm], out_shape=[_sds((8, R, 128), F32), _sds((R, 128), F32)],
               scratch_shapes=[pltpu.SemaphoreType.DMA((7,)), pltpu.SemaphoreType.DMA((7,))],
               compiler_params=_cp(None, 32 << 20))(xv)


def _other_chips(mx, my):
    return [(1 - mx, my), (mx, 1 - my), (1 - mx, 1 - my)]


def _halves(r, mc, mult):
    h = r // 2
    return pl.ds(pl.multiple_of(mc * h, mult), h), pl.ds(pl.multiple_of((1 - mc) * h, mult), h)


def _rcopy(src, dst, send_sems, recv_sems, k, to):
    return pltpu.make_async_remote_copy(src_ref=src, dst_ref=dst, send_sem=send_sems.at[k], recv_sem=recv_sems.at[k],
                                        device_id=to, device_id_type=MESH)


def _gather_body(xs, outs, send_sems, recv_sems):
    n = len(xs)
    mx, my, mc = _me()
    chip = 2 * mx + my
    sib = (mx, my, 1 - mc)
    chips = _other_chips(mx, my)
    idx = [2 * cx + cy for cx, cy in chips]
    cp = functools.partial(_rcopy, send_sems=send_sems, recv_sems=recv_sems)
    hv = [_halves(x.shape[0], mc, 16) for x in xs]
    first, passed = [], []
    for a in range(n):
        for j, (cx, cy) in enumerate(chips):
            first.append(cp(xs[a].at[hv[a][0]], outs[a].at[chip, hv[a][0]], k=6 * a + j, to=(cx, cy, mc)))
            first[-1].start()
    for a in range(n):
        for j in range(3):
            cp(xs[a].at[hv[a][0]], outs[a].at[idx[j], hv[a][0]], k=6 * a + j, to=sib).wait_recv()
            passed.append(cp(outs[a].at[idx[j], hv[a][0]], outs[a].at[idx[j], hv[a][0]], k=6 * a + 3 + j, to=sib))
            passed[-1].start()
    for a in range(n):
        for j in range(3):
            cp(xs[a].at[hv[a][1]], outs[a].at[idx[j], hv[a][1]], k=6 * a + 3 + j, to=sib).wait_recv()
    for c_ in first + passed:
        c_.wait_send()


def _my_chip():
    return 2 * lax.axis_index("x") + lax.axis_index("y")


def _own_slots(outs, shards):
    return [lax.dynamic_update_index_in_dim(o, x, _my_chip(), 0) for o, x in zip(outs, shards)]


def gather_weights(shards, name):
    n = len(shards)

    def body(*refs):
        _gather_body(refs[:n], refs[n:2 * n], *refs[2 * n:])

    hbm = pl.BlockSpec(memory_space=pl.ANY)
    outs = _pc(body, name=name, in_specs=[hbm] * n, out_specs=[hbm] * n, out_shape=[_sds((4,) + x.shape, x.dtype) for x in shards],
               scratch_shapes=[pltpu.SemaphoreType.DMA((6 * n,)), pltpu.SemaphoreType.DMA((6 * n,))])(*shards)
    return _own_slots(outs, shards)


GATHER_REST_ID = 3


def gather_weights_sc(shards, name):
    n = len(shards)
    x_refs = [jax.new_ref(x, memory_space=pltpu.MemorySpace.HBM) for x in shards]
    out_refs = [jax.empty_ref(_sds((4,) + x.shape, x.dtype), memory_space=pltpu.MemorySpace.HBM) for x in shards]

    @pl.kernel(mesh=plsc.ScalarSubcoreMesh(axis_name="sc", num_cores=1), name=name,
               scratch_types=(pltpu.SemaphoreType.DMA((6 * n,)), pltpu.SemaphoreType.DMA((6 * n,))),
               compiler_params=pltpu.CompilerParams(collective_id=GATHER_REST_ID))
    def launch(send_sems, recv_sems):
        mx, my, mc = _me()
        barrier = pltpu.get_barrier_semaphore()
        for peer in [(mx, my, 1 - mc)] + [(cx, cy, mc) for cx, cy in _other_chips(mx, my)]:
            pl.semaphore_signal(barrier, inc=1, device_id=peer, device_id_type=MESH)
        pl.semaphore_wait(barrier, 4)
        _gather_body(x_refs, out_refs, send_sems, recv_sems)

    launch()
    return _own_slots([o[...] for o in out_refs], shards)


def swap_halves(arrs, name):
    n = len(arrs)

    def body(*refs):
        xs, outs = refs[:n], refs[n:2 * n]
        send_sems, recv_sems = refs[2 * n:]
        mx, my, mc = _me()
        cps = []
        for a in range(n):
            theirs = _halves(xs[a].shape[1], mc, 16)[1]
            cps.append(_rcopy(xs[a].at[pl.ds(0, 4), theirs], outs[a], send_sems, recv_sems, a, (mx, my, 1 - mc)))
            cps[-1].start()
        for c_ in cps:
            c_.wait()

    hbm = pl.BlockSpec(memory_space=pl.ANY)
    return _pc(body, name=name, in_specs=[hbm] * n, out_specs=[hbm] * n,
               out_shape=[_sds((4, x.shape[1] // 2, x.shape[2]), x.dtype) for x in arrs],
               scratch_shapes=[pltpu.SemaphoreType.DMA((n,)), pltpu.SemaphoreType.DMA((n,))])(*arrs)


SCATTER_ID = 4


def scatter_chips_sc(arrs, name):
    n = len(arrs)
    x_refs = [jax.new_ref(x, memory_space=pltpu.MemorySpace.HBM) for x in arrs]
    out_refs = [jax.empty_ref(_sds(x.shape, x.dtype), memory_space=pltpu.MemorySpace.HBM) for x in arrs]

    @pl.kernel(mesh=plsc.ScalarSubcoreMesh(axis_name="sc", num_cores=1), name=name,
               scratch_types=(pltpu.SemaphoreType.DMA((3 * n,)), pltpu.SemaphoreType.DMA((3 * n,))),
               compiler_params=pltpu.CompilerParams(collective_id=SCATTER_ID))
    def launch(send_sems, recv_sems):
        mx, my, mc = _me()
        chip = 2 * mx + my
        chips = _other_chips(mx, my)
        idx = [2 * cx + cy for cx, cy in chips]
        barrier = pltpu.get_barrier_semaphore()
        for cx, cy in chips:
            pl.semaphore_signal(barrier, inc=1, device_id=(cx, cy, mc), device_id_type=MESH)
        pl.semaphore_wait(barrier, 3)
        cp = functools.partial(_rcopy, send_sems=send_sems, recv_sems=recv_sems)
        sends = []
        for a in range(n):
            for j, (cx, cy) in enumerate(chips):
                sends.append(cp(x_refs[a].at[idx[j]], out_refs[a].at[chip], k=3 * a + j, to=(cx, cy, mc)))
                sends[-1].start()
        for a in range(n):
            for j, (cx, cy) in enumerate(chips):
                cp(x_refs[a].at[idx[j]], out_refs[a].at[idx[j]], k=3 * a + j, to=(cx, cy, mc)).wait_recv()
        for c_ in sends:
            c_.wait_send()

    launch()
    return _own_slots([o[...] for o in out_refs], [lax.dynamic_index_in_dim(x, _my_chip(), 0, keepdims=False) for x in arrs])


def share_halves(parts, name):
    flat = [p for w in parts for p in w]
    nw, n = len(parts), len(flat)
    depth = n // nw

    def body(*refs):
        xs, outs = refs[:n], refs[n:n + nw]
        send_sems, recv_sems = refs[n + nw:]
        mx, my, mc = _me()
        sib = (mx, my, 1 - mc)
        sends, recvs = [], []
        for a in range(n):
            w, l = a // depth, a % depth
            mine, theirs = _halves(outs[w].shape[1], mc, 8)
            sends.append(_rcopy(xs[a], outs[w].at[l, mine], send_sems, recv_sems, a, sib))
            recvs.append(_rcopy(xs[a], outs[w].at[l, theirs], send_sems, recv_sems, a, sib))
            sends[-1].start()
        for c_ in recvs:
            c_.wait_recv()
        for c_ in sends:
            c_.wait_send()

    hbm = pl.BlockSpec(memory_space=pl.ANY)
    outs = _pc(body, name=name, in_specs=[hbm] * n, out_specs=[hbm] * nw,
               out_shape=[_sds((depth, 2 * w[0].shape[0], w[0].shape[1]), F32) for w in parts],
               scratch_shapes=[pltpu.SemaphoreType.DMA((n,)), pltpu.SemaphoreType.DMA((n,))])(*flat)
    outs = list(outs)
    mc = lax.axis_index("c")
    for w in range(nw):
        for l in range(depth):
            h = parts[w][l].shape[0]
            outs[w] = lax.dynamic_update_slice(outs[w], parts[w][l][None], (l, mc * h, 0))
    return outs


_BIG = ("w_in", "w_out", "w_ffn_in", "w_ffn_out")
N_CHIPS = 4
DEPTH = 2


def _pad_rows(v, mult=8):
    n = v.shape[0]
    rows = -(-n // 128)
    rows = -(-rows // mult) * mult
    return jnp.pad(v, (0, rows * 128 - n)).reshape(rows, 128)


class _Flat:
    def __init__(self):
        self.items = []

    def add(self, name, a):
        self.items.append((name, a.shape, a.reshape(-1).astype(F32)))

    def rows(self):
        return _pad_rows(jnp.concatenate([a for _, _, a in self.items]))

    def split(self, rows):
        flat = rows.reshape(-1)
        out, o = {}, 0
        for name, shape, a in self.items:
            out[name] = flat[o:o + a.shape[0]].reshape(shape)
            o += a.shape[0]
        return out

    def split_lead(self, rows3):
        n = rows3.shape[0]
        flat = rows3.reshape(n, -1)
        out, o = {}, 0
        for name, shape, a in self.items:
            out[name] = flat[:, o:o + a.shape[0]].reshape((n,) + tuple(shape))
            o += a.shape[0]
        return out


def _gsv(rows):
    z = jnp.zeros((2, D), F32)
    r = [z if a is None else a for a in rows] + [z] * 5
    return jnp.stack(r, axis=1)


def _pad8(a, rows=8, cols=128):
    return jnp.zeros((rows, cols), F32).at[:a.shape[0], :a.shape[1]].set(a.astype(F32))


def kernel(x, c, ctx, c_ctx, w_mod, b_mod, g_mix, w_in, wa_sink, na_rpb, ssm_conv_w, ssm_conv_b, ssm_dt_bias, ssm_a_log, ssm_d, ssm_norm_g, w_out, g_ffn, w_ffn_in, w_ffn_out, g_final, loss_target, m_c_ctx, m_w_mod, m_b_mod, m_g_mix, m_w_in, m_wa_sink, m_na_rpb, m_ssm_conv_w, m_ssm_conv_b, m_ssm_dt_bias, m_ssm_a_log, m_ssm_d, m_ssm_norm_g, m_w_out, m_g_ffn, m_w_ffn_in, m_w_ffn_out, m_g_final, v_c_ctx, v_w_mod, v_b_mod, v_g_mix, v_w_in, v_wa_sink, v_na_rpb, v_ssm_conv_w, v_ssm_conv_b, v_ssm_dt_bias, v_ssm_a_log, v_ssm_d, v_ssm_norm_g, v_w_out, v_g_ffn, v_w_ffn_in, v_w_ffn_out, v_g_final):
    L, Lc = x.shape[1], ctx.shape[1]
    T = L + Lc
    nL = L // TR
    mx, my, mc = lax.axis_index("x"), lax.axis_index("y"), lax.axis_index("c")
    dev = 4 * mx + 2 * my + mc
    chip = 2 * mx + my
    MODW = 6 * D // N_CHIPS
    CW = 1024 // N_CHIPS

    sc = _silu(c.astype(F32))
    scc = _silu(c_ctx.astype(F32))[None]
    f1 = _Flat()
    f1.add("sc", sc)
    f1.add("conv_w", ssm_conv_w)
    g1, _ = allgather8(f1.rows(), "gather_cond")
    g1 = f1.split_lead(g1)
    sc_all = g1["sc"][:, 0]
    conv_w = jnp.concatenate([g1["conv_w"][2 * k] for k in range(N_CHIPS)], axis=-1)
    A16 = jnp.concatenate([sc_all, scc, jnp.zeros((7, D), F32)], axis=0)

    mod_part = matmul_layers(A16, w_mod, "nn", "mod_fwd")
    f2 = _Flat()
    f2.add("mod", mod_part)
    g2, _ = allgather8(f2.rows(), "gather_mod")
    g2 = f2.split_lead(g2)["mod"]
    mods = jnp.concatenate([g2[2 * k] for k in range(N_CHIPS)], axis=-1) + b_mod[:, None, :]
    mod_l = lax.dynamic_index_in_dim(mods, dev, axis=1, keepdims=False).reshape(DEPTH, 6, D)
    mod_c = mods[:, 8].reshape(DEPTH, 6, D)
    mod = jnp.stack([mod_l, mod_c], axis=1)
    mrow = lambda l, j: mod[l, :, j]

    own = {"w_in": w_in, "w_out": w_out, "w_ffn_in": w_ffn_in, "w_ffn_out": w_ffn_out}
    sh16 = [own[n][l].astype(BF16) for n in _BIG for l in range(DEPTH)]
    after_mod = (g2[0, 0, 0, 0] * 0).astype(BF16)
    gath = list(gather_weights([sh16[0] + after_mod], "gather_first"))
    after_first = (gath[0][0, 0, 0] * 0).astype(BF16)
    gath += list(gather_weights_sc([sh16[1] + after_first] + sh16[2:], "gather_rest"))
    gw = {n: [gath[DEPTH * i + l] for l in range(DEPTH)] for i, n in enumerate(_BIG)}
    W_in = [jnp.pad(jnp.concatenate([g[k] for k in range(N_CHIPS)], axis=1), ((0, 0), (0, IN_PAD - IN_COLS))) for g in gw["w_in"]]
    W_out = [g.reshape(D, D) for g in gw["w_out"]]
    W_fo = [g.reshape(D_FF, D) for g in gw["w_ffn_out"]]
    W_fi = gw["w_ffn_in"]

    cos, sin, rotm = rope_tables(L, Lc)
    x0 = jnp.concatenate([x[0], ctx[0]], axis=0).astype(F32)

    sv = []
    xin = x0
    gsv_first = _gsv([None, mrow(0, 0), mrow(0, 1)])
    _, h1 = res_norm_mod(x0, None, gsv_first, g_mix[0][None], nL, "norm_first")
    for l in range(DEPTH):
        s = {"xin": xin, "h1": h1}
        P = matmul(h1, W_in[l], "nn", F32, f"in_proj{l}", tn=IN_PAD)
        qr, kr, kb, vb = rope_apply(P, C_QA // 256, P, C_KA // 128, cos, sin, rotm, False, f"rope{l}", kv_src=P)
        sink8 = _pad8(jnp.broadcast_to(wa_sink[l][:, None], (WA_HEADS, 128)))
        krs, va = _swap_halves_lanes(kr), P[:, C_VA:C_VA + 128]
        vas = _swap_halves_lanes(va)
        oa, sta = win_attn_fwd(qr, kr, krs, va, vas, sink8, L, Lc, f"wa_fwd{l}")
        bias = na_bias_table(na_rpb[l], l)
        ob, stb = na_fwd(P, kb, vb, bias, L, Lc, f"na_fwd{l}")
        w8 = jnp.concatenate([conv_w[l], jnp.zeros((1, 1024), F32)], axis=0)
        pre, act = conv_silu_fwd(P, w8, ssm_conv_b[l][None], nL, f"conv_fwd{l}")
        dtb8, al8 = _pad8(ssm_dt_bias[l]), _pad8(ssm_a_log[l])
        yf, yb, hsf, hsb = ssd_fwd(act, P, dtb8, al8, L, Lc, f"ssd_fwd{l}")
        dskip = jnp.repeat(ssm_d[l], S_P)[None]
        oc = ssm_out_fwd(yf, yb, act, P, dskip, ssm_norm_g[l][None], f"ssm_out_fwd{l}")
        mixin = [(oa, 0), (ob, 256), (oc, 512)]
        mix = out_proj_fwd(mixin, W_out[l], f"out_proj{l}")
        gsv_mid = _gsv([mrow(l, 2), mrow(l, 3), mrow(l, 4)])
        x1, h2 = res_norm_mod(xin, mix, gsv_mid, g_ffn[l][None], nL, f"norm_mid{l}")
        gu = matmul_fi(h2, W_fi[l], "nn", BF16, f"ffn_in{l}")
        af = swiglu_fwd(gu, f"swiglu_fwd{l}")
        fo = matmul(af, W_fo[l], "nn", BF16, f"ffn_out{l}", tk=D_FF)
        s.update(P=P, qr=qr, kr=kr, krs=krs, va=va, vas=vas, sink8=sink8, oa=oa, sta=sta, ob=ob, stb=stb, kb=kb, vb=vb, bias=bias, w8=w8, pre=pre, act=act, dtb8=dtb8, al8=al8, yf=yf,
                 yb=yb, hsf=hsf, hsb=hsb, dskip=dskip, mixin=mixin, mix=mix, gsv_mid=gsv_mid, x1=x1, h2=h2, gu=gu, af=af, fo=fo)
        if l + 1 < DEPTH:
            s["gsv_end"] = _gsv([mrow(l, 5), mrow(l + 1, 0), mrow(l + 1, 1)])
            xin, h1 = res_norm_mod(x1, fo, s["gsv_end"], g_mix[l + 1][None], nL, f"norm_end{l}")
        else:
            s["gsv_end"] = _gsv([mrow(l, 5), None, None])
        sv.append(s)

    last = sv[-1]
    loss8, dres, dfo, dgsv_end, dg_final = final_loss(last["x1"], last["fo"], last["gsv_end"], g_final[None], loss_target[0].astype(F32), nL, "final_loss")
    loss = lax.psum(loss8[0, 0], ("x", "y", "c"))

    dmod = [[None] * 6 for _ in range(DEPTH)]
    gW = {n: [None] * DEPTH for n in _BIG}
    small = [dict() for _ in range(DEPTH)]
    parts = [None] * DEPTH
    cvec = mc.astype(jnp.int32).reshape(1)
    grad_x = None
    for l in reversed(range(DEPTH)):
        s = sv[l]
        dmod[l][5] = dgsv_end[:, 0]
        if l + 1 < DEPTH:
            dmod[l + 1][0], dmod[l + 1][1] = dgsv_end[:, 1], dgsv_end[:, 2]
        daf = matmul(dfo, W_fo[l], "nt", BF16, f"ffn_out_dx{l}", tn=D_FF)
        gW["w_ffn_out"][l] = matmul(s["af"], dfo, "tn", BF16, f"ffn_out_dw{l}", tm=1408, tk=T).reshape(N_CHIPS, D_FF // N_CHIPS, D)
        dgu = swiglu_bwd(s["gu"], daf, f"swiglu_bwd{l}")
        dh2 = matmul_fi(dgu, W_fi[l], "nt", BF16, f"ffn_in_dx{l}")
        gW["w_ffn_in"][l] = matmul_fi(s["h2"], dgu, "tn", BF16, f"ffn_in_dw{l}")
        dres, dmix, dgsv_mid, dg_ffn = res_norm_mod_bwd(s["x1"], s["mix"], s["gsv_mid"], g_ffn[l][None], dh2, dres, nL, f"norm_mid_bwd{l}")
        dmod[l][2], dmod[l][3], dmod[l][4] = dgsv_mid[:, 0], dgsv_mid[:, 1], dgsv_mid[:, 2]
        dmixin = matmul(dmix, W_out[l], "nt", BF16, f"out_proj_dx{l}")
        gW["w_out"][l] = out_proj_dw(s["mixin"], dmix, f"out_proj_dw{l}").reshape(N_CHIPS, D // N_CHIPS, D)
        P = s["P"]
        dqr, dkr, dkrs, dva, dvas, dsink = win_attn_bwd(s["qr"], s["kr"], s["krs"], s["va"], s["vas"], s["sink8"], dmixin, s["oa"], s["sta"], L, Lc,
                                                        f"wa_bwd{l}")
        dkr, dva = dkr + _swap_halves_lanes(dkrs), dva + _swap_halves_lanes(dvas)
        dqa, dka = rope_apply(dqr, 0, dkr[WA_BLK:WA_BLK + T], 0, cos, sin, rotm, True, f"rope_bwd{l}")
        dqb, dkb, dvb, dbias = na_bwd(P, s["kb"], s["vb"], s["bias"], dmixin, s["ob"], s["stb"], L, Lc, f"na_bwd{l}")
        dy, dxs1, dz, dvec = ssm_out_bwd(s["yf"], s["yb"], s["act"], P, s["dskip"], ssm_norm_g[l][None], dmixin, f"ssm_out_bwd{l}")
        dxf, dbf, dcf, ddf, dxb, dbb, dcb, ddb, ddtb, dal = ssd_bwd(s["act"], P, s["dtb8"], s["al8"], s["hsf"], s["hsb"], dy, L, Lc, f"ssd_bwd{l}")
        dpre = dsilu(s["pre"], [dxf, dxb, dxs1], [dbf, dbb], [dcf, dcb], f"dsilu{l}")
        dxbc, dw8, db8 = conv_bwd(dpre, P, s["w8"], nL, f"conv_bwd{l}")
        ddt = jnp.concatenate([ddf, ddb, jnp.zeros((T, IN_PAD - IN_COLS), F32)], axis=1)
        pieces = [(dqa, C_QA), (dqb, C_QB), (dz, C_Z), (dka, C_KA), (dva[WA_BLK:WA_BLK + T], C_VA), (dkb, C_KB), (dvb, C_VB),
                  (dxbc, C_XBC), (ddt, C_DT)]
        dh1, dwin = in_proj_bwd(pieces, s["h1"], W_in[l], f"in_proj_bwd{l}")
        cw = IN_COLS // N_CHIPS
        gW["w_in"][l] = jnp.stack([dwin[:, k * cw:(k + 1) * cw] for k in range(N_CHIPS)])
        garr = [gW[n][l] for n in _BIG]
        got = swap_halves(garr, f"reduce_d2d{l}")
        chip_sum = [add_halves(garr[a], got[a], cvec, f"reduce_add_pair{l}_{a}") for a in range(len(garr))]
        parts[l] = scatter_chips_sc(chip_sum, f"reduce_ici{l}")
        small[l] = dict(g_ffn=dg_ffn[0], wa_sink=dsink[:WA_HEADS, 0], na_rpb=na_rpb_grad(dbias, l), conv_w=dw8[:S_CONV], conv_b=db8[0],
                        dt_bias=ddtb[:2, :8], a_log=dal[:2, :8], ssm_d=dvec[0].reshape(S_HEADS, S_P).sum(axis=1), norm_g=dvec[1])
        if l > 0:
            p = sv[l - 1]
            dres, dfo, dgsv_end, dg_mix = res_norm_mod_bwd(s["xin"], p["fo"], p["gsv_end"], g_mix[l][None], dh1, dres, nL, f"norm_end_bwd{l - 1}")
        else:
            grad_x, _, dgsv_first, dg_mix = res_norm_mod_bwd(s["xin"], None, gsv_first, g_mix[0][None], dh1, dres, nL, "norm_first_bwd")
            dmod[0][0], dmod[0][1] = dgsv_first[:, 1], dgsv_first[:, 2]
        small[l]["g_mix"] = dg_mix[0]
    for l in range(DEPTH):
        for j in range(6):
            if dmod[l][j] is None:
                dmod[l][j] = jnp.zeros((2, D), F32)
    dmod = jnp.stack([jnp.stack(r, axis=1) for r in dmod])

    f3 = _Flat()
    f3.add("dmod_l", dmod[:, 0].reshape(DEPTH, 6 * D))
    f3.add("dmod_c", dmod[:, 1].reshape(DEPTH, 6 * D))
    f3.add("g_final", dg_final[0])
    for n in ("g_mix", "g_ffn", "wa_sink", "na_rpb", "conv_w", "conv_b", "dt_bias", "a_log", "ssm_d", "norm_g"):
        f3.add(n, jnp.stack([small[l][n] for l in range(DEPTH)]))
    g3, s3 = allgather8(f3.rows(), "reduce_small")
    dmod_all = f3.split_lead(g3)["dmod_l"]
    s3 = f3.split(s3)
    dmodc_tot = s3["dmod_c"]
    col0 = chip * MODW
    G16, G16c = [], []
    for l in range(DEPTH):
        rows = jnp.concatenate([dmod_all[:, l], dmodc_tot[l][None], jnp.zeros((7, 6 * D), F32)], axis=0)
        G16.append(lax.dynamic_slice_in_dim(rows, col0, MODW, axis=1))
        rc = jnp.concatenate([dmodc_tot[l][None], jnp.zeros((15, 6 * D), F32)], axis=0)
        G16c.append(lax.dynamic_slice_in_dim(rc, col0, MODW, axis=1))
    grad_w_mod = matmul_layers(A16, jnp.stack(G16), "tn", "mod_dw")
    dscc_part = matmul_layers(jnp.stack(G16c), w_mod, "nt", "mod_dx")[:, 0].sum(axis=0)
    _, s4 = allgather8(_pad_rows(dscc_part * (mc == 1).astype(F32)), "reduce_cctx")
    dscc = s4.reshape(-1)[:D]
    cc = c_ctx.astype(F32)
    sg = 1.0 / (1.0 + jnp.exp(-cc))
    grad_c_ctx = dscc * (sg * (1.0 + cc * (1.0 - sg)))

    halves = [[sum_slots(parts[l][i], f"reduce_add_chips{l}_{i}") for l in range(DEPTH)] for i in range(len(_BIG))]
    gsh = dict(zip(_BIG, share_halves(halves, "reduce_share")))

    grads = {"c_ctx": grad_c_ctx, "w_mod": grad_w_mod, "b_mod": s3["dmod_l"] + s3["dmod_c"], "g_mix": s3["g_mix"], "w_in": gsh["w_in"],
             "wa_sink": s3["wa_sink"], "na_rpb": s3["na_rpb"],
             "ssm_conv_w": lax.dynamic_slice_in_dim(s3["conv_w"], chip * CW, CW, axis=2), "ssm_conv_b": s3["conv_b"],
             "ssm_dt_bias": s3["dt_bias"], "ssm_a_log": s3["a_log"], "ssm_d": s3["ssm_d"], "ssm_norm_g": s3["norm_g"],
             "w_out": gsh["w_out"], "g_ffn": s3["g_ffn"], "w_ffn_in": gsh["w_ffn_in"], "w_ffn_out": gsh["w_ffn_out"], "g_final": s3["g_final"]}
    wts = {"c_ctx": c_ctx, "w_mod": w_mod, "b_mod": b_mod, "g_mix": g_mix, "w_in": w_in, "wa_sink": wa_sink, "na_rpb": na_rpb,
           "ssm_conv_w": ssm_conv_w, "ssm_conv_b": ssm_conv_b, "ssm_dt_bias": ssm_dt_bias, "ssm_a_log": ssm_a_log, "ssm_d": ssm_d,
           "ssm_norm_g": ssm_norm_g, "w_out": w_out, "g_ffn": g_ffn, "w_ffn_in": w_ffn_in, "w_ffn_out": w_ffn_out, "g_final": g_final}
    ms = {"c_ctx": m_c_ctx, "w_mod": m_w_mod, "b_mod": m_b_mod, "g_mix": m_g_mix, "w_in": m_w_in, "wa_sink": m_wa_sink, "na_rpb": m_na_rpb,
          "ssm_conv_w": m_ssm_conv_w, "ssm_conv_b": m_ssm_conv_b, "ssm_dt_bias": m_ssm_dt_bias, "ssm_a_log": m_ssm_a_log, "ssm_d": m_ssm_d,
          "ssm_norm_g": m_ssm_norm_g, "w_out": m_w_out, "g_ffn": m_g_ffn, "w_ffn_in": m_w_ffn_in, "w_ffn_out": m_w_ffn_out, "g_final": m_g_final}
    vs = {"c_ctx": v_c_ctx, "w_mod": v_w_mod, "b_mod": v_b_mod, "g_mix": v_g_mix, "w_in": v_w_in, "wa_sink": v_wa_sink, "na_rpb": v_na_rpb,
          "ssm_conv_w": v_ssm_conv_w, "ssm_conv_b": v_ssm_conv_b, "ssm_dt_bias": v_ssm_dt_bias, "ssm_a_log": v_ssm_a_log, "ssm_d": v_ssm_d,
          "ssm_norm_g": v_ssm_norm_g, "w_out": v_w_out, "g_ffn": v_g_ffn, "w_ffn_in": v_w_ffn_in, "w_ffn_out": v_w_ffn_out, "g_final": v_g_final}
    names = list(wts)
    grads = {n: grads[n].reshape(wts[n].shape).astype(F32) for n in names}
    big = ("w_mod", "w_in", "w_out", "w_ffn_in", "w_ffn_out")
    delta, new_m, new_v = {}, {}, {}
    for n in big:
        delta[n], new_m[n], new_v[n] = adamw(wts[n], grads[n], ms[n], vs[n], f"adamw_{n}")
    packs = []
    for src in (wts, grads, ms, vs):
        f = _Flat()
        for n in names:
            if n not in big:
                f.add(n, src[n])
        packs.append(f)
    d_, m_, v_ = adamw(*[f.rows()[None] for f in packs], "adamw_small")
    for dst, rows in ((delta, d_), (new_m, m_), (new_v, v_)):
        dst.update(packs[0].split(rows[0]))

    return (loss, grad_x[:L][None], *[grads[n] for n in names], *[delta[n] for n in names],
            *[new_m[n] for n in names], *[new_v[n] for n in names])
```

```python
import functools

import numpy as np
import jax
import jax.numpy as jnp
from jax import lax
from jax.experimental import pallas as pl
from jax.experimental.pallas import tpu as pltpu
from jax.experimental.pallas import tpu_sc as plsc

F32 = jnp.float32
BF16 = jnp.bfloat16
_MXU = jnp.bfloat16
_HI = lax.Precision.HIGHEST
MESH = pl.DeviceIdType.MESH

D = 1024
HD = 64
GRID_W = 64
EPS = 1e-6
ROPE_BASE = 10000.0
WA_HEADS, WA_KV = 4, 2
WA_BLK = 128
NA_HEADS, NA_KH, NA_KW = 4, 8, 16
S_HEADS, S_P, S_INNER, S_GROUPS, S_N, S_CONV, S_Q = 8, 64, 512, 2, 128, 7, 128
D_FF = 2816
IN_COLS = 2832
IN_PAD = 2944
C_QA, C_QB, C_Z, C_KA, C_VA, C_KB, C_VB, C_XBC, C_DT = 0, 256, 512, 1024, 1152, 1280, 1536, 1792, 2816
ADAM_LR, ADAM_B1, ADAM_B2, ADAM_EPS, ADAM_WD, ADAM_STEP = 0.001, 0.9, 0.999, 1e-08, 0.01, 10

TR = 256
NEG = -1e30
VMEM_CAP = 56 * 1024 * 1024


PIN_BYTES = 256 * 1024


def _is_big(a):
    return hasattr(a, "shape") and len(a.shape) >= 2 and int(np.prod(a.shape)) * jnp.dtype(a.dtype).itemsize >= PIN_BYTES


def _pc(body, *, out_shape, pin=True, **kw):
    if not pin:
        return pl.pallas_call(body, out_shape=out_shape, **kw)
    one = isinstance(out_shape, jax.ShapeDtypeStruct)
    outs = [pltpu.HBM(s.shape, s.dtype) if _is_big(s) else s for s in ([out_shape] if one else out_shape)]
    call = pl.pallas_call(body, out_shape=outs[0] if one else outs, **kw)
    return lambda *args: call(*[pltpu.with_memory_space_constraint(a, pltpu.HBM) if _is_big(a) else a for a in args])


def _cp(sem=None, vmem=None):
    kw = {}
    if sem is not None:
        kw["dimension_semantics"] = sem
    if vmem is not None:
        kw["vmem_limit_bytes"] = int(min(max(vmem, 16 * 1024 * 1024), VMEM_CAP))
    return pltpu.CompilerParams(**kw)


def _sds(shape, dtype):
    return jax.ShapeDtypeStruct(tuple(shape), dtype)


_DIMS = {"nn": ((1,), (0,)), "nt": ((1,), (1,)), "tn": ((0,), (0,))}


def _dg(a, b, dims):
    return lax.dot_general(a.astype(_MXU), b.astype(_MXU), (dims, ((), ())), preferred_element_type=F32)


@functools.partial(jax.custom_vjp, nondiff_argnums=(2,))
def bdot(a, b, mode):
    return _dg(a, b, _DIMS[mode])


def _bdot_fwd(a, b, mode):
    return bdot(a, b, mode), (a, b)


def _bdot_bwd(mode, res, g):
    a, b = res
    if mode == "nn":
        return bdot(g, b, "nt"), bdot(a, g, "tn")
    if mode == "nt":
        return bdot(g, b, "nn"), bdot(g, a, "tn")
    return bdot(b, g, "nt"), bdot(a, g, "nn")


bdot.defvjp(_bdot_fwd, _bdot_bwd)


def hdot(a, b, mode="nn"):
    return lax.dot_general(a, b, (_DIMS[mode], ((), ())), precision=_HI, preferred_element_type=F32)


def _silu(x):
    return x / (1.0 + jnp.exp(-x))


def _softplus(x):
    return jnp.maximum(x, 0.0) + jnp.log(1.0 + jnp.exp(-jnp.abs(x)))


def _div_tile(n, cap, mult):
    if n <= cap:
        return n
    best = None
    for t in range(mult, cap + 1, mult):
        if n % t == 0:
            best = t
    assert best is not None, (n, cap, mult)
    return best


def matmul(a, b, mode, out_dtype, name, tm=640, tn=1536, tk=1408, hi=False):
    if mode == "tn":
        K, M = a.shape
    else:
        M, K = a.shape
    N = b.shape[0] if mode == "nt" else b.shape[1]
    tm = _div_tile(M, tm, 128 if mode == "tn" else 16)
    tn = _div_tile(N, tn, 128)
    tk = _div_tile(K, tk, 128 if mode != "tn" else 16)
    nk = K // tk
    dims = _DIMS[mode]

    def body(a_ref, b_ref, o_ref, *acc):
        if hi:
            part = lax.dot_general(a_ref[...], b_ref[...], (dims, ((), ())), precision=_HI, preferred_element_type=F32)
        else:
            part = _dg(a_ref[...], b_ref[...], dims)
        if nk == 1:
            o_ref[...] = part.astype(o_ref.dtype)
        else:
            k = pl.program_id(2)

            @pl.when(k == 0)
            def _():
                acc[0][...] = part

            @pl.when(k > 0)
            def _():
                acc[0][...] += part

            @pl.when(k == nk - 1)
            def _():
                o_ref[...] = acc[0][...].astype(o_ref.dtype)

    if mode == "tn":
        a_spec = pl.BlockSpec((tk, tm), lambda i, j, k: (k, i))
    else:
        a_spec = pl.BlockSpec((tm, tk), lambda i, j, k: (i, k))
    if mode == "nt":
        b_spec = pl.BlockSpec((tn, tk), lambda i, j, k: (j, k))
    else:
        b_spec = pl.BlockSpec((tk, tn), lambda i, j, k: (k, j))
    isz = lambda x: jnp.dtype(x.dtype).itemsize
    vmem = 2 * (tm * tk * isz(a) + tk * tn * isz(b) + tm * tn * jnp.dtype(out_dtype).itemsize) + 3 * tm * tn * 4
    return _pc(
        body, name=name, grid=(M // tm, N // tn, nk),
        in_specs=[a_spec, b_spec], out_specs=pl.BlockSpec((tm, tn), lambda i, j, k: (i, j)),
        out_shape=_sds((M, N), out_dtype),
        scratch_shapes=[pltpu.VMEM((tm, tn), F32)] if nk > 1 else [],
        compiler_params=_cp(("parallel", "parallel", "arbitrary"), vmem + (8 << 20)),
    )(a, b)


def matmul_layers(a, b, mode, name):
    nl = b.shape[0]
    a3 = a if a.ndim == 3 else a[None]
    shared = a3.shape[0] == 1
    M = a3.shape[2] if mode == "tn" else a3.shape[1]
    N = b.shape[1] if mode == "nt" else b.shape[2]

    def body(a_ref, b_ref, o_ref):
        o_ref[0] = _dg(a_ref[0], b_ref[0], _DIMS[mode])

    return _pc(body, name=name, grid=(nl,),
               in_specs=[pl.BlockSpec((1,) + a3.shape[1:], (lambda l: (0, 0, 0)) if shared else (lambda l: (l, 0, 0))),
                         pl.BlockSpec((1,) + b.shape[1:], lambda l: (l, 0, 0))],
               out_specs=pl.BlockSpec((1, M, N), lambda l: (l, 0, 0)), out_shape=_sds((nl, M, N), F32),
               compiler_params=_cp(("parallel",), 48 << 20))(a3, b)


def out_proj_fwd(pieces, w, name):
    T = pieces[0][0].shape[0]
    arrs, offs = [a for a, _ in pieces], [o for _, o in pieces]
    n = len(arrs)
    tm = _div_tile(T, 640, 16)

    def body(*refs):
        w_ref, o_ref = refs[n], refs[n + 1]
        acc = None
        for j in range(n):
            part = _dg(refs[j][...], w_ref[offs[j]:offs[j] + arrs[j].shape[1], :], _DIMS["nn"])
            acc = part if acc is None else acc + part
        o_ref[...] = acc.astype(o_ref.dtype)

    return _pc(body, name=name, grid=(T // tm,),
               in_specs=[pl.BlockSpec((tm, a.shape[1]), lambda i: (i, 0)) for a in arrs] + [pl.BlockSpec(w.shape, lambda i: (0, 0))],
               out_specs=pl.BlockSpec((tm, w.shape[1]), lambda i: (i, 0)), out_shape=_sds((T, w.shape[1]), BF16),
               compiler_params=_cp(("parallel",), 32 << 20))(*arrs, w)


def out_proj_dw(pieces, dy, name):
    T, N = dy.shape
    arrs, offs = [a for a, _ in pieces], [o for _, o in pieces]
    n = len(arrs)
    rows = sum(a.shape[1] for a in arrs)
    tn = 512

    def body(*refs):
        d_ref, o_ref = refs[n], refs[n + 1]
        for j in range(n):
            o_ref[offs[j]:offs[j] + arrs[j].shape[1], :] = _dg(refs[j][...], d_ref[...], _DIMS["tn"]).astype(o_ref.dtype)

    return _pc(body, name=name, grid=(N // tn,),
               in_specs=[pl.BlockSpec(a.shape, lambda j: (0, 0)) for a in arrs] + [pl.BlockSpec((T, tn), lambda j: (0, j))],
               out_specs=pl.BlockSpec((rows, tn), lambda j: (0, j)), out_shape=_sds((rows, N), BF16),
               compiler_params=_cp(("parallel",), 48 << 20))(*arrs, dy)


def in_proj_bwd(pieces, h1, w, name):
    T = h1.shape[0]
    arrs = [a for a, _ in pieces]
    offs = [o for _, o in pieces]
    wid = [a.shape[1] for a in arrs]
    n = len(arrs)
    assert sum(wid) == IN_PAD, "the pieces must tile all columns of P"
    tm = _div_tile(T, 640, 16)

    def dx_body(*refs):
        w_ref, o_ref = refs[n], refs[n + 1]
        acc = None
        for j in range(n):
            part = _dg(refs[j][...], w_ref[:, offs[j]:offs[j] + wid[j]], _DIMS["nt"])
            acc = part if acc is None else acc + part
        o_ref[...] = acc.astype(o_ref.dtype)

    dh1 = _pc(dx_body, name=name + "_dx", grid=(T // tm,),
              in_specs=[pl.BlockSpec((tm, wj), lambda i: (i, 0)) for wj in wid] + [pl.BlockSpec((D, IN_PAD), lambda i: (0, 0))],
              out_specs=pl.BlockSpec((tm, D), lambda i: (i, 0)), out_shape=_sds((T, D), BF16),
              compiler_params=_cp(("parallel",), 40 << 20))(*arrs, w)

    tmd, nk = 512, 4
    tk = T // nk

    def dw_body(h_ref, *refs):
        o_ref, acc = refs[n], refs[n + 1]
        k = pl.program_id(1)

        @pl.when(k == 0)
        def _():
            acc[...] = jnp.zeros_like(acc)

        for j in range(n):
            acc[:, offs[j]:offs[j] + wid[j]] += _dg(h_ref[...], refs[j][...], _DIMS["tn"])

        @pl.when(k == nk - 1)
        def _():
            o_ref[...] = acc[...].astype(o_ref.dtype)

    dw = _pc(dw_body, name=name + "_dw", grid=(D // tmd, nk),
             in_specs=[pl.BlockSpec((tk, tmd), lambda i, k: (k, i))] + [pl.BlockSpec((tk, wj), lambda i, k: (k, 0)) for wj in wid],
             out_specs=pl.BlockSpec((tmd, IN_PAD), lambda i, k: (i, 0)), out_shape=_sds((D, IN_PAD), BF16),
             scratch_shapes=[pltpu.VMEM((tmd, IN_PAD), F32)], compiler_params=_cp(("parallel", "arbitrary"), 48 << 20))(h1, *arrs)
    return dh1, dw


def _norm_mod(xo, shift, scale, g):
    r = lax.rsqrt(jnp.mean(xo * xo, axis=-1, keepdims=True) + EPS)
    return (xo * r) * g * (1.0 + scale) + shift


def res_norm_mod(x, y, gsv, g, nL, name):
    T = x.shape[0]
    has_y = y is not None

    def body(*refs):
        if has_y:
            x_ref, y_ref, gsv_ref, g_ref, xo_ref, h_ref = refs
            xo = x_ref[...] + gsv_ref[0, 0:1, :] * y_ref[...]
            xo_ref[...] = xo
        else:
            x_ref, gsv_ref, g_ref, h_ref = refs
            xo = x_ref[...]
        h_ref[...] = _norm_mod(xo, gsv_ref[0, 1:2, :], gsv_ref[0, 2:3, :], g_ref[...]).astype(h_ref.dtype)

    row = pl.BlockSpec((TR, D), lambda i: (i, 0))
    in_specs = [row] + ([row] if has_y else []) + [pl.BlockSpec((1, 8, D), lambda i: (i // nL, 0, 0)),
                                                     pl.BlockSpec((1, D), lambda i: (0, 0))]
    out_specs = ([row] if has_y else []) + [row]
    out_shape = ([_sds((T, D), F32)] if has_y else []) + [_sds((T, D), BF16)]
    args = (x, y, gsv, g) if has_y else (x, gsv, g)
    outs = _pc(body, name=name, grid=(T // TR,), in_specs=in_specs, out_specs=out_specs, out_shape=out_shape,
               compiler_params=_cp(("arbitrary",), 24 << 20))(*args)
    return (outs[0], outs[1]) if has_y else (None, outs[0])


def res_norm_mod_bwd(xo, y, gsv, g, dh, dres, nL, name):
    T = xo.shape[0]
    has_y = y is not None

    def body(*refs):
        if has_y:
            xo_ref, y_ref, gsv_ref, g_ref, dh_ref, dres_ref, dx_ref, dy_ref, dgsv_ref, dg_ref = refs
        else:
            xo_ref, gsv_ref, g_ref, dh_ref, dres_ref, dx_ref, dgsv_ref, dg_ref = refs
        i = pl.program_id(0)

        @pl.when((i == 0) | (i == nL))
        def _():
            dgsv_ref[...] = jnp.zeros_like(dgsv_ref)

        @pl.when(i == 0)
        def _():
            dg_ref[...] = jnp.zeros_like(dg_ref)

        _, vjp = jax.vjp(_norm_mod, xo_ref[...], gsv_ref[0, 1:2, :], gsv_ref[0, 2:3, :], g_ref[...])
        dxn, dshift, dscale, dg = vjp(dh_ref[...].astype(F32))
        dxo = dres_ref[...] + dxn
        dx_ref[...] = dxo
        if has_y:
            dy_ref[...] = (gsv_ref[0, 0:1, :] * dxo).astype(dy_ref.dtype)
            dgsv_ref[0, 0:1, :] += jnp.sum(y_ref[...] * dxo, axis=0, keepdims=True)
        dgsv_ref[0, 1:2, :] += dshift
        dgsv_ref[0, 2:3, :] += dscale
        dg_ref[0:1, :] += dg

    row = pl.BlockSpec((TR, D), lambda i: (i, 0))
    gspec = pl.BlockSpec((1, 8, D), lambda i: (i // nL, 0, 0))
    in_specs = [row] + ([row] if has_y else []) + [gspec, pl.BlockSpec((1, D), lambda i: (0, 0)), row, row]
    out_specs = [row] + ([row] if has_y else []) + [gspec, pl.BlockSpec((8, D), lambda i: (0, 0))]
    out_shape = [_sds((T, D), F32)] + ([_sds((T, D), BF16)] if has_y else []) + [_sds((2, 8, D), F32), _sds((8, D), F32)]
    args = (xo, y, gsv, g, dh, dres) if has_y else (xo, gsv, g, dh, dres)
    outs = _pc(body, name=name, grid=(T // TR,), in_specs=in_specs, out_specs=out_specs, out_shape=out_shape,
               compiler_params=_cp(("arbitrary",), 32 << 20))(*args)
    if has_y:
        return outs
    return outs[0], None, outs[1], outs[2]


def final_loss(x, y, gsv, g, target, nL, name):
    T = x.shape[0]

    def lossf(xo, gv, t):
        yn = (xo * lax.rsqrt(jnp.mean(xo * xo, axis=-1, keepdims=True) + EPS)) * gv
        e = yn - t
        return 0.5 * jnp.sum(jnp.sum(e * e, axis=-1, keepdims=True) * (1.0 / D), axis=0, keepdims=True)

    def body(x_ref, y_ref, gsv_ref, g_ref, t_ref, loss_ref, dx_ref, dy_ref, dgsv_ref, dg_ref):
        i = pl.program_id(0)

        @pl.when(i == 0)
        def _():
            loss_ref[...] = jnp.zeros_like(loss_ref)
            dg_ref[...] = jnp.zeros_like(dg_ref)

        @pl.when((i == 0) | (i == nL))
        def _():
            dgsv_ref[...] = jnp.zeros_like(dgsv_ref)

        @pl.when(i < nL)
        def _():
            gate = gsv_ref[0, 0:1, :]
            yv = y_ref[...]
            xo = x_ref[...] + gate * yv
            lv, vjp = jax.vjp(lossf, xo, g_ref[...], t_ref[...])
            dxo, dg, _ = vjp(jnp.ones((1, 1), F32))
            loss_ref[...] += jnp.broadcast_to(lv, loss_ref.shape)
            dx_ref[...] = dxo
            dy_ref[...] = (gate * dxo).astype(dy_ref.dtype)
            dgsv_ref[0, 0:1, :] += jnp.sum(yv * dxo, axis=0, keepdims=True)
            dg_ref[0:1, :] += dg

        @pl.when(i >= nL)
        def _():
            dx_ref[...] = jnp.zeros_like(dx_ref)
            dy_ref[...] = jnp.zeros_like(dy_ref)

    row = pl.BlockSpec((TR, D), lambda i: (i, 0))
    gspec = pl.BlockSpec((1, 8, D), lambda i: (i // nL, 0, 0))
    return _pc(
        body, name=name, grid=(T // TR,),
        in_specs=[row, row, gspec, pl.BlockSpec((1, D), lambda i: (0, 0)),
                  pl.BlockSpec((TR, D), lambda i: (jnp.minimum(i, nL - 1), 0))],
        out_specs=[pl.BlockSpec((8, 128), lambda i: (0, 0)), row, row, gspec, pl.BlockSpec((8, D), lambda i: (0, 0))],
        out_shape=[_sds((8, 128), F32), _sds((T, D), F32), _sds((T, D), BF16), _sds((2, 8, D), F32), _sds((8, D), F32)],
        compiler_params=_cp(("arbitrary",), 32 << 20),
    )(x, y, gsv, g, target)


FI_BLK = 2 * D_FF // 4


def _fi_chip(j):
    return (j % 2) * 2 + j // 2


def matmul_fi(a, b, mode, out_dtype, name):
    T = a.shape[0]
    if mode == "tn":
        tmd = 512

        def body(a_ref, b_ref, o_ref):
            o_ref[0] = _dg(a_ref[...], b_ref[...], _DIMS["tn"]).astype(o_ref.dtype)

        return _pc(body, name=name, grid=(D // tmd, 4),
                   in_specs=[pl.BlockSpec((T, tmd), lambda i, j: (0, i)), pl.BlockSpec((T, FI_BLK), lambda i, j: (0, j))],
                   out_specs=pl.BlockSpec((1, tmd, FI_BLK), lambda i, j: (_fi_chip(j), i, 0)),
                   out_shape=_sds((4, D, FI_BLK), out_dtype), compiler_params=_cp(("parallel", "arbitrary"), 48 << 20))(a, b)
    assert mode == "nt"
    tm = _div_tile(T, 640, 16)

    def body(a_ref, b_ref, o_ref):
        acc = None
        for k in range(4):
            part = _dg(a_ref[:, k * FI_BLK:(k + 1) * FI_BLK], b_ref[_fi_chip(k)], _DIMS["nt"])
            acc = part if acc is None else acc + part
        o_ref[...] = acc.astype(o_ref.dtype)

    return _pc(body, name=name, grid=(T // tm,),
               in_specs=[pl.BlockSpec((tm, 4 * FI_BLK), lambda i: (i, 0)), pl.BlockSpec((4, D, FI_BLK), lambda i: (0, 0, 0))],
               out_specs=pl.BlockSpec((tm, D), lambda i: (i, 0)), out_shape=_sds((T, D), out_dtype),
               compiler_params=_cp(("parallel",), VMEM_CAP))(a, b)


def _swiglu(gate, up):
    return _silu(gate) * up


def ffn_in_swiglu(a, w, name):
    T = a.shape[0]
    tm = _div_tile(T, 640, 32)
    half = tm // 2

    def body(a_ref, wg_ref, wu_ref, gu_ref, act_ref):
        for rows in (slice(0, half), slice(half, tm)):
            g = _dg(a_ref[rows, :], wg_ref[0], _DIMS["nn"]).astype(BF16)
            u = _dg(a_ref[rows, :], wu_ref[0], _DIMS["nn"]).astype(BF16)
            gu_ref[rows, :FI_BLK] = g
            gu_ref[rows, FI_BLK:] = u
            act_ref[rows, :] = _swiglu(g.astype(F32), u.astype(F32)).astype(BF16)

    wspec = lambda r: pl.BlockSpec((1, D, FI_BLK), lambda j, i: (_fi_chip(2 * j + r), 0, 0))
    return _pc(body, name=name, grid=(2, T // tm),
               in_specs=[pl.BlockSpec((tm, D), lambda j, i: (i, 0)), wspec(0), wspec(1)],
               out_specs=[pl.BlockSpec((tm, 2 * FI_BLK), lambda j, i: (i, j)), pl.BlockSpec((tm, FI_BLK), lambda j, i: (i, j))],
               out_shape=[_sds((T, 4 * FI_BLK), BF16), _sds((T, D_FF), BF16)],
               compiler_params=_cp(("parallel", "arbitrary"), 48 << 20))(a, w, w)


def ffn_out_dx_swiglu(d, w, gu, name):
    T = d.shape[0]
    tm = _div_tile(T, 320, 16)

    def body(d_ref, w_ref, gu_ref, o_ref):
        for j in range(2):
            dact = _dg(d_ref[...], w_ref[j * FI_BLK:(j + 1) * FI_BLK, :], _DIMS["nt"]).astype(BF16).astype(F32)
            gs, us = slice(2 * j * FI_BLK, (2 * j + 1) * FI_BLK), slice((2 * j + 1) * FI_BLK, (2 * j + 2) * FI_BLK)
            g, u = gu_ref[:, gs].astype(F32), gu_ref[:, us].astype(F32)
            sg = 1.0 / (1.0 + jnp.exp(-g))
            sl = g * sg
            o_ref[:, gs] = (dact * u * (sg + sl * (1.0 - sg))).astype(o_ref.dtype)
            o_ref[:, us] = (dact * sl).astype(o_ref.dtype)

    return _pc(body, name=name, grid=(T // tm,),
               in_specs=[pl.BlockSpec((tm, D), lambda i: (i, 0)), pl.BlockSpec((D_FF, D), lambda i: (0, 0)), pl.BlockSpec((tm, 4 * FI_BLK), lambda i: (i, 0))],
               out_specs=pl.BlockSpec((tm, 4 * FI_BLK), lambda i: (i, 0)), out_shape=_sds((T, 4 * FI_BLK), BF16),
               compiler_params=_cp(("parallel",), 48 << 20))(d, w, gu)


def rope_tables(L, Lc):
    t = np.arange(L)
    rows, cols = t // GRID_W, t % GRID_W
    inv = ROPE_BASE ** (-np.arange(16, dtype=np.float32) / 16)
    lane = np.arange(64)
    pos = np.where((lane // 32)[None, :] == 0, rows[:, None], cols[:, None]).astype(np.float32)
    ang = jnp.asarray(pos) * jnp.asarray(inv[lane % 16])[None, :]
    cos = jnp.concatenate([jnp.cos(ang), jnp.ones((Lc, 64), F32)], axis=0)
    sin = jnp.concatenate([jnp.sin(ang), jnp.zeros((Lc, 64), F32)], axis=0)
    R = np.zeros((128, 128), np.float32)
    for i in range(128):
        if (i % 32) < 16:
            R[i + 16, i] = -1.0
        else:
            R[i - 16, i] = 1.0
    return jnp.tile(cos, (1, 2)), jnp.tile(sin, (1, 2)), jnp.asarray(R)


def rope_apply(q_src, q_col, k_src, k_col, cos, sin, R, transpose, name, kv_src=None):
    T = cos.shape[0]
    with_kv = kv_src is not None

    def rot(x, c, s, Rm):
        if transpose:
            return x * c + hdot(x * s, Rm, "nt")
        return x * c + hdot(x, Rm) * s

    def body(q_ref, k_ref, c_ref, s_ref, R_ref, *rest):
        qo_ref, ko_ref = rest[-4:-2] if with_kv else rest
        c, s, Rm = c_ref[...], s_ref[...], R_ref[...]
        for j in range(2):
            qo_ref[:, j * 128:(j + 1) * 128] = rot(q_ref[:, j * 128:(j + 1) * 128].astype(F32), c, s, Rm).astype(qo_ref.dtype)
        ko_ref[...] = rot(k_ref[...].astype(F32), c, s, Rm).astype(ko_ref.dtype)
        if with_kv:
            rest[-2][...] = rest[0][...].astype(BF16)
            rest[-1][...] = rest[1][...].astype(BF16)

    tab = pl.BlockSpec((TR, 128), lambda i: (i, 0))
    wide = pl.BlockSpec((TR, 256), lambda i: (i, 0))
    kv_in = [pl.BlockSpec((TR, 256), lambda i: (i, C_KB // 256)), pl.BlockSpec((TR, 256), lambda i: (i, C_VB // 256))] if with_kv else []
    return _pc(body, name=name, grid=(T // TR,),
               in_specs=[pl.BlockSpec((TR, 256), lambda i: (i, q_col)), pl.BlockSpec((TR, 128), lambda i: (i, k_col)),
                         tab, tab, pl.BlockSpec((128, 128), lambda i: (0, 0))] + kv_in,
               out_specs=[wide, tab] + ([wide, wide] if with_kv else []),
               out_shape=[_sds((T, 256), BF16), _sds((T, 128), BF16)] + ([_sds((T, 256), BF16)] * 2 if with_kv else []),
               compiler_params=_cp(("parallel",), 16 << 20))(q_src, k_src, cos, sin, R, *([kv_src, kv_src] if with_kv else []))


_SCALE = HD ** -0.5


def _attn_tile(qh, ks, vs, extra):
    ss = []
    for k, add in ks:
        s = _dg(qh, k, _DIMS["nt"]) * _SCALE
        ss.append(s if add is None else s + add)
    m = ss[0].max(axis=-1, keepdims=True)
    for s in ss[1:]:
        m = jnp.maximum(m, s.max(axis=-1, keepdims=True))
    if extra is not None:
        m = jnp.maximum(m, extra)
    ps = [jnp.exp(s - m) for s in ss]
    den = ps[0].sum(axis=-1, keepdims=True)
    for p in ps[1:]:
        den = den + p.sum(axis=-1, keepdims=True)
    if extra is not None:
        den = den + jnp.exp(extra - m)
    num = _dg(ps[0], vs[0], _DIMS["nn"])
    for p, v in zip(ps[1:], vs[1:]):
        num = num + _dg(p, v, _DIMS["nn"])
    linv = 1.0 / den
    return num * linv, m, linv


def _attn_bwd_tile(qh, ks, vs, extra, m, linv, oh, doh):
    delta = jnp.sum(doh * oh, axis=-1, keepdims=True)
    dq = None
    dks, dvs, dss = [], [], []
    for (k, add), v in zip(ks, vs):
        s = _dg(qh, k, _DIMS["nt"]) * _SCALE
        if add is not None:
            s = s + add
        p = jnp.exp(s - m) * linv
        dvs.append(_dg(p, doh, _DIMS["tn"]))
        ds = p * (_dg(doh, v, _DIMS["nt"]) - delta)
        dss.append(ds)
        dsq = ds * _SCALE
        part = _dg(dsq, k, _DIMS["nn"])
        dq = part if dq is None else dq + part
        dks.append(_dg(dsq, qh, _DIMS["tn"]))
    dextra = None
    if extra is not None:
        dextra = -(jnp.exp(extra - m) * linv * delta)
    return dq, dks, dvs, dss, dextra


def _wa_mask(n, L):
    qpos = n * WA_BLK + lax.broadcasted_iota(jnp.int32, (WA_BLK, 3 * WA_BLK), 0)
    kpos = (n - 1) * WA_BLK + lax.broadcasted_iota(jnp.int32, (WA_BLK, 3 * WA_BLK), 1)
    ok = (jnp.abs(qpos - kpos) <= WA_BLK) & (kpos >= 0) & (kpos < L)
    return jnp.where(ok, 0.0, NEG).astype(F32)


WA_BPS = 2
_WA_PAIRS = (((0, 0), (1, 3), False), ((1, 2), (0, 1), True))


def _swap_halves_lanes(a):
    return jnp.concatenate([a[:, HD:], a[:, :HD]], axis=1)


def _wa_specs(L, Lc):
    nb = L // WA_BLK
    cb = L // Lc

    def blk(j):
        return pl.BlockSpec((WA_BLK, 128), lambda s: (jnp.clip(s * WA_BPS - 1 + j, 0, nb - 1), 0))

    return nb, [blk(j) for j in range(WA_BPS + 2)] + [pl.BlockSpec((Lc, 128), lambda s: (cb, 0))]


def _wa_pair_q(q_ref, qs, lo, hi):
    a = q_ref[qs, lo[0] * 128:(lo[0] + 1) * 128]
    b = q_ref[qs, hi[0] * 128:(hi[0] + 1) * 128]
    lane = lax.broadcasted_iota(jnp.int32, a.shape, 1)
    zero = jnp.zeros_like(a)
    return jnp.concatenate([jnp.where(lane < HD, a, zero), jnp.where(lane >= HD, b, zero)], axis=0)


def _wa_pair_vec(ref, qs, lo, hi, base=0):
    return jnp.concatenate([ref[qs, base + lo[1]:base + lo[1] + 1], ref[qs, base + hi[1]:base + hi[1] + 1]], axis=0)


def _wa_pair_sink(s_ref, n, lo, hi):
    return jnp.concatenate([jnp.broadcast_to(s_ref[lo[1]:lo[1] + 1, 0:1], (n, 1)), jnp.broadcast_to(s_ref[hi[1]:hi[1] + 1, 0:1], (n, 1))], axis=0)


def win_attn_fwd(qr, kr, krs, v, vs, sink, L, Lc, name):
    T = L + Lc
    nb, specs = _wa_specs(L, Lc)
    nk = WA_BPS + 2
    QB = WA_BPS * WA_BLK
    nlat = nb // WA_BPS

    def body(q_ref, *refs):
        groups = [refs[g * (nk + 1):(g + 1) * (nk + 1)] for g in range(4)]
        s_ref, o_ref, st_ref = refs[-3], refs[-2], refs[-1]
        s = pl.program_id(0)

        def run(qs, n, ks_of, vs_of):
            outs = []
            for lo, hi, swapped in _WA_PAIRS:
                kb, vb = groups[1 if swapped else 0], groups[3 if swapped else 2]
                o2, m2, l2 = _attn_tile(_wa_pair_q(q_ref, qs, lo, hi), ks_of(kb), vs_of(vb), _wa_pair_sink(s_ref, n, lo, hi))
                outs.append(o2)
                for r, (_, h) in enumerate((lo, hi)):
                    st_ref[qs, h:h + 1] = m2[r * n:(r + 1) * n]
                    st_ref[qs, WA_HEADS + h:WA_HEADS + h + 1] = l2[r * n:(r + 1) * n]
            lane = lax.broadcasted_iota(jnp.int32, (n, 128), 1)
            o_ref[qs, 0:128] = jnp.where(lane < HD, outs[0][:n], outs[1][n:]).astype(o_ref.dtype)
            o_ref[qs, 128:256] = jnp.where(lane < HD, outs[1][:n], outs[0][n:]).astype(o_ref.dtype)

        @pl.when(s < nlat)
        def _():
            for b in range(WA_BPS):
                m1 = _wa_mask(s * WA_BPS + b, L)
                mask = jnp.concatenate([m1, m1], axis=0)
                cat = lambda g: jnp.concatenate([g[b + j][...] for j in range(3)], axis=0)
                run(slice(b * WA_BLK, (b + 1) * WA_BLK), WA_BLK,
                    lambda kb: [(cat(kb), mask), (kb[nk][...], None)], lambda vb: [cat(vb), vb[nk][...]])

        @pl.when(s >= nlat)
        def _():
            run(slice(None), QB, lambda kb: [(kb[nk][...], None)], lambda vb: [vb[nk][...]])

    qspec = pl.BlockSpec((QB, 256), lambda s: (s, 0))
    return _pc(body, name=name, grid=(T // QB,),
               in_specs=[qspec] + specs * 4 + [pl.BlockSpec((8, 128), lambda s: (0, 0))],
               out_specs=[qspec, pl.BlockSpec((QB, 8), lambda s: (s, 0))], out_shape=[_sds((T, 256), BF16), _sds((T, 8), F32)],
               compiler_params=_cp(("arbitrary",), 40 << 20))(qr, *([kr] * (nk + 1)), *([krs] * (nk + 1)), *([v] * (nk + 1)), *([vs] * (nk + 1)), sink)


def win_attn_bwd(qr, kr, krs, v, vs, sink, do_src, o, stats, L, Lc, name):
    T = L + Lc
    nb, specs = _wa_specs(L, Lc)
    nk = WA_BPS + 2
    QB = WA_BPS * WA_BLK
    nlat = nb // WA_BPS
    cx = WA_BLK + L

    def body(q_ref, *refs):
        groups = [refs[g * (nk + 1):(g + 1) * (nk + 1)] for g in range(4)]
        s_ref, do_ref, o_ref, st_ref, dq_ref, dk_ref, dks_ref, dv_ref, dvs_ref, ds_ref = refs[4 * (nk + 1):]
        s = pl.program_id(0)

        @pl.when(s == 0)
        def _():
            for r in (dk_ref, dks_ref, dv_ref, dvs_ref, ds_ref):
                r[...] = jnp.zeros_like(r)

        def run(qs, n, ks_of, vs_of, rows):
            lane = lax.broadcasted_iota(jnp.int32, (n, 128), 1)
            dqs = []
            for lo, hi, swapped in _WA_PAIRS:
                kb, vb = groups[1 if swapped else 0], groups[3 if swapped else 2]
                dka, dva = (dks_ref, dvs_ref) if swapped else (dk_ref, dv_ref)
                pair = lambda ref: jnp.concatenate([jnp.where(lane < HD, ref[qs, lo[0] * 128:(lo[0] + 1) * 128].astype(F32), 0.0),
                                                    jnp.where(lane >= HD, ref[qs, hi[0] * 128:(hi[0] + 1) * 128].astype(F32), 0.0)], axis=0)
                dq2, dks, dvs, _, dex = _attn_bwd_tile(_wa_pair_q(q_ref, qs, lo, hi), ks_of(kb), vs_of(vb), _wa_pair_sink(s_ref, n, lo, hi),
                                                       _wa_pair_vec(st_ref, qs, lo, hi), _wa_pair_vec(st_ref, qs, lo, hi, WA_HEADS), pair(o_ref), pair(do_ref))
                dqs.append(dq2)
                for r, (_, h) in enumerate((lo, hi)):
                    ds_ref[h:h + 1, :] += jnp.broadcast_to(jnp.sum(dex[r * n:(r + 1) * n], axis=0, keepdims=True), (1, 128))
                if rows is not None:
                    dka[rows, :] += dks[0]
                    dva[rows, :] += dvs[0]
                dka[cx:cx + Lc, :] += dks[-1]
                dva[cx:cx + Lc, :] += dvs[-1]
            dq_ref[qs, 0:128] = jnp.where(lane < HD, dqs[0][:n], dqs[1][n:])
            dq_ref[qs, 128:256] = jnp.where(lane < HD, dqs[1][:n], dqs[0][n:])

        @pl.when(s < nlat)
        def _():
            for b in range(WA_BPS):
                nblk = s * WA_BPS + b
                m1 = _wa_mask(nblk, L)
                mask = jnp.concatenate([m1, m1], axis=0)
                cat = lambda g: jnp.concatenate([g[b + j][...] for j in range(3)], axis=0)
                run(slice(b * WA_BLK, (b + 1) * WA_BLK), WA_BLK, lambda kb: [(cat(kb), mask), (kb[nk][...], None)],
                    lambda vb: [cat(vb), vb[nk][...]], pl.ds(pl.multiple_of(nblk * WA_BLK, WA_BLK), 3 * WA_BLK))

        @pl.when(s >= nlat)
        def _():
            run(slice(None), QB, lambda kb: [(kb[nk][...], None)], lambda vb: [vb[nk][...]], None)

    qspec = pl.BlockSpec((QB, 256), lambda s: (s, 0))
    acc_spec = pl.BlockSpec((T + 2 * WA_BLK, 128), lambda s: (0, 0))
    acc_shape = _sds((T + 2 * WA_BLK, 128), F32)
    return _pc(body, name=name, grid=(T // QB,),
               in_specs=[qspec] + specs * 4 + [pl.BlockSpec((8, 128), lambda s: (0, 0)), qspec, qspec, pl.BlockSpec((QB, 8), lambda s: (s, 0))],
               out_specs=[qspec, acc_spec, acc_spec, acc_spec, acc_spec, pl.BlockSpec((8, 128), lambda s: (0, 0))],
               out_shape=[_sds((T, 256), F32), acc_shape, acc_shape, acc_shape, acc_shape, _sds((8, 128), F32)],
               compiler_params=_cp(("arbitrary",), 48 << 20))(qr, *([kr] * (nk + 1)), *([krs] * (nk + 1)), *([v] * (nk + 1)), *([vs] * (nk + 1)),
                                                              sink, do_src, o, stats)


def na_index_tables():
    qc = np.arange(GRID_W)[:, None]
    kc = np.arange(GRID_W)[None, :]
    cstart = np.clip(qc - NA_KW // 2, 0, GRID_W - NA_KW)
    ok = (kc >= cstart) & (kc < cstart + NA_KW)
    dx = np.clip(kc - qc, -(NA_KW - 1), NA_KW - 1) + (NA_KW - 1)
    off = np.arange(NA_KH)[:, None]
    kr = np.arange(NA_KH)[None, :]
    dy = kr - off + (NA_KH - 1)
    return ok, dx, dy


def _na_selectors():
    ok, dx, dy = na_index_tables()
    e1 = np.zeros((GRID_W * GRID_W, 128), np.float32)
    qi, ki = np.nonzero(ok)
    e1[qi * GRID_W + ki, dx[qi, ki]] = 1.0
    e2 = np.zeros((16, NA_KH * NA_KH), np.float32)
    oi, ri = np.meshgrid(np.arange(NA_KH), np.arange(NA_KH), indexing="ij")
    e2[dy[oi, ri].ravel(), (oi * NA_KH + ri).ravel()] = 1.0
    return ok, jnp.asarray(e1), jnp.asarray(np.kron(np.eye(NA_HEADS, dtype=np.float32), e2))


def na_bias_table(rpb, tag):
    ok, e1, e2 = _na_selectors()
    r2 = jnp.pad(rpb.astype(F32), ((0, 0), (0, 1), (0, 128 - (2 * NA_KW - 1)))).reshape(NA_HEADS * 16, 128)
    r1 = matmul(e2, r2, "tn", F32, f"na_bias_sel1_{tag}", hi=True)
    x = matmul(r1, e1, "nt", F32, f"na_bias_sel2_{tag}", hi=True)
    b = x.reshape(NA_HEADS, NA_KH, NA_KH, GRID_W, GRID_W).transpose(0, 1, 3, 2, 4)
    b = b + jnp.asarray(np.where(ok, 0.0, NEG).astype(np.float32))[None, None, :, None, :]
    return b.reshape(NA_HEADS, NA_KH, GRID_W, NA_KH * GRID_W)


def _na_rows(r, GR):
    r0 = jnp.clip(r - NA_KH // 2, 0, GR - NA_KH)
    return r0, jnp.clip(r - r0, 0, NA_KH - 1)


NA_RPS = 4


def _pair_rows(x):
    lane = lax.broadcasted_iota(jnp.int32, x.shape, 1)
    zero = jnp.zeros_like(x)
    return jnp.concatenate([jnp.where(lane < HD, x, zero), jnp.where(lane >= HD, x, zero)], axis=0)


def _unpair_rows(x2):
    n = x2.shape[0] // 2
    lane = lax.broadcasted_iota(jnp.int32, (n, 128), 1)
    return jnp.where(lane < HD, x2[:n], x2[n:])


def na_fwd(P, kb, vb, bias, L, Lc, name):
    T = L + Lc
    GR = L // GRID_W
    W = NA_KH * GRID_W
    QB = GRID_W * NA_RPS
    nlat = GR // NA_RPS

    def body(q_ref, k_ref, v_ref, b_ref, o_ref, st_ref):
        s = pl.program_id(0)

        def put(qs, p, res):
            o2, m2, l2 = res
            n = o2.shape[0] // 2
            o_ref[qs, p * 128:(p + 1) * 128] = _unpair_rows(o2).astype(o_ref.dtype)
            for r in range(2):
                st_ref[qs, 2 * p + r:2 * p + r + 1] = m2[r * n:(r + 1) * n]
                st_ref[qs, NA_HEADS + 2 * p + r:NA_HEADS + 2 * p + r + 1] = l2[r * n:(r + 1) * n]

        @pl.when(s < nlat)
        def _():
            for rr in range(NA_RPS):
                r0, off = _na_rows(s * NA_RPS + rr, GR)
                rows = pl.ds(pl.multiple_of(r0 * GRID_W, GRID_W), W)
                qs = slice(rr * GRID_W, (rr + 1) * GRID_W)
                for p in range(NA_HEADS // 2):
                    ps = slice(p * 128, (p + 1) * 128)
                    b2 = jnp.concatenate([b_ref[2 * p, off], b_ref[2 * p + 1, off]], axis=0)
                    put(qs, p, _attn_tile(_pair_rows(q_ref[qs, ps]), [(k_ref[rows, ps], b2), (k_ref[L:T, ps], None)],
                                          [v_ref[rows, ps], v_ref[L:T, ps]], None))

        @pl.when(s >= nlat)
        def _():
            for p in range(NA_HEADS // 2):
                ps = slice(p * 128, (p + 1) * 128)
                put(slice(None), p, _attn_tile(_pair_rows(q_ref[:, ps]), [(k_ref[L:T, ps], None)], [v_ref[L:T, ps]], None))

    one = pl.Buffered(1)
    return _pc(body, name=name, grid=(T // QB,),
               in_specs=[pl.BlockSpec((QB, 256), lambda r: (r, C_QB // 256)),
                         pl.BlockSpec((T, 256), lambda r: (0, 0), pipeline_mode=one),
                         pl.BlockSpec((T, 256), lambda r: (0, 0), pipeline_mode=one),
                         pl.BlockSpec((NA_HEADS, NA_KH, GRID_W, W), lambda r: (0, 0, 0, 0), pipeline_mode=one)],
               out_specs=[pl.BlockSpec((QB, 256), lambda r: (r, 0)), pl.BlockSpec((QB, 8), lambda r: (r, 0))],
               out_shape=[_sds((T, 256), BF16), _sds((T, 8), F32)],
               compiler_params=_cp(("arbitrary",), 32 << 20))(P, kb, vb, bias)


def na_bwd(P, kb, vb, bias, do_src, o, stats, L, Lc, name):
    T = L + Lc
    GR = L // GRID_W
    W = NA_KH * GRID_W
    QB = GRID_W * NA_RPS
    nlat = GR // NA_RPS

    def body(q_ref, k_ref, v_ref, b_ref, do_ref, o_ref, st_ref, dq_ref, dk_ref, dv_ref, db_ref):
        s = pl.program_id(0)

        @pl.when(s == 0)
        def _():
            dk_ref[...] = jnp.zeros_like(dk_ref)
            dv_ref[...] = jnp.zeros_like(dv_ref)
            db_ref[...] = jnp.zeros_like(db_ref)

        def tile(qs, p, ks, vs):
            ps = slice(p * 128, (p + 1) * 128)
            m2 = jnp.concatenate([st_ref[qs, 2 * p:2 * p + 1], st_ref[qs, 2 * p + 1:2 * p + 2]], axis=0)
            l2 = jnp.concatenate([st_ref[qs, NA_HEADS + 2 * p:NA_HEADS + 2 * p + 1], st_ref[qs, NA_HEADS + 2 * p + 1:NA_HEADS + 2 * p + 2]], axis=0)
            dq2, dks, dvs, dss, _ = _attn_bwd_tile(_pair_rows(q_ref[qs, ps]), ks, vs, None, m2, l2,
                                                   _pair_rows(o_ref[qs, ps].astype(F32)), _pair_rows(do_ref[qs, ps].astype(F32)))
            dq_ref[qs, ps] = _unpair_rows(dq2).astype(dq_ref.dtype)
            return dks, dvs, dss

        @pl.when(s < nlat)
        def _():
            for rr in range(NA_RPS):
                r0, off = _na_rows(s * NA_RPS + rr, GR)
                rows = pl.ds(pl.multiple_of(r0 * GRID_W, GRID_W), W)
                qs = slice(rr * GRID_W, (rr + 1) * GRID_W)
                for p in range(NA_HEADS // 2):
                    ps = slice(p * 128, (p + 1) * 128)
                    b2 = jnp.concatenate([b_ref[2 * p, off], b_ref[2 * p + 1, off]], axis=0)
                    dks, dvs, dss = tile(qs, p, [(k_ref[rows, ps], b2), (k_ref[L:T, ps], None)], [v_ref[rows, ps], v_ref[L:T, ps]])
                    dk_ref[rows, ps] += dks[0]
                    dv_ref[rows, ps] += dvs[0]
                    dk_ref[L:T, ps] += dks[1]
                    dv_ref[L:T, ps] += dvs[1]
                    db_ref[2 * p, off] += dss[0][:GRID_W]
                    db_ref[2 * p + 1, off] += dss[0][GRID_W:]

        @pl.when(s >= nlat)
        def _():
            for p in range(NA_HEADS // 2):
                ps = slice(p * 128, (p + 1) * 128)
                dks, dvs, _ = tile(slice(None), p, [(k_ref[L:T, ps], None)], [v_ref[L:T, ps]])
                dk_ref[L:T, ps] += dks[0]
                dv_ref[L:T, ps] += dvs[0]

    one = pl.Buffered(1)
    full = lambda shape: pl.BlockSpec(shape, lambda r: (0,) * len(shape), pipeline_mode=one)
    qspec = pl.BlockSpec((QB, 256), lambda r: (r, 0))
    return _pc(body, name=name, grid=(T // QB,),
               in_specs=[pl.BlockSpec((QB, 256), lambda r: (r, C_QB // 256)), full((T, 256)), full((T, 256)),
                         full((NA_HEADS, NA_KH, GRID_W, W)), pl.BlockSpec((QB, 256), lambda r: (r, 1)), qspec, pl.BlockSpec((QB, 8), lambda r: (r, 0))],
               out_specs=[qspec, full((T, 256)), full((T, 256)), full((NA_HEADS, NA_KH, GRID_W, W))],
               out_shape=[_sds((T, 256), BF16), _sds((T, 256), F32), _sds((T, 256), F32), _sds((NA_HEADS, NA_KH, GRID_W, W), F32)],
               compiler_params=_cp(("arbitrary",), 48 << 20))(P, kb, vb, bias, do_src, o, stats)


def na_rpb_grad(dbias, tag):
    _, e1, e2 = _na_selectors()
    x = dbias.reshape(NA_HEADS, NA_KH, GRID_W, NA_KH, GRID_W).transpose(0, 1, 3, 2, 4).reshape(NA_HEADS * NA_KH * NA_KH, GRID_W * GRID_W)
    r1 = matmul(x, e1, "nn", F32, f"na_rpb_sel1_{tag}", hi=True, tk=1024)
    r2 = matmul(e2, r1, "nn", F32, f"na_rpb_sel2_{tag}", hi=True)
    return r2.reshape(NA_HEADS, 16, 128)[:, :2 * NA_KH - 1, :2 * NA_KW - 1]


_HALO = 8
CONV_CB = 4
CONV_RB = 32


def _halo_specs(T, col0):
    nh = TR // _HALO
    specs = []
    for j in range(CONV_CB):
        specs.append(pl.BlockSpec((_HALO, 256), lambda i, j=j: (jnp.maximum(i * nh - 1, 0), col0 + j)))
        specs.append(pl.BlockSpec((TR, 256), lambda i, j=j: (i, col0 + j)))
        specs.append(pl.BlockSpec((_HALO, 256), lambda i, j=j: (jnp.minimum((i + 1) * nh, T // _HALO - 1), col0 + j)))
    return specs


def _fill_ext(ext, prv, cur, nxt, i, nL, nT):
    has_prev = jnp.where((i != 0) & (i != nL), 1.0, 0.0)
    has_next = jnp.where((i != nL - 1) & (i != nT - 1), 1.0, 0.0)
    ext[0:_HALO, :] = prv[...].astype(F32) * has_prev
    ext[_HALO:_HALO + TR, :] = cur[...].astype(F32)
    ext[_HALO + TR:, :] = nxt[...].astype(F32) * has_next


def conv_silu_fwd(P, w8, b, nL, name):
    T = P.shape[0]
    nT = T // TR

    def body(*refs):
        xin, (w_ref, b_ref, pre_ref, act_ref, ext) = refs[:3 * CONV_CB], refs[3 * CONV_CB:]
        i = pl.program_id(0)
        for j in range(CONV_CB):
            cs = slice(j * 256, (j + 1) * 256)
            _fill_ext(ext, *xin[3 * j:3 * j + 3], i, nL, nT)
            for r in range(0, TR, CONV_RB):
                y = jnp.broadcast_to(b_ref[:, cs], (CONV_RB, 256))
                for k in range(S_CONV):
                    y = y + w_ref[k:k + 1, cs] * ext[pl.ds(_HALO - S_CONV // 2 + k + r, CONV_RB), :]
                pre_ref[r:r + CONV_RB, cs] = y
                act_ref[r:r + CONV_RB, cs] = _silu(y)

    out = pl.BlockSpec((TR, 1024), lambda i: (i, 0))
    return _pc(body, name=name, grid=(nT,),
               in_specs=_halo_specs(T, C_XBC // 256) + [pl.BlockSpec((8, 1024), lambda i: (0, 0)), pl.BlockSpec((1, 1024), lambda i: (0, 0))],
               out_specs=[out, out], out_shape=[_sds((T, 1024), F32), _sds((T, 1024), F32)],
               scratch_shapes=[pltpu.VMEM((TR + 2 * _HALO, 256), F32)],
               compiler_params=_cp(("parallel",), 24 << 20))(*([P] * (3 * CONV_CB)), w8, b)


def dsilu(pre, dxs_list, db_list, dc_list, name):
    T = pre.shape[0]
    n1, n2, n3 = len(dxs_list), len(db_list), len(dc_list)

    def body(*refs):
        pre_ref = refs[0]
        ins = refs[1:1 + n1 + n2 + n3]
        out = refs[-1]

        def part(rs, lo, hi):
            g = rs[0][...].astype(F32)
            for r in rs[1:]:
                g = g + r[...].astype(F32)
            x = pre_ref[:, lo:hi]
            sg = 1.0 / (1.0 + jnp.exp(-x))
            sl = x * sg
            out[:, lo:hi] = g * (sg + sl * (1.0 - sg))

        part(ins[:n1], 0, 512)
        part(ins[n1:n1 + n2], 512, 768)
        part(ins[n1 + n2:], 768, 1024)

    spec = lambda w: pl.BlockSpec((TR, w), lambda i: (i, 0))
    return _pc(body, name=name, grid=(T // TR,),
               in_specs=[spec(1024)] + [spec(512)] * n1 + [spec(256)] * (n2 + n3),
               out_specs=spec(1024), out_shape=_sds((T, 1024), F32),
               compiler_params=_cp(("parallel",), 32 << 20))(pre, *dxs_list, *db_list, *dc_list)


def conv_bwd(dpre, P, w8, nL, name):
    T = P.shape[0]
    nT = T // TR

    def body(*refs):
        din, xin, (w_ref, dx_ref, dw_ref, db_ref, extd) = refs[:3 * CONV_CB], refs[3 * CONV_CB:4 * CONV_CB], refs[4 * CONV_CB:]
        i = pl.program_id(0)

        @pl.when(i == 0)
        def _():
            dw_ref[...] = jnp.zeros_like(dw_ref)
            db_ref[...] = jnp.zeros_like(db_ref)

        fold = lambda a: functools.reduce(lambda p, q: p + q, [a[q:q + 8] for q in range(0, CONV_RB, 8)])
        for j in range(CONV_CB):
            cs = slice(j * 256, (j + 1) * 256)
            _fill_ext(extd, *din[3 * j:3 * j + 3], i, nL, nT)
            dws = [jnp.zeros((8, 256), F32) for _ in range(S_CONV)]
            dbs = jnp.zeros((8, 256), F32)
            for r in range(0, TR, CONV_RB):
                x = xin[j][r:r + CONV_RB, :]
                dx = jnp.zeros((CONV_RB, 256), F32)
                for k in range(S_CONV):
                    sd = extd[pl.ds(_HALO + S_CONV // 2 - k + r, CONV_RB), :]
                    dx = dx + w_ref[k:k + 1, cs] * sd
                    dws[k] = dws[k] + fold(sd * x)
                dx_ref[r:r + CONV_RB, cs] = dx.astype(dx_ref.dtype)
                dbs = dbs + fold(din[3 * j + 1][r:r + CONV_RB, :])
            for k in range(S_CONV):
                dw_ref[k:k + 1, cs] += jnp.sum(dws[k], axis=0, keepdims=True)
            db_ref[0:1, cs] += jnp.sum(dbs, axis=0, keepdims=True)

    acc = pl.BlockSpec((8, 1024), lambda i: (0, 0))
    xspecs = [pl.BlockSpec((TR, 256), lambda i, j=j: (i, C_XBC // 256 + j)) for j in range(CONV_CB)]
    return _pc(body, name=name, grid=(nT,),
               in_specs=_halo_specs(T, 0) + xspecs + [acc],
               out_specs=[pl.BlockSpec((TR, 1024), lambda i: (i, 0)), acc, acc],
               out_shape=[_sds((T, 1024), BF16), _sds((8, 1024), F32), _sds((8, 1024), F32)],
               scratch_shapes=[pltpu.VMEM((TR + 2 * _HALO, 256), F32)],
               compiler_params=_cp(("arbitrary",), 24 << 20))(*([dpre] * (3 * CONV_CB)), *([P] * CONV_CB), w8)


def _onehot_row(h, n):
    return (lax.broadcasted_iota(jnp.int32, (1, n), 1) == h).astype(F32)


def _onehot_col(h, n):
    return (lax.broadcasted_iota(jnp.int32, (n, 1), 0) == h).astype(F32)


S_PAIRS = S_HEADS // 2


def _ssd_chunk(xs, dtr, dtb, alog, bm, cm, hin, reverse):
    Qn = S_Q
    ii = lax.broadcasted_iota(jnp.int32, (Qn, Qn), 0)
    jj = lax.broadcasted_iota(jnp.int32, (Qn, Qn), 1)
    keep = (ii <= jj) if reverse else (ii >= jj)
    tri = keep.astype(F32)
    triT = ((jj <= ii) if reverse else (jj >= ii)).astype(F32)
    eye = (ii == jj).astype(F32)
    low = jj < S_P
    top = ii < S_P
    dt = _softplus(dtr + dtb)
    a = dt * (-jnp.exp(alog))
    cs = hdot(tri, a)
    csT = hdot(a, triT, "tn")
    dtT = hdot(dt, eye, "tn")
    last = _onehot_row(0 if reverse else Qn - 1, Qn)
    ys, houts = [], []
    for p in range(S_PAIRS):
        g = p // (S_PAIRS // S_GROUPS)
        if p % (S_PAIRS // S_GROUPS) == 0:
            G = bdot(cm[g], bm[g], "nt")
        per_head = []
        for h in (2 * p, 2 * p + 1):
            eh_r, eh_c = _onehot_row(h, S_HEADS), _onehot_col(h, S_HEADS)
            cs_c = jnp.sum(cs * eh_r, axis=1, keepdims=True)
            dt_c = jnp.sum(dt * eh_r, axis=1, keepdims=True)
            cs_r = jnp.sum(csT * eh_c, axis=0, keepdims=True)
            dt_r = jnp.sum(dtT * eh_c, axis=0, keepdims=True)
            tot = jnp.sum(cs_r * last, axis=1, keepdims=True)
            w = G * jnp.exp(jnp.where(keep, cs_c - cs_r, NEG)) * dt_r
            per_head.append((bdot(w, xs[p], "nn"), jnp.exp(cs_c), jnp.exp(tot - cs_c) * dt_c, jnp.exp(tot)))
        (y0, e0, f0, d0), (y1, e1, f1, d1) = per_head
        y = jnp.where(low, y0, y1) + bdot(cm[g], hin[p], "nt") * jnp.where(low, e0, e1)
        hout = hin[p] * jnp.where(top, d0, d1) + bdot(xs[p] * jnp.where(low, f0, f1), bm[g], "tn")
        ys.append(y)
        houts.append(hout)
    return ys, houts


def _ssd_orders(L, Lc):
    nl, ncx = L // S_Q, Lc // S_Q
    fwd = lambda s: jnp.where(s < ncx, nl + s, s - ncx)
    bwd = lambda s: nl + ncx - 1 - s
    return nl + ncx, fwd, bwd


def _ssd_in_specs(fo, bo, step):
    def at(order, w, col):
        return pl.BlockSpec((S_Q, w), lambda u: (order(step(u)), col))
    specs = []
    for order in (fo, bo):
        specs += [at(order, 512, 0), at(order, 256, 2), at(order, 256, 3), at(order, 128, C_DT // 128)]
    return specs


def ssd_fwd(act, P, dtb, alog, L, Lc, name):
    T = L + Lc
    ns, fo, bo = _ssd_orders(L, Lc)

    def body(xf, bf, cf, df, xb, bb, cb, db, dtb_ref, al_ref, yf, yb, hsf, hsb, Hf, Hb):
        s = pl.program_id(0)

        @pl.when(s == 0)
        def _():
            Hf[...] = jnp.zeros_like(Hf)
            Hb[...] = jnp.zeros_like(Hb)

        for d, (x_r, b_r, c_r, dt_r, y_r, hs_r, H) in enumerate(((xf, bf, cf, df, yf, hsf, Hf), (xb, bb, cb, db, yb, hsb, Hb))):
            hin = [H[p] for p in range(S_PAIRS)]
            hs_r[0] = H[...]
            ys, houts = _ssd_chunk(
                [x_r[:, p * 128:(p + 1) * 128] for p in range(S_PAIRS)], dt_r[:, d * 8:(d + 1) * 8],
                dtb_ref[d:d + 1, 0:8], al_ref[d:d + 1, 0:8],
                [b_r[:, g * S_N:(g + 1) * S_N] for g in range(S_GROUPS)], [c_r[:, g * S_N:(g + 1) * S_N] for g in range(S_GROUPS)],
                hin, reverse=(d == 1))
            for p in range(S_PAIRS):
                y_r[:, p * 128:(p + 1) * 128] = ys[p]
                H[p] = houts[p]

    ident = lambda u: u
    small = pl.BlockSpec((8, 128), lambda u: (0, 0))
    hspec = pl.BlockSpec((1, S_PAIRS, 2 * S_P, S_N), lambda u: (u, 0, 0, 0))
    return _pc(body, name=name, grid=(ns,),
               in_specs=_ssd_in_specs(fo, bo, ident) + [small, small],
               out_specs=[pl.BlockSpec((S_Q, 512), lambda u: (fo(u), 0)), pl.BlockSpec((S_Q, 512), lambda u: (bo(u), 0)), hspec, hspec],
               out_shape=[_sds((T, 512), F32), _sds((T, 512), F32), _sds((ns, S_PAIRS, 2 * S_P, S_N), F32), _sds((ns, S_PAIRS, 2 * S_P, S_N), F32)],
               scratch_shapes=[pltpu.VMEM((S_PAIRS, 2 * S_P, S_N), F32), pltpu.VMEM((S_PAIRS, 2 * S_P, S_N), F32)],
               compiler_params=_cp(("arbitrary",), 32 << 20))(act, act, act, P, act, act, act, P, dtb, alog)


def ssd_bwd(act, P, dtb, alog, hsf, hsb, dy, L, Lc, name):
    T = L + Lc
    ns, fo, bo = _ssd_orders(L, Lc)
    step = lambda u: ns - 1 - u

    def body(xf, bf, cf, df, xb, bb, cb, db, dtb_ref, al_ref, hsf_r, hsb_r, dyf, dyb,
             dxf, dbf, dcf, ddf, dxb, dbb, dcb, ddb, ddtb, dal, dHf, dHb):
        u = pl.program_id(0)

        @pl.when(u == 0)
        def _():
            dHf[...] = jnp.zeros_like(dHf)
            dHb[...] = jnp.zeros_like(dHb)
            ddtb[...] = jnp.zeros_like(ddtb)
            dal[...] = jnp.zeros_like(dal)

        dirs = ((xf, bf, cf, df, hsf_r, dyf, dxf, dbf, dcf, ddf, dHf), (xb, bb, cb, db, hsb_r, dyb, dxb, dbb, dcb, ddb, dHb))
        for d, (x_r, b_r, c_r, dt_r, hs_r, dy_r, dx_o, db_o, dc_o, dd_o, dH) in enumerate(dirs):
            f = functools.partial(_ssd_chunk, reverse=(d == 1))
            _, vjp = jax.vjp(
                f, [x_r[:, p * 128:(p + 1) * 128] for p in range(S_PAIRS)], dt_r[:, d * 8:(d + 1) * 8],
                dtb_ref[d:d + 1, 0:8], al_ref[d:d + 1, 0:8],
                [b_r[:, g * S_N:(g + 1) * S_N] for g in range(S_GROUPS)], [c_r[:, g * S_N:(g + 1) * S_N] for g in range(S_GROUPS)],
                [hs_r[0, p] for p in range(S_PAIRS)])
            gx, gdt, gdtb, gal, gb, gc, gh = vjp(([dy_r[:, p * 128:(p + 1) * 128] for p in range(S_PAIRS)],
                                                  [dH[p] for p in range(S_PAIRS)]))
            for p in range(S_PAIRS):
                dx_o[:, p * 128:(p + 1) * 128] = gx[p]
                dH[p] = gh[p]
            for g in range(S_GROUPS):
                db_o[:, g * S_N:(g + 1) * S_N] = gb[g]
                dc_o[:, g * S_N:(g + 1) * S_N] = gc[g]
            dd_o[...] = gdt
            ddtb[d:d + 1, 0:8] += gdtb
            dal[d:d + 1, 0:8] += gal

    small = pl.BlockSpec((8, 128), lambda u: (0, 0))
    hspec = pl.BlockSpec((1, S_PAIRS, 2 * S_P, S_N), lambda u: (step(u), 0, 0, 0))
    at = lambda order, w: pl.BlockSpec((S_Q, w), lambda u: (order(step(u)), 0))
    outs = []
    for order in (fo, bo):
        outs += [at(order, 512), at(order, 256), at(order, 256), at(order, 8)]
    oshape = [_sds((T, 512), F32), _sds((T, 256), F32), _sds((T, 256), F32), _sds((T, 8), F32)]
    return _pc(body, name=name, grid=(ns,),
               in_specs=_ssd_in_specs(fo, bo, step) + [small, small, hspec, hspec, at(fo, 512), at(bo, 512)],
               out_specs=outs + [small, small], out_shape=oshape + oshape + [_sds((8, 128), F32), _sds((8, 128), F32)],
               scratch_shapes=[pltpu.VMEM((S_PAIRS, 2 * S_P, S_N), F32), pltpu.VMEM((S_PAIRS, 2 * S_P, S_N), F32)],
               compiler_params=_cp(("arbitrary",), 40 << 20))(act, act, act, P, act, act, act, P, dtb, alog, hsf, hsb, dy, dy)


def _ssm_out(yf, yb, xs, z, dskip, g):
    y = (yf + yb + dskip * xs) * _silu(z)
    return (y * lax.rsqrt(jnp.mean(y * y, axis=-1, keepdims=True) + EPS)) * g


def ssm_out_fwd(yf, yb, act, P, dskip, g, name):
    T = yf.shape[0]

    def body(yf_r, yb_r, xs_r, z_r, d_r, g_r, o_r):
        o_r[...] = _ssm_out(yf_r[...], yb_r[...], xs_r[...], z_r[...], d_r[...], g_r[...]).astype(o_r.dtype)

    row = pl.BlockSpec((TR, 512), lambda i: (i, 0))
    vec = pl.BlockSpec((1, 512), lambda i: (0, 0))
    return _pc(body, name=name, grid=(T // TR,),
               in_specs=[row, row, row, pl.BlockSpec((TR, 512), lambda i: (i, C_Z // 512)), vec, vec],
               out_specs=row, out_shape=_sds((T, 512), BF16),
               compiler_params=_cp(("parallel",), 16 << 20))(yf, yb, act, P, dskip, g)


def ssm_out_bwd(yf, yb, act, P, dskip, g, do_src, name):
    T = yf.shape[0]

    def body(yf_r, yb_r, xs_r, z_r, d_r, g_r, do_r, dy_r, dxs_r, dz_r, dv_r):
        @pl.when(pl.program_id(0) == 0)
        def _():
            dv_r[...] = jnp.zeros_like(dv_r)

        _, vjp = jax.vjp(_ssm_out, yf_r[...], yb_r[...], xs_r[...], z_r[...], d_r[...], g_r[...])
        dyf, _, dxs, dz, dd, dg = vjp(do_r[...].astype(F32))
        dy_r[...] = dyf
        dxs_r[...] = dxs
        dz_r[...] = dz.astype(dz_r.dtype)
        dv_r[0:1, :] += dd
        dv_r[1:2, :] += dg

    row = pl.BlockSpec((TR, 512), lambda i: (i, 0))
    vec = pl.BlockSpec((1, 512), lambda i: (0, 0))
    return _pc(body, name=name, grid=(T // TR,),
               in_specs=[row, row, row, pl.BlockSpec((TR, 512), lambda i: (i, C_Z // 512)), vec, vec,
                         pl.BlockSpec((TR, 512), lambda i: (i, 1))],
               out_specs=[row, row, row, pl.BlockSpec((8, 512), lambda i: (0, 0))],
               out_shape=[_sds((T, 512), F32), _sds((T, 512), F32), _sds((T, 512), BF16), _sds((8, 512), F32)],
               compiler_params=_cp(("arbitrary",), 24 << 20))(yf, yb, act, P, dskip, g, do_src)


def add_halves(xv, got, cvec, name):
    n, r, cdim = xv.shape
    h = r // 2

    def body(c_ref, x_ref, g_ref, o_ref):
        o_ref[...] = (x_ref[...].astype(F32) + g_ref[...].astype(F32)).astype(o_ref.dtype)

    gs = pltpu.PrefetchScalarGridSpec(
        num_scalar_prefetch=1, grid=(n,),
        in_specs=[pl.BlockSpec((1, h, cdim), lambda k, c_ref: (k, c_ref[0], 0)), pl.BlockSpec((1, h, cdim), lambda k, c_ref: (k, 0, 0))],
        out_specs=pl.BlockSpec((1, h, cdim), lambda k, c_ref: (k, 0, 0)))
    return _pc(body, name=name, grid_spec=gs, out_shape=_sds((n, h, cdim), BF16),
               compiler_params=_cp(("arbitrary",), 24 << 20))(cvec, xv, got)


def sum_slots(a, name):
    n, r, cdim = a.shape
    tr = _div_tile(r, 512, 16)

    def body(a_ref, o_ref):
        acc = a_ref[0].astype(F32)
        for k in range(1, n):
            acc = acc + a_ref[k].astype(F32)
        o_ref[...] = acc

    return _pc(body, name=name, grid=(r // tr,), in_specs=[pl.BlockSpec((n, tr, cdim), lambda i: (0, i, 0))],
               out_specs=pl.BlockSpec((tr, cdim), lambda i: (i, 0)), out_shape=_sds((r, cdim), F32),
               compiler_params=_cp(("parallel",), 32 << 20))(a)


def adamw(w, g, m, v, name):
    B, R, C = w.shape
    tr = _div_tile(R, max(8, (1 << 19) // max(C, 1) // 8 * 8), 8) if R % 8 == 0 else R
    c1 = 1.0 / (1.0 - ADAM_B1 ** ADAM_STEP)
    c2 = 1.0 / (1.0 - ADAM_B2 ** ADAM_STEP)

    def body(w_ref, g_ref, m_ref, v_ref, d_ref, mo_ref, vo_ref):
        gg = g_ref[...]
        mn = ADAM_B1 * m_ref[...] + (1.0 - ADAM_B1) * gg
        vn = ADAM_B2 * v_ref[...] + (1.0 - ADAM_B2) * (gg * gg)
        d_ref[...] = -ADAM_LR * ((mn * c1) / (jnp.sqrt(vn * c2) + ADAM_EPS) + ADAM_WD * w_ref[...])
        mo_ref[...] = mn
        vo_ref[...] = vn

    spec = pl.BlockSpec((1, tr, C), lambda b, i: (b, i, 0))
    return _pc(body, name=name, grid=(B, R // tr), in_specs=[spec] * 4, out_specs=[spec] * 3,
               out_shape=[_sds((B, R, C), F32)] * 3, compiler_params=_cp(("parallel", "parallel"), 32 << 20))(w, g, m, v)


def _me():
    return lax.axis_index("x"), lax.axis_index("y"), lax.axis_index("c")


def _flip(v, bit):
    return 1 - v if bit else v


def allgather8(xv, name):
    R = xv.shape[0]

    def body(x_ref, out_ref, sum_ref, send_sems, recv_sems):
        mx, my, mc = _me()
        me = 4 * mx + 2 * my + mc
        out_ref[me] = x_ref[...]
        sends, recvs = [], []
        for k in range(1, 8):
            px, py, pc = _flip(mx, k & 4), _flip(my, k & 2), _flip(mc, k & 1)
            peer = 4 * px + 2 * py + pc
            sends.append(pltpu.make_async_remote_copy(src_ref=x_ref, dst_ref=out_ref.at[me], send_sem=send_sems.at[k - 1],
                                                      recv_sem=recv_sems.at[k - 1], device_id=(px, py, pc), device_id_type=MESH))
            recvs.append(pltpu.make_async_remote_copy(src_ref=x_ref, dst_ref=out_ref.at[peer], send_sem=send_sems.at[k - 1],
                                                      recv_sem=recv_sems.at[k - 1], device_id=(px, py, pc), device_id_type=MESH))
        for cp in sends:
            cp.start()
        for cp in recvs:
            cp.wait_recv()
        for cp in sends:
            cp.wait_send()
        acc = out_ref[0]
        for d in range(1, 8):
            acc = acc + out_ref[d]
        sum_ref[...] = acc

    vm = pl.BlockSpec(memory_space=pltpu.VMEM)
    return _pc(body, name=name, pin=False, in_specs=[vm], out_specs=[vm, vm], out_shape=[_sds((8, R, 128), F32), _sds((R, 128), F32)],
               scratch_shapes=[pltpu.SemaphoreType.DMA((7,)), pltpu.SemaphoreType.DMA((7,))],
               compiler_params=_cp(None, 32 << 20))(xv)


def _other_chips(mx, my):
    return [(1 - mx, my), (mx, 1 - my), (1 - mx, 1 - my)]


def _halves(r, mc, mult):
    h = r // 2
    return pl.ds(pl.multiple_of(mc * h, mult), h), pl.ds(pl.multiple_of((1 - mc) * h, mult), h)


def _rcopy(src, dst, send_sems, recv_sems, k, to):
    return pltpu.make_async_remote_copy(src_ref=src, dst_ref=dst, send_sem=send_sems.at[k], recv_sem=recv_sems.at[k],
                                        device_id=to, device_id_type=MESH)


def _gather_body(xs, outs, send_sems, recv_sems):
    n = len(xs)
    mx, my, mc = _me()
    chip = 2 * mx + my
    sib = (mx, my, 1 - mc)
    chips = _other_chips(mx, my)
    idx = [2 * cx + cy for cx, cy in chips]
    cp = functools.partial(_rcopy, send_sems=send_sems, recv_sems=recv_sems)
    hv = [_halves(x.shape[0], mc, 16) for x in xs]
    first, passed = [], []
    for a in range(n):
        for j, (cx, cy) in enumerate(chips):
            first.append(cp(xs[a].at[hv[a][0]], outs[a].at[chip, hv[a][0]], k=6 * a + j, to=(cx, cy, mc)))
            first[-1].start()
    for a in range(n):
        for j in range(3):
            cp(xs[a].at[hv[a][0]], outs[a].at[idx[j], hv[a][0]], k=6 * a + j, to=sib).wait_recv()
            passed.append(cp(outs[a].at[idx[j], hv[a][0]], outs[a].at[idx[j], hv[a][0]], k=6 * a + 3 + j, to=sib))
            passed[-1].start()
    for a in range(n):
        for j in range(3):
            cp(xs[a].at[hv[a][1]], outs[a].at[idx[j], hv[a][1]], k=6 * a + 3 + j, to=sib).wait_recv()
    for c_ in first + passed:
        c_.wait_send()


def _my_chip():
    return 2 * lax.axis_index("x") + lax.axis_index("y")


def _own_slots(outs, shards):
    return [lax.dynamic_update_index_in_dim(o, x, _my_chip(), 0) for o, x in zip(outs, shards)]


def gather_weights(shards, name):
    n = len(shards)

    def body(*refs):
        _gather_body(refs[:n], refs[n:2 * n], *refs[2 * n:])

    hbm = pl.BlockSpec(memory_space=pl.ANY)
    outs = _pc(body, name=name, in_specs=[hbm] * n, out_specs=[hbm] * n, out_shape=[_sds((4,) + x.shape, x.dtype) for x in shards],
               scratch_shapes=[pltpu.SemaphoreType.DMA((6 * n,)), pltpu.SemaphoreType.DMA((6 * n,))])(*shards)
    return _own_slots(outs, shards)


GATHER_REST_ID = 3


def gather_weights_sc(shards, name):
    n = len(shards)
    x_refs = [jax.new_ref(x, memory_space=pltpu.MemorySpace.HBM) for x in shards]
    out_refs = [jax.empty_ref(_sds((4,) + x.shape, x.dtype), memory_space=pltpu.MemorySpace.HBM) for x in shards]

    @pl.kernel(mesh=plsc.ScalarSubcoreMesh(axis_name="sc", num_cores=1), name=name,
               scratch_types=(pltpu.SemaphoreType.DMA((6 * n,)), pltpu.SemaphoreType.DMA((6 * n,))),
               compiler_params=pltpu.CompilerParams(collective_id=GATHER_REST_ID))
    def launch(send_sems, recv_sems):
        mx, my, mc = _me()
        barrier = pltpu.get_barrier_semaphore()
        for peer in [(mx, my, 1 - mc)] + [(cx, cy, mc) for cx, cy in _other_chips(mx, my)]:
            pl.semaphore_signal(barrier, inc=1, device_id=peer, device_id_type=MESH)
        pl.semaphore_wait(barrier, 4)
        _gather_body(x_refs, out_refs, send_sems, recv_sems)

    launch()
    return _own_slots([o[...] for o in out_refs], shards)


def swap_halves(arrs, name):
    n = len(arrs)

    def body(*refs):
        xs, outs = refs[:n], refs[n:2 * n]
        send_sems, recv_sems = refs[2 * n:]
        mx, my, mc = _me()
        cps = []
        for a in range(n):
            theirs = _halves(xs[a].shape[1], mc, 16)[1]
            cps.append(_rcopy(xs[a].at[pl.ds(0, 4), theirs], outs[a], send_sems, recv_sems, a, (mx, my, 1 - mc)))
            cps[-1].start()
        for c_ in cps:
            c_.wait()

    hbm = pl.BlockSpec(memory_space=pl.ANY)
    return _pc(body, name=name, in_specs=[hbm] * n, out_specs=[hbm] * n,
               out_shape=[_sds((4, x.shape[1] // 2, x.shape[2]), x.dtype) for x in arrs],
               scratch_shapes=[pltpu.SemaphoreType.DMA((n,)), pltpu.SemaphoreType.DMA((n,))])(*arrs)


SCATTER_ID = 4


def scatter_chips_sc(arrs, name):
    n = len(arrs)
    x_refs = [jax.new_ref(x, memory_space=pltpu.MemorySpace.HBM) for x in arrs]
    out_refs = [jax.empty_ref(_sds(x.shape, x.dtype), memory_space=pltpu.MemorySpace.HBM) for x in arrs]

    @pl.kernel(mesh=plsc.ScalarSubcoreMesh(axis_name="sc", num_cores=1), name=name,
               scratch_types=(pltpu.SemaphoreType.DMA((3 * n,)), pltpu.SemaphoreType.DMA((3 * n,))),
               compiler_params=pltpu.CompilerParams(collective_id=SCATTER_ID))
    def launch(send_sems, recv_sems):
        mx, my, mc = _me()
        chip = 2 * mx + my
        chips = _other_chips(mx, my)
        idx = [2 * cx + cy for cx, cy in chips]
        barrier = pltpu.get_barrier_semaphore()
        for cx, cy in chips:
            pl.semaphore_signal(barrier, inc=1, device_id=(cx, cy, mc), device_id_type=MESH)
        pl.semaphore_wait(barrier, 3)
        cp = functools.partial(_rcopy, send_sems=send_sems, recv_sems=recv_sems)
        sends = []
        for a in range(n):
            for j, (cx, cy) in enumerate(chips):
                sends.append(cp(x_refs[a].at[idx[j]], out_refs[a].at[chip], k=3 * a + j, to=(cx, cy, mc)))
                sends[-1].start()
        for a in range(n):
            for j, (cx, cy) in enumerate(chips):
                cp(x_refs[a].at[idx[j]], out_refs[a].at[idx[j]], k=3 * a + j, to=(cx, cy, mc)).wait_recv()
        for c_ in sends:
            c_.wait_send()

    launch()
    return _own_slots([o[...] for o in out_refs], [lax.dynamic_index_in_dim(x, _my_chip(), 0, keepdims=False) for x in arrs])


def share_halves(parts, name):
    flat = [p for w in parts for p in w]
    nw, n = len(parts), len(flat)
    depth = n // nw

    def body(*refs):
        xs, outs = refs[:n], refs[n:n + nw]
        send_sems, recv_sems = refs[n + nw:]
        mx, my, mc = _me()
        sib = (mx, my, 1 - mc)
        sends, recvs = [], []
        for a in range(n):
            w, l = a // depth, a % depth
            mine, theirs = _halves(outs[w].shape[1], mc, 8)
            sends.append(_rcopy(xs[a], outs[w].at[l, mine], send_sems, recv_sems, a, sib))
            recvs.append(_rcopy(xs[a], outs[w].at[l, theirs], send_sems, recv_sems, a, sib))
            sends[-1].start()
        for c_ in recvs:
            c_.wait_recv()
        for c_ in sends:
            c_.wait_send()

    hbm = pl.BlockSpec(memory_space=pl.ANY)
    outs = _pc(body, name=name, in_specs=[hbm] * n, out_specs=[hbm] * nw,
               out_shape=[_sds((depth, 2 * w[0].shape[0], w[0].shape[1]), F32) for w in parts],
               scratch_shapes=[pltpu.SemaphoreType.DMA((n,)), pltpu.SemaphoreType.DMA((n,))])(*flat)
    outs = list(outs)
    mc = lax.axis_index("c")
    for w in range(nw):
        for l in range(depth):
            h = parts[w][l].shape[0]
            outs[w] = lax.dynamic_update_slice(outs[w], parts[w][l][None], (l, mc * h, 0))
    return outs


_BIG = ("w_in", "w_out", "w_ffn_in", "w_ffn_out")
N_CHIPS = 4
DEPTH = 2


def _pad_rows(v, mult=8):
    n = v.shape[0]
    rows = -(-n // 128)
    rows = -(-rows // mult) * mult
    return jnp.pad(v, (0, rows * 128 - n)).reshape(rows, 128)


class _Flat:
    def __init__(self):
        self.items = []

    def add(self, name, a):
        self.items.append((name, a.shape, a.reshape(-1).astype(F32)))

    def rows(self):
        return _pad_rows(jnp.concatenate([a for _, _, a in self.items]))

    def split(self, rows):
        flat = rows.reshape(-1)
        out, o = {}, 0
        for name, shape, a in self.items:
            out[name] = flat[o:o + a.shape[0]].reshape(shape)
            o += a.shape[0]
        return out

    def split_lead(self, rows3):
        n = rows3.shape[0]
        flat = rows3.reshape(n, -1)
        out, o = {}, 0
        for name, shape, a in self.items:
            out[name] = flat[:, o:o + a.shape[0]].reshape((n,) + tuple(shape))
            o += a.shape[0]
        return out


def _gsv(rows):
    z = jnp.zeros((2, D), F32)
    r = [z if a is None else a for a in rows] + [z] * 5
    return jnp.stack(r, axis=1)


def _pad8(a, rows=8, cols=128):
    return jnp.zeros((rows, cols), F32).at[:a.shape[0], :a.shape[1]].set(a.astype(F32))


def kernel(x, c, ctx, c_ctx, w_mod, b_mod, g_mix, w_in, wa_sink, na_rpb, ssm_conv_w, ssm_conv_b, ssm_dt_bias, ssm_a_log, ssm_d, ssm_norm_g, w_out, g_ffn, w_ffn_in, w_ffn_out, g_final, loss_target, m_c_ctx, m_w_mod, m_b_mod, m_g_mix, m_w_in, m_wa_sink, m_na_rpb, m_ssm_conv_w, m_ssm_conv_b, m_ssm_dt_bias, m_ssm_a_log, m_ssm_d, m_ssm_norm_g, m_w_out, m_g_ffn, m_w_ffn_in, m_w_ffn_out, m_g_final, v_c_ctx, v_w_mod, v_b_mod, v_g_mix, v_w_in, v_wa_sink, v_na_rpb, v_ssm_conv_w, v_ssm_conv_b, v_ssm_dt_bias, v_ssm_a_log, v_ssm_d, v_ssm_norm_g, v_w_out, v_g_ffn, v_w_ffn_in, v_w_ffn_out, v_g_final):
    L, Lc = x.shape[1], ctx.shape[1]
    T = L + Lc
    nL = L // TR
    mx, my, mc = lax.axis_index("x"), lax.axis_index("y"), lax.axis_index("c")
    dev = 4 * mx + 2 * my + mc
    chip = 2 * mx + my
    MODW = 6 * D // N_CHIPS
    CW = 1024 // N_CHIPS

    sc = _silu(c.astype(F32))
    scc = _silu(c_ctx.astype(F32))[None]
    f1 = _Flat()
    f1.add("sc", sc)
    f1.add("conv_w", ssm_conv_w)
    g1, _ = allgather8(f1.rows(), "gather_cond")
    g1 = f1.split_lead(g1)
    sc_all = g1["sc"][:, 0]
    conv_w = jnp.concatenate([g1["conv_w"][2 * k] for k in range(N_CHIPS)], axis=-1)
    A16 = jnp.concatenate([sc_all, scc, jnp.zeros((7, D), F32)], axis=0)

    mod_part = matmul_layers(A16, w_mod, "nn", "mod_fwd")
    f2 = _Flat()
    f2.add("mod", mod_part)
    g2, _ = allgather8(f2.rows(), "gather_mod")
    g2 = f2.split_lead(g2)["mod"]
    mods = jnp.concatenate([g2[2 * k] for k in range(N_CHIPS)], axis=-1) + b_mod[:, None, :]
    mod_l = lax.dynamic_index_in_dim(mods, dev, axis=1, keepdims=False).reshape(DEPTH, 6, D)
    mod_c = mods[:, 8].reshape(DEPTH, 6, D)
    mod = jnp.stack([mod_l, mod_c], axis=1)
    mrow = lambda l, j: mod[l, :, j]

    own = {"w_in": w_in, "w_out": w_out, "w_ffn_in": w_ffn_in, "w_ffn_out": w_ffn_out}
    sh16 = [own[n][l].astype(BF16) for n in _BIG for l in range(DEPTH)]
    after_mod = (g2[0, 0, 0, 0] * 0).astype(BF16)
    gath = list(gather_weights([sh16[0] + after_mod], "gather_first"))
    after_first = (gath[0][0, 0, 0] * 0).astype(BF16)
    gath += list(gather_weights_sc([sh16[1] + after_first] + sh16[2:], "gather_rest"))
    gw = {n: [gath[DEPTH * i + l] for l in range(DEPTH)] for i, n in enumerate(_BIG)}
    W_in = [jnp.pad(jnp.concatenate([g[k] for k in range(N_CHIPS)], axis=1), ((0, 0), (0, IN_PAD - IN_COLS))) for g in gw["w_in"]]
    W_out = [g.reshape(D, D) for g in gw["w_out"]]
    W_fo = [g.reshape(D_FF, D) for g in gw["w_ffn_out"]]
    W_fi = gw["w_ffn_in"]

    cos, sin, rotm = rope_tables(L, Lc)
    x0 = jnp.concatenate([x[0], ctx[0]], axis=0).astype(F32)

    sv = []
    xin = x0
    gsv_first = _gsv([None, mrow(0, 0), mrow(0, 1)])
    _, h1 = res_norm_mod(x0, None, gsv_first, g_mix[0][None], nL, "norm_first")
    for l in range(DEPTH):
        s = {"xin": xin, "h1": h1}
        P = matmul(h1, W_in[l], "nn", F32, f"in_proj{l}", tn=IN_PAD)
        qr, kr, kb, vb = rope_apply(P, C_QA // 256, P, C_KA // 128, cos, sin, rotm, False, f"rope{l}", kv_src=P)
        sink8 = _pad8(jnp.broadcast_to(wa_sink[l][:, None], (WA_HEADS, 128)))
        krs, va = _swap_halves_lanes(kr), P[:, C_VA:C_VA + 128]
        vas = _swap_halves_lanes(va)
        oa, sta = win_attn_fwd(qr, kr, krs, va, vas, sink8, L, Lc, f"wa_fwd{l}")
        bias = na_bias_table(na_rpb[l], l)
        ob, stb = na_fwd(P, kb, vb, bias, L, Lc, f"na_fwd{l}")
        w8 = jnp.concatenate([conv_w[l], jnp.zeros((1, 1024), F32)], axis=0)
        pre, act = conv_silu_fwd(P, w8, ssm_conv_b[l][None], nL, f"conv_fwd{l}")
        dtb8, al8 = _pad8(ssm_dt_bias[l]), _pad8(ssm_a_log[l])
        yf, yb, hsf, hsb = ssd_fwd(act, P, dtb8, al8, L, Lc, f"ssd_fwd{l}")
        dskip = jnp.repeat(ssm_d[l], S_P)[None]
        oc = ssm_out_fwd(yf, yb, act, P, dskip, ssm_norm_g[l][None], f"ssm_out_fwd{l}")
        mixin = [(oa, 0), (ob, 256), (oc, 512)]
        mix = out_proj_fwd(mixin, W_out[l], f"out_proj{l}")
        gsv_mid = _gsv([mrow(l, 2), mrow(l, 3), mrow(l, 4)])
        x1, h2 = res_norm_mod(xin, mix, gsv_mid, g_ffn[l][None], nL, f"norm_mid{l}")
        gu, af = ffn_in_swiglu(h2, W_fi[l], f"ffn_in{l}")
        fo = matmul(af, W_fo[l], "nn", BF16, f"ffn_out{l}", tk=D_FF)
        s.update(P=P, qr=qr, kr=kr, krs=krs, va=va, vas=vas, sink8=sink8, oa=oa, sta=sta, ob=ob, stb=stb, kb=kb, vb=vb, bias=bias, w8=w8, pre=pre, act=act, dtb8=dtb8, al8=al8, yf=yf,
                 yb=yb, hsf=hsf, hsb=hsb, dskip=dskip, mixin=mixin, mix=mix, gsv_mid=gsv_mid, x1=x1, h2=h2, gu=gu, af=af, fo=fo)
        if l + 1 < DEPTH:
            s["gsv_end"] = _gsv([mrow(l, 5), mrow(l + 1, 0), mrow(l + 1, 1)])
            xin, h1 = res_norm_mod(x1, fo, s["gsv_end"], g_mix[l + 1][None], nL, f"norm_end{l}")
        else:
            s["gsv_end"] = _gsv([mrow(l, 5), None, None])
        sv.append(s)

    last = sv[-1]
    loss8, dres, dfo, dgsv_end, dg_final = final_loss(last["x1"], last["fo"], last["gsv_end"], g_final[None], loss_target[0].astype(F32), nL, "final_loss")
    loss = lax.psum(loss8[0, 0], ("x", "y", "c"))

    dmod = [[None] * 6 for _ in range(DEPTH)]
    gW = {n: [None] * DEPTH for n in _BIG}
    small = [dict() for _ in range(DEPTH)]
    parts = [None] * DEPTH
    cvec = mc.astype(jnp.int32).reshape(1)
    grad_x = None
    for l in reversed(range(DEPTH)):
        s = sv[l]
        dmod[l][5] = dgsv_end[:, 0]
        if l + 1 < DEPTH:
            dmod[l + 1][0], dmod[l + 1][1] = dgsv_end[:, 1], dgsv_end[:, 2]
        dgu = ffn_out_dx_swiglu(dfo, W_fo[l], s["gu"], f"ffn_out_dx{l}")
        gW["w_ffn_out"][l] = matmul(s["af"], dfo, "tn", BF16, f"ffn_out_dw{l}", tm=1408, tk=T).reshape(N_CHIPS, D_FF // N_CHIPS, D)
        dh2 = matmul_fi(dgu, W_fi[l], "nt", BF16, f"ffn_in_dx{l}")
        gW["w_ffn_in"][l] = matmul_fi(s["h2"], dgu, "tn", BF16, f"ffn_in_dw{l}")
        dres, dmix, dgsv_mid, dg_ffn = res_norm_mod_bwd(s["x1"], s["mix"], s["gsv_mid"], g_ffn[l][None], dh2, dres, nL, f"norm_mid_bwd{l}")
        dmod[l][2], dmod[l][3], dmod[l][4] = dgsv_mid[:, 0], dgsv_mid[:, 1], dgsv_mid[:, 2]
        dmixin = matmul(dmix, W_out[l], "nt", BF16, f"out_proj_dx{l}")
        gW["w_out"][l] = out_proj_dw(s["mixin"], dmix, f"out_proj_dw{l}").reshape(N_CHIPS, D // N_CHIPS, D)
        P = s["P"]
        dqr, dkr, dkrs, dva, dvas, dsink = win_attn_bwd(s["qr"], s["kr"], s["krs"], s["va"], s["vas"], s["sink8"], dmixin, s["oa"], s["sta"], L, Lc,
                                                        f"wa_bwd{l}")
        dkr, dva = dkr + _swap_halves_lanes(dkrs), dva + _swap_halves_lanes(dvas)
        dqa, dka = rope_apply(dqr, 0, dkr[WA_BLK:WA_BLK + T], 0, cos, sin, rotm, True, f"rope_bwd{l}")
        dqb, dkb, dvb, dbias = na_bwd(P, s["kb"], s["vb"], s["bias"], dmixin, s["ob"], s["stb"], L, Lc, f"na_bwd{l}")
        dy, dxs1, dz, dvec = ssm_out_bwd(s["yf"], s["yb"], s["act"], P, s["dskip"], ssm_norm_g[l][None], dmixin, f"ssm_out_bwd{l}")
        dxf, dbf, dcf, ddf, dxb, dbb, dcb, ddb, ddtb, dal = ssd_bwd(s["act"], P, s["dtb8"], s["al8"], s["hsf"], s["hsb"], dy, L, Lc, f"ssd_bwd{l}")
        dpre = dsilu(s["pre"], [dxf, dxb, dxs1], [dbf, dbb], [dcf, dcb], f"dsilu{l}")
        dxbc, dw8, db8 = conv_bwd(dpre, P, s["w8"], nL, f"conv_bwd{l}")
        ddt = jnp.concatenate([ddf, ddb, jnp.zeros((T, IN_PAD - IN_COLS), F32)], axis=1)
        pieces = [(dqa, C_QA), (dqb, C_QB), (dz, C_Z), (dka, C_KA), (dva[WA_BLK:WA_BLK + T], C_VA), (dkb, C_KB), (dvb, C_VB),
                  (dxbc, C_XBC), (ddt, C_DT)]
        dh1, dwin = in_proj_bwd(pieces, s["h1"], W_in[l], f"in_proj_bwd{l}")
        cw = IN_COLS // N_CHIPS
        gW["w_in"][l] = jnp.stack([dwin[:, k * cw:(k + 1) * cw] for k in range(N_CHIPS)])
        garr = [gW[n][l] for n in _BIG]
        got = swap_halves(garr, f"reduce_d2d{l}")
        chip_sum = [add_halves(garr[a], got[a], cvec, f"reduce_add_pair{l}_{a}") for a in range(len(garr))]
        parts[l] = scatter_chips_sc(chip_sum, f"reduce_ici{l}")
        small[l] = dict(g_ffn=dg_ffn[0], wa_sink=dsink[:WA_HEADS, 0], na_rpb=na_rpb_grad(dbias, l), conv_w=dw8[:S_CONV], conv_b=db8[0],
                        dt_bias=ddtb[:2, :8], a_log=dal[:2, :8], ssm_d=dvec[0].reshape(S_HEADS, S_P).sum(axis=1), norm_g=dvec[1])
        if l > 0:
            p = sv[l - 1]
            dres, dfo, dgsv_end, dg_mix = res_norm_mod_bwd(s["xin"], p["fo"], p["gsv_end"], g_mix[l][None], dh1, dres, nL, f"norm_end_bwd{l - 1}")
        else:
            grad_x, _, dgsv_first, dg_mix = res_norm_mod_bwd(s["xin"], None, gsv_first, g_mix[0][None], dh1, dres, nL, "norm_first_bwd")
            dmod[0][0], dmod[0][1] = dgsv_first[:, 1], dgsv_first[:, 2]
        small[l]["g_mix"] = dg_mix[0]
    for l in range(DEPTH):
        for j in range(6):
            if dmod[l][j] is None:
                dmod[l][j] = jnp.zeros((2, D), F32)
    dmod = jnp.stack([jnp.stack(r, axis=1) for r in dmod])

    f3 = _Flat()
    f3.add("dmod_l", dmod[:, 0].reshape(DEPTH, 6 * D))
    f3.add("dmod_c", dmod[:, 1].reshape(DEPTH, 6 * D))
    f3.add("g_final", dg_final[0])
    for n in ("g_mix", "g_ffn", "wa_sink", "na_rpb", "conv_w", "conv_b", "dt_bias", "a_log", "ssm_d", "norm_g"):
        f3.add(n, jnp.stack([small[l][n] for l in range(DEPTH)]))
    g3, s3 = allgather8(f3.rows(), "reduce_small")
    dmod_all = f3.split_lead(g3)["dmod_l"]
    s3 = f3.split(s3)
    dmodc_tot = s3["dmod_c"]
    col0 = chip * MODW
    G16, G16c = [], []
    for l in range(DEPTH):
        rows = jnp.concatenate([dmod_all[:, l], dmodc_tot[l][None], jnp.zeros((7, 6 * D), F32)], axis=0)
        G16.append(lax.dynamic_slice_in_dim(rows, col0, MODW, axis=1))
        rc = jnp.concatenate([dmodc_tot[l][None], jnp.zeros((15, 6 * D), F32)], axis=0)
        G16c.append(lax.dynamic_slice_in_dim(rc, col0, MODW, axis=1))
    grad_w_mod = matmul_layers(A16, jnp.stack(G16), "tn", "mod_dw")
    dscc_part = matmul_layers(jnp.stack(G16c), w_mod, "nt", "mod_dx")[:, 0].sum(axis=0)
    _, s4 = allgather8(_pad_rows(dscc_part * (mc == 1).astype(F32)), "reduce_cctx")
    dscc = s4.reshape(-1)[:D]
    cc = c_ctx.astype(F32)
    sg = 1.0 / (1.0 + jnp.exp(-cc))
    grad_c_ctx = dscc * (sg * (1.0 + cc * (1.0 - sg)))

    halves = [[sum_slots(parts[l][i], f"reduce_add_chips{l}_{i}") for l in range(DEPTH)] for i in range(len(_BIG))]
    gsh = dict(zip(_BIG, share_halves(halves, "reduce_share")))

    grads = {"c_ctx": grad_c_ctx, "w_mod": grad_w_mod, "b_mod": s3["dmod_l"] + s3["dmod_c"], "g_mix": s3["g_mix"], "w_in": gsh["w_in"],
             "wa_sink": s3["wa_sink"], "na_rpb": s3["na_rpb"],
             "ssm_conv_w": lax.dynamic_slice_in_dim(s3["conv_w"], chip * CW, CW, axis=2), "ssm_conv_b": s3["conv_b"],
             "ssm_dt_bias": s3["dt_bias"], "ssm_a_log": s3["a_log"], "ssm_d": s3["ssm_d"], "ssm_norm_g": s3["norm_g"],
             "w_out": gsh["w_out"], "g_ffn": s3["g_ffn"], "w_ffn_in": gsh["w_ffn_in"], "w_ffn_out": gsh["w_ffn_out"], "g_final": s3["g_final"]}
    wts = {"c_ctx": c_ctx, "w_mod": w_mod, "b_mod": b_mod, "g_mix": g_mix, "w_in": w_in, "wa_sink": wa_sink, "na_rpb": na_rpb,
           "ssm_conv_w": ssm_conv_w, "ssm_conv_b": ssm_conv_b, "ssm_dt_bias": ssm_dt_bias, "ssm_a_log": ssm_a_log, "ssm_d": ssm_d,
           "ssm_norm_g": ssm_norm_g, "w_out": w_out, "g_ffn": g_ffn, "w_ffn_in": w_ffn_in, "w_ffn_out": w_ffn_out, "g_final": g_final}
    ms = {"c_ctx": m_c_ctx, "w_mod": m_w_mod, "b_mod": m_b_mod, "g_mix": m_g_mix, "w_in": m_w_in, "wa_sink": m_wa_sink, "na_rpb": m_na_rpb,
          "ssm_conv_w": m_ssm_conv_w, "ssm_conv_b": m_ssm_conv_b, "ssm_dt_bias": m_ssm_dt_bias, "ssm_a_log": m_ssm_a_log, "ssm_d": m_ssm_d,
          "ssm_norm_g": m_ssm_norm_g, "w_out": m_w_out, "g_ffn": m_g_ffn, "w_ffn_in": m_w_ffn_in, "w_ffn_out": m_w_ffn_out, "g_final": m_g_final}
    vs = {"c_ctx": v_c_ctx, "w_mod": v_w_mod, "b_mod": v_b_mod, "g_mix": v_g_mix, "w_in": v_w_in, "wa_sink": v_wa_sink, "na_rpb": v_na_rpb,
          "ssm_conv_w": v_ssm_conv_w, "ssm_conv_b": v_ssm_conv_b, "ssm_dt_bias": v_ssm_dt_bias, "ssm_a_log": v_ssm_a_log, "ssm_d": v_ssm_d,
          "ssm_norm_g": v_ssm_norm_g, "w_out": v_w_out, "g_ffn": v_g_ffn, "w_ffn_in": v_w_ffn_in, "w_ffn_out": v_w_ffn_out, "g_final": v_g_final}
    names = list(wts)
    grads = {n: grads[n].reshape(wts[n].shape).astype(F32) for n in names}
    big = ("w_mod", "w_in", "w_out", "w_ffn_in", "w_ffn_out")
    delta, new_m, new_v = {}, {}, {}
    for n in big:
        delta[n], new_m[n], new_v[n] = adamw(wts[n], grads[n], ms[n], vs[n], f"adamw_{n}")
    packs = []
    for src in (wts, grads, ms, vs):
        f = _Flat()
        for n in names:
            if n not in big:
                f.add(n, src[n])
        packs.append(f)
    d_, m_, v_ = adamw(*[f.rows()[None] for f in packs], "adamw_small")
    for dst, rows in ((delta, d_), (new_m, m_), (new_v, v_)):
        dst.update(packs[0].split(rows[0]))

    return (loss, grad_x[:L][None], *[grads[n] for n in names], *[delta[n] for n in names],
            *[new_m[n] for n in names], *[new_v[n] for n in names])
```

```python
import functools

import numpy as np
import jax
import jax.numpy as jnp
from jax import lax
from jax.experimental import pallas as pl
from jax.experimental.pallas import tpu as pltpu
from jax.experimental.pallas import tpu_sc as plsc

F32 = jnp.float32
BF16 = jnp.bfloat16
_MXU = jnp.bfloat16
_HI = lax.Precision.HIGHEST
MESH = pl.DeviceIdType.MESH

D = 1024
HD = 64
GRID_W = 64
EPS = 1e-6
ROPE_BASE = 10000.0
WA_HEADS, WA_KV = 4, 2
WA_BLK = 128
NA_HEADS, NA_KH, NA_KW = 4, 8, 16
S_HEADS, S_P, S_INNER, S_GROUPS, S_N, S_CONV, S_Q = 8, 64, 512, 2, 128, 7, 128
D_FF = 2816
IN_COLS = 2832
IN_PAD = 2944
C_QA, C_QB, C_Z, C_KA, C_VA, C_KB, C_VB, C_XBC, C_DT = 0, 256, 512, 1024, 1152, 1280, 1536, 1792, 2816
ADAM_LR, ADAM_B1, ADAM_B2, ADAM_EPS, ADAM_WD, ADAM_STEP = 0.001, 0.9, 0.999, 1e-08, 0.01, 10

TR = 256
NEG = -1e30
VMEM_CAP = 56 * 1024 * 1024


PIN_BYTES = 256 * 1024


def _is_big(a):
    return hasattr(a, "shape") and len(a.shape) >= 2 and int(np.prod(a.shape)) * jnp.dtype(a.dtype).itemsize >= PIN_BYTES


def _pc(body, *, out_shape, pin=True, **kw):
    if not pin:
        return pl.pallas_call(body, out_shape=out_shape, **kw)
    one = isinstance(out_shape, jax.ShapeDtypeStruct)
    outs = [pltpu.HBM(s.shape, s.dtype) if _is_big(s) else s for s in ([out_shape] if one else out_shape)]
    call = pl.pallas_call(body, out_shape=outs[0] if one else outs, **kw)
    return lambda *args: call(*[pltpu.with_memory_space_constraint(a, pltpu.HBM) if _is_big(a) else a for a in args])


def _cp(sem=None, vmem=None):
    kw = {}
    if sem is not None:
        kw["dimension_semantics"] = sem
    if vmem is not None:
        kw["vmem_limit_bytes"] = int(min(max(vmem, 16 * 1024 * 1024), VMEM_CAP))
    return pltpu.CompilerParams(**kw)


def _sds(shape, dtype):
    return jax.ShapeDtypeStruct(tuple(shape), dtype)


_DIMS = {"nn": ((1,), (0,)), "nt": ((1,), (1,)), "tn": ((0,), (0,))}


def _dg(a, b, dims):
    return lax.dot_general(a.astype(_MXU), b.astype(_MXU), (dims, ((), ())), preferred_element_type=F32)


@functools.partial(jax.custom_vjp, nondiff_argnums=(2,))
def bdot(a, b, mode):
    return _dg(a, b, _DIMS[mode])


def _bdot_fwd(a, b, mode):
    return bdot(a, b, mode), (a, b)


def _bdot_bwd(mode, res, g):
    a, b = res
    if mode == "nn":
        return bdot(g, b, "nt"), bdot(a, g, "tn")
    if mode == "nt":
        return bdot(g, b, "nn"), bdot(g, a, "tn")
    return bdot(b, g, "nt"), bdot(a, g, "nn")


bdot.defvjp(_bdot_fwd, _bdot_bwd)


def hdot(a, b, mode="nn"):
    return lax.dot_general(a, b, (_DIMS[mode], ((), ())), precision=_HI, preferred_element_type=F32)


def _silu(x):
    return x / (1.0 + jnp.exp(-x))


def _softplus(x):
    return jnp.maximum(x, 0.0) + jnp.log(1.0 + jnp.exp(-jnp.abs(x)))


def _div_tile(n, cap, mult):
    if n <= cap:
        return n
    best = None
    for t in range(mult, cap + 1, mult):
        if n % t == 0:
            best = t
    assert best is not None, (n, cap, mult)
    return best


def matmul(a, b, mode, out_dtype, name, tm=640, tn=1536, tk=1408, hi=False):
    if mode == "tn":
        K, M = a.shape
    else:
        M, K = a.shape
    N = b.shape[0] if mode == "nt" else b.shape[1]
    tm = _div_tile(M, tm, 128 if mode == "tn" else 16)
    tn = _div_tile(N, tn, 128)
    tk = _div_tile(K, tk, 128 if mode != "tn" else 16)
    nk = K // tk
    dims = _DIMS[mode]

    def body(a_ref, b_ref, o_ref, *acc):
        if hi:
            part = lax.dot_general(a_ref[...], b_ref[...], (dims, ((), ())), precision=_HI, preferred_element_type=F32)
        else:
            part = _dg(a_ref[...], b_ref[...], dims)
        if nk == 1:
            o_ref[...] = part.astype(o_ref.dtype)
        else:
            k = pl.program_id(2)

            @pl.when(k == 0)
            def _():
                acc[0][...] = part

            @pl.when(k > 0)
            def _():
                acc[0][...] += part

            @pl.when(k == nk - 1)
            def _():
                o_ref[...] = acc[0][...].astype(o_ref.dtype)

    if mode == "tn":
        a_spec = pl.BlockSpec((tk, tm), lambda i, j, k: (k, i))
    else:
        a_spec = pl.BlockSpec((tm, tk), lambda i, j, k: (i, k))
    if mode == "nt":
        b_spec = pl.BlockSpec((tn, tk), lambda i, j, k: (j, k))
    else:
        b_spec = pl.BlockSpec((tk, tn), lambda i, j, k: (k, j))
    isz = lambda x: jnp.dtype(x.dtype).itemsize
    vmem = 2 * (tm * tk * isz(a) + tk * tn * isz(b) + tm * tn * jnp.dtype(out_dtype).itemsize) + 3 * tm * tn * 4
    return _pc(
        body, name=name, grid=(M // tm, N // tn, nk),
        in_specs=[a_spec, b_spec], out_specs=pl.BlockSpec((tm, tn), lambda i, j, k: (i, j)),
        out_shape=_sds((M, N), out_dtype),
        scratch_shapes=[pltpu.VMEM((tm, tn), F32)] if nk > 1 else [],
        compiler_params=_cp(("parallel", "parallel", "arbitrary"), vmem + (8 << 20)),
    )(a, b)


def matmul_layers(a, b, mode, name):
    nl = b.shape[0]
    a3 = a if a.ndim == 3 else a[None]
    shared = a3.shape[0] == 1
    M = a3.shape[2] if mode == "tn" else a3.shape[1]
    N = b.shape[1] if mode == "nt" else b.shape[2]

    def body(a_ref, b_ref, o_ref):
        o_ref[0] = _dg(a_ref[0], b_ref[0], _DIMS[mode])

    return _pc(body, name=name, grid=(nl,),
               in_specs=[pl.BlockSpec((1,) + a3.shape[1:], (lambda l: (0, 0, 0)) if shared else (lambda l: (l, 0, 0))),
                         pl.BlockSpec((1,) + b.shape[1:], lambda l: (l, 0, 0))],
               out_specs=pl.BlockSpec((1, M, N), lambda l: (l, 0, 0)), out_shape=_sds((nl, M, N), F32),
               compiler_params=_cp(("parallel",), 48 << 20))(a3, b)


def out_proj_fwd(pieces, w, name):
    T = pieces[0][0].shape[0]
    arrs, offs = [a for a, _ in pieces], [o for _, o in pieces]
    n = len(arrs)
    tm = _div_tile(T, 640, 16)

    def body(*refs):
        w_ref, o_ref = refs[n], refs[n + 1]
        acc = None
        for j in range(n):
            part = _dg(refs[j][...], w_ref[offs[j]:offs[j] + arrs[j].shape[1], :], _DIMS["nn"])
            acc = part if acc is None else acc + part
        o_ref[...] = acc.astype(o_ref.dtype)

    return _pc(body, name=name, grid=(T // tm,),
               in_specs=[pl.BlockSpec((tm, a.shape[1]), lambda i: (i, 0)) for a in arrs] + [pl.BlockSpec(w.shape, lambda i: (0, 0))],
               out_specs=pl.BlockSpec((tm, w.shape[1]), lambda i: (i, 0)), out_shape=_sds((T, w.shape[1]), BF16),
               compiler_params=_cp(("parallel",), 32 << 20))(*arrs, w)


def out_proj_dw(pieces, dy, name):
    T, N = dy.shape
    arrs, offs = [a for a, _ in pieces], [o for _, o in pieces]
    n = len(arrs)
    rows = sum(a.shape[1] for a in arrs)
    tn = 512

    def body(*refs):
        d_ref, o_ref = refs[n], refs[n + 1]
        for j in range(n):
            o_ref[offs[j]:offs[j] + arrs[j].shape[1], :] = _dg(refs[j][...], d_ref[...], _DIMS["tn"]).astype(o_ref.dtype)

    return _pc(body, name=name, grid=(N // tn,),
               in_specs=[pl.BlockSpec(a.shape, lambda j: (0, 0)) for a in arrs] + [pl.BlockSpec((T, tn), lambda j: (0, j))],
               out_specs=pl.BlockSpec((rows, tn), lambda j: (0, j)), out_shape=_sds((rows, N), BF16),
               compiler_params=_cp(("parallel",), 48 << 20))(*arrs, dy)


def in_proj_bwd(pieces, h1, w, name):
    T = h1.shape[0]
    arrs = [a for a, _ in pieces]
    offs = [o for _, o in pieces]
    wid = [a.shape[1] for a in arrs]
    n = len(arrs)
    assert sum(wid) == IN_PAD, "the pieces must tile all columns of P"
    tm = _div_tile(T, 640, 16)

    def dx_body(*refs):
        w_ref, o_ref = refs[n], refs[n + 1]
        acc = None
        for j in range(n):
            part = _dg(refs[j][...], w_ref[:, offs[j]:offs[j] + wid[j]], _DIMS["nt"])
            acc = part if acc is None else acc + part
        o_ref[...] = acc.astype(o_ref.dtype)

    dh1 = _pc(dx_body, name=name + "_dx", grid=(T // tm,),
              in_specs=[pl.BlockSpec((tm, wj), lambda i: (i, 0)) for wj in wid] + [pl.BlockSpec((D, IN_PAD), lambda i: (0, 0))],
              out_specs=pl.BlockSpec((tm, D), lambda i: (i, 0)), out_shape=_sds((T, D), BF16),
              compiler_params=_cp(("parallel",), 40 << 20))(*arrs, w)

    tmd, nk = 512, 4
    tk = T // nk

    def dw_body(h_ref, *refs):
        o_ref, acc = refs[n], refs[n + 1]
        k = pl.program_id(1)

        @pl.when(k == 0)
        def _():
            acc[...] = jnp.zeros_like(acc)

        for j in range(n):
            acc[:, offs[j]:offs[j] + wid[j]] += _dg(h_ref[...], refs[j][...], _DIMS["tn"])

        @pl.when(k == nk - 1)
        def _():
            o_ref[...] = acc[...].astype(o_ref.dtype)

    dw = _pc(dw_body, name=name + "_dw", grid=(D // tmd, nk),
             in_specs=[pl.BlockSpec((tk, tmd), lambda i, k: (k, i))] + [pl.BlockSpec((tk, wj), lambda i, k: (k, 0)) for wj in wid],
             out_specs=pl.BlockSpec((tmd, IN_PAD), lambda i, k: (i, 0)), out_shape=_sds((D, IN_PAD), BF16),
             scratch_shapes=[pltpu.VMEM((tmd, IN_PAD), F32)], compiler_params=_cp(("parallel", "arbitrary"), 48 << 20))(h1, *arrs)
    return dh1, dw


def _norm_mod(xo, shift, scale, g):
    r = lax.rsqrt(jnp.mean(xo * xo, axis=-1, keepdims=True) + EPS)
    return (xo * r) * g * (1.0 + scale) + shift


def res_norm_mod(x, y, gsv, g, nL, name):
    T = x.shape[0]
    has_y = y is not None

    def body(*refs):
        if has_y:
            x_ref, y_ref, gsv_ref, g_ref, xo_ref, h_ref = refs
            xo = x_ref[...] + gsv_ref[0, 0:1, :] * y_ref[...]
            xo_ref[...] = xo
        else:
            x_ref, gsv_ref, g_ref, h_ref = refs
            xo = x_ref[...]
        h_ref[...] = _norm_mod(xo, gsv_ref[0, 1:2, :], gsv_ref[0, 2:3, :], g_ref[...]).astype(h_ref.dtype)

    row = pl.BlockSpec((TR, D), lambda i: (i, 0))
    in_specs = [row] + ([row] if has_y else []) + [pl.BlockSpec((1, 8, D), lambda i: (i // nL, 0, 0)),
                                                     pl.BlockSpec((1, D), lambda i: (0, 0))]
    out_specs = ([row] if has_y else []) + [row]
    out_shape = ([_sds((T, D), F32)] if has_y else []) + [_sds((T, D), BF16)]
    args = (x, y, gsv, g) if has_y else (x, gsv, g)
    outs = _pc(body, name=name, grid=(T // TR,), in_specs=in_specs, out_specs=out_specs, out_shape=out_shape,
               compiler_params=_cp(("arbitrary",), 24 << 20))(*args)
    return (outs[0], outs[1]) if has_y else (None, outs[0])


def res_norm_mod_bwd(xo, y, gsv, g, dh, dres, nL, name):
    T = xo.shape[0]
    has_y = y is not None

    def body(*refs):
        if has_y:
            xo_ref, y_ref, gsv_ref, g_ref, dh_ref, dres_ref, dx_ref, dy_ref, dgsv_ref, dg_ref = refs
        else:
            xo_ref, gsv_ref, g_ref, dh_ref, dres_ref, dx_ref, dgsv_ref, dg_ref = refs
        i = pl.program_id(0)

        @pl.when((i == 0) | (i == nL))
        def _():
            dgsv_ref[...] = jnp.zeros_like(dgsv_ref)

        @pl.when(i == 0)
        def _():
            dg_ref[...] = jnp.zeros_like(dg_ref)

        _, vjp = jax.vjp(_norm_mod, xo_ref[...], gsv_ref[0, 1:2, :], gsv_ref[0, 2:3, :], g_ref[...])
        dxn, dshift, dscale, dg = vjp(dh_ref[...].astype(F32))
        dxo = dres_ref[...] + dxn
        dx_ref[...] = dxo
        if has_y:
            dy_ref[...] = (gsv_ref[0, 0:1, :] * dxo).astype(dy_ref.dtype)
            dgsv_ref[0, 0:1, :] += jnp.sum(y_ref[...] * dxo, axis=0, keepdims=True)
        dgsv_ref[0, 1:2, :] += dshift
        dgsv_ref[0, 2:3, :] += dscale
        dg_ref[0:1, :] += dg

    row = pl.BlockSpec((TR, D), lambda i: (i, 0))
    gspec = pl.BlockSpec((1, 8, D), lambda i: (i // nL, 0, 0))
    in_specs = [row] + ([row] if has_y else []) + [gspec, pl.BlockSpec((1, D), lambda i: (0, 0)), row, row]
    out_specs = [row] + ([row] if has_y else []) + [gspec, pl.BlockSpec((8, D), lambda i: (0, 0))]
    out_shape = [_sds((T, D), F32)] + ([_sds((T, D), BF16)] if has_y else []) + [_sds((2, 8, D), F32), _sds((8, D), F32)]
    args = (xo, y, gsv, g, dh, dres) if has_y else (xo, gsv, g, dh, dres)
    outs = _pc(body, name=name, grid=(T // TR,), in_specs=in_specs, out_specs=out_specs, out_shape=out_shape,
               compiler_params=_cp(("arbitrary",), 32 << 20))(*args)
    if has_y:
        return outs
    return outs[0], None, outs[1], outs[2]


def final_loss(x, y, gsv, g, target, nL, name):
    T = x.shape[0]

    def lossf(xo, gv, t):
        yn = (xo * lax.rsqrt(jnp.mean(xo * xo, axis=-1, keepdims=True) + EPS)) * gv
        e = yn - t
        return 0.5 * jnp.sum(jnp.sum(e * e, axis=-1, keepdims=True) * (1.0 / D), axis=0, keepdims=True)

    def body(x_ref, y_ref, gsv_ref, g_ref, t_ref, loss_ref, dx_ref, dy_ref, dgsv_ref, dg_ref):
        i = pl.program_id(0)

        @pl.when(i == 0)
        def _():
            loss_ref[...] = jnp.zeros_like(loss_ref)
            dg_ref[...] = jnp.zeros_like(dg_ref)

        @pl.when((i == 0) | (i == nL))
        def _():
            dgsv_ref[...] = jnp.zeros_like(dgsv_ref)

        @pl.when(i < nL)
        def _():
            gate = gsv_ref[0, 0:1, :]
            yv = y_ref[...]
            xo = x_ref[...] + gate * yv
            lv, vjp = jax.vjp(lossf, xo, g_ref[...], t_ref[...])
            dxo, dg, _ = vjp(jnp.ones((1, 1), F32))
            loss_ref[...] += jnp.broadcast_to(lv, loss_ref.shape)
            dx_ref[...] = dxo
            dy_ref[...] = (gate * dxo).astype(dy_ref.dtype)
            dgsv_ref[0, 0:1, :] += jnp.sum(yv * dxo, axis=0, keepdims=True)
            dg_ref[0:1, :] += dg

        @pl.when(i >= nL)
        def _():
            dx_ref[...] = jnp.zeros_like(dx_ref)
            dy_ref[...] = jnp.zeros_like(dy_ref)

    row = pl.BlockSpec((TR, D), lambda i: (i, 0))
    gspec = pl.BlockSpec((1, 8, D), lambda i: (i // nL, 0, 0))
    return _pc(
        body, name=name, grid=(T // TR,),
        in_specs=[row, row, gspec, pl.BlockSpec((1, D), lambda i: (0, 0)),
                  pl.BlockSpec((TR, D), lambda i: (jnp.minimum(i, nL - 1), 0))],
        out_specs=[pl.BlockSpec((8, 128), lambda i: (0, 0)), row, row, gspec, pl.BlockSpec((8, D), lambda i: (0, 0))],
        out_shape=[_sds((8, 128), F32), _sds((T, D), F32), _sds((T, D), BF16), _sds((2, 8, D), F32), _sds((8, D), F32)],
        compiler_params=_cp(("arbitrary",), 32 << 20),
    )(x, y, gsv, g, target)


FI_BLK = 2 * D_FF // 4


def _fi_chip(j):
    return (j % 2) * 2 + j // 2


def matmul_fi(a, b, mode, out_dtype, name):
    T = a.shape[0]
    if mode == "tn":
        tmd = 512

        def body(a_ref, b_ref, o_ref):
            o_ref[0] = _dg(a_ref[...], b_ref[...], _DIMS["tn"]).astype(o_ref.dtype)

        return _pc(body, name=name, grid=(D // tmd, 4),
                   in_specs=[pl.BlockSpec((T, tmd), lambda i, j: (0, i)), pl.BlockSpec((T, FI_BLK), lambda i, j: (0, j))],
                   out_specs=pl.BlockSpec((1, tmd, FI_BLK), lambda i, j: (_fi_chip(j), i, 0)),
                   out_shape=_sds((4, D, FI_BLK), out_dtype), compiler_params=_cp(("parallel", "arbitrary"), 48 << 20))(a, b)
    assert mode == "nt"
    tm = _div_tile(T, 640, 16)

    def body(a_ref, b_ref, o_ref):
        acc = None
        for k in range(4):
            part = _dg(a_ref[:, k * FI_BLK:(k + 1) * FI_BLK], b_ref[_fi_chip(k)], _DIMS["nt"])
            acc = part if acc is None else acc + part
        o_ref[...] = acc.astype(o_ref.dtype)

    return _pc(body, name=name, grid=(T // tm,),
               in_specs=[pl.BlockSpec((tm, 4 * FI_BLK), lambda i: (i, 0)), pl.BlockSpec((4, D, FI_BLK), lambda i: (0, 0, 0))],
               out_specs=pl.BlockSpec((tm, D), lambda i: (i, 0)), out_shape=_sds((T, D), out_dtype),
               compiler_params=_cp(("parallel",), VMEM_CAP))(a, b)


def _swiglu(gate, up):
    return _silu(gate) * up


def ffn_in_swiglu(a, w, name):
    T = a.shape[0]
    tm = _div_tile(T, 640, 32)
    half = tm // 2

    def body(a_ref, wg_ref, wu_ref, gu_ref, act_ref):
        for rows in (slice(0, half), slice(half, tm)):
            g = _dg(a_ref[rows, :], wg_ref[0], _DIMS["nn"]).astype(BF16)
            u = _dg(a_ref[rows, :], wu_ref[0], _DIMS["nn"]).astype(BF16)
            gu_ref[rows, :FI_BLK] = g
            gu_ref[rows, FI_BLK:] = u
            act_ref[rows, :] = _swiglu(g.astype(F32), u.astype(F32)).astype(BF16)

    wspec = lambda r: pl.BlockSpec((1, D, FI_BLK), lambda j, i: (_fi_chip(2 * j + r), 0, 0))
    return _pc(body, name=name, grid=(2, T // tm),
               in_specs=[pl.BlockSpec((tm, D), lambda j, i: (i, 0)), wspec(0), wspec(1)],
               out_specs=[pl.BlockSpec((tm, 2 * FI_BLK), lambda j, i: (i, j)), pl.BlockSpec((tm, FI_BLK), lambda j, i: (i, j))],
               out_shape=[_sds((T, 4 * FI_BLK), BF16), _sds((T, D_FF), BF16)],
               compiler_params=_cp(("parallel", "arbitrary"), 48 << 20))(a, w, w)


def ffn_out_dx_swiglu(d, w, gu, name):
    T = d.shape[0]
    tm = _div_tile(T, 320, 16)

    def body(d_ref, w_ref, gu_ref, o_ref):
        for j in range(2):
            dact = _dg(d_ref[...], w_ref[j * FI_BLK:(j + 1) * FI_BLK, :], _DIMS["nt"]).astype(BF16).astype(F32)
            gs, us = slice(2 * j * FI_BLK, (2 * j + 1) * FI_BLK), slice((2 * j + 1) * FI_BLK, (2 * j + 2) * FI_BLK)
            g, u = gu_ref[:, gs].astype(F32), gu_ref[:, us].astype(F32)
            sg = 1.0 / (1.0 + jnp.exp(-g))
            sl = g * sg
            o_ref[:, gs] = (dact * u * (sg + sl * (1.0 - sg))).astype(o_ref.dtype)
            o_ref[:, us] = (dact * sl).astype(o_ref.dtype)

    return _pc(body, name=name, grid=(T // tm,),
               in_specs=[pl.BlockSpec((tm, D), lambda i: (i, 0)), pl.BlockSpec((D_FF, D), lambda i: (0, 0)), pl.BlockSpec((tm, 4 * FI_BLK), lambda i: (i, 0))],
               out_specs=pl.BlockSpec((tm, 4 * FI_BLK), lambda i: (i, 0)), out_shape=_sds((T, 4 * FI_BLK), BF16),
               compiler_params=_cp(("parallel",), 48 << 20))(d, w, gu)


def rope_tables(L, Lc):
    t = np.arange(L)
    rows, cols = t // GRID_W, t % GRID_W
    inv = ROPE_BASE ** (-np.arange(16, dtype=np.float32) / 16)
    lane = np.arange(64)
    pos = np.where((lane // 32)[None, :] == 0, rows[:, None], cols[:, None]).astype(np.float32)
    ang = jnp.asarray(pos) * jnp.asarray(inv[lane % 16])[None, :]
    cos = jnp.concatenate([jnp.cos(ang), jnp.ones((Lc, 64), F32)], axis=0)
    sin = jnp.concatenate([jnp.sin(ang), jnp.zeros((Lc, 64), F32)], axis=0)
    return jnp.tile(cos, (1, 2)), jnp.tile(sin, (1, 2))


def rope_apply(q_src, q_col, k_src, k_col, cos, sin, transpose, name, kv_src=None):
    T = cos.shape[0]
    with_kv = kv_src is not None
    tr = _div_tile(T, 640, 16)

    def rot(x, c, s):
        first = (lax.broadcasted_iota(jnp.int32, x.shape, 1) % 32) < 16
        if transpose:
            y = x * s
            return x * c + jnp.where(first, pltpu.roll(y, 112, 1), -pltpu.roll(y, 16, 1))
        return x * c + jnp.where(first, -pltpu.roll(x, 112, 1), pltpu.roll(x, 16, 1)) * s

    def body(q_ref, k_ref, c_ref, s_ref, *rest):
        qo_ref, ko_ref = rest[-4:-2] if with_kv else rest
        c, s = c_ref[...], s_ref[...]
        for j in range(2):
            qo_ref[:, j * 128:(j + 1) * 128] = rot(q_ref[:, j * 128:(j + 1) * 128].astype(F32), c, s).astype(qo_ref.dtype)
        ko_ref[...] = rot(k_ref[...].astype(F32), c, s).astype(ko_ref.dtype)
        if with_kv:
            rest[-2][...] = rest[0][...].astype(BF16)
            rest[-1][...] = rest[1][...].astype(BF16)

    tab = pl.BlockSpec((tr, 128), lambda i: (i, 0))
    wide = pl.BlockSpec((tr, 256), lambda i: (i, 0))
    kv_in = [pl.BlockSpec((tr, 256), lambda i: (i, C_KB // 256)), pl.BlockSpec((tr, 256), lambda i: (i, C_VB // 256))] if with_kv else []
    return _pc(body, name=name, grid=(T // tr,),
               in_specs=[pl.BlockSpec((tr, 256), lambda i: (i, q_col)), pl.BlockSpec((tr, 128), lambda i: (i, k_col)), tab, tab] + kv_in,
               out_specs=[wide, tab] + ([wide, wide] if with_kv else []),
               out_shape=[_sds((T, 256), BF16), _sds((T, 128), BF16)] + ([_sds((T, 256), BF16)] * 2 if with_kv else []),
               compiler_params=_cp(("parallel",), 32 << 20))(q_src, k_src, cos, sin, *([kv_src, kv_src] if with_kv else []))


_SCALE = HD ** -0.5


def _attn_tile(qh, ks, vs, extra):
    ss = []
    for k, add in ks:
        s = _dg(qh, k, _DIMS["nt"]) * _SCALE
        ss.append(s if add is None else s + add)
    m = ss[0].max(axis=-1, keepdims=True)
    for s in ss[1:]:
        m = jnp.maximum(m, s.max(axis=-1, keepdims=True))
    if extra is not None:
        m = jnp.maximum(m, extra)
    ps = [jnp.exp(s - m) for s in ss]
    den = ps[0].sum(axis=-1, keepdims=True)
    for p in ps[1:]:
        den = den + p.sum(axis=-1, keepdims=True)
    if extra is not None:
        den = den + jnp.exp(extra - m)
    num = _dg(ps[0], vs[0], _DIMS["nn"])
    for p, v in zip(ps[1:], vs[1:]):
        num = num + _dg(p, v, _DIMS["nn"])
    linv = 1.0 / den
    return num * linv, m, linv


def _attn_bwd_tile(qh, ks, vs, extra, m, linv, oh, doh):
    delta = jnp.sum(doh * oh, axis=-1, keepdims=True)
    dq = None
    dks, dvs, dss = [], [], []
    for (k, add), v in zip(ks, vs):
        s = _dg(qh, k, _DIMS["nt"]) * _SCALE
        if add is not None:
            s = s + add
        p = jnp.exp(s - m) * linv
        dvs.append(_dg(p, doh, _DIMS["tn"]))
        ds = p * (_dg(doh, v, _DIMS["nt"]) - delta)
        dss.append(ds)
        dsq = ds * _SCALE
        part = _dg(dsq, k, _DIMS["nn"])
        dq = part if dq is None else dq + part
        dks.append(_dg(dsq, qh, _DIMS["tn"]))
    dextra = None
    if extra is not None:
        dextra = -(jnp.exp(extra - m) * linv * delta)
    return dq, dks, dvs, dss, dextra


def _wa_mask(n, L):
    qpos = n * WA_BLK + lax.broadcasted_iota(jnp.int32, (WA_BLK, 3 * WA_BLK), 0)
    kpos = (n - 1) * WA_BLK + lax.broadcasted_iota(jnp.int32, (WA_BLK, 3 * WA_BLK), 1)
    ok = (jnp.abs(qpos - kpos) <= WA_BLK) & (kpos >= 0) & (kpos < L)
    return jnp.where(ok, 0.0, NEG).astype(F32)


WA_BPS = 2
_WA_PAIRS = (((0, 0), (1, 3), False), ((1, 2), (0, 1), True))


def _swap_halves_lanes(a):
    return jnp.concatenate([a[:, HD:], a[:, :HD]], axis=1)


def _wa_specs(L, Lc):
    nb = L // WA_BLK
    cb = L // Lc

    def blk(j):
        return pl.BlockSpec((WA_BLK, 128), lambda s: (jnp.clip(s * WA_BPS - 1 + j, 0, nb - 1), 0))

    return nb, [blk(j) for j in range(WA_BPS + 2)] + [pl.BlockSpec((Lc, 128), lambda s: (cb, 0))]


def _wa_pair_q(q_ref, qs, lo, hi):
    a = q_ref[qs, lo[0] * 128:(lo[0] + 1) * 128]
    b = q_ref[qs, hi[0] * 128:(hi[0] + 1) * 128]
    lane = lax.broadcasted_iota(jnp.int32, a.shape, 1)
    zero = jnp.zeros_like(a)
    return jnp.concatenate([jnp.where(lane < HD, a, zero), jnp.where(lane >= HD, b, zero)], axis=0)


def _wa_pair_vec(ref, qs, lo, hi, base=0):
    return jnp.concatenate([ref[qs, base + lo[1]:base + lo[1] + 1], ref[qs, base + hi[1]:base + hi[1] + 1]], axis=0)


def _wa_pair_sink(s_ref, n, lo, hi):
    return jnp.concatenate([jnp.broadcast_to(s_ref[lo[1]:lo[1] + 1, 0:1], (n, 1)), jnp.broadcast_to(s_ref[hi[1]:hi[1] + 1, 0:1], (n, 1))], axis=0)


def win_attn_fwd(qr, kr, krs, v, vs, sink, L, Lc, name):
    T = L + Lc
    nb, specs = _wa_specs(L, Lc)
    nk = WA_BPS + 2
    QB = WA_BPS * WA_BLK
    nlat = nb // WA_BPS

    def body(q_ref, *refs):
        groups = [refs[g * (nk + 1):(g + 1) * (nk + 1)] for g in range(4)]
        s_ref, o_ref, st_ref = refs[-3], refs[-2], refs[-1]
        s = pl.program_id(0)

        def run(qs, n, ks_of, vs_of):
            outs = []
            for lo, hi, swapped in _WA_PAIRS:
                kb, vb = groups[1 if swapped else 0], groups[3 if swapped else 2]
                o2, m2, l2 = _attn_tile(_wa_pair_q(q_ref, qs, lo, hi), ks_of(kb), vs_of(vb), _wa_pair_sink(s_ref, n, lo, hi))
                outs.append(o2)
                for r, (_, h) in enumerate((lo, hi)):
                    st_ref[qs, h:h + 1] = m2[r * n:(r + 1) * n]
                    st_ref[qs, WA_HEADS + h:WA_HEADS + h + 1] = l2[r * n:(r + 1) * n]
            lane = lax.broadcasted_iota(jnp.int32, (n, 128), 1)
            o_ref[qs, 0:128] = jnp.where(lane < HD, outs[0][:n], outs[1][n:]).astype(o_ref.dtype)
            o_ref[qs, 128:256] = jnp.where(lane < HD, outs[1][:n], outs[0][n:]).astype(o_ref.dtype)

        @pl.when(s < nlat)
        def _():
            for b in range(WA_BPS):
                m1 = _wa_mask(s * WA_BPS + b, L)
                mask = jnp.concatenate([m1, m1], axis=0)
                cat = lambda g: jnp.concatenate([g[b + j][...] for j in range(3)], axis=0)
                run(slice(b * WA_BLK, (b + 1) * WA_BLK), WA_BLK,
                    lambda kb: [(cat(kb), mask), (kb[nk][...], None)], lambda vb: [cat(vb), vb[nk][...]])

        @pl.when(s >= nlat)
        def _():
            run(slice(None), QB, lambda kb: [(kb[nk][...], None)], lambda vb: [vb[nk][...]])

    qspec = pl.BlockSpec((QB, 256), lambda s: (s, 0))
    return _pc(body, name=name, grid=(T // QB,),
               in_specs=[qspec] + specs * 4 + [pl.BlockSpec((8, 128), lambda s: (0, 0))],
               out_specs=[qspec, pl.BlockSpec((QB, 8), lambda s: (s, 0))], out_shape=[_sds((T, 256), BF16), _sds((T, 8), F32)],
               compiler_params=_cp(("arbitrary",), 40 << 20))(qr, *([kr] * (nk + 1)), *([krs] * (nk + 1)), *([v] * (nk + 1)), *([vs] * (nk + 1)), sink)


def win_attn_bwd(qr, kr, krs, v, vs, sink, do_src, o, stats, L, Lc, name):
    T = L + Lc
    nb, specs = _wa_specs(L, Lc)
    nk = WA_BPS + 2
    QB = WA_BPS * WA_BLK
    nlat = nb // WA_BPS
    cx = WA_BLK + L

    def body(q_ref, *refs):
        groups = [refs[g * (nk + 1):(g + 1) * (nk + 1)] for g in range(4)]
        s_ref, do_ref, o_ref, st_ref, dq_ref, dk_ref, dks_ref, dv_ref, dvs_ref, ds_ref = refs[4 * (nk + 1):]
        s = pl.program_id(0)

        @pl.when(s == 0)
        def _():
            for r in (dk_ref, dks_ref, dv_ref, dvs_ref, ds_ref):
                r[...] = jnp.zeros_like(r)

        def run(qs, n, ks_of, vs_of, rows):
            lane = lax.broadcasted_iota(jnp.int32, (n, 128), 1)
            dqs = []
            for lo, hi, swapped in _WA_PAIRS:
                kb, vb = groups[1 if swapped else 0], groups[3 if swapped else 2]
                dka, dva = (dks_ref, dvs_ref) if swapped else (dk_ref, dv_ref)
                pair = lambda ref: jnp.concatenate([jnp.where(lane < HD, ref[qs, lo[0] * 128:(lo[0] + 1) * 128].astype(F32), 0.0),
                                                    jnp.where(lane >= HD, ref[qs, hi[0] * 128:(hi[0] + 1) * 128].astype(F32), 0.0)], axis=0)
                dq2, dks, dvs, _, dex = _attn_bwd_tile(_wa_pair_q(q_ref, qs, lo, hi), ks_of(kb), vs_of(vb), _wa_pair_sink(s_ref, n, lo, hi),
                                                       _wa_pair_vec(st_ref, qs, lo, hi), _wa_pair_vec(st_ref, qs, lo, hi, WA_HEADS), pair(o_ref), pair(do_ref))
                dqs.append(dq2)
                for r, (_, h) in enumerate((lo, hi)):
                    ds_ref[h:h + 1, :] += jnp.broadcast_to(jnp.sum(dex[r * n:(r + 1) * n], axis=0, keepdims=True), (1, 128))
                if rows is not None:
                    dka[rows, :] += dks[0]
                    dva[rows, :] += dvs[0]
                dka[cx:cx + Lc, :] += dks[-1]
                dva[cx:cx + Lc, :] += dvs[-1]
            dq_ref[qs, 0:128] = jnp.where(lane < HD, dqs[0][:n], dqs[1][n:])
            dq_ref[qs, 128:256] = jnp.where(lane < HD, dqs[1][:n], dqs[0][n:])

        @pl.when(s < nlat)
        def _():
            for b in range(WA_BPS):
                nblk = s * WA_BPS + b
                m1 = _wa_mask(nblk, L)
                mask = jnp.concatenate([m1, m1], axis=0)
                cat = lambda g: jnp.concatenate([g[b + j][...] for j in range(3)], axis=0)
                run(slice(b * WA_BLK, (b + 1) * WA_BLK), WA_BLK, lambda kb: [(cat(kb), mask), (kb[nk][...], None)],
                    lambda vb: [cat(vb), vb[nk][...]], pl.ds(pl.multiple_of(nblk * WA_BLK, WA_BLK), 3 * WA_BLK))

        @pl.when(s >= nlat)
        def _():
            run(slice(None), QB, lambda kb: [(kb[nk][...], None)], lambda vb: [vb[nk][...]], None)

    qspec = pl.BlockSpec((QB, 256), lambda s: (s, 0))
    acc_spec = pl.BlockSpec((T + 2 * WA_BLK, 128), lambda s: (0, 0))
    acc_shape = _sds((T + 2 * WA_BLK, 128), F32)
    return _pc(body, name=name, grid=(T // QB,),
               in_specs=[qspec] + specs * 4 + [pl.BlockSpec((8, 128), lambda s: (0, 0)), qspec, qspec, pl.BlockSpec((QB, 8), lambda s: (s, 0))],
               out_specs=[qspec, acc_spec, acc_spec, acc_spec, acc_spec, pl.BlockSpec((8, 128), lambda s: (0, 0))],
               out_shape=[_sds((T, 256), F32), acc_shape, acc_shape, acc_shape, acc_shape, _sds((8, 128), F32)],
               compiler_params=_cp(("arbitrary",), 48 << 20))(qr, *([kr] * (nk + 1)), *([krs] * (nk + 1)), *([v] * (nk + 1)), *([vs] * (nk + 1)),
                                                              sink, do_src, o, stats)


def na_index_tables():
    qc = np.arange(GRID_W)[:, None]
    kc = np.arange(GRID_W)[None, :]
    cstart = np.clip(qc - NA_KW // 2, 0, GRID_W - NA_KW)
    ok = (kc >= cstart) & (kc < cstart + NA_KW)
    dx = np.clip(kc - qc, -(NA_KW - 1), NA_KW - 1) + (NA_KW - 1)
    off = np.arange(NA_KH)[:, None]
    kr = np.arange(NA_KH)[None, :]
    dy = kr - off + (NA_KH - 1)
    return ok, dx, dy


def _na_selectors():
    ok, dx, dy = na_index_tables()
    e1 = np.zeros((GRID_W * GRID_W, 128), np.float32)
    qi, ki = np.nonzero(ok)
    e1[qi * GRID_W + ki, dx[qi, ki]] = 1.0
    e2 = np.zeros((16, NA_KH * NA_KH), np.float32)
    oi, ri = np.meshgrid(np.arange(NA_KH), np.arange(NA_KH), indexing="ij")
    e2[dy[oi, ri].ravel(), (oi * NA_KH + ri).ravel()] = 1.0
    return ok, jnp.asarray(e1), jnp.asarray(np.kron(np.eye(NA_HEADS, dtype=np.float32), e2))


def na_bias_table(rpb, tag):
    ok, e1, e2 = _na_selectors()
    r2 = jnp.pad(rpb.astype(F32), ((0, 0), (0, 1), (0, 128 - (2 * NA_KW - 1)))).reshape(NA_HEADS * 16, 128)
    r1 = matmul(e2, r2, "tn", F32, f"na_bias_sel1_{tag}", hi=True)
    x = matmul(r1, e1, "nt", F32, f"na_bias_sel2_{tag}", hi=True)
    b = x.reshape(NA_HEADS, NA_KH, NA_KH, GRID_W, GRID_W).transpose(0, 1, 3, 2, 4)
    b = b + jnp.asarray(np.where(ok, 0.0, NEG).astype(np.float32))[None, None, :, None, :]
    return b.reshape(NA_HEADS, NA_KH, GRID_W, NA_KH * GRID_W)


def _na_rows(r, GR):
    r0 = jnp.clip(r - NA_KH // 2, 0, GR - NA_KH)
    return r0, jnp.clip(r - r0, 0, NA_KH - 1)


NA_RPS = 4


def _pair_rows(x):
    lane = lax.broadcasted_iota(jnp.int32, x.shape, 1)
    zero = jnp.zeros_like(x)
    return jnp.concatenate([jnp.where(lane < HD, x, zero), jnp.where(lane >= HD, x, zero)], axis=0)


def _unpair_rows(x2):
    n = x2.shape[0] // 2
    lane = lax.broadcasted_iota(jnp.int32, (n, 128), 1)
    return jnp.where(lane < HD, x2[:n], x2[n:])


def na_fwd(P, kb, vb, bias, L, Lc, name):
    T = L + Lc
    GR = L // GRID_W
    W = NA_KH * GRID_W
    QB = GRID_W * NA_RPS
    nlat = GR // NA_RPS

    def body(q_ref, k_ref, v_ref, b_ref, o_ref, st_ref):
        s = pl.program_id(0)

        def put(qs, p, res):
            o2, m2, l2 = res
            n = o2.shape[0] // 2
            o_ref[qs, p * 128:(p + 1) * 128] = _unpair_rows(o2).astype(o_ref.dtype)
            for r in range(2):
                st_ref[qs, 2 * p + r:2 * p + r + 1] = m2[r * n:(r + 1) * n]
                st_ref[qs, NA_HEADS + 2 * p + r:NA_HEADS + 2 * p + r + 1] = l2[r * n:(r + 1) * n]

        @pl.when(s < nlat)
        def _():
            for rr in range(NA_RPS):
                r0, off = _na_rows(s * NA_RPS + rr, GR)
                rows = pl.ds(pl.multiple_of(r0 * GRID_W, GRID_W), W)
                qs = slice(rr * GRID_W, (rr + 1) * GRID_W)
                for p in range(NA_HEADS // 2):
                    ps = slice(p * 128, (p + 1) * 128)
                    b2 = jnp.concatenate([b_ref[2 * p, off], b_ref[2 * p + 1, off]], axis=0)
                    put(qs, p, _attn_tile(_pair_rows(q_ref[qs, ps]), [(k_ref[rows, ps], b2), (k_ref[L:T, ps], None)],
                                          [v_ref[rows, ps], v_ref[L:T, ps]], None))

        @pl.when(s >= nlat)
        def _():
            for p in range(NA_HEADS // 2):
                ps = slice(p * 128, (p + 1) * 128)
                put(slice(None), p, _attn_tile(_pair_rows(q_ref[:, ps]), [(k_ref[L:T, ps], None)], [v_ref[L:T, ps]], None))

    one = pl.Buffered(1)
    return _pc(body, name=name, grid=(T // QB,),
               in_specs=[pl.BlockSpec((QB, 256), lambda r: (r, C_QB // 256)),
                         pl.BlockSpec((T, 256), lambda r: (0, 0), pipeline_mode=one),
                         pl.BlockSpec((T, 256), lambda r: (0, 0), pipeline_mode=one),
                         pl.BlockSpec((NA_HEADS, NA_KH, GRID_W, W), lambda r: (0, 0, 0, 0), pipeline_mode=one)],
               out_specs=[pl.BlockSpec((QB, 256), lambda r: (r, 0)), pl.BlockSpec((QB, 8), lambda r: (r, 0))],
               out_shape=[_sds((T, 256), BF16), _sds((T, 8), F32)],
               compiler_params=_cp(("arbitrary",), 32 << 20))(P, kb, vb, bias)


def na_bwd(P, kb, vb, bias, do_src, o, stats, L, Lc, name):
    T = L + Lc
    GR = L // GRID_W
    W = NA_KH * GRID_W
    QB = GRID_W * NA_RPS
    nlat = GR // NA_RPS

    def body(q_ref, k_ref, v_ref, b_ref, do_ref, o_ref, st_ref, dq_ref, dk_ref, dv_ref, db_ref):
        s = pl.program_id(0)

        @pl.when(s == 0)
        def _():
            dk_ref[...] = jnp.zeros_like(dk_ref)
            dv_ref[...] = jnp.zeros_like(dv_ref)
            db_ref[...] = jnp.zeros_like(db_ref)

        def tile(qs, p, ks, vs):
            ps = slice(p * 128, (p + 1) * 128)
            m2 = jnp.concatenate([st_ref[qs, 2 * p:2 * p + 1], st_ref[qs, 2 * p + 1:2 * p + 2]], axis=0)
            l2 = jnp.concatenate([st_ref[qs, NA_HEADS + 2 * p:NA_HEADS + 2 * p + 1], st_ref[qs, NA_HEADS + 2 * p + 1:NA_HEADS + 2 * p + 2]], axis=0)
            dq2, dks, dvs, dss, _ = _attn_bwd_tile(_pair_rows(q_ref[qs, ps]), ks, vs, None, m2, l2,
                                                   _pair_rows(o_ref[qs, ps].astype(F32)), _pair_rows(do_ref[qs, ps].astype(F32)))
            dq_ref[qs, ps] = _unpair_rows(dq2).astype(dq_ref.dtype)
            return dks, dvs, dss

        @pl.when(s < nlat)
        def _():
            for rr in range(NA_RPS):
                r0, off = _na_rows(s * NA_RPS + rr, GR)
                rows = pl.ds(pl.multiple_of(r0 * GRID_W, GRID_W), W)
                qs = slice(rr * GRID_W, (rr + 1) * GRID_W)
                for p in range(NA_HEADS // 2):
                    ps = slice(p * 128, (p + 1) * 128)
                    b2 = jnp.concatenate([b_ref[2 * p, off], b_ref[2 * p + 1, off]], axis=0)
                    dks, dvs, dss = tile(qs, p, [(k_ref[rows, ps], b2), (k_ref[L:T, ps], None)], [v_ref[rows, ps], v_ref[L:T, ps]])
                    dk_ref[rows, ps] += dks[0]
                    dv_ref[rows, ps] += dvs[0]
                    dk_ref[L:T, ps] += dks[1]
                    dv_ref[L:T, ps] += dvs[1]
                    db_ref[2 * p, off] += dss[0][:GRID_W]
                    db_ref[2 * p + 1, off] += dss[0][GRID_W:]

        @pl.when(s >= nlat)
        def _():
            for p in range(NA_HEADS // 2):
                ps = slice(p * 128, (p + 1) * 128)
                dks, dvs, _ = tile(slice(None), p, [(k_ref[L:T, ps], None)], [v_ref[L:T, ps]])
                dk_ref[L:T, ps] += dks[0]
                dv_ref[L:T, ps] += dvs[0]

    one = pl.Buffered(1)
    full = lambda shape: pl.BlockSpec(shape, lambda r: (0,) * len(shape), pipeline_mode=one)
    qspec = pl.BlockSpec((QB, 256), lambda r: (r, 0))
    return _pc(body, name=name, grid=(T // QB,),
               in_specs=[pl.BlockSpec((QB, 256), lambda r: (r, C_QB // 256)), full((T, 256)), full((T, 256)),
                         full((NA_HEADS, NA_KH, GRID_W, W)), pl.BlockSpec((QB, 256), lambda r: (r, 1)), qspec, pl.BlockSpec((QB, 8), lambda r: (r, 0))],
               out_specs=[qspec, full((T, 256)), full((T, 256)), full((NA_HEADS, NA_KH, GRID_W, W))],
               out_shape=[_sds((T, 256), BF16), _sds((T, 256), F32), _sds((T, 256), F32), _sds((NA_HEADS, NA_KH, GRID_W, W), F32)],
               compiler_params=_cp(("arbitrary",), 48 << 20))(P, kb, vb, bias, do_src, o, stats)


def na_rpb_grad(dbias, tag):
    _, e1, e2 = _na_selectors()
    x = dbias.reshape(NA_HEADS, NA_KH, GRID_W, NA_KH, GRID_W).transpose(0, 1, 3, 2, 4).reshape(NA_HEADS * NA_KH * NA_KH, GRID_W * GRID_W)
    r1 = matmul(x, e1, "nn", F32, f"na_rpb_sel1_{tag}", hi=True, tk=1024)
    r2 = matmul(e2, r1, "nn", F32, f"na_rpb_sel2_{tag}", hi=True)
    return r2.reshape(NA_HEADS, 16, 128)[:, :2 * NA_KH - 1, :2 * NA_KW - 1]


_HALO = 8
CONV_CB = 4
CONV_RB = 32


def _halo_specs(T, col0):
    nh = TR // _HALO
    specs = []
    for j in range(CONV_CB):
        specs.append(pl.BlockSpec((_HALO, 256), lambda i, j=j: (jnp.maximum(i * nh - 1, 0), col0 + j)))
        specs.append(pl.BlockSpec((TR, 256), lambda i, j=j: (i, col0 + j)))
        specs.append(pl.BlockSpec((_HALO, 256), lambda i, j=j: (jnp.minimum((i + 1) * nh, T // _HALO - 1), col0 + j)))
    return specs


def _fill_ext(ext, prv, cur, nxt, i, nL, nT):
    has_prev = jnp.where((i != 0) & (i != nL), 1.0, 0.0)
    has_next = jnp.where((i != nL - 1) & (i != nT - 1), 1.0, 0.0)
    ext[0:_HALO, :] = prv[...].astype(F32) * has_prev
    ext[_HALO:_HALO + TR, :] = cur[...].astype(F32)
    ext[_HALO + TR:, :] = nxt[...].astype(F32) * has_next


def conv_silu_fwd(P, w8, b, nL, name):
    T = P.shape[0]
    nT = T // TR

    def body(*refs):
        xin, (w_ref, b_ref, pre_ref, act_ref, ext) = refs[:3 * CONV_CB], refs[3 * CONV_CB:]
        i = pl.program_id(0)
        for j in range(CONV_CB):
            cs = slice(j * 256, (j + 1) * 256)
            _fill_ext(ext, *xin[3 * j:3 * j + 3], i, nL, nT)
            for r in range(0, TR, CONV_RB):
                y = jnp.broadcast_to(b_ref[:, cs], (CONV_RB, 256))
                for k in range(S_CONV):
                    y = y + w_ref[k:k + 1, cs] * ext[pl.ds(_HALO - S_CONV // 2 + k + r, CONV_RB), :]
                pre_ref[r:r + CONV_RB, cs] = y
                act_ref[r:r + CONV_RB, cs] = _silu(y)

    out = pl.BlockSpec((TR, 1024), lambda i: (i, 0))
    return _pc(body, name=name, grid=(nT,),
               in_specs=_halo_specs(T, C_XBC // 256) + [pl.BlockSpec((8, 1024), lambda i: (0, 0)), pl.BlockSpec((1, 1024), lambda i: (0, 0))],
               out_specs=[out, out], out_shape=[_sds((T, 1024), F32), _sds((T, 1024), F32)],
               scratch_shapes=[pltpu.VMEM((TR + 2 * _HALO, 256), F32)],
               compiler_params=_cp(("parallel",), 24 << 20))(*([P] * (3 * CONV_CB)), w8, b)


def dsilu(pre, dxs_list, db_list, dc_list, name):
    T = pre.shape[0]
    n1, n2, n3 = len(dxs_list), len(db_list), len(dc_list)

    def body(*refs):
        pre_ref = refs[0]
        ins = refs[1:1 + n1 + n2 + n3]
        out = refs[-1]

        def part(rs, lo, hi):
            g = rs[0][...].astype(F32)
            for r in rs[1:]:
                g = g + r[...].astype(F32)
            x = pre_ref[:, lo:hi]
            sg = 1.0 / (1.0 + jnp.exp(-x))
            sl = x * sg
            out[:, lo:hi] = g * (sg + sl * (1.0 - sg))

        part(ins[:n1], 0, 512)
        part(ins[n1:n1 + n2], 512, 768)
        part(ins[n1 + n2:], 768, 1024)

    spec = lambda w: pl.BlockSpec((TR, w), lambda i: (i, 0))
    return _pc(body, name=name, grid=(T // TR,),
               in_specs=[spec(1024)] + [spec(512)] * n1 + [spec(256)] * (n2 + n3),
               out_specs=spec(1024), out_shape=_sds((T, 1024), F32),
               compiler_params=_cp(("parallel",), 32 << 20))(pre, *dxs_list, *db_list, *dc_list)


def conv_bwd(dpre, P, w8, nL, name):
    T = P.shape[0]
    nT = T // TR

    def body(*refs):
        din, xin, (w_ref, dx_ref, dw_ref, db_ref, extd) = refs[:3 * CONV_CB], refs[3 * CONV_CB:4 * CONV_CB], refs[4 * CONV_CB:]
        i = pl.program_id(0)

        @pl.when(i == 0)
        def _():
            dw_ref[...] = jnp.zeros_like(dw_ref)
            db_ref[...] = jnp.zeros_like(db_ref)

        fold = lambda a: functools.reduce(lambda p, q: p + q, [a[q:q + 8] for q in range(0, CONV_RB, 8)])
        for j in range(CONV_CB):
            cs = slice(j * 256, (j + 1) * 256)
            _fill_ext(extd, *din[3 * j:3 * j + 3], i, nL, nT)
            dws = [jnp.zeros((8, 256), F32) for _ in range(S_CONV)]
            dbs = jnp.zeros((8, 256), F32)
            for r in range(0, TR, CONV_RB):
                x = xin[j][r:r + CONV_RB, :]
                dx = jnp.zeros((CONV_RB, 256), F32)
                for k in range(S_CONV):
                    sd = extd[pl.ds(_HALO + S_CONV // 2 - k + r, CONV_RB), :]
                    dx = dx + w_ref[k:k + 1, cs] * sd
                    dws[k] = dws[k] + fold(sd * x)
                dx_ref[r:r + CONV_RB, cs] = dx.astype(dx_ref.dtype)
                dbs = dbs + fold(din[3 * j + 1][r:r + CONV_RB, :])
            for k in range(S_CONV):
                dw_ref[k:k + 1, cs] += jnp.sum(dws[k], axis=0, keepdims=True)
            db_ref[0:1, cs] += jnp.sum(dbs, axis=0, keepdims=True)

    acc = pl.BlockSpec((8, 1024), lambda i: (0, 0))
    xspecs = [pl.BlockSpec((TR, 256), lambda i, j=j: (i, C_XBC // 256 + j)) for j in range(CONV_CB)]
    return _pc(body, name=name, grid=(nT,),
               in_specs=_halo_specs(T, 0) + xspecs + [acc],
               out_specs=[pl.BlockSpec((TR, 1024), lambda i: (i, 0)), acc, acc],
               out_shape=[_sds((T, 1024), BF16), _sds((8, 1024), F32), _sds((8, 1024), F32)],
               scratch_shapes=[pltpu.VMEM((TR + 2 * _HALO, 256), F32)],
               compiler_params=_cp(("arbitrary",), 24 << 20))(*([dpre] * (3 * CONV_CB)), *([P] * CONV_CB), w8)


def _onehot_row(h, n):
    return (lax.broadcasted_iota(jnp.int32, (1, n), 1) == h).astype(F32)


def _onehot_col(h, n):
    return (lax.broadcasted_iota(jnp.int32, (n, 1), 0) == h).astype(F32)


S_PAIRS = S_HEADS // 2


def _ssd_chunk(xs, dtr, dtb, alog, bm, cm, hin, reverse):
    Qn = S_Q
    ii = lax.broadcasted_iota(jnp.int32, (Qn, Qn), 0)
    jj = lax.broadcasted_iota(jnp.int32, (Qn, Qn), 1)
    keep = (ii <= jj) if reverse else (ii >= jj)
    tri = keep.astype(F32)
    triT = ((jj <= ii) if reverse else (jj >= ii)).astype(F32)
    eye = (ii == jj).astype(F32)
    low = jj < S_P
    top = ii < S_P
    dt = _softplus(dtr + dtb)
    a = dt * (-jnp.exp(alog))
    cs = hdot(tri, a)
    csT = hdot(a, triT, "tn")
    dtT = hdot(dt, eye, "tn")
    last = _onehot_row(0 if reverse else Qn - 1, Qn)
    ys, houts = [], []
    for p in range(S_PAIRS):
        g = p // (S_PAIRS // S_GROUPS)
        if p % (S_PAIRS // S_GROUPS) == 0:
            G = bdot(cm[g], bm[g], "nt")
        per_head = []
        for h in (2 * p, 2 * p + 1):
            eh_r, eh_c = _onehot_row(h, S_HEADS), _onehot_col(h, S_HEADS)
            cs_c = jnp.sum(cs * eh_r, axis=1, keepdims=True)
            dt_c = jnp.sum(dt * eh_r, axis=1, keepdims=True)
            cs_r = jnp.sum(csT * eh_c, axis=0, keepdims=True)
            dt_r = jnp.sum(dtT * eh_c, axis=0, keepdims=True)
            tot = jnp.sum(cs_r * last, axis=1, keepdims=True)
            w = G * jnp.exp(jnp.where(keep, cs_c - cs_r, NEG)) * dt_r
            per_head.append((bdot(w, xs[p], "nn"), jnp.exp(cs_c), jnp.exp(tot - cs_c) * dt_c, jnp.exp(tot)))
        (y0, e0, f0, d0), (y1, e1, f1, d1) = per_head
        y = jnp.where(low, y0, y1) + bdot(cm[g], hin[p], "nt") * jnp.where(low, e0, e1)
        hout = hin[p] * jnp.where(top, d0, d1) + bdot(xs[p] * jnp.where(low, f0, f1), bm[g], "tn")
        ys.append(y)
        houts.append(hout)
    return ys, houts


def _ssd_orders(L, Lc):
    nl, ncx = L // S_Q, Lc // S_Q
    fwd = lambda s: jnp.where(s < ncx, nl + s, s - ncx)
    bwd = lambda s: nl + ncx - 1 - s
    return nl + ncx, fwd, bwd


def _ssd_in_specs(fo, bo, step):
    def at(order, w, col):
        return pl.BlockSpec((S_Q, w), lambda u: (order(step(u)), col))
    specs = []
    for order in (fo, bo):
        specs += [at(order, 512, 0), at(order, 256, 2), at(order, 256, 3), at(order, 128, C_DT // 128)]
    return specs


def ssd_fwd(act, P, dtb, alog, L, Lc, name):
    T = L + Lc
    ns, fo, bo = _ssd_orders(L, Lc)

    def body(xf, bf, cf, df, xb, bb, cb, db, dtb_ref, al_ref, yf, yb, hsf, hsb, Hf, Hb):
        s = pl.program_id(0)

        @pl.when(s == 0)
        def _():
            Hf[...] = jnp.zeros_like(Hf)
            Hb[...] = jnp.zeros_like(Hb)

        for d, (x_r, b_r, c_r, dt_r, y_r, hs_r, H) in enumerate(((xf, bf, cf, df, yf, hsf, Hf), (xb, bb, cb, db, yb, hsb, Hb))):
            hin = [H[p] for p in range(S_PAIRS)]
            hs_r[0] = H[...]
            ys, houts = _ssd_chunk(
                [x_r[:, p * 128:(p + 1) * 128] for p in range(S_PAIRS)], dt_r[:, d * 8:(d + 1) * 8],
                dtb_ref[d:d + 1, 0:8], al_ref[d:d + 1, 0:8],
                [b_r[:, g * S_N:(g + 1) * S_N] for g in range(S_GROUPS)], [c_r[:, g * S_N:(g + 1) * S_N] for g in range(S_GROUPS)],
                hin, reverse=(d == 1))
            for p in range(S_PAIRS):
                y_r[:, p * 128:(p + 1) * 128] = ys[p]
                H[p] = houts[p]

    ident = lambda u: u
    small = pl.BlockSpec((8, 128), lambda u: (0, 0))
    hspec = pl.BlockSpec((1, S_PAIRS, 2 * S_P, S_N), lambda u: (u, 0, 0, 0))
    return _pc(body, name=name, grid=(ns,),
               in_specs=_ssd_in_specs(fo, bo, ident) + [small, small],
               out_specs=[pl.BlockSpec((S_Q, 512), lambda u: (fo(u), 0)), pl.BlockSpec((S_Q, 512), lambda u: (bo(u), 0)), hspec, hspec],
               out_shape=[_sds((T, 512), F32), _sds((T, 512), F32), _sds((ns, S_PAIRS, 2 * S_P, S_N), F32), _sds((ns, S_PAIRS, 2 * S_P, S_N), F32)],
               scratch_shapes=[pltpu.VMEM((S_PAIRS, 2 * S_P, S_N), F32), pltpu.VMEM((S_PAIRS, 2 * S_P, S_N), F32)],
               compiler_params=_cp(("arbitrary",), 32 << 20))(act, act, act, P, act, act, act, P, dtb, alog)


def ssd_bwd(act, P, dtb, alog, hsf, hsb, dy, L, Lc, name):
    T = L + Lc
    ns, fo, bo = _ssd_orders(L, Lc)
    step = lambda u: ns - 1 - u

    def body(xf, bf, cf, df, xb, bb, cb, db, dtb_ref, al_ref, hsf_r, hsb_r, dyf, dyb,
             dxf, dbf, dcf, ddf, dxb, dbb, dcb, ddb, ddtb, dal, dHf, dHb):
        u = pl.program_id(0)

        @pl.when(u == 0)
        def _():
            dHf[...] = jnp.zeros_like(dHf)
            dHb[...] = jnp.zeros_like(dHb)
            ddtb[...] = jnp.zeros_like(ddtb)
            dal[...] = jnp.zeros_like(dal)

        dirs = ((xf, bf, cf, df, hsf_r, dyf, dxf, dbf, dcf, ddf, dHf), (xb, bb, cb, db, hsb_r, dyb, dxb, dbb, dcb, ddb, dHb))
        for d, (x_r, b_r, c_r, dt_r, hs_r, dy_r, dx_o, db_o, dc_o, dd_o, dH) in enumerate(dirs):
            f = functools.partial(_ssd_chunk, reverse=(d == 1))
            _, vjp = jax.vjp(
                f, [x_r[:, p * 128:(p + 1) * 128] for p in range(S_PAIRS)], dt_r[:, d * 8:(d + 1) * 8],
                dtb_ref[d:d + 1, 0:8], al_ref[d:d + 1, 0:8],
                [b_r[:, g * S_N:(g + 1) * S_N] for g in range(S_GROUPS)], [c_r[:, g * S_N:(g + 1) * S_N] for g in range(S_GROUPS)],
                [hs_r[0, p] for p in range(S_PAIRS)])
            gx, gdt, gdtb, gal, gb, gc, gh = vjp(([dy_r[:, p * 128:(p + 1) * 128] for p in range(S_PAIRS)],
                                                  [dH[p] for p in range(S_PAIRS)]))
            for p in range(S_PAIRS):
                dx_o[:, p * 128:(p + 1) * 128] = gx[p]
                dH[p] = gh[p]
            for g in range(S_GROUPS):
                db_o[:, g * S_N:(g + 1) * S_N] = gb[g]
                dc_o[:, g * S_N:(g + 1) * S_N] = gc[g]
            dd_o[...] = gdt
            ddtb[d:d + 1, 0:8] += gdtb
            dal[d:d + 1, 0:8] += gal

    small = pl.BlockSpec((8, 128), lambda u: (0, 0))
    hspec = pl.BlockSpec((1, S_PAIRS, 2 * S_P, S_N), lambda u: (step(u), 0, 0, 0))
    at = lambda order, w: pl.BlockSpec((S_Q, w), lambda u: (order(step(u)), 0))
    outs = []
    for order in (fo, bo):
        outs += [at(order, 512), at(order, 256), at(order, 256), at(order, 8)]
    oshape = [_sds((T, 512), F32), _sds((T, 256), F32), _sds((T, 256), F32), _sds((T, 8), F32)]
    return _pc(body, name=name, grid=(ns,),
               in_specs=_ssd_in_specs(fo, bo, step) + [small, small, hspec, hspec, at(fo, 512), at(bo, 512)],
               out_specs=outs + [small, small], out_shape=oshape + oshape + [_sds((8, 128), F32), _sds((8, 128), F32)],
               scratch_shapes=[pltpu.VMEM((S_PAIRS, 2 * S_P, S_N), F32), pltpu.VMEM((S_PAIRS, 2 * S_P, S_N), F32)],
               compiler_params=_cp(("arbitrary",), 40 << 20))(act, act, act, P, act, act, act, P, dtb, alog, hsf, hsb, dy, dy)


def _ssm_out(yf, yb, xs, z, dskip, g):
    y = (yf + yb + dskip * xs) * _silu(z)
    return (y * lax.rsqrt(jnp.mean(y * y, axis=-1, keepdims=True) + EPS)) * g


def ssm_out_fwd(yf, yb, act, P, dskip, g, name):
    T = yf.shape[0]

    def body(yf_r, yb_r, xs_r, z_r, d_r, g_r, o_r):
        o_r[...] = _ssm_out(yf_r[...], yb_r[...], xs_r[...], z_r[...], d_r[...], g_r[...]).astype(o_r.dtype)

    row = pl.BlockSpec((TR, 512), lambda i: (i, 0))
    vec = pl.BlockSpec((1, 512), lambda i: (0, 0))
    return _pc(body, name=name, grid=(T // TR,),
               in_specs=[row, row, row, pl.BlockSpec((TR, 512), lambda i: (i, C_Z // 512)), vec, vec],
               out_specs=row, out_shape=_sds((T, 512), BF16),
               compiler_params=_cp(("parallel",), 16 << 20))(yf, yb, act, P, dskip, g)


def ssm_out_bwd(yf, yb, act, P, dskip, g, do_src, name):
    T = yf.shape[0]

    def body(yf_r, yb_r, xs_r, z_r, d_r, g_r, do_r, dy_r, dxs_r, dz_r, dv_r):
        @pl.when(pl.program_id(0) == 0)
        def _():
            dv_r[...] = jnp.zeros_like(dv_r)

        _, vjp = jax.vjp(_ssm_out, yf_r[...], yb_r[...], xs_r[...], z_r[...], d_r[...], g_r[...])
        dyf, _, dxs, dz, dd, dg = vjp(do_r[...].astype(F32))
        dy_r[...] = dyf
        dxs_r[...] = dxs
        dz_r[...] = dz.astype(dz_r.dtype)
        dv_r[0:1, :] += dd
        dv_r[1:2, :] += dg

    row = pl.BlockSpec((TR, 512), lambda i: (i, 0))
    vec = pl.BlockSpec((1, 512), lambda i: (0, 0))
    return _pc(body, name=name, grid=(T // TR,),
               in_specs=[row, row, row, pl.BlockSpec((TR, 512), lambda i: (i, C_Z // 512)), vec, vec,
                         pl.BlockSpec((TR, 512), lambda i: (i, 1))],
               out_specs=[row, row, row, pl.BlockSpec((8, 512), lambda i: (0, 0))],
               out_shape=[_sds((T, 512), F32), _sds((T, 512), F32), _sds((T, 512), BF16), _sds((8, 512), F32)],
               compiler_params=_cp(("arbitrary",), 24 << 20))(yf, yb, act, P, dskip, g, do_src)


def add_halves(xv, got, cvec, name):
    n, r, cdim = xv.shape
    h = r // 2

    def body(c_ref, x_ref, g_ref, o_ref):
        o_ref[...] = (x_ref[...].astype(F32) + g_ref[...].astype(F32)).astype(o_ref.dtype)

    gs = pltpu.PrefetchScalarGridSpec(
        num_scalar_prefetch=1, grid=(n,),
        in_specs=[pl.BlockSpec((1, h, cdim), lambda k, c_ref: (k, c_ref[0], 0)), pl.BlockSpec((1, h, cdim), lambda k, c_ref: (k, 0, 0))],
        out_specs=pl.BlockSpec((1, h, cdim), lambda k, c_ref: (k, 0, 0)))
    return _pc(body, name=name, grid_spec=gs, out_shape=_sds((n, h, cdim), BF16),
               compiler_params=_cp(("arbitrary",), 24 << 20))(cvec, xv, got)


def sum_slots(a, name):
    n, r, cdim = a.shape
    tr = _div_tile(r, 512, 16)

    def body(a_ref, o_ref):
        acc = a_ref[0].astype(F32)
        for k in range(1, n):
            acc = acc + a_ref[k].astype(F32)
        o_ref[...] = acc

    return _pc(body, name=name, grid=(r // tr,), in_specs=[pl.BlockSpec((n, tr, cdim), lambda i: (0, i, 0))],
               out_specs=pl.BlockSpec((tr, cdim), lambda i: (i, 0)), out_shape=_sds((r, cdim), F32),
               compiler_params=_cp(("parallel",), 32 << 20))(a)


def adamw(w, g, m, v, name):
    B, R, C = w.shape
    tr = _div_tile(R, max(8, (1 << 19) // max(C, 1) // 8 * 8), 8) if R % 8 == 0 else R
    c1 = 1.0 / (1.0 - ADAM_B1 ** ADAM_STEP)
    c2 = 1.0 / (1.0 - ADAM_B2 ** ADAM_STEP)

    def body(w_ref, g_ref, m_ref, v_ref, d_ref, mo_ref, vo_ref):
        gg = g_ref[...]
        mn = ADAM_B1 * m_ref[...] + (1.0 - ADAM_B1) * gg
        vn = ADAM_B2 * v_ref[...] + (1.0 - ADAM_B2) * (gg * gg)
        d_ref[...] = -ADAM_LR * ((mn * c1) / (jnp.sqrt(vn * c2) + ADAM_EPS) + ADAM_WD * w_ref[...])
        mo_ref[...] = mn
        vo_ref[...] = vn

    spec = pl.BlockSpec((1, tr, C), lambda b, i: (b, i, 0))
    return _pc(body, name=name, grid=(B, R // tr), in_specs=[spec] * 4, out_specs=[spec] * 3,
               out_shape=[_sds((B, R, C), F32)] * 3, compiler_params=_cp(("parallel", "parallel"), 32 << 20))(w, g, m, v)


def _me():
    return lax.axis_index("x"), lax.axis_index("y"), lax.axis_index("c")


def _flip(v, bit):
    return 1 - v if bit else v


def allgather8(xv, name):
    R = xv.shape[0]

    def body(x_ref, out_ref, sum_ref, send_sems, recv_sems):
        mx, my, mc = _me()
        me = 4 * mx + 2 * my + mc
        out_ref[me] = x_ref[...]
        sends, recvs = [], []
        for k in range(1, 8):
            px, py, pc = _flip(mx, k & 4), _flip(my, k & 2), _flip(mc, k & 1)
            peer = 4 * px + 2 * py + pc
            sends.append(pltpu.make_async_remote_copy(src_ref=x_ref, dst_ref=out_ref.at[me], send_sem=send_sems.at[k - 1],
                                                      recv_sem=recv_sems.at[k - 1], device_id=(px, py, pc), device_id_type=MESH))
            recvs.append(pltpu.make_async_remote_copy(src_ref=x_ref, dst_ref=out_ref.at[peer], send_sem=send_sems.at[k - 1],
                                                      recv_sem=recv_sems.at[k - 1], device_id=(px, py, pc), device_id_type=MESH))
        for cp in sends:
            cp.start()
        for cp in recvs:
            cp.wait_recv()
        for cp in sends:
            cp.wait_send()
        acc = out_ref[0]
        for d in range(1, 8):
            acc = acc + out_ref[d]
        sum_ref[...] = acc

    vm = pl.BlockSpec(memory_space=pltpu.VMEM)
    return _pc(body, name=name, pin=False, in_specs=[vm], out_specs=[vm, vm], out_shape=[_sds((8, R, 128), F32), _sds((R, 128), F32)],
               scratch_shapes=[pltpu.SemaphoreType.DMA((7,)), pltpu.SemaphoreType.DMA((7,))],
               compiler_params=_cp(None, 32 << 20))(xv)


def _other_chips(mx, my):
    return [(1 - mx, my), (mx, 1 - my), (1 - mx, 1 - my)]


def _halves(r, mc, mult):
    h = r // 2
    return pl.ds(pl.multiple_of(mc * h, mult), h), pl.ds(pl.multiple_of((1 - mc) * h, mult), h)


def _rcopy(src, dst, send_sems, recv_sems, k, to):
    return pltpu.make_async_remote_copy(src_ref=src, dst_ref=dst, send_sem=send_sems.at[k], recv_sem=recv_sems.at[k],
                                        device_id=to, device_id_type=MESH)


def _gather_body(xs, outs, send_sems, recv_sems):
    n = len(xs)
    mx, my, mc = _me()
    chip = 2 * mx + my
    sib = (mx, my, 1 - mc)
    chips = _other_chips(mx, my)
    idx = [2 * cx + cy for cx, cy in chips]
    cp = functools.partial(_rcopy, send_sems=send_sems, recv_sems=recv_sems)
    hv = [_halves(x.shape[0], mc, 16) for x in xs]
    first, passed = [], []
    for a in range(n):
        for j, (cx, cy) in enumerate(chips):
            first.append(cp(xs[a].at[hv[a][0]], outs[a].at[chip, hv[a][0]], k=6 * a + j, to=(cx, cy, mc)))
            first[-1].start()
    for a in range(n):
        for j in range(3):
            cp(xs[a].at[hv[a][0]], outs[a].at[idx[j], hv[a][0]], k=6 * a + j, to=sib).wait_recv()
            passed.append(cp(outs[a].at[idx[j], hv[a][0]], outs[a].at[idx[j], hv[a][0]], k=6 * a + 3 + j, to=sib))
            passed[-1].start()
    for a in range(n):
        for j in range(3):
            cp(xs[a].at[hv[a][1]], outs[a].at[idx[j], hv[a][1]], k=6 * a + 3 + j, to=sib).wait_recv()
    for c_ in first + passed:
        c_.wait_send()


def _my_chip():
    return 2 * lax.axis_index("x") + lax.axis_index("y")


def _own_slots(outs, shards):
    return [lax.dynamic_update_index_in_dim(o, x, _my_chip(), 0) for o, x in zip(outs, shards)]


def gather_weights(shards, name):
    n = len(shards)

    def body(*refs):
        _gather_body(refs[:n], refs[n:2 * n], *refs[2 * n:])

    hbm = pl.BlockSpec(memory_space=pl.ANY)
    outs = _pc(body, name=name, in_specs=[hbm] * n, out_specs=[hbm] * n, out_shape=[_sds((4,) + x.shape, x.dtype) for x in shards],
               scratch_shapes=[pltpu.SemaphoreType.DMA((6 * n,)), pltpu.SemaphoreType.DMA((6 * n,))])(*shards)
    return _own_slots(outs, shards)


GATHER_REST_ID = 3


def gather_weights_sc(shards, name):
    n = len(shards)
    x_refs = [jax.new_ref(x, memory_space=pltpu.MemorySpace.HBM) for x in shards]
    out_refs = [jax.empty_ref(_sds((4,) + x.shape, x.dtype), memory_space=pltpu.MemorySpace.HBM) for x in shards]

    @pl.kernel(mesh=plsc.ScalarSubcoreMesh(axis_name="sc", num_cores=1), name=name,
               scratch_types=(pltpu.SemaphoreType.DMA((6 * n,)), pltpu.SemaphoreType.DMA((6 * n,))),
               compiler_params=pltpu.CompilerParams(collective_id=GATHER_REST_ID))
    def launch(send_sems, recv_sems):
        mx, my, mc = _me()
        barrier = pltpu.get_barrier_semaphore()
        for peer in [(mx, my, 1 - mc)] + [(cx, cy, mc) for cx, cy in _other_chips(mx, my)]:
            pl.semaphore_signal(barrier, inc=1, device_id=peer, device_id_type=MESH)
        pl.semaphore_wait(barrier, 4)
        _gather_body(x_refs, out_refs, send_sems, recv_sems)

    launch()
    return _own_slots([o[...] for o in out_refs], shards)


def swap_halves(arrs, name):
    n = len(arrs)

    def body(*refs):
        xs, outs = refs[:n], refs[n:2 * n]
        send_sems, recv_sems = refs[2 * n:]
        mx, my, mc = _me()
        cps = []
        for a in range(n):
            theirs = _halves(xs[a].shape[1], mc, 16)[1]
            cps.append(_rcopy(xs[a].at[pl.ds(0, 4), theirs], outs[a], send_sems, recv_sems, a, (mx, my, 1 - mc)))
            cps[-1].start()
        for c_ in cps:
            c_.wait()

    hbm = pl.BlockSpec(memory_space=pl.ANY)
    return _pc(body, name=name, in_specs=[hbm] * n, out_specs=[hbm] * n,
               out_shape=[_sds((4, x.shape[1] // 2, x.shape[2]), x.dtype) for x in arrs],
               scratch_shapes=[pltpu.SemaphoreType.DMA((n,)), pltpu.SemaphoreType.DMA((n,))])(*arrs)


SCATTER_ID = 4


def scatter_chips_sc(arrs, name):
    n = len(arrs)
    x_refs = [jax.new_ref(x, memory_space=pltpu.MemorySpace.HBM) for x in arrs]
    out_refs = [jax.empty_ref(_sds(x.shape, x.dtype), memory_space=pltpu.MemorySpace.HBM) for x in arrs]

    @pl.kernel(mesh=plsc.ScalarSubcoreMesh(axis_name="sc", num_cores=1), name=name,
               scratch_types=(pltpu.SemaphoreType.DMA((3 * n,)), pltpu.SemaphoreType.DMA((3 * n,))),
               compiler_params=pltpu.CompilerParams(collective_id=SCATTER_ID))
    def launch(send_sems, recv_sems):
        mx, my, mc = _me()
        chip = 2 * mx + my
        chips = _other_chips(mx, my)
        idx = [2 * cx + cy for cx, cy in chips]
        barrier = pltpu.get_barrier_semaphore()
        for cx, cy in chips:
            pl.semaphore_signal(barrier, inc=1, device_id=(cx, cy, mc), device_id_type=MESH)
        pl.semaphore_wait(barrier, 3)
        cp = functools.partial(_rcopy, send_sems=send_sems, recv_sems=recv_sems)
        sends = []
        for a in range(n):
            for j, (cx, cy) in enumerate(chips):
                sends.append(cp(x_refs[a].at[idx[j]], out_refs[a].at[chip], k=3 * a + j, to=(cx, cy, mc)))
                sends[-1].start()
        for a in range(n):
            for j, (cx, cy) in enumerate(chips):
                cp(x_refs[a].at[idx[j]], out_refs[a].at[idx[j]], k=3 * a + j, to=(cx, cy, mc)).wait_recv()
        for c_ in sends:
            c_.wait_send()

    launch()
    return _own_slots([o[...] for o in out_refs], [lax.dynamic_index_in_dim(x, _my_chip(), 0, keepdims=False) for x in arrs])


def share_halves(parts, name):
    flat = [p for w in parts for p in w]
    nw, n = len(parts), len(flat)
    depth = n // nw

    def body(*refs):
        xs, outs = refs[:n], refs[n:n + nw]
        send_sems, recv_sems = refs[n + nw:]
        mx, my, mc = _me()
        sib = (mx, my, 1 - mc)
        sends, recvs = [], []
        for a in range(n):
            w, l = a // depth, a % depth
            mine, theirs = _halves(outs[w].shape[1], mc, 8)
            sends.append(_rcopy(xs[a], outs[w].at[l, mine], send_sems, recv_sems, a, sib))
            recvs.append(_rcopy(xs[a], outs[w].at[l, theirs], send_sems, recv_sems, a, sib))
            sends[-1].start()
        for c_ in recvs:
            c_.wait_recv()
        for c_ in sends:
            c_.wait_send()

    hbm = pl.BlockSpec(memory_space=pl.ANY)
    outs = _pc(body, name=name, in_specs=[hbm] * n, out_specs=[hbm] * nw,
               out_shape=[_sds((depth, 2 * w[0].shape[0], w[0].shape[1]), F32) for w in parts],
               scratch_shapes=[pltpu.SemaphoreType.DMA((n,)), pltpu.SemaphoreType.DMA((n,))])(*flat)
    outs = list(outs)
    mc = lax.axis_index("c")
    for w in range(nw):
        for l in range(depth):
            h = parts[w][l].shape[0]
            outs[w] = lax.dynamic_update_slice(outs[w], parts[w][l][None], (l, mc * h, 0))
    return outs


_BIG = ("w_in", "w_out", "w_ffn_in", "w_ffn_out")
N_CHIPS = 4
DEPTH = 2


def _pad_rows(v, mult=8):
    n = v.shape[0]
    rows = -(-n // 128)
    rows = -(-rows // mult) * mult
    return jnp.pad(v, (0, rows * 128 - n)).reshape(rows, 128)


class _Flat:
    def __init__(self):
        self.items = []

    def add(self, name, a):
        self.items.append((name, a.shape, a.reshape(-1).astype(F32)))

    def rows(self):
        return _pad_rows(jnp.concatenate([a for _, _, a in self.items]))

    def split(self, rows):
        flat = rows.reshape(-1)
        out, o = {}, 0
        for name, shape, a in self.items:
            out[name] = flat[o:o + a.shape[0]].reshape(shape)
            o += a.shape[0]
        return out

    def split_lead(self, rows3):
        n = rows3.shape[0]
        flat = rows3.reshape(n, -1)
        out, o = {}, 0
        for name, shape, a in self.items:
            out[name] = flat[:, o:o + a.shape[0]].reshape((n,) + tuple(shape))
            o += a.shape[0]
        return out


def _gsv(rows):
    z = jnp.zeros((2, D), F32)
    r = [z if a is None else a for a in rows] + [z] * 5
    return jnp.stack(r, axis=1)


def _pad8(a, rows=8, cols=128):
    return jnp.zeros((rows, cols), F32).at[:a.shape[0], :a.shape[1]].set(a.astype(F32))


def kernel(x, c, ctx, c_ctx, w_mod, b_mod, g_mix, w_in, wa_sink, na_rpb, ssm_conv_w, ssm_conv_b, ssm_dt_bias, ssm_a_log, ssm_d, ssm_norm_g, w_out, g_ffn, w_ffn_in, w_ffn_out, g_final, loss_target, m_c_ctx, m_w_mod, m_b_mod, m_g_mix, m_w_in, m_wa_sink, m_na_rpb, m_ssm_conv_w, m_ssm_conv_b, m_ssm_dt_bias, m_ssm_a_log, m_ssm_d, m_ssm_norm_g, m_w_out, m_g_ffn, m_w_ffn_in, m_w_ffn_out, m_g_final, v_c_ctx, v_w_mod, v_b_mod, v_g_mix, v_w_in, v_wa_sink, v_na_rpb, v_ssm_conv_w, v_ssm_conv_b, v_ssm_dt_bias, v_ssm_a_log, v_ssm_d, v_ssm_norm_g, v_w_out, v_g_ffn, v_w_ffn_in, v_w_ffn_out, v_g_final):
    L, Lc = x.shape[1], ctx.shape[1]
    T = L + Lc
    nL = L // TR
    mx, my, mc = lax.axis_index("x"), lax.axis_index("y"), lax.axis_index("c")
    dev = 4 * mx + 2 * my + mc
    chip = 2 * mx + my
    MODW = 6 * D // N_CHIPS
    CW = 1024 // N_CHIPS

    sc = _silu(c.astype(F32))
    scc = _silu(c_ctx.astype(F32))[None]
    f1 = _Flat()
    f1.add("sc", sc)
    f1.add("conv_w", ssm_conv_w)
    g1, _ = allgather8(f1.rows(), "gather_cond")
    g1 = f1.split_lead(g1)
    sc_all = g1["sc"][:, 0]
    conv_w = jnp.concatenate([g1["conv_w"][2 * k] for k in range(N_CHIPS)], axis=-1)
    A16 = jnp.concatenate([sc_all, scc, jnp.zeros((7, D), F32)], axis=0)

    mod_part = matmul_layers(A16, w_mod, "nn", "mod_fwd")
    f2 = _Flat()
    f2.add("mod", mod_part)
    g2, _ = allgather8(f2.rows(), "gather_mod")
    g2 = f2.split_lead(g2)["mod"]
    mods = jnp.concatenate([g2[2 * k] for k in range(N_CHIPS)], axis=-1) + b_mod[:, None, :]
    mod_l = lax.dynamic_index_in_dim(mods, dev, axis=1, keepdims=False).reshape(DEPTH, 6, D)
    mod_c = mods[:, 8].reshape(DEPTH, 6, D)
    mod = jnp.stack([mod_l, mod_c], axis=1)
    mrow = lambda l, j: mod[l, :, j]

    own = {"w_in": w_in, "w_out": w_out, "w_ffn_in": w_ffn_in, "w_ffn_out": w_ffn_out}
    sh16 = [own[n][l].astype(BF16) for n in _BIG for l in range(DEPTH)]
    after_mod = (g2[0, 0, 0, 0] * 0).astype(BF16)
    gath = list(gather_weights([sh16[0] + after_mod], "gather_first"))
    after_first = (gath[0][0, 0, 0] * 0).astype(BF16)
    gath += list(gather_weights_sc([sh16[1] + after_first] + sh16[2:], "gather_rest"))
    gw = {n: [gath[DEPTH * i + l] for l in range(DEPTH)] for i, n in enumerate(_BIG)}
    W_in = [jnp.pad(jnp.concatenate([g[k] for k in range(N_CHIPS)], axis=1), ((0, 0), (0, IN_PAD - IN_COLS))) for g in gw["w_in"]]
    W_out = [g.reshape(D, D) for g in gw["w_out"]]
    W_fo = [g.reshape(D_FF, D) for g in gw["w_ffn_out"]]
    W_fi = gw["w_ffn_in"]

    cos, sin = rope_tables(L, Lc)
    x0 = jnp.concatenate([x[0], ctx[0]], axis=0).astype(F32)

    sv = []
    xin = x0
    gsv_first = _gsv([None, mrow(0, 0), mrow(0, 1)])
    _, h1 = res_norm_mod(x0, None, gsv_first, g_mix[0][None], nL, "norm_first")
    for l in range(DEPTH):
        s = {"xin": xin, "h1": h1}
        P = matmul(h1, W_in[l], "nn", F32, f"in_proj{l}", tn=IN_PAD)
        qr, kr, kb, vb = rope_apply(P, C_QA // 256, P, C_KA // 128, cos, sin, False, f"rope{l}", kv_src=P)
        sink8 = _pad8(jnp.broadcast_to(wa_sink[l][:, None], (WA_HEADS, 128)))
        krs, va = _swap_halves_lanes(kr), P[:, C_VA:C_VA + 128]
        vas = _swap_halves_lanes(va)
        oa, sta = win_attn_fwd(qr, kr, krs, va, vas, sink8, L, Lc, f"wa_fwd{l}")
        bias = na_bias_table(na_rpb[l], l)
        ob, stb = na_fwd(P, kb, vb, bias, L, Lc, f"na_fwd{l}")
        w8 = jnp.concatenate([conv_w[l], jnp.zeros((1, 1024), F32)], axis=0)
        pre, act = conv_silu_fwd(P, w8, ssm_conv_b[l][None], nL, f"conv_fwd{l}")
        dtb8, al8 = _pad8(ssm_dt_bias[l]), _pad8(ssm_a_log[l])
        yf, yb, hsf, hsb = ssd_fwd(act, P, dtb8, al8, L, Lc, f"ssd_fwd{l}")
        dskip = jnp.repeat(ssm_d[l], S_P)[None]
        oc = ssm_out_fwd(yf, yb, act, P, dskip, ssm_norm_g[l][None], f"ssm_out_fwd{l}")
        mixin = [(oa, 0), (ob, 256), (oc, 512)]
        mix = out_proj_fwd(mixin, W_out[l], f"out_proj{l}")
        gsv_mid = _gsv([mrow(l, 2), mrow(l, 3), mrow(l, 4)])
        x1, h2 = res_norm_mod(xin, mix, gsv_mid, g_ffn[l][None], nL, f"norm_mid{l}")
        gu, af = ffn_in_swiglu(h2, W_fi[l], f"ffn_in{l}")
        fo = matmul(af, W_fo[l], "nn", BF16, f"ffn_out{l}", tk=D_FF)
        s.update(P=P, qr=qr, kr=kr, krs=krs, va=va, vas=vas, sink8=sink8, oa=oa, sta=sta, ob=ob, stb=stb, kb=kb, vb=vb, bias=bias, w8=w8, pre=pre, act=act, dtb8=dtb8, al8=al8, yf=yf,
                 yb=yb, hsf=hsf, hsb=hsb, dskip=dskip, mixin=mixin, mix=mix, gsv_mid=gsv_mid, x1=x1, h2=h2, gu=gu, af=af, fo=fo)
        if l + 1 < DEPTH:
            s["gsv_end"] = _gsv([mrow(l, 5), mrow(l + 1, 0), mrow(l + 1, 1)])
            xin, h1 = res_norm_mod(x1, fo, s["gsv_end"], g_mix[l + 1][None], nL, f"norm_end{l}")
        else:
            s["gsv_end"] = _gsv([mrow(l, 5), None, None])
        sv.append(s)

    last = sv[-1]
    loss8, dres, dfo, dgsv_end, dg_final = final_loss(last["x1"], last["fo"], last["gsv_end"], g_final[None], loss_target[0].astype(F32), nL, "final_loss")
    loss = lax.psum(loss8[0, 0], ("x", "y", "c"))

    dmod = [[None] * 6 for _ in range(DEPTH)]
    gW = {n: [None] * DEPTH for n in _BIG}
    small = [dict() for _ in range(DEPTH)]
    parts = [None] * DEPTH
    cvec = mc.astype(jnp.int32).reshape(1)
    grad_x = None
    for l in reversed(range(DEPTH)):
        s = sv[l]
        dmod[l][5] = dgsv_end[:, 0]
        if l + 1 < DEPTH:
            dmod[l + 1][0], dmod[l + 1][1] = dgsv_end[:, 1], dgsv_end[:, 2]
        dgu = ffn_out_dx_swiglu(dfo, W_fo[l], s["gu"], f"ffn_out_dx{l}")
        gW["w_ffn_out"][l] = matmul(s["af"], dfo, "tn", BF16, f"ffn_out_dw{l}", tm=1408, tk=T).reshape(N_CHIPS, D_FF // N_CHIPS, D)
        dh2 = matmul_fi(dgu, W_fi[l], "nt", BF16, f"ffn_in_dx{l}")
        gW["w_ffn_in"][l] = matmul_fi(s["h2"], dgu, "tn", BF16, f"ffn_in_dw{l}")
        dres, dmix, dgsv_mid, dg_ffn = res_norm_mod_bwd(s["x1"], s["mix"], s["gsv_mid"], g_ffn[l][None], dh2, dres, nL, f"norm_mid_bwd{l}")
        dmod[l][2], dmod[l][3], dmod[l][4] = dgsv_mid[:, 0], dgsv_mid[:, 1], dgsv_mid[:, 2]
        dmixin = matmul(dmix, W_out[l], "nt", BF16, f"out_proj_dx{l}")
        gW["w_out"][l] = out_proj_dw(s["mixin"], dmix, f"out_proj_dw{l}").reshape(N_CHIPS, D // N_CHIPS, D)
        P = s["P"]
        dqr, dkr, dkrs, dva, dvas, dsink = win_attn_bwd(s["qr"], s["kr"], s["krs"], s["va"], s["vas"], s["sink8"], dmixin, s["oa"], s["sta"], L, Lc,
                                                        f"wa_bwd{l}")
        dkr, dva = dkr + _swap_halves_lanes(dkrs), dva + _swap_halves_lanes(dvas)
        dqa, dka = rope_apply(dqr, 0, dkr[WA_BLK:WA_BLK + T], 0, cos, sin, True, f"rope_bwd{l}")
        dqb, dkb, dvb, dbias = na_bwd(P, s["kb"], s["vb"], s["bias"], dmixin, s["ob"], s["stb"], L, Lc, f"na_bwd{l}")
        dy, dxs1, dz, dvec = ssm_out_bwd(s["yf"], s["yb"], s["act"], P, s["dskip"], ssm_norm_g[l][None], dmixin, f"ssm_out_bwd{l}")
        dxf, dbf, dcf, ddf, dxb, dbb, dcb, ddb, ddtb, dal = ssd_bwd(s["act"], P, s["dtb8"], s["al8"], s["hsf"], s["hsb"], dy, L, Lc, f"ssd_bwd{l}")
        dpre = dsilu(s["pre"], [dxf, dxb, dxs1], [dbf, dbb], [dcf, dcb], f"dsilu{l}")
        dxbc, dw8, db8 = conv_bwd(dpre, P, s["w8"], nL, f"conv_bwd{l}")
        ddt = jnp.concatenate([ddf, ddb, jnp.zeros((T, IN_PAD - IN_COLS), F32)], axis=1)
        pieces = [(dqa, C_QA), (dqb, C_QB), (dz, C_Z), (dka, C_KA), (dva[WA_BLK:WA_BLK + T], C_VA), (dkb, C_KB), (dvb, C_VB),
                  (dxbc, C_XBC), (ddt, C_DT)]
        dh1, dwin = in_proj_bwd(pieces, s["h1"], W_in[l], f"in_proj_bwd{l}")
        cw = IN_COLS // N_CHIPS
        gW["w_in"][l] = jnp.stack([dwin[:, k * cw:(k + 1) * cw] for k in range(N_CHIPS)])
        garr = [gW[n][l] for n in _BIG]
        got = swap_halves(garr, f"reduce_d2d{l}")
        chip_sum = [add_halves(garr[a], got[a], cvec, f"reduce_add_pair{l}_{a}") for a in range(len(garr))]
        parts[l] = scatter_chips_sc(chip_sum, f"reduce_ici{l}")
        small[l] = dict(g_ffn=dg_ffn[0], wa_sink=dsink[:WA_HEADS, 0], na_rpb=na_rpb_grad(dbias, l), conv_w=dw8[:S_CONV], conv_b=db8[0],
                        dt_bias=ddtb[:2, :8], a_log=dal[:2, :8], ssm_d=dvec[0].reshape(S_HEADS, S_P).sum(axis=1), norm_g=dvec[1])
        if l > 0:
            p = sv[l - 1]
            dres, dfo, dgsv_end, dg_mix = res_norm_mod_bwd(s["xin"], p["fo"], p["gsv_end"], g_mix[l][None], dh1, dres, nL, f"norm_end_bwd{l - 1}")
        else:
            grad_x, _, dgsv_first, dg_mix = res_norm_mod_bwd(s["xin"], None, gsv_first, g_mix[0][None], dh1, dres, nL, "norm_first_bwd")
            dmod[0][0], dmod[0][1] = dgsv_first[:, 1], dgsv_first[:, 2]
        small[l]["g_mix"] = dg_mix[0]
    for l in range(DEPTH):
        for j in range(6):
            if dmod[l][j] is None:
                dmod[l][j] = jnp.zeros((2, D), F32)
    dmod = jnp.stack([jnp.stack(r, axis=1) for r in dmod])

    f3 = _Flat()
    f3.add("dmod_l", dmod[:, 0].reshape(DEPTH, 6 * D))
    f3.add("dmod_c", dmod[:, 1].reshape(DEPTH, 6 * D))
    f3.add("g_final", dg_final[0])
    for n in ("g_mix", "g_ffn", "wa_sink", "na_rpb", "conv_w", "conv_b", "dt_bias", "a_log", "ssm_d", "norm_g"):
        f3.add(n, jnp.stack([small[l][n] for l in range(DEPTH)]))
    g3, s3 = allgather8(f3.rows(), "reduce_small")
    dmod_all = f3.split_lead(g3)["dmod_l"]
    s3 = f3.split(s3)
    dmodc_tot = s3["dmod_c"]
    col0 = chip * MODW
    G16, G16c = [], []
    for l in range(DEPTH):
        rows = jnp.concatenate([dmod_all[:, l], dmodc_tot[l][None], jnp.zeros((7, 6 * D), F32)], axis=0)
        G16.append(lax.dynamic_slice_in_dim(rows, col0, MODW, axis=1))
        rc = jnp.concatenate([dmodc_tot[l][None], jnp.zeros((15, 6 * D), F32)], axis=0)
        G16c.append(lax.dynamic_slice_in_dim(rc, col0, MODW, axis=1))
    grad_w_mod = matmul_layers(A16, jnp.stack(G16), "tn", "mod_dw")
    dscc_part = matmul_layers(jnp.stack(G16c), w_mod, "nt", "mod_dx")[:, 0].sum(axis=0)
    _, s4 = allgather8(_pad_rows(dscc_part * (mc == 1).astype(F32)), "reduce_cctx")
    dscc = s4.reshape(-1)[:D]
    cc = c_ctx.astype(F32)
    sg = 1.0 / (1.0 + jnp.exp(-cc))
    grad_c_ctx = dscc * (sg * (1.0 + cc * (1.0 - sg)))

    halves = [[sum_slots(parts[l][i], f"reduce_add_chips{l}_{i}") for l in range(DEPTH)] for i in range(len(_BIG))]
    gsh = dict(zip(_BIG, share_halves(halves, "reduce_share")))

    grads = {"c_ctx": grad_c_ctx, "w_mod": grad_w_mod, "b_mod": s3["dmod_l"] + s3["dmod_c"], "g_mix": s3["g_mix"], "w_in": gsh["w_in"],
             "wa_sink": s3["wa_sink"], "na_rpb": s3["na_rpb"],
             "ssm_conv_w": lax.dynamic_slice_in_dim(s3["conv_w"], chip * CW, CW, axis=2), "ssm_conv_b": s3["conv_b"],
             "ssm_dt_bias": s3["dt_bias"], "ssm_a_log": s3["a_log"], "ssm_d": s3["ssm_d"], "ssm_norm_g": s3["norm_g"],
             "w_out": gsh["w_out"], "g_ffn": s3["g_ffn"], "w_ffn_in": gsh["w_ffn_in"], "w_ffn_out": gsh["w_ffn_out"], "g_final": s3["g_final"]}
    wts = {"c_ctx": c_ctx, "w_mod": w_mod, "b_mod": b_mod, "g_mix": g_mix, "w_in": w_in, "wa_sink": wa_sink, "na_rpb": na_rpb,
           "ssm_conv_w": ssm_conv_w, "ssm_conv_b": ssm_conv_b, "ssm_dt_bias": ssm_dt_bias, "ssm_a_log": ssm_a_log, "ssm_d": ssm_d,
           "ssm_norm_g": ssm_norm_g, "w_out": w_out, "g_ffn": g_ffn, "w_ffn_in": w_ffn_in, "w_ffn_out": w_ffn_out, "g_final": g_final}
    ms = {"c_ctx": m_c_ctx, "w_mod": m_w_mod, "b_mod": m_b_mod, "g_mix": m_g_mix, "w_in": m_w_in, "wa_sink": m_wa_sink, "na_rpb": m_na_rpb,
          "ssm_conv_w": m_ssm_conv_w, "ssm_conv_b": m_ssm_conv_b, "ssm_dt_bias": m_ssm_dt_bias, "ssm_a_log": m_ssm_a_log, "ssm_d": m_ssm_d,
          "ssm_norm_g": m_ssm_norm_g, "w_out": m_w_out, "g_ffn": m_g_ffn, "w_ffn_in": m_w_ffn_in, "w_ffn_out": m_w_ffn_out, "g_final": m_g_final}
    vs = {"c_ctx": v_c_ctx, "w_mod": v_w_mod, "b_mod": v_b_mod, "g_mix": v_g_mix, "w_in": v_w_in, "wa_sink": v_wa_sink, "na_rpb": v_na_rpb,
          "ssm_conv_w": v_ssm_conv_w, "ssm_conv_b": v_ssm_conv_b, "ssm_dt_bias": v_ssm_dt_bias, "ssm_a_log": v_ssm_a_log, "ssm_d": v_ssm_d,
          "ssm_norm_g": v_ssm_norm_g, "w_out": v_w_out, "g_ffn": v_g_ffn, "w_ffn_in": v_w_ffn_in, "w_ffn_out": v_w_ffn_out, "g_final": v_g_final}
    names = list(wts)
    grads = {n: grads[n].reshape(wts[n].shape).astype(F32) for n in names}
    big = ("w_mod", "w_in", "w_out", "w_ffn_in", "w_ffn_out")
    delta, new_m, new_v = {}, {}, {}
    for n in big:
        delta[n], new_m[n], new_v[n] = adamw(wts[n], grads[n], ms[n], vs[n], f"adamw_{n}")
    packs = []
    for src in (wts, grads, ms, vs):
        f = _Flat()
        for n in names:
            if n not in big:
                f.add(n, src[n])
        packs.append(f)
    d_, m_, v_ = adamw(*[f.rows()[None] for f in packs], "adamw_small")
    for dst, rows in ((delta, d_), (new_m, m_), (new_v, v_)):
        dst.update(packs[0].split(rows[0]))

    return (loss, grad_x[:L][None], *[grads[n] for n in names], *[delta[n] for n in names],
            *[new_m[n] for n in names], *[new_v[n] for n in names])
```

```python
import functools

import numpy as np
import jax
import jax.numpy as jnp
from jax import lax
from jax.experimental import pallas as pl
from jax.experimental.pallas import tpu as pltpu
from jax.experimental.pallas import tpu_sc as plsc

F32 = jnp.float32
BF16 = jnp.bfloat16
_MXU = jnp.bfloat16
_HI = lax.Precision.HIGHEST
MESH = pl.DeviceIdType.MESH

D = 1024
HD = 64
GRID_W = 64
EPS = 1e-6
ROPE_BASE = 10000.0
WA_HEADS, WA_KV = 4, 2
WA_BLK = 128
NA_HEADS, NA_KH, NA_KW = 4, 8, 16
S_HEADS, S_P, S_INNER, S_GROUPS, S_N, S_CONV, S_Q = 8, 64, 512, 2, 128, 7, 128
D_FF = 2816
IN_COLS = 2832
IN_PAD = 2944
C_QA, C_QB, C_Z, C_KA, C_VA, C_KB, C_VB, C_XBC, C_DT = 0, 256, 512, 1024, 1152, 1280, 1536, 1792, 2816
ADAM_LR, ADAM_B1, ADAM_B2, ADAM_EPS, ADAM_WD, ADAM_STEP = 0.001, 0.9, 0.999, 1e-08, 0.01, 10

TR = 256
NEG = -1e30
VMEM_CAP = 56 * 1024 * 1024


PIN_BYTES = 256 * 1024


def _is_big(a):
    return hasattr(a, "shape") and len(a.shape) >= 2 and int(np.prod(a.shape)) * jnp.dtype(a.dtype).itemsize >= PIN_BYTES


def _pc(body, *, out_shape, pin=True, **kw):
    if not pin:
        return pl.pallas_call(body, out_shape=out_shape, **kw)
    one = isinstance(out_shape, jax.ShapeDtypeStruct)
    outs = [pltpu.HBM(s.shape, s.dtype) if _is_big(s) else s for s in ([out_shape] if one else out_shape)]
    call = pl.pallas_call(body, out_shape=outs[0] if one else outs, **kw)
    return lambda *args: call(*[pltpu.with_memory_space_constraint(a, pltpu.HBM) if _is_big(a) else a for a in args])


def _cp(sem=None, vmem=None):
    kw = {}
    if sem is not None:
        kw["dimension_semantics"] = sem
    if vmem is not None:
        kw["vmem_limit_bytes"] = int(min(max(vmem, 16 * 1024 * 1024), VMEM_CAP))
    return pltpu.CompilerParams(**kw)


def _sds(shape, dtype):
    return jax.ShapeDtypeStruct(tuple(shape), dtype)


_DIMS = {"nn": ((1,), (0,)), "nt": ((1,), (1,)), "tn": ((0,), (0,))}


def _dg(a, b, dims):
    return lax.dot_general(a.astype(_MXU), b.astype(_MXU), (dims, ((), ())), preferred_element_type=F32)


@functools.partial(jax.custom_vjp, nondiff_argnums=(2,))
def bdot(a, b, mode):
    return _dg(a, b, _DIMS[mode])


def _bdot_fwd(a, b, mode):
    return bdot(a, b, mode), (a, b)


def _bdot_bwd(mode, res, g):
    a, b = res
    if mode == "nn":
        return bdot(g, b, "nt"), bdot(a, g, "tn")
    if mode == "nt":
        return bdot(g, b, "nn"), bdot(g, a, "tn")
    return bdot(b, g, "nt"), bdot(a, g, "nn")


bdot.defvjp(_bdot_fwd, _bdot_bwd)


def hdot(a, b, mode="nn"):
    return lax.dot_general(a, b, (_DIMS[mode], ((), ())), precision=_HI, preferred_element_type=F32)


def _silu(x):
    return x / (1.0 + jnp.exp(-x))


def _softplus(x):
    return jnp.maximum(x, 0.0) + jnp.log(1.0 + jnp.exp(-jnp.abs(x)))


def _div_tile(n, cap, mult):
    if n <= cap:
        return n
    best = None
    for t in range(mult, cap + 1, mult):
        if n % t == 0:
            best = t
    assert best is not None, (n, cap, mult)
    return best


def matmul(a, b, mode, out_dtype, name, tm=640, tn=1536, tk=1408, hi=False):
    if mode == "tn":
        K, M = a.shape
    else:
        M, K = a.shape
    N = b.shape[0] if mode == "nt" else b.shape[1]
    tm = _div_tile(M, tm, 128 if mode == "tn" else 16)
    tn = _div_tile(N, tn, 128)
    tk = _div_tile(K, tk, 128 if mode != "tn" else 16)
    nk = K // tk
    dims = _DIMS[mode]

    def body(a_ref, b_ref, o_ref, *acc):
        if hi:
            part = lax.dot_general(a_ref[...], b_ref[...], (dims, ((), ())), precision=_HI, preferred_element_type=F32)
        else:
            part = _dg(a_ref[...], b_ref[...], dims)
        if nk == 1:
            o_ref[...] = part.astype(o_ref.dtype)
        else:
            k = pl.program_id(2)

            @pl.when(k == 0)
            def _():
                acc[0][...] = part

            @pl.when(k > 0)
            def _():
                acc[0][...] += part

            @pl.when(k == nk - 1)
            def _():
                o_ref[...] = acc[0][...].astype(o_ref.dtype)

    if mode == "tn":
        a_spec = pl.BlockSpec((tk, tm), lambda i, j, k: (k, i))
    else:
        a_spec = pl.BlockSpec((tm, tk), lambda i, j, k: (i, k))
    if mode == "nt":
        b_spec = pl.BlockSpec((tn, tk), lambda i, j, k: (j, k))
    else:
        b_spec = pl.BlockSpec((tk, tn), lambda i, j, k: (k, j))
    isz = lambda x: jnp.dtype(x.dtype).itemsize
    vmem = 2 * (tm * tk * isz(a) + tk * tn * isz(b) + tm * tn * jnp.dtype(out_dtype).itemsize) + 3 * tm * tn * 4
    return _pc(
        body, name=name, grid=(M // tm, N // tn, nk),
        in_specs=[a_spec, b_spec], out_specs=pl.BlockSpec((tm, tn), lambda i, j, k: (i, j)),
        out_shape=_sds((M, N), out_dtype),
        scratch_shapes=[pltpu.VMEM((tm, tn), F32)] if nk > 1 else [],
        compiler_params=_cp(("parallel", "parallel", "arbitrary"), vmem + (8 << 20)),
    )(a, b)


def matmul_layers(a, b, mode, name):
    nl = b.shape[0]
    a3 = a if a.ndim == 3 else a[None]
    shared = a3.shape[0] == 1
    M = a3.shape[2] if mode == "tn" else a3.shape[1]
    N = b.shape[1] if mode == "nt" else b.shape[2]

    def body(a_ref, b_ref, o_ref):
        o_ref[0] = _dg(a_ref[0], b_ref[0], _DIMS[mode])

    return _pc(body, name=name, grid=(nl,),
               in_specs=[pl.BlockSpec((1,) + a3.shape[1:], (lambda l: (0, 0, 0)) if shared else (lambda l: (l, 0, 0))),
                         pl.BlockSpec((1,) + b.shape[1:], lambda l: (l, 0, 0))],
               out_specs=pl.BlockSpec((1, M, N), lambda l: (l, 0, 0)), out_shape=_sds((nl, M, N), F32),
               compiler_params=_cp(("parallel",), 48 << 20))(a3, b)


def out_proj_fwd(pieces, w, name):
    T = pieces[0][0].shape[0]
    arrs, offs = [a for a, _ in pieces], [o for _, o in pieces]
    n = len(arrs)
    tm = _div_tile(T, 640, 16)

    def body(*refs):
        w_ref, o_ref = refs[n], refs[n + 1]
        acc = None
        for j in range(n):
            part = _dg(refs[j][...], w_ref[offs[j]:offs[j] + arrs[j].shape[1], :], _DIMS["nn"])
            acc = part if acc is None else acc + part
        o_ref[...] = acc.astype(o_ref.dtype)

    return _pc(body, name=name, grid=(T // tm,),
               in_specs=[pl.BlockSpec((tm, a.shape[1]), lambda i: (i, 0)) for a in arrs] + [pl.BlockSpec(w.shape, lambda i: (0, 0))],
               out_specs=pl.BlockSpec((tm, w.shape[1]), lambda i: (i, 0)), out_shape=_sds((T, w.shape[1]), BF16),
               compiler_params=_cp(("parallel",), 32 << 20))(*arrs, w)


def out_proj_dw(pieces, dy, name):
    T, N = dy.shape
    arrs, offs = [a for a, _ in pieces], [o for _, o in pieces]
    n = len(arrs)
    rows = sum(a.shape[1] for a in arrs)
    tn = 512

    def body(*refs):
        d_ref, o_ref = refs[n], refs[n + 1]
        for j in range(n):
            o_ref[offs[j]:offs[j] + arrs[j].shape[1], :] = _dg(refs[j][...], d_ref[...], _DIMS["tn"]).astype(o_ref.dtype)

    return _pc(body, name=name, grid=(N // tn,),
               in_specs=[pl.BlockSpec(a.shape, lambda j: (0, 0)) for a in arrs] + [pl.BlockSpec((T, tn), lambda j: (0, j))],
               out_specs=pl.BlockSpec((rows, tn), lambda j: (0, j)), out_shape=_sds((rows, N), BF16),
               compiler_params=_cp(("parallel",), 48 << 20))(*arrs, dy)


def in_proj_bwd(pieces, h1, w, name):
    T = h1.shape[0]
    arrs = [a for a, _ in pieces]
    offs = [o for _, o in pieces]
    wid = [a.shape[1] for a in arrs]
    n = len(arrs)
    assert sum(wid) == IN_PAD, "the pieces must tile all columns of P"
    tm = _div_tile(T, 640, 16)

    def dx_body(*refs):
        w_ref, o_ref = refs[n], refs[n + 1]
        acc = None
        for j in range(n):
            part = _dg(refs[j][...], w_ref[:, offs[j]:offs[j] + wid[j]], _DIMS["nt"])
            acc = part if acc is None else acc + part
        o_ref[...] = acc.astype(o_ref.dtype)

    dh1 = _pc(dx_body, name=name + "_dx", grid=(T // tm,),
              in_specs=[pl.BlockSpec((tm, wj), lambda i: (i, 0)) for wj in wid] + [pl.BlockSpec((D, IN_PAD), lambda i: (0, 0))],
              out_specs=pl.BlockSpec((tm, D), lambda i: (i, 0)), out_shape=_sds((T, D), BF16),
              compiler_params=_cp(("parallel",), 40 << 20))(*arrs, w)

    tmd, nk = 512, 4
    tk = T // nk

    def dw_body(h_ref, *refs):
        o_ref, acc = refs[n], refs[n + 1]
        k = pl.program_id(1)

        @pl.when(k == 0)
        def _():
            acc[...] = jnp.zeros_like(acc)

        for j in range(n):
            acc[:, offs[j]:offs[j] + wid[j]] += _dg(h_ref[...], refs[j][...], _DIMS["tn"])

        @pl.when(k == nk - 1)
        def _():
            o_ref[...] = acc[...].astype(o_ref.dtype)

    dw = _pc(dw_body, name=name + "_dw", grid=(D // tmd, nk),
             in_specs=[pl.BlockSpec((tk, tmd), lambda i, k: (k, i))] + [pl.BlockSpec((tk, wj), lambda i, k: (k, 0)) for wj in wid],
             out_specs=pl.BlockSpec((tmd, IN_PAD), lambda i, k: (i, 0)), out_shape=_sds((D, IN_PAD), BF16),
             scratch_shapes=[pltpu.VMEM((tmd, IN_PAD), F32)], compiler_params=_cp(("parallel", "arbitrary"), 48 << 20))(h1, *arrs)
    return dh1, dw


def _norm_mod(xo, shift, scale, g):
    r = lax.rsqrt(jnp.mean(xo * xo, axis=-1, keepdims=True) + EPS)
    return (xo * r) * g * (1.0 + scale) + shift


def res_norm_mod(x, y, gsv, g, nL, name):
    T = x.shape[0]
    has_y = y is not None

    def body(*refs):
        if has_y:
            x_ref, y_ref, gsv_ref, g_ref, xo_ref, h_ref = refs
            xo = x_ref[...] + gsv_ref[0, 0:1, :] * y_ref[...]
            xo_ref[...] = xo
        else:
            x_ref, gsv_ref, g_ref, h_ref = refs
            xo = x_ref[...]
        h_ref[...] = _norm_mod(xo, gsv_ref[0, 1:2, :], gsv_ref[0, 2:3, :], g_ref[...]).astype(h_ref.dtype)

    row = pl.BlockSpec((TR, D), lambda i: (i, 0))
    in_specs = [row] + ([row] if has_y else []) + [pl.BlockSpec((1, 8, D), lambda i: (i // nL, 0, 0)),
                                                     pl.BlockSpec((1, D), lambda i: (0, 0))]
    out_specs = ([row] if has_y else []) + [row]
    out_shape = ([_sds((T, D), F32)] if has_y else []) + [_sds((T, D), BF16)]
    args = (x, y, gsv, g) if has_y else (x, gsv, g)
    outs = _pc(body, name=name, grid=(T // TR,), in_specs=in_specs, out_specs=out_specs, out_shape=out_shape,
               compiler_params=_cp(("arbitrary",), 24 << 20))(*args)
    return (outs[0], outs[1]) if has_y else (None, outs[0])


def res_norm_mod_bwd(xo, y, gsv, g, dh, dres, nL, name):
    T = xo.shape[0]
    has_y = y is not None

    def body(*refs):
        if has_y:
            xo_ref, y_ref, gsv_ref, g_ref, dh_ref, dres_ref, dx_ref, dy_ref, dgsv_ref, dg_ref = refs
        else:
            xo_ref, gsv_ref, g_ref, dh_ref, dres_ref, dx_ref, dgsv_ref, dg_ref = refs
        i = pl.program_id(0)

        @pl.when((i == 0) | (i == nL))
        def _():
            dgsv_ref[...] = jnp.zeros_like(dgsv_ref)

        @pl.when(i == 0)
        def _():
            dg_ref[...] = jnp.zeros_like(dg_ref)

        _, vjp = jax.vjp(_norm_mod, xo_ref[...], gsv_ref[0, 1:2, :], gsv_ref[0, 2:3, :], g_ref[...])
        dxn, dshift, dscale, dg = vjp(dh_ref[...].astype(F32))
        dxo = dres_ref[...] + dxn
        dx_ref[...] = dxo
        if has_y:
            dy_ref[...] = (gsv_ref[0, 0:1, :] * dxo).astype(dy_ref.dtype)
            dgsv_ref[0, 0:1, :] += jnp.sum(y_ref[...] * dxo, axis=0, keepdims=True)
        dgsv_ref[0, 1:2, :] += dshift
        dgsv_ref[0, 2:3, :] += dscale
        dg_ref[0:1, :] += dg

    row = pl.BlockSpec((TR, D), lambda i: (i, 0))
    gspec = pl.BlockSpec((1, 8, D), lambda i: (i // nL, 0, 0))
    in_specs = [row] + ([row] if has_y else []) + [gspec, pl.BlockSpec((1, D), lambda i: (0, 0)), row, row]
    out_specs = [row] + ([row] if has_y else []) + [gspec, pl.BlockSpec((8, D), lambda i: (0, 0))]
    out_shape = [_sds((T, D), F32)] + ([_sds((T, D), BF16)] if has_y else []) + [_sds((2, 8, D), F32), _sds((8, D), F32)]
    args = (xo, y, gsv, g, dh, dres) if has_y else (xo, gsv, g, dh, dres)
    outs = _pc(body, name=name, grid=(T // TR,), in_specs=in_specs, out_specs=out_specs, out_shape=out_shape,
               compiler_params=_cp(("arbitrary",), 32 << 20))(*args)
    if has_y:
        return outs
    return outs[0], None, outs[1], outs[2]


def final_loss(x, y, gsv, g, target, nL, name):
    T = x.shape[0]

    def lossf(xo, gv, t):
        yn = (xo * lax.rsqrt(jnp.mean(xo * xo, axis=-1, keepdims=True) + EPS)) * gv
        e = yn - t
        return 0.5 * jnp.sum(jnp.sum(e * e, axis=-1, keepdims=True) * (1.0 / D), axis=0, keepdims=True)

    def body(x_ref, y_ref, gsv_ref, g_ref, t_ref, loss_ref, dx_ref, dy_ref, dgsv_ref, dg_ref):
        i = pl.program_id(0)

        @pl.when(i == 0)
        def _():
            loss_ref[...] = jnp.zeros_like(loss_ref)
            dg_ref[...] = jnp.zeros_like(dg_ref)

        @pl.when((i == 0) | (i == nL))
        def _():
            dgsv_ref[...] = jnp.zeros_like(dgsv_ref)

        @pl.when(i < nL)
        def _():
            gate = gsv_ref[0, 0:1, :]
            yv = y_ref[...]
            xo = x_ref[...] + gate * yv
            lv, vjp = jax.vjp(lossf, xo, g_ref[...], t_ref[...])
            dxo, dg, _ = vjp(jnp.ones((1, 1), F32))
            loss_ref[...] += jnp.broadcast_to(lv, loss_ref.shape)
            dx_ref[...] = dxo
            dy_ref[...] = (gate * dxo).astype(dy_ref.dtype)
            dgsv_ref[0, 0:1, :] += jnp.sum(yv * dxo, axis=0, keepdims=True)
            dg_ref[0:1, :] += dg

        @pl.when(i >= nL)
        def _():
            dx_ref[...] = jnp.zeros_like(dx_ref)
            dy_ref[...] = jnp.zeros_like(dy_ref)

    row = pl.BlockSpec((TR, D), lambda i: (i, 0))
    gspec = pl.BlockSpec((1, 8, D), lambda i: (i // nL, 0, 0))
    return _pc(
        body, name=name, grid=(T // TR,),
        in_specs=[row, row, gspec, pl.BlockSpec((1, D), lambda i: (0, 0)),
                  pl.BlockSpec((TR, D), lambda i: (jnp.minimum(i, nL - 1), 0))],
        out_specs=[pl.BlockSpec((8, 128), lambda i: (0, 0)), row, row, gspec, pl.BlockSpec((8, D), lambda i: (0, 0))],
        out_shape=[_sds((8, 128), F32), _sds((T, D), F32), _sds((T, D), BF16), _sds((2, 8, D), F32), _sds((8, D), F32)],
        compiler_params=_cp(("arbitrary",), 32 << 20),
    )(x, y, gsv, g, target)


FI_BLK = 2 * D_FF // 4


def _fi_chip(j):
    return (j % 2) * 2 + j // 2


def matmul_fi(a, b, mode, out_dtype, name):
    T = a.shape[0]
    if mode == "tn":
        tmd = 512

        def body(a_ref, b_ref, o_ref):
            o_ref[0] = _dg(a_ref[...], b_ref[...], _DIMS["tn"]).astype(o_ref.dtype)

        return _pc(body, name=name, grid=(4, D // tmd),
                   in_specs=[pl.BlockSpec((T, tmd), lambda j, i: (0, i)), pl.BlockSpec((T, FI_BLK), lambda j, i: (0, j))],
                   out_specs=pl.BlockSpec((1, tmd, FI_BLK), lambda j, i: (_fi_chip(j), i, 0)),
                   out_shape=_sds((4, D, FI_BLK), out_dtype), compiler_params=_cp(("parallel", "arbitrary"), 48 << 20))(a, b)
    assert mode == "nt"
    tm = _div_tile(T, 640, 16)

    def body(a_ref, b_ref, o_ref):
        acc = None
        for k in range(4):
            part = _dg(a_ref[:, k * FI_BLK:(k + 1) * FI_BLK], b_ref[_fi_chip(k)], _DIMS["nt"])
            acc = part if acc is None else acc + part
        o_ref[...] = acc.astype(o_ref.dtype)

    return _pc(body, name=name, grid=(T // tm,),
               in_specs=[pl.BlockSpec((tm, 4 * FI_BLK), lambda i: (i, 0)), pl.BlockSpec((4, D, FI_BLK), lambda i: (0, 0, 0))],
               out_specs=pl.BlockSpec((tm, D), lambda i: (i, 0)), out_shape=_sds((T, D), out_dtype),
               compiler_params=_cp(("parallel",), VMEM_CAP))(a, b)


def _swiglu(gate, up):
    return _silu(gate) * up


def ffn_in_swiglu(a, w, name):
    T = a.shape[0]
    tm = _div_tile(T, 640, 32)
    half = tm // 2

    def body(a_ref, wg_ref, wu_ref, gu_ref, act_ref):
        for rows in (slice(0, half), slice(half, tm)):
            g = _dg(a_ref[rows, :], wg_ref[0], _DIMS["nn"]).astype(BF16)
            u = _dg(a_ref[rows, :], wu_ref[0], _DIMS["nn"]).astype(BF16)
            gu_ref[rows, :FI_BLK] = g
            gu_ref[rows, FI_BLK:] = u
            act_ref[rows, :] = _swiglu(g.astype(F32), u.astype(F32)).astype(BF16)

    wspec = lambda r: pl.BlockSpec((1, D, FI_BLK), lambda j, i: (_fi_chip(2 * j + r), 0, 0))
    return _pc(body, name=name, grid=(2, T // tm),
               in_specs=[pl.BlockSpec((tm, D), lambda j, i: (i, 0)), wspec(0), wspec(1)],
               out_specs=[pl.BlockSpec((tm, 2 * FI_BLK), lambda j, i: (i, j)), pl.BlockSpec((tm, FI_BLK), lambda j, i: (i, j))],
               out_shape=[_sds((T, 4 * FI_BLK), BF16), _sds((T, D_FF), BF16)],
               compiler_params=_cp(("parallel", "arbitrary"), 48 << 20))(a, w, w)


def ffn_out_dx_swiglu(d, w, gu, name):
    T = d.shape[0]
    tm = _div_tile(T, 320, 16)

    def body(d_ref, w_ref, gu_ref, o_ref):
        for j in range(2):
            dact = _dg(d_ref[...], w_ref[j * FI_BLK:(j + 1) * FI_BLK, :], _DIMS["nt"]).astype(BF16).astype(F32)
            gs, us = slice(2 * j * FI_BLK, (2 * j + 1) * FI_BLK), slice((2 * j + 1) * FI_BLK, (2 * j + 2) * FI_BLK)
            g, u = gu_ref[:, gs].astype(F32), gu_ref[:, us].astype(F32)
            sg = 1.0 / (1.0 + jnp.exp(-g))
            sl = g * sg
            o_ref[:, gs] = (dact * u * (sg + sl * (1.0 - sg))).astype(o_ref.dtype)
            o_ref[:, us] = (dact * sl).astype(o_ref.dtype)

    return _pc(body, name=name, grid=(T // tm,),
               in_specs=[pl.BlockSpec((tm, D), lambda i: (i, 0)), pl.BlockSpec((D_FF, D), lambda i: (0, 0)), pl.BlockSpec((tm, 4 * FI_BLK), lambda i: (i, 0))],
               out_specs=pl.BlockSpec((tm, 4 * FI_BLK), lambda i: (i, 0)), out_shape=_sds((T, 4 * FI_BLK), BF16),
               compiler_params=_cp(("parallel",), 48 << 20))(d, w, gu)


def rope_tables(L, Lc):
    t = np.arange(L)
    rows, cols = t // GRID_W, t % GRID_W
    inv = ROPE_BASE ** (-np.arange(16, dtype=np.float32) / 16)
    lane = np.arange(64)
    pos = np.where((lane // 32)[None, :] == 0, rows[:, None], cols[:, None]).astype(np.float32)
    ang = jnp.asarray(pos) * jnp.asarray(inv[lane % 16])[None, :]
    cos = jnp.concatenate([jnp.cos(ang), jnp.ones((Lc, 64), F32)], axis=0)
    sin = jnp.concatenate([jnp.sin(ang), jnp.zeros((Lc, 64), F32)], axis=0)
    return jnp.tile(cos, (1, 2)), jnp.tile(sin, (1, 2))


def rope_apply(q_src, q_col, k_src, k_col, cos, sin, transpose, name, kv_src=None):
    T = cos.shape[0]
    with_kv = kv_src is not None
    tr = _div_tile(T, 640, 16)

    def rot(x, c, s):
        first = (lax.broadcasted_iota(jnp.int32, x.shape, 1) % 32) < 16
        if transpose:
            y = x * s
            return x * c + jnp.where(first, pltpu.roll(y, 112, 1), -pltpu.roll(y, 16, 1))
        return x * c + jnp.where(first, -pltpu.roll(x, 112, 1), pltpu.roll(x, 16, 1)) * s

    def body(q_ref, k_ref, c_ref, s_ref, *rest):
        qo_ref, ko_ref = rest[-4:-2] if with_kv else rest
        c, s = c_ref[...], s_ref[...]
        for j in range(2):
            qo_ref[:, j * 128:(j + 1) * 128] = rot(q_ref[:, j * 128:(j + 1) * 128].astype(F32), c, s).astype(qo_ref.dtype)
        ko_ref[...] = rot(k_ref[...].astype(F32), c, s).astype(ko_ref.dtype)
        if with_kv:
            rest[-2][...] = rest[0][...].astype(BF16)
            rest[-1][...] = rest[1][...].astype(BF16)

    tab = pl.BlockSpec((tr, 128), lambda i: (i, 0))
    wide = pl.BlockSpec((tr, 256), lambda i: (i, 0))
    kv_in = [pl.BlockSpec((tr, 256), lambda i: (i, C_KB // 256)), pl.BlockSpec((tr, 256), lambda i: (i, C_VB // 256))] if with_kv else []
    return _pc(body, name=name, grid=(T // tr,),
               in_specs=[pl.BlockSpec((tr, 256), lambda i: (i, q_col)), pl.BlockSpec((tr, 128), lambda i: (i, k_col)), tab, tab] + kv_in,
               out_specs=[wide, tab] + ([wide, wide] if with_kv else []),
               out_shape=[_sds((T, 256), BF16), _sds((T, 128), BF16)] + ([_sds((T, 256), BF16)] * 2 if with_kv else []),
               compiler_params=_cp(("parallel",), 32 << 20))(q_src, k_src, cos, sin, *([kv_src, kv_src] if with_kv else []))


_SCALE = HD ** -0.5


def _attn_tile(qh, ks, vs, extra):
    ss = []
    for k, add in ks:
        s = _dg(qh, k, _DIMS["nt"]) * _SCALE
        ss.append(s if add is None else s + add)
    m = ss[0].max(axis=-1, keepdims=True)
    for s in ss[1:]:
        m = jnp.maximum(m, s.max(axis=-1, keepdims=True))
    if extra is not None:
        m = jnp.maximum(m, extra)
    ps = [jnp.exp(s - m) for s in ss]
    den = ps[0].sum(axis=-1, keepdims=True)
    for p in ps[1:]:
        den = den + p.sum(axis=-1, keepdims=True)
    if extra is not None:
        den = den + jnp.exp(extra - m)
    num = _dg(ps[0], vs[0], _DIMS["nn"])
    for p, v in zip(ps[1:], vs[1:]):
        num = num + _dg(p, v, _DIMS["nn"])
    linv = 1.0 / den
    return num * linv, m, linv


def _attn_bwd_tile(qh, ks, vs, extra, m, linv, oh, doh):
    delta = jnp.sum(doh * oh, axis=-1, keepdims=True)
    dq = None
    dks, dvs, dss = [], [], []
    for (k, add), v in zip(ks, vs):
        s = _dg(qh, k, _DIMS["nt"]) * _SCALE
        if add is not None:
            s = s + add
        p = jnp.exp(s - m) * linv
        dvs.append(_dg(p, doh, _DIMS["tn"]))
        ds = p * (_dg(doh, v, _DIMS["nt"]) - delta)
        dss.append(ds)
        dsq = ds * _SCALE
        part = _dg(dsq, k, _DIMS["nn"])
        dq = part if dq is None else dq + part
        dks.append(_dg(dsq, qh, _DIMS["tn"]))
    dextra = None
    if extra is not None:
        dextra = -(jnp.exp(extra - m) * linv * delta)
    return dq, dks, dvs, dss, dextra


def _wa_mask(n, L):
    qpos = n * WA_BLK + lax.broadcasted_iota(jnp.int32, (WA_BLK, 3 * WA_BLK), 0)
    kpos = (n - 1) * WA_BLK + lax.broadcasted_iota(jnp.int32, (WA_BLK, 3 * WA_BLK), 1)
    ok = (jnp.abs(qpos - kpos) <= WA_BLK) & (kpos >= 0) & (kpos < L)
    return jnp.where(ok, 0.0, NEG).astype(F32)


WA_BPS = 2
_WA_PAIRS = (((0, 0), (1, 3), False), ((1, 2), (0, 1), True))


def _swap_halves_lanes(a):
    return jnp.concatenate([a[:, HD:], a[:, :HD]], axis=1)


def _wa_specs(L, Lc):
    nb = L // WA_BLK
    cb = L // Lc

    def blk(j):
        return pl.BlockSpec((WA_BLK, 128), lambda s: (jnp.clip(s * WA_BPS - 1 + j, 0, nb - 1), 0))

    return nb, [blk(j) for j in range(WA_BPS + 2)] + [pl.BlockSpec((Lc, 128), lambda s: (cb, 0))]


def _wa_pair_q(q_ref, qs, lo, hi):
    a = q_ref[qs, lo[0] * 128:(lo[0] + 1) * 128]
    b = q_ref[qs, hi[0] * 128:(hi[0] + 1) * 128]
    lane = lax.broadcasted_iota(jnp.int32, a.shape, 1)
    zero = jnp.zeros_like(a)
    return jnp.concatenate([jnp.where(lane < HD, a, zero), jnp.where(lane >= HD, b, zero)], axis=0)


def _wa_pair_vec(ref, qs, lo, hi, base=0):
    return jnp.concatenate([ref[qs, base + lo[1]:base + lo[1] + 1], ref[qs, base + hi[1]:base + hi[1] + 1]], axis=0)


def _wa_pair_sink(s_ref, n, lo, hi):
    return jnp.concatenate([jnp.broadcast_to(s_ref[lo[1]:lo[1] + 1, 0:1], (n, 1)), jnp.broadcast_to(s_ref[hi[1]:hi[1] + 1, 0:1], (n, 1))], axis=0)


def win_attn_fwd(qr, kr, krs, v, vs, sink, L, Lc, name):
    T = L + Lc
    nb, specs = _wa_specs(L, Lc)
    nk = WA_BPS + 2
    QB = WA_BPS * WA_BLK
    nlat = nb // WA_BPS

    def body(q_ref, *refs):
        groups = [refs[g * (nk + 1):(g + 1) * (nk + 1)] for g in range(4)]
        s_ref, o_ref, st_ref = refs[-3], refs[-2], refs[-1]
        s = pl.program_id(0)

        def run(qs, n, ks_of, vs_of):
            outs = []
            for lo, hi, swapped in _WA_PAIRS:
                kb, vb = groups[1 if swapped else 0], groups[3 if swapped else 2]
                o2, m2, l2 = _attn_tile(_wa_pair_q(q_ref, qs, lo, hi), ks_of(kb), vs_of(vb), _wa_pair_sink(s_ref, n, lo, hi))
                outs.append(o2)
                for r, (_, h) in enumerate((lo, hi)):
                    st_ref[qs, h:h + 1] = m2[r * n:(r + 1) * n]
                    st_ref[qs, WA_HEADS + h:WA_HEADS + h + 1] = l2[r * n:(r + 1) * n]
            lane = lax.broadcasted_iota(jnp.int32, (n, 128), 1)
            o_ref[qs, 0:128] = jnp.where(lane < HD, outs[0][:n], outs[1][n:]).astype(o_ref.dtype)
            o_ref[qs, 128:256] = jnp.where(lane < HD, outs[1][:n], outs[0][n:]).astype(o_ref.dtype)

        @pl.when(s < nlat)
        def _():
            for b in range(WA_BPS):
                m1 = _wa_mask(s * WA_BPS + b, L)
                mask = jnp.concatenate([m1, m1], axis=0)
                cat = lambda g: jnp.concatenate([g[b + j][...] for j in range(3)], axis=0)
                run(slice(b * WA_BLK, (b + 1) * WA_BLK), WA_BLK,
                    lambda kb: [(cat(kb), mask), (kb[nk][...], None)], lambda vb: [cat(vb), vb[nk][...]])

        @pl.when(s >= nlat)
        def _():
            run(slice(None), QB, lambda kb: [(kb[nk][...], None)], lambda vb: [vb[nk][...]])

    qspec = pl.BlockSpec((QB, 256), lambda s: (s, 0))
    return _pc(body, name=name, grid=(T // QB,),
               in_specs=[qspec] + specs * 4 + [pl.BlockSpec((8, 128), lambda s: (0, 0))],
               out_specs=[qspec, pl.BlockSpec((QB, 8), lambda s: (s, 0))], out_shape=[_sds((T, 256), BF16), _sds((T, 8), F32)],
               compiler_params=_cp(("arbitrary",), 40 << 20))(qr, *([kr] * (nk + 1)), *([krs] * (nk + 1)), *([v] * (nk + 1)), *([vs] * (nk + 1)), sink)


def win_attn_bwd(qr, kr, krs, v, vs, sink, do_src, o, stats, L, Lc, name):
    T = L + Lc
    nb, specs = _wa_specs(L, Lc)
    nk = WA_BPS + 2
    QB = WA_BPS * WA_BLK
    nlat = nb // WA_BPS
    cx = WA_BLK + L

    def body(q_ref, *refs):
        groups = [refs[g * (nk + 1):(g + 1) * (nk + 1)] for g in range(4)]
        s_ref, do_ref, o_ref, st_ref, dq_ref, dk_ref, dks_ref, dv_ref, dvs_ref, ds_ref = refs[4 * (nk + 1):]
        s = pl.program_id(0)

        @pl.when(s == 0)
        def _():
            for r in (dk_ref, dks_ref, dv_ref, dvs_ref, ds_ref):
                r[...] = jnp.zeros_like(r)

        def run(qs, n, ks_of, vs_of, rows):
            lane = lax.broadcasted_iota(jnp.int32, (n, 128), 1)
            dqs = []
            for lo, hi, swapped in _WA_PAIRS:
                kb, vb = groups[1 if swapped else 0], groups[3 if swapped else 2]
                dka, dva = (dks_ref, dvs_ref) if swapped else (dk_ref, dv_ref)
                pair = lambda ref: jnp.concatenate([jnp.where(lane < HD, ref[qs, lo[0] * 128:(lo[0] + 1) * 128].astype(F32), 0.0),
                                                    jnp.where(lane >= HD, ref[qs, hi[0] * 128:(hi[0] + 1) * 128].astype(F32), 0.0)], axis=0)
                dq2, dks, dvs, _, dex = _attn_bwd_tile(_wa_pair_q(q_ref, qs, lo, hi), ks_of(kb), vs_of(vb), _wa_pair_sink(s_ref, n, lo, hi),
                                                       _wa_pair_vec(st_ref, qs, lo, hi), _wa_pair_vec(st_ref, qs, lo, hi, WA_HEADS), pair(o_ref), pair(do_ref))
                dqs.append(dq2)
                for r, (_, h) in enumerate((lo, hi)):
                    ds_ref[h:h + 1, :] += jnp.broadcast_to(jnp.sum(dex[r * n:(r + 1) * n], axis=0, keepdims=True), (1, 128))
                if rows is not None:
                    dka[rows, :] += dks[0]
                    dva[rows, :] += dvs[0]
                dka[cx:cx + Lc, :] += dks[-1]
                dva[cx:cx + Lc, :] += dvs[-1]
            dq_ref[qs, 0:128] = jnp.where(lane < HD, dqs[0][:n], dqs[1][n:])
            dq_ref[qs, 128:256] = jnp.where(lane < HD, dqs[1][:n], dqs[0][n:])

        @pl.when(s < nlat)
        def _():
            for b in range(WA_BPS):
                nblk = s * WA_BPS + b
                m1 = _wa_mask(nblk, L)
                mask = jnp.concatenate([m1, m1], axis=0)
                cat = lambda g: jnp.concatenate([g[b + j][...] for j in range(3)], axis=0)
                run(slice(b * WA_BLK, (b + 1) * WA_BLK), WA_BLK, lambda kb: [(cat(kb), mask), (kb[nk][...], None)],
                    lambda vb: [cat(vb), vb[nk][...]], pl.ds(pl.multiple_of(nblk * WA_BLK, WA_BLK), 3 * WA_BLK))

        @pl.when(s >= nlat)
        def _():
            run(slice(None), QB, lambda kb: [(kb[nk][...], None)], lambda vb: [vb[nk][...]], None)

    qspec = pl.BlockSpec((QB, 256), lambda s: (s, 0))
    acc_spec = pl.BlockSpec((T + 2 * WA_BLK, 128), lambda s: (0, 0))
    acc_shape = _sds((T + 2 * WA_BLK, 128), F32)
    return _pc(body, name=name, grid=(T // QB,),
               in_specs=[qspec] + specs * 4 + [pl.BlockSpec((8, 128), lambda s: (0, 0)), qspec, qspec, pl.BlockSpec((QB, 8), lambda s: (s, 0))],
               out_specs=[qspec, acc_spec, acc_spec, acc_spec, acc_spec, pl.BlockSpec((8, 128), lambda s: (0, 0))],
               out_shape=[_sds((T, 256), F32), acc_shape, acc_shape, acc_shape, acc_shape, _sds((8, 128), F32)],
               compiler_params=_cp(("arbitrary",), 48 << 20))(qr, *([kr] * (nk + 1)), *([krs] * (nk + 1)), *([v] * (nk + 1)), *([vs] * (nk + 1)),
                                                              sink, do_src, o, stats)


def na_index_tables():
    qc = np.arange(GRID_W)[:, None]
    kc = np.arange(GRID_W)[None, :]
    cstart = np.clip(qc - NA_KW // 2, 0, GRID_W - NA_KW)
    ok = (kc >= cstart) & (kc < cstart + NA_KW)
    dx = np.clip(kc - qc, -(NA_KW - 1), NA_KW - 1) + (NA_KW - 1)
    off = np.arange(NA_KH)[:, None]
    kr = np.arange(NA_KH)[None, :]
    dy = kr - off + (NA_KH - 1)
    return ok, dx, dy


def _na_selectors():
    ok, dx, dy = na_index_tables()
    e1 = np.zeros((GRID_W * GRID_W, 128), np.float32)
    qi, ki = np.nonzero(ok)
    e1[qi * GRID_W + ki, dx[qi, ki]] = 1.0
    e2 = np.zeros((16, NA_KH * NA_KH), np.float32)
    oi, ri = np.meshgrid(np.arange(NA_KH), np.arange(NA_KH), indexing="ij")
    e2[dy[oi, ri].ravel(), (oi * NA_KH + ri).ravel()] = 1.0
    return ok, jnp.asarray(e1), jnp.asarray(np.kron(np.eye(NA_HEADS, dtype=np.float32), e2))


def na_bias_table(rpb, tag):
    ok, e1, e2 = _na_selectors()
    r2 = jnp.pad(rpb.astype(F32), ((0, 0), (0, 1), (0, 128 - (2 * NA_KW - 1)))).reshape(NA_HEADS * 16, 128)
    r1 = matmul(e2, r2, "tn", F32, f"na_bias_sel1_{tag}", hi=True)
    x = matmul(r1, e1, "nt", F32, f"na_bias_sel2_{tag}", hi=True)
    b = x.reshape(NA_HEADS, NA_KH, NA_KH, GRID_W, GRID_W).transpose(0, 1, 3, 2, 4)
    b = b + jnp.asarray(np.where(ok, 0.0, NEG).astype(np.float32))[None, None, :, None, :]
    return b.reshape(NA_HEADS, NA_KH, GRID_W, NA_KH * GRID_W)


def _na_rows(r, GR):
    r0 = jnp.clip(r - NA_KH // 2, 0, GR - NA_KH)
    return r0, jnp.clip(r - r0, 0, NA_KH - 1)


NA_RPS = 4


def _pair_rows(x):
    lane = lax.broadcasted_iota(jnp.int32, x.shape, 1)
    zero = jnp.zeros_like(x)
    return jnp.concatenate([jnp.where(lane < HD, x, zero), jnp.where(lane >= HD, x, zero)], axis=0)


def _unpair_rows(x2):
    n = x2.shape[0] // 2
    lane = lax.broadcasted_iota(jnp.int32, (n, 128), 1)
    return jnp.where(lane < HD, x2[:n], x2[n:])


def na_fwd(P, kb, vb, bias, L, Lc, name):
    T = L + Lc
    GR = L // GRID_W
    W = NA_KH * GRID_W
    QB = GRID_W * NA_RPS
    nlat = GR // NA_RPS

    def body(q_ref, k_ref, v_ref, b_ref, o_ref, st_ref):
        s = pl.program_id(0)

        def put(qs, p, res):
            o2, m2, l2 = res
            n = o2.shape[0] // 2
            o_ref[qs, p * 128:(p + 1) * 128] = _unpair_rows(o2).astype(o_ref.dtype)
            for r in range(2):
                st_ref[qs, 2 * p + r:2 * p + r + 1] = m2[r * n:(r + 1) * n]
                st_ref[qs, NA_HEADS + 2 * p + r:NA_HEADS + 2 * p + r + 1] = l2[r * n:(r + 1) * n]

        @pl.when(s < nlat)
        def _():
            for rr in range(NA_RPS):
                r0, off = _na_rows(s * NA_RPS + rr, GR)
                rows = pl.ds(pl.multiple_of(r0 * GRID_W, GRID_W), W)
                qs = slice(rr * GRID_W, (rr + 1) * GRID_W)
                for p in range(NA_HEADS // 2):
                    ps = slice(p * 128, (p + 1) * 128)
                    b2 = jnp.concatenate([b_ref[2 * p, off], b_ref[2 * p + 1, off]], axis=0)
                    put(qs, p, _attn_tile(_pair_rows(q_ref[qs, ps]), [(k_ref[rows, ps], b2), (k_ref[L:T, ps], None)],
                                          [v_ref[rows, ps], v_ref[L:T, ps]], None))

        @pl.when(s >= nlat)
        def _():
            for p in range(NA_HEADS // 2):
                ps = slice(p * 128, (p + 1) * 128)
                put(slice(None), p, _attn_tile(_pair_rows(q_ref[:, ps]), [(k_ref[L:T, ps], None)], [v_ref[L:T, ps]], None))

    one = pl.Buffered(1)
    return _pc(body, name=name, grid=(T // QB,),
               in_specs=[pl.BlockSpec((QB, 256), lambda r: (r, C_QB // 256)),
                         pl.BlockSpec((T, 256), lambda r: (0, 0), pipeline_mode=one),
                         pl.BlockSpec((T, 256), lambda r: (0, 0), pipeline_mode=one),
                         pl.BlockSpec((NA_HEADS, NA_KH, GRID_W, W), lambda r: (0, 0, 0, 0), pipeline_mode=one)],
               out_specs=[pl.BlockSpec((QB, 256), lambda r: (r, 0)), pl.BlockSpec((QB, 8), lambda r: (r, 0))],
               out_shape=[_sds((T, 256), BF16), _sds((T, 8), F32)],
               compiler_params=_cp(("arbitrary",), 32 << 20))(P, kb, vb, bias)


def na_bwd(P, kb, vb, bias, do_src, o, stats, L, Lc, name):
    T = L + Lc
    GR = L // GRID_W
    W = NA_KH * GRID_W
    QB = GRID_W * NA_RPS
    nlat = GR // NA_RPS

    def body(q_ref, k_ref, v_ref, b_ref, do_ref, o_ref, st_ref, dq_ref, dk_ref, dv_ref, db_ref):
        s = pl.program_id(0)

        @pl.when(s == 0)
        def _():
            dk_ref[...] = jnp.zeros_like(dk_ref)
            dv_ref[...] = jnp.zeros_like(dv_ref)
            db_ref[...] = jnp.zeros_like(db_ref)

        def tile(qs, p, ks, vs):
            ps = slice(p * 128, (p + 1) * 128)
            m2 = jnp.concatenate([st_ref[qs, 2 * p:2 * p + 1], st_ref[qs, 2 * p + 1:2 * p + 2]], axis=0)
            l2 = jnp.concatenate([st_ref[qs, NA_HEADS + 2 * p:NA_HEADS + 2 * p + 1], st_ref[qs, NA_HEADS + 2 * p + 1:NA_HEADS + 2 * p + 2]], axis=0)
            dq2, dks, dvs, dss, _ = _attn_bwd_tile(_pair_rows(q_ref[qs, ps]), ks, vs, None, m2, l2,
                                                   _pair_rows(o_ref[qs, ps].astype(F32)), _pair_rows(do_ref[qs, ps].astype(F32)))
            dq_ref[qs, ps] = _unpair_rows(dq2).astype(dq_ref.dtype)
            return dks, dvs, dss

        @pl.when(s < nlat)
        def _():
            for rr in range(NA_RPS):
                r0, off = _na_rows(s * NA_RPS + rr, GR)
                rows = pl.ds(pl.multiple_of(r0 * GRID_W, GRID_W), W)
                qs = slice(rr * GRID_W, (rr + 1) * GRID_W)
                for p in range(NA_HEADS // 2):
                    ps = slice(p * 128, (p + 1) * 128)
                    b2 = jnp.concatenate([b_ref[2 * p, off], b_ref[2 * p + 1, off]], axis=0)
                    dks, dvs, dss = tile(qs, p, [(k_ref[rows, ps], b2), (k_ref[L:T, ps], None)], [v_ref[rows, ps], v_ref[L:T, ps]])
                    dk_ref[rows, ps] += dks[0]
                    dv_ref[rows, ps] += dvs[0]
                    dk_ref[L:T, ps] += dks[1]
                    dv_ref[L:T, ps] += dvs[1]
                    db_ref[2 * p, off] += dss[0][:GRID_W]
                    db_ref[2 * p + 1, off] += dss[0][GRID_W:]

        @pl.when(s >= nlat)
        def _():
            for p in range(NA_HEADS // 2):
                ps = slice(p * 128, (p + 1) * 128)
                dks, dvs, _ = tile(slice(None), p, [(k_ref[L:T, ps], None)], [v_ref[L:T, ps]])
                dk_ref[L:T, ps] += dks[0]
                dv_ref[L:T, ps] += dvs[0]

    one = pl.Buffered(1)
    full = lambda shape: pl.BlockSpec(shape, lambda r: (0,) * len(shape), pipeline_mode=one)
    qspec = pl.BlockSpec((QB, 256), lambda r: (r, 0))
    return _pc(body, name=name, grid=(T // QB,),
               in_specs=[pl.BlockSpec((QB, 256), lambda r: (r, C_QB // 256)), full((T, 256)), full((T, 256)),
                         full((NA_HEADS, NA_KH, GRID_W, W)), pl.BlockSpec((QB, 256), lambda r: (r, 1)), qspec, pl.BlockSpec((QB, 8), lambda r: (r, 0))],
               out_specs=[qspec, full((T, 256)), full((T, 256)), full((NA_HEADS, NA_KH, GRID_W, W))],
               out_shape=[_sds((T, 256), BF16), _sds((T, 256), F32), _sds((T, 256), F32), _sds((NA_HEADS, NA_KH, GRID_W, W), F32)],
               compiler_params=_cp(("arbitrary",), 48 << 20))(P, kb, vb, bias, do_src, o, stats)


def na_rpb_grad(dbias, tag):
    _, e1, e2 = _na_selectors()
    x = dbias.reshape(NA_HEADS, NA_KH, GRID_W, NA_KH, GRID_W).transpose(0, 1, 3, 2, 4).reshape(NA_HEADS * NA_KH * NA_KH, GRID_W * GRID_W)
    r1 = matmul(x, e1, "nn", F32, f"na_rpb_sel1_{tag}", hi=True, tk=1024)
    r2 = matmul(e2, r1, "nn", F32, f"na_rpb_sel2_{tag}", hi=True)
    return r2.reshape(NA_HEADS, 16, 128)[:, :2 * NA_KH - 1, :2 * NA_KW - 1]


_HALO = 8
CONV_CB = 4
CONV_RB = 64


def _halo_specs(T, col0):
    nh = TR // _HALO
    specs = []
    for j in range(CONV_CB):
        specs.append(pl.BlockSpec((_HALO, 256), lambda i, j=j: (jnp.maximum(i * nh - 1, 0), col0 + j)))
        specs.append(pl.BlockSpec((TR, 256), lambda i, j=j: (i, col0 + j)))
        specs.append(pl.BlockSpec((_HALO, 256), lambda i, j=j: (jnp.minimum((i + 1) * nh, T // _HALO - 1), col0 + j)))
    return specs


def _fill_ext(ext, prv, cur, nxt, i, nL, nT):
    has_prev = jnp.where((i != 0) & (i != nL), 1.0, 0.0)
    has_next = jnp.where((i != nL - 1) & (i != nT - 1), 1.0, 0.0)
    ext[0:_HALO, :] = prv[...].astype(F32) * has_prev
    ext[_HALO:_HALO + TR, :] = cur[...].astype(F32)
    ext[_HALO + TR:, :] = nxt[...].astype(F32) * has_next


def conv_silu_fwd(P, w8, b, nL, name):
    T = P.shape[0]
    nT = T // TR

    def body(*refs):
        xin, (w_ref, b_ref, pre_ref, act_ref, ext) = refs[:3 * CONV_CB], refs[3 * CONV_CB:]
        i = pl.program_id(0)
        for j in range(CONV_CB):
            cs = slice(j * 256, (j + 1) * 256)
            _fill_ext(ext, *xin[3 * j:3 * j + 3], i, nL, nT)
            for r in range(0, TR, CONV_RB):
                y = jnp.broadcast_to(b_ref[:, cs], (CONV_RB, 256))
                for k in range(S_CONV):
                    y = y + w_ref[k:k + 1, cs] * ext[pl.ds(_HALO - S_CONV // 2 + k + r, CONV_RB), :]
                pre_ref[r:r + CONV_RB, cs] = y
                act_ref[r:r + CONV_RB, cs] = _silu(y)

    out = pl.BlockSpec((TR, 1024), lambda i: (i, 0))
    return _pc(body, name=name, grid=(nT,),
               in_specs=_halo_specs(T, C_XBC // 256) + [pl.BlockSpec((8, 1024), lambda i: (0, 0)), pl.BlockSpec((1, 1024), lambda i: (0, 0))],
               out_specs=[out, out], out_shape=[_sds((T, 1024), F32), _sds((T, 1024), F32)],
               scratch_shapes=[pltpu.VMEM((TR + 2 * _HALO, 256), F32)],
               compiler_params=_cp(("parallel",), 24 << 20))(*([P] * (3 * CONV_CB)), w8, b)


def dsilu(pre, dxs_list, db_list, dc_list, name):
    T = pre.shape[0]
    n1, n2, n3 = len(dxs_list), len(db_list), len(dc_list)

    def body(*refs):
        pre_ref = refs[0]
        ins = refs[1:1 + n1 + n2 + n3]
        out = refs[-1]

        def part(rs, lo, hi):
            g = rs[0][...].astype(F32)
            for r in rs[1:]:
                g = g + r[...].astype(F32)
            x = pre_ref[:, lo:hi]
            sg = 1.0 / (1.0 + jnp.exp(-x))
            sl = x * sg
            out[:, lo:hi] = g * (sg + sl * (1.0 - sg))

        part(ins[:n1], 0, 512)
        part(ins[n1:n1 + n2], 512, 768)
        part(ins[n1 + n2:], 768, 1024)

    spec = lambda w: pl.BlockSpec((TR, w), lambda i: (i, 0))
    return _pc(body, name=name, grid=(T // TR,),
               in_specs=[spec(1024)] + [spec(512)] * n1 + [spec(256)] * (n2 + n3),
               out_specs=spec(1024), out_shape=_sds((T, 1024), F32),
               compiler_params=_cp(("parallel",), 32 << 20))(pre, *dxs_list, *db_list, *dc_list)


def conv_bwd(dpre, P, w8, nL, name):
    T = P.shape[0]
    nT = T // TR

    def body(*refs):
        din, xin, (w_ref, dx_ref, dw_ref, db_ref, extd) = refs[:3 * CONV_CB], refs[3 * CONV_CB:4 * CONV_CB], refs[4 * CONV_CB:]
        i = pl.program_id(0)

        @pl.when(i == 0)
        def _():
            dw_ref[...] = jnp.zeros_like(dw_ref)
            db_ref[...] = jnp.zeros_like(db_ref)

        fold = lambda a: functools.reduce(lambda p, q: p + q, [a[q:q + 8] for q in range(0, CONV_RB, 8)])
        for j in range(CONV_CB):
            cs = slice(j * 256, (j + 1) * 256)
            _fill_ext(extd, *din[3 * j:3 * j + 3], i, nL, nT)
            dws = [jnp.zeros((8, 256), F32) for _ in range(S_CONV)]
            dbs = jnp.zeros((8, 256), F32)
            for r in range(0, TR, CONV_RB):
                x = xin[j][r:r + CONV_RB, :]
                dx = jnp.zeros((CONV_RB, 256), F32)
                for k in range(S_CONV):
                    sd = extd[pl.ds(_HALO + S_CONV // 2 - k + r, CONV_RB), :]
                    dx = dx + w_ref[k:k + 1, cs] * sd
                    dws[k] = dws[k] + fold(sd * x)
                dx_ref[r:r + CONV_RB, cs] = dx.astype(dx_ref.dtype)
                dbs = dbs + fold(din[3 * j + 1][r:r + CONV_RB, :])
            for k in range(S_CONV):
                dw_ref[k:k + 1, cs] += jnp.sum(dws[k], axis=0, keepdims=True)
            db_ref[0:1, cs] += jnp.sum(dbs, axis=0, keepdims=True)

    acc = pl.BlockSpec((8, 1024), lambda i: (0, 0))
    xspecs = [pl.BlockSpec((TR, 256), lambda i, j=j: (i, C_XBC // 256 + j)) for j in range(CONV_CB)]
    return _pc(body, name=name, grid=(nT,),
               in_specs=_halo_specs(T, 0) + xspecs + [acc],
               out_specs=[pl.BlockSpec((TR, 1024), lambda i: (i, 0)), acc, acc],
               out_shape=[_sds((T, 1024), BF16), _sds((8, 1024), F32), _sds((8, 1024), F32)],
               scratch_shapes=[pltpu.VMEM((TR + 2 * _HALO, 256), F32)],
               compiler_params=_cp(("arbitrary",), 24 << 20))(*([dpre] * (3 * CONV_CB)), *([P] * CONV_CB), w8)


def _onehot_row(h, n):
    return (lax.broadcasted_iota(jnp.int32, (1, n), 1) == h).astype(F32)


def _onehot_col(h, n):
    return (lax.broadcasted_iota(jnp.int32, (n, 1), 0) == h).astype(F32)


S_PAIRS = S_HEADS // 2


def _ssd_chunk(xs, dtr, dtb, alog, bm, cm, hin, reverse):
    Qn = S_Q
    ii = lax.broadcasted_iota(jnp.int32, (Qn, Qn), 0)
    jj = lax.broadcasted_iota(jnp.int32, (Qn, Qn), 1)
    keep = (ii <= jj) if reverse else (ii >= jj)
    tri = keep.astype(F32)
    triT = ((jj <= ii) if reverse else (jj >= ii)).astype(F32)
    eye = (ii == jj).astype(F32)
    low = jj < S_P
    top = ii < S_P
    dt = _softplus(dtr + dtb)
    a = dt * (-jnp.exp(alog))
    cs = hdot(tri, a)
    csT = hdot(a, triT, "tn")
    dtT = hdot(dt, eye, "tn")
    last = _onehot_row(0 if reverse else Qn - 1, Qn)
    ys, houts = [], []
    for p in range(S_PAIRS):
        g = p // (S_PAIRS // S_GROUPS)
        if p % (S_PAIRS // S_GROUPS) == 0:
            G = bdot(cm[g], bm[g], "nt")
        per_head = []
        for h in (2 * p, 2 * p + 1):
            eh_r, eh_c = _onehot_row(h, S_HEADS), _onehot_col(h, S_HEADS)
            cs_c = jnp.sum(cs * eh_r, axis=1, keepdims=True)
            dt_c = jnp.sum(dt * eh_r, axis=1, keepdims=True)
            cs_r = jnp.sum(csT * eh_c, axis=0, keepdims=True)
            dt_r = jnp.sum(dtT * eh_c, axis=0, keepdims=True)
            tot = jnp.sum(cs_r * last, axis=1, keepdims=True)
            w = G * jnp.exp(jnp.where(keep, cs_c - cs_r, NEG)) * dt_r
            per_head.append((bdot(w, xs[p], "nn"), jnp.exp(cs_c), jnp.exp(tot - cs_c) * dt_c, jnp.exp(tot)))
        (y0, e0, f0, d0), (y1, e1, f1, d1) = per_head
        y = jnp.where(low, y0, y1) + bdot(cm[g], hin[p], "nt") * jnp.where(low, e0, e1)
        hout = hin[p] * jnp.where(top, d0, d1) + bdot(xs[p] * jnp.where(low, f0, f1), bm[g], "tn")
        ys.append(y)
        houts.append(hout)
    return ys, houts


def _ssd_orders(L, Lc):
    nl, ncx = L // S_Q, Lc // S_Q
    fwd = lambda s: jnp.where(s < ncx, nl + s, s - ncx)
    bwd = lambda s: nl + ncx - 1 - s
    return nl + ncx, fwd, bwd


def _ssd_in_specs(fo, bo, step):
    def at(order, w, col):
        return pl.BlockSpec((S_Q, w), lambda u: (order(step(u)), col))
    specs = []
    for order in (fo, bo):
        specs += [at(order, 512, 0), at(order, 256, 2), at(order, 256, 3), at(order, 128, C_DT // 128)]
    return specs


def ssd_fwd(act, P, dtb, alog, L, Lc, name):
    T = L + Lc
    ns, fo, bo = _ssd_orders(L, Lc)

    def body(xf, bf, cf, df, xb, bb, cb, db, dtb_ref, al_ref, yf, yb, hsf, hsb, Hf, Hb):
        s = pl.program_id(0)

        @pl.when(s == 0)
        def _():
            Hf[...] = jnp.zeros_like(Hf)
            Hb[...] = jnp.zeros_like(Hb)

        for d, (x_r, b_r, c_r, dt_r, y_r, hs_r, H) in enumerate(((xf, bf, cf, df, yf, hsf, Hf), (xb, bb, cb, db, yb, hsb, Hb))):
            hin = [H[p] for p in range(S_PAIRS)]
            hs_r[0] = H[...]
            ys, houts = _ssd_chunk(
                [x_r[:, p * 128:(p + 1) * 128] for p in range(S_PAIRS)], dt_r[:, d * 8:(d + 1) * 8],
                dtb_ref[d:d + 1, 0:8], al_ref[d:d + 1, 0:8],
                [b_r[:, g * S_N:(g + 1) * S_N] for g in range(S_GROUPS)], [c_r[:, g * S_N:(g + 1) * S_N] for g in range(S_GROUPS)],
                hin, reverse=(d == 1))
            for p in range(S_PAIRS):
                y_r[:, p * 128:(p + 1) * 128] = ys[p]
                H[p] = houts[p]

    ident = lambda u: u
    small = pl.BlockSpec((8, 128), lambda u: (0, 0))
    hspec = pl.BlockSpec((1, S_PAIRS, 2 * S_P, S_N), lambda u: (u, 0, 0, 0))
    return _pc(body, name=name, grid=(ns,),
               in_specs=_ssd_in_specs(fo, bo, ident) + [small, small],
               out_specs=[pl.BlockSpec((S_Q, 512), lambda u: (fo(u), 0)), pl.BlockSpec((S_Q, 512), lambda u: (bo(u), 0)), hspec, hspec],
               out_shape=[_sds((T, 512), F32), _sds((T, 512), F32), _sds((ns, S_PAIRS, 2 * S_P, S_N), F32), _sds((ns, S_PAIRS, 2 * S_P, S_N), F32)],
               scratch_shapes=[pltpu.VMEM((S_PAIRS, 2 * S_P, S_N), F32), pltpu.VMEM((S_PAIRS, 2 * S_P, S_N), F32)],
               compiler_params=_cp(("arbitrary",), 32 << 20))(act, act, act, P, act, act, act, P, dtb, alog)


def ssd_bwd(act, P, dtb, alog, hsf, hsb, dy, L, Lc, name):
    T = L + Lc
    ns, fo, bo = _ssd_orders(L, Lc)
    step = lambda u: ns - 1 - u

    def body(xf, bf, cf, df, xb, bb, cb, db, dtb_ref, al_ref, hsf_r, hsb_r, dyf, dyb,
             dxf, dbf, dcf, ddf, dxb, dbb, dcb, ddb, ddtb, dal, dHf, dHb):
        u = pl.program_id(0)

        @pl.when(u == 0)
        def _():
            dHf[...] = jnp.zeros_like(dHf)
            dHb[...] = jnp.zeros_like(dHb)
            ddtb[...] = jnp.zeros_like(ddtb)
            dal[...] = jnp.zeros_like(dal)

        dirs = ((xf, bf, cf, df, hsf_r, dyf, dxf, dbf, dcf, ddf, dHf), (xb, bb, cb, db, hsb_r, dyb, dxb, dbb, dcb, ddb, dHb))
        for d, (x_r, b_r, c_r, dt_r, hs_r, dy_r, dx_o, db_o, dc_o, dd_o, dH) in enumerate(dirs):
            f = functools.partial(_ssd_chunk, reverse=(d == 1))
            _, vjp = jax.vjp(
                f, [x_r[:, p * 128:(p + 1) * 128] for p in range(S_PAIRS)], dt_r[:, d * 8:(d + 1) * 8],
                dtb_ref[d:d + 1, 0:8], al_ref[d:d + 1, 0:8],
                [b_r[:, g * S_N:(g + 1) * S_N] for g in range(S_GROUPS)], [c_r[:, g * S_N:(g + 1) * S_N] for g in range(S_GROUPS)],
                [hs_r[0, p] for p in range(S_PAIRS)])
            gx, gdt, gdtb, gal, gb, gc, gh = vjp(([dy_r[:, p * 128:(p + 1) * 128] for p in range(S_PAIRS)],
                                                  [dH[p] for p in range(S_PAIRS)]))
            for p in range(S_PAIRS):
                dx_o[:, p * 128:(p + 1) * 128] = gx[p]
                dH[p] = gh[p]
            for g in range(S_GROUPS):
                db_o[:, g * S_N:(g + 1) * S_N] = gb[g]
                dc_o[:, g * S_N:(g + 1) * S_N] = gc[g]
            dd_o[...] = gdt
            ddtb[d:d + 1, 0:8] += gdtb
            dal[d:d + 1, 0:8] += gal

    small = pl.BlockSpec((8, 128), lambda u: (0, 0))
    hspec = pl.BlockSpec((1, S_PAIRS, 2 * S_P, S_N), lambda u: (step(u), 0, 0, 0))
    at = lambda order, w: pl.BlockSpec((S_Q, w), lambda u: (order(step(u)), 0))
    outs = []
    for order in (fo, bo):
        outs += [at(order, 512), at(order, 256), at(order, 256), at(order, 8)]
    oshape = [_sds((T, 512), F32), _sds((T, 256), F32), _sds((T, 256), F32), _sds((T, 8), F32)]
    return _pc(body, name=name, grid=(ns,),
               in_specs=_ssd_in_specs(fo, bo, step) + [small, small, hspec, hspec, at(fo, 512), at(bo, 512)],
               out_specs=outs + [small, small], out_shape=oshape + oshape + [_sds((8, 128), F32), _sds((8, 128), F32)],
               scratch_shapes=[pltpu.VMEM((S_PAIRS, 2 * S_P, S_N), F32), pltpu.VMEM((S_PAIRS, 2 * S_P, S_N), F32)],
               compiler_params=_cp(("arbitrary",), 40 << 20))(act, act, act, P, act, act, act, P, dtb, alog, hsf, hsb, dy, dy)


def _ssm_out(yf, yb, xs, z, dskip, g):
    y = (yf + yb + dskip * xs) * _silu(z)
    return (y * lax.rsqrt(jnp.mean(y * y, axis=-1, keepdims=True) + EPS)) * g


def ssm_out_fwd(yf, yb, act, P, dskip, g, name):
    T = yf.shape[0]

    def body(yf_r, yb_r, xs_r, z_r, d_r, g_r, o_r):
        o_r[...] = _ssm_out(yf_r[...], yb_r[...], xs_r[...], z_r[...], d_r[...], g_r[...]).astype(o_r.dtype)

    row = pl.BlockSpec((TR, 512), lambda i: (i, 0))
    vec = pl.BlockSpec((1, 512), lambda i: (0, 0))
    return _pc(body, name=name, grid=(T // TR,),
               in_specs=[row, row, row, pl.BlockSpec((TR, 512), lambda i: (i, C_Z // 512)), vec, vec],
               out_specs=row, out_shape=_sds((T, 512), BF16),
               compiler_params=_cp(("parallel",), 16 << 20))(yf, yb, act, P, dskip, g)


def ssm_out_bwd(yf, yb, act, P, dskip, g, do_src, name):
    T = yf.shape[0]

    def body(yf_r, yb_r, xs_r, z_r, d_r, g_r, do_r, dy_r, dxs_r, dz_r, dv_r):
        @pl.when(pl.program_id(0) == 0)
        def _():
            dv_r[...] = jnp.zeros_like(dv_r)

        _, vjp = jax.vjp(_ssm_out, yf_r[...], yb_r[...], xs_r[...], z_r[...], d_r[...], g_r[...])
        dyf, _, dxs, dz, dd, dg = vjp(do_r[...].astype(F32))
        dy_r[...] = dyf
        dxs_r[...] = dxs
        dz_r[...] = dz.astype(dz_r.dtype)
        dv_r[0:1, :] += dd
        dv_r[1:2, :] += dg

    row = pl.BlockSpec((TR, 512), lambda i: (i, 0))
    vec = pl.BlockSpec((1, 512), lambda i: (0, 0))
    return _pc(body, name=name, grid=(T // TR,),
               in_specs=[row, row, row, pl.BlockSpec((TR, 512), lambda i: (i, C_Z // 512)), vec, vec,
                         pl.BlockSpec((TR, 512), lambda i: (i, 1))],
               out_specs=[row, row, row, pl.BlockSpec((8, 512), lambda i: (0, 0))],
               out_shape=[_sds((T, 512), F32), _sds((T, 512), F32), _sds((T, 512), BF16), _sds((8, 512), F32)],
               compiler_params=_cp(("arbitrary",), 24 << 20))(yf, yb, act, P, dskip, g, do_src)


def add_halves(xv, got, cvec, name):
    n, r, cdim = xv.shape
    h = r // 2

    def body(c_ref, x_ref, g_ref, o_ref):
        o_ref[...] = (x_ref[...].astype(F32) + g_ref[...].astype(F32)).astype(o_ref.dtype)

    gs = pltpu.PrefetchScalarGridSpec(
        num_scalar_prefetch=1, grid=(n,),
        in_specs=[pl.BlockSpec((1, h, cdim), lambda k, c_ref: (k, c_ref[0], 0)), pl.BlockSpec((1, h, cdim), lambda k, c_ref: (k, 0, 0))],
        out_specs=pl.BlockSpec((1, h, cdim), lambda k, c_ref: (k, 0, 0)))
    return _pc(body, name=name, grid_spec=gs, out_shape=_sds((n, h, cdim), BF16),
               compiler_params=_cp(("arbitrary",), 24 << 20))(cvec, xv, got)


def sum_slots(a, name):
    n, r, cdim = a.shape
    tr = _div_tile(r, 512, 16)

    def body(a_ref, o_ref):
        acc = a_ref[0].astype(F32)
        for k in range(1, n):
            acc = acc + a_ref[k].astype(F32)
        o_ref[...] = acc

    return _pc(body, name=name, grid=(r // tr,), in_specs=[pl.BlockSpec((n, tr, cdim), lambda i: (0, i, 0))],
               out_specs=pl.BlockSpec((tr, cdim), lambda i: (i, 0)), out_shape=_sds((r, cdim), F32),
               compiler_params=_cp(("parallel",), 32 << 20))(a)


def adamw(w, g, m, v, name):
    B, R, C = w.shape
    tr = _div_tile(R, max(8, (1 << 19) // max(C, 1) // 8 * 8), 8) if R % 8 == 0 else R
    c1 = 1.0 / (1.0 - ADAM_B1 ** ADAM_STEP)
    c2 = 1.0 / (1.0 - ADAM_B2 ** ADAM_STEP)

    def body(w_ref, g_ref, m_ref, v_ref, d_ref, mo_ref, vo_ref):
        gg = g_ref[...]
        mn = ADAM_B1 * m_ref[...] + (1.0 - ADAM_B1) * gg
        vn = ADAM_B2 * v_ref[...] + (1.0 - ADAM_B2) * (gg * gg)
        d_ref[...] = -ADAM_LR * ((mn * c1) / (jnp.sqrt(vn * c2) + ADAM_EPS) + ADAM_WD * w_ref[...])
        mo_ref[...] = mn
        vo_ref[...] = vn

    spec = pl.BlockSpec((1, tr, C), lambda b, i: (b, i, 0))
    return _pc(body, name=name, grid=(B, R // tr), in_specs=[spec] * 4, out_specs=[spec] * 3,
               out_shape=[_sds((B, R, C), F32)] * 3, compiler_params=_cp(("parallel", "parallel"), 32 << 20))(w, g, m, v)


def _me():
    return lax.axis_index("x"), lax.axis_index("y"), lax.axis_index("c")


def _flip(v, bit):
    return 1 - v if bit else v


def allgather8(xv, name):
    R = xv.shape[0]

    def body(x_ref, out_ref, sum_ref, send_sems, recv_sems):
        mx, my, mc = _me()
        me = 4 * mx + 2 * my + mc
        out_ref[me] = x_ref[...]
        sends, recvs = [], []
        for k in range(1, 8):
            px, py, pc = _flip(mx, k & 4), _flip(my, k & 2), _flip(mc, k & 1)
            peer = 4 * px + 2 * py + pc
            sends.append(pltpu.make_async_remote_copy(src_ref=x_ref, dst_ref=out_ref.at[me], send_sem=send_sems.at[k - 1],
                                                      recv_sem=recv_sems.at[k - 1], device_id=(px, py, pc), device_id_type=MESH))
            recvs.append(pltpu.make_async_remote_copy(src_ref=x_ref, dst_ref=out_ref.at[peer], send_sem=send_sems.at[k - 1],
                                                      recv_sem=recv_sems.at[k - 1], device_id=(px, py, pc), device_id_type=MESH))
        for cp in sends:
            cp.start()
        for cp in recvs:
            cp.wait_recv()
        for cp in sends:
            cp.wait_send()
        acc = out_ref[0]
        for d in range(1, 8):
            acc = acc + out_ref[d]
        sum_ref[...] = acc

    vm = pl.BlockSpec(memory_space=pltpu.VMEM)
    return _pc(body, name=name, pin=False, in_specs=[vm], out_specs=[vm, vm], out_shape=[_sds((8, R, 128), F32), _sds((R, 128), F32)],
               scratch_shapes=[pltpu.SemaphoreType.DMA((7,)), pltpu.SemaphoreType.DMA((7,))],
               compiler_params=_cp(None, 32 << 20))(xv)


def _other_chips(mx, my):
    return [(1 - mx, my), (mx, 1 - my), (1 - mx, 1 - my)]


def _halves(r, mc, mult):
    h = r // 2
    return pl.ds(pl.multiple_of(mc * h, mult), h), pl.ds(pl.multiple_of((1 - mc) * h, mult), h)


def _rcopy(src, dst, send_sems, recv_sems, k, to):
    return pltpu.make_async_remote_copy(src_ref=src, dst_ref=dst, send_sem=send_sems.at[k], recv_sem=recv_sems.at[k],
                                        device_id=to, device_id_type=MESH)


def _gather_body(xs, outs, send_sems, recv_sems):
    n = len(xs)
    mx, my, mc = _me()
    chip = 2 * mx + my
    sib = (mx, my, 1 - mc)
    chips = _other_chips(mx, my)
    idx = [2 * cx + cy for cx, cy in chips]
    cp = functools.partial(_rcopy, send_sems=send_sems, recv_sems=recv_sems)
    hv = [_halves(x.shape[0], mc, 16) for x in xs]
    first, passed = [], []
    for a in range(n):
        for j, (cx, cy) in enumerate(chips):
            first.append(cp(xs[a].at[hv[a][0]], outs[a].at[chip, hv[a][0]], k=6 * a + j, to=(cx, cy, mc)))
            first[-1].start()
    for a in range(n):
        for j in range(3):
            cp(xs[a].at[hv[a][0]], outs[a].at[idx[j], hv[a][0]], k=6 * a + j, to=sib).wait_recv()
            passed.append(cp(outs[a].at[idx[j], hv[a][0]], outs[a].at[idx[j], hv[a][0]], k=6 * a + 3 + j, to=sib))
            passed[-1].start()
    for a in range(n):
        for j in range(3):
            cp(xs[a].at[hv[a][1]], outs[a].at[idx[j], hv[a][1]], k=6 * a + 3 + j, to=sib).wait_recv()
    for c_ in first + passed:
        c_.wait_send()


def _my_chip():
    return 2 * lax.axis_index("x") + lax.axis_index("y")


def _own_slots(outs, shards):
    return [lax.dynamic_update_index_in_dim(o, x, _my_chip(), 0) for o, x in zip(outs, shards)]


def gather_weights(shards, name):
    n = len(shards)

    def body(*refs):
        _gather_body(refs[:n], refs[n:2 * n], *refs[2 * n:])

    hbm = pl.BlockSpec(memory_space=pl.ANY)
    outs = _pc(body, name=name, in_specs=[hbm] * n, out_specs=[hbm] * n, out_shape=[_sds((4,) + x.shape, x.dtype) for x in shards],
               scratch_shapes=[pltpu.SemaphoreType.DMA((6 * n,)), pltpu.SemaphoreType.DMA((6 * n,))])(*shards)
    return _own_slots(outs, shards)


GATHER_REST_ID = 3


def gather_weights_sc(shards, name):
    n = len(shards)
    x_refs = [jax.new_ref(x, memory_space=pltpu.MemorySpace.HBM) for x in shards]
    out_refs = [jax.empty_ref(_sds((4,) + x.shape, x.dtype), memory_space=pltpu.MemorySpace.HBM) for x in shards]

    @pl.kernel(mesh=plsc.ScalarSubcoreMesh(axis_name="sc", num_cores=1), name=name,
               scratch_types=(pltpu.SemaphoreType.DMA((6 * n,)), pltpu.SemaphoreType.DMA((6 * n,))),
               compiler_params=pltpu.CompilerParams(collective_id=GATHER_REST_ID))
    def launch(send_sems, recv_sems):
        mx, my, mc = _me()
        barrier = pltpu.get_barrier_semaphore()
        for peer in [(mx, my, 1 - mc)] + [(cx, cy, mc) for cx, cy in _other_chips(mx, my)]:
            pl.semaphore_signal(barrier, inc=1, device_id=peer, device_id_type=MESH)
        pl.semaphore_wait(barrier, 4)
        _gather_body(x_refs, out_refs, send_sems, recv_sems)

    launch()
    return _own_slots([o[...] for o in out_refs], shards)


def swap_halves(arrs, name):
    n = len(arrs)

    def body(*refs):
        xs, outs = refs[:n], refs[n:2 * n]
        send_sems, recv_sems = refs[2 * n:]
        mx, my, mc = _me()
        cps = []
        for a in range(n):
            theirs = _halves(xs[a].shape[1], mc, 16)[1]
            cps.append(_rcopy(xs[a].at[pl.ds(0, 4), theirs], outs[a], send_sems, recv_sems, a, (mx, my, 1 - mc)))
            cps[-1].start()
        for c_ in cps:
            c_.wait()

    hbm = pl.BlockSpec(memory_space=pl.ANY)
    return _pc(body, name=name, in_specs=[hbm] * n, out_specs=[hbm] * n,
               out_shape=[_sds((4, x.shape[1] // 2, x.shape[2]), x.dtype) for x in arrs],
               scratch_shapes=[pltpu.SemaphoreType.DMA((n,)), pltpu.SemaphoreType.DMA((n,))])(*arrs)


SCATTER_ID = 4


def scatter_chips_sc(arrs, name):
    n = len(arrs)
    x_refs = [jax.new_ref(x, memory_space=pltpu.MemorySpace.HBM) for x in arrs]
    out_refs = [jax.empty_ref(_sds(x.shape, x.dtype), memory_space=pltpu.MemorySpace.HBM) for x in arrs]

    @pl.kernel(mesh=plsc.ScalarSubcoreMesh(axis_name="sc", num_cores=1), name=name,
               scratch_types=(pltpu.SemaphoreType.DMA((3 * n,)), pltpu.SemaphoreType.DMA((3 * n,))),
               compiler_params=pltpu.CompilerParams(collective_id=SCATTER_ID))
    def launch(send_sems, recv_sems):
        mx, my, mc = _me()
        chip = 2 * mx + my
        chips = _other_chips(mx, my)
        idx = [2 * cx + cy for cx, cy in chips]
        barrier = pltpu.get_barrier_semaphore()
        for cx, cy in chips:
            pl.semaphore_signal(barrier, inc=1, device_id=(cx, cy, mc), device_id_type=MESH)
        pl.semaphore_wait(barrier, 3)
        cp = functools.partial(_rcopy, send_sems=send_sems, recv_sems=recv_sems)
        sends = []
        for a in range(n):
            for j, (cx, cy) in enumerate(chips):
                sends.append(cp(x_refs[a].at[idx[j]], out_refs[a].at[chip], k=3 * a + j, to=(cx, cy, mc)))
                sends[-1].start()
        for a in range(n):
            for j, (cx, cy) in enumerate(chips):
                cp(x_refs[a].at[idx[j]], out_refs[a].at[idx[j]], k=3 * a + j, to=(cx, cy, mc)).wait_recv()
        for c_ in sends:
            c_.wait_send()

    launch()
    return _own_slots([o[...] for o in out_refs], [lax.dynamic_index_in_dim(x, _my_chip(), 0, keepdims=False) for x in arrs])


def share_halves(parts, name):
    flat = [p for w in parts for p in w]
    nw, n = len(parts), len(flat)
    depth = n // nw

    def body(*refs):
        xs, outs = refs[:n], refs[n:n + nw]
        send_sems, recv_sems = refs[n + nw:]
        mx, my, mc = _me()
        sib = (mx, my, 1 - mc)
        sends, recvs = [], []
        for a in range(n):
            w, l = a // depth, a % depth
            mine, theirs = _halves(outs[w].shape[1], mc, 8)
            sends.append(_rcopy(xs[a], outs[w].at[l, mine], send_sems, recv_sems, a, sib))
            recvs.append(_rcopy(xs[a], outs[w].at[l, theirs], send_sems, recv_sems, a, sib))
            sends[-1].start()
        for c_ in recvs:
            c_.wait_recv()
        for c_ in sends:
            c_.wait_send()

    hbm = pl.BlockSpec(memory_space=pl.ANY)
    outs = _pc(body, name=name, in_specs=[hbm] * n, out_specs=[hbm] * nw,
               out_shape=[_sds((depth, 2 * w[0].shape[0], w[0].shape[1]), F32) for w in parts],
               scratch_shapes=[pltpu.SemaphoreType.DMA((n,)), pltpu.SemaphoreType.DMA((n,))])(*flat)
    outs = list(outs)
    mc = lax.axis_index("c")
    for w in range(nw):
        for l in range(depth):
            h = parts[w][l].shape[0]
            outs[w] = lax.dynamic_update_slice(outs[w], parts[w][l][None], (l, mc * h, 0))
    return outs


_BIG = ("w_in", "w_out", "w_ffn_in", "w_ffn_out")
N_CHIPS = 4
DEPTH = 2


def _pad_rows(v, mult=8):
    n = v.shape[0]
    rows = -(-n // 128)
    rows = -(-rows // mult) * mult
    return jnp.pad(v, (0, rows * 128 - n)).reshape(rows, 128)


class _Flat:
    def __init__(self):
        self.items = []

    def add(self, name, a):
        self.items.append((name, a.shape, a.reshape(-1).astype(F32)))

    def rows(self):
        return _pad_rows(jnp.concatenate([a for _, _, a in self.items]))

    def split(self, rows):
        flat = rows.reshape(-1)
        out, o = {}, 0
        for name, shape, a in self.items:
            out[name] = flat[o:o + a.shape[0]].reshape(shape)
            o += a.shape[0]
        return out

    def split_lead(self, rows3):
        n = rows3.shape[0]
        flat = rows3.reshape(n, -1)
        out, o = {}, 0
        for name, shape, a in self.items:
            out[name] = flat[:, o:o + a.shape[0]].reshape((n,) + tuple(shape))
            o += a.shape[0]
        return out


def _gsv(rows):
    z = jnp.zeros((2, D), F32)
    r = [z if a is None else a for a in rows] + [z] * 5
    return jnp.stack(r, axis=1)


def _pad8(a, rows=8, cols=128):
    return jnp.zeros((rows, cols), F32).at[:a.shape[0], :a.shape[1]].set(a.astype(F32))


def kernel(x, c, ctx, c_ctx, w_mod, b_mod, g_mix, w_in, wa_sink, na_rpb, ssm_conv_w, ssm_conv_b, ssm_dt_bias, ssm_a_log, ssm_d, ssm_norm_g, w_out, g_ffn, w_ffn_in, w_ffn_out, g_final, loss_target, m_c_ctx, m_w_mod, m_b_mod, m_g_mix, m_w_in, m_wa_sink, m_na_rpb, m_ssm_conv_w, m_ssm_conv_b, m_ssm_dt_bias, m_ssm_a_log, m_ssm_d, m_ssm_norm_g, m_w_out, m_g_ffn, m_w_ffn_in, m_w_ffn_out, m_g_final, v_c_ctx, v_w_mod, v_b_mod, v_g_mix, v_w_in, v_wa_sink, v_na_rpb, v_ssm_conv_w, v_ssm_conv_b, v_ssm_dt_bias, v_ssm_a_log, v_ssm_d, v_ssm_norm_g, v_w_out, v_g_ffn, v_w_ffn_in, v_w_ffn_out, v_g_final):
    L, Lc = x.shape[1], ctx.shape[1]
    T = L + Lc
    nL = L // TR
    mx, my, mc = lax.axis_index("x"), lax.axis_index("y"), lax.axis_index("c")
    dev = 4 * mx + 2 * my + mc
    chip = 2 * mx + my
    MODW = 6 * D // N_CHIPS
    CW = 1024 // N_CHIPS

    sc = _silu(c.astype(F32))
    scc = _silu(c_ctx.astype(F32))[None]
    f1 = _Flat()
    f1.add("sc", sc)
    f1.add("conv_w", ssm_conv_w)
    g1, _ = allgather8(f1.rows(), "gather_cond")
    g1 = f1.split_lead(g1)
    sc_all = g1["sc"][:, 0]
    conv_w = jnp.concatenate([g1["conv_w"][2 * k] for k in range(N_CHIPS)], axis=-1)
    A16 = jnp.concatenate([sc_all, scc, jnp.zeros((7, D), F32)], axis=0)

    mod_part = matmul_layers(A16, w_mod, "nn", "mod_fwd")
    f2 = _Flat()
    f2.add("mod", mod_part)
    g2, _ = allgather8(f2.rows(), "gather_mod")
    g2 = f2.split_lead(g2)["mod"]
    mods = jnp.concatenate([g2[2 * k] for k in range(N_CHIPS)], axis=-1) + b_mod[:, None, :]
    mod_l = lax.dynamic_index_in_dim(mods, dev, axis=1, keepdims=False).reshape(DEPTH, 6, D)
    mod_c = mods[:, 8].reshape(DEPTH, 6, D)
    mod = jnp.stack([mod_l, mod_c], axis=1)
    mrow = lambda l, j: mod[l, :, j]

    own = {"w_in": w_in, "w_out": w_out, "w_ffn_in": w_ffn_in, "w_ffn_out": w_ffn_out}
    sh16 = [own[n][l].astype(BF16) for n in _BIG for l in range(DEPTH)]
    after_mod = (g2[0, 0, 0, 0] * 0).astype(BF16)
    gath = list(gather_weights([sh16[0] + after_mod], "gather_first"))
    after_first = (gath[0][0, 0, 0] * 0).astype(BF16)
    gath += list(gather_weights_sc([sh16[1] + after_first] + sh16[2:], "gather_rest"))
    gw = {n: [gath[DEPTH * i + l] for l in range(DEPTH)] for i, n in enumerate(_BIG)}
    W_in = [jnp.pad(jnp.concatenate([g[k] for k in range(N_CHIPS)], axis=1), ((0, 0), (0, IN_PAD - IN_COLS))) for g in gw["w_in"]]
    W_out = [g.reshape(D, D) for g in gw["w_out"]]
    W_fo = [g.reshape(D_FF, D) for g in gw["w_ffn_out"]]
    W_fi = gw["w_ffn_in"]

    cos, sin = rope_tables(L, Lc)
    x0 = jnp.concatenate([x[0], ctx[0]], axis=0).astype(F32)

    sv = []
    xin = x0
    gsv_first = _gsv([None, mrow(0, 0), mrow(0, 1)])
    _, h1 = res_norm_mod(x0, None, gsv_first, g_mix[0][None], nL, "norm_first")
    for l in range(DEPTH):
        s = {"xin": xin, "h1": h1}
        P = matmul(h1, W_in[l], "nn", F32, f"in_proj{l}", tn=IN_PAD)
        qr, kr, kb, vb = rope_apply(P, C_QA // 256, P, C_KA // 128, cos, sin, False, f"rope{l}", kv_src=P)
        sink8 = _pad8(jnp.broadcast_to(wa_sink[l][:, None], (WA_HEADS, 128)))
        krs, va = _swap_halves_lanes(kr), P[:, C_VA:C_VA + 128]
        vas = _swap_halves_lanes(va)
        oa, sta = win_attn_fwd(qr, kr, krs, va, vas, sink8, L, Lc, f"wa_fwd{l}")
        bias = na_bias_table(na_rpb[l], l)
        ob, stb = na_fwd(P, kb, vb, bias, L, Lc, f"na_fwd{l}")
        w8 = jnp.concatenate([conv_w[l], jnp.zeros((1, 1024), F32)], axis=0)
        pre, act = conv_silu_fwd(P, w8, ssm_conv_b[l][None], nL, f"conv_fwd{l}")
        dtb8, al8 = _pad8(ssm_dt_bias[l]), _pad8(ssm_a_log[l])
        yf, yb, hsf, hsb = ssd_fwd(act, P, dtb8, al8, L, Lc, f"ssd_fwd{l}")
        dskip = jnp.repeat(ssm_d[l], S_P)[None]
        oc = ssm_out_fwd(yf, yb, act, P, dskip, ssm_norm_g[l][None], f"ssm_out_fwd{l}")
        mixin = [(oa, 0), (ob, 256), (oc, 512)]
        mix = out_proj_fwd(mixin, W_out[l], f"out_proj{l}")
        gsv_mid = _gsv([mrow(l, 2), mrow(l, 3), mrow(l, 4)])
        x1, h2 = res_norm_mod(xin, mix, gsv_mid, g_ffn[l][None], nL, f"norm_mid{l}")
        gu, af = ffn_in_swiglu(h2, W_fi[l], f"ffn_in{l}")
        fo = matmul(af, W_fo[l], "nn", BF16, f"ffn_out{l}", tk=D_FF)
        s.update(P=P, qr=qr, kr=kr, krs=krs, va=va, vas=vas, sink8=sink8, oa=oa, sta=sta, ob=ob, stb=stb, kb=kb, vb=vb, bias=bias, w8=w8, pre=pre, act=act, dtb8=dtb8, al8=al8, yf=yf,
                 yb=yb, hsf=hsf, hsb=hsb, dskip=dskip, mixin=mixin, mix=mix, gsv_mid=gsv_mid, x1=x1, h2=h2, gu=gu, af=af, fo=fo)
        if l + 1 < DEPTH:
            s["gsv_end"] = _gsv([mrow(l, 5), mrow(l + 1, 0), mrow(l + 1, 1)])
            xin, h1 = res_norm_mod(x1, fo, s["gsv_end"], g_mix[l + 1][None], nL, f"norm_end{l}")
        else:
            s["gsv_end"] = _gsv([mrow(l, 5), None, None])
        sv.append(s)

    last = sv[-1]
    loss8, dres, dfo, dgsv_end, dg_final = final_loss(last["x1"], last["fo"], last["gsv_end"], g_final[None], loss_target[0].astype(F32), nL, "final_loss")
    loss = lax.psum(loss8[0, 0], ("x", "y", "c"))

    dmod = [[None] * 6 for _ in range(DEPTH)]
    gW = {n: [None] * DEPTH for n in _BIG}
    small = [dict() for _ in range(DEPTH)]
    parts = [None] * DEPTH
    cvec = mc.astype(jnp.int32).reshape(1)
    grad_x = None
    for l in reversed(range(DEPTH)):
        s = sv[l]
        dmod[l][5] = dgsv_end[:, 0]
        if l + 1 < DEPTH:
            dmod[l + 1][0], dmod[l + 1][1] = dgsv_end[:, 1], dgsv_end[:, 2]
        dgu = ffn_out_dx_swiglu(dfo, W_fo[l], s["gu"], f"ffn_out_dx{l}")
        gW["w_ffn_out"][l] = matmul(s["af"], dfo, "tn", BF16, f"ffn_out_dw{l}", tm=1408, tk=T).reshape(N_CHIPS, D_FF // N_CHIPS, D)
        dh2 = matmul_fi(dgu, W_fi[l], "nt", BF16, f"ffn_in_dx{l}")
        gW["w_ffn_in"][l] = matmul_fi(s["h2"], dgu, "tn", BF16, f"ffn_in_dw{l}")
        dres, dmix, dgsv_mid, dg_ffn = res_norm_mod_bwd(s["x1"], s["mix"], s["gsv_mid"], g_ffn[l][None], dh2, dres, nL, f"norm_mid_bwd{l}")
        dmod[l][2], dmod[l][3], dmod[l][4] = dgsv_mid[:, 0], dgsv_mid[:, 1], dgsv_mid[:, 2]
        dmixin = matmul(dmix, W_out[l], "nt", BF16, f"out_proj_dx{l}")
        gW["w_out"][l] = out_proj_dw(s["mixin"], dmix, f"out_proj_dw{l}").reshape(N_CHIPS, D // N_CHIPS, D)
        P = s["P"]
        dqr, dkr, dkrs, dva, dvas, dsink = win_attn_bwd(s["qr"], s["kr"], s["krs"], s["va"], s["vas"], s["sink8"], dmixin, s["oa"], s["sta"], L, Lc,
                                                        f"wa_bwd{l}")
        dkr, dva = dkr + _swap_halves_lanes(dkrs), dva + _swap_halves_lanes(dvas)
        dqa, dka = rope_apply(dqr, 0, dkr[WA_BLK:WA_BLK + T], 0, cos, sin, True, f"rope_bwd{l}")
        dqb, dkb, dvb, dbias = na_bwd(P, s["kb"], s["vb"], s["bias"], dmixin, s["ob"], s["stb"], L, Lc, f"na_bwd{l}")
        dy, dxs1, dz, dvec = ssm_out_bwd(s["yf"], s["yb"], s["act"], P, s["dskip"], ssm_norm_g[l][None], dmixin, f"ssm_out_bwd{l}")
        dxf, dbf, dcf, ddf, dxb, dbb, dcb, ddb, ddtb, dal = ssd_bwd(s["act"], P, s["dtb8"], s["al8"], s["hsf"], s["hsb"], dy, L, Lc, f"ssd_bwd{l}")
        dpre = dsilu(s["pre"], [dxf, dxb, dxs1], [dbf, dbb], [dcf, dcb], f"dsilu{l}")
        dxbc, dw8, db8 = conv_bwd(dpre, P, s["w8"], nL, f"conv_bwd{l}")
        ddt = jnp.concatenate([ddf, ddb, jnp.zeros((T, IN_PAD - IN_COLS), F32)], axis=1)
        pieces = [(dqa, C_QA), (dqb, C_QB), (dz, C_Z), (dka, C_KA), (dva[WA_BLK:WA_BLK + T], C_VA), (dkb, C_KB), (dvb, C_VB),
                  (dxbc, C_XBC), (ddt, C_DT)]
        dh1, dwin = in_proj_bwd(pieces, s["h1"], W_in[l], f"in_proj_bwd{l}")
        cw = IN_COLS // N_CHIPS
        gW["w_in"][l] = jnp.stack([dwin[:, k * cw:(k + 1) * cw] for k in range(N_CHIPS)])
        garr = [gW[n][l] for n in _BIG]
        got = swap_halves(garr, f"reduce_d2d{l}")
        chip_sum = [add_halves(garr[a], got[a], cvec, f"reduce_add_pair{l}_{a}") for a in range(len(garr))]
        parts[l] = scatter_chips_sc(chip_sum, f"reduce_ici{l}")
        small[l] = dict(g_ffn=dg_ffn[0], wa_sink=dsink[:WA_HEADS, 0], na_rpb=na_rpb_grad(dbias, l), conv_w=dw8[:S_CONV], conv_b=db8[0],
                        dt_bias=ddtb[:2, :8], a_log=dal[:2, :8], ssm_d=dvec[0].reshape(S_HEADS, S_P).sum(axis=1), norm_g=dvec[1])
        if l > 0:
            p = sv[l - 1]
            dres, dfo, dgsv_end, dg_mix = res_norm_mod_bwd(s["xin"], p["fo"], p["gsv_end"], g_mix[l][None], dh1, dres, nL, f"norm_end_bwd{l - 1}")
        else:
            grad_x, _, dgsv_first, dg_mix = res_norm_mod_bwd(s["xin"], None, gsv_first, g_mix[0][None], dh1, dres, nL, "norm_first_bwd")
            dmod[0][0], dmod[0][1] = dgsv_first[:, 1], dgsv_first[:, 2]
        small[l]["g_mix"] = dg_mix[0]
    for l in range(DEPTH):
        for j in range(6):
            if dmod[l][j] is None:
                dmod[l][j] = jnp.zeros((2, D), F32)
    dmod = jnp.stack([jnp.stack(r, axis=1) for r in dmod])

    f3 = _Flat()
    f3.add("dmod_l", dmod[:, 0].reshape(DEPTH, 6 * D))
    f3.add("dmod_c", dmod[:, 1].reshape(DEPTH, 6 * D))
    f3.add("g_final", dg_final[0])
    for n in ("g_mix", "g_ffn", "wa_sink", "na_rpb", "conv_w", "conv_b", "dt_bias", "a_log", "ssm_d", "norm_g"):
        f3.add(n, jnp.stack([small[l][n] for l in range(DEPTH)]))
    g3, s3 = allgather8(f3.rows(), "reduce_small")
    dmod_all = f3.split_lead(g3)["dmod_l"]
    s3 = f3.split(s3)
    dmodc_tot = s3["dmod_c"]
    col0 = chip * MODW
    G16, G16c = [], []
    for l in range(DEPTH):
        rows = jnp.concatenate([dmod_all[:, l], dmodc_tot[l][None], jnp.zeros((7, 6 * D), F32)], axis=0)
        G16.append(lax.dynamic_slice_in_dim(rows, col0, MODW, axis=1))
        rc = jnp.concatenate([dmodc_tot[l][None], jnp.zeros((15, 6 * D), F32)], axis=0)
        G16c.append(lax.dynamic_slice_in_dim(rc, col0, MODW, axis=1))
    grad_w_mod = matmul_layers(A16, jnp.stack(G16), "tn", "mod_dw")
    dscc_part = matmul_layers(jnp.stack(G16c), w_mod, "nt", "mod_dx")[:, 0].sum(axis=0)
    _, s4 = allgather8(_pad_rows(dscc_part * (mc == 1).astype(F32)), "reduce_cctx")
    dscc = s4.reshape(-1)[:D]
    cc = c_ctx.astype(F32)
    sg = 1.0 / (1.0 + jnp.exp(-cc))
    grad_c_ctx = dscc * (sg * (1.0 + cc * (1.0 - sg)))

    halves = [[sum_slots(parts[l][i], f"reduce_add_chips{l}_{i}") for l in range(DEPTH)] for i in range(len(_BIG))]
    gsh = dict(zip(_BIG, share_halves(halves, "reduce_share")))

    grads = {"c_ctx": grad_c_ctx, "w_mod": grad_w_mod, "b_mod": s3["dmod_l"] + s3["dmod_c"], "g_mix": s3["g_mix"], "w_in": gsh["w_in"],
             "wa_sink": s3["wa_sink"], "na_rpb": s3["na_rpb"],
             "ssm_conv_w": lax.dynamic_slice_in_dim(s3["conv_w"], chip * CW, CW, axis=2), "ssm_conv_b": s3["conv_b"],
             "ssm_dt_bias": s3["dt_bias"], "ssm_a_log": s3["a_log"], "ssm_d": s3["ssm_d"], "ssm_norm_g": s3["norm_g"],
             "w_out": gsh["w_out"], "g_ffn": s3["g_ffn"], "w_ffn_in": gsh["w_ffn_in"], "w_ffn_out": gsh["w_ffn_out"], "g_final": s3["g_final"]}
    wts = {"c_ctx": c_ctx, "w_mod": w_mod, "b_mod": b_mod, "g_mix": g_mix, "w_in": w_in, "wa_sink": wa_sink, "na_rpb": na_rpb,
           "ssm_conv_w": ssm_conv_w, "ssm_conv_b": ssm_conv_b, "ssm_dt_bias": ssm_dt_bias, "ssm_a_log": ssm_a_log, "ssm_d": ssm_d,
           "ssm_norm_g": ssm_norm_g, "w_out": w_out, "g_ffn": g_ffn, "w_ffn_in": w_ffn_in, "w_ffn_out": w_ffn_out, "g_final": g_final}
    ms = {"c_ctx": m_c_ctx, "w_mod": m_w_mod, "b_mod": m_b_mod, "g_mix": m_g_mix, "w_in": m_w_in, "wa_sink": m_wa_sink, "na_rpb": m_na_rpb,
          "ssm_conv_w": m_ssm_conv_w, "ssm_conv_b": m_ssm_conv_b, "ssm_dt_bias": m_ssm_dt_bias, "ssm_a_log": m_ssm_a_log, "ssm_d": m_ssm_d,
          "ssm_norm_g": m_ssm_norm_g, "w_out": m_w_out, "g_ffn": m_g_ffn, "w_ffn_in": m_w_ffn_in, "w_ffn_out": m_w_ffn_out, "g_final": m_g_final}
    vs = {"c_ctx": v_c_ctx, "w_mod": v_w_mod, "b_mod": v_b_mod, "g_mix": v_g_mix, "w_in": v_w_in, "wa_sink": v_wa_sink, "na_rpb": v_na_rpb,
          "ssm_conv_w": v_ssm_conv_w, "ssm_conv_b": v_ssm_conv_b, "ssm_dt_bias": v_ssm_dt_bias, "ssm_a_log": v_ssm_a_log, "ssm_d": v_ssm_d,
          "ssm_norm_g": v_ssm_norm_g, "w_out": v_w_out, "g_ffn": v_g_ffn, "w_ffn_in": v_w_ffn_in, "w_ffn_out": v_w_ffn_out, "g_final": v_g_final}
    names = list(wts)
    grads = {n: grads[n].reshape(wts[n].shape).astype(F32) for n in names}
    big = ("w_mod", "w_in", "w_out", "w_ffn_in", "w_ffn_out")
    delta, new_m, new_v = {}, {}, {}
    for n in big:
        delta[n], new_m[n], new_v[n] = adamw(wts[n], grads[n], ms[n], vs[n], f"adamw_{n}")
    packs = []
    for src in (wts, grads, ms, vs):
        f = _Flat()
        for n in names:
            if n not in big:
                f.add(n, src[n])
        packs.append(f)
    d_, m_, v_ = adamw(*[f.rows()[None] for f in packs], "adamw_small")
    for dst, rows in ((delta, d_), (new_m, m_), (new_v, v_)):
        dst.update(packs[0].split(rows[0]))

    return (loss, grad_x[:L][None], *[grads[n] for n in names], *[delta[n] for n in names],
            *[new_m[n] for n in names], *[new_v[n] for n in names])
```

```python
import functools

import numpy as np
import jax
import jax.numpy as jnp
from jax import lax
from jax.experimental import pallas as pl
from jax.experimental.pallas import tpu as pltpu
from jax.experimental.pallas import tpu_sc as plsc

F32 = jnp.float32
BF16 = jnp.bfloat16
_MXU = jnp.bfloat16
_HI = lax.Precision.HIGHEST
MESH = pl.DeviceIdType.MESH

D = 1024
HD = 64
GRID_W = 64
EPS = 1e-6
ROPE_BASE = 10000.0
WA_HEADS, WA_KV = 4, 2
WA_BLK = 128
NA_HEADS, NA_KH, NA_KW = 4, 8, 16
S_HEADS, S_P, S_INNER, S_GROUPS, S_N, S_CONV, S_Q = 8, 64, 512, 2, 128, 7, 128
D_FF = 2816
IN_COLS = 2832
IN_PAD = 2944
C_QA, C_QB, C_Z, C_KA, C_VA, C_KB, C_VB, C_XBC, C_DT = 0, 256, 512, 1024, 1152, 1280, 1536, 1792, 2816
ADAM_LR, ADAM_B1, ADAM_B2, ADAM_EPS, ADAM_WD, ADAM_STEP = 0.001, 0.9, 0.999, 1e-08, 0.01, 10

TR = 256
NEG = -1e30
VMEM_CAP = 56 * 1024 * 1024


PIN_BYTES = 256 * 1024


def _is_big(a):
    return hasattr(a, "shape") and len(a.shape) >= 2 and int(np.prod(a.shape)) * jnp.dtype(a.dtype).itemsize >= PIN_BYTES


def _pc(body, *, out_shape, pin=True, **kw):
    if not pin:
        return pl.pallas_call(body, out_shape=out_shape, **kw)
    one = isinstance(out_shape, jax.ShapeDtypeStruct)
    outs = [pltpu.HBM(s.shape, s.dtype) if _is_big(s) else s for s in ([out_shape] if one else out_shape)]
    call = pl.pallas_call(body, out_shape=outs[0] if one else outs, **kw)
    return lambda *args: call(*[pltpu.with_memory_space_constraint(a, pltpu.HBM) if _is_big(a) else a for a in args])


def _cp(sem=None, vmem=None):
    kw = {}
    if sem is not None:
        kw["dimension_semantics"] = sem
    if vmem is not None:
        kw["vmem_limit_bytes"] = int(min(max(vmem, 16 * 1024 * 1024), VMEM_CAP))
    return pltpu.CompilerParams(**kw)


def _sds(shape, dtype):
    return jax.ShapeDtypeStruct(tuple(shape), dtype)


_DIMS = {"nn": ((1,), (0,)), "nt": ((1,), (1,)), "tn": ((0,), (0,))}


def _dg(a, b, dims):
    return lax.dot_general(a.astype(_MXU), b.astype(_MXU), (dims, ((), ())), preferred_element_type=F32)


@functools.partial(jax.custom_vjp, nondiff_argnums=(2,))
def bdot(a, b, mode):
    return _dg(a, b, _DIMS[mode])


def _bdot_fwd(a, b, mode):
    return bdot(a, b, mode), (a, b)


def _bdot_bwd(mode, res, g):
    a, b = res
    if mode == "nn":
        return bdot(g, b, "nt"), bdot(a, g, "tn")
    if mode == "nt":
        return bdot(g, b, "nn"), bdot(g, a, "tn")
    return bdot(b, g, "nt"), bdot(a, g, "nn")


bdot.defvjp(_bdot_fwd, _bdot_bwd)


def hdot(a, b, mode="nn"):
    return lax.dot_general(a, b, (_DIMS[mode], ((), ())), precision=_HI, preferred_element_type=F32)


def _silu(x):
    return x / (1.0 + jnp.exp(-x))


def _softplus(x):
    return jnp.maximum(x, 0.0) + jnp.log(1.0 + jnp.exp(-jnp.abs(x)))


def _div_tile(n, cap, mult):
    if n <= cap:
        return n
    best = None
    for t in range(mult, cap + 1, mult):
        if n % t == 0:
            best = t
    assert best is not None, (n, cap, mult)
    return best


def matmul(a, b, mode, out_dtype, name, tm=640, tn=1536, tk=1408, hi=False):
    if mode == "tn":
        K, M = a.shape
    else:
        M, K = a.shape
    N = b.shape[0] if mode == "nt" else b.shape[1]
    tm = _div_tile(M, tm, 128 if mode == "tn" else 16)
    tn = _div_tile(N, tn, 128)
    tk = _div_tile(K, tk, 128 if mode != "tn" else 16)
    nk = K // tk
    dims = _DIMS[mode]

    def body(a_ref, b_ref, o_ref, *acc):
        if hi:
            part = lax.dot_general(a_ref[...], b_ref[...], (dims, ((), ())), precision=_HI, preferred_element_type=F32)
        else:
            part = _dg(a_ref[...], b_ref[...], dims)
        if nk == 1:
            o_ref[...] = part.astype(o_ref.dtype)
        else:
            k = pl.program_id(2)

            @pl.when(k == 0)
            def _():
                acc[0][...] = part

            @pl.when(k > 0)
            def _():
                acc[0][...] += part

            @pl.when(k == nk - 1)
            def _():
                o_ref[...] = acc[0][...].astype(o_ref.dtype)

    if mode == "tn":
        a_spec = pl.BlockSpec((tk, tm), lambda i, j, k: (k, i))
    else:
        a_spec = pl.BlockSpec((tm, tk), lambda i, j, k: (i, k))
    if mode == "nt":
        b_spec = pl.BlockSpec((tn, tk), lambda i, j, k: (j, k))
    else:
        b_spec = pl.BlockSpec((tk, tn), lambda i, j, k: (k, j))
    isz = lambda x: jnp.dtype(x.dtype).itemsize
    vmem = 2 * (tm * tk * isz(a) + tk * tn * isz(b) + tm * tn * jnp.dtype(out_dtype).itemsize) + 3 * tm * tn * 4
    return _pc(
        body, name=name, grid=(M // tm, N // tn, nk),
        in_specs=[a_spec, b_spec], out_specs=pl.BlockSpec((tm, tn), lambda i, j, k: (i, j)),
        out_shape=_sds((M, N), out_dtype),
        scratch_shapes=[pltpu.VMEM((tm, tn), F32)] if nk > 1 else [],
        compiler_params=_cp(("parallel", "parallel", "arbitrary"), vmem + (8 << 20)),
    )(a, b)


def matmul_layers(a, b, mode, name):
    nl = b.shape[0]
    a3 = a if a.ndim == 3 else a[None]
    shared = a3.shape[0] == 1
    M = a3.shape[2] if mode == "tn" else a3.shape[1]
    N = b.shape[1] if mode == "nt" else b.shape[2]

    def body(a_ref, b_ref, o_ref):
        o_ref[0] = _dg(a_ref[0], b_ref[0], _DIMS[mode])

    return _pc(body, name=name, grid=(nl,),
               in_specs=[pl.BlockSpec((1,) + a3.shape[1:], (lambda l: (0, 0, 0)) if shared else (lambda l: (l, 0, 0))),
                         pl.BlockSpec((1,) + b.shape[1:], lambda l: (l, 0, 0))],
               out_specs=pl.BlockSpec((1, M, N), lambda l: (l, 0, 0)), out_shape=_sds((nl, M, N), F32),
               compiler_params=_cp(("parallel",), 48 << 20))(a3, b)


def out_proj_fwd(pieces, w, name):
    T = pieces[0][0].shape[0]
    arrs, offs = [a for a, _ in pieces], [o for _, o in pieces]
    n = len(arrs)
    tm = _div_tile(T, 640, 16)

    def body(*refs):
        w_ref, o_ref = refs[n], refs[n + 1]
        acc = None
        for j in range(n):
            part = _dg(refs[j][...], w_ref[offs[j]:offs[j] + arrs[j].shape[1], :], _DIMS["nn"])
            acc = part if acc is None else acc + part
        o_ref[...] = acc.astype(o_ref.dtype)

    return _pc(body, name=name, grid=(T // tm,),
               in_specs=[pl.BlockSpec((tm, a.shape[1]), lambda i: (i, 0)) for a in arrs] + [pl.BlockSpec(w.shape, lambda i: (0, 0))],
               out_specs=pl.BlockSpec((tm, w.shape[1]), lambda i: (i, 0)), out_shape=_sds((T, w.shape[1]), BF16),
               compiler_params=_cp(("parallel",), 32 << 20))(*arrs, w)


def out_proj_dw(pieces, dy, name):
    T, N = dy.shape
    arrs, offs = [a for a, _ in pieces], [o for _, o in pieces]
    n = len(arrs)
    rows = sum(a.shape[1] for a in arrs)
    tn = 512

    def body(*refs):
        d_ref, o_ref = refs[n], refs[n + 1]
        for j in range(n):
            o_ref[offs[j]:offs[j] + arrs[j].shape[1], :] = _dg(refs[j][...], d_ref[...], _DIMS["tn"]).astype(o_ref.dtype)

    return _pc(body, name=name, grid=(N // tn,),
               in_specs=[pl.BlockSpec(a.shape, lambda j: (0, 0)) for a in arrs] + [pl.BlockSpec((T, tn), lambda j: (0, j))],
               out_specs=pl.BlockSpec((rows, tn), lambda j: (0, j)), out_shape=_sds((rows, N), BF16),
               compiler_params=_cp(("parallel",), 48 << 20))(*arrs, dy)


def in_proj_bwd(pieces, h1, w, name):
    T = h1.shape[0]
    arrs = [a for a, _ in pieces]
    offs = [o for _, o in pieces]
    wid = [a.shape[1] for a in arrs]
    n = len(arrs)
    assert sum(wid) == IN_PAD, "the pieces must tile all columns of P"
    tm = _div_tile(T, 640, 16)

    def dx_body(*refs):
        w_ref, o_ref = refs[n], refs[n + 1]
        acc = None
        for j in range(n):
            part = _dg(refs[j][...], w_ref[:, offs[j]:offs[j] + wid[j]], _DIMS["nt"])
            acc = part if acc is None else acc + part
        o_ref[...] = acc.astype(o_ref.dtype)

    dh1 = _pc(dx_body, name=name + "_dx", grid=(T // tm,),
              in_specs=[pl.BlockSpec((tm, wj), lambda i: (i, 0)) for wj in wid] + [pl.BlockSpec((D, IN_PAD), lambda i: (0, 0))],
              out_specs=pl.BlockSpec((tm, D), lambda i: (i, 0)), out_shape=_sds((T, D), BF16),
              compiler_params=_cp(("parallel",), 40 << 20))(*arrs, w)

    tmd, nk = 512, 4
    tk = T // nk

    def dw_body(h_ref, *refs):
        o_ref, acc = refs[n], refs[n + 1]
        k = pl.program_id(1)

        @pl.when(k == 0)
        def _():
            acc[...] = jnp.zeros_like(acc)

        for j in range(n):
            acc[:, offs[j]:offs[j] + wid[j]] += _dg(h_ref[...], refs[j][...], _DIMS["tn"])

        @pl.when(k == nk - 1)
        def _():
            o_ref[...] = acc[...].astype(o_ref.dtype)

    dw = _pc(dw_body, name=name + "_dw", grid=(D // tmd, nk),
             in_specs=[pl.BlockSpec((tk, tmd), lambda i, k: (k, i))] + [pl.BlockSpec((tk, wj), lambda i, k: (k, 0)) for wj in wid],
             out_specs=pl.BlockSpec((tmd, IN_PAD), lambda i, k: (i, 0)), out_shape=_sds((D, IN_PAD), BF16),
             scratch_shapes=[pltpu.VMEM((tmd, IN_PAD), F32)], compiler_params=_cp(("parallel", "arbitrary"), 48 << 20))(h1, *arrs)
    return dh1, dw


def _norm_mod(xo, shift, scale, g):
    r = lax.rsqrt(jnp.mean(xo * xo, axis=-1, keepdims=True) + EPS)
    return (xo * r) * g * (1.0 + scale) + shift


def res_norm_mod(x, y, gsv, g, nL, name):
    T = x.shape[0]
    has_y = y is not None

    def body(*refs):
        if has_y:
            x_ref, y_ref, gsv_ref, g_ref, xo_ref, h_ref = refs
            xo = x_ref[...] + gsv_ref[0, 0:1, :] * y_ref[...]
            xo_ref[...] = xo
        else:
            x_ref, gsv_ref, g_ref, h_ref = refs
            xo = x_ref[...]
        h_ref[...] = _norm_mod(xo, gsv_ref[0, 1:2, :], gsv_ref[0, 2:3, :], g_ref[...]).astype(h_ref.dtype)

    row = pl.BlockSpec((TR, D), lambda i: (i, 0))
    in_specs = [row] + ([row] if has_y else []) + [pl.BlockSpec((1, 8, D), lambda i: (i // nL, 0, 0)),
                                                     pl.BlockSpec((1, D), lambda i: (0, 0))]
    out_specs = ([row] if has_y else []) + [row]
    out_shape = ([_sds((T, D), F32)] if has_y else []) + [_sds((T, D), BF16)]
    args = (x, y, gsv, g) if has_y else (x, gsv, g)
    outs = _pc(body, name=name, grid=(T // TR,), in_specs=in_specs, out_specs=out_specs, out_shape=out_shape,
               compiler_params=_cp(("arbitrary",), 24 << 20))(*args)
    return (outs[0], outs[1]) if has_y else (None, outs[0])


def res_norm_mod_bwd(xo, y, gsv, g, dh, dres, nL, name):
    T = xo.shape[0]
    has_y = y is not None

    def body(*refs):
        if has_y:
            xo_ref, y_ref, gsv_ref, g_ref, dh_ref, dres_ref, dx_ref, dy_ref, dgsv_ref, dg_ref = refs
        else:
            xo_ref, gsv_ref, g_ref, dh_ref, dres_ref, dx_ref, dgsv_ref, dg_ref = refs
        i = pl.program_id(0)

        @pl.when((i == 0) | (i == nL))
        def _():
            dgsv_ref[...] = jnp.zeros_like(dgsv_ref)

        @pl.when(i == 0)
        def _():
            dg_ref[...] = jnp.zeros_like(dg_ref)

        _, vjp = jax.vjp(_norm_mod, xo_ref[...], gsv_ref[0, 1:2, :], gsv_ref[0, 2:3, :], g_ref[...])
        dxn, dshift, dscale, dg = vjp(dh_ref[...].astype(F32))
        dxo = dres_ref[...] + dxn
        dx_ref[...] = dxo
        if has_y:
            dy_ref[...] = (gsv_ref[0, 0:1, :] * dxo).astype(dy_ref.dtype)
            dgsv_ref[0, 0:1, :] += jnp.sum(y_ref[...] * dxo, axis=0, keepdims=True)
        dgsv_ref[0, 1:2, :] += dshift
        dgsv_ref[0, 2:3, :] += dscale
        dg_ref[0:1, :] += dg

    row = pl.BlockSpec((TR, D), lambda i: (i, 0))
    gspec = pl.BlockSpec((1, 8, D), lambda i: (i // nL, 0, 0))
    in_specs = [row] + ([row] if has_y else []) + [gspec, pl.BlockSpec((1, D), lambda i: (0, 0)), row, row]
    out_specs = [row] + ([row] if has_y else []) + [gspec, pl.BlockSpec((8, D), lambda i: (0, 0))]
    out_shape = [_sds((T, D), F32)] + ([_sds((T, D), BF16)] if has_y else []) + [_sds((2, 8, D), F32), _sds((8, D), F32)]
    args = (xo, y, gsv, g, dh, dres) if has_y else (xo, gsv, g, dh, dres)
    outs = _pc(body, name=name, grid=(T // TR,), in_specs=in_specs, out_specs=out_specs, out_shape=out_shape,
               compiler_params=_cp(("arbitrary",), 32 << 20))(*args)
    if has_y:
        return outs
    return outs[0], None, outs[1], outs[2]


def final_loss(x, y, gsv, g, target, nL, name):
    T = x.shape[0]

    def lossf(xo, gv, t):
        yn = (xo * lax.rsqrt(jnp.mean(xo * xo, axis=-1, keepdims=True) + EPS)) * gv
        e = yn - t
        return 0.5 * jnp.sum(jnp.sum(e * e, axis=-1, keepdims=True) * (1.0 / D), axis=0, keepdims=True)

    def body(x_ref, y_ref, gsv_ref, g_ref, t_ref, loss_ref, dx_ref, dy_ref, dgsv_ref, dg_ref):
        i = pl.program_id(0)

        @pl.when(i == 0)
        def _():
            loss_ref[...] = jnp.zeros_like(loss_ref)
            dg_ref[...] = jnp.zeros_like(dg_ref)

        @pl.when((i == 0) | (i == nL))
        def _():
            dgsv_ref[...] = jnp.zeros_like(dgsv_ref)

        @pl.when(i < nL)
        def _():
            gate = gsv_ref[0, 0:1, :]
            yv = y_ref[...]
            xo = x_ref[...] + gate * yv
            lv, vjp = jax.vjp(lossf, xo, g_ref[...], t_ref[...])
            dxo, dg, _ = vjp(jnp.ones((1, 1), F32))
            loss_ref[...] += jnp.broadcast_to(lv, loss_ref.shape)
            dx_ref[...] = dxo
            dy_ref[...] = (gate * dxo).astype(dy_ref.dtype)
            dgsv_ref[0, 0:1, :] += jnp.sum(yv * dxo, axis=0, keepdims=True)
            dg_ref[0:1, :] += dg

        @pl.when(i >= nL)
        def _():
            dx_ref[...] = jnp.zeros_like(dx_ref)
            dy_ref[...] = jnp.zeros_like(dy_ref)

    row = pl.BlockSpec((TR, D), lambda i: (i, 0))
    gspec = pl.BlockSpec((1, 8, D), lambda i: (i // nL, 0, 0))
    return _pc(
        body, name=name, grid=(T // TR,),
        in_specs=[row, row, gspec, pl.BlockSpec((1, D), lambda i: (0, 0)),
                  pl.BlockSpec((TR, D), lambda i: (jnp.minimum(i, nL - 1), 0))],
        out_specs=[pl.BlockSpec((8, 128), lambda i: (0, 0)), row, row, gspec, pl.BlockSpec((8, D), lambda i: (0, 0))],
        out_shape=[_sds((8, 128), F32), _sds((T, D), F32), _sds((T, D), BF16), _sds((2, 8, D), F32), _sds((8, D), F32)],
        compiler_params=_cp(("arbitrary",), 32 << 20),
    )(x, y, gsv, g, target)


FI_BLK = 2 * D_FF // 4


def _fi_chip(j):
    return (j % 2) * 2 + j // 2


def matmul_fi(a, b, mode, out_dtype, name):
    T = a.shape[0]
    if mode == "tn":
        tmd = 512

        def body(a_ref, b_ref, o_ref):
            o_ref[0] = _dg(a_ref[...], b_ref[...], _DIMS["tn"]).astype(o_ref.dtype)

        return _pc(body, name=name, grid=(4, D // tmd),
                   in_specs=[pl.BlockSpec((T, tmd), lambda j, i: (0, i)), pl.BlockSpec((T, FI_BLK), lambda j, i: (0, j))],
                   out_specs=pl.BlockSpec((1, tmd, FI_BLK), lambda j, i: (_fi_chip(j), i, 0)),
                   out_shape=_sds((4, D, FI_BLK), out_dtype), compiler_params=_cp(("parallel", "arbitrary"), 48 << 20))(a, b)
    assert mode == "nt"
    tm = _div_tile(T, 640, 16)

    def body(a_ref, b_ref, o_ref):
        acc = None
        for k in range(4):
            part = _dg(a_ref[:, k * FI_BLK:(k + 1) * FI_BLK], b_ref[_fi_chip(k)], _DIMS["nt"])
            acc = part if acc is None else acc + part
        o_ref[...] = acc.astype(o_ref.dtype)

    return _pc(body, name=name, grid=(T // tm,),
               in_specs=[pl.BlockSpec((tm, 4 * FI_BLK), lambda i: (i, 0)), pl.BlockSpec((4, D, FI_BLK), lambda i: (0, 0, 0))],
               out_specs=pl.BlockSpec((tm, D), lambda i: (i, 0)), out_shape=_sds((T, D), out_dtype),
               compiler_params=_cp(("parallel",), VMEM_CAP))(a, b)


def _swiglu(gate, up):
    return _silu(gate) * up


def ffn_in_swiglu(a, w, name):
    T = a.shape[0]
    tm = _div_tile(T, 640, 32)
    half = tm // 2

    def body(a_ref, wg_ref, wu_ref, gu_ref, act_ref):
        for rows in (slice(0, half), slice(half, tm)):
            g = _dg(a_ref[rows, :], wg_ref[0], _DIMS["nn"]).astype(BF16)
            u = _dg(a_ref[rows, :], wu_ref[0], _DIMS["nn"]).astype(BF16)
            gu_ref[rows, :FI_BLK] = g
            gu_ref[rows, FI_BLK:] = u
            act_ref[rows, :] = _swiglu(g.astype(F32), u.astype(F32)).astype(BF16)

    wspec = lambda r: pl.BlockSpec((1, D, FI_BLK), lambda j, i: (_fi_chip(2 * j + r), 0, 0))
    return _pc(body, name=name, grid=(2, T // tm),
               in_specs=[pl.BlockSpec((tm, D), lambda j, i: (i, 0)), wspec(0), wspec(1)],
               out_specs=[pl.BlockSpec((tm, 2 * FI_BLK), lambda j, i: (i, j)), pl.BlockSpec((tm, FI_BLK), lambda j, i: (i, j))],
               out_shape=[_sds((T, 4 * FI_BLK), BF16), _sds((T, D_FF), BF16)],
               compiler_params=_cp(("parallel", "arbitrary"), 48 << 20))(a, w, w)


def ffn_out_dx_swiglu(d, w, gu, name):
    T = d.shape[0]
    tm = _div_tile(T, 640, 16)

    def body(d_ref, w_ref, gu_ref, o_ref):
        for j in range(2):
            dact = _dg(d_ref[...], w_ref[j * FI_BLK:(j + 1) * FI_BLK, :], _DIMS["nt"]).astype(BF16).astype(F32)
            gs, us = slice(2 * j * FI_BLK, (2 * j + 1) * FI_BLK), slice((2 * j + 1) * FI_BLK, (2 * j + 2) * FI_BLK)
            g, u = gu_ref[:, gs].astype(F32), gu_ref[:, us].astype(F32)
            sg = 1.0 / (1.0 + jnp.exp(-g))
            sl = g * sg
            o_ref[:, gs] = (dact * u * (sg + sl * (1.0 - sg))).astype(o_ref.dtype)
            o_ref[:, us] = (dact * sl).astype(o_ref.dtype)

    return _pc(body, name=name, grid=(T // tm,),
               in_specs=[pl.BlockSpec((tm, D), lambda i: (i, 0)), pl.BlockSpec((D_FF, D), lambda i: (0, 0)), pl.BlockSpec((tm, 4 * FI_BLK), lambda i: (i, 0))],
               out_specs=pl.BlockSpec((tm, 4 * FI_BLK), lambda i: (i, 0)), out_shape=_sds((T, 4 * FI_BLK), BF16),
               compiler_params=_cp(("parallel",), 48 << 20))(d, w, gu)


def rope_tables(L, Lc):
    t = np.arange(L)
    rows, cols = t // GRID_W, t % GRID_W
    inv = ROPE_BASE ** (-np.arange(16, dtype=np.float32) / 16)
    lane = np.arange(64)
    pos = np.where((lane // 32)[None, :] == 0, rows[:, None], cols[:, None]).astype(np.float32)
    ang = jnp.asarray(pos) * jnp.asarray(inv[lane % 16])[None, :]
    cos = jnp.concatenate([jnp.cos(ang), jnp.ones((Lc, 64), F32)], axis=0)
    sin = jnp.concatenate([jnp.sin(ang), jnp.zeros((Lc, 64), F32)], axis=0)
    return jnp.tile(cos, (1, 2)), jnp.tile(sin, (1, 2))


def rope_apply(q_src, q_col, k_src, k_col, cos, sin, transpose, name, kv_src=None):
    T = cos.shape[0]
    with_kv = kv_src is not None
    tr = _div_tile(T, 640, 16)

    def rot(x, c, s):
        first = (lax.broadcasted_iota(jnp.int32, x.shape, 1) % 32) < 16
        if transpose:
            y = x * s
            return x * c + jnp.where(first, pltpu.roll(y, 112, 1), -pltpu.roll(y, 16, 1))
        return x * c + jnp.where(first, -pltpu.roll(x, 112, 1), pltpu.roll(x, 16, 1)) * s

    def body(q_ref, k_ref, c_ref, s_ref, *rest):
        qo_ref, ko_ref = rest[-4:-2] if with_kv else rest
        c, s = c_ref[...], s_ref[...]
        for j in range(2):
            qo_ref[:, j * 128:(j + 1) * 128] = rot(q_ref[:, j * 128:(j + 1) * 128].astype(F32), c, s).astype(qo_ref.dtype)
        ko_ref[...] = rot(k_ref[...].astype(F32), c, s).astype(ko_ref.dtype)
        if with_kv:
            rest[-2][...] = rest[0][...].astype(BF16)
            rest[-1][...] = rest[1][...].astype(BF16)

    tab = pl.BlockSpec((tr, 128), lambda i: (i, 0))
    wide = pl.BlockSpec((tr, 256), lambda i: (i, 0))
    kv_in = [pl.BlockSpec((tr, 256), lambda i: (i, C_KB // 256)), pl.BlockSpec((tr, 256), lambda i: (i, C_VB // 256))] if with_kv else []
    return _pc(body, name=name, grid=(T // tr,),
               in_specs=[pl.BlockSpec((tr, 256), lambda i: (i, q_col)), pl.BlockSpec((tr, 128), lambda i: (i, k_col)), tab, tab] + kv_in,
               out_specs=[wide, tab] + ([wide, wide] if with_kv else []),
               out_shape=[_sds((T, 256), BF16), _sds((T, 128), BF16)] + ([_sds((T, 256), BF16)] * 2 if with_kv else []),
               compiler_params=_cp(("parallel",), 32 << 20))(q_src, k_src, cos, sin, *([kv_src, kv_src] if with_kv else []))


_SCALE = HD ** -0.5


def _attn_tile(qh, ks, vs, extra):
    ss = []
    for k, add in ks:
        s = _dg(qh, k, _DIMS["nt"]) * _SCALE
        ss.append(s if add is None else s + add)
    m = ss[0].max(axis=-1, keepdims=True)
    for s in ss[1:]:
        m = jnp.maximum(m, s.max(axis=-1, keepdims=True))
    if extra is not None:
        m = jnp.maximum(m, extra)
    ps = [jnp.exp(s - m) for s in ss]
    den = ps[0].sum(axis=-1, keepdims=True)
    for p in ps[1:]:
        den = den + p.sum(axis=-1, keepdims=True)
    if extra is not None:
        den = den + jnp.exp(extra - m)
    num = _dg(ps[0], vs[0], _DIMS["nn"])
    for p, v in zip(ps[1:], vs[1:]):
        num = num + _dg(p, v, _DIMS["nn"])
    linv = 1.0 / den
    return num * linv, m, linv


def _attn_bwd_tile(qh, ks, vs, extra, m, linv, oh, doh):
    delta = jnp.sum(doh * oh, axis=-1, keepdims=True)
    dq = None
    dks, dvs, dss = [], [], []
    for (k, add), v in zip(ks, vs):
        s = _dg(qh, k, _DIMS["nt"]) * _SCALE
        if add is not None:
            s = s + add
        p = jnp.exp(s - m) * linv
        dvs.append(_dg(p, doh, _DIMS["tn"]))
        ds = p * (_dg(doh, v, _DIMS["nt"]) - delta)
        dss.append(ds)
        dsq = ds * _SCALE
        part = _dg(dsq, k, _DIMS["nn"])
        dq = part if dq is None else dq + part
        dks.append(_dg(dsq, qh, _DIMS["tn"]))
    dextra = None
    if extra is not None:
        dextra = -(jnp.exp(extra - m) * linv * delta)
    return dq, dks, dvs, dss, dextra


def _wa_mask(n, L):
    qpos = n * WA_BLK + lax.broadcasted_iota(jnp.int32, (WA_BLK, 3 * WA_BLK), 0)
    kpos = (n - 1) * WA_BLK + lax.broadcasted_iota(jnp.int32, (WA_BLK, 3 * WA_BLK), 1)
    ok = (jnp.abs(qpos - kpos) <= WA_BLK) & (kpos >= 0) & (kpos < L)
    return jnp.where(ok, 0.0, NEG).astype(F32)


WA_BPS = 2
_WA_PAIRS = (((0, 0), (1, 3), False), ((1, 2), (0, 1), True))


def _swap_halves_lanes(a):
    return jnp.concatenate([a[:, HD:], a[:, :HD]], axis=1)


def _wa_specs(L, Lc):
    nb = L // WA_BLK
    cb = L // Lc

    def blk(j):
        return pl.BlockSpec((WA_BLK, 128), lambda s: (jnp.clip(s * WA_BPS - 1 + j, 0, nb - 1), 0))

    return nb, [blk(j) for j in range(WA_BPS + 2)] + [pl.BlockSpec((Lc, 128), lambda s: (cb, 0))]


def _wa_pair_q(q_ref, qs, lo, hi):
    a = q_ref[qs, lo[0] * 128:(lo[0] + 1) * 128]
    b = q_ref[qs, hi[0] * 128:(hi[0] + 1) * 128]
    lane = lax.broadcasted_iota(jnp.int32, a.shape, 1)
    zero = jnp.zeros_like(a)
    return jnp.concatenate([jnp.where(lane < HD, a, zero), jnp.where(lane >= HD, b, zero)], axis=0)


def _wa_pair_vec(ref, qs, lo, hi, base=0):
    return jnp.concatenate([ref[qs, base + lo[1]:base + lo[1] + 1], ref[qs, base + hi[1]:base + hi[1] + 1]], axis=0)


def _wa_pair_sink(s_ref, n, lo, hi):
    return jnp.concatenate([jnp.broadcast_to(s_ref[lo[1]:lo[1] + 1, 0:1], (n, 1)), jnp.broadcast_to(s_ref[hi[1]:hi[1] + 1, 0:1], (n, 1))], axis=0)


def win_attn_fwd(qr, kr, krs, v, vs, sink, L, Lc, name):
    T = L + Lc
    nb, specs = _wa_specs(L, Lc)
    nk = WA_BPS + 2
    QB = WA_BPS * WA_BLK
    nlat = nb // WA_BPS

    def body(q_ref, *refs):
        groups = [refs[g * (nk + 1):(g + 1) * (nk + 1)] for g in range(4)]
        s_ref, o_ref, st_ref = refs[-3], refs[-2], refs[-1]
        s = pl.program_id(0)

        def run(qs, n, ks_of, vs_of):
            outs = []
            for lo, hi, swapped in _WA_PAIRS:
                kb, vb = groups[1 if swapped else 0], groups[3 if swapped else 2]
                o2, m2, l2 = _attn_tile(_wa_pair_q(q_ref, qs, lo, hi), ks_of(kb), vs_of(vb), _wa_pair_sink(s_ref, n, lo, hi))
                outs.append(o2)
                for r, (_, h) in enumerate((lo, hi)):
                    st_ref[qs, h:h + 1] = m2[r * n:(r + 1) * n]
                    st_ref[qs, WA_HEADS + h:WA_HEADS + h + 1] = l2[r * n:(r + 1) * n]
            lane = lax.broadcasted_iota(jnp.int32, (n, 128), 1)
            o_ref[qs, 0:128] = jnp.where(lane < HD, outs[0][:n], outs[1][n:]).astype(o_ref.dtype)
            o_ref[qs, 128:256] = jnp.where(lane < HD, outs[1][:n], outs[0][n:]).astype(o_ref.dtype)

        @pl.when(s < nlat)
        def _():
            for b in range(WA_BPS):
                m1 = _wa_mask(s * WA_BPS + b, L)
                mask = jnp.concatenate([m1, m1], axis=0)
                cat = lambda g: jnp.concatenate([g[b + j][...] for j in range(3)], axis=0)
                run(slice(b * WA_BLK, (b + 1) * WA_BLK), WA_BLK,
                    lambda kb: [(cat(kb), mask), (kb[nk][...], None)], lambda vb: [cat(vb), vb[nk][...]])

        @pl.when(s >= nlat)
        def _():
            run(slice(None), QB, lambda kb: [(kb[nk][...], None)], lambda vb: [vb[nk][...]])

    qspec = pl.BlockSpec((QB, 256), lambda s: (s, 0))
    return _pc(body, name=name, grid=(T // QB,),
               in_specs=[qspec] + specs * 4 + [pl.BlockSpec((8, 128), lambda s: (0, 0))],
               out_specs=[qspec, pl.BlockSpec((QB, 8), lambda s: (s, 0))], out_shape=[_sds((T, 256), BF16), _sds((T, 8), F32)],
               compiler_params=_cp(("arbitrary",), 40 << 20))(qr, *([kr] * (nk + 1)), *([krs] * (nk + 1)), *([v] * (nk + 1)), *([vs] * (nk + 1)), sink)


def win_attn_bwd(qr, kr, krs, v, vs, sink, do_src, o, stats, L, Lc, name):
    T = L + Lc
    nb, specs = _wa_specs(L, Lc)
    nk = WA_BPS + 2
    QB = WA_BPS * WA_BLK
    nlat = nb // WA_BPS
    cx = WA_BLK + L

    def body(q_ref, *refs):
        groups = [refs[g * (nk + 1):(g + 1) * (nk + 1)] for g in range(4)]
        s_ref, do_ref, o_ref, st_ref, dq_ref, dk_ref, dks_ref, dv_ref, dvs_ref, ds_ref = refs[4 * (nk + 1):]
        s = pl.program_id(0)

        @pl.when(s == 0)
        def _():
            for r in (dk_ref, dks_ref, dv_ref, dvs_ref, ds_ref):
                r[...] = jnp.zeros_like(r)

        def run(qs, n, ks_of, vs_of, rows):
            lane = lax.broadcasted_iota(jnp.int32, (n, 128), 1)
            dqs = []
            for lo, hi, swapped in _WA_PAIRS:
                kb, vb = groups[1 if swapped else 0], groups[3 if swapped else 2]
                dka, dva = (dks_ref, dvs_ref) if swapped else (dk_ref, dv_ref)
                pair = lambda ref: jnp.concatenate([jnp.where(lane < HD, ref[qs, lo[0] * 128:(lo[0] + 1) * 128].astype(F32), 0.0),
                                                    jnp.where(lane >= HD, ref[qs, hi[0] * 128:(hi[0] + 1) * 128].astype(F32), 0.0)], axis=0)
                dq2, dks, dvs, _, dex = _attn_bwd_tile(_wa_pair_q(q_ref, qs, lo, hi), ks_of(kb), vs_of(vb), _wa_pair_sink(s_ref, n, lo, hi),
                                                       _wa_pair_vec(st_ref, qs, lo, hi), _wa_pair_vec(st_ref, qs, lo, hi, WA_HEADS), pair(o_ref), pair(do_ref))
                dqs.append(dq2)
                for r, (_, h) in enumerate((lo, hi)):
                    ds_ref[h:h + 1, :] += jnp.broadcast_to(jnp.sum(dex[r * n:(r + 1) * n], axis=0, keepdims=True), (1, 128))
                if rows is not None:
                    dka[rows, :] += dks[0]
                    dva[rows, :] += dvs[0]
                dka[cx:cx + Lc, :] += dks[-1]
                dva[cx:cx + Lc, :] += dvs[-1]
            dq_ref[qs, 0:128] = jnp.where(lane < HD, dqs[0][:n], dqs[1][n:])
            dq_ref[qs, 128:256] = jnp.where(lane < HD, dqs[1][:n], dqs[0][n:])

        @pl.when(s < nlat)
        def _():
            for b in range(WA_BPS):
                nblk = s * WA_BPS + b
                m1 = _wa_mask(nblk, L)
                mask = jnp.concatenate([m1, m1], axis=0)
                cat = lambda g: jnp.concatenate([g[b + j][...] for j in range(3)], axis=0)
                run(slice(b * WA_BLK, (b + 1) * WA_BLK), WA_BLK, lambda kb: [(cat(kb), mask), (kb[nk][...], None)],
                    lambda vb: [cat(vb), vb[nk][...]], pl.ds(pl.multiple_of(nblk * WA_BLK, WA_BLK), 3 * WA_BLK))

        @pl.when(s >= nlat)
        def _():
            run(slice(None), QB, lambda kb: [(kb[nk][...], None)], lambda vb: [vb[nk][...]], None)

    qspec = pl.BlockSpec((QB, 256), lambda s: (s, 0))
    acc_spec = pl.BlockSpec((T + 2 * WA_BLK, 128), lambda s: (0, 0))
    acc_shape = _sds((T + 2 * WA_BLK, 128), F32)
    return _pc(body, name=name, grid=(T // QB,),
               in_specs=[qspec] + specs * 4 + [pl.BlockSpec((8, 128), lambda s: (0, 0)), qspec, qspec, pl.BlockSpec((QB, 8), lambda s: (s, 0))],
               out_specs=[qspec, acc_spec, acc_spec, acc_spec, acc_spec, pl.BlockSpec((8, 128), lambda s: (0, 0))],
               out_shape=[_sds((T, 256), F32), acc_shape, acc_shape, acc_shape, acc_shape, _sds((8, 128), F32)],
               compiler_params=_cp(("arbitrary",), 48 << 20))(qr, *([kr] * (nk + 1)), *([krs] * (nk + 1)), *([v] * (nk + 1)), *([vs] * (nk + 1)),
                                                              sink, do_src, o, stats)


def na_index_tables():
    qc = np.arange(GRID_W)[:, None]
    kc = np.arange(GRID_W)[None, :]
    cstart = np.clip(qc - NA_KW // 2, 0, GRID_W - NA_KW)
    ok = (kc >= cstart) & (kc < cstart + NA_KW)
    dx = np.clip(kc - qc, -(NA_KW - 1), NA_KW - 1) + (NA_KW - 1)
    off = np.arange(NA_KH)[:, None]
    kr = np.arange(NA_KH)[None, :]
    dy = kr - off + (NA_KH - 1)
    return ok, dx, dy


def _na_selectors():
    ok, dx, dy = na_index_tables()
    e1 = np.zeros((GRID_W * GRID_W, 128), np.float32)
    qi, ki = np.nonzero(ok)
    e1[qi * GRID_W + ki, dx[qi, ki]] = 1.0
    e2 = np.zeros((16, NA_KH * NA_KH), np.float32)
    oi, ri = np.meshgrid(np.arange(NA_KH), np.arange(NA_KH), indexing="ij")
    e2[dy[oi, ri].ravel(), (oi * NA_KH + ri).ravel()] = 1.0
    return ok, jnp.asarray(e1), jnp.asarray(np.kron(np.eye(NA_HEADS, dtype=np.float32), e2))


def na_bias_table(rpb, tag):
    ok, e1, e2 = _na_selectors()
    r2 = jnp.pad(rpb.astype(F32), ((0, 0), (0, 1), (0, 128 - (2 * NA_KW - 1)))).reshape(NA_HEADS * 16, 128)
    r1 = matmul(e2, r2, "tn", F32, f"na_bias_sel1_{tag}", hi=True)
    x = matmul(r1, e1, "nt", F32, f"na_bias_sel2_{tag}", hi=True)
    b = x.reshape(NA_HEADS, NA_KH, NA_KH, GRID_W, GRID_W).transpose(0, 1, 3, 2, 4)
    b = b + jnp.asarray(np.where(ok, 0.0, NEG).astype(np.float32))[None, None, :, None, :]
    return b.reshape(NA_HEADS, NA_KH, GRID_W, NA_KH * GRID_W)


def _na_rows(r, GR):
    r0 = jnp.clip(r - NA_KH // 2, 0, GR - NA_KH)
    return r0, jnp.clip(r - r0, 0, NA_KH - 1)


NA_RPS = 4


def _pair_rows(x):
    lane = lax.broadcasted_iota(jnp.int32, x.shape, 1)
    zero = jnp.zeros_like(x)
    return jnp.concatenate([jnp.where(lane < HD, x, zero), jnp.where(lane >= HD, x, zero)], axis=0)


def _unpair_rows(x2):
    n = x2.shape[0] // 2
    lane = lax.broadcasted_iota(jnp.int32, (n, 128), 1)
    return jnp.where(lane < HD, x2[:n], x2[n:])


def na_fwd(P, kb, vb, bias, L, Lc, name):
    T = L + Lc
    GR = L // GRID_W
    W = NA_KH * GRID_W
    QB = GRID_W * NA_RPS
    nlat = GR // NA_RPS

    def body(q_ref, k_ref, v_ref, b_ref, o_ref, st_ref):
        s = pl.program_id(0)

        def put(qs, p, res):
            o2, m2, l2 = res
            n = o2.shape[0] // 2
            o_ref[qs, p * 128:(p + 1) * 128] = _unpair_rows(o2).astype(o_ref.dtype)
            for r in range(2):
                st_ref[qs, 2 * p + r:2 * p + r + 1] = m2[r * n:(r + 1) * n]
                st_ref[qs, NA_HEADS + 2 * p + r:NA_HEADS + 2 * p + r + 1] = l2[r * n:(r + 1) * n]

        @pl.when(s < nlat)
        def _():
            for rr in range(NA_RPS):
                r0, off = _na_rows(s * NA_RPS + rr, GR)
                rows = pl.ds(pl.multiple_of(r0 * GRID_W, GRID_W), W)
                qs = slice(rr * GRID_W, (rr + 1) * GRID_W)
                for p in range(NA_HEADS // 2):
                    ps = slice(p * 128, (p + 1) * 128)
                    b2 = jnp.concatenate([b_ref[2 * p, off], b_ref[2 * p + 1, off]], axis=0)
                    put(qs, p, _attn_tile(_pair_rows(q_ref[qs, ps]), [(k_ref[rows, ps], b2), (k_ref[L:T, ps], None)],
                                          [v_ref[rows, ps], v_ref[L:T, ps]], None))

        @pl.when(s >= nlat)
        def _():
            for p in range(NA_HEADS // 2):
                ps = slice(p * 128, (p + 1) * 128)
                put(slice(None), p, _attn_tile(_pair_rows(q_ref[:, ps]), [(k_ref[L:T, ps], None)], [v_ref[L:T, ps]], None))

    one = pl.Buffered(1)
    return _pc(body, name=name, grid=(T // QB,),
               in_specs=[pl.BlockSpec((QB, 256), lambda r: (r, C_QB // 256)),
                         pl.BlockSpec((T, 256), lambda r: (0, 0), pipeline_mode=one),
                         pl.BlockSpec((T, 256), lambda r: (0, 0), pipeline_mode=one),
                         pl.BlockSpec((NA_HEADS, NA_KH, GRID_W, W), lambda r: (0, 0, 0, 0), pipeline_mode=one)],
               out_specs=[pl.BlockSpec((QB, 256), lambda r: (r, 0)), pl.BlockSpec((QB, 8), lambda r: (r, 0))],
               out_shape=[_sds((T, 256), BF16), _sds((T, 8), F32)],
               compiler_params=_cp(("arbitrary",), 32 << 20))(P, kb, vb, bias)


def na_bwd(P, kb, vb, bias, do_src, o, stats, L, Lc, name):
    T = L + Lc
    GR = L // GRID_W
    W = NA_KH * GRID_W
    QB = GRID_W * NA_RPS
    nlat = GR // NA_RPS

    def body(q_ref, k_ref, v_ref, b_ref, do_ref, o_ref, st_ref, dq_ref, dk_ref, dv_ref, db_ref):
        s = pl.program_id(0)

        @pl.when(s == 0)
        def _():
            dk_ref[...] = jnp.zeros_like(dk_ref)
            dv_ref[...] = jnp.zeros_like(dv_ref)
            db_ref[...] = jnp.zeros_like(db_ref)

        def tile(qs, p, ks, vs):
            ps = slice(p * 128, (p + 1) * 128)
            m2 = jnp.concatenate([st_ref[qs, 2 * p:2 * p + 1], st_ref[qs, 2 * p + 1:2 * p + 2]], axis=0)
            l2 = jnp.concatenate([st_ref[qs, NA_HEADS + 2 * p:NA_HEADS + 2 * p + 1], st_ref[qs, NA_HEADS + 2 * p + 1:NA_HEADS + 2 * p + 2]], axis=0)
            dq2, dks, dvs, dss, _ = _attn_bwd_tile(_pair_rows(q_ref[qs, ps]), ks, vs, None, m2, l2,
                                                   _pair_rows(o_ref[qs, ps].astype(F32)), _pair_rows(do_ref[qs, ps].astype(F32)))
            dq_ref[qs, ps] = _unpair_rows(dq2).astype(dq_ref.dtype)
            return dks, dvs, dss

        @pl.when(s < nlat)
        def _():
            for rr in range(NA_RPS):
                r0, off = _na_rows(s * NA_RPS + rr, GR)
                rows = pl.ds(pl.multiple_of(r0 * GRID_W, GRID_W), W)
                qs = slice(rr * GRID_W, (rr + 1) * GRID_W)
                for p in range(NA_HEADS // 2):
                    ps = slice(p * 128, (p + 1) * 128)
                    b2 = jnp.concatenate([b_ref[2 * p, off], b_ref[2 * p + 1, off]], axis=0)
                    dks, dvs, dss = tile(qs, p, [(k_ref[rows, ps], b2), (k_ref[L:T, ps], None)], [v_ref[rows, ps], v_ref[L:T, ps]])
                    dk_ref[rows, ps] += dks[0]
                    dv_ref[rows, ps] += dvs[0]
                    dk_ref[L:T, ps] += dks[1]
                    dv_ref[L:T, ps] += dvs[1]
                    db_ref[2 * p, off] += dss[0][:GRID_W]
                    db_ref[2 * p + 1, off] += dss[0][GRID_W:]

        @pl.when(s >= nlat)
        def _():
            for p in range(NA_HEADS // 2):
                ps = slice(p * 128, (p + 1) * 128)
                dks, dvs, _ = tile(slice(None), p, [(k_ref[L:T, ps], None)], [v_ref[L:T, ps]])
                dk_ref[L:T, ps] += dks[0]
                dv_ref[L:T, ps] += dvs[0]

    one = pl.Buffered(1)
    full = lambda shape: pl.BlockSpec(shape, lambda r: (0,) * len(shape), pipeline_mode=one)
    qspec = pl.BlockSpec((QB, 256), lambda r: (r, 0))
    return _pc(body, name=name, grid=(T // QB,),
               in_specs=[pl.BlockSpec((QB, 256), lambda r: (r, C_QB // 256)), full((T, 256)), full((T, 256)),
                         full((NA_HEADS, NA_KH, GRID_W, W)), pl.BlockSpec((QB, 256), lambda r: (r, 1)), qspec, pl.BlockSpec((QB, 8), lambda r: (r, 0))],
               out_specs=[qspec, full((T, 256)), full((T, 256)), full((NA_HEADS, NA_KH, GRID_W, W))],
               out_shape=[_sds((T, 256), BF16), _sds((T, 256), F32), _sds((T, 256), F32), _sds((NA_HEADS, NA_KH, GRID_W, W), F32)],
               compiler_params=_cp(("arbitrary",), 48 << 20))(P, kb, vb, bias, do_src, o, stats)


def na_rpb_grad(dbias, tag):
    _, e1, e2 = _na_selectors()
    x = dbias.reshape(NA_HEADS, NA_KH, GRID_W, NA_KH, GRID_W).transpose(0, 1, 3, 2, 4).reshape(NA_HEADS * NA_KH * NA_KH, GRID_W * GRID_W)
    r1 = matmul(x, e1, "nn", F32, f"na_rpb_sel1_{tag}", hi=True, tk=1024)
    r2 = matmul(e2, r1, "nn", F32, f"na_rpb_sel2_{tag}", hi=True)
    return r2.reshape(NA_HEADS, 16, 128)[:, :2 * NA_KH - 1, :2 * NA_KW - 1]


_HALO = 8
CONV_CB = 4
CONV_RB = 64


def _halo_specs(T, col0):
    nh = TR // _HALO
    specs = []
    for j in range(CONV_CB):
        specs.append(pl.BlockSpec((_HALO, 256), lambda i, j=j: (jnp.maximum(i * nh - 1, 0), col0 + j)))
        specs.append(pl.BlockSpec((TR, 256), lambda i, j=j: (i, col0 + j)))
        specs.append(pl.BlockSpec((_HALO, 256), lambda i, j=j: (jnp.minimum((i + 1) * nh, T // _HALO - 1), col0 + j)))
    return specs


def _fill_ext(ext, prv, cur, nxt, i, nL, nT):
    has_prev = jnp.where((i != 0) & (i != nL), 1.0, 0.0)
    has_next = jnp.where((i != nL - 1) & (i != nT - 1), 1.0, 0.0)
    ext[0:_HALO, :] = prv[...].astype(F32) * has_prev
    ext[_HALO:_HALO + TR, :] = cur[...].astype(F32)
    ext[_HALO + TR:, :] = nxt[...].astype(F32) * has_next


def conv_silu_fwd(P, w8, b, nL, name):
    T = P.shape[0]
    nT = T // TR

    def body(*refs):
        xin, (w_ref, b_ref, pre_ref, act_ref, ext) = refs[:3 * CONV_CB], refs[3 * CONV_CB:]
        i = pl.program_id(0)
        for j in range(CONV_CB):
            cs = slice(j * 256, (j + 1) * 256)
            _fill_ext(ext, *xin[3 * j:3 * j + 3], i, nL, nT)
            for r in range(0, TR, CONV_RB):
                y = jnp.broadcast_to(b_ref[:, cs], (CONV_RB, 256))
                for k in range(S_CONV):
                    y = y + w_ref[k:k + 1, cs] * ext[pl.ds(_HALO - S_CONV // 2 + k + r, CONV_RB), :]
                pre_ref[r:r + CONV_RB, cs] = y
                act_ref[r:r + CONV_RB, cs] = _silu(y)

    out = pl.BlockSpec((TR, 1024), lambda i: (i, 0))
    return _pc(body, name=name, grid=(nT,),
               in_specs=_halo_specs(T, C_XBC // 256) + [pl.BlockSpec((8, 1024), lambda i: (0, 0)), pl.BlockSpec((1, 1024), lambda i: (0, 0))],
               out_specs=[out, out], out_shape=[_sds((T, 1024), F32), _sds((T, 1024), F32)],
               scratch_shapes=[pltpu.VMEM((TR + 2 * _HALO, 256), F32)],
               compiler_params=_cp(("parallel",), 24 << 20))(*([P] * (3 * CONV_CB)), w8, b)


def dsilu(pre, dxs_list, db_list, dc_list, name):
    T = pre.shape[0]
    n1, n2, n3 = len(dxs_list), len(db_list), len(dc_list)

    def body(*refs):
        pre_ref = refs[0]
        ins = refs[1:1 + n1 + n2 + n3]
        out = refs[-1]

        def part(rs, lo, hi):
            g = rs[0][...].astype(F32)
            for r in rs[1:]:
                g = g + r[...].astype(F32)
            x = pre_ref[:, lo:hi]
            sg = 1.0 / (1.0 + jnp.exp(-x))
            sl = x * sg
            out[:, lo:hi] = g * (sg + sl * (1.0 - sg))

        part(ins[:n1], 0, 512)
        part(ins[n1:n1 + n2], 512, 768)
        part(ins[n1 + n2:], 768, 1024)

    spec = lambda w: pl.BlockSpec((TR, w), lambda i: (i, 0))
    return _pc(body, name=name, grid=(T // TR,),
               in_specs=[spec(1024)] + [spec(512)] * n1 + [spec(256)] * (n2 + n3),
               out_specs=spec(1024), out_shape=_sds((T, 1024), F32),
               compiler_params=_cp(("parallel",), 32 << 20))(pre, *dxs_list, *db_list, *dc_list)


def conv_bwd(dpre, P, w8, nL, name):
    T = P.shape[0]
    nT = T // TR

    def body(*refs):
        din, xin, (w_ref, dx_ref, dw_ref, db_ref, extd) = refs[:3 * CONV_CB], refs[3 * CONV_CB:4 * CONV_CB], refs[4 * CONV_CB:]
        i = pl.program_id(0)

        @pl.when(i == 0)
        def _():
            dw_ref[...] = jnp.zeros_like(dw_ref)
            db_ref[...] = jnp.zeros_like(db_ref)

        fold = lambda a: functools.reduce(lambda p, q: p + q, [a[q:q + 8] for q in range(0, CONV_RB, 8)])
        for j in range(CONV_CB):
            cs = slice(j * 256, (j + 1) * 256)
            _fill_ext(extd, *din[3 * j:3 * j + 3], i, nL, nT)
            dws = [jnp.zeros((8, 256), F32) for _ in range(S_CONV)]
            dbs = jnp.zeros((8, 256), F32)
            for r in range(0, TR, CONV_RB):
                x = xin[j][r:r + CONV_RB, :]
                dx = jnp.zeros((CONV_RB, 256), F32)
                for k in range(S_CONV):
                    sd = extd[pl.ds(_HALO + S_CONV // 2 - k + r, CONV_RB), :]
                    dx = dx + w_ref[k:k + 1, cs] * sd
                    dws[k] = dws[k] + fold(sd * x)
                dx_ref[r:r + CONV_RB, cs] = dx.astype(dx_ref.dtype)
                dbs = dbs + fold(din[3 * j + 1][r:r + CONV_RB, :])
            for k in range(S_CONV):
                dw_ref[k:k + 1, cs] += jnp.sum(dws[k], axis=0, keepdims=True)
            db_ref[0:1, cs] += jnp.sum(dbs, axis=0, keepdims=True)

    acc = pl.BlockSpec((8, 1024), lambda i: (0, 0))
    xspecs = [pl.BlockSpec((TR, 256), lambda i, j=j: (i, C_XBC // 256 + j)) for j in range(CONV_CB)]
    return _pc(body, name=name, grid=(nT,),
               in_specs=_halo_specs(T, 0) + xspecs + [acc],
               out_specs=[pl.BlockSpec((TR, 1024), lambda i: (i, 0)), acc, acc],
               out_shape=[_sds((T, 1024), BF16), _sds((8, 1024), F32), _sds((8, 1024), F32)],
               scratch_shapes=[pltpu.VMEM((TR + 2 * _HALO, 256), F32)],
               compiler_params=_cp(("arbitrary",), 24 << 20))(*([dpre] * (3 * CONV_CB)), *([P] * CONV_CB), w8)


def _onehot_row(h, n):
    return (lax.broadcasted_iota(jnp.int32, (1, n), 1) == h).astype(F32)


def _onehot_col(h, n):
    return (lax.broadcasted_iota(jnp.int32, (n, 1), 0) == h).astype(F32)


S_PAIRS = S_HEADS // 2


def _ssd_chunk(xs, dtr, dtb, alog, bm, cm, hin, reverse):
    Qn = S_Q
    ii = lax.broadcasted_iota(jnp.int32, (Qn, Qn), 0)
    jj = lax.broadcasted_iota(jnp.int32, (Qn, Qn), 1)
    keep = (ii <= jj) if reverse else (ii >= jj)
    tri = keep.astype(F32)
    triT = ((jj <= ii) if reverse else (jj >= ii)).astype(F32)
    eye = (ii == jj).astype(F32)
    low = jj < S_P
    top = ii < S_P
    dt = _softplus(dtr + dtb)
    a = dt * (-jnp.exp(alog))
    cs = hdot(tri, a)
    csT = hdot(a, triT, "tn")
    dtT = hdot(dt, eye, "tn")
    last = _onehot_row(0 if reverse else Qn - 1, Qn)
    ys, houts = [], []
    for p in range(S_PAIRS):
        g = p // (S_PAIRS // S_GROUPS)
        if p % (S_PAIRS // S_GROUPS) == 0:
            G = bdot(cm[g], bm[g], "nt")
        per_head = []
        for h in (2 * p, 2 * p + 1):
            eh_r, eh_c = _onehot_row(h, S_HEADS), _onehot_col(h, S_HEADS)
            cs_c = jnp.sum(cs * eh_r, axis=1, keepdims=True)
            dt_c = jnp.sum(dt * eh_r, axis=1, keepdims=True)
            cs_r = jnp.sum(csT * eh_c, axis=0, keepdims=True)
            dt_r = jnp.sum(dtT * eh_c, axis=0, keepdims=True)
            tot = jnp.sum(cs_r * last, axis=1, keepdims=True)
            w = G * jnp.exp(jnp.where(keep, cs_c - cs_r, NEG)) * dt_r
            per_head.append((bdot(w, xs[p], "nn"), jnp.exp(cs_c), jnp.exp(tot - cs_c) * dt_c, jnp.exp(tot)))
        (y0, e0, f0, d0), (y1, e1, f1, d1) = per_head
        y = jnp.where(low, y0, y1) + bdot(cm[g], hin[p], "nt") * jnp.where(low, e0, e1)
        hout = hin[p] * jnp.where(top, d0, d1) + bdot(xs[p] * jnp.where(low, f0, f1), bm[g], "tn")
        ys.append(y)
        houts.append(hout)
    return ys, houts


def _ssd_orders(L, Lc):
    nl, ncx = L // S_Q, Lc // S_Q
    fwd = lambda s: jnp.where(s < ncx, nl + s, s - ncx)
    bwd = lambda s: nl + ncx - 1 - s
    return nl + ncx, fwd, bwd


def _ssd_in_specs(fo, bo, step):
    def at(order, w, col):
        return pl.BlockSpec((S_Q, w), lambda u: (order(step(u)), col))
    specs = []
    for order in (fo, bo):
        specs += [at(order, 512, 0), at(order, 256, 2), at(order, 256, 3), at(order, 128, C_DT // 128)]
    return specs


def ssd_fwd(act, P, dtb, alog, L, Lc, name):
    T = L + Lc
    ns, fo, bo = _ssd_orders(L, Lc)

    def body(xf, bf, cf, df, xb, bb, cb, db, dtb_ref, al_ref, yf, yb, hsf, hsb, Hf, Hb):
        s = pl.program_id(0)

        @pl.when(s == 0)
        def _():
            Hf[...] = jnp.zeros_like(Hf)
            Hb[...] = jnp.zeros_like(Hb)

        for d, (x_r, b_r, c_r, dt_r, y_r, hs_r, H) in enumerate(((xf, bf, cf, df, yf, hsf, Hf), (xb, bb, cb, db, yb, hsb, Hb))):
            hin = [H[p] for p in range(S_PAIRS)]
            hs_r[0] = H[...]
            ys, houts = _ssd_chunk(
                [x_r[:, p * 128:(p + 1) * 128] for p in range(S_PAIRS)], dt_r[:, d * 8:(d + 1) * 8],
                dtb_ref[d:d + 1, 0:8], al_ref[d:d + 1, 0:8],
                [b_r[:, g * S_N:(g + 1) * S_N] for g in range(S_GROUPS)], [c_r[:, g * S_N:(g + 1) * S_N] for g in range(S_GROUPS)],
                hin, reverse=(d == 1))
            for p in range(S_PAIRS):
                y_r[:, p * 128:(p + 1) * 128] = ys[p]
                H[p] = houts[p]

    ident = lambda u: u
    small = pl.BlockSpec((8, 128), lambda u: (0, 0))
    hspec = pl.BlockSpec((1, S_PAIRS, 2 * S_P, S_N), lambda u: (u, 0, 0, 0))
    return _pc(body, name=name, grid=(ns,),
               in_specs=_ssd_in_specs(fo, bo, ident) + [small, small],
               out_specs=[pl.BlockSpec((S_Q, 512), lambda u: (fo(u), 0)), pl.BlockSpec((S_Q, 512), lambda u: (bo(u), 0)), hspec, hspec],
               out_shape=[_sds((T, 512), F32), _sds((T, 512), F32), _sds((ns, S_PAIRS, 2 * S_P, S_N), F32), _sds((ns, S_PAIRS, 2 * S_P, S_N), F32)],
               scratch_shapes=[pltpu.VMEM((S_PAIRS, 2 * S_P, S_N), F32), pltpu.VMEM((S_PAIRS, 2 * S_P, S_N), F32)],
               compiler_params=_cp(("arbitrary",), 32 << 20))(act, act, act, P, act, act, act, P, dtb, alog)


def ssd_bwd(act, P, dtb, alog, hsf, hsb, dy, L, Lc, name):
    T = L + Lc
    ns, fo, bo = _ssd_orders(L, Lc)
    step = lambda u: ns - 1 - u

    def body(xf, bf, cf, df, xb, bb, cb, db, dtb_ref, al_ref, hsf_r, hsb_r, dyf, dyb,
             dxf, dbf, dcf, ddf, dxb, dbb, dcb, ddb, ddtb, dal, dHf, dHb):
        u = pl.program_id(0)

        @pl.when(u == 0)
        def _():
            dHf[...] = jnp.zeros_like(dHf)
            dHb[...] = jnp.zeros_like(dHb)
            ddtb[...] = jnp.zeros_like(ddtb)
            dal[...] = jnp.zeros_like(dal)

        dirs = ((xf, bf, cf, df, hsf_r, dyf, dxf, dbf, dcf, ddf, dHf), (xb, bb, cb, db, hsb_r, dyb, dxb, dbb, dcb, ddb, dHb))
        for d, (x_r, b_r, c_r, dt_r, hs_r, dy_r, dx_o, db_o, dc_o, dd_o, dH) in enumerate(dirs):
            f = functools.partial(_ssd_chunk, reverse=(d == 1))
            _, vjp = jax.vjp(
                f, [x_r[:, p * 128:(p + 1) * 128] for p in range(S_PAIRS)], dt_r[:, d * 8:(d + 1) * 8],
                dtb_ref[d:d + 1, 0:8], al_ref[d:d + 1, 0:8],
                [b_r[:, g * S_N:(g + 1) * S_N] for g in range(S_GROUPS)], [c_r[:, g * S_N:(g + 1) * S_N] for g in range(S_GROUPS)],
                [hs_r[0, p] for p in range(S_PAIRS)])
            gx, gdt, gdtb, gal, gb, gc, gh = vjp(([dy_r[:, p * 128:(p + 1) * 128] for p in range(S_PAIRS)],
                                                  [dH[p] for p in range(S_PAIRS)]))
            for p in range(S_PAIRS):
                dx_o[:, p * 128:(p + 1) * 128] = gx[p]
                dH[p] = gh[p]
            for g in range(S_GROUPS):
                db_o[:, g * S_N:(g + 1) * S_N] = gb[g]
                dc_o[:, g * S_N:(g + 1) * S_N] = gc[g]
            dd_o[...] = gdt
            ddtb[d:d + 1, 0:8] += gdtb
            dal[d:d + 1, 0:8] += gal

    small = pl.BlockSpec((8, 128), lambda u: (0, 0))
    hspec = pl.BlockSpec((1, S_PAIRS, 2 * S_P, S_N), lambda u: (step(u), 0, 0, 0))
    at = lambda order, w: pl.BlockSpec((S_Q, w), lambda u: (order(step(u)), 0))
    outs = []
    for order in (fo, bo):
        outs += [at(order, 512), at(order, 256), at(order, 256), at(order, 8)]
    oshape = [_sds((T, 512), F32), _sds((T, 256), F32), _sds((T, 256), F32), _sds((T, 8), F32)]
    return _pc(body, name=name, grid=(ns,),
               in_specs=_ssd_in_specs(fo, bo, step) + [small, small, hspec, hspec, at(fo, 512), at(bo, 512)],
               out_specs=outs + [small, small], out_shape=oshape + oshape + [_sds((8, 128), F32), _sds((8, 128), F32)],
               scratch_shapes=[pltpu.VMEM((S_PAIRS, 2 * S_P, S_N), F32), pltpu.VMEM((S_PAIRS, 2 * S_P, S_N), F32)],
               compiler_params=_cp(("arbitrary",), 40 << 20))(act, act, act, P, act, act, act, P, dtb, alog, hsf, hsb, dy, dy)


def _ssm_out(yf, yb, xs, z, dskip, g):
    y = (yf + yb + dskip * xs) * _silu(z)
    return (y * lax.rsqrt(jnp.mean(y * y, axis=-1, keepdims=True) + EPS)) * g


def ssm_out_fwd(yf, yb, act, P, dskip, g, name):
    T = yf.shape[0]

    def body(yf_r, yb_r, xs_r, z_r, d_r, g_r, o_r):
        o_r[...] = _ssm_out(yf_r[...], yb_r[...], xs_r[...], z_r[...], d_r[...], g_r[...]).astype(o_r.dtype)

    row = pl.BlockSpec((TR, 512), lambda i: (i, 0))
    vec = pl.BlockSpec((1, 512), lambda i: (0, 0))
    return _pc(body, name=name, grid=(T // TR,),
               in_specs=[row, row, row, pl.BlockSpec((TR, 512), lambda i: (i, C_Z // 512)), vec, vec],
               out_specs=row, out_shape=_sds((T, 512), BF16),
               compiler_params=_cp(("parallel",), 16 << 20))(yf, yb, act, P, dskip, g)


def ssm_out_bwd(yf, yb, act, P, dskip, g, do_src, name):
    T = yf.shape[0]

    def body(yf_r, yb_r, xs_r, z_r, d_r, g_r, do_r, dy_r, dxs_r, dz_r, dv_r):
        @pl.when(pl.program_id(0) == 0)
        def _():
            dv_r[...] = jnp.zeros_like(dv_r)

        _, vjp = jax.vjp(_ssm_out, yf_r[...], yb_r[...], xs_r[...], z_r[...], d_r[...], g_r[...])
        dyf, _, dxs, dz, dd, dg = vjp(do_r[...].astype(F32))
        dy_r[...] = dyf
        dxs_r[...] = dxs
        dz_r[...] = dz.astype(dz_r.dtype)
        dv_r[0:1, :] += dd
        dv_r[1:2, :] += dg

    row = pl.BlockSpec((TR, 512), lambda i: (i, 0))
    vec = pl.BlockSpec((1, 512), lambda i: (0, 0))
    return _pc(body, name=name, grid=(T // TR,),
               in_specs=[row, row, row, pl.BlockSpec((TR, 512), lambda i: (i, C_Z // 512)), vec, vec,
                         pl.BlockSpec((TR, 512), lambda i: (i, 1))],
               out_specs=[row, row, row, pl.BlockSpec((8, 512), lambda i: (0, 0))],
               out_shape=[_sds((T, 512), F32), _sds((T, 512), F32), _sds((T, 512), BF16), _sds((8, 512), F32)],
               compiler_params=_cp(("arbitrary",), 24 << 20))(yf, yb, act, P, dskip, g, do_src)


def add_halves(xv, got, cvec, name):
    n, r, cdim = xv.shape
    h = r // 2

    def body(c_ref, x_ref, g_ref, o_ref):
        o_ref[...] = (x_ref[...].astype(F32) + g_ref[...].astype(F32)).astype(o_ref.dtype)

    gs = pltpu.PrefetchScalarGridSpec(
        num_scalar_prefetch=1, grid=(n,),
        in_specs=[pl.BlockSpec((1, h, cdim), lambda k, c_ref: (k, c_ref[0], 0)), pl.BlockSpec((1, h, cdim), lambda k, c_ref: (k, 0, 0))],
        out_specs=pl.BlockSpec((1, h, cdim), lambda k, c_ref: (k, 0, 0)))
    return _pc(body, name=name, grid_spec=gs, out_shape=_sds((n, h, cdim), BF16),
               compiler_params=_cp(("arbitrary",), 24 << 20))(cvec, xv, got)


def sum_slots(a, name):
    n, r, cdim = a.shape
    tr = _div_tile(r, 512, 16)

    def body(a_ref, o_ref):
        acc = a_ref[0].astype(F32)
        for k in range(1, n):
            acc = acc + a_ref[k].astype(F32)
        o_ref[...] = acc

    return _pc(body, name=name, grid=(r // tr,), in_specs=[pl.BlockSpec((n, tr, cdim), lambda i: (0, i, 0))],
               out_specs=pl.BlockSpec((tr, cdim), lambda i: (i, 0)), out_shape=_sds((r, cdim), F32),
               compiler_params=_cp(("parallel",), 32 << 20))(a)


def adamw(w, g, m, v, name):
    B, R, C = w.shape
    tr = _div_tile(R, max(8, (1 << 19) // max(C, 1) // 8 * 8), 8) if R % 8 == 0 else R
    c1 = 1.0 / (1.0 - ADAM_B1 ** ADAM_STEP)
    c2 = 1.0 / (1.0 - ADAM_B2 ** ADAM_STEP)

    def body(w_ref, g_ref, m_ref, v_ref, d_ref, mo_ref, vo_ref):
        gg = g_ref[...]
        mn = ADAM_B1 * m_ref[...] + (1.0 - ADAM_B1) * gg
        vn = ADAM_B2 * v_ref[...] + (1.0 - ADAM_B2) * (gg * gg)
        d_ref[...] = -ADAM_LR * ((mn * c1) / (jnp.sqrt(vn * c2) + ADAM_EPS) + ADAM_WD * w_ref[...])
        mo_ref[...] = mn
        vo_ref[...] = vn

    spec = pl.BlockSpec((1, tr, C), lambda b, i: (b, i, 0))
    return _pc(body, name=name, grid=(B, R // tr), in_specs=[spec] * 4, out_specs=[spec] * 3,
               out_shape=[_sds((B, R, C), F32)] * 3, compiler_params=_cp(("parallel", "parallel"), 32 << 20))(w, g, m, v)


def _me():
    return lax.axis_index("x"), lax.axis_index("y"), lax.axis_index("c")


def _flip(v, bit):
    return 1 - v if bit else v


def allgather8(xv, name):
    R = xv.shape[0]

    def body(x_ref, out_ref, sum_ref, send_sems, recv_sems):
        mx, my, mc = _me()
        me = 4 * mx + 2 * my + mc
        out_ref[me] = x_ref[...]
        sends, recvs = [], []
        for k in range(1, 8):
            px, py, pc = _flip(mx, k & 4), _flip(my, k & 2), _flip(mc, k & 1)
            peer = 4 * px + 2 * py + pc
            sends.append(pltpu.make_async_remote_copy(src_ref=x_ref, dst_ref=out_ref.at[me], send_sem=send_sems.at[k - 1],
                                                      recv_sem=recv_sems.at[k - 1], device_id=(px, py, pc), device_id_type=MESH))
            recvs.append(pltpu.make_async_remote_copy(src_ref=x_ref, dst_ref=out_ref.at[peer], send_sem=send_sems.at[k - 1],
                                                      recv_sem=recv_sems.at[k - 1], device_id=(px, py, pc), device_id_type=MESH))
        for cp in sends:
            cp.start()
        for cp in recvs:
            cp.wait_recv()
        for cp in sends:
            cp.wait_send()
        acc = out_ref[0]
        for d in range(1, 8):
            acc = acc + out_ref[d]
        sum_ref[...] = acc

    vm = pl.BlockSpec(memory_space=pltpu.VMEM)
    return _pc(body, name=name, pin=False, in_specs=[vm], out_specs=[vm, vm], out_shape=[_sds((8, R, 128), F32), _sds((R, 128), F32)],
               scratch_shapes=[pltpu.SemaphoreType.DMA((7,)), pltpu.SemaphoreType.DMA((7,))],
               compiler_params=_cp(None, 32 << 20))(xv)


def _other_chips(mx, my):
    return [(1 - mx, my), (mx, 1 - my), (1 - mx, 1 - my)]


def _halves(r, mc, mult):
    h = r // 2
    return pl.ds(pl.multiple_of(mc * h, mult), h), pl.ds(pl.multiple_of((1 - mc) * h, mult), h)


def _rcopy(src, dst, send_sems, recv_sems, k, to):
    return pltpu.make_async_remote_copy(src_ref=src, dst_ref=dst, send_sem=send_sems.at[k], recv_sem=recv_sems.at[k],
                                        device_id=to, device_id_type=MESH)


def _gather_body(xs, outs, send_sems, recv_sems):
    n = len(xs)
    mx, my, mc = _me()
    chip = 2 * mx + my
    sib = (mx, my, 1 - mc)
    chips = _other_chips(mx, my)
    idx = [2 * cx + cy for cx, cy in chips]
    cp = functools.partial(_rcopy, send_sems=send_sems, recv_sems=recv_sems)
    hv = [_halves(x.shape[0], mc, 16) for x in xs]
    first, passed = [], []
    for a in range(n):
        for j, (cx, cy) in enumerate(chips):
            first.append(cp(xs[a].at[hv[a][0]], outs[a].at[chip, hv[a][0]], k=6 * a + j, to=(cx, cy, mc)))
            first[-1].start()
    for a in range(n):
        for j in range(3):
            cp(xs[a].at[hv[a][0]], outs[a].at[idx[j], hv[a][0]], k=6 * a + j, to=sib).wait_recv()
            passed.append(cp(outs[a].at[idx[j], hv[a][0]], outs[a].at[idx[j], hv[a][0]], k=6 * a + 3 + j, to=sib))
            passed[-1].start()
    for a in range(n):
        for j in range(3):
            cp(xs[a].at[hv[a][1]], outs[a].at[idx[j], hv[a][1]], k=6 * a + 3 + j, to=sib).wait_recv()
    for c_ in first + passed:
        c_.wait_send()


def _my_chip():
    return 2 * lax.axis_index("x") + lax.axis_index("y")


def _own_slots(outs, shards):
    return [lax.dynamic_update_index_in_dim(o, x, _my_chip(), 0) for o, x in zip(outs, shards)]


def gather_weights(shards, name):
    n = len(shards)

    def body(*refs):
        _gather_body(refs[:n], refs[n:2 * n], *refs[2 * n:])

    hbm = pl.BlockSpec(memory_space=pl.ANY)
    outs = _pc(body, name=name, in_specs=[hbm] * n, out_specs=[hbm] * n, out_shape=[_sds((4,) + x.shape, x.dtype) for x in shards],
               scratch_shapes=[pltpu.SemaphoreType.DMA((6 * n,)), pltpu.SemaphoreType.DMA((6 * n,))])(*shards)
    return _own_slots(outs, shards)


GATHER_REST_ID = 3


def gather_weights_sc(shards, name):
    n = len(shards)
    x_refs = [jax.new_ref(x, memory_space=pltpu.MemorySpace.HBM) for x in shards]
    out_refs = [jax.empty_ref(_sds((4,) + x.shape, x.dtype), memory_space=pltpu.MemorySpace.HBM) for x in shards]

    @pl.kernel(mesh=plsc.ScalarSubcoreMesh(axis_name="sc", num_cores=1), name=name,
               scratch_types=(pltpu.SemaphoreType.DMA((6 * n,)), pltpu.SemaphoreType.DMA((6 * n,))),
               compiler_params=pltpu.CompilerParams(collective_id=GATHER_REST_ID))
    def launch(send_sems, recv_sems):
        mx, my, mc = _me()
        barrier = pltpu.get_barrier_semaphore()
        for peer in [(mx, my, 1 - mc)] + [(cx, cy, mc) for cx, cy in _other_chips(mx, my)]:
            pl.semaphore_signal(barrier, inc=1, device_id=peer, device_id_type=MESH)
        pl.semaphore_wait(barrier, 4)
        _gather_body(x_refs, out_refs, send_sems, recv_sems)

    launch()
    return _own_slots([o[...] for o in out_refs], shards)


def swap_halves(arrs, name):
    n = len(arrs)

    def body(*refs):
        xs, outs = refs[:n], refs[n:2 * n]
        send_sems, recv_sems = refs[2 * n:]
        mx, my, mc = _me()
        cps = []
        for a in range(n):
            theirs = _halves(xs[a].shape[1], mc, 16)[1]
            cps.append(_rcopy(xs[a].at[pl.ds(0, 4), theirs], outs[a], send_sems, recv_sems, a, (mx, my, 1 - mc)))
            cps[-1].start()
        for c_ in cps:
            c_.wait()

    hbm = pl.BlockSpec(memory_space=pl.ANY)
    return _pc(body, name=name, in_specs=[hbm] * n, out_specs=[hbm] * n,
               out_shape=[_sds((4, x.shape[1] // 2, x.shape[2]), x.dtype) for x in arrs],
               scratch_shapes=[pltpu.SemaphoreType.DMA((n,)), pltpu.SemaphoreType.DMA((n,))])(*arrs)


SCATTER_ID = 4


def scatter_chips_sc(arrs, name):
    n = len(arrs)
    x_refs = [jax.new_ref(x, memory_space=pltpu.MemorySpace.HBM) for x in arrs]
    out_refs = [jax.empty_ref(_sds(x.shape, x.dtype), memory_space=pltpu.MemorySpace.HBM) for x in arrs]

    @pl.kernel(mesh=plsc.ScalarSubcoreMesh(axis_name="sc", num_cores=1), name=name,
               scratch_types=(pltpu.SemaphoreType.DMA((3 * n,)), pltpu.SemaphoreType.DMA((3 * n,))),
               compiler_params=pltpu.CompilerParams(collective_id=SCATTER_ID))
    def launch(send_sems, recv_sems):
        mx, my, mc = _me()
        chip = 2 * mx + my
        chips = _other_chips(mx, my)
        idx = [2 * cx + cy for cx, cy in chips]
        barrier = pltpu.get_barrier_semaphore()
        for cx, cy in chips:
            pl.semaphore_signal(barrier, inc=1, device_id=(cx, cy, mc), device_id_type=MESH)
        pl.semaphore_wait(barrier, 3)
        cp = functools.partial(_rcopy, send_sems=send_sems, recv_sems=recv_sems)
        sends = []
        for a in range(n):
            for j, (cx, cy) in enumerate(chips):
                sends.append(cp(x_refs[a].at[idx[j]], out_refs[a].at[chip], k=3 * a + j, to=(cx, cy, mc)))
                sends[-1].start()
        for a in range(n):
            for j, (cx, cy) in enumerate(chips):
                cp(x_refs[a].at[idx[j]], out_refs[a].at[idx[j]], k=3 * a + j, to=(cx, cy, mc)).wait_recv()
        for c_ in sends:
            c_.wait_send()

    launch()
    return _own_slots([o[...] for o in out_refs], [lax.dynamic_index_in_dim(x, _my_chip(), 0, keepdims=False) for x in arrs])


def share_halves(parts, name):
    flat = [p for w in parts for p in w]
    nw, n = len(parts), len(flat)
    depth = n // nw

    def body(*refs):
        xs, outs = refs[:n], refs[n:n + nw]
        send_sems, recv_sems = refs[n + nw:]
        mx, my, mc = _me()
        sib = (mx, my, 1 - mc)
        sends, recvs = [], []
        for a in range(n):
            w, l = a // depth, a % depth
            mine, theirs = _halves(outs[w].shape[1], mc, 8)
            sends.append(_rcopy(xs[a], outs[w].at[l, mine], send_sems, recv_sems, a, sib))
            recvs.append(_rcopy(xs[a], outs[w].at[l, theirs], send_sems, recv_sems, a, sib))
            sends[-1].start()
        for c_ in recvs:
            c_.wait_recv()
        for c_ in sends:
            c_.wait_send()

    hbm = pl.BlockSpec(memory_space=pl.ANY)
    outs = _pc(body, name=name, in_specs=[hbm] * n, out_specs=[hbm] * nw,
               out_shape=[_sds((depth, 2 * w[0].shape[0], w[0].shape[1]), F32) for w in parts],
               scratch_shapes=[pltpu.SemaphoreType.DMA((n,)), pltpu.SemaphoreType.DMA((n,))])(*flat)
    outs = list(outs)
    mc = lax.axis_index("c")
    for w in range(nw):
        for l in range(depth):
            h = parts[w][l].shape[0]
            outs[w] = lax.dynamic_update_slice(outs[w], parts[w][l][None], (l, mc * h, 0))
    return outs


_BIG = ("w_in", "w_out", "w_ffn_in", "w_ffn_out")
N_CHIPS = 4
DEPTH = 2


def _pad_rows(v, mult=8):
    n = v.shape[0]
    rows = -(-n // 128)
    rows = -(-rows // mult) * mult
    return jnp.pad(v, (0, rows * 128 - n)).reshape(rows, 128)


class _Flat:
    def __init__(self):
        self.items = []

    def add(self, name, a):
        self.items.append((name, a.shape, a.reshape(-1).astype(F32)))

    def rows(self):
        return _pad_rows(jnp.concatenate([a for _, _, a in self.items]))

    def split(self, rows):
        flat = rows.reshape(-1)
        out, o = {}, 0
        for name, shape, a in self.items:
            out[name] = flat[o:o + a.shape[0]].reshape(shape)
            o += a.shape[0]
        return out

    def split_lead(self, rows3):
        n = rows3.shape[0]
        flat = rows3.reshape(n, -1)
        out, o = {}, 0
        for name, shape, a in self.items:
            out[name] = flat[:, o:o + a.shape[0]].reshape((n,) + tuple(shape))
            o += a.shape[0]
        return out


def _gsv(rows):
    z = jnp.zeros((2, D), F32)
    r = [z if a is None else a for a in rows] + [z] * 5
    return jnp.stack(r, axis=1)


def _pad8(a, rows=8, cols=128):
    return jnp.zeros((rows, cols), F32).at[:a.shape[0], :a.shape[1]].set(a.astype(F32))


def kernel(x, c, ctx, c_ctx, w_mod, b_mod, g_mix, w_in, wa_sink, na_rpb, ssm_conv_w, ssm_conv_b, ssm_dt_bias, ssm_a_log, ssm_d, ssm_norm_g, w_out, g_ffn, w_ffn_in, w_ffn_out, g_final, loss_target, m_c_ctx, m_w_mod, m_b_mod, m_g_mix, m_w_in, m_wa_sink, m_na_rpb, m_ssm_conv_w, m_ssm_conv_b, m_ssm_dt_bias, m_ssm_a_log, m_ssm_d, m_ssm_norm_g, m_w_out, m_g_ffn, m_w_ffn_in, m_w_ffn_out, m_g_final, v_c_ctx, v_w_mod, v_b_mod, v_g_mix, v_w_in, v_wa_sink, v_na_rpb, v_ssm_conv_w, v_ssm_conv_b, v_ssm_dt_bias, v_ssm_a_log, v_ssm_d, v_ssm_norm_g, v_w_out, v_g_ffn, v_w_ffn_in, v_w_ffn_out, v_g_final):
    L, Lc = x.shape[1], ctx.shape[1]
    T = L + Lc
    nL = L // TR
    mx, my, mc = lax.axis_index("x"), lax.axis_index("y"), lax.axis_index("c")
    dev = 4 * mx + 2 * my + mc
    chip = 2 * mx + my
    MODW = 6 * D // N_CHIPS
    CW = 1024 // N_CHIPS

    sc = _silu(c.astype(F32))
    scc = _silu(c_ctx.astype(F32))[None]
    f1 = _Flat()
    f1.add("sc", sc)
    f1.add("conv_w", ssm_conv_w)
    g1, _ = allgather8(f1.rows(), "gather_cond")
    g1 = f1.split_lead(g1)
    sc_all = g1["sc"][:, 0]
    conv_w = jnp.concatenate([g1["conv_w"][2 * k] for k in range(N_CHIPS)], axis=-1)
    A16 = jnp.concatenate([sc_all, scc, jnp.zeros((7, D), F32)], axis=0)

    mod_part = matmul_layers(A16, w_mod, "nn", "mod_fwd")
    f2 = _Flat()
    f2.add("mod", mod_part)
    g2, _ = allgather8(f2.rows(), "gather_mod")
    g2 = f2.split_lead(g2)["mod"]
    mods = jnp.concatenate([g2[2 * k] for k in range(N_CHIPS)], axis=-1) + b_mod[:, None, :]
    mod_l = lax.dynamic_index_in_dim(mods, dev, axis=1, keepdims=False).reshape(DEPTH, 6, D)
    mod_c = mods[:, 8].reshape(DEPTH, 6, D)
    mod = jnp.stack([mod_l, mod_c], axis=1)
    mrow = lambda l, j: mod[l, :, j]

    own = {"w_in": w_in, "w_out": w_out, "w_ffn_in": w_ffn_in, "w_ffn_out": w_ffn_out}
    sh16 = [own[n][l].astype(BF16) for n in _BIG for l in range(DEPTH)]
    after_mod = (g2[0, 0, 0, 0] * 0).astype(BF16)
    gath = list(gather_weights([sh16[0] + after_mod], "gather_first"))
    after_first = (gath[0][0, 0, 0] * 0).astype(BF16)
    gath += list(gather_weights_sc([sh16[1] + after_first] + sh16[2:], "gather_rest"))
    gw = {n: [gath[DEPTH * i + l] for l in range(DEPTH)] for i, n in enumerate(_BIG)}
    W_in = [jnp.pad(jnp.concatenate([g[k] for k in range(N_CHIPS)], axis=1), ((0, 0), (0, IN_PAD - IN_COLS))) for g in gw["w_in"]]
    W_out = [g.reshape(D, D) for g in gw["w_out"]]
    W_fo = [g.reshape(D_FF, D) for g in gw["w_ffn_out"]]
    W_fi = gw["w_ffn_in"]

    cos, sin = rope_tables(L, Lc)
    x0 = jnp.concatenate([x[0], ctx[0]], axis=0).astype(F32)

    sv = []
    xin = x0
    gsv_first = _gsv([None, mrow(0, 0), mrow(0, 1)])
    _, h1 = res_norm_mod(x0, None, gsv_first, g_mix[0][None], nL, "norm_first")
    for l in range(DEPTH):
        s = {"xin": xin, "h1": h1}
        P = matmul(h1, W_in[l], "nn", F32, f"in_proj{l}", tn=IN_PAD)
        qr, kr, kb, vb = rope_apply(P, C_QA // 256, P, C_KA // 128, cos, sin, False, f"rope{l}", kv_src=P)
        sink8 = _pad8(jnp.broadcast_to(wa_sink[l][:, None], (WA_HEADS, 128)))
        krs, va = _swap_halves_lanes(kr), P[:, C_VA:C_VA + 128]
        vas = _swap_halves_lanes(va)
        oa, sta = win_attn_fwd(qr, kr, krs, va, vas, sink8, L, Lc, f"wa_fwd{l}")
        bias = na_bias_table(na_rpb[l], l)
        ob, stb = na_fwd(P, kb, vb, bias, L, Lc, f"na_fwd{l}")
        w8 = jnp.concatenate([conv_w[l], jnp.zeros((1, 1024), F32)], axis=0)
        pre, act = conv_silu_fwd(P, w8, ssm_conv_b[l][None], nL, f"conv_fwd{l}")
        dtb8, al8 = _pad8(ssm_dt_bias[l]), _pad8(ssm_a_log[l])
        yf, yb, hsf, hsb = ssd_fwd(act, P, dtb8, al8, L, Lc, f"ssd_fwd{l}")
        dskip = jnp.repeat(ssm_d[l], S_P)[None]
        oc = ssm_out_fwd(yf, yb, act, P, dskip, ssm_norm_g[l][None], f"ssm_out_fwd{l}")
        mixin = [(oa, 0), (ob, 256), (oc, 512)]
        mix = out_proj_fwd(mixin, W_out[l], f"out_proj{l}")
        gsv_mid = _gsv([mrow(l, 2), mrow(l, 3), mrow(l, 4)])
        x1, h2 = res_norm_mod(xin, mix, gsv_mid, g_ffn[l][None], nL, f"norm_mid{l}")
        gu, af = ffn_in_swiglu(h2, W_fi[l], f"ffn_in{l}")
        fo = matmul(af, W_fo[l], "nn", BF16, f"ffn_out{l}", tk=D_FF)
        s.update(P=P, qr=qr, kr=kr, krs=krs, va=va, vas=vas, sink8=sink8, oa=oa, sta=sta, ob=ob, stb=stb, kb=kb, vb=vb, bias=bias, w8=w8, pre=pre, act=act, dtb8=dtb8, al8=al8, yf=yf,
                 yb=yb, hsf=hsf, hsb=hsb, dskip=dskip, mixin=mixin, mix=mix, gsv_mid=gsv_mid, x1=x1, h2=h2, gu=gu, af=af, fo=fo)
        if l + 1 < DEPTH:
            s["gsv_end"] = _gsv([mrow(l, 5), mrow(l + 1, 0), mrow(l + 1, 1)])
            xin, h1 = res_norm_mod(x1, fo, s["gsv_end"], g_mix[l + 1][None], nL, f"norm_end{l}")
        else:
            s["gsv_end"] = _gsv([mrow(l, 5), None, None])
        sv.append(s)

    last = sv[-1]
    loss8, dres, dfo, dgsv_end, dg_final = final_loss(last["x1"], last["fo"], last["gsv_end"], g_final[None], loss_target[0].astype(F32), nL, "final_loss")
    loss = lax.psum(loss8[0, 0], ("x", "y", "c"))

    dmod = [[None] * 6 for _ in range(DEPTH)]
    gW = {n: [None] * DEPTH for n in _BIG}
    small = [dict() for _ in range(DEPTH)]
    parts = [None] * DEPTH
    cvec = mc.astype(jnp.int32).reshape(1)
    grad_x = None
    for l in reversed(range(DEPTH)):
        s = sv[l]
        dmod[l][5] = dgsv_end[:, 0]
        if l + 1 < DEPTH:
            dmod[l + 1][0], dmod[l + 1][1] = dgsv_end[:, 1], dgsv_end[:, 2]
        dgu = ffn_out_dx_swiglu(dfo, W_fo[l], s["gu"], f"ffn_out_dx{l}")
        gW["w_ffn_out"][l] = matmul(s["af"], dfo, "tn", BF16, f"ffn_out_dw{l}", tm=1408, tk=T).reshape(N_CHIPS, D_FF // N_CHIPS, D)
        dh2 = matmul_fi(dgu, W_fi[l], "nt", BF16, f"ffn_in_dx{l}")
        gW["w_ffn_in"][l] = matmul_fi(s["h2"], dgu, "tn", BF16, f"ffn_in_dw{l}")
        dres, dmix, dgsv_mid, dg_ffn = res_norm_mod_bwd(s["x1"], s["mix"], s["gsv_mid"], g_ffn[l][None], dh2, dres, nL, f"norm_mid_bwd{l}")
        dmod[l][2], dmod[l][3], dmod[l][4] = dgsv_mid[:, 0], dgsv_mid[:, 1], dgsv_mid[:, 2]
        dmixin = matmul(dmix, W_out[l], "nt", BF16, f"out_proj_dx{l}")
        gW["w_out"][l] = out_proj_dw(s["mixin"], dmix, f"out_proj_dw{l}").reshape(N_CHIPS, D // N_CHIPS, D)
        P = s["P"]
        dqr, dkr, dkrs, dva, dvas, dsink = win_attn_bwd(s["qr"], s["kr"], s["krs"], s["va"], s["vas"], s["sink8"], dmixin, s["oa"], s["sta"], L, Lc,
                                                        f"wa_bwd{l}")
        dkr, dva = dkr + _swap_halves_lanes(dkrs), dva + _swap_halves_lanes(dvas)
        dqa, dka = rope_apply(dqr, 0, dkr[WA_BLK:WA_BLK + T], 0, cos, sin, True, f"rope_bwd{l}")
        dqb, dkb, dvb, dbias = na_bwd(P, s["kb"], s["vb"], s["bias"], dmixin, s["ob"], s["stb"], L, Lc, f"na_bwd{l}")
        dy, dxs1, dz, dvec = ssm_out_bwd(s["yf"], s["yb"], s["act"], P, s["dskip"], ssm_norm_g[l][None], dmixin, f"ssm_out_bwd{l}")
        dxf, dbf, dcf, ddf, dxb, dbb, dcb, ddb, ddtb, dal = ssd_bwd(s["act"], P, s["dtb8"], s["al8"], s["hsf"], s["hsb"], dy, L, Lc, f"ssd_bwd{l}")
        dpre = dsilu(s["pre"], [dxf, dxb, dxs1], [dbf, dbb], [dcf, dcb], f"dsilu{l}")
        dxbc, dw8, db8 = conv_bwd(dpre, P, s["w8"], nL, f"conv_bwd{l}")
        ddt = jnp.concatenate([ddf, ddb, jnp.zeros((T, IN_PAD - IN_COLS), F32)], axis=1)
        pieces = [(dqa, C_QA), (dqb, C_QB), (dz, C_Z), (dka, C_KA), (dva[WA_BLK:WA_BLK + T], C_VA), (dkb, C_KB), (dvb, C_VB),
                  (dxbc, C_XBC), (ddt, C_DT)]
        dh1, dwin = in_proj_bwd(pieces, s["h1"], W_in[l], f"in_proj_bwd{l}")
        cw = IN_COLS // N_CHIPS
        gW["w_in"][l] = jnp.stack([dwin[:, k * cw:(k + 1) * cw] for k in range(N_CHIPS)])
        garr = [gW[n][l] for n in _BIG]
        got = swap_halves(garr, f"reduce_d2d{l}")
        chip_sum = [add_halves(garr[a], got[a], cvec, f"reduce_add_pair{l}_{a}") for a in range(len(garr))]
        parts[l] = scatter_chips_sc(chip_sum, f"reduce_ici{l}")
        small[l] = dict(g_ffn=dg_ffn[0], wa_sink=dsink[:WA_HEADS, 0], na_rpb=na_rpb_grad(dbias, l), conv_w=dw8[:S_CONV], conv_b=db8[0],
                        dt_bias=ddtb[:2, :8], a_log=dal[:2, :8], ssm_d=dvec[0].reshape(S_HEADS, S_P).sum(axis=1), norm_g=dvec[1])
        if l > 0:
            p = sv[l - 1]
            dres, dfo, dgsv_end, dg_mix = res_norm_mod_bwd(s["xin"], p["fo"], p["gsv_end"], g_mix[l][None], dh1, dres, nL, f"norm_end_bwd{l - 1}")
        else:
            grad_x, _, dgsv_first, dg_mix = res_norm_mod_bwd(s["xin"], None, gsv_first, g_mix[0][None], dh1, dres, nL, "norm_first_bwd")
            dmod[0][0], dmod[0][1] = dgsv_first[:, 1], dgsv_first[:, 2]
        small[l]["g_mix"] = dg_mix[0]
    for l in range(DEPTH):
        for j in range(6):
            if dmod[l][j] is None:
                dmod[l][j] = jnp.zeros((2, D), F32)
    dmod = jnp.stack([jnp.stack(r, axis=1) for r in dmod])

    f3 = _Flat()
    f3.add("dmod_l", dmod[:, 0].reshape(DEPTH, 6 * D))
    f3.add("dmod_c", dmod[:, 1].reshape(DEPTH, 6 * D))
    f3.add("g_final", dg_final[0])
    for n in ("g_mix", "g_ffn", "wa_sink", "na_rpb", "conv_w", "conv_b", "dt_bias", "a_log", "ssm_d", "norm_g"):
        f3.add(n, jnp.stack([small[l][n] for l in range(DEPTH)]))
    g3, s3 = allgather8(f3.rows(), "reduce_small")
    dmod_all = f3.split_lead(g3)["dmod_l"]
    s3 = f3.split(s3)
    dmodc_tot = s3["dmod_c"]
    col0 = chip * MODW
    G16, G16c = [], []
    for l in range(DEPTH):
        rows = jnp.concatenate([dmod_all[:, l], dmodc_tot[l][None], jnp.zeros((7, 6 * D), F32)], axis=0)
        G16.append(lax.dynamic_slice_in_dim(rows, col0, MODW, axis=1))
        rc = jnp.concatenate([dmodc_tot[l][None], jnp.zeros((15, 6 * D), F32)], axis=0)
        G16c.append(lax.dynamic_slice_in_dim(rc, col0, MODW, axis=1))
    grad_w_mod = matmul_layers(A16, jnp.stack(G16), "tn", "mod_dw")
    dscc_part = matmul_layers(jnp.stack(G16c), w_mod, "nt", "mod_dx")[:, 0].sum(axis=0)
    _, s4 = allgather8(_pad_rows(dscc_part * (mc == 1).astype(F32)), "reduce_cctx")
    dscc = s4.reshape(-1)[:D]
    cc = c_ctx.astype(F32)
    sg = 1.0 / (1.0 + jnp.exp(-cc))
    grad_c_ctx = dscc * (sg * (1.0 + cc * (1.0 - sg)))

    halves = [[sum_slots(parts[l][i], f"reduce_add_chips{l}_{i}") for l in range(DEPTH)] for i in range(len(_BIG))]
    gsh = dict(zip(_BIG, share_halves(halves, "reduce_share")))

    grads = {"c_ctx": grad_c_ctx, "w_mod": grad_w_mod, "b_mod": s3["dmod_l"] + s3["dmod_c"], "g_mix": s3["g_mix"], "w_in": gsh["w_in"],
             "wa_sink": s3["wa_sink"], "na_rpb": s3["na_rpb"],
             "ssm_conv_w": lax.dynamic_slice_in_dim(s3["conv_w"], chip * CW, CW, axis=2), "ssm_conv_b": s3["conv_b"],
             "ssm_dt_bias": s3["dt_bias"], "ssm_a_log": s3["a_log"], "ssm_d": s3["ssm_d"], "ssm_norm_g": s3["norm_g"],
             "w_out": gsh["w_out"], "g_ffn": s3["g_ffn"], "w_ffn_in": gsh["w_ffn_in"], "w_ffn_out": gsh["w_ffn_out"], "g_final": s3["g_final"]}
    wts = {"c_ctx": c_ctx, "w_mod": w_mod, "b_mod": b_mod, "g_mix": g_mix, "w_in": w_in, "wa_sink": wa_sink, "na_rpb": na_rpb,
           "ssm_conv_w": ssm_conv_w, "ssm_conv_b": ssm_conv_b, "ssm_dt_bias": ssm_dt_bias, "ssm_a_log": ssm_a_log, "ssm_d": ssm_d,
           "ssm_norm_g": ssm_norm_g, "w_out": w_out, "g_ffn": g_ffn, "w_ffn_in": w_ffn_in, "w_ffn_out": w_ffn_out, "g_final": g_final}
    ms = {"c_ctx": m_c_ctx, "w_mod": m_w_mod, "b_mod": m_b_mod, "g_mix": m_g_mix, "w_in": m_w_in, "wa_sink": m_wa_sink, "na_rpb": m_na_rpb,
          "ssm_conv_w": m_ssm_conv_w, "ssm_conv_b": m_ssm_conv_b, "ssm_dt_bias": m_ssm_dt_bias, "ssm_a_log": m_ssm_a_log, "ssm_d": m_ssm_d,
          "ssm_norm_g": m_ssm_norm_g, "w_out": m_w_out, "g_ffn": m_g_ffn, "w_ffn_in": m_w_ffn_in, "w_ffn_out": m_w_ffn_out, "g_final": m_g_final}
    vs = {"c_ctx": v_c_ctx, "w_mod": v_w_mod, "b_mod": v_b_mod, "g_mix": v_g_mix, "w_in": v_w_in, "wa_sink": v_wa_sink, "na_rpb": v_na_rpb,
          "ssm_conv_w": v_ssm_conv_w, "ssm_conv_b": v_ssm_conv_b, "ssm_dt_bias": v_ssm_dt_bias, "ssm_a_log": v_ssm_a_log, "ssm_d": v_ssm_d,
          "ssm_norm_g": v_ssm_norm_g, "w_out": v_w_out, "g_ffn": v_g_ffn, "w_ffn_in": v_w_ffn_in, "w_ffn_out": v_w_ffn_out, "g_final": v_g_final}
    names = list(wts)
    grads = {n: grads[n].reshape(wts[n].shape).astype(F32) for n in names}
    big = ("w_mod", "w_in", "w_out", "w_ffn_in", "w_ffn_out")
    delta, new_m, new_v = {}, {}, {}
    for n in big:
        delta[n], new_m[n], new_v[n] = adamw(wts[n], grads[n], ms[n], vs[n], f"adamw_{n}")
    packs = []
    for src in (wts, grads, ms, vs):
        f = _Flat()
        for n in names:
            if n not in big:
                f.add(n, src[n])
        packs.append(f)
    d_, m_, v_ = adamw(*[f.rows()[None] for f in packs], "adamw_small")
    for dst, rows in ((delta, d_), (new_m, m_), (new_v, v_)):
        dst.update(packs[0].split(rows[0]))

    return (loss, grad_x[:L][None], *[grads[n] for n in names], *[delta[n] for n in names],
            *[new_m[n] for n in names], *[new_v[n] for n in names])
```

```python
import functools

import numpy as np
import jax
import jax.numpy as jnp
from jax import lax
from jax.experimental import pallas as pl
from jax.experimental.pallas import tpu as pltpu
from jax.experimental.pallas import tpu_sc as plsc

F32 = jnp.float32
BF16 = jnp.bfloat16
_MXU = jnp.bfloat16
_HI = lax.Precision.HIGHEST
MESH = pl.DeviceIdType.MESH

D = 1024
HD = 64
GRID_W = 64
EPS = 1e-6
ROPE_BASE = 10000.0
WA_HEADS, WA_KV = 4, 2
WA_BLK = 128
NA_HEADS, NA_KH, NA_KW = 4, 8, 16
S_HEADS, S_P, S_INNER, S_GROUPS, S_N, S_CONV, S_Q = 8, 64, 512, 2, 128, 7, 128
D_FF = 2816
IN_COLS = 2832
IN_PAD = 2944
C_QA, C_QB, C_Z, C_KA, C_VA, C_KB, C_VB, C_XBC, C_DT = 0, 256, 512, 1024, 1152, 1280, 1536, 1792, 2816
ADAM_LR, ADAM_B1, ADAM_B2, ADAM_EPS, ADAM_WD, ADAM_STEP = 0.001, 0.9, 0.999, 1e-08, 0.01, 10

TR = 256
NEG = -1e30
VMEM_CAP = 56 * 1024 * 1024


PIN_BYTES = 256 * 1024


def _is_big(a):
    return hasattr(a, "shape") and len(a.shape) >= 2 and int(np.prod(a.shape)) * jnp.dtype(a.dtype).itemsize >= PIN_BYTES


def _pc(body, *, out_shape, pin=True, **kw):
    if not pin:
        return pl.pallas_call(body, out_shape=out_shape, **kw)
    one = isinstance(out_shape, jax.ShapeDtypeStruct)
    outs = [pltpu.HBM(s.shape, s.dtype) if _is_big(s) else s for s in ([out_shape] if one else out_shape)]
    call = pl.pallas_call(body, out_shape=outs[0] if one else outs, **kw)
    return lambda *args: call(*[pltpu.with_memory_space_constraint(a, pltpu.HBM) if _is_big(a) else a for a in args])


def _cp(sem=None, vmem=None):
    kw = {}
    if sem is not None:
        kw["dimension_semantics"] = sem
    if vmem is not None:
        kw["vmem_limit_bytes"] = int(min(max(vmem, 16 * 1024 * 1024), VMEM_CAP))
    return pltpu.CompilerParams(**kw)


def _sds(shape, dtype):
    return jax.ShapeDtypeStruct(tuple(shape), dtype)


_DIMS = {"nn": ((1,), (0,)), "nt": ((1,), (1,)), "tn": ((0,), (0,))}


def _dg(a, b, dims):
    return lax.dot_general(a.astype(_MXU), b.astype(_MXU), (dims, ((), ())), preferred_element_type=F32)


@functools.partial(jax.custom_vjp, nondiff_argnums=(2,))
def bdot(a, b, mode):
    return _dg(a, b, _DIMS[mode])


def _bdot_fwd(a, b, mode):
    return bdot(a, b, mode), (a, b)


def _bdot_bwd(mode, res, g):
    a, b = res
    if mode == "nn":
        return bdot(g, b, "nt"), bdot(a, g, "tn")
    if mode == "nt":
        return bdot(g, b, "nn"), bdot(g, a, "tn")
    return bdot(b, g, "nt"), bdot(a, g, "nn")


bdot.defvjp(_bdot_fwd, _bdot_bwd)


def hdot(a, b, mode="nn"):
    return lax.dot_general(a, b, (_DIMS[mode], ((), ())), precision=_HI, preferred_element_type=F32)


def _silu(x):
    return x / (1.0 + jnp.exp(-x))


def _softplus(x):
    return jnp.maximum(x, 0.0) + jnp.log(1.0 + jnp.exp(-jnp.abs(x)))


def _div_tile(n, cap, mult):
    if n <= cap:
        return n
    best = None
    for t in range(mult, cap + 1, mult):
        if n % t == 0:
            best = t
    assert best is not None, (n, cap, mult)
    return best


def matmul(a, b, mode, out_dtype, name, tm=640, tn=1536, tk=1408, hi=False):
    if mode == "tn":
        K, M = a.shape
    else:
        M, K = a.shape
    N = b.shape[0] if mode == "nt" else b.shape[1]
    tm = _div_tile(M, tm, 128 if mode == "tn" else 16)
    tn = _div_tile(N, tn, 128)
    tk = _div_tile(K, tk, 128 if mode != "tn" else 16)
    nk = K // tk
    dims = _DIMS[mode]

    def body(a_ref, b_ref, o_ref, *acc):
        if hi:
            part = lax.dot_general(a_ref[...], b_ref[...], (dims, ((), ())), precision=_HI, preferred_element_type=F32)
        else:
            part = _dg(a_ref[...], b_ref[...], dims)
        if nk == 1:
            o_ref[...] = part.astype(o_ref.dtype)
        else:
            k = pl.program_id(2)

            @pl.when(k == 0)
            def _():
                acc[0][...] = part

            @pl.when(k > 0)
            def _():
                acc[0][...] += part

            @pl.when(k == nk - 1)
            def _():
                o_ref[...] = acc[0][...].astype(o_ref.dtype)

    if mode == "tn":
        a_spec = pl.BlockSpec((tk, tm), lambda i, j, k: (k, i))
    else:
        a_spec = pl.BlockSpec((tm, tk), lambda i, j, k: (i, k))
    if mode == "nt":
        b_spec = pl.BlockSpec((tn, tk), lambda i, j, k: (j, k))
    else:
        b_spec = pl.BlockSpec((tk, tn), lambda i, j, k: (k, j))
    isz = lambda x: jnp.dtype(x.dtype).itemsize
    vmem = 2 * (tm * tk * isz(a) + tk * tn * isz(b) + tm * tn * jnp.dtype(out_dtype).itemsize) + 3 * tm * tn * 4
    return _pc(
        body, name=name, grid=(M // tm, N // tn, nk),
        in_specs=[a_spec, b_spec], out_specs=pl.BlockSpec((tm, tn), lambda i, j, k: (i, j)),
        out_shape=_sds((M, N), out_dtype),
        scratch_shapes=[pltpu.VMEM((tm, tn), F32)] if nk > 1 else [],
        compiler_params=_cp(("parallel", "parallel", "arbitrary"), vmem + (8 << 20)),
    )(a, b)


def matmul_layers(a, b, mode, name):
    nl = b.shape[0]
    a3 = a if a.ndim == 3 else a[None]
    shared = a3.shape[0] == 1
    M = a3.shape[2] if mode == "tn" else a3.shape[1]
    N = b.shape[1] if mode == "nt" else b.shape[2]

    def body(a_ref, b_ref, o_ref):
        o_ref[0] = _dg(a_ref[0], b_ref[0], _DIMS[mode])

    return _pc(body, name=name, grid=(nl,),
               in_specs=[pl.BlockSpec((1,) + a3.shape[1:], (lambda l: (0, 0, 0)) if shared else (lambda l: (l, 0, 0))),
                         pl.BlockSpec((1,) + b.shape[1:], lambda l: (l, 0, 0))],
               out_specs=pl.BlockSpec((1, M, N), lambda l: (l, 0, 0)), out_shape=_sds((nl, M, N), F32),
               compiler_params=_cp(("parallel",), 48 << 20))(a3, b)


def out_proj_fwd(pieces, w, name):
    T = pieces[0][0].shape[0]
    arrs, offs = [a for a, _ in pieces], [o for _, o in pieces]
    n = len(arrs)
    tm = _div_tile(T, 640, 16)

    def body(*refs):
        w_ref, o_ref = refs[n], refs[n + 1]
        acc = None
        for j in range(n):
            part = _dg(refs[j][...], w_ref[offs[j]:offs[j] + arrs[j].shape[1], :], _DIMS["nn"])
            acc = part if acc is None else acc + part
        o_ref[...] = acc.astype(o_ref.dtype)

    return _pc(body, name=name, grid=(T // tm,),
               in_specs=[pl.BlockSpec((tm, a.shape[1]), lambda i: (i, 0)) for a in arrs] + [pl.BlockSpec(w.shape, lambda i: (0, 0))],
               out_specs=pl.BlockSpec((tm, w.shape[1]), lambda i: (i, 0)), out_shape=_sds((T, w.shape[1]), BF16),
               compiler_params=_cp(("parallel",), 32 << 20))(*arrs, w)


def out_proj_dw(pieces, dy, name):
    T, N = dy.shape
    arrs, offs = [a for a, _ in pieces], [o for _, o in pieces]
    n = len(arrs)
    rows = sum(a.shape[1] for a in arrs)
    tn = 512

    def body(*refs):
        d_ref, o_ref = refs[n], refs[n + 1]
        for j in range(n):
            o_ref[offs[j]:offs[j] + arrs[j].shape[1], :] = _dg(refs[j][...], d_ref[...], _DIMS["tn"]).astype(o_ref.dtype)

    return _pc(body, name=name, grid=(N // tn,),
               in_specs=[pl.BlockSpec(a.shape, lambda j: (0, 0)) for a in arrs] + [pl.BlockSpec((T, tn), lambda j: (0, j))],
               out_specs=pl.BlockSpec((rows, tn), lambda j: (0, j)), out_shape=_sds((rows, N), BF16),
               compiler_params=_cp(("parallel",), 48 << 20))(*arrs, dy)


def in_proj_bwd(pieces, h1, w, name):
    T = h1.shape[0]
    arrs = [a for a, _ in pieces]
    offs = [o for _, o in pieces]
    wid = [a.shape[1] for a in arrs]
    n = len(arrs)
    assert sum(wid) == IN_PAD, "the pieces must tile all columns of P"
    tm = _div_tile(T, 640, 16)

    def dx_body(*refs):
        w_ref, o_ref = refs[n], refs[n + 1]
        acc = None
        for j in range(n):
            part = _dg(refs[j][...], w_ref[:, offs[j]:offs[j] + wid[j]], _DIMS["nt"])
            acc = part if acc is None else acc + part
        o_ref[...] = acc.astype(o_ref.dtype)

    dh1 = _pc(dx_body, name=name + "_dx", grid=(T // tm,),
              in_specs=[pl.BlockSpec((tm, wj), lambda i: (i, 0)) for wj in wid] + [pl.BlockSpec((D, IN_PAD), lambda i: (0, 0))],
              out_specs=pl.BlockSpec((tm, D), lambda i: (i, 0)), out_shape=_sds((T, D), BF16),
              compiler_params=_cp(("parallel",), 40 << 20))(*arrs, w)

    tmd, nk = 512, 4
    tk = T // nk

    def dw_body(h_ref, *refs):
        o_ref, acc = refs[n], refs[n + 1]
        k = pl.program_id(1)

        @pl.when(k == 0)
        def _():
            acc[...] = jnp.zeros_like(acc)

        for j in range(n):
            acc[:, offs[j]:offs[j] + wid[j]] += _dg(h_ref[...], refs[j][...], _DIMS["tn"])

        @pl.when(k == nk - 1)
        def _():
            o_ref[...] = acc[...].astype(o_ref.dtype)

    dw = _pc(dw_body, name=name + "_dw", grid=(D // tmd, nk),
             in_specs=[pl.BlockSpec((tk, tmd), lambda i, k: (k, i))] + [pl.BlockSpec((tk, wj), lambda i, k: (k, 0)) for wj in wid],
             out_specs=pl.BlockSpec((tmd, IN_PAD), lambda i, k: (i, 0)), out_shape=_sds((D, IN_PAD), BF16),
             scratch_shapes=[pltpu.VMEM((tmd, IN_PAD), F32)], compiler_params=_cp(("parallel", "arbitrary"), 48 << 20))(h1, *arrs)
    return dh1, dw


def _norm_mod(xo, shift, scale, g):
    r = lax.rsqrt(jnp.mean(xo * xo, axis=-1, keepdims=True) + EPS)
    return (xo * r) * g * (1.0 + scale) + shift


def res_norm_mod(x, y, gsv, g, nL, name):
    T = x.shape[0]
    has_y = y is not None

    def body(*refs):
        if has_y:
            x_ref, y_ref, gsv_ref, g_ref, xo_ref, h_ref = refs
            xo = x_ref[...] + gsv_ref[0, 0:1, :] * y_ref[...]
            xo_ref[...] = xo
        else:
            x_ref, gsv_ref, g_ref, h_ref = refs
            xo = x_ref[...]
        h_ref[...] = _norm_mod(xo, gsv_ref[0, 1:2, :], gsv_ref[0, 2:3, :], g_ref[...]).astype(h_ref.dtype)

    row = pl.BlockSpec((TR, D), lambda i: (i, 0))
    in_specs = [row] + ([row] if has_y else []) + [pl.BlockSpec((1, 8, D), lambda i: (i // nL, 0, 0)),
                                                     pl.BlockSpec((1, D), lambda i: (0, 0))]
    out_specs = ([row] if has_y else []) + [row]
    out_shape = ([_sds((T, D), F32)] if has_y else []) + [_sds((T, D), BF16)]
    args = (x, y, gsv, g) if has_y else (x, gsv, g)
    outs = _pc(body, name=name, grid=(T // TR,), in_specs=in_specs, out_specs=out_specs, out_shape=out_shape,
               compiler_params=_cp(("arbitrary",), 24 << 20))(*args)
    return (outs[0], outs[1]) if has_y else (None, outs[0])


def res_norm_mod_bwd(xo, y, gsv, g, dh, dres, nL, name):
    T = xo.shape[0]
    has_y = y is not None

    def body(*refs):
        if has_y:
            xo_ref, y_ref, gsv_ref, g_ref, dh_ref, dres_ref, dx_ref, dy_ref, dgsv_ref, dg_ref = refs
        else:
            xo_ref, gsv_ref, g_ref, dh_ref, dres_ref, dx_ref, dgsv_ref, dg_ref = refs
        i = pl.program_id(0)

        @pl.when((i == 0) | (i == nL))
        def _():
            dgsv_ref[...] = jnp.zeros_like(dgsv_ref)

        @pl.when(i == 0)
        def _():
            dg_ref[...] = jnp.zeros_like(dg_ref)

        _, vjp = jax.vjp(_norm_mod, xo_ref[...], gsv_ref[0, 1:2, :], gsv_ref[0, 2:3, :], g_ref[...])
        dxn, dshift, dscale, dg = vjp(dh_ref[...].astype(F32))
        dxo = dres_ref[...] + dxn
        dx_ref[...] = dxo
        if has_y:
            dy_ref[...] = (gsv_ref[0, 0:1, :] * dxo).astype(dy_ref.dtype)
            dgsv_ref[0, 0:1, :] += jnp.sum(y_ref[...] * dxo, axis=0, keepdims=True)
        dgsv_ref[0, 1:2, :] += dshift
        dgsv_ref[0, 2:3, :] += dscale
        dg_ref[0:1, :] += dg

    row = pl.BlockSpec((TR, D), lambda i: (i, 0))
    gspec = pl.BlockSpec((1, 8, D), lambda i: (i // nL, 0, 0))
    in_specs = [row] + ([row] if has_y else []) + [gspec, pl.BlockSpec((1, D), lambda i: (0, 0)), row, row]
    out_specs = [row] + ([row] if has_y else []) + [gspec, pl.BlockSpec((8, D), lambda i: (0, 0))]
    out_shape = [_sds((T, D), F32)] + ([_sds((T, D), BF16)] if has_y else []) + [_sds((2, 8, D), F32), _sds((8, D), F32)]
    args = (xo, y, gsv, g, dh, dres) if has_y else (xo, gsv, g, dh, dres)
    outs = _pc(body, name=name, grid=(T // TR,), in_specs=in_specs, out_specs=out_specs, out_shape=out_shape,
               compiler_params=_cp(("arbitrary",), 32 << 20))(*args)
    if has_y:
        return outs
    return outs[0], None, outs[1], outs[2]


def final_loss(x, y, gsv, g, target, nL, name):
    T = x.shape[0]

    def lossf(xo, gv, t):
        yn = (xo * lax.rsqrt(jnp.mean(xo * xo, axis=-1, keepdims=True) + EPS)) * gv
        e = yn - t
        return 0.5 * jnp.sum(jnp.sum(e * e, axis=-1, keepdims=True) * (1.0 / D), axis=0, keepdims=True)

    def body(x_ref, y_ref, gsv_ref, g_ref, t_ref, loss_ref, dx_ref, dy_ref, dgsv_ref, dg_ref):
        i = pl.program_id(0)

        @pl.when(i == 0)
        def _():
            loss_ref[...] = jnp.zeros_like(loss_ref)
            dg_ref[...] = jnp.zeros_like(dg_ref)

        @pl.when((i == 0) | (i == nL))
        def _():
            dgsv_ref[...] = jnp.zeros_like(dgsv_ref)

        @pl.when(i < nL)
        def _():
            gate = gsv_ref[0, 0:1, :]
            yv = y_ref[...]
            xo = x_ref[...] + gate * yv
            lv, vjp = jax.vjp(lossf, xo, g_ref[...], t_ref[...])
            dxo, dg, _ = vjp(jnp.ones((1, 1), F32))
            loss_ref[...] += jnp.broadcast_to(lv, loss_ref.shape)
            dx_ref[...] = dxo
            dy_ref[...] = (gate * dxo).astype(dy_ref.dtype)
            dgsv_ref[0, 0:1, :] += jnp.sum(yv * dxo, axis=0, keepdims=True)
            dg_ref[0:1, :] += dg

        @pl.when(i >= nL)
        def _():
            dx_ref[...] = jnp.zeros_like(dx_ref)
            dy_ref[...] = jnp.zeros_like(dy_ref)

    row = pl.BlockSpec((TR, D), lambda i: (i, 0))
    gspec = pl.BlockSpec((1, 8, D), lambda i: (i // nL, 0, 0))
    return _pc(
        body, name=name, grid=(T // TR,),
        in_specs=[row, row, gspec, pl.BlockSpec((1, D), lambda i: (0, 0)),
                  pl.BlockSpec((TR, D), lambda i: (jnp.minimum(i, nL - 1), 0))],
        out_specs=[pl.BlockSpec((8, 128), lambda i: (0, 0)), row, row, gspec, pl.BlockSpec((8, D), lambda i: (0, 0))],
        out_shape=[_sds((8, 128), F32), _sds((T, D), F32), _sds((T, D), BF16), _sds((2, 8, D), F32), _sds((8, D), F32)],
        compiler_params=_cp(("arbitrary",), 32 << 20),
    )(x, y, gsv, g, target)


FI_BLK = 2 * D_FF // 4


def _fi_chip(j):
    return (j % 2) * 2 + j // 2


def matmul_fi(a, b, mode, out_dtype, name):
    T = a.shape[0]
    if mode == "tn":
        tmd = 512

        def body(a_ref, b_ref, o_ref):
            o_ref[0] = _dg(a_ref[...], b_ref[...], _DIMS["tn"]).astype(o_ref.dtype)

        return _pc(body, name=name, grid=(4, D // tmd),
                   in_specs=[pl.BlockSpec((T, tmd), lambda j, i: (0, i)), pl.BlockSpec((T, FI_BLK), lambda j, i: (0, j))],
                   out_specs=pl.BlockSpec((1, tmd, FI_BLK), lambda j, i: (_fi_chip(j), i, 0)),
                   out_shape=_sds((4, D, FI_BLK), out_dtype), compiler_params=_cp(("parallel", "arbitrary"), 48 << 20))(a, b)
    assert mode == "nt"
    tm = _div_tile(T, 640, 16)

    def body(a_ref, b_ref, o_ref):
        acc = None
        for k in range(4):
            part = _dg(a_ref[:, k * FI_BLK:(k + 1) * FI_BLK], b_ref[_fi_chip(k)], _DIMS["nt"])
            acc = part if acc is None else acc + part
        o_ref[...] = acc.astype(o_ref.dtype)

    return _pc(body, name=name, grid=(T // tm,),
               in_specs=[pl.BlockSpec((tm, 4 * FI_BLK), lambda i: (i, 0)), pl.BlockSpec((4, D, FI_BLK), lambda i: (0, 0, 0))],
               out_specs=pl.BlockSpec((tm, D), lambda i: (i, 0)), out_shape=_sds((T, D), out_dtype),
               compiler_params=_cp(("parallel",), VMEM_CAP))(a, b)


def _swiglu(gate, up):
    return _silu(gate) * up


def ffn_in_swiglu(a, w, name):
    T = a.shape[0]
    tm = _div_tile(T, 640, 32)
    half = tm // 2

    def body(a_ref, wg_ref, wu_ref, gu_ref, act_ref):
        for rows in (slice(0, half), slice(half, tm)):
            g = _dg(a_ref[rows, :], wg_ref[0], _DIMS["nn"]).astype(BF16)
            u = _dg(a_ref[rows, :], wu_ref[0], _DIMS["nn"]).astype(BF16)
            gu_ref[rows, :FI_BLK] = g
            gu_ref[rows, FI_BLK:] = u
            act_ref[rows, :] = _swiglu(g.astype(F32), u.astype(F32)).astype(BF16)

    wspec = lambda r: pl.BlockSpec((1, D, FI_BLK), lambda j, i: (_fi_chip(2 * j + r), 0, 0))
    return _pc(body, name=name, grid=(2, T // tm),
               in_specs=[pl.BlockSpec((tm, D), lambda j, i: (i, 0)), wspec(0), wspec(1)],
               out_specs=[pl.BlockSpec((tm, 2 * FI_BLK), lambda j, i: (i, j)), pl.BlockSpec((tm, FI_BLK), lambda j, i: (i, j))],
               out_shape=[_sds((T, 4 * FI_BLK), BF16), _sds((T, D_FF), BF16)],
               compiler_params=_cp(("parallel", "arbitrary"), 48 << 20))(a, w, w)


FODX_BUFS = 3


def ffn_out_dx_swiglu(d, w, gu, name):
    T = d.shape[0]
    tm = _div_tile(T, 320, 16)
    nt = T // tm
    nbuf = min(FODX_BUFS, nt)

    def body(d_ref, w_ref, gu_hbm, o_ref, gbuf, sems):
        i = pl.program_id(0)

        def fetch(t, slot):
            return pltpu.make_async_copy(gu_hbm.at[pl.ds(pl.multiple_of(t * tm, tm), tm), :], gbuf.at[slot], sems.at[slot])

        @pl.when(i == 0)
        def _():
            for t in range(nbuf):
                fetch(t, t).start()

        slot = i % nbuf
        fetch(i, slot).wait()
        gu_ref = gbuf.at[slot]
        for j in range(2):
            dact = _dg(d_ref[...], w_ref[j * FI_BLK:(j + 1) * FI_BLK, :], _DIMS["nt"]).astype(BF16).astype(F32)
            gs, us = slice(2 * j * FI_BLK, (2 * j + 1) * FI_BLK), slice((2 * j + 1) * FI_BLK, (2 * j + 2) * FI_BLK)
            g, u = gu_ref[:, gs].astype(F32), gu_ref[:, us].astype(F32)
            sg = 1.0 / (1.0 + jnp.exp(-g))
            sl = g * sg
            o_ref[:, gs] = (dact * u * (sg + sl * (1.0 - sg))).astype(o_ref.dtype)
            o_ref[:, us] = (dact * sl).astype(o_ref.dtype)

        @pl.when(i + nbuf < nt)
        def _():
            fetch(i + nbuf, slot).start()

    return _pc(body, name=name, grid=(nt,),
               in_specs=[pl.BlockSpec((tm, D), lambda i: (i, 0)), pl.BlockSpec((D_FF, D), lambda i: (0, 0)), pl.BlockSpec(memory_space=pl.ANY)],
               out_specs=pl.BlockSpec((tm, 4 * FI_BLK), lambda i: (i, 0)), out_shape=_sds((T, 4 * FI_BLK), BF16),
               scratch_shapes=[pltpu.VMEM((nbuf, tm, 4 * FI_BLK), BF16), pltpu.SemaphoreType.DMA((nbuf,))],
               compiler_params=_cp(("arbitrary",), 48 << 20))(d, w, gu)


def rope_tables(L, Lc):
    t = np.arange(L)
    rows, cols = t // GRID_W, t % GRID_W
    inv = ROPE_BASE ** (-np.arange(16, dtype=np.float32) / 16)
    lane = np.arange(64)
    pos = np.where((lane // 32)[None, :] == 0, rows[:, None], cols[:, None]).astype(np.float32)
    ang = jnp.asarray(pos) * jnp.asarray(inv[lane % 16])[None, :]
    cos = jnp.concatenate([jnp.cos(ang), jnp.ones((Lc, 64), F32)], axis=0)
    sin = jnp.concatenate([jnp.sin(ang), jnp.zeros((Lc, 64), F32)], axis=0)
    return jnp.tile(cos, (1, 2)), jnp.tile(sin, (1, 2))


def rope_apply(q_src, q_col, k_src, k_col, cos, sin, transpose, name, kv_src=None):
    T = cos.shape[0]
    with_kv = kv_src is not None
    tr = _div_tile(T, 640, 16)

    def rot(x, c, s):
        first = (lax.broadcasted_iota(jnp.int32, x.shape, 1) % 32) < 16
        if transpose:
            y = x * s
            return x * c + jnp.where(first, pltpu.roll(y, 112, 1), -pltpu.roll(y, 16, 1))
        return x * c + jnp.where(first, -pltpu.roll(x, 112, 1), pltpu.roll(x, 16, 1)) * s

    def body(q_ref, k_ref, c_ref, s_ref, *rest):
        qo_ref, ko_ref = rest[-4:-2] if with_kv else rest
        c, s = c_ref[...], s_ref[...]
        for j in range(2):
            qo_ref[:, j * 128:(j + 1) * 128] = rot(q_ref[:, j * 128:(j + 1) * 128].astype(F32), c, s).astype(qo_ref.dtype)
        ko_ref[...] = rot(k_ref[...].astype(F32), c, s).astype(ko_ref.dtype)
        if with_kv:
            rest[-2][...] = rest[0][...].astype(BF16)
            rest[-1][...] = rest[1][...].astype(BF16)

    tab = pl.BlockSpec((tr, 128), lambda i: (i, 0))
    wide = pl.BlockSpec((tr, 256), lambda i: (i, 0))
    kv_in = [pl.BlockSpec((tr, 256), lambda i: (i, C_KB // 256)), pl.BlockSpec((tr, 256), lambda i: (i, C_VB // 256))] if with_kv else []
    return _pc(body, name=name, grid=(T // tr,),
               in_specs=[pl.BlockSpec((tr, 256), lambda i: (i, q_col)), pl.BlockSpec((tr, 128), lambda i: (i, k_col)), tab, tab] + kv_in,
               out_specs=[wide, tab] + ([wide, wide] if with_kv else []),
               out_shape=[_sds((T, 256), BF16), _sds((T, 128), BF16)] + ([_sds((T, 256), BF16)] * 2 if with_kv else []),
               compiler_params=_cp(("parallel",), 32 << 20))(q_src, k_src, cos, sin, *([kv_src, kv_src] if with_kv else []))


_SCALE = HD ** -0.5


def _attn_tile(qh, ks, vs, extra):
    ss = []
    for k, add in ks:
        s = _dg(qh, k, _DIMS["nt"]) * _SCALE
        ss.append(s if add is None else s + add)
    m = ss[0].max(axis=-1, keepdims=True)
    for s in ss[1:]:
        m = jnp.maximum(m, s.max(axis=-1, keepdims=True))
    if extra is not None:
        m = jnp.maximum(m, extra)
    ps = [jnp.exp(s - m) for s in ss]
    den = ps[0].sum(axis=-1, keepdims=True)
    for p in ps[1:]:
        den = den + p.sum(axis=-1, keepdims=True)
    if extra is not None:
        den = den + jnp.exp(extra - m)
    num = _dg(ps[0], vs[0], _DIMS["nn"])
    for p, v in zip(ps[1:], vs[1:]):
        num = num + _dg(p, v, _DIMS["nn"])
    linv = 1.0 / den
    return num * linv, m, linv


def _attn_bwd_tile(qh, ks, vs, extra, m, linv, oh, doh):
    delta = jnp.sum(doh * oh, axis=-1, keepdims=True)
    dq = None
    dks, dvs, dss = [], [], []
    for (k, add), v in zip(ks, vs):
        s = _dg(qh, k, _DIMS["nt"]) * _SCALE
        if add is not None:
            s = s + add
        p = jnp.exp(s - m) * linv
        dvs.append(_dg(p, doh, _DIMS["tn"]))
        ds = p * (_dg(doh, v, _DIMS["nt"]) - delta)
        dss.append(ds)
        dsq = ds * _SCALE
        part = _dg(dsq, k, _DIMS["nn"])
        dq = part if dq is None else dq + part
        dks.append(_dg(dsq, qh, _DIMS["tn"]))
    dextra = None
    if extra is not None:
        dextra = -(jnp.exp(extra - m) * linv * delta)
    return dq, dks, dvs, dss, dextra


def _wa_mask(n, L):
    qpos = n * WA_BLK + lax.broadcasted_iota(jnp.int32, (WA_BLK, 3 * WA_BLK), 0)
    kpos = (n - 1) * WA_BLK + lax.broadcasted_iota(jnp.int32, (WA_BLK, 3 * WA_BLK), 1)
    ok = (jnp.abs(qpos - kpos) <= WA_BLK) & (kpos >= 0) & (kpos < L)
    return jnp.where(ok, 0.0, NEG).astype(F32)


WA_BPS = 2
_WA_PAIRS = (((0, 0), (1, 3), False), ((1, 2), (0, 1), True))


def _swap_halves_lanes(a):
    return jnp.concatenate([a[:, HD:], a[:, :HD]], axis=1)


def _wa_specs(L, Lc):
    nb = L // WA_BLK
    cb = L // Lc

    def blk(j):
        return pl.BlockSpec((WA_BLK, 128), lambda s: (jnp.clip(s * WA_BPS - 1 + j, 0, nb - 1), 0))

    return nb, [blk(j) for j in range(WA_BPS + 2)] + [pl.BlockSpec((Lc, 128), lambda s: (cb, 0))]


def _wa_pair_q(q_ref, qs, lo, hi):
    a = q_ref[qs, lo[0] * 128:(lo[0] + 1) * 128]
    b = q_ref[qs, hi[0] * 128:(hi[0] + 1) * 128]
    lane = lax.broadcasted_iota(jnp.int32, a.shape, 1)
    zero = jnp.zeros_like(a)
    return jnp.concatenate([jnp.where(lane < HD, a, zero), jnp.where(lane >= HD, b, zero)], axis=0)


def _wa_pair_vec(ref, qs, lo, hi, base=0):
    return jnp.concatenate([ref[qs, base + lo[1]:base + lo[1] + 1], ref[qs, base + hi[1]:base + hi[1] + 1]], axis=0)


def _wa_pair_sink(s_ref, n, lo, hi):
    return jnp.concatenate([jnp.broadcast_to(s_ref[lo[1]:lo[1] + 1, 0:1], (n, 1)), jnp.broadcast_to(s_ref[hi[1]:hi[1] + 1, 0:1], (n, 1))], axis=0)


def win_attn_fwd(qr, kr, krs, v, vs, sink, L, Lc, name):
    T = L + Lc
    nb, specs = _wa_specs(L, Lc)
    nk = WA_BPS + 2
    QB = WA_BPS * WA_BLK
    nlat = nb // WA_BPS

    def body(q_ref, *refs):
        groups = [refs[g * (nk + 1):(g + 1) * (nk + 1)] for g in range(4)]
        s_ref, o_ref, st_ref = refs[-3], refs[-2], refs[-1]
        s = pl.program_id(0)

        def run(qs, n, ks_of, vs_of):
            outs = []
            for lo, hi, swapped in _WA_PAIRS:
                kb, vb = groups[1 if swapped else 0], groups[3 if swapped else 2]
                o2, m2, l2 = _attn_tile(_wa_pair_q(q_ref, qs, lo, hi), ks_of(kb), vs_of(vb), _wa_pair_sink(s_ref, n, lo, hi))
                outs.append(o2)
                for r, (_, h) in enumerate((lo, hi)):
                    st_ref[qs, h:h + 1] = m2[r * n:(r + 1) * n]
                    st_ref[qs, WA_HEADS + h:WA_HEADS + h + 1] = l2[r * n:(r + 1) * n]
            lane = lax.broadcasted_iota(jnp.int32, (n, 128), 1)
            o_ref[qs, 0:128] = jnp.where(lane < HD, outs[0][:n], outs[1][n:]).astype(o_ref.dtype)
            o_ref[qs, 128:256] = jnp.where(lane < HD, outs[1][:n], outs[0][n:]).astype(o_ref.dtype)

        @pl.when(s < nlat)
        def _():
            for b in range(WA_BPS):
                m1 = _wa_mask(s * WA_BPS + b, L)
                mask = jnp.concatenate([m1, m1], axis=0)
                cat = lambda g: jnp.concatenate([g[b + j][...] for j in range(3)], axis=0)
                run(slice(b * WA_BLK, (b + 1) * WA_BLK), WA_BLK,
                    lambda kb: [(cat(kb), mask), (kb[nk][...], None)], lambda vb: [cat(vb), vb[nk][...]])

        @pl.when(s >= nlat)
        def _():
            run(slice(None), QB, lambda kb: [(kb[nk][...], None)], lambda vb: [vb[nk][...]])

    qspec = pl.BlockSpec((QB, 256), lambda s: (s, 0))
    return _pc(body, name=name, grid=(T // QB,),
               in_specs=[qspec] + specs * 4 + [pl.BlockSpec((8, 128), lambda s: (0, 0))],
               out_specs=[qspec, pl.BlockSpec((QB, 8), lambda s: (s, 0))], out_shape=[_sds((T, 256), BF16), _sds((T, 8), F32)],
               compiler_params=_cp(("arbitrary",), 40 << 20))(qr, *([kr] * (nk + 1)), *([krs] * (nk + 1)), *([v] * (nk + 1)), *([vs] * (nk + 1)), sink)


def win_attn_bwd(qr, kr, krs, v, vs, sink, do_src, o, stats, L, Lc, name):
    T = L + Lc
    nb, specs = _wa_specs(L, Lc)
    nk = WA_BPS + 2
    QB = WA_BPS * WA_BLK
    nlat = nb // WA_BPS
    cx = WA_BLK + L

    def body(q_ref, *refs):
        groups = [refs[g * (nk + 1):(g + 1) * (nk + 1)] for g in range(4)]
        s_ref, do_ref, o_ref, st_ref, dq_ref, dk_ref, dks_ref, dv_ref, dvs_ref, ds_ref = refs[4 * (nk + 1):]
        s = pl.program_id(0)

        @pl.when(s == 0)
        def _():
            for r in (dk_ref, dks_ref, dv_ref, dvs_ref, ds_ref):
                r[...] = jnp.zeros_like(r)

        def run(qs, n, ks_of, vs_of, rows):
            lane = lax.broadcasted_iota(jnp.int32, (n, 128), 1)
            dqs = []
            for lo, hi, swapped in _WA_PAIRS:
                kb, vb = groups[1 if swapped else 0], groups[3 if swapped else 2]
                dka, dva = (dks_ref, dvs_ref) if swapped else (dk_ref, dv_ref)
                pair = lambda ref: jnp.concatenate([jnp.where(lane < HD, ref[qs, lo[0] * 128:(lo[0] + 1) * 128].astype(F32), 0.0),
                                                    jnp.where(lane >= HD, ref[qs, hi[0] * 128:(hi[0] + 1) * 128].astype(F32), 0.0)], axis=0)
                dq2, dks, dvs, _, dex = _attn_bwd_tile(_wa_pair_q(q_ref, qs, lo, hi), ks_of(kb), vs_of(vb), _wa_pair_sink(s_ref, n, lo, hi),
                                                       _wa_pair_vec(st_ref, qs, lo, hi), _wa_pair_vec(st_ref, qs, lo, hi, WA_HEADS), pair(o_ref), pair(do_ref))
                dqs.append(dq2)
                for r, (_, h) in enumerate((lo, hi)):
                    ds_ref[h:h + 1, :] += jnp.broadcast_to(jnp.sum(dex[r * n:(r + 1) * n], axis=0, keepdims=True), (1, 128))
                if rows is not None:
                    dka[rows, :] += dks[0]
                    dva[rows, :] += dvs[0]
                dka[cx:cx + Lc, :] += dks[-1]
                dva[cx:cx + Lc, :] += dvs[-1]
            dq_ref[qs, 0:128] = jnp.where(lane < HD, dqs[0][:n], dqs[1][n:])
            dq_ref[qs, 128:256] = jnp.where(lane < HD, dqs[1][:n], dqs[0][n:])

        @pl.when(s < nlat)
        def _():
            for b in range(WA_BPS):
                nblk = s * WA_BPS + b
                m1 = _wa_mask(nblk, L)
                mask = jnp.concatenate([m1, m1], axis=0)
                cat = lambda g: jnp.concatenate([g[b + j][...] for j in range(3)], axis=0)
                run(slice(b * WA_BLK, (b + 1) * WA_BLK), WA_BLK, lambda kb: [(cat(kb), mask), (kb[nk][...], None)],
                    lambda vb: [cat(vb), vb[nk][...]], pl.ds(pl.multiple_of(nblk * WA_BLK, WA_BLK), 3 * WA_BLK))

        @pl.when(s >= nlat)
        def _():
            run(slice(None), QB, lambda kb: [(kb[nk][...], None)], lambda vb: [vb[nk][...]], None)

    qspec = pl.BlockSpec((QB, 256), lambda s: (s, 0))
    acc_spec = pl.BlockSpec((T + 2 * WA_BLK, 128), lambda s: (0, 0))
    acc_shape = _sds((T + 2 * WA_BLK, 128), F32)
    return _pc(body, name=name, grid=(T // QB,),
               in_specs=[qspec] + specs * 4 + [pl.BlockSpec((8, 128), lambda s: (0, 0)), qspec, qspec, pl.BlockSpec((QB, 8), lambda s: (s, 0))],
               out_specs=[qspec, acc_spec, acc_spec, acc_spec, acc_spec, pl.BlockSpec((8, 128), lambda s: (0, 0))],
               out_shape=[_sds((T, 256), F32), acc_shape, acc_shape, acc_shape, acc_shape, _sds((8, 128), F32)],
               compiler_params=_cp(("arbitrary",), 48 << 20))(qr, *([kr] * (nk + 1)), *([krs] * (nk + 1)), *([v] * (nk + 1)), *([vs] * (nk + 1)),
                                                              sink, do_src, o, stats)


def na_index_tables():
    qc = np.arange(GRID_W)[:, None]
    kc = np.arange(GRID_W)[None, :]
    cstart = np.clip(qc - NA_KW // 2, 0, GRID_W - NA_KW)
    ok = (kc >= cstart) & (kc < cstart + NA_KW)
    dx = np.clip(kc - qc, -(NA_KW - 1), NA_KW - 1) + (NA_KW - 1)
    off = np.arange(NA_KH)[:, None]
    kr = np.arange(NA_KH)[None, :]
    dy = kr - off + (NA_KH - 1)
    return ok, dx, dy


def _na_selectors():
    ok, dx, dy = na_index_tables()
    e1 = np.zeros((GRID_W * GRID_W, 128), np.float32)
    qi, ki = np.nonzero(ok)
    e1[qi * GRID_W + ki, dx[qi, ki]] = 1.0
    e2 = np.zeros((16, NA_KH * NA_KH), np.float32)
    oi, ri = np.meshgrid(np.arange(NA_KH), np.arange(NA_KH), indexing="ij")
    e2[dy[oi, ri].ravel(), (oi * NA_KH + ri).ravel()] = 1.0
    return ok, jnp.asarray(e1), jnp.asarray(np.kron(np.eye(NA_HEADS, dtype=np.float32), e2))


def na_bias_table(rpb, tag):
    ok, e1, e2 = _na_selectors()
    r2 = jnp.pad(rpb.astype(F32), ((0, 0), (0, 1), (0, 128 - (2 * NA_KW - 1)))).reshape(NA_HEADS * 16, 128)
    r1 = matmul(e2, r2, "tn", F32, f"na_bias_sel1_{tag}", hi=True)
    x = matmul(r1, e1, "nt", F32, f"na_bias_sel2_{tag}", hi=True)
    b = x.reshape(NA_HEADS, NA_KH, NA_KH, GRID_W, GRID_W).transpose(0, 1, 3, 2, 4)
    b = b + jnp.asarray(np.where(ok, 0.0, NEG).astype(np.float32))[None, None, :, None, :]
    return b.reshape(NA_HEADS, NA_KH, GRID_W, NA_KH * GRID_W)


def _na_rows(r, GR):
    r0 = jnp.clip(r - NA_KH // 2, 0, GR - NA_KH)
    return r0, jnp.clip(r - r0, 0, NA_KH - 1)


NA_RPS = 4


def _pair_rows(x):
    lane = lax.broadcasted_iota(jnp.int32, x.shape, 1)
    zero = jnp.zeros_like(x)
    return jnp.concatenate([jnp.where(lane < HD, x, zero), jnp.where(lane >= HD, x, zero)], axis=0)


def _unpair_rows(x2):
    n = x2.shape[0] // 2
    lane = lax.broadcasted_iota(jnp.int32, (n, 128), 1)
    return jnp.where(lane < HD, x2[:n], x2[n:])


def na_fwd(P, kb, vb, bias, L, Lc, name):
    T = L + Lc
    GR = L // GRID_W
    W = NA_KH * GRID_W
    QB = GRID_W * NA_RPS
    nlat = GR // NA_RPS

    def body(q_ref, k_ref, v_ref, b_ref, o_ref, st_ref):
        s = pl.program_id(0)

        def put(qs, p, res):
            o2, m2, l2 = res
            n = o2.shape[0] // 2
            o_ref[qs, p * 128:(p + 1) * 128] = _unpair_rows(o2).astype(o_ref.dtype)
            for r in range(2):
                st_ref[qs, 2 * p + r:2 * p + r + 1] = m2[r * n:(r + 1) * n]
                st_ref[qs, NA_HEADS + 2 * p + r:NA_HEADS + 2 * p + r + 1] = l2[r * n:(r + 1) * n]

        @pl.when(s < nlat)
        def _():
            for rr in range(NA_RPS):
                r0, off = _na_rows(s * NA_RPS + rr, GR)
                rows = pl.ds(pl.multiple_of(r0 * GRID_W, GRID_W), W)
                qs = slice(rr * GRID_W, (rr + 1) * GRID_W)
                for p in range(NA_HEADS // 2):
                    ps = slice(p * 128, (p + 1) * 128)
                    b2 = jnp.concatenate([b_ref[2 * p, off], b_ref[2 * p + 1, off]], axis=0)
                    put(qs, p, _attn_tile(_pair_rows(q_ref[qs, ps]), [(k_ref[rows, ps], b2), (k_ref[L:T, ps], None)],
                                          [v_ref[rows, ps], v_ref[L:T, ps]], None))

        @pl.when(s >= nlat)
        def _():
            for p in range(NA_HEADS // 2):
                ps = slice(p * 128, (p + 1) * 128)
                put(slice(None), p, _attn_tile(_pair_rows(q_ref[:, ps]), [(k_ref[L:T, ps], None)], [v_ref[L:T, ps]], None))

    one = pl.Buffered(1)
    return _pc(body, name=name, grid=(T // QB,),
               in_specs=[pl.BlockSpec((QB, 256), lambda r: (r, C_QB // 256)),
                         pl.BlockSpec((T, 256), lambda r: (0, 0), pipeline_mode=one),
                         pl.BlockSpec((T, 256), lambda r: (0, 0), pipeline_mode=one),
                         pl.BlockSpec((NA_HEADS, NA_KH, GRID_W, W), lambda r: (0, 0, 0, 0), pipeline_mode=one)],
               out_specs=[pl.BlockSpec((QB, 256), lambda r: (r, 0)), pl.BlockSpec((QB, 8), lambda r: (r, 0))],
               out_shape=[_sds((T, 256), BF16), _sds((T, 8), F32)],
               compiler_params=_cp(("arbitrary",), 32 << 20))(P, kb, vb, bias)


def na_bwd(P, kb, vb, bias, do_src, o, stats, L, Lc, name):
    T = L + Lc
    GR = L // GRID_W
    W = NA_KH * GRID_W
    QB = GRID_W * NA_RPS
    nlat = GR // NA_RPS

    def body(q_ref, k_ref, v_ref, b_ref, do_ref, o_ref, st_ref, dq_ref, dk_ref, dv_ref, db_ref):
        s = pl.program_id(0)

        @pl.when(s == 0)
        def _():
            dk_ref[...] = jnp.zeros_like(dk_ref)
            dv_ref[...] = jnp.zeros_like(dv_ref)
            db_ref[...] = jnp.zeros_like(db_ref)

        def tile(qs, p, ks, vs):
            ps = slice(p * 128, (p + 1) * 128)
            m2 = jnp.concatenate([st_ref[qs, 2 * p:2 * p + 1], st_ref[qs, 2 * p + 1:2 * p + 2]], axis=0)
            l2 = jnp.concatenate([st_ref[qs, NA_HEADS + 2 * p:NA_HEADS + 2 * p + 1], st_ref[qs, NA_HEADS + 2 * p + 1:NA_HEADS + 2 * p + 2]], axis=0)
            dq2, dks, dvs, dss, _ = _attn_bwd_tile(_pair_rows(q_ref[qs, ps]), ks, vs, None, m2, l2,
                                                   _pair_rows(o_ref[qs, ps].astype(F32)), _pair_rows(do_ref[qs, ps].astype(F32)))
            dq_ref[qs, ps] = _unpair_rows(dq2).astype(dq_ref.dtype)
            return dks, dvs, dss

        @pl.when(s < nlat)
        def _():
            for rr in range(NA_RPS):
                r0, off = _na_rows(s * NA_RPS + rr, GR)
                rows = pl.ds(pl.multiple_of(r0 * GRID_W, GRID_W), W)
                qs = slice(rr * GRID_W, (rr + 1) * GRID_W)
                for p in range(NA_HEADS // 2):
                    ps = slice(p * 128, (p + 1) * 128)
                    b2 = jnp.concatenate([b_ref[2 * p, off], b_ref[2 * p + 1, off]], axis=0)
                    dks, dvs, dss = tile(qs, p, [(k_ref[rows, ps], b2), (k_ref[L:T, ps], None)], [v_ref[rows, ps], v_ref[L:T, ps]])
                    dk_ref[rows, ps] += dks[0]
                    dv_ref[rows, ps] += dvs[0]
                    dk_ref[L:T, ps] += dks[1]
                    dv_ref[L:T, ps] += dvs[1]
                    db_ref[2 * p, off] += dss[0][:GRID_W]
                    db_ref[2 * p + 1, off] += dss[0][GRID_W:]

        @pl.when(s >= nlat)
        def _():
            for p in range(NA_HEADS // 2):
                ps = slice(p * 128, (p + 1) * 128)
                dks, dvs, _ = tile(slice(None), p, [(k_ref[L:T, ps], None)], [v_ref[L:T, ps]])
                dk_ref[L:T, ps] += dks[0]
                dv_ref[L:T, ps] += dvs[0]

    one = pl.Buffered(1)
    full = lambda shape: pl.BlockSpec(shape, lambda r: (0,) * len(shape), pipeline_mode=one)
    qspec = pl.BlockSpec((QB, 256), lambda r: (r, 0))
    return _pc(body, name=name, grid=(T // QB,),
               in_specs=[pl.BlockSpec((QB, 256), lambda r: (r, C_QB // 256)), full((T, 256)), full((T, 256)),
                         full((NA_HEADS, NA_KH, GRID_W, W)), pl.BlockSpec((QB, 256), lambda r: (r, 1)), qspec, pl.BlockSpec((QB, 8), lambda r: (r, 0))],
               out_specs=[qspec, full((T, 256)), full((T, 256)), full((NA_HEADS, NA_KH, GRID_W, W))],
               out_shape=[_sds((T, 256), BF16), _sds((T, 256), F32), _sds((T, 256), F32), _sds((NA_HEADS, NA_KH, GRID_W, W), F32)],
               compiler_params=_cp(("arbitrary",), 48 << 20))(P, kb, vb, bias, do_src, o, stats)


def na_rpb_grad(dbias, tag):
    _, e1, e2 = _na_selectors()
    x = dbias.reshape(NA_HEADS, NA_KH, GRID_W, NA_KH, GRID_W).transpose(0, 1, 3, 2, 4).reshape(NA_HEADS * NA_KH * NA_KH, GRID_W * GRID_W)
    r1 = matmul(x, e1, "nn", F32, f"na_rpb_sel1_{tag}", hi=True, tk=1024)
    r2 = matmul(e2, r1, "nn", F32, f"na_rpb_sel2_{tag}", hi=True)
    return r2.reshape(NA_HEADS, 16, 128)[:, :2 * NA_KH - 1, :2 * NA_KW - 1]


_HALO = 8
CONV_CB = 4
CONV_RB = 64


def _halo_specs(T, col0):
    nh = TR // _HALO
    specs = []
    for j in range(CONV_CB):
        specs.append(pl.BlockSpec((_HALO, 256), lambda i, j=j: (jnp.maximum(i * nh - 1, 0), col0 + j)))
        specs.append(pl.BlockSpec((TR, 256), lambda i, j=j: (i, col0 + j)))
        specs.append(pl.BlockSpec((_HALO, 256), lambda i, j=j: (jnp.minimum((i + 1) * nh, T // _HALO - 1), col0 + j)))
    return specs


def _fill_ext(ext, prv, cur, nxt, i, nL, nT):
    has_prev = jnp.where((i != 0) & (i != nL), 1.0, 0.0)
    has_next = jnp.where((i != nL - 1) & (i != nT - 1), 1.0, 0.0)
    ext[0:_HALO, :] = prv[...].astype(F32) * has_prev
    ext[_HALO:_HALO + TR, :] = cur[...].astype(F32)
    ext[_HALO + TR:, :] = nxt[...].astype(F32) * has_next


def conv_silu_fwd(P, w8, b, nL, name):
    T = P.shape[0]
    nT = T // TR

    def body(*refs):
        xin, (w_ref, b_ref, pre_ref, act_ref, ext) = refs[:3 * CONV_CB], refs[3 * CONV_CB:]
        i = pl.program_id(0)
        for j in range(CONV_CB):
            cs = slice(j * 256, (j + 1) * 256)
            _fill_ext(ext, *xin[3 * j:3 * j + 3], i, nL, nT)
            for r in range(0, TR, CONV_RB):
                y = jnp.broadcast_to(b_ref[:, cs], (CONV_RB, 256))
                for k in range(S_CONV):
                    y = y + w_ref[k:k + 1, cs] * ext[pl.ds(_HALO - S_CONV // 2 + k + r, CONV_RB), :]
                pre_ref[r:r + CONV_RB, cs] = y
                act_ref[r:r + CONV_RB, cs] = _silu(y)

    out = pl.BlockSpec((TR, 1024), lambda i: (i, 0))
    return _pc(body, name=name, grid=(nT,),
               in_specs=_halo_specs(T, C_XBC // 256) + [pl.BlockSpec((8, 1024), lambda i: (0, 0)), pl.BlockSpec((1, 1024), lambda i: (0, 0))],
               out_specs=[out, out], out_shape=[_sds((T, 1024), F32), _sds((T, 1024), F32)],
               scratch_shapes=[pltpu.VMEM((TR + 2 * _HALO, 256), F32)],
               compiler_params=_cp(("parallel",), 24 << 20))(*([P] * (3 * CONV_CB)), w8, b)


def dsilu(pre, dxs_list, db_list, dc_list, name):
    T = pre.shape[0]
    n1, n2, n3 = len(dxs_list), len(db_list), len(dc_list)

    def body(*refs):
        pre_ref = refs[0]
        ins = refs[1:1 + n1 + n2 + n3]
        out = refs[-1]

        def part(rs, lo, hi):
            g = rs[0][...].astype(F32)
            for r in rs[1:]:
                g = g + r[...].astype(F32)
            x = pre_ref[:, lo:hi]
            sg = 1.0 / (1.0 + jnp.exp(-x))
            sl = x * sg
            out[:, lo:hi] = g * (sg + sl * (1.0 - sg))

        part(ins[:n1], 0, 512)
        part(ins[n1:n1 + n2], 512, 768)
        part(ins[n1 + n2:], 768, 1024)

    spec = lambda w: pl.BlockSpec((TR, w), lambda i: (i, 0))
    return _pc(body, name=name, grid=(T // TR,),
               in_specs=[spec(1024)] + [spec(512)] * n1 + [spec(256)] * (n2 + n3),
               out_specs=spec(1024), out_shape=_sds((T, 1024), F32),
               compiler_params=_cp(("parallel",), 32 << 20))(pre, *dxs_list, *db_list, *dc_list)


def conv_bwd(dpre, P, w8, nL, name):
    T = P.shape[0]
    nT = T // TR

    def body(*refs):
        din, xin, (w_ref, dx_ref, dw_ref, db_ref, extd) = refs[:3 * CONV_CB], refs[3 * CONV_CB:4 * CONV_CB], refs[4 * CONV_CB:]
        i = pl.program_id(0)

        @pl.when(i == 0)
        def _():
            dw_ref[...] = jnp.zeros_like(dw_ref)
            db_ref[...] = jnp.zeros_like(db_ref)

        fold = lambda a: functools.reduce(lambda p, q: p + q, [a[q:q + 8] for q in range(0, CONV_RB, 8)])
        for j in range(CONV_CB):
            cs = slice(j * 256, (j + 1) * 256)
            _fill_ext(extd, *din[3 * j:3 * j + 3], i, nL, nT)
            dws = [jnp.zeros((8, 256), F32) for _ in range(S_CONV)]
            dbs = jnp.zeros((8, 256), F32)
            for r in range(0, TR, CONV_RB):
                x = xin[j][r:r + CONV_RB, :]
                dx = jnp.zeros((CONV_RB, 256), F32)
                for k in range(S_CONV):
                    sd = extd[pl.ds(_HALO + S_CONV // 2 - k + r, CONV_RB), :]
                    dx = dx + w_ref[k:k + 1, cs] * sd
                    dws[k] = dws[k] + fold(sd * x)
                dx_ref[r:r + CONV_RB, cs] = dx.astype(dx_ref.dtype)
                dbs = dbs + fold(din[3 * j + 1][r:r + CONV_RB, :])
            for k in range(S_CONV):
                dw_ref[k:k + 1, cs] += jnp.sum(dws[k], axis=0, keepdims=True)
            db_ref[0:1, cs] += jnp.sum(dbs, axis=0, keepdims=True)

    acc = pl.BlockSpec((8, 1024), lambda i: (0, 0))
    xspecs = [pl.BlockSpec((TR, 256), lambda i, j=j: (i, C_XBC // 256 + j)) for j in range(CONV_CB)]
    return _pc(body, name=name, grid=(nT,),
               in_specs=_halo_specs(T, 0) + xspecs + [acc],
               out_specs=[pl.BlockSpec((TR, 1024), lambda i: (i, 0)), acc, acc],
               out_shape=[_sds((T, 1024), BF16), _sds((8, 1024), F32), _sds((8, 1024), F32)],
               scratch_shapes=[pltpu.VMEM((TR + 2 * _HALO, 256), F32)],
               compiler_params=_cp(("arbitrary",), 24 << 20))(*([dpre] * (3 * CONV_CB)), *([P] * CONV_CB), w8)


def _onehot_row(h, n):
    return (lax.broadcasted_iota(jnp.int32, (1, n), 1) == h).astype(F32)


def _onehot_col(h, n):
    return (lax.broadcasted_iota(jnp.int32, (n, 1), 0) == h).astype(F32)


S_PAIRS = S_HEADS // 2


def _ssd_chunk(xs, dtr, dtb, alog, bm, cm, hin, reverse):
    Qn = S_Q
    ii = lax.broadcasted_iota(jnp.int32, (Qn, Qn), 0)
    jj = lax.broadcasted_iota(jnp.int32, (Qn, Qn), 1)
    keep = (ii <= jj) if reverse else (ii >= jj)
    tri = keep.astype(F32)
    triT = ((jj <= ii) if reverse else (jj >= ii)).astype(F32)
    eye = (ii == jj).astype(F32)
    low = jj < S_P
    top = ii < S_P
    dt = _softplus(dtr + dtb)
    a = dt * (-jnp.exp(alog))
    cs = hdot(tri, a)
    csT = hdot(a, triT, "tn")
    dtT = hdot(dt, eye, "tn")
    last = _onehot_row(0 if reverse else Qn - 1, Qn)
    ys, houts = [], []
    for p in range(S_PAIRS):
        g = p // (S_PAIRS // S_GROUPS)
        if p % (S_PAIRS // S_GROUPS) == 0:
            G = bdot(cm[g], bm[g], "nt")
        per_head = []
        for h in (2 * p, 2 * p + 1):
            eh_r, eh_c = _onehot_row(h, S_HEADS), _onehot_col(h, S_HEADS)
            cs_c = jnp.sum(cs * eh_r, axis=1, keepdims=True)
            dt_c = jnp.sum(dt * eh_r, axis=1, keepdims=True)
            cs_r = jnp.sum(csT * eh_c, axis=0, keepdims=True)
            dt_r = jnp.sum(dtT * eh_c, axis=0, keepdims=True)
            tot = jnp.sum(cs_r * last, axis=1, keepdims=True)
            w = G * jnp.exp(jnp.where(keep, cs_c - cs_r, NEG)) * dt_r
            per_head.append((bdot(w, xs[p], "nn"), jnp.exp(cs_c), jnp.exp(tot - cs_c) * dt_c, jnp.exp(tot)))
        (y0, e0, f0, d0), (y1, e1, f1, d1) = per_head
        y = jnp.where(low, y0, y1) + bdot(cm[g], hin[p], "nt") * jnp.where(low, e0, e1)
        hout = hin[p] * jnp.where(top, d0, d1) + bdot(xs[p] * jnp.where(low, f0, f1), bm[g], "tn")
        ys.append(y)
        houts.append(hout)
    return ys, houts


def _ssd_orders(L, Lc):
    nl, ncx = L // S_Q, Lc // S_Q
    fwd = lambda s: jnp.where(s < ncx, nl + s, s - ncx)
    bwd = lambda s: nl + ncx - 1 - s
    return nl + ncx, fwd, bwd


def _ssd_in_specs(fo, bo, step):
    def at(order, w, col):
        return pl.BlockSpec((S_Q, w), lambda u: (order(step(u)), col))
    specs = []
    for order in (fo, bo):
        specs += [at(order, 512, 0), at(order, 256, 2), at(order, 256, 3), at(order, 128, C_DT // 128)]
    return specs


def ssd_fwd(act, P, dtb, alog, L, Lc, name):
    T = L + Lc
    ns, fo, bo = _ssd_orders(L, Lc)

    def body(xf, bf, cf, df, xb, bb, cb, db, dtb_ref, al_ref, yf, yb, hsf, hsb, Hf, Hb):
        s = pl.program_id(0)

        @pl.when(s == 0)
        def _():
            Hf[...] = jnp.zeros_like(Hf)
            Hb[...] = jnp.zeros_like(Hb)

        for d, (x_r, b_r, c_r, dt_r, y_r, hs_r, H) in enumerate(((xf, bf, cf, df, yf, hsf, Hf), (xb, bb, cb, db, yb, hsb, Hb))):
            hin = [H[p] for p in range(S_PAIRS)]
            hs_r[0] = H[...]
            ys, houts = _ssd_chunk(
                [x_r[:, p * 128:(p + 1) * 128] for p in range(S_PAIRS)], dt_r[:, d * 8:(d + 1) * 8],
                dtb_ref[d:d + 1, 0:8], al_ref[d:d + 1, 0:8],
                [b_r[:, g * S_N:(g + 1) * S_N] for g in range(S_GROUPS)], [c_r[:, g * S_N:(g + 1) * S_N] for g in range(S_GROUPS)],
                hin, reverse=(d == 1))
            for p in range(S_PAIRS):
                y_r[:, p * 128:(p + 1) * 128] = ys[p]
                H[p] = houts[p]

    ident = lambda u: u
    small = pl.BlockSpec((8, 128), lambda u: (0, 0))
    hspec = pl.BlockSpec((1, S_PAIRS, 2 * S_P, S_N), lambda u: (u, 0, 0, 0))
    return _pc(body, name=name, grid=(ns,),
               in_specs=_ssd_in_specs(fo, bo, ident) + [small, small],
               out_specs=[pl.BlockSpec((S_Q, 512), lambda u: (fo(u), 0)), pl.BlockSpec((S_Q, 512), lambda u: (bo(u), 0)), hspec, hspec],
               out_shape=[_sds((T, 512), F32), _sds((T, 512), F32), _sds((ns, S_PAIRS, 2 * S_P, S_N), F32), _sds((ns, S_PAIRS, 2 * S_P, S_N), F32)],
               scratch_shapes=[pltpu.VMEM((S_PAIRS, 2 * S_P, S_N), F32), pltpu.VMEM((S_PAIRS, 2 * S_P, S_N), F32)],
               compiler_params=_cp(("arbitrary",), 32 << 20))(act, act, act, P, act, act, act, P, dtb, alog)


def ssd_bwd(act, P, dtb, alog, hsf, hsb, dy, L, Lc, name):
    T = L + Lc
    ns, fo, bo = _ssd_orders(L, Lc)
    step = lambda u: ns - 1 - u

    def body(xf, bf, cf, df, xb, bb, cb, db, dtb_ref, al_ref, hsf_r, hsb_r, dyf, dyb,
             dxf, dbf, dcf, ddf, dxb, dbb, dcb, ddb, ddtb, dal, dHf, dHb):
        u = pl.program_id(0)

        @pl.when(u == 0)
        def _():
            dHf[...] = jnp.zeros_like(dHf)
            dHb[...] = jnp.zeros_like(dHb)
            ddtb[...] = jnp.zeros_like(ddtb)
            dal[...] = jnp.zeros_like(dal)

        dirs = ((xf, bf, cf, df, hsf_r, dyf, dxf, dbf, dcf, ddf, dHf), (xb, bb, cb, db, hsb_r, dyb, dxb, dbb, dcb, ddb, dHb))
        for d, (x_r, b_r, c_r, dt_r, hs_r, dy_r, dx_o, db_o, dc_o, dd_o, dH) in enumerate(dirs):
            f = functools.partial(_ssd_chunk, reverse=(d == 1))
            _, vjp = jax.vjp(
                f, [x_r[:, p * 128:(p + 1) * 128] for p in range(S_PAIRS)], dt_r[:, d * 8:(d + 1) * 8],
                dtb_ref[d:d + 1, 0:8], al_ref[d:d + 1, 0:8],
                [b_r[:, g * S_N:(g + 1) * S_N] for g in range(S_GROUPS)], [c_r[:, g * S_N:(g + 1) * S_N] for g in range(S_GROUPS)],
                [hs_r[0, p] for p in range(S_PAIRS)])
            gx, gdt, gdtb, gal, gb, gc, gh = vjp(([dy_r[:, p * 128:(p + 1) * 128] for p in range(S_PAIRS)],
                                                  [dH[p] for p in range(S_PAIRS)]))
            for p in range(S_PAIRS):
                dx_o[:, p * 128:(p + 1) * 128] = gx[p]
                dH[p] = gh[p]
            for g in range(S_GROUPS):
                db_o[:, g * S_N:(g + 1) * S_N] = gb[g]
                dc_o[:, g * S_N:(g + 1) * S_N] = gc[g]
            dd_o[...] = gdt
            ddtb[d:d + 1, 0:8] += gdtb
            dal[d:d + 1, 0:8] += gal

    small = pl.BlockSpec((8, 128), lambda u: (0, 0))
    hspec = pl.BlockSpec((1, S_PAIRS, 2 * S_P, S_N), lambda u: (step(u), 0, 0, 0))
    at = lambda order, w: pl.BlockSpec((S_Q, w), lambda u: (order(step(u)), 0))
    outs = []
    for order in (fo, bo):
        outs += [at(order, 512), at(order, 256), at(order, 256), at(order, 8)]
    oshape = [_sds((T, 512), F32), _sds((T, 256), F32), _sds((T, 256), F32), _sds((T, 8), F32)]
    return _pc(body, name=name, grid=(ns,),
               in_specs=_ssd_in_specs(fo, bo, step) + [small, small, hspec, hspec, at(fo, 512), at(bo, 512)],
               out_specs=outs + [small, small], out_shape=oshape + oshape + [_sds((8, 128), F32), _sds((8, 128), F32)],
               scratch_shapes=[pltpu.VMEM((S_PAIRS, 2 * S_P, S_N), F32), pltpu.VMEM((S_PAIRS, 2 * S_P, S_N), F32)],
               compiler_params=_cp(("arbitrary",), 40 << 20))(act, act, act, P, act, act, act, P, dtb, alog, hsf, hsb, dy, dy)


def _ssm_out(yf, yb, xs, z, dskip, g):
    y = (yf + yb + dskip * xs) * _silu(z)
    return (y * lax.rsqrt(jnp.mean(y * y, axis=-1, keepdims=True) + EPS)) * g


def ssm_out_fwd(yf, yb, act, P, dskip, g, name):
    T = yf.shape[0]

    def body(yf_r, yb_r, xs_r, z_r, d_r, g_r, o_r):
        o_r[...] = _ssm_out(yf_r[...], yb_r[...], xs_r[...], z_r[...], d_r[...], g_r[...]).astype(o_r.dtype)

    row = pl.BlockSpec((TR, 512), lambda i: (i, 0))
    vec = pl.BlockSpec((1, 512), lambda i: (0, 0))
    return _pc(body, name=name, grid=(T // TR,),
               in_specs=[row, row, row, pl.BlockSpec((TR, 512), lambda i: (i, C_Z // 512)), vec, vec],
               out_specs=row, out_shape=_sds((T, 512), BF16),
               compiler_params=_cp(("parallel",), 16 << 20))(yf, yb, act, P, dskip, g)


def ssm_out_bwd(yf, yb, act, P, dskip, g, do_src, name):
    T = yf.shape[0]

    def body(yf_r, yb_r, xs_r, z_r, d_r, g_r, do_r, dy_r, dxs_r, dz_r, dv_r):
        @pl.when(pl.program_id(0) == 0)
        def _():
            dv_r[...] = jnp.zeros_like(dv_r)

        _, vjp = jax.vjp(_ssm_out, yf_r[...], yb_r[...], xs_r[...], z_r[...], d_r[...], g_r[...])
        dyf, _, dxs, dz, dd, dg = vjp(do_r[...].astype(F32))
        dy_r[...] = dyf
        dxs_r[...] = dxs
        dz_r[...] = dz.astype(dz_r.dtype)
        dv_r[0:1, :] += dd
        dv_r[1:2, :] += dg

    row = pl.BlockSpec((TR, 512), lambda i: (i, 0))
    vec = pl.BlockSpec((1, 512), lambda i: (0, 0))
    return _pc(body, name=name, grid=(T // TR,),
               in_specs=[row, row, row, pl.BlockSpec((TR, 512), lambda i: (i, C_Z // 512)), vec, vec,
                         pl.BlockSpec((TR, 512), lambda i: (i, 1))],
               out_specs=[row, row, row, pl.BlockSpec((8, 512), lambda i: (0, 0))],
               out_shape=[_sds((T, 512), F32), _sds((T, 512), F32), _sds((T, 512), BF16), _sds((8, 512), F32)],
               compiler_params=_cp(("arbitrary",), 24 << 20))(yf, yb, act, P, dskip, g, do_src)


def add_halves(xv, got, cvec, name):
    n, r, cdim = xv.shape
    h = r // 2

    def body(c_ref, x_ref, g_ref, o_ref):
        o_ref[...] = (x_ref[...].astype(F32) + g_ref[...].astype(F32)).astype(o_ref.dtype)

    gs = pltpu.PrefetchScalarGridSpec(
        num_scalar_prefetch=1, grid=(n,),
        in_specs=[pl.BlockSpec((1, h, cdim), lambda k, c_ref: (k, c_ref[0], 0)), pl.BlockSpec((1, h, cdim), lambda k, c_ref: (k, 0, 0))],
        out_specs=pl.BlockSpec((1, h, cdim), lambda k, c_ref: (k, 0, 0)))
    return _pc(body, name=name, grid_spec=gs, out_shape=_sds((n, h, cdim), BF16),
               compiler_params=_cp(("arbitrary",), 24 << 20))(cvec, xv, got)


def sum_slots(a, name):
    n, r, cdim = a.shape
    tr = _div_tile(r, 512, 16)

    def body(a_ref, o_ref):
        acc = a_ref[0].astype(F32)
        for k in range(1, n):
            acc = acc + a_ref[k].astype(F32)
        o_ref[...] = acc

    return _pc(body, name=name, grid=(r // tr,), in_specs=[pl.BlockSpec((n, tr, cdim), lambda i: (0, i, 0))],
               out_specs=pl.BlockSpec((tr, cdim), lambda i: (i, 0)), out_shape=_sds((r, cdim), F32),
               compiler_params=_cp(("parallel",), 32 << 20))(a)


def adamw(w, g, m, v, name):
    B, R, C = w.shape
    tr = _div_tile(R, max(8, (1 << 19) // max(C, 1) // 8 * 8), 8) if R % 8 == 0 else R
    c1 = 1.0 / (1.0 - ADAM_B1 ** ADAM_STEP)
    c2 = 1.0 / (1.0 - ADAM_B2 ** ADAM_STEP)

    def body(w_ref, g_ref, m_ref, v_ref, d_ref, mo_ref, vo_ref):
        gg = g_ref[...]
        mn = ADAM_B1 * m_ref[...] + (1.0 - ADAM_B1) * gg
        vn = ADAM_B2 * v_ref[...] + (1.0 - ADAM_B2) * (gg * gg)
        d_ref[...] = -ADAM_LR * ((mn * c1) / (jnp.sqrt(vn * c2) + ADAM_EPS) + ADAM_WD * w_ref[...])
        mo_ref[...] = mn
        vo_ref[...] = vn

    spec = pl.BlockSpec((1, tr, C), lambda b, i: (b, i, 0))
    return _pc(body, name=name, grid=(B, R // tr), in_specs=[spec] * 4, out_specs=[spec] * 3,
               out_shape=[_sds((B, R, C), F32)] * 3, compiler_params=_cp(("parallel", "parallel"), 32 << 20))(w, g, m, v)


def _me():
    return lax.axis_index("x"), lax.axis_index("y"), lax.axis_index("c")


def _flip(v, bit):
    return 1 - v if bit else v


def allgather8(xv, name):
    R = xv.shape[0]

    def body(x_ref, out_ref, sum_ref, send_sems, recv_sems):
        mx, my, mc = _me()
        me = 4 * mx + 2 * my + mc
        out_ref[me] = x_ref[...]
        sends, recvs = [], []
        for k in range(1, 8):
            px, py, pc = _flip(mx, k & 4), _flip(my, k & 2), _flip(mc, k & 1)
            peer = 4 * px + 2 * py + pc
            sends.append(pltpu.make_async_remote_copy(src_ref=x_ref, dst_ref=out_ref.at[me], send_sem=send_sems.at[k - 1],
                                                      recv_sem=recv_sems.at[k - 1], device_id=(px, py, pc), device_id_type=MESH))
            recvs.append(pltpu.make_async_remote_copy(src_ref=x_ref, dst_ref=out_ref.at[peer], send_sem=send_sems.at[k - 1],
                                                      recv_sem=recv_sems.at[k - 1], device_id=(px, py, pc), device_id_type=MESH))
        for cp in sends:
            cp.start()
        for cp in recvs:
            cp.wait_recv()
        for cp in sends:
            cp.wait_send()
        acc = out_ref[0]
        for d in range(1, 8):
            acc = acc + out_ref[d]
        sum_ref[...] = acc

    vm = pl.BlockSpec(memory_space=pltpu.VMEM)
    return _pc(body, name=name, pin=False, in_specs=[vm], out_specs=[vm, vm], out_shape=[_sds((8, R, 128), F32), _sds((R, 128), F32)],
               scratch_shapes=[pltpu.SemaphoreType.DMA((7,)), pltpu.SemaphoreType.DMA((7,))],
               compiler_params=_cp(None, 32 << 20))(xv)


def _other_chips(mx, my):
    return [(1 - mx, my), (mx, 1 - my), (1 - mx, 1 - my)]


def _halves(r, mc, mult):
    h = r // 2
    return pl.ds(pl.multiple_of(mc * h, mult), h), pl.ds(pl.multiple_of((1 - mc) * h, mult), h)


def _rcopy(src, dst, send_sems, recv_sems, k, to):
    return pltpu.make_async_remote_copy(src_ref=src, dst_ref=dst, send_sem=send_sems.at[k], recv_sem=recv_sems.at[k],
                                        device_id=to, device_id_type=MESH)


def _gather_body(xs, outs, send_sems, recv_sems):
    n = len(xs)
    mx, my, mc = _me()
    chip = 2 * mx + my
    sib = (mx, my, 1 - mc)
    chips = _other_chips(mx, my)
    idx = [2 * cx + cy for cx, cy in chips]
    cp = functools.partial(_rcopy, send_sems=send_sems, recv_sems=recv_sems)
    hv = [_halves(x.shape[0], mc, 16) for x in xs]
    first, passed = [], []
    for a in range(n):
        for j, (cx, cy) in enumerate(chips):
            first.append(cp(xs[a].at[hv[a][0]], outs[a].at[chip, hv[a][0]], k=6 * a + j, to=(cx, cy, mc)))
            first[-1].start()
    for a in range(n):
        for j in range(3):
            cp(xs[a].at[hv[a][0]], outs[a].at[idx[j], hv[a][0]], k=6 * a + j, to=sib).wait_recv()
            passed.append(cp(outs[a].at[idx[j], hv[a][0]], outs[a].at[idx[j], hv[a][0]], k=6 * a + 3 + j, to=sib))
            passed[-1].start()
    for a in range(n):
        for j in range(3):
            cp(xs[a].at[hv[a][1]], outs[a].at[idx[j], hv[a][1]], k=6 * a + 3 + j, to=sib).wait_recv()
    for c_ in first + passed:
        c_.wait_send()


def _my_chip():
    return 2 * lax.axis_index("x") + lax.axis_index("y")


def _own_slots(outs, shards):
    return [lax.dynamic_update_index_in_dim(o, x, _my_chip(), 0) for o, x in zip(outs, shards)]


def gather_weights(shards, name):
    n = len(shards)

    def body(*refs):
        _gather_body(refs[:n], refs[n:2 * n], *refs[2 * n:])

    hbm = pl.BlockSpec(memory_space=pl.ANY)
    outs = _pc(body, name=name, in_specs=[hbm] * n, out_specs=[hbm] * n, out_shape=[_sds((4,) + x.shape, x.dtype) for x in shards],
               scratch_shapes=[pltpu.SemaphoreType.DMA((6 * n,)), pltpu.SemaphoreType.DMA((6 * n,))])(*shards)
    return _own_slots(outs, shards)


GATHER_REST_ID = 3


def gather_weights_sc(shards, name):
    n = len(shards)
    x_refs = [jax.new_ref(x, memory_space=pltpu.MemorySpace.HBM) for x in shards]
    out_refs = [jax.empty_ref(_sds((4,) + x.shape, x.dtype), memory_space=pltpu.MemorySpace.HBM) for x in shards]

    @pl.kernel(mesh=plsc.ScalarSubcoreMesh(axis_name="sc", num_cores=1), name=name,
               scratch_types=(pltpu.SemaphoreType.DMA((6 * n,)), pltpu.SemaphoreType.DMA((6 * n,))),
               compiler_params=pltpu.CompilerParams(collective_id=GATHER_REST_ID))
    def launch(send_sems, recv_sems):
        mx, my, mc = _me()
        barrier = pltpu.get_barrier_semaphore()
        for peer in [(mx, my, 1 - mc)] + [(cx, cy, mc) for cx, cy in _other_chips(mx, my)]:
            pl.semaphore_signal(barrier, inc=1, device_id=peer, device_id_type=MESH)
        pl.semaphore_wait(barrier, 4)
        _gather_body(x_refs, out_refs, send_sems, recv_sems)

    launch()
    return _own_slots([o[...] for o in out_refs], shards)


def swap_halves(arrs, name):
    n = len(arrs)

    def body(*refs):
        xs, outs = refs[:n], refs[n:2 * n]
        send_sems, recv_sems = refs[2 * n:]
        mx, my, mc = _me()
        cps = []
        for a in range(n):
            theirs = _halves(xs[a].shape[1], mc, 16)[1]
            cps.append(_rcopy(xs[a].at[pl.ds(0, 4), theirs], outs[a], send_sems, recv_sems, a, (mx, my, 1 - mc)))
            cps[-1].start()
        for c_ in cps:
            c_.wait()

    hbm = pl.BlockSpec(memory_space=pl.ANY)
    return _pc(body, name=name, in_specs=[hbm] * n, out_specs=[hbm] * n,
               out_shape=[_sds((4, x.shape[1] // 2, x.shape[2]), x.dtype) for x in arrs],
               scratch_shapes=[pltpu.SemaphoreType.DMA((n,)), pltpu.SemaphoreType.DMA((n,))])(*arrs)


SCATTER_ID = 4


def scatter_chips_sc(arrs, name):
    n = len(arrs)
    x_refs = [jax.new_ref(x, memory_space=pltpu.MemorySpace.HBM) for x in arrs]
    out_refs = [jax.empty_ref(_sds(x.shape, x.dtype), memory_space=pltpu.MemorySpace.HBM) for x in arrs]

    @pl.kernel(mesh=plsc.ScalarSubcoreMesh(axis_name="sc", num_cores=1), name=name,
               scratch_types=(pltpu.SemaphoreType.DMA((3 * n,)), pltpu.SemaphoreType.DMA((3 * n,))),
               compiler_params=pltpu.CompilerParams(collective_id=SCATTER_ID))
    def launch(send_sems, recv_sems):
        mx, my, mc = _me()
        chip = 2 * mx + my
        chips = _other_chips(mx, my)
        idx = [2 * cx + cy for cx, cy in chips]
        barrier = pltpu.get_barrier_semaphore()
        for cx, cy in chips:
            pl.semaphore_signal(barrier, inc=1, device_id=(cx, cy, mc), device_id_type=MESH)
        pl.semaphore_wait(barrier, 3)
        cp = functools.partial(_rcopy, send_sems=send_sems, recv_sems=recv_sems)
        sends = []
        for a in range(n):
            for j, (cx, cy) in enumerate(chips):
                sends.append(cp(x_refs[a].at[idx[j]], out_refs[a].at[chip], k=3 * a + j, to=(cx, cy, mc)))
                sends[-1].start()
        for a in range(n):
            for j, (cx, cy) in enumerate(chips):
                cp(x_refs[a].at[idx[j]], out_refs[a].at[idx[j]], k=3 * a + j, to=(cx, cy, mc)).wait_recv()
        for c_ in sends:
            c_.wait_send()

    launch()
    return _own_slots([o[...] for o in out_refs], [lax.dynamic_index_in_dim(x, _my_chip(), 0, keepdims=False) for x in arrs])


def share_halves(parts, name):
    flat = [p for w in parts for p in w]
    nw, n = len(parts), len(flat)
    depth = n // nw

    def body(*refs):
        xs, outs = refs[:n], refs[n:n + nw]
        send_sems, recv_sems = refs[n + nw:]
        mx, my, mc = _me()
        sib = (mx, my, 1 - mc)
        sends, recvs = [], []
        for a in range(n):
            w, l = a // depth, a % depth
            mine, theirs = _halves(outs[w].shape[1], mc, 8)
            sends.append(_rcopy(xs[a], outs[w].at[l, mine], send_sems, recv_sems, a, sib))
            recvs.append(_rcopy(xs[a], outs[w].at[l, theirs], send_sems, recv_sems, a, sib))
            sends[-1].start()
        for c_ in recvs:
            c_.wait_recv()
        for c_ in sends:
            c_.wait_send()

    hbm = pl.BlockSpec(memory_space=pl.ANY)
    outs = _pc(body, name=name, in_specs=[hbm] * n, out_specs=[hbm] * nw,
               out_shape=[_sds((depth, 2 * w[0].shape[0], w[0].shape[1]), F32) for w in parts],
               scratch_shapes=[pltpu.SemaphoreType.DMA((n,)), pltpu.SemaphoreType.DMA((n,))])(*flat)
    outs = list(outs)
    mc = lax.axis_index("c")
    for w in range(nw):
        for l in range(depth):
            h = parts[w][l].shape[0]
            outs[w] = lax.dynamic_update_slice(outs[w], parts[w][l][None], (l, mc * h, 0))
    return outs


_BIG = ("w_in", "w_out", "w_ffn_in", "w_ffn_out")
N_CHIPS = 4
DEPTH = 2


def _pad_rows(v, mult=8):
    n = v.shape[0]
    rows = -(-n // 128)
    rows = -(-rows // mult) * mult
    return jnp.pad(v, (0, rows * 128 - n)).reshape(rows, 128)


class _Flat:
    def __init__(self):
        self.items = []

    def add(self, name, a):
        self.items.append((name, a.shape, a.reshape(-1).astype(F32)))

    def rows(self):
        return _pad_rows(jnp.concatenate([a for _, _, a in self.items]))

    def split(self, rows):
        flat = rows.reshape(-1)
        out, o = {}, 0
        for name, shape, a in self.items:
            out[name] = flat[o:o + a.shape[0]].reshape(shape)
            o += a.shape[0]
        return out

    def split_lead(self, rows3):
        n = rows3.shape[0]
        flat = rows3.reshape(n, -1)
        out, o = {}, 0
        for name, shape, a in self.items:
            out[name] = flat[:, o:o + a.shape[0]].reshape((n,) + tuple(shape))
            o += a.shape[0]
        return out


def _gsv(rows):
    z = jnp.zeros((2, D), F32)
    r = [z if a is None else a for a in rows] + [z] * 5
    return jnp.stack(r, axis=1)


def _pad8(a, rows=8, cols=128):
    return jnp.zeros((rows, cols), F32).at[:a.shape[0], :a.shape[1]].set(a.astype(F32))


def kernel(x, c, ctx, c_ctx, w_mod, b_mod, g_mix, w_in, wa_sink, na_rpb, ssm_conv_w, ssm_conv_b, ssm_dt_bias, ssm_a_log, ssm_d, ssm_norm_g, w_out, g_ffn, w_ffn_in, w_ffn_out, g_final, loss_target, m_c_ctx, m_w_mod, m_b_mod, m_g_mix, m_w_in, m_wa_sink, m_na_rpb, m_ssm_conv_w, m_ssm_conv_b, m_ssm_dt_bias, m_ssm_a_log, m_ssm_d, m_ssm_norm_g, m_w_out, m_g_ffn, m_w_ffn_in, m_w_ffn_out, m_g_final, v_c_ctx, v_w_mod, v_b_mod, v_g_mix, v_w_in, v_wa_sink, v_na_rpb, v_ssm_conv_w, v_ssm_conv_b, v_ssm_dt_bias, v_ssm_a_log, v_ssm_d, v_ssm_norm_g, v_w_out, v_g_ffn, v_w_ffn_in, v_w_ffn_out, v_g_final):
    L, Lc = x.shape[1], ctx.shape[1]
    T = L + Lc
    nL = L // TR
    mx, my, mc = lax.axis_index("x"), lax.axis_index("y"), lax.axis_index("c")
    dev = 4 * mx + 2 * my + mc
    chip = 2 * mx + my
    MODW = 6 * D // N_CHIPS
    CW = 1024 // N_CHIPS

    sc = _silu(c.astype(F32))
    scc = _silu(c_ctx.astype(F32))[None]
    f1 = _Flat()
    f1.add("sc", sc)
    f1.add("conv_w", ssm_conv_w)
    g1, _ = allgather8(f1.rows(), "gather_cond")
    g1 = f1.split_lead(g1)
    sc_all = g1["sc"][:, 0]
    conv_w = jnp.concatenate([g1["conv_w"][2 * k] for k in range(N_CHIPS)], axis=-1)
    A16 = jnp.concatenate([sc_all, scc, jnp.zeros((7, D), F32)], axis=0)

    mod_part = matmul_layers(A16, w_mod, "nn", "mod_fwd")
    f2 = _Flat()
    f2.add("mod", mod_part)
    g2, _ = allgather8(f2.rows(), "gather_mod")
    g2 = f2.split_lead(g2)["mod"]
    mods = jnp.concatenate([g2[2 * k] for k in range(N_CHIPS)], axis=-1) + b_mod[:, None, :]
    mod_l = lax.dynamic_index_in_dim(mods, dev, axis=1, keepdims=False).reshape(DEPTH, 6, D)
    mod_c = mods[:, 8].reshape(DEPTH, 6, D)
    mod = jnp.stack([mod_l, mod_c], axis=1)
    mrow = lambda l, j: mod[l, :, j]

    own = {"w_in": w_in, "w_out": w_out, "w_ffn_in": w_ffn_in, "w_ffn_out": w_ffn_out}
    sh16 = [own[n][l].astype(BF16) for n in _BIG for l in range(DEPTH)]
    after_mod = (g2[0, 0, 0, 0] * 0).astype(BF16)
    gath = list(gather_weights([sh16[0] + after_mod], "gather_first"))
    after_first = (gath[0][0, 0, 0] * 0).astype(BF16)
    gath += list(gather_weights_sc([sh16[1] + after_first] + sh16[2:], "gather_rest"))
    gw = {n: [gath[DEPTH * i + l] for l in range(DEPTH)] for i, n in enumerate(_BIG)}
    W_in = [jnp.pad(jnp.concatenate([g[k] for k in range(N_CHIPS)], axis=1), ((0, 0), (0, IN_PAD - IN_COLS))) for g in gw["w_in"]]
    W_out = [g.reshape(D, D) for g in gw["w_out"]]
    W_fo = [g.reshape(D_FF, D) for g in gw["w_ffn_out"]]
    W_fi = gw["w_ffn_in"]

    cos, sin = rope_tables(L, Lc)
    x0 = jnp.concatenate([x[0], ctx[0]], axis=0).astype(F32)

    sv = []
    xin = x0
    gsv_first = _gsv([None, mrow(0, 0), mrow(0, 1)])
    _, h1 = res_norm_mod(x0, None, gsv_first, g_mix[0][None], nL, "norm_first")
    for l in range(DEPTH):
        s = {"xin": xin, "h1": h1}
        P = matmul(h1, W_in[l], "nn", F32, f"in_proj{l}", tn=IN_PAD)
        qr, kr, kb, vb = rope_apply(P, C_QA // 256, P, C_KA // 128, cos, sin, False, f"rope{l}", kv_src=P)
        sink8 = _pad8(jnp.broadcast_to(wa_sink[l][:, None], (WA_HEADS, 128)))
        krs, va = _swap_halves_lanes(kr), P[:, C_VA:C_VA + 128]
        vas = _swap_halves_lanes(va)
        oa, sta = win_attn_fwd(qr, kr, krs, va, vas, sink8, L, Lc, f"wa_fwd{l}")
        bias = na_bias_table(na_rpb[l], l)
        ob, stb = na_fwd(P, kb, vb, bias, L, Lc, f"na_fwd{l}")
        w8 = jnp.concatenate([conv_w[l], jnp.zeros((1, 1024), F32)], axis=0)
        pre, act = conv_silu_fwd(P, w8, ssm_conv_b[l][None], nL, f"conv_fwd{l}")
        dtb8, al8 = _pad8(ssm_dt_bias[l]), _pad8(ssm_a_log[l])
        yf, yb, hsf, hsb = ssd_fwd(act, P, dtb8, al8, L, Lc, f"ssd_fwd{l}")
        dskip = jnp.repeat(ssm_d[l], S_P)[None]
        oc = ssm_out_fwd(yf, yb, act, P, dskip, ssm_norm_g[l][None], f"ssm_out_fwd{l}")
        mixin = [(oa, 0), (ob, 256), (oc, 512)]
        mix = out_proj_fwd(mixin, W_out[l], f"out_proj{l}")
        gsv_mid = _gsv([mrow(l, 2), mrow(l, 3), mrow(l, 4)])
        x1, h2 = res_norm_mod(xin, mix, gsv_mid, g_ffn[l][None], nL, f"norm_mid{l}")
        gu, af = ffn_in_swiglu(h2, W_fi[l], f"ffn_in{l}")
        fo = matmul(af, W_fo[l], "nn", BF16, f"ffn_out{l}", tk=D_FF)
        s.update(P=P, qr=qr, kr=kr, krs=krs, va=va, vas=vas, sink8=sink8, oa=oa, sta=sta, ob=ob, stb=stb, kb=kb, vb=vb, bias=bias, w8=w8, pre=pre, act=act, dtb8=dtb8, al8=al8, yf=yf,
                 yb=yb, hsf=hsf, hsb=hsb, dskip=dskip, mixin=mixin, mix=mix, gsv_mid=gsv_mid, x1=x1, h2=h2, gu=gu, af=af, fo=fo)
        if l + 1 < DEPTH:
            s["gsv_end"] = _gsv([mrow(l, 5), mrow(l + 1, 0), mrow(l + 1, 1)])
            xin, h1 = res_norm_mod(x1, fo, s["gsv_end"], g_mix[l + 1][None], nL, f"norm_end{l}")
        else:
            s["gsv_end"] = _gsv([mrow(l, 5), None, None])
        sv.append(s)

    last = sv[-1]
    loss8, dres, dfo, dgsv_end, dg_final = final_loss(last["x1"], last["fo"], last["gsv_end"], g_final[None], loss_target[0].astype(F32), nL, "final_loss")
    loss = lax.psum(loss8[0, 0], ("x", "y", "c"))

    dmod = [[None] * 6 for _ in range(DEPTH)]
    gW = {n: [None] * DEPTH for n in _BIG}
    small = [dict() for _ in range(DEPTH)]
    parts = [None] * DEPTH
    cvec = mc.astype(jnp.int32).reshape(1)
    grad_x = None
    for l in reversed(range(DEPTH)):
        s = sv[l]
        dmod[l][5] = dgsv_end[:, 0]
        if l + 1 < DEPTH:
            dmod[l + 1][0], dmod[l + 1][1] = dgsv_end[:, 1], dgsv_end[:, 2]
        dgu = ffn_out_dx_swiglu(dfo, W_fo[l], s["gu"], f"ffn_out_dx{l}")
        gW["w_ffn_out"][l] = matmul(s["af"], dfo, "tn", BF16, f"ffn_out_dw{l}", tm=1408, tk=T).reshape(N_CHIPS, D_FF // N_CHIPS, D)
        dh2 = matmul_fi(dgu, W_fi[l], "nt", BF16, f"ffn_in_dx{l}")
        gW["w_ffn_in"][l] = matmul_fi(s["h2"], dgu, "tn", BF16, f"ffn_in_dw{l}")
        dres, dmix, dgsv_mid, dg_ffn = res_norm_mod_bwd(s["x1"], s["mix"], s["gsv_mid"], g_ffn[l][None], dh2, dres, nL, f"norm_mid_bwd{l}")
        dmod[l][2], dmod[l][3], dmod[l][4] = dgsv_mid[:, 0], dgsv_mid[:, 1], dgsv_mid[:, 2]
        dmixin = matmul(dmix, W_out[l], "nt", BF16, f"out_proj_dx{l}")
        gW["w_out"][l] = out_proj_dw(s["mixin"], dmix, f"out_proj_dw{l}").reshape(N_CHIPS, D // N_CHIPS, D)
        P = s["P"]
        dqr, dkr, dkrs, dva, dvas, dsink = win_attn_bwd(s["qr"], s["kr"], s["krs"], s["va"], s["vas"], s["sink8"], dmixin, s["oa"], s["sta"], L, Lc,
                                                        f"wa_bwd{l}")
        dkr, dva = dkr + _swap_halves_lanes(dkrs), dva + _swap_halves_lanes(dvas)
        dqa, dka = rope_apply(dqr, 0, dkr[WA_BLK:WA_BLK + T], 0, cos, sin, True, f"rope_bwd{l}")
        dqb, dkb, dvb, dbias = na_bwd(P, s["kb"], s["vb"], s["bias"], dmixin, s["ob"], s["stb"], L, Lc, f"na_bwd{l}")
        dy, dxs1, dz, dvec = ssm_out_bwd(s["yf"], s["yb"], s["act"], P, s["dskip"], ssm_norm_g[l][None], dmixin, f"ssm_out_bwd{l}")
        dxf, dbf, dcf, ddf, dxb, dbb, dcb, ddb, ddtb, dal = ssd_bwd(s["act"], P, s["dtb8"], s["al8"], s["hsf"], s["hsb"], dy, L, Lc, f"ssd_bwd{l}")
        dpre = dsilu(s["pre"], [dxf, dxb, dxs1], [dbf, dbb], [dcf, dcb], f"dsilu{l}")
        dxbc, dw8, db8 = conv_bwd(dpre, P, s["w8"], nL, f"conv_bwd{l}")
        ddt = jnp.concatenate([ddf, ddb, jnp.zeros((T, IN_PAD - IN_COLS), F32)], axis=1)
        pieces = [(dqa, C_QA), (dqb, C_QB), (dz, C_Z), (dka, C_KA), (dva[WA_BLK:WA_BLK + T], C_VA), (dkb, C_KB), (dvb, C_VB),
                  (dxbc, C_XBC), (ddt, C_DT)]
        dh1, dwin = in_proj_bwd(pieces, s["h1"], W_in[l], f"in_proj_bwd{l}")
        cw = IN_COLS // N_CHIPS
        gW["w_in"][l] = jnp.stack([dwin[:, k * cw:(k + 1) * cw] for k in range(N_CHIPS)])
        garr = [gW[n][l] for n in _BIG]
        got = swap_halves(garr, f"reduce_d2d{l}")
        chip_sum = [add_halves(garr[a], got[a], cvec, f"reduce_add_pair{l}_{a}") for a in range(len(garr))]
        parts[l] = scatter_chips_sc(chip_sum, f"reduce_ici{l}")
        small[l] = dict(g_ffn=dg_ffn[0], wa_sink=dsink[:WA_HEADS, 0], na_rpb=na_rpb_grad(dbias, l), conv_w=dw8[:S_CONV], conv_b=db8[0],
                        dt_bias=ddtb[:2, :8], a_log=dal[:2, :8], ssm_d=dvec[0].reshape(S_HEADS, S_P).sum(axis=1), norm_g=dvec[1])
        if l > 0:
            p = sv[l - 1]
            dres, dfo, dgsv_end, dg_mix = res_norm_mod_bwd(s["xin"], p["fo"], p["gsv_end"], g_mix[l][None], dh1, dres, nL, f"norm_end_bwd{l - 1}")
        else:
            grad_x, _, dgsv_first, dg_mix = res_norm_mod_bwd(s["xin"], None, gsv_first, g_mix[0][None], dh1, dres, nL, "norm_first_bwd")
            dmod[0][0], dmod[0][1] = dgsv_first[:, 1], dgsv_first[:, 2]
        small[l]["g_mix"] = dg_mix[0]
    for l in range(DEPTH):
        for j in range(6):
            if dmod[l][j] is None:
                dmod[l][j] = jnp.zeros((2, D), F32)
    dmod = jnp.stack([jnp.stack(r, axis=1) for r in dmod])

    f3 = _Flat()
    f3.add("dmod_l", dmod[:, 0].reshape(DEPTH, 6 * D))
    f3.add("dmod_c", dmod[:, 1].reshape(DEPTH, 6 * D))
    f3.add("g_final", dg_final[0])
    for n in ("g_mix", "g_ffn", "wa_sink", "na_rpb", "conv_w", "conv_b", "dt_bias", "a_log", "ssm_d", "norm_g"):
        f3.add(n, jnp.stack([small[l][n] for l in range(DEPTH)]))
    g3, s3 = allgather8(f3.rows(), "reduce_small")
    dmod_all = f3.split_lead(g3)["dmod_l"]
    s3 = f3.split(s3)
    dmodc_tot = s3["dmod_c"]
    col0 = chip * MODW
    G16, G16c = [], []
    for l in range(DEPTH):
        rows = jnp.concatenate([dmod_all[:, l], dmodc_tot[l][None], jnp.zeros((7, 6 * D), F32)], axis=0)
        G16.append(lax.dynamic_slice_in_dim(rows, col0, MODW, axis=1))
        rc = jnp.concatenate([dmodc_tot[l][None], jnp.zeros((15, 6 * D), F32)], axis=0)
        G16c.append(lax.dynamic_slice_in_dim(rc, col0, MODW, axis=1))
    grad_w_mod = matmul_layers(A16, jnp.stack(G16), "tn", "mod_dw")
    dscc_part = matmul_layers(jnp.stack(G16c), w_mod, "nt", "mod_dx")[:, 0].sum(axis=0)
    _, s4 = allgather8(_pad_rows(dscc_part * (mc == 1).astype(F32)), "reduce_cctx")
    dscc = s4.reshape(-1)[:D]
    cc = c_ctx.astype(F32)
    sg = 1.0 / (1.0 + jnp.exp(-cc))
    grad_c_ctx = dscc * (sg * (1.0 + cc * (1.0 - sg)))

    halves = [[sum_slots(parts[l][i], f"reduce_add_chips{l}_{i}") for l in range(DEPTH)] for i in range(len(_BIG))]
    gsh = dict(zip(_BIG, share_halves(halves, "reduce_share")))

    grads = {"c_ctx": grad_c_ctx, "w_mod": grad_w_mod, "b_mod": s3["dmod_l"] + s3["dmod_c"], "g_mix": s3["g_mix"], "w_in": gsh["w_in"],
             "wa_sink": s3["wa_sink"], "na_rpb": s3["na_rpb"],
             "ssm_conv_w": lax.dynamic_slice_in_dim(s3["conv_w"], chip * CW, CW, axis=2), "ssm_conv_b": s3["conv_b"],
             "ssm_dt_bias": s3["dt_bias"], "ssm_a_log": s3["a_log"], "ssm_d": s3["ssm_d"], "ssm_norm_g": s3["norm_g"],
             "w_out": gsh["w_out"], "g_ffn": s3["g_ffn"], "w_ffn_in": gsh["w_ffn_in"], "w_ffn_out": gsh["w_ffn_out"], "g_final": s3["g_final"]}
    wts = {"c_ctx": c_ctx, "w_mod": w_mod, "b_mod": b_mod, "g_mix": g_mix, "w_in": w_in, "wa_sink": wa_sink, "na_rpb": na_rpb,
           "ssm_conv_w": ssm_conv_w, "ssm_conv_b": ssm_conv_b, "ssm_dt_bias": ssm_dt_bias, "ssm_a_log": ssm_a_log, "ssm_d": ssm_d,
           "ssm_norm_g": ssm_norm_g, "w_out": w_out, "g_ffn": g_ffn, "w_ffn_in": w_ffn_in, "w_ffn_out": w_ffn_out, "g_final": g_final}
    ms = {"c_ctx": m_c_ctx, "w_mod": m_w_mod, "b_mod": m_b_mod, "g_mix": m_g_mix, "w_in": m_w_in, "wa_sink": m_wa_sink, "na_rpb": m_na_rpb,
          "ssm_conv_w": m_ssm_conv_w, "ssm_conv_b": m_ssm_conv_b, "ssm_dt_bias": m_ssm_dt_bias, "ssm_a_log": m_ssm_a_log, "ssm_d": m_ssm_d,
          "ssm_norm_g": m_ssm_norm_g, "w_out": m_w_out, "g_ffn": m_g_ffn, "w_ffn_in": m_w_ffn_in, "w_ffn_out": m_w_ffn_out, "g_final": m_g_final}
    vs = {"c_ctx": v_c_ctx, "w_mod": v_w_mod, "b_mod": v_b_mod, "g_mix": v_g_mix, "w_in": v_w_in, "wa_sink": v_wa_sink, "na_rpb": v_na_rpb,
          "ssm_conv_w": v_ssm_conv_w, "ssm_conv_b": v_ssm_conv_b, "ssm_dt_bias": v_ssm_dt_bias, "ssm_a_log": v_ssm_a_log, "ssm_d": v_ssm_d,
          "ssm_norm_g": v_ssm_norm_g, "w_out": v_w_out, "g_ffn": v_g_ffn, "w_ffn_in": v_w_ffn_in, "w_ffn_out": v_w_ffn_out, "g_final": v_g_final}
    names = list(wts)
    grads = {n: grads[n].reshape(wts[n].shape).astype(F32) for n in names}
    big = ("w_mod", "w_in", "w_out", "w_ffn_in", "w_ffn_out")
    delta, new_m, new_v = {}, {}, {}
    for n in big:
        delta[n], new_m[n], new_v[n] = adamw(wts[n], grads[n], ms[n], vs[n], f"adamw_{n}")
    packs = []
    for src in (wts, grads, ms, vs):
        f = _Flat()
        for n in names:
            if n not in big:
                f.add(n, src[n])
        packs.append(f)
    d_, m_, v_ = adamw(*[f.rows()[None] for f in packs], "adamw_small")
    for dst, rows in ((delta, d_), (new_m, m_), (new_v, v_)):
        dst.update(packs[0].split(rows[0]))

    return (loss, grad_x[:L][None], *[grads[n] for n in names], *[delta[n] for n in names],
            *[new_m[n] for n in names], *[new_v[n] for n in names])
```
